```python
import jax, jax.numpy as jnp
from jax import lax
import numpy as np

D_MODEL = 1024
BATCH = 16
SEQ = 2048
DEPTH = 1

GRID_W = 64
Q_BLOCK = 128
ROPE_THETA = 10000.0
EPS = 1e-6
H_A = 8
QK_NOPE = 64
QK_ROPE = 32
V_DIM_A = 64
Q_LORA = 256
KV_LORA = 128
H_B = 8
KV_B = 2
HD_B = 64
D_FF = 4 * D_MODEL
PLE_DIM = 256
IN_SPLITS = [Q_LORA, KV_LORA + QK_ROPE, H_B * HD_B, KV_B * HD_B, KV_B * HD_B, D_MODEL, D_MODEL]
D_IN = sum(IN_SPLITS)

kernel_name = "hybrid_mla_axial_gqa_gated_block"


def rmsnorm(x, g):
    xf = x.astype(jnp.float32)
    y = xf * lax.rsqrt(jnp.mean(xf * xf, axis=-1, keepdims=True) + EPS)
    return (y * g.astype(jnp.float32)).astype(x.dtype)


def rope_angles(pos, dim):
    inv = ROPE_THETA ** (-jnp.arange(0, dim, 2, dtype=jnp.float32) / dim)
    return pos.astype(jnp.float32)[:, None] * inv[None, :]


def apply_rope(x, ang):
    cos = jnp.cos(ang)[None, :, None, :].astype(x.dtype)
    sin = jnp.sin(ang)[None, :, None, :].astype(x.dtype)
    x1, x2 = jnp.split(x, 2, axis=-1)
    return jnp.concatenate([x1 * cos - x2 * sin, x1 * sin + x2 * cos], axis=-1)


def blocked_attention(q, k, v):
    B, S, Hk, G, dk = q.shape
    nb = S // Q_BLOCK
    qb = q.reshape(B, nb, Q_BLOCK, Hk, G, dk).transpose(1, 0, 2, 3, 4, 5)

    def one_block(qblk):
        s = jnp.einsum("bqhgd,bkhd->bhgqk", qblk, k).astype(jnp.float32)
        pr = jax.nn.softmax(s, axis=-1).astype(v.dtype)
        return jnp.einsum("bhgqk,bkhd->bqhgd", pr, v)

    o = lax.map(one_block, qb)
    return o.transpose(1, 0, 2, 3, 4, 5).reshape(B, S, Hk * G, v.shape[-1])


def _fwd_setup_inputs(seed: int = 0) -> dict:
    key = jax.random.key(seed)
    ks = jax.random.split(key, 24)
    f32 = jnp.float32

    def w(k, shape, fan_in):
        return jax.random.normal(k, shape, f32) * (fan_in ** -0.5)

    def gain(k, shape):
        return 1.0 + 0.05 * jax.random.normal(k, shape, f32)

    L = DEPTH
    return {
        "x": jax.random.normal(ks[0], (BATCH, SEQ, D_MODEL), f32),
        "p": jax.random.normal(ks[1], (DEPTH, BATCH, SEQ, PLE_DIM), f32),
        "g_mix": gain(ks[2], (L, D_MODEL)),
        "w_in": w(ks[3], (L, D_MODEL, D_IN), D_MODEL),
        "g_qa": gain(ks[4], (L, Q_LORA)),
        "w_qb": w(ks[5], (L, Q_LORA, H_A * (QK_NOPE + QK_ROPE)), Q_LORA),
        "g_kva": gain(ks[6], (L, KV_LORA)),
        "w_kvb": w(ks[7], (L, KV_LORA, H_A * (QK_NOPE + V_DIM_A)), KV_LORA),
        "g_qn": gain(ks[8], (L, HD_B)),
        "g_kn": gain(ks[9], (L, HD_B)),
        "w_oa": w(ks[10], (L, H_A * V_DIM_A, D_MODEL), H_A * V_DIM_A),
        "w_ob": w(ks[11], (L, H_B * HD_B, D_MODEL), H_B * HD_B),
        "w_o": w(ks[12], (L, D_MODEL, D_MODEL), D_MODEL),
        "g_mlp": gain(ks[13], (L, D_MODEL)),
        "w_up": w(ks[14], (L, D_MODEL, D_FF), D_MODEL),
        "w_down": w(ks[15], (L, D_FF, D_MODEL), D_FF),
        "g_ple": gain(ks[16], (L, D_MODEL)),
        "w_ple_gate": w(ks[17], (L, D_MODEL, D_MODEL), D_MODEL),
        "w_ple": w(ks[18], (L, PLE_DIM, D_MODEL), PLE_DIM),
        "g_final": gain(ks[19], (D_MODEL,)),
    }


def _fwd_reference(x, p, g_mix, w_in, g_qa, w_qb, g_kva, w_kvb, g_qn, g_kn, w_oa, w_ob, w_o,
              g_mlp, w_up, w_down, g_ple, w_ple_gate, w_ple, g_final):
    B, S, D = x.shape
    ROWS = S // GRID_W
    t = jnp.arange(S)
    row = jnp.broadcast_to(jnp.arange(ROWS)[:, None], (ROWS, GRID_W)).reshape(-1)
    col = jnp.broadcast_to(jnp.arange(GRID_W)[None, :], (ROWS, GRID_W)).reshape(-1)
    ang_1d = rope_angles(t, QK_ROPE)
    ang_row = rope_angles(row, HD_B // 2)
    ang_col = rope_angles(col, HD_B // 2)
    split_idx = list(np.cumsum(IN_SPLITS)[:-1])
    scale_a = (QK_NOPE + QK_ROPE) ** -0.5
    scale_b = HD_B ** -0.5

    for i in range(DEPTH):
        h = rmsnorm(x, g_mix[i])
        z = jnp.einsum("bsd,de->bse", h, w_in[i])
        q_lat, kv_lat, qb, kb, vb, gate_a, gate_b = jnp.split(z, split_idx, axis=-1)

        cq = rmsnorm(q_lat, g_qa[i])
        qa = jnp.einsum("bsr,re->bse", cq, w_qb[i]).reshape(B, S, H_A, QK_NOPE + QK_ROPE)
        qa_nope, qa_rope = jnp.split(qa, [QK_NOPE], axis=-1)
        qa_rope = apply_rope(qa_rope, ang_1d)
        c_kv, k_pe = jnp.split(kv_lat, [KV_LORA], axis=-1)
        c_kv = rmsnorm(c_kv, g_kva[i])
        k_pe = apply_rope(k_pe[:, :, None, :], ang_1d)
        kva = jnp.einsum("bsr,re->bse", c_kv, w_kvb[i]).reshape(B, S, H_A, QK_NOPE + V_DIM_A)
        ka_nope, va = jnp.split(kva, [QK_NOPE], axis=-1)
        qa_full = (jnp.concatenate([qa_nope, qa_rope], axis=-1) * scale_a)[:, :, :, None, :]
        ka_full = jnp.concatenate([ka_nope, jnp.broadcast_to(k_pe, (B, S, H_A, QK_ROPE))], axis=-1)
        oa = blocked_attention(qa_full, ka_full, va).reshape(B, S, H_A * V_DIM_A)
        ya = jnp.einsum("bse,ed->bsd", oa, w_oa[i])

        qb = rmsnorm(qb.reshape(B, S, H_B, HD_B), g_qn[i])
        kb = rmsnorm(kb.reshape(B, S, KV_B, HD_B), g_kn[i])
        vb = vb.reshape(B, S, KV_B, HD_B)
        qr, qc = jnp.split(qb, 2, axis=-1)
        qb = jnp.concatenate([apply_rope(qr, ang_row), apply_rope(qc, ang_col)], axis=-1)
        kr, kc = jnp.split(kb, 2, axis=-1)
        kb = jnp.concatenate([apply_rope(kr, ang_row), apply_rope(kc, ang_col)], axis=-1)
        qb = (qb * scale_b).reshape(B, S, KV_B, H_B // KV_B, HD_B)
        ob = blocked_attention(qb, kb, vb).reshape(B, S, H_B * HD_B)
        yb = jnp.einsum("bse,ed->bsd", ob, w_ob[i])

        merged = jax.nn.sigmoid(gate_a) * ya + jax.nn.sigmoid(gate_b) * yb
        x = x + jnp.einsum("bsd,de->bse", merged, w_o[i])

        h2 = rmsnorm(x, g_mlp[i])
        u = jax.nn.relu(jnp.einsum("bsd,df->bsf", h2, w_up[i]))
        x = x + jnp.einsum("bsf,fd->bsd", u * u, w_down[i])

        h3 = rmsnorm(x, g_ple[i])
        gate = jax.nn.sigmoid(jnp.einsum("bsd,de->bse", h3, w_ple_gate[i]))
        x = x + gate * jnp.einsum("bsk,kd->bsd", p[i], w_ple[i])

    return rmsnorm(x, g_final)


import jax as _jax
import jax.numpy as _jnp

TWIN_FORMAT = 'train_step'
FWD_PARAMS = ['x', 'p', 'g_mix', 'w_in', 'g_qa', 'w_qb', 'g_kva', 'w_kvb', 'g_qn', 'g_kn', 'w_oa', 'w_ob', 'w_o', 'g_mlp', 'w_up', 'w_down', 'g_ple', 'w_ple_gate', 'w_ple', 'g_final']
TWIN_WEIGHTS = ['g_mix', 'w_in', 'g_qa', 'w_qb', 'g_kva', 'w_kvb', 'g_qn', 'g_kn', 'w_oa', 'w_ob', 'w_o', 'g_mlp', 'w_up', 'w_down', 'g_ple', 'w_ple_gate', 'w_ple', 'g_final']
TWIN_DIFF_INPUT = 'x'
TWIN_INPUTS = ['x', 'p', 'g_mix', 'w_in', 'g_qa', 'w_qb', 'g_kva', 'w_kvb', 'g_qn', 'g_kn', 'w_oa', 'w_ob', 'w_o', 'g_mlp', 'w_up', 'w_down', 'g_ple', 'w_ple_gate', 'w_ple', 'g_final', 'loss_target', 'm_g_mix', 'm_w_in', 'm_g_qa', 'm_w_qb', 'm_g_kva', 'm_w_kvb', 'm_g_qn', 'm_g_kn', 'm_w_oa', 'm_w_ob', 'm_w_o', 'm_g_mlp', 'm_w_up', 'm_w_down', 'm_g_ple', 'm_w_ple_gate', 'm_w_ple', 'm_g_final', 'v_g_mix', 'v_w_in', 'v_g_qa', 'v_w_qb', 'v_g_kva', 'v_w_kvb', 'v_g_qn', 'v_g_kn', 'v_w_oa', 'v_w_ob', 'v_w_o', 'v_g_mlp', 'v_w_up', 'v_w_down', 'v_g_ple', 'v_w_ple_gate', 'v_w_ple', 'v_g_final']
TWIN_OUTPUTS = ['loss', 'grad_x', 'grad_g_mix', 'grad_w_in', 'grad_g_qa', 'grad_w_qb', 'grad_g_kva', 'grad_w_kvb', 'grad_g_qn', 'grad_g_kn', 'grad_w_oa', 'grad_w_ob', 'grad_w_o', 'grad_g_mlp', 'grad_w_up', 'grad_w_down', 'grad_g_ple', 'grad_w_ple_gate', 'grad_w_ple', 'grad_g_final', 'delta_g_mix', 'delta_w_in', 'delta_g_qa', 'delta_w_qb', 'delta_g_kva', 'delta_w_kvb', 'delta_g_qn', 'delta_g_kn', 'delta_w_oa', 'delta_w_ob', 'delta_w_o', 'delta_g_mlp', 'delta_w_up', 'delta_w_down', 'delta_g_ple', 'delta_w_ple_gate', 'delta_w_ple', 'delta_g_final', 'new_m_g_mix', 'new_m_w_in', 'new_m_g_qa', 'new_m_w_qb', 'new_m_g_kva', 'new_m_w_kvb', 'new_m_g_qn', 'new_m_g_kn', 'new_m_w_oa', 'new_m_w_ob', 'new_m_w_o', 'new_m_g_mlp', 'new_m_w_up', 'new_m_w_down', 'new_m_g_ple', 'new_m_w_ple_gate', 'new_m_w_ple', 'new_m_g_final', 'new_v_g_mix', 'new_v_w_in', 'new_v_g_qa', 'new_v_w_qb', 'new_v_g_kva', 'new_v_w_kvb', 'new_v_g_qn', 'new_v_g_kn', 'new_v_w_oa', 'new_v_w_ob', 'new_v_w_o', 'new_v_g_mlp', 'new_v_w_up', 'new_v_w_down', 'new_v_g_ple', 'new_v_w_ple_gate', 'new_v_w_ple', 'new_v_g_final']
TWIN_LEAF_KINDS = {'loss': 'loss', 'grad_x': 'grad_x', 'grad_g_mix': 'grad_w', 'grad_w_in': 'grad_w', 'grad_g_qa': 'grad_w', 'grad_w_qb': 'grad_w', 'grad_g_kva': 'grad_w', 'grad_w_kvb': 'grad_w', 'grad_g_qn': 'grad_w', 'grad_g_kn': 'grad_w', 'grad_w_oa': 'grad_w', 'grad_w_ob': 'grad_w', 'grad_w_o': 'grad_w', 'grad_g_mlp': 'grad_w', 'grad_w_up': 'grad_w', 'grad_w_down': 'grad_w', 'grad_g_ple': 'grad_w', 'grad_w_ple_gate': 'grad_w', 'grad_w_ple': 'grad_w', 'grad_g_final': 'grad_w', 'delta_g_mix': 'delta_w', 'delta_w_in': 'delta_w', 'delta_g_qa': 'delta_w', 'delta_w_qb': 'delta_w', 'delta_g_kva': 'delta_w', 'delta_w_kvb': 'delta_w', 'delta_g_qn': 'delta_w', 'delta_g_kn': 'delta_w', 'delta_w_oa': 'delta_w', 'delta_w_ob': 'delta_w', 'delta_w_o': 'delta_w', 'delta_g_mlp': 'delta_w', 'delta_w_up': 'delta_w', 'delta_w_down': 'delta_w', 'delta_g_ple': 'delta_w', 'delta_w_ple_gate': 'delta_w', 'delta_w_ple': 'delta_w', 'delta_g_final': 'delta_w', 'new_m_g_mix': 'new_m', 'new_m_w_in': 'new_m', 'new_m_g_qa': 'new_m', 'new_m_w_qb': 'new_m', 'new_m_g_kva': 'new_m', 'new_m_w_kvb': 'new_m', 'new_m_g_qn': 'new_m', 'new_m_g_kn': 'new_m', 'new_m_w_oa': 'new_m', 'new_m_w_ob': 'new_m', 'new_m_w_o': 'new_m', 'new_m_g_mlp': 'new_m', 'new_m_w_up': 'new_m', 'new_m_w_down': 'new_m', 'new_m_g_ple': 'new_m', 'new_m_w_ple_gate': 'new_m', 'new_m_w_ple': 'new_m', 'new_m_g_final': 'new_m', 'new_v_g_mix': 'new_v', 'new_v_w_in': 'new_v', 'new_v_g_qa': 'new_v', 'new_v_w_qb': 'new_v', 'new_v_g_kva': 'new_v', 'new_v_w_kvb': 'new_v', 'new_v_g_qn': 'new_v', 'new_v_g_kn': 'new_v', 'new_v_w_oa': 'new_v', 'new_v_w_ob': 'new_v', 'new_v_w_o': 'new_v', 'new_v_g_mlp': 'new_v', 'new_v_w_up': 'new_v', 'new_v_w_down': 'new_v', 'new_v_g_ple': 'new_v', 'new_v_w_ple_gate': 'new_v', 'new_v_w_ple': 'new_v', 'new_v_g_final': 'new_v'}


def _forward(args):
    return _fwd_reference(*[args[k] for k in FWD_PARAMS])


def _output_shape():
    out = _jax.eval_shape(lambda: _forward(_fwd_setup_inputs(0)))
    return out.shape, out.dtype

N_MICROBATCH = 1
ADAM_LR = 0.001
ADAM_B1 = 0.9
ADAM_B2 = 0.999
ADAM_EPS = 1e-08
ADAM_WD = 0.01
ADAM_STEP = 10
PER_EXAMPLE_BATCH_AXIS = {'x': 0, 'p': 1, 'loss_target': 0}
SHARED_INPUTS = []
_WEIGHT_DTYPES = {'g_mix': _jnp.float32, 'w_in': _jnp.float32, 'g_qa': _jnp.float32, 'w_qb': _jnp.float32, 'g_kva': _jnp.float32, 'w_kvb': _jnp.float32, 'g_qn': _jnp.float32, 'g_kn': _jnp.float32, 'w_oa': _jnp.float32, 'w_ob': _jnp.float32, 'w_o': _jnp.float32, 'g_mlp': _jnp.float32, 'w_up': _jnp.float32, 'w_down': _jnp.float32, 'g_ple': _jnp.float32, 'w_ple_gate': _jnp.float32, 'w_ple': _jnp.float32, 'g_final': _jnp.float32}
MOMENT_SCALE = {'g_mix': 2.825720e-02, 'w_in': 1.533752e-02, 'g_qa': 2.489705e-02, 'w_qb': 1.460358e-02, 'g_kva': 5.566627e-02, 'w_kvb': 1.709108e-02, 'g_qn': 3.980393e-02, 'g_kn': 4.235165e-02, 'w_oa': 1.294979e-02, 'w_ob': 1.054206e-02, 'w_o': 1.676149e-02, 'g_mlp': 1.521636e-01, 'w_up': 7.675857e-02, 'w_down': 2.124916e-01, 'g_ple': 3.162209e-02, 'w_ple_gate': 3.059183e-02, 'w_ple': 5.821850e-02, 'g_final': 3.223795e+01}


def _to_microbatches(a, axis):
    t = _jnp.moveaxis(a, axis, 0)
    t = t.reshape((N_MICROBATCH, t.shape[0] // N_MICROBATCH) + t.shape[1:])
    return _jnp.moveaxis(t, 1, axis + 1)


def setup_inputs(seed: int = 0) -> dict:
    inp = _fwd_setup_inputs(seed)
    key = _jax.random.fold_in(_jax.random.key(seed), 7919)
    shape, _ = _output_shape()
    out = dict(inp)
    out["loss_target"] = _jax.random.normal(_jax.random.fold_in(key, 0), shape, _jnp.float32)
    for i, name in enumerate(TWIN_WEIGHTS):
        w = inp[name].astype(_jnp.float32)
        if MOMENT_SCALE is None:
            s = _jnp.sqrt(_jnp.mean(_jnp.square(w)) + 1e-30)
        else:
            s = MOMENT_SCALE[name]
        km, kv = _jax.random.split(_jax.random.fold_in(key, i + 1))
        out[name] = w
        out["m_" + name] = s * _jax.random.normal(km, w.shape, _jnp.float32)
        out["v_" + name] = (s * s) * _jax.random.uniform(kv, w.shape, _jnp.float32, 0.5, 1.5)
    if N_MICROBATCH > 1:
        for name, axis in PER_EXAMPLE_BATCH_AXIS.items():
            out[name] = _to_microbatches(out[name], axis)
    return {'x': out['x'], 'p': out['p'], 'g_mix': out['g_mix'], 'w_in': out['w_in'], 'g_qa': out['g_qa'], 'w_qb': out['w_qb'], 'g_kva': out['g_kva'], 'w_kvb': out['w_kvb'], 'g_qn': out['g_qn'], 'g_kn': out['g_kn'], 'w_oa': out['w_oa'], 'w_ob': out['w_ob'], 'w_o': out['w_o'], 'g_mlp': out['g_mlp'], 'w_up': out['w_up'], 'w_down': out['w_down'], 'g_ple': out['g_ple'], 'w_ple_gate': out['w_ple_gate'], 'w_ple': out['w_ple'], 'g_final': out['g_final'], 'loss_target': out['loss_target'], 'm_g_mix': out['m_g_mix'], 'm_w_in': out['m_w_in'], 'm_g_qa': out['m_g_qa'], 'm_w_qb': out['m_w_qb'], 'm_g_kva': out['m_g_kva'], 'm_w_kvb': out['m_w_kvb'], 'm_g_qn': out['m_g_qn'], 'm_g_kn': out['m_g_kn'], 'm_w_oa': out['m_w_oa'], 'm_w_ob': out['m_w_ob'], 'm_w_o': out['m_w_o'], 'm_g_mlp': out['m_g_mlp'], 'm_w_up': out['m_w_up'], 'm_w_down': out['m_w_down'], 'm_g_ple': out['m_g_ple'], 'm_w_ple_gate': out['m_w_ple_gate'], 'm_w_ple': out['m_w_ple'], 'm_g_final': out['m_g_final'], 'v_g_mix': out['v_g_mix'], 'v_w_in': out['v_w_in'], 'v_g_qa': out['v_g_qa'], 'v_w_qb': out['v_w_qb'], 'v_g_kva': out['v_g_kva'], 'v_w_kvb': out['v_w_kvb'], 'v_g_qn': out['v_g_qn'], 'v_g_kn': out['v_g_kn'], 'v_w_oa': out['v_w_oa'], 'v_w_ob': out['v_w_ob'], 'v_w_o': out['v_w_o'], 'v_g_mlp': out['v_g_mlp'], 'v_w_up': out['v_w_up'], 'v_w_down': out['v_w_down'], 'v_g_ple': out['v_g_ple'], 'v_w_ple_gate': out['v_w_ple_gate'], 'v_w_ple': out['v_w_ple'], 'v_g_final': out['v_g_final']}


def _loss(weights, diff, rest, loss_target):
    with _jax.named_scope("forward"):
        args = {**rest, TWIN_DIFF_INPUT: diff, **{k: w.astype(_WEIGHT_DTYPES[k]) for k, w in weights.items()}}
        y = _forward(args)
    with _jax.named_scope("loss_head"):
        err = _jnp.square(y.astype(_jnp.float32) - loss_target)
        return 0.5 * _jnp.sum(_jnp.mean(err, axis=-1)) if err.ndim else 0.5 * err


def _adamw(w, g, m, v):
    m = ADAM_B1 * m + (1.0 - ADAM_B1) * g
    v = ADAM_B2 * v + (1.0 - ADAM_B2) * _jnp.square(g)
    m_hat = m / (1.0 - ADAM_B1 ** ADAM_STEP)
    v_hat = v / (1.0 - ADAM_B2 ** ADAM_STEP)
    delta = -ADAM_LR * (m_hat / (_jnp.sqrt(v_hat) + ADAM_EPS) + ADAM_WD * w)
    return delta, m, v


def reference(x, p, g_mix, w_in, g_qa, w_qb, g_kva, w_kvb, g_qn, g_kn, w_oa, w_ob, w_o, g_mlp, w_up, w_down, g_ple, w_ple_gate, w_ple, g_final, loss_target, m_g_mix, m_w_in, m_g_qa, m_w_qb, m_g_kva, m_w_kvb, m_g_qn, m_g_kn, m_w_oa, m_w_ob, m_w_o, m_g_mlp, m_w_up, m_w_down, m_g_ple, m_w_ple_gate, m_w_ple, m_g_final, v_g_mix, v_w_in, v_g_qa, v_w_qb, v_g_kva, v_w_kvb, v_g_qn, v_g_kn, v_w_oa, v_w_ob, v_w_o, v_g_mlp, v_w_up, v_w_down, v_g_ple, v_w_ple_gate, v_w_ple, v_g_final):
    given = dict(x=x, p=p, g_mix=g_mix, w_in=w_in, g_qa=g_qa, w_qb=w_qb, g_kva=g_kva, w_kvb=w_kvb, g_qn=g_qn, g_kn=g_kn, w_oa=w_oa, w_ob=w_ob, w_o=w_o, g_mlp=g_mlp, w_up=w_up, w_down=w_down, g_ple=g_ple, w_ple_gate=w_ple_gate, w_ple=w_ple, g_final=g_final, loss_target=loss_target, m_g_mix=m_g_mix, m_w_in=m_w_in, m_g_qa=m_g_qa, m_w_qb=m_w_qb, m_g_kva=m_g_kva, m_w_kvb=m_w_kvb, m_g_qn=m_g_qn, m_g_kn=m_g_kn, m_w_oa=m_w_oa, m_w_ob=m_w_ob, m_w_o=m_w_o, m_g_mlp=m_g_mlp, m_w_up=m_w_up, m_w_down=m_w_down, m_g_ple=m_g_ple, m_w_ple_gate=m_w_ple_gate, m_w_ple=m_w_ple, m_g_final=m_g_final, v_g_mix=v_g_mix, v_w_in=v_w_in, v_g_qa=v_g_qa, v_w_qb=v_w_qb, v_g_kva=v_g_kva, v_w_kvb=v_w_kvb, v_g_qn=v_g_qn, v_g_kn=v_g_kn, v_w_oa=v_w_oa, v_w_ob=v_w_ob, v_w_o=v_w_o, v_g_mlp=v_g_mlp, v_w_up=v_w_up, v_w_down=v_w_down, v_g_ple=v_g_ple, v_w_ple_gate=v_w_ple_gate, v_w_ple=v_w_ple, v_g_final=v_g_final)
    weights = {n: given[n] for n in TWIN_WEIGHTS}
    shared = {n: given[n] for n in SHARED_INPUTS}
    per_example = {n: given[n] for n in ['x', 'p']}
    grad_fn = _jax.value_and_grad(_loss, argnums=(0, 1))

    def one_microbatch(ex, loss_target):
        ex = dict(ex)
        diff = ex.pop(TWIN_DIFF_INPUT)
        return grad_fn(weights, diff, {**shared, **ex}, loss_target)

    if N_MICROBATCH == 1:
        loss, (grad_w, grad_x) = one_microbatch(per_example, given["loss_target"])
    else:
        def body(carry, xs):
            loss_sum, grad_sum = carry
            l_k, (gw_k, gx_k) = one_microbatch(xs[0], xs[1])
            with _jax.named_scope("update"):
                return (loss_sum + l_k, _jax.tree.map(_jnp.add, grad_sum, gw_k)), gx_k

        init = (_jnp.zeros((), _jnp.float32), _jax.tree.map(_jnp.zeros_like, weights))
        (loss, grad_w), grad_x = _jax.lax.scan(body, init, (per_example, given["loss_target"]))
    with _jax.named_scope("update"):
        delta_w, new_m, new_v = {}, {}, {}
        for n in TWIN_WEIGHTS:
            delta_w[n], new_m[n], new_v[n] = _adamw(weights[n], grad_w[n], given["m_" + n], given["v_" + n])
    return (loss, grad_x, *[grad_w[n] for n in TWIN_WEIGHTS], *[delta_w[n] for n in TWIN_WEIGHTS],
            *[new_m[n] for n in TWIN_WEIGHTS], *[new_v[n] for n in TWIN_WEIGHTS])
```

```python
import functools

import jax
import jax.numpy as jnp
from jax import lax
from jax.experimental import pallas as pl
from jax.experimental.pallas import tpu as pltpu

F32 = jnp.float32
BF16 = jnp.bfloat16

D_MODEL = 1024
EPS = 1e-6
ROPE_THETA = 10000.0
GRID_W = 64
H_A = 8
QK_NOPE = 64
QK_ROPE = 32
V_DIM_A = 64
Q_LORA = 256
KV_LORA = 128
H_B = 8
KV_B = 2
HD_B = 64
D_FF = 4 * D_MODEL
PLE_DIM = 256
HP = 128
ZP = 4096
N_DEV = 8
N_CHIP = 4

ADAM_LR = 0.001
ADAM_B1 = 0.9
ADAM_B2 = 0.999
ADAM_EPS = 1e-08
ADAM_WD = 0.01
ADAM_STEP = 10

VMEM_LIMIT = 52 * 1024 * 1024

PACK = (("w_in", 416), ("w_qb", 32), ("w_kvb", 16), ("w_oa", 64), ("w_ob", 64), ("w_o", 128),
        ("w_up", 512), ("w_down", 512), ("w_ple_gate", 128), ("w_ple", 32))
PACK_OFF = {}
_o = 0
for _n, _r in PACK:
    PACK_OFF[_n] = (_o, _r)
    _o += _r
PACK_ROWS = _o

ST_G_MIX, ST_G_QA, ST_G_KVA, ST_G_QN, ST_G_KN, ST_G_MLP, ST_G_PLE, ST_G_FINAL, ST_LOSS = range(9)
ST_ROWS = 16


def _dot_nn(a, b):
    return lax.dot_general(a, b, (((1,), (0,)), ((), ())), preferred_element_type=F32)


def _dot_nt(a, b):
    return lax.dot_general(a, b, (((1,), (1,)), ((), ())), preferred_element_type=F32)


def _dot_tn(a, b):
    return lax.dot_general(a, b, (((0,), (0,)), ((), ())), preferred_element_type=F32)


def _rstd(x, n):
    return lax.rsqrt(jnp.sum(x * x, axis=-1, keepdims=True) * (1.0 / n) + EPS)


def _rms_bwd(dy, xh, r, g, n):
    dxh = dy * g
    return r * (dxh - xh * (jnp.sum(dxh * xh, axis=-1, keepdims=True) * (1.0 / n)))


def _rope_fwd(x, c, s1, s2):
    return x * c + pltpu.roll(x, HP - 16, 1) * s1 + pltpu.roll(x, 16, 1) * s2


def _rope_bwd(d, c, s1, s2):
    return d * c + pltpu.roll(d * s1, 16, 1) + pltpu.roll(d * s2, HP - 16, 1)


def _colsum(v):
    return jnp.sum(v, axis=0, keepdims=True)


def _params(sem=None, vmem=VMEM_LIMIT):
    return pltpu.CompilerParams(dimension_semantics=sem, vmem_limit_bytes=vmem)


def _resident(shape):
    nd = len(shape)
    return pl.BlockSpec(shape, lambda *_: (0,) * nd, pipeline_mode=pl.Buffered(1))


def _rows(tm, width, col=0):
    return pl.BlockSpec((tm, width), lambda i: (i, col))


def _mesh_pos():
    return lax.axis_index("x"), lax.axis_index("y"), lax.axis_index("c")


def _flip(v, bit):
    return (1 - v) if bit else v


_ANY = pl.BlockSpec(memory_space=pl.ANY)
_MESH = pl.DeviceIdType.MESH


def allgather_rows(shard):
    r, w = shard.shape

    def body(x_ref, out_ref, send_sems, recv_sems, local_sem):
        x, y, c = _mesh_pos()
        me, sibling = (x, y, c), (x, y, 1 - c)
        chips = [(1 - x, y), (x, 1 - y), (1 - x, 1 - y)]

        def slot(px, py, pc):
            return out_ref.at[4 * px + 2 * py + pc]

        def copy(k, block, to, src=None):
            return pltpu.make_async_remote_copy(
                src_ref=slot(*block) if src is None else src, dst_ref=slot(*block),
                send_sem=send_sems.at[k], recv_sem=recv_sems.at[k], device_id=to, device_id_type=_MESH)

        mine = pltpu.make_async_copy(x_ref, slot(*me), local_sem)
        mine.start()
        first = [copy(0, me, sibling, src=x_ref)]
        first += [copy(1 + j, me, (*chip, c), src=x_ref) for j, chip in enumerate(chips)]
        for cp in first:
            cp.start()
        passed = [copy(4 + j, (*chip, c), sibling) for j, chip in enumerate(chips)]
        for j, chip in enumerate(chips):
            copy(1 + j, (*chip, c), me).wait_recv()
            passed[j].start()
        copy(0, sibling, me).wait_recv()
        for j, chip in enumerate(chips):
            copy(4 + j, (*chip, 1 - c), me).wait_recv()
        for cp in first + passed:
            cp.wait_send()
        mine.wait()

    return pl.pallas_call(
        body, name="allgather_rows",
        out_shape=jax.ShapeDtypeStruct((N_DEV, r, w), shard.dtype),
        in_specs=[_ANY], out_specs=_ANY,
        scratch_shapes=[pltpu.SemaphoreType.DMA((7,)), pltpu.SemaphoreType.DMA((7,)), pltpu.SemaphoreType.DMA],
    )(shard)


def exchange_sibling(g):
    _, r, w = g.shape

    def body(g_ref, own_ref, got_ref, send_sems, recv_sems, local_sems):
        x, y, c = _mesh_pos()
        sibling = (x, y, 1 - c)
        local, remote = [], []
        for j in range(N_CHIP):
            local.append(pltpu.make_async_copy(g_ref.at[2 * j + c], own_ref.at[j], local_sems.at[j]))
            remote.append(pltpu.make_async_remote_copy(
                src_ref=g_ref.at[2 * j + (1 - c)], dst_ref=got_ref.at[j],
                send_sem=send_sems.at[j], recv_sem=recv_sems.at[j], device_id=sibling, device_id_type=_MESH))
        for cp in remote + local:
            cp.start()
        for cp in remote + local:
            cp.wait()

    shp = jax.ShapeDtypeStruct((N_CHIP, r, w), g.dtype)
    return pl.pallas_call(
        body, name="exchange_sibling", out_shape=(shp, shp), in_specs=[_ANY], out_specs=(_ANY, _ANY),
        scratch_shapes=[pltpu.SemaphoreType.DMA((N_CHIP,)), pltpu.SemaphoreType.DMA((N_CHIP,)),
                        pltpu.SemaphoreType.DMA((N_CHIP,))],
    )(g)


def exchange_chips(part):
    _, r, w = part.shape

    def body(p_ref, land_ref, send_sems, recv_sems, local_sem):
        x, y, c = _mesh_pos()
        my_chip = 2 * x + y
        mine = pltpu.make_async_copy(p_ref.at[my_chip], land_ref.at[my_chip], local_sem)
        mine.start()
        copies = []
        for k in (1, 2, 3):
            tx, ty = _flip(x, k & 2), _flip(y, k & 1)
            copies.append(pltpu.make_async_remote_copy(
                src_ref=p_ref.at[2 * tx + ty], dst_ref=land_ref.at[my_chip],
                send_sem=send_sems.at[k - 1], recv_sem=recv_sems.at[k - 1],
                device_id=(tx, ty, c), device_id_type=_MESH))
        for cp in copies:
            cp.start()
        for cp in copies:
            cp.wait()
        mine.wait()

    return pl.pallas_call(
        body, name="exchange_chips", out_shape=jax.ShapeDtypeStruct((N_CHIP, r, w), part.dtype),
        in_specs=[_ANY], out_specs=_ANY,
        scratch_shapes=[pltpu.SemaphoreType.DMA((3,)), pltpu.SemaphoreType.DMA((3,)), pltpu.SemaphoreType.DMA],
    )(part)


def allreduce_stats(st):
    def body(st_ref, out_ref, gath, send_sems, recv_sems):
        x, y, c = _mesh_pos()
        me = 4 * x + 2 * y + c
        gath[me] = st_ref[...]
        copies = []
        for k in range(1, N_DEV):
            peer = (_flip(x, k & 4), _flip(y, k & 2), _flip(c, k & 1))
            copies.append(pltpu.make_async_remote_copy(
                src_ref=st_ref, dst_ref=gath.at[me], send_sem=send_sems.at[k - 1], recv_sem=recv_sems.at[k - 1],
                device_id=peer, device_id_type=_MESH))
        for cp in copies:
            cp.start()
        for cp in copies:
            cp.wait()
        acc = gath[0]
        for d in range(1, N_DEV):
            acc = acc + gath[d]
        out_ref[...] = acc

    vm = pl.BlockSpec(memory_space=pltpu.VMEM)
    return pl.pallas_call(
        body, name="allreduce_stats", out_shape=jax.ShapeDtypeStruct(st.shape, F32),
        in_specs=[vm], out_specs=vm,
        scratch_shapes=[pltpu.VMEM((N_DEV,) + st.shape, F32),
                        pltpu.SemaphoreType.DMA((N_DEV - 1,)), pltpu.SemaphoreType.DMA((N_DEV - 1,))],
    )(st)


def add_pairs(a, b):
    n, r, w = a.shape
    tr = 272 if r % 272 == 0 else 16

    def body(a_ref, b_ref, o_ref):
        o_ref[...] = (a_ref[...].astype(F32) + b_ref[...].astype(F32)).astype(o_ref.dtype)

    spec = pl.BlockSpec((1, tr, w), lambda i, j: (i, j, 0))
    return pl.pallas_call(
        body, name="add_pairs", grid=(n, r // tr), out_shape=jax.ShapeDtypeStruct(a.shape, a.dtype),
        in_specs=[spec, spec], out_specs=spec, compiler_params=_params(("parallel", "parallel")),
    )(a, b)


def sum_chips(land):
    n, r, w = land.shape
    tr = 272 if r % 272 == 0 else 16

    def body(l_ref, o_ref):
        acc = l_ref[0].astype(F32)
        for s in range(1, N_CHIP):
            acc = acc + l_ref[s].astype(F32)
        o_ref[...] = acc

    return pl.pallas_call(
        body, name="sum_chips", grid=(r // tr,), out_shape=jax.ShapeDtypeStruct((r, w), F32),
        in_specs=[pl.BlockSpec((n, tr, w), lambda i: (0, i, 0))], out_specs=pl.BlockSpec((tr, w), lambda i: (i, 0)),
        compiler_params=_params(("parallel",)),
    )(land)


def in_proj(x, g_mix, w_in_t, tm):
    t = x.shape[0]
    nc = 512

    def body(x_ref, g_ref, w_ref, z_ref):
        xv = x_ref[...]
        h = (xv * _rstd(xv, D_MODEL) * g_ref[...]).astype(BF16)
        for cidx in range(ZP // nc):
            z_ref[:, cidx * nc:(cidx + 1) * nc] = _dot_nt(h, w_ref[cidx * nc:(cidx + 1) * nc, :])

    return pl.pallas_call(
        body, name="in_proj", grid=(t // tm,), out_shape=jax.ShapeDtypeStruct((t, ZP), F32),
        in_specs=[_rows(tm, D_MODEL), _resident((1, D_MODEL)), _resident((ZP, D_MODEL))],
        out_specs=_rows(tm, ZP), compiler_params=_params(("parallel",)),
    )(x, g_mix, w_in_t)


def attn_prep(zp, tabs, g_qa, g_kva, g_qn, g_kn, w_qb_t, w_kvb_t, tm, s_len):
    t = zp.shape[0]
    nsb = s_len // tm
    scale_a = (QK_NOPE + QK_ROPE) ** -0.5
    scale_b = HD_B ** -0.5

    def body(qb_ref, qlat_ref, kb_ref, vb_ref, ckv_ref, kpe_ref, tab_ref, gqa_ref, gkva_ref, gqn_ref, gkn_ref,
             wqb_ref, wkvb_ref, qa_o, ka_o, va_o, qb_o, kb_o, vb_o, cq_o, ckvn_o):
        ca, s1a, s2a = tab_ref[0], tab_ref[1], tab_ref[2]
        ck = tab_ref[3]
        cb, s1b, s2b = tab_ref[4], tab_ref[5], tab_ref[6]
        ql = qlat_ref[...]
        cq = (ql * _rstd(ql, Q_LORA) * gqa_ref[...]).astype(BF16)
        cq_o[...] = cq
        qa = _dot_nt(cq, wqb_ref[...])
        for h in range(H_A):
            sl = slice(h * HP, (h + 1) * HP)
            qa_o[:, sl] = (_rope_fwd(qa[:, sl], ca, s1a, s2a) * scale_a).astype(BF16)
        cr = ckv_ref[...]
        ckv = (cr * _rstd(cr, KV_LORA) * gkva_ref[...]).astype(BF16)
        ckvn_o[...] = ckv
        kva = _dot_nt(ckv, wkvb_ref[...])
        kpe = _rope_fwd(kpe_ref[...], ck, s1a, s2a)
        for h in range(H_A):
            sl = slice(h * HP, (h + 1) * HP)
            ka_o[:, sl] = (kva[:, sl] + kpe).astype(BF16)
        va_o[...] = kva[:, H_A * HP:].astype(BF16)
        gqn, gkn = gqn_ref[...], gkn_ref[...]
        for h in range(H_B):
            sl = slice(h * HP, (h + 1) * HP)
            xs = qb_ref[:, sl]
            y = xs * _rstd(xs, HD_B) * gqn
            qb_o[:, sl] = (_rope_fwd(y, cb, s1b, s2b) * scale_b).astype(BF16)
        for h in range(KV_B):
            sl = slice(h * HP, (h + 1) * HP)
            xs = kb_ref[:, sl]
            y = xs * _rstd(xs, HD_B) * gkn
            kb_o[:, sl] = _rope_fwd(y, cb, s1b, s2b).astype(BF16)
        vb_o[...] = vb_ref[...].astype(BF16)

    def o(width):
        return jax.ShapeDtypeStruct((t, width), BF16)

    return pl.pallas_call(
        body, name="attn_prep", grid=(t // tm,),
        out_shape=(o(H_A * HP), o(H_A * HP), o(H_A * HP), o(H_B * HP), o(KV_B * HP), o(KV_B * HP), o(Q_LORA), o(KV_LORA)),
        in_specs=[_rows(tm, 1024, 0), _rows(tm, 256, 12), _rows(tm, 256, 13), _rows(tm, 256, 14),
                  _rows(tm, 128, 30), _rows(tm, 128, 31),
                  pl.BlockSpec((7, tm, HP), lambda i: (0, i % nsb, 0)),
                  _resident((1, Q_LORA)), _resident((1, KV_LORA)), _resident((1, HP)), _resident((1, HP)),
                  _resident((H_A * HP, Q_LORA)), _resident((2 * H_A * HP, KV_LORA))],
        out_specs=(_rows(tm, H_A * HP), _rows(tm, H_A * HP), _rows(tm, H_A * HP), _rows(tm, H_B * HP),
                   _rows(tm, KV_B * HP), _rows(tm, KV_B * HP), _rows(tm, Q_LORA), _rows(tm, KV_LORA)),
        compiler_params=_params(("parallel",)),
    )(zp, zp, zp, zp, zp, zp, tabs, g_qa, g_kva, g_qn, g_kn, w_qb_t, w_kvb_t)


def attn_fwd(q, k, v, n_b, s_len, tq, name):
    t = q.shape[0]
    n_h, n_hk = q.shape[1] // HP, k.shape[1] // HP
    grp = n_h // n_hk
    nq = s_len // tq

    def body(q_ref, k_ref, v_ref, o_ref):
        s = _dot_nt(q_ref[...], k_ref[...])
        p = jnp.exp(s - jnp.max(s, axis=-1, keepdims=True))
        pr = (p * (1.0 / jnp.sum(p, axis=-1, keepdims=True))).astype(BF16)
        o_ref[...] = _dot_nn(pr, v_ref[...]).astype(o_ref.dtype)

    qspec = pl.BlockSpec((tq, HP), lambda b, h, i: (b * nq + i, h))
    kspec = pl.BlockSpec((s_len, HP), lambda b, h, i: (b, h // grp))
    return pl.pallas_call(
        body, name=name, grid=(n_b, n_h, nq), out_shape=jax.ShapeDtypeStruct((t, n_h * HP), BF16),
        in_specs=[qspec, kspec, kspec], out_specs=qspec,
        compiler_params=_params(("parallel", "parallel", "parallel")),
    )(q, k, v)


def merge_fwd(oa, ob, zp, x, w_oa_t, w_ob_t, w_o, tm):
    t = x.shape[0]

    def body(oa_ref, ob_ref, ga_ref, gb_ref, x_ref, woa_ref, wob_ref, wo_ref, x1_o, mg_o):
        ya = _dot_nt(oa_ref[...], woa_ref[...])
        yb = _dot_nt(ob_ref[...], wob_ref[...])
        merged = (jax.nn.sigmoid(ga_ref[...]) * ya + jax.nn.sigmoid(gb_ref[...]) * yb).astype(BF16)
        mg_o[...] = merged
        x1_o[...] = x_ref[...] + _dot_nn(merged, wo_ref[...])

    return pl.pallas_call(
        body, name="merge_fwd", grid=(t // tm,),
        out_shape=(jax.ShapeDtypeStruct((t, D_MODEL), F32), jax.ShapeDtypeStruct((t, D_MODEL), BF16)),
        in_specs=[_rows(tm, H_A * HP), _rows(tm, H_B * HP), _rows(tm, 1024, 1), _rows(tm, 1024, 2), _rows(tm, D_MODEL),
                  _resident((D_MODEL, H_A * HP)), _resident((D_MODEL, H_B * HP)), _resident((D_MODEL, D_MODEL))],
        out_specs=(_rows(tm, D_MODEL), _rows(tm, D_MODEL)), compiler_params=_params(("parallel",)),
    )(oa, ob, zp, zp, x, w_oa_t, w_ob_t, w_o)


def mlp_fwd(x1, g_mlp, w_up_t, w_down, tm):
    t = x1.shape[0]
    fc = 1024

    def body(x_ref, g_ref, wup_ref, wdn_ref, x2_o, u_o):
        xv = x_ref[...]
        h2 = (xv * _rstd(xv, D_MODEL) * g_ref[...]).astype(BF16)
        acc = xv
        for cidx in range(D_FF // fc):
            sl = slice(cidx * fc, (cidx + 1) * fc)
            u = jnp.maximum(_dot_nt(h2, wup_ref[sl, :]), 0.0)
            u_o[:, sl] = u.astype(BF16)
            acc = acc + _dot_nn((u * u).astype(BF16), wdn_ref[sl, :])
        x2_o[...] = acc

    return pl.pallas_call(
        body, name="mlp_fwd", grid=(t // tm,),
        out_shape=(jax.ShapeDtypeStruct((t, D_MODEL), F32), jax.ShapeDtypeStruct((t, D_FF), BF16)),
        in_specs=[_rows(tm, D_MODEL), _resident((1, D_MODEL)), _resident((D_FF, D_MODEL)), _resident((D_FF, D_MODEL))],
        out_specs=(_rows(tm, D_MODEL), _rows(tm, D_FF)), compiler_params=_params(("parallel",)),
    )(x1, g_mlp, w_up_t, w_down)


def ple_loss_bwd(x2, p, tgt, g_ple, g_final, w_pg, w_ple_t, tm):
    t = x2.shape[0]
    inv_d = 1.0 / D_MODEL

    def body(x2_ref, p_ref, tg_ref, gp_ref, gf_ref, wpg_ref, wple_ref, dx2_o, dt_o, h3_o, dpe_o, st_o):
        @pl.when(pl.program_id(0) == 0)
        def _():
            st_o[...] = jnp.zeros_like(st_o)

        x2v = x2_ref[...]
        gp, gf = gp_ref[...], gf_ref[...]
        r2 = _rstd(x2v, D_MODEL)
        xh2 = x2v * r2
        h3 = (xh2 * gp).astype(BF16)
        h3_o[...] = h3
        gate = jax.nn.sigmoid(_dot_nn(h3, wpg_ref[...]))
        pe = _dot_nt(p_ref[...].astype(BF16), wple_ref[...])
        x3 = x2v + gate * pe
        r3 = _rstd(x3, D_MODEL)
        xh3 = x3 * r3
        err = xh3 * gf - tg_ref[...]
        dy = err * inv_d
        dx3 = _rms_bwd(dy, xh3, r3, gf, D_MODEL)
        dpe_o[...] = (dx3 * gate).astype(BF16)
        dt = (dx3 * pe * gate * (1.0 - gate)).astype(BF16)
        dt_o[...] = dt
        dh3 = _dot_nt(dt, wpg_ref[...])
        dx2_o[...] = dx3 + _rms_bwd(dh3, xh2, r2, gp, D_MODEL)
        st_o[0:1, :] += _colsum(dh3 * xh2)
        st_o[1:2, :] += _colsum(dy * xh3)
        st_o[2:3, :] += _colsum(err * err) * (0.5 * inv_d)

    bf = jax.ShapeDtypeStruct((t, D_MODEL), BF16)
    return pl.pallas_call(
        body, name="ple_loss_bwd", grid=(t // tm,),
        out_shape=(jax.ShapeDtypeStruct((t, D_MODEL), F32), bf, bf, bf, jax.ShapeDtypeStruct((3, D_MODEL), F32)),
        in_specs=[_rows(tm, D_MODEL), _rows(tm, PLE_DIM), _rows(tm, D_MODEL), _resident((1, D_MODEL)), _resident((1, D_MODEL)),
                  _resident((D_MODEL, D_MODEL)), _resident((D_MODEL, PLE_DIM))],
        out_specs=(_rows(tm, D_MODEL), _rows(tm, D_MODEL), _rows(tm, D_MODEL), _rows(tm, D_MODEL),
                   pl.BlockSpec((3, D_MODEL), lambda i: (0, 0))),
        compiler_params=_params(("arbitrary",)),
    )(x2, p, tgt, g_ple, g_final, w_pg, w_ple_t)


def mlp_bwd(dx2, x1, u, g_mlp, w_up_t, w_down, tm):
    t = x1.shape[0]
    fc = 1024

    def body(dx2_ref, x1_ref, u_ref, g_ref, wup_ref, wdn_ref, dx1_o, da_o, h2_o, st_o):
        @pl.when(pl.program_id(0) == 0)
        def _():
            st_o[...] = jnp.zeros_like(st_o)

        d2 = dx2_ref[...]
        d2b = d2.astype(BF16)
        dh2 = jnp.zeros((tm, D_MODEL), F32)
        for cidx in range(D_FF // fc):
            sl = slice(cidx * fc, (cidx + 1) * fc)
            da = (_dot_nt(d2b, wdn_ref[sl, :]) * (2.0 * u_ref[:, sl].astype(F32))).astype(BF16)
            da_o[:, sl] = da
            dh2 = dh2 + _dot_nn(da, wup_ref[sl, :])
        xv = x1_ref[...]
        g = g_ref[...]
        r1 = _rstd(xv, D_MODEL)
        xh1 = xv * r1
        h2_o[...] = (xh1 * g).astype(BF16)
        st_o[...] += _colsum(dh2 * xh1)
        dx1_o[...] = d2 + _rms_bwd(dh2, xh1, r1, g, D_MODEL)

    return pl.pallas_call(
        body, name="mlp_bwd", grid=(t // tm,),
        out_shape=(jax.ShapeDtypeStruct((t, D_MODEL), F32), jax.ShapeDtypeStruct((t, D_FF), BF16),
                   jax.ShapeDtypeStruct((t, D_MODEL), BF16), jax.ShapeDtypeStruct((1, D_MODEL), F32)),
        in_specs=[_rows(tm, D_MODEL), _rows(tm, D_MODEL), _rows(tm, D_FF), _resident((1, D_MODEL)),
                  _resident((D_FF, D_MODEL)), _resident((D_FF, D_MODEL))],
        out_specs=(_rows(tm, D_MODEL), _rows(tm, D_FF), _rows(tm, D_MODEL), pl.BlockSpec((1, D_MODEL), lambda i: (0, 0))),
        compiler_params=_params(("arbitrary",)),
    )(dx2, x1, u, g_mlp, w_up_t, w_down)


def merge_bwd(dx1, oa, ob, zp, w_oa_t, w_ob_t, w_o, tm):
    t = dx1.shape[0]

    def body(dx1_ref, oa_ref, ob_ref, ga_ref, gb_ref, woa_ref, wob_ref, wo_ref, doa_o, dob_o, dg_o, dya_o, dyb_o):
        dm = _dot_nt(dx1_ref[...].astype(BF16), wo_ref[...])
        for o_ref, g_ref, w_ref, do_o, dy_o, col in ((oa_ref, ga_ref, woa_ref, doa_o, dya_o, 0),
                                                     (ob_ref, gb_ref, wob_ref, dob_o, dyb_o, 1)):
            yv = _dot_nt(o_ref[...], w_ref[...])
            sg = jax.nn.sigmoid(g_ref[...])
            dyv = (dm * sg).astype(BF16)
            dy_o[...] = dyv
            dg_o[:, col * D_MODEL:(col + 1) * D_MODEL] = (dm * yv * sg * (1.0 - sg)).astype(BF16)
            do_o[...] = _dot_nn(dyv, w_ref[...]).astype(BF16)

    bf = jax.ShapeDtypeStruct((t, D_MODEL), BF16)
    return pl.pallas_call(
        body, name="merge_bwd", grid=(t // tm,),
        out_shape=(bf, bf, jax.ShapeDtypeStruct((t, 2 * D_MODEL), BF16), bf, bf),
        in_specs=[_rows(tm, D_MODEL), _rows(tm, H_A * HP), _rows(tm, H_B * HP), _rows(tm, 1024, 1), _rows(tm, 1024, 2),
                  _resident((D_MODEL, H_A * HP)), _resident((D_MODEL, H_B * HP)), _resident((D_MODEL, D_MODEL))],
        out_specs=(_rows(tm, D_MODEL), _rows(tm, D_MODEL), _rows(tm, 2 * D_MODEL), _rows(tm, D_MODEL), _rows(tm, D_MODEL)),
        compiler_params=_params(("parallel",)),
    )(dx1, oa, ob, zp, zp, w_oa_t, w_ob_t, w_o)


def attn_bwd(q, k, v, do, n_b, s_len, tq, name):
    t = q.shape[0]
    n_h, n_hk = q.shape[1] // HP, k.shape[1] // HP
    grp = n_h // n_hk
    nq = s_len // tq

    def body(q_ref, k_ref, v_ref, do_ref, dq_o, dk_o, dv_o):
        @pl.when((pl.program_id(2) == 0) & (pl.program_id(3) == 0))
        def _():
            dk_o[...] = jnp.zeros_like(dk_o)
            dv_o[...] = jnp.zeros_like(dv_o)

        qv, kv, dov = q_ref[...], k_ref[...], do_ref[...]
        s = _dot_nt(qv, kv)
        p = jnp.exp(s - jnp.max(s, axis=-1, keepdims=True))
        p = p * (1.0 / jnp.sum(p, axis=-1, keepdims=True))
        dp = _dot_nt(dov, v_ref[...])
        ds = (p * (dp - jnp.sum(p * dp, axis=-1, keepdims=True))).astype(BF16)
        dq_o[...] = _dot_nn(ds, kv)
        dk_o[...] += _dot_tn(ds, qv)
        dv_o[...] += _dot_tn(p.astype(BF16), dov)

    qspec = pl.BlockSpec((tq, HP), lambda b, hk, g, i: (b * nq + i, hk * grp + g))
    kspec = pl.BlockSpec((s_len, HP), lambda b, hk, g, i: (b, hk))
    return pl.pallas_call(
        body, name=name, grid=(n_b, n_hk, grp, nq),
        out_shape=(jax.ShapeDtypeStruct((t, n_h * HP), F32), jax.ShapeDtypeStruct((t, n_hk * HP), F32),
                   jax.ShapeDtypeStruct((t, n_hk * HP), F32)),
        in_specs=[qspec, kspec, kspec, qspec], out_specs=(qspec, kspec, kspec),
        compiler_params=_params(("parallel", "parallel", "arbitrary", "arbitrary")),
    )(q, k, v, do)


def prep_bwd(dqa, dka, dva, dqb, dkb, dvb, zp, tabs, g_qa, g_kva, g_qn, g_kn, w_qb_t, w_kvb_t, tm, s_len):
    t = zp.shape[0]
    nsb = s_len // tm
    scale_a = (QK_NOPE + QK_ROPE) ** -0.5
    scale_b = HD_B ** -0.5

    def body(dqa_ref, dka_ref, dva_ref, dqb_ref, dkb_ref, dvb_ref, qb_ref, qlat_ref, kb_ref, ckv_ref, tab_ref,
             gqa_ref, gkva_ref, gqn_ref, gkn_ref, wqb_ref, wkvb_ref, dzq_o, dsm_o, dqap_o, dkva_o, st_o):
        @pl.when(pl.program_id(0) == 0)
        def _():
            st_o[...] = jnp.zeros_like(st_o)

        ca, s1a, s2a = tab_ref[0], tab_ref[1], tab_ref[2]
        ck = tab_ref[3]
        cb, s1b, s2b = tab_ref[4], tab_ref[5], tab_ref[6]
        for h in range(H_A):
            sl = slice(h * HP, (h + 1) * HP)
            dqap_o[:, sl] = _rope_bwd(dqa_ref[:, sl] * scale_a, ca, s1a, s2a).astype(BF16)
        dcq = _dot_nn(dqap_o[...], wqb_ref[...])
        ql = qlat_ref[...]
        rq = _rstd(ql, Q_LORA)
        xh = ql * rq
        gqa = gqa_ref[...]
        st_o[0:1, :] += _colsum(dcq * xh)
        dsm_o[:, 0:256] = _rms_bwd(dcq, xh, rq, gqa, Q_LORA).astype(BF16)
        dkpe = jnp.zeros((tm, HP), F32)
        for h in range(H_A):
            sl = slice(h * HP, (h + 1) * HP)
            dk = dka_ref[:, sl]
            dkpe = dkpe + dk
            dkva_o[:, sl] = dk.astype(BF16)
        dkva_o[:, H_A * HP:] = dva_ref[...].astype(BF16)
        dsm_o[:, 896:1024] = _rope_bwd(dkpe, ck, s1a, s2a).astype(BF16)
        dckv = _dot_nn(dkva_o[...], wkvb_ref[...])
        cr = ckv_ref[...]
        rk = _rstd(cr, KV_LORA)
        xh = cr * rk
        st_o[1:2, 0:128] += _colsum(dckv * xh)
        dsm_o[:, 768:896] = _rms_bwd(dckv, xh, rk, gkva_ref[...], KV_LORA).astype(BF16)
        gqn, gkn = gqn_ref[...], gkn_ref[...]
        dgq = jnp.zeros((1, HP), F32)
        for h in range(H_B):
            sl = slice(h * HP, (h + 1) * HP)
            dy = _rope_bwd(dqb_ref[:, sl] * scale_b, cb, s1b, s2b)
            xs = qb_ref[:, sl]
            r = _rstd(xs, HD_B)
            xh = xs * r
            dgq = dgq + _colsum(dy * xh)
            dzq_o[:, sl] = _rms_bwd(dy, xh, r, gqn, HD_B).astype(BF16)
        st_o[2:3, 0:128] += dgq
        dgk = jnp.zeros((1, HP), F32)
        for h in range(KV_B):
            sl = slice(h * HP, (h + 1) * HP)
            dy = _rope_bwd(dkb_ref[:, sl], cb, s1b, s2b)
            xs = kb_ref[:, sl]
            r = _rstd(xs, HD_B)
            xh = xs * r
            dgk = dgk + _colsum(dy * xh)
            dsm_o[:, 256 + h * HP:256 + (h + 1) * HP] = _rms_bwd(dy, xh, r, gkn, HD_B).astype(BF16)
        st_o[3:4, 0:128] += dgk
        dsm_o[:, 512:768] = dvb_ref[...].astype(BF16)

    bf = jax.ShapeDtypeStruct((t, 1024), BF16)
    return pl.pallas_call(
        body, name="prep_bwd", grid=(t // tm,),
        out_shape=(bf, bf, bf, jax.ShapeDtypeStruct((t, 2048), BF16), jax.ShapeDtypeStruct((4, 256), F32)),
        in_specs=[_rows(tm, 1024), _rows(tm, 1024), _rows(tm, 1024), _rows(tm, 1024), _rows(tm, 256), _rows(tm, 256),
                  _rows(tm, 1024, 0), _rows(tm, 256, 12), _rows(tm, 256, 13), _rows(tm, 128, 30),
                  pl.BlockSpec((7, tm, HP), lambda i: (0, i % nsb, 0)),
                  _resident((1, Q_LORA)), _resident((1, KV_LORA)), _resident((1, HP)), _resident((1, HP)),
                  _resident((H_A * HP, Q_LORA)), _resident((2 * H_A * HP, KV_LORA))],
        out_specs=(_rows(tm, 1024), _rows(tm, 1024), _rows(tm, 1024), _rows(tm, 2048), pl.BlockSpec((4, 256), lambda i: (0, 0))),
        compiler_params=_params(("arbitrary",)),
    )(dqa, dka, dva, dqb, dkb, dvb, zp, zp, zp, zp, tabs, g_qa, g_kva, g_qn, g_kn, w_qb_t, w_kvb_t)


def in_bwd(dzq, dgab, dsm, x, dx1, g_mix, w_in_t, tm):
    t = x.shape[0]

    def body(dzq_ref, dg_ref, dsm_ref, x_ref, dx1_ref, g_ref, w_ref, dx_o, h_o, st_o):
        @pl.when(pl.program_id(0) == 0)
        def _():
            st_o[...] = jnp.zeros_like(st_o)

        dh = _dot_nn(dzq_ref[...], w_ref[0:1024, :])
        dh = dh + _dot_nn(dg_ref[...], w_ref[1024:3072, :])
        dh = dh + _dot_nn(dsm_ref[...], w_ref[3072:4096, :])
        xv = x_ref[...]
        g = g_ref[...]
        r = _rstd(xv, D_MODEL)
        xh = xv * r
        h_o[...] = (xh * g).astype(BF16)
        st_o[...] += _colsum(dh * xh)
        dx_o[...] = dx1_ref[...] + _rms_bwd(dh, xh, r, g, D_MODEL)

    return pl.pallas_call(
        body, name="in_bwd", grid=(t // tm,),
        out_shape=(jax.ShapeDtypeStruct((t, D_MODEL), F32), jax.ShapeDtypeStruct((t, D_MODEL), BF16),
                   jax.ShapeDtypeStruct((1, D_MODEL), F32)),
        in_specs=[_rows(tm, 1024), _rows(tm, 2048), _rows(tm, 1024), _rows(tm, D_MODEL), _rows(tm, D_MODEL),
                  _resident((1, D_MODEL)), _resident((ZP, D_MODEL))],
        out_specs=(_rows(tm, D_MODEL), _rows(tm, D_MODEL), pl.BlockSpec((1, D_MODEL), lambda i: (0, 0))),
        compiler_params=_params(("arbitrary",)),
    )(dzq, dgab, dsm, x, dx1, g_mix, w_in_t)


def matmul_tn(a, b, name, square_a=False):
    t, m = a.shape
    n = b.shape[1]
    bm = min(m, 512)
    tk = min(t, 512)

    def body(a_ref, b_ref, o_ref):
        @pl.when(pl.program_id(1) == 0)
        def _():
            o_ref[...] = jnp.zeros_like(o_ref)

        av = a_ref[...]
        if square_a:
            av = (av.astype(F32) * av.astype(F32))
        o_ref[...] += _dot_tn(av.astype(BF16), b_ref[...].astype(BF16))

    return pl.pallas_call(
        body, name=name, grid=(m // bm, t // tk), out_shape=jax.ShapeDtypeStruct((m, n), F32),
        in_specs=[pl.BlockSpec((tk, bm), lambda i, kk: (kk, i)), pl.BlockSpec((tk, n), lambda i, kk: (kk, 0))],
        out_specs=pl.BlockSpec((bm, n), lambda i, kk: (i, 0)),
        compiler_params=_params(("parallel", "arbitrary")),
    )(a, b)


def adamw(w, g, m, v, name):
    r, c = w.shape
    tr = r if r <= 256 else 256
    c1 = 1.0 - ADAM_B1 ** ADAM_STEP
    c2 = 1.0 - ADAM_B2 ** ADAM_STEP

    def body(w_ref, g_ref, m_ref, v_ref, d_o, m_o, v_o):
        gv = g_ref[...]
        mn = ADAM_B1 * m_ref[...] + (1.0 - ADAM_B1) * gv
        vn = ADAM_B2 * v_ref[...] + (1.0 - ADAM_B2) * (gv * gv)
        m_o[...] = mn
        v_o[...] = vn
        d_o[...] = -ADAM_LR * ((mn / c1) / (jnp.sqrt(vn / c2) + ADAM_EPS) + ADAM_WD * w_ref[...])

    spec = pl.BlockSpec((tr, c), lambda i: (i, 0))
    shp = jax.ShapeDtypeStruct((r, c), F32)
    return pl.pallas_call(
        body, name=name, grid=(r // tr,), out_shape=(shp, shp, shp), in_specs=[spec] * 4, out_specs=(spec,) * 3,
        compiler_params=_params(("parallel",)),
    )(w, g, m, v)


def _rope_tables(s_len):
    def angles(pos, dim):
        inv = ROPE_THETA ** (-jnp.arange(0, dim, 2, dtype=F32) / dim)
        return pos.astype(F32)[:, None] * inv[None, :]

    tpos = jnp.arange(s_len)
    a1 = angles(tpos, QK_ROPE)
    ar = angles(tpos // GRID_W, HD_B // 2)
    ac = angles(tpos % GRID_W, HD_B // 2)
    z16 = jnp.zeros((s_len, 16), F32)
    z32 = jnp.zeros((s_len, 32), F32)
    z64 = jnp.zeros((s_len, 64), F32)
    one64 = jnp.ones((s_len, 64), F32)
    c1, s1 = jnp.cos(a1), jnp.sin(a1)
    ca = jnp.concatenate([one64, c1, c1, z32], axis=1)
    ck = jnp.concatenate([z64, c1, c1, z32], axis=1)
    s1a = jnp.concatenate([z64, -s1, z16, z32], axis=1)
    s2a = jnp.concatenate([z64, z16, s1, z32], axis=1)
    cr, sr, cc, sc = jnp.cos(ar), jnp.sin(ar), jnp.cos(ac), jnp.sin(ac)
    cb = jnp.concatenate([cr, cr, cc, cc, z64], axis=1)
    s1b = jnp.concatenate([-sr, z16, -sc, z16, z64], axis=1)
    s2b = jnp.concatenate([z16, sr, z16, sc, z64], axis=1)
    return jnp.stack([ca, s1a, s2a, ck, cb, s1b, s2b])


def _pad_heads(a, n_heads, axis):
    shp = a.shape
    a = a.reshape(shp[:axis] + (n_heads, shp[axis] // n_heads) + shp[axis + 1:])
    pad = [(0, 0)] * a.ndim
    pad[axis + 1] = (0, HP - a.shape[axis + 1])
    a = jnp.pad(a, pad)
    return a.reshape(shp[:axis] + (n_heads * HP,) + shp[axis + 1:])


def _unpad_heads(a, n_heads, width, axis):
    shp = a.shape
    a = a.reshape(shp[:axis] + (n_heads, HP) + shp[axis + 1:])
    a = lax.slice_in_dim(a, 0, width, axis=axis + 1)
    return a.reshape(shp[:axis] + (n_heads * width,) + shp[axis + 1:])


def _pack_rows(blocks, lead):
    parts = []
    for name, rows in PACK:
        b = blocks[name]
        padr = rows - b.shape[-2]
        if padr:
            b = jnp.pad(b, [(0, 0)] * (b.ndim - 2) + [(0, padr), (0, 0)])
        parts.append(b)
    return jnp.concatenate(parts, axis=len(lead))


def _expand_w_in(wt):
    z64 = jnp.zeros((64, D_MODEL), wt.dtype)
    z32 = jnp.zeros((32, D_MODEL), wt.dtype)
    return jnp.concatenate([
        _pad_heads(wt[416:928], H_B, 0), wt[1184:2208], wt[2208:3232], wt[0:256],
        _pad_heads(wt[928:1056], KV_B, 0), _pad_heads(wt[1056:1184], KV_B, 0), wt[256:384],
        z64, wt[384:416], z32], axis=0)


def _collapse_w_in(dq, dg, ds):
    return jnp.concatenate([
        ds[0:256], ds[768:896], ds[960:992], _unpad_heads(dq, H_B, HD_B, 0), _unpad_heads(ds[256:512], KV_B, HD_B, 0),
        _unpad_heads(ds[512:768], KV_B, HD_B, 0), dg], axis=0)


def kernel(x, p, g_mix, w_in, g_qa, w_qb, g_kva, w_kvb, g_qn, g_kn, w_oa, w_ob, w_o, g_mlp, w_up, w_down, g_ple, w_ple_gate, w_ple, g_final, loss_target, m_g_mix, m_w_in, m_g_qa, m_w_qb, m_g_kva, m_w_kvb, m_g_qn, m_g_kn, m_w_oa, m_w_ob, m_w_o, m_g_mlp, m_w_up, m_w_down, m_g_ple, m_w_ple_gate, m_w_ple, m_g_final, v_g_mix, v_w_in, v_g_qa, v_w_qb, v_g_kva, v_w_kvb, v_g_qn, v_g_kn, v_w_oa, v_w_ob, v_w_o, v_g_mlp, v_w_up, v_w_down, v_g_ple, v_w_ple_gate, v_w_ple, v_g_final):
    n_b, s_len, _ = x.shape
    t = n_b * s_len
    tm = min(256, s_len)
    tq_f = min(512, s_len)
    tq_b = min(256, s_len)

    mats = dict(w_in=(w_in, m_w_in, v_w_in), w_qb=(w_qb, m_w_qb, v_w_qb), w_kvb=(w_kvb, m_w_kvb, v_w_kvb),
                w_oa=(w_oa, m_w_oa, v_w_oa), w_ob=(w_ob, m_w_ob, v_w_ob), w_o=(w_o, m_w_o, v_w_o),
                w_up=(w_up, m_w_up, v_w_up), w_down=(w_down, m_w_down, v_w_down),
                w_ple_gate=(w_ple_gate, m_w_ple_gate, v_w_ple_gate), w_ple=(w_ple, m_w_ple, v_w_ple))
    col_sharded = ("w_in", "w_qb", "w_kvb", "w_oa", "w_ob", "w_up", "w_ple")

    blocks = {}
    for name, (w, _, _) in mats.items():
        w2 = w[0]
        if name in col_sharded:
            w2 = w2.T
        blocks[name] = w2.reshape(-1, D_MODEL).astype(BF16)
    full = allgather_rows(_pack_rows(blocks, ()))

    def gathered(name, rows, width):
        off, _ = PACK_OFF[name]
        return full[:, off:off + rows].reshape(-1, width)

    w_in_t = _expand_w_in(gathered("w_in", 404, D_MODEL))
    w_qb_t = _pad_heads(gathered("w_qb", 24, Q_LORA), H_A, 0)
    wkvb = gathered("w_kvb", 16, KV_LORA).reshape(H_A, 2, 64, KV_LORA)
    w_kvb_t = jnp.concatenate([_pad_heads(wkvb[:, 0].reshape(-1, KV_LORA), H_A, 0),
                               _pad_heads(wkvb[:, 1].reshape(-1, KV_LORA), H_A, 0)], axis=0)
    w_oa_t = _pad_heads(gathered("w_oa", 64, H_A * V_DIM_A), H_A, 1)
    w_ob_t = _pad_heads(gathered("w_ob", 64, H_B * HD_B), H_B, 1)
    w_o_f = gathered("w_o", 128, D_MODEL)
    w_up_t = gathered("w_up", 512, D_MODEL)
    w_down_f = gathered("w_down", 512, D_MODEL)
    w_pg_f = gathered("w_ple_gate", 128, D_MODEL)
    w_ple_t = gathered("w_ple", 32, PLE_DIM)

    tabs = _rope_tables(s_len)
    g_qn_p = jnp.pad(g_qn, ((0, 0), (0, HP - HD_B)))
    g_kn_p = jnp.pad(g_kn, ((0, 0), (0, HP - HD_B)))
    xf = x.reshape(t, D_MODEL)
    pf = p.reshape(t, PLE_DIM)
    tgt = loss_target.reshape(t, D_MODEL)

    zp = in_proj(xf, g_mix, w_in_t, tm)
    qa, ka, va, qb, kb, vb, cq, ckv = attn_prep(zp, tabs, g_qa, g_kva, g_qn_p, g_kn_p, w_qb_t, w_kvb_t, tm, s_len)
    oa = attn_fwd(qa, ka, va, n_b, s_len, tq_f, "attn_a_fwd")
    ob = attn_fwd(qb, kb, vb, n_b, s_len, tq_f, "attn_b_fwd")
    x1, merged = merge_fwd(oa, ob, zp, xf, w_oa_t, w_ob_t, w_o_f, tm)
    x2, u = mlp_fwd(x1, g_mlp, w_up_t, w_down_f, tm)
    dx2, dt, h3, dpe, st_ple = ple_loss_bwd(x2, pf, tgt, g_ple, g_final.reshape(1, D_MODEL), w_pg_f, w_ple_t, tm)
    dx1, da, h2, st_mlp = mlp_bwd(dx2, x1, u, g_mlp, w_up_t, w_down_f, tm)
    doa, dob, dgab, dya, dyb = merge_bwd(dx1, oa, ob, zp, w_oa_t, w_ob_t, w_o_f, tm)
    dqa, dka, dva = attn_bwd(qa, ka, va, doa, n_b, s_len, tq_b, "attn_a_bwd")
    dqb, dkb, dvb = attn_bwd(qb, kb, vb, dob, n_b, s_len, tq_b, "attn_b_bwd")
    dzq, dsm, dqap, dkva, st_prep = prep_bwd(dqa, dka, dva, dqb, dkb, dvb, zp, tabs, g_qa, g_kva, g_qn_p, g_kn_p,
                                             w_qb_t, w_kvb_t, tm, s_len)
    grad_x, h, st_mix = in_bwd(dzq, dgab, dsm, xf, dx1, g_mix, w_in_t, tm)

    gw_in = _collapse_w_in(matmul_tn(dzq, h, "gw_in_q"), matmul_tn(dgab, h, "gw_in_g"), matmul_tn(dsm, h, "gw_in_s"))
    gw_qb = _unpad_heads(matmul_tn(dqap, cq, "gw_qb"), H_A, QK_NOPE + QK_ROPE, 0)
    gkv = matmul_tn(dkva, ckv, "gw_kvb")
    gw_kvb = jnp.stack([_unpad_heads(gkv[:H_A * HP], H_A, 64, 0).reshape(H_A, 64, KV_LORA),
                        _unpad_heads(gkv[H_A * HP:], H_A, 64, 0).reshape(H_A, 64, KV_LORA)], axis=1)
    gw_oa = _unpad_heads(matmul_tn(dya, oa, "gw_oa"), H_A, V_DIM_A, 1)
    gw_ob = _unpad_heads(matmul_tn(dyb, ob, "gw_ob"), H_B, HD_B, 1)
    gw_o = matmul_tn(merged, dx1, "gw_o")
    gw_up = matmul_tn(da, h2, "gw_up")
    gw_down = matmul_tn(u, dx2, "gw_down", square_a=True)
    gw_pg = matmul_tn(h3, dt, "gw_pg")
    gw_ple = matmul_tn(dpe, pf, "gw_ple")
    gblocks = dict(w_in=gw_in, w_qb=gw_qb, w_kvb=gw_kvb, w_oa=gw_oa, w_ob=gw_ob, w_o=gw_o, w_up=gw_up,
                   w_down=gw_down, w_ple_gate=gw_pg, w_ple=gw_ple)
    gblocks = {n: a.reshape(N_DEV, -1, D_MODEL).astype(BF16) for n, a in gblocks.items()}
    gpack = _pack_rows(gblocks, (N_DEV,))

    own, got = exchange_sibling(gpack)
    land = exchange_chips(add_pairs(own, got))
    gshard = sum_chips(land)

    def row(a, width):
        return jnp.pad(a, ((0, 0), (0, D_MODEL - width)))

    stats = jnp.concatenate([
        st_mix, row(st_prep[0:1], 256), row(st_prep[1:2], 256), row(st_prep[2:3], 256), row(st_prep[3:4], 256),
        st_mlp, st_ple[0:1], st_ple[1:2], st_ple[2:3], jnp.zeros((ST_ROWS - 9, D_MODEL), F32)], axis=0)
    stats = allreduce_stats(stats)
    loss = jnp.sum(stats[ST_LOSS])

    out_g, out_d, out_m, out_v = {}, {}, {}, {}
    for name, (w, m, v) in mats.items():
        off, _ = PACK_OFF[name]
        shard_shape = w.shape[1:]
        if name in col_sharded:
            rows_t = shard_shape[1]
            g2 = gshard[off:off + (rows_t * shard_shape[0]) // D_MODEL].reshape(rows_t, shard_shape[0]).T
        else:
            g2 = gshard[off:off + shard_shape[0]]
        d2, m2, v2 = adamw(w[0], g2, m[0], v[0], "adamw_" + name)
        out_g[name], out_d[name], out_m[name], out_v[name] = g2[None], d2[None], m2[None], v2[None]

    gains = (("g_mix", g_mix, m_g_mix, v_g_mix, ST_G_MIX, D_MODEL), ("g_qa", g_qa, m_g_qa, v_g_qa, ST_G_QA, Q_LORA),
             ("g_kva", g_kva, m_g_kva, v_g_kva, ST_G_KVA, KV_LORA), ("g_qn", g_qn, m_g_qn, v_g_qn, ST_G_QN, HD_B),
             ("g_kn", g_kn, m_g_kn, v_g_kn, ST_G_KN, HD_B), ("g_mlp", g_mlp, m_g_mlp, v_g_mlp, ST_G_MLP, D_MODEL),
             ("g_ple", g_ple, m_g_ple, v_g_ple, ST_G_PLE, D_MODEL),
             ("g_final", g_final, m_g_final, v_g_final, ST_G_FINAL, D_MODEL))

    def gain_pack(idx):
        rows_ = [row(gn[idx].reshape(1, -1), gn[5]) for gn in gains]
        return jnp.concatenate(rows_ + [jnp.zeros((ST_ROWS - len(gains), D_MODEL), F32)], axis=0)

    gd, gm, gv = adamw(gain_pack(1), stats, gain_pack(2), gain_pack(3), "adamw_gains")
    for name, w, _, _, r_, width in gains:
        for dst, src in ((out_g, stats), (out_d, gd), (out_m, gm), (out_v, gv)):
            dst[name] = src[r_, :width].reshape(w.shape)

    order = ("g_mix", "w_in", "g_qa", "w_qb", "g_kva", "w_kvb", "g_qn", "g_kn", "w_oa", "w_ob", "w_o", "g_mlp",
             "w_up", "w_down", "g_ple", "w_ple_gate", "w_ple", "g_final")
    return (loss, grad_x.reshape(x.shape), *[out_g[n] for n in order], *[out_d[n] for n in order],
            *[out_m[n] for n in order], *[out_v[n] for n in order])
```

```python
import numpy as np
import jax
import jax.numpy as jnp
from jax import lax
from jax.experimental import pallas as pl
from jax.experimental.pallas import tpu as pltpu

F32 = jnp.float32
BF16 = jnp.bfloat16

D_MODEL = 1024
EPS = 1e-6
ROPE_THETA = 10000.0
GRID_W = 64
H_A = 8
QK_NOPE = 64
QK_ROPE = 32
V_DIM_A = 64
Q_LORA = 256
KV_LORA = 128
H_B = 8
KV_B = 2
HD_B = 64
D_FF = 4 * D_MODEL
PLE_DIM = 256
HP = 128
ZP = 4096
N_DEV = 8
N_CHIP = 4

ADAM_LR = 0.001
ADAM_B1 = 0.9
ADAM_B2 = 0.999
ADAM_EPS = 1e-08
ADAM_WD = 0.01
ADAM_STEP = 10

VMEM_LIMIT = 52 * 1024 * 1024

PACK = (("w_in", 416), ("w_qb", 32), ("w_kvb", 16), ("w_oa", 64), ("w_ob", 64), ("w_o", 128),
        ("w_up", 512), ("w_down", 512), ("w_ple_gate", 128), ("w_ple", 32))
PACK_OFF = {}
_o = 0
for _n, _r in PACK:
    PACK_OFF[_n] = (_o, _r)
    _o += _r
PACK_ROWS = _o

ST_G_MIX, ST_G_QA, ST_G_KVA, ST_G_QN, ST_G_KN, ST_G_MLP, ST_G_PLE, ST_G_FINAL, ST_LOSS = range(9)
ST_ROWS = 16


def _dot_nn(a, b):
    return lax.dot_general(a, b, (((1,), (0,)), ((), ())), preferred_element_type=F32)


def _dot_nt(a, b):
    return lax.dot_general(a, b, (((1,), (1,)), ((), ())), preferred_element_type=F32)


def _dot_tn(a, b):
    return lax.dot_general(a, b, (((0,), (0,)), ((), ())), preferred_element_type=F32)


def _rstd(x, n):
    return lax.rsqrt(jnp.sum(x * x, axis=-1, keepdims=True) * (1.0 / n) + EPS)


def _rms_bwd(dy, xh, r, g, n):
    dxh = dy * g
    return r * (dxh - xh * (jnp.sum(dxh * xh, axis=-1, keepdims=True) * (1.0 / n)))


def _rope_fwd(x, c, s1, s2):
    return x * c + pltpu.roll(x, HP - 16, 1) * s1 + pltpu.roll(x, 16, 1) * s2


def _rope_bwd(d, c, s1, s2):
    return d * c + pltpu.roll(d * s1, 16, 1) + pltpu.roll(d * s2, HP - 16, 1)


def _colsum(v):
    return jnp.sum(v, axis=0, keepdims=True)


def _params(sem=None, vmem=VMEM_LIMIT):
    return pltpu.CompilerParams(dimension_semantics=sem, vmem_limit_bytes=vmem)


def _resident(shape):
    nd = len(shape)
    return pl.BlockSpec(shape, lambda *_: (0,) * nd, pipeline_mode=pl.Buffered(1))


def _rows(tm, width, col=0):
    return pl.BlockSpec((tm, width), lambda i: (i, col))


def _mesh_pos():
    return lax.axis_index("x"), lax.axis_index("y"), lax.axis_index("c")


def _flip(v, bit):
    return (1 - v) if bit else v


_ANY = pl.BlockSpec(memory_space=pl.ANY)
_MESH = pl.DeviceIdType.MESH


def allgather_rows(shard):
    r, w = shard.shape

    def body(x_ref, out_ref, send_sems, recv_sems, local_sem):
        x, y, c = _mesh_pos()
        me, sibling = (x, y, c), (x, y, 1 - c)
        chips = [(1 - x, y), (x, 1 - y), (1 - x, 1 - y)]

        def slot(px, py, pc):
            return out_ref.at[4 * px + 2 * py + pc]

        def copy(k, block, to, src=None):
            return pltpu.make_async_remote_copy(
                src_ref=slot(*block) if src is None else src, dst_ref=slot(*block),
                send_sem=send_sems.at[k], recv_sem=recv_sems.at[k], device_id=to, device_id_type=_MESH)

        mine = pltpu.make_async_copy(x_ref, slot(*me), local_sem)
        mine.start()
        first = [copy(0, me, sibling, src=x_ref)]
        first += [copy(1 + j, me, (*chip, c), src=x_ref) for j, chip in enumerate(chips)]
        for cp in first:
            cp.start()
        passed = [copy(4 + j, (*chip, c), sibling) for j, chip in enumerate(chips)]
        for j, chip in enumerate(chips):
            copy(1 + j, (*chip, c), me).wait_recv()
            passed[j].start()
        copy(0, sibling, me).wait_recv()
        for j, chip in enumerate(chips):
            copy(4 + j, (*chip, 1 - c), me).wait_recv()
        for cp in first + passed:
            cp.wait_send()
        mine.wait()

    return pl.pallas_call(
        body, name="allgather_rows",
        out_shape=jax.ShapeDtypeStruct((N_DEV, r, w), shard.dtype),
        in_specs=[_ANY], out_specs=_ANY,
        scratch_shapes=[pltpu.SemaphoreType.DMA((7,)), pltpu.SemaphoreType.DMA((7,)), pltpu.SemaphoreType.DMA],
    )(shard)


def exchange_sibling(g):
    _, r, w = g.shape

    def body(g_ref, got_ref, send_sems, recv_sems):
        x, y, c = _mesh_pos()
        copies = [pltpu.make_async_remote_copy(
            src_ref=g_ref.at[2 * j + (1 - c)], dst_ref=got_ref.at[j], send_sem=send_sems.at[j], recv_sem=recv_sems.at[j],
            device_id=(x, y, 1 - c), device_id_type=_MESH) for j in range(N_CHIP)]
        for cp in copies:
            cp.start()
        for cp in copies:
            cp.wait()

    return pl.pallas_call(
        body, name="exchange_sibling", out_shape=jax.ShapeDtypeStruct((N_CHIP, r, w), g.dtype),
        in_specs=[_ANY], out_specs=_ANY,
        scratch_shapes=[pltpu.SemaphoreType.DMA((N_CHIP,)), pltpu.SemaphoreType.DMA((N_CHIP,))],
    )(g)


def exchange_chips(part):
    _, r, w = part.shape

    def body(p_ref, land_ref, send_sems, recv_sems):
        x, y, c = _mesh_pos()
        copies = []
        for k in (1, 2, 3):
            tx, ty = _flip(x, k & 2), _flip(y, k & 1)
            copies.append(pltpu.make_async_remote_copy(
                src_ref=p_ref.at[2 * tx + ty], dst_ref=land_ref.at[k - 1],
                send_sem=send_sems.at[k - 1], recv_sem=recv_sems.at[k - 1],
                device_id=(tx, ty, c), device_id_type=_MESH))
        for cp in copies:
            cp.start()
        for cp in copies:
            cp.wait()

    return pl.pallas_call(
        body, name="exchange_chips", out_shape=jax.ShapeDtypeStruct((N_CHIP - 1, r, w), part.dtype),
        in_specs=[_ANY], out_specs=_ANY,
        scratch_shapes=[pltpu.SemaphoreType.DMA((3,)), pltpu.SemaphoreType.DMA((3,))],
    )(part)


def allreduce_stats(st):
    def body(st_ref, out_ref, gath, send_sems, recv_sems):
        x, y, c = _mesh_pos()
        me = 4 * x + 2 * y + c
        gath[me] = st_ref[...]
        copies = []
        for k in range(1, N_DEV):
            peer = (_flip(x, k & 4), _flip(y, k & 2), _flip(c, k & 1))
            copies.append(pltpu.make_async_remote_copy(
                src_ref=st_ref, dst_ref=gath.at[me], send_sem=send_sems.at[k - 1], recv_sem=recv_sems.at[k - 1],
                device_id=peer, device_id_type=_MESH))
        for cp in copies:
            cp.start()
        for cp in copies:
            cp.wait()
        acc = gath[0]
        for d in range(1, N_DEV):
            acc = acc + gath[d]
        out_ref[...] = acc

    vm = pl.BlockSpec(memory_space=pltpu.VMEM)
    return pl.pallas_call(
        body, name="allreduce_stats", out_shape=jax.ShapeDtypeStruct(st.shape, F32),
        in_specs=[vm], out_specs=vm,
        scratch_shapes=[pltpu.VMEM((N_DEV,) + st.shape, F32),
                        pltpu.SemaphoreType.DMA((N_DEV - 1,)), pltpu.SemaphoreType.DMA((N_DEV - 1,))],
    )(st)


def add_pairs(g, got, core):
    n, r, w = got.shape
    tr = 272 if r % 272 == 0 else 16

    def body(c_ref, a_ref, b_ref, o_ref):
        o_ref[...] = (a_ref[...].astype(F32) + b_ref[...].astype(F32)).astype(o_ref.dtype)

    spec = pl.BlockSpec((1, tr, w), lambda i, j, c: (i, j, 0))
    return pl.pallas_call(
        body, name="add_pairs", out_shape=jax.ShapeDtypeStruct(got.shape, got.dtype),
        grid_spec=pltpu.PrefetchScalarGridSpec(
            num_scalar_prefetch=1, grid=(n, r // tr),
            in_specs=[pl.BlockSpec((1, tr, w), lambda i, j, c: (2 * i + c[0], j, 0)), spec], out_specs=spec),
        compiler_params=_params(("parallel", "parallel")),
    )(core, g, got)


def sum_chips(part, land, chip):
    _, r, w = part.shape
    tr = 272 if r % 272 == 0 else 16

    def body(c_ref, p_ref, l_ref, o_ref):
        acc = p_ref[0].astype(F32)
        for s in range(N_CHIP - 1):
            acc = acc + l_ref[s].astype(F32)
        o_ref[...] = acc

    return pl.pallas_call(
        body, name="sum_chips", out_shape=jax.ShapeDtypeStruct((r, w), F32),
        grid_spec=pltpu.PrefetchScalarGridSpec(
            num_scalar_prefetch=1, grid=(r // tr,),
            in_specs=[pl.BlockSpec((1, tr, w), lambda i, c: (c[0], i, 0)), pl.BlockSpec((N_CHIP - 1, tr, w), lambda i, c: (0, i, 0))],
            out_specs=pl.BlockSpec((tr, w), lambda i, c: (i, 0))),
        compiler_params=_params(("parallel",)),
    )(chip, part, land)


def in_proj(x, g_mix, w_in_t, tm):
    t = x.shape[0]
    nc = 512

    def body(x_ref, g_ref, w_ref, z_ref):
        xv = x_ref[...]
        h = (xv * _rstd(xv, D_MODEL) * g_ref[...]).astype(BF16)
        for cidx in range(ZP // nc):
            z_ref[:, cidx * nc:(cidx + 1) * nc] = _dot_nt(h, w_ref[cidx * nc:(cidx + 1) * nc, :])

    return pl.pallas_call(
        body, name="in_proj", grid=(t // tm,), out_shape=jax.ShapeDtypeStruct((t, ZP), F32),
        in_specs=[_rows(tm, D_MODEL), _resident((1, D_MODEL)), _resident((ZP, D_MODEL))],
        out_specs=_rows(tm, ZP), compiler_params=_params(("parallel",)),
    )(x, g_mix, w_in_t)


def attn_prep(zp, tabs, g_qa, g_kva, g_qn, g_kn, w_qb_t, w_kvb_t, tm, s_len):
    t = zp.shape[0]
    nsb = s_len // tm
    scale_a = (QK_NOPE + QK_ROPE) ** -0.5
    scale_b = HD_B ** -0.5

    def body(qb_ref, qlat_ref, kb_ref, vb_ref, ckv_ref, kpe_ref, tab_ref, gqa_ref, gkva_ref, gqn_ref, gkn_ref,
             wqb_ref, wkvb_ref, qa_o, ka_o, va_o, qb_o, kb_o, vb_o, cq_o, ckvn_o):
        ca, s1a, s2a = tab_ref[0], tab_ref[1], tab_ref[2]
        ck = tab_ref[3]
        cb, s1b, s2b = tab_ref[4], tab_ref[5], tab_ref[6]
        ql = qlat_ref[...]
        cq = (ql * _rstd(ql, Q_LORA) * gqa_ref[...]).astype(BF16)
        cq_o[...] = cq
        qa = _dot_nt(cq, wqb_ref[...])
        for h in range(H_A):
            sl = slice(h * HP, (h + 1) * HP)
            qa_o[:, sl] = (_rope_fwd(qa[:, sl], ca, s1a, s2a) * scale_a).astype(BF16)
        cr = ckv_ref[...]
        ckv = (cr * _rstd(cr, KV_LORA) * gkva_ref[...]).astype(BF16)
        ckvn_o[...] = ckv
        kva = _dot_nt(ckv, wkvb_ref[...])
        kpe = _rope_fwd(kpe_ref[...], ck, s1a, s2a)
        for h in range(H_A):
            sl = slice(h * HP, (h + 1) * HP)
            ka_o[:, sl] = (kva[:, sl] + kpe).astype(BF16)
        va_o[...] = kva[:, H_A * HP:].astype(BF16)
        gqn, gkn = gqn_ref[...], gkn_ref[...]
        for h in range(H_B):
            sl = slice(h * HP, (h + 1) * HP)
            xs = qb_ref[:, sl]
            y = xs * _rstd(xs, HD_B) * gqn
            qb_o[:, sl] = (_rope_fwd(y, cb, s1b, s2b) * scale_b).astype(BF16)
        for h in range(KV_B):
            sl = slice(h * HP, (h + 1) * HP)
            xs = kb_ref[:, sl]
            y = xs * _rstd(xs, HD_B) * gkn
            kb_o[:, sl] = _rope_fwd(y, cb, s1b, s2b).astype(BF16)
        vb_o[...] = vb_ref[...].astype(BF16)

    def o(width):
        return jax.ShapeDtypeStruct((t, width), BF16)

    return pl.pallas_call(
        body, name="attn_prep", grid=(t // tm,),
        out_shape=(o(H_A * HP), o(H_A * HP), o(H_A * HP), o(H_B * HP), o(KV_B * HP), o(KV_B * HP), o(Q_LORA), o(KV_LORA)),
        in_specs=[_rows(tm, 1024, 0), _rows(tm, 256, 12), _rows(tm, 256, 13), _rows(tm, 256, 14),
                  _rows(tm, 128, 30), _rows(tm, 128, 31),
                  pl.BlockSpec((7, tm, HP), lambda i: (0, i % nsb, 0)),
                  _resident((1, Q_LORA)), _resident((1, KV_LORA)), _resident((1, HP)), _resident((1, HP)),
                  _resident((H_A * HP, Q_LORA)), _resident((2 * H_A * HP, KV_LORA))],
        out_specs=(_rows(tm, H_A * HP), _rows(tm, H_A * HP), _rows(tm, H_A * HP), _rows(tm, H_B * HP),
                   _rows(tm, KV_B * HP), _rows(tm, KV_B * HP), _rows(tm, Q_LORA), _rows(tm, KV_LORA)),
        compiler_params=_params(("parallel",)),
    )(zp, zp, zp, zp, zp, zp, tabs, g_qa, g_kva, g_qn, g_kn, w_qb_t, w_kvb_t)


def attn_fwd(q, k, v, n_b, s_len, tq, name):
    t = q.shape[0]
    n_h, n_hk = q.shape[1] // HP, k.shape[1] // HP
    grp = n_h // n_hk
    nq = s_len // tq

    def body(q_ref, k_ref, v_ref, o_ref):
        s = _dot_nt(q_ref[...], k_ref[...])
        p = jnp.exp(s - jnp.max(s, axis=-1, keepdims=True))
        pr = (p * (1.0 / jnp.sum(p, axis=-1, keepdims=True))).astype(BF16)
        o_ref[...] = _dot_nn(pr, v_ref[...]).astype(o_ref.dtype)

    qspec = pl.BlockSpec((tq, HP), lambda b, h, i: (b * nq + i, h))
    kspec = pl.BlockSpec((s_len, HP), lambda b, h, i: (b, h // grp))
    return pl.pallas_call(
        body, name=name, grid=(n_b, n_h, nq), out_shape=jax.ShapeDtypeStruct((t, n_h * HP), BF16),
        in_specs=[qspec, kspec, kspec], out_specs=qspec,
        compiler_params=_params(("parallel", "parallel", "parallel")),
    )(q, k, v)


def merge_fwd(oa, ob, zp, x, w_oa_t, w_ob_t, w_o, tm):
    t = x.shape[0]

    def body(oa_ref, ob_ref, ga_ref, gb_ref, x_ref, woa_ref, wob_ref, wo_ref, x1_o, mg_o):
        ya = _dot_nt(oa_ref[...], woa_ref[...])
        yb = _dot_nt(ob_ref[...], wob_ref[...])
        merged = (jax.nn.sigmoid(ga_ref[...]) * ya + jax.nn.sigmoid(gb_ref[...]) * yb).astype(BF16)
        mg_o[...] = merged
        x1_o[...] = x_ref[...] + _dot_nn(merged, wo_ref[...])

    return pl.pallas_call(
        body, name="merge_fwd", grid=(t // tm,),
        out_shape=(jax.ShapeDtypeStruct((t, D_MODEL), F32), jax.ShapeDtypeStruct((t, D_MODEL), BF16)),
        in_specs=[_rows(tm, H_A * HP), _rows(tm, H_B * HP), _rows(tm, 1024, 1), _rows(tm, 1024, 2), _rows(tm, D_MODEL),
                  _resident((D_MODEL, H_A * HP)), _resident((D_MODEL, H_B * HP)), _resident((D_MODEL, D_MODEL))],
        out_specs=(_rows(tm, D_MODEL), _rows(tm, D_MODEL)), compiler_params=_params(("parallel",)),
    )(oa, ob, zp, zp, x, w_oa_t, w_ob_t, w_o)


def mlp_fwd(x1, g_mlp, w_up_t, w_down, tm):
    t = x1.shape[0]
    fc = 1024

    def body(x_ref, g_ref, wup_ref, wdn_ref, x2_o, u_o):
        xv = x_ref[...]
        h2 = (xv * _rstd(xv, D_MODEL) * g_ref[...]).astype(BF16)
        acc = xv
        for cidx in range(D_FF // fc):
            sl = slice(cidx * fc, (cidx + 1) * fc)
            u = jnp.maximum(_dot_nt(h2, wup_ref[sl, :]), 0.0)
            u_o[:, sl] = u.astype(BF16)
            acc = acc + _dot_nn((u * u).astype(BF16), wdn_ref[sl, :])
        x2_o[...] = acc

    return pl.pallas_call(
        body, name="mlp_fwd", grid=(t // tm,),
        out_shape=(jax.ShapeDtypeStruct((t, D_MODEL), F32), jax.ShapeDtypeStruct((t, D_FF), BF16)),
        in_specs=[_rows(tm, D_MODEL), _resident((1, D_MODEL)), _resident((D_FF, D_MODEL)), _resident((D_FF, D_MODEL))],
        out_specs=(_rows(tm, D_MODEL), _rows(tm, D_FF)), compiler_params=_params(("parallel",)),
    )(x1, g_mlp, w_up_t, w_down)


def ple_loss_bwd(x2, p, tgt, g_ple, g_final, w_pg, w_ple_t, tm):
    t = x2.shape[0]
    inv_d = 1.0 / D_MODEL

    def body(x2_ref, p_ref, tg_ref, gp_ref, gf_ref, wpg_ref, wple_ref, dx2_o, dt_o, h3_o, dpe_o, st_o):
        @pl.when(pl.program_id(0) == 0)
        def _():
            st_o[...] = jnp.zeros_like(st_o)

        x2v = x2_ref[...]
        gp, gf = gp_ref[...], gf_ref[...]
        r2 = _rstd(x2v, D_MODEL)
        xh2 = x2v * r2
        h3 = (xh2 * gp).astype(BF16)
        h3_o[...] = h3
        gate = jax.nn.sigmoid(_dot_nn(h3, wpg_ref[...]))
        pe = _dot_nt(p_ref[...].astype(BF16), wple_ref[...])
        x3 = x2v + gate * pe
        r3 = _rstd(x3, D_MODEL)
        xh3 = x3 * r3
        err = xh3 * gf - tg_ref[...]
        dy = err * inv_d
        dx3 = _rms_bwd(dy, xh3, r3, gf, D_MODEL)
        dpe_o[...] = (dx3 * gate).astype(BF16)
        dt = (dx3 * pe * gate * (1.0 - gate)).astype(BF16)
        dt_o[...] = dt
        dh3 = _dot_nt(dt, wpg_ref[...])
        dx2_o[...] = dx3 + _rms_bwd(dh3, xh2, r2, gp, D_MODEL)
        st_o[0:1, :] += _colsum(dh3 * xh2)
        st_o[1:2, :] += _colsum(dy * xh3)
        st_o[2:3, :] += _colsum(err * err) * (0.5 * inv_d)

    bf = jax.ShapeDtypeStruct((t, D_MODEL), BF16)
    return pl.pallas_call(
        body, name="ple_loss_bwd", grid=(t // tm,),
        out_shape=(jax.ShapeDtypeStruct((t, D_MODEL), F32), bf, bf, bf, jax.ShapeDtypeStruct((3, D_MODEL), F32)),
        in_specs=[_rows(tm, D_MODEL), _rows(tm, PLE_DIM), _rows(tm, D_MODEL), _resident((1, D_MODEL)), _resident((1, D_MODEL)),
                  _resident((D_MODEL, D_MODEL)), _resident((D_MODEL, PLE_DIM))],
        out_specs=(_rows(tm, D_MODEL), _rows(tm, D_MODEL), _rows(tm, D_MODEL), _rows(tm, D_MODEL),
                   pl.BlockSpec((3, D_MODEL), lambda i: (0, 0))),
        compiler_params=_params(("arbitrary",)),
    )(x2, p, tgt, g_ple, g_final, w_pg, w_ple_t)


def mlp_bwd(dx2, x1, u, g_mlp, w_up_t, w_down, tm):
    t = x1.shape[0]
    fc = 1024

    def body(dx2_ref, x1_ref, u_ref, g_ref, wup_ref, wdn_ref, dx1_o, da_o, h2_o, st_o):
        @pl.when(pl.program_id(0) == 0)
        def _():
            st_o[...] = jnp.zeros_like(st_o)

        d2 = dx2_ref[...]
        d2b = d2.astype(BF16)
        dh2 = jnp.zeros((tm, D_MODEL), F32)
        for cidx in range(D_FF // fc):
            sl = slice(cidx * fc, (cidx + 1) * fc)
            da = (_dot_nt(d2b, wdn_ref[sl, :]) * (2.0 * u_ref[:, sl].astype(F32))).astype(BF16)
            da_o[:, sl] = da
            dh2 = dh2 + _dot_nn(da, wup_ref[sl, :])
        xv = x1_ref[...]
        g = g_ref[...]
        r1 = _rstd(xv, D_MODEL)
        xh1 = xv * r1
        h2_o[...] = (xh1 * g).astype(BF16)
        st_o[...] += _colsum(dh2 * xh1)
        dx1_o[...] = d2 + _rms_bwd(dh2, xh1, r1, g, D_MODEL)

    return pl.pallas_call(
        body, name="mlp_bwd", grid=(t // tm,),
        out_shape=(jax.ShapeDtypeStruct((t, D_MODEL), F32), jax.ShapeDtypeStruct((t, D_FF), BF16),
                   jax.ShapeDtypeStruct((t, D_MODEL), BF16), jax.ShapeDtypeStruct((1, D_MODEL), F32)),
        in_specs=[_rows(tm, D_MODEL), _rows(tm, D_MODEL), _rows(tm, D_FF), _resident((1, D_MODEL)),
                  _resident((D_FF, D_MODEL)), _resident((D_FF, D_MODEL))],
        out_specs=(_rows(tm, D_MODEL), _rows(tm, D_FF), _rows(tm, D_MODEL), pl.BlockSpec((1, D_MODEL), lambda i: (0, 0))),
        compiler_params=_params(("arbitrary",)),
    )(dx2, x1, u, g_mlp, w_up_t, w_down)


def merge_bwd(dx1, oa, ob, zp, w_oa_t, w_ob_t, w_o, tm):
    t = dx1.shape[0]

    def body(dx1_ref, oa_ref, ob_ref, ga_ref, gb_ref, woa_ref, wob_ref, wo_ref, doa_o, dob_o, dg_o, dya_o, dyb_o):
        dm = _dot_nt(dx1_ref[...].astype(BF16), wo_ref[...])
        for o_ref, g_ref, w_ref, do_o, dy_o, col in ((oa_ref, ga_ref, woa_ref, doa_o, dya_o, 0),
                                                     (ob_ref, gb_ref, wob_ref, dob_o, dyb_o, 1)):
            yv = _dot_nt(o_ref[...], w_ref[...])
            sg = jax.nn.sigmoid(g_ref[...])
            dyv = (dm * sg).astype(BF16)
            dy_o[...] = dyv
            dg_o[:, col * D_MODEL:(col + 1) * D_MODEL] = (dm * yv * sg * (1.0 - sg)).astype(BF16)
            do_o[...] = _dot_nn(dyv, w_ref[...]).astype(BF16)

    bf = jax.ShapeDtypeStruct((t, D_MODEL), BF16)
    return pl.pallas_call(
        body, name="merge_bwd", grid=(t // tm,),
        out_shape=(bf, bf, jax.ShapeDtypeStruct((t, 2 * D_MODEL), BF16), bf, bf),
        in_specs=[_rows(tm, D_MODEL), _rows(tm, H_A * HP), _rows(tm, H_B * HP), _rows(tm, 1024, 1), _rows(tm, 1024, 2),
                  _resident((D_MODEL, H_A * HP)), _resident((D_MODEL, H_B * HP)), _resident((D_MODEL, D_MODEL))],
        out_specs=(_rows(tm, D_MODEL), _rows(tm, D_MODEL), _rows(tm, 2 * D_MODEL), _rows(tm, D_MODEL), _rows(tm, D_MODEL)),
        compiler_params=_params(("parallel",)),
    )(dx1, oa, ob, zp, zp, w_oa_t, w_ob_t, w_o)


def attn_bwd(q, k, v, do, n_b, s_len, tq, name):
    t = q.shape[0]
    n_h, n_hk = q.shape[1] // HP, k.shape[1] // HP
    grp = n_h // n_hk
    nq = s_len // tq

    def body(q_ref, k_ref, v_ref, do_ref, dq_o, dk_o, dv_o):
        @pl.when((pl.program_id(2) == 0) & (pl.program_id(3) == 0))
        def _():
            dk_o[...] = jnp.zeros_like(dk_o)
            dv_o[...] = jnp.zeros_like(dv_o)

        qv, kv, dov = q_ref[...], k_ref[...], do_ref[...]
        s = _dot_nt(qv, kv)
        p = jnp.exp(s - jnp.max(s, axis=-1, keepdims=True))
        p = p * (1.0 / jnp.sum(p, axis=-1, keepdims=True))
        dp = _dot_nt(dov, v_ref[...])
        ds = (p * (dp - jnp.sum(p * dp, axis=-1, keepdims=True))).astype(BF16)
        dq_o[...] = _dot_nn(ds, kv)
        dk_o[...] += _dot_tn(ds, qv)
        dv_o[...] += _dot_tn(p.astype(BF16), dov)

    qspec = pl.BlockSpec((tq, HP), lambda b, hk, g, i: (b * nq + i, hk * grp + g))
    kspec = pl.BlockSpec((s_len, HP), lambda b, hk, g, i: (b, hk))
    return pl.pallas_call(
        body, name=name, grid=(n_b, n_hk, grp, nq),
        out_shape=(jax.ShapeDtypeStruct((t, n_h * HP), F32), jax.ShapeDtypeStruct((t, n_hk * HP), F32),
                   jax.ShapeDtypeStruct((t, n_hk * HP), F32)),
        in_specs=[qspec, kspec, kspec, qspec], out_specs=(qspec, kspec, kspec),
        compiler_params=_params(("parallel", "parallel", "arbitrary", "arbitrary")),
    )(q, k, v, do)


def prep_bwd(dqa, dka, dva, dqb, dkb, dvb, zp, tabs, g_qa, g_kva, g_qn, g_kn, w_qb_t, w_kvb_t, tm, s_len):
    t = zp.shape[0]
    nsb = s_len // tm
    scale_a = (QK_NOPE + QK_ROPE) ** -0.5
    scale_b = HD_B ** -0.5

    def body(dqa_ref, dka_ref, dva_ref, dqb_ref, dkb_ref, dvb_ref, qb_ref, qlat_ref, kb_ref, ckv_ref, tab_ref,
             gqa_ref, gkva_ref, gqn_ref, gkn_ref, wqb_ref, wkvb_ref, dzq_o, dsm_o, dqap_o, dkva_o, st_o):
        @pl.when(pl.program_id(0) == 0)
        def _():
            st_o[...] = jnp.zeros_like(st_o)

        ca, s1a, s2a = tab_ref[0], tab_ref[1], tab_ref[2]
        ck = tab_ref[3]
        cb, s1b, s2b = tab_ref[4], tab_ref[5], tab_ref[6]
        for h in range(H_A):
            sl = slice(h * HP, (h + 1) * HP)
            dqap_o[:, sl] = _rope_bwd(dqa_ref[:, sl] * scale_a, ca, s1a, s2a).astype(BF16)
        dcq = _dot_nn(dqap_o[...], wqb_ref[...])
        ql = qlat_ref[...]
        rq = _rstd(ql, Q_LORA)
        xh = ql * rq
        gqa = gqa_ref[...]
        st_o[0:1, :] += _colsum(dcq * xh)
        dsm_o[:, 0:256] = _rms_bwd(dcq, xh, rq, gqa, Q_LORA).astype(BF16)
        dkpe = jnp.zeros((tm, HP), F32)
        for h in range(H_A):
            sl = slice(h * HP, (h + 1) * HP)
            dk = dka_ref[:, sl]
            dkpe = dkpe + dk
            dkva_o[:, sl] = dk.astype(BF16)
        dkva_o[:, H_A * HP:] = dva_ref[...].astype(BF16)
        dsm_o[:, 896:1024] = _rope_bwd(dkpe, ck, s1a, s2a).astype(BF16)
        dckv = _dot_nn(dkva_o[...], wkvb_ref[...])
        cr = ckv_ref[...]
        rk = _rstd(cr, KV_LORA)
        xh = cr * rk
        st_o[1:2, 0:128] += _colsum(dckv * xh)
        dsm_o[:, 768:896] = _rms_bwd(dckv, xh, rk, gkva_ref[...], KV_LORA).astype(BF16)
        gqn, gkn = gqn_ref[...], gkn_ref[...]
        dgq = jnp.zeros((1, HP), F32)
        for h in range(H_B):
            sl = slice(h * HP, (h + 1) * HP)
            dy = _rope_bwd(dqb_ref[:, sl] * scale_b, cb, s1b, s2b)
            xs = qb_ref[:, sl]
            r = _rstd(xs, HD_B)
            xh = xs * r
            dgq = dgq + _colsum(dy * xh)
            dzq_o[:, sl] = _rms_bwd(dy, xh, r, gqn, HD_B).astype(BF16)
        st_o[2:3, 0:128] += dgq
        dgk = jnp.zeros((1, HP), F32)
        for h in range(KV_B):
            sl = slice(h * HP, (h + 1) * HP)
            dy = _rope_bwd(dkb_ref[:, sl], cb, s1b, s2b)
            xs = kb_ref[:, sl]
            r = _rstd(xs, HD_B)
            xh = xs * r
            dgk = dgk + _colsum(dy * xh)
            dsm_o[:, 256 + h * HP:256 + (h + 1) * HP] = _rms_bwd(dy, xh, r, gkn, HD_B).astype(BF16)
        st_o[3:4, 0:128] += dgk
        dsm_o[:, 512:768] = dvb_ref[...].astype(BF16)

    bf = jax.ShapeDtypeStruct((t, 1024), BF16)
    return pl.pallas_call(
        body, name="prep_bwd", grid=(t // tm,),
        out_shape=(bf, bf, bf, jax.ShapeDtypeStruct((t, 2048), BF16), jax.ShapeDtypeStruct((4, 256), F32)),
        in_specs=[_rows(tm, 1024), _rows(tm, 1024), _rows(tm, 1024), _rows(tm, 1024), _rows(tm, 256), _rows(tm, 256),
                  _rows(tm, 1024, 0), _rows(tm, 256, 12), _rows(tm, 256, 13), _rows(tm, 128, 30),
                  pl.BlockSpec((7, tm, HP), lambda i: (0, i % nsb, 0)),
                  _resident((1, Q_LORA)), _resident((1, KV_LORA)), _resident((1, HP)), _resident((1, HP)),
                  _resident((H_A * HP, Q_LORA)), _resident((2 * H_A * HP, KV_LORA))],
        out_specs=(_rows(tm, 1024), _rows(tm, 1024), _rows(tm, 1024), _rows(tm, 2048), pl.BlockSpec((4, 256), lambda i: (0, 0))),
        compiler_params=_params(("arbitrary",)),
    )(dqa, dka, dva, dqb, dkb, dvb, zp, zp, zp, zp, tabs, g_qa, g_kva, g_qn, g_kn, w_qb_t, w_kvb_t)


def in_bwd(dzq, dgab, dsm, x, dx1, g_mix, w_in_t, tm):
    t = x.shape[0]

    def body(dzq_ref, dg_ref, dsm_ref, x_ref, dx1_ref, g_ref, w_ref, dx_o, h_o, st_o):
        @pl.when(pl.program_id(0) == 0)
        def _():
            st_o[...] = jnp.zeros_like(st_o)

        dh = _dot_nn(dzq_ref[...], w_ref[0:1024, :])
        dh = dh + _dot_nn(dg_ref[...], w_ref[1024:3072, :])
        dh = dh + _dot_nn(dsm_ref[...], w_ref[3072:4096, :])
        xv = x_ref[...]
        g = g_ref[...]
        r = _rstd(xv, D_MODEL)
        xh = xv * r
        h_o[...] = (xh * g).astype(BF16)
        st_o[...] += _colsum(dh * xh)
        dx_o[...] = dx1_ref[...] + _rms_bwd(dh, xh, r, g, D_MODEL)

    return pl.pallas_call(
        body, name="in_bwd", grid=(t // tm,),
        out_shape=(jax.ShapeDtypeStruct((t, D_MODEL), F32), jax.ShapeDtypeStruct((t, D_MODEL), BF16),
                   jax.ShapeDtypeStruct((1, D_MODEL), F32)),
        in_specs=[_rows(tm, 1024), _rows(tm, 2048), _rows(tm, 1024), _rows(tm, D_MODEL), _rows(tm, D_MODEL),
                  _resident((1, D_MODEL)), _resident((ZP, D_MODEL))],
        out_specs=(_rows(tm, D_MODEL), _rows(tm, D_MODEL), pl.BlockSpec((1, D_MODEL), lambda i: (0, 0))),
        compiler_params=_params(("arbitrary",)),
    )(dzq, dgab, dsm, x, dx1, g_mix, w_in_t)


def matmul_tn(a, b, name, square_a=False):
    t, m = a.shape
    n = b.shape[1]
    bm = min(m, 512)
    tk = min(t, 512)

    def body(a_ref, b_ref, o_ref):
        @pl.when(pl.program_id(1) == 0)
        def _():
            o_ref[...] = jnp.zeros_like(o_ref)

        av = a_ref[...]
        if square_a:
            av = (av.astype(F32) * av.astype(F32))
        o_ref[...] += _dot_tn(av.astype(BF16), b_ref[...].astype(BF16))

    return pl.pallas_call(
        body, name=name, grid=(m // bm, t // tk), out_shape=jax.ShapeDtypeStruct((m, n), F32),
        in_specs=[pl.BlockSpec((tk, bm), lambda i, kk: (kk, i)), pl.BlockSpec((tk, n), lambda i, kk: (kk, 0))],
        out_specs=pl.BlockSpec((bm, n), lambda i, kk: (i, 0)),
        compiler_params=_params(("parallel", "arbitrary")),
    )(a, b)


def adamw(w, g, m, v, name):
    r, c = w.shape
    tr = r if r <= 256 else 256
    c1 = 1.0 - ADAM_B1 ** ADAM_STEP
    c2 = 1.0 - ADAM_B2 ** ADAM_STEP

    def body(w_ref, g_ref, m_ref, v_ref, d_o, m_o, v_o):
        gv = g_ref[...]
        mn = ADAM_B1 * m_ref[...] + (1.0 - ADAM_B1) * gv
        vn = ADAM_B2 * v_ref[...] + (1.0 - ADAM_B2) * (gv * gv)
        m_o[...] = mn
        v_o[...] = vn
        d_o[...] = -ADAM_LR * ((mn / c1) / (jnp.sqrt(vn / c2) + ADAM_EPS) + ADAM_WD * w_ref[...])

    spec = pl.BlockSpec((tr, c), lambda i: (i, 0))
    shp = jax.ShapeDtypeStruct((r, c), F32)
    return pl.pallas_call(
        body, name=name, grid=(r // tr,), out_shape=(shp, shp, shp), in_specs=[spec] * 4, out_specs=(spec,) * 3,
        compiler_params=_params(("parallel",)),
    )(w, g, m, v)


def _rope_tables(s_len):
    def angles(pos, dim):
        inv = np.float32(ROPE_THETA) ** (-np.arange(0, dim, 2, dtype=np.float32) / np.float32(dim))
        return pos.astype(np.float32)[:, None] * inv[None, :]

    tpos = np.arange(s_len)
    a1 = angles(tpos, QK_ROPE)
    ar = angles(tpos // GRID_W, HD_B // 2)
    ac = angles(tpos % GRID_W, HD_B // 2)
    z16 = np.zeros((s_len, 16), np.float32)
    z32 = np.zeros((s_len, 32), np.float32)
    z64 = np.zeros((s_len, 64), np.float32)
    one64 = np.ones((s_len, 64), np.float32)
    c1, s1 = np.cos(a1), np.sin(a1)
    ca = np.concatenate([one64, c1, c1, z32], axis=1)
    ck = np.concatenate([z64, c1, c1, z32], axis=1)
    s1a = np.concatenate([z64, -s1, z16, z32], axis=1)
    s2a = np.concatenate([z64, z16, s1, z32], axis=1)
    cr, sr, cc, sc = np.cos(ar), np.sin(ar), np.cos(ac), np.sin(ac)
    cb = np.concatenate([cr, cr, cc, cc, z64], axis=1)
    s1b = np.concatenate([-sr, z16, -sc, z16, z64], axis=1)
    s2b = np.concatenate([z16, sr, z16, sc, z64], axis=1)
    return jnp.asarray(np.stack([ca, s1a, s2a, ck, cb, s1b, s2b]).astype(np.float32))


def _pad_heads(a, n_heads, axis):
    shp = a.shape
    a = a.reshape(shp[:axis] + (n_heads, shp[axis] // n_heads) + shp[axis + 1:])
    pad = [(0, 0)] * a.ndim
    pad[axis + 1] = (0, HP - a.shape[axis + 1])
    a = jnp.pad(a, pad)
    return a.reshape(shp[:axis] + (n_heads * HP,) + shp[axis + 1:])


def _unpad_heads(a, n_heads, width, axis):
    shp = a.shape
    a = a.reshape(shp[:axis] + (n_heads, HP) + shp[axis + 1:])
    a = lax.slice_in_dim(a, 0, width, axis=axis + 1)
    return a.reshape(shp[:axis] + (n_heads * width,) + shp[axis + 1:])


def _pack_rows(blocks, lead):
    parts = []
    for name, rows in PACK:
        b = blocks[name]
        padr = rows - b.shape[-2]
        if padr:
            b = jnp.pad(b, [(0, 0)] * (b.ndim - 2) + [(0, padr), (0, 0)])
        parts.append(b)
    return jnp.concatenate(parts, axis=len(lead))


def _expand_w_in(wt):
    z64 = jnp.zeros((64, D_MODEL), wt.dtype)
    z32 = jnp.zeros((32, D_MODEL), wt.dtype)
    return jnp.concatenate([
        _pad_heads(wt[416:928], H_B, 0), wt[1184:2208], wt[2208:3232], wt[0:256],
        _pad_heads(wt[928:1056], KV_B, 0), _pad_heads(wt[1056:1184], KV_B, 0), wt[256:384],
        z64, wt[384:416], z32], axis=0)


def _collapse_w_in(dq, dg, ds):
    return jnp.concatenate([
        ds[0:256], ds[768:896], ds[960:992], _unpad_heads(dq, H_B, HD_B, 0), _unpad_heads(ds[256:512], KV_B, HD_B, 0),
        _unpad_heads(ds[512:768], KV_B, HD_B, 0), dg], axis=0)


def kernel(x, p, g_mix, w_in, g_qa, w_qb, g_kva, w_kvb, g_qn, g_kn, w_oa, w_ob, w_o, g_mlp, w_up, w_down, g_ple, w_ple_gate, w_ple, g_final, loss_target, m_g_mix, m_w_in, m_g_qa, m_w_qb, m_g_kva, m_w_kvb, m_g_qn, m_g_kn, m_w_oa, m_w_ob, m_w_o, m_g_mlp, m_w_up, m_w_down, m_g_ple, m_w_ple_gate, m_w_ple, m_g_final, v_g_mix, v_w_in, v_g_qa, v_w_qb, v_g_kva, v_w_kvb, v_g_qn, v_g_kn, v_w_oa, v_w_ob, v_w_o, v_g_mlp, v_w_up, v_w_down, v_g_ple, v_w_ple_gate, v_w_ple, v_g_final):
    n_b, s_len, _ = x.shape
    t = n_b * s_len
    tm = min(256, s_len)
    tq_f = min(512, s_len)
    tq_b = min(256, s_len)

    mats = dict(w_in=(w_in, m_w_in, v_w_in), w_qb=(w_qb, m_w_qb, v_w_qb), w_kvb=(w_kvb, m_w_kvb, v_w_kvb),
                w_oa=(w_oa, m_w_oa, v_w_oa), w_ob=(w_ob, m_w_ob, v_w_ob), w_o=(w_o, m_w_o, v_w_o),
                w_up=(w_up, m_w_up, v_w_up), w_down=(w_down, m_w_down, v_w_down),
                w_ple_gate=(w_ple_gate, m_w_ple_gate, v_w_ple_gate), w_ple=(w_ple, m_w_ple, v_w_ple))
    col_sharded = ("w_in", "w_qb", "w_kvb", "w_oa", "w_ob", "w_up", "w_ple")

    blocks = {}
    for name, (w, _, _) in mats.items():
        w2 = w[0]
        if name in col_sharded:
            w2 = w2.T
        blocks[name] = w2.reshape(-1, D_MODEL).astype(BF16)
    full = allgather_rows(_pack_rows(blocks, ()))

    def gathered(name, rows, width):
        off, _ = PACK_OFF[name]
        return full[:, off:off + rows].reshape(-1, width)

    w_in_t = _expand_w_in(gathered("w_in", 404, D_MODEL))
    w_qb_t = _pad_heads(gathered("w_qb", 24, Q_LORA), H_A, 0)
    wkvb = gathered("w_kvb", 16, KV_LORA).reshape(H_A, 2, 64, KV_LORA)
    w_kvb_t = jnp.concatenate([_pad_heads(wkvb[:, 0].reshape(-1, KV_LORA), H_A, 0),
                               _pad_heads(wkvb[:, 1].reshape(-1, KV_LORA), H_A, 0)], axis=0)
    w_oa_t = _pad_heads(gathered("w_oa", 64, H_A * V_DIM_A), H_A, 1)
    w_ob_t = _pad_heads(gathered("w_ob", 64, H_B * HD_B), H_B, 1)
    w_o_f = gathered("w_o", 128, D_MODEL)
    w_up_t = gathered("w_up", 512, D_MODEL)
    w_down_f = gathered("w_down", 512, D_MODEL)
    w_pg_f = gathered("w_ple_gate", 128, D_MODEL)
    w_ple_t = gathered("w_ple", 32, PLE_DIM)

    tabs = _rope_tables(s_len)
    g_qn_p = jnp.pad(g_qn, ((0, 0), (0, HP - HD_B)))
    g_kn_p = jnp.pad(g_kn, ((0, 0), (0, HP - HD_B)))
    xf = x.reshape(t, D_MODEL)
    pf = p.reshape(t, PLE_DIM)
    tgt = loss_target.reshape(t, D_MODEL)

    zp = in_proj(xf, g_mix, w_in_t, tm)
    qa, ka, va, qb, kb, vb, cq, ckv = attn_prep(zp, tabs, g_qa, g_kva, g_qn_p, g_kn_p, w_qb_t, w_kvb_t, tm, s_len)
    oa = attn_fwd(qa, ka, va, n_b, s_len, tq_f, "attn_a_fwd")
    ob = attn_fwd(qb, kb, vb, n_b, s_len, tq_f, "attn_b_fwd")
    x1, merged = merge_fwd(oa, ob, zp, xf, w_oa_t, w_ob_t, w_o_f, tm)
    x2, u = mlp_fwd(x1, g_mlp, w_up_t, w_down_f, tm)
    dx2, dt, h3, dpe, st_ple = ple_loss_bwd(x2, pf, tgt, g_ple, g_final.reshape(1, D_MODEL), w_pg_f, w_ple_t, tm)
    dx1, da, h2, st_mlp = mlp_bwd(dx2, x1, u, g_mlp, w_up_t, w_down_f, tm)
    doa, dob, dgab, dya, dyb = merge_bwd(dx1, oa, ob, zp, w_oa_t, w_ob_t, w_o_f, tm)
    dqa, dka, dva = attn_bwd(qa, ka, va, doa, n_b, s_len, tq_b, "attn_a_bwd")
    dqb, dkb, dvb = attn_bwd(qb, kb, vb, dob, n_b, s_len, tq_b, "attn_b_bwd")
    dzq, dsm, dqap, dkva, st_prep = prep_bwd(dqa, dka, dva, dqb, dkb, dvb, zp, tabs, g_qa, g_kva, g_qn_p, g_kn_p,
                                             w_qb_t, w_kvb_t, tm, s_len)
    grad_x, h, st_mix = in_bwd(dzq, dgab, dsm, xf, dx1, g_mix, w_in_t, tm)

    gw_in = _collapse_w_in(matmul_tn(dzq, h, "gw_in_q"), matmul_tn(dgab, h, "gw_in_g"), matmul_tn(dsm, h, "gw_in_s"))
    gw_qb = _unpad_heads(matmul_tn(dqap, cq, "gw_qb"), H_A, QK_NOPE + QK_ROPE, 0)
    gkv = matmul_tn(dkva, ckv, "gw_kvb")
    gw_kvb = jnp.stack([_unpad_heads(gkv[:H_A * HP], H_A, 64, 0).reshape(H_A, 64, KV_LORA),
                        _unpad_heads(gkv[H_A * HP:], H_A, 64, 0).reshape(H_A, 64, KV_LORA)], axis=1)
    gw_oa = _unpad_heads(matmul_tn(dya, oa, "gw_oa"), H_A, V_DIM_A, 1)
    gw_ob = _unpad_heads(matmul_tn(dyb, ob, "gw_ob"), H_B, HD_B, 1)
    gw_o = matmul_tn(merged, dx1, "gw_o")
    gw_up = matmul_tn(da, h2, "gw_up")
    gw_down = matmul_tn(u, dx2, "gw_down", square_a=True)
    gw_pg = matmul_tn(h3, dt, "gw_pg")
    gw_ple = matmul_tn(dpe, pf, "gw_ple")
    gblocks = dict(w_in=gw_in, w_qb=gw_qb, w_kvb=gw_kvb, w_oa=gw_oa, w_ob=gw_ob, w_o=gw_o, w_up=gw_up,
                   w_down=gw_down, w_ple_gate=gw_pg, w_ple=gw_ple)
    gblocks = {n: a.reshape(N_DEV, -1, D_MODEL).astype(BF16) for n, a in gblocks.items()}
    gpack = _pack_rows(gblocks, (N_DEV,))

    core = lax.axis_index("c").astype(jnp.int32).reshape(1)
    chip = (2 * lax.axis_index("x") + lax.axis_index("y")).astype(jnp.int32).reshape(1)
    part = add_pairs(gpack, exchange_sibling(gpack), core)
    gshard = sum_chips(part, exchange_chips(part), chip)

    def row(a, width):
        return jnp.pad(a, ((0, 0), (0, D_MODEL - width)))

    def tile8(a, width):
        return jnp.pad(a, ((0, 7), (0, D_MODEL - width)))

    stats = jnp.concatenate([
        tile8(st_mix, D_MODEL), tile8(st_prep[0:1], 256), tile8(st_prep[1:2], 256), tile8(st_prep[2:3], 256),
        tile8(st_prep[3:4], 256), tile8(st_mlp, D_MODEL), tile8(st_ple[0:1], D_MODEL), tile8(st_ple[1:2], D_MODEL),
        tile8(st_ple[2:3], D_MODEL)], axis=0)
    stats = allreduce_stats(stats)[::8]
    loss = jnp.sum(stats[ST_LOSS])
    stats = jnp.pad(stats, ((0, ST_ROWS - 9), (0, 0)))

    out_g, out_d, out_m, out_v = {}, {}, {}, {}
    for name, (w, m, v) in mats.items():
        off, _ = PACK_OFF[name]
        shard_shape = w.shape[1:]
        if name in col_sharded:
            rows_t = shard_shape[1]
            g2 = gshard[off:off + (rows_t * shard_shape[0]) // D_MODEL].reshape(rows_t, shard_shape[0]).T
        else:
            g2 = gshard[off:off + shard_shape[0]]
        d2, m2, v2 = adamw(w[0], g2, m[0], v[0], "adamw_" + name)
        out_g[name], out_d[name], out_m[name], out_v[name] = g2[None], d2[None], m2[None], v2[None]

    gains = (("g_mix", g_mix, m_g_mix, v_g_mix, ST_G_MIX, D_MODEL), ("g_qa", g_qa, m_g_qa, v_g_qa, ST_G_QA, Q_LORA),
             ("g_kva", g_kva, m_g_kva, v_g_kva, ST_G_KVA, KV_LORA), ("g_qn", g_qn, m_g_qn, v_g_qn, ST_G_QN, HD_B),
             ("g_kn", g_kn, m_g_kn, v_g_kn, ST_G_KN, HD_B), ("g_mlp", g_mlp, m_g_mlp, v_g_mlp, ST_G_MLP, D_MODEL),
             ("g_ple", g_ple, m_g_ple, v_g_ple, ST_G_PLE, D_MODEL),
             ("g_final", g_final, m_g_final, v_g_final, ST_G_FINAL, D_MODEL))

    def gain_pack(idx):
        rows_ = [row(gn[idx].reshape(1, -1), gn[5]) for gn in gains]
        return jnp.concatenate(rows_ + [jnp.zeros((ST_ROWS - len(gains), D_MODEL), F32)], axis=0)

    gd, gm, gv = adamw(gain_pack(1), stats, gain_pack(2), gain_pack(3), "adamw_gains")
    for name, w, _, _, r_, width in gains:
        for dst, src in ((out_g, stats), (out_d, gd), (out_m, gm), (out_v, gv)):
            dst[name] = src[r_, :width].reshape(w.shape)

    order = ("g_mix", "w_in", "g_qa", "w_qb", "g_kva", "w_kvb", "g_qn", "g_kn", "w_oa", "w_ob", "w_o", "g_mlp",
             "w_up", "w_down", "g_ple", "w_ple_gate", "w_ple", "g_final")
    return (loss, grad_x.reshape(x.shape), *[out_g[n] for n in order], *[out_d[n] for n in order],
            *[out_m[n] for n in order], *[out_v[n] for n in order])
```

```python
import numpy as np
import jax
import jax.numpy as jnp
from jax import lax
from jax.experimental import pallas as pl
from jax.experimental.pallas import tpu as pltpu

F32 = jnp.float32
BF16 = jnp.bfloat16

D_MODEL = 1024
EPS = 1e-6
ROPE_THETA = 10000.0
GRID_W = 64
H_A = 8
QK_NOPE = 64
QK_ROPE = 32
V_DIM_A = 64
Q_LORA = 256
KV_LORA = 128
H_B = 8
KV_B = 2
HD_B = 64
D_FF = 4 * D_MODEL
PLE_DIM = 256
HP = 128
ZP = 4096
N_DEV = 8
N_CHIP = 4

ADAM_LR = 0.001
ADAM_B1 = 0.9
ADAM_B2 = 0.999
ADAM_EPS = 1e-08
ADAM_WD = 0.01
ADAM_STEP = 10

VMEM_LIMIT = 52 * 1024 * 1024

PACK = (("w_in", 416), ("w_qb", 32), ("w_kvb", 16), ("w_oa", 64), ("w_ob", 64), ("w_o", 128),
        ("w_up", 512), ("w_down", 512), ("w_ple_gate", 128), ("w_ple", 32))
PACK_OFF = {}
_o = 0
for _n, _r in PACK:
    PACK_OFF[_n] = (_o, _r)
    _o += _r
PACK_ROWS = _o

ST_G_MIX, ST_G_QA, ST_G_KVA, ST_G_QN, ST_G_KN, ST_G_MLP, ST_G_PLE, ST_G_FINAL, ST_LOSS = range(9)
ST_ROWS = 16


def _dot_nn(a, b):
    return lax.dot_general(a, b, (((1,), (0,)), ((), ())), preferred_element_type=F32)


def _dot_nt(a, b):
    return lax.dot_general(a, b, (((1,), (1,)), ((), ())), preferred_element_type=F32)


def _dot_tn(a, b):
    return lax.dot_general(a, b, (((0,), (0,)), ((), ())), preferred_element_type=F32)


def _rstd(x, n):
    return lax.rsqrt(jnp.sum(x * x, axis=-1, keepdims=True) * (1.0 / n) + EPS)


def _rms_bwd(dy, xh, r, g, n):
    dxh = dy * g
    return r * (dxh - xh * (jnp.sum(dxh * xh, axis=-1, keepdims=True) * (1.0 / n)))


def _rope_fwd(x, c, s1, s2):
    return x * c + pltpu.roll(x, HP - 16, 1) * s1 + pltpu.roll(x, 16, 1) * s2


def _rope_bwd(d, c, s1, s2):
    return d * c + pltpu.roll(d * s1, 16, 1) + pltpu.roll(d * s2, HP - 16, 1)


def _colsum(v):
    return jnp.sum(v, axis=0, keepdims=True)


def _params(sem=None, vmem=VMEM_LIMIT):
    return pltpu.CompilerParams(dimension_semantics=sem, vmem_limit_bytes=vmem)


def _resident(shape):
    nd = len(shape)
    return pl.BlockSpec(shape, lambda *_: (0,) * nd, pipeline_mode=pl.Buffered(1))


def _rows(tm, width, col=0):
    return pl.BlockSpec((tm, width), lambda i: (i, col))


def _mesh_pos():
    return lax.axis_index("x"), lax.axis_index("y"), lax.axis_index("c")


def _flip(v, bit):
    return (1 - v) if bit else v


_ANY = pl.BlockSpec(memory_space=pl.ANY)
_MESH = pl.DeviceIdType.MESH


def allgather_rows(shard):
    r, w = shard.shape

    def body(x_ref, out_ref, send_sems, recv_sems, local_sem):
        x, y, c = _mesh_pos()
        me, sibling = (x, y, c), (x, y, 1 - c)
        chips = [(1 - x, y), (x, 1 - y), (1 - x, 1 - y)]

        def slot(px, py, pc):
            return out_ref.at[4 * px + 2 * py + pc]

        def copy(k, block, to, src=None):
            return pltpu.make_async_remote_copy(
                src_ref=slot(*block) if src is None else src, dst_ref=slot(*block),
                send_sem=send_sems.at[k], recv_sem=recv_sems.at[k], device_id=to, device_id_type=_MESH)

        mine = pltpu.make_async_copy(x_ref, slot(*me), local_sem)
        mine.start()
        first = [copy(0, me, sibling, src=x_ref)]
        first += [copy(1 + j, me, (*chip, c), src=x_ref) for j, chip in enumerate(chips)]
        for cp in first:
            cp.start()
        passed = [copy(4 + j, (*chip, c), sibling) for j, chip in enumerate(chips)]
        for j, chip in enumerate(chips):
            copy(1 + j, (*chip, c), me).wait_recv()
            passed[j].start()
        copy(0, sibling, me).wait_recv()
        for j, chip in enumerate(chips):
            copy(4 + j, (*chip, 1 - c), me).wait_recv()
        for cp in first + passed:
            cp.wait_send()
        mine.wait()

    return pl.pallas_call(
        body, name="allgather_rows",
        out_shape=jax.ShapeDtypeStruct((N_DEV, r, w), shard.dtype),
        in_specs=[_ANY], out_specs=_ANY,
        scratch_shapes=[pltpu.SemaphoreType.DMA((7,)), pltpu.SemaphoreType.DMA((7,)), pltpu.SemaphoreType.DMA],
    )(shard)


def exchange_sibling(g):
    _, r, w = g.shape

    def body(g_ref, got_ref, send_sems, recv_sems):
        x, y, c = _mesh_pos()
        copies = [pltpu.make_async_remote_copy(
            src_ref=g_ref.at[2 * j + (1 - c)], dst_ref=got_ref.at[j], send_sem=send_sems.at[j], recv_sem=recv_sems.at[j],
            device_id=(x, y, 1 - c), device_id_type=_MESH) for j in range(N_CHIP)]
        for cp in copies:
            cp.start()
        for cp in copies:
            cp.wait()

    return pl.pallas_call(
        body, name="exchange_sibling", out_shape=jax.ShapeDtypeStruct((N_CHIP, r, w), g.dtype),
        in_specs=[_ANY], out_specs=_ANY,
        scratch_shapes=[pltpu.SemaphoreType.DMA((N_CHIP,)), pltpu.SemaphoreType.DMA((N_CHIP,))],
    )(g)


def exchange_chips(part):
    _, r, w = part.shape

    def body(p_ref, land_ref, send_sems, recv_sems):
        x, y, c = _mesh_pos()
        copies = []
        for k in (1, 2, 3):
            tx, ty = _flip(x, k & 2), _flip(y, k & 1)
            copies.append(pltpu.make_async_remote_copy(
                src_ref=p_ref.at[2 * tx + ty], dst_ref=land_ref.at[k - 1],
                send_sem=send_sems.at[k - 1], recv_sem=recv_sems.at[k - 1],
                device_id=(tx, ty, c), device_id_type=_MESH))
        for cp in copies:
            cp.start()
        for cp in copies:
            cp.wait()

    return pl.pallas_call(
        body, name="exchange_chips", out_shape=jax.ShapeDtypeStruct((N_CHIP - 1, r, w), part.dtype),
        in_specs=[_ANY], out_specs=_ANY,
        scratch_shapes=[pltpu.SemaphoreType.DMA((3,)), pltpu.SemaphoreType.DMA((3,))],
    )(part)


def allreduce_stats(st):
    def body(st_ref, out_ref, gath, send_sems, recv_sems):
        x, y, c = _mesh_pos()
        me = 4 * x + 2 * y + c
        gath[me] = st_ref[...]
        copies = []
        for k in range(1, N_DEV):
            peer = (_flip(x, k & 4), _flip(y, k & 2), _flip(c, k & 1))
            copies.append(pltpu.make_async_remote_copy(
                src_ref=st_ref, dst_ref=gath.at[me], send_sem=send_sems.at[k - 1], recv_sem=recv_sems.at[k - 1],
                device_id=peer, device_id_type=_MESH))
        for cp in copies:
            cp.start()
        for cp in copies:
            cp.wait()
        acc = gath[0]
        for d in range(1, N_DEV):
            acc = acc + gath[d]
        out_ref[...] = acc

    vm = pl.BlockSpec(memory_space=pltpu.VMEM)
    return pl.pallas_call(
        body, name="allreduce_stats", out_shape=jax.ShapeDtypeStruct(st.shape, F32),
        in_specs=[vm], out_specs=vm,
        scratch_shapes=[pltpu.VMEM((N_DEV,) + st.shape, F32),
                        pltpu.SemaphoreType.DMA((N_DEV - 1,)), pltpu.SemaphoreType.DMA((N_DEV - 1,))],
    )(st)


def add_pairs(g, got, core):
    n, r, w = got.shape
    tr = 272 if r % 272 == 0 else 16

    def body(c_ref, a_ref, b_ref, o_ref):
        o_ref[...] = (a_ref[...].astype(F32) + b_ref[...].astype(F32)).astype(o_ref.dtype)

    spec = pl.BlockSpec((1, tr, w), lambda i, j, c: (i, j, 0))
    return pl.pallas_call(
        body, name="add_pairs", out_shape=jax.ShapeDtypeStruct(got.shape, got.dtype),
        grid_spec=pltpu.PrefetchScalarGridSpec(
            num_scalar_prefetch=1, grid=(n, r // tr),
            in_specs=[pl.BlockSpec((1, tr, w), lambda i, j, c: (2 * i + c[0], j, 0)), spec], out_specs=spec),
        compiler_params=_params(("parallel", "parallel")),
    )(core, g, got)


def sum_chips(part, land, chip):
    _, r, w = part.shape
    tr = 272 if r % 272 == 0 else 16

    def body(c_ref, p_ref, l_ref, o_ref):
        acc = p_ref[0].astype(F32)
        for s in range(N_CHIP - 1):
            acc = acc + l_ref[s].astype(F32)
        o_ref[...] = acc

    return pl.pallas_call(
        body, name="sum_chips", out_shape=jax.ShapeDtypeStruct((r, w), F32),
        grid_spec=pltpu.PrefetchScalarGridSpec(
            num_scalar_prefetch=1, grid=(r // tr,),
            in_specs=[pl.BlockSpec((1, tr, w), lambda i, c: (c[0], i, 0)), pl.BlockSpec((N_CHIP - 1, tr, w), lambda i, c: (0, i, 0))],
            out_specs=pl.BlockSpec((tr, w), lambda i, c: (i, 0))),
        compiler_params=_params(("parallel",)),
    )(chip, part, land)


def in_proj(x, g_mix, w_in_t, tm):
    t = x.shape[0]
    nc = 512

    def body(x_ref, g_ref, w_ref, z_ref):
        xv = x_ref[...]
        h = (xv * _rstd(xv, D_MODEL) * g_ref[...]).astype(BF16)
        for cidx in range(ZP // nc):
            z_ref[:, cidx * nc:(cidx + 1) * nc] = _dot_nt(h, w_ref[cidx * nc:(cidx + 1) * nc, :])

    return pl.pallas_call(
        body, name="in_proj", grid=(t // tm,), out_shape=jax.ShapeDtypeStruct((t, ZP), F32),
        in_specs=[_rows(tm, D_MODEL), _resident((1, D_MODEL)), _resident((ZP, D_MODEL))],
        out_specs=_rows(tm, ZP), compiler_params=_params(("parallel",)),
    )(x, g_mix, w_in_t)


def attn_prep(zp, tabs, g_qa, g_kva, g_qn, g_kn, w_qb_t, w_kvb_t, tm, s_len):
    t = zp.shape[0]
    nsb = s_len // tm
    scale_a = (QK_NOPE + QK_ROPE) ** -0.5
    scale_b = HD_B ** -0.5

    def body(qb_ref, qlat_ref, kb_ref, vb_ref, ckv_ref, kpe_ref, tab_ref, gqa_ref, gkva_ref, gqn_ref, gkn_ref,
             wqb_ref, wkvb_ref, qa_o, ka_o, va_o, qb_o, kb_o, vb_o, cq_o, ckvn_o):
        ca, s1a, s2a = tab_ref[0], tab_ref[1], tab_ref[2]
        ck = tab_ref[3]
        cb, s1b, s2b = tab_ref[4], tab_ref[5], tab_ref[6]
        ql = qlat_ref[...]
        cq = (ql * _rstd(ql, Q_LORA) * gqa_ref[...]).astype(BF16)
        cq_o[...] = cq
        qa = _dot_nt(cq, wqb_ref[...])
        for h in range(H_A):
            sl = slice(h * HP, (h + 1) * HP)
            qa_o[:, sl] = (_rope_fwd(qa[:, sl], ca, s1a, s2a) * scale_a).astype(BF16)
        cr = ckv_ref[...]
        ckv = (cr * _rstd(cr, KV_LORA) * gkva_ref[...]).astype(BF16)
        ckvn_o[...] = ckv
        kva = _dot_nt(ckv, wkvb_ref[...])
        kpe = _rope_fwd(kpe_ref[...], ck, s1a, s2a)
        for h in range(H_A):
            sl = slice(h * HP, (h + 1) * HP)
            ka_o[:, sl] = (kva[:, sl] + kpe).astype(BF16)
        va_o[...] = kva[:, H_A * HP:].astype(BF16)
        gqn, gkn = gqn_ref[...], gkn_ref[...]
        for h in range(H_B):
            sl = slice(h * HP, (h + 1) * HP)
            xs = qb_ref[:, sl]
            y = xs * _rstd(xs, HD_B) * gqn
            qb_o[:, sl] = (_rope_fwd(y, cb, s1b, s2b) * scale_b).astype(BF16)
        for h in range(KV_B):
            sl = slice(h * HP, (h + 1) * HP)
            xs = kb_ref[:, sl]
            y = xs * _rstd(xs, HD_B) * gkn
            kb_o[:, sl] = _rope_fwd(y, cb, s1b, s2b).astype(BF16)
        vb_o[...] = vb_ref[...].astype(BF16)

    def o(width):
        return jax.ShapeDtypeStruct((t, width), BF16)

    return pl.pallas_call(
        body, name="attn_prep", grid=(t // tm,),
        out_shape=(o(H_A * HP), o(H_A * HP), o(H_A * HP), o(H_B * HP), o(KV_B * HP), o(KV_B * HP), o(Q_LORA), o(KV_LORA)),
        in_specs=[_rows(tm, 1024, 0), _rows(tm, 256, 12), _rows(tm, 256, 13), _rows(tm, 256, 14),
                  _rows(tm, 128, 30), _rows(tm, 128, 31),
                  pl.BlockSpec((7, tm, HP), lambda i: (0, i % nsb, 0)),
                  _resident((1, Q_LORA)), _resident((1, KV_LORA)), _resident((1, HP)), _resident((1, HP)),
                  _resident((H_A * HP, Q_LORA)), _resident((2 * H_A * HP, KV_LORA))],
        out_specs=(_rows(tm, H_A * HP), _rows(tm, H_A * HP), _rows(tm, H_A * HP), _rows(tm, H_B * HP),
                   _rows(tm, KV_B * HP), _rows(tm, KV_B * HP), _rows(tm, Q_LORA), _rows(tm, KV_LORA)),
        compiler_params=_params(("parallel",)),
    )(zp, zp, zp, zp, zp, zp, tabs, g_qa, g_kva, g_qn, g_kn, w_qb_t, w_kvb_t)


def attn_fwd(q, k, v, n_b, s_len, tq, name):
    t = q.shape[0]
    n_h, n_hk = q.shape[1] // HP, k.shape[1] // HP
    grp = n_h // n_hk
    nq = s_len // tq
    sub = min(tq, 256)

    def body(q_ref, k_ref, v_ref, o_ref, lse_ref):
        kv, vv = k_ref[...], v_ref[...]
        for r in range(tq // sub):
            rows = slice(r * sub, (r + 1) * sub)
            s = _dot_nt(q_ref[rows, :], kv)
            m = jnp.max(s, axis=-1, keepdims=True)
            p = jnp.exp(s - m)
            l = jnp.sum(p, axis=-1, keepdims=True)
            o_ref[rows, :] = (_dot_nn(p.astype(BF16), vv) * (1.0 / l)).astype(o_ref.dtype)
            lse_ref[rows, :] = jnp.broadcast_to(m + jnp.log(l), (sub, HP))

    qspec = pl.BlockSpec((tq, HP), lambda b, h, i: (b * nq + i, h))
    kspec = pl.BlockSpec((s_len, HP), lambda b, h, i: (b, h // grp))
    return pl.pallas_call(
        body, name=name, grid=(n_b, n_h, nq),
        out_shape=(jax.ShapeDtypeStruct((t, n_h * HP), BF16), jax.ShapeDtypeStruct((t, n_h * HP), F32)),
        in_specs=[qspec, kspec, kspec], out_specs=(qspec, qspec),
        compiler_params=_params(("parallel", "parallel", "parallel")),
    )(q, k, v)


def merge_fwd(oa, ob, zp, x, w_oa_t, w_ob_t, w_o, tm):
    t = x.shape[0]

    def body(oa_ref, ob_ref, ga_ref, gb_ref, x_ref, woa_ref, wob_ref, wo_ref, x1_o, mg_o):
        ya = _dot_nt(oa_ref[...], woa_ref[...])
        yb = _dot_nt(ob_ref[...], wob_ref[...])
        merged = (jax.nn.sigmoid(ga_ref[...]) * ya + jax.nn.sigmoid(gb_ref[...]) * yb).astype(BF16)
        mg_o[...] = merged
        x1_o[...] = x_ref[...] + _dot_nn(merged, wo_ref[...])

    return pl.pallas_call(
        body, name="merge_fwd", grid=(t // tm,),
        out_shape=(jax.ShapeDtypeStruct((t, D_MODEL), F32), jax.ShapeDtypeStruct((t, D_MODEL), BF16)),
        in_specs=[_rows(tm, H_A * HP), _rows(tm, H_B * HP), _rows(tm, 1024, 1), _rows(tm, 1024, 2), _rows(tm, D_MODEL),
                  _resident((D_MODEL, H_A * HP)), _resident((D_MODEL, H_B * HP)), _resident((D_MODEL, D_MODEL))],
        out_specs=(_rows(tm, D_MODEL), _rows(tm, D_MODEL)), compiler_params=_params(("parallel",)),
    )(oa, ob, zp, zp, x, w_oa_t, w_ob_t, w_o)


def mlp_fwd(x1, g_mlp, w_up_t, w_down, tm):
    t = x1.shape[0]
    fc = 1024

    def body(x_ref, g_ref, wup_ref, wdn_ref, x2_o, u_o):
        xv = x_ref[...]
        h2 = (xv * _rstd(xv, D_MODEL) * g_ref[...]).astype(BF16)
        acc = xv
        for cidx in range(D_FF // fc):
            sl = slice(cidx * fc, (cidx + 1) * fc)
            u = jnp.maximum(_dot_nt(h2, wup_ref[sl, :]), 0.0)
            u_o[:, sl] = u.astype(BF16)
            acc = acc + _dot_nn((u * u).astype(BF16), wdn_ref[sl, :])
        x2_o[...] = acc

    return pl.pallas_call(
        body, name="mlp_fwd", grid=(t // tm,),
        out_shape=(jax.ShapeDtypeStruct((t, D_MODEL), F32), jax.ShapeDtypeStruct((t, D_FF), BF16)),
        in_specs=[_rows(tm, D_MODEL), _resident((1, D_MODEL)), _resident((D_FF, D_MODEL)), _resident((D_FF, D_MODEL))],
        out_specs=(_rows(tm, D_MODEL), _rows(tm, D_FF)), compiler_params=_params(("parallel",)),
    )(x1, g_mlp, w_up_t, w_down)


def ple_loss_bwd(x2, p, tgt, g_ple, g_final, w_pg, w_ple_t, tm):
    t = x2.shape[0]
    inv_d = 1.0 / D_MODEL

    def body(x2_ref, p_ref, tg_ref, gp_ref, gf_ref, wpg_ref, wple_ref, dx2_o, dt_o, h3_o, dpe_o, st_o):
        @pl.when(pl.program_id(0) == 0)
        def _():
            st_o[...] = jnp.zeros_like(st_o)

        x2v = x2_ref[...]
        gp, gf = gp_ref[...], gf_ref[...]
        r2 = _rstd(x2v, D_MODEL)
        xh2 = x2v * r2
        h3 = (xh2 * gp).astype(BF16)
        h3_o[...] = h3
        gate = jax.nn.sigmoid(_dot_nn(h3, wpg_ref[...]))
        pe = _dot_nt(p_ref[...].astype(BF16), wple_ref[...])
        x3 = x2v + gate * pe
        r3 = _rstd(x3, D_MODEL)
        xh3 = x3 * r3
        err = xh3 * gf - tg_ref[...]
        dy = err * inv_d
        dx3 = _rms_bwd(dy, xh3, r3, gf, D_MODEL)
        dpe_o[...] = (dx3 * gate).astype(BF16)
        dt = (dx3 * pe * gate * (1.0 - gate)).astype(BF16)
        dt_o[...] = dt
        dh3 = _dot_nt(dt, wpg_ref[...])
        dx2_o[...] = dx3 + _rms_bwd(dh3, xh2, r2, gp, D_MODEL)
        st_o[0:1, :] += _colsum(dh3 * xh2)
        st_o[1:2, :] += _colsum(dy * xh3)
        st_o[2:3, :] += _colsum(err * err) * (0.5 * inv_d)

    bf = jax.ShapeDtypeStruct((t, D_MODEL), BF16)
    return pl.pallas_call(
        body, name="ple_loss_bwd", grid=(t // tm,),
        out_shape=(jax.ShapeDtypeStruct((t, D_MODEL), F32), bf, bf, bf, jax.ShapeDtypeStruct((3, D_MODEL), F32)),
        in_specs=[_rows(tm, D_MODEL), _rows(tm, PLE_DIM), _rows(tm, D_MODEL), _resident((1, D_MODEL)), _resident((1, D_MODEL)),
                  _resident((D_MODEL, D_MODEL)), _resident((D_MODEL, PLE_DIM))],
        out_specs=(_rows(tm, D_MODEL), _rows(tm, D_MODEL), _rows(tm, D_MODEL), _rows(tm, D_MODEL),
                   pl.BlockSpec((3, D_MODEL), lambda i: (0, 0))),
        compiler_params=_params(("arbitrary",)),
    )(x2, p, tgt, g_ple, g_final, w_pg, w_ple_t)


def mlp_bwd(dx2, x1, u, g_mlp, w_up_t, w_down, tm):
    t = x1.shape[0]
    fc = 1024

    def body(dx2_ref, x1_ref, u_ref, g_ref, wup_ref, wdn_ref, dx1_o, da_o, h2_o, st_o):
        @pl.when(pl.program_id(0) == 0)
        def _():
            st_o[...] = jnp.zeros_like(st_o)

        d2 = dx2_ref[...]
        d2b = d2.astype(BF16)
        dh2 = jnp.zeros((tm, D_MODEL), F32)
        for cidx in range(D_FF // fc):
            sl = slice(cidx * fc, (cidx + 1) * fc)
            da = (_dot_nt(d2b, wdn_ref[sl, :]) * (2.0 * u_ref[:, sl].astype(F32))).astype(BF16)
            da_o[:, sl] = da
            dh2 = dh2 + _dot_nn(da, wup_ref[sl, :])
        xv = x1_ref[...]
        g = g_ref[...]
        r1 = _rstd(xv, D_MODEL)
        xh1 = xv * r1
        h2_o[...] = (xh1 * g).astype(BF16)
        st_o[...] += _colsum(dh2 * xh1)
        dx1_o[...] = d2 + _rms_bwd(dh2, xh1, r1, g, D_MODEL)

    return pl.pallas_call(
        body, name="mlp_bwd", grid=(t // tm,),
        out_shape=(jax.ShapeDtypeStruct((t, D_MODEL), F32), jax.ShapeDtypeStruct((t, D_FF), BF16),
                   jax.ShapeDtypeStruct((t, D_MODEL), BF16), jax.ShapeDtypeStruct((1, D_MODEL), F32)),
        in_specs=[_rows(tm, D_MODEL), _rows(tm, D_MODEL), _rows(tm, D_FF), _resident((1, D_MODEL)),
                  _resident((D_FF, D_MODEL)), _resident((D_FF, D_MODEL))],
        out_specs=(_rows(tm, D_MODEL), _rows(tm, D_FF), _rows(tm, D_MODEL), pl.BlockSpec((1, D_MODEL), lambda i: (0, 0))),
        compiler_params=_params(("arbitrary",)),
    )(dx2, x1, u, g_mlp, w_up_t, w_down)


def merge_bwd(dx1, oa, ob, zp, w_oa_t, w_ob_t, w_o, tm):
    t = dx1.shape[0]

    def body(dx1_ref, oa_ref, ob_ref, ga_ref, gb_ref, woa_ref, wob_ref, wo_ref, doa_o, dob_o, dg_o, dya_o, dyb_o):
        dm = _dot_nt(dx1_ref[...].astype(BF16), wo_ref[...])
        for o_ref, g_ref, w_ref, do_o, dy_o, col in ((oa_ref, ga_ref, woa_ref, doa_o, dya_o, 0),
                                                     (ob_ref, gb_ref, wob_ref, dob_o, dyb_o, 1)):
            yv = _dot_nt(o_ref[...], w_ref[...])
            sg = jax.nn.sigmoid(g_ref[...])
            dyv = (dm * sg).astype(BF16)
            dy_o[...] = dyv
            dg_o[:, col * D_MODEL:(col + 1) * D_MODEL] = (dm * yv * sg * (1.0 - sg)).astype(BF16)
            do_o[...] = _dot_nn(dyv, w_ref[...]).astype(BF16)

    bf = jax.ShapeDtypeStruct((t, D_MODEL), BF16)
    return pl.pallas_call(
        body, name="merge_bwd", grid=(t // tm,),
        out_shape=(bf, bf, jax.ShapeDtypeStruct((t, 2 * D_MODEL), BF16), bf, bf),
        in_specs=[_rows(tm, D_MODEL), _rows(tm, H_A * HP), _rows(tm, H_B * HP), _rows(tm, 1024, 1), _rows(tm, 1024, 2),
                  _resident((D_MODEL, H_A * HP)), _resident((D_MODEL, H_B * HP)), _resident((D_MODEL, D_MODEL))],
        out_specs=(_rows(tm, D_MODEL), _rows(tm, D_MODEL), _rows(tm, 2 * D_MODEL), _rows(tm, D_MODEL), _rows(tm, D_MODEL)),
        compiler_params=_params(("parallel",)),
    )(dx1, oa, ob, zp, zp, w_oa_t, w_ob_t, w_o)


def attn_bwd(q, k, v, do, o, lse, n_b, s_len, tq, name):
    t = q.shape[0]
    n_h, n_hk = q.shape[1] // HP, k.shape[1] // HP
    grp = n_h // n_hk
    nq = s_len // tq
    sub = min(tq, 256)

    def body(q_ref, k_ref, v_ref, do_ref, o_ref, lse_ref, dq_o, dk_o, dv_o, p_s, ds_s):
        @pl.when((pl.program_id(2) == 0) & (pl.program_id(3) == 0))
        def _():
            dk_o[...] = jnp.zeros_like(dk_o)
            dv_o[...] = jnp.zeros_like(dv_o)

        kv, vv = k_ref[...], v_ref[...]
        for r in range(tq // sub):
            rows = slice(r * sub, (r + 1) * sub)
            qv, dov = q_ref[rows, :], do_ref[rows, :]
            delta = jnp.sum(dov.astype(F32) * o_ref[rows, :].astype(F32), axis=-1, keepdims=True)
            p = jnp.exp(_dot_nt(qv, kv) - lse_ref[rows, 0:1])
            ds = (p * (_dot_nt(dov, vv) - delta)).astype(BF16)
            p_s[rows, :] = p.astype(BF16)
            ds_s[rows, :] = ds
            dq_o[rows, :] = _dot_nn(ds, kv)
        dk_o[...] += _dot_tn(ds_s[...], q_ref[...])
        dv_o[...] += _dot_tn(p_s[...], do_ref[...])

    qspec = pl.BlockSpec((tq, HP), lambda b, hk, g, i: (b * nq + i, hk * grp + g))
    kspec = pl.BlockSpec((s_len, HP), lambda b, hk, g, i: (b, hk))
    return pl.pallas_call(
        body, name=name, grid=(n_b, n_hk, grp, nq),
        out_shape=(jax.ShapeDtypeStruct((t, n_h * HP), F32), jax.ShapeDtypeStruct((t, n_hk * HP), F32),
                   jax.ShapeDtypeStruct((t, n_hk * HP), F32)),
        in_specs=[qspec, kspec, kspec, qspec, qspec, qspec], out_specs=(qspec, kspec, kspec),
        scratch_shapes=[pltpu.VMEM((tq, s_len), BF16), pltpu.VMEM((tq, s_len), BF16)],
        compiler_params=_params(("parallel", "parallel", "arbitrary", "arbitrary")),
    )(q, k, v, do, o, lse)


def prep_bwd(dqa, dka, dva, dqb, dkb, dvb, zp, tabs, g_qa, g_kva, g_qn, g_kn, w_qb_t, w_kvb_t, tm, s_len):
    t = zp.shape[0]
    nsb = s_len // tm
    scale_a = (QK_NOPE + QK_ROPE) ** -0.5
    scale_b = HD_B ** -0.5

    def body(dqa_ref, dka_ref, dva_ref, dqb_ref, dkb_ref, dvb_ref, qb_ref, qlat_ref, kb_ref, ckv_ref, tab_ref,
             gqa_ref, gkva_ref, gqn_ref, gkn_ref, wqb_ref, wkvb_ref, dzq_o, dsm_o, dqap_o, dkva_o, st_o):
        @pl.when(pl.program_id(0) == 0)
        def _():
            st_o[...] = jnp.zeros_like(st_o)

        ca, s1a, s2a = tab_ref[0], tab_ref[1], tab_ref[2]
        ck = tab_ref[3]
        cb, s1b, s2b = tab_ref[4], tab_ref[5], tab_ref[6]
        for h in range(H_A):
            sl = slice(h * HP, (h + 1) * HP)
            dqap_o[:, sl] = _rope_bwd(dqa_ref[:, sl] * scale_a, ca, s1a, s2a).astype(BF16)
        dcq = _dot_nn(dqap_o[...], wqb_ref[...])
        ql = qlat_ref[...]
        rq = _rstd(ql, Q_LORA)
        xh = ql * rq
        gqa = gqa_ref[...]
        st_o[0:1, :] += _colsum(dcq * xh)
        dsm_o[:, 0:256] = _rms_bwd(dcq, xh, rq, gqa, Q_LORA).astype(BF16)
        dkpe = jnp.zeros((tm, HP), F32)
        for h in range(H_A):
            sl = slice(h * HP, (h + 1) * HP)
            dk = dka_ref[:, sl]
            dkpe = dkpe + dk
            dkva_o[:, sl] = dk.astype(BF16)
        dkva_o[:, H_A * HP:] = dva_ref[...].astype(BF16)
        dsm_o[:, 896:1024] = _rope_bwd(dkpe, ck, s1a, s2a).astype(BF16)
        dckv = _dot_nn(dkva_o[...], wkvb_ref[...])
        cr = ckv_ref[...]
        rk = _rstd(cr, KV_LORA)
        xh = cr * rk
        st_o[1:2, 0:128] += _colsum(dckv * xh)
        dsm_o[:, 768:896] = _rms_bwd(dckv, xh, rk, gkva_ref[...], KV_LORA).astype(BF16)
        gqn, gkn = gqn_ref[...], gkn_ref[...]
        dgq = jnp.zeros((1, HP), F32)
        for h in range(H_B):
            sl = slice(h * HP, (h + 1) * HP)
            dy = _rope_bwd(dqb_ref[:, sl] * scale_b, cb, s1b, s2b)
            xs = qb_ref[:, sl]
            r = _rstd(xs, HD_B)
            xh = xs * r
            dgq = dgq + _colsum(dy * xh)
            dzq_o[:, sl] = _rms_bwd(dy, xh, r, gqn, HD_B).astype(BF16)
        st_o[2:3, 0:128] += dgq
        dgk = jnp.zeros((1, HP), F32)
        for h in range(KV_B):
            sl = slice(h * HP, (h + 1) * HP)
            dy = _rope_bwd(dkb_ref[:, sl], cb, s1b, s2b)
            xs = kb_ref[:, sl]
            r = _rstd(xs, HD_B)
            xh = xs * r
            dgk = dgk + _colsum(dy * xh)
            dsm_o[:, 256 + h * HP:256 + (h + 1) * HP] = _rms_bwd(dy, xh, r, gkn, HD_B).astype(BF16)
        st_o[3:4, 0:128] += dgk
        dsm_o[:, 512:768] = dvb_ref[...].astype(BF16)

    bf = jax.ShapeDtypeStruct((t, 1024), BF16)
    return pl.pallas_call(
        body, name="prep_bwd", grid=(t // tm,),
        out_shape=(bf, bf, bf, jax.ShapeDtypeStruct((t, 2048), BF16), jax.ShapeDtypeStruct((4, 256), F32)),
        in_specs=[_rows(tm, 1024), _rows(tm, 1024), _rows(tm, 1024), _rows(tm, 1024), _rows(tm, 256), _rows(tm, 256),
                  _rows(tm, 1024, 0), _rows(tm, 256, 12), _rows(tm, 256, 13), _rows(tm, 128, 30),
                  pl.BlockSpec((7, tm, HP), lambda i: (0, i % nsb, 0)),
                  _resident((1, Q_LORA)), _resident((1, KV_LORA)), _resident((1, HP)), _resident((1, HP)),
                  _resident((H_A * HP, Q_LORA)), _resident((2 * H_A * HP, KV_LORA))],
        out_specs=(_rows(tm, 1024), _rows(tm, 1024), _rows(tm, 1024), _rows(tm, 2048), pl.BlockSpec((4, 256), lambda i: (0, 0))),
        compiler_params=_params(("arbitrary",)),
    )(dqa, dka, dva, dqb, dkb, dvb, zp, zp, zp, zp, tabs, g_qa, g_kva, g_qn, g_kn, w_qb_t, w_kvb_t)


def in_bwd(dzq, dgab, dsm, x, dx1, g_mix, w_in_t, tm):
    t = x.shape[0]

    def body(dzq_ref, dg_ref, dsm_ref, x_ref, dx1_ref, g_ref, w_ref, dx_o, h_o, st_o):
        @pl.when(pl.program_id(0) == 0)
        def _():
            st_o[...] = jnp.zeros_like(st_o)

        dh = _dot_nn(dzq_ref[...], w_ref[0:1024, :])
        dh = dh + _dot_nn(dg_ref[...], w_ref[1024:3072, :])
        dh = dh + _dot_nn(dsm_ref[...], w_ref[3072:4096, :])
        xv = x_ref[...]
        g = g_ref[...]
        r = _rstd(xv, D_MODEL)
        xh = xv * r
        h_o[...] = (xh * g).astype(BF16)
        st_o[...] += _colsum(dh * xh)
        dx_o[...] = dx1_ref[...] + _rms_bwd(dh, xh, r, g, D_MODEL)

    return pl.pallas_call(
        body, name="in_bwd", grid=(t // tm,),
        out_shape=(jax.ShapeDtypeStruct((t, D_MODEL), F32), jax.ShapeDtypeStruct((t, D_MODEL), BF16),
                   jax.ShapeDtypeStruct((1, D_MODEL), F32)),
        in_specs=[_rows(tm, 1024), _rows(tm, 2048), _rows(tm, 1024), _rows(tm, D_MODEL), _rows(tm, D_MODEL),
                  _resident((1, D_MODEL)), _resident((ZP, D_MODEL))],
        out_specs=(_rows(tm, D_MODEL), _rows(tm, D_MODEL), pl.BlockSpec((1, D_MODEL), lambda i: (0, 0))),
        compiler_params=_params(("arbitrary",)),
    )(dzq, dgab, dsm, x, dx1, g_mix, w_in_t)


def matmul_tn(a, b, name, square_a=False):
    t, m = a.shape
    n = b.shape[1]
    bm = min(m, 512)
    tk = min(t, 512)

    def body(a_ref, b_ref, o_ref):
        @pl.when(pl.program_id(1) == 0)
        def _():
            o_ref[...] = jnp.zeros_like(o_ref)

        av = a_ref[...]
        if square_a:
            av = (av.astype(F32) * av.astype(F32))
        o_ref[...] += _dot_tn(av.astype(BF16), b_ref[...].astype(BF16))

    return pl.pallas_call(
        body, name=name, grid=(m // bm, t // tk), out_shape=jax.ShapeDtypeStruct((m, n), F32),
        in_specs=[pl.BlockSpec((tk, bm), lambda i, kk: (kk, i)), pl.BlockSpec((tk, n), lambda i, kk: (kk, 0))],
        out_specs=pl.BlockSpec((bm, n), lambda i, kk: (i, 0)),
        compiler_params=_params(("parallel", "arbitrary")),
    )(a, b)


def adamw(w, g, m, v, name):
    r, c = w.shape
    tr = r if r <= 256 else 256
    c1 = 1.0 - ADAM_B1 ** ADAM_STEP
    c2 = 1.0 - ADAM_B2 ** ADAM_STEP

    def body(w_ref, g_ref, m_ref, v_ref, d_o, m_o, v_o):
        gv = g_ref[...]
        mn = ADAM_B1 * m_ref[...] + (1.0 - ADAM_B1) * gv
        vn = ADAM_B2 * v_ref[...] + (1.0 - ADAM_B2) * (gv * gv)
        m_o[...] = mn
        v_o[...] = vn
        d_o[...] = -ADAM_LR * ((mn / c1) / (jnp.sqrt(vn / c2) + ADAM_EPS) + ADAM_WD * w_ref[...])

    spec = pl.BlockSpec((tr, c), lambda i: (i, 0))
    shp = jax.ShapeDtypeStruct((r, c), F32)
    return pl.pallas_call(
        body, name=name, grid=(r // tr,), out_shape=(shp, shp, shp), in_specs=[spec] * 4, out_specs=(spec,) * 3,
        compiler_params=_params(("parallel",)),
    )(w, g, m, v)


def _rope_tables(s_len):
    def angles(pos, dim):
        inv = np.float32(ROPE_THETA) ** (-np.arange(0, dim, 2, dtype=np.float32) / np.float32(dim))
        return pos.astype(np.float32)[:, None] * inv[None, :]

    tpos = np.arange(s_len)
    a1 = angles(tpos, QK_ROPE)
    ar = angles(tpos // GRID_W, HD_B // 2)
    ac = angles(tpos % GRID_W, HD_B // 2)
    z16 = np.zeros((s_len, 16), np.float32)
    z32 = np.zeros((s_len, 32), np.float32)
    z64 = np.zeros((s_len, 64), np.float32)
    one64 = np.ones((s_len, 64), np.float32)
    c1, s1 = np.cos(a1), np.sin(a1)
    ca = np.concatenate([one64, c1, c1, z32], axis=1)
    ck = np.concatenate([z64, c1, c1, z32], axis=1)
    s1a = np.concatenate([z64, -s1, z16, z32], axis=1)
    s2a = np.concatenate([z64, z16, s1, z32], axis=1)
    cr, sr, cc, sc = np.cos(ar), np.sin(ar), np.cos(ac), np.sin(ac)
    cb = np.concatenate([cr, cr, cc, cc, z64], axis=1)
    s1b = np.concatenate([-sr, z16, -sc, z16, z64], axis=1)
    s2b = np.concatenate([z16, sr, z16, sc, z64], axis=1)
    return jnp.asarray(np.stack([ca, s1a, s2a, ck, cb, s1b, s2b]).astype(np.float32))


def _pad_heads(a, n_heads, axis):
    shp = a.shape
    a = a.reshape(shp[:axis] + (n_heads, shp[axis] // n_heads) + shp[axis + 1:])
    pad = [(0, 0)] * a.ndim
    pad[axis + 1] = (0, HP - a.shape[axis + 1])
    a = jnp.pad(a, pad)
    return a.reshape(shp[:axis] + (n_heads * HP,) + shp[axis + 1:])


def _unpad_heads(a, n_heads, width, axis):
    shp = a.shape
    a = a.reshape(shp[:axis] + (n_heads, HP) + shp[axis + 1:])
    a = lax.slice_in_dim(a, 0, width, axis=axis + 1)
    return a.reshape(shp[:axis] + (n_heads * width,) + shp[axis + 1:])


def _pack_rows(blocks, lead):
    parts = []
    for name, rows in PACK:
        b = blocks[name]
        padr = rows - b.shape[-2]
        if padr:
            b = jnp.pad(b, [(0, 0)] * (b.ndim - 2) + [(0, padr), (0, 0)])
        parts.append(b)
    return jnp.concatenate(parts, axis=len(lead))


def _expand_w_in(wt):
    z64 = jnp.zeros((64, D_MODEL), wt.dtype)
    z32 = jnp.zeros((32, D_MODEL), wt.dtype)
    return jnp.concatenate([
        _pad_heads(wt[416:928], H_B, 0), wt[1184:2208], wt[2208:3232], wt[0:256],
        _pad_heads(wt[928:1056], KV_B, 0), _pad_heads(wt[1056:1184], KV_B, 0), wt[256:384],
        z64, wt[384:416], z32], axis=0)


def _collapse_w_in(dq, dg, ds):
    return jnp.concatenate([
        ds[0:256], ds[768:896], ds[960:992], _unpad_heads(dq, H_B, HD_B, 0), _unpad_heads(ds[256:512], KV_B, HD_B, 0),
        _unpad_heads(ds[512:768], KV_B, HD_B, 0), dg], axis=0)


def kernel(x, p, g_mix, w_in, g_qa, w_qb, g_kva, w_kvb, g_qn, g_kn, w_oa, w_ob, w_o, g_mlp, w_up, w_down, g_ple, w_ple_gate, w_ple, g_final, loss_target, m_g_mix, m_w_in, m_g_qa, m_w_qb, m_g_kva, m_w_kvb, m_g_qn, m_g_kn, m_w_oa, m_w_ob, m_w_o, m_g_mlp, m_w_up, m_w_down, m_g_ple, m_w_ple_gate, m_w_ple, m_g_final, v_g_mix, v_w_in, v_g_qa, v_w_qb, v_g_kva, v_w_kvb, v_g_qn, v_g_kn, v_w_oa, v_w_ob, v_w_o, v_g_mlp, v_w_up, v_w_down, v_g_ple, v_w_ple_gate, v_w_ple, v_g_final):
    n_b, s_len, _ = x.shape
    t = n_b * s_len
    tm = min(256, s_len)
    tq_f = min(512, s_len)
    tq_b = min(512, s_len)

    mats = dict(w_in=(w_in, m_w_in, v_w_in), w_qb=(w_qb, m_w_qb, v_w_qb), w_kvb=(w_kvb, m_w_kvb, v_w_kvb),
                w_oa=(w_oa, m_w_oa, v_w_oa), w_ob=(w_ob, m_w_ob, v_w_ob), w_o=(w_o, m_w_o, v_w_o),
                w_up=(w_up, m_w_up, v_w_up), w_down=(w_down, m_w_down, v_w_down),
                w_ple_gate=(w_ple_gate, m_w_ple_gate, v_w_ple_gate), w_ple=(w_ple, m_w_ple, v_w_ple))
    col_sharded = ("w_in", "w_qb", "w_kvb", "w_oa", "w_ob", "w_up", "w_ple")

    blocks = {}
    for name, (w, _, _) in mats.items():
        w2 = w[0]
        if name in col_sharded:
            w2 = w2.T
        blocks[name] = w2.reshape(-1, D_MODEL).astype(BF16)
    full = allgather_rows(_pack_rows(blocks, ()))

    def gathered(name, rows, width):
        off, _ = PACK_OFF[name]
        return full[:, off:off + rows].reshape(-1, width)

    w_in_t = _expand_w_in(gathered("w_in", 404, D_MODEL))
    w_qb_t = _pad_heads(gathered("w_qb", 24, Q_LORA), H_A, 0)
    wkvb = gathered("w_kvb", 16, KV_LORA).reshape(H_A, 2, 64, KV_LORA)
    w_kvb_t = jnp.concatenate([_pad_heads(wkvb[:, 0].reshape(-1, KV_LORA), H_A, 0),
                               _pad_heads(wkvb[:, 1].reshape(-1, KV_LORA), H_A, 0)], axis=0)
    w_oa_t = _pad_heads(gathered("w_oa", 64, H_A * V_DIM_A), H_A, 1)
    w_ob_t = _pad_heads(gathered("w_ob", 64, H_B * HD_B), H_B, 1)
    w_o_f = gathered("w_o", 128, D_MODEL)
    w_up_t = gathered("w_up", 512, D_MODEL)
    w_down_f = gathered("w_down", 512, D_MODEL)
    w_pg_f = gathered("w_ple_gate", 128, D_MODEL)
    w_ple_t = gathered("w_ple", 32, PLE_DIM)

    tabs = _rope_tables(s_len)
    g_qn_p = jnp.pad(g_qn, ((0, 0), (0, HP - HD_B)))
    g_kn_p = jnp.pad(g_kn, ((0, 0), (0, HP - HD_B)))
    xf = x.reshape(t, D_MODEL)
    pf = p.reshape(t, PLE_DIM)
    tgt = loss_target.reshape(t, D_MODEL)

    zp = in_proj(xf, g_mix, w_in_t, tm)
    qa, ka, va, qb, kb, vb, cq, ckv = attn_prep(zp, tabs, g_qa, g_kva, g_qn_p, g_kn_p, w_qb_t, w_kvb_t, tm, s_len)
    oa, lse_a = attn_fwd(qa, ka, va, n_b, s_len, tq_f, "attn_a_fwd")
    ob, lse_b = attn_fwd(qb, kb, vb, n_b, s_len, tq_f, "attn_b_fwd")
    x1, merged = merge_fwd(oa, ob, zp, xf, w_oa_t, w_ob_t, w_o_f, tm)
    x2, u = mlp_fwd(x1, g_mlp, w_up_t, w_down_f, tm)
    dx2, dt, h3, dpe, st_ple = ple_loss_bwd(x2, pf, tgt, g_ple, g_final.reshape(1, D_MODEL), w_pg_f, w_ple_t, tm)
    dx1, da, h2, st_mlp = mlp_bwd(dx2, x1, u, g_mlp, w_up_t, w_down_f, tm)
    doa, dob, dgab, dya, dyb = merge_bwd(dx1, oa, ob, zp, w_oa_t, w_ob_t, w_o_f, tm)
    dqa, dka, dva = attn_bwd(qa, ka, va, doa, oa, lse_a, n_b, s_len, tq_b, "attn_a_bwd")
    dqb, dkb, dvb = attn_bwd(qb, kb, vb, dob, ob, lse_b, n_b, s_len, tq_b, "attn_b_bwd")
    dzq, dsm, dqap, dkva, st_prep = prep_bwd(dqa, dka, dva, dqb, dkb, dvb, zp, tabs, g_qa, g_kva, g_qn_p, g_kn_p,
                                             w_qb_t, w_kvb_t, tm, s_len)
    grad_x, h, st_mix = in_bwd(dzq, dgab, dsm, xf, dx1, g_mix, w_in_t, tm)

    gw_in = _collapse_w_in(matmul_tn(dzq, h, "gw_in_q"), matmul_tn(dgab, h, "gw_in_g"), matmul_tn(dsm, h, "gw_in_s"))
    gw_qb = _unpad_heads(matmul_tn(dqap, cq, "gw_qb"), H_A, QK_NOPE + QK_ROPE, 0)
    gkv = matmul_tn(dkva, ckv, "gw_kvb")
    gw_kvb = jnp.stack([_unpad_heads(gkv[:H_A * HP], H_A, 64, 0).reshape(H_A, 64, KV_LORA),
                        _unpad_heads(gkv[H_A * HP:], H_A, 64, 0).reshape(H_A, 64, KV_LORA)], axis=1)
    gw_oa = _unpad_heads(matmul_tn(dya, oa, "gw_oa"), H_A, V_DIM_A, 1)
    gw_ob = _unpad_heads(matmul_tn(dyb, ob, "gw_ob"), H_B, HD_B, 1)
    gw_o = matmul_tn(merged, dx1, "gw_o")
    gw_up = matmul_tn(da, h2, "gw_up")
    gw_down = matmul_tn(u, dx2, "gw_down", square_a=True)
    gw_pg = matmul_tn(h3, dt, "gw_pg")
    gw_ple = matmul_tn(dpe, pf, "gw_ple")
    gblocks = dict(w_in=gw_in, w_qb=gw_qb, w_kvb=gw_kvb, w_oa=gw_oa, w_ob=gw_ob, w_o=gw_o, w_up=gw_up,
                   w_down=gw_down, w_ple_gate=gw_pg, w_ple=gw_ple)
    gblocks = {n: a.reshape(N_DEV, -1, D_MODEL).astype(BF16) for n, a in gblocks.items()}
    gpack = _pack_rows(gblocks, (N_DEV,))

    core = lax.axis_index("c").astype(jnp.int32).reshape(1)
    chip = (2 * lax.axis_index("x") + lax.axis_index("y")).astype(jnp.int32).reshape(1)
    part = add_pairs(gpack, exchange_sibling(gpack), core)
    gshard = sum_chips(part, exchange_chips(part), chip)

    def row(a, width):
        return jnp.pad(a, ((0, 0), (0, D_MODEL - width)))

    def tile8(a, width):
        return jnp.pad(a, ((0, 7), (0, D_MODEL - width)))

    stats = jnp.concatenate([
        tile8(st_mix, D_MODEL), tile8(st_prep[0:1], 256), tile8(st_prep[1:2], 256), tile8(st_prep[2:3], 256),
        tile8(st_prep[3:4], 256), tile8(st_mlp, D_MODEL), tile8(st_ple[0:1], D_MODEL), tile8(st_ple[1:2], D_MODEL),
        tile8(st_ple[2:3], D_MODEL)], axis=0)
    stats = allreduce_stats(stats)[::8]
    loss = jnp.sum(stats[ST_LOSS])
    stats = jnp.pad(stats, ((0, ST_ROWS - 9), (0, 0)))

    out_g, out_d, out_m, out_v = {}, {}, {}, {}
    for name, (w, m, v) in mats.items():
        off, _ = PACK_OFF[name]
        shard_shape = w.shape[1:]
        if name in col_sharded:
            rows_t = shard_shape[1]
            g2 = gshard[off:off + (rows_t * shard_shape[0]) // D_MODEL].reshape(rows_t, shard_shape[0]).T
        else:
            g2 = gshard[off:off + shard_shape[0]]
        d2, m2, v2 = adamw(w[0], g2, m[0], v[0], "adamw_" + name)
        out_g[name], out_d[name], out_m[name], out_v[name] = g2[None], d2[None], m2[None], v2[None]

    gains = (("g_mix", g_mix, m_g_mix, v_g_mix, ST_G_MIX, D_MODEL), ("g_qa", g_qa, m_g_qa, v_g_qa, ST_G_QA, Q_LORA),
             ("g_kva", g_kva, m_g_kva, v_g_kva, ST_G_KVA, KV_LORA), ("g_qn", g_qn, m_g_qn, v_g_qn, ST_G_QN, HD_B),
             ("g_kn", g_kn, m_g_kn, v_g_kn, ST_G_KN, HD_B), ("g_mlp", g_mlp, m_g_mlp, v_g_mlp, ST_G_MLP, D_MODEL),
             ("g_ple", g_ple, m_g_ple, v_g_ple, ST_G_PLE, D_MODEL),
             ("g_final", g_final, m_g_final, v_g_final, ST_G_FINAL, D_MODEL))

    def gain_pack(idx):
        rows_ = [row(gn[idx].reshape(1, -1), gn[5]) for gn in gains]
        return jnp.concatenate(rows_ + [jnp.zeros((ST_ROWS - len(gains), D_MODEL), F32)], axis=0)

    gd, gm, gv = adamw(gain_pack(1), stats, gain_pack(2), gain_pack(3), "adamw_gains")
    for name, w, _, _, r_, width in gains:
        for dst, src in ((out_g, stats), (out_d, gd), (out_m, gm), (out_v, gv)):
            dst[name] = src[r_, :width].reshape(w.shape)

    order = ("g_mix", "w_in", "g_qa", "w_qb", "g_kva", "w_kvb", "g_qn", "g_kn", "w_oa", "w_ob", "w_o", "g_mlp",
             "w_up", "w_down", "g_ple", "w_ple_gate", "w_ple", "g_final")
    return (loss, grad_x.reshape(x.shape), *[out_g[n] for n in order], *[out_d[n] for n in order],
            *[out_m[n] for n in order], *[out_v[n] for n in order])
```

```python
import numpy as np
import jax
import jax.numpy as jnp
from jax import lax
from jax.experimental import pallas as pl
from jax.experimental.pallas import tpu as pltpu

F32 = jnp.float32
BF16 = jnp.bfloat16

D_MODEL = 1024
EPS = 1e-6
ROPE_THETA = 10000.0
GRID_W = 64
H_A = 8
QK_NOPE = 64
QK_ROPE = 32
V_DIM_A = 64
Q_LORA = 256
KV_LORA = 128
H_B = 8
KV_B = 2
HD_B = 64
D_FF = 4 * D_MODEL
PLE_DIM = 256
HP = 128
ZP = 4096
N_DEV = 8
N_CHIP = 4

ADAM_LR = 0.001
ADAM_B1 = 0.9
ADAM_B2 = 0.999
ADAM_EPS = 1e-08
ADAM_WD = 0.01
ADAM_STEP = 10

VMEM_LIMIT = 52 * 1024 * 1024

PACK = (("w_in", 416), ("w_qb", 32), ("w_kvb", 16), ("w_oa", 64), ("w_ob", 64), ("w_o", 128),
        ("w_up", 512), ("w_down", 512), ("w_ple_gate", 128), ("w_ple", 32))
PACK_OFF = {}
_o = 0
for _n, _r in PACK:
    PACK_OFF[_n] = (_o, _r)
    _o += _r
PACK_ROWS = _o

ST_G_MIX, ST_G_QA, ST_G_KVA, ST_G_QN, ST_G_KN, ST_G_MLP, ST_G_PLE, ST_G_FINAL, ST_LOSS = range(9)
ST_ROWS = 16


def _dot_nn(a, b):
    return lax.dot_general(a, b, (((1,), (0,)), ((), ())), preferred_element_type=F32)


def _dot_nt(a, b):
    return lax.dot_general(a, b, (((1,), (1,)), ((), ())), preferred_element_type=F32)


def _dot_tn(a, b):
    return lax.dot_general(a, b, (((0,), (0,)), ((), ())), preferred_element_type=F32)


def _rstd(x, n):
    return lax.rsqrt(jnp.sum(x * x, axis=-1, keepdims=True) * (1.0 / n) + EPS)


def _rms_bwd(dy, xh, r, g, n):
    dxh = dy * g
    return r * (dxh - xh * (jnp.sum(dxh * xh, axis=-1, keepdims=True) * (1.0 / n)))


def _rope_fwd(x, c, s1, s2):
    return x * c + pltpu.roll(x, HP - 16, 1) * s1 + pltpu.roll(x, 16, 1) * s2


def _rope_bwd(d, c, s1, s2):
    return d * c + pltpu.roll(d * s1, 16, 1) + pltpu.roll(d * s2, HP - 16, 1)


def _colsum(v):
    return jnp.sum(v, axis=0, keepdims=True)


def _params(sem=None, vmem=VMEM_LIMIT):
    return pltpu.CompilerParams(dimension_semantics=sem, vmem_limit_bytes=vmem)


def _resident(shape):
    nd = len(shape)
    return pl.BlockSpec(shape, lambda *_: (0,) * nd, pipeline_mode=pl.Buffered(1))


def _rows(tm, width, col=0):
    return pl.BlockSpec((tm, width), lambda i: (i, col))


def _mesh_pos():
    return lax.axis_index("x"), lax.axis_index("y"), lax.axis_index("c")


def _flip(v, bit):
    return (1 - v) if bit else v


_ANY = pl.BlockSpec(memory_space=pl.ANY)
_MESH = pl.DeviceIdType.MESH


def allgather_rows(shard):
    r, w = shard.shape

    def body(x_ref, out_ref, send_sems, recv_sems, local_sem):
        x, y, c = _mesh_pos()
        me, sibling = (x, y, c), (x, y, 1 - c)
        chips = [(1 - x, y), (x, 1 - y), (1 - x, 1 - y)]

        def slot(px, py, pc):
            return out_ref.at[4 * px + 2 * py + pc]

        def copy(k, block, to, src=None):
            return pltpu.make_async_remote_copy(
                src_ref=slot(*block) if src is None else src, dst_ref=slot(*block),
                send_sem=send_sems.at[k], recv_sem=recv_sems.at[k], device_id=to, device_id_type=_MESH)

        mine = pltpu.make_async_copy(x_ref, slot(*me), local_sem)
        mine.start()
        first = [copy(0, me, sibling, src=x_ref)]
        first += [copy(1 + j, me, (*chip, c), src=x_ref) for j, chip in enumerate(chips)]
        for cp in first:
            cp.start()
        passed = [copy(4 + j, (*chip, c), sibling) for j, chip in enumerate(chips)]
        for j, chip in enumerate(chips):
            copy(1 + j, (*chip, c), me).wait_recv()
            passed[j].start()
        copy(0, sibling, me).wait_recv()
        for j, chip in enumerate(chips):
            copy(4 + j, (*chip, 1 - c), me).wait_recv()
        for cp in first + passed:
            cp.wait_send()
        mine.wait()

    return pl.pallas_call(
        body, name="allgather_rows",
        out_shape=jax.ShapeDtypeStruct((N_DEV, r, w), shard.dtype),
        in_specs=[_ANY], out_specs=_ANY,
        scratch_shapes=[pltpu.SemaphoreType.DMA((7,)), pltpu.SemaphoreType.DMA((7,)), pltpu.SemaphoreType.DMA],
    )(shard)


def exchange_sibling(g):
    _, r, w = g.shape

    def body(g_ref, got_ref, send_sems, recv_sems):
        x, y, c = _mesh_pos()
        copies = [pltpu.make_async_remote_copy(
            src_ref=g_ref.at[2 * j + (1 - c)], dst_ref=got_ref.at[j], send_sem=send_sems.at[j], recv_sem=recv_sems.at[j],
            device_id=(x, y, 1 - c), device_id_type=_MESH) for j in range(N_CHIP)]
        for cp in copies:
            cp.start()
        for cp in copies:
            cp.wait()

    return pl.pallas_call(
        body, name="exchange_sibling", out_shape=jax.ShapeDtypeStruct((N_CHIP, r, w), g.dtype),
        in_specs=[_ANY], out_specs=_ANY,
        scratch_shapes=[pltpu.SemaphoreType.DMA((N_CHIP,)), pltpu.SemaphoreType.DMA((N_CHIP,))],
    )(g)


def exchange_chips(part):
    _, r, w = part.shape

    def body(p_ref, land_ref, send_sems, recv_sems):
        x, y, c = _mesh_pos()
        copies = []
        for k in (1, 2, 3):
            tx, ty = _flip(x, k & 2), _flip(y, k & 1)
            copies.append(pltpu.make_async_remote_copy(
                src_ref=p_ref.at[2 * tx + ty], dst_ref=land_ref.at[k - 1],
                send_sem=send_sems.at[k - 1], recv_sem=recv_sems.at[k - 1],
                device_id=(tx, ty, c), device_id_type=_MESH))
        for cp in copies:
            cp.start()
        for cp in copies:
            cp.wait()

    return pl.pallas_call(
        body, name="exchange_chips", out_shape=jax.ShapeDtypeStruct((N_CHIP - 1, r, w), part.dtype),
        in_specs=[_ANY], out_specs=_ANY,
        scratch_shapes=[pltpu.SemaphoreType.DMA((3,)), pltpu.SemaphoreType.DMA((3,))],
    )(part)


def allreduce_stats(st):
    def body(st_ref, out_ref, gath, send_sems, recv_sems):
        x, y, c = _mesh_pos()
        me = 4 * x + 2 * y + c
        gath[me] = st_ref[...]
        copies = []
        for k in range(1, N_DEV):
            peer = (_flip(x, k & 4), _flip(y, k & 2), _flip(c, k & 1))
            copies.append(pltpu.make_async_remote_copy(
                src_ref=st_ref, dst_ref=gath.at[me], send_sem=send_sems.at[k - 1], recv_sem=recv_sems.at[k - 1],
                device_id=peer, device_id_type=_MESH))
        for cp in copies:
            cp.start()
        for cp in copies:
            cp.wait()
        acc = gath[0]
        for d in range(1, N_DEV):
            acc = acc + gath[d]
        out_ref[...] = acc

    vm = pl.BlockSpec(memory_space=pltpu.VMEM)
    return pl.pallas_call(
        body, name="allreduce_stats", out_shape=jax.ShapeDtypeStruct(st.shape, F32),
        in_specs=[vm], out_specs=vm,
        scratch_shapes=[pltpu.VMEM((N_DEV,) + st.shape, F32),
                        pltpu.SemaphoreType.DMA((N_DEV - 1,)), pltpu.SemaphoreType.DMA((N_DEV - 1,))],
    )(st)


def add_pairs(g, got, core):
    n, r, w = got.shape
    tr = 272 if r % 272 == 0 else 16

    def body(c_ref, a_ref, b_ref, o_ref):
        o_ref[...] = (a_ref[...].astype(F32) + b_ref[...].astype(F32)).astype(o_ref.dtype)

    spec = pl.BlockSpec((1, tr, w), lambda i, j, c: (i, j, 0))
    return pl.pallas_call(
        body, name="add_pairs", out_shape=jax.ShapeDtypeStruct(got.shape, got.dtype),
        grid_spec=pltpu.PrefetchScalarGridSpec(
            num_scalar_prefetch=1, grid=(n, r // tr),
            in_specs=[pl.BlockSpec((1, tr, w), lambda i, j, c: (2 * i + c[0], j, 0)), spec], out_specs=spec),
        compiler_params=_params(("parallel", "parallel")),
    )(core, g, got)


def sum_chips(part, land, chip):
    _, r, w = part.shape
    tr = 272 if r % 272 == 0 else 16

    def body(c_ref, p_ref, l_ref, o_ref):
        acc = p_ref[0].astype(F32)
        for s in range(N_CHIP - 1):
            acc = acc + l_ref[s].astype(F32)
        o_ref[...] = acc

    return pl.pallas_call(
        body, name="sum_chips", out_shape=jax.ShapeDtypeStruct((r, w), F32),
        grid_spec=pltpu.PrefetchScalarGridSpec(
            num_scalar_prefetch=1, grid=(r // tr,),
            in_specs=[pl.BlockSpec((1, tr, w), lambda i, c: (c[0], i, 0)), pl.BlockSpec((N_CHIP - 1, tr, w), lambda i, c: (0, i, 0))],
            out_specs=pl.BlockSpec((tr, w), lambda i, c: (i, 0))),
        compiler_params=_params(("parallel",)),
    )(chip, part, land)


def in_proj(x, g_mix, w_in_t, tm):
    t = x.shape[0]
    nc = 512

    def body(x_ref, g_ref, w_ref, z_ref):
        xv = x_ref[...]
        h = (xv * _rstd(xv, D_MODEL) * g_ref[...]).astype(BF16)
        for cidx in range(ZP // nc):
            z_ref[:, cidx * nc:(cidx + 1) * nc] = _dot_nt(h, w_ref[cidx * nc:(cidx + 1) * nc, :])

    return pl.pallas_call(
        body, name="in_proj", grid=(t // tm,), out_shape=jax.ShapeDtypeStruct((t, ZP), F32),
        in_specs=[_rows(tm, D_MODEL), _resident((1, D_MODEL)), _resident((ZP, D_MODEL))],
        out_specs=_rows(tm, ZP), compiler_params=_params(("parallel",)),
    )(x, g_mix, w_in_t)


def attn_prep(zp, tabs, g_qa, g_kva, g_qn, g_kn, w_qb_t, w_kvb_t, tm, s_len):
    t = zp.shape[0]
    nsb = s_len // tm
    scale_a = (QK_NOPE + QK_ROPE) ** -0.5
    scale_b = HD_B ** -0.5

    def body(qb_ref, qlat_ref, kb_ref, vb_ref, ckv_ref, kpe_ref, tab_ref, gqa_ref, gkva_ref, gqn_ref, gkn_ref,
             wqb_ref, wkvb_ref, qa_o, ka_o, va_o, qb_o, kb_o, vb_o, cq_o, ckvn_o):
        ca, s1a, s2a = tab_ref[0], tab_ref[1], tab_ref[2]
        ck = tab_ref[3]
        cb, s1b, s2b = tab_ref[4], tab_ref[5], tab_ref[6]
        ql = qlat_ref[...]
        cq = (ql * _rstd(ql, Q_LORA) * gqa_ref[...]).astype(BF16)
        cq_o[...] = cq
        qa = _dot_nt(cq, wqb_ref[...])
        for h in range(H_A):
            sl = slice(h * HP, (h + 1) * HP)
            qa_o[:, sl] = (_rope_fwd(qa[:, sl], ca, s1a, s2a) * scale_a).astype(BF16)
        cr = ckv_ref[...]
        ckv = (cr * _rstd(cr, KV_LORA) * gkva_ref[...]).astype(BF16)
        ckvn_o[...] = ckv
        kva = _dot_nt(ckv, wkvb_ref[...])
        kpe = _rope_fwd(kpe_ref[...], ck, s1a, s2a)
        for h in range(H_A):
            sl = slice(h * HP, (h + 1) * HP)
            ka_o[:, sl] = (kva[:, sl] + kpe).astype(BF16)
        va_o[...] = kva[:, H_A * HP:].astype(BF16)
        gqn, gkn = gqn_ref[...], gkn_ref[...]
        for h in range(H_B):
            sl = slice(h * HP, (h + 1) * HP)
            xs = qb_ref[:, sl]
            y = xs * _rstd(xs, HD_B) * gqn
            qb_o[:, sl] = (_rope_fwd(y, cb, s1b, s2b) * scale_b).astype(BF16)
        for h in range(KV_B):
            sl = slice(h * HP, (h + 1) * HP)
            xs = kb_ref[:, sl]
            y = xs * _rstd(xs, HD_B) * gkn
            kb_o[:, sl] = _rope_fwd(y, cb, s1b, s2b).astype(BF16)
        vb_o[...] = vb_ref[...].astype(BF16)

    def o(width):
        return jax.ShapeDtypeStruct((t, width), BF16)

    return pl.pallas_call(
        body, name="attn_prep", grid=(t // tm,),
        out_shape=(o(H_A * HP), o(H_A * HP), o(H_A * HP), o(H_B * HP), o(KV_B * HP), o(KV_B * HP), o(Q_LORA), o(KV_LORA)),
        in_specs=[_rows(tm, 1024, 0), _rows(tm, 256, 12), _rows(tm, 256, 13), _rows(tm, 256, 14),
                  _rows(tm, 128, 30), _rows(tm, 128, 31),
                  pl.BlockSpec((7, tm, HP), lambda i: (0, i % nsb, 0)),
                  _resident((1, Q_LORA)), _resident((1, KV_LORA)), _resident((1, HP)), _resident((1, HP)),
                  _resident((H_A * HP, Q_LORA)), _resident((2 * H_A * HP, KV_LORA))],
        out_specs=(_rows(tm, H_A * HP), _rows(tm, H_A * HP), _rows(tm, H_A * HP), _rows(tm, H_B * HP),
                   _rows(tm, KV_B * HP), _rows(tm, KV_B * HP), _rows(tm, Q_LORA), _rows(tm, KV_LORA)),
        compiler_params=_params(("parallel",)),
    )(zp, zp, zp, zp, zp, zp, tabs, g_qa, g_kva, g_qn, g_kn, w_qb_t, w_kvb_t)


def attn_fwd(q, k, v, n_b, s_len, tq, name):
    t = q.shape[0]
    n_h, n_hk = q.shape[1] // HP, k.shape[1] // HP
    grp = n_h // n_hk
    nq = s_len // tq
    sub = min(tq, 256)

    def body(q_ref, k_ref, v_ref, o_ref, lse_ref):
        kv, vv = k_ref[...], v_ref[...]
        for r in range(tq // sub):
            rows = slice(r * sub, (r + 1) * sub)
            s = _dot_nt(q_ref[rows, :], kv)
            m = jnp.max(s, axis=-1, keepdims=True)
            p = jnp.exp(s - m)
            l = jnp.sum(p, axis=-1, keepdims=True)
            o_ref[rows, :] = (_dot_nn(p.astype(BF16), vv) * (1.0 / l)).astype(o_ref.dtype)
            lse_ref[rows, :] = jnp.broadcast_to(m + jnp.log(l), (sub, HP))

    qspec = pl.BlockSpec((tq, HP), lambda b, h, i: (b * nq + i, h))
    kspec = pl.BlockSpec((s_len, HP), lambda b, h, i: (b, h // grp))
    return pl.pallas_call(
        body, name=name, grid=(n_b, n_h, nq),
        out_shape=(jax.ShapeDtypeStruct((t, n_h * HP), BF16), jax.ShapeDtypeStruct((t, n_h * HP), F32)),
        in_specs=[qspec, kspec, kspec], out_specs=(qspec, qspec),
        compiler_params=_params(("parallel", "parallel", "parallel")),
    )(q, k, v)


def merge_fwd(oa, ob, zp, x, w_oa_t, w_ob_t, w_o, tm):
    t = x.shape[0]

    def body(oa_ref, ob_ref, ga_ref, gb_ref, x_ref, woa_ref, wob_ref, wo_ref, x1_o, mg_o):
        ya = _dot_nt(oa_ref[...], woa_ref[...])
        yb = _dot_nt(ob_ref[...], wob_ref[...])
        merged = (jax.nn.sigmoid(ga_ref[...]) * ya + jax.nn.sigmoid(gb_ref[...]) * yb).astype(BF16)
        mg_o[...] = merged
        x1_o[...] = x_ref[...] + _dot_nn(merged, wo_ref[...])

    return pl.pallas_call(
        body, name="merge_fwd", grid=(t // tm,),
        out_shape=(jax.ShapeDtypeStruct((t, D_MODEL), F32), jax.ShapeDtypeStruct((t, D_MODEL), BF16)),
        in_specs=[_rows(tm, H_A * HP), _rows(tm, H_B * HP), _rows(tm, 1024, 1), _rows(tm, 1024, 2), _rows(tm, D_MODEL),
                  _resident((D_MODEL, H_A * HP)), _resident((D_MODEL, H_B * HP)), _resident((D_MODEL, D_MODEL))],
        out_specs=(_rows(tm, D_MODEL), _rows(tm, D_MODEL)), compiler_params=_params(("parallel",)),
    )(oa, ob, zp, zp, x, w_oa_t, w_ob_t, w_o)


def mlp_fwd(x1, g_mlp, w_up_t, w_down, tm):
    t = x1.shape[0]
    fc = 1024

    def body(x_ref, g_ref, wup_ref, wdn_ref, x2_o, u_o):
        xv = x_ref[...]
        h2 = (xv * _rstd(xv, D_MODEL) * g_ref[...]).astype(BF16)
        acc = xv
        for cidx in range(D_FF // fc):
            sl = slice(cidx * fc, (cidx + 1) * fc)
            u = jnp.maximum(_dot_nt(h2, wup_ref[sl, :]), 0.0)
            u_o[:, sl] = u.astype(BF16)
            acc = acc + _dot_nn((u * u).astype(BF16), wdn_ref[sl, :])
        x2_o[...] = acc

    return pl.pallas_call(
        body, name="mlp_fwd", grid=(t // tm,),
        out_shape=(jax.ShapeDtypeStruct((t, D_MODEL), F32), jax.ShapeDtypeStruct((t, D_FF), BF16)),
        in_specs=[_rows(tm, D_MODEL), _resident((1, D_MODEL)), _resident((D_FF, D_MODEL)), _resident((D_FF, D_MODEL))],
        out_specs=(_rows(tm, D_MODEL), _rows(tm, D_FF)), compiler_params=_params(("parallel",)),
    )(x1, g_mlp, w_up_t, w_down)


def ple_loss_bwd(x2, p, tgt, g_ple, g_final, w_pg, w_ple_t, tm):
    t = x2.shape[0]
    inv_d = 1.0 / D_MODEL

    def body(x2_ref, p_ref, tg_ref, gp_ref, gf_ref, wpg_ref, wple_ref, dx2_o, dt_o, h3_o, dpe_o, st_o):
        @pl.when(pl.program_id(0) == 0)
        def _():
            st_o[...] = jnp.zeros_like(st_o)

        x2v = x2_ref[...]
        gp, gf = gp_ref[...], gf_ref[...]
        r2 = _rstd(x2v, D_MODEL)
        xh2 = x2v * r2
        h3 = (xh2 * gp).astype(BF16)
        h3_o[...] = h3
        gate = jax.nn.sigmoid(_dot_nn(h3, wpg_ref[...]))
        pe = _dot_nt(p_ref[...].astype(BF16), wple_ref[...])
        x3 = x2v + gate * pe
        r3 = _rstd(x3, D_MODEL)
        xh3 = x3 * r3
        err = xh3 * gf - tg_ref[...]
        dy = err * inv_d
        dx3 = _rms_bwd(dy, xh3, r3, gf, D_MODEL)
        dpe_o[...] = (dx3 * gate).astype(BF16)
        dt = (dx3 * pe * gate * (1.0 - gate)).astype(BF16)
        dt_o[...] = dt
        dh3 = _dot_nt(dt, wpg_ref[...])
        dx2_o[...] = dx3 + _rms_bwd(dh3, xh2, r2, gp, D_MODEL)
        st_o[0:1, :] += _colsum(dh3 * xh2)
        st_o[1:2, :] += _colsum(dy * xh3)
        st_o[2:3, :] += _colsum(err * err) * (0.5 * inv_d)

    bf = jax.ShapeDtypeStruct((t, D_MODEL), BF16)
    return pl.pallas_call(
        body, name="ple_loss_bwd", grid=(t // tm,),
        out_shape=(jax.ShapeDtypeStruct((t, D_MODEL), F32), bf, bf, bf, jax.ShapeDtypeStruct((3, D_MODEL), F32)),
        in_specs=[_rows(tm, D_MODEL), _rows(tm, PLE_DIM), _rows(tm, D_MODEL), _resident((1, D_MODEL)), _resident((1, D_MODEL)),
                  _resident((D_MODEL, D_MODEL)), _resident((D_MODEL, PLE_DIM))],
        out_specs=(_rows(tm, D_MODEL), _rows(tm, D_MODEL), _rows(tm, D_MODEL), _rows(tm, D_MODEL),
                   pl.BlockSpec((3, D_MODEL), lambda i: (0, 0))),
        compiler_params=_params(("arbitrary",)),
    )(x2, p, tgt, g_ple, g_final, w_pg, w_ple_t)


def mlp_bwd(dx2, x1, u, g_mlp, w_up_t, w_down, tm):
    t = x1.shape[0]
    fc = 1024

    def body(dx2_ref, x1_ref, u_ref, g_ref, wup_ref, wdn_ref, dx1_o, da_o, h2_o, st_o):
        @pl.when(pl.program_id(0) == 0)
        def _():
            st_o[...] = jnp.zeros_like(st_o)

        d2 = dx2_ref[...]
        d2b = d2.astype(BF16)
        dh2 = jnp.zeros((tm, D_MODEL), F32)
        for cidx in range(D_FF // fc):
            sl = slice(cidx * fc, (cidx + 1) * fc)
            da = (_dot_nt(d2b, wdn_ref[sl, :]) * (2.0 * u_ref[:, sl].astype(F32))).astype(BF16)
            da_o[:, sl] = da
            dh2 = dh2 + _dot_nn(da, wup_ref[sl, :])
        xv = x1_ref[...]
        g = g_ref[...]
        r1 = _rstd(xv, D_MODEL)
        xh1 = xv * r1
        h2_o[...] = (xh1 * g).astype(BF16)
        st_o[...] += _colsum(dh2 * xh1)
        dx1_o[...] = d2 + _rms_bwd(dh2, xh1, r1, g, D_MODEL)

    return pl.pallas_call(
        body, name="mlp_bwd", grid=(t // tm,),
        out_shape=(jax.ShapeDtypeStruct((t, D_MODEL), F32), jax.ShapeDtypeStruct((t, D_FF), BF16),
                   jax.ShapeDtypeStruct((t, D_MODEL), BF16), jax.ShapeDtypeStruct((1, D_MODEL), F32)),
        in_specs=[_rows(tm, D_MODEL), _rows(tm, D_MODEL), _rows(tm, D_FF), _resident((1, D_MODEL)),
                  _resident((D_FF, D_MODEL)), _resident((D_FF, D_MODEL))],
        out_specs=(_rows(tm, D_MODEL), _rows(tm, D_FF), _rows(tm, D_MODEL), pl.BlockSpec((1, D_MODEL), lambda i: (0, 0))),
        compiler_params=_params(("arbitrary",)),
    )(dx2, x1, u, g_mlp, w_up_t, w_down)


def merge_bwd(dx1, oa, ob, zp, w_oa_t, w_ob_t, w_o, tm):
    t = dx1.shape[0]

    def body(dx1_ref, oa_ref, ob_ref, ga_ref, gb_ref, woa_ref, wob_ref, wo_ref, doa_o, dob_o, dg_o, dya_o, dyb_o):
        dm = _dot_nt(dx1_ref[...].astype(BF16), wo_ref[...])
        for o_ref, g_ref, w_ref, do_o, dy_o, col in ((oa_ref, ga_ref, woa_ref, doa_o, dya_o, 0),
                                                     (ob_ref, gb_ref, wob_ref, dob_o, dyb_o, 1)):
            yv = _dot_nt(o_ref[...], w_ref[...])
            sg = jax.nn.sigmoid(g_ref[...])
            dyv = (dm * sg).astype(BF16)
            dy_o[...] = dyv
            dg_o[:, col * D_MODEL:(col + 1) * D_MODEL] = (dm * yv * sg * (1.0 - sg)).astype(BF16)
            do_o[...] = _dot_nn(dyv, w_ref[...]).astype(BF16)

    bf = jax.ShapeDtypeStruct((t, D_MODEL), BF16)
    return pl.pallas_call(
        body, name="merge_bwd", grid=(t // tm,),
        out_shape=(bf, bf, jax.ShapeDtypeStruct((t, 2 * D_MODEL), BF16), bf, bf),
        in_specs=[_rows(tm, D_MODEL), _rows(tm, H_A * HP), _rows(tm, H_B * HP), _rows(tm, 1024, 1), _rows(tm, 1024, 2),
                  _resident((D_MODEL, H_A * HP)), _resident((D_MODEL, H_B * HP)), _resident((D_MODEL, D_MODEL))],
        out_specs=(_rows(tm, D_MODEL), _rows(tm, D_MODEL), _rows(tm, 2 * D_MODEL), _rows(tm, D_MODEL), _rows(tm, D_MODEL)),
        compiler_params=_params(("parallel",)),
    )(dx1, oa, ob, zp, zp, w_oa_t, w_ob_t, w_o)


def attn_bwd(q, k, v, do, o, lse, n_b, s_len, tq, name):
    t = q.shape[0]
    n_h, n_hk = q.shape[1] // HP, k.shape[1] // HP
    grp = n_h // n_hk
    nq = s_len // tq
    sub = min(tq, 256)

    def body(q_ref, k_ref, v_ref, do_ref, o_ref, lse_ref, dq_o, dk_o, dv_o, p_s, ds_s):
        @pl.when((pl.program_id(2) == 0) & (pl.program_id(3) == 0))
        def _():
            dk_o[...] = jnp.zeros_like(dk_o)
            dv_o[...] = jnp.zeros_like(dv_o)

        kv, vv = k_ref[...], v_ref[...]
        for r in range(tq // sub):
            rows = slice(r * sub, (r + 1) * sub)
            qv, dov = q_ref[rows, :], do_ref[rows, :]
            delta = jnp.sum(dov.astype(F32) * o_ref[rows, :].astype(F32), axis=-1, keepdims=True)
            p = jnp.exp(_dot_nt(qv, kv) - lse_ref[rows, 0:1])
            ds = (p * (_dot_nt(dov, vv) - delta)).astype(BF16)
            p_s[rows, :] = p.astype(BF16)
            ds_s[rows, :] = ds
            dq_o[rows, :] = _dot_nn(ds, kv)
        dk_o[...] += _dot_tn(ds_s[...], q_ref[...])
        dv_o[...] += _dot_tn(p_s[...], do_ref[...])

    qspec = pl.BlockSpec((tq, HP), lambda b, hk, g, i: (b * nq + i, hk * grp + g))
    kspec = pl.BlockSpec((s_len, HP), lambda b, hk, g, i: (b, hk))
    return pl.pallas_call(
        body, name=name, grid=(n_b, n_hk, grp, nq),
        out_shape=(jax.ShapeDtypeStruct((t, n_h * HP), F32), jax.ShapeDtypeStruct((t, n_hk * HP), F32),
                   jax.ShapeDtypeStruct((t, n_hk * HP), F32)),
        in_specs=[qspec, kspec, kspec, qspec, qspec, qspec], out_specs=(qspec, kspec, kspec),
        scratch_shapes=[pltpu.VMEM((tq, s_len), BF16), pltpu.VMEM((tq, s_len), BF16)],
        compiler_params=_params(("parallel", "parallel", "arbitrary", "arbitrary")),
    )(q, k, v, do, o, lse)


def prep_bwd(dqa, dka, dva, dqb, dkb, dvb, zp, tabs, g_qa, g_kva, g_qn, g_kn, w_qb_t, w_kvb_t, tm, s_len):
    t = zp.shape[0]
    nsb = s_len // tm
    scale_a = (QK_NOPE + QK_ROPE) ** -0.5
    scale_b = HD_B ** -0.5

    def body(dqa_ref, dka_ref, dva_ref, dqb_ref, dkb_ref, dvb_ref, qb_ref, qlat_ref, kb_ref, ckv_ref, tab_ref,
             gqa_ref, gkva_ref, gqn_ref, gkn_ref, wqb_ref, wkvb_ref, dzq_o, dsm_o, dqap_o, dkva_o, st_o):
        @pl.when(pl.program_id(0) == 0)
        def _():
            st_o[...] = jnp.zeros_like(st_o)

        ca, s1a, s2a = tab_ref[0], tab_ref[1], tab_ref[2]
        ck = tab_ref[3]
        cb, s1b, s2b = tab_ref[4], tab_ref[5], tab_ref[6]
        for h in range(H_A):
            sl = slice(h * HP, (h + 1) * HP)
            dqap_o[:, sl] = _rope_bwd(dqa_ref[:, sl] * scale_a, ca, s1a, s2a).astype(BF16)
        dcq = _dot_nn(dqap_o[...], wqb_ref[...])
        ql = qlat_ref[...]
        rq = _rstd(ql, Q_LORA)
        xh = ql * rq
        gqa = gqa_ref[...]
        st_o[0:1, :] += _colsum(dcq * xh)
        dsm_o[:, 0:256] = _rms_bwd(dcq, xh, rq, gqa, Q_LORA).astype(BF16)
        dkpe = jnp.zeros((tm, HP), F32)
        for h in range(H_A):
            sl = slice(h * HP, (h + 1) * HP)
            dk = dka_ref[:, sl]
            dkpe = dkpe + dk
            dkva_o[:, sl] = dk.astype(BF16)
        dkva_o[:, H_A * HP:] = dva_ref[...].astype(BF16)
        dsm_o[:, 896:1024] = _rope_bwd(dkpe, ck, s1a, s2a).astype(BF16)
        dckv = _dot_nn(dkva_o[...], wkvb_ref[...])
        cr = ckv_ref[...]
        rk = _rstd(cr, KV_LORA)
        xh = cr * rk
        st_o[1:2, 0:128] += _colsum(dckv * xh)
        dsm_o[:, 768:896] = _rms_bwd(dckv, xh, rk, gkva_ref[...], KV_LORA).astype(BF16)
        gqn, gkn = gqn_ref[...], gkn_ref[...]
        dgq = jnp.zeros((1, HP), F32)
        for h in range(H_B):
            sl = slice(h * HP, (h + 1) * HP)
            dy = _rope_bwd(dqb_ref[:, sl] * scale_b, cb, s1b, s2b)
            xs = qb_ref[:, sl]
            r = _rstd(xs, HD_B)
            xh = xs * r
            dgq = dgq + _colsum(dy * xh)
            dzq_o[:, sl] = _rms_bwd(dy, xh, r, gqn, HD_B).astype(BF16)
        st_o[2:3, 0:128] += dgq
        dgk = jnp.zeros((1, HP), F32)
        for h in range(KV_B):
            sl = slice(h * HP, (h + 1) * HP)
            dy = _rope_bwd(dkb_ref[:, sl], cb, s1b, s2b)
            xs = kb_ref[:, sl]
            r = _rstd(xs, HD_B)
            xh = xs * r
            dgk = dgk + _colsum(dy * xh)
            dsm_o[:, 256 + h * HP:256 + (h + 1) * HP] = _rms_bwd(dy, xh, r, gkn, HD_B).astype(BF16)
        st_o[3:4, 0:128] += dgk
        dsm_o[:, 512:768] = dvb_ref[...].astype(BF16)

    bf = jax.ShapeDtypeStruct((t, 1024), BF16)
    return pl.pallas_call(
        body, name="prep_bwd", grid=(t // tm,),
        out_shape=(bf, bf, bf, jax.ShapeDtypeStruct((t, 2048), BF16), jax.ShapeDtypeStruct((4, 256), F32)),
        in_specs=[_rows(tm, 1024), _rows(tm, 1024), _rows(tm, 1024), _rows(tm, 1024), _rows(tm, 256), _rows(tm, 256),
                  _rows(tm, 1024, 0), _rows(tm, 256, 12), _rows(tm, 256, 13), _rows(tm, 128, 30),
                  pl.BlockSpec((7, tm, HP), lambda i: (0, i % nsb, 0)),
                  _resident((1, Q_LORA)), _resident((1, KV_LORA)), _resident((1, HP)), _resident((1, HP)),
                  _resident((H_A * HP, Q_LORA)), _resident((2 * H_A * HP, KV_LORA))],
        out_specs=(_rows(tm, 1024), _rows(tm, 1024), _rows(tm, 1024), _rows(tm, 2048), pl.BlockSpec((4, 256), lambda i: (0, 0))),
        compiler_params=_params(("arbitrary",)),
    )(dqa, dka, dva, dqb, dkb, dvb, zp, zp, zp, zp, tabs, g_qa, g_kva, g_qn, g_kn, w_qb_t, w_kvb_t)


def in_bwd(dzq, dgab, dsm, x, dx1, g_mix, w_in_t, tm):
    t = x.shape[0]

    def body(dzq_ref, dg_ref, dsm_ref, x_ref, dx1_ref, g_ref, w_ref, dx_o, h_o, st_o):
        @pl.when(pl.program_id(0) == 0)
        def _():
            st_o[...] = jnp.zeros_like(st_o)

        dh = _dot_nn(dzq_ref[...], w_ref[0:1024, :])
        dh = dh + _dot_nn(dg_ref[...], w_ref[1024:3072, :])
        dh = dh + _dot_nn(dsm_ref[...], w_ref[3072:4096, :])
        xv = x_ref[...]
        g = g_ref[...]
        r = _rstd(xv, D_MODEL)
        xh = xv * r
        h_o[...] = (xh * g).astype(BF16)
        st_o[...] += _colsum(dh * xh)
        dx_o[...] = dx1_ref[...] + _rms_bwd(dh, xh, r, g, D_MODEL)

    return pl.pallas_call(
        body, name="in_bwd", grid=(t // tm,),
        out_shape=(jax.ShapeDtypeStruct((t, D_MODEL), F32), jax.ShapeDtypeStruct((t, D_MODEL), BF16),
                   jax.ShapeDtypeStruct((1, D_MODEL), F32)),
        in_specs=[_rows(tm, 1024), _rows(tm, 2048), _rows(tm, 1024), _rows(tm, D_MODEL), _rows(tm, D_MODEL),
                  _resident((1, D_MODEL)), _resident((ZP, D_MODEL))],
        out_specs=(_rows(tm, D_MODEL), _rows(tm, D_MODEL), pl.BlockSpec((1, D_MODEL), lambda i: (0, 0))),
        compiler_params=_params(("arbitrary",)),
    )(dzq, dgab, dsm, x, dx1, g_mix, w_in_t)


def matmul_tn(a, b, name, square_a=False):
    t, m = a.shape
    n = b.shape[1]
    bm = min(m, 512)
    tk = min(t, 2048)

    def body(a_ref, b_ref, o_ref):
        @pl.when(pl.program_id(1) == 0)
        def _():
            o_ref[...] = jnp.zeros_like(o_ref)

        av = a_ref[...]
        if square_a:
            av = (av.astype(F32) * av.astype(F32))
        o_ref[...] += _dot_tn(av.astype(BF16), b_ref[...].astype(BF16))

    return pl.pallas_call(
        body, name=name, grid=(m // bm, t // tk), out_shape=jax.ShapeDtypeStruct((m, n), F32),
        in_specs=[pl.BlockSpec((tk, bm), lambda i, kk: (kk, i)), pl.BlockSpec((tk, n), lambda i, kk: (kk, 0))],
        out_specs=pl.BlockSpec((bm, n), lambda i, kk: (i, 0)),
        compiler_params=_params(("parallel", "arbitrary")),
    )(a, b)


def adamw(w, g, m, v, name):
    r, c = w.shape
    tr = r if r <= 256 else 256
    c1 = 1.0 - ADAM_B1 ** ADAM_STEP
    c2 = 1.0 - ADAM_B2 ** ADAM_STEP

    def body(w_ref, g_ref, m_ref, v_ref, d_o, m_o, v_o):
        gv = g_ref[...]
        mn = ADAM_B1 * m_ref[...] + (1.0 - ADAM_B1) * gv
        vn = ADAM_B2 * v_ref[...] + (1.0 - ADAM_B2) * (gv * gv)
        m_o[...] = mn
        v_o[...] = vn
        d_o[...] = -ADAM_LR * ((mn / c1) / (jnp.sqrt(vn / c2) + ADAM_EPS) + ADAM_WD * w_ref[...])

    spec = pl.BlockSpec((tr, c), lambda i: (i, 0))
    shp = jax.ShapeDtypeStruct((r, c), F32)
    return pl.pallas_call(
        body, name=name, grid=(r // tr,), out_shape=(shp, shp, shp), in_specs=[spec] * 4, out_specs=(spec,) * 3,
        compiler_params=_params(("parallel",)),
    )(w, g, m, v)


def _rope_tables(s_len):
    def angles(pos, dim):
        inv = np.float32(ROPE_THETA) ** (-np.arange(0, dim, 2, dtype=np.float32) / np.float32(dim))
        return pos.astype(np.float32)[:, None] * inv[None, :]

    tpos = np.arange(s_len)
    a1 = angles(tpos, QK_ROPE)
    ar = angles(tpos // GRID_W, HD_B // 2)
    ac = angles(tpos % GRID_W, HD_B // 2)
    z16 = np.zeros((s_len, 16), np.float32)
    z32 = np.zeros((s_len, 32), np.float32)
    z64 = np.zeros((s_len, 64), np.float32)
    one64 = np.ones((s_len, 64), np.float32)
    c1, s1 = np.cos(a1), np.sin(a1)
    ca = np.concatenate([one64, c1, c1, z32], axis=1)
    ck = np.concatenate([z64, c1, c1, z32], axis=1)
    s1a = np.concatenate([z64, -s1, z16, z32], axis=1)
    s2a = np.concatenate([z64, z16, s1, z32], axis=1)
    cr, sr, cc, sc = np.cos(ar), np.sin(ar), np.cos(ac), np.sin(ac)
    cb = np.concatenate([cr, cr, cc, cc, z64], axis=1)
    s1b = np.concatenate([-sr, z16, -sc, z16, z64], axis=1)
    s2b = np.concatenate([z16, sr, z16, sc, z64], axis=1)
    return jnp.asarray(np.stack([ca, s1a, s2a, ck, cb, s1b, s2b]).astype(np.float32))


def _pad_heads(a, n_heads, axis):
    shp = a.shape
    a = a.reshape(shp[:axis] + (n_heads, shp[axis] // n_heads) + shp[axis + 1:])
    pad = [(0, 0)] * a.ndim
    pad[axis + 1] = (0, HP - a.shape[axis + 1])
    a = jnp.pad(a, pad)
    return a.reshape(shp[:axis] + (n_heads * HP,) + shp[axis + 1:])


def _unpad_heads(a, n_heads, width, axis):
    shp = a.shape
    a = a.reshape(shp[:axis] + (n_heads, HP) + shp[axis + 1:])
    a = lax.slice_in_dim(a, 0, width, axis=axis + 1)
    return a.reshape(shp[:axis] + (n_heads * width,) + shp[axis + 1:])


def _pack_rows(blocks, lead):
    parts = []
    for name, rows in PACK:
        b = blocks[name]
        padr = rows - b.shape[-2]
        if padr:
            b = jnp.pad(b, [(0, 0)] * (b.ndim - 2) + [(0, padr), (0, 0)])
        parts.append(b)
    return jnp.concatenate(parts, axis=len(lead))


def _expand_w_in(wt):
    z64 = jnp.zeros((64, D_MODEL), wt.dtype)
    z32 = jnp.zeros((32, D_MODEL), wt.dtype)
    return jnp.concatenate([
        _pad_heads(wt[416:928], H_B, 0), wt[1184:2208], wt[2208:3232], wt[0:256],
        _pad_heads(wt[928:1056], KV_B, 0), _pad_heads(wt[1056:1184], KV_B, 0), wt[256:384],
        z64, wt[384:416], z32], axis=0)


def _collapse_w_in(dq, dg, ds):
    return jnp.concatenate([
        ds[0:256], ds[768:896], ds[960:992], _unpad_heads(dq, H_B, HD_B, 0), _unpad_heads(ds[256:512], KV_B, HD_B, 0),
        _unpad_heads(ds[512:768], KV_B, HD_B, 0), dg], axis=0)


def kernel(x, p, g_mix, w_in, g_qa, w_qb, g_kva, w_kvb, g_qn, g_kn, w_oa, w_ob, w_o, g_mlp, w_up, w_down, g_ple, w_ple_gate, w_ple, g_final, loss_target, m_g_mix, m_w_in, m_g_qa, m_w_qb, m_g_kva, m_w_kvb, m_g_qn, m_g_kn, m_w_oa, m_w_ob, m_w_o, m_g_mlp, m_w_up, m_w_down, m_g_ple, m_w_ple_gate, m_w_ple, m_g_final, v_g_mix, v_w_in, v_g_qa, v_w_qb, v_g_kva, v_w_kvb, v_g_qn, v_g_kn, v_w_oa, v_w_ob, v_w_o, v_g_mlp, v_w_up, v_w_down, v_g_ple, v_w_ple_gate, v_w_ple, v_g_final):
    n_b, s_len, _ = x.shape
    t = n_b * s_len
    tm = min(256, s_len)
    tq_f = min(512, s_len)
    tq_b = min(512, s_len)

    mats = dict(w_in=(w_in, m_w_in, v_w_in), w_qb=(w_qb, m_w_qb, v_w_qb), w_kvb=(w_kvb, m_w_kvb, v_w_kvb),
                w_oa=(w_oa, m_w_oa, v_w_oa), w_ob=(w_ob, m_w_ob, v_w_ob), w_o=(w_o, m_w_o, v_w_o),
                w_up=(w_up, m_w_up, v_w_up), w_down=(w_down, m_w_down, v_w_down),
                w_ple_gate=(w_ple_gate, m_w_ple_gate, v_w_ple_gate), w_ple=(w_ple, m_w_ple, v_w_ple))
    col_sharded = ("w_in", "w_qb", "w_kvb", "w_oa", "w_ob", "w_up", "w_ple")

    blocks = {}
    for name, (w, _, _) in mats.items():
        w2 = w[0]
        if name in col_sharded:
            w2 = w2.T
        blocks[name] = w2.reshape(-1, D_MODEL).astype(BF16)
    full = allgather_rows(_pack_rows(blocks, ()))

    def gathered(name, rows, width):
        off, _ = PACK_OFF[name]
        return full[:, off:off + rows].reshape(-1, width)

    w_in_t = _expand_w_in(gathered("w_in", 404, D_MODEL))
    w_qb_t = _pad_heads(gathered("w_qb", 24, Q_LORA), H_A, 0)
    wkvb = gathered("w_kvb", 16, KV_LORA).reshape(H_A, 2, 64, KV_LORA)
    w_kvb_t = jnp.concatenate([_pad_heads(wkvb[:, 0].reshape(-1, KV_LORA), H_A, 0),
                               _pad_heads(wkvb[:, 1].reshape(-1, KV_LORA), H_A, 0)], axis=0)
    w_oa_t = _pad_heads(gathered("w_oa", 64, H_A * V_DIM_A), H_A, 1)
    w_ob_t = _pad_heads(gathered("w_ob", 64, H_B * HD_B), H_B, 1)
    w_o_f = gathered("w_o", 128, D_MODEL)
    w_up_t = gathered("w_up", 512, D_MODEL)
    w_down_f = gathered("w_down", 512, D_MODEL)
    w_pg_f = gathered("w_ple_gate", 128, D_MODEL)
    w_ple_t = gathered("w_ple", 32, PLE_DIM)

    tabs = _rope_tables(s_len)
    g_qn_p = jnp.pad(g_qn, ((0, 0), (0, HP - HD_B)))
    g_kn_p = jnp.pad(g_kn, ((0, 0), (0, HP - HD_B)))
    xf = x.reshape(t, D_MODEL)
    pf = p.reshape(t, PLE_DIM)
    tgt = loss_target.reshape(t, D_MODEL)

    zp = in_proj(xf, g_mix, w_in_t, tm)
    qa, ka, va, qb, kb, vb, cq, ckv = attn_prep(zp, tabs, g_qa, g_kva, g_qn_p, g_kn_p, w_qb_t, w_kvb_t, tm, s_len)
    oa, lse_a = attn_fwd(qa, ka, va, n_b, s_len, tq_f, "attn_a_fwd")
    ob, lse_b = attn_fwd(qb, kb, vb, n_b, s_len, tq_f, "attn_b_fwd")
    x1, merged = merge_fwd(oa, ob, zp, xf, w_oa_t, w_ob_t, w_o_f, tm)
    x2, u = mlp_fwd(x1, g_mlp, w_up_t, w_down_f, tm)
    dx2, dt, h3, dpe, st_ple = ple_loss_bwd(x2, pf, tgt, g_ple, g_final.reshape(1, D_MODEL), w_pg_f, w_ple_t, tm)
    dx1, da, h2, st_mlp = mlp_bwd(dx2, x1, u, g_mlp, w_up_t, w_down_f, tm)
    doa, dob, dgab, dya, dyb = merge_bwd(dx1, oa, ob, zp, w_oa_t, w_ob_t, w_o_f, tm)
    dqa, dka, dva = attn_bwd(qa, ka, va, doa, oa, lse_a, n_b, s_len, tq_b, "attn_a_bwd")
    dqb, dkb, dvb = attn_bwd(qb, kb, vb, dob, ob, lse_b, n_b, s_len, tq_b, "attn_b_bwd")
    dzq, dsm, dqap, dkva, st_prep = prep_bwd(dqa, dka, dva, dqb, dkb, dvb, zp, tabs, g_qa, g_kva, g_qn_p, g_kn_p,
                                             w_qb_t, w_kvb_t, tm, s_len)
    grad_x, h, st_mix = in_bwd(dzq, dgab, dsm, xf, dx1, g_mix, w_in_t, tm)

    gw_in = _collapse_w_in(matmul_tn(dzq, h, "gw_in_q"), matmul_tn(dgab, h, "gw_in_g"), matmul_tn(dsm, h, "gw_in_s"))
    gw_qb = _unpad_heads(matmul_tn(dqap, cq, "gw_qb"), H_A, QK_NOPE + QK_ROPE, 0)
    gkv = matmul_tn(dkva, ckv, "gw_kvb")
    gw_kvb = jnp.stack([_unpad_heads(gkv[:H_A * HP], H_A, 64, 0).reshape(H_A, 64, KV_LORA),
                        _unpad_heads(gkv[H_A * HP:], H_A, 64, 0).reshape(H_A, 64, KV_LORA)], axis=1)
    gw_oa = _unpad_heads(matmul_tn(dya, oa, "gw_oa"), H_A, V_DIM_A, 1)
    gw_ob = _unpad_heads(matmul_tn(dyb, ob, "gw_ob"), H_B, HD_B, 1)
    gw_o = matmul_tn(merged, dx1, "gw_o")
    gw_up = matmul_tn(da, h2, "gw_up")
    gw_down = matmul_tn(u, dx2, "gw_down", square_a=True)
    gw_pg = matmul_tn(h3, dt, "gw_pg")
    gw_ple = matmul_tn(dpe, pf, "gw_ple")
    gblocks = dict(w_in=gw_in, w_qb=gw_qb, w_kvb=gw_kvb, w_oa=gw_oa, w_ob=gw_ob, w_o=gw_o, w_up=gw_up,
                   w_down=gw_down, w_ple_gate=gw_pg, w_ple=gw_ple)
    gblocks = {n: a.reshape(N_DEV, -1, D_MODEL).astype(BF16) for n, a in gblocks.items()}
    gpack = _pack_rows(gblocks, (N_DEV,))

    core = lax.axis_index("c").astype(jnp.int32).reshape(1)
    chip = (2 * lax.axis_index("x") + lax.axis_index("y")).astype(jnp.int32).reshape(1)
    part = add_pairs(gpack, exchange_sibling(gpack), core)
    gshard = sum_chips(part, exchange_chips(part), chip)

    def row(a, width):
        return jnp.pad(a, ((0, 0), (0, D_MODEL - width)))

    def tile8(a, width):
        return jnp.pad(a, ((0, 7), (0, D_MODEL - width)))

    stats = jnp.concatenate([
        tile8(st_mix, D_MODEL), tile8(st_prep[0:1], 256), tile8(st_prep[1:2], 256), tile8(st_prep[2:3], 256),
        tile8(st_prep[3:4], 256), tile8(st_mlp, D_MODEL), tile8(st_ple[0:1], D_MODEL), tile8(st_ple[1:2], D_MODEL),
        tile8(st_ple[2:3], D_MODEL)], axis=0)
    stats = allreduce_stats(stats)[::8]
    loss = jnp.sum(stats[ST_LOSS])
    stats = jnp.pad(stats, ((0, ST_ROWS - 9), (0, 0)))

    out_g, out_d, out_m, out_v = {}, {}, {}, {}
    for name, (w, m, v) in mats.items():
        off, _ = PACK_OFF[name]
        shard_shape = w.shape[1:]
        if name in col_sharded:
            rows_t = shard_shape[1]
            g2 = gshard[off:off + (rows_t * shard_shape[0]) // D_MODEL].reshape(rows_t, shard_shape[0]).T
        else:
            g2 = gshard[off:off + shard_shape[0]]
        d2, m2, v2 = adamw(w[0], g2, m[0], v[0], "adamw_" + name)
        out_g[name], out_d[name], out_m[name], out_v[name] = g2[None], d2[None], m2[None], v2[None]

    gains = (("g_mix", g_mix, m_g_mix, v_g_mix, ST_G_MIX, D_MODEL), ("g_qa", g_qa, m_g_qa, v_g_qa, ST_G_QA, Q_LORA),
             ("g_kva", g_kva, m_g_kva, v_g_kva, ST_G_KVA, KV_LORA), ("g_qn", g_qn, m_g_qn, v_g_qn, ST_G_QN, HD_B),
             ("g_kn", g_kn, m_g_kn, v_g_kn, ST_G_KN, HD_B), ("g_mlp", g_mlp, m_g_mlp, v_g_mlp, ST_G_MLP, D_MODEL),
             ("g_ple", g_ple, m_g_ple, v_g_ple, ST_G_PLE, D_MODEL),
             ("g_final", g_final, m_g_final, v_g_final, ST_G_FINAL, D_MODEL))

    def gain_pack(idx):
        rows_ = [row(gn[idx].reshape(1, -1), gn[5]) for gn in gains]
        return jnp.concatenate(rows_ + [jnp.zeros((ST_ROWS - len(gains), D_MODEL), F32)], axis=0)

    gd, gm, gv = adamw(gain_pack(1), stats, gain_pack(2), gain_pack(3), "adamw_gains")
    for name, w, _, _, r_, width in gains:
        for dst, src in ((out_g, stats), (out_d, gd), (out_m, gm), (out_v, gv)):
            dst[name] = src[r_, :width].reshape(w.shape)

    order = ("g_mix", "w_in", "g_qa", "w_qb", "g_kva", "w_kvb", "g_qn", "g_kn", "w_oa", "w_ob", "w_o", "g_mlp",
             "w_up", "w_down", "g_ple", "w_ple_gate", "w_ple", "g_final")
    return (loss, grad_x.reshape(x.shape), *[out_g[n] for n in order], *[out_d[n] for n in order],
            *[out_m[n] for n in order], *[out_v[n] for n in order])
```

```python
import numpy as np
import jax
import jax.numpy as jnp
from jax import lax
from jax.experimental import pallas as pl
from jax.experimental.pallas import tpu as pltpu

F32 = jnp.float32
BF16 = jnp.bfloat16

D_MODEL = 1024
EPS = 1e-6
ROPE_THETA = 10000.0
GRID_W = 64
H_A = 8
QK_NOPE = 64
QK_ROPE = 32
V_DIM_A = 64
Q_LORA = 256
KV_LORA = 128
H_B = 8
KV_B = 2
HD_B = 64
D_FF = 4 * D_MODEL
PLE_DIM = 256
HP = 128
ZP = 4096
N_DEV = 8
N_CHIP = 4

ADAM_LR = 0.001
ADAM_B1 = 0.9
ADAM_B2 = 0.999
ADAM_EPS = 1e-08
ADAM_WD = 0.01
ADAM_STEP = 10

VMEM_LIMIT = 52 * 1024 * 1024

PACK_ROWS = dict(w_in=416, w_qb=32, w_kvb=16, w_oa=64, w_ob=64, w_o=128, w_up=512, w_down=512, w_ple_gate=128, w_ple=32)
PACK_W1 = ("w_in", "w_qb", "w_kvb")
PACK_W2 = ("w_up", "w_down", "w_o", "w_ple_gate", "w_oa", "w_ob", "w_ple")
PACK_G1 = ("w_up", "w_down", "w_ple_gate", "w_ple")
PACK_G2 = ("w_in", "w_qb", "w_kvb", "w_o", "w_oa", "w_ob")


def _pack_offsets(names):
    off, o = {}, 0
    for n in names:
        off[n] = o
        o += PACK_ROWS[n]
    return off, o

ST_G_MIX, ST_G_QA, ST_G_KVA, ST_G_QN, ST_G_KN, ST_G_MLP, ST_G_PLE, ST_G_FINAL, ST_LOSS = range(9)
ST_ROWS = 16


def _dot_nn(a, b):
    return lax.dot_general(a, b, (((1,), (0,)), ((), ())), preferred_element_type=F32)


def _dot_nt(a, b):
    return lax.dot_general(a, b, (((1,), (1,)), ((), ())), preferred_element_type=F32)


def _dot_tn(a, b):
    return lax.dot_general(a, b, (((0,), (0,)), ((), ())), preferred_element_type=F32)


def _rstd(x, n):
    return lax.rsqrt(jnp.sum(x * x, axis=-1, keepdims=True) * (1.0 / n) + EPS)


def _rms_bwd(dy, xh, r, g, n):
    dxh = dy * g
    return r * (dxh - xh * (jnp.sum(dxh * xh, axis=-1, keepdims=True) * (1.0 / n)))


def _rope_fwd(x, c, s1, s2):
    return x * c + pltpu.roll(x, HP - 16, 1) * s1 + pltpu.roll(x, 16, 1) * s2


def _rope_bwd(d, c, s1, s2):
    return d * c + pltpu.roll(d * s1, 16, 1) + pltpu.roll(d * s2, HP - 16, 1)


def _colsum(v):
    return jnp.sum(v, axis=0, keepdims=True)


def _params(sem=None, vmem=VMEM_LIMIT):
    return pltpu.CompilerParams(dimension_semantics=sem, vmem_limit_bytes=vmem)


def _resident(shape):
    nd = len(shape)
    return pl.BlockSpec(shape, lambda *_: (0,) * nd, pipeline_mode=pl.Buffered(1))


def _rows(tm, width, col=0):
    return pl.BlockSpec((tm, width), lambda i: (i, col))


def _mesh_pos():
    return lax.axis_index("x"), lax.axis_index("y"), lax.axis_index("c")


def _flip(v, bit):
    return (1 - v) if bit else v


_ANY = pl.BlockSpec(memory_space=pl.ANY)
_MESH = pl.DeviceIdType.MESH


def _remote(src, dst, send_sems, recv_sems, k, to):
    return pltpu.make_async_remote_copy(src_ref=src, dst_ref=dst, send_sem=send_sems.at[k], recv_sem=recv_sems.at[k],
                                        device_id=to, device_id_type=_MESH)


def _sibling_copies(g_ref, got_ref, send_sems, recv_sems):
    x, y, c = _mesh_pos()
    return [_remote(g_ref.at[2 * j + (1 - c)], got_ref.at[j], send_sems, recv_sems, j, (x, y, 1 - c)) for j in range(N_CHIP)]


def _chip_copies(p_ref, land_ref, send_sems, recv_sems):
    x, y, c = _mesh_pos()
    copies = []
    for k in (1, 2, 3):
        tx, ty = _flip(x, k & 2), _flip(y, k & 1)
        copies.append(_remote(p_ref.at[2 * tx + ty], land_ref.at[k - 1], send_sems, recv_sems, k - 1, (tx, ty, c)))
    return copies


class _Comm:
    def __init__(self, ins, out_shapes, sems, make, aliases=None):
        self.ins, self.out_shapes, self.sems, self.make, self.aliases = list(ins), list(out_shapes), list(sems), make, aliases or {}


def _comm_parts(comm, n_in, n_out):
    if comm is None:
        return [], [], [], [], {}
    alias = {n_in + j: n_out + k for j, k in comm.aliases.items()}
    return comm.ins, [_ANY] * len(comm.ins), comm.out_shapes, comm.sems, alias


def _split_refs(refs, n_in, n_out, n_scratch, comm):
    n_ci = len(comm.ins) if comm else 0
    n_co = len(comm.out_shapes) if comm else 0
    cuts, i = [], 0
    for n in (n_in, n_ci, n_out, n_co, n_scratch):
        cuts.append(refs[i:i + n])
        i += n
    return (*cuts, refs[i:])


def _grid_edge(grid, last):
    cond = None
    for d, n in enumerate(grid):
        here = pl.program_id(d) == (n - 1 if last else 0)
        cond = here if cond is None else cond & here
    return cond


def _comm_start(comm, cin, cout, csem, grid):
    if comm is not None:
        @pl.when(_grid_edge(grid, False))
        def _():
            for cp in comm.make(cin, cout, csem):
                cp.start()


def _comm_finish(comm, cin, cout, csem, grid):
    if comm is not None:
        @pl.when(_grid_edge(grid, True))
        def _():
            for cp in comm.make(cin, cout, csem):
                cp.wait()


def gather_first_comm(shard):
    r, w = shard.shape

    def make(cin, cout, sems):
        (x_ref,), (out_ref,), (send_sems, recv_sems, local_sem) = cin, cout, sems
        x, y, c = _mesh_pos()
        mine = out_ref.at[4 * x + 2 * y + c]
        targets = [(x, y, 1 - c), (1 - x, y, c), (x, 1 - y, c), (1 - x, 1 - y, c)]
        return [_remote(x_ref, mine, send_sems, recv_sems, k, to) for k, to in enumerate(targets)] + [
            pltpu.make_async_copy(x_ref, mine, local_sem)]

    return _Comm([shard], [jax.ShapeDtypeStruct((N_DEV, r, w), shard.dtype)],
                 [pltpu.SemaphoreType.DMA((4,)), pltpu.SemaphoreType.DMA((4,)), pltpu.SemaphoreType.DMA], make)


def gather_pass_comm(full):
    def make(cin, cout, sems):
        (in_ref,), (out_ref,), (send_sems, recv_sems) = cin, cout, sems
        x, y, c = _mesh_pos()
        copies = []
        for k, (px, py) in enumerate([(1 - x, y), (x, 1 - y), (1 - x, 1 - y)]):
            idx = 4 * px + 2 * py + c
            copies.append(_remote(in_ref.at[idx], out_ref.at[idx], send_sems, recv_sems, k, (x, y, 1 - c)))
        return copies

    return _Comm([full], [jax.ShapeDtypeStruct(full.shape, full.dtype)],
                 [pltpu.SemaphoreType.DMA((3,)), pltpu.SemaphoreType.DMA((3,))], make, aliases={0: 0})


def scatter_sibling_comm(g):
    _, r, w = g.shape
    return _Comm([g], [jax.ShapeDtypeStruct((N_CHIP, r, w), g.dtype)],
                 [pltpu.SemaphoreType.DMA((N_CHIP,)), pltpu.SemaphoreType.DMA((N_CHIP,))],
                 lambda cin, cout, sems: _sibling_copies(cin[0], cout[0], sems[0], sems[1]))


def scatter_chips_comm(part):
    _, r, w = part.shape
    return _Comm([part], [jax.ShapeDtypeStruct((N_CHIP - 1, r, w), part.dtype)],
                 [pltpu.SemaphoreType.DMA((3,)), pltpu.SemaphoreType.DMA((3,))],
                 lambda cin, cout, sems: _chip_copies(cin[0], cout[0], sems[0], sems[1]))


def allgather_rows(shard):
    r, w = shard.shape

    def body(x_ref, out_ref, send_sems, recv_sems, local_sem):
        x, y, c = _mesh_pos()
        me, sibling = (x, y, c), (x, y, 1 - c)
        chips = [(1 - x, y), (x, 1 - y), (1 - x, 1 - y)]

        def slot(px, py, pc):
            return out_ref.at[4 * px + 2 * py + pc]

        def copy(k, block, to, src=None):
            return pltpu.make_async_remote_copy(
                src_ref=slot(*block) if src is None else src, dst_ref=slot(*block),
                send_sem=send_sems.at[k], recv_sem=recv_sems.at[k], device_id=to, device_id_type=_MESH)

        mine = pltpu.make_async_copy(x_ref, slot(*me), local_sem)
        mine.start()
        first = [copy(0, me, sibling, src=x_ref)]
        first += [copy(1 + j, me, (*chip, c), src=x_ref) for j, chip in enumerate(chips)]
        for cp in first:
            cp.start()
        passed = [copy(4 + j, (*chip, c), sibling) for j, chip in enumerate(chips)]
        for j, chip in enumerate(chips):
            copy(1 + j, (*chip, c), me).wait_recv()
            passed[j].start()
        copy(0, sibling, me).wait_recv()
        for j, chip in enumerate(chips):
            copy(4 + j, (*chip, 1 - c), me).wait_recv()
        for cp in first + passed:
            cp.wait_send()
        mine.wait()

    return pl.pallas_call(
        body, name="allgather_rows",
        out_shape=jax.ShapeDtypeStruct((N_DEV, r, w), shard.dtype),
        in_specs=[_ANY], out_specs=_ANY,
        scratch_shapes=[pltpu.SemaphoreType.DMA((7,)), pltpu.SemaphoreType.DMA((7,)), pltpu.SemaphoreType.DMA],
    )(shard)


def exchange_sibling(g):
    _, r, w = g.shape

    def body(g_ref, got_ref, send_sems, recv_sems):
        copies = _sibling_copies(g_ref, got_ref, send_sems, recv_sems)
        for cp in copies:
            cp.start()
        for cp in copies:
            cp.wait()

    return pl.pallas_call(
        body, name="exchange_sibling", out_shape=jax.ShapeDtypeStruct((N_CHIP, r, w), g.dtype),
        in_specs=[_ANY], out_specs=_ANY,
        scratch_shapes=[pltpu.SemaphoreType.DMA((N_CHIP,)), pltpu.SemaphoreType.DMA((N_CHIP,))],
    )(g)


def exchange_chips(part):
    _, r, w = part.shape

    def body(p_ref, land_ref, send_sems, recv_sems):
        copies = _chip_copies(p_ref, land_ref, send_sems, recv_sems)
        for cp in copies:
            cp.start()
        for cp in copies:
            cp.wait()

    return pl.pallas_call(
        body, name="exchange_chips", out_shape=jax.ShapeDtypeStruct((N_CHIP - 1, r, w), part.dtype),
        in_specs=[_ANY], out_specs=_ANY,
        scratch_shapes=[pltpu.SemaphoreType.DMA((3,)), pltpu.SemaphoreType.DMA((3,))],
    )(part)


def allreduce_stats(st):
    def body(st_ref, out_ref, gath, send_sems, recv_sems):
        x, y, c = _mesh_pos()
        me = 4 * x + 2 * y + c
        gath[me] = st_ref[...]
        copies = []
        for k in range(1, N_DEV):
            peer = (_flip(x, k & 4), _flip(y, k & 2), _flip(c, k & 1))
            copies.append(pltpu.make_async_remote_copy(
                src_ref=st_ref, dst_ref=gath.at[me], send_sem=send_sems.at[k - 1], recv_sem=recv_sems.at[k - 1],
                device_id=peer, device_id_type=_MESH))
        for cp in copies:
            cp.start()
        for cp in copies:
            cp.wait()
        acc = gath[0]
        for d in range(1, N_DEV):
            acc = acc + gath[d]
        out_ref[...] = acc

    vm = pl.BlockSpec(memory_space=pltpu.VMEM)
    return pl.pallas_call(
        body, name="allreduce_stats", out_shape=jax.ShapeDtypeStruct(st.shape, F32),
        in_specs=[vm], out_specs=vm,
        scratch_shapes=[pltpu.VMEM((N_DEV,) + st.shape, F32),
                        pltpu.SemaphoreType.DMA((N_DEV - 1,)), pltpu.SemaphoreType.DMA((N_DEV - 1,))],
    )(st)


def _row_tile(r, cap=640):
    return max(d for d in range(16, min(r, cap) + 1, 16) if r % d == 0)


def add_pairs(g, got, core):
    n, r, w = got.shape
    tr = _row_tile(r)

    def body(c_ref, a_ref, b_ref, o_ref):
        o_ref[...] = (a_ref[...].astype(F32) + b_ref[...].astype(F32)).astype(o_ref.dtype)

    spec = pl.BlockSpec((1, tr, w), lambda i, j, c: (i, j, 0))
    return pl.pallas_call(
        body, name="add_pairs", out_shape=jax.ShapeDtypeStruct(got.shape, got.dtype),
        grid_spec=pltpu.PrefetchScalarGridSpec(
            num_scalar_prefetch=1, grid=(n, r // tr),
            in_specs=[pl.BlockSpec((1, tr, w), lambda i, j, c: (2 * i + c[0], j, 0)), spec], out_specs=spec),
        compiler_params=_params(("parallel", "parallel")),
    )(core, g, got)


def sum_chips(part, land, chip):
    _, r, w = part.shape
    tr = _row_tile(r)

    def body(c_ref, p_ref, l_ref, o_ref):
        acc = p_ref[0].astype(F32)
        for s in range(N_CHIP - 1):
            acc = acc + l_ref[s].astype(F32)
        o_ref[...] = acc

    return pl.pallas_call(
        body, name="sum_chips", out_shape=jax.ShapeDtypeStruct((r, w), F32),
        grid_spec=pltpu.PrefetchScalarGridSpec(
            num_scalar_prefetch=1, grid=(r // tr,),
            in_specs=[pl.BlockSpec((1, tr, w), lambda i, c: (c[0], i, 0)), pl.BlockSpec((N_CHIP - 1, tr, w), lambda i, c: (0, i, 0))],
            out_specs=pl.BlockSpec((tr, w), lambda i, c: (i, 0))),
        compiler_params=_params(("parallel",)),
    )(chip, part, land)


def in_proj(x, g_mix, w_in_t, tm):
    t = x.shape[0]
    nc = 512

    def body(x_ref, g_ref, w_ref, z_ref):
        xv = x_ref[...]
        h = (xv * _rstd(xv, D_MODEL) * g_ref[...]).astype(BF16)
        for cidx in range(ZP // nc):
            z_ref[:, cidx * nc:(cidx + 1) * nc] = _dot_nt(h, w_ref[cidx * nc:(cidx + 1) * nc, :])

    return pl.pallas_call(
        body, name="in_proj", grid=(t // tm,), out_shape=jax.ShapeDtypeStruct((t, ZP), F32),
        in_specs=[_rows(tm, D_MODEL), _resident((1, D_MODEL)), _resident((ZP, D_MODEL))],
        out_specs=_rows(tm, ZP), compiler_params=_params(("parallel",)),
    )(x, g_mix, w_in_t)


def attn_prep(zp, tabs, g_qa, g_kva, g_qn, g_kn, w_qb_t, w_kvb_t, tm, s_len):
    t = zp.shape[0]
    nsb = s_len // tm
    scale_a = (QK_NOPE + QK_ROPE) ** -0.5
    scale_b = HD_B ** -0.5

    def body(qb_ref, qlat_ref, kb_ref, vb_ref, ckv_ref, kpe_ref, tab_ref, gqa_ref, gkva_ref, gqn_ref, gkn_ref,
             wqb_ref, wkvb_ref, qa_o, ka_o, va_o, qb_o, kb_o, vb_o, cq_o, ckvn_o):
        ca, s1a, s2a = tab_ref[0], tab_ref[1], tab_ref[2]
        ck = tab_ref[3]
        cb, s1b, s2b = tab_ref[4], tab_ref[5], tab_ref[6]
        ql = qlat_ref[...]
        cq = (ql * _rstd(ql, Q_LORA) * gqa_ref[...]).astype(BF16)
        cq_o[...] = cq
        qa = _dot_nt(cq, wqb_ref[...])
        for h in range(H_A):
            sl = slice(h * HP, (h + 1) * HP)
            qa_o[:, sl] = (_rope_fwd(qa[:, sl], ca, s1a, s2a) * scale_a).astype(BF16)
        cr = ckv_ref[...]
        ckv = (cr * _rstd(cr, KV_LORA) * gkva_ref[...]).astype(BF16)
        ckvn_o[...] = ckv
        kva = _dot_nt(ckv, wkvb_ref[...])
        kpe = _rope_fwd(kpe_ref[...], ck, s1a, s2a)
        for h in range(H_A):
            sl = slice(h * HP, (h + 1) * HP)
            ka_o[:, sl] = (kva[:, sl] + kpe).astype(BF16)
        va_o[...] = kva[:, H_A * HP:].astype(BF16)
        gqn, gkn = gqn_ref[...], gkn_ref[...]
        for h in range(H_B):
            sl = slice(h * HP, (h + 1) * HP)
            xs = qb_ref[:, sl]
            y = xs * _rstd(xs, HD_B) * gqn
            qb_o[:, sl] = (_rope_fwd(y, cb, s1b, s2b) * scale_b).astype(BF16)
        for h in range(KV_B):
            sl = slice(h * HP, (h + 1) * HP)
            xs = kb_ref[:, sl]
            y = xs * _rstd(xs, HD_B) * gkn
            kb_o[:, sl] = _rope_fwd(y, cb, s1b, s2b).astype(BF16)
        vb_o[...] = vb_ref[...].astype(BF16)

    def o(width):
        return jax.ShapeDtypeStruct((t, width), BF16)

    return pl.pallas_call(
        body, name="attn_prep", grid=(t // tm,),
        out_shape=(o(H_A * HP), o(H_A * HP), o(H_A * HP), o(H_B * HP), o(KV_B * HP), o(KV_B * HP), o(Q_LORA), o(KV_LORA)),
        in_specs=[_rows(tm, 1024, 0), _rows(tm, 256, 12), _rows(tm, 256, 13), _rows(tm, 256, 14),
                  _rows(tm, 128, 30), _rows(tm, 128, 31),
                  pl.BlockSpec((7, tm, HP), lambda i: (0, i % nsb, 0)),
                  _resident((1, Q_LORA)), _resident((1, KV_LORA)), _resident((1, HP)), _resident((1, HP)),
                  _resident((H_A * HP, Q_LORA)), _resident((2 * H_A * HP, KV_LORA))],
        out_specs=(_rows(tm, H_A * HP), _rows(tm, H_A * HP), _rows(tm, H_A * HP), _rows(tm, H_B * HP),
                   _rows(tm, KV_B * HP), _rows(tm, KV_B * HP), _rows(tm, Q_LORA), _rows(tm, KV_LORA)),
        compiler_params=_params(("parallel",)),
    )(zp, zp, zp, zp, zp, zp, tabs, g_qa, g_kva, g_qn, g_kn, w_qb_t, w_kvb_t)


def attn_fwd(q, k, v, n_b, s_len, tq, name, comm=None):
    t = q.shape[0]
    n_h, n_hk = q.shape[1] // HP, k.shape[1] // HP
    grp = n_h // n_hk
    nq = s_len // tq
    sub = min(tq, 256)
    grid = (n_b, n_h, nq)
    c_ins, c_in_specs, c_outs, c_sems, alias = _comm_parts(comm, 3, 2)

    def body(*refs):
        (q_ref, k_ref, v_ref), cin, (o_ref, lse_ref), cout, _, csem = _split_refs(refs, 3, 2, 0, comm)
        _comm_start(comm, cin, cout, csem, grid)
        kv, vv = k_ref[...], v_ref[...]
        for r in range(tq // sub):
            rows = slice(r * sub, (r + 1) * sub)
            s = _dot_nt(q_ref[rows, :], kv)
            m = jnp.max(s, axis=-1, keepdims=True)
            p = jnp.exp(s - m)
            l = jnp.sum(p, axis=-1, keepdims=True)
            o_ref[rows, :] = (_dot_nn(p.astype(BF16), vv) * (1.0 / l)).astype(o_ref.dtype)
            lse_ref[rows, :] = jnp.broadcast_to(m + jnp.log(l), (sub, HP))
        _comm_finish(comm, cin, cout, csem, grid)

    qspec = pl.BlockSpec((tq, HP), lambda b, h, i: (b * nq + i, h))
    kspec = pl.BlockSpec((s_len, HP), lambda b, h, i: (b, h // grp))
    return pl.pallas_call(
        body, name=name, grid=grid,
        out_shape=(jax.ShapeDtypeStruct((t, n_h * HP), BF16), jax.ShapeDtypeStruct((t, n_h * HP), F32), *c_outs),
        in_specs=[qspec, kspec, kspec, *c_in_specs], out_specs=(qspec, qspec, *([_ANY] * len(c_outs))),
        scratch_shapes=c_sems, input_output_aliases=alias,
        compiler_params=_params(("arbitrary", "arbitrary", "arbitrary")),
    )(q, k, v, *c_ins)


def merge_fwd(oa, ob, zp, x, w_oa_t, w_ob_t, w_o, tm):
    t = x.shape[0]

    def body(oa_ref, ob_ref, ga_ref, gb_ref, x_ref, woa_ref, wob_ref, wo_ref, x1_o, mg_o):
        ya = _dot_nt(oa_ref[...], woa_ref[...])
        yb = _dot_nt(ob_ref[...], wob_ref[...])
        merged = (jax.nn.sigmoid(ga_ref[...]) * ya + jax.nn.sigmoid(gb_ref[...]) * yb).astype(BF16)
        mg_o[...] = merged
        x1_o[...] = x_ref[...] + _dot_nn(merged, wo_ref[...])

    return pl.pallas_call(
        body, name="merge_fwd", grid=(t // tm,),
        out_shape=(jax.ShapeDtypeStruct((t, D_MODEL), F32), jax.ShapeDtypeStruct((t, D_MODEL), BF16)),
        in_specs=[_rows(tm, H_A * HP), _rows(tm, H_B * HP), _rows(tm, 1024, 1), _rows(tm, 1024, 2), _rows(tm, D_MODEL),
                  _resident((D_MODEL, H_A * HP)), _resident((D_MODEL, H_B * HP)), _resident((D_MODEL, D_MODEL))],
        out_specs=(_rows(tm, D_MODEL), _rows(tm, D_MODEL)), compiler_params=_params(("parallel",)),
    )(oa, ob, zp, zp, x, w_oa_t, w_ob_t, w_o)


def mlp_fwd(x1, g_mlp, w_up_t, w_down, tm):
    t = x1.shape[0]
    fc = 1024

    def body(x_ref, g_ref, wup_ref, wdn_ref, x2_o, u_o):
        xv = x_ref[...]
        h2 = (xv * _rstd(xv, D_MODEL) * g_ref[...]).astype(BF16)
        acc = xv
        for cidx in range(D_FF // fc):
            sl = slice(cidx * fc, (cidx + 1) * fc)
            u = jnp.maximum(_dot_nt(h2, wup_ref[sl, :]), 0.0)
            u_o[:, sl] = u.astype(BF16)
            acc = acc + _dot_nn((u * u).astype(BF16), wdn_ref[sl, :])
        x2_o[...] = acc

    return pl.pallas_call(
        body, name="mlp_fwd", grid=(t // tm,),
        out_shape=(jax.ShapeDtypeStruct((t, D_MODEL), F32), jax.ShapeDtypeStruct((t, D_FF), BF16)),
        in_specs=[_rows(tm, D_MODEL), _resident((1, D_MODEL)), _resident((D_FF, D_MODEL)), _resident((D_FF, D_MODEL))],
        out_specs=(_rows(tm, D_MODEL), _rows(tm, D_FF)), compiler_params=_params(("parallel",)),
    )(x1, g_mlp, w_up_t, w_down)


def ple_loss_bwd(x2, p, tgt, g_ple, g_final, w_pg, w_ple_t, tm):
    t = x2.shape[0]
    inv_d = 1.0 / D_MODEL

    def body(x2_ref, p_ref, tg_ref, gp_ref, gf_ref, wpg_ref, wple_ref, dx2_o, dt_o, h3_o, dpe_o, st_o):
        @pl.when(pl.program_id(0) == 0)
        def _():
            st_o[...] = jnp.zeros_like(st_o)

        x2v = x2_ref[...]
        gp, gf = gp_ref[...], gf_ref[...]
        r2 = _rstd(x2v, D_MODEL)
        xh2 = x2v * r2
        h3 = (xh2 * gp).astype(BF16)
        h3_o[...] = h3
        gate = jax.nn.sigmoid(_dot_nn(h3, wpg_ref[...]))
        pe = _dot_nt(p_ref[...].astype(BF16), wple_ref[...])
        x3 = x2v + gate * pe
        r3 = _rstd(x3, D_MODEL)
        xh3 = x3 * r3
        err = xh3 * gf - tg_ref[...]
        dy = err * inv_d
        dx3 = _rms_bwd(dy, xh3, r3, gf, D_MODEL)
        dpe_o[...] = (dx3 * gate).astype(BF16)
        dt = (dx3 * pe * gate * (1.0 - gate)).astype(BF16)
        dt_o[...] = dt
        dh3 = _dot_nt(dt, wpg_ref[...])
        dx2_o[...] = dx3 + _rms_bwd(dh3, xh2, r2, gp, D_MODEL)
        st_o[0:1, :] += _colsum(dh3 * xh2)
        st_o[1:2, :] += _colsum(dy * xh3)
        st_o[2:3, :] += _colsum(err * err) * (0.5 * inv_d)

    bf = jax.ShapeDtypeStruct((t, D_MODEL), BF16)
    return pl.pallas_call(
        body, name="ple_loss_bwd", grid=(t // tm,),
        out_shape=(jax.ShapeDtypeStruct((t, D_MODEL), F32), bf, bf, bf, jax.ShapeDtypeStruct((3, D_MODEL), F32)),
        in_specs=[_rows(tm, D_MODEL), _rows(tm, PLE_DIM), _rows(tm, D_MODEL), _resident((1, D_MODEL)), _resident((1, D_MODEL)),
                  _resident((D_MODEL, D_MODEL)), _resident((D_MODEL, PLE_DIM))],
        out_specs=(_rows(tm, D_MODEL), _rows(tm, D_MODEL), _rows(tm, D_MODEL), _rows(tm, D_MODEL),
                   pl.BlockSpec((3, D_MODEL), lambda i: (0, 0))),
        compiler_params=_params(("arbitrary",)),
    )(x2, p, tgt, g_ple, g_final, w_pg, w_ple_t)


def mlp_bwd(dx2, x1, u, g_mlp, w_up_t, w_down, tm):
    t = x1.shape[0]
    fc = 1024

    def body(dx2_ref, x1_ref, u_ref, g_ref, wup_ref, wdn_ref, dx1_o, da_o, h2_o, st_o):
        @pl.when(pl.program_id(0) == 0)
        def _():
            st_o[...] = jnp.zeros_like(st_o)

        d2 = dx2_ref[...]
        d2b = d2.astype(BF16)
        dh2 = jnp.zeros((tm, D_MODEL), F32)
        for cidx in range(D_FF // fc):
            sl = slice(cidx * fc, (cidx + 1) * fc)
            da = (_dot_nt(d2b, wdn_ref[sl, :]) * (2.0 * u_ref[:, sl].astype(F32))).astype(BF16)
            da_o[:, sl] = da
            dh2 = dh2 + _dot_nn(da, wup_ref[sl, :])
        xv = x1_ref[...]
        g = g_ref[...]
        r1 = _rstd(xv, D_MODEL)
        xh1 = xv * r1
        h2_o[...] = (xh1 * g).astype(BF16)
        st_o[...] += _colsum(dh2 * xh1)
        dx1_o[...] = d2 + _rms_bwd(dh2, xh1, r1, g, D_MODEL)

    return pl.pallas_call(
        body, name="mlp_bwd", grid=(t // tm,),
        out_shape=(jax.ShapeDtypeStruct((t, D_MODEL), F32), jax.ShapeDtypeStruct((t, D_FF), BF16),
                   jax.ShapeDtypeStruct((t, D_MODEL), BF16), jax.ShapeDtypeStruct((1, D_MODEL), F32)),
        in_specs=[_rows(tm, D_MODEL), _rows(tm, D_MODEL), _rows(tm, D_FF), _resident((1, D_MODEL)),
                  _resident((D_FF, D_MODEL)), _resident((D_FF, D_MODEL))],
        out_specs=(_rows(tm, D_MODEL), _rows(tm, D_FF), _rows(tm, D_MODEL), pl.BlockSpec((1, D_MODEL), lambda i: (0, 0))),
        compiler_params=_params(("arbitrary",)),
    )(dx2, x1, u, g_mlp, w_up_t, w_down)


def merge_bwd(dx1, oa, ob, zp, w_oa_t, w_ob_t, w_o, tm, comm=None):
    t = dx1.shape[0]
    grid = (t // tm,)
    c_ins, c_in_specs, c_outs, c_sems, alias = _comm_parts(comm, 8, 5)

    def body(*refs):
        ((dx1_ref, oa_ref, ob_ref, ga_ref, gb_ref, woa_ref, wob_ref, wo_ref), cin,
         (doa_o, dob_o, dg_o, dya_o, dyb_o), cout, _, csem) = _split_refs(refs, 8, 5, 0, comm)
        _comm_start(comm, cin, cout, csem, grid)
        dm = _dot_nt(dx1_ref[...].astype(BF16), wo_ref[...])
        for o_ref, g_ref, w_ref, do_o, dy_o, col in ((oa_ref, ga_ref, woa_ref, doa_o, dya_o, 0),
                                                     (ob_ref, gb_ref, wob_ref, dob_o, dyb_o, 1)):
            yv = _dot_nt(o_ref[...], w_ref[...])
            sg = jax.nn.sigmoid(g_ref[...])
            dyv = (dm * sg).astype(BF16)
            dy_o[...] = dyv
            dg_o[:, col * D_MODEL:(col + 1) * D_MODEL] = (dm * yv * sg * (1.0 - sg)).astype(BF16)
            do_o[...] = _dot_nn(dyv, w_ref[...]).astype(BF16)
        _comm_finish(comm, cin, cout, csem, grid)

    bf = jax.ShapeDtypeStruct((t, D_MODEL), BF16)
    return pl.pallas_call(
        body, name="merge_bwd", grid=grid,
        out_shape=(bf, bf, jax.ShapeDtypeStruct((t, 2 * D_MODEL), BF16), bf, bf, *c_outs),
        in_specs=[_rows(tm, D_MODEL), _rows(tm, H_A * HP), _rows(tm, H_B * HP), _rows(tm, 1024, 1), _rows(tm, 1024, 2),
                  _resident((D_MODEL, H_A * HP)), _resident((D_MODEL, H_B * HP)), _resident((D_MODEL, D_MODEL)), *c_in_specs],
        out_specs=(_rows(tm, D_MODEL), _rows(tm, D_MODEL), _rows(tm, 2 * D_MODEL), _rows(tm, D_MODEL), _rows(tm, D_MODEL),
                   *([_ANY] * len(c_outs))),
        scratch_shapes=c_sems, input_output_aliases=alias,
        compiler_params=_params(("arbitrary",)),
    )(dx1, oa, ob, zp, zp, w_oa_t, w_ob_t, w_o, *c_ins)


def attn_bwd(q, k, v, do, o, lse, n_b, s_len, tq, name, comm=None):
    t = q.shape[0]
    n_h, n_hk = q.shape[1] // HP, k.shape[1] // HP
    grp = n_h // n_hk
    nq = s_len // tq
    sub = min(tq, 256)
    grid = (n_b, n_hk, grp, nq)
    c_ins, c_in_specs, c_outs, c_sems, alias = _comm_parts(comm, 6, 3)

    def body(*refs):
        ((q_ref, k_ref, v_ref, do_ref, o_ref, lse_ref), cin, (dq_o, dk_o, dv_o), cout, (p_s, ds_s),
         csem) = _split_refs(refs, 6, 3, 2, comm)
        _comm_start(comm, cin, cout, csem, grid)

        @pl.when((pl.program_id(2) == 0) & (pl.program_id(3) == 0))
        def _():
            dk_o[...] = jnp.zeros_like(dk_o)
            dv_o[...] = jnp.zeros_like(dv_o)

        kv, vv = k_ref[...], v_ref[...]
        for r in range(tq // sub):
            rows = slice(r * sub, (r + 1) * sub)
            qv, dov = q_ref[rows, :], do_ref[rows, :]
            delta = jnp.sum(dov.astype(F32) * o_ref[rows, :].astype(F32), axis=-1, keepdims=True)
            p = jnp.exp(_dot_nt(qv, kv) - lse_ref[rows, 0:1])
            ds = (p * (_dot_nt(dov, vv) - delta)).astype(BF16)
            p_s[rows, :] = p.astype(BF16)
            ds_s[rows, :] = ds
            dq_o[rows, :] = _dot_nn(ds, kv)
        dk_o[...] += _dot_tn(ds_s[...], q_ref[...])
        dv_o[...] += _dot_tn(p_s[...], do_ref[...])
        _comm_finish(comm, cin, cout, csem, grid)

    qspec = pl.BlockSpec((tq, HP), lambda b, hk, g, i: (b * nq + i, hk * grp + g))
    kspec = pl.BlockSpec((s_len, HP), lambda b, hk, g, i: (b, hk))
    return pl.pallas_call(
        body, name=name, grid=grid,
        out_shape=(jax.ShapeDtypeStruct((t, n_h * HP), F32), jax.ShapeDtypeStruct((t, n_hk * HP), F32),
                   jax.ShapeDtypeStruct((t, n_hk * HP), F32), *c_outs),
        in_specs=[qspec, kspec, kspec, qspec, qspec, qspec, *c_in_specs],
        out_specs=(qspec, kspec, kspec, *([_ANY] * len(c_outs))),
        scratch_shapes=[pltpu.VMEM((tq, s_len), BF16), pltpu.VMEM((tq, s_len), BF16), *c_sems],
        input_output_aliases=alias,
        compiler_params=_params(("arbitrary", "arbitrary", "arbitrary", "arbitrary")),
    )(q, k, v, do, o, lse, *c_ins)


def prep_bwd(dqa, dka, dva, dqb, dkb, dvb, zp, tabs, g_qa, g_kva, g_qn, g_kn, w_qb_t, w_kvb_t, tm, s_len):
    t = zp.shape[0]
    nsb = s_len // tm
    scale_a = (QK_NOPE + QK_ROPE) ** -0.5
    scale_b = HD_B ** -0.5

    def body(dqa_ref, dka_ref, dva_ref, dqb_ref, dkb_ref, dvb_ref, qb_ref, qlat_ref, kb_ref, ckv_ref, tab_ref,
             gqa_ref, gkva_ref, gqn_ref, gkn_ref, wqb_ref, wkvb_ref, dzq_o, dsm_o, dqap_o, dkva_o, st_o):
        @pl.when(pl.program_id(0) == 0)
        def _():
            st_o[...] = jnp.zeros_like(st_o)

        ca, s1a, s2a = tab_ref[0], tab_ref[1], tab_ref[2]
        ck = tab_ref[3]
        cb, s1b, s2b = tab_ref[4], tab_ref[5], tab_ref[6]
        for h in range(H_A):
            sl = slice(h * HP, (h + 1) * HP)
            dqap_o[:, sl] = _rope_bwd(dqa_ref[:, sl] * scale_a, ca, s1a, s2a).astype(BF16)
        dcq = _dot_nn(dqap_o[...], wqb_ref[...])
        ql = qlat_ref[...]
        rq = _rstd(ql, Q_LORA)
        xh = ql * rq
        gqa = gqa_ref[...]
        st_o[0:1, :] += _colsum(dcq * xh)
        dsm_o[:, 0:256] = _rms_bwd(dcq, xh, rq, gqa, Q_LORA).astype(BF16)
        dkpe = jnp.zeros((tm, HP), F32)
        for h in range(H_A):
            sl = slice(h * HP, (h + 1) * HP)
            dk = dka_ref[:, sl]
            dkpe = dkpe + dk
            dkva_o[:, sl] = dk.astype(BF16)
        dkva_o[:, H_A * HP:] = dva_ref[...].astype(BF16)
        dsm_o[:, 896:1024] = _rope_bwd(dkpe, ck, s1a, s2a).astype(BF16)
        dckv = _dot_nn(dkva_o[...], wkvb_ref[...])
        cr = ckv_ref[...]
        rk = _rstd(cr, KV_LORA)
        xh = cr * rk
        st_o[1:2, 0:128] += _colsum(dckv * xh)
        dsm_o[:, 768:896] = _rms_bwd(dckv, xh, rk, gkva_ref[...], KV_LORA).astype(BF16)
        gqn, gkn = gqn_ref[...], gkn_ref[...]
        dgq = jnp.zeros((1, HP), F32)
        for h in range(H_B):
            sl = slice(h * HP, (h + 1) * HP)
            dy = _rope_bwd(dqb_ref[:, sl] * scale_b, cb, s1b, s2b)
            xs = qb_ref[:, sl]
            r = _rstd(xs, HD_B)
            xh = xs * r
            dgq = dgq + _colsum(dy * xh)
            dzq_o[:, sl] = _rms_bwd(dy, xh, r, gqn, HD_B).astype(BF16)
        st_o[2:3, 0:128] += dgq
        dgk = jnp.zeros((1, HP), F32)
        for h in range(KV_B):
            sl = slice(h * HP, (h + 1) * HP)
            dy = _rope_bwd(dkb_ref[:, sl], cb, s1b, s2b)
            xs = kb_ref[:, sl]
            r = _rstd(xs, HD_B)
            xh = xs * r
            dgk = dgk + _colsum(dy * xh)
            dsm_o[:, 256 + h * HP:256 + (h + 1) * HP] = _rms_bwd(dy, xh, r, gkn, HD_B).astype(BF16)
        st_o[3:4, 0:128] += dgk
        dsm_o[:, 512:768] = dvb_ref[...].astype(BF16)

    bf = jax.ShapeDtypeStruct((t, 1024), BF16)
    return pl.pallas_call(
        body, name="prep_bwd", grid=(t // tm,),
        out_shape=(bf, bf, bf, jax.ShapeDtypeStruct((t, 2048), BF16), jax.ShapeDtypeStruct((4, 256), F32)),
        in_specs=[_rows(tm, 1024), _rows(tm, 1024), _rows(tm, 1024), _rows(tm, 1024), _rows(tm, 256), _rows(tm, 256),
                  _rows(tm, 1024, 0), _rows(tm, 256, 12), _rows(tm, 256, 13), _rows(tm, 128, 30),
                  pl.BlockSpec((7, tm, HP), lambda i: (0, i % nsb, 0)),
                  _resident((1, Q_LORA)), _resident((1, KV_LORA)), _resident((1, HP)), _resident((1, HP)),
                  _resident((H_A * HP, Q_LORA)), _resident((2 * H_A * HP, KV_LORA))],
        out_specs=(_rows(tm, 1024), _rows(tm, 1024), _rows(tm, 1024), _rows(tm, 2048), pl.BlockSpec((4, 256), lambda i: (0, 0))),
        compiler_params=_params(("arbitrary",)),
    )(dqa, dka, dva, dqb, dkb, dvb, zp, zp, zp, zp, tabs, g_qa, g_kva, g_qn, g_kn, w_qb_t, w_kvb_t)


def in_bwd(dzq, dgab, dsm, x, dx1, g_mix, w_in_t, tm):
    t = x.shape[0]

    def body(dzq_ref, dg_ref, dsm_ref, x_ref, dx1_ref, g_ref, w_ref, dx_o, h_o, st_o):
        @pl.when(pl.program_id(0) == 0)
        def _():
            st_o[...] = jnp.zeros_like(st_o)

        dh = _dot_nn(dzq_ref[...], w_ref[0:1024, :])
        dh = dh + _dot_nn(dg_ref[...], w_ref[1024:3072, :])
        dh = dh + _dot_nn(dsm_ref[...], w_ref[3072:4096, :])
        xv = x_ref[...]
        g = g_ref[...]
        r = _rstd(xv, D_MODEL)
        xh = xv * r
        h_o[...] = (xh * g).astype(BF16)
        st_o[...] += _colsum(dh * xh)
        dx_o[...] = dx1_ref[...] + _rms_bwd(dh, xh, r, g, D_MODEL)

    return pl.pallas_call(
        body, name="in_bwd", grid=(t // tm,),
        out_shape=(jax.ShapeDtypeStruct((t, D_MODEL), F32), jax.ShapeDtypeStruct((t, D_MODEL), BF16),
                   jax.ShapeDtypeStruct((1, D_MODEL), F32)),
        in_specs=[_rows(tm, 1024), _rows(tm, 2048), _rows(tm, 1024), _rows(tm, D_MODEL), _rows(tm, D_MODEL),
                  _resident((1, D_MODEL)), _resident((ZP, D_MODEL))],
        out_specs=(_rows(tm, D_MODEL), _rows(tm, D_MODEL), pl.BlockSpec((1, D_MODEL), lambda i: (0, 0))),
        compiler_params=_params(("arbitrary",)),
    )(dzq, dgab, dsm, x, dx1, g_mix, w_in_t)


def matmul_tn(a, b, name, square_a=False):
    t, m = a.shape
    n = b.shape[1]
    bm = min(m, 512)
    tk = min(t, 2048)

    def body(a_ref, b_ref, o_ref):
        @pl.when(pl.program_id(1) == 0)
        def _():
            o_ref[...] = jnp.zeros_like(o_ref)

        av = a_ref[...]
        if square_a:
            av = (av.astype(F32) * av.astype(F32))
        o_ref[...] += _dot_tn(av.astype(BF16), b_ref[...].astype(BF16))

    return pl.pallas_call(
        body, name=name, grid=(m // bm, t // tk), out_shape=jax.ShapeDtypeStruct((m, n), F32),
        in_specs=[pl.BlockSpec((tk, bm), lambda i, kk: (kk, i)), pl.BlockSpec((tk, n), lambda i, kk: (kk, 0))],
        out_specs=pl.BlockSpec((bm, n), lambda i, kk: (i, 0)),
        compiler_params=_params(("parallel", "arbitrary")),
    )(a, b)


def adamw(w, g, m, v, name):
    r, c = w.shape
    tr = r if r <= 256 else 256
    c1 = 1.0 - ADAM_B1 ** ADAM_STEP
    c2 = 1.0 - ADAM_B2 ** ADAM_STEP

    def body(w_ref, g_ref, m_ref, v_ref, d_o, m_o, v_o):
        gv = g_ref[...]
        mn = ADAM_B1 * m_ref[...] + (1.0 - ADAM_B1) * gv
        vn = ADAM_B2 * v_ref[...] + (1.0 - ADAM_B2) * (gv * gv)
        m_o[...] = mn
        v_o[...] = vn
        d_o[...] = -ADAM_LR * ((mn / c1) / (jnp.sqrt(vn / c2) + ADAM_EPS) + ADAM_WD * w_ref[...])

    spec = pl.BlockSpec((tr, c), lambda i: (i, 0))
    shp = jax.ShapeDtypeStruct((r, c), F32)
    return pl.pallas_call(
        body, name=name, grid=(r // tr,), out_shape=(shp, shp, shp), in_specs=[spec] * 4, out_specs=(spec,) * 3,
        compiler_params=_params(("parallel",)),
    )(w, g, m, v)


def _rope_tables(s_len):
    def angles(pos, dim):
        inv = np.float32(ROPE_THETA) ** (-np.arange(0, dim, 2, dtype=np.float32) / np.float32(dim))
        return pos.astype(np.float32)[:, None] * inv[None, :]

    tpos = np.arange(s_len)
    a1 = angles(tpos, QK_ROPE)
    ar = angles(tpos // GRID_W, HD_B // 2)
    ac = angles(tpos % GRID_W, HD_B // 2)
    z16 = np.zeros((s_len, 16), np.float32)
    z32 = np.zeros((s_len, 32), np.float32)
    z64 = np.zeros((s_len, 64), np.float32)
    one64 = np.ones((s_len, 64), np.float32)
    c1, s1 = np.cos(a1), np.sin(a1)
    ca = np.concatenate([one64, c1, c1, z32], axis=1)
    ck = np.concatenate([z64, c1, c1, z32], axis=1)
    s1a = np.concatenate([z64, -s1, z16, z32], axis=1)
    s2a = np.concatenate([z64, z16, s1, z32], axis=1)
    cr, sr, cc, sc = np.cos(ar), np.sin(ar), np.cos(ac), np.sin(ac)
    cb = np.concatenate([cr, cr, cc, cc, z64], axis=1)
    s1b = np.concatenate([-sr, z16, -sc, z16, z64], axis=1)
    s2b = np.concatenate([z16, sr, z16, sc, z64], axis=1)
    return jnp.asarray(np.stack([ca, s1a, s2a, ck, cb, s1b, s2b]).astype(np.float32))


def _pad_heads(a, n_heads, axis):
    shp = a.shape
    a = a.reshape(shp[:axis] + (n_heads, shp[axis] // n_heads) + shp[axis + 1:])
    pad = [(0, 0)] * a.ndim
    pad[axis + 1] = (0, HP - a.shape[axis + 1])
    a = jnp.pad(a, pad)
    return a.reshape(shp[:axis] + (n_heads * HP,) + shp[axis + 1:])


def _unpad_heads(a, n_heads, width, axis):
    shp = a.shape
    a = a.reshape(shp[:axis] + (n_heads, HP) + shp[axis + 1:])
    a = lax.slice_in_dim(a, 0, width, axis=axis + 1)
    return a.reshape(shp[:axis] + (n_heads * width,) + shp[axis + 1:])


def _pack_rows(blocks, names):
    parts = []
    for name in names:
        b = blocks[name]
        padr = PACK_ROWS[name] - b.shape[-2]
        if padr:
            b = jnp.pad(b, [(0, 0)] * (b.ndim - 2) + [(0, padr), (0, 0)])
        parts.append(b)
    return jnp.concatenate(parts, axis=parts[0].ndim - 2)


def _expand_w_in(wt):
    z64 = jnp.zeros((64, D_MODEL), wt.dtype)
    z32 = jnp.zeros((32, D_MODEL), wt.dtype)
    return jnp.concatenate([
        _pad_heads(wt[416:928], H_B, 0), wt[1184:2208], wt[2208:3232], wt[0:256],
        _pad_heads(wt[928:1056], KV_B, 0), _pad_heads(wt[1056:1184], KV_B, 0), wt[256:384],
        z64, wt[384:416], z32], axis=0)


def _collapse_w_in(dq, dg, ds):
    return jnp.concatenate([
        ds[0:256], ds[768:896], ds[960:992], _unpad_heads(dq, H_B, HD_B, 0), _unpad_heads(ds[256:512], KV_B, HD_B, 0),
        _unpad_heads(ds[512:768], KV_B, HD_B, 0), dg], axis=0)


def kernel(x, p, g_mix, w_in, g_qa, w_qb, g_kva, w_kvb, g_qn, g_kn, w_oa, w_ob, w_o, g_mlp, w_up, w_down, g_ple, w_ple_gate, w_ple, g_final, loss_target, m_g_mix, m_w_in, m_g_qa, m_w_qb, m_g_kva, m_w_kvb, m_g_qn, m_g_kn, m_w_oa, m_w_ob, m_w_o, m_g_mlp, m_w_up, m_w_down, m_g_ple, m_w_ple_gate, m_w_ple, m_g_final, v_g_mix, v_w_in, v_g_qa, v_w_qb, v_g_kva, v_w_kvb, v_g_qn, v_g_kn, v_w_oa, v_w_ob, v_w_o, v_g_mlp, v_w_up, v_w_down, v_g_ple, v_w_ple_gate, v_w_ple, v_g_final):
    n_b, s_len, _ = x.shape
    t = n_b * s_len
    tm = min(256, s_len)
    tq_f = min(512, s_len)
    tq_b = min(512, s_len)

    mats = dict(w_in=(w_in, m_w_in, v_w_in), w_qb=(w_qb, m_w_qb, v_w_qb), w_kvb=(w_kvb, m_w_kvb, v_w_kvb),
                w_oa=(w_oa, m_w_oa, v_w_oa), w_ob=(w_ob, m_w_ob, v_w_ob), w_o=(w_o, m_w_o, v_w_o),
                w_up=(w_up, m_w_up, v_w_up), w_down=(w_down, m_w_down, v_w_down),
                w_ple_gate=(w_ple_gate, m_w_ple_gate, v_w_ple_gate), w_ple=(w_ple, m_w_ple, v_w_ple))
    col_sharded = ("w_in", "w_qb", "w_kvb", "w_oa", "w_ob", "w_up", "w_ple")

    blocks = {}
    for name, (w, _, _) in mats.items():
        w2 = w[0]
        if name in col_sharded:
            w2 = w2.T
        blocks[name] = w2.reshape(-1, D_MODEL).astype(BF16)
    off_w1, _ = _pack_offsets(PACK_W1)
    off_w2, _ = _pack_offsets(PACK_W2)
    full1 = allgather_rows(_pack_rows(blocks, PACK_W1))
    pack2 = _pack_rows(blocks, PACK_W2)

    def gathered(full, offs, name, rows, width):
        return full[:, offs[name]:offs[name] + rows].reshape(-1, width)

    w_in_t = _expand_w_in(gathered(full1, off_w1, "w_in", 404, D_MODEL))
    w_qb_t = _pad_heads(gathered(full1, off_w1, "w_qb", 24, Q_LORA), H_A, 0)
    wkvb = gathered(full1, off_w1, "w_kvb", 16, KV_LORA).reshape(H_A, 2, 64, KV_LORA)
    w_kvb_t = jnp.concatenate([_pad_heads(wkvb[:, 0].reshape(-1, KV_LORA), H_A, 0),
                               _pad_heads(wkvb[:, 1].reshape(-1, KV_LORA), H_A, 0)], axis=0)

    tabs = _rope_tables(s_len)
    g_qn_p = jnp.pad(g_qn, ((0, 0), (0, HP - HD_B)))
    g_kn_p = jnp.pad(g_kn, ((0, 0), (0, HP - HD_B)))
    xf = x.reshape(t, D_MODEL)
    pf = p.reshape(t, PLE_DIM)
    tgt = loss_target.reshape(t, D_MODEL)

    zp = in_proj(xf, g_mix, w_in_t, tm)
    qa, ka, va, qb, kb, vb, cq, ckv = attn_prep(zp, tabs, g_qa, g_kva, g_qn_p, g_kn_p, w_qb_t, w_kvb_t, tm, s_len)
    oa, lse_a, full2 = attn_fwd(qa, ka, va, n_b, s_len, tq_f, "attn_a_fwd", comm=gather_first_comm(pack2))
    ob, lse_b, full2 = attn_fwd(qb, kb, vb, n_b, s_len, tq_f, "attn_b_fwd", comm=gather_pass_comm(full2))
    w_oa_t = _pad_heads(gathered(full2, off_w2, "w_oa", 64, H_A * V_DIM_A), H_A, 1)
    w_ob_t = _pad_heads(gathered(full2, off_w2, "w_ob", 64, H_B * HD_B), H_B, 1)
    w_o_f = gathered(full2, off_w2, "w_o", 128, D_MODEL)
    w_up_t = gathered(full2, off_w2, "w_up", 512, D_MODEL)
    w_down_f = gathered(full2, off_w2, "w_down", 512, D_MODEL)
    w_pg_f = gathered(full2, off_w2, "w_ple_gate", 128, D_MODEL)
    w_ple_t = gathered(full2, off_w2, "w_ple", 32, PLE_DIM)
    x1, merged = merge_fwd(oa, ob, zp, xf, w_oa_t, w_ob_t, w_o_f, tm)
    x2, u = mlp_fwd(x1, g_mlp, w_up_t, w_down_f, tm)
    dx2, dt, h3, dpe, st_ple = ple_loss_bwd(x2, pf, tgt, g_ple, g_final.reshape(1, D_MODEL), w_pg_f, w_ple_t, tm)
    dx1, da, h2, st_mlp = mlp_bwd(dx2, x1, u, g_mlp, w_up_t, w_down_f, tm)

    core = lax.axis_index("c").astype(jnp.int32).reshape(1)
    chip = (2 * lax.axis_index("x") + lax.axis_index("y")).astype(jnp.int32).reshape(1)

    def packed(gblocks, names):
        return _pack_rows({n: gblocks[n].reshape(N_DEV, -1, D_MODEL).astype(BF16) for n in names}, names)

    g1 = dict(w_up=matmul_tn(da, h2, "gw_up"), w_down=matmul_tn(u, dx2, "gw_down", square_a=True),
              w_ple_gate=matmul_tn(h3, dt, "gw_pg"), w_ple=matmul_tn(dpe, pf, "gw_ple"))
    gpack1 = packed(g1, PACK_G1)
    doa, dob, dgab, dya, dyb, got1 = merge_bwd(dx1, oa, ob, zp, w_oa_t, w_ob_t, w_o_f, tm, comm=scatter_sibling_comm(gpack1))
    part1 = add_pairs(gpack1, got1, core)
    dqa, dka, dva, land1 = attn_bwd(qa, ka, va, doa, oa, lse_a, n_b, s_len, tq_b, "attn_a_bwd", comm=scatter_chips_comm(part1))
    gshard1 = sum_chips(part1, land1, chip)
    dqb, dkb, dvb = attn_bwd(qb, kb, vb, dob, ob, lse_b, n_b, s_len, tq_b, "attn_b_bwd")
    dzq, dsm, dqap, dkva, st_prep = prep_bwd(dqa, dka, dva, dqb, dkb, dvb, zp, tabs, g_qa, g_kva, g_qn_p, g_kn_p,
                                             w_qb_t, w_kvb_t, tm, s_len)
    grad_x, h, st_mix = in_bwd(dzq, dgab, dsm, xf, dx1, g_mix, w_in_t, tm)

    gkv = matmul_tn(dkva, ckv, "gw_kvb")
    g2 = dict(
        w_in=_collapse_w_in(matmul_tn(dzq, h, "gw_in_q"), matmul_tn(dgab, h, "gw_in_g"), matmul_tn(dsm, h, "gw_in_s")),
        w_qb=_unpad_heads(matmul_tn(dqap, cq, "gw_qb"), H_A, QK_NOPE + QK_ROPE, 0),
        w_kvb=jnp.stack([_unpad_heads(gkv[:H_A * HP], H_A, 64, 0).reshape(H_A, 64, KV_LORA),
                         _unpad_heads(gkv[H_A * HP:], H_A, 64, 0).reshape(H_A, 64, KV_LORA)], axis=1),
        w_oa=_unpad_heads(matmul_tn(dya, oa, "gw_oa"), H_A, V_DIM_A, 1),
        w_ob=_unpad_heads(matmul_tn(dyb, ob, "gw_ob"), H_B, HD_B, 1),
        w_o=matmul_tn(merged, dx1, "gw_o"))
    gpack2 = packed(g2, PACK_G2)
    part2 = add_pairs(gpack2, exchange_sibling(gpack2), core)
    gshard2 = sum_chips(part2, exchange_chips(part2), chip)
    off_g1, _ = _pack_offsets(PACK_G1)
    off_g2, _ = _pack_offsets(PACK_G2)

    def row(a, width):
        return jnp.pad(a, ((0, 0), (0, D_MODEL - width)))

    def tile8(a, width):
        return jnp.pad(a, ((0, 7), (0, D_MODEL - width)))

    stats = jnp.concatenate([
        tile8(st_mix, D_MODEL), tile8(st_prep[0:1], 256), tile8(st_prep[1:2], 256), tile8(st_prep[2:3], 256),
        tile8(st_prep[3:4], 256), tile8(st_mlp, D_MODEL), tile8(st_ple[0:1], D_MODEL), tile8(st_ple[1:2], D_MODEL),
        tile8(st_ple[2:3], D_MODEL)], axis=0)
    stats = allreduce_stats(stats)[::8]
    loss = jnp.sum(stats[ST_LOSS])
    stats = jnp.pad(stats, ((0, ST_ROWS - 9), (0, 0)))

    out_g, out_d, out_m, out_v = {}, {}, {}, {}
    for name, (w, m, v) in mats.items():
        gshard, off = (gshard1, off_g1[name]) if name in PACK_G1 else (gshard2, off_g2[name])
        shard_shape = w.shape[1:]
        if name in col_sharded:
            rows_t = shard_shape[1]
            g2 = gshard[off:off + (rows_t * shard_shape[0]) // D_MODEL].reshape(rows_t, shard_shape[0]).T
        else:
            g2 = gshard[off:off + shard_shape[0]]
        d2, m2, v2 = adamw(w[0], g2, m[0], v[0], "adamw_" + name)
        out_g[name], out_d[name], out_m[name], out_v[name] = g2[None], d2[None], m2[None], v2[None]

    gains = (("g_mix", g_mix, m_g_mix, v_g_mix, ST_G_MIX, D_MODEL), ("g_qa", g_qa, m_g_qa, v_g_qa, ST_G_QA, Q_LORA),
             ("g_kva", g_kva, m_g_kva, v_g_kva, ST_G_KVA, KV_LORA), ("g_qn", g_qn, m_g_qn, v_g_qn, ST_G_QN, HD_B),
             ("g_kn", g_kn, m_g_kn, v_g_kn, ST_G_KN, HD_B), ("g_mlp", g_mlp, m_g_mlp, v_g_mlp, ST_G_MLP, D_MODEL),
             ("g_ple", g_ple, m_g_ple, v_g_ple, ST_G_PLE, D_MODEL),
             ("g_final", g_final, m_g_final, v_g_final, ST_G_FINAL, D_MODEL))

    def gain_pack(idx):
        rows_ = [row(gn[idx].reshape(1, -1), gn[5]) for gn in gains]
        return jnp.concatenate(rows_ + [jnp.zeros((ST_ROWS - len(gains), D_MODEL), F32)], axis=0)

    gd, gm, gv = adamw(gain_pack(1), stats, gain_pack(2), gain_pack(3), "adamw_gains")
    for name, w, _, _, r_, width in gains:
        for dst, src in ((out_g, stats), (out_d, gd), (out_m, gm), (out_v, gv)):
            dst[name] = src[r_, :width].reshape(w.shape)

    order = ("g_mix", "w_in", "g_qa", "w_qb", "g_kva", "w_kvb", "g_qn", "g_kn", "w_oa", "w_ob", "w_o", "g_mlp",
             "w_up", "w_down", "g_ple", "w_ple_gate", "w_ple", "g_final")
    return (loss, grad_x.reshape(x.shape), *[out_g[n] for n in order], *[out_d[n] for n in order],
            *[out_m[n] for n in order], *[out_v[n] for n in order])
```

```python
import numpy as np
import jax
import jax.numpy as jnp
from jax import lax
from jax.experimental import pallas as pl
from jax.experimental.pallas import tpu as pltpu

F32 = jnp.float32
BF16 = jnp.bfloat16

D_MODEL = 1024
EPS = 1e-6
ROPE_THETA = 10000.0
GRID_W = 64
H_A = 8
QK_NOPE = 64
QK_ROPE = 32
V_DIM_A = 64
Q_LORA = 256
KV_LORA = 128
H_B = 8
KV_B = 2
HD_B = 64
D_FF = 4 * D_MODEL
PLE_DIM = 256
HP = 128
ZP = 4096
N_DEV = 8
N_CHIP = 4

ADAM_LR = 0.001
ADAM_B1 = 0.9
ADAM_B2 = 0.999
ADAM_EPS = 1e-08
ADAM_WD = 0.01
ADAM_STEP = 10

VMEM_LIMIT = 52 * 1024 * 1024

PACK_ROWS = dict(w_in=416, w_qb=32, w_kvb=16, w_oa=64, w_ob=64, w_o=128, w_up=512, w_down=512, w_ple_gate=128, w_ple=32)
PACK_W1 = ("w_in", "w_qb", "w_kvb")
PACK_W2 = ("w_up", "w_down", "w_o", "w_ple_gate", "w_oa", "w_ob", "w_ple")
PACK_G1 = ("w_up", "w_down", "w_ple_gate", "w_ple")
PACK_G2 = ("w_in", "w_qb", "w_kvb", "w_o", "w_oa", "w_ob")


def _pack_offsets(names):
    off, o = {}, 0
    for n in names:
        off[n] = o
        o += PACK_ROWS[n]
    return off, o

ST_G_MIX, ST_G_QA, ST_G_KVA, ST_G_QN, ST_G_KN, ST_G_MLP, ST_G_PLE, ST_G_FINAL, ST_LOSS = range(9)
ST_ROWS = 16


def _dot_nn(a, b):
    return lax.dot_general(a, b, (((1,), (0,)), ((), ())), preferred_element_type=F32)


def _dot_nt(a, b):
    return lax.dot_general(a, b, (((1,), (1,)), ((), ())), preferred_element_type=F32)


def _dot_tn(a, b):
    return lax.dot_general(a, b, (((0,), (0,)), ((), ())), preferred_element_type=F32)


def _rstd(x, n):
    return lax.rsqrt(jnp.sum(x * x, axis=-1, keepdims=True) * (1.0 / n) + EPS)


def _rms_bwd(dy, xh, r, g, n):
    dxh = dy * g
    return r * (dxh - xh * (jnp.sum(dxh * xh, axis=-1, keepdims=True) * (1.0 / n)))


def _rope_fwd(x, c, s1, s2):
    return x * c + pltpu.roll(x, HP - 16, 1) * s1 + pltpu.roll(x, 16, 1) * s2


def _rope_bwd(d, c, s1, s2):
    return d * c + pltpu.roll(d * s1, 16, 1) + pltpu.roll(d * s2, HP - 16, 1)


def _colsum(v):
    return jnp.sum(v, axis=0, keepdims=True)


def _params(sem=None, vmem=VMEM_LIMIT):
    return pltpu.CompilerParams(dimension_semantics=sem, vmem_limit_bytes=vmem)


def _resident(shape):
    nd = len(shape)
    return pl.BlockSpec(shape, lambda *_: (0,) * nd, pipeline_mode=pl.Buffered(1))


def _rows(tm, width, col=0):
    return pl.BlockSpec((tm, width), lambda i: (i, col))


def _packed_weight(rows, off):
    return pl.BlockSpec((N_DEV, rows, D_MODEL), lambda *_: (0, off // rows, 0), pipeline_mode=pl.Buffered(1))


def _wrows(ref, start, size):
    rows = ref.shape[1]
    return ref[start // rows:(start + size) // rows].reshape(size, D_MODEL)


def _mesh_pos():
    return lax.axis_index("x"), lax.axis_index("y"), lax.axis_index("c")


def _flip(v, bit):
    return (1 - v) if bit else v


_ANY = pl.BlockSpec(memory_space=pl.ANY)
_MESH = pl.DeviceIdType.MESH


def _remote(src, dst, send_sems, recv_sems, k, to):
    return pltpu.make_async_remote_copy(src_ref=src, dst_ref=dst, send_sem=send_sems.at[k], recv_sem=recv_sems.at[k],
                                        device_id=to, device_id_type=_MESH)


def _sibling_copies(g_ref, got_ref, send_sems, recv_sems):
    x, y, c = _mesh_pos()
    return [_remote(g_ref.at[2 * j + (1 - c)], got_ref.at[j], send_sems, recv_sems, j, (x, y, 1 - c)) for j in range(N_CHIP)]


def _chip_copies(p_ref, land_ref, send_sems, recv_sems):
    x, y, c = _mesh_pos()
    copies = []
    for k in (1, 2, 3):
        tx, ty = _flip(x, k & 2), _flip(y, k & 1)
        copies.append(_remote(p_ref.at[2 * tx + ty], land_ref.at[k - 1], send_sems, recv_sems, k - 1, (tx, ty, c)))
    return copies


class _Comm:
    def __init__(self, ins, out_shapes, sems, make, aliases=None):
        self.ins, self.out_shapes, self.sems, self.make, self.aliases = list(ins), list(out_shapes), list(sems), make, aliases or {}


def _comm_parts(comm, n_in, n_out):
    if comm is None:
        return [], [], [], [], {}
    alias = {n_in + j: n_out + k for j, k in comm.aliases.items()}
    return comm.ins, [_ANY] * len(comm.ins), comm.out_shapes, comm.sems, alias


def _split_refs(refs, n_in, n_out, n_scratch, comm):
    n_ci = len(comm.ins) if comm else 0
    n_co = len(comm.out_shapes) if comm else 0
    cuts, i = [], 0
    for n in (n_in, n_ci, n_out, n_co, n_scratch):
        cuts.append(refs[i:i + n])
        i += n
    return (*cuts, refs[i:])


def _grid_edge(grid, last):
    cond = None
    for d, n in enumerate(grid):
        here = pl.program_id(d) == (n - 1 if last else 0)
        cond = here if cond is None else cond & here
    return cond


def _comm_start(comm, cin, cout, csem, grid):
    if comm is not None:
        @pl.when(_grid_edge(grid, False))
        def _():
            for cp in comm.make(cin, cout, csem):
                cp.start()


def _comm_finish(comm, cin, cout, csem, grid):
    if comm is not None:
        @pl.when(_grid_edge(grid, True))
        def _():
            for cp in comm.make(cin, cout, csem):
                cp.wait()


def gather_first_comm(shard):
    r, w = shard.shape

    def make(cin, cout, sems):
        (x_ref,), (out_ref,), (send_sems, recv_sems, local_sem) = cin, cout, sems
        x, y, c = _mesh_pos()
        mine = out_ref.at[4 * x + 2 * y + c]
        targets = [(x, y, 1 - c), (1 - x, y, c), (x, 1 - y, c), (1 - x, 1 - y, c)]
        return [_remote(x_ref, mine, send_sems, recv_sems, k, to) for k, to in enumerate(targets)] + [
            pltpu.make_async_copy(x_ref, mine, local_sem)]

    return _Comm([shard], [jax.ShapeDtypeStruct((N_DEV, r, w), shard.dtype)],
                 [pltpu.SemaphoreType.DMA((4,)), pltpu.SemaphoreType.DMA((4,)), pltpu.SemaphoreType.DMA], make)


def gather_pass_comm(full):
    def make(cin, cout, sems):
        (in_ref,), (out_ref,), (send_sems, recv_sems) = cin, cout, sems
        x, y, c = _mesh_pos()
        copies = []
        for k, (px, py) in enumerate([(1 - x, y), (x, 1 - y), (1 - x, 1 - y)]):
            idx = 4 * px + 2 * py + c
            copies.append(_remote(in_ref.at[idx], out_ref.at[idx], send_sems, recv_sems, k, (x, y, 1 - c)))
        return copies

    return _Comm([full], [jax.ShapeDtypeStruct(full.shape, full.dtype)],
                 [pltpu.SemaphoreType.DMA((3,)), pltpu.SemaphoreType.DMA((3,))], make, aliases={0: 0})


def scatter_sibling_comm(g):
    _, r, w = g.shape
    return _Comm([g], [jax.ShapeDtypeStruct((N_CHIP, r, w), g.dtype)],
                 [pltpu.SemaphoreType.DMA((N_CHIP,)), pltpu.SemaphoreType.DMA((N_CHIP,))],
                 lambda cin, cout, sems: _sibling_copies(cin[0], cout[0], sems[0], sems[1]))


def scatter_chips_comm(part):
    _, r, w = part.shape
    return _Comm([part], [jax.ShapeDtypeStruct((N_CHIP - 1, r, w), part.dtype)],
                 [pltpu.SemaphoreType.DMA((3,)), pltpu.SemaphoreType.DMA((3,))],
                 lambda cin, cout, sems: _chip_copies(cin[0], cout[0], sems[0], sems[1]))


def allgather_rows(shard):
    r, w = shard.shape

    def body(x_ref, out_ref, send_sems, recv_sems, local_sem):
        x, y, c = _mesh_pos()
        me, sibling = (x, y, c), (x, y, 1 - c)
        chips = [(1 - x, y), (x, 1 - y), (1 - x, 1 - y)]

        def slot(px, py, pc):
            return out_ref.at[4 * px + 2 * py + pc]

        def copy(k, block, to, src=None):
            return pltpu.make_async_remote_copy(
                src_ref=slot(*block) if src is None else src, dst_ref=slot(*block),
                send_sem=send_sems.at[k], recv_sem=recv_sems.at[k], device_id=to, device_id_type=_MESH)

        mine = pltpu.make_async_copy(x_ref, slot(*me), local_sem)
        mine.start()
        first = [copy(0, me, sibling, src=x_ref)]
        first += [copy(1 + j, me, (*chip, c), src=x_ref) for j, chip in enumerate(chips)]
        for cp in first:
            cp.start()
        passed = [copy(4 + j, (*chip, c), sibling) for j, chip in enumerate(chips)]
        for j, chip in enumerate(chips):
            copy(1 + j, (*chip, c), me).wait_recv()
            passed[j].start()
        copy(0, sibling, me).wait_recv()
        for j, chip in enumerate(chips):
            copy(4 + j, (*chip, 1 - c), me).wait_recv()
        for cp in first + passed:
            cp.wait_send()
        mine.wait()

    return pl.pallas_call(
        body, name="allgather_rows",
        out_shape=jax.ShapeDtypeStruct((N_DEV, r, w), shard.dtype),
        in_specs=[_ANY], out_specs=_ANY,
        scratch_shapes=[pltpu.SemaphoreType.DMA((7,)), pltpu.SemaphoreType.DMA((7,)), pltpu.SemaphoreType.DMA],
    )(shard)


def exchange_sibling(g):
    _, r, w = g.shape

    def body(g_ref, got_ref, send_sems, recv_sems):
        copies = _sibling_copies(g_ref, got_ref, send_sems, recv_sems)
        for cp in copies:
            cp.start()
        for cp in copies:
            cp.wait()

    return pl.pallas_call(
        body, name="exchange_sibling", out_shape=jax.ShapeDtypeStruct((N_CHIP, r, w), g.dtype),
        in_specs=[_ANY], out_specs=_ANY,
        scratch_shapes=[pltpu.SemaphoreType.DMA((N_CHIP,)), pltpu.SemaphoreType.DMA((N_CHIP,))],
    )(g)


def exchange_chips(part):
    _, r, w = part.shape

    def body(p_ref, land_ref, send_sems, recv_sems):
        copies = _chip_copies(p_ref, land_ref, send_sems, recv_sems)
        for cp in copies:
            cp.start()
        for cp in copies:
            cp.wait()

    return pl.pallas_call(
        body, name="exchange_chips", out_shape=jax.ShapeDtypeStruct((N_CHIP - 1, r, w), part.dtype),
        in_specs=[_ANY], out_specs=_ANY,
        scratch_shapes=[pltpu.SemaphoreType.DMA((3,)), pltpu.SemaphoreType.DMA((3,))],
    )(part)


def allreduce_stats(st_mix, st_prep, st_mlp, st_ple):
    def body(mix_ref, prep_ref, mlp_ref, ple_ref, out_ref, mine, gath, send_sems, recv_sems):
        x, y, c = _mesh_pos()
        me = 4 * x + 2 * y + c
        mine[...] = jnp.zeros_like(mine)
        mine[ST_G_MIX:ST_G_MIX + 1, :] = mix_ref[...]
        mine[ST_G_QA:ST_G_KN + 1, 0:256] = prep_ref[...]
        mine[ST_G_MLP:ST_G_MLP + 1, :] = mlp_ref[...]
        mine[ST_G_PLE:ST_LOSS + 1, :] = ple_ref[...]
        gath[me] = mine[...]
        copies = []
        for k in range(1, N_DEV):
            peer = (_flip(x, k & 4), _flip(y, k & 2), _flip(c, k & 1))
            copies.append(_remote(mine, gath.at[me], send_sems, recv_sems, k - 1, peer))
        for cp in copies:
            cp.start()
        for cp in copies:
            cp.wait()
        acc = gath[0]
        for d in range(1, N_DEV):
            acc = acc + gath[d]
        out_ref[...] = acc

    vm = pl.BlockSpec(memory_space=pltpu.VMEM)
    return pl.pallas_call(
        body, name="allreduce_stats", out_shape=jax.ShapeDtypeStruct((ST_ROWS, D_MODEL), F32),
        in_specs=[vm] * 4, out_specs=vm,
        scratch_shapes=[pltpu.VMEM((ST_ROWS, D_MODEL), F32), pltpu.VMEM((N_DEV, ST_ROWS, D_MODEL), F32),
                        pltpu.SemaphoreType.DMA((N_DEV - 1,)), pltpu.SemaphoreType.DMA((N_DEV - 1,))],
    )(st_mix, st_prep, st_mlp, st_ple)


def adamw_gains(stats, gains):
    c1 = 1.0 - ADAM_B1 ** ADAM_STEP
    c2 = 1.0 - ADAM_B2 ** ADAM_STEP
    n = len(gains)

    def body(st_ref, *refs):
        ins, outs = refs[:3 * n], refs[3 * n:]
        for i, (row, w, _, _) in enumerate(gains):
            width = w.shape[1]
            gv = st_ref[row:row + 1, 0:width]
            mn = ADAM_B1 * ins[3 * i + 1][...] + (1.0 - ADAM_B1) * gv
            vn = ADAM_B2 * ins[3 * i + 2][...] + (1.0 - ADAM_B2) * (gv * gv)
            outs[4 * i][...] = gv
            outs[4 * i + 1][...] = -ADAM_LR * ((mn / c1) / (jnp.sqrt(vn / c2) + ADAM_EPS) + ADAM_WD * ins[3 * i][...])
            outs[4 * i + 2][...] = mn
            outs[4 * i + 3][...] = vn

    vm = pl.BlockSpec(memory_space=pltpu.VMEM)
    flat = [a for (_, w, m, v) in gains for a in (w, m, v)]
    out_shape = tuple(jax.ShapeDtypeStruct(w.shape, F32) for (_, w, _, _) in gains for _ in range(4))
    res = pl.pallas_call(body, name="adamw_gains", out_shape=out_shape, in_specs=[vm] * (1 + 3 * n),
                         out_specs=tuple([vm] * (4 * n)))(stats, *flat)
    return [res[4 * i:4 * i + 4] for i in range(n)]


def _row_tile(r, cap=640):
    return max(d for d in range(16, min(r, cap) + 1, 16) if r % d == 0)


def add_pairs(g, got, core):
    n, r, w = got.shape
    tr = _row_tile(r)

    def body(c_ref, a_ref, b_ref, o_ref):
        o_ref[...] = (a_ref[...].astype(F32) + b_ref[...].astype(F32)).astype(o_ref.dtype)

    spec = pl.BlockSpec((1, tr, w), lambda i, j, c: (i, j, 0))
    return pl.pallas_call(
        body, name="add_pairs", out_shape=jax.ShapeDtypeStruct(got.shape, got.dtype),
        grid_spec=pltpu.PrefetchScalarGridSpec(
            num_scalar_prefetch=1, grid=(n, r // tr),
            in_specs=[pl.BlockSpec((1, tr, w), lambda i, j, c: (2 * i + c[0], j, 0)), spec], out_specs=spec),
        compiler_params=_params(("parallel", "parallel")),
    )(core, g, got)


def sum_chips(part, land, chip):
    _, r, w = part.shape
    tr = _row_tile(r)

    def body(c_ref, p_ref, l_ref, o_ref):
        acc = p_ref[0].astype(F32)
        for s in range(N_CHIP - 1):
            acc = acc + l_ref[s].astype(F32)
        o_ref[...] = acc

    return pl.pallas_call(
        body, name="sum_chips", out_shape=jax.ShapeDtypeStruct((r, w), F32),
        grid_spec=pltpu.PrefetchScalarGridSpec(
            num_scalar_prefetch=1, grid=(r // tr,),
            in_specs=[pl.BlockSpec((1, tr, w), lambda i, c: (c[0], i, 0)), pl.BlockSpec((N_CHIP - 1, tr, w), lambda i, c: (0, i, 0))],
            out_specs=pl.BlockSpec((tr, w), lambda i, c: (i, 0))),
        compiler_params=_params(("parallel",)),
    )(chip, part, land)


def in_proj(x, g_mix, w_in_t, tm):
    t = x.shape[0]
    nc = 512

    def body(x_ref, g_ref, w_ref, z_ref):
        xv = x_ref[...]
        h = (xv * _rstd(xv, D_MODEL) * g_ref[...]).astype(BF16)
        for cidx in range(ZP // nc):
            z_ref[:, cidx * nc:(cidx + 1) * nc] = _dot_nt(h, w_ref[cidx * nc:(cidx + 1) * nc, :])

    return pl.pallas_call(
        body, name="in_proj", grid=(t // tm,), out_shape=jax.ShapeDtypeStruct((t, ZP), F32),
        in_specs=[_rows(tm, D_MODEL), _resident((1, D_MODEL)), _resident((ZP, D_MODEL))],
        out_specs=_rows(tm, ZP), compiler_params=_params(("parallel",)),
    )(x, g_mix, w_in_t)


def attn_prep(zp, tabs, g_qa, g_kva, g_qn, g_kn, w_qb_t, w_kvb_t, tm, s_len):
    t = zp.shape[0]
    nsb = s_len // tm
    scale_a = (QK_NOPE + QK_ROPE) ** -0.5
    scale_b = HD_B ** -0.5

    def body(qb_ref, qlat_ref, kb_ref, vb_ref, ckv_ref, kpe_ref, tab_ref, gqa_ref, gkva_ref, gqn_ref, gkn_ref,
             wqb_ref, wkvb_ref, qa_o, ka_o, va_o, qb_o, kb_o, vb_o, cq_o, ckvn_o):
        ca, s1a, s2a = tab_ref[0], tab_ref[1], tab_ref[2]
        ck = tab_ref[3]
        cb, s1b, s2b = tab_ref[4], tab_ref[5], tab_ref[6]
        ql = qlat_ref[...]
        cq = (ql * _rstd(ql, Q_LORA) * gqa_ref[...]).astype(BF16)
        cq_o[...] = cq
        qa = _dot_nt(cq, wqb_ref[...])
        for h in range(H_A):
            sl = slice(h * HP, (h + 1) * HP)
            qa_o[:, sl] = (_rope_fwd(qa[:, sl], ca, s1a, s2a) * scale_a).astype(BF16)
        cr = ckv_ref[...]
        ckv = (cr * _rstd(cr, KV_LORA) * gkva_ref[...]).astype(BF16)
        ckvn_o[...] = ckv
        kva = _dot_nt(ckv, wkvb_ref[...])
        kpe = _rope_fwd(kpe_ref[...], ck, s1a, s2a)
        for h in range(H_A):
            sl = slice(h * HP, (h + 1) * HP)
            ka_o[:, sl] = (kva[:, sl] + kpe).astype(BF16)
        va_o[...] = kva[:, H_A * HP:].astype(BF16)
        gqn, gkn = gqn_ref[...], gkn_ref[...]
        for h in range(H_B):
            sl = slice(h * HP, (h + 1) * HP)
            xs = qb_ref[:, sl]
            y = xs * _rstd(xs, HD_B) * gqn
            qb_o[:, sl] = (_rope_fwd(y, cb, s1b, s2b) * scale_b).astype(BF16)
        for h in range(KV_B):
            sl = slice(h * HP, (h + 1) * HP)
            xs = kb_ref[:, sl]
            y = xs * _rstd(xs, HD_B) * gkn
            kb_o[:, sl] = _rope_fwd(y, cb, s1b, s2b).astype(BF16)
        vb_o[...] = vb_ref[...].astype(BF16)

    def o(width):
        return jax.ShapeDtypeStruct((t, width), BF16)

    return pl.pallas_call(
        body, name="attn_prep", grid=(t // tm,),
        out_shape=(o(H_A * HP), o(H_A * HP), o(H_A * HP), o(H_B * HP), o(KV_B * HP), o(KV_B * HP), o(Q_LORA), o(KV_LORA)),
        in_specs=[_rows(tm, 1024, 0), _rows(tm, 256, 12), _rows(tm, 256, 13), _rows(tm, 256, 14),
                  _rows(tm, 128, 30), _rows(tm, 128, 31),
                  pl.BlockSpec((7, tm, HP), lambda i: (0, i % nsb, 0)),
                  _resident((1, Q_LORA)), _resident((1, KV_LORA)), _resident((1, HP)), _resident((1, HP)),
                  _resident((H_A * HP, Q_LORA)), _resident((2 * H_A * HP, KV_LORA))],
        out_specs=(_rows(tm, H_A * HP), _rows(tm, H_A * HP), _rows(tm, H_A * HP), _rows(tm, H_B * HP),
                   _rows(tm, KV_B * HP), _rows(tm, KV_B * HP), _rows(tm, Q_LORA), _rows(tm, KV_LORA)),
        compiler_params=_params(("parallel",)),
    )(zp, zp, zp, zp, zp, zp, tabs, g_qa, g_kva, g_qn, g_kn, w_qb_t, w_kvb_t)


def attn_fwd(q, k, v, n_b, s_len, tq, name, comm=None):
    t = q.shape[0]
    n_h, n_hk = q.shape[1] // HP, k.shape[1] // HP
    grp = n_h // n_hk
    nq = s_len // tq
    sub = min(tq, 256)
    grid = (n_b, n_h, nq)
    c_ins, c_in_specs, c_outs, c_sems, alias = _comm_parts(comm, 3, 2)

    def body(*refs):
        (q_ref, k_ref, v_ref), cin, (o_ref, lse_ref), cout, _, csem = _split_refs(refs, 3, 2, 0, comm)
        _comm_start(comm, cin, cout, csem, grid)
        kv, vv = k_ref[...], v_ref[...]
        for r in range(tq // sub):
            rows = slice(r * sub, (r + 1) * sub)
            s = _dot_nt(q_ref[rows, :], kv)
            m = jnp.max(s, axis=-1, keepdims=True)
            p = jnp.exp(s - m)
            l = jnp.sum(p, axis=-1, keepdims=True)
            o_ref[rows, :] = (_dot_nn(p.astype(BF16), vv) * (1.0 / l)).astype(o_ref.dtype)
            lse_ref[rows, :] = jnp.broadcast_to(m + jnp.log(l), (sub, HP))
        _comm_finish(comm, cin, cout, csem, grid)

    qspec = pl.BlockSpec((tq, HP), lambda b, h, i: (b * nq + i, h))
    kspec = pl.BlockSpec((s_len, HP), lambda b, h, i: (b, h // grp))
    return pl.pallas_call(
        body, name=name, grid=grid,
        out_shape=(jax.ShapeDtypeStruct((t, n_h * HP), BF16), jax.ShapeDtypeStruct((t, n_h * HP), F32), *c_outs),
        in_specs=[qspec, kspec, kspec, *c_in_specs], out_specs=(qspec, qspec, *([_ANY] * len(c_outs))),
        scratch_shapes=c_sems, input_output_aliases=alias,
        compiler_params=_params(("arbitrary", "arbitrary", "arbitrary")),
    )(q, k, v, *c_ins)


def merge_fwd(oa, ob, zp, x, w_oa_t, w_ob_t, wpack, off, tm):
    t = x.shape[0]

    def body(oa_ref, ob_ref, ga_ref, gb_ref, x_ref, woa_ref, wob_ref, wo_ref, x1_o, mg_o):
        ya = _dot_nt(oa_ref[...], woa_ref[...])
        yb = _dot_nt(ob_ref[...], wob_ref[...])
        merged = (jax.nn.sigmoid(ga_ref[...]) * ya + jax.nn.sigmoid(gb_ref[...]) * yb).astype(BF16)
        mg_o[...] = merged
        x1_o[...] = x_ref[...] + _dot_nn(merged, _wrows(wo_ref, 0, D_MODEL))

    return pl.pallas_call(
        body, name="merge_fwd", grid=(t // tm,),
        out_shape=(jax.ShapeDtypeStruct((t, D_MODEL), F32), jax.ShapeDtypeStruct((t, D_MODEL), BF16)),
        in_specs=[_rows(tm, H_A * HP), _rows(tm, H_B * HP), _rows(tm, 1024, 1), _rows(tm, 1024, 2), _rows(tm, D_MODEL),
                  _resident((D_MODEL, H_A * HP)), _resident((D_MODEL, H_B * HP)), _packed_weight(128, off["w_o"])],
        out_specs=(_rows(tm, D_MODEL), _rows(tm, D_MODEL)), compiler_params=_params(("parallel",)),
    )(oa, ob, zp, zp, x, w_oa_t, w_ob_t, wpack)


def mlp_fwd(x1, g_mlp, wpack, off, tm):
    t = x1.shape[0]
    fc = 1024

    def body(x_ref, g_ref, wup_ref, wdn_ref, x2_o, u_o):
        xv = x_ref[...]
        h2 = (xv * _rstd(xv, D_MODEL) * g_ref[...]).astype(BF16)
        acc = xv
        for cidx in range(D_FF // fc):
            sl = slice(cidx * fc, (cidx + 1) * fc)
            u = jnp.maximum(_dot_nt(h2, _wrows(wup_ref, cidx * fc, fc)), 0.0)
            u_o[:, sl] = u.astype(BF16)
            acc = acc + _dot_nn((u * u).astype(BF16), _wrows(wdn_ref, cidx * fc, fc))
        x2_o[...] = acc

    return pl.pallas_call(
        body, name="mlp_fwd", grid=(t // tm,),
        out_shape=(jax.ShapeDtypeStruct((t, D_MODEL), F32), jax.ShapeDtypeStruct((t, D_FF), BF16)),
        in_specs=[_rows(tm, D_MODEL), _resident((1, D_MODEL)), _packed_weight(512, off["w_up"]), _packed_weight(512, off["w_down"])],
        out_specs=(_rows(tm, D_MODEL), _rows(tm, D_FF)), compiler_params=_params(("parallel",)),
    )(x1, g_mlp, wpack, wpack)


def ple_loss_bwd(x2, p, tgt, g_ple, g_final, wpack, off, w_ple_t, tm):
    t = x2.shape[0]
    inv_d = 1.0 / D_MODEL

    def body(x2_ref, p_ref, tg_ref, gp_ref, gf_ref, wpg_ref, wple_ref, dx2_o, dt_o, h3_o, dpe_o, st_o):
        @pl.when(pl.program_id(0) == 0)
        def _():
            st_o[...] = jnp.zeros_like(st_o)

        x2v = x2_ref[...]
        gp, gf = gp_ref[...], gf_ref[...]
        w_pg = _wrows(wpg_ref, 0, D_MODEL)
        r2 = _rstd(x2v, D_MODEL)
        xh2 = x2v * r2
        h3 = (xh2 * gp).astype(BF16)
        h3_o[...] = h3
        gate = jax.nn.sigmoid(_dot_nn(h3, w_pg))
        pe = _dot_nt(p_ref[...].astype(BF16), wple_ref[...])
        x3 = x2v + gate * pe
        r3 = _rstd(x3, D_MODEL)
        xh3 = x3 * r3
        err = xh3 * gf - tg_ref[...]
        dy = err * inv_d
        dx3 = _rms_bwd(dy, xh3, r3, gf, D_MODEL)
        dpe_o[...] = (dx3 * gate).astype(BF16)
        dt = (dx3 * pe * gate * (1.0 - gate)).astype(BF16)
        dt_o[...] = dt
        dh3 = _dot_nt(dt, w_pg)
        dx2_o[...] = dx3 + _rms_bwd(dh3, xh2, r2, gp, D_MODEL)
        st_o[0:1, :] += _colsum(dh3 * xh2)
        st_o[1:2, :] += _colsum(dy * xh3)
        st_o[2:3, :] += _colsum(err * err) * (0.5 * inv_d)

    bf = jax.ShapeDtypeStruct((t, D_MODEL), BF16)
    return pl.pallas_call(
        body, name="ple_loss_bwd", grid=(t // tm,),
        out_shape=(jax.ShapeDtypeStruct((t, D_MODEL), F32), bf, bf, bf, jax.ShapeDtypeStruct((3, D_MODEL), F32)),
        in_specs=[_rows(tm, D_MODEL), _rows(tm, PLE_DIM), _rows(tm, D_MODEL), _resident((1, D_MODEL)), _resident((1, D_MODEL)),
                  _packed_weight(128, off["w_ple_gate"]), _resident((D_MODEL, PLE_DIM))],
        out_specs=(_rows(tm, D_MODEL), _rows(tm, D_MODEL), _rows(tm, D_MODEL), _rows(tm, D_MODEL),
                   pl.BlockSpec((3, D_MODEL), lambda i: (0, 0))),
        compiler_params=_params(("arbitrary",)),
    )(x2, p, tgt, g_ple, g_final, wpack, w_ple_t)


def mlp_bwd(dx2, x1, u, g_mlp, wpack, off, tm):
    t = x1.shape[0]
    fc = 1024

    def body(dx2_ref, x1_ref, u_ref, g_ref, wup_ref, wdn_ref, dx1_o, da_o, h2_o, st_o):
        @pl.when(pl.program_id(0) == 0)
        def _():
            st_o[...] = jnp.zeros_like(st_o)

        d2 = dx2_ref[...]
        d2b = d2.astype(BF16)
        dh2 = jnp.zeros((tm, D_MODEL), F32)
        for cidx in range(D_FF // fc):
            sl = slice(cidx * fc, (cidx + 1) * fc)
            da = (_dot_nt(d2b, _wrows(wdn_ref, cidx * fc, fc)) * (2.0 * u_ref[:, sl].astype(F32))).astype(BF16)
            da_o[:, sl] = da
            dh2 = dh2 + _dot_nn(da, _wrows(wup_ref, cidx * fc, fc))
        xv = x1_ref[...]
        g = g_ref[...]
        r1 = _rstd(xv, D_MODEL)
        xh1 = xv * r1
        h2_o[...] = (xh1 * g).astype(BF16)
        st_o[...] += _colsum(dh2 * xh1)
        dx1_o[...] = d2 + _rms_bwd(dh2, xh1, r1, g, D_MODEL)

    return pl.pallas_call(
        body, name="mlp_bwd", grid=(t // tm,),
        out_shape=(jax.ShapeDtypeStruct((t, D_MODEL), F32), jax.ShapeDtypeStruct((t, D_FF), BF16),
                   jax.ShapeDtypeStruct((t, D_MODEL), BF16), jax.ShapeDtypeStruct((1, D_MODEL), F32)),
        in_specs=[_rows(tm, D_MODEL), _rows(tm, D_MODEL), _rows(tm, D_FF), _resident((1, D_MODEL)),
                  _packed_weight(512, off["w_up"]), _packed_weight(512, off["w_down"])],
        out_specs=(_rows(tm, D_MODEL), _rows(tm, D_FF), _rows(tm, D_MODEL), pl.BlockSpec((1, D_MODEL), lambda i: (0, 0))),
        compiler_params=_params(("arbitrary",)),
    )(dx2, x1, u, g_mlp, wpack, wpack)


def merge_bwd(dx1, oa, ob, zp, w_oa_t, w_ob_t, wpack, off, tm, comm=None):
    t = dx1.shape[0]
    grid = (t // tm,)
    c_ins, c_in_specs, c_outs, c_sems, alias = _comm_parts(comm, 8, 5)

    def body(*refs):
        ((dx1_ref, oa_ref, ob_ref, ga_ref, gb_ref, woa_ref, wob_ref, wo_ref), cin,
         (doa_o, dob_o, dg_o, dya_o, dyb_o), cout, _, csem) = _split_refs(refs, 8, 5, 0, comm)
        _comm_start(comm, cin, cout, csem, grid)
        dm = _dot_nt(dx1_ref[...].astype(BF16), _wrows(wo_ref, 0, D_MODEL))
        for o_ref, g_ref, w_ref, do_o, dy_o, col in ((oa_ref, ga_ref, woa_ref, doa_o, dya_o, 0),
                                                     (ob_ref, gb_ref, wob_ref, dob_o, dyb_o, 1)):
            yv = _dot_nt(o_ref[...], w_ref[...])
            sg = jax.nn.sigmoid(g_ref[...])
            dyv = (dm * sg).astype(BF16)
            dy_o[...] = dyv
            dg_o[:, col * D_MODEL:(col + 1) * D_MODEL] = (dm * yv * sg * (1.0 - sg)).astype(BF16)
            do_o[...] = _dot_nn(dyv, w_ref[...]).astype(BF16)
        _comm_finish(comm, cin, cout, csem, grid)

    bf = jax.ShapeDtypeStruct((t, D_MODEL), BF16)
    return pl.pallas_call(
        body, name="merge_bwd", grid=grid,
        out_shape=(bf, bf, jax.ShapeDtypeStruct((t, 2 * D_MODEL), BF16), bf, bf, *c_outs),
        in_specs=[_rows(tm, D_MODEL), _rows(tm, H_A * HP), _rows(tm, H_B * HP), _rows(tm, 1024, 1), _rows(tm, 1024, 2),
                  _resident((D_MODEL, H_A * HP)), _resident((D_MODEL, H_B * HP)), _packed_weight(128, off["w_o"]), *c_in_specs],
        out_specs=(_rows(tm, D_MODEL), _rows(tm, D_MODEL), _rows(tm, 2 * D_MODEL), _rows(tm, D_MODEL), _rows(tm, D_MODEL),
                   *([_ANY] * len(c_outs))),
        scratch_shapes=c_sems, input_output_aliases=alias,
        compiler_params=_params(("arbitrary",)),
    )(dx1, oa, ob, zp, zp, w_oa_t, w_ob_t, wpack, *c_ins)


def attn_bwd(q, k, v, do, o, lse, n_b, s_len, tq, name, comm=None):
    t = q.shape[0]
    n_h, n_hk = q.shape[1] // HP, k.shape[1] // HP
    grp = n_h // n_hk
    nq = s_len // tq
    sub = min(tq, 256)
    grid = (n_b, n_hk, grp, nq)
    c_ins, c_in_specs, c_outs, c_sems, alias = _comm_parts(comm, 6, 3)

    def body(*refs):
        ((q_ref, k_ref, v_ref, do_ref, o_ref, lse_ref), cin, (dq_o, dk_o, dv_o), cout, (p_s, ds_s),
         csem) = _split_refs(refs, 6, 3, 2, comm)
        _comm_start(comm, cin, cout, csem, grid)

        @pl.when((pl.program_id(2) == 0) & (pl.program_id(3) == 0))
        def _():
            dk_o[...] = jnp.zeros_like(dk_o)
            dv_o[...] = jnp.zeros_like(dv_o)

        kv, vv = k_ref[...], v_ref[...]
        for r in range(tq // sub):
            rows = slice(r * sub, (r + 1) * sub)
            qv, dov = q_ref[rows, :], do_ref[rows, :]
            delta = jnp.sum(dov.astype(F32) * o_ref[rows, :].astype(F32), axis=-1, keepdims=True)
            p = jnp.exp(_dot_nt(qv, kv) - lse_ref[rows, 0:1])
            ds = (p * (_dot_nt(dov, vv) - delta)).astype(BF16)
            p_s[rows, :] = p.astype(BF16)
            ds_s[rows, :] = ds
            dq_o[rows, :] = _dot_nn(ds, kv)
        dk_o[...] += _dot_tn(ds_s[...], q_ref[...])
        dv_o[...] += _dot_tn(p_s[...], do_ref[...])
        _comm_finish(comm, cin, cout, csem, grid)

    qspec = pl.BlockSpec((tq, HP), lambda b, hk, g, i: (b * nq + i, hk * grp + g))
    kspec = pl.BlockSpec((s_len, HP), lambda b, hk, g, i: (b, hk))
    return pl.pallas_call(
        body, name=name, grid=grid,
        out_shape=(jax.ShapeDtypeStruct((t, n_h * HP), F32), jax.ShapeDtypeStruct((t, n_hk * HP), F32),
                   jax.ShapeDtypeStruct((t, n_hk * HP), F32), *c_outs),
        in_specs=[qspec, kspec, kspec, qspec, qspec, qspec, *c_in_specs],
        out_specs=(qspec, kspec, kspec, *([_ANY] * len(c_outs))),
        scratch_shapes=[pltpu.VMEM((tq, s_len), BF16), pltpu.VMEM((tq, s_len), BF16), *c_sems],
        input_output_aliases=alias,
        compiler_params=_params(("arbitrary", "arbitrary", "arbitrary", "arbitrary")),
    )(q, k, v, do, o, lse, *c_ins)


def prep_bwd(dqa, dka, dva, dqb, dkb, dvb, zp, tabs, g_qa, g_kva, g_qn, g_kn, w_qb_t, w_kvb_t, tm, s_len):
    t = zp.shape[0]
    nsb = s_len // tm
    scale_a = (QK_NOPE + QK_ROPE) ** -0.5
    scale_b = HD_B ** -0.5

    def body(dqa_ref, dka_ref, dva_ref, dqb_ref, dkb_ref, dvb_ref, qb_ref, qlat_ref, kb_ref, ckv_ref, tab_ref,
             gqa_ref, gkva_ref, gqn_ref, gkn_ref, wqb_ref, wkvb_ref, dzq_o, dsm_o, dqap_o, dkva_o, st_o):
        @pl.when(pl.program_id(0) == 0)
        def _():
            st_o[...] = jnp.zeros_like(st_o)

        ca, s1a, s2a = tab_ref[0], tab_ref[1], tab_ref[2]
        ck = tab_ref[3]
        cb, s1b, s2b = tab_ref[4], tab_ref[5], tab_ref[6]
        for h in range(H_A):
            sl = slice(h * HP, (h + 1) * HP)
            dqap_o[:, sl] = _rope_bwd(dqa_ref[:, sl] * scale_a, ca, s1a, s2a).astype(BF16)
        dcq = _dot_nn(dqap_o[...], wqb_ref[...])
        ql = qlat_ref[...]
        rq = _rstd(ql, Q_LORA)
        xh = ql * rq
        gqa = gqa_ref[...]
        st_o[0:1, :] += _colsum(dcq * xh)
        dsm_o[:, 0:256] = _rms_bwd(dcq, xh, rq, gqa, Q_LORA).astype(BF16)
        dkpe = jnp.zeros((tm, HP), F32)
        for h in range(H_A):
            sl = slice(h * HP, (h + 1) * HP)
            dk = dka_ref[:, sl]
            dkpe = dkpe + dk
            dkva_o[:, sl] = dk.astype(BF16)
        dkva_o[:, H_A * HP:] = dva_ref[...].astype(BF16)
        dsm_o[:, 896:1024] = _rope_bwd(dkpe, ck, s1a, s2a).astype(BF16)
        dckv = _dot_nn(dkva_o[...], wkvb_ref[...])
        cr = ckv_ref[...]
        rk = _rstd(cr, KV_LORA)
        xh = cr * rk
        st_o[1:2, 0:128] += _colsum(dckv * xh)
        dsm_o[:, 768:896] = _rms_bwd(dckv, xh, rk, gkva_ref[...], KV_LORA).astype(BF16)
        gqn, gkn = gqn_ref[...], gkn_ref[...]
        dgq = jnp.zeros((1, HP), F32)
        for h in range(H_B):
            sl = slice(h * HP, (h + 1) * HP)
            dy = _rope_bwd(dqb_ref[:, sl] * scale_b, cb, s1b, s2b)
            xs = qb_ref[:, sl]
            r = _rstd(xs, HD_B)
            xh = xs * r
            dgq = dgq + _colsum(dy * xh)
            dzq_o[:, sl] = _rms_bwd(dy, xh, r, gqn, HD_B).astype(BF16)
        st_o[2:3, 0:128] += dgq
        dgk = jnp.zeros((1, HP), F32)
        for h in range(KV_B):
            sl = slice(h * HP, (h + 1) * HP)
            dy = _rope_bwd(dkb_ref[:, sl], cb, s1b, s2b)
            xs = kb_ref[:, sl]
            r = _rstd(xs, HD_B)
            xh = xs * r
            dgk = dgk + _colsum(dy * xh)
            dsm_o[:, 256 + h * HP:256 + (h + 1) * HP] = _rms_bwd(dy, xh, r, gkn, HD_B).astype(BF16)
        st_o[3:4, 0:128] += dgk
        dsm_o[:, 512:768] = dvb_ref[...].astype(BF16)

    bf = jax.ShapeDtypeStruct((t, 1024), BF16)
    return pl.pallas_call(
        body, name="prep_bwd", grid=(t // tm,),
        out_shape=(bf, bf, bf, jax.ShapeDtypeStruct((t, 2048), BF16), jax.ShapeDtypeStruct((4, 256), F32)),
        in_specs=[_rows(tm, 1024), _rows(tm, 1024), _rows(tm, 1024), _rows(tm, 1024), _rows(tm, 256), _rows(tm, 256),
                  _rows(tm, 1024, 0), _rows(tm, 256, 12), _rows(tm, 256, 13), _rows(tm, 128, 30),
                  pl.BlockSpec((7, tm, HP), lambda i: (0, i % nsb, 0)),
                  _resident((1, Q_LORA)), _resident((1, KV_LORA)), _resident((1, HP)), _resident((1, HP)),
                  _resident((H_A * HP, Q_LORA)), _resident((2 * H_A * HP, KV_LORA))],
        out_specs=(_rows(tm, 1024), _rows(tm, 1024), _rows(tm, 1024), _rows(tm, 2048), pl.BlockSpec((4, 256), lambda i: (0, 0))),
        compiler_params=_params(("arbitrary",)),
    )(dqa, dka, dva, dqb, dkb, dvb, zp, zp, zp, zp, tabs, g_qa, g_kva, g_qn, g_kn, w_qb_t, w_kvb_t)


def in_bwd(dzq, dgab, dsm, x, dx1, g_mix, w_in_t, tm):
    t = x.shape[0]

    def body(dzq_ref, dg_ref, dsm_ref, x_ref, dx1_ref, g_ref, w_ref, dx_o, h_o, st_o):
        @pl.when(pl.program_id(0) == 0)
        def _():
            st_o[...] = jnp.zeros_like(st_o)

        dh = _dot_nn(dzq_ref[...], w_ref[0:1024, :])
        dh = dh + _dot_nn(dg_ref[...], w_ref[1024:3072, :])
        dh = dh + _dot_nn(dsm_ref[...], w_ref[3072:4096, :])
        xv = x_ref[...]
        g = g_ref[...]
        r = _rstd(xv, D_MODEL)
        xh = xv * r
        h_o[...] = (xh * g).astype(BF16)
        st_o[...] += _colsum(dh * xh)
        dx_o[...] = dx1_ref[...] + _rms_bwd(dh, xh, r, g, D_MODEL)

    return pl.pallas_call(
        body, name="in_bwd", grid=(t // tm,),
        out_shape=(jax.ShapeDtypeStruct((t, D_MODEL), F32), jax.ShapeDtypeStruct((t, D_MODEL), BF16),
                   jax.ShapeDtypeStruct((1, D_MODEL), F32)),
        in_specs=[_rows(tm, 1024), _rows(tm, 2048), _rows(tm, 1024), _rows(tm, D_MODEL), _rows(tm, D_MODEL),
                  _resident((1, D_MODEL)), _resident((ZP, D_MODEL))],
        out_specs=(_rows(tm, D_MODEL), _rows(tm, D_MODEL), pl.BlockSpec((1, D_MODEL), lambda i: (0, 0))),
        compiler_params=_params(("arbitrary",)),
    )(dzq, dgab, dsm, x, dx1, g_mix, w_in_t)


def matmul_tn(a, b, name, square_a=False):
    t, m = a.shape
    n = b.shape[1]
    bm = min(m, 512)
    tk = min(t, 2048)

    def body(a_ref, b_ref, o_ref):
        @pl.when(pl.program_id(1) == 0)
        def _():
            o_ref[...] = jnp.zeros_like(o_ref)

        av = a_ref[...]
        if square_a:
            av = (av.astype(F32) * av.astype(F32))
        o_ref[...] += _dot_tn(av.astype(BF16), b_ref[...].astype(BF16))

    return pl.pallas_call(
        body, name=name, grid=(m // bm, t // tk), out_shape=jax.ShapeDtypeStruct((m, n), F32),
        in_specs=[pl.BlockSpec((tk, bm), lambda i, kk: (kk, i)), pl.BlockSpec((tk, n), lambda i, kk: (kk, 0))],
        out_specs=pl.BlockSpec((bm, n), lambda i, kk: (i, 0)),
        compiler_params=_params(("parallel", "arbitrary")),
    )(a, b)


def matmul_tn_packed(a, b, name, rows, row_off, total_rows, buf=None, square_a=False):
    t, m = a.shape
    n = b.shape[1]
    pd = max(1, 512 // rows)
    bm = pd * rows
    tk = min(t, 2048)
    nk = t // tk

    def body(a_ref, b_ref, *rest):
        o_ref, acc = rest[-2], rest[-1]

        @pl.when(pl.program_id(1) == 0)
        def _():
            acc[...] = jnp.zeros_like(acc)

        av = a_ref[...]
        if square_a:
            av = (av.astype(F32) * av.astype(F32))
        acc[...] += _dot_tn(av.astype(BF16), b_ref[...].astype(BF16))

        @pl.when(pl.program_id(1) == nk - 1)
        def _():
            o_ref[...] = acc[...].reshape(pd, rows, n).astype(o_ref.dtype)

    in_specs = [pl.BlockSpec((tk, bm), lambda i, kk: (kk, i)), pl.BlockSpec((tk, n), lambda i, kk: (kk, 0))]
    args = [a, b]
    if buf is not None:
        in_specs.append(_ANY)
        args.append(buf)
    return pl.pallas_call(
        body, name=name, grid=(m // bm, nk), out_shape=jax.ShapeDtypeStruct((N_DEV, total_rows, n), BF16),
        in_specs=in_specs, out_specs=pl.BlockSpec((pd, rows, n), lambda i, kk: (i, row_off // rows, 0)),
        scratch_shapes=[pltpu.VMEM((bm, n), F32)], input_output_aliases={2: 0} if buf is not None else {},
        compiler_params=_params(("parallel", "arbitrary")),
    )(*args)


def adamw(w, g, m, v, name):
    r, c = w.shape
    tr = r if r <= 256 else 256
    c1 = 1.0 - ADAM_B1 ** ADAM_STEP
    c2 = 1.0 - ADAM_B2 ** ADAM_STEP

    def body(w_ref, g_ref, m_ref, v_ref, d_o, m_o, v_o):
        gv = g_ref[...]
        mn = ADAM_B1 * m_ref[...] + (1.0 - ADAM_B1) * gv
        vn = ADAM_B2 * v_ref[...] + (1.0 - ADAM_B2) * (gv * gv)
        m_o[...] = mn
        v_o[...] = vn
        d_o[...] = -ADAM_LR * ((mn / c1) / (jnp.sqrt(vn / c2) + ADAM_EPS) + ADAM_WD * w_ref[...])

    spec = pl.BlockSpec((tr, c), lambda i: (i, 0))
    shp = jax.ShapeDtypeStruct((r, c), F32)
    return pl.pallas_call(
        body, name=name, grid=(r // tr,), out_shape=(shp, shp, shp), in_specs=[spec] * 4, out_specs=(spec,) * 3,
        compiler_params=_params(("parallel",)),
    )(w, g, m, v)


def _rope_tables(s_len):
    def angles(pos, dim):
        inv = np.float32(ROPE_THETA) ** (-np.arange(0, dim, 2, dtype=np.float32) / np.float32(dim))
        return pos.astype(np.float32)[:, None] * inv[None, :]

    tpos = np.arange(s_len)
    a1 = angles(tpos, QK_ROPE)
    ar = angles(tpos // GRID_W, HD_B // 2)
    ac = angles(tpos % GRID_W, HD_B // 2)
    z16 = np.zeros((s_len, 16), np.float32)
    z32 = np.zeros((s_len, 32), np.float32)
    z64 = np.zeros((s_len, 64), np.float32)
    one64 = np.ones((s_len, 64), np.float32)
    c1, s1 = np.cos(a1), np.sin(a1)
    ca = np.concatenate([one64, c1, c1, z32], axis=1)
    ck = np.concatenate([z64, c1, c1, z32], axis=1)
    s1a = np.concatenate([z64, -s1, z16, z32], axis=1)
    s2a = np.concatenate([z64, z16, s1, z32], axis=1)
    cr, sr, cc, sc = np.cos(ar), np.sin(ar), np.cos(ac), np.sin(ac)
    cb = np.concatenate([cr, cr, cc, cc, z64], axis=1)
    s1b = np.concatenate([-sr, z16, -sc, z16, z64], axis=1)
    s2b = np.concatenate([z16, sr, z16, sc, z64], axis=1)
    return jnp.asarray(np.stack([ca, s1a, s2a, ck, cb, s1b, s2b]).astype(np.float32))


def _pad_heads(a, n_heads, axis):
    shp = a.shape
    a = a.reshape(shp[:axis] + (n_heads, shp[axis] // n_heads) + shp[axis + 1:])
    pad = [(0, 0)] * a.ndim
    pad[axis + 1] = (0, HP - a.shape[axis + 1])
    a = jnp.pad(a, pad)
    return a.reshape(shp[:axis] + (n_heads * HP,) + shp[axis + 1:])


def _unpad_heads(a, n_heads, width, axis):
    shp = a.shape
    a = a.reshape(shp[:axis] + (n_heads, HP) + shp[axis + 1:])
    a = lax.slice_in_dim(a, 0, width, axis=axis + 1)
    return a.reshape(shp[:axis] + (n_heads * width,) + shp[axis + 1:])


def _pack_rows(blocks, names):
    parts = []
    for name in names:
        b = blocks[name]
        padr = PACK_ROWS[name] - b.shape[-2]
        if padr:
            b = jnp.pad(b, [(0, 0)] * (b.ndim - 2) + [(0, padr), (0, 0)])
        parts.append(b)
    return jnp.concatenate(parts, axis=parts[0].ndim - 2)


def _expand_w_in(wt):
    z64 = jnp.zeros((64, D_MODEL), wt.dtype)
    z32 = jnp.zeros((32, D_MODEL), wt.dtype)
    return jnp.concatenate([
        _pad_heads(wt[416:928], H_B, 0), wt[1184:2208], wt[2208:3232], wt[0:256],
        _pad_heads(wt[928:1056], KV_B, 0), _pad_heads(wt[1056:1184], KV_B, 0), wt[256:384],
        z64, wt[384:416], z32], axis=0)


def _collapse_w_in(dq, dg, ds):
    return jnp.concatenate([
        ds[0:256], ds[768:896], ds[960:992], _unpad_heads(dq, H_B, HD_B, 0), _unpad_heads(ds[256:512], KV_B, HD_B, 0),
        _unpad_heads(ds[512:768], KV_B, HD_B, 0), dg], axis=0)


def kernel(x, p, g_mix, w_in, g_qa, w_qb, g_kva, w_kvb, g_qn, g_kn, w_oa, w_ob, w_o, g_mlp, w_up, w_down, g_ple, w_ple_gate, w_ple, g_final, loss_target, m_g_mix, m_w_in, m_g_qa, m_w_qb, m_g_kva, m_w_kvb, m_g_qn, m_g_kn, m_w_oa, m_w_ob, m_w_o, m_g_mlp, m_w_up, m_w_down, m_g_ple, m_w_ple_gate, m_w_ple, m_g_final, v_g_mix, v_w_in, v_g_qa, v_w_qb, v_g_kva, v_w_kvb, v_g_qn, v_g_kn, v_w_oa, v_w_ob, v_w_o, v_g_mlp, v_w_up, v_w_down, v_g_ple, v_w_ple_gate, v_w_ple, v_g_final):
    n_b, s_len, _ = x.shape
    t = n_b * s_len
    tm = min(256, s_len)
    tq_f = min(512, s_len)
    tq_b = min(512, s_len)

    mats = dict(w_in=(w_in, m_w_in, v_w_in), w_qb=(w_qb, m_w_qb, v_w_qb), w_kvb=(w_kvb, m_w_kvb, v_w_kvb),
                w_oa=(w_oa, m_w_oa, v_w_oa), w_ob=(w_ob, m_w_ob, v_w_ob), w_o=(w_o, m_w_o, v_w_o),
                w_up=(w_up, m_w_up, v_w_up), w_down=(w_down, m_w_down, v_w_down),
                w_ple_gate=(w_ple_gate, m_w_ple_gate, v_w_ple_gate), w_ple=(w_ple, m_w_ple, v_w_ple))
    col_sharded = ("w_in", "w_qb", "w_kvb", "w_oa", "w_ob", "w_up", "w_ple")

    blocks = {}
    for name, (w, _, _) in mats.items():
        w2 = w[0]
        if name in col_sharded:
            w2 = w2.T
        blocks[name] = w2.reshape(-1, D_MODEL).astype(BF16)
    off_w1, _ = _pack_offsets(PACK_W1)
    off_w2, _ = _pack_offsets(PACK_W2)
    full1 = allgather_rows(_pack_rows(blocks, PACK_W1))
    pack2 = _pack_rows(blocks, PACK_W2)

    def gathered(full, offs, name, rows, width):
        return full[:, offs[name]:offs[name] + rows].reshape(-1, width)

    w_in_t = _expand_w_in(gathered(full1, off_w1, "w_in", 404, D_MODEL))
    w_qb_t = _pad_heads(gathered(full1, off_w1, "w_qb", 24, Q_LORA), H_A, 0)
    wkvb = gathered(full1, off_w1, "w_kvb", 16, KV_LORA).reshape(H_A, 2, 64, KV_LORA)
    w_kvb_t = jnp.concatenate([_pad_heads(wkvb[:, 0].reshape(-1, KV_LORA), H_A, 0),
                               _pad_heads(wkvb[:, 1].reshape(-1, KV_LORA), H_A, 0)], axis=0)

    tabs = _rope_tables(s_len)
    g_qn_p = jnp.pad(g_qn, ((0, 0), (0, HP - HD_B)))
    g_kn_p = jnp.pad(g_kn, ((0, 0), (0, HP - HD_B)))
    xf = x.reshape(t, D_MODEL)
    pf = p.reshape(t, PLE_DIM)
    tgt = loss_target.reshape(t, D_MODEL)

    zp = in_proj(xf, g_mix, w_in_t, tm)
    qa, ka, va, qb, kb, vb, cq, ckv = attn_prep(zp, tabs, g_qa, g_kva, g_qn_p, g_kn_p, w_qb_t, w_kvb_t, tm, s_len)
    oa, lse_a, full2 = attn_fwd(qa, ka, va, n_b, s_len, tq_f, "attn_a_fwd", comm=gather_first_comm(pack2))
    ob, lse_b, full2 = attn_fwd(qb, kb, vb, n_b, s_len, tq_f, "attn_b_fwd", comm=gather_pass_comm(full2))
    w_oa_t = _pad_heads(gathered(full2, off_w2, "w_oa", 64, H_A * V_DIM_A), H_A, 1)
    w_ob_t = _pad_heads(gathered(full2, off_w2, "w_ob", 64, H_B * HD_B), H_B, 1)
    w_ple_t = gathered(full2, off_w2, "w_ple", 32, PLE_DIM)
    x1, merged = merge_fwd(oa, ob, zp, xf, w_oa_t, w_ob_t, full2, off_w2, tm)
    x2, u = mlp_fwd(x1, g_mlp, full2, off_w2, tm)
    dx2, dt, h3, dpe, st_ple = ple_loss_bwd(x2, pf, tgt, g_ple, g_final.reshape(1, D_MODEL), full2, off_w2, w_ple_t, tm)
    dx1, da, h2, st_mlp = mlp_bwd(dx2, x1, u, g_mlp, full2, off_w2, tm)

    core = lax.axis_index("c").astype(jnp.int32).reshape(1)
    chip = (2 * lax.axis_index("x") + lax.axis_index("y")).astype(jnp.int32).reshape(1)

    def packed(gblocks, names):
        return _pack_rows({n: gblocks[n].reshape(N_DEV, -1, D_MODEL).astype(BF16) for n in names}, names)

    off_g1, rows_g1 = _pack_offsets(PACK_G1)
    gpack1 = matmul_tn_packed(da, h2, "gw_up", 512, off_g1["w_up"], rows_g1)
    gpack1 = matmul_tn_packed(u, dx2, "gw_down", 512, off_g1["w_down"], rows_g1, buf=gpack1, square_a=True)
    gpack1 = matmul_tn_packed(h3, dt, "gw_pg", 128, off_g1["w_ple_gate"], rows_g1, buf=gpack1)
    gple = matmul_tn(dpe, pf, "gw_ple").reshape(N_DEV, -1, D_MODEL).astype(BF16)
    gpack1 = lax.dynamic_update_slice(gpack1, gple, (0, off_g1["w_ple"], 0))
    doa, dob, dgab, dya, dyb, got1 = merge_bwd(dx1, oa, ob, zp, w_oa_t, w_ob_t, full2, off_w2, tm,
                                               comm=scatter_sibling_comm(gpack1))
    part1 = add_pairs(gpack1, got1, core)
    dqa, dka, dva, land1 = attn_bwd(qa, ka, va, doa, oa, lse_a, n_b, s_len, tq_b, "attn_a_bwd", comm=scatter_chips_comm(part1))
    gshard1 = sum_chips(part1, land1, chip)
    dqb, dkb, dvb = attn_bwd(qb, kb, vb, dob, ob, lse_b, n_b, s_len, tq_b, "attn_b_bwd")
    dzq, dsm, dqap, dkva, st_prep = prep_bwd(dqa, dka, dva, dqb, dkb, dvb, zp, tabs, g_qa, g_kva, g_qn_p, g_kn_p,
                                             w_qb_t, w_kvb_t, tm, s_len)
    grad_x, h, st_mix = in_bwd(dzq, dgab, dsm, xf, dx1, g_mix, w_in_t, tm)

    gkv = matmul_tn(dkva, ckv, "gw_kvb")
    g2 = dict(
        w_in=_collapse_w_in(matmul_tn(dzq, h, "gw_in_q"), matmul_tn(dgab, h, "gw_in_g"), matmul_tn(dsm, h, "gw_in_s")),
        w_qb=_unpad_heads(matmul_tn(dqap, cq, "gw_qb"), H_A, QK_NOPE + QK_ROPE, 0),
        w_kvb=jnp.stack([_unpad_heads(gkv[:H_A * HP], H_A, 64, 0).reshape(H_A, 64, KV_LORA),
                         _unpad_heads(gkv[H_A * HP:], H_A, 64, 0).reshape(H_A, 64, KV_LORA)], axis=1),
        w_oa=_unpad_heads(matmul_tn(dya, oa, "gw_oa"), H_A, V_DIM_A, 1),
        w_ob=_unpad_heads(matmul_tn(dyb, ob, "gw_ob"), H_B, HD_B, 1),
        w_o=matmul_tn(merged, dx1, "gw_o"))
    gpack2 = packed(g2, PACK_G2)
    part2 = add_pairs(gpack2, exchange_sibling(gpack2), core)
    gshard2 = sum_chips(part2, exchange_chips(part2), chip)
    off_g1, _ = _pack_offsets(PACK_G1)
    off_g2, _ = _pack_offsets(PACK_G2)

    stats = allreduce_stats(st_mix, st_prep, st_mlp, st_ple)
    loss = jnp.sum(stats[ST_LOSS])

    out_g, out_d, out_m, out_v = {}, {}, {}, {}
    for name, (w, m, v) in mats.items():
        gshard, off = (gshard1, off_g1[name]) if name in PACK_G1 else (gshard2, off_g2[name])
        shard_shape = w.shape[1:]
        if name in col_sharded:
            rows_t = shard_shape[1]
            g2 = gshard[off:off + (rows_t * shard_shape[0]) // D_MODEL].reshape(rows_t, shard_shape[0]).T
        else:
            g2 = gshard[off:off + shard_shape[0]]
        d2, m2, v2 = adamw(w[0], g2, m[0], v[0], "adamw_" + name)
        out_g[name], out_d[name], out_m[name], out_v[name] = g2[None], d2[None], m2[None], v2[None]

    gains = (("g_mix", g_mix, m_g_mix, v_g_mix, ST_G_MIX), ("g_qa", g_qa, m_g_qa, v_g_qa, ST_G_QA),
             ("g_kva", g_kva, m_g_kva, v_g_kva, ST_G_KVA), ("g_qn", g_qn, m_g_qn, v_g_qn, ST_G_QN),
             ("g_kn", g_kn, m_g_kn, v_g_kn, ST_G_KN), ("g_mlp", g_mlp, m_g_mlp, v_g_mlp, ST_G_MLP),
             ("g_ple", g_ple, m_g_ple, v_g_ple, ST_G_PLE), ("g_final", g_final, m_g_final, v_g_final, ST_G_FINAL))
    res = adamw_gains(stats, [(r_, w.reshape(1, -1), m.reshape(1, -1), v.reshape(1, -1)) for _, w, m, v, r_ in gains])
    for (name, w, _, _, _), (gg, gd, gm, gv) in zip(gains, res):
        out_g[name], out_d[name], out_m[name], out_v[name] = (a.reshape(w.shape) for a in (gg, gd, gm, gv))

    order = ("g_mix", "w_in", "g_qa", "w_qb", "g_kva", "w_kvb", "g_qn", "g_kn", "w_oa", "w_ob", "w_o", "g_mlp",
             "w_up", "w_down", "g_ple", "w_ple_gate", "w_ple", "g_final")
    return (loss, grad_x.reshape(x.shape), *[out_g[n] for n in order], *[out_d[n] for n in order],
            *[out_m[n] for n in order], *[out_v[n] for n in order])
```

```python
import numpy as np
import jax
import jax.numpy as jnp
from jax import lax
from jax.experimental import pallas as pl
from jax.experimental.pallas import tpu as pltpu

F32 = jnp.float32
BF16 = jnp.bfloat16

D_MODEL = 1024
EPS = 1e-6
ROPE_THETA = 10000.0
GRID_W = 64
H_A = 8
QK_NOPE = 64
QK_ROPE = 32
V_DIM_A = 64
Q_LORA = 256
KV_LORA = 128
H_B = 8
KV_B = 2
HD_B = 64
D_FF = 4 * D_MODEL
PLE_DIM = 256
HP = 128
ZP = 4096
N_DEV = 8
N_CHIP = 4

ADAM_LR = 0.001
ADAM_B1 = 0.9
ADAM_B2 = 0.999
ADAM_EPS = 1e-08
ADAM_WD = 0.01
ADAM_STEP = 10

VMEM_LIMIT = 52 * 1024 * 1024

PACK_ROWS = dict(w_in=416, w_qb=32, w_kvb=16, w_oa=64, w_ob=64, w_o=128, w_up=512, w_down=512, w_ple_gate=128, w_ple=32)
PACK_W1 = ("w_in", "w_qb", "w_kvb")
PACK_W2 = ("w_up", "w_down", "w_o", "w_ple_gate", "w_oa", "w_ob", "w_ple")
PACK_G1 = ("w_up", "w_down", "w_ple_gate", "w_ple")
PACK_G2 = ("w_o", "w_oa", "w_ob")
PACK_G3 = ("w_in", "w_qb", "w_kvb")


def _pack_offsets(names):
    off, o = {}, 0
    for n in names:
        off[n] = o
        o += PACK_ROWS[n]
    return off, o

ST_G_MIX, ST_G_QA, ST_G_KVA, ST_G_QN, ST_G_KN, ST_G_MLP, ST_G_PLE, ST_G_FINAL, ST_LOSS = range(9)
ST_ROWS = 16


def _dot_nn(a, b):
    return lax.dot_general(a, b, (((1,), (0,)), ((), ())), preferred_element_type=F32)


def _dot_nt(a, b):
    return lax.dot_general(a, b, (((1,), (1,)), ((), ())), preferred_element_type=F32)


def _dot_tn(a, b):
    return lax.dot_general(a, b, (((0,), (0,)), ((), ())), preferred_element_type=F32)


def _rstd(x, n):
    return lax.rsqrt(jnp.sum(x * x, axis=-1, keepdims=True) * (1.0 / n) + EPS)


def _rms_bwd(dy, xh, r, g, n):
    dxh = dy * g
    return r * (dxh - xh * (jnp.sum(dxh * xh, axis=-1, keepdims=True) * (1.0 / n)))


def _rope_fwd(x, c, s1, s2):
    return x * c + pltpu.roll(x, HP - 16, 1) * s1 + pltpu.roll(x, 16, 1) * s2


def _rope_bwd(d, c, s1, s2):
    return d * c + pltpu.roll(d * s1, 16, 1) + pltpu.roll(d * s2, HP - 16, 1)


def _colsum(v):
    return jnp.sum(v, axis=0, keepdims=True)


def _params(sem=None, vmem=VMEM_LIMIT):
    return pltpu.CompilerParams(dimension_semantics=sem, vmem_limit_bytes=vmem)


def _resident(shape):
    nd = len(shape)
    return pl.BlockSpec(shape, lambda *_: (0,) * nd, pipeline_mode=pl.Buffered(1))


def _rows(tm, width, col=0):
    return pl.BlockSpec((tm, width), lambda i: (i, col))


def _packed_weight(rows, off):
    return pl.BlockSpec((N_DEV, rows, D_MODEL), lambda *_: (0, off // rows, 0), pipeline_mode=pl.Buffered(1))


def _wrows(ref, start, size):
    rows = ref.shape[1]
    return ref[start // rows:(start + size) // rows].reshape(size, D_MODEL)


def _mesh_pos():
    return lax.axis_index("x"), lax.axis_index("y"), lax.axis_index("c")


def _flip(v, bit):
    return (1 - v) if bit else v


_ANY = pl.BlockSpec(memory_space=pl.ANY)
_MESH = pl.DeviceIdType.MESH


def _remote(src, dst, send_sems, recv_sems, k, to):
    return pltpu.make_async_remote_copy(src_ref=src, dst_ref=dst, send_sem=send_sems.at[k], recv_sem=recv_sems.at[k],
                                        device_id=to, device_id_type=_MESH)


def _sibling_copies(g_ref, got_ref, send_sems, recv_sems):
    x, y, c = _mesh_pos()
    return [_remote(g_ref.at[2 * j + (1 - c)], got_ref.at[j], send_sems, recv_sems, j, (x, y, 1 - c)) for j in range(N_CHIP)]


def _chip_copies(p_ref, land_ref, send_sems, recv_sems):
    x, y, c = _mesh_pos()
    copies = []
    for k in (1, 2, 3):
        tx, ty = _flip(x, k & 2), _flip(y, k & 1)
        copies.append(_remote(p_ref.at[2 * tx + ty], land_ref.at[k - 1], send_sems, recv_sems, k - 1, (tx, ty, c)))
    return copies


class _Comm:
    def __init__(self, ins, out_shapes, sems, make, aliases=None):
        self.ins, self.out_shapes, self.sems, self.make, self.aliases = list(ins), list(out_shapes), list(sems), make, aliases or {}


def _comm_parts(comm, n_in, n_out):
    if comm is None:
        return [], [], [], [], {}
    alias = {n_in + j: n_out + k for j, k in comm.aliases.items()}
    return comm.ins, [_ANY] * len(comm.ins), comm.out_shapes, comm.sems, alias


def _split_refs(refs, n_in, n_out, n_scratch, comm):
    n_ci = len(comm.ins) if comm else 0
    n_co = len(comm.out_shapes) if comm else 0
    cuts, i = [], 0
    for n in (n_in, n_ci, n_out, n_co, n_scratch):
        cuts.append(refs[i:i + n])
        i += n
    return (*cuts, refs[i:])


def _grid_edge(grid, last):
    cond = None
    for d, n in enumerate(grid):
        here = pl.program_id(d) == (n - 1 if last else 0)
        cond = here if cond is None else cond & here
    return cond


def _comm_start(comm, cin, cout, csem, grid):
    if comm is not None:
        @pl.when(_grid_edge(grid, False))
        def _():
            for cp in comm.make(cin, cout, csem):
                cp.start()


def _comm_finish(comm, cin, cout, csem, grid):
    if comm is not None:
        @pl.when(_grid_edge(grid, True))
        def _():
            for cp in comm.make(cin, cout, csem):
                cp.wait()


def gather_first_comm(shard):
    r, w = shard.shape

    def make(cin, cout, sems):
        (x_ref,), (out_ref,), (send_sems, recv_sems, local_sem) = cin, cout, sems
        x, y, c = _mesh_pos()
        mine = out_ref.at[4 * x + 2 * y + c]
        targets = [(x, y, 1 - c), (1 - x, y, c), (x, 1 - y, c), (1 - x, 1 - y, c)]
        return [_remote(x_ref, mine, send_sems, recv_sems, k, to) for k, to in enumerate(targets)] + [
            pltpu.make_async_copy(x_ref, mine, local_sem)]

    return _Comm([shard], [jax.ShapeDtypeStruct((N_DEV, r, w), shard.dtype)],
                 [pltpu.SemaphoreType.DMA((4,)), pltpu.SemaphoreType.DMA((4,)), pltpu.SemaphoreType.DMA], make)


def gather_pass_comm(full):
    def make(cin, cout, sems):
        (in_ref,), (out_ref,), (send_sems, recv_sems) = cin, cout, sems
        x, y, c = _mesh_pos()
        copies = []
        for k, (px, py) in enumerate([(1 - x, y), (x, 1 - y), (1 - x, 1 - y)]):
            idx = 4 * px + 2 * py + c
            copies.append(_remote(in_ref.at[idx], out_ref.at[idx], send_sems, recv_sems, k, (x, y, 1 - c)))
        return copies

    return _Comm([full], [jax.ShapeDtypeStruct(full.shape, full.dtype)],
                 [pltpu.SemaphoreType.DMA((3,)), pltpu.SemaphoreType.DMA((3,))], make, aliases={0: 0})


def scatter_sibling_comm(g):
    _, r, w = g.shape
    return _Comm([g], [jax.ShapeDtypeStruct((N_CHIP, r, w), g.dtype)],
                 [pltpu.SemaphoreType.DMA((N_CHIP,)), pltpu.SemaphoreType.DMA((N_CHIP,))],
                 lambda cin, cout, sems: _sibling_copies(cin[0], cout[0], sems[0], sems[1]))


def scatter_chips_comm(part):
    _, r, w = part.shape
    return _Comm([part], [jax.ShapeDtypeStruct((N_CHIP - 1, r, w), part.dtype)],
                 [pltpu.SemaphoreType.DMA((3,)), pltpu.SemaphoreType.DMA((3,))],
                 lambda cin, cout, sems: _chip_copies(cin[0], cout[0], sems[0], sems[1]))


def allgather_rows(shard):
    r, w = shard.shape

    def body(x_ref, out_ref, send_sems, recv_sems, local_sem):
        x, y, c = _mesh_pos()
        me, sibling = (x, y, c), (x, y, 1 - c)
        chips = [(1 - x, y), (x, 1 - y), (1 - x, 1 - y)]

        def slot(px, py, pc):
            return out_ref.at[4 * px + 2 * py + pc]

        def copy(k, block, to, src=None):
            return pltpu.make_async_remote_copy(
                src_ref=slot(*block) if src is None else src, dst_ref=slot(*block),
                send_sem=send_sems.at[k], recv_sem=recv_sems.at[k], device_id=to, device_id_type=_MESH)

        mine = pltpu.make_async_copy(x_ref, slot(*me), local_sem)
        mine.start()
        first = [copy(0, me, sibling, src=x_ref)]
        first += [copy(1 + j, me, (*chip, c), src=x_ref) for j, chip in enumerate(chips)]
        for cp in first:
            cp.start()
        passed = [copy(4 + j, (*chip, c), sibling) for j, chip in enumerate(chips)]
        for j, chip in enumerate(chips):
            copy(1 + j, (*chip, c), me).wait_recv()
            passed[j].start()
        copy(0, sibling, me).wait_recv()
        for j, chip in enumerate(chips):
            copy(4 + j, (*chip, 1 - c), me).wait_recv()
        for cp in first + passed:
            cp.wait_send()
        mine.wait()

    return pl.pallas_call(
        body, name="allgather_rows",
        out_shape=jax.ShapeDtypeStruct((N_DEV, r, w), shard.dtype),
        in_specs=[_ANY], out_specs=_ANY,
        scratch_shapes=[pltpu.SemaphoreType.DMA((7,)), pltpu.SemaphoreType.DMA((7,)), pltpu.SemaphoreType.DMA],
    )(shard)


def exchange_sibling(g):
    _, r, w = g.shape

    def body(g_ref, got_ref, send_sems, recv_sems):
        copies = _sibling_copies(g_ref, got_ref, send_sems, recv_sems)
        for cp in copies:
            cp.start()
        for cp in copies:
            cp.wait()

    return pl.pallas_call(
        body, name="exchange_sibling", out_shape=jax.ShapeDtypeStruct((N_CHIP, r, w), g.dtype),
        in_specs=[_ANY], out_specs=_ANY,
        scratch_shapes=[pltpu.SemaphoreType.DMA((N_CHIP,)), pltpu.SemaphoreType.DMA((N_CHIP,))],
    )(g)


def exchange_chips(part):
    _, r, w = part.shape

    def body(p_ref, land_ref, send_sems, recv_sems):
        copies = _chip_copies(p_ref, land_ref, send_sems, recv_sems)
        for cp in copies:
            cp.start()
        for cp in copies:
            cp.wait()

    return pl.pallas_call(
        body, name="exchange_chips", out_shape=jax.ShapeDtypeStruct((N_CHIP - 1, r, w), part.dtype),
        in_specs=[_ANY], out_specs=_ANY,
        scratch_shapes=[pltpu.SemaphoreType.DMA((3,)), pltpu.SemaphoreType.DMA((3,))],
    )(part)


def allreduce_stats(st_mix, st_prep, st_mlp, st_ple):
    def body(mix_ref, prep_ref, mlp_ref, ple_ref, out_ref, mine, gath, send_sems, recv_sems):
        x, y, c = _mesh_pos()
        me = 4 * x + 2 * y + c
        mine[...] = jnp.zeros_like(mine)
        mine[ST_G_MIX:ST_G_MIX + 1, :] = mix_ref[...]
        mine[ST_G_QA:ST_G_KN + 1, 0:256] = prep_ref[...]
        mine[ST_G_MLP:ST_G_MLP + 1, :] = mlp_ref[...]
        mine[ST_G_PLE:ST_LOSS + 1, :] = ple_ref[...]
        gath[me] = mine[...]
        copies = []
        for k in range(1, N_DEV):
            peer = (_flip(x, k & 4), _flip(y, k & 2), _flip(c, k & 1))
            copies.append(_remote(mine, gath.at[me], send_sems, recv_sems, k - 1, peer))
        for cp in copies:
            cp.start()
        for cp in copies:
            cp.wait()
        acc = gath[0]
        for d in range(1, N_DEV):
            acc = acc + gath[d]
        out_ref[...] = acc

    vm = pl.BlockSpec(memory_space=pltpu.VMEM)
    return pl.pallas_call(
        body, name="allreduce_stats", out_shape=jax.ShapeDtypeStruct((ST_ROWS, D_MODEL), F32),
        in_specs=[vm] * 4, out_specs=vm,
        scratch_shapes=[pltpu.VMEM((ST_ROWS, D_MODEL), F32), pltpu.VMEM((N_DEV, ST_ROWS, D_MODEL), F32),
                        pltpu.SemaphoreType.DMA((N_DEV - 1,)), pltpu.SemaphoreType.DMA((N_DEV - 1,))],
    )(st_mix, st_prep, st_mlp, st_ple)


def adamw_gains(stats, gains):
    c1 = 1.0 - ADAM_B1 ** ADAM_STEP
    c2 = 1.0 - ADAM_B2 ** ADAM_STEP
    n = len(gains)

    def body(st_ref, *refs):
        ins, outs = refs[:3 * n], refs[3 * n:]
        for i, (row, w, _, _) in enumerate(gains):
            width = w.shape[1]
            gv = st_ref[row:row + 1, 0:width]
            mn = ADAM_B1 * ins[3 * i + 1][...] + (1.0 - ADAM_B1) * gv
            vn = ADAM_B2 * ins[3 * i + 2][...] + (1.0 - ADAM_B2) * (gv * gv)
            outs[4 * i][...] = gv
            outs[4 * i + 1][...] = -ADAM_LR * ((mn / c1) / (jnp.sqrt(vn / c2) + ADAM_EPS) + ADAM_WD * ins[3 * i][...])
            outs[4 * i + 2][...] = mn
            outs[4 * i + 3][...] = vn

    vm = pl.BlockSpec(memory_space=pltpu.VMEM)
    flat = [a for (_, w, m, v) in gains for a in (w, m, v)]
    out_shape = tuple(jax.ShapeDtypeStruct(w.shape, F32) for (_, w, _, _) in gains for _ in range(4))
    res = pl.pallas_call(body, name="adamw_gains", out_shape=out_shape, in_specs=[vm] * (1 + 3 * n),
                         out_specs=tuple([vm] * (4 * n)))(stats, *flat)
    return [res[4 * i:4 * i + 4] for i in range(n)]


def _row_tile(r, cap=640):
    return max(d for d in range(16, min(r, cap) + 1, 16) if r % d == 0)


def add_pairs(g, got, core):
    n, r, w = got.shape
    tr = _row_tile(r)

    def body(c_ref, a_ref, b_ref, o_ref):
        o_ref[...] = (a_ref[...].astype(F32) + b_ref[...].astype(F32)).astype(o_ref.dtype)

    spec = pl.BlockSpec((1, tr, w), lambda i, j, c: (i, j, 0))
    return pl.pallas_call(
        body, name="add_pairs", out_shape=jax.ShapeDtypeStruct(got.shape, got.dtype),
        grid_spec=pltpu.PrefetchScalarGridSpec(
            num_scalar_prefetch=1, grid=(n, r // tr),
            in_specs=[pl.BlockSpec((1, tr, w), lambda i, j, c: (2 * i + c[0], j, 0)), spec], out_specs=spec),
        compiler_params=_params(("parallel", "parallel")),
    )(core, g, got)


def sum_chips(part, land, chip):
    _, r, w = part.shape
    tr = _row_tile(r)

    def body(c_ref, p_ref, l_ref, o_ref):
        acc = p_ref[0].astype(F32)
        for s in range(N_CHIP - 1):
            acc = acc + l_ref[s].astype(F32)
        o_ref[...] = acc

    return pl.pallas_call(
        body, name="sum_chips", out_shape=jax.ShapeDtypeStruct((r, w), F32),
        grid_spec=pltpu.PrefetchScalarGridSpec(
            num_scalar_prefetch=1, grid=(r // tr,),
            in_specs=[pl.BlockSpec((1, tr, w), lambda i, c: (c[0], i, 0)), pl.BlockSpec((N_CHIP - 1, tr, w), lambda i, c: (0, i, 0))],
            out_specs=pl.BlockSpec((tr, w), lambda i, c: (i, 0))),
        compiler_params=_params(("parallel",)),
    )(chip, part, land)


def in_proj(x, g_mix, w_in_t, tm):
    t = x.shape[0]
    nc = 512

    def body(x_ref, g_ref, w_ref, z_ref):
        xv = x_ref[...]
        h = (xv * _rstd(xv, D_MODEL) * g_ref[...]).astype(BF16)
        for cidx in range(ZP // nc):
            z_ref[:, cidx * nc:(cidx + 1) * nc] = _dot_nt(h, w_ref[cidx * nc:(cidx + 1) * nc, :])

    return pl.pallas_call(
        body, name="in_proj", grid=(t // tm,), out_shape=jax.ShapeDtypeStruct((t, ZP), F32),
        in_specs=[_rows(tm, D_MODEL), _resident((1, D_MODEL)), _resident((ZP, D_MODEL))],
        out_specs=_rows(tm, ZP), compiler_params=_params(("parallel",)),
    )(x, g_mix, w_in_t)


def attn_prep(zp, tabs, g_qa, g_kva, g_qn, g_kn, w_qb_t, w_kvb_t, tm, s_len):
    t = zp.shape[0]
    nsb = s_len // tm
    scale_a = (QK_NOPE + QK_ROPE) ** -0.5
    scale_b = HD_B ** -0.5

    def body(qb_ref, qlat_ref, kb_ref, vb_ref, ckv_ref, kpe_ref, tab_ref, gqa_ref, gkva_ref, gqn_ref, gkn_ref,
             wqb_ref, wkvb_ref, qa_o, ka_o, va_o, qb_o, kb_o, vb_o, cq_o, ckvn_o):
        ca, s1a, s2a = tab_ref[0], tab_ref[1], tab_ref[2]
        ck = tab_ref[3]
        cb, s1b, s2b = tab_ref[4], tab_ref[5], tab_ref[6]
        ql = qlat_ref[...]
        cq = (ql * _rstd(ql, Q_LORA) * gqa_ref[...]).astype(BF16)
        cq_o[...] = cq
        qa = _dot_nt(cq, wqb_ref[...])
        for h in range(H_A):
            sl = slice(h * HP, (h + 1) * HP)
            qa_o[:, sl] = (_rope_fwd(qa[:, sl], ca, s1a, s2a) * scale_a).astype(BF16)
        cr = ckv_ref[...]
        ckv = (cr * _rstd(cr, KV_LORA) * gkva_ref[...]).astype(BF16)
        ckvn_o[...] = ckv
        kva = _dot_nt(ckv, wkvb_ref[...])
        kpe = _rope_fwd(kpe_ref[...], ck, s1a, s2a)
        for h in range(H_A):
            sl = slice(h * HP, (h + 1) * HP)
            ka_o[:, sl] = (kva[:, sl] + kpe).astype(BF16)
        va_o[...] = kva[:, H_A * HP:].astype(BF16)
        gqn, gkn = gqn_ref[...], gkn_ref[...]
        for h in range(H_B):
            sl = slice(h * HP, (h + 1) * HP)
            xs = qb_ref[:, sl]
            y = xs * _rstd(xs, HD_B) * gqn
            qb_o[:, sl] = (_rope_fwd(y, cb, s1b, s2b) * scale_b).astype(BF16)
        for h in range(KV_B):
            sl = slice(h * HP, (h + 1) * HP)
            xs = kb_ref[:, sl]
            y = xs * _rstd(xs, HD_B) * gkn
            kb_o[:, sl] = _rope_fwd(y, cb, s1b, s2b).astype(BF16)
        vb_o[...] = vb_ref[...].astype(BF16)

    def o(width):
        return jax.ShapeDtypeStruct((t, width), BF16)

    return pl.pallas_call(
        body, name="attn_prep", grid=(t // tm,),
        out_shape=(o(H_A * HP), o(H_A * HP), o(H_A * HP), o(H_B * HP), o(KV_B * HP), o(KV_B * HP), o(Q_LORA), o(KV_LORA)),
        in_specs=[_rows(tm, 1024, 0), _rows(tm, 256, 12), _rows(tm, 256, 13), _rows(tm, 256, 14),
                  _rows(tm, 128, 30), _rows(tm, 128, 31),
                  pl.BlockSpec((7, tm, HP), lambda i: (0, i % nsb, 0)),
                  _resident((1, Q_LORA)), _resident((1, KV_LORA)), _resident((1, HP)), _resident((1, HP)),
                  _resident((H_A * HP, Q_LORA)), _resident((2 * H_A * HP, KV_LORA))],
        out_specs=(_rows(tm, H_A * HP), _rows(tm, H_A * HP), _rows(tm, H_A * HP), _rows(tm, H_B * HP),
                   _rows(tm, KV_B * HP), _rows(tm, KV_B * HP), _rows(tm, Q_LORA), _rows(tm, KV_LORA)),
        compiler_params=_params(("parallel",)),
    )(zp, zp, zp, zp, zp, zp, tabs, g_qa, g_kva, g_qn, g_kn, w_qb_t, w_kvb_t)


def attn_fwd(q, k, v, n_b, s_len, tq, name, comm=None):
    t = q.shape[0]
    n_h, n_hk = q.shape[1] // HP, k.shape[1] // HP
    grp = n_h // n_hk
    nq = s_len // tq
    sub = min(tq, 256)
    grid = (n_b, n_h, nq)
    c_ins, c_in_specs, c_outs, c_sems, alias = _comm_parts(comm, 3, 2)

    def body(*refs):
        (q_ref, k_ref, v_ref), cin, (o_ref, lse_ref), cout, _, csem = _split_refs(refs, 3, 2, 0, comm)
        _comm_start(comm, cin, cout, csem, grid)
        kv, vv = k_ref[...], v_ref[...]
        for r in range(tq // sub):
            rows = slice(r * sub, (r + 1) * sub)
            s = _dot_nt(q_ref[rows, :], kv)
            m = jnp.max(s, axis=-1, keepdims=True)
            p = jnp.exp(s - m)
            l = jnp.sum(p, axis=-1, keepdims=True)
            o_ref[rows, :] = (_dot_nn(p.astype(BF16), vv) * (1.0 / l)).astype(o_ref.dtype)
            lse_ref[rows, :] = jnp.broadcast_to(m + jnp.log(l), (sub, HP))
        _comm_finish(comm, cin, cout, csem, grid)

    qspec = pl.BlockSpec((tq, HP), lambda b, h, i: (b * nq + i, h))
    kspec = pl.BlockSpec((s_len, HP), lambda b, h, i: (b, h // grp))
    return pl.pallas_call(
        body, name=name, grid=grid,
        out_shape=(jax.ShapeDtypeStruct((t, n_h * HP), BF16), jax.ShapeDtypeStruct((t, n_h * HP), F32), *c_outs),
        in_specs=[qspec, kspec, kspec, *c_in_specs], out_specs=(qspec, qspec, *([_ANY] * len(c_outs))),
        scratch_shapes=c_sems, input_output_aliases=alias,
        compiler_params=_params(("arbitrary", "arbitrary", "arbitrary")),
    )(q, k, v, *c_ins)


def merge_fwd(oa, ob, zp, x, w_oa_t, w_ob_t, wpack, off, tm):
    t = x.shape[0]

    def body(oa_ref, ob_ref, ga_ref, gb_ref, x_ref, woa_ref, wob_ref, wo_ref, x1_o, mg_o):
        ya = _dot_nt(oa_ref[...], woa_ref[...])
        yb = _dot_nt(ob_ref[...], wob_ref[...])
        merged = (jax.nn.sigmoid(ga_ref[...]) * ya + jax.nn.sigmoid(gb_ref[...]) * yb).astype(BF16)
        mg_o[...] = merged
        x1_o[...] = x_ref[...] + _dot_nn(merged, _wrows(wo_ref, 0, D_MODEL))

    return pl.pallas_call(
        body, name="merge_fwd", grid=(t // tm,),
        out_shape=(jax.ShapeDtypeStruct((t, D_MODEL), F32), jax.ShapeDtypeStruct((t, D_MODEL), BF16)),
        in_specs=[_rows(tm, H_A * HP), _rows(tm, H_B * HP), _rows(tm, 1024, 1), _rows(tm, 1024, 2), _rows(tm, D_MODEL),
                  _resident((D_MODEL, H_A * HP)), _resident((D_MODEL, H_B * HP)), _packed_weight(128, off["w_o"])],
        out_specs=(_rows(tm, D_MODEL), _rows(tm, D_MODEL)), compiler_params=_params(("parallel",)),
    )(oa, ob, zp, zp, x, w_oa_t, w_ob_t, wpack)


def mlp_fwd(x1, g_mlp, wpack, off, tm):
    t = x1.shape[0]
    fc = 1024

    def body(x_ref, g_ref, wup_ref, wdn_ref, x2_o, u_o):
        xv = x_ref[...]
        h2 = (xv * _rstd(xv, D_MODEL) * g_ref[...]).astype(BF16)
        acc = xv
        for cidx in range(D_FF // fc):
            sl = slice(cidx * fc, (cidx + 1) * fc)
            u = jnp.maximum(_dot_nt(h2, _wrows(wup_ref, cidx * fc, fc)), 0.0)
            u_o[:, sl] = u.astype(BF16)
            acc = acc + _dot_nn((u * u).astype(BF16), _wrows(wdn_ref, cidx * fc, fc))
        x2_o[...] = acc

    return pl.pallas_call(
        body, name="mlp_fwd", grid=(t // tm,),
        out_shape=(jax.ShapeDtypeStruct((t, D_MODEL), F32), jax.ShapeDtypeStruct((t, D_FF), BF16)),
        in_specs=[_rows(tm, D_MODEL), _resident((1, D_MODEL)), _packed_weight(512, off["w_up"]), _packed_weight(512, off["w_down"])],
        out_specs=(_rows(tm, D_MODEL), _rows(tm, D_FF)), compiler_params=_params(("parallel",)),
    )(x1, g_mlp, wpack, wpack)


def ple_loss_bwd(x2, p, tgt, g_ple, g_final, wpack, off, w_ple_t, tm):
    t = x2.shape[0]
    inv_d = 1.0 / D_MODEL

    def body(x2_ref, p_ref, tg_ref, gp_ref, gf_ref, wpg_ref, wple_ref, dx2_o, dt_o, h3_o, dpe_o, st_o, dx2b_o):
        @pl.when(pl.program_id(0) == 0)
        def _():
            st_o[...] = jnp.zeros_like(st_o)

        x2v = x2_ref[...]
        gp, gf = gp_ref[...], gf_ref[...]
        w_pg = _wrows(wpg_ref, 0, D_MODEL)
        r2 = _rstd(x2v, D_MODEL)
        xh2 = x2v * r2
        h3 = (xh2 * gp).astype(BF16)
        h3_o[...] = h3
        gate = jax.nn.sigmoid(_dot_nn(h3, w_pg))
        pe = _dot_nt(p_ref[...].astype(BF16), wple_ref[...])
        x3 = x2v + gate * pe
        r3 = _rstd(x3, D_MODEL)
        xh3 = x3 * r3
        err = xh3 * gf - tg_ref[...]
        dy = err * inv_d
        dx3 = _rms_bwd(dy, xh3, r3, gf, D_MODEL)
        dpe_o[...] = (dx3 * gate).astype(BF16)
        dt = (dx3 * pe * gate * (1.0 - gate)).astype(BF16)
        dt_o[...] = dt
        dh3 = _dot_nt(dt, w_pg)
        dx2 = dx3 + _rms_bwd(dh3, xh2, r2, gp, D_MODEL)
        dx2_o[...] = dx2
        dx2b_o[...] = dx2.astype(BF16)
        st_o[0:1, :] += _colsum(dh3 * xh2)
        st_o[1:2, :] += _colsum(dy * xh3)
        st_o[2:3, :] += _colsum(err * err) * (0.5 * inv_d)

    bf = jax.ShapeDtypeStruct((t, D_MODEL), BF16)
    return pl.pallas_call(
        body, name="ple_loss_bwd", grid=(t // tm,),
        out_shape=(jax.ShapeDtypeStruct((t, D_MODEL), F32), bf, bf, bf, jax.ShapeDtypeStruct((3, D_MODEL), F32), bf),
        in_specs=[_rows(tm, D_MODEL), _rows(tm, PLE_DIM), _rows(tm, D_MODEL), _resident((1, D_MODEL)), _resident((1, D_MODEL)),
                  _packed_weight(128, off["w_ple_gate"]), _resident((D_MODEL, PLE_DIM))],
        out_specs=(_rows(tm, D_MODEL), _rows(tm, D_MODEL), _rows(tm, D_MODEL), _rows(tm, D_MODEL),
                   pl.BlockSpec((3, D_MODEL), lambda i: (0, 0)), _rows(tm, D_MODEL)),
        compiler_params=_params(("arbitrary",)),
    )(x2, p, tgt, g_ple, g_final, wpack, w_ple_t)


def mlp_bwd(dx2, x1, u, g_mlp, wpack, off, tm):
    t = x1.shape[0]
    fc = 1024

    def body(dx2_ref, x1_ref, u_ref, g_ref, wup_ref, wdn_ref, dx1_o, da_o, h2_o, st_o, dx1b_o):
        @pl.when(pl.program_id(0) == 0)
        def _():
            st_o[...] = jnp.zeros_like(st_o)

        d2 = dx2_ref[...]
        d2b = d2.astype(BF16)
        dh2 = jnp.zeros((tm, D_MODEL), F32)
        for cidx in range(D_FF // fc):
            sl = slice(cidx * fc, (cidx + 1) * fc)
            da = (_dot_nt(d2b, _wrows(wdn_ref, cidx * fc, fc)) * (2.0 * u_ref[:, sl].astype(F32))).astype(BF16)
            da_o[:, sl] = da
            dh2 = dh2 + _dot_nn(da, _wrows(wup_ref, cidx * fc, fc))
        xv = x1_ref[...]
        g = g_ref[...]
        r1 = _rstd(xv, D_MODEL)
        xh1 = xv * r1
        h2_o[...] = (xh1 * g).astype(BF16)
        st_o[...] += _colsum(dh2 * xh1)
        dx1 = d2 + _rms_bwd(dh2, xh1, r1, g, D_MODEL)
        dx1_o[...] = dx1
        dx1b_o[...] = dx1.astype(BF16)

    return pl.pallas_call(
        body, name="mlp_bwd", grid=(t // tm,),
        out_shape=(jax.ShapeDtypeStruct((t, D_MODEL), F32), jax.ShapeDtypeStruct((t, D_FF), BF16),
                   jax.ShapeDtypeStruct((t, D_MODEL), BF16), jax.ShapeDtypeStruct((1, D_MODEL), F32),
                   jax.ShapeDtypeStruct((t, D_MODEL), BF16)),
        in_specs=[_rows(tm, D_MODEL), _rows(tm, D_MODEL), _rows(tm, D_FF), _resident((1, D_MODEL)),
                  _packed_weight(512, off["w_up"]), _packed_weight(512, off["w_down"])],
        out_specs=(_rows(tm, D_MODEL), _rows(tm, D_FF), _rows(tm, D_MODEL), pl.BlockSpec((1, D_MODEL), lambda i: (0, 0)),
                   _rows(tm, D_MODEL)),
        compiler_params=_params(("arbitrary",)),
    )(dx2, x1, u, g_mlp, wpack, wpack)


def merge_bwd(dx1, oa, ob, zp, w_oa_t, w_ob_t, wpack, off, tm, comm=None):
    t = dx1.shape[0]
    grid = (t // tm,)
    c_ins, c_in_specs, c_outs, c_sems, alias = _comm_parts(comm, 8, 5)

    def body(*refs):
        ((dx1_ref, oa_ref, ob_ref, ga_ref, gb_ref, woa_ref, wob_ref, wo_ref), cin,
         (doa_o, dob_o, dg_o, dya_o, dyb_o), cout, _, csem) = _split_refs(refs, 8, 5, 0, comm)
        _comm_start(comm, cin, cout, csem, grid)
        dm = _dot_nt(dx1_ref[...].astype(BF16), _wrows(wo_ref, 0, D_MODEL))
        for o_ref, g_ref, w_ref, do_o, dy_o, col in ((oa_ref, ga_ref, woa_ref, doa_o, dya_o, 0),
                                                     (ob_ref, gb_ref, wob_ref, dob_o, dyb_o, 1)):
            yv = _dot_nt(o_ref[...], w_ref[...])
            sg = jax.nn.sigmoid(g_ref[...])
            dyv = (dm * sg).astype(BF16)
            dy_o[...] = dyv
            dg_o[:, col * D_MODEL:(col + 1) * D_MODEL] = (dm * yv * sg * (1.0 - sg)).astype(BF16)
            do_o[...] = _dot_nn(dyv, w_ref[...]).astype(BF16)
        _comm_finish(comm, cin, cout, csem, grid)

    bf = jax.ShapeDtypeStruct((t, D_MODEL), BF16)
    return pl.pallas_call(
        body, name="merge_bwd", grid=grid,
        out_shape=(bf, bf, jax.ShapeDtypeStruct((t, 2 * D_MODEL), BF16), bf, bf, *c_outs),
        in_specs=[_rows(tm, D_MODEL), _rows(tm, H_A * HP), _rows(tm, H_B * HP), _rows(tm, 1024, 1), _rows(tm, 1024, 2),
                  _resident((D_MODEL, H_A * HP)), _resident((D_MODEL, H_B * HP)), _packed_weight(128, off["w_o"]), *c_in_specs],
        out_specs=(_rows(tm, D_MODEL), _rows(tm, D_MODEL), _rows(tm, 2 * D_MODEL), _rows(tm, D_MODEL), _rows(tm, D_MODEL),
                   *([_ANY] * len(c_outs))),
        scratch_shapes=c_sems, input_output_aliases=alias,
        compiler_params=_params(("arbitrary",)),
    )(dx1, oa, ob, zp, zp, w_oa_t, w_ob_t, wpack, *c_ins)


def attn_bwd(q, k, v, do, o, lse, n_b, s_len, tq, name, comm=None):
    t = q.shape[0]
    n_h, n_hk = q.shape[1] // HP, k.shape[1] // HP
    grp = n_h // n_hk
    nq = s_len // tq
    sub = min(tq, 256)
    grid = (n_b, n_hk, grp, nq)
    c_ins, c_in_specs, c_outs, c_sems, alias = _comm_parts(comm, 6, 3)

    def body(*refs):
        ((q_ref, k_ref, v_ref, do_ref, o_ref, lse_ref), cin, (dq_o, dk_o, dv_o), cout, (p_s, ds_s),
         csem) = _split_refs(refs, 6, 3, 2, comm)
        _comm_start(comm, cin, cout, csem, grid)

        @pl.when((pl.program_id(2) == 0) & (pl.program_id(3) == 0))
        def _():
            dk_o[...] = jnp.zeros_like(dk_o)
            dv_o[...] = jnp.zeros_like(dv_o)

        kv, vv = k_ref[...], v_ref[...]
        for r in range(tq // sub):
            rows = slice(r * sub, (r + 1) * sub)
            qv, dov = q_ref[rows, :], do_ref[rows, :]
            delta = jnp.sum(dov.astype(F32) * o_ref[rows, :].astype(F32), axis=-1, keepdims=True)
            p = jnp.exp(_dot_nt(qv, kv) - lse_ref[rows, 0:1])
            ds = (p * (_dot_nt(dov, vv) - delta)).astype(BF16)
            p_s[rows, :] = p.astype(BF16)
            ds_s[rows, :] = ds
            dq_o[rows, :] = _dot_nn(ds, kv)
        dk_o[...] += _dot_tn(ds_s[...], q_ref[...])
        dv_o[...] += _dot_tn(p_s[...], do_ref[...])
        _comm_finish(comm, cin, cout, csem, grid)

    qspec = pl.BlockSpec((tq, HP), lambda b, hk, g, i: (b * nq + i, hk * grp + g))
    kspec = pl.BlockSpec((s_len, HP), lambda b, hk, g, i: (b, hk))
    return pl.pallas_call(
        body, name=name, grid=grid,
        out_shape=(jax.ShapeDtypeStruct((t, n_h * HP), F32), jax.ShapeDtypeStruct((t, n_hk * HP), F32),
                   jax.ShapeDtypeStruct((t, n_hk * HP), F32), *c_outs),
        in_specs=[qspec, kspec, kspec, qspec, qspec, qspec, *c_in_specs],
        out_specs=(qspec, kspec, kspec, *([_ANY] * len(c_outs))),
        scratch_shapes=[pltpu.VMEM((tq, s_len), BF16), pltpu.VMEM((tq, s_len), BF16), *c_sems],
        input_output_aliases=alias,
        compiler_params=_params(("arbitrary", "arbitrary", "arbitrary", "arbitrary")),
    )(q, k, v, do, o, lse, *c_ins)


def prep_bwd(dqa, dka, dva, dqb, dkb, dvb, zp, tabs, g_qa, g_kva, g_qn, g_kn, w_qb_t, w_kvb_t, tm, s_len):
    t = zp.shape[0]
    nsb = s_len // tm
    scale_a = (QK_NOPE + QK_ROPE) ** -0.5
    scale_b = HD_B ** -0.5

    def body(dqa_ref, dka_ref, dva_ref, dqb_ref, dkb_ref, dvb_ref, qb_ref, qlat_ref, kb_ref, ckv_ref, tab_ref,
             gqa_ref, gkva_ref, gqn_ref, gkn_ref, wqb_ref, wkvb_ref, dzq_o, dsm_o, dqap_o, dkva_o, st_o):
        @pl.when(pl.program_id(0) == 0)
        def _():
            st_o[...] = jnp.zeros_like(st_o)

        ca, s1a, s2a = tab_ref[0], tab_ref[1], tab_ref[2]
        ck = tab_ref[3]
        cb, s1b, s2b = tab_ref[4], tab_ref[5], tab_ref[6]
        for h in range(H_A):
            sl = slice(h * HP, (h + 1) * HP)
            dqap_o[:, sl] = _rope_bwd(dqa_ref[:, sl] * scale_a, ca, s1a, s2a).astype(BF16)
        dcq = _dot_nn(dqap_o[...], wqb_ref[...])
        ql = qlat_ref[...]
        rq = _rstd(ql, Q_LORA)
        xh = ql * rq
        gqa = gqa_ref[...]
        st_o[0:1, :] += _colsum(dcq * xh)
        dsm_o[:, 0:256] = _rms_bwd(dcq, xh, rq, gqa, Q_LORA).astype(BF16)
        dkpe = jnp.zeros((tm, HP), F32)
        for h in range(H_A):
            sl = slice(h * HP, (h + 1) * HP)
            dk = dka_ref[:, sl]
            dkpe = dkpe + dk
            dkva_o[:, sl] = dk.astype(BF16)
        dkva_o[:, H_A * HP:] = dva_ref[...].astype(BF16)
        dsm_o[:, 896:1024] = _rope_bwd(dkpe, ck, s1a, s2a).astype(BF16)
        dckv = _dot_nn(dkva_o[...], wkvb_ref[...])
        cr = ckv_ref[...]
        rk = _rstd(cr, KV_LORA)
        xh = cr * rk
        st_o[1:2, 0:128] += _colsum(dckv * xh)
        dsm_o[:, 768:896] = _rms_bwd(dckv, xh, rk, gkva_ref[...], KV_LORA).astype(BF16)
        gqn, gkn = gqn_ref[...], gkn_ref[...]
        dgq = jnp.zeros((1, HP), F32)
        for h in range(H_B):
            sl = slice(h * HP, (h + 1) * HP)
            dy = _rope_bwd(dqb_ref[:, sl] * scale_b, cb, s1b, s2b)
            xs = qb_ref[:, sl]
            r = _rstd(xs, HD_B)
            xh = xs * r
            dgq = dgq + _colsum(dy * xh)
            dzq_o[:, sl] = _rms_bwd(dy, xh, r, gqn, HD_B).astype(BF16)
        st_o[2:3, 0:128] += dgq
        dgk = jnp.zeros((1, HP), F32)
        for h in range(KV_B):
            sl = slice(h * HP, (h + 1) * HP)
            dy = _rope_bwd(dkb_ref[:, sl], cb, s1b, s2b)
            xs = kb_ref[:, sl]
            r = _rstd(xs, HD_B)
            xh = xs * r
            dgk = dgk + _colsum(dy * xh)
            dsm_o[:, 256 + h * HP:256 + (h + 1) * HP] = _rms_bwd(dy, xh, r, gkn, HD_B).astype(BF16)
        st_o[3:4, 0:128] += dgk
        dsm_o[:, 512:768] = dvb_ref[...].astype(BF16)

    bf = jax.ShapeDtypeStruct((t, 1024), BF16)
    return pl.pallas_call(
        body, name="prep_bwd", grid=(t // tm,),
        out_shape=(bf, bf, bf, jax.ShapeDtypeStruct((t, 2048), BF16), jax.ShapeDtypeStruct((4, 256), F32)),
        in_specs=[_rows(tm, 1024), _rows(tm, 1024), _rows(tm, 1024), _rows(tm, 1024), _rows(tm, 256), _rows(tm, 256),
                  _rows(tm, 1024, 0), _rows(tm, 256, 12), _rows(tm, 256, 13), _rows(tm, 128, 30),
                  pl.BlockSpec((7, tm, HP), lambda i: (0, i % nsb, 0)),
                  _resident((1, Q_LORA)), _resident((1, KV_LORA)), _resident((1, HP)), _resident((1, HP)),
                  _resident((H_A * HP, Q_LORA)), _resident((2 * H_A * HP, KV_LORA))],
        out_specs=(_rows(tm, 1024), _rows(tm, 1024), _rows(tm, 1024), _rows(tm, 2048), pl.BlockSpec((4, 256), lambda i: (0, 0))),
        compiler_params=_params(("arbitrary",)),
    )(dqa, dka, dva, dqb, dkb, dvb, zp, zp, zp, zp, tabs, g_qa, g_kva, g_qn, g_kn, w_qb_t, w_kvb_t)


def in_bwd(dzq, dgab, dsm, x, dx1, g_mix, w_in_t, tm):
    t = x.shape[0]

    def body(dzq_ref, dg_ref, dsm_ref, x_ref, dx1_ref, g_ref, w_ref, dx_o, h_o, st_o):
        @pl.when(pl.program_id(0) == 0)
        def _():
            st_o[...] = jnp.zeros_like(st_o)

        dh = _dot_nn(dzq_ref[...], w_ref[0:1024, :])
        dh = dh + _dot_nn(dg_ref[...], w_ref[1024:3072, :])
        dh = dh + _dot_nn(dsm_ref[...], w_ref[3072:4096, :])
        xv = x_ref[...]
        g = g_ref[...]
        r = _rstd(xv, D_MODEL)
        xh = xv * r
        h_o[...] = (xh * g).astype(BF16)
        st_o[...] += _colsum(dh * xh)
        dx_o[...] = dx1_ref[...] + _rms_bwd(dh, xh, r, g, D_MODEL)

    return pl.pallas_call(
        body, name="in_bwd", grid=(t // tm,),
        out_shape=(jax.ShapeDtypeStruct((t, D_MODEL), F32), jax.ShapeDtypeStruct((t, D_MODEL), BF16),
                   jax.ShapeDtypeStruct((1, D_MODEL), F32)),
        in_specs=[_rows(tm, 1024), _rows(tm, 2048), _rows(tm, 1024), _rows(tm, D_MODEL), _rows(tm, D_MODEL),
                  _resident((1, D_MODEL)), _resident((ZP, D_MODEL))],
        out_specs=(_rows(tm, D_MODEL), _rows(tm, D_MODEL), pl.BlockSpec((1, D_MODEL), lambda i: (0, 0))),
        compiler_params=_params(("arbitrary",)),
    )(dzq, dgab, dsm, x, dx1, g_mix, w_in_t)


def matmul_tn(a, b, name, square_a=False):
    t, m = a.shape
    n = b.shape[1]
    bm = min(m, 512)
    tk = min(t, 4096)

    def body(a_ref, b_ref, o_ref):
        @pl.when(pl.program_id(1) == 0)
        def _():
            o_ref[...] = jnp.zeros_like(o_ref)

        av = a_ref[...]
        if square_a:
            av = (av.astype(F32) * av.astype(F32))
        o_ref[...] += _dot_tn(av.astype(BF16), b_ref[...].astype(BF16))

    return pl.pallas_call(
        body, name=name, grid=(m // bm, t // tk), out_shape=jax.ShapeDtypeStruct((m, n), F32),
        in_specs=[pl.BlockSpec((tk, bm), lambda i, kk: (kk, i)), pl.BlockSpec((tk, n), lambda i, kk: (kk, 0))],
        out_specs=pl.BlockSpec((bm, n), lambda i, kk: (i, 0)),
        compiler_params=_params(("parallel", "arbitrary")),
    )(a, b)


def matmul_tn_packed(a, b, name, rows, row_off, total_rows, buf=None, square_a=False):
    t, m = a.shape
    n = b.shape[1]
    pd = max(1, 512 // rows)
    bm = pd * rows
    tk = min(t, 4096)
    nk = t // tk

    def body(a_ref, b_ref, *rest):
        o_ref, acc = rest[-2], rest[-1]

        @pl.when(pl.program_id(1) == 0)
        def _():
            acc[...] = jnp.zeros_like(acc)

        av = a_ref[...]
        if square_a:
            av = (av.astype(F32) * av.astype(F32))
        acc[...] += _dot_tn(av.astype(BF16), b_ref[...].astype(BF16))

        @pl.when(pl.program_id(1) == nk - 1)
        def _():
            o_ref[...] = acc[...].reshape(pd, rows, n).astype(o_ref.dtype)

    in_specs = [pl.BlockSpec((tk, bm), lambda i, kk: (kk, i)), pl.BlockSpec((tk, n), lambda i, kk: (kk, 0))]
    args = [a, b]
    if buf is not None:
        in_specs.append(_ANY)
        args.append(buf)
    return pl.pallas_call(
        body, name=name, grid=(m // bm, nk), out_shape=jax.ShapeDtypeStruct((N_DEV, total_rows, n), BF16),
        in_specs=in_specs, out_specs=pl.BlockSpec((pd, rows, n), lambda i, kk: (i, row_off // rows, 0)),
        scratch_shapes=[pltpu.VMEM((bm, n), F32)], input_output_aliases={2: 0} if buf is not None else {},
        compiler_params=_params(("parallel", "arbitrary")),
    )(*args)


def adamw(w, g, m, v, name):
    r, c = w.shape
    tr = r if r <= 256 else 256
    c1 = 1.0 - ADAM_B1 ** ADAM_STEP
    c2 = 1.0 - ADAM_B2 ** ADAM_STEP

    def body(w_ref, g_ref, m_ref, v_ref, d_o, m_o, v_o):
        gv = g_ref[...]
        mn = ADAM_B1 * m_ref[...] + (1.0 - ADAM_B1) * gv
        vn = ADAM_B2 * v_ref[...] + (1.0 - ADAM_B2) * (gv * gv)
        m_o[...] = mn
        v_o[...] = vn
        d_o[...] = -ADAM_LR * ((mn / c1) / (jnp.sqrt(vn / c2) + ADAM_EPS) + ADAM_WD * w_ref[...])

    spec = pl.BlockSpec((tr, c), lambda i: (i, 0))
    shp = jax.ShapeDtypeStruct((r, c), F32)
    return pl.pallas_call(
        body, name=name, grid=(r // tr,), out_shape=(shp, shp, shp), in_specs=[spec] * 4, out_specs=(spec,) * 3,
        compiler_params=_params(("parallel",)),
    )(w, g, m, v)


def _rope_tables(s_len):
    def angles(pos, dim):
        inv = np.float32(ROPE_THETA) ** (-np.arange(0, dim, 2, dtype=np.float32) / np.float32(dim))
        return pos.astype(np.float32)[:, None] * inv[None, :]

    tpos = np.arange(s_len)
    a1 = angles(tpos, QK_ROPE)
    ar = angles(tpos // GRID_W, HD_B // 2)
    ac = angles(tpos % GRID_W, HD_B // 2)
    z16 = np.zeros((s_len, 16), np.float32)
    z32 = np.zeros((s_len, 32), np.float32)
    z64 = np.zeros((s_len, 64), np.float32)
    one64 = np.ones((s_len, 64), np.float32)
    c1, s1 = np.cos(a1), np.sin(a1)
    ca = np.concatenate([one64, c1, c1, z32], axis=1)
    ck = np.concatenate([z64, c1, c1, z32], axis=1)
    s1a = np.concatenate([z64, -s1, z16, z32], axis=1)
    s2a = np.concatenate([z64, z16, s1, z32], axis=1)
    cr, sr, cc, sc = np.cos(ar), np.sin(ar), np.cos(ac), np.sin(ac)
    cb = np.concatenate([cr, cr, cc, cc, z64], axis=1)
    s1b = np.concatenate([-sr, z16, -sc, z16, z64], axis=1)
    s2b = np.concatenate([z16, sr, z16, sc, z64], axis=1)
    return jnp.asarray(np.stack([ca, s1a, s2a, ck, cb, s1b, s2b]).astype(np.float32))


def _pad_heads(a, n_heads, axis):
    shp = a.shape
    a = a.reshape(shp[:axis] + (n_heads, shp[axis] // n_heads) + shp[axis + 1:])
    pad = [(0, 0)] * a.ndim
    pad[axis + 1] = (0, HP - a.shape[axis + 1])
    a = jnp.pad(a, pad)
    return a.reshape(shp[:axis] + (n_heads * HP,) + shp[axis + 1:])


def _unpad_heads(a, n_heads, width, axis):
    shp = a.shape
    a = a.reshape(shp[:axis] + (n_heads, HP) + shp[axis + 1:])
    a = lax.slice_in_dim(a, 0, width, axis=axis + 1)
    return a.reshape(shp[:axis] + (n_heads * width,) + shp[axis + 1:])


def _pack_rows(blocks, names):
    parts = []
    for name in names:
        b = blocks[name]
        padr = PACK_ROWS[name] - b.shape[-2]
        if padr:
            b = jnp.pad(b, [(0, 0)] * (b.ndim - 2) + [(0, padr), (0, 0)])
        parts.append(b)
    return jnp.concatenate(parts, axis=parts[0].ndim - 2)


def _expand_w_in(wt):
    z64 = jnp.zeros((64, D_MODEL), wt.dtype)
    z32 = jnp.zeros((32, D_MODEL), wt.dtype)
    return jnp.concatenate([
        _pad_heads(wt[416:928], H_B, 0), wt[1184:2208], wt[2208:3232], wt[0:256],
        _pad_heads(wt[928:1056], KV_B, 0), _pad_heads(wt[1056:1184], KV_B, 0), wt[256:384],
        z64, wt[384:416], z32], axis=0)


def _collapse_w_in(dq, dg, ds):
    return jnp.concatenate([
        ds[0:256], ds[768:896], ds[960:992], _unpad_heads(dq, H_B, HD_B, 0), _unpad_heads(ds[256:512], KV_B, HD_B, 0),
        _unpad_heads(ds[512:768], KV_B, HD_B, 0), dg], axis=0)


def kernel(x, p, g_mix, w_in, g_qa, w_qb, g_kva, w_kvb, g_qn, g_kn, w_oa, w_ob, w_o, g_mlp, w_up, w_down, g_ple, w_ple_gate, w_ple, g_final, loss_target, m_g_mix, m_w_in, m_g_qa, m_w_qb, m_g_kva, m_w_kvb, m_g_qn, m_g_kn, m_w_oa, m_w_ob, m_w_o, m_g_mlp, m_w_up, m_w_down, m_g_ple, m_w_ple_gate, m_w_ple, m_g_final, v_g_mix, v_w_in, v_g_qa, v_w_qb, v_g_kva, v_w_kvb, v_g_qn, v_g_kn, v_w_oa, v_w_ob, v_w_o, v_g_mlp, v_w_up, v_w_down, v_g_ple, v_w_ple_gate, v_w_ple, v_g_final):
    n_b, s_len, _ = x.shape
    t = n_b * s_len
    tm = min(256, s_len)
    tq_f = min(512, s_len)
    tq_b = min(512, s_len)

    mats = dict(w_in=(w_in, m_w_in, v_w_in), w_qb=(w_qb, m_w_qb, v_w_qb), w_kvb=(w_kvb, m_w_kvb, v_w_kvb),
                w_oa=(w_oa, m_w_oa, v_w_oa), w_ob=(w_ob, m_w_ob, v_w_ob), w_o=(w_o, m_w_o, v_w_o),
                w_up=(w_up, m_w_up, v_w_up), w_down=(w_down, m_w_down, v_w_down),
                w_ple_gate=(w_ple_gate, m_w_ple_gate, v_w_ple_gate), w_ple=(w_ple, m_w_ple, v_w_ple))
    col_sharded = ("w_in", "w_qb", "w_kvb", "w_oa", "w_ob", "w_up", "w_ple")

    blocks = {}
    for name, (w, _, _) in mats.items():
        w2 = w[0]
        if name in col_sharded:
            w2 = w2.T
        blocks[name] = w2.reshape(-1, D_MODEL).astype(BF16)
    off_w1, _ = _pack_offsets(PACK_W1)
    off_w2, _ = _pack_offsets(PACK_W2)
    full1 = allgather_rows(_pack_rows(blocks, PACK_W1))
    pack2 = _pack_rows(blocks, PACK_W2)

    def gathered(full, offs, name, rows, width):
        return full[:, offs[name]:offs[name] + rows].reshape(-1, width)

    w_in_t = _expand_w_in(gathered(full1, off_w1, "w_in", 404, D_MODEL))
    w_qb_t = _pad_heads(gathered(full1, off_w1, "w_qb", 24, Q_LORA), H_A, 0)
    wkvb = gathered(full1, off_w1, "w_kvb", 16, KV_LORA).reshape(H_A, 2, 64, KV_LORA)
    w_kvb_t = jnp.concatenate([_pad_heads(wkvb[:, 0].reshape(-1, KV_LORA), H_A, 0),
                               _pad_heads(wkvb[:, 1].reshape(-1, KV_LORA), H_A, 0)], axis=0)

    tabs = _rope_tables(s_len)
    g_qn_p = jnp.pad(g_qn, ((0, 0), (0, HP - HD_B)))
    g_kn_p = jnp.pad(g_kn, ((0, 0), (0, HP - HD_B)))
    xf = x.reshape(t, D_MODEL)
    pf = p.reshape(t, PLE_DIM)
    tgt = loss_target.reshape(t, D_MODEL)

    zp = in_proj(xf, g_mix, w_in_t, tm)
    qa, ka, va, qb, kb, vb, cq, ckv = attn_prep(zp, tabs, g_qa, g_kva, g_qn_p, g_kn_p, w_qb_t, w_kvb_t, tm, s_len)
    oa, lse_a, full2 = attn_fwd(qa, ka, va, n_b, s_len, tq_f, "attn_a_fwd", comm=gather_first_comm(pack2))
    ob, lse_b, full2 = attn_fwd(qb, kb, vb, n_b, s_len, tq_f, "attn_b_fwd", comm=gather_pass_comm(full2))
    w_oa_t = _pad_heads(gathered(full2, off_w2, "w_oa", 64, H_A * V_DIM_A), H_A, 1)
    w_ob_t = _pad_heads(gathered(full2, off_w2, "w_ob", 64, H_B * HD_B), H_B, 1)
    w_ple_t = gathered(full2, off_w2, "w_ple", 32, PLE_DIM)
    x1, merged = merge_fwd(oa, ob, zp, xf, w_oa_t, w_ob_t, full2, off_w2, tm)
    x2, u = mlp_fwd(x1, g_mlp, full2, off_w2, tm)
    dx2, dt, h3, dpe, st_ple, dx2b = ple_loss_bwd(x2, pf, tgt, g_ple, g_final.reshape(1, D_MODEL), full2, off_w2, w_ple_t, tm)
    dx1, da, h2, st_mlp, dx1b = mlp_bwd(dx2, x1, u, g_mlp, full2, off_w2, tm)

    core = lax.axis_index("c").astype(jnp.int32).reshape(1)
    chip = (2 * lax.axis_index("x") + lax.axis_index("y")).astype(jnp.int32).reshape(1)

    def packed(gblocks, names):
        return _pack_rows({n: gblocks[n].reshape(N_DEV, -1, D_MODEL).astype(BF16) for n in names}, names)

    off_g1, rows_g1 = _pack_offsets(PACK_G1)
    gpack1 = matmul_tn_packed(da, h2, "gw_up", 512, off_g1["w_up"], rows_g1)
    gpack1 = matmul_tn_packed(u, dx2b, "gw_down", 512, off_g1["w_down"], rows_g1, buf=gpack1, square_a=True)
    gpack1 = matmul_tn_packed(h3, dt, "gw_pg", 128, off_g1["w_ple_gate"], rows_g1, buf=gpack1)
    gple = matmul_tn(dpe, pf, "gw_ple").reshape(N_DEV, -1, D_MODEL).astype(BF16)
    gpack1 = lax.dynamic_update_slice(gpack1, gple, (0, off_g1["w_ple"], 0))
    doa, dob, dgab, dya, dyb, got1 = merge_bwd(dx1, oa, ob, zp, w_oa_t, w_ob_t, full2, off_w2, tm,
                                               comm=scatter_sibling_comm(gpack1))
    part1 = add_pairs(gpack1, got1, core)
    dqa, dka, dva, land1 = attn_bwd(qa, ka, va, doa, oa, lse_a, n_b, s_len, tq_b, "attn_a_bwd", comm=scatter_chips_comm(part1))
    gshard1 = sum_chips(part1, land1, chip)

    off_g2, rows_g2 = _pack_offsets(PACK_G2)
    g2 = dict(w_oa=_unpad_heads(matmul_tn(dya, oa, "gw_oa"), H_A, V_DIM_A, 1),
              w_ob=_unpad_heads(matmul_tn(dyb, ob, "gw_ob"), H_B, HD_B, 1))
    gpack2 = matmul_tn_packed(merged, dx1b, "gw_o", 128, off_g2["w_o"], rows_g2)
    gpack2 = lax.dynamic_update_slice(gpack2, packed(g2, ("w_oa", "w_ob")), (0, off_g2["w_oa"], 0))
    part2 = add_pairs(gpack2, exchange_sibling(gpack2), core)
    dqb, dkb, dvb, land2 = attn_bwd(qb, kb, vb, dob, ob, lse_b, n_b, s_len, tq_b, "attn_b_bwd", comm=scatter_chips_comm(part2))
    gshard2 = sum_chips(part2, land2, chip)
    dzq, dsm, dqap, dkva, st_prep = prep_bwd(dqa, dka, dva, dqb, dkb, dvb, zp, tabs, g_qa, g_kva, g_qn_p, g_kn_p,
                                             w_qb_t, w_kvb_t, tm, s_len)
    grad_x, h, st_mix = in_bwd(dzq, dgab, dsm, xf, dx1, g_mix, w_in_t, tm)

    gkv = matmul_tn(dkva, ckv, "gw_kvb")
    g3 = dict(
        w_in=_collapse_w_in(matmul_tn(dzq, h, "gw_in_q"), matmul_tn(dgab, h, "gw_in_g"), matmul_tn(dsm, h, "gw_in_s")),
        w_qb=_unpad_heads(matmul_tn(dqap, cq, "gw_qb"), H_A, QK_NOPE + QK_ROPE, 0),
        w_kvb=jnp.stack([_unpad_heads(gkv[:H_A * HP], H_A, 64, 0).reshape(H_A, 64, KV_LORA),
                         _unpad_heads(gkv[H_A * HP:], H_A, 64, 0).reshape(H_A, 64, KV_LORA)], axis=1))
    gpack3 = packed(g3, PACK_G3)
    part3 = add_pairs(gpack3, exchange_sibling(gpack3), core)
    gshard3 = sum_chips(part3, exchange_chips(part3), chip)
    off_g3, _ = _pack_offsets(PACK_G3)
    shards = {n: (gshard1, off_g1[n]) for n in PACK_G1}
    shards.update({n: (gshard2, off_g2[n]) for n in PACK_G2})
    shards.update({n: (gshard3, off_g3[n]) for n in PACK_G3})

    stats = allreduce_stats(st_mix, st_prep, st_mlp, st_ple)
    loss = jnp.sum(stats[ST_LOSS])

    out_g, out_d, out_m, out_v = {}, {}, {}, {}
    for name, (w, m, v) in mats.items():
        gshard, off = shards[name]
        shard_shape = w.shape[1:]
        if name in col_sharded:
            rows_t = shard_shape[1]
            g2 = gshard[off:off + (rows_t * shard_shape[0]) // D_MODEL].reshape(rows_t, shard_shape[0]).T
        else:
            g2 = gshard[off:off + shard_shape[0]]
        d2, m2, v2 = adamw(w[0], g2, m[0], v[0], "adamw_" + name)
        out_g[name], out_d[name], out_m[name], out_v[name] = g2[None], d2[None], m2[None], v2[None]

    gains = (("g_mix", g_mix, m_g_mix, v_g_mix, ST_G_MIX), ("g_qa", g_qa, m_g_qa, v_g_qa, ST_G_QA),
             ("g_kva", g_kva, m_g_kva, v_g_kva, ST_G_KVA), ("g_qn", g_qn, m_g_qn, v_g_qn, ST_G_QN),
             ("g_kn", g_kn, m_g_kn, v_g_kn, ST_G_KN), ("g_mlp", g_mlp, m_g_mlp, v_g_mlp, ST_G_MLP),
             ("g_ple", g_ple, m_g_ple, v_g_ple, ST_G_PLE), ("g_final", g_final, m_g_final, v_g_final, ST_G_FINAL))
    res = adamw_gains(stats, [(r_, w.reshape(1, -1), m.reshape(1, -1), v.reshape(1, -1)) for _, w, m, v, r_ in gains])
    for (name, w, _, _, _), (gg, gd, gm, gv) in zip(gains, res):
        out_g[name], out_d[name], out_m[name], out_v[name] = (a.reshape(w.shape) for a in (gg, gd, gm, gv))

    order = ("g_mix", "w_in", "g_qa", "w_qb", "g_kva", "w_kvb", "g_qn", "g_kn", "w_oa", "w_ob", "w_o", "g_mlp",
             "w_up", "w_down", "g_ple", "w_ple_gate", "w_ple", "g_final")
    return (loss, grad_x.reshape(x.shape), *[out_g[n] for n in order], *[out_d[n] for n in order],
            *[out_m[n] for n in order], *[out_v[n] for n in order])
```

```python
import numpy as np
import jax
import jax.numpy as jnp
from jax import lax
from jax.experimental import pallas as pl
from jax.experimental.pallas import tpu as pltpu

F32 = jnp.float32
BF16 = jnp.bfloat16

D_MODEL = 1024
EPS = 1e-6
ROPE_THETA = 10000.0
GRID_W = 64
H_A = 8
QK_NOPE = 64
QK_ROPE = 32
V_DIM_A = 64
Q_LORA = 256
KV_LORA = 128
H_B = 8
KV_B = 2
HD_B = 64
D_FF = 4 * D_MODEL
PLE_DIM = 256
HP = 128
ZP = 4096
N_DEV = 8
N_CHIP = 4

ADAM_LR = 0.001
ADAM_B1 = 0.9
ADAM_B2 = 0.999
ADAM_EPS = 1e-08
ADAM_WD = 0.01
ADAM_STEP = 10

VMEM_LIMIT = 52 * 1024 * 1024

PACK_ROWS = dict(w_in=416, w_qb=32, w_kvb=16, w_oa=64, w_ob=64, w_o=128, w_up=512, w_down=512, w_ple_gate=128, w_ple=32)
PACK_W1 = ("w_in", "w_qb", "w_kvb")
PACK_W2 = ("w_up", "w_down", "w_o", "w_ple_gate", "w_oa", "w_ob", "w_ple")
PACK_G1 = ("w_up", "w_down", "w_ple_gate", "w_ple")
PACK_G2 = ("w_o", "w_oa", "w_ob")
PACK_G3 = ("w_in", "w_qb", "w_kvb")


def _pack_offsets(names):
    off, o = {}, 0
    for n in names:
        off[n] = o
        o += PACK_ROWS[n]
    return off, o

ST_G_MIX, ST_G_QA, ST_G_KVA, ST_G_QN, ST_G_KN, ST_G_MLP, ST_G_PLE, ST_G_FINAL, ST_LOSS = range(9)
ST_ROWS = 16


def _dot_nn(a, b):
    return lax.dot_general(a, b, (((1,), (0,)), ((), ())), preferred_element_type=F32)


def _dot_nt(a, b):
    return lax.dot_general(a, b, (((1,), (1,)), ((), ())), preferred_element_type=F32)


def _dot_tn(a, b):
    return lax.dot_general(a, b, (((0,), (0,)), ((), ())), preferred_element_type=F32)


def _rstd(x, n):
    return lax.rsqrt(jnp.sum(x * x, axis=-1, keepdims=True) * (1.0 / n) + EPS)


def _rms_bwd(dy, xh, r, g, n):
    dxh = dy * g
    return r * (dxh - xh * (jnp.sum(dxh * xh, axis=-1, keepdims=True) * (1.0 / n)))


def _rope_fwd(x, c, s1, s2):
    return x * c + pltpu.roll(x, HP - 16, 1) * s1 + pltpu.roll(x, 16, 1) * s2


def _rope_bwd(d, c, s1, s2):
    return d * c + pltpu.roll(d * s1, 16, 1) + pltpu.roll(d * s2, HP - 16, 1)


def _colsum(v):
    return jnp.sum(v, axis=0, keepdims=True)


def _params(sem=None, vmem=VMEM_LIMIT):
    return pltpu.CompilerParams(dimension_semantics=sem, vmem_limit_bytes=vmem)


def _resident(shape):
    nd = len(shape)
    return pl.BlockSpec(shape, lambda *_: (0,) * nd, pipeline_mode=pl.Buffered(1))


def _rows(tm, width, col=0):
    return pl.BlockSpec((tm, width), lambda i: (i, col))


def _packed_weight(rows, off):
    return pl.BlockSpec((N_DEV, rows, D_MODEL), lambda *_: (0, off // rows, 0), pipeline_mode=pl.Buffered(1))


def _wrows(ref, start, size):
    rows = ref.shape[1]
    return ref[start // rows:(start + size) // rows].reshape(size, D_MODEL)


def _mesh_pos():
    return lax.axis_index("x"), lax.axis_index("y"), lax.axis_index("c")


def _flip(v, bit):
    return (1 - v) if bit else v


_ANY = pl.BlockSpec(memory_space=pl.ANY)
_MESH = pl.DeviceIdType.MESH


def _remote(src, dst, send_sems, recv_sems, k, to):
    return pltpu.make_async_remote_copy(src_ref=src, dst_ref=dst, send_sem=send_sems.at[k], recv_sem=recv_sems.at[k],
                                        device_id=to, device_id_type=_MESH)


def _sibling_copies(g_ref, got_ref, send_sems, recv_sems):
    x, y, c = _mesh_pos()
    return [_remote(g_ref.at[2 * j + (1 - c)], got_ref.at[j], send_sems, recv_sems, j, (x, y, 1 - c)) for j in range(N_CHIP)]


def _chip_copies(p_ref, land_ref, send_sems, recv_sems):
    x, y, c = _mesh_pos()
    copies = []
    for k in (1, 2, 3):
        tx, ty = _flip(x, k & 2), _flip(y, k & 1)
        copies.append(_remote(p_ref.at[2 * tx + ty], land_ref.at[k - 1], send_sems, recv_sems, k - 1, (tx, ty, c)))
    return copies


class _Comm:
    def __init__(self, ins, out_shapes, sems, make, aliases=None):
        self.ins, self.out_shapes, self.sems, self.make, self.aliases = list(ins), list(out_shapes), list(sems), make, aliases or {}


def _comm_parts(comm, n_in, n_out):
    if comm is None:
        return [], [], [], [], {}
    alias = {n_in + j: n_out + k for j, k in comm.aliases.items()}
    return comm.ins, [_ANY] * len(comm.ins), comm.out_shapes, comm.sems, alias


def _split_refs(refs, n_in, n_out, n_scratch, comm):
    n_ci = len(comm.ins) if comm else 0
    n_co = len(comm.out_shapes) if comm else 0
    cuts, i = [], 0
    for n in (n_in, n_ci, n_out, n_co, n_scratch):
        cuts.append(refs[i:i + n])
        i += n
    return (*cuts, refs[i:])


def _grid_edge(grid, last):
    cond = None
    for d, n in enumerate(grid):
        here = pl.program_id(d) == (n - 1 if last else 0)
        cond = here if cond is None else cond & here
    return cond


def _comm_start(comm, cin, cout, csem, grid):
    if comm is not None:
        @pl.when(_grid_edge(grid, False))
        def _():
            for cp in comm.make(cin, cout, csem):
                cp.start()


def _comm_finish(comm, cin, cout, csem, grid):
    if comm is not None:
        @pl.when(_grid_edge(grid, True))
        def _():
            for cp in comm.make(cin, cout, csem):
                cp.wait()


def gather_first_comm(shard):
    r, w = shard.shape

    def make(cin, cout, sems):
        (x_ref,), (out_ref,), (send_sems, recv_sems, local_sem) = cin, cout, sems
        x, y, c = _mesh_pos()
        mine = out_ref.at[4 * x + 2 * y + c]
        targets = [(x, y, 1 - c), (1 - x, y, c), (x, 1 - y, c), (1 - x, 1 - y, c)]
        return [_remote(x_ref, mine, send_sems, recv_sems, k, to) for k, to in enumerate(targets)] + [
            pltpu.make_async_copy(x_ref, mine, local_sem)]

    return _Comm([shard], [jax.ShapeDtypeStruct((N_DEV, r, w), shard.dtype)],
                 [pltpu.SemaphoreType.DMA((4,)), pltpu.SemaphoreType.DMA((4,)), pltpu.SemaphoreType.DMA], make)


def gather_pass_comm(full):
    def make(cin, cout, sems):
        (in_ref,), (out_ref,), (send_sems, recv_sems) = cin, cout, sems
        x, y, c = _mesh_pos()
        copies = []
        for k, (px, py) in enumerate([(1 - x, y), (x, 1 - y), (1 - x, 1 - y)]):
            idx = 4 * px + 2 * py + c
            copies.append(_remote(in_ref.at[idx], out_ref.at[idx], send_sems, recv_sems, k, (x, y, 1 - c)))
        return copies

    return _Comm([full], [jax.ShapeDtypeStruct(full.shape, full.dtype)],
                 [pltpu.SemaphoreType.DMA((3,)), pltpu.SemaphoreType.DMA((3,))], make, aliases={0: 0})


def scatter_sibling_comm(g):
    _, r, w = g.shape
    return _Comm([g], [jax.ShapeDtypeStruct((N_CHIP, r, w), g.dtype)],
                 [pltpu.SemaphoreType.DMA((N_CHIP,)), pltpu.SemaphoreType.DMA((N_CHIP,))],
                 lambda cin, cout, sems: _sibling_copies(cin[0], cout[0], sems[0], sems[1]))


def scatter_chips_comm(part):
    _, r, w = part.shape
    return _Comm([part], [jax.ShapeDtypeStruct((N_CHIP - 1, r, w), part.dtype)],
                 [pltpu.SemaphoreType.DMA((3,)), pltpu.SemaphoreType.DMA((3,))],
                 lambda cin, cout, sems: _chip_copies(cin[0], cout[0], sems[0], sems[1]))


def allgather_rows(shard):
    r, w = shard.shape

    def body(x_ref, out_ref, send_sems, recv_sems, local_sem):
        x, y, c = _mesh_pos()
        me, sibling = (x, y, c), (x, y, 1 - c)
        chips = [(1 - x, y), (x, 1 - y), (1 - x, 1 - y)]

        def slot(px, py, pc):
            return out_ref.at[4 * px + 2 * py + pc]

        def copy(k, block, to, src=None):
            return pltpu.make_async_remote_copy(
                src_ref=slot(*block) if src is None else src, dst_ref=slot(*block),
                send_sem=send_sems.at[k], recv_sem=recv_sems.at[k], device_id=to, device_id_type=_MESH)

        mine = pltpu.make_async_copy(x_ref, slot(*me), local_sem)
        mine.start()
        first = [copy(0, me, sibling, src=x_ref)]
        first += [copy(1 + j, me, (*chip, c), src=x_ref) for j, chip in enumerate(chips)]
        for cp in first:
            cp.start()
        passed = [copy(4 + j, (*chip, c), sibling) for j, chip in enumerate(chips)]
        for j, chip in enumerate(chips):
            copy(1 + j, (*chip, c), me).wait_recv()
            passed[j].start()
        copy(0, sibling, me).wait_recv()
        for j, chip in enumerate(chips):
            copy(4 + j, (*chip, 1 - c), me).wait_recv()
        for cp in first + passed:
            cp.wait_send()
        mine.wait()

    return pl.pallas_call(
        body, name="allgather_rows",
        out_shape=jax.ShapeDtypeStruct((N_DEV, r, w), shard.dtype),
        in_specs=[_ANY], out_specs=_ANY,
        scratch_shapes=[pltpu.SemaphoreType.DMA((7,)), pltpu.SemaphoreType.DMA((7,)), pltpu.SemaphoreType.DMA],
    )(shard)


def exchange_sibling(g):
    _, r, w = g.shape

    def body(g_ref, got_ref, send_sems, recv_sems):
        copies = _sibling_copies(g_ref, got_ref, send_sems, recv_sems)
        for cp in copies:
            cp.start()
        for cp in copies:
            cp.wait()

    return pl.pallas_call(
        body, name="exchange_sibling", out_shape=jax.ShapeDtypeStruct((N_CHIP, r, w), g.dtype),
        in_specs=[_ANY], out_specs=_ANY,
        scratch_shapes=[pltpu.SemaphoreType.DMA((N_CHIP,)), pltpu.SemaphoreType.DMA((N_CHIP,))],
    )(g)


def exchange_chips(part):
    _, r, w = part.shape

    def body(p_ref, land_ref, send_sems, recv_sems):
        copies = _chip_copies(p_ref, land_ref, send_sems, recv_sems)
        for cp in copies:
            cp.start()
        for cp in copies:
            cp.wait()

    return pl.pallas_call(
        body, name="exchange_chips", out_shape=jax.ShapeDtypeStruct((N_CHIP - 1, r, w), part.dtype),
        in_specs=[_ANY], out_specs=_ANY,
        scratch_shapes=[pltpu.SemaphoreType.DMA((3,)), pltpu.SemaphoreType.DMA((3,))],
    )(part)


def allreduce_stats(st_mix, st_prep, st_mlp, st_ple):
    def body(mix_ref, prep_ref, mlp_ref, ple_ref, out_ref, mine, gath, send_sems, recv_sems):
        x, y, c = _mesh_pos()
        me = 4 * x + 2 * y + c
        mine[...] = jnp.zeros_like(mine)
        mine[ST_G_MIX:ST_G_MIX + 1, :] = mix_ref[...]
        mine[ST_G_QA:ST_G_KN + 1, 0:256] = prep_ref[...]
        mine[ST_G_MLP:ST_G_MLP + 1, :] = mlp_ref[...]
        mine[ST_G_PLE:ST_LOSS + 1, :] = ple_ref[...]
        gath[me] = mine[...]
        copies = []
        for k in range(1, N_DEV):
            peer = (_flip(x, k & 4), _flip(y, k & 2), _flip(c, k & 1))
            copies.append(_remote(mine, gath.at[me], send_sems, recv_sems, k - 1, peer))
        for cp in copies:
            cp.start()
        for cp in copies:
            cp.wait()
        acc = gath[0]
        for d in range(1, N_DEV):
            acc = acc + gath[d]
        out_ref[...] = acc

    vm = pl.BlockSpec(memory_space=pltpu.VMEM)
    return pl.pallas_call(
        body, name="allreduce_stats", out_shape=jax.ShapeDtypeStruct((ST_ROWS, D_MODEL), F32),
        in_specs=[vm] * 4, out_specs=vm,
        scratch_shapes=[pltpu.VMEM((ST_ROWS, D_MODEL), F32), pltpu.VMEM((N_DEV, ST_ROWS, D_MODEL), F32),
                        pltpu.SemaphoreType.DMA((N_DEV - 1,)), pltpu.SemaphoreType.DMA((N_DEV - 1,))],
    )(st_mix, st_prep, st_mlp, st_ple)


def adamw_gains(stats, gains):
    c1 = 1.0 - ADAM_B1 ** ADAM_STEP
    c2 = 1.0 - ADAM_B2 ** ADAM_STEP
    n = len(gains)

    def body(st_ref, *refs):
        ins, outs = refs[:3 * n], refs[3 * n:]
        for i, (row, w, _, _) in enumerate(gains):
            width = w.shape[1]
            gv = st_ref[row:row + 1, 0:width]
            mn = ADAM_B1 * ins[3 * i + 1][...] + (1.0 - ADAM_B1) * gv
            vn = ADAM_B2 * ins[3 * i + 2][...] + (1.0 - ADAM_B2) * (gv * gv)
            outs[4 * i][...] = gv
            outs[4 * i + 1][...] = -ADAM_LR * ((mn / c1) / (jnp.sqrt(vn / c2) + ADAM_EPS) + ADAM_WD * ins[3 * i][...])
            outs[4 * i + 2][...] = mn
            outs[4 * i + 3][...] = vn

    vm = pl.BlockSpec(memory_space=pltpu.VMEM)
    flat = [a for (_, w, m, v) in gains for a in (w, m, v)]
    out_shape = tuple(jax.ShapeDtypeStruct(w.shape, F32) for (_, w, _, _) in gains for _ in range(4))
    res = pl.pallas_call(body, name="adamw_gains", out_shape=out_shape, in_specs=[vm] * (1 + 3 * n),
                         out_specs=tuple([vm] * (4 * n)))(stats, *flat)
    return [res[4 * i:4 * i + 4] for i in range(n)]


def _row_tile(r, cap=640):
    return max(d for d in range(16, min(r, cap) + 1, 16) if r % d == 0)


def add_pairs(g, got, core):
    n, r, w = got.shape
    tr = _row_tile(r)

    def body(c_ref, a_ref, b_ref, o_ref):
        o_ref[...] = (a_ref[...].astype(F32) + b_ref[...].astype(F32)).astype(o_ref.dtype)

    spec = pl.BlockSpec((1, tr, w), lambda i, j, c: (i, j, 0))
    return pl.pallas_call(
        body, name="add_pairs", out_shape=jax.ShapeDtypeStruct(got.shape, got.dtype),
        grid_spec=pltpu.PrefetchScalarGridSpec(
            num_scalar_prefetch=1, grid=(n, r // tr),
            in_specs=[pl.BlockSpec((1, tr, w), lambda i, j, c: (2 * i + c[0], j, 0)), spec], out_specs=spec),
        compiler_params=_params(("parallel", "parallel")),
    )(core, g, got)


def sum_chips(part, land, chip):
    _, r, w = part.shape
    tr = _row_tile(r)

    def body(c_ref, p_ref, l_ref, o_ref):
        acc = p_ref[0].astype(F32)
        for s in range(N_CHIP - 1):
            acc = acc + l_ref[s].astype(F32)
        o_ref[...] = acc

    return pl.pallas_call(
        body, name="sum_chips", out_shape=jax.ShapeDtypeStruct((r, w), F32),
        grid_spec=pltpu.PrefetchScalarGridSpec(
            num_scalar_prefetch=1, grid=(r // tr,),
            in_specs=[pl.BlockSpec((1, tr, w), lambda i, c: (c[0], i, 0)), pl.BlockSpec((N_CHIP - 1, tr, w), lambda i, c: (0, i, 0))],
            out_specs=pl.BlockSpec((tr, w), lambda i, c: (i, 0))),
        compiler_params=_params(("parallel",)),
    )(chip, part, land)


def in_proj(x, g_mix, w_in_t, tm):
    t = x.shape[0]
    nc = 512

    def body(x_ref, g_ref, w_ref, z_ref, h_ref):
        xv = x_ref[...]
        h = (xv * _rstd(xv, D_MODEL) * g_ref[...]).astype(BF16)
        h_ref[...] = h
        for cidx in range(ZP // nc):
            z_ref[:, cidx * nc:(cidx + 1) * nc] = _dot_nt(h, w_ref[cidx * nc:(cidx + 1) * nc, :])

    return pl.pallas_call(
        body, name="in_proj", grid=(t // tm,),
        out_shape=(jax.ShapeDtypeStruct((t, ZP), F32), jax.ShapeDtypeStruct((t, D_MODEL), BF16)),
        in_specs=[_rows(tm, D_MODEL), _resident((1, D_MODEL)), _resident((ZP, D_MODEL))],
        out_specs=(_rows(tm, ZP), _rows(tm, D_MODEL)), compiler_params=_params(("parallel",)),
    )(x, g_mix, w_in_t)


def attn_prep(zp, tabs, g_qa, g_kva, g_qn, g_kn, w_qb_t, w_kvb_t, tm, s_len):
    t = zp.shape[0]
    nsb = s_len // tm
    scale_a = (QK_NOPE + QK_ROPE) ** -0.5
    scale_b = HD_B ** -0.5

    def body(qb_ref, qlat_ref, kb_ref, vb_ref, ckv_ref, kpe_ref, tab_ref, gqa_ref, gkva_ref, gqn_ref, gkn_ref,
             wqb_ref, wkvb_ref, qa_o, ka_o, va_o, qb_o, kb_o, vb_o, cq_o, ckvn_o):
        ca, s1a, s2a = tab_ref[0], tab_ref[1], tab_ref[2]
        ck = tab_ref[3]
        cb, s1b, s2b = tab_ref[4], tab_ref[5], tab_ref[6]
        ql = qlat_ref[...]
        cq = (ql * _rstd(ql, Q_LORA) * gqa_ref[...]).astype(BF16)
        cq_o[...] = cq
        qa = _dot_nt(cq, wqb_ref[...])
        slabs = [slice(h * HP, (h + 1) * HP) for h in range(H_A)]
        qa_o[...] = jnp.concatenate(
            [(_rope_fwd(qa[:, sl], ca, s1a, s2a) * scale_a).astype(BF16) for sl in slabs], axis=1)
        cr = ckv_ref[...]
        ckv = (cr * _rstd(cr, KV_LORA) * gkva_ref[...]).astype(BF16)
        ckvn_o[...] = ckv
        kva = _dot_nt(ckv, wkvb_ref[...])
        kpe = _rope_fwd(kpe_ref[...], ck, s1a, s2a)
        ka_o[...] = jnp.concatenate([(kva[:, sl] + kpe).astype(BF16) for sl in slabs], axis=1)
        va_o[...] = kva[:, H_A * HP:].astype(BF16)
        gqn, gkn = gqn_ref[...], gkn_ref[...]

        def norm_rope(ref, sl, g, scale):
            xs = ref[:, sl]
            y = _rope_fwd(xs * _rstd(xs, HD_B) * g, cb, s1b, s2b)
            return (y if scale is None else y * scale).astype(BF16)

        qb_o[...] = jnp.concatenate([norm_rope(qb_ref, sl, gqn, scale_b) for sl in slabs], axis=1)
        kb_o[...] = jnp.concatenate([norm_rope(kb_ref, sl, gkn, None) for sl in slabs[:KV_B]], axis=1)
        vb_o[...] = vb_ref[...].astype(BF16)

    def o(width):
        return jax.ShapeDtypeStruct((t, width), BF16)

    return pl.pallas_call(
        body, name="attn_prep", grid=(t // tm,),
        out_shape=(o(H_A * HP), o(H_A * HP), o(H_A * HP), o(H_B * HP), o(KV_B * HP), o(KV_B * HP), o(Q_LORA), o(KV_LORA)),
        in_specs=[_rows(tm, 1024, 0), _rows(tm, 256, 12), _rows(tm, 256, 13), _rows(tm, 256, 14),
                  _rows(tm, 128, 30), _rows(tm, 128, 31),
                  pl.BlockSpec((7, tm, HP), lambda i: (0, i % nsb, 0)),
                  _resident((1, Q_LORA)), _resident((1, KV_LORA)), _resident((1, HP)), _resident((1, HP)),
                  _resident((H_A * HP, Q_LORA)), _resident((2 * H_A * HP, KV_LORA))],
        out_specs=(_rows(tm, H_A * HP), _rows(tm, H_A * HP), _rows(tm, H_A * HP), _rows(tm, H_B * HP),
                   _rows(tm, KV_B * HP), _rows(tm, KV_B * HP), _rows(tm, Q_LORA), _rows(tm, KV_LORA)),
        compiler_params=_params(("parallel",)),
    )(zp, zp, zp, zp, zp, zp, tabs, g_qa, g_kva, g_qn, g_kn, w_qb_t, w_kvb_t)


def attn_fwd(q, k, v, n_b, s_len, tq, name, comm=None):
    t = q.shape[0]
    n_h, n_hk = q.shape[1] // HP, k.shape[1] // HP
    grp = n_h // n_hk
    nq = s_len // tq
    sub = min(tq, 256)
    grid = (n_b, n_h, nq)
    c_ins, c_in_specs, c_outs, c_sems, alias = _comm_parts(comm, 3, 2)

    def body(*refs):
        (q_ref, k_ref, v_ref), cin, (o_ref, lse_ref), cout, _, csem = _split_refs(refs, 3, 2, 0, comm)
        _comm_start(comm, cin, cout, csem, grid)
        kv, vv = k_ref[...], v_ref[...]
        for r in range(tq // sub):
            rows = slice(r * sub, (r + 1) * sub)
            s = _dot_nt(q_ref[rows, :], kv)
            m = jnp.max(s, axis=-1, keepdims=True)
            p = jnp.exp(s - m)
            l = jnp.sum(p, axis=-1, keepdims=True)
            o_ref[rows, :] = (_dot_nn(p.astype(BF16), vv) * (1.0 / l)).astype(o_ref.dtype)
            lse_ref[rows, :] = jnp.broadcast_to(m + jnp.log(l), (sub, HP))
        _comm_finish(comm, cin, cout, csem, grid)

    qspec = pl.BlockSpec((tq, HP), lambda b, h, i: (b * nq + i, h))
    kspec = pl.BlockSpec((s_len, HP), lambda b, h, i: (b, h // grp))
    return pl.pallas_call(
        body, name=name, grid=grid,
        out_shape=(jax.ShapeDtypeStruct((t, n_h * HP), BF16), jax.ShapeDtypeStruct((t, n_h * HP), F32), *c_outs),
        in_specs=[qspec, kspec, kspec, *c_in_specs], out_specs=(qspec, qspec, *([_ANY] * len(c_outs))),
        scratch_shapes=c_sems, input_output_aliases=alias,
        compiler_params=_params(("arbitrary", "arbitrary", "arbitrary")),
    )(q, k, v, *c_ins)


def merge_fwd(oa, ob, zp, x, w_oa_t, w_ob_t, wpack, off, tm):
    t = x.shape[0]

    def body(oa_ref, ob_ref, ga_ref, gb_ref, x_ref, woa_ref, wob_ref, wo_ref, x1_o, mg_o):
        ya = _dot_nt(oa_ref[...], woa_ref[...])
        yb = _dot_nt(ob_ref[...], wob_ref[...])
        merged = (jax.nn.sigmoid(ga_ref[...]) * ya + jax.nn.sigmoid(gb_ref[...]) * yb).astype(BF16)
        mg_o[...] = merged
        x1_o[...] = x_ref[...] + _dot_nn(merged, _wrows(wo_ref, 0, D_MODEL))

    return pl.pallas_call(
        body, name="merge_fwd", grid=(t // tm,),
        out_shape=(jax.ShapeDtypeStruct((t, D_MODEL), F32), jax.ShapeDtypeStruct((t, D_MODEL), BF16)),
        in_specs=[_rows(tm, H_A * HP), _rows(tm, H_B * HP), _rows(tm, 1024, 1), _rows(tm, 1024, 2), _rows(tm, D_MODEL),
                  _resident((D_MODEL, H_A * HP)), _resident((D_MODEL, H_B * HP)), _packed_weight(128, off["w_o"])],
        out_specs=(_rows(tm, D_MODEL), _rows(tm, D_MODEL)), compiler_params=_params(("parallel",)),
    )(oa, ob, zp, zp, x, w_oa_t, w_ob_t, wpack)


def mlp_fwd(x1, g_mlp, wpack, off, tm):
    t = x1.shape[0]
    fc = 1024

    def body(x_ref, g_ref, wup_ref, wdn_ref, x2_o, u_o):
        xv = x_ref[...]
        h2 = (xv * _rstd(xv, D_MODEL) * g_ref[...]).astype(BF16)
        acc = xv
        for cidx in range(D_FF // fc):
            sl = slice(cidx * fc, (cidx + 1) * fc)
            u = jnp.maximum(_dot_nt(h2, _wrows(wup_ref, cidx * fc, fc)), 0.0)
            u_o[:, sl] = u.astype(BF16)
            acc = acc + _dot_nn((u * u).astype(BF16), _wrows(wdn_ref, cidx * fc, fc))
        x2_o[...] = acc

    return pl.pallas_call(
        body, name="mlp_fwd", grid=(t // tm,),
        out_shape=(jax.ShapeDtypeStruct((t, D_MODEL), F32), jax.ShapeDtypeStruct((t, D_FF), BF16)),
        in_specs=[_rows(tm, D_MODEL), _resident((1, D_MODEL)), _packed_weight(512, off["w_up"]), _packed_weight(512, off["w_down"])],
        out_specs=(_rows(tm, D_MODEL), _rows(tm, D_FF)), compiler_params=_params(("parallel",)),
    )(x1, g_mlp, wpack, wpack)


def ple_loss_bwd(x2, p, tgt, g_ple, g_final, wpack, off, w_ple_t, tm):
    t = x2.shape[0]
    inv_d = 1.0 / D_MODEL

    def body(x2_ref, p_ref, tg_ref, gp_ref, gf_ref, wpg_ref, wple_ref, dx2_o, dt_o, h3_o, dpe_o, st_o, dx2b_o):
        @pl.when(pl.program_id(0) == 0)
        def _():
            st_o[...] = jnp.zeros_like(st_o)

        x2v = x2_ref[...]
        gp, gf = gp_ref[...], gf_ref[...]
        w_pg = _wrows(wpg_ref, 0, D_MODEL)
        r2 = _rstd(x2v, D_MODEL)
        xh2 = x2v * r2
        h3 = (xh2 * gp).astype(BF16)
        h3_o[...] = h3
        gate = jax.nn.sigmoid(_dot_nn(h3, w_pg))
        pe = _dot_nt(p_ref[...].astype(BF16), wple_ref[...])
        x3 = x2v + gate * pe
        r3 = _rstd(x3, D_MODEL)
        xh3 = x3 * r3
        err = xh3 * gf - tg_ref[...]
        dy = err * inv_d
        dx3 = _rms_bwd(dy, xh3, r3, gf, D_MODEL)
        dpe_o[...] = (dx3 * gate).astype(BF16)
        dt = (dx3 * pe * gate * (1.0 - gate)).astype(BF16)
        dt_o[...] = dt
        dh3 = _dot_nt(dt, w_pg)
        dx2 = dx3 + _rms_bwd(dh3, xh2, r2, gp, D_MODEL)
        dx2_o[...] = dx2
        dx2b_o[...] = dx2.astype(BF16)
        st_o[0:1, :] += _colsum(dh3 * xh2)
        st_o[1:2, :] += _colsum(dy * xh3)
        st_o[2:3, :] += _colsum(err * err) * (0.5 * inv_d)

    bf = jax.ShapeDtypeStruct((t, D_MODEL), BF16)
    return pl.pallas_call(
        body, name="ple_loss_bwd", grid=(t // tm,),
        out_shape=(jax.ShapeDtypeStruct((t, D_MODEL), F32), bf, bf, bf, jax.ShapeDtypeStruct((3, D_MODEL), F32), bf),
        in_specs=[_rows(tm, D_MODEL), _rows(tm, PLE_DIM), _rows(tm, D_MODEL), _resident((1, D_MODEL)), _resident((1, D_MODEL)),
                  _packed_weight(128, off["w_ple_gate"]), _resident((D_MODEL, PLE_DIM))],
        out_specs=(_rows(tm, D_MODEL), _rows(tm, D_MODEL), _rows(tm, D_MODEL), _rows(tm, D_MODEL),
                   pl.BlockSpec((3, D_MODEL), lambda i: (0, 0)), _rows(tm, D_MODEL)),
        compiler_params=_params(("arbitrary",)),
    )(x2, p, tgt, g_ple, g_final, wpack, w_ple_t)


def mlp_bwd(dx2, x1, u, g_mlp, wpack, off, tm):
    t = x1.shape[0]
    fc = 1024

    def body(dx2_ref, x1_ref, u_ref, g_ref, wup_ref, wdn_ref, dx1_o, da_o, h2_o, st_o, dx1b_o):
        @pl.when(pl.program_id(0) == 0)
        def _():
            st_o[...] = jnp.zeros_like(st_o)

        d2 = dx2_ref[...]
        d2b = d2.astype(BF16)
        dh2 = jnp.zeros((tm, D_MODEL), F32)
        for cidx in range(D_FF // fc):
            sl = slice(cidx * fc, (cidx + 1) * fc)
            da = (_dot_nt(d2b, _wrows(wdn_ref, cidx * fc, fc)) * (2.0 * u_ref[:, sl].astype(F32))).astype(BF16)
            da_o[:, sl] = da
            dh2 = dh2 + _dot_nn(da, _wrows(wup_ref, cidx * fc, fc))
        xv = x1_ref[...]
        g = g_ref[...]
        r1 = _rstd(xv, D_MODEL)
        xh1 = xv * r1
        h2_o[...] = (xh1 * g).astype(BF16)
        st_o[...] += _colsum(dh2 * xh1)
        dx1 = d2 + _rms_bwd(dh2, xh1, r1, g, D_MODEL)
        dx1_o[...] = dx1
        dx1b_o[...] = dx1.astype(BF16)

    return pl.pallas_call(
        body, name="mlp_bwd", grid=(t // tm,),
        out_shape=(jax.ShapeDtypeStruct((t, D_MODEL), F32), jax.ShapeDtypeStruct((t, D_FF), BF16),
                   jax.ShapeDtypeStruct((t, D_MODEL), BF16), jax.ShapeDtypeStruct((1, D_MODEL), F32),
                   jax.ShapeDtypeStruct((t, D_MODEL), BF16)),
        in_specs=[_rows(tm, D_MODEL), _rows(tm, D_MODEL), _rows(tm, D_FF), _resident((1, D_MODEL)),
                  _packed_weight(512, off["w_up"]), _packed_weight(512, off["w_down"])],
        out_specs=(_rows(tm, D_MODEL), _rows(tm, D_FF), _rows(tm, D_MODEL), pl.BlockSpec((1, D_MODEL), lambda i: (0, 0)),
                   _rows(tm, D_MODEL)),
        compiler_params=_params(("arbitrary",)),
    )(dx2, x1, u, g_mlp, wpack, wpack)


def merge_bwd(dx1, oa, ob, zp, w_oa_t, w_ob_t, wpack, off, tm, comm=None):
    t = dx1.shape[0]
    grid = (t // tm,)
    c_ins, c_in_specs, c_outs, c_sems, alias = _comm_parts(comm, 8, 5)

    def body(*refs):
        ((dx1_ref, oa_ref, ob_ref, ga_ref, gb_ref, woa_ref, wob_ref, wo_ref), cin,
         (doa_o, dob_o, dg_o, dya_o, dyb_o), cout, _, csem) = _split_refs(refs, 8, 5, 0, comm)
        _comm_start(comm, cin, cout, csem, grid)
        dm = _dot_nt(dx1_ref[...].astype(BF16), _wrows(wo_ref, 0, D_MODEL))
        for o_ref, g_ref, w_ref, do_o, dy_o, col in ((oa_ref, ga_ref, woa_ref, doa_o, dya_o, 0),
                                                     (ob_ref, gb_ref, wob_ref, dob_o, dyb_o, 1)):
            yv = _dot_nt(o_ref[...], w_ref[...])
            sg = jax.nn.sigmoid(g_ref[...])
            dyv = (dm * sg).astype(BF16)
            dy_o[...] = dyv
            dg_o[:, col * D_MODEL:(col + 1) * D_MODEL] = (dm * yv * sg * (1.0 - sg)).astype(BF16)
            do_o[...] = _dot_nn(dyv, w_ref[...]).astype(BF16)
        _comm_finish(comm, cin, cout, csem, grid)

    bf = jax.ShapeDtypeStruct((t, D_MODEL), BF16)
    return pl.pallas_call(
        body, name="merge_bwd", grid=grid,
        out_shape=(bf, bf, jax.ShapeDtypeStruct((t, 2 * D_MODEL), BF16), bf, bf, *c_outs),
        in_specs=[_rows(tm, D_MODEL), _rows(tm, H_A * HP), _rows(tm, H_B * HP), _rows(tm, 1024, 1), _rows(tm, 1024, 2),
                  _resident((D_MODEL, H_A * HP)), _resident((D_MODEL, H_B * HP)), _packed_weight(128, off["w_o"]), *c_in_specs],
        out_specs=(_rows(tm, D_MODEL), _rows(tm, D_MODEL), _rows(tm, 2 * D_MODEL), _rows(tm, D_MODEL), _rows(tm, D_MODEL),
                   *([_ANY] * len(c_outs))),
        scratch_shapes=c_sems, input_output_aliases=alias,
        compiler_params=_params(("arbitrary",)),
    )(dx1, oa, ob, zp, zp, w_oa_t, w_ob_t, wpack, *c_ins)


def attn_bwd(q, k, v, do, o, lse, n_b, s_len, tq, name, comm=None):
    t = q.shape[0]
    n_h, n_hk = q.shape[1] // HP, k.shape[1] // HP
    grp = n_h // n_hk
    nq = s_len // tq
    sub = min(tq, 256)
    grid = (n_b, n_hk, grp, nq)
    c_ins, c_in_specs, c_outs, c_sems, alias = _comm_parts(comm, 6, 3)

    def body(*refs):
        ((q_ref, k_ref, v_ref, do_ref, o_ref, lse_ref), cin, (dq_o, dk_o, dv_o), cout, (p_s, ds_s),
         csem) = _split_refs(refs, 6, 3, 2, comm)
        _comm_start(comm, cin, cout, csem, grid)

        @pl.when((pl.program_id(2) == 0) & (pl.program_id(3) == 0))
        def _():
            dk_o[...] = jnp.zeros_like(dk_o)
            dv_o[...] = jnp.zeros_like(dv_o)

        kv, vv = k_ref[...], v_ref[...]
        for r in range(tq // sub):
            rows = slice(r * sub, (r + 1) * sub)
            qv, dov = q_ref[rows, :], do_ref[rows, :]
            delta = jnp.sum(dov.astype(F32) * o_ref[rows, :].astype(F32), axis=-1, keepdims=True)
            p = jnp.exp(_dot_nt(qv, kv) - lse_ref[rows, 0:1])
            ds = (p * (_dot_nt(dov, vv) - delta)).astype(BF16)
            p_s[rows, :] = p.astype(BF16)
            ds_s[rows, :] = ds
            dq_o[rows, :] = _dot_nn(ds, kv)
        dk_o[...] += _dot_tn(ds_s[...], q_ref[...])
        dv_o[...] += _dot_tn(p_s[...], do_ref[...])
        _comm_finish(comm, cin, cout, csem, grid)

    qspec = pl.BlockSpec((tq, HP), lambda b, hk, g, i: (b * nq + i, hk * grp + g))
    kspec = pl.BlockSpec((s_len, HP), lambda b, hk, g, i: (b, hk))
    return pl.pallas_call(
        body, name=name, grid=grid,
        out_shape=(jax.ShapeDtypeStruct((t, n_h * HP), F32), jax.ShapeDtypeStruct((t, n_hk * HP), F32),
                   jax.ShapeDtypeStruct((t, n_hk * HP), F32), *c_outs),
        in_specs=[qspec, kspec, kspec, qspec, qspec, qspec, *c_in_specs],
        out_specs=(qspec, kspec, kspec, *([_ANY] * len(c_outs))),
        scratch_shapes=[pltpu.VMEM((tq, s_len), BF16), pltpu.VMEM((tq, s_len), BF16), *c_sems],
        input_output_aliases=alias,
        compiler_params=_params(("arbitrary", "arbitrary", "arbitrary", "arbitrary")),
    )(q, k, v, do, o, lse, *c_ins)


def prep_bwd(dqa, dka, dva, dqb, dkb, dvb, zp, tabs, g_qa, g_kva, g_qn, g_kn, w_qb_t, w_kvb_t, tm, s_len):
    t = zp.shape[0]
    nsb = s_len // tm
    scale_a = (QK_NOPE + QK_ROPE) ** -0.5
    scale_b = HD_B ** -0.5

    def body(dqa_ref, dka_ref, dva_ref, dqb_ref, dkb_ref, dvb_ref, qb_ref, qlat_ref, kb_ref, ckv_ref, tab_ref,
             gqa_ref, gkva_ref, gqn_ref, gkn_ref, wqb_ref, wkvb_ref, dzq_o, dsm_o, dqap_o, dkva_o, st_o):
        @pl.when(pl.program_id(0) == 0)
        def _():
            st_o[...] = jnp.zeros_like(st_o)

        ca, s1a, s2a = tab_ref[0], tab_ref[1], tab_ref[2]
        ck = tab_ref[3]
        cb, s1b, s2b = tab_ref[4], tab_ref[5], tab_ref[6]
        for h in range(H_A):
            sl = slice(h * HP, (h + 1) * HP)
            dqap_o[:, sl] = _rope_bwd(dqa_ref[:, sl] * scale_a, ca, s1a, s2a).astype(BF16)
        dcq = _dot_nn(dqap_o[...], wqb_ref[...])
        ql = qlat_ref[...]
        rq = _rstd(ql, Q_LORA)
        xh = ql * rq
        gqa = gqa_ref[...]
        st_o[0:1, :] += _colsum(dcq * xh)
        dsm_o[:, 0:256] = _rms_bwd(dcq, xh, rq, gqa, Q_LORA).astype(BF16)
        dkpe = jnp.zeros((tm, HP), F32)
        for h in range(H_A):
            sl = slice(h * HP, (h + 1) * HP)
            dk = dka_ref[:, sl]
            dkpe = dkpe + dk
            dkva_o[:, sl] = dk.astype(BF16)
        dkva_o[:, H_A * HP:] = dva_ref[...].astype(BF16)
        dsm_o[:, 896:1024] = _rope_bwd(dkpe, ck, s1a, s2a).astype(BF16)
        dckv = _dot_nn(dkva_o[...], wkvb_ref[...])
        cr = ckv_ref[...]
        rk = _rstd(cr, KV_LORA)
        xh = cr * rk
        st_o[1:2, 0:128] += _colsum(dckv * xh)
        dsm_o[:, 768:896] = _rms_bwd(dckv, xh, rk, gkva_ref[...], KV_LORA).astype(BF16)
        gqn, gkn = gqn_ref[...], gkn_ref[...]
        dgq = jnp.zeros((1, HP), F32)
        for h in range(H_B):
            sl = slice(h * HP, (h + 1) * HP)
            dy = _rope_bwd(dqb_ref[:, sl] * scale_b, cb, s1b, s2b)
            xs = qb_ref[:, sl]
            r = _rstd(xs, HD_B)
            xh = xs * r
            dgq = dgq + _colsum(dy * xh)
            dzq_o[:, sl] = _rms_bwd(dy, xh, r, gqn, HD_B).astype(BF16)
        st_o[2:3, 0:128] += dgq
        dgk = jnp.zeros((1, HP), F32)
        for h in range(KV_B):
            sl = slice(h * HP, (h + 1) * HP)
            dy = _rope_bwd(dkb_ref[:, sl], cb, s1b, s2b)
            xs = kb_ref[:, sl]
            r = _rstd(xs, HD_B)
            xh = xs * r
            dgk = dgk + _colsum(dy * xh)
            dsm_o[:, 256 + h * HP:256 + (h + 1) * HP] = _rms_bwd(dy, xh, r, gkn, HD_B).astype(BF16)
        st_o[3:4, 0:128] += dgk
        dsm_o[:, 512:768] = dvb_ref[...].astype(BF16)

    bf = jax.ShapeDtypeStruct((t, 1024), BF16)
    return pl.pallas_call(
        body, name="prep_bwd", grid=(t // tm,),
        out_shape=(bf, bf, bf, jax.ShapeDtypeStruct((t, 2048), BF16), jax.ShapeDtypeStruct((4, 256), F32)),
        in_specs=[_rows(tm, 1024), _rows(tm, 1024), _rows(tm, 1024), _rows(tm, 1024), _rows(tm, 256), _rows(tm, 256),
                  _rows(tm, 1024, 0), _rows(tm, 256, 12), _rows(tm, 256, 13), _rows(tm, 128, 30),
                  pl.BlockSpec((7, tm, HP), lambda i: (0, i % nsb, 0)),
                  _resident((1, Q_LORA)), _resident((1, KV_LORA)), _resident((1, HP)), _resident((1, HP)),
                  _resident((H_A * HP, Q_LORA)), _resident((2 * H_A * HP, KV_LORA))],
        out_specs=(_rows(tm, 1024), _rows(tm, 1024), _rows(tm, 1024), _rows(tm, 2048), pl.BlockSpec((4, 256), lambda i: (0, 0))),
        compiler_params=_params(("arbitrary",)),
    )(dqa, dka, dva, dqb, dkb, dvb, zp, zp, zp, zp, tabs, g_qa, g_kva, g_qn, g_kn, w_qb_t, w_kvb_t)


def in_bwd(dzq, dgab, dsm, x, dx1, g_mix, w_in_t, tm, comm=None):
    t = x.shape[0]
    grid = (t // tm,)
    c_ins, c_in_specs, c_outs, c_sems, alias = _comm_parts(comm, 7, 2)

    def body(*refs):
        ((dzq_ref, dg_ref, dsm_ref, x_ref, dx1_ref, g_ref, w_ref), cin, (dx_o, st_o), cout, _,
         csem) = _split_refs(refs, 7, 2, 0, comm)
        _comm_start(comm, cin, cout, csem, grid)

        @pl.when(pl.program_id(0) == 0)
        def _():
            st_o[...] = jnp.zeros_like(st_o)

        dh = _dot_nn(dzq_ref[...], w_ref[0:1024, :])
        dh = dh + _dot_nn(dg_ref[...], w_ref[1024:3072, :])
        dh = dh + _dot_nn(dsm_ref[...], w_ref[3072:4096, :])
        xv = x_ref[...]
        g = g_ref[...]
        r = _rstd(xv, D_MODEL)
        xh = xv * r
        st_o[...] += _colsum(dh * xh)
        dx_o[...] = dx1_ref[...] + _rms_bwd(dh, xh, r, g, D_MODEL)
        _comm_finish(comm, cin, cout, csem, grid)

    return pl.pallas_call(
        body, name="in_bwd", grid=grid,
        out_shape=(jax.ShapeDtypeStruct((t, D_MODEL), F32), jax.ShapeDtypeStruct((1, D_MODEL), F32), *c_outs),
        in_specs=[_rows(tm, 1024), _rows(tm, 2048), _rows(tm, 1024), _rows(tm, D_MODEL), _rows(tm, D_MODEL),
                  _resident((1, D_MODEL)), _resident((ZP, D_MODEL)), *c_in_specs],
        out_specs=(_rows(tm, D_MODEL), pl.BlockSpec((1, D_MODEL), lambda i: (0, 0)), *([_ANY] * len(c_outs))),
        scratch_shapes=c_sems, input_output_aliases=alias,
        compiler_params=_params(("arbitrary",)),
    )(dzq, dgab, dsm, x, dx1, g_mix, w_in_t, *c_ins)


def matmul_tn(a, b, name, square_a=False):
    t, m = a.shape
    n = b.shape[1]
    bm = min(m, 512)
    tk = min(t, 4096)

    def body(a_ref, b_ref, o_ref):
        @pl.when(pl.program_id(1) == 0)
        def _():
            o_ref[...] = jnp.zeros_like(o_ref)

        av = a_ref[...]
        if square_a:
            av = (av.astype(F32) * av.astype(F32))
        o_ref[...] += _dot_tn(av.astype(BF16), b_ref[...].astype(BF16))

    return pl.pallas_call(
        body, name=name, grid=(m // bm, t // tk), out_shape=jax.ShapeDtypeStruct((m, n), F32),
        in_specs=[pl.BlockSpec((tk, bm), lambda i, kk: (kk, i)), pl.BlockSpec((tk, n), lambda i, kk: (kk, 0))],
        out_specs=pl.BlockSpec((bm, n), lambda i, kk: (i, 0)),
        compiler_params=_params(("parallel", "arbitrary")),
    )(a, b)


def matmul_tn_packed(a, b, name, rows, row_off, total_rows, buf=None, square_a=False):
    t, m = a.shape
    n = b.shape[1]
    pd = max(1, 512 // rows)
    bm = pd * rows
    tk = min(t, 4096)
    nk = t // tk

    def body(a_ref, b_ref, *rest):
        o_ref, acc = rest[-2], rest[-1]

        @pl.when(pl.program_id(1) == 0)
        def _():
            acc[...] = jnp.zeros_like(acc)

        av = a_ref[...]
        if square_a:
            av = (av.astype(F32) * av.astype(F32))
        acc[...] += _dot_tn(av.astype(BF16), b_ref[...].astype(BF16))

        @pl.when(pl.program_id(1) == nk - 1)
        def _():
            o_ref[...] = acc[...].reshape(pd, rows, n).astype(o_ref.dtype)

    in_specs = [pl.BlockSpec((tk, bm), lambda i, kk: (kk, i)), pl.BlockSpec((tk, n), lambda i, kk: (kk, 0))]
    args = [a, b]
    if buf is not None:
        in_specs.append(_ANY)
        args.append(buf)
    return pl.pallas_call(
        body, name=name, grid=(m // bm, nk), out_shape=jax.ShapeDtypeStruct((N_DEV, total_rows, n), BF16),
        in_specs=in_specs, out_specs=pl.BlockSpec((pd, rows, n), lambda i, kk: (i, row_off // rows, 0)),
        scratch_shapes=[pltpu.VMEM((bm, n), F32)], input_output_aliases={2: 0} if buf is not None else {},
        compiler_params=_params(("parallel", "arbitrary")),
    )(*args)


def adamw(w, g, m, v, name):
    r, c = w.shape
    tr = r if r <= 256 else 256
    c1 = 1.0 - ADAM_B1 ** ADAM_STEP
    c2 = 1.0 - ADAM_B2 ** ADAM_STEP

    def body(w_ref, g_ref, m_ref, v_ref, d_o, m_o, v_o):
        gv = g_ref[...]
        mn = ADAM_B1 * m_ref[...] + (1.0 - ADAM_B1) * gv
        vn = ADAM_B2 * v_ref[...] + (1.0 - ADAM_B2) * (gv * gv)
        m_o[...] = mn
        v_o[...] = vn
        d_o[...] = -ADAM_LR * ((mn / c1) / (jnp.sqrt(vn / c2) + ADAM_EPS) + ADAM_WD * w_ref[...])

    spec = pl.BlockSpec((tr, c), lambda i: (i, 0))
    shp = jax.ShapeDtypeStruct((r, c), F32)
    return pl.pallas_call(
        body, name=name, grid=(r // tr,), out_shape=(shp, shp, shp), in_specs=[spec] * 4, out_specs=(spec,) * 3,
        compiler_params=_params(("parallel",)),
    )(w, g, m, v)


def _rope_tables(s_len):
    def angles(pos, dim):
        inv = np.float32(ROPE_THETA) ** (-np.arange(0, dim, 2, dtype=np.float32) / np.float32(dim))
        return pos.astype(np.float32)[:, None] * inv[None, :]

    tpos = np.arange(s_len)
    a1 = angles(tpos, QK_ROPE)
    ar = angles(tpos // GRID_W, HD_B // 2)
    ac = angles(tpos % GRID_W, HD_B // 2)
    z16 = np.zeros((s_len, 16), np.float32)
    z32 = np.zeros((s_len, 32), np.float32)
    z64 = np.zeros((s_len, 64), np.float32)
    one64 = np.ones((s_len, 64), np.float32)
    c1, s1 = np.cos(a1), np.sin(a1)
    ca = np.concatenate([one64, c1, c1, z32], axis=1)
    ck = np.concatenate([z64, c1, c1, z32], axis=1)
    s1a = np.concatenate([z64, -s1, z16, z32], axis=1)
    s2a = np.concatenate([z64, z16, s1, z32], axis=1)
    cr, sr, cc, sc = np.cos(ar), np.sin(ar), np.cos(ac), np.sin(ac)
    cb = np.concatenate([cr, cr, cc, cc, z64], axis=1)
    s1b = np.concatenate([-sr, z16, -sc, z16, z64], axis=1)
    s2b = np.concatenate([z16, sr, z16, sc, z64], axis=1)
    return jnp.asarray(np.stack([ca, s1a, s2a, ck, cb, s1b, s2b]).astype(np.float32))


def _pad_heads(a, n_heads, axis):
    shp = a.shape
    a = a.reshape(shp[:axis] + (n_heads, shp[axis] // n_heads) + shp[axis + 1:])
    pad = [(0, 0)] * a.ndim
    pad[axis + 1] = (0, HP - a.shape[axis + 1])
    a = jnp.pad(a, pad)
    return a.reshape(shp[:axis] + (n_heads * HP,) + shp[axis + 1:])


def _unpad_heads(a, n_heads, width, axis):
    shp = a.shape
    a = a.reshape(shp[:axis] + (n_heads, HP) + shp[axis + 1:])
    a = lax.slice_in_dim(a, 0, width, axis=axis + 1)
    return a.reshape(shp[:axis] + (n_heads * width,) + shp[axis + 1:])


def _pack_rows(blocks, names):
    parts = []
    for name in names:
        b = blocks[name]
        padr = PACK_ROWS[name] - b.shape[-2]
        if padr:
            b = jnp.pad(b, [(0, 0)] * (b.ndim - 2) + [(0, padr), (0, 0)])
        parts.append(b)
    return jnp.concatenate(parts, axis=parts[0].ndim - 2)


def _expand_w_in(wt):
    z64 = jnp.zeros((64, D_MODEL), wt.dtype)
    z32 = jnp.zeros((32, D_MODEL), wt.dtype)
    return jnp.concatenate([
        _pad_heads(wt[416:928], H_B, 0), wt[1184:2208], wt[2208:3232], wt[0:256],
        _pad_heads(wt[928:1056], KV_B, 0), _pad_heads(wt[1056:1184], KV_B, 0), wt[256:384],
        z64, wt[384:416], z32], axis=0)


def _collapse_w_in(dq, dg, ds):
    return jnp.concatenate([
        ds[0:256], ds[768:896], ds[960:992], _unpad_heads(dq, H_B, HD_B, 0), _unpad_heads(ds[256:512], KV_B, HD_B, 0),
        _unpad_heads(ds[512:768], KV_B, HD_B, 0), dg], axis=0)


def kernel(x, p, g_mix, w_in, g_qa, w_qb, g_kva, w_kvb, g_qn, g_kn, w_oa, w_ob, w_o, g_mlp, w_up, w_down, g_ple, w_ple_gate, w_ple, g_final, loss_target, m_g_mix, m_w_in, m_g_qa, m_w_qb, m_g_kva, m_w_kvb, m_g_qn, m_g_kn, m_w_oa, m_w_ob, m_w_o, m_g_mlp, m_w_up, m_w_down, m_g_ple, m_w_ple_gate, m_w_ple, m_g_final, v_g_mix, v_w_in, v_g_qa, v_w_qb, v_g_kva, v_w_kvb, v_g_qn, v_g_kn, v_w_oa, v_w_ob, v_w_o, v_g_mlp, v_w_up, v_w_down, v_g_ple, v_w_ple_gate, v_w_ple, v_g_final):
    n_b, s_len, _ = x.shape
    t = n_b * s_len
    tm = min(256, s_len)
    tq_f = min(1024, s_len)
    tq_b = min(1024, s_len)

    mats = dict(w_in=(w_in, m_w_in, v_w_in), w_qb=(w_qb, m_w_qb, v_w_qb), w_kvb=(w_kvb, m_w_kvb, v_w_kvb),
                w_oa=(w_oa, m_w_oa, v_w_oa), w_ob=(w_ob, m_w_ob, v_w_ob), w_o=(w_o, m_w_o, v_w_o),
                w_up=(w_up, m_w_up, v_w_up), w_down=(w_down, m_w_down, v_w_down),
                w_ple_gate=(w_ple_gate, m_w_ple_gate, v_w_ple_gate), w_ple=(w_ple, m_w_ple, v_w_ple))
    col_sharded = ("w_in", "w_qb", "w_kvb", "w_oa", "w_ob", "w_up", "w_ple")

    blocks = {}
    for name, (w, _, _) in mats.items():
        w2 = w[0]
        if name in col_sharded:
            w2 = w2.T
        blocks[name] = w2.reshape(-1, D_MODEL).astype(BF16)
    off_w1, _ = _pack_offsets(PACK_W1)
    off_w2, _ = _pack_offsets(PACK_W2)
    full1 = allgather_rows(_pack_rows(blocks, PACK_W1))
    pack2 = _pack_rows(blocks, PACK_W2)

    def gathered(full, offs, name, rows, width):
        return full[:, offs[name]:offs[name] + rows].reshape(-1, width)

    w_in_t = _expand_w_in(gathered(full1, off_w1, "w_in", 404, D_MODEL))
    w_qb_t = _pad_heads(gathered(full1, off_w1, "w_qb", 24, Q_LORA), H_A, 0)
    wkvb = gathered(full1, off_w1, "w_kvb", 16, KV_LORA).reshape(H_A, 2, 64, KV_LORA)
    w_kvb_t = jnp.concatenate([_pad_heads(wkvb[:, 0].reshape(-1, KV_LORA), H_A, 0),
                               _pad_heads(wkvb[:, 1].reshape(-1, KV_LORA), H_A, 0)], axis=0)

    tabs = _rope_tables(s_len)
    g_qn_p = jnp.pad(g_qn, ((0, 0), (0, HP - HD_B)))
    g_kn_p = jnp.pad(g_kn, ((0, 0), (0, HP - HD_B)))
    xf = x.reshape(t, D_MODEL)
    pf = p.reshape(t, PLE_DIM)
    tgt = loss_target.reshape(t, D_MODEL)

    zp, h = in_proj(xf, g_mix, w_in_t, tm)
    qa, ka, va, qb, kb, vb, cq, ckv = attn_prep(zp, tabs, g_qa, g_kva, g_qn_p, g_kn_p, w_qb_t, w_kvb_t, tm, s_len)
    oa, lse_a, full2 = attn_fwd(qa, ka, va, n_b, s_len, tq_f, "attn_a_fwd", comm=gather_first_comm(pack2))
    ob, lse_b, full2 = attn_fwd(qb, kb, vb, n_b, s_len, tq_f, "attn_b_fwd", comm=gather_pass_comm(full2))
    w_oa_t = _pad_heads(gathered(full2, off_w2, "w_oa", 64, H_A * V_DIM_A), H_A, 1)
    w_ob_t = _pad_heads(gathered(full2, off_w2, "w_ob", 64, H_B * HD_B), H_B, 1)
    w_ple_t = gathered(full2, off_w2, "w_ple", 32, PLE_DIM)
    x1, merged = merge_fwd(oa, ob, zp, xf, w_oa_t, w_ob_t, full2, off_w2, tm)
    x2, u = mlp_fwd(x1, g_mlp, full2, off_w2, tm)
    dx2, dt, h3, dpe, st_ple, dx2b = ple_loss_bwd(x2, pf, tgt, g_ple, g_final.reshape(1, D_MODEL), full2, off_w2, w_ple_t, tm)
    dx1, da, h2, st_mlp, dx1b = mlp_bwd(dx2, x1, u, g_mlp, full2, off_w2, tm)

    core = lax.axis_index("c").astype(jnp.int32).reshape(1)
    chip = (2 * lax.axis_index("x") + lax.axis_index("y")).astype(jnp.int32).reshape(1)

    def packed(gblocks, names):
        return _pack_rows({n: gblocks[n].reshape(N_DEV, -1, D_MODEL).astype(BF16) for n in names}, names)

    off_g1, rows_g1 = _pack_offsets(PACK_G1)
    gpack1 = matmul_tn_packed(da, h2, "gw_up", 512, off_g1["w_up"], rows_g1)
    gpack1 = matmul_tn_packed(u, dx2b, "gw_down", 512, off_g1["w_down"], rows_g1, buf=gpack1, square_a=True)
    gpack1 = matmul_tn_packed(h3, dt, "gw_pg", 128, off_g1["w_ple_gate"], rows_g1, buf=gpack1)
    gple = matmul_tn(dpe, pf, "gw_ple").reshape(N_DEV, -1, D_MODEL).astype(BF16)
    gpack1 = lax.dynamic_update_slice(gpack1, gple, (0, off_g1["w_ple"], 0))
    doa, dob, dgab, dya, dyb, got1 = merge_bwd(dx1, oa, ob, zp, w_oa_t, w_ob_t, full2, off_w2, tm,
                                               comm=scatter_sibling_comm(gpack1))
    part1 = add_pairs(gpack1, got1, core)
    dqa, dka, dva, land1 = attn_bwd(qa, ka, va, doa, oa, lse_a, n_b, s_len, tq_b, "attn_a_bwd", comm=scatter_chips_comm(part1))
    gshard1 = sum_chips(part1, land1, chip)

    off_g2, rows_g2 = _pack_offsets(PACK_G2)
    g2 = dict(w_oa=_unpad_heads(matmul_tn(dya, oa, "gw_oa"), H_A, V_DIM_A, 1),
              w_ob=_unpad_heads(matmul_tn(dyb, ob, "gw_ob"), H_B, HD_B, 1))
    gpack2 = matmul_tn_packed(merged, dx1b, "gw_o", 128, off_g2["w_o"], rows_g2)
    gpack2 = lax.dynamic_update_slice(gpack2, packed(g2, ("w_oa", "w_ob")), (0, off_g2["w_oa"], 0))
    part2 = add_pairs(gpack2, exchange_sibling(gpack2), core)
    dqb, dkb, dvb, land2 = attn_bwd(qb, kb, vb, dob, ob, lse_b, n_b, s_len, tq_b, "attn_b_bwd", comm=scatter_chips_comm(part2))
    gshard2 = sum_chips(part2, land2, chip)
    dzq, dsm, dqap, dkva, st_prep = prep_bwd(dqa, dka, dva, dqb, dkb, dvb, zp, tabs, g_qa, g_kva, g_qn_p, g_kn_p,
                                             w_qb_t, w_kvb_t, tm, s_len)

    gkv = matmul_tn(dkva, ckv, "gw_kvb")
    g3 = dict(
        w_in=_collapse_w_in(matmul_tn(dzq, h, "gw_in_q"), matmul_tn(dgab, h, "gw_in_g"), matmul_tn(dsm, h, "gw_in_s")),
        w_qb=_unpad_heads(matmul_tn(dqap, cq, "gw_qb"), H_A, QK_NOPE + QK_ROPE, 0),
        w_kvb=jnp.stack([_unpad_heads(gkv[:H_A * HP], H_A, 64, 0).reshape(H_A, 64, KV_LORA),
                         _unpad_heads(gkv[H_A * HP:], H_A, 64, 0).reshape(H_A, 64, KV_LORA)], axis=1))
    gpack3 = packed(g3, PACK_G3)
    part3 = add_pairs(gpack3, exchange_sibling(gpack3), core)
    grad_x, st_mix, land3 = in_bwd(dzq, dgab, dsm, xf, dx1, g_mix, w_in_t, tm, comm=scatter_chips_comm(part3))
    gshard3 = sum_chips(part3, land3, chip)
    off_g3, _ = _pack_offsets(PACK_G3)
    shards = {n: (gshard1, off_g1[n]) for n in PACK_G1}
    shards.update({n: (gshard2, off_g2[n]) for n in PACK_G2})
    shards.update({n: (gshard3, off_g3[n]) for n in PACK_G3})

    stats = allreduce_stats(st_mix, st_prep, st_mlp, st_ple)
    loss = jnp.sum(stats[ST_LOSS])

    out_g, out_d, out_m, out_v = {}, {}, {}, {}
    for name, (w, m, v) in mats.items():
        gshard, off = shards[name]
        shard_shape = w.shape[1:]
        if name in col_sharded:
            rows_t = shard_shape[1]
            g2 = gshard[off:off + (rows_t * shard_shape[0]) // D_MODEL].reshape(rows_t, shard_shape[0]).T
        else:
            g2 = gshard[off:off + shard_shape[0]]
        d2, m2, v2 = adamw(w[0], g2, m[0], v[0], "adamw_" + name)
        out_g[name], out_d[name], out_m[name], out_v[name] = g2[None], d2[None], m2[None], v2[None]

    gains = (("g_mix", g_mix, m_g_mix, v_g_mix, ST_G_MIX), ("g_qa", g_qa, m_g_qa, v_g_qa, ST_G_QA),
             ("g_kva", g_kva, m_g_kva, v_g_kva, ST_G_KVA), ("g_qn", g_qn, m_g_qn, v_g_qn, ST_G_QN),
             ("g_kn", g_kn, m_g_kn, v_g_kn, ST_G_KN), ("g_mlp", g_mlp, m_g_mlp, v_g_mlp, ST_G_MLP),
             ("g_ple", g_ple, m_g_ple, v_g_ple, ST_G_PLE), ("g_final", g_final, m_g_final, v_g_final, ST_G_FINAL))
    res = adamw_gains(stats, [(r_, w.reshape(1, -1), m.reshape(1, -1), v.reshape(1, -1)) for _, w, m, v, r_ in gains])
    for (name, w, _, _, _), (gg, gd, gm, gv) in zip(gains, res):
        out_g[name], out_d[name], out_m[name], out_v[name] = (a.reshape(w.shape) for a in (gg, gd, gm, gv))

    order = ("g_mix", "w_in", "g_qa", "w_qb", "g_kva", "w_kvb", "g_qn", "g_kn", "w_oa", "w_ob", "w_o", "g_mlp",
             "w_up", "w_down", "g_ple", "w_ple_gate", "w_ple", "g_final")
    return (loss, grad_x.reshape(x.shape), *[out_g[n] for n in order], *[out_d[n] for n in order],
            *[out_m[n] for n in order], *[out_v[n] for n in order])
```

```python
import numpy as np
import jax
import jax.numpy as jnp
from jax import lax
from jax.experimental import pallas as pl
from jax.experimental.pallas import tpu as pltpu

F32 = jnp.float32
BF16 = jnp.bfloat16

D_MODEL = 1024
EPS = 1e-6
ROPE_THETA = 10000.0
GRID_W = 64
H_A = 8
QK_NOPE = 64
QK_ROPE = 32
V_DIM_A = 64
Q_LORA = 256
KV_LORA = 128
H_B = 8
KV_B = 2
HD_B = 64
D_FF = 4 * D_MODEL
PLE_DIM = 256
HP = 128
ZP = 4096
N_DEV = 8
N_CHIP = 4

ADAM_LR = 0.001
ADAM_B1 = 0.9
ADAM_B2 = 0.999
ADAM_EPS = 1e-08
ADAM_WD = 0.01
ADAM_STEP = 10

VMEM_LIMIT = 52 * 1024 * 1024

PACK_ROWS = dict(w_in=416, w_qb=32, w_kvb=16, w_oa=64, w_ob=64, w_o=128, w_up=512, w_down=512, w_ple_gate=128, w_ple=32)
PACK_W1 = ("w_in", "w_qb", "w_kvb")
PACK_W2 = ("w_up", "w_down", "w_o", "w_ple_gate", "w_oa", "w_ob", "w_ple")
PACK_G1 = ("w_up", "w_down", "w_ple_gate", "w_ple")
PACK_G2 = ("w_o", "w_oa", "w_ob")
PACK_G3 = ("w_in", "w_qb", "w_kvb")


def _pack_offsets(names):
    off, o = {}, 0
    for n in names:
        off[n] = o
        o += PACK_ROWS[n]
    return off, o

ST_G_MIX, ST_G_QA, ST_G_KVA, ST_G_QN, ST_G_KN, ST_G_MLP, ST_G_PLE, ST_G_FINAL, ST_LOSS = range(9)
ST_ROWS = 16


def _dot_nn(a, b):
    return lax.dot_general(a, b, (((1,), (0,)), ((), ())), preferred_element_type=F32)


def _dot_nt(a, b):
    return lax.dot_general(a, b, (((1,), (1,)), ((), ())), preferred_element_type=F32)


def _dot_tn(a, b):
    return lax.dot_general(a, b, (((0,), (0,)), ((), ())), preferred_element_type=F32)


def _rstd(x, n):
    return lax.rsqrt(jnp.sum(x * x, axis=-1, keepdims=True) * (1.0 / n) + EPS)


def _rms_bwd(dy, xh, r, g, n):
    dxh = dy * g
    return r * (dxh - xh * (jnp.sum(dxh * xh, axis=-1, keepdims=True) * (1.0 / n)))


def _rope_fwd(x, c, s1, s2):
    return x * c + pltpu.roll(x, HP - 16, 1) * s1 + pltpu.roll(x, 16, 1) * s2


def _rope_bwd(d, c, s1, s2):
    return d * c + pltpu.roll(d * s1, 16, 1) + pltpu.roll(d * s2, HP - 16, 1)


def _colsum(v):
    return jnp.sum(v, axis=0, keepdims=True)


def _params(sem=None, vmem=VMEM_LIMIT):
    return pltpu.CompilerParams(dimension_semantics=sem, vmem_limit_bytes=vmem)


def _resident(shape):
    nd = len(shape)
    return pl.BlockSpec(shape, lambda *_: (0,) * nd, pipeline_mode=pl.Buffered(1))


def _rows(tm, width, col=0):
    return pl.BlockSpec((tm, width), lambda i: (i, col))


def _packed_weight(rows, off):
    return pl.BlockSpec((N_DEV, rows, D_MODEL), lambda *_: (0, off // rows, 0), pipeline_mode=pl.Buffered(1))


def _wrows(ref, start, size):
    rows = ref.shape[1]
    return ref[start // rows:(start + size) // rows].reshape(size, D_MODEL)


def _mesh_pos():
    return lax.axis_index("x"), lax.axis_index("y"), lax.axis_index("c")


def _flip(v, bit):
    return (1 - v) if bit else v


_ANY = pl.BlockSpec(memory_space=pl.ANY)
_MESH = pl.DeviceIdType.MESH


def _remote(src, dst, send_sems, recv_sems, k, to):
    return pltpu.make_async_remote_copy(src_ref=src, dst_ref=dst, send_sem=send_sems.at[k], recv_sem=recv_sems.at[k],
                                        device_id=to, device_id_type=_MESH)


def _sibling_copies(g_ref, got_ref, send_sems, recv_sems):
    x, y, c = _mesh_pos()
    return [_remote(g_ref.at[2 * j + (1 - c)], got_ref.at[j], send_sems, recv_sems, j, (x, y, 1 - c)) for j in range(N_CHIP)]


def _chip_copies(p_ref, land_ref, send_sems, recv_sems):
    x, y, c = _mesh_pos()
    copies = []
    for k in (1, 2, 3):
        tx, ty = _flip(x, k & 2), _flip(y, k & 1)
        copies.append(_remote(p_ref.at[2 * tx + ty], land_ref.at[k - 1], send_sems, recv_sems, k - 1, (tx, ty, c)))
    return copies


class _Comm:
    def __init__(self, ins, out_shapes, sems, make, aliases=None):
        self.ins, self.out_shapes, self.sems, self.make, self.aliases = list(ins), list(out_shapes), list(sems), make, aliases or {}


def _comm_parts(comm, n_in, n_out):
    if comm is None:
        return [], [], [], [], {}
    alias = {n_in + j: n_out + k for j, k in comm.aliases.items()}
    return comm.ins, [_ANY] * len(comm.ins), comm.out_shapes, comm.sems, alias


def _split_refs(refs, n_in, n_out, n_scratch, comm):
    n_ci = len(comm.ins) if comm else 0
    n_co = len(comm.out_shapes) if comm else 0
    cuts, i = [], 0
    for n in (n_in, n_ci, n_out, n_co, n_scratch):
        cuts.append(refs[i:i + n])
        i += n
    return (*cuts, refs[i:])


def _grid_edge(grid, last):
    cond = None
    for d, n in enumerate(grid):
        here = pl.program_id(d) == (n - 1 if last else 0)
        cond = here if cond is None else cond & here
    return cond


def _comm_start(comm, cin, cout, csem, grid):
    if comm is not None:
        @pl.when(_grid_edge(grid, False))
        def _():
            for cp in comm.make(cin, cout, csem):
                cp.start()


def _comm_finish(comm, cin, cout, csem, grid):
    if comm is not None:
        @pl.when(_grid_edge(grid, True))
        def _():
            for cp in comm.make(cin, cout, csem):
                cp.wait()


def gather_first_comm(shard):
    r, w = shard.shape

    def make(cin, cout, sems):
        (x_ref,), (out_ref,), (send_sems, recv_sems, local_sem) = cin, cout, sems
        x, y, c = _mesh_pos()
        mine = out_ref.at[4 * x + 2 * y + c]
        targets = [(x, y, 1 - c), (1 - x, y, c), (x, 1 - y, c), (1 - x, 1 - y, c)]
        return [_remote(x_ref, mine, send_sems, recv_sems, k, to) for k, to in enumerate(targets)] + [
            pltpu.make_async_copy(x_ref, mine, local_sem)]

    return _Comm([shard], [jax.ShapeDtypeStruct((N_DEV, r, w), shard.dtype)],
                 [pltpu.SemaphoreType.DMA((4,)), pltpu.SemaphoreType.DMA((4,)), pltpu.SemaphoreType.DMA], make)


def gather_pass_comm(full):
    def make(cin, cout, sems):
        (in_ref,), (out_ref,), (send_sems, recv_sems) = cin, cout, sems
        x, y, c = _mesh_pos()
        copies = []
        for k, (px, py) in enumerate([(1 - x, y), (x, 1 - y), (1 - x, 1 - y)]):
            idx = 4 * px + 2 * py + c
            copies.append(_remote(in_ref.at[idx], out_ref.at[idx], send_sems, recv_sems, k, (x, y, 1 - c)))
        return copies

    return _Comm([full], [jax.ShapeDtypeStruct(full.shape, full.dtype)],
                 [pltpu.SemaphoreType.DMA((3,)), pltpu.SemaphoreType.DMA((3,))], make, aliases={0: 0})


def scatter_sibling_comm(g):
    _, r, w = g.shape
    return _Comm([g], [jax.ShapeDtypeStruct((N_CHIP, r, w), g.dtype)],
                 [pltpu.SemaphoreType.DMA((N_CHIP,)), pltpu.SemaphoreType.DMA((N_CHIP,))],
                 lambda cin, cout, sems: _sibling_copies(cin[0], cout[0], sems[0], sems[1]))


def scatter_chips_comm(part):
    _, r, w = part.shape
    return _Comm([part], [jax.ShapeDtypeStruct((N_CHIP - 1, r, w), part.dtype)],
                 [pltpu.SemaphoreType.DMA((3,)), pltpu.SemaphoreType.DMA((3,))],
                 lambda cin, cout, sems: _chip_copies(cin[0], cout[0], sems[0], sems[1]))


def allgather_rows(shard):
    r, w = shard.shape

    def body(x_ref, out_ref, send_sems, recv_sems, local_sem):
        x, y, c = _mesh_pos()
        me, sibling = (x, y, c), (x, y, 1 - c)
        chips = [(1 - x, y), (x, 1 - y), (1 - x, 1 - y)]

        def slot(px, py, pc):
            return out_ref.at[4 * px + 2 * py + pc]

        def copy(k, block, to, src=None):
            return pltpu.make_async_remote_copy(
                src_ref=slot(*block) if src is None else src, dst_ref=slot(*block),
                send_sem=send_sems.at[k], recv_sem=recv_sems.at[k], device_id=to, device_id_type=_MESH)

        mine = pltpu.make_async_copy(x_ref, slot(*me), local_sem)
        mine.start()
        first = [copy(0, me, sibling, src=x_ref)]
        first += [copy(1 + j, me, (*chip, c), src=x_ref) for j, chip in enumerate(chips)]
        for cp in first:
            cp.start()
        passed = [copy(4 + j, (*chip, c), sibling) for j, chip in enumerate(chips)]
        for j, chip in enumerate(chips):
            copy(1 + j, (*chip, c), me).wait_recv()
            passed[j].start()
        copy(0, sibling, me).wait_recv()
        for j, chip in enumerate(chips):
            copy(4 + j, (*chip, 1 - c), me).wait_recv()
        for cp in first + passed:
            cp.wait_send()
        mine.wait()

    return pl.pallas_call(
        body, name="allgather_rows",
        out_shape=jax.ShapeDtypeStruct((N_DEV, r, w), shard.dtype),
        in_specs=[_ANY], out_specs=_ANY,
        scratch_shapes=[pltpu.SemaphoreType.DMA((7,)), pltpu.SemaphoreType.DMA((7,)), pltpu.SemaphoreType.DMA],
    )(shard)


def exchange_sibling(g):
    _, r, w = g.shape

    def body(g_ref, got_ref, send_sems, recv_sems):
        copies = _sibling_copies(g_ref, got_ref, send_sems, recv_sems)
        for cp in copies:
            cp.start()
        for cp in copies:
            cp.wait()

    return pl.pallas_call(
        body, name="exchange_sibling", out_shape=jax.ShapeDtypeStruct((N_CHIP, r, w), g.dtype),
        in_specs=[_ANY], out_specs=_ANY,
        scratch_shapes=[pltpu.SemaphoreType.DMA((N_CHIP,)), pltpu.SemaphoreType.DMA((N_CHIP,))],
    )(g)


def exchange_chips(part):
    _, r, w = part.shape

    def body(p_ref, land_ref, send_sems, recv_sems):
        copies = _chip_copies(p_ref, land_ref, send_sems, recv_sems)
        for cp in copies:
            cp.start()
        for cp in copies:
            cp.wait()

    return pl.pallas_call(
        body, name="exchange_chips", out_shape=jax.ShapeDtypeStruct((N_CHIP - 1, r, w), part.dtype),
        in_specs=[_ANY], out_specs=_ANY,
        scratch_shapes=[pltpu.SemaphoreType.DMA((3,)), pltpu.SemaphoreType.DMA((3,))],
    )(part)


def allreduce_stats(st_mix, st_prep, st_mlp, st_ple):
    def body(mix_ref, prep_ref, mlp_ref, ple_ref, out_ref, mine, gath, send_sems, recv_sems):
        x, y, c = _mesh_pos()
        me = 4 * x + 2 * y + c
        mine[...] = jnp.zeros_like(mine)
        mine[ST_G_MIX:ST_G_MIX + 1, :] = mix_ref[...]
        mine[ST_G_QA:ST_G_KN + 1, 0:256] = prep_ref[...]
        mine[ST_G_MLP:ST_G_MLP + 1, :] = mlp_ref[...]
        mine[ST_G_PLE:ST_LOSS + 1, :] = ple_ref[...]
        gath[me] = mine[...]
        copies = []
        for k in range(1, N_DEV):
            peer = (_flip(x, k & 4), _flip(y, k & 2), _flip(c, k & 1))
            copies.append(_remote(mine, gath.at[me], send_sems, recv_sems, k - 1, peer))
        for cp in copies:
            cp.start()
        for cp in copies:
            cp.wait()
        acc = gath[0]
        for d in range(1, N_DEV):
            acc = acc + gath[d]
        out_ref[...] = acc

    vm = pl.BlockSpec(memory_space=pltpu.VMEM)
    return pl.pallas_call(
        body, name="allreduce_stats", out_shape=jax.ShapeDtypeStruct((ST_ROWS, D_MODEL), F32),
        in_specs=[vm] * 4, out_specs=vm,
        scratch_shapes=[pltpu.VMEM((ST_ROWS, D_MODEL), F32), pltpu.VMEM((N_DEV, ST_ROWS, D_MODEL), F32),
                        pltpu.SemaphoreType.DMA((N_DEV - 1,)), pltpu.SemaphoreType.DMA((N_DEV - 1,))],
    )(st_mix, st_prep, st_mlp, st_ple)


def adamw_gains(stats, gains):
    c1 = 1.0 - ADAM_B1 ** ADAM_STEP
    c2 = 1.0 - ADAM_B2 ** ADAM_STEP
    n = len(gains)

    def body(st_ref, *refs):
        ins, outs = refs[:3 * n], refs[3 * n:]
        for i, (row, w, _, _) in enumerate(gains):
            width = w.shape[1]
            gv = st_ref[row:row + 1, 0:width]
            mn = ADAM_B1 * ins[3 * i + 1][...] + (1.0 - ADAM_B1) * gv
            vn = ADAM_B2 * ins[3 * i + 2][...] + (1.0 - ADAM_B2) * (gv * gv)
            outs[4 * i][...] = gv
            outs[4 * i + 1][...] = -ADAM_LR * ((mn / c1) / (jnp.sqrt(vn / c2) + ADAM_EPS) + ADAM_WD * ins[3 * i][...])
            outs[4 * i + 2][...] = mn
            outs[4 * i + 3][...] = vn

    vm = pl.BlockSpec(memory_space=pltpu.VMEM)
    flat = [a for (_, w, m, v) in gains for a in (w, m, v)]
    out_shape = tuple(jax.ShapeDtypeStruct(w.shape, F32) for (_, w, _, _) in gains for _ in range(4))
    res = pl.pallas_call(body, name="adamw_gains", out_shape=out_shape, in_specs=[vm] * (1 + 3 * n),
                         out_specs=tuple([vm] * (4 * n)))(stats, *flat)
    return [res[4 * i:4 * i + 4] for i in range(n)]


def _row_tile(r, cap=640):
    return max(d for d in range(16, min(r, cap) + 1, 16) if r % d == 0)


def add_pairs(g, got, core):
    n, r, w = got.shape
    tr = _row_tile(r)

    def body(c_ref, a_ref, b_ref, o_ref):
        o_ref[...] = (a_ref[...].astype(F32) + b_ref[...].astype(F32)).astype(o_ref.dtype)

    spec = pl.BlockSpec((1, tr, w), lambda i, j, c: (i, j, 0))
    return pl.pallas_call(
        body, name="add_pairs", out_shape=jax.ShapeDtypeStruct(got.shape, got.dtype),
        grid_spec=pltpu.PrefetchScalarGridSpec(
            num_scalar_prefetch=1, grid=(n, r // tr),
            in_specs=[pl.BlockSpec((1, tr, w), lambda i, j, c: (2 * i + c[0], j, 0)), spec], out_specs=spec),
        compiler_params=_params(("parallel", "parallel")),
    )(core, g, got)


def sum_chips(part, land, chip):
    _, r, w = part.shape
    tr = _row_tile(r)

    def body(c_ref, p_ref, l_ref, o_ref):
        acc = p_ref[0].astype(F32)
        for s in range(N_CHIP - 1):
            acc = acc + l_ref[s].astype(F32)
        o_ref[...] = acc

    return pl.pallas_call(
        body, name="sum_chips", out_shape=jax.ShapeDtypeStruct((r, w), F32),
        grid_spec=pltpu.PrefetchScalarGridSpec(
            num_scalar_prefetch=1, grid=(r // tr,),
            in_specs=[pl.BlockSpec((1, tr, w), lambda i, c: (c[0], i, 0)), pl.BlockSpec((N_CHIP - 1, tr, w), lambda i, c: (0, i, 0))],
            out_specs=pl.BlockSpec((tr, w), lambda i, c: (i, 0))),
        compiler_params=_params(("parallel",)),
    )(chip, part, land)


def in_proj(x, g_mix, w_in_t, tm):
    t = x.shape[0]
    nc = 512

    def body(x_ref, g_ref, w_ref, z_ref, h_ref):
        xv = x_ref[...]
        h = (xv * _rstd(xv, D_MODEL) * g_ref[...]).astype(BF16)
        h_ref[...] = h
        for cidx in range(ZP // nc):
            z_ref[:, cidx * nc:(cidx + 1) * nc] = _dot_nt(h, w_ref[cidx * nc:(cidx + 1) * nc, :])

    return pl.pallas_call(
        body, name="in_proj", grid=(t // tm,),
        out_shape=(jax.ShapeDtypeStruct((t, ZP), F32), jax.ShapeDtypeStruct((t, D_MODEL), BF16)),
        in_specs=[_rows(tm, D_MODEL), _resident((1, D_MODEL)), _resident((ZP, D_MODEL))],
        out_specs=(_rows(tm, ZP), _rows(tm, D_MODEL)), compiler_params=_params(("parallel",)),
    )(x, g_mix, w_in_t)


def attn_prep(zp, tabs, g_qa, g_kva, g_qn, g_kn, w_qb_t, w_kvb_t, tm, s_len):
    t = zp.shape[0]
    nsb = s_len // tm
    scale_a = (QK_NOPE + QK_ROPE) ** -0.5
    scale_b = HD_B ** -0.5

    def body(qb_ref, qlat_ref, kb_ref, vb_ref, ckv_ref, kpe_ref, tab_ref, gqa_ref, gkva_ref, gqn_ref, gkn_ref,
             wqb_ref, wkvb_ref, qa_o, ka_o, va_o, qb_o, kb_o, vb_o, cq_o, ckvn_o):
        ca, s1a, s2a = tab_ref[0], tab_ref[1], tab_ref[2]
        ck = tab_ref[3]
        cb, s1b, s2b = tab_ref[4], tab_ref[5], tab_ref[6]
        ql = qlat_ref[...]
        cq = (ql * _rstd(ql, Q_LORA) * gqa_ref[...]).astype(BF16)
        cq_o[...] = cq
        qa = _dot_nt(cq, wqb_ref[...])
        slabs = [slice(h * HP, (h + 1) * HP) for h in range(H_A)]
        qa_o[...] = jnp.concatenate(
            [(_rope_fwd(qa[:, sl], ca, s1a, s2a) * scale_a).astype(BF16) for sl in slabs], axis=1)
        cr = ckv_ref[...]
        ckv = (cr * _rstd(cr, KV_LORA) * gkva_ref[...]).astype(BF16)
        ckvn_o[...] = ckv
        kva = _dot_nt(ckv, wkvb_ref[...])
        kpe = _rope_fwd(kpe_ref[...], ck, s1a, s2a)
        ka_o[...] = jnp.concatenate([(kva[:, sl] + kpe).astype(BF16) for sl in slabs], axis=1)
        va_o[...] = kva[:, H_A * HP:].astype(BF16)
        gqn, gkn = gqn_ref[...], gkn_ref[...]

        def norm_rope(ref, sl, g, scale):
            xs = ref[:, sl]
            y = _rope_fwd(xs * _rstd(xs, HD_B) * g, cb, s1b, s2b)
            return (y if scale is None else y * scale).astype(BF16)

        qb_o[...] = jnp.concatenate([norm_rope(qb_ref, sl, gqn, scale_b) for sl in slabs], axis=1)
        kb_o[...] = jnp.concatenate([norm_rope(kb_ref, sl, gkn, None) for sl in slabs[:KV_B]], axis=1)
        vb_o[...] = vb_ref[...].astype(BF16)

    def o(width):
        return jax.ShapeDtypeStruct((t, width), BF16)

    return pl.pallas_call(
        body, name="attn_prep", grid=(t // tm,),
        out_shape=(o(H_A * HP), o(H_A * HP), o(H_A * HP), o(H_B * HP), o(KV_B * HP), o(KV_B * HP), o(Q_LORA), o(KV_LORA)),
        in_specs=[_rows(tm, 1024, 0), _rows(tm, 256, 12), _rows(tm, 256, 13), _rows(tm, 256, 14),
                  _rows(tm, 128, 30), _rows(tm, 128, 31),
                  pl.BlockSpec((7, tm, HP), lambda i: (0, i % nsb, 0)),
                  _resident((1, Q_LORA)), _resident((1, KV_LORA)), _resident((1, HP)), _resident((1, HP)),
                  _resident((H_A * HP, Q_LORA)), _resident((2 * H_A * HP, KV_LORA))],
        out_specs=(_rows(tm, H_A * HP), _rows(tm, H_A * HP), _rows(tm, H_A * HP), _rows(tm, H_B * HP),
                   _rows(tm, KV_B * HP), _rows(tm, KV_B * HP), _rows(tm, Q_LORA), _rows(tm, KV_LORA)),
        compiler_params=_params(("parallel",)),
    )(zp, zp, zp, zp, zp, zp, tabs, g_qa, g_kva, g_qn, g_kn, w_qb_t, w_kvb_t)


def attn_fwd(q, k, v, n_b, s_len, tq, name, comm=None):
    t = q.shape[0]
    n_h, n_hk = q.shape[1] // HP, k.shape[1] // HP
    grp = n_h // n_hk
    nq = s_len // tq
    sub = min(tq, 256)
    grid = (n_b, n_h, nq)
    c_ins, c_in_specs, c_outs, c_sems, alias = _comm_parts(comm, 3, 2)

    def body(*refs):
        (q_ref, k_ref, v_ref), cin, (o_ref, lse_ref), cout, _, csem = _split_refs(refs, 3, 2, 0, comm)
        _comm_start(comm, cin, cout, csem, grid)
        kv, vv = k_ref[...], v_ref[...]
        for r in range(tq // sub):
            rows = slice(r * sub, (r + 1) * sub)
            s = _dot_nt(q_ref[rows, :], kv)
            m = jnp.max(s, axis=-1, keepdims=True)
            p = jnp.exp(s - m)
            l = jnp.sum(p, axis=-1, keepdims=True)
            o_ref[rows, :] = (_dot_nn(p.astype(BF16), vv) * (1.0 / l)).astype(o_ref.dtype)
            lse_ref[rows, :] = jnp.broadcast_to(m + jnp.log(l), (sub, HP))
        _comm_finish(comm, cin, cout, csem, grid)

    qspec = pl.BlockSpec((tq, HP), lambda b, h, i: (b * nq + i, h))
    kspec = pl.BlockSpec((s_len, HP), lambda b, h, i: (b, h // grp))
    return pl.pallas_call(
        body, name=name, grid=grid,
        out_shape=(jax.ShapeDtypeStruct((t, n_h * HP), BF16), jax.ShapeDtypeStruct((t, n_h * HP), F32), *c_outs),
        in_specs=[qspec, kspec, kspec, *c_in_specs], out_specs=(qspec, qspec, *([_ANY] * len(c_outs))),
        scratch_shapes=c_sems, input_output_aliases=alias,
        compiler_params=_params(("arbitrary", "arbitrary", "arbitrary")),
    )(q, k, v, *c_ins)


def merge_fwd(oa, ob, zp, x, w_oa_t, w_ob_t, wpack, off, tm):
    t = x.shape[0]

    def body(oa_ref, ob_ref, ga_ref, gb_ref, x_ref, woa_ref, wob_ref, wo_ref, x1_o, mg_o):
        ya = _dot_nt(oa_ref[...], woa_ref[...])
        yb = _dot_nt(ob_ref[...], wob_ref[...])
        merged = (jax.nn.sigmoid(ga_ref[...]) * ya + jax.nn.sigmoid(gb_ref[...]) * yb).astype(BF16)
        mg_o[...] = merged
        x1_o[...] = x_ref[...] + _dot_nn(merged, _wrows(wo_ref, 0, D_MODEL))

    return pl.pallas_call(
        body, name="merge_fwd", grid=(t // tm,),
        out_shape=(jax.ShapeDtypeStruct((t, D_MODEL), F32), jax.ShapeDtypeStruct((t, D_MODEL), BF16)),
        in_specs=[_rows(tm, H_A * HP), _rows(tm, H_B * HP), _rows(tm, 1024, 1), _rows(tm, 1024, 2), _rows(tm, D_MODEL),
                  _resident((D_MODEL, H_A * HP)), _resident((D_MODEL, H_B * HP)), _packed_weight(128, off["w_o"])],
        out_specs=(_rows(tm, D_MODEL), _rows(tm, D_MODEL)), compiler_params=_params(("parallel",)),
    )(oa, ob, zp, zp, x, w_oa_t, w_ob_t, wpack)


def mlp_fwd(x1, g_mlp, wpack, off, tm):
    t = x1.shape[0]
    fc = 1024

    def body(x_ref, g_ref, wup_ref, wdn_ref, x2_o, u_o):
        xv = x_ref[...]
        h2 = (xv * _rstd(xv, D_MODEL) * g_ref[...]).astype(BF16)
        acc = xv
        for cidx in range(D_FF // fc):
            sl = slice(cidx * fc, (cidx + 1) * fc)
            u = jnp.maximum(_dot_nt(h2, _wrows(wup_ref, cidx * fc, fc)), 0.0)
            u_o[:, sl] = u.astype(BF16)
            acc = acc + _dot_nn((u * u).astype(BF16), _wrows(wdn_ref, cidx * fc, fc))
        x2_o[...] = acc

    return pl.pallas_call(
        body, name="mlp_fwd", grid=(t // tm,),
        out_shape=(jax.ShapeDtypeStruct((t, D_MODEL), F32), jax.ShapeDtypeStruct((t, D_FF), BF16)),
        in_specs=[_rows(tm, D_MODEL), _resident((1, D_MODEL)), _packed_weight(512, off["w_up"]), _packed_weight(512, off["w_down"])],
        out_specs=(_rows(tm, D_MODEL), _rows(tm, D_FF)), compiler_params=_params(("parallel",)),
    )(x1, g_mlp, wpack, wpack)


def ple_loss_bwd(x2, p, tgt, g_ple, g_final, wpack, off, w_ple_t, tm):
    t = x2.shape[0]
    inv_d = 1.0 / D_MODEL

    def body(x2_ref, p_ref, tg_ref, gp_ref, gf_ref, wpg_ref, wple_ref, dx2_o, dt_o, h3_o, dpe_o, st_o, dx2b_o):
        @pl.when(pl.program_id(0) == 0)
        def _():
            st_o[...] = jnp.zeros_like(st_o)

        x2v = x2_ref[...]
        gp, gf = gp_ref[...], gf_ref[...]
        w_pg = _wrows(wpg_ref, 0, D_MODEL)
        r2 = _rstd(x2v, D_MODEL)
        xh2 = x2v * r2
        h3 = (xh2 * gp).astype(BF16)
        h3_o[...] = h3
        gate = jax.nn.sigmoid(_dot_nn(h3, w_pg))
        pe = _dot_nt(p_ref[...].astype(BF16), wple_ref[...])
        x3 = x2v + gate * pe
        r3 = _rstd(x3, D_MODEL)
        xh3 = x3 * r3
        err = xh3 * gf - tg_ref[...]
        dy = err * inv_d
        dx3 = _rms_bwd(dy, xh3, r3, gf, D_MODEL)
        dpe_o[...] = (dx3 * gate).astype(BF16)
        dt = (dx3 * pe * gate * (1.0 - gate)).astype(BF16)
        dt_o[...] = dt
        dh3 = _dot_nt(dt, w_pg)
        dx2 = dx3 + _rms_bwd(dh3, xh2, r2, gp, D_MODEL)
        dx2_o[...] = dx2
        dx2b_o[...] = dx2.astype(BF16)
        st_o[0:1, :] += _colsum(dh3 * xh2)
        st_o[1:2, :] += _colsum(dy * xh3)
        st_o[2:3, :] += _colsum(err * err) * (0.5 * inv_d)

    bf = jax.ShapeDtypeStruct((t, D_MODEL), BF16)
    return pl.pallas_call(
        body, name="ple_loss_bwd", grid=(t // tm,),
        out_shape=(jax.ShapeDtypeStruct((t, D_MODEL), F32), bf, bf, bf, jax.ShapeDtypeStruct((3, D_MODEL), F32), bf),
        in_specs=[_rows(tm, D_MODEL), _rows(tm, PLE_DIM), _rows(tm, D_MODEL), _resident((1, D_MODEL)), _resident((1, D_MODEL)),
                  _packed_weight(128, off["w_ple_gate"]), _resident((D_MODEL, PLE_DIM))],
        out_specs=(_rows(tm, D_MODEL), _rows(tm, D_MODEL), _rows(tm, D_MODEL), _rows(tm, D_MODEL),
                   pl.BlockSpec((3, D_MODEL), lambda i: (0, 0)), _rows(tm, D_MODEL)),
        compiler_params=_params(("arbitrary",)),
    )(x2, p, tgt, g_ple, g_final, wpack, w_ple_t)


def mlp_bwd(dx2, x1, u, g_mlp, wpack, off, tm):
    t = x1.shape[0]
    fc = 1024

    def body(dx2_ref, x1_ref, u_ref, g_ref, wup_ref, wdn_ref, dx1_o, da_o, h2_o, st_o, dx1b_o):
        @pl.when(pl.program_id(0) == 0)
        def _():
            st_o[...] = jnp.zeros_like(st_o)

        d2 = dx2_ref[...]
        d2b = d2.astype(BF16)
        dh2 = jnp.zeros((tm, D_MODEL), F32)
        for cidx in range(D_FF // fc):
            sl = slice(cidx * fc, (cidx + 1) * fc)
            da = (_dot_nt(d2b, _wrows(wdn_ref, cidx * fc, fc)) * (2.0 * u_ref[:, sl].astype(F32))).astype(BF16)
            da_o[:, sl] = da
            dh2 = dh2 + _dot_nn(da, _wrows(wup_ref, cidx * fc, fc))
        xv = x1_ref[...]
        g = g_ref[...]
        r1 = _rstd(xv, D_MODEL)
        xh1 = xv * r1
        h2_o[...] = (xh1 * g).astype(BF16)
        st_o[...] += _colsum(dh2 * xh1)
        dx1 = d2 + _rms_bwd(dh2, xh1, r1, g, D_MODEL)
        dx1_o[...] = dx1
        dx1b_o[...] = dx1.astype(BF16)

    return pl.pallas_call(
        body, name="mlp_bwd", grid=(t // tm,),
        out_shape=(jax.ShapeDtypeStruct((t, D_MODEL), F32), jax.ShapeDtypeStruct((t, D_FF), BF16),
                   jax.ShapeDtypeStruct((t, D_MODEL), BF16), jax.ShapeDtypeStruct((1, D_MODEL), F32),
                   jax.ShapeDtypeStruct((t, D_MODEL), BF16)),
        in_specs=[_rows(tm, D_MODEL), _rows(tm, D_MODEL), _rows(tm, D_FF), _resident((1, D_MODEL)),
                  _packed_weight(512, off["w_up"]), _packed_weight(512, off["w_down"])],
        out_specs=(_rows(tm, D_MODEL), _rows(tm, D_FF), _rows(tm, D_MODEL), pl.BlockSpec((1, D_MODEL), lambda i: (0, 0)),
                   _rows(tm, D_MODEL)),
        compiler_params=_params(("arbitrary",)),
    )(dx2, x1, u, g_mlp, wpack, wpack)


def merge_bwd(dx1, oa, ob, zp, w_oa_t, w_ob_t, wpack, off, tm, comm=None):
    t = dx1.shape[0]
    grid = (t // tm,)
    c_ins, c_in_specs, c_outs, c_sems, alias = _comm_parts(comm, 8, 5)

    def body(*refs):
        ((dx1_ref, oa_ref, ob_ref, ga_ref, gb_ref, woa_ref, wob_ref, wo_ref), cin,
         (doa_o, dob_o, dg_o, dya_o, dyb_o), cout, _, csem) = _split_refs(refs, 8, 5, 0, comm)
        _comm_start(comm, cin, cout, csem, grid)
        dm = _dot_nt(dx1_ref[...].astype(BF16), _wrows(wo_ref, 0, D_MODEL))
        for o_ref, g_ref, w_ref, do_o, dy_o, col in ((oa_ref, ga_ref, woa_ref, doa_o, dya_o, 0),
                                                     (ob_ref, gb_ref, wob_ref, dob_o, dyb_o, 1)):
            yv = _dot_nt(o_ref[...], w_ref[...])
            sg = jax.nn.sigmoid(g_ref[...])
            dyv = (dm * sg).astype(BF16)
            dy_o[...] = dyv
            dg_o[:, col * D_MODEL:(col + 1) * D_MODEL] = (dm * yv * sg * (1.0 - sg)).astype(BF16)
            do_o[...] = _dot_nn(dyv, w_ref[...]).astype(BF16)
        _comm_finish(comm, cin, cout, csem, grid)

    bf = jax.ShapeDtypeStruct((t, D_MODEL), BF16)
    return pl.pallas_call(
        body, name="merge_bwd", grid=grid,
        out_shape=(bf, bf, jax.ShapeDtypeStruct((t, 2 * D_MODEL), BF16), bf, bf, *c_outs),
        in_specs=[_rows(tm, D_MODEL), _rows(tm, H_A * HP), _rows(tm, H_B * HP), _rows(tm, 1024, 1), _rows(tm, 1024, 2),
                  _resident((D_MODEL, H_A * HP)), _resident((D_MODEL, H_B * HP)), _packed_weight(128, off["w_o"]), *c_in_specs],
        out_specs=(_rows(tm, D_MODEL), _rows(tm, D_MODEL), _rows(tm, 2 * D_MODEL), _rows(tm, D_MODEL), _rows(tm, D_MODEL),
                   *([_ANY] * len(c_outs))),
        scratch_shapes=c_sems, input_output_aliases=alias,
        compiler_params=_params(("arbitrary",)),
    )(dx1, oa, ob, zp, zp, w_oa_t, w_ob_t, wpack, *c_ins)


def attn_bwd(q, k, v, do, o, lse, n_b, s_len, tq, name, comm=None):
    t = q.shape[0]
    n_h, n_hk = q.shape[1] // HP, k.shape[1] // HP
    grp = n_h // n_hk
    nq = s_len // tq
    sub = min(tq, 256)
    grid = (n_b, n_hk, grp, nq)
    c_ins, c_in_specs, c_outs, c_sems, alias = _comm_parts(comm, 6, 3)

    def body(*refs):
        ((q_ref, k_ref, v_ref, do_ref, o_ref, lse_ref), cin, (dq_o, dk_o, dv_o), cout, (p_s, ds_s),
         csem) = _split_refs(refs, 6, 3, 2, comm)
        _comm_start(comm, cin, cout, csem, grid)

        @pl.when((pl.program_id(2) == 0) & (pl.program_id(3) == 0))
        def _():
            dk_o[...] = jnp.zeros_like(dk_o)
            dv_o[...] = jnp.zeros_like(dv_o)

        kv, vv = k_ref[...], v_ref[...]
        for r in range(tq // sub):
            rows = slice(r * sub, (r + 1) * sub)
            qv, dov = q_ref[rows, :], do_ref[rows, :]
            delta = jnp.sum(dov.astype(F32) * o_ref[rows, :].astype(F32), axis=-1, keepdims=True)
            p = jnp.exp(_dot_nt(qv, kv) - lse_ref[rows, 0:1])
            ds = (p * (_dot_nt(dov, vv) - delta)).astype(BF16)
            p_s[rows, :] = p.astype(BF16)
            ds_s[rows, :] = ds
            dq_o[rows, :] = _dot_nn(ds, kv)
        dk_o[...] += _dot_tn(ds_s[...], q_ref[...])
        dv_o[...] += _dot_tn(p_s[...], do_ref[...])
        _comm_finish(comm, cin, cout, csem, grid)

    qspec = pl.BlockSpec((tq, HP), lambda b, hk, g, i: (b * nq + i, hk * grp + g))
    kspec = pl.BlockSpec((s_len, HP), lambda b, hk, g, i: (b, hk))
    return pl.pallas_call(
        body, name=name, grid=grid,
        out_shape=(jax.ShapeDtypeStruct((t, n_h * HP), F32), jax.ShapeDtypeStruct((t, n_hk * HP), F32),
                   jax.ShapeDtypeStruct((t, n_hk * HP), F32), *c_outs),
        in_specs=[qspec, kspec, kspec, qspec, qspec, qspec, *c_in_specs],
        out_specs=(qspec, kspec, kspec, *([_ANY] * len(c_outs))),
        scratch_shapes=[pltpu.VMEM((tq, s_len), BF16), pltpu.VMEM((tq, s_len), BF16), *c_sems],
        input_output_aliases=alias,
        compiler_params=_params(("arbitrary", "arbitrary", "arbitrary", "arbitrary")),
    )(q, k, v, do, o, lse, *c_ins)


def prep_bwd(dqa, dka, dva, dqb, dkb, dvb, zp, tabs, g_qa, g_kva, g_qn, g_kn, w_qb_t, w_kvb_t, tm, s_len):
    t = zp.shape[0]
    nsb = s_len // tm
    scale_a = (QK_NOPE + QK_ROPE) ** -0.5
    scale_b = HD_B ** -0.5

    def body(dqa_ref, dka_ref, dva_ref, dqb_ref, dkb_ref, dvb_ref, qb_ref, qlat_ref, kb_ref, ckv_ref, tab_ref,
             gqa_ref, gkva_ref, gqn_ref, gkn_ref, wqb_ref, wkvb_ref, dzq_o, dsm_o, dqap_o, dkva_o, st_o):
        @pl.when(pl.program_id(0) == 0)
        def _():
            st_o[...] = jnp.zeros_like(st_o)

        ca, s1a, s2a = tab_ref[0], tab_ref[1], tab_ref[2]
        ck = tab_ref[3]
        cb, s1b, s2b = tab_ref[4], tab_ref[5], tab_ref[6]
        for h in range(H_A):
            sl = slice(h * HP, (h + 1) * HP)
            dqap_o[:, sl] = _rope_bwd(dqa_ref[:, sl] * scale_a, ca, s1a, s2a).astype(BF16)
        dcq = _dot_nn(dqap_o[...], wqb_ref[...])
        ql = qlat_ref[...]
        rq = _rstd(ql, Q_LORA)
        xh = ql * rq
        gqa = gqa_ref[...]
        st_o[0:1, :] += _colsum(dcq * xh)
        dsm_o[:, 0:256] = _rms_bwd(dcq, xh, rq, gqa, Q_LORA).astype(BF16)
        dkpe = jnp.zeros((tm, HP), F32)
        for h in range(H_A):
            sl = slice(h * HP, (h + 1) * HP)
            dk = dka_ref[:, sl]
            dkpe = dkpe + dk
            dkva_o[:, sl] = dk.astype(BF16)
        dkva_o[:, H_A * HP:] = dva_ref[...].astype(BF16)
        dsm_o[:, 896:1024] = _rope_bwd(dkpe, ck, s1a, s2a).astype(BF16)
        dckv = _dot_nn(dkva_o[...], wkvb_ref[...])
        cr = ckv_ref[...]
        rk = _rstd(cr, KV_LORA)
        xh = cr * rk
        st_o[1:2, 0:128] += _colsum(dckv * xh)
        dsm_o[:, 768:896] = _rms_bwd(dckv, xh, rk, gkva_ref[...], KV_LORA).astype(BF16)
        gqn, gkn = gqn_ref[...], gkn_ref[...]
        dgq = jnp.zeros((1, HP), F32)
        for h in range(H_B):
            sl = slice(h * HP, (h + 1) * HP)
            dy = _rope_bwd(dqb_ref[:, sl] * scale_b, cb, s1b, s2b)
            xs = qb_ref[:, sl]
            r = _rstd(xs, HD_B)
            xh = xs * r
            dgq = dgq + _colsum(dy * xh)
            dzq_o[:, sl] = _rms_bwd(dy, xh, r, gqn, HD_B).astype(BF16)
        st_o[2:3, 0:128] += dgq
        dgk = jnp.zeros((1, HP), F32)
        for h in range(KV_B):
            sl = slice(h * HP, (h + 1) * HP)
            dy = _rope_bwd(dkb_ref[:, sl], cb, s1b, s2b)
            xs = kb_ref[:, sl]
            r = _rstd(xs, HD_B)
            xh = xs * r
            dgk = dgk + _colsum(dy * xh)
            dsm_o[:, 256 + h * HP:256 + (h + 1) * HP] = _rms_bwd(dy, xh, r, gkn, HD_B).astype(BF16)
        st_o[3:4, 0:128] += dgk
        dsm_o[:, 512:768] = dvb_ref[...].astype(BF16)

    bf = jax.ShapeDtypeStruct((t, 1024), BF16)
    return pl.pallas_call(
        body, name="prep_bwd", grid=(t // tm,),
        out_shape=(bf, bf, bf, jax.ShapeDtypeStruct((t, 2048), BF16), jax.ShapeDtypeStruct((4, 256), F32)),
        in_specs=[_rows(tm, 1024), _rows(tm, 1024), _rows(tm, 1024), _rows(tm, 1024), _rows(tm, 256), _rows(tm, 256),
                  _rows(tm, 1024, 0), _rows(tm, 256, 12), _rows(tm, 256, 13), _rows(tm, 128, 30),
                  pl.BlockSpec((7, tm, HP), lambda i: (0, i % nsb, 0)),
                  _resident((1, Q_LORA)), _resident((1, KV_LORA)), _resident((1, HP)), _resident((1, HP)),
                  _resident((H_A * HP, Q_LORA)), _resident((2 * H_A * HP, KV_LORA))],
        out_specs=(_rows(tm, 1024), _rows(tm, 1024), _rows(tm, 1024), _rows(tm, 2048), pl.BlockSpec((4, 256), lambda i: (0, 0))),
        compiler_params=_params(("arbitrary",)),
    )(dqa, dka, dva, dqb, dkb, dvb, zp, zp, zp, zp, tabs, g_qa, g_kva, g_qn, g_kn, w_qb_t, w_kvb_t)


def in_bwd(dzq, dgab, dsm, x, dx1, g_mix, w_in_t, tm, comm=None):
    t = x.shape[0]
    grid = (t // tm,)
    c_ins, c_in_specs, c_outs, c_sems, alias = _comm_parts(comm, 7, 2)

    def body(*refs):
        ((dzq_ref, dg_ref, dsm_ref, x_ref, dx1_ref, g_ref, w_ref), cin, (dx_o, st_o), cout, _,
         csem) = _split_refs(refs, 7, 2, 0, comm)
        _comm_start(comm, cin, cout, csem, grid)

        @pl.when(pl.program_id(0) == 0)
        def _():
            st_o[...] = jnp.zeros_like(st_o)

        dh = _dot_nn(dzq_ref[...], w_ref[0:1024, :])
        dh = dh + _dot_nn(dg_ref[...], w_ref[1024:3072, :])
        dh = dh + _dot_nn(dsm_ref[...], w_ref[3072:4096, :])
        xv = x_ref[...]
        g = g_ref[...]
        r = _rstd(xv, D_MODEL)
        xh = xv * r
        st_o[...] += _colsum(dh * xh)
        dx_o[...] = dx1_ref[...] + _rms_bwd(dh, xh, r, g, D_MODEL)
        _comm_finish(comm, cin, cout, csem, grid)

    return pl.pallas_call(
        body, name="in_bwd", grid=grid,
        out_shape=(jax.ShapeDtypeStruct((t, D_MODEL), F32), jax.ShapeDtypeStruct((1, D_MODEL), F32), *c_outs),
        in_specs=[_rows(tm, 1024), _rows(tm, 2048), _rows(tm, 1024), _rows(tm, D_MODEL), _rows(tm, D_MODEL),
                  _resident((1, D_MODEL)), _resident((ZP, D_MODEL)), *c_in_specs],
        out_specs=(_rows(tm, D_MODEL), pl.BlockSpec((1, D_MODEL), lambda i: (0, 0)), *([_ANY] * len(c_outs))),
        scratch_shapes=c_sems, input_output_aliases=alias,
        compiler_params=_params(("arbitrary",)),
    )(dzq, dgab, dsm, x, dx1, g_mix, w_in_t, *c_ins)


def matmul_tn(a, b, name, square_a=False):
    t, m = a.shape
    n = b.shape[1]
    bm = min(m, 512)
    tk = min(t, 4096)

    def body(a_ref, b_ref, o_ref):
        @pl.when(pl.program_id(1) == 0)
        def _():
            o_ref[...] = jnp.zeros_like(o_ref)

        av = a_ref[...]
        if square_a:
            av = (av.astype(F32) * av.astype(F32))
        o_ref[...] += _dot_tn(av.astype(BF16), b_ref[...].astype(BF16))

    return pl.pallas_call(
        body, name=name, grid=(m // bm, t // tk), out_shape=jax.ShapeDtypeStruct((m, n), F32),
        in_specs=[pl.BlockSpec((tk, bm), lambda i, kk: (kk, i)), pl.BlockSpec((tk, n), lambda i, kk: (kk, 0))],
        out_specs=pl.BlockSpec((bm, n), lambda i, kk: (i, 0)),
        compiler_params=_params(("parallel", "arbitrary")),
    )(a, b)


def matmul_tn_packed(a, b, name, rows, row_off, total_rows, buf=None, square_a=False):
    t, m = a.shape
    n = b.shape[1]
    pd = max(1, 512 // rows)
    bm = pd * rows
    tk = min(t, 4096)
    nk = t // tk

    def body(a_ref, b_ref, *rest):
        o_ref, acc = rest[-2], rest[-1]

        @pl.when(pl.program_id(1) == 0)
        def _():
            acc[...] = jnp.zeros_like(acc)

        av = a_ref[...]
        if square_a:
            av = (av.astype(F32) * av.astype(F32))
        acc[...] += _dot_tn(av.astype(BF16), b_ref[...].astype(BF16))

        @pl.when(pl.program_id(1) == nk - 1)
        def _():
            o_ref[...] = acc[...].reshape(pd, rows, n).astype(o_ref.dtype)

    in_specs = [pl.BlockSpec((tk, bm), lambda i, kk: (kk, i)), pl.BlockSpec((tk, n), lambda i, kk: (kk, 0))]
    args = [a, b]
    if buf is not None:
        in_specs.append(_ANY)
        args.append(buf)
    return pl.pallas_call(
        body, name=name, grid=(m // bm, nk), out_shape=jax.ShapeDtypeStruct((N_DEV, total_rows, n), BF16),
        in_specs=in_specs, out_specs=pl.BlockSpec((pd, rows, n), lambda i, kk: (i, row_off // rows, 0)),
        scratch_shapes=[pltpu.VMEM((bm, n), F32)], input_output_aliases={2: 0} if buf is not None else {},
        compiler_params=_params(("parallel", "arbitrary")),
    )(*args)


def adamw(w, g, m, v, name, g_transposed=False):
    _, r, c = w.shape
    tr = r if (g_transposed or r <= 256) else 256
    c1 = 1.0 - ADAM_B1 ** ADAM_STEP
    c2 = 1.0 - ADAM_B2 ** ADAM_STEP

    def body(w_ref, g_ref, m_ref, v_ref, g_o, d_o, m_o, v_o):
        gv = g_ref[...].T if g_transposed else g_ref[...]
        mn = ADAM_B1 * m_ref[0] + (1.0 - ADAM_B1) * gv
        vn = ADAM_B2 * v_ref[0] + (1.0 - ADAM_B2) * (gv * gv)
        g_o[0] = gv
        m_o[0] = mn
        v_o[0] = vn
        d_o[0] = -ADAM_LR * ((mn / c1) / (jnp.sqrt(vn / c2) + ADAM_EPS) + ADAM_WD * w_ref[0])

    spec = pl.BlockSpec((1, tr, c), lambda i: (0, i, 0))
    gspec = pl.BlockSpec((c, r), lambda i: (0, 0)) if g_transposed else pl.BlockSpec((tr, c), lambda i: (i, 0))
    shp = jax.ShapeDtypeStruct((1, r, c), F32)
    return pl.pallas_call(
        body, name=name, grid=(r // tr,), out_shape=(shp,) * 4, in_specs=[spec, gspec, spec, spec], out_specs=(spec,) * 4,
        compiler_params=_params(("parallel",)),
    )(w, g, m, v)


def _rope_tables(s_len):
    def angles(pos, dim):
        inv = np.float32(ROPE_THETA) ** (-np.arange(0, dim, 2, dtype=np.float32) / np.float32(dim))
        return pos.astype(np.float32)[:, None] * inv[None, :]

    tpos = np.arange(s_len)
    a1 = angles(tpos, QK_ROPE)
    ar = angles(tpos // GRID_W, HD_B // 2)
    ac = angles(tpos % GRID_W, HD_B // 2)
    z16 = np.zeros((s_len, 16), np.float32)
    z32 = np.zeros((s_len, 32), np.float32)
    z64 = np.zeros((s_len, 64), np.float32)
    one64 = np.ones((s_len, 64), np.float32)
    c1, s1 = np.cos(a1), np.sin(a1)
    ca = np.concatenate([one64, c1, c1, z32], axis=1)
    ck = np.concatenate([z64, c1, c1, z32], axis=1)
    s1a = np.concatenate([z64, -s1, z16, z32], axis=1)
    s2a = np.concatenate([z64, z16, s1, z32], axis=1)
    cr, sr, cc, sc = np.cos(ar), np.sin(ar), np.cos(ac), np.sin(ac)
    cb = np.concatenate([cr, cr, cc, cc, z64], axis=1)
    s1b = np.concatenate([-sr, z16, -sc, z16, z64], axis=1)
    s2b = np.concatenate([z16, sr, z16, sc, z64], axis=1)
    return jnp.asarray(np.stack([ca, s1a, s2a, ck, cb, s1b, s2b]).astype(np.float32))


def _pad_heads(a, n_heads, axis):
    shp = a.shape
    a = a.reshape(shp[:axis] + (n_heads, shp[axis] // n_heads) + shp[axis + 1:])
    pad = [(0, 0)] * a.ndim
    pad[axis + 1] = (0, HP - a.shape[axis + 1])
    a = jnp.pad(a, pad)
    return a.reshape(shp[:axis] + (n_heads * HP,) + shp[axis + 1:])


def _unpad_heads(a, n_heads, width, axis):
    shp = a.shape
    a = a.reshape(shp[:axis] + (n_heads, HP) + shp[axis + 1:])
    a = lax.slice_in_dim(a, 0, width, axis=axis + 1)
    return a.reshape(shp[:axis] + (n_heads * width,) + shp[axis + 1:])


def _pack_rows(blocks, names):
    parts = []
    for name in names:
        b = blocks[name]
        padr = PACK_ROWS[name] - b.shape[-2]
        if padr:
            b = jnp.pad(b, [(0, 0)] * (b.ndim - 2) + [(0, padr), (0, 0)])
        parts.append(b)
    return jnp.concatenate(parts, axis=parts[0].ndim - 2)


def _expand_w_in(wt):
    z64 = jnp.zeros((64, D_MODEL), wt.dtype)
    z32 = jnp.zeros((32, D_MODEL), wt.dtype)
    return jnp.concatenate([
        _pad_heads(wt[416:928], H_B, 0), wt[1184:2208], wt[2208:3232], wt[0:256],
        _pad_heads(wt[928:1056], KV_B, 0), _pad_heads(wt[1056:1184], KV_B, 0), wt[256:384],
        z64, wt[384:416], z32], axis=0)


def _collapse_w_in(dq, dg, ds):
    return jnp.concatenate([
        ds[0:256], ds[768:896], ds[960:992], _unpad_heads(dq, H_B, HD_B, 0), _unpad_heads(ds[256:512], KV_B, HD_B, 0),
        _unpad_heads(ds[512:768], KV_B, HD_B, 0), dg], axis=0)


def kernel(x, p, g_mix, w_in, g_qa, w_qb, g_kva, w_kvb, g_qn, g_kn, w_oa, w_ob, w_o, g_mlp, w_up, w_down, g_ple, w_ple_gate, w_ple, g_final, loss_target, m_g_mix, m_w_in, m_g_qa, m_w_qb, m_g_kva, m_w_kvb, m_g_qn, m_g_kn, m_w_oa, m_w_ob, m_w_o, m_g_mlp, m_w_up, m_w_down, m_g_ple, m_w_ple_gate, m_w_ple, m_g_final, v_g_mix, v_w_in, v_g_qa, v_w_qb, v_g_kva, v_w_kvb, v_g_qn, v_g_kn, v_w_oa, v_w_ob, v_w_o, v_g_mlp, v_w_up, v_w_down, v_g_ple, v_w_ple_gate, v_w_ple, v_g_final):
    n_b, s_len, _ = x.shape
    t = n_b * s_len
    tm = min(256, s_len)
    tq_f = min(1024, s_len)
    tq_b = min(1024, s_len)

    mats = dict(w_in=(w_in, m_w_in, v_w_in), w_qb=(w_qb, m_w_qb, v_w_qb), w_kvb=(w_kvb, m_w_kvb, v_w_kvb),
                w_oa=(w_oa, m_w_oa, v_w_oa), w_ob=(w_ob, m_w_ob, v_w_ob), w_o=(w_o, m_w_o, v_w_o),
                w_up=(w_up, m_w_up, v_w_up), w_down=(w_down, m_w_down, v_w_down),
                w_ple_gate=(w_ple_gate, m_w_ple_gate, v_w_ple_gate), w_ple=(w_ple, m_w_ple, v_w_ple))
    col_sharded = ("w_in", "w_qb", "w_kvb", "w_oa", "w_ob", "w_up", "w_ple")

    blocks = {}
    for name, (w, _, _) in mats.items():
        w2 = w[0]
        if name in col_sharded:
            w2 = w2.T
        blocks[name] = w2.reshape(-1, D_MODEL).astype(BF16)
    off_w1, _ = _pack_offsets(PACK_W1)
    off_w2, _ = _pack_offsets(PACK_W2)
    full1 = allgather_rows(_pack_rows(blocks, PACK_W1))
    pack2 = _pack_rows(blocks, PACK_W2)

    def gathered(full, offs, name, rows, width):
        return full[:, offs[name]:offs[name] + rows].reshape(-1, width)

    w_in_t = _expand_w_in(gathered(full1, off_w1, "w_in", 404, D_MODEL))
    w_qb_t = _pad_heads(gathered(full1, off_w1, "w_qb", 24, Q_LORA), H_A, 0)
    wkvb = gathered(full1, off_w1, "w_kvb", 16, KV_LORA).reshape(H_A, 2, 64, KV_LORA)
    w_kvb_t = jnp.concatenate([_pad_heads(wkvb[:, 0].reshape(-1, KV_LORA), H_A, 0),
                               _pad_heads(wkvb[:, 1].reshape(-1, KV_LORA), H_A, 0)], axis=0)

    tabs = _rope_tables(s_len)
    g_qn_p = jnp.pad(g_qn, ((0, 0), (0, HP - HD_B)))
    g_kn_p = jnp.pad(g_kn, ((0, 0), (0, HP - HD_B)))
    xf = x.reshape(t, D_MODEL)
    pf = p.reshape(t, PLE_DIM)
    tgt = loss_target.reshape(t, D_MODEL)

    zp, h = in_proj(xf, g_mix, w_in_t, tm)
    qa, ka, va, qb, kb, vb, cq, ckv = attn_prep(zp, tabs, g_qa, g_kva, g_qn_p, g_kn_p, w_qb_t, w_kvb_t, tm, s_len)
    oa, lse_a, full2 = attn_fwd(qa, ka, va, n_b, s_len, tq_f, "attn_a_fwd", comm=gather_first_comm(pack2))
    ob, lse_b, full2 = attn_fwd(qb, kb, vb, n_b, s_len, tq_f, "attn_b_fwd", comm=gather_pass_comm(full2))
    w_oa_t = _pad_heads(gathered(full2, off_w2, "w_oa", 64, H_A * V_DIM_A), H_A, 1)
    w_ob_t = _pad_heads(gathered(full2, off_w2, "w_ob", 64, H_B * HD_B), H_B, 1)
    w_ple_t = gathered(full2, off_w2, "w_ple", 32, PLE_DIM)
    x1, merged = merge_fwd(oa, ob, zp, xf, w_oa_t, w_ob_t, full2, off_w2, tm)
    x2, u = mlp_fwd(x1, g_mlp, full2, off_w2, tm)
    dx2, dt, h3, dpe, st_ple, dx2b = ple_loss_bwd(x2, pf, tgt, g_ple, g_final.reshape(1, D_MODEL), full2, off_w2, w_ple_t, tm)
    dx1, da, h2, st_mlp, dx1b = mlp_bwd(dx2, x1, u, g_mlp, full2, off_w2, tm)

    core = lax.axis_index("c").astype(jnp.int32).reshape(1)
    chip = (2 * lax.axis_index("x") + lax.axis_index("y")).astype(jnp.int32).reshape(1)

    def packed(gblocks, names):
        return _pack_rows({n: gblocks[n].reshape(N_DEV, -1, D_MODEL).astype(BF16) for n in names}, names)

    off_g1, rows_g1 = _pack_offsets(PACK_G1)
    gpack1 = matmul_tn_packed(da, h2, "gw_up", 512, off_g1["w_up"], rows_g1)
    gpack1 = matmul_tn_packed(u, dx2b, "gw_down", 512, off_g1["w_down"], rows_g1, buf=gpack1, square_a=True)
    gpack1 = matmul_tn_packed(h3, dt, "gw_pg", 128, off_g1["w_ple_gate"], rows_g1, buf=gpack1)
    gple = matmul_tn(dpe, pf, "gw_ple").reshape(N_DEV, -1, D_MODEL).astype(BF16)
    gpack1 = lax.dynamic_update_slice(gpack1, gple, (0, off_g1["w_ple"], 0))
    doa, dob, dgab, dya, dyb, got1 = merge_bwd(dx1, oa, ob, zp, w_oa_t, w_ob_t, full2, off_w2, tm,
                                               comm=scatter_sibling_comm(gpack1))
    part1 = add_pairs(gpack1, got1, core)
    dqa, dka, dva, land1 = attn_bwd(qa, ka, va, doa, oa, lse_a, n_b, s_len, tq_b, "attn_a_bwd", comm=scatter_chips_comm(part1))
    gshard1 = sum_chips(part1, land1, chip)

    off_g2, rows_g2 = _pack_offsets(PACK_G2)
    g2 = dict(w_oa=_unpad_heads(matmul_tn(dya, oa, "gw_oa"), H_A, V_DIM_A, 1),
              w_ob=_unpad_heads(matmul_tn(dyb, ob, "gw_ob"), H_B, HD_B, 1))
    gpack2 = matmul_tn_packed(merged, dx1b, "gw_o", 128, off_g2["w_o"], rows_g2)
    gpack2 = lax.dynamic_update_slice(gpack2, packed(g2, ("w_oa", "w_ob")), (0, off_g2["w_oa"], 0))
    part2 = add_pairs(gpack2, exchange_sibling(gpack2), core)
    dqb, dkb, dvb, land2 = attn_bwd(qb, kb, vb, dob, ob, lse_b, n_b, s_len, tq_b, "attn_b_bwd", comm=scatter_chips_comm(part2))
    gshard2 = sum_chips(part2, land2, chip)
    dzq, dsm, dqap, dkva, st_prep = prep_bwd(dqa, dka, dva, dqb, dkb, dvb, zp, tabs, g_qa, g_kva, g_qn_p, g_kn_p,
                                             w_qb_t, w_kvb_t, tm, s_len)

    gkv = matmul_tn(dkva, ckv, "gw_kvb")
    g3 = dict(
        w_in=_collapse_w_in(matmul_tn(dzq, h, "gw_in_q"), matmul_tn(dgab, h, "gw_in_g"), matmul_tn(dsm, h, "gw_in_s")),
        w_qb=_unpad_heads(matmul_tn(dqap, cq, "gw_qb"), H_A, QK_NOPE + QK_ROPE, 0),
        w_kvb=jnp.stack([_unpad_heads(gkv[:H_A * HP], H_A, 64, 0).reshape(H_A, 64, KV_LORA),
                         _unpad_heads(gkv[H_A * HP:], H_A, 64, 0).reshape(H_A, 64, KV_LORA)], axis=1))
    gpack3 = packed(g3, PACK_G3)
    part3 = add_pairs(gpack3, exchange_sibling(gpack3), core)
    grad_x, st_mix, land3 = in_bwd(dzq, dgab, dsm, xf, dx1, g_mix, w_in_t, tm, comm=scatter_chips_comm(part3))
    gshard3 = sum_chips(part3, land3, chip)
    off_g3, _ = _pack_offsets(PACK_G3)
    shards = {n: (gshard1, off_g1[n]) for n in PACK_G1}
    shards.update({n: (gshard2, off_g2[n]) for n in PACK_G2})
    shards.update({n: (gshard3, off_g3[n]) for n in PACK_G3})

    stats = allreduce_stats(st_mix, st_prep, st_mlp, st_ple)
    loss = jnp.sum(stats[ST_LOSS])

    out_g, out_d, out_m, out_v = {}, {}, {}, {}
    for name, (w, m, v) in mats.items():
        gshard, off = shards[name]
        r, c = w.shape[1:]
        if name in col_sharded:
            g2 = gshard[off:off + (r * c) // D_MODEL].reshape(c, r)
            in_kernel = r % 128 == 0 and c % 128 == 0
            res = adamw(w, g2 if in_kernel else g2.T, m, v, "adamw_" + name, g_transposed=in_kernel)
        else:
            res = adamw(w, gshard[off:off + r], m, v, "adamw_" + name)
        out_g[name], out_d[name], out_m[name], out_v[name] = res

    gains = (("g_mix", g_mix, m_g_mix, v_g_mix, ST_G_MIX), ("g_qa", g_qa, m_g_qa, v_g_qa, ST_G_QA),
             ("g_kva", g_kva, m_g_kva, v_g_kva, ST_G_KVA), ("g_qn", g_qn, m_g_qn, v_g_qn, ST_G_QN),
             ("g_kn", g_kn, m_g_kn, v_g_kn, ST_G_KN), ("g_mlp", g_mlp, m_g_mlp, v_g_mlp, ST_G_MLP),
             ("g_ple", g_ple, m_g_ple, v_g_ple, ST_G_PLE), ("g_final", g_final, m_g_final, v_g_final, ST_G_FINAL))
    res = adamw_gains(stats, [(r_, w.reshape(1, -1), m.reshape(1, -1), v.reshape(1, -1)) for _, w, m, v, r_ in gains])
    for (name, w, _, _, _), (gg, gd, gm, gv) in zip(gains, res):
        out_g[name], out_d[name], out_m[name], out_v[name] = (a.reshape(w.shape) for a in (gg, gd, gm, gv))

    order = ("g_mix", "w_in", "g_qa", "w_qb", "g_kva", "w_kvb", "g_qn", "g_kn", "w_oa", "w_ob", "w_o", "g_mlp",
             "w_up", "w_down", "g_ple", "w_ple_gate", "w_ple", "g_final")
    return (loss, grad_x.reshape(x.shape), *[out_g[n] for n in order], *[out_d[n] for n in order],
            *[out_m[n] for n in order], *[out_v[n] for n in order])
```

```python
import numpy as np
import jax
import jax.numpy as jnp
from jax import lax
from jax.experimental import pallas as pl
from jax.experimental.pallas import tpu as pltpu

F32 = jnp.float32
BF16 = jnp.bfloat16

D_MODEL = 1024
EPS = 1e-6
ROPE_THETA = 10000.0
GRID_W = 64
H_A = 8
QK_NOPE = 64
QK_ROPE = 32
V_DIM_A = 64
Q_LORA = 256
KV_LORA = 128
H_B = 8
KV_B = 2
HD_B = 64
D_FF = 4 * D_MODEL
PLE_DIM = 256
HP = 128
ZP = 4096
N_DEV = 8
N_CHIP = 4

ADAM_LR = 0.001
ADAM_B1 = 0.9
ADAM_B2 = 0.999
ADAM_EPS = 1e-08
ADAM_WD = 0.01
ADAM_STEP = 10

VMEM_LIMIT = 52 * 1024 * 1024

PACK_ROWS = dict(w_in=416, w_qb=32, w_kvb=16, w_oa=64, w_ob=64, w_o=128, w_up=512, w_down=512, w_ple_gate=128, w_ple=32)
PACK_W1 = ("w_in", "w_qb", "w_kvb")
PACK_W2 = ("w_up", "w_down", "w_o", "w_ple_gate", "w_oa", "w_ob", "w_ple")
PACK_G1 = ("w_up", "w_down", "w_ple_gate", "w_ple")
PACK_G2 = ("w_o", "w_oa", "w_ob")
PACK_G3 = ("w_in", "w_qb", "w_kvb")


def _pack_offsets(names):
    off, o = {}, 0
    for n in names:
        off[n] = o
        o += PACK_ROWS[n]
    return off, o

ST_G_MIX, ST_G_QA, ST_G_KVA, ST_G_QN, ST_G_KN, ST_G_MLP, ST_G_PLE, ST_G_FINAL, ST_LOSS = range(9)
ST_ROWS = 16


def _dot_nn(a, b):
    return lax.dot_general(a, b, (((1,), (0,)), ((), ())), preferred_element_type=F32)


def _dot_nt(a, b):
    return lax.dot_general(a, b, (((1,), (1,)), ((), ())), preferred_element_type=F32)


def _dot_tn(a, b):
    return lax.dot_general(a, b, (((0,), (0,)), ((), ())), preferred_element_type=F32)


def _rstd(x, n):
    return lax.rsqrt(jnp.sum(x * x, axis=-1, keepdims=True) * (1.0 / n) + EPS)


def _rms_bwd(dy, xh, r, g, n):
    dxh = dy * g
    return r * (dxh - xh * (jnp.sum(dxh * xh, axis=-1, keepdims=True) * (1.0 / n)))


def _rope_fwd(x, c, s1, s2):
    return x * c + pltpu.roll(x, HP - 16, 1) * s1 + pltpu.roll(x, 16, 1) * s2


def _rope_bwd(d, c, s1, s2):
    return d * c + pltpu.roll(d * s1, 16, 1) + pltpu.roll(d * s2, HP - 16, 1)


def _colsum(v):
    return jnp.sum(v, axis=0, keepdims=True)


def _params(sem=None, vmem=VMEM_LIMIT):
    return pltpu.CompilerParams(dimension_semantics=sem, vmem_limit_bytes=vmem)


def _resident(shape):
    nd = len(shape)
    return pl.BlockSpec(shape, lambda *_: (0,) * nd, pipeline_mode=pl.Buffered(1))


def _rows(tm, width, col=0):
    return pl.BlockSpec((tm, width), lambda i: (i, col))


def _packed_weight(rows, off):
    return pl.BlockSpec((N_DEV, rows, D_MODEL), lambda *_: (0, off // rows, 0), pipeline_mode=pl.Buffered(1))


def _wrows(ref, start, size):
    rows = ref.shape[1]
    return ref[start // rows:(start + size) // rows].reshape(size, D_MODEL)


def _mesh_pos():
    return lax.axis_index("x"), lax.axis_index("y"), lax.axis_index("c")


def _flip(v, bit):
    return (1 - v) if bit else v


_ANY = pl.BlockSpec(memory_space=pl.ANY)
_MESH = pl.DeviceIdType.MESH


def _remote(src, dst, send_sems, recv_sems, k, to):
    return pltpu.make_async_remote_copy(src_ref=src, dst_ref=dst, send_sem=send_sems.at[k], recv_sem=recv_sems.at[k],
                                        device_id=to, device_id_type=_MESH)


def _sibling_copies(g_ref, got_ref, send_sems, recv_sems):
    x, y, c = _mesh_pos()
    return [_remote(g_ref.at[2 * j + (1 - c)], got_ref.at[j], send_sems, recv_sems, j, (x, y, 1 - c)) for j in range(N_CHIP)]


def _chip_copies(p_ref, land_ref, send_sems, recv_sems):
    x, y, c = _mesh_pos()
    copies = []
    for k in (1, 2, 3):
        tx, ty = _flip(x, k & 2), _flip(y, k & 1)
        copies.append(_remote(p_ref.at[2 * tx + ty], land_ref.at[k - 1], send_sems, recv_sems, k - 1, (tx, ty, c)))
    return copies


class _Comm:
    def __init__(self, ins, out_shapes, sems, make, aliases=None):
        self.ins, self.out_shapes, self.sems, self.make, self.aliases = list(ins), list(out_shapes), list(sems), make, aliases or {}


def _comm_parts(comm, n_in, n_out):
    if comm is None:
        return [], [], [], [], {}
    alias = {n_in + j: n_out + k for j, k in comm.aliases.items()}
    return comm.ins, [_ANY] * len(comm.ins), comm.out_shapes, comm.sems, alias


def _split_refs(refs, n_in, n_out, n_scratch, comm):
    n_ci = len(comm.ins) if comm else 0
    n_co = len(comm.out_shapes) if comm else 0
    cuts, i = [], 0
    for n in (n_in, n_ci, n_out, n_co, n_scratch):
        cuts.append(refs[i:i + n])
        i += n
    return (*cuts, refs[i:])


def _grid_edge(grid, last):
    cond = None
    for d, n in enumerate(grid):
        here = pl.program_id(d) == (n - 1 if last else 0)
        cond = here if cond is None else cond & here
    return cond


def _comm_start(comm, cin, cout, csem, grid):
    if comm is not None:
        @pl.when(_grid_edge(grid, False))
        def _():
            for cp in comm.make(cin, cout, csem):
                cp.start()


def _comm_finish(comm, cin, cout, csem, grid):
    if comm is not None:
        @pl.when(_grid_edge(grid, True))
        def _():
            for cp in comm.make(cin, cout, csem):
                cp.wait()


def gather_first_comm(shard):
    r, w = shard.shape

    def make(cin, cout, sems):
        (x_ref,), (out_ref,), (send_sems, recv_sems, local_sem) = cin, cout, sems
        x, y, c = _mesh_pos()
        mine = out_ref.at[4 * x + 2 * y + c]
        targets = [(x, y, 1 - c), (1 - x, y, c), (x, 1 - y, c), (1 - x, 1 - y, c)]
        return [_remote(x_ref, mine, send_sems, recv_sems, k, to) for k, to in enumerate(targets)] + [
            pltpu.make_async_copy(x_ref, mine, local_sem)]

    return _Comm([shard], [jax.ShapeDtypeStruct((N_DEV, r, w), shard.dtype)],
                 [pltpu.SemaphoreType.DMA((4,)), pltpu.SemaphoreType.DMA((4,)), pltpu.SemaphoreType.DMA], make)


def gather_pass_comm(full):
    def make(cin, cout, sems):
        (in_ref,), (out_ref,), (send_sems, recv_sems) = cin, cout, sems
        x, y, c = _mesh_pos()
        copies = []
        for k, (px, py) in enumerate([(1 - x, y), (x, 1 - y), (1 - x, 1 - y)]):
            idx = 4 * px + 2 * py + c
            copies.append(_remote(in_ref.at[idx], out_ref.at[idx], send_sems, recv_sems, k, (x, y, 1 - c)))
        return copies

    return _Comm([full], [jax.ShapeDtypeStruct(full.shape, full.dtype)],
                 [pltpu.SemaphoreType.DMA((3,)), pltpu.SemaphoreType.DMA((3,))], make, aliases={0: 0})


def scatter_sibling_comm(g):
    _, r, w = g.shape
    return _Comm([g], [jax.ShapeDtypeStruct((N_CHIP, r, w), g.dtype)],
                 [pltpu.SemaphoreType.DMA((N_CHIP,)), pltpu.SemaphoreType.DMA((N_CHIP,))],
                 lambda cin, cout, sems: _sibling_copies(cin[0], cout[0], sems[0], sems[1]))


def scatter_chips_comm(part):
    _, r, w = part.shape
    return _Comm([part], [jax.ShapeDtypeStruct((N_CHIP - 1, r, w), part.dtype)],
                 [pltpu.SemaphoreType.DMA((3,)), pltpu.SemaphoreType.DMA((3,))],
                 lambda cin, cout, sems: _chip_copies(cin[0], cout[0], sems[0], sems[1]))


def allgather_rows(shard):
    r, w = shard.shape

    def body(x_ref, out_ref, send_sems, recv_sems, local_sem):
        x, y, c = _mesh_pos()
        me, sibling = (x, y, c), (x, y, 1 - c)
        chips = [(1 - x, y), (x, 1 - y), (1 - x, 1 - y)]

        def slot(px, py, pc):
            return out_ref.at[4 * px + 2 * py + pc]

        def copy(k, block, to, src=None):
            return pltpu.make_async_remote_copy(
                src_ref=slot(*block) if src is None else src, dst_ref=slot(*block),
                send_sem=send_sems.at[k], recv_sem=recv_sems.at[k], device_id=to, device_id_type=_MESH)

        mine = pltpu.make_async_copy(x_ref, slot(*me), local_sem)
        mine.start()
        first = [copy(0, me, sibling, src=x_ref)]
        first += [copy(1 + j, me, (*chip, c), src=x_ref) for j, chip in enumerate(chips)]
        for cp in first:
            cp.start()
        passed = [copy(4 + j, (*chip, c), sibling) for j, chip in enumerate(chips)]
        for j, chip in enumerate(chips):
            copy(1 + j, (*chip, c), me).wait_recv()
            passed[j].start()
        copy(0, sibling, me).wait_recv()
        for j, chip in enumerate(chips):
            copy(4 + j, (*chip, 1 - c), me).wait_recv()
        for cp in first + passed:
            cp.wait_send()
        mine.wait()

    return pl.pallas_call(
        body, name="allgather_rows",
        out_shape=jax.ShapeDtypeStruct((N_DEV, r, w), shard.dtype),
        in_specs=[_ANY], out_specs=_ANY,
        scratch_shapes=[pltpu.SemaphoreType.DMA((7,)), pltpu.SemaphoreType.DMA((7,)), pltpu.SemaphoreType.DMA],
    )(shard)


def exchange_sibling(g):
    _, r, w = g.shape

    def body(g_ref, got_ref, send_sems, recv_sems):
        copies = _sibling_copies(g_ref, got_ref, send_sems, recv_sems)
        for cp in copies:
            cp.start()
        for cp in copies:
            cp.wait()

    return pl.pallas_call(
        body, name="exchange_sibling", out_shape=jax.ShapeDtypeStruct((N_CHIP, r, w), g.dtype),
        in_specs=[_ANY], out_specs=_ANY,
        scratch_shapes=[pltpu.SemaphoreType.DMA((N_CHIP,)), pltpu.SemaphoreType.DMA((N_CHIP,))],
    )(g)


def exchange_chips(part):
    _, r, w = part.shape

    def body(p_ref, land_ref, send_sems, recv_sems):
        copies = _chip_copies(p_ref, land_ref, send_sems, recv_sems)
        for cp in copies:
            cp.start()
        for cp in copies:
            cp.wait()

    return pl.pallas_call(
        body, name="exchange_chips", out_shape=jax.ShapeDtypeStruct((N_CHIP - 1, r, w), part.dtype),
        in_specs=[_ANY], out_specs=_ANY,
        scratch_shapes=[pltpu.SemaphoreType.DMA((3,)), pltpu.SemaphoreType.DMA((3,))],
    )(part)


def allreduce_stats(st_mix, st_prep, st_mlp, st_ple):
    def body(mix_ref, prep_ref, mlp_ref, ple_ref, out_ref, mine, gath, send_sems, recv_sems):
        x, y, c = _mesh_pos()
        me = 4 * x + 2 * y + c
        mine[...] = jnp.zeros_like(mine)
        mine[ST_G_MIX:ST_G_MIX + 1, :] = mix_ref[...]
        mine[ST_G_QA:ST_G_KN + 1, 0:256] = prep_ref[...]
        mine[ST_G_MLP:ST_G_MLP + 1, :] = mlp_ref[...]
        mine[ST_G_PLE:ST_LOSS + 1, :] = ple_ref[...]
        gath[me] = mine[...]
        copies = []
        for k in range(1, N_DEV):
            peer = (_flip(x, k & 4), _flip(y, k & 2), _flip(c, k & 1))
            copies.append(_remote(mine, gath.at[me], send_sems, recv_sems, k - 1, peer))
        for cp in copies:
            cp.start()
        for cp in copies:
            cp.wait()
        acc = gath[0]
        for d in range(1, N_DEV):
            acc = acc + gath[d]
        out_ref[...] = acc

    vm = pl.BlockSpec(memory_space=pltpu.VMEM)
    return pl.pallas_call(
        body, name="allreduce_stats", out_shape=jax.ShapeDtypeStruct((ST_ROWS, D_MODEL), F32),
        in_specs=[vm] * 4, out_specs=vm,
        scratch_shapes=[pltpu.VMEM((ST_ROWS, D_MODEL), F32), pltpu.VMEM((N_DEV, ST_ROWS, D_MODEL), F32),
                        pltpu.SemaphoreType.DMA((N_DEV - 1,)), pltpu.SemaphoreType.DMA((N_DEV - 1,))],
    )(st_mix, st_prep, st_mlp, st_ple)


def adamw_gains(stats, gains):
    c1 = 1.0 - ADAM_B1 ** ADAM_STEP
    c2 = 1.0 - ADAM_B2 ** ADAM_STEP
    n = len(gains)

    def body(st_ref, *refs):
        ins, outs = refs[:3 * n], refs[3 * n:]
        for i, (row, w, _, _) in enumerate(gains):
            width = w.shape[1]
            gv = st_ref[row:row + 1, 0:width]
            mn = ADAM_B1 * ins[3 * i + 1][...] + (1.0 - ADAM_B1) * gv
            vn = ADAM_B2 * ins[3 * i + 2][...] + (1.0 - ADAM_B2) * (gv * gv)
            outs[4 * i][...] = gv
            outs[4 * i + 1][...] = -ADAM_LR * ((mn / c1) / (jnp.sqrt(vn / c2) + ADAM_EPS) + ADAM_WD * ins[3 * i][...])
            outs[4 * i + 2][...] = mn
            outs[4 * i + 3][...] = vn

    vm = pl.BlockSpec(memory_space=pltpu.VMEM)
    flat = [a for (_, w, m, v) in gains for a in (w, m, v)]
    out_shape = tuple(jax.ShapeDtypeStruct(w.shape, F32) for (_, w, _, _) in gains for _ in range(4))
    res = pl.pallas_call(body, name="adamw_gains", out_shape=out_shape, in_specs=[vm] * (1 + 3 * n),
                         out_specs=tuple([vm] * (4 * n)))(stats, *flat)
    return [res[4 * i:4 * i + 4] for i in range(n)]


def _row_tile(r, cap=640):
    return max(d for d in range(16, min(r, cap) + 1, 16) if r % d == 0)


def add_pairs(g, got, core):
    n, r, w = got.shape
    tr = _row_tile(r)

    def body(c_ref, a_ref, b_ref, o_ref):
        o_ref[...] = (a_ref[...].astype(F32) + b_ref[...].astype(F32)).astype(o_ref.dtype)

    spec = pl.BlockSpec((1, tr, w), lambda i, j, c: (i, j, 0))
    return pl.pallas_call(
        body, name="add_pairs", out_shape=jax.ShapeDtypeStruct(got.shape, got.dtype),
        grid_spec=pltpu.PrefetchScalarGridSpec(
            num_scalar_prefetch=1, grid=(n, r // tr),
            in_specs=[pl.BlockSpec((1, tr, w), lambda i, j, c: (2 * i + c[0], j, 0)), spec], out_specs=spec),
        compiler_params=_params(("parallel", "parallel")),
    )(core, g, got)


def sum_chips(part, land, chip):
    _, r, w = part.shape
    tr = _row_tile(r)

    def body(c_ref, p_ref, l_ref, o_ref):
        acc = p_ref[0].astype(F32)
        for s in range(N_CHIP - 1):
            acc = acc + l_ref[s].astype(F32)
        o_ref[...] = acc

    return pl.pallas_call(
        body, name="sum_chips", out_shape=jax.ShapeDtypeStruct((r, w), F32),
        grid_spec=pltpu.PrefetchScalarGridSpec(
            num_scalar_prefetch=1, grid=(r // tr,),
            in_specs=[pl.BlockSpec((1, tr, w), lambda i, c: (c[0], i, 0)), pl.BlockSpec((N_CHIP - 1, tr, w), lambda i, c: (0, i, 0))],
            out_specs=pl.BlockSpec((tr, w), lambda i, c: (i, 0))),
        compiler_params=_params(("parallel",)),
    )(chip, part, land)


def in_proj(x, g_mix, w_in_t, tm):
    t = x.shape[0]
    nc = 512

    def body(x_ref, g_ref, w_ref, z_ref, h_ref):
        xv = x_ref[...]
        h = (xv * _rstd(xv, D_MODEL) * g_ref[...]).astype(BF16)
        h_ref[...] = h
        for cidx in range(ZP // nc):
            z_ref[:, cidx * nc:(cidx + 1) * nc] = _dot_nt(h, w_ref[cidx * nc:(cidx + 1) * nc, :])

    return pl.pallas_call(
        body, name="in_proj", grid=(t // tm,),
        out_shape=(jax.ShapeDtypeStruct((t, ZP), F32), jax.ShapeDtypeStruct((t, D_MODEL), BF16)),
        in_specs=[_rows(tm, D_MODEL), _resident((1, D_MODEL)), _resident((ZP, D_MODEL))],
        out_specs=(_rows(tm, ZP), _rows(tm, D_MODEL)), compiler_params=_params(("parallel",)),
    )(x, g_mix, w_in_t)


def attn_prep(zp, tabs, g_qa, g_kva, g_qn, g_kn, w_qb_t, w_kvb_t, tm, s_len):
    t = zp.shape[0]
    nsb = s_len // tm
    scale_a = (QK_NOPE + QK_ROPE) ** -0.5
    scale_b = HD_B ** -0.5

    def body(qb_ref, qlat_ref, kb_ref, vb_ref, ckv_ref, kpe_ref, tab_ref, gqa_ref, gkva_ref, gqn_ref, gkn_ref,
             wqb_ref, wkvb_ref, qa_o, ka_o, va_o, qb_o, kb_o, vb_o, cq_o, ckvn_o):
        ca, s1a, s2a = tab_ref[0], tab_ref[1], tab_ref[2]
        ck = tab_ref[3]
        cb, s1b, s2b = tab_ref[4], tab_ref[5], tab_ref[6]
        ql = qlat_ref[...]
        cq = (ql * _rstd(ql, Q_LORA) * gqa_ref[...]).astype(BF16)
        cq_o[...] = cq
        qa = _dot_nt(cq, wqb_ref[...])
        slabs = [slice(h * HP, (h + 1) * HP) for h in range(H_A)]
        qa_o[...] = jnp.concatenate(
            [(_rope_fwd(qa[:, sl], ca, s1a, s2a) * scale_a).astype(BF16) for sl in slabs], axis=1)
        cr = ckv_ref[...]
        ckv = (cr * _rstd(cr, KV_LORA) * gkva_ref[...]).astype(BF16)
        ckvn_o[...] = ckv
        kva = _dot_nt(ckv, wkvb_ref[...])
        kpe = _rope_fwd(kpe_ref[...], ck, s1a, s2a)
        ka_o[...] = jnp.concatenate([(kva[:, sl] + kpe).astype(BF16) for sl in slabs], axis=1)
        va_o[...] = kva[:, H_A * HP:].astype(BF16)
        gqn, gkn = gqn_ref[...], gkn_ref[...]

        def norm_rope(ref, sl, g, scale):
            xs = ref[:, sl]
            y = _rope_fwd(xs * _rstd(xs, HD_B) * g, cb, s1b, s2b)
            return (y if scale is None else y * scale).astype(BF16)

        qb_o[...] = jnp.concatenate([norm_rope(qb_ref, sl, gqn, scale_b) for sl in slabs], axis=1)
        kb_o[...] = jnp.concatenate([norm_rope(kb_ref, sl, gkn, None) for sl in slabs[:KV_B]], axis=1)
        vb_o[...] = vb_ref[...].astype(BF16)

    def o(width):
        return jax.ShapeDtypeStruct((t, width), BF16)

    return pl.pallas_call(
        body, name="attn_prep", grid=(t // tm,),
        out_shape=(o(H_A * HP), o(H_A * HP), o(H_A * HP), o(H_B * HP), o(KV_B * HP), o(KV_B * HP), o(Q_LORA), o(KV_LORA)),
        in_specs=[_rows(tm, 1024, 0), _rows(tm, 256, 12), _rows(tm, 256, 13), _rows(tm, 256, 14),
                  _rows(tm, 128, 30), _rows(tm, 128, 31),
                  pl.BlockSpec((7, tm, HP), lambda i: (0, i % nsb, 0)),
                  _resident((1, Q_LORA)), _resident((1, KV_LORA)), _resident((1, HP)), _resident((1, HP)),
                  _resident((H_A * HP, Q_LORA)), _resident((2 * H_A * HP, KV_LORA))],
        out_specs=(_rows(tm, H_A * HP), _rows(tm, H_A * HP), _rows(tm, H_A * HP), _rows(tm, H_B * HP),
                   _rows(tm, KV_B * HP), _rows(tm, KV_B * HP), _rows(tm, Q_LORA), _rows(tm, KV_LORA)),
        compiler_params=_params(("parallel",)),
    )(zp, zp, zp, zp, zp, zp, tabs, g_qa, g_kva, g_qn, g_kn, w_qb_t, w_kvb_t)


def attn_fwd(q, k, v, n_b, s_len, tq, name, comm=None):
    t = q.shape[0]
    n_h, n_hk = q.shape[1] // HP, k.shape[1] // HP
    grp = n_h // n_hk
    nq = s_len // tq
    sub = min(tq, 256)
    grid = (n_b, n_h, nq)
    c_ins, c_in_specs, c_outs, c_sems, alias = _comm_parts(comm, 3, 2)

    def body(*refs):
        (q_ref, k_ref, v_ref), cin, (o_ref, lse_ref), cout, _, csem = _split_refs(refs, 3, 2, 0, comm)
        _comm_start(comm, cin, cout, csem, grid)
        kv, vv = k_ref[...], v_ref[...]
        for r in range(tq // sub):
            rows = slice(r * sub, (r + 1) * sub)
            s = _dot_nt(q_ref[rows, :], kv)
            m = jnp.max(s, axis=-1, keepdims=True)
            p = jnp.exp(s - m)
            l = jnp.sum(p, axis=-1, keepdims=True)
            o_ref[rows, :] = (_dot_nn(p.astype(BF16), vv) * (1.0 / l)).astype(o_ref.dtype)
            lse_ref[rows, :] = jnp.broadcast_to(m + jnp.log(l), (sub, HP))
        _comm_finish(comm, cin, cout, csem, grid)

    qspec = pl.BlockSpec((tq, HP), lambda b, h, i: (b * nq + i, h))
    kspec = pl.BlockSpec((s_len, HP), lambda b, h, i: (b, h // grp))
    return pl.pallas_call(
        body, name=name, grid=grid,
        out_shape=(jax.ShapeDtypeStruct((t, n_h * HP), BF16), jax.ShapeDtypeStruct((t, n_h * HP), F32), *c_outs),
        in_specs=[qspec, kspec, kspec, *c_in_specs], out_specs=(qspec, qspec, *([_ANY] * len(c_outs))),
        scratch_shapes=c_sems, input_output_aliases=alias,
        compiler_params=_params(("arbitrary", "arbitrary", "arbitrary")),
    )(q, k, v, *c_ins)


def merge_fwd(oa, ob, zp, x, w_oa_t, w_ob_t, wpack, off, tm):
    t = x.shape[0]

    def body(oa_ref, ob_ref, ga_ref, gb_ref, x_ref, woa_ref, wob_ref, wo_ref, x1_o, mg_o):
        ya = _dot_nt(oa_ref[...], woa_ref[...])
        yb = _dot_nt(ob_ref[...], wob_ref[...])
        merged = (jax.nn.sigmoid(ga_ref[...]) * ya + jax.nn.sigmoid(gb_ref[...]) * yb).astype(BF16)
        mg_o[...] = merged
        x1_o[...] = x_ref[...] + _dot_nn(merged, _wrows(wo_ref, 0, D_MODEL))

    return pl.pallas_call(
        body, name="merge_fwd", grid=(t // tm,),
        out_shape=(jax.ShapeDtypeStruct((t, D_MODEL), F32), jax.ShapeDtypeStruct((t, D_MODEL), BF16)),
        in_specs=[_rows(tm, H_A * HP), _rows(tm, H_B * HP), _rows(tm, 1024, 1), _rows(tm, 1024, 2), _rows(tm, D_MODEL),
                  _resident((D_MODEL, H_A * HP)), _resident((D_MODEL, H_B * HP)), _packed_weight(128, off["w_o"])],
        out_specs=(_rows(tm, D_MODEL), _rows(tm, D_MODEL)), compiler_params=_params(("parallel",)),
    )(oa, ob, zp, zp, x, w_oa_t, w_ob_t, wpack)


def mlp_fwd(x1, g_mlp, wpack, off, tm):
    t = x1.shape[0]
    fc = 1024

    def body(x_ref, g_ref, wup_ref, wdn_ref, x2_o, u_o):
        xv = x_ref[...]
        h2 = (xv * _rstd(xv, D_MODEL) * g_ref[...]).astype(BF16)
        acc = xv
        for cidx in range(D_FF // fc):
            sl = slice(cidx * fc, (cidx + 1) * fc)
            u = jnp.maximum(_dot_nt(h2, _wrows(wup_ref, cidx * fc, fc)), 0.0)
            u_o[:, sl] = u.astype(BF16)
            acc = acc + _dot_nn((u * u).astype(BF16), _wrows(wdn_ref, cidx * fc, fc))
        x2_o[...] = acc

    return pl.pallas_call(
        body, name="mlp_fwd", grid=(t // tm,),
        out_shape=(jax.ShapeDtypeStruct((t, D_MODEL), F32), jax.ShapeDtypeStruct((t, D_FF), BF16)),
        in_specs=[_rows(tm, D_MODEL), _resident((1, D_MODEL)), _packed_weight(512, off["w_up"]), _packed_weight(512, off["w_down"])],
        out_specs=(_rows(tm, D_MODEL), _rows(tm, D_FF)), compiler_params=_params(("parallel",)),
    )(x1, g_mlp, wpack, wpack)


def ple_loss_bwd(x2, p, tgt, g_ple, g_final, wpack, off, w_ple_t, tm):
    t = x2.shape[0]
    inv_d = 1.0 / D_MODEL

    def body(x2_ref, p_ref, tg_ref, gp_ref, gf_ref, wpg_ref, wple_ref, dx2_o, dt_o, h3_o, dpe_o, st_o, dx2b_o):
        @pl.when(pl.program_id(0) == 0)
        def _():
            st_o[...] = jnp.zeros_like(st_o)

        x2v = x2_ref[...]
        gp, gf = gp_ref[...], gf_ref[...]
        w_pg = _wrows(wpg_ref, 0, D_MODEL)
        r2 = _rstd(x2v, D_MODEL)
        xh2 = x2v * r2
        h3 = (xh2 * gp).astype(BF16)
        h3_o[...] = h3
        gate = jax.nn.sigmoid(_dot_nn(h3, w_pg))
        pe = _dot_nt(p_ref[...].astype(BF16), wple_ref[...])
        x3 = x2v + gate * pe
        r3 = _rstd(x3, D_MODEL)
        xh3 = x3 * r3
        err = xh3 * gf - tg_ref[...]
        dy = err * inv_d
        dx3 = _rms_bwd(dy, xh3, r3, gf, D_MODEL)
        dpe_o[...] = (dx3 * gate).astype(BF16)
        dt = (dx3 * pe * gate * (1.0 - gate)).astype(BF16)
        dt_o[...] = dt
        dh3 = _dot_nt(dt, w_pg)
        dx2 = dx3 + _rms_bwd(dh3, xh2, r2, gp, D_MODEL)
        dx2_o[...] = dx2
        dx2b_o[...] = dx2.astype(BF16)
        st_o[0:1, :] += _colsum(dh3 * xh2)
        st_o[1:2, :] += _colsum(dy * xh3)
        st_o[2:3, :] += _colsum(err * err) * (0.5 * inv_d)

    bf = jax.ShapeDtypeStruct((t, D_MODEL), BF16)
    return pl.pallas_call(
        body, name="ple_loss_bwd", grid=(t // tm,),
        out_shape=(jax.ShapeDtypeStruct((t, D_MODEL), F32), bf, bf, bf, jax.ShapeDtypeStruct((3, D_MODEL), F32), bf),
        in_specs=[_rows(tm, D_MODEL), _rows(tm, PLE_DIM), _rows(tm, D_MODEL), _resident((1, D_MODEL)), _resident((1, D_MODEL)),
                  _packed_weight(128, off["w_ple_gate"]), _resident((D_MODEL, PLE_DIM))],
        out_specs=(_rows(tm, D_MODEL), _rows(tm, D_MODEL), _rows(tm, D_MODEL), _rows(tm, D_MODEL),
                   pl.BlockSpec((3, D_MODEL), lambda i: (0, 0)), _rows(tm, D_MODEL)),
        compiler_params=_params(("arbitrary",)),
    )(x2, p, tgt, g_ple, g_final, wpack, w_ple_t)


def mlp_bwd(dx2, x1, u, g_mlp, wpack, off, tm):
    t = x1.shape[0]
    fc = 1024

    def body(dx2_ref, x1_ref, u_ref, g_ref, wup_ref, wdn_ref, dx1_o, da_o, h2_o, st_o, dx1b_o):
        @pl.when(pl.program_id(0) == 0)
        def _():
            st_o[...] = jnp.zeros_like(st_o)

        d2 = dx2_ref[...]
        d2b = d2.astype(BF16)
        dh2 = jnp.zeros((tm, D_MODEL), F32)
        for cidx in range(D_FF // fc):
            sl = slice(cidx * fc, (cidx + 1) * fc)
            da = (_dot_nt(d2b, _wrows(wdn_ref, cidx * fc, fc)) * (2.0 * u_ref[:, sl].astype(F32))).astype(BF16)
            da_o[:, sl] = da
            dh2 = dh2 + _dot_nn(da, _wrows(wup_ref, cidx * fc, fc))
        xv = x1_ref[...]
        g = g_ref[...]
        r1 = _rstd(xv, D_MODEL)
        xh1 = xv * r1
        h2_o[...] = (xh1 * g).astype(BF16)
        st_o[...] += _colsum(dh2 * xh1)
        dx1 = d2 + _rms_bwd(dh2, xh1, r1, g, D_MODEL)
        dx1_o[...] = dx1
        dx1b_o[...] = dx1.astype(BF16)

    return pl.pallas_call(
        body, name="mlp_bwd", grid=(t // tm,),
        out_shape=(jax.ShapeDtypeStruct((t, D_MODEL), F32), jax.ShapeDtypeStruct((t, D_FF), BF16),
                   jax.ShapeDtypeStruct((t, D_MODEL), BF16), jax.ShapeDtypeStruct((1, D_MODEL), F32),
                   jax.ShapeDtypeStruct((t, D_MODEL), BF16)),
        in_specs=[_rows(tm, D_MODEL), _rows(tm, D_MODEL), _rows(tm, D_FF), _resident((1, D_MODEL)),
                  _packed_weight(512, off["w_up"]), _packed_weight(512, off["w_down"])],
        out_specs=(_rows(tm, D_MODEL), _rows(tm, D_FF), _rows(tm, D_MODEL), pl.BlockSpec((1, D_MODEL), lambda i: (0, 0)),
                   _rows(tm, D_MODEL)),
        compiler_params=_params(("arbitrary",)),
    )(dx2, x1, u, g_mlp, wpack, wpack)


def merge_bwd(dx1, oa, ob, zp, w_oa_t, w_ob_t, wpack, off, tm, comm=None):
    t = dx1.shape[0]
    grid = (t // tm,)
    c_ins, c_in_specs, c_outs, c_sems, alias = _comm_parts(comm, 8, 5)

    def body(*refs):
        ((dx1_ref, oa_ref, ob_ref, ga_ref, gb_ref, woa_ref, wob_ref, wo_ref), cin,
         (doa_o, dob_o, dg_o, dya_o, dyb_o), cout, _, csem) = _split_refs(refs, 8, 5, 0, comm)
        _comm_start(comm, cin, cout, csem, grid)
        dm = _dot_nt(dx1_ref[...].astype(BF16), _wrows(wo_ref, 0, D_MODEL))
        for o_ref, g_ref, w_ref, do_o, dy_o, col in ((oa_ref, ga_ref, woa_ref, doa_o, dya_o, 0),
                                                     (ob_ref, gb_ref, wob_ref, dob_o, dyb_o, 1)):
            yv = _dot_nt(o_ref[...], w_ref[...])
            sg = jax.nn.sigmoid(g_ref[...])
            dyv = (dm * sg).astype(BF16)
            dy_o[...] = dyv
            dg_o[:, col * D_MODEL:(col + 1) * D_MODEL] = (dm * yv * sg * (1.0 - sg)).astype(BF16)
            do_o[...] = _dot_nn(dyv, w_ref[...]).astype(BF16)
        _comm_finish(comm, cin, cout, csem, grid)

    bf = jax.ShapeDtypeStruct((t, D_MODEL), BF16)
    return pl.pallas_call(
        body, name="merge_bwd", grid=grid,
        out_shape=(bf, bf, jax.ShapeDtypeStruct((t, 2 * D_MODEL), BF16), bf, bf, *c_outs),
        in_specs=[_rows(tm, D_MODEL), _rows(tm, H_A * HP), _rows(tm, H_B * HP), _rows(tm, 1024, 1), _rows(tm, 1024, 2),
                  _resident((D_MODEL, H_A * HP)), _resident((D_MODEL, H_B * HP)), _packed_weight(128, off["w_o"]), *c_in_specs],
        out_specs=(_rows(tm, D_MODEL), _rows(tm, D_MODEL), _rows(tm, 2 * D_MODEL), _rows(tm, D_MODEL), _rows(tm, D_MODEL),
                   *([_ANY] * len(c_outs))),
        scratch_shapes=c_sems, input_output_aliases=alias,
        compiler_params=_params(("arbitrary",)),
    )(dx1, oa, ob, zp, zp, w_oa_t, w_ob_t, wpack, *c_ins)


def attn_bwd(q, k, v, do, o, lse, n_b, s_len, tq, name, comm=None):
    t = q.shape[0]
    n_h, n_hk = q.shape[1] // HP, k.shape[1] // HP
    grp = n_h // n_hk
    nq = s_len // tq
    sub = min(tq, 256)
    grid = (n_b, n_hk, grp, nq)
    c_ins, c_in_specs, c_outs, c_sems, alias = _comm_parts(comm, 6, 3)

    def body(*refs):
        ((q_ref, k_ref, v_ref, do_ref, o_ref, lse_ref), cin, (dq_o, dk_o, dv_o), cout, (p_s, ds_s),
         csem) = _split_refs(refs, 6, 3, 2, comm)
        _comm_start(comm, cin, cout, csem, grid)

        @pl.when((pl.program_id(2) == 0) & (pl.program_id(3) == 0))
        def _():
            dk_o[...] = jnp.zeros_like(dk_o)
            dv_o[...] = jnp.zeros_like(dv_o)

        kv, vv = k_ref[...], v_ref[...]
        for r in range(tq // sub):
            rows = slice(r * sub, (r + 1) * sub)
            qv, dov = q_ref[rows, :], do_ref[rows, :]
            delta = jnp.sum(dov.astype(F32) * o_ref[rows, :].astype(F32), axis=-1, keepdims=True)
            p = jnp.exp(_dot_nt(qv, kv) - lse_ref[rows, 0:1])
            ds = (p * (_dot_nt(dov, vv) - delta)).astype(BF16)
            p_s[rows, :] = p.astype(BF16)
            ds_s[rows, :] = ds
            dq_o[rows, :] = _dot_nn(ds, kv)
        dk_o[...] += _dot_tn(ds_s[...], q_ref[...])
        dv_o[...] += _dot_tn(p_s[...], do_ref[...])
        _comm_finish(comm, cin, cout, csem, grid)

    qspec = pl.BlockSpec((tq, HP), lambda b, hk, g, i: (b * nq + i, hk * grp + g))
    kspec = pl.BlockSpec((s_len, HP), lambda b, hk, g, i: (b, hk))
    return pl.pallas_call(
        body, name=name, grid=grid,
        out_shape=(jax.ShapeDtypeStruct((t, n_h * HP), F32), jax.ShapeDtypeStruct((t, n_hk * HP), F32),
                   jax.ShapeDtypeStruct((t, n_hk * HP), F32), *c_outs),
        in_specs=[qspec, kspec, kspec, qspec, qspec, qspec, *c_in_specs],
        out_specs=(qspec, kspec, kspec, *([_ANY] * len(c_outs))),
        scratch_shapes=[pltpu.VMEM((tq, s_len), BF16), pltpu.VMEM((tq, s_len), BF16), *c_sems],
        input_output_aliases=alias,
        compiler_params=_params(("arbitrary", "arbitrary", "arbitrary", "arbitrary")),
    )(q, k, v, do, o, lse, *c_ins)


def prep_bwd(dqa, dka, dva, dqb, dkb, dvb, zp, tabs, g_qa, g_kva, g_qn, g_kn, w_qb_t, w_kvb_t, tm, s_len):
    t = zp.shape[0]
    nsb = s_len // tm
    scale_a = (QK_NOPE + QK_ROPE) ** -0.5
    scale_b = HD_B ** -0.5

    def body(dqa_ref, dka_ref, dva_ref, dqb_ref, dkb_ref, dvb_ref, qb_ref, qlat_ref, kb_ref, ckv_ref, tab_ref,
             gqa_ref, gkva_ref, gqn_ref, gkn_ref, wqb_ref, wkvb_ref, dzq_o, dsm_o, dqap_o, dkva_o, st_o):
        @pl.when(pl.program_id(0) == 0)
        def _():
            st_o[...] = jnp.zeros_like(st_o)

        ca, s1a, s2a = tab_ref[0], tab_ref[1], tab_ref[2]
        ck = tab_ref[3]
        cb, s1b, s2b = tab_ref[4], tab_ref[5], tab_ref[6]
        for h in range(H_A):
            sl = slice(h * HP, (h + 1) * HP)
            dqap_o[:, sl] = _rope_bwd(dqa_ref[:, sl] * scale_a, ca, s1a, s2a).astype(BF16)
        dcq = _dot_nn(dqap_o[...], wqb_ref[...])
        ql = qlat_ref[...]
        rq = _rstd(ql, Q_LORA)
        xh = ql * rq
        gqa = gqa_ref[...]
        st_o[0:1, :] += _colsum(dcq * xh)
        dsm_o[:, 0:256] = _rms_bwd(dcq, xh, rq, gqa, Q_LORA).astype(BF16)
        dkpe = jnp.zeros((tm, HP), F32)
        for h in range(H_A):
            sl = slice(h * HP, (h + 1) * HP)
            dk = dka_ref[:, sl]
            dkpe = dkpe + dk
            dkva_o[:, sl] = dk.astype(BF16)
        dkva_o[:, H_A * HP:] = dva_ref[...].astype(BF16)
        dsm_o[:, 896:1024] = _rope_bwd(dkpe, ck, s1a, s2a).astype(BF16)
        dckv = _dot_nn(dkva_o[...], wkvb_ref[...])
        cr = ckv_ref[...]
        rk = _rstd(cr, KV_LORA)
        xh = cr * rk
        st_o[1:2, 0:128] += _colsum(dckv * xh)
        dsm_o[:, 768:896] = _rms_bwd(dckv, xh, rk, gkva_ref[...], KV_LORA).astype(BF16)
        gqn, gkn = gqn_ref[...], gkn_ref[...]
        dgq = jnp.zeros((1, HP), F32)
        for h in range(H_B):
            sl = slice(h * HP, (h + 1) * HP)
            dy = _rope_bwd(dqb_ref[:, sl] * scale_b, cb, s1b, s2b)
            xs = qb_ref[:, sl]
            r = _rstd(xs, HD_B)
            xh = xs * r
            dgq = dgq + _colsum(dy * xh)
            dzq_o[:, sl] = _rms_bwd(dy, xh, r, gqn, HD_B).astype(BF16)
        st_o[2:3, 0:128] += dgq
        dgk = jnp.zeros((1, HP), F32)
        for h in range(KV_B):
            sl = slice(h * HP, (h + 1) * HP)
            dy = _rope_bwd(dkb_ref[:, sl], cb, s1b, s2b)
            xs = kb_ref[:, sl]
            r = _rstd(xs, HD_B)
            xh = xs * r
            dgk = dgk + _colsum(dy * xh)
            dsm_o[:, 256 + h * HP:256 + (h + 1) * HP] = _rms_bwd(dy, xh, r, gkn, HD_B).astype(BF16)
        st_o[3:4, 0:128] += dgk
        dsm_o[:, 512:768] = dvb_ref[...].astype(BF16)

    bf = jax.ShapeDtypeStruct((t, 1024), BF16)
    return pl.pallas_call(
        body, name="prep_bwd", grid=(t // tm,),
        out_shape=(bf, bf, bf, jax.ShapeDtypeStruct((t, 2048), BF16), jax.ShapeDtypeStruct((4, 256), F32)),
        in_specs=[_rows(tm, 1024), _rows(tm, 1024), _rows(tm, 1024), _rows(tm, 1024), _rows(tm, 256), _rows(tm, 256),
                  _rows(tm, 1024, 0), _rows(tm, 256, 12), _rows(tm, 256, 13), _rows(tm, 128, 30),
                  pl.BlockSpec((7, tm, HP), lambda i: (0, i % nsb, 0)),
                  _resident((1, Q_LORA)), _resident((1, KV_LORA)), _resident((1, HP)), _resident((1, HP)),
                  _resident((H_A * HP, Q_LORA)), _resident((2 * H_A * HP, KV_LORA))],
        out_specs=(_rows(tm, 1024), _rows(tm, 1024), _rows(tm, 1024), _rows(tm, 2048), pl.BlockSpec((4, 256), lambda i: (0, 0))),
        compiler_params=_params(("arbitrary",)),
    )(dqa, dka, dva, dqb, dkb, dvb, zp, zp, zp, zp, tabs, g_qa, g_kva, g_qn, g_kn, w_qb_t, w_kvb_t)


def in_bwd(dzq, dgab, dsm, x, dx1, g_mix, w_in_t, tm, comm=None):
    t = x.shape[0]
    grid = (t // tm,)
    c_ins, c_in_specs, c_outs, c_sems, alias = _comm_parts(comm, 7, 2)

    def body(*refs):
        ((dzq_ref, dg_ref, dsm_ref, x_ref, dx1_ref, g_ref, w_ref), cin, (dx_o, st_o), cout, _,
         csem) = _split_refs(refs, 7, 2, 0, comm)
        _comm_start(comm, cin, cout, csem, grid)

        @pl.when(pl.program_id(0) == 0)
        def _():
            st_o[...] = jnp.zeros_like(st_o)

        dh = _dot_nn(dzq_ref[...], w_ref[0:1024, :])
        dh = dh + _dot_nn(dg_ref[...], w_ref[1024:3072, :])
        dh = dh + _dot_nn(dsm_ref[...], w_ref[3072:4096, :])
        xv = x_ref[...]
        g = g_ref[...]
        r = _rstd(xv, D_MODEL)
        xh = xv * r
        st_o[...] += _colsum(dh * xh)
        dx_o[...] = dx1_ref[...] + _rms_bwd(dh, xh, r, g, D_MODEL)
        _comm_finish(comm, cin, cout, csem, grid)

    return pl.pallas_call(
        body, name="in_bwd", grid=grid,
        out_shape=(jax.ShapeDtypeStruct((t, D_MODEL), F32), jax.ShapeDtypeStruct((1, D_MODEL), F32), *c_outs),
        in_specs=[_rows(tm, 1024), _rows(tm, 2048), _rows(tm, 1024), _rows(tm, D_MODEL), _rows(tm, D_MODEL),
                  _resident((1, D_MODEL)), _resident((ZP, D_MODEL)), *c_in_specs],
        out_specs=(_rows(tm, D_MODEL), pl.BlockSpec((1, D_MODEL), lambda i: (0, 0)), *([_ANY] * len(c_outs))),
        scratch_shapes=c_sems, input_output_aliases=alias,
        compiler_params=_params(("arbitrary",)),
    )(dzq, dgab, dsm, x, dx1, g_mix, w_in_t, *c_ins)


def matmul_tn(a, b, name, square_a=False):
    t, m = a.shape
    n = b.shape[1]
    bm = min(m, 512)
    tk = min(t, 4096)

    def body(a_ref, b_ref, o_ref):
        @pl.when(pl.program_id(1) == 0)
        def _():
            o_ref[...] = jnp.zeros_like(o_ref)

        av = a_ref[...]
        if square_a:
            av = (av.astype(F32) * av.astype(F32))
        o_ref[...] += _dot_tn(av.astype(BF16), b_ref[...].astype(BF16))

    return pl.pallas_call(
        body, name=name, grid=(m // bm, t // tk), out_shape=jax.ShapeDtypeStruct((m, n), F32),
        in_specs=[pl.BlockSpec((tk, bm), lambda i, kk: (kk, i)), pl.BlockSpec((tk, n), lambda i, kk: (kk, 0))],
        out_specs=pl.BlockSpec((bm, n), lambda i, kk: (i, 0)),
        compiler_params=_params(("parallel", "arbitrary")),
    )(a, b)


def matmul_tn_packed(a, b, name, rows, row_off, total_rows, buf=None, square_a=False):
    t, m = a.shape
    n = b.shape[1]
    pd = max(1, 512 // rows)
    bm = pd * rows
    tk = min(t, 4096)
    nk = t // tk

    def body(a_ref, b_ref, *rest):
        o_ref, acc = rest[-2], rest[-1]

        @pl.when(pl.program_id(1) == 0)
        def _():
            acc[...] = jnp.zeros_like(acc)

        av = a_ref[...]
        if square_a:
            av = (av.astype(F32) * av.astype(F32))
        acc[...] += _dot_tn(av.astype(BF16), b_ref[...].astype(BF16))

        @pl.when(pl.program_id(1) == nk - 1)
        def _():
            o_ref[...] = acc[...].reshape(pd, rows, n).astype(o_ref.dtype)

    in_specs = [pl.BlockSpec((tk, bm), lambda i, kk: (kk, i)), pl.BlockSpec((tk, n), lambda i, kk: (kk, 0))]
    args = [a, b]
    if buf is not None:
        in_specs.append(_ANY)
        args.append(buf)
    return pl.pallas_call(
        body, name=name, grid=(m // bm, nk), out_shape=jax.ShapeDtypeStruct((N_DEV, total_rows, n), BF16),
        in_specs=in_specs, out_specs=pl.BlockSpec((pd, rows, n), lambda i, kk: (i, row_off // rows, 0)),
        scratch_shapes=[pltpu.VMEM((bm, n), F32)], input_output_aliases={2: 0} if buf is not None else {},
        compiler_params=_params(("parallel", "arbitrary")),
    )(*args)


def adamw(w, g, m, v, name, g_transposed=False):
    _, r, c = w.shape
    tr = r if (g_transposed or r <= 256) else 256
    c1 = 1.0 - ADAM_B1 ** ADAM_STEP
    c2 = 1.0 - ADAM_B2 ** ADAM_STEP

    def body(w_ref, g_ref, m_ref, v_ref, g_o, d_o, m_o, v_o):
        gv = g_ref[...].T if g_transposed else g_ref[...]
        mn = ADAM_B1 * m_ref[0] + (1.0 - ADAM_B1) * gv
        vn = ADAM_B2 * v_ref[0] + (1.0 - ADAM_B2) * (gv * gv)
        g_o[0] = gv
        m_o[0] = mn
        v_o[0] = vn
        d_o[0] = -ADAM_LR * ((mn / c1) / (jnp.sqrt(vn / c2) + ADAM_EPS) + ADAM_WD * w_ref[0])

    spec = pl.BlockSpec((1, tr, c), lambda i: (0, i, 0))
    gspec = pl.BlockSpec((c, r), lambda i: (0, 0)) if g_transposed else pl.BlockSpec((tr, c), lambda i: (i, 0))
    shp = jax.ShapeDtypeStruct((1, r, c), F32)
    return pl.pallas_call(
        body, name=name, grid=(r // tr,), out_shape=(shp,) * 4, in_specs=[spec, gspec, spec, spec], out_specs=(spec,) * 4,
        compiler_params=_params(("parallel",)),
    )(w, g, m, v)


def _rope_tables(s_len):
    def angles(pos, dim):
        inv = np.float32(ROPE_THETA) ** (-np.arange(0, dim, 2, dtype=np.float32) / np.float32(dim))
        return pos.astype(np.float32)[:, None] * inv[None, :]

    tpos = np.arange(s_len)
    a1 = angles(tpos, QK_ROPE)
    ar = angles(tpos // GRID_W, HD_B // 2)
    ac = angles(tpos % GRID_W, HD_B // 2)
    z16 = np.zeros((s_len, 16), np.float32)
    z32 = np.zeros((s_len, 32), np.float32)
    z64 = np.zeros((s_len, 64), np.float32)
    one64 = np.ones((s_len, 64), np.float32)
    c1, s1 = np.cos(a1), np.sin(a1)
    ca = np.concatenate([one64, c1, c1, z32], axis=1)
    ck = np.concatenate([z64, c1, c1, z32], axis=1)
    s1a = np.concatenate([z64, -s1, z16, z32], axis=1)
    s2a = np.concatenate([z64, z16, s1, z32], axis=1)
    cr, sr, cc, sc = np.cos(ar), np.sin(ar), np.cos(ac), np.sin(ac)
    cb = np.concatenate([cr, cr, cc, cc, z64], axis=1)
    s1b = np.concatenate([-sr, z16, -sc, z16, z64], axis=1)
    s2b = np.concatenate([z16, sr, z16, sc, z64], axis=1)
    return jnp.asarray(np.stack([ca, s1a, s2a, ck, cb, s1b, s2b]).astype(np.float32))


def _pad_heads(a, n_heads, axis):
    shp = a.shape
    a = a.reshape(shp[:axis] + (n_heads, shp[axis] // n_heads) + shp[axis + 1:])
    pad = [(0, 0)] * a.ndim
    pad[axis + 1] = (0, HP - a.shape[axis + 1])
    a = jnp.pad(a, pad)
    return a.reshape(shp[:axis] + (n_heads * HP,) + shp[axis + 1:])


def _unpad_heads(a, n_heads, width, axis):
    shp = a.shape
    a = a.reshape(shp[:axis] + (n_heads, HP) + shp[axis + 1:])
    a = lax.slice_in_dim(a, 0, width, axis=axis + 1)
    return a.reshape(shp[:axis] + (n_heads * width,) + shp[axis + 1:])


def _pack_rows(blocks, names):
    parts = []
    for name in names:
        b = blocks[name]
        padr = PACK_ROWS[name] - b.shape[-2]
        if padr:
            b = jnp.pad(b, [(0, 0)] * (b.ndim - 2) + [(0, padr), (0, 0)])
        parts.append(b)
    return jnp.concatenate(parts, axis=parts[0].ndim - 2)


def _expand_w_in(wt):
    z64 = jnp.zeros((64, D_MODEL), wt.dtype)
    z32 = jnp.zeros((32, D_MODEL), wt.dtype)
    return jnp.concatenate([
        _pad_heads(wt[416:928], H_B, 0), wt[1184:2208], wt[2208:3232], wt[0:256],
        _pad_heads(wt[928:1056], KV_B, 0), _pad_heads(wt[1056:1184], KV_B, 0), wt[256:384],
        z64, wt[384:416], z32], axis=0)


def _collapse_w_in(dq, dg, ds):
    return jnp.concatenate([
        ds[0:256], ds[768:896], ds[960:992], _unpad_heads(dq, H_B, HD_B, 0), _unpad_heads(ds[256:512], KV_B, HD_B, 0),
        _unpad_heads(ds[512:768], KV_B, HD_B, 0), dg], axis=0)


def kernel(x, p, g_mix, w_in, g_qa, w_qb, g_kva, w_kvb, g_qn, g_kn, w_oa, w_ob, w_o, g_mlp, w_up, w_down, g_ple, w_ple_gate, w_ple, g_final, loss_target, m_g_mix, m_w_in, m_g_qa, m_w_qb, m_g_kva, m_w_kvb, m_g_qn, m_g_kn, m_w_oa, m_w_ob, m_w_o, m_g_mlp, m_w_up, m_w_down, m_g_ple, m_w_ple_gate, m_w_ple, m_g_final, v_g_mix, v_w_in, v_g_qa, v_w_qb, v_g_kva, v_w_kvb, v_g_qn, v_g_kn, v_w_oa, v_w_ob, v_w_o, v_g_mlp, v_w_up, v_w_down, v_g_ple, v_w_ple_gate, v_w_ple, v_g_final):
    n_b, s_len, _ = x.shape
    t = n_b * s_len
    tm = min(512, s_len)
    tq_f = min(1024, s_len)
    tq_b = min(1024, s_len)

    mats = dict(w_in=(w_in, m_w_in, v_w_in), w_qb=(w_qb, m_w_qb, v_w_qb), w_kvb=(w_kvb, m_w_kvb, v_w_kvb),
                w_oa=(w_oa, m_w_oa, v_w_oa), w_ob=(w_ob, m_w_ob, v_w_ob), w_o=(w_o, m_w_o, v_w_o),
                w_up=(w_up, m_w_up, v_w_up), w_down=(w_down, m_w_down, v_w_down),
                w_ple_gate=(w_ple_gate, m_w_ple_gate, v_w_ple_gate), w_ple=(w_ple, m_w_ple, v_w_ple))
    col_sharded = ("w_in", "w_qb", "w_kvb", "w_oa", "w_ob", "w_up", "w_ple")

    blocks = {}
    for name, (w, _, _) in mats.items():
        w2 = w[0]
        if name in col_sharded:
            w2 = w2.T
        blocks[name] = w2.reshape(-1, D_MODEL).astype(BF16)
    off_w1, _ = _pack_offsets(PACK_W1)
    off_w2, _ = _pack_offsets(PACK_W2)
    full1 = allgather_rows(_pack_rows(blocks, PACK_W1))
    pack2 = _pack_rows(blocks, PACK_W2)

    def gathered(full, offs, name, rows, width):
        return full[:, offs[name]:offs[name] + rows].reshape(-1, width)

    w_in_t = _expand_w_in(gathered(full1, off_w1, "w_in", 404, D_MODEL))
    w_qb_t = _pad_heads(gathered(full1, off_w1, "w_qb", 24, Q_LORA), H_A, 0)
    wkvb = gathered(full1, off_w1, "w_kvb", 16, KV_LORA).reshape(H_A, 2, 64, KV_LORA)
    w_kvb_t = jnp.concatenate([_pad_heads(wkvb[:, 0].reshape(-1, KV_LORA), H_A, 0),
                               _pad_heads(wkvb[:, 1].reshape(-1, KV_LORA), H_A, 0)], axis=0)

    tabs = _rope_tables(s_len)
    g_qn_p = jnp.pad(g_qn, ((0, 0), (0, HP - HD_B)))
    g_kn_p = jnp.pad(g_kn, ((0, 0), (0, HP - HD_B)))
    xf = x.reshape(t, D_MODEL)
    pf = p.reshape(t, PLE_DIM)
    tgt = loss_target.reshape(t, D_MODEL)

    zp, h = in_proj(xf, g_mix, w_in_t, tm)
    qa, ka, va, qb, kb, vb, cq, ckv = attn_prep(zp, tabs, g_qa, g_kva, g_qn_p, g_kn_p, w_qb_t, w_kvb_t, tm, s_len)
    oa, lse_a, full2 = attn_fwd(qa, ka, va, n_b, s_len, tq_f, "attn_a_fwd", comm=gather_first_comm(pack2))
    ob, lse_b, full2 = attn_fwd(qb, kb, vb, n_b, s_len, tq_f, "attn_b_fwd", comm=gather_pass_comm(full2))
    w_oa_t = _pad_heads(gathered(full2, off_w2, "w_oa", 64, H_A * V_DIM_A), H_A, 1)
    w_ob_t = _pad_heads(gathered(full2, off_w2, "w_ob", 64, H_B * HD_B), H_B, 1)
    w_ple_t = gathered(full2, off_w2, "w_ple", 32, PLE_DIM)
    x1, merged = merge_fwd(oa, ob, zp, xf, w_oa_t, w_ob_t, full2, off_w2, tm)
    x2, u = mlp_fwd(x1, g_mlp, full2, off_w2, tm)
    dx2, dt, h3, dpe, st_ple, dx2b = ple_loss_bwd(x2, pf, tgt, g_ple, g_final.reshape(1, D_MODEL), full2, off_w2, w_ple_t, tm)
    dx1, da, h2, st_mlp, dx1b = mlp_bwd(dx2, x1, u, g_mlp, full2, off_w2, tm)

    core = lax.axis_index("c").astype(jnp.int32).reshape(1)
    chip = (2 * lax.axis_index("x") + lax.axis_index("y")).astype(jnp.int32).reshape(1)

    def packed(gblocks, names):
        return _pack_rows({n: gblocks[n].reshape(N_DEV, -1, D_MODEL).astype(BF16) for n in names}, names)

    off_g1, rows_g1 = _pack_offsets(PACK_G1)
    gpack1 = matmul_tn_packed(da, h2, "gw_up", 512, off_g1["w_up"], rows_g1)
    gpack1 = matmul_tn_packed(u, dx2b, "gw_down", 512, off_g1["w_down"], rows_g1, buf=gpack1, square_a=True)
    gpack1 = matmul_tn_packed(h3, dt, "gw_pg", 128, off_g1["w_ple_gate"], rows_g1, buf=gpack1)
    gple = matmul_tn(dpe, pf, "gw_ple").reshape(N_DEV, -1, D_MODEL).astype(BF16)
    gpack1 = lax.dynamic_update_slice(gpack1, gple, (0, off_g1["w_ple"], 0))
    doa, dob, dgab, dya, dyb, got1 = merge_bwd(dx1, oa, ob, zp, w_oa_t, w_ob_t, full2, off_w2, tm,
                                               comm=scatter_sibling_comm(gpack1))
    part1 = add_pairs(gpack1, got1, core)
    dqa, dka, dva, land1 = attn_bwd(qa, ka, va, doa, oa, lse_a, n_b, s_len, tq_b, "attn_a_bwd", comm=scatter_chips_comm(part1))
    gshard1 = sum_chips(part1, land1, chip)

    off_g2, rows_g2 = _pack_offsets(PACK_G2)
    g2 = dict(w_oa=_unpad_heads(matmul_tn(dya, oa, "gw_oa"), H_A, V_DIM_A, 1),
              w_ob=_unpad_heads(matmul_tn(dyb, ob, "gw_ob"), H_B, HD_B, 1))
    gpack2 = matmul_tn_packed(merged, dx1b, "gw_o", 128, off_g2["w_o"], rows_g2)
    gpack2 = lax.dynamic_update_slice(gpack2, packed(g2, ("w_oa", "w_ob")), (0, off_g2["w_oa"], 0))
    part2 = add_pairs(gpack2, exchange_sibling(gpack2), core)
    dqb, dkb, dvb, land2 = attn_bwd(qb, kb, vb, dob, ob, lse_b, n_b, s_len, tq_b, "attn_b_bwd", comm=scatter_chips_comm(part2))
    gshard2 = sum_chips(part2, land2, chip)
    dzq, dsm, dqap, dkva, st_prep = prep_bwd(dqa, dka, dva, dqb, dkb, dvb, zp, tabs, g_qa, g_kva, g_qn_p, g_kn_p,
                                             w_qb_t, w_kvb_t, tm, s_len)

    gkv = matmul_tn(dkva, ckv, "gw_kvb")
    g3 = dict(
        w_in=_collapse_w_in(matmul_tn(dzq, h, "gw_in_q"), matmul_tn(dgab, h, "gw_in_g"), matmul_tn(dsm, h, "gw_in_s")),
        w_qb=_unpad_heads(matmul_tn(dqap, cq, "gw_qb"), H_A, QK_NOPE + QK_ROPE, 0),
        w_kvb=jnp.stack([_unpad_heads(gkv[:H_A * HP], H_A, 64, 0).reshape(H_A, 64, KV_LORA),
                         _unpad_heads(gkv[H_A * HP:], H_A, 64, 0).reshape(H_A, 64, KV_LORA)], axis=1))
    gpack3 = packed(g3, PACK_G3)
    part3 = add_pairs(gpack3, exchange_sibling(gpack3), core)
    grad_x, st_mix, land3 = in_bwd(dzq, dgab, dsm, xf, dx1, g_mix, w_in_t, tm, comm=scatter_chips_comm(part3))
    gshard3 = sum_chips(part3, land3, chip)
    off_g3, _ = _pack_offsets(PACK_G3)
    shards = {n: (gshard1, off_g1[n]) for n in PACK_G1}
    shards.update({n: (gshard2, off_g2[n]) for n in PACK_G2})
    shards.update({n: (gshard3, off_g3[n]) for n in PACK_G3})

    stats = allreduce_stats(st_mix, st_prep, st_mlp, st_ple)
    loss = jnp.sum(stats[ST_LOSS])

    out_g, out_d, out_m, out_v = {}, {}, {}, {}
    for name, (w, m, v) in mats.items():
        gshard, off = shards[name]
        r, c = w.shape[1:]
        if name in col_sharded:
            g2 = gshard[off:off + (r * c) // D_MODEL].reshape(c, r)
            in_kernel = r % 128 == 0 and c % 128 == 0
            res = adamw(w, g2 if in_kernel else g2.T, m, v, "adamw_" + name, g_transposed=in_kernel)
        else:
            res = adamw(w, gshard[off:off + r], m, v, "adamw_" + name)
        out_g[name], out_d[name], out_m[name], out_v[name] = res

    gains = (("g_mix", g_mix, m_g_mix, v_g_mix, ST_G_MIX), ("g_qa", g_qa, m_g_qa, v_g_qa, ST_G_QA),
             ("g_kva", g_kva, m_g_kva, v_g_kva, ST_G_KVA), ("g_qn", g_qn, m_g_qn, v_g_qn, ST_G_QN),
             ("g_kn", g_kn, m_g_kn, v_g_kn, ST_G_KN), ("g_mlp", g_mlp, m_g_mlp, v_g_mlp, ST_G_MLP),
             ("g_ple", g_ple, m_g_ple, v_g_ple, ST_G_PLE), ("g_final", g_final, m_g_final, v_g_final, ST_G_FINAL))
    res = adamw_gains(stats, [(r_, w.reshape(1, -1), m.reshape(1, -1), v.reshape(1, -1)) for _, w, m, v, r_ in gains])
    for (name, w, _, _, _), (gg, gd, gm, gv) in zip(gains, res):
        out_g[name], out_d[name], out_m[name], out_v[name] = (a.reshape(w.shape) for a in (gg, gd, gm, gv))

    order = ("g_mix", "w_in", "g_qa", "w_qb", "g_kva", "w_kvb", "g_qn", "g_kn", "w_oa", "w_ob", "w_o", "g_mlp",
             "w_up", "w_down", "g_ple", "w_ple_gate", "w_ple", "g_final")
    return (loss, grad_x.reshape(x.shape), *[out_g[n] for n in order], *[out_d[n] for n in order],
            *[out_m[n] for n in order], *[out_v[n] for n in order])
```

```python
import numpy as np
import jax
import jax.numpy as jnp
from jax import lax
from jax.experimental import pallas as pl
from jax.experimental.pallas import tpu as pltpu

F32 = jnp.float32
BF16 = jnp.bfloat16

D_MODEL = 1024
EPS = 1e-6
ROPE_THETA = 10000.0
GRID_W = 64
H_A = 8
QK_NOPE = 64
QK_ROPE = 32
V_DIM_A = 64
Q_LORA = 256
KV_LORA = 128
H_B = 8
KV_B = 2
HD_B = 64
D_FF = 4 * D_MODEL
PLE_DIM = 256
HP = 128
ZP = 4096
N_DEV = 8
N_CHIP = 4

ADAM_LR = 0.001
ADAM_B1 = 0.9
ADAM_B2 = 0.999
ADAM_EPS = 1e-08
ADAM_WD = 0.01
ADAM_STEP = 10

VMEM_LIMIT = 52 * 1024 * 1024

PACK_ROWS = dict(w_in=416, w_qb=32, w_kvb=16, w_oa=64, w_ob=64, w_o=128, w_up=512, w_down=512, w_ple_gate=128, w_ple=32)
PACK_W1 = ("w_in", "w_qb", "w_kvb")
PACK_W2 = ("w_up", "w_down", "w_o", "w_ple_gate", "w_oa", "w_ob", "w_ple")
PACK_G1 = ("w_up", "w_down", "w_ple_gate", "w_ple")
PACK_G2 = ("w_o", "w_oa", "w_ob")
PACK_G3 = ("w_in", "w_qb", "w_kvb")


def _pack_offsets(names):
    off, o = {}, 0
    for n in names:
        off[n] = o
        o += PACK_ROWS[n]
    return off, o

ST_G_MIX, ST_G_QA, ST_G_KVA, ST_G_QN, ST_G_KN, ST_G_MLP, ST_G_PLE, ST_G_FINAL, ST_LOSS = range(9)
ST_ROWS = 16


def _dot_nn(a, b):
    return lax.dot_general(a, b, (((1,), (0,)), ((), ())), preferred_element_type=F32)


def _dot_nt(a, b):
    return lax.dot_general(a, b, (((1,), (1,)), ((), ())), preferred_element_type=F32)


def _dot_tn(a, b):
    return lax.dot_general(a, b, (((0,), (0,)), ((), ())), preferred_element_type=F32)


def _rstd(x, n):
    return lax.rsqrt(jnp.sum(x * x, axis=-1, keepdims=True) * (1.0 / n) + EPS)


def _rms_bwd(dy, xh, r, g, n):
    dxh = dy * g
    return r * (dxh - xh * (jnp.sum(dxh * xh, axis=-1, keepdims=True) * (1.0 / n)))


def _rope_fwd(x, c, s1, s2):
    return x * c + pltpu.roll(x, HP - 16, 1) * s1 + pltpu.roll(x, 16, 1) * s2


def _rope_bwd(d, c, s1, s2):
    return d * c + pltpu.roll(d * s1, 16, 1) + pltpu.roll(d * s2, HP - 16, 1)


def _colsum(v):
    return jnp.sum(v, axis=0, keepdims=True)


def _params(sem=None, vmem=VMEM_LIMIT):
    return pltpu.CompilerParams(dimension_semantics=sem, vmem_limit_bytes=vmem)


def _resident(shape):
    nd = len(shape)
    return pl.BlockSpec(shape, lambda *_: (0,) * nd, pipeline_mode=pl.Buffered(1))


def _rows(tm, width, col=0):
    return pl.BlockSpec((tm, width), lambda i: (i, col))


def _packed_weight(rows, off):
    return pl.BlockSpec((N_DEV, rows, D_MODEL), lambda *_: (0, off // rows, 0), pipeline_mode=pl.Buffered(1))


def _wrows(ref, start, size):
    rows = ref.shape[1]
    return ref[start // rows:(start + size) // rows].reshape(size, D_MODEL)


def _mesh_pos():
    return lax.axis_index("x"), lax.axis_index("y"), lax.axis_index("c")


def _flip(v, bit):
    return (1 - v) if bit else v


_ANY = pl.BlockSpec(memory_space=pl.ANY)
_MESH = pl.DeviceIdType.MESH


def _remote(src, dst, send_sems, recv_sems, k, to):
    return pltpu.make_async_remote_copy(src_ref=src, dst_ref=dst, send_sem=send_sems.at[k], recv_sem=recv_sems.at[k],
                                        device_id=to, device_id_type=_MESH)


def _sibling_copies(g_ref, got_ref, send_sems, recv_sems):
    x, y, c = _mesh_pos()
    return [_remote(g_ref.at[2 * j + (1 - c)], got_ref.at[j], send_sems, recv_sems, j, (x, y, 1 - c)) for j in range(N_CHIP)]


def _chip_copies(p_ref, land_ref, send_sems, recv_sems):
    x, y, c = _mesh_pos()
    copies = []
    for k in (1, 2, 3):
        tx, ty = _flip(x, k & 2), _flip(y, k & 1)
        copies.append(_remote(p_ref.at[2 * tx + ty], land_ref.at[k - 1], send_sems, recv_sems, k - 1, (tx, ty, c)))
    return copies


class _Comm:
    def __init__(self, ins, out_shapes, sems, make, aliases=None):
        self.ins, self.out_shapes, self.sems, self.make, self.aliases = list(ins), list(out_shapes), list(sems), make, aliases or {}


def _comm_parts(comm, n_in, n_out):
    if comm is None:
        return [], [], [], [], {}
    alias = {n_in + j: n_out + k for j, k in comm.aliases.items()}
    return comm.ins, [_ANY] * len(comm.ins), comm.out_shapes, comm.sems, alias


def _split_refs(refs, n_in, n_out, n_scratch, comm):
    n_ci = len(comm.ins) if comm else 0
    n_co = len(comm.out_shapes) if comm else 0
    cuts, i = [], 0
    for n in (n_in, n_ci, n_out, n_co, n_scratch):
        cuts.append(refs[i:i + n])
        i += n
    return (*cuts, refs[i:])


def _grid_edge(grid, last):
    cond = None
    for d, n in enumerate(grid):
        here = pl.program_id(d) == (n - 1 if last else 0)
        cond = here if cond is None else cond & here
    return cond


def _comm_start(comm, cin, cout, csem, grid):
    if comm is not None:
        @pl.when(_grid_edge(grid, False))
        def _():
            for cp in comm.make(cin, cout, csem):
                cp.start()


def _comm_finish(comm, cin, cout, csem, grid):
    if comm is not None:
        @pl.when(_grid_edge(grid, True))
        def _():
            for cp in comm.make(cin, cout, csem):
                cp.wait()


def gather_first_comm(shard):
    r, w = shard.shape

    def make(cin, cout, sems):
        (x_ref,), (out_ref,), (send_sems, recv_sems, local_sem) = cin, cout, sems
        x, y, c = _mesh_pos()
        mine = out_ref.at[4 * x + 2 * y + c]
        targets = [(x, y, 1 - c), (1 - x, y, c), (x, 1 - y, c), (1 - x, 1 - y, c)]
        return [_remote(x_ref, mine, send_sems, recv_sems, k, to) for k, to in enumerate(targets)] + [
            pltpu.make_async_copy(x_ref, mine, local_sem)]

    return _Comm([shard], [jax.ShapeDtypeStruct((N_DEV, r, w), shard.dtype)],
                 [pltpu.SemaphoreType.DMA((4,)), pltpu.SemaphoreType.DMA((4,)), pltpu.SemaphoreType.DMA], make)


def gather_pass_comm(full):
    def make(cin, cout, sems):
        (in_ref,), (out_ref,), (send_sems, recv_sems) = cin, cout, sems
        x, y, c = _mesh_pos()
        copies = []
        for k, (px, py) in enumerate([(1 - x, y), (x, 1 - y), (1 - x, 1 - y)]):
            idx = 4 * px + 2 * py + c
            copies.append(_remote(in_ref.at[idx], out_ref.at[idx], send_sems, recv_sems, k, (x, y, 1 - c)))
        return copies

    return _Comm([full], [jax.ShapeDtypeStruct(full.shape, full.dtype)],
                 [pltpu.SemaphoreType.DMA((3,)), pltpu.SemaphoreType.DMA((3,))], make, aliases={0: 0})


def scatter_sibling_comm(g):
    _, r, w = g.shape
    return _Comm([g], [jax.ShapeDtypeStruct((N_CHIP, r, w), g.dtype)],
                 [pltpu.SemaphoreType.DMA((N_CHIP,)), pltpu.SemaphoreType.DMA((N_CHIP,))],
                 lambda cin, cout, sems: _sibling_copies(cin[0], cout[0], sems[0], sems[1]))


def scatter_chips_comm(part):
    _, r, w = part.shape
    return _Comm([part], [jax.ShapeDtypeStruct((N_CHIP - 1, r, w), part.dtype)],
                 [pltpu.SemaphoreType.DMA((3,)), pltpu.SemaphoreType.DMA((3,))],
                 lambda cin, cout, sems: _chip_copies(cin[0], cout[0], sems[0], sems[1]))


def exchange_sibling(g):
    _, r, w = g.shape

    def body(g_ref, got_ref, send_sems, recv_sems):
        copies = _sibling_copies(g_ref, got_ref, send_sems, recv_sems)
        for cp in copies:
            cp.start()
        for cp in copies:
            cp.wait()

    return pl.pallas_call(
        body, name="exchange_sibling", out_shape=jax.ShapeDtypeStruct((N_CHIP, r, w), g.dtype),
        in_specs=[_ANY], out_specs=_ANY,
        scratch_shapes=[pltpu.SemaphoreType.DMA((N_CHIP,)), pltpu.SemaphoreType.DMA((N_CHIP,))],
    )(g)


def exchange_chips(part):
    _, r, w = part.shape

    def body(p_ref, land_ref, send_sems, recv_sems):
        copies = _chip_copies(p_ref, land_ref, send_sems, recv_sems)
        for cp in copies:
            cp.start()
        for cp in copies:
            cp.wait()

    return pl.pallas_call(
        body, name="exchange_chips", out_shape=jax.ShapeDtypeStruct((N_CHIP - 1, r, w), part.dtype),
        in_specs=[_ANY], out_specs=_ANY,
        scratch_shapes=[pltpu.SemaphoreType.DMA((3,)), pltpu.SemaphoreType.DMA((3,))],
    )(part)


def allreduce_stats(st_mix, st_prep, st_mlp, st_ple):
    def body(mix_ref, prep_ref, mlp_ref, ple_ref, out_ref, mine, gath, send_sems, recv_sems):
        x, y, c = _mesh_pos()
        me = 4 * x + 2 * y + c
        mine[...] = jnp.zeros_like(mine)
        mine[ST_G_MIX:ST_G_MIX + 1, :] = mix_ref[...]
        mine[ST_G_QA:ST_G_KN + 1, 0:256] = prep_ref[...]
        mine[ST_G_MLP:ST_G_MLP + 1, :] = mlp_ref[...]
        mine[ST_G_PLE:ST_LOSS + 1, :] = ple_ref[...]
        gath[me] = mine[...]
        copies = []
        for k in range(1, N_DEV):
            peer = (_flip(x, k & 4), _flip(y, k & 2), _flip(c, k & 1))
            copies.append(_remote(mine, gath.at[me], send_sems, recv_sems, k - 1, peer))
        for cp in copies:
            cp.start()
        for cp in copies:
            cp.wait()
        acc = gath[0]
        for d in range(1, N_DEV):
            acc = acc + gath[d]
        out_ref[...] = acc

    vm = pl.BlockSpec(memory_space=pltpu.VMEM)
    return pl.pallas_call(
        body, name="allreduce_stats", out_shape=jax.ShapeDtypeStruct((ST_ROWS, D_MODEL), F32),
        in_specs=[vm] * 4, out_specs=vm,
        scratch_shapes=[pltpu.VMEM((ST_ROWS, D_MODEL), F32), pltpu.VMEM((N_DEV, ST_ROWS, D_MODEL), F32),
                        pltpu.SemaphoreType.DMA((N_DEV - 1,)), pltpu.SemaphoreType.DMA((N_DEV - 1,))],
    )(st_mix, st_prep, st_mlp, st_ple)


def adamw_gains(stats, gains):
    c1 = 1.0 - ADAM_B1 ** ADAM_STEP
    c2 = 1.0 - ADAM_B2 ** ADAM_STEP
    n = len(gains)

    def body(st_ref, *refs):
        ins, outs = refs[:3 * n], refs[3 * n:]
        for i, (row, w, _, _) in enumerate(gains):
            width = w.shape[1]
            gv = st_ref[row:row + 1, 0:width]
            mn = ADAM_B1 * ins[3 * i + 1][...] + (1.0 - ADAM_B1) * gv
            vn = ADAM_B2 * ins[3 * i + 2][...] + (1.0 - ADAM_B2) * (gv * gv)
            outs[4 * i][...] = gv
            outs[4 * i + 1][...] = -ADAM_LR * ((mn / c1) / (jnp.sqrt(vn / c2) + ADAM_EPS) + ADAM_WD * ins[3 * i][...])
            outs[4 * i + 2][...] = mn
            outs[4 * i + 3][...] = vn

    vm = pl.BlockSpec(memory_space=pltpu.VMEM)
    flat = [a for (_, w, m, v) in gains for a in (w, m, v)]
    out_shape = tuple(jax.ShapeDtypeStruct(w.shape, F32) for (_, w, _, _) in gains for _ in range(4))
    res = pl.pallas_call(body, name="adamw_gains", out_shape=out_shape, in_specs=[vm] * (1 + 3 * n),
                         out_specs=tuple([vm] * (4 * n)))(stats, *flat)
    return [res[4 * i:4 * i + 4] for i in range(n)]


def _row_tile(r, cap=640):
    return max(d for d in range(16, min(r, cap) + 1, 16) if r % d == 0)


def add_pairs(g, got, core):
    n, r, w = got.shape
    tr = _row_tile(r)

    def body(c_ref, a_ref, b_ref, o_ref):
        o_ref[...] = (a_ref[...].astype(F32) + b_ref[...].astype(F32)).astype(o_ref.dtype)

    spec = pl.BlockSpec((1, tr, w), lambda i, j, c: (i, j, 0))
    return pl.pallas_call(
        body, name="add_pairs", out_shape=jax.ShapeDtypeStruct(got.shape, got.dtype),
        grid_spec=pltpu.PrefetchScalarGridSpec(
            num_scalar_prefetch=1, grid=(n, r // tr),
            in_specs=[pl.BlockSpec((1, tr, w), lambda i, j, c: (2 * i + c[0], j, 0)), spec], out_specs=spec),
        compiler_params=_params(("parallel", "parallel")),
    )(core, g, got)


def sum_chips(part, land, chip):
    _, r, w = part.shape
    tr = _row_tile(r)

    def body(c_ref, p_ref, l_ref, o_ref):
        acc = p_ref[0].astype(F32)
        for s in range(N_CHIP - 1):
            acc = acc + l_ref[s].astype(F32)
        o_ref[...] = acc

    return pl.pallas_call(
        body, name="sum_chips", out_shape=jax.ShapeDtypeStruct((r, w), F32),
        grid_spec=pltpu.PrefetchScalarGridSpec(
            num_scalar_prefetch=1, grid=(r // tr,),
            in_specs=[pl.BlockSpec((1, tr, w), lambda i, c: (c[0], i, 0)), pl.BlockSpec((N_CHIP - 1, tr, w), lambda i, c: (0, i, 0))],
            out_specs=pl.BlockSpec((tr, w), lambda i, c: (i, 0))),
        compiler_params=_params(("parallel",)),
    )(chip, part, land)


def norm_x(x, g_mix, tm, comm=None):
    t = x.shape[0]
    grid = (t // tm,)
    c_ins, c_in_specs, c_outs, c_sems, alias = _comm_parts(comm, 2, 1)

    def body(*refs):
        (x_ref, g_ref), cin, (h_ref,), cout, _, csem = _split_refs(refs, 2, 1, 0, comm)
        _comm_start(comm, cin, cout, csem, grid)
        xv = x_ref[...]
        h_ref[...] = (xv * _rstd(xv, D_MODEL) * g_ref[...]).astype(BF16)
        _comm_finish(comm, cin, cout, csem, grid)

    return pl.pallas_call(
        body, name="norm_x", grid=grid, out_shape=(jax.ShapeDtypeStruct((t, D_MODEL), BF16), *c_outs),
        in_specs=[_rows(tm, D_MODEL), _resident((1, D_MODEL)), *c_in_specs],
        out_specs=(_rows(tm, D_MODEL), *([_ANY] * len(c_outs))),
        scratch_shapes=c_sems, input_output_aliases=alias, compiler_params=_params(("arbitrary",)),
    )(x, g_mix, *c_ins)


def pack_late_weights(w_up, w_down, w_o, w_pg, small, comm=None):
    rows = sum(PACK_ROWS[n] for n in PACK_W2)
    c_ins, c_in_specs, c_outs, c_sems, alias = _comm_parts(comm, 5, 1)
    grid = (1,)

    def body(*refs):
        (up_ref, dn_ref, o_ref, pg_ref, sm_ref), cin, (out_ref,), cout, _, csem = _split_refs(refs, 5, 1, 0, comm)
        _comm_start(comm, cin, cout, csem, grid)
        out_ref[0:512, :] = up_ref[0].T.astype(BF16)
        out_ref[512:1024, :] = dn_ref[0].astype(BF16)
        out_ref[1024:1152, :] = o_ref[0].astype(BF16)
        out_ref[1152:1280, :] = pg_ref[0].astype(BF16)
        out_ref[1280:rows, :] = sm_ref[...]
        _comm_finish(comm, cin, cout, csem, grid)

    def whole(a):
        nd = a.ndim
        return pl.BlockSpec(a.shape, lambda i: (0,) * nd)

    args = (w_up, w_down, w_o, w_pg, small)
    return pl.pallas_call(
        body, name="pack_late_weights", grid=grid, out_shape=(jax.ShapeDtypeStruct((rows, D_MODEL), BF16), *c_outs),
        in_specs=[*[whole(a) for a in args], *c_in_specs],
        out_specs=(pl.BlockSpec((rows, D_MODEL), lambda i: (0, 0)), *([_ANY] * len(c_outs))),
        scratch_shapes=c_sems, input_output_aliases=alias, compiler_params=_params(("arbitrary",)),
    )(*args, *c_ins)


def in_proj(h, w_in_t, tm):
    t = h.shape[0]
    nc = 512

    def body(h_ref, w_ref, z_ref):
        hv = h_ref[...]
        for cidx in range(ZP // nc):
            z_ref[:, cidx * nc:(cidx + 1) * nc] = _dot_nt(hv, w_ref[cidx * nc:(cidx + 1) * nc, :])

    return pl.pallas_call(
        body, name="in_proj", grid=(t // tm,), out_shape=jax.ShapeDtypeStruct((t, ZP), F32),
        in_specs=[_rows(tm, D_MODEL), _resident((ZP, D_MODEL))],
        out_specs=_rows(tm, ZP), compiler_params=_params(("parallel",)),
    )(h, w_in_t)


def attn_prep(zp, tabs, g_qa, g_kva, g_qn, g_kn, w_qb_t, w_kvb_t, tm, s_len):
    t = zp.shape[0]
    nsb = s_len // tm
    scale_a = (QK_NOPE + QK_ROPE) ** -0.5
    scale_b = HD_B ** -0.5

    def body(qb_ref, qlat_ref, kb_ref, vb_ref, ckv_ref, kpe_ref, tab_ref, gqa_ref, gkva_ref, gqn_ref, gkn_ref,
             wqb_ref, wkvb_ref, qa_o, ka_o, va_o, qb_o, kb_o, vb_o, cq_o, ckvn_o):
        ca, s1a, s2a = tab_ref[0], tab_ref[1], tab_ref[2]
        ck = tab_ref[3]
        cb, s1b, s2b = tab_ref[4], tab_ref[5], tab_ref[6]
        ql = qlat_ref[...]
        cq = (ql * _rstd(ql, Q_LORA) * gqa_ref[...]).astype(BF16)
        cq_o[...] = cq
        qa = _dot_nt(cq, wqb_ref[...])
        slabs = [slice(h * HP, (h + 1) * HP) for h in range(H_A)]
        qa_o[...] = jnp.concatenate(
            [(_rope_fwd(qa[:, sl], ca, s1a, s2a) * scale_a).astype(BF16) for sl in slabs], axis=1)
        cr = ckv_ref[...]
        ckv = (cr * _rstd(cr, KV_LORA) * gkva_ref[...]).astype(BF16)
        ckvn_o[...] = ckv
        kva = _dot_nt(ckv, wkvb_ref[...])
        kpe = _rope_fwd(kpe_ref[...], ck, s1a, s2a)
        ka_o[...] = jnp.concatenate([(kva[:, sl] + kpe).astype(BF16) for sl in slabs], axis=1)
        va_o[...] = kva[:, H_A * HP:].astype(BF16)
        gqn, gkn = gqn_ref[...], gkn_ref[...]

        def norm_rope(ref, sl, g, scale):
            xs = ref[:, sl]
            y = _rope_fwd(xs * _rstd(xs, HD_B) * g, cb, s1b, s2b)
            return (y if scale is None else y * scale).astype(BF16)

        qb_o[...] = jnp.concatenate([norm_rope(qb_ref, sl, gqn, scale_b) for sl in slabs], axis=1)
        kb_o[...] = jnp.concatenate([norm_rope(kb_ref, sl, gkn, None) for sl in slabs[:KV_B]], axis=1)
        vb_o[...] = vb_ref[...].astype(BF16)

    def o(width):
        return jax.ShapeDtypeStruct((t, width), BF16)

    return pl.pallas_call(
        body, name="attn_prep", grid=(t // tm,),
        out_shape=(o(H_A * HP), o(H_A * HP), o(H_A * HP), o(H_B * HP), o(KV_B * HP), o(KV_B * HP), o(Q_LORA), o(KV_LORA)),
        in_specs=[_rows(tm, 1024, 0), _rows(tm, 256, 12), _rows(tm, 256, 13), _rows(tm, 256, 14),
                  _rows(tm, 128, 30), _rows(tm, 128, 31),
                  pl.BlockSpec((7, tm, HP), lambda i: (0, i % nsb, 0)),
                  _resident((1, Q_LORA)), _resident((1, KV_LORA)), _resident((1, HP)), _resident((1, HP)),
                  _resident((H_A * HP, Q_LORA)), _resident((2 * H_A * HP, KV_LORA))],
        out_specs=(_rows(tm, H_A * HP), _rows(tm, H_A * HP), _rows(tm, H_A * HP), _rows(tm, H_B * HP),
                   _rows(tm, KV_B * HP), _rows(tm, KV_B * HP), _rows(tm, Q_LORA), _rows(tm, KV_LORA)),
        compiler_params=_params(("parallel",)),
    )(zp, zp, zp, zp, zp, zp, tabs, g_qa, g_kva, g_qn, g_kn, w_qb_t, w_kvb_t)


def attn_fwd(q, k, v, n_b, s_len, tq, name, comm=None):
    t = q.shape[0]
    n_h, n_hk = q.shape[1] // HP, k.shape[1] // HP
    grp = n_h // n_hk
    nq = s_len // tq
    sub = min(tq, 256)
    grid = (n_b, n_h, nq)
    c_ins, c_in_specs, c_outs, c_sems, alias = _comm_parts(comm, 3, 2)

    def body(*refs):
        (q_ref, k_ref, v_ref), cin, (o_ref, lse_ref), cout, _, csem = _split_refs(refs, 3, 2, 0, comm)
        _comm_start(comm, cin, cout, csem, grid)
        kv, vv = k_ref[...], v_ref[...]
        for r in range(tq // sub):
            rows = slice(r * sub, (r + 1) * sub)
            s = _dot_nt(q_ref[rows, :], kv)
            m = jnp.max(s, axis=-1, keepdims=True)
            p = jnp.exp(s - m)
            l = jnp.sum(p, axis=-1, keepdims=True)
            o_ref[rows, :] = (_dot_nn(p.astype(BF16), vv) * (1.0 / l)).astype(o_ref.dtype)
            lse_ref[rows, :] = jnp.broadcast_to(m + jnp.log(l), (sub, HP))
        _comm_finish(comm, cin, cout, csem, grid)

    qspec = pl.BlockSpec((tq, HP), lambda b, h, i: (b * nq + i, h))
    kspec = pl.BlockSpec((s_len, HP), lambda b, h, i: (b, h // grp))
    return pl.pallas_call(
        body, name=name, grid=grid,
        out_shape=(jax.ShapeDtypeStruct((t, n_h * HP), BF16), jax.ShapeDtypeStruct((t, n_h * HP), F32), *c_outs),
        in_specs=[qspec, kspec, kspec, *c_in_specs], out_specs=(qspec, qspec, *([_ANY] * len(c_outs))),
        scratch_shapes=c_sems, input_output_aliases=alias,
        compiler_params=_params(("arbitrary", "arbitrary", "arbitrary")),
    )(q, k, v, *c_ins)


def merge_fwd(oa, ob, zp, x, w_oa_t, w_ob_t, wpack, off, tm):
    t = x.shape[0]

    def body(oa_ref, ob_ref, ga_ref, gb_ref, x_ref, woa_ref, wob_ref, wo_ref, x1_o, mg_o):
        ya = _dot_nt(oa_ref[...], woa_ref[...])
        yb = _dot_nt(ob_ref[...], wob_ref[...])
        merged = (jax.nn.sigmoid(ga_ref[...]) * ya + jax.nn.sigmoid(gb_ref[...]) * yb).astype(BF16)
        mg_o[...] = merged
        x1_o[...] = x_ref[...] + _dot_nn(merged, _wrows(wo_ref, 0, D_MODEL))

    return pl.pallas_call(
        body, name="merge_fwd", grid=(t // tm,),
        out_shape=(jax.ShapeDtypeStruct((t, D_MODEL), F32), jax.ShapeDtypeStruct((t, D_MODEL), BF16)),
        in_specs=[_rows(tm, H_A * HP), _rows(tm, H_B * HP), _rows(tm, 1024, 1), _rows(tm, 1024, 2), _rows(tm, D_MODEL),
                  _resident((D_MODEL, H_A * HP)), _resident((D_MODEL, H_B * HP)), _packed_weight(128, off["w_o"])],
        out_specs=(_rows(tm, D_MODEL), _rows(tm, D_MODEL)), compiler_params=_params(("parallel",)),
    )(oa, ob, zp, zp, x, w_oa_t, w_ob_t, wpack)


def mlp_fwd(x1, g_mlp, wpack, off, tm):
    t = x1.shape[0]
    fc = 1024

    def body(x_ref, g_ref, wup_ref, wdn_ref, x2_o, u_o):
        xv = x_ref[...]
        h2 = (xv * _rstd(xv, D_MODEL) * g_ref[...]).astype(BF16)
        acc = xv
        for cidx in range(D_FF // fc):
            sl = slice(cidx * fc, (cidx + 1) * fc)
            u = jnp.maximum(_dot_nt(h2, _wrows(wup_ref, cidx * fc, fc)), 0.0)
            u_o[:, sl] = u.astype(BF16)
            acc = acc + _dot_nn((u * u).astype(BF16), _wrows(wdn_ref, cidx * fc, fc))
        x2_o[...] = acc

    return pl.pallas_call(
        body, name="mlp_fwd", grid=(t // tm,),
        out_shape=(jax.ShapeDtypeStruct((t, D_MODEL), F32), jax.ShapeDtypeStruct((t, D_FF), BF16)),
        in_specs=[_rows(tm, D_MODEL), _resident((1, D_MODEL)), _packed_weight(512, off["w_up"]), _packed_weight(512, off["w_down"])],
        out_specs=(_rows(tm, D_MODEL), _rows(tm, D_FF)), compiler_params=_params(("parallel",)),
    )(x1, g_mlp, wpack, wpack)


def ple_loss_bwd(x2, p, tgt, g_ple, g_final, wpack, off, w_ple_t, tm):
    t = x2.shape[0]
    inv_d = 1.0 / D_MODEL

    def body(x2_ref, p_ref, tg_ref, gp_ref, gf_ref, wpg_ref, wple_ref, dx2_o, dt_o, h3_o, dpe_o, st_o, dx2b_o):
        @pl.when(pl.program_id(0) == 0)
        def _():
            st_o[...] = jnp.zeros_like(st_o)

        x2v = x2_ref[...]
        gp, gf = gp_ref[...], gf_ref[...]
        w_pg = _wrows(wpg_ref, 0, D_MODEL)
        r2 = _rstd(x2v, D_MODEL)
        xh2 = x2v * r2
        h3 = (xh2 * gp).astype(BF16)
        h3_o[...] = h3
        gate = jax.nn.sigmoid(_dot_nn(h3, w_pg))
        pe = _dot_nt(p_ref[...].astype(BF16), wple_ref[...])
        x3 = x2v + gate * pe
        r3 = _rstd(x3, D_MODEL)
        xh3 = x3 * r3
        err = xh3 * gf - tg_ref[...]
        dy = err * inv_d
        dx3 = _rms_bwd(dy, xh3, r3, gf, D_MODEL)
        dpe_o[...] = (dx3 * gate).astype(BF16)
        dt = (dx3 * pe * gate * (1.0 - gate)).astype(BF16)
        dt_o[...] = dt
        dh3 = _dot_nt(dt, w_pg)
        dx2 = dx3 + _rms_bwd(dh3, xh2, r2, gp, D_MODEL)
        dx2_o[...] = dx2
        dx2b_o[...] = dx2.astype(BF16)
        st_o[0:1, :] += _colsum(dh3 * xh2)
        st_o[1:2, :] += _colsum(dy * xh3)
        st_o[2:3, :] += _colsum(err * err) * (0.5 * inv_d)

    bf = jax.ShapeDtypeStruct((t, D_MODEL), BF16)
    return pl.pallas_call(
        body, name="ple_loss_bwd", grid=(t // tm,),
        out_shape=(jax.ShapeDtypeStruct((t, D_MODEL), F32), bf, bf, bf, jax.ShapeDtypeStruct((3, D_MODEL), F32), bf),
        in_specs=[_rows(tm, D_MODEL), _rows(tm, PLE_DIM), _rows(tm, D_MODEL), _resident((1, D_MODEL)), _resident((1, D_MODEL)),
                  _packed_weight(128, off["w_ple_gate"]), _resident((D_MODEL, PLE_DIM))],
        out_specs=(_rows(tm, D_MODEL), _rows(tm, D_MODEL), _rows(tm, D_MODEL), _rows(tm, D_MODEL),
                   pl.BlockSpec((3, D_MODEL), lambda i: (0, 0)), _rows(tm, D_MODEL)),
        compiler_params=_params(("arbitrary",)),
    )(x2, p, tgt, g_ple, g_final, wpack, w_ple_t)


def mlp_bwd(dx2, x1, u, g_mlp, wpack, off, tm):
    t = x1.shape[0]
    fc = 1024

    def body(dx2_ref, x1_ref, u_ref, g_ref, wup_ref, wdn_ref, dx1_o, da_o, h2_o, st_o, dx1b_o):
        @pl.when(pl.program_id(0) == 0)
        def _():
            st_o[...] = jnp.zeros_like(st_o)

        d2 = dx2_ref[...]
        d2b = d2.astype(BF16)
        dh2 = jnp.zeros((tm, D_MODEL), F32)
        for cidx in range(D_FF // fc):
            sl = slice(cidx * fc, (cidx + 1) * fc)
            da = (_dot_nt(d2b, _wrows(wdn_ref, cidx * fc, fc)) * (2.0 * u_ref[:, sl].astype(F32))).astype(BF16)
            da_o[:, sl] = da
            dh2 = dh2 + _dot_nn(da, _wrows(wup_ref, cidx * fc, fc))
        xv = x1_ref[...]
        g = g_ref[...]
        r1 = _rstd(xv, D_MODEL)
        xh1 = xv * r1
        h2_o[...] = (xh1 * g).astype(BF16)
        st_o[...] += _colsum(dh2 * xh1)
        dx1 = d2 + _rms_bwd(dh2, xh1, r1, g, D_MODEL)
        dx1_o[...] = dx1
        dx1b_o[...] = dx1.astype(BF16)

    return pl.pallas_call(
        body, name="mlp_bwd", grid=(t // tm,),
        out_shape=(jax.ShapeDtypeStruct((t, D_MODEL), F32), jax.ShapeDtypeStruct((t, D_FF), BF16),
                   jax.ShapeDtypeStruct((t, D_MODEL), BF16), jax.ShapeDtypeStruct((1, D_MODEL), F32),
                   jax.ShapeDtypeStruct((t, D_MODEL), BF16)),
        in_specs=[_rows(tm, D_MODEL), _rows(tm, D_MODEL), _rows(tm, D_FF), _resident((1, D_MODEL)),
                  _packed_weight(512, off["w_up"]), _packed_weight(512, off["w_down"])],
        out_specs=(_rows(tm, D_MODEL), _rows(tm, D_FF), _rows(tm, D_MODEL), pl.BlockSpec((1, D_MODEL), lambda i: (0, 0)),
                   _rows(tm, D_MODEL)),
        compiler_params=_params(("arbitrary",)),
    )(dx2, x1, u, g_mlp, wpack, wpack)


def merge_bwd(dx1, oa, ob, zp, w_oa_t, w_ob_t, wpack, off, tm, comm=None):
    t = dx1.shape[0]
    grid = (t // tm,)
    c_ins, c_in_specs, c_outs, c_sems, alias = _comm_parts(comm, 8, 5)

    def body(*refs):
        ((dx1_ref, oa_ref, ob_ref, ga_ref, gb_ref, woa_ref, wob_ref, wo_ref), cin,
         (doa_o, dob_o, dg_o, dya_o, dyb_o), cout, _, csem) = _split_refs(refs, 8, 5, 0, comm)
        _comm_start(comm, cin, cout, csem, grid)
        dm = _dot_nt(dx1_ref[...].astype(BF16), _wrows(wo_ref, 0, D_MODEL))
        for o_ref, g_ref, w_ref, do_o, dy_o, col in ((oa_ref, ga_ref, woa_ref, doa_o, dya_o, 0),
                                                     (ob_ref, gb_ref, wob_ref, dob_o, dyb_o, 1)):
            yv = _dot_nt(o_ref[...], w_ref[...])
            sg = jax.nn.sigmoid(g_ref[...])
            dyv = (dm * sg).astype(BF16)
            dy_o[...] = dyv
            dg_o[:, col * D_MODEL:(col + 1) * D_MODEL] = (dm * yv * sg * (1.0 - sg)).astype(BF16)
            do_o[...] = _dot_nn(dyv, w_ref[...]).astype(BF16)
        _comm_finish(comm, cin, cout, csem, grid)

    bf = jax.ShapeDtypeStruct((t, D_MODEL), BF16)
    return pl.pallas_call(
        body, name="merge_bwd", grid=grid,
        out_shape=(bf, bf, jax.ShapeDtypeStruct((t, 2 * D_MODEL), BF16), bf, bf, *c_outs),
        in_specs=[_rows(tm, D_MODEL), _rows(tm, H_A * HP), _rows(tm, H_B * HP), _rows(tm, 1024, 1), _rows(tm, 1024, 2),
                  _resident((D_MODEL, H_A * HP)), _resident((D_MODEL, H_B * HP)), _packed_weight(128, off["w_o"]), *c_in_specs],
        out_specs=(_rows(tm, D_MODEL), _rows(tm, D_MODEL), _rows(tm, 2 * D_MODEL), _rows(tm, D_MODEL), _rows(tm, D_MODEL),
                   *([_ANY] * len(c_outs))),
        scratch_shapes=c_sems, input_output_aliases=alias,
        compiler_params=_params(("arbitrary",)),
    )(dx1, oa, ob, zp, zp, w_oa_t, w_ob_t, wpack, *c_ins)


def attn_bwd(q, k, v, do, o, lse, n_b, s_len, tq, name, comm=None):
    t = q.shape[0]
    n_h, n_hk = q.shape[1] // HP, k.shape[1] // HP
    grp = n_h // n_hk
    nq = s_len // tq
    sub = min(tq, 256)
    grid = (n_b, n_hk, grp, nq)
    c_ins, c_in_specs, c_outs, c_sems, alias = _comm_parts(comm, 6, 3)

    def body(*refs):
        ((q_ref, k_ref, v_ref, do_ref, o_ref, lse_ref), cin, (dq_o, dk_o, dv_o), cout, (p_s, ds_s),
         csem) = _split_refs(refs, 6, 3, 2, comm)
        _comm_start(comm, cin, cout, csem, grid)

        @pl.when((pl.program_id(2) == 0) & (pl.program_id(3) == 0))
        def _():
            dk_o[...] = jnp.zeros_like(dk_o)
            dv_o[...] = jnp.zeros_like(dv_o)

        kv, vv = k_ref[...], v_ref[...]
        for r in range(tq // sub):
            rows = slice(r * sub, (r + 1) * sub)
            qv, dov = q_ref[rows, :], do_ref[rows, :]
            delta = jnp.sum(dov.astype(F32) * o_ref[rows, :].astype(F32), axis=-1, keepdims=True)
            p = jnp.exp(_dot_nt(qv, kv) - lse_ref[rows, 0:1])
            ds = (p * (_dot_nt(dov, vv) - delta)).astype(BF16)
            p_s[rows, :] = p.astype(BF16)
            ds_s[rows, :] = ds
            dq_o[rows, :] = _dot_nn(ds, kv)
        dk_o[...] += _dot_tn(ds_s[...], q_ref[...])
        dv_o[...] += _dot_tn(p_s[...], do_ref[...])
        _comm_finish(comm, cin, cout, csem, grid)

    qspec = pl.BlockSpec((tq, HP), lambda b, hk, g, i: (b * nq + i, hk * grp + g))
    kspec = pl.BlockSpec((s_len, HP), lambda b, hk, g, i: (b, hk))
    return pl.pallas_call(
        body, name=name, grid=grid,
        out_shape=(jax.ShapeDtypeStruct((t, n_h * HP), F32), jax.ShapeDtypeStruct((t, n_hk * HP), F32),
                   jax.ShapeDtypeStruct((t, n_hk * HP), F32), *c_outs),
        in_specs=[qspec, kspec, kspec, qspec, qspec, qspec, *c_in_specs],
        out_specs=(qspec, kspec, kspec, *([_ANY] * len(c_outs))),
        scratch_shapes=[pltpu.VMEM((tq, s_len), BF16), pltpu.VMEM((tq, s_len), BF16), *c_sems],
        input_output_aliases=alias,
        compiler_params=_params(("arbitrary", "arbitrary", "arbitrary", "arbitrary")),
    )(q, k, v, do, o, lse, *c_ins)


def prep_bwd(dqa, dka, dva, dqb, dkb, dvb, zp, tabs, g_qa, g_kva, g_qn, g_kn, w_qb_t, w_kvb_t, tm, s_len):
    t = zp.shape[0]
    nsb = s_len // tm
    scale_a = (QK_NOPE + QK_ROPE) ** -0.5
    scale_b = HD_B ** -0.5

    def body(dqa_ref, dka_ref, dva_ref, dqb_ref, dkb_ref, dvb_ref, qb_ref, qlat_ref, kb_ref, ckv_ref, tab_ref,
             gqa_ref, gkva_ref, gqn_ref, gkn_ref, wqb_ref, wkvb_ref, dzq_o, dsm_o, dqap_o, dkva_o, st_o):
        @pl.when(pl.program_id(0) == 0)
        def _():
            st_o[...] = jnp.zeros_like(st_o)

        ca, s1a, s2a = tab_ref[0], tab_ref[1], tab_ref[2]
        ck = tab_ref[3]
        cb, s1b, s2b = tab_ref[4], tab_ref[5], tab_ref[6]
        for h in range(H_A):
            sl = slice(h * HP, (h + 1) * HP)
            dqap_o[:, sl] = _rope_bwd(dqa_ref[:, sl] * scale_a, ca, s1a, s2a).astype(BF16)
        dcq = _dot_nn(dqap_o[...], wqb_ref[...])
        ql = qlat_ref[...]
        rq = _rstd(ql, Q_LORA)
        xh = ql * rq
        gqa = gqa_ref[...]
        st_o[0:1, :] += _colsum(dcq * xh)
        dsm_o[:, 0:256] = _rms_bwd(dcq, xh, rq, gqa, Q_LORA).astype(BF16)
        dkpe = jnp.zeros((tm, HP), F32)
        for h in range(H_A):
            sl = slice(h * HP, (h + 1) * HP)
            dk = dka_ref[:, sl]
            dkpe = dkpe + dk
            dkva_o[:, sl] = dk.astype(BF16)
        dkva_o[:, H_A * HP:] = dva_ref[...].astype(BF16)
        dsm_o[:, 896:1024] = _rope_bwd(dkpe, ck, s1a, s2a).astype(BF16)
        dckv = _dot_nn(dkva_o[...], wkvb_ref[...])
        cr = ckv_ref[...]
        rk = _rstd(cr, KV_LORA)
        xh = cr * rk
        st_o[1:2, 0:128] += _colsum(dckv * xh)
        dsm_o[:, 768:896] = _rms_bwd(dckv, xh, rk, gkva_ref[...], KV_LORA).astype(BF16)
        gqn, gkn = gqn_ref[...], gkn_ref[...]
        dgq = jnp.zeros((1, HP), F32)
        for h in range(H_B):
            sl = slice(h * HP, (h + 1) * HP)
            dy = _rope_bwd(dqb_ref[:, sl] * scale_b, cb, s1b, s2b)
            xs = qb_ref[:, sl]
            r = _rstd(xs, HD_B)
            xh = xs * r
            dgq = dgq + _colsum(dy * xh)
            dzq_o[:, sl] = _rms_bwd(dy, xh, r, gqn, HD_B).astype(BF16)
        st_o[2:3, 0:128] += dgq
        dgk = jnp.zeros((1, HP), F32)
        for h in range(KV_B):
            sl = slice(h * HP, (h + 1) * HP)
            dy = _rope_bwd(dkb_ref[:, sl], cb, s1b, s2b)
            xs = kb_ref[:, sl]
            r = _rstd(xs, HD_B)
            xh = xs * r
            dgk = dgk + _colsum(dy * xh)
            dsm_o[:, 256 + h * HP:256 + (h + 1) * HP] = _rms_bwd(dy, xh, r, gkn, HD_B).astype(BF16)
        st_o[3:4, 0:128] += dgk
        dsm_o[:, 512:768] = dvb_ref[...].astype(BF16)

    bf = jax.ShapeDtypeStruct((t, 1024), BF16)
    return pl.pallas_call(
        body, name="prep_bwd", grid=(t // tm,),
        out_shape=(bf, bf, bf, jax.ShapeDtypeStruct((t, 2048), BF16), jax.ShapeDtypeStruct((4, 256), F32)),
        in_specs=[_rows(tm, 1024), _rows(tm, 1024), _rows(tm, 1024), _rows(tm, 1024), _rows(tm, 256), _rows(tm, 256),
                  _rows(tm, 1024, 0), _rows(tm, 256, 12), _rows(tm, 256, 13), _rows(tm, 128, 30),
                  pl.BlockSpec((7, tm, HP), lambda i: (0, i % nsb, 0)),
                  _resident((1, Q_LORA)), _resident((1, KV_LORA)), _resident((1, HP)), _resident((1, HP)),
                  _resident((H_A * HP, Q_LORA)), _resident((2 * H_A * HP, KV_LORA))],
        out_specs=(_rows(tm, 1024), _rows(tm, 1024), _rows(tm, 1024), _rows(tm, 2048), pl.BlockSpec((4, 256), lambda i: (0, 0))),
        compiler_params=_params(("arbitrary",)),
    )(dqa, dka, dva, dqb, dkb, dvb, zp, zp, zp, zp, tabs, g_qa, g_kva, g_qn, g_kn, w_qb_t, w_kvb_t)


def in_bwd(dzq, dgab, dsm, x, dx1, g_mix, w_in_t, tm, comm=None):
    t = x.shape[0]
    grid = (t // tm,)
    c_ins, c_in_specs, c_outs, c_sems, alias = _comm_parts(comm, 7, 2)

    def body(*refs):
        ((dzq_ref, dg_ref, dsm_ref, x_ref, dx1_ref, g_ref, w_ref), cin, (dx_o, st_o), cout, _,
         csem) = _split_refs(refs, 7, 2, 0, comm)
        _comm_start(comm, cin, cout, csem, grid)

        @pl.when(pl.program_id(0) == 0)
        def _():
            st_o[...] = jnp.zeros_like(st_o)

        dh = _dot_nn(dzq_ref[...], w_ref[0:1024, :])
        dh = dh + _dot_nn(dg_ref[...], w_ref[1024:3072, :])
        dh = dh + _dot_nn(dsm_ref[...], w_ref[3072:4096, :])
        xv = x_ref[...]
        g = g_ref[...]
        r = _rstd(xv, D_MODEL)
        xh = xv * r
        st_o[...] += _colsum(dh * xh)
        dx_o[...] = dx1_ref[...] + _rms_bwd(dh, xh, r, g, D_MODEL)
        _comm_finish(comm, cin, cout, csem, grid)

    return pl.pallas_call(
        body, name="in_bwd", grid=grid,
        out_shape=(jax.ShapeDtypeStruct((t, D_MODEL), F32), jax.ShapeDtypeStruct((1, D_MODEL), F32), *c_outs),
        in_specs=[_rows(tm, 1024), _rows(tm, 2048), _rows(tm, 1024), _rows(tm, D_MODEL), _rows(tm, D_MODEL),
                  _resident((1, D_MODEL)), _resident((ZP, D_MODEL)), *c_in_specs],
        out_specs=(_rows(tm, D_MODEL), pl.BlockSpec((1, D_MODEL), lambda i: (0, 0)), *([_ANY] * len(c_outs))),
        scratch_shapes=c_sems, input_output_aliases=alias,
        compiler_params=_params(("arbitrary",)),
    )(dzq, dgab, dsm, x, dx1, g_mix, w_in_t, *c_ins)


def matmul_tn(a, b, name, square_a=False):
    t, m = a.shape
    n = b.shape[1]
    bm = min(m, 512)
    tk = min(t, 4096)

    def body(a_ref, b_ref, o_ref):
        @pl.when(pl.program_id(1) == 0)
        def _():
            o_ref[...] = jnp.zeros_like(o_ref)

        av = a_ref[...]
        if square_a:
            av = (av.astype(F32) * av.astype(F32))
        o_ref[...] += _dot_tn(av.astype(BF16), b_ref[...].astype(BF16))

    return pl.pallas_call(
        body, name=name, grid=(m // bm, t // tk), out_shape=jax.ShapeDtypeStruct((m, n), F32),
        in_specs=[pl.BlockSpec((tk, bm), lambda i, kk: (kk, i)), pl.BlockSpec((tk, n), lambda i, kk: (kk, 0))],
        out_specs=pl.BlockSpec((bm, n), lambda i, kk: (i, 0)),
        compiler_params=_params(("parallel", "arbitrary")),
    )(a, b)


def matmul_tn_packed(a, b, name, rows, row_off, total_rows, buf=None, square_a=False):
    t, m = a.shape
    n = b.shape[1]
    pd = max(1, 512 // rows)
    bm = pd * rows
    tk = min(t, 4096)
    nk = t // tk

    def body(a_ref, b_ref, *rest):
        o_ref, acc = rest[-2], rest[-1]

        @pl.when(pl.program_id(1) == 0)
        def _():
            acc[...] = jnp.zeros_like(acc)

        av = a_ref[...]
        if square_a:
            av = (av.astype(F32) * av.astype(F32))
        acc[...] += _dot_tn(av.astype(BF16), b_ref[...].astype(BF16))

        @pl.when(pl.program_id(1) == nk - 1)
        def _():
            o_ref[...] = acc[...].reshape(pd, rows, n).astype(o_ref.dtype)

    in_specs = [pl.BlockSpec((tk, bm), lambda i, kk: (kk, i)), pl.BlockSpec((tk, n), lambda i, kk: (kk, 0))]
    args = [a, b]
    if buf is not None:
        in_specs.append(_ANY)
        args.append(buf)
    return pl.pallas_call(
        body, name=name, grid=(m // bm, nk), out_shape=jax.ShapeDtypeStruct((N_DEV, total_rows, n), BF16),
        in_specs=in_specs, out_specs=pl.BlockSpec((pd, rows, n), lambda i, kk: (i, row_off // rows, 0)),
        scratch_shapes=[pltpu.VMEM((bm, n), F32)], input_output_aliases={2: 0} if buf is not None else {},
        compiler_params=_params(("parallel", "arbitrary")),
    )(*args)


def adamw(w, g, m, v, name, g_transposed=False):
    _, r, c = w.shape
    tr = r if (g_transposed or r <= 256) else 256
    c1 = 1.0 - ADAM_B1 ** ADAM_STEP
    c2 = 1.0 - ADAM_B2 ** ADAM_STEP

    def body(w_ref, g_ref, m_ref, v_ref, g_o, d_o, m_o, v_o):
        gv = g_ref[...].T if g_transposed else g_ref[...]
        mn = ADAM_B1 * m_ref[0] + (1.0 - ADAM_B1) * gv
        vn = ADAM_B2 * v_ref[0] + (1.0 - ADAM_B2) * (gv * gv)
        g_o[0] = gv
        m_o[0] = mn
        v_o[0] = vn
        d_o[0] = -ADAM_LR * ((mn / c1) / (jnp.sqrt(vn / c2) + ADAM_EPS) + ADAM_WD * w_ref[0])

    spec = pl.BlockSpec((1, tr, c), lambda i: (0, i, 0))
    gspec = pl.BlockSpec((c, r), lambda i: (0, 0)) if g_transposed else pl.BlockSpec((tr, c), lambda i: (i, 0))
    shp = jax.ShapeDtypeStruct((1, r, c), F32)
    return pl.pallas_call(
        body, name=name, grid=(r // tr,), out_shape=(shp,) * 4, in_specs=[spec, gspec, spec, spec], out_specs=(spec,) * 4,
        compiler_params=_params(("parallel",)),
    )(w, g, m, v)


def _rope_tables(s_len):
    def angles(pos, dim):
        inv = np.float32(ROPE_THETA) ** (-np.arange(0, dim, 2, dtype=np.float32) / np.float32(dim))
        return pos.astype(np.float32)[:, None] * inv[None, :]

    tpos = np.arange(s_len)
    a1 = angles(tpos, QK_ROPE)
    ar = angles(tpos // GRID_W, HD_B // 2)
    ac = angles(tpos % GRID_W, HD_B // 2)
    z16 = np.zeros((s_len, 16), np.float32)
    z32 = np.zeros((s_len, 32), np.float32)
    z64 = np.zeros((s_len, 64), np.float32)
    one64 = np.ones((s_len, 64), np.float32)
    c1, s1 = np.cos(a1), np.sin(a1)
    ca = np.concatenate([one64, c1, c1, z32], axis=1)
    ck = np.concatenate([z64, c1, c1, z32], axis=1)
    s1a = np.concatenate([z64, -s1, z16, z32], axis=1)
    s2a = np.concatenate([z64, z16, s1, z32], axis=1)
    cr, sr, cc, sc = np.cos(ar), np.sin(ar), np.cos(ac), np.sin(ac)
    cb = np.concatenate([cr, cr, cc, cc, z64], axis=1)
    s1b = np.concatenate([-sr, z16, -sc, z16, z64], axis=1)
    s2b = np.concatenate([z16, sr, z16, sc, z64], axis=1)
    return jnp.asarray(np.stack([ca, s1a, s2a, ck, cb, s1b, s2b]).astype(np.float32))


def _pad_heads(a, n_heads, axis):
    shp = a.shape
    a = a.reshape(shp[:axis] + (n_heads, shp[axis] // n_heads) + shp[axis + 1:])
    pad = [(0, 0)] * a.ndim
    pad[axis + 1] = (0, HP - a.shape[axis + 1])
    a = jnp.pad(a, pad)
    return a.reshape(shp[:axis] + (n_heads * HP,) + shp[axis + 1:])


def _unpad_heads(a, n_heads, width, axis):
    shp = a.shape
    a = a.reshape(shp[:axis] + (n_heads, HP) + shp[axis + 1:])
    a = lax.slice_in_dim(a, 0, width, axis=axis + 1)
    return a.reshape(shp[:axis] + (n_heads * width,) + shp[axis + 1:])


def _pack_rows(blocks, names):
    parts = []
    for name in names:
        b = blocks[name]
        padr = PACK_ROWS[name] - b.shape[-2]
        if padr:
            b = jnp.pad(b, [(0, 0)] * (b.ndim - 2) + [(0, padr), (0, 0)])
        parts.append(b)
    return jnp.concatenate(parts, axis=parts[0].ndim - 2)


def _expand_w_in(wt):
    z64 = jnp.zeros((64, D_MODEL), wt.dtype)
    z32 = jnp.zeros((32, D_MODEL), wt.dtype)
    return jnp.concatenate([
        _pad_heads(wt[416:928], H_B, 0), wt[1184:2208], wt[2208:3232], wt[0:256],
        _pad_heads(wt[928:1056], KV_B, 0), _pad_heads(wt[1056:1184], KV_B, 0), wt[256:384],
        z64, wt[384:416], z32], axis=0)


def _collapse_w_in(dq, dg, ds):
    return jnp.concatenate([
        ds[0:256], ds[768:896], ds[960:992], _unpad_heads(dq, H_B, HD_B, 0), _unpad_heads(ds[256:512], KV_B, HD_B, 0),
        _unpad_heads(ds[512:768], KV_B, HD_B, 0), dg], axis=0)


def kernel(x, p, g_mix, w_in, g_qa, w_qb, g_kva, w_kvb, g_qn, g_kn, w_oa, w_ob, w_o, g_mlp, w_up, w_down, g_ple, w_ple_gate, w_ple, g_final, loss_target, m_g_mix, m_w_in, m_g_qa, m_w_qb, m_g_kva, m_w_kvb, m_g_qn, m_g_kn, m_w_oa, m_w_ob, m_w_o, m_g_mlp, m_w_up, m_w_down, m_g_ple, m_w_ple_gate, m_w_ple, m_g_final, v_g_mix, v_w_in, v_g_qa, v_w_qb, v_g_kva, v_w_kvb, v_g_qn, v_g_kn, v_w_oa, v_w_ob, v_w_o, v_g_mlp, v_w_up, v_w_down, v_g_ple, v_w_ple_gate, v_w_ple, v_g_final):
    n_b, s_len, _ = x.shape
    t = n_b * s_len
    tm = min(512, s_len)
    tq_f = min(1024, s_len)
    tq_b = min(1024, s_len)

    mats = dict(w_in=(w_in, m_w_in, v_w_in), w_qb=(w_qb, m_w_qb, v_w_qb), w_kvb=(w_kvb, m_w_kvb, v_w_kvb),
                w_oa=(w_oa, m_w_oa, v_w_oa), w_ob=(w_ob, m_w_ob, v_w_ob), w_o=(w_o, m_w_o, v_w_o),
                w_up=(w_up, m_w_up, v_w_up), w_down=(w_down, m_w_down, v_w_down),
                w_ple_gate=(w_ple_gate, m_w_ple_gate, v_w_ple_gate), w_ple=(w_ple, m_w_ple, v_w_ple))
    col_sharded = ("w_in", "w_qb", "w_kvb", "w_oa", "w_ob", "w_up", "w_ple")

    blocks = {}
    for name in PACK_W1 + ("w_oa", "w_ob", "w_ple"):
        blocks[name] = mats[name][0][0].T.reshape(-1, D_MODEL).astype(BF16)
    off_w1, _ = _pack_offsets(PACK_W1)
    off_w2, _ = _pack_offsets(PACK_W2)
    xf = x.reshape(t, D_MODEL)
    h, full1 = norm_x(xf, g_mix, tm, comm=gather_first_comm(_pack_rows(blocks, PACK_W1)))
    pack2, full1 = pack_late_weights(w_up, w_down, w_o, w_ple_gate, _pack_rows(blocks, ("w_oa", "w_ob", "w_ple")),
                                     comm=gather_pass_comm(full1))

    def gathered(full, offs, name, rows, width):
        return full[:, offs[name]:offs[name] + rows].reshape(-1, width)

    w_in_t = _expand_w_in(gathered(full1, off_w1, "w_in", 404, D_MODEL))
    w_qb_t = _pad_heads(gathered(full1, off_w1, "w_qb", 24, Q_LORA), H_A, 0)
    wkvb = gathered(full1, off_w1, "w_kvb", 16, KV_LORA).reshape(H_A, 2, 64, KV_LORA)
    w_kvb_t = jnp.concatenate([_pad_heads(wkvb[:, 0].reshape(-1, KV_LORA), H_A, 0),
                               _pad_heads(wkvb[:, 1].reshape(-1, KV_LORA), H_A, 0)], axis=0)

    tabs = _rope_tables(s_len)
    g_qn_p = jnp.pad(g_qn, ((0, 0), (0, HP - HD_B)))
    g_kn_p = jnp.pad(g_kn, ((0, 0), (0, HP - HD_B)))
    pf = p.reshape(t, PLE_DIM)
    tgt = loss_target.reshape(t, D_MODEL)

    zp = in_proj(h, w_in_t, tm)
    qa, ka, va, qb, kb, vb, cq, ckv = attn_prep(zp, tabs, g_qa, g_kva, g_qn_p, g_kn_p, w_qb_t, w_kvb_t, tm, s_len)
    oa, lse_a, full2 = attn_fwd(qa, ka, va, n_b, s_len, tq_f, "attn_a_fwd", comm=gather_first_comm(pack2))
    ob, lse_b, full2 = attn_fwd(qb, kb, vb, n_b, s_len, tq_f, "attn_b_fwd", comm=gather_pass_comm(full2))
    w_oa_t = _pad_heads(gathered(full2, off_w2, "w_oa", 64, H_A * V_DIM_A), H_A, 1)
    w_ob_t = _pad_heads(gathered(full2, off_w2, "w_ob", 64, H_B * HD_B), H_B, 1)
    w_ple_t = gathered(full2, off_w2, "w_ple", 32, PLE_DIM)
    x1, merged = merge_fwd(oa, ob, zp, xf, w_oa_t, w_ob_t, full2, off_w2, tm)
    x2, u = mlp_fwd(x1, g_mlp, full2, off_w2, tm)
    dx2, dt, h3, dpe, st_ple, dx2b = ple_loss_bwd(x2, pf, tgt, g_ple, g_final.reshape(1, D_MODEL), full2, off_w2, w_ple_t, tm)
    dx1, da, h2, st_mlp, dx1b = mlp_bwd(dx2, x1, u, g_mlp, full2, off_w2, tm)

    core = lax.axis_index("c").astype(jnp.int32).reshape(1)
    chip = (2 * lax.axis_index("x") + lax.axis_index("y")).astype(jnp.int32).reshape(1)

    def packed(gblocks, names):
        return _pack_rows({n: gblocks[n].reshape(N_DEV, -1, D_MODEL).astype(BF16) for n in names}, names)

    off_g1, rows_g1 = _pack_offsets(PACK_G1)
    gpack1 = matmul_tn_packed(da, h2, "gw_up", 512, off_g1["w_up"], rows_g1)
    gpack1 = matmul_tn_packed(u, dx2b, "gw_down", 512, off_g1["w_down"], rows_g1, buf=gpack1, square_a=True)
    gpack1 = matmul_tn_packed(h3, dt, "gw_pg", 128, off_g1["w_ple_gate"], rows_g1, buf=gpack1)
    gple = matmul_tn(dpe, pf, "gw_ple").reshape(N_DEV, -1, D_MODEL).astype(BF16)
    gpack1 = lax.dynamic_update_slice(gpack1, gple, (0, off_g1["w_ple"], 0))
    doa, dob, dgab, dya, dyb, got1 = merge_bwd(dx1, oa, ob, zp, w_oa_t, w_ob_t, full2, off_w2, tm,
                                               comm=scatter_sibling_comm(gpack1))
    part1 = add_pairs(gpack1, got1, core)
    dqa, dka, dva, land1 = attn_bwd(qa, ka, va, doa, oa, lse_a, n_b, s_len, tq_b, "attn_a_bwd", comm=scatter_chips_comm(part1))
    gshard1 = sum_chips(part1, land1, chip)

    off_g2, rows_g2 = _pack_offsets(PACK_G2)
    g2 = dict(w_oa=_unpad_heads(matmul_tn(dya, oa, "gw_oa"), H_A, V_DIM_A, 1),
              w_ob=_unpad_heads(matmul_tn(dyb, ob, "gw_ob"), H_B, HD_B, 1))
    gpack2 = matmul_tn_packed(merged, dx1b, "gw_o", 128, off_g2["w_o"], rows_g2)
    gpack2 = lax.dynamic_update_slice(gpack2, packed(g2, ("w_oa", "w_ob")), (0, off_g2["w_oa"], 0))
    part2 = add_pairs(gpack2, exchange_sibling(gpack2), core)
    dqb, dkb, dvb, land2 = attn_bwd(qb, kb, vb, dob, ob, lse_b, n_b, s_len, tq_b, "attn_b_bwd", comm=scatter_chips_comm(part2))
    gshard2 = sum_chips(part2, land2, chip)
    dzq, dsm, dqap, dkva, st_prep = prep_bwd(dqa, dka, dva, dqb, dkb, dvb, zp, tabs, g_qa, g_kva, g_qn_p, g_kn_p,
                                             w_qb_t, w_kvb_t, tm, s_len)

    gkv = matmul_tn(dkva, ckv, "gw_kvb")
    g3 = dict(
        w_in=_collapse_w_in(matmul_tn(dzq, h, "gw_in_q"), matmul_tn(dgab, h, "gw_in_g"), matmul_tn(dsm, h, "gw_in_s")),
        w_qb=_unpad_heads(matmul_tn(dqap, cq, "gw_qb"), H_A, QK_NOPE + QK_ROPE, 0),
        w_kvb=jnp.stack([_unpad_heads(gkv[:H_A * HP], H_A, 64, 0).reshape(H_A, 64, KV_LORA),
                         _unpad_heads(gkv[H_A * HP:], H_A, 64, 0).reshape(H_A, 64, KV_LORA)], axis=1))
    gpack3 = packed(g3, PACK_G3)
    part3 = add_pairs(gpack3, exchange_sibling(gpack3), core)
    grad_x, st_mix, land3 = in_bwd(dzq, dgab, dsm, xf, dx1, g_mix, w_in_t, tm, comm=scatter_chips_comm(part3))
    gshard3 = sum_chips(part3, land3, chip)
    off_g3, _ = _pack_offsets(PACK_G3)
    shards = {n: (gshard1, off_g1[n]) for n in PACK_G1}
    shards.update({n: (gshard2, off_g2[n]) for n in PACK_G2})
    shards.update({n: (gshard3, off_g3[n]) for n in PACK_G3})

    stats = allreduce_stats(st_mix, st_prep, st_mlp, st_ple)
    loss = jnp.sum(stats[ST_LOSS])

    out_g, out_d, out_m, out_v = {}, {}, {}, {}
    for name, (w, m, v) in mats.items():
        gshard, off = shards[name]
        r, c = w.shape[1:]
        if name in col_sharded:
            g2 = gshard[off:off + (r * c) // D_MODEL].reshape(c, r)
            in_kernel = r % 128 == 0 and c % 128 == 0
            res = adamw(w, g2 if in_kernel else g2.T, m, v, "adamw_" + name, g_transposed=in_kernel)
        else:
            res = adamw(w, gshard[off:off + r], m, v, "adamw_" + name)
        out_g[name], out_d[name], out_m[name], out_v[name] = res

    gains = (("g_mix", g_mix, m_g_mix, v_g_mix, ST_G_MIX), ("g_qa", g_qa, m_g_qa, v_g_qa, ST_G_QA),
             ("g_kva", g_kva, m_g_kva, v_g_kva, ST_G_KVA), ("g_qn", g_qn, m_g_qn, v_g_qn, ST_G_QN),
             ("g_kn", g_kn, m_g_kn, v_g_kn, ST_G_KN), ("g_mlp", g_mlp, m_g_mlp, v_g_mlp, ST_G_MLP),
             ("g_ple", g_ple, m_g_ple, v_g_ple, ST_G_PLE), ("g_final", g_final, m_g_final, v_g_final, ST_G_FINAL))
    res = adamw_gains(stats, [(r_, w.reshape(1, -1), m.reshape(1, -1), v.reshape(1, -1)) for _, w, m, v, r_ in gains])
    for (name, w, _, _, _), (gg, gd, gm, gv) in zip(gains, res):
        out_g[name], out_d[name], out_m[name], out_v[name] = (a.reshape(w.shape) for a in (gg, gd, gm, gv))

    order = ("g_mix", "w_in", "g_qa", "w_qb", "g_kva", "w_kvb", "g_qn", "g_kn", "w_oa", "w_ob", "w_o", "g_mlp",
             "w_up", "w_down", "g_ple", "w_ple_gate", "w_ple", "g_final")
    return (loss, grad_x.reshape(x.shape), *[out_g[n] for n in order], *[out_d[n] for n in order],
            *[out_m[n] for n in order], *[out_v[n] for n in order])
```

```python
import numpy as np
import jax
import jax.numpy as jnp
from jax import lax
from jax.experimental import pallas as pl
from jax.experimental.pallas import tpu as pltpu

F32 = jnp.float32
BF16 = jnp.bfloat16

D_MODEL = 1024
EPS = 1e-6
ROPE_THETA = 10000.0
GRID_W = 64
H_A = 8
QK_NOPE = 64
QK_ROPE = 32
V_DIM_A = 64
Q_LORA = 256
KV_LORA = 128
H_B = 8
KV_B = 2
HD_B = 64
D_FF = 4 * D_MODEL
PLE_DIM = 256
HP = 128
ZP = 4096
N_DEV = 8
N_CHIP = 4

ADAM_LR = 0.001
ADAM_B1 = 0.9
ADAM_B2 = 0.999
ADAM_EPS = 1e-08
ADAM_WD = 0.01
ADAM_STEP = 10

VMEM_LIMIT = 52 * 1024 * 1024

PACK_ROWS = dict(w_in=416, w_qb=32, w_kvb=16, w_oa=64, w_ob=64, w_o=128, w_up=512, w_down=512, w_ple_gate=128, w_ple=32)
PACK_W1 = ("w_in", "w_qb", "w_kvb")
PACK_W2 = ("w_up", "w_down", "w_o", "w_ple_gate", "w_oa", "w_ob", "w_ple")
PACK_G1 = ("w_up", "w_down", "w_ple_gate", "w_ple")
PACK_G2 = ("w_o", "w_oa", "w_ob")
PACK_G3 = ("w_in", "w_qb", "w_kvb")


def _pack_offsets(names):
    off, o = {}, 0
    for n in names:
        off[n] = o
        o += PACK_ROWS[n]
    return off, o

ST_G_MIX, ST_G_QA, ST_G_KVA, ST_G_QN, ST_G_KN, ST_G_MLP, ST_G_PLE, ST_G_FINAL, ST_LOSS = range(9)
ST_ROWS = 16


def _dot_nn(a, b):
    return lax.dot_general(a, b, (((1,), (0,)), ((), ())), preferred_element_type=F32)


def _dot_nt(a, b):
    return lax.dot_general(a, b, (((1,), (1,)), ((), ())), preferred_element_type=F32)


def _dot_tn(a, b):
    return lax.dot_general(a, b, (((0,), (0,)), ((), ())), preferred_element_type=F32)


def _rstd(x, n):
    return lax.rsqrt(jnp.sum(x * x, axis=-1, keepdims=True) * (1.0 / n) + EPS)


def _rms_bwd(dy, xh, r, g, n):
    dxh = dy * g
    return r * (dxh - xh * (jnp.sum(dxh * xh, axis=-1, keepdims=True) * (1.0 / n)))


def _rope_fwd(x, c, s1, s2):
    return x * c + pltpu.roll(x, HP - 16, 1) * s1 + pltpu.roll(x, 16, 1) * s2


def _rope_bwd(d, c, s1, s2):
    return d * c + pltpu.roll(d * s1, 16, 1) + pltpu.roll(d * s2, HP - 16, 1)


def _colsum(v):
    return jnp.sum(v, axis=0, keepdims=True)


def _params(sem=None, vmem=VMEM_LIMIT):
    return pltpu.CompilerParams(dimension_semantics=sem, vmem_limit_bytes=vmem)


def _resident(shape):
    nd = len(shape)
    return pl.BlockSpec(shape, lambda *_: (0,) * nd, pipeline_mode=pl.Buffered(1))


def _rows(tm, width, col=0):
    return pl.BlockSpec((tm, width), lambda i: (i, col))


def _packed_weight(rows, off):
    return pl.BlockSpec((N_DEV, rows, D_MODEL), lambda *_: (0, off // rows, 0), pipeline_mode=pl.Buffered(1))


def _wrows(ref, start, size):
    rows = ref.shape[1]
    return ref[start // rows:(start + size) // rows].reshape(size, D_MODEL)


def _mesh_pos():
    return lax.axis_index("x"), lax.axis_index("y"), lax.axis_index("c")


def _flip(v, bit):
    return (1 - v) if bit else v


_ANY = pl.BlockSpec(memory_space=pl.ANY)
_MESH = pl.DeviceIdType.MESH


def _remote(src, dst, send_sems, recv_sems, k, to):
    return pltpu.make_async_remote_copy(src_ref=src, dst_ref=dst, send_sem=send_sems.at[k], recv_sem=recv_sems.at[k],
                                        device_id=to, device_id_type=_MESH)


def _sibling_copies(g_ref, got_ref, send_sems, recv_sems):
    x, y, c = _mesh_pos()
    return [_remote(g_ref.at[2 * j + (1 - c)], got_ref.at[j], send_sems, recv_sems, j, (x, y, 1 - c)) for j in range(N_CHIP)]


def _chip_copies(p_ref, land_ref, send_sems, recv_sems):
    x, y, c = _mesh_pos()
    copies = []
    for k in (1, 2, 3):
        tx, ty = _flip(x, k & 2), _flip(y, k & 1)
        copies.append(_remote(p_ref.at[2 * tx + ty], land_ref.at[k - 1], send_sems, recv_sems, k - 1, (tx, ty, c)))
    return copies


class _Comm:
    def __init__(self, ins, out_shapes, sems, make, aliases=None):
        self.ins, self.out_shapes, self.sems, self.make, self.aliases = list(ins), list(out_shapes), list(sems), make, aliases or {}


def _comm_parts(comm, n_in, n_out):
    if comm is None:
        return [], [], [], [], {}
    alias = {n_in + j: n_out + k for j, k in comm.aliases.items()}
    return comm.ins, [_ANY] * len(comm.ins), comm.out_shapes, comm.sems, alias


def _split_refs(refs, n_in, n_out, n_scratch, comm):
    n_ci = len(comm.ins) if comm else 0
    n_co = len(comm.out_shapes) if comm else 0
    cuts, i = [], 0
    for n in (n_in, n_ci, n_out, n_co, n_scratch):
        cuts.append(refs[i:i + n])
        i += n
    return (*cuts, refs[i:])


def _grid_edge(grid, last):
    cond = None
    for d, n in enumerate(grid):
        here = pl.program_id(d) == (n - 1 if last else 0)
        cond = here if cond is None else cond & here
    return cond


def _comm_start(comm, cin, cout, csem, grid):
    if comm is not None:
        @pl.when(_grid_edge(grid, False))
        def _():
            for cp in comm.make(cin, cout, csem):
                cp.start()


def _comm_finish(comm, cin, cout, csem, grid):
    if comm is not None:
        @pl.when(_grid_edge(grid, True))
        def _():
            for cp in comm.make(cin, cout, csem):
                cp.wait()


def gather_first_comm(shard):
    r, w = shard.shape

    def make(cin, cout, sems):
        (x_ref,), (out_ref,), (send_sems, recv_sems, local_sem) = cin, cout, sems
        x, y, c = _mesh_pos()
        mine = out_ref.at[4 * x + 2 * y + c]
        targets = [(x, y, 1 - c), (1 - x, y, c), (x, 1 - y, c), (1 - x, 1 - y, c)]
        return [_remote(x_ref, mine, send_sems, recv_sems, k, to) for k, to in enumerate(targets)] + [
            pltpu.make_async_copy(x_ref, mine, local_sem)]

    return _Comm([shard], [jax.ShapeDtypeStruct((N_DEV, r, w), shard.dtype)],
                 [pltpu.SemaphoreType.DMA((4,)), pltpu.SemaphoreType.DMA((4,)), pltpu.SemaphoreType.DMA], make)


def gather_pass_comm(full):
    def make(cin, cout, sems):
        (in_ref,), (out_ref,), (send_sems, recv_sems) = cin, cout, sems
        x, y, c = _mesh_pos()
        copies = []
        for k, (px, py) in enumerate([(1 - x, y), (x, 1 - y), (1 - x, 1 - y)]):
            idx = 4 * px + 2 * py + c
            copies.append(_remote(in_ref.at[idx], out_ref.at[idx], send_sems, recv_sems, k, (x, y, 1 - c)))
        return copies

    return _Comm([full], [jax.ShapeDtypeStruct(full.shape, full.dtype)],
                 [pltpu.SemaphoreType.DMA((3,)), pltpu.SemaphoreType.DMA((3,))], make, aliases={0: 0})


def scatter_sibling_comm(g):
    _, r, w = g.shape
    return _Comm([g], [jax.ShapeDtypeStruct((N_CHIP, r, w), g.dtype)],
                 [pltpu.SemaphoreType.DMA((N_CHIP,)), pltpu.SemaphoreType.DMA((N_CHIP,))],
                 lambda cin, cout, sems: _sibling_copies(cin[0], cout[0], sems[0], sems[1]))


def scatter_chips_comm(part):
    _, r, w = part.shape
    return _Comm([part], [jax.ShapeDtypeStruct((N_CHIP - 1, r, w), part.dtype)],
                 [pltpu.SemaphoreType.DMA((3,)), pltpu.SemaphoreType.DMA((3,))],
                 lambda cin, cout, sems: _chip_copies(cin[0], cout[0], sems[0], sems[1]))


def exchange_sibling(g):
    _, r, w = g.shape

    def body(g_ref, got_ref, send_sems, recv_sems):
        copies = _sibling_copies(g_ref, got_ref, send_sems, recv_sems)
        for cp in copies:
            cp.start()
        for cp in copies:
            cp.wait()

    return pl.pallas_call(
        body, name="exchange_sibling", out_shape=jax.ShapeDtypeStruct((N_CHIP, r, w), g.dtype),
        in_specs=[_ANY], out_specs=_ANY,
        scratch_shapes=[pltpu.SemaphoreType.DMA((N_CHIP,)), pltpu.SemaphoreType.DMA((N_CHIP,))],
    )(g)


def exchange_chips(part):
    _, r, w = part.shape

    def body(p_ref, land_ref, send_sems, recv_sems):
        copies = _chip_copies(p_ref, land_ref, send_sems, recv_sems)
        for cp in copies:
            cp.start()
        for cp in copies:
            cp.wait()

    return pl.pallas_call(
        body, name="exchange_chips", out_shape=jax.ShapeDtypeStruct((N_CHIP - 1, r, w), part.dtype),
        in_specs=[_ANY], out_specs=_ANY,
        scratch_shapes=[pltpu.SemaphoreType.DMA((3,)), pltpu.SemaphoreType.DMA((3,))],
    )(part)


def allreduce_stats(st_mix, st_prep, st_mlp, st_ple):
    def body(mix_ref, prep_ref, mlp_ref, ple_ref, out_ref, mine, gath, send_sems, recv_sems):
        x, y, c = _mesh_pos()
        me = 4 * x + 2 * y + c
        mine[...] = jnp.zeros_like(mine)
        mine[ST_G_MIX:ST_G_MIX + 1, :] = mix_ref[...]
        mine[ST_G_QA:ST_G_KN + 1, 0:256] = prep_ref[...]
        mine[ST_G_MLP:ST_G_MLP + 1, :] = mlp_ref[...]
        mine[ST_G_PLE:ST_LOSS + 1, :] = ple_ref[...]
        gath[me] = mine[...]
        copies = []
        for k in range(1, N_DEV):
            peer = (_flip(x, k & 4), _flip(y, k & 2), _flip(c, k & 1))
            copies.append(_remote(mine, gath.at[me], send_sems, recv_sems, k - 1, peer))
        for cp in copies:
            cp.start()
        for cp in copies:
            cp.wait()
        acc = gath[0]
        for d in range(1, N_DEV):
            acc = acc + gath[d]
        out_ref[...] = acc

    vm = pl.BlockSpec(memory_space=pltpu.VMEM)
    return pl.pallas_call(
        body, name="allreduce_stats", out_shape=jax.ShapeDtypeStruct((ST_ROWS, D_MODEL), F32),
        in_specs=[vm] * 4, out_specs=vm,
        scratch_shapes=[pltpu.VMEM((ST_ROWS, D_MODEL), F32), pltpu.VMEM((N_DEV, ST_ROWS, D_MODEL), F32),
                        pltpu.SemaphoreType.DMA((N_DEV - 1,)), pltpu.SemaphoreType.DMA((N_DEV - 1,))],
    )(st_mix, st_prep, st_mlp, st_ple)


def adamw_gains(stats, gains):
    c1 = 1.0 - ADAM_B1 ** ADAM_STEP
    c2 = 1.0 - ADAM_B2 ** ADAM_STEP
    n = len(gains)

    def body(st_ref, *refs):
        ins, outs = refs[:3 * n], refs[3 * n:]
        for i, (row, w, _, _) in enumerate(gains):
            width = w.shape[1]
            gv = st_ref[row:row + 1, 0:width]
            mn = ADAM_B1 * ins[3 * i + 1][...] + (1.0 - ADAM_B1) * gv
            vn = ADAM_B2 * ins[3 * i + 2][...] + (1.0 - ADAM_B2) * (gv * gv)
            outs[4 * i][...] = gv
            outs[4 * i + 1][...] = -ADAM_LR * ((mn / c1) / (jnp.sqrt(vn / c2) + ADAM_EPS) + ADAM_WD * ins[3 * i][...])
            outs[4 * i + 2][...] = mn
            outs[4 * i + 3][...] = vn

    vm = pl.BlockSpec(memory_space=pltpu.VMEM)
    flat = [a for (_, w, m, v) in gains for a in (w, m, v)]
    out_shape = tuple(jax.ShapeDtypeStruct(w.shape, F32) for (_, w, _, _) in gains for _ in range(4))
    res = pl.pallas_call(body, name="adamw_gains", out_shape=out_shape, in_specs=[vm] * (1 + 3 * n),
                         out_specs=tuple([vm] * (4 * n)))(stats, *flat)
    return [res[4 * i:4 * i + 4] for i in range(n)]


def _row_tile(r, cap=640):
    return max(d for d in range(16, min(r, cap) + 1, 16) if r % d == 0)


def add_pairs(g, got, core):
    n, r, w = got.shape
    tr = _row_tile(r)

    def body(c_ref, a_ref, b_ref, o_ref):
        o_ref[...] = (a_ref[...].astype(F32) + b_ref[...].astype(F32)).astype(o_ref.dtype)

    spec = pl.BlockSpec((1, tr, w), lambda i, j, c: (i, j, 0))
    return pl.pallas_call(
        body, name="add_pairs", out_shape=jax.ShapeDtypeStruct(got.shape, got.dtype),
        grid_spec=pltpu.PrefetchScalarGridSpec(
            num_scalar_prefetch=1, grid=(n, r // tr),
            in_specs=[pl.BlockSpec((1, tr, w), lambda i, j, c: (2 * i + c[0], j, 0)), spec], out_specs=spec),
        compiler_params=_params(("parallel", "parallel")),
    )(core, g, got)


def sum_chips(part, land, chip):
    _, r, w = part.shape
    tr = _row_tile(r)

    def body(c_ref, p_ref, l_ref, o_ref):
        acc = p_ref[0].astype(F32)
        for s in range(N_CHIP - 1):
            acc = acc + l_ref[s].astype(F32)
        o_ref[...] = acc

    return pl.pallas_call(
        body, name="sum_chips", out_shape=jax.ShapeDtypeStruct((r, w), F32),
        grid_spec=pltpu.PrefetchScalarGridSpec(
            num_scalar_prefetch=1, grid=(r // tr,),
            in_specs=[pl.BlockSpec((1, tr, w), lambda i, c: (c[0], i, 0)), pl.BlockSpec((N_CHIP - 1, tr, w), lambda i, c: (0, i, 0))],
            out_specs=pl.BlockSpec((tr, w), lambda i, c: (i, 0))),
        compiler_params=_params(("parallel",)),
    )(chip, part, land)


def norm_x(x, g_mix, tm, comm=None):
    t = x.shape[0]
    grid = (t // tm,)
    c_ins, c_in_specs, c_outs, c_sems, alias = _comm_parts(comm, 2, 1)

    def body(*refs):
        (x_ref, g_ref), cin, (h_ref,), cout, _, csem = _split_refs(refs, 2, 1, 0, comm)
        _comm_start(comm, cin, cout, csem, grid)
        xv = x_ref[...]
        h_ref[...] = (xv * _rstd(xv, D_MODEL) * g_ref[...]).astype(BF16)
        _comm_finish(comm, cin, cout, csem, grid)

    return pl.pallas_call(
        body, name="norm_x", grid=grid, out_shape=(jax.ShapeDtypeStruct((t, D_MODEL), BF16), *c_outs),
        in_specs=[_rows(tm, D_MODEL), _resident((1, D_MODEL)), *c_in_specs],
        out_specs=(_rows(tm, D_MODEL), *([_ANY] * len(c_outs))),
        scratch_shapes=c_sems, input_output_aliases=alias, compiler_params=_params(("arbitrary",)),
    )(x, g_mix, *c_ins)


def pack_late_weights(w_up, w_down, w_o, w_pg, small, comm=None):
    rows = sum(PACK_ROWS[n] for n in PACK_W2)
    c_ins, c_in_specs, c_outs, c_sems, alias = _comm_parts(comm, 5, 1)
    grid = (1,)

    def body(*refs):
        (up_ref, dn_ref, o_ref, pg_ref, sm_ref), cin, (out_ref,), cout, _, csem = _split_refs(refs, 5, 1, 0, comm)
        _comm_start(comm, cin, cout, csem, grid)
        out_ref[0:512, :] = up_ref[0].T.astype(BF16)
        out_ref[512:1024, :] = dn_ref[0].astype(BF16)
        out_ref[1024:1152, :] = o_ref[0].astype(BF16)
        out_ref[1152:1280, :] = pg_ref[0].astype(BF16)
        out_ref[1280:rows, :] = sm_ref[...]
        _comm_finish(comm, cin, cout, csem, grid)

    def whole(a):
        nd = a.ndim
        return pl.BlockSpec(a.shape, lambda i: (0,) * nd)

    args = (w_up, w_down, w_o, w_pg, small)
    return pl.pallas_call(
        body, name="pack_late_weights", grid=grid, out_shape=(jax.ShapeDtypeStruct((rows, D_MODEL), BF16), *c_outs),
        in_specs=[*[whole(a) for a in args], *c_in_specs],
        out_specs=(pl.BlockSpec((rows, D_MODEL), lambda i: (0, 0)), *([_ANY] * len(c_outs))),
        scratch_shapes=c_sems, input_output_aliases=alias, compiler_params=_params(("arbitrary",)),
    )(*args, *c_ins)


def in_proj(h, w_in_t, tm):
    t = h.shape[0]
    nc = 512

    def body(h_ref, w_ref, z_ref):
        hv = h_ref[...]
        for cidx in range(ZP // nc):
            z_ref[:, cidx * nc:(cidx + 1) * nc] = _dot_nt(hv, w_ref[cidx * nc:(cidx + 1) * nc, :])

    return pl.pallas_call(
        body, name="in_proj", grid=(t // tm,), out_shape=jax.ShapeDtypeStruct((t, ZP), F32),
        in_specs=[_rows(tm, D_MODEL), _resident((ZP, D_MODEL))],
        out_specs=_rows(tm, ZP), compiler_params=_params(("parallel",)),
    )(h, w_in_t)


def attn_prep(zp, tabs, g_qa, g_kva, g_qn, g_kn, w_qb_t, w_kvb_t, tm, s_len):
    t = zp.shape[0]
    nsb = s_len // tm
    scale_a = (QK_NOPE + QK_ROPE) ** -0.5
    scale_b = HD_B ** -0.5

    def body(qb_ref, qlat_ref, kb_ref, vb_ref, ckv_ref, kpe_ref, tab_ref, gqa_ref, gkva_ref, gqn_ref, gkn_ref,
             wqb_ref, wkvb_ref, qa_o, ka_o, va_o, qb_o, kb_o, vb_o, cq_o, ckvn_o):
        ca, s1a, s2a = tab_ref[0], tab_ref[1], tab_ref[2]
        ck = tab_ref[3]
        cb, s1b, s2b = tab_ref[4], tab_ref[5], tab_ref[6]
        ql = qlat_ref[...]
        cq = (ql * _rstd(ql, Q_LORA) * gqa_ref[...]).astype(BF16)
        cq_o[...] = cq
        qa = _dot_nt(cq, wqb_ref[...])
        slabs = [slice(h * HP, (h + 1) * HP) for h in range(H_A)]
        qa_o[...] = jnp.concatenate(
            [(_rope_fwd(qa[:, sl], ca, s1a, s2a) * scale_a).astype(BF16) for sl in slabs], axis=1)
        cr = ckv_ref[...]
        ckv = (cr * _rstd(cr, KV_LORA) * gkva_ref[...]).astype(BF16)
        ckvn_o[...] = ckv
        kva = _dot_nt(ckv, wkvb_ref[...])
        kpe = _rope_fwd(kpe_ref[...], ck, s1a, s2a)
        ka_o[...] = jnp.concatenate([(kva[:, sl] + kpe).astype(BF16) for sl in slabs], axis=1)
        va_o[...] = kva[:, H_A * HP:].astype(BF16)
        gqn, gkn = gqn_ref[...], gkn_ref[...]

        def norm_rope(ref, sl, g, scale):
            xs = ref[:, sl]
            y = _rope_fwd(xs * _rstd(xs, HD_B) * g, cb, s1b, s2b)
            return (y if scale is None else y * scale).astype(BF16)

        qb_o[...] = jnp.concatenate([norm_rope(qb_ref, sl, gqn, scale_b) for sl in slabs], axis=1)
        kb_o[...] = jnp.concatenate([norm_rope(kb_ref, sl, gkn, None) for sl in slabs[:KV_B]], axis=1)
        vb_o[...] = vb_ref[...].astype(BF16)

    def o(width):
        return jax.ShapeDtypeStruct((t, width), BF16)

    return pl.pallas_call(
        body, name="attn_prep", grid=(t // tm,),
        out_shape=(o(H_A * HP), o(H_A * HP), o(H_A * HP), o(H_B * HP), o(KV_B * HP), o(KV_B * HP), o(Q_LORA), o(KV_LORA)),
        in_specs=[_rows(tm, 1024, 0), _rows(tm, 256, 12), _rows(tm, 256, 13), _rows(tm, 256, 14),
                  _rows(tm, 128, 30), _rows(tm, 128, 31),
                  pl.BlockSpec((7, tm, HP), lambda i: (0, i % nsb, 0)),
                  _resident((1, Q_LORA)), _resident((1, KV_LORA)), _resident((1, HP)), _resident((1, HP)),
                  _resident((H_A * HP, Q_LORA)), _resident((2 * H_A * HP, KV_LORA))],
        out_specs=(_rows(tm, H_A * HP), _rows(tm, H_A * HP), _rows(tm, H_A * HP), _rows(tm, H_B * HP),
                   _rows(tm, KV_B * HP), _rows(tm, KV_B * HP), _rows(tm, Q_LORA), _rows(tm, KV_LORA)),
        compiler_params=_params(("parallel",)),
    )(zp, zp, zp, zp, zp, zp, tabs, g_qa, g_kva, g_qn, g_kn, w_qb_t, w_kvb_t)


def attn_fwd(q, k, v, n_b, s_len, tq, name, comm=None):
    t = q.shape[0]
    n_h, n_hk = q.shape[1] // HP, k.shape[1] // HP
    grp = n_h // n_hk
    nq = s_len // tq
    sub = min(tq, 256)
    grid = (n_b, n_h, nq)
    c_ins, c_in_specs, c_outs, c_sems, alias = _comm_parts(comm, 3, 2)

    def body(*refs):
        (q_ref, k_ref, v_ref), cin, (o_ref, lse_ref), cout, _, csem = _split_refs(refs, 3, 2, 0, comm)
        _comm_start(comm, cin, cout, csem, grid)
        kv, vv = k_ref[...], v_ref[...]
        for r in range(tq // sub):
            rows = slice(r * sub, (r + 1) * sub)
            s = _dot_nt(q_ref[rows, :], kv)
            m = jnp.max(s, axis=-1, keepdims=True)
            p = jnp.exp(s - m)
            l = jnp.sum(p, axis=-1, keepdims=True)
            o_ref[rows, :] = (_dot_nn(p.astype(BF16), vv) * (1.0 / l)).astype(o_ref.dtype)
            lse_ref[rows, :] = jnp.broadcast_to(m + jnp.log(l), (sub, HP))
        _comm_finish(comm, cin, cout, csem, grid)

    qspec = pl.BlockSpec((tq, HP), lambda b, h, i: (b * nq + i, h))
    kspec = pl.BlockSpec((s_len, HP), lambda b, h, i: (b, h // grp))
    return pl.pallas_call(
        body, name=name, grid=grid,
        out_shape=(jax.ShapeDtypeStruct((t, n_h * HP), BF16), jax.ShapeDtypeStruct((t, n_h * HP), F32), *c_outs),
        in_specs=[qspec, kspec, kspec, *c_in_specs], out_specs=(qspec, qspec, *([_ANY] * len(c_outs))),
        scratch_shapes=c_sems, input_output_aliases=alias,
        compiler_params=_params(("arbitrary", "arbitrary", "arbitrary")),
    )(q, k, v, *c_ins)


def merge_fwd(oa, ob, zp, x, w_oa_t, w_ob_t, wpack, off, tm):
    t = x.shape[0]

    def body(oa_ref, ob_ref, ga_ref, gb_ref, x_ref, woa_ref, wob_ref, wo_ref, x1_o, mg_o, y_o):
        ya = _dot_nt(oa_ref[...], woa_ref[...])
        yb = _dot_nt(ob_ref[...], wob_ref[...])
        y_o[:, 0:D_MODEL] = ya.astype(BF16)
        y_o[:, D_MODEL:2 * D_MODEL] = yb.astype(BF16)
        merged = (jax.nn.sigmoid(ga_ref[...]) * ya + jax.nn.sigmoid(gb_ref[...]) * yb).astype(BF16)
        mg_o[...] = merged
        x1_o[...] = x_ref[...] + _dot_nn(merged, _wrows(wo_ref, 0, D_MODEL))

    return pl.pallas_call(
        body, name="merge_fwd", grid=(t // tm,),
        out_shape=(jax.ShapeDtypeStruct((t, D_MODEL), F32), jax.ShapeDtypeStruct((t, D_MODEL), BF16),
                   jax.ShapeDtypeStruct((t, 2 * D_MODEL), BF16)),
        in_specs=[_rows(tm, H_A * HP), _rows(tm, H_B * HP), _rows(tm, 1024, 1), _rows(tm, 1024, 2), _rows(tm, D_MODEL),
                  _resident((D_MODEL, H_A * HP)), _resident((D_MODEL, H_B * HP)), _packed_weight(128, off["w_o"])],
        out_specs=(_rows(tm, D_MODEL), _rows(tm, D_MODEL), _rows(tm, 2 * D_MODEL)), compiler_params=_params(("parallel",)),
    )(oa, ob, zp, zp, x, w_oa_t, w_ob_t, wpack)


def mlp_fwd(x1, g_mlp, wpack, off, tm):
    t = x1.shape[0]
    fc = 1024

    def body(x_ref, g_ref, wup_ref, wdn_ref, x2_o, u_o):
        xv = x_ref[...]
        h2 = (xv * _rstd(xv, D_MODEL) * g_ref[...]).astype(BF16)
        acc = xv
        for cidx in range(D_FF // fc):
            sl = slice(cidx * fc, (cidx + 1) * fc)
            u = jnp.maximum(_dot_nt(h2, _wrows(wup_ref, cidx * fc, fc)), 0.0)
            u_o[:, sl] = u.astype(BF16)
            acc = acc + _dot_nn((u * u).astype(BF16), _wrows(wdn_ref, cidx * fc, fc))
        x2_o[...] = acc

    return pl.pallas_call(
        body, name="mlp_fwd", grid=(t // tm,),
        out_shape=(jax.ShapeDtypeStruct((t, D_MODEL), F32), jax.ShapeDtypeStruct((t, D_FF), BF16)),
        in_specs=[_rows(tm, D_MODEL), _resident((1, D_MODEL)), _packed_weight(512, off["w_up"]), _packed_weight(512, off["w_down"])],
        out_specs=(_rows(tm, D_MODEL), _rows(tm, D_FF)), compiler_params=_params(("parallel",)),
    )(x1, g_mlp, wpack, wpack)


def ple_loss_bwd(x2, p, tgt, g_ple, g_final, wpack, off, w_ple_t, tm):
    t = x2.shape[0]
    inv_d = 1.0 / D_MODEL

    def body(x2_ref, p_ref, tg_ref, gp_ref, gf_ref, wpg_ref, wple_ref, dx2_o, dt_o, h3_o, dpe_o, st_o, dx2b_o):
        @pl.when(pl.program_id(0) == 0)
        def _():
            st_o[...] = jnp.zeros_like(st_o)

        x2v = x2_ref[...]
        gp, gf = gp_ref[...], gf_ref[...]
        w_pg = _wrows(wpg_ref, 0, D_MODEL)
        r2 = _rstd(x2v, D_MODEL)
        xh2 = x2v * r2
        h3 = (xh2 * gp).astype(BF16)
        h3_o[...] = h3
        gate = jax.nn.sigmoid(_dot_nn(h3, w_pg))
        pe = _dot_nt(p_ref[...].astype(BF16), wple_ref[...])
        x3 = x2v + gate * pe
        r3 = _rstd(x3, D_MODEL)
        xh3 = x3 * r3
        err = xh3 * gf - tg_ref[...]
        dy = err * inv_d
        dx3 = _rms_bwd(dy, xh3, r3, gf, D_MODEL)
        dpe_o[...] = (dx3 * gate).astype(BF16)
        dt = (dx3 * pe * gate * (1.0 - gate)).astype(BF16)
        dt_o[...] = dt
        dh3 = _dot_nt(dt, w_pg)
        dx2 = dx3 + _rms_bwd(dh3, xh2, r2, gp, D_MODEL)
        dx2_o[...] = dx2
        dx2b_o[...] = dx2.astype(BF16)
        st_o[0:1, :] += _colsum(dh3 * xh2)
        st_o[1:2, :] += _colsum(dy * xh3)
        st_o[2:3, :] += _colsum(err * err) * (0.5 * inv_d)

    bf = jax.ShapeDtypeStruct((t, D_MODEL), BF16)
    return pl.pallas_call(
        body, name="ple_loss_bwd", grid=(t // tm,),
        out_shape=(jax.ShapeDtypeStruct((t, D_MODEL), F32), bf, bf, bf, jax.ShapeDtypeStruct((3, D_MODEL), F32), bf),
        in_specs=[_rows(tm, D_MODEL), _rows(tm, PLE_DIM), _rows(tm, D_MODEL), _resident((1, D_MODEL)), _resident((1, D_MODEL)),
                  _packed_weight(128, off["w_ple_gate"]), _resident((D_MODEL, PLE_DIM))],
        out_specs=(_rows(tm, D_MODEL), _rows(tm, D_MODEL), _rows(tm, D_MODEL), _rows(tm, D_MODEL),
                   pl.BlockSpec((3, D_MODEL), lambda i: (0, 0)), _rows(tm, D_MODEL)),
        compiler_params=_params(("arbitrary",)),
    )(x2, p, tgt, g_ple, g_final, wpack, w_ple_t)


def mlp_bwd(dx2, x1, u, g_mlp, wpack, off, tm):
    t = x1.shape[0]
    fc = 1024

    def body(dx2_ref, x1_ref, u_ref, g_ref, wup_ref, wdn_ref, dx1_o, da_o, h2_o, st_o, dx1b_o):
        @pl.when(pl.program_id(0) == 0)
        def _():
            st_o[...] = jnp.zeros_like(st_o)

        d2 = dx2_ref[...]
        d2b = d2.astype(BF16)
        dh2 = jnp.zeros((tm, D_MODEL), F32)
        for cidx in range(D_FF // fc):
            sl = slice(cidx * fc, (cidx + 1) * fc)
            da = (_dot_nt(d2b, _wrows(wdn_ref, cidx * fc, fc)) * (2.0 * u_ref[:, sl].astype(F32))).astype(BF16)
            da_o[:, sl] = da
            dh2 = dh2 + _dot_nn(da, _wrows(wup_ref, cidx * fc, fc))
        xv = x1_ref[...]
        g = g_ref[...]
        r1 = _rstd(xv, D_MODEL)
        xh1 = xv * r1
        h2_o[...] = (xh1 * g).astype(BF16)
        st_o[...] += _colsum(dh2 * xh1)
        dx1 = d2 + _rms_bwd(dh2, xh1, r1, g, D_MODEL)
        dx1_o[...] = dx1
        dx1b_o[...] = dx1.astype(BF16)

    return pl.pallas_call(
        body, name="mlp_bwd", grid=(t // tm,),
        out_shape=(jax.ShapeDtypeStruct((t, D_MODEL), F32), jax.ShapeDtypeStruct((t, D_FF), BF16),
                   jax.ShapeDtypeStruct((t, D_MODEL), BF16), jax.ShapeDtypeStruct((1, D_MODEL), F32),
                   jax.ShapeDtypeStruct((t, D_MODEL), BF16)),
        in_specs=[_rows(tm, D_MODEL), _rows(tm, D_MODEL), _rows(tm, D_FF), _resident((1, D_MODEL)),
                  _packed_weight(512, off["w_up"]), _packed_weight(512, off["w_down"])],
        out_specs=(_rows(tm, D_MODEL), _rows(tm, D_FF), _rows(tm, D_MODEL), pl.BlockSpec((1, D_MODEL), lambda i: (0, 0)),
                   _rows(tm, D_MODEL)),
        compiler_params=_params(("arbitrary",)),
    )(dx2, x1, u, g_mlp, wpack, wpack)


def merge_bwd(dx1b, yab, zp, w_oa_t, w_ob_t, wpack, off, tm, comm=None):
    t = dx1b.shape[0]
    grid = (t // tm,)
    c_ins, c_in_specs, c_outs, c_sems, alias = _comm_parts(comm, 7, 5)

    def body(*refs):
        ((dx1_ref, y_ref, ga_ref, gb_ref, woa_ref, wob_ref, wo_ref), cin,
         (doa_o, dob_o, dg_o, dya_o, dyb_o), cout, _, csem) = _split_refs(refs, 7, 5, 0, comm)
        _comm_start(comm, cin, cout, csem, grid)
        dm = _dot_nt(dx1_ref[...], _wrows(wo_ref, 0, D_MODEL))
        for g_ref, w_ref, do_o, dy_o, col in ((ga_ref, woa_ref, doa_o, dya_o, 0), (gb_ref, wob_ref, dob_o, dyb_o, 1)):
            cols = slice(col * D_MODEL, (col + 1) * D_MODEL)
            sg = jax.nn.sigmoid(g_ref[...])
            dyv = (dm * sg).astype(BF16)
            dy_o[...] = dyv
            dg_o[:, cols] = (dm * y_ref[:, cols].astype(F32) * sg * (1.0 - sg)).astype(BF16)
            do_o[...] = _dot_nn(dyv, w_ref[...]).astype(BF16)
        _comm_finish(comm, cin, cout, csem, grid)

    bf = jax.ShapeDtypeStruct((t, D_MODEL), BF16)
    return pl.pallas_call(
        body, name="merge_bwd", grid=grid,
        out_shape=(bf, bf, jax.ShapeDtypeStruct((t, 2 * D_MODEL), BF16), bf, bf, *c_outs),
        in_specs=[_rows(tm, D_MODEL), _rows(tm, 2 * D_MODEL), _rows(tm, 1024, 1), _rows(tm, 1024, 2),
                  _resident((D_MODEL, H_A * HP)), _resident((D_MODEL, H_B * HP)), _packed_weight(128, off["w_o"]), *c_in_specs],
        out_specs=(_rows(tm, D_MODEL), _rows(tm, D_MODEL), _rows(tm, 2 * D_MODEL), _rows(tm, D_MODEL), _rows(tm, D_MODEL),
                   *([_ANY] * len(c_outs))),
        scratch_shapes=c_sems, input_output_aliases=alias,
        compiler_params=_params(("arbitrary",)),
    )(dx1b, yab, zp, zp, w_oa_t, w_ob_t, wpack, *c_ins)


def attn_bwd(q, k, v, do, o, lse, n_b, s_len, tq, name, comm=None):
    t = q.shape[0]
    n_h, n_hk = q.shape[1] // HP, k.shape[1] // HP
    grp = n_h // n_hk
    nq = s_len // tq
    sub = min(tq, 256)
    grid = (n_b, n_hk, grp, nq)
    c_ins, c_in_specs, c_outs, c_sems, alias = _comm_parts(comm, 6, 3)

    def body(*refs):
        ((q_ref, k_ref, v_ref, do_ref, o_ref, lse_ref), cin, (dq_o, dk_o, dv_o), cout, (p_s, ds_s, dk_acc, dv_acc),
         csem) = _split_refs(refs, 6, 3, 4, comm)
        _comm_start(comm, cin, cout, csem, grid)

        @pl.when((pl.program_id(2) == 0) & (pl.program_id(3) == 0))
        def _():
            dk_acc[...] = jnp.zeros_like(dk_acc)
            dv_acc[...] = jnp.zeros_like(dv_acc)

        kv, vv = k_ref[...], v_ref[...]
        for r in range(tq // sub):
            rows = slice(r * sub, (r + 1) * sub)
            qv, dov = q_ref[rows, :], do_ref[rows, :]
            delta = jnp.sum(dov.astype(F32) * o_ref[rows, :].astype(F32), axis=-1, keepdims=True)
            p = jnp.exp(_dot_nt(qv, kv) - lse_ref[rows, 0:1])
            ds = (p * (_dot_nt(dov, vv) - delta)).astype(BF16)
            p_s[rows, :] = p.astype(BF16)
            ds_s[rows, :] = ds
            dq_o[rows, :] = _dot_nn(ds, kv).astype(dq_o.dtype)
        dk_acc[...] += _dot_tn(ds_s[...], q_ref[...])
        dv_acc[...] += _dot_tn(p_s[...], do_ref[...])

        @pl.when((pl.program_id(2) == grp - 1) & (pl.program_id(3) == nq - 1))
        def _():
            dk_o[...] = dk_acc[...].astype(dk_o.dtype)
            dv_o[...] = dv_acc[...].astype(dv_o.dtype)

        _comm_finish(comm, cin, cout, csem, grid)

    qspec = pl.BlockSpec((tq, HP), lambda b, hk, g, i: (b * nq + i, hk * grp + g))
    kspec = pl.BlockSpec((s_len, HP), lambda b, hk, g, i: (b, hk))
    return pl.pallas_call(
        body, name=name, grid=grid,
        out_shape=(jax.ShapeDtypeStruct((t, n_h * HP), BF16), jax.ShapeDtypeStruct((t, n_hk * HP), BF16),
                   jax.ShapeDtypeStruct((t, n_hk * HP), BF16), *c_outs),
        in_specs=[qspec, kspec, kspec, qspec, qspec, qspec, *c_in_specs],
        out_specs=(qspec, kspec, kspec, *([_ANY] * len(c_outs))),
        scratch_shapes=[pltpu.VMEM((tq, s_len), BF16), pltpu.VMEM((tq, s_len), BF16),
                        pltpu.VMEM((s_len, HP), F32), pltpu.VMEM((s_len, HP), F32), *c_sems],
        input_output_aliases=alias,
        compiler_params=_params(("arbitrary", "arbitrary", "arbitrary", "arbitrary")),
    )(q, k, v, do, o, lse, *c_ins)


def prep_bwd(dqa, dka, dva, dqb, dkb, dvb, zp, tabs, g_qa, g_kva, g_qn, g_kn, w_qb_t, w_kvb_t, tm, s_len):
    t = zp.shape[0]
    nsb = s_len // tm
    scale_a = (QK_NOPE + QK_ROPE) ** -0.5
    scale_b = HD_B ** -0.5

    def body(dqa_ref, dka_ref, dva_ref, dqb_ref, dkb_ref, dvb_ref, qb_ref, qlat_ref, kb_ref, ckv_ref, tab_ref,
             gqa_ref, gkva_ref, gqn_ref, gkn_ref, wqb_ref, wkvb_ref, dzq_o, dsm_o, dqap_o, dkva_o, st_o):
        @pl.when(pl.program_id(0) == 0)
        def _():
            st_o[...] = jnp.zeros_like(st_o)

        ca, s1a, s2a = tab_ref[0], tab_ref[1], tab_ref[2]
        ck = tab_ref[3]
        cb, s1b, s2b = tab_ref[4], tab_ref[5], tab_ref[6]
        for h in range(H_A):
            sl = slice(h * HP, (h + 1) * HP)
            dqap_o[:, sl] = _rope_bwd(dqa_ref[:, sl].astype(F32) * scale_a, ca, s1a, s2a).astype(BF16)
        dcq = _dot_nn(dqap_o[...], wqb_ref[...])
        ql = qlat_ref[...]
        rq = _rstd(ql, Q_LORA)
        xh = ql * rq
        gqa = gqa_ref[...]
        st_o[0:1, :] += _colsum(dcq * xh)
        dsm_o[:, 0:256] = _rms_bwd(dcq, xh, rq, gqa, Q_LORA).astype(BF16)
        dkpe = jnp.zeros((tm, HP), F32)
        for h in range(H_A):
            sl = slice(h * HP, (h + 1) * HP)
            dk = dka_ref[:, sl]
            dkpe = dkpe + dk.astype(F32)
            dkva_o[:, sl] = dk.astype(BF16)
        dkva_o[:, H_A * HP:] = dva_ref[...].astype(BF16)
        dsm_o[:, 896:1024] = _rope_bwd(dkpe, ck, s1a, s2a).astype(BF16)
        dckv = _dot_nn(dkva_o[...], wkvb_ref[...])
        cr = ckv_ref[...]
        rk = _rstd(cr, KV_LORA)
        xh = cr * rk
        st_o[1:2, 0:128] += _colsum(dckv * xh)
        dsm_o[:, 768:896] = _rms_bwd(dckv, xh, rk, gkva_ref[...], KV_LORA).astype(BF16)
        gqn, gkn = gqn_ref[...], gkn_ref[...]
        dgq = jnp.zeros((1, HP), F32)
        for h in range(H_B):
            sl = slice(h * HP, (h + 1) * HP)
            dy = _rope_bwd(dqb_ref[:, sl].astype(F32) * scale_b, cb, s1b, s2b)
            xs = qb_ref[:, sl]
            r = _rstd(xs, HD_B)
            xh = xs * r
            dgq = dgq + _colsum(dy * xh)
            dzq_o[:, sl] = _rms_bwd(dy, xh, r, gqn, HD_B).astype(BF16)
        st_o[2:3, 0:128] += dgq
        dgk = jnp.zeros((1, HP), F32)
        for h in range(KV_B):
            sl = slice(h * HP, (h + 1) * HP)
            dy = _rope_bwd(dkb_ref[:, sl].astype(F32), cb, s1b, s2b)
            xs = kb_ref[:, sl]
            r = _rstd(xs, HD_B)
            xh = xs * r
            dgk = dgk + _colsum(dy * xh)
            dsm_o[:, 256 + h * HP:256 + (h + 1) * HP] = _rms_bwd(dy, xh, r, gkn, HD_B).astype(BF16)
        st_o[3:4, 0:128] += dgk
        dsm_o[:, 512:768] = dvb_ref[...].astype(BF16)

    bf = jax.ShapeDtypeStruct((t, 1024), BF16)
    return pl.pallas_call(
        body, name="prep_bwd", grid=(t // tm,),
        out_shape=(bf, bf, bf, jax.ShapeDtypeStruct((t, 2048), BF16), jax.ShapeDtypeStruct((4, 256), F32)),
        in_specs=[_rows(tm, 1024), _rows(tm, 1024), _rows(tm, 1024), _rows(tm, 1024), _rows(tm, 256), _rows(tm, 256),
                  _rows(tm, 1024, 0), _rows(tm, 256, 12), _rows(tm, 256, 13), _rows(tm, 128, 30),
                  pl.BlockSpec((7, tm, HP), lambda i: (0, i % nsb, 0)),
                  _resident((1, Q_LORA)), _resident((1, KV_LORA)), _resident((1, HP)), _resident((1, HP)),
                  _resident((H_A * HP, Q_LORA)), _resident((2 * H_A * HP, KV_LORA))],
        out_specs=(_rows(tm, 1024), _rows(tm, 1024), _rows(tm, 1024), _rows(tm, 2048), pl.BlockSpec((4, 256), lambda i: (0, 0))),
        compiler_params=_params(("arbitrary",)),
    )(dqa, dka, dva, dqb, dkb, dvb, zp, zp, zp, zp, tabs, g_qa, g_kva, g_qn, g_kn, w_qb_t, w_kvb_t)


def in_bwd(dzq, dgab, dsm, x, dx1, g_mix, w_in_t, tm, comm=None):
    t = x.shape[0]
    grid = (t // tm,)
    c_ins, c_in_specs, c_outs, c_sems, alias = _comm_parts(comm, 7, 2)

    def body(*refs):
        ((dzq_ref, dg_ref, dsm_ref, x_ref, dx1_ref, g_ref, w_ref), cin, (dx_o, st_o), cout, _,
         csem) = _split_refs(refs, 7, 2, 0, comm)
        _comm_start(comm, cin, cout, csem, grid)

        @pl.when(pl.program_id(0) == 0)
        def _():
            st_o[...] = jnp.zeros_like(st_o)

        dh = _dot_nn(dzq_ref[...], w_ref[0:1024, :])
        dh = dh + _dot_nn(dg_ref[...], w_ref[1024:3072, :])
        dh = dh + _dot_nn(dsm_ref[...], w_ref[3072:4096, :])
        xv = x_ref[...]
        g = g_ref[...]
        r = _rstd(xv, D_MODEL)
        xh = xv * r
        st_o[...] += _colsum(dh * xh)
        dx_o[...] = dx1_ref[...] + _rms_bwd(dh, xh, r, g, D_MODEL)
        _comm_finish(comm, cin, cout, csem, grid)

    return pl.pallas_call(
        body, name="in_bwd", grid=grid,
        out_shape=(jax.ShapeDtypeStruct((t, D_MODEL), F32), jax.ShapeDtypeStruct((1, D_MODEL), F32), *c_outs),
        in_specs=[_rows(tm, 1024), _rows(tm, 2048), _rows(tm, 1024), _rows(tm, D_MODEL), _rows(tm, D_MODEL),
                  _resident((1, D_MODEL)), _resident((ZP, D_MODEL)), *c_in_specs],
        out_specs=(_rows(tm, D_MODEL), pl.BlockSpec((1, D_MODEL), lambda i: (0, 0)), *([_ANY] * len(c_outs))),
        scratch_shapes=c_sems, input_output_aliases=alias,
        compiler_params=_params(("arbitrary",)),
    )(dzq, dgab, dsm, x, dx1, g_mix, w_in_t, *c_ins)


def matmul_tn(a, b, name, square_a=False):
    t, m = a.shape
    n = b.shape[1]
    bm = min(m, 512)
    tk = min(t, 4096)

    def body(a_ref, b_ref, o_ref):
        @pl.when(pl.program_id(1) == 0)
        def _():
            o_ref[...] = jnp.zeros_like(o_ref)

        av = a_ref[...]
        if square_a:
            av = (av.astype(F32) * av.astype(F32))
        o_ref[...] += _dot_tn(av.astype(BF16), b_ref[...].astype(BF16))

    return pl.pallas_call(
        body, name=name, grid=(m // bm, t // tk), out_shape=jax.ShapeDtypeStruct((m, n), F32),
        in_specs=[pl.BlockSpec((tk, bm), lambda i, kk: (kk, i)), pl.BlockSpec((tk, n), lambda i, kk: (kk, 0))],
        out_specs=pl.BlockSpec((bm, n), lambda i, kk: (i, 0)),
        compiler_params=_params(("parallel", "arbitrary")),
    )(a, b)


def matmul_tn_packed(a, b, name, rows, row_off, total_rows, buf=None, square_a=False):
    t, m = a.shape
    n = b.shape[1]
    pd = max(1, 512 // rows)
    bm = pd * rows
    tk = min(t, 4096)
    nk = t // tk

    def body(a_ref, b_ref, *rest):
        o_ref, acc = rest[-2], rest[-1]

        @pl.when(pl.program_id(1) == 0)
        def _():
            acc[...] = jnp.zeros_like(acc)

        av = a_ref[...]
        if square_a:
            av = (av.astype(F32) * av.astype(F32))
        acc[...] += _dot_tn(av.astype(BF16), b_ref[...].astype(BF16))

        @pl.when(pl.program_id(1) == nk - 1)
        def _():
            o_ref[...] = acc[...].reshape(pd, rows, n).astype(o_ref.dtype)

    in_specs = [pl.BlockSpec((tk, bm), lambda i, kk: (kk, i)), pl.BlockSpec((tk, n), lambda i, kk: (kk, 0))]
    args = [a, b]
    if buf is not None:
        in_specs.append(_ANY)
        args.append(buf)
    return pl.pallas_call(
        body, name=name, grid=(m // bm, nk), out_shape=jax.ShapeDtypeStruct((N_DEV, total_rows, n), BF16),
        in_specs=in_specs, out_specs=pl.BlockSpec((pd, rows, n), lambda i, kk: (i, row_off // rows, 0)),
        scratch_shapes=[pltpu.VMEM((bm, n), F32)], input_output_aliases={2: 0} if buf is not None else {},
        compiler_params=_params(("parallel", "arbitrary")),
    )(*args)


def adamw(w, g, m, v, name, g_transposed=False):
    _, r, c = w.shape
    tr = r if (g_transposed or r <= 256) else 256
    c1 = 1.0 - ADAM_B1 ** ADAM_STEP
    c2 = 1.0 - ADAM_B2 ** ADAM_STEP

    def body(w_ref, g_ref, m_ref, v_ref, g_o, d_o, m_o, v_o):
        gv = g_ref[...].T if g_transposed else g_ref[...]
        mn = ADAM_B1 * m_ref[0] + (1.0 - ADAM_B1) * gv
        vn = ADAM_B2 * v_ref[0] + (1.0 - ADAM_B2) * (gv * gv)
        g_o[0] = gv
        m_o[0] = mn
        v_o[0] = vn
        d_o[0] = -ADAM_LR * ((mn / c1) / (jnp.sqrt(vn / c2) + ADAM_EPS) + ADAM_WD * w_ref[0])

    spec = pl.BlockSpec((1, tr, c), lambda i: (0, i, 0))
    gspec = pl.BlockSpec((c, r), lambda i: (0, 0)) if g_transposed else pl.BlockSpec((tr, c), lambda i: (i, 0))
    shp = jax.ShapeDtypeStruct((1, r, c), F32)
    return pl.pallas_call(
        body, name=name, grid=(r // tr,), out_shape=(shp,) * 4, in_specs=[spec, gspec, spec, spec], out_specs=(spec,) * 4,
        compiler_params=_params(("parallel",)),
    )(w, g, m, v)


def _rope_tables(s_len):
    def angles(pos, dim):
        inv = np.float32(ROPE_THETA) ** (-np.arange(0, dim, 2, dtype=np.float32) / np.float32(dim))
        return pos.astype(np.float32)[:, None] * inv[None, :]

    tpos = np.arange(s_len)
    a1 = angles(tpos, QK_ROPE)
    ar = angles(tpos // GRID_W, HD_B // 2)
    ac = angles(tpos % GRID_W, HD_B // 2)
    z16 = np.zeros((s_len, 16), np.float32)
    z32 = np.zeros((s_len, 32), np.float32)
    z64 = np.zeros((s_len, 64), np.float32)
    one64 = np.ones((s_len, 64), np.float32)
    c1, s1 = np.cos(a1), np.sin(a1)
    ca = np.concatenate([one64, c1, c1, z32], axis=1)
    ck = np.concatenate([z64, c1, c1, z32], axis=1)
    s1a = np.concatenate([z64, -s1, z16, z32], axis=1)
    s2a = np.concatenate([z64, z16, s1, z32], axis=1)
    cr, sr, cc, sc = np.cos(ar), np.sin(ar), np.cos(ac), np.sin(ac)
    cb = np.concatenate([cr, cr, cc, cc, z64], axis=1)
    s1b = np.concatenate([-sr, z16, -sc, z16, z64], axis=1)
    s2b = np.concatenate([z16, sr, z16, sc, z64], axis=1)
    return jnp.asarray(np.stack([ca, s1a, s2a, ck, cb, s1b, s2b]).astype(np.float32))


def _pad_heads(a, n_heads, axis):
    shp = a.shape
    a = a.reshape(shp[:axis] + (n_heads, shp[axis] // n_heads) + shp[axis + 1:])
    pad = [(0, 0)] * a.ndim
    pad[axis + 1] = (0, HP - a.shape[axis + 1])
    a = jnp.pad(a, pad)
    return a.reshape(shp[:axis] + (n_heads * HP,) + shp[axis + 1:])


def _unpad_heads(a, n_heads, width, axis):
    shp = a.shape
    a = a.reshape(shp[:axis] + (n_heads, HP) + shp[axis + 1:])
    a = lax.slice_in_dim(a, 0, width, axis=axis + 1)
    return a.reshape(shp[:axis] + (n_heads * width,) + shp[axis + 1:])


def _pack_rows(blocks, names):
    parts = []
    for name in names:
        b = blocks[name]
        padr = PACK_ROWS[name] - b.shape[-2]
        if padr:
            b = jnp.pad(b, [(0, 0)] * (b.ndim - 2) + [(0, padr), (0, 0)])
        parts.append(b)
    return jnp.concatenate(parts, axis=parts[0].ndim - 2)


def _expand_w_in(wt):
    z64 = jnp.zeros((64, D_MODEL), wt.dtype)
    z32 = jnp.zeros((32, D_MODEL), wt.dtype)
    return jnp.concatenate([
        _pad_heads(wt[416:928], H_B, 0), wt[1184:2208], wt[2208:3232], wt[0:256],
        _pad_heads(wt[928:1056], KV_B, 0), _pad_heads(wt[1056:1184], KV_B, 0), wt[256:384],
        z64, wt[384:416], z32], axis=0)


def _collapse_w_in(dq, dg, ds):
    return jnp.concatenate([
        ds[0:256], ds[768:896], ds[960:992], _unpad_heads(dq, H_B, HD_B, 0), _unpad_heads(ds[256:512], KV_B, HD_B, 0),
        _unpad_heads(ds[512:768], KV_B, HD_B, 0), dg], axis=0)


def kernel(x, p, g_mix, w_in, g_qa, w_qb, g_kva, w_kvb, g_qn, g_kn, w_oa, w_ob, w_o, g_mlp, w_up, w_down, g_ple, w_ple_gate, w_ple, g_final, loss_target, m_g_mix, m_w_in, m_g_qa, m_w_qb, m_g_kva, m_w_kvb, m_g_qn, m_g_kn, m_w_oa, m_w_ob, m_w_o, m_g_mlp, m_w_up, m_w_down, m_g_ple, m_w_ple_gate, m_w_ple, m_g_final, v_g_mix, v_w_in, v_g_qa, v_w_qb, v_g_kva, v_w_kvb, v_g_qn, v_g_kn, v_w_oa, v_w_ob, v_w_o, v_g_mlp, v_w_up, v_w_down, v_g_ple, v_w_ple_gate, v_w_ple, v_g_final):
    n_b, s_len, _ = x.shape
    t = n_b * s_len
    tm = min(512, s_len)
    tq_f = min(1024, s_len)
    tq_b = min(1024, s_len)

    mats = dict(w_in=(w_in, m_w_in, v_w_in), w_qb=(w_qb, m_w_qb, v_w_qb), w_kvb=(w_kvb, m_w_kvb, v_w_kvb),
                w_oa=(w_oa, m_w_oa, v_w_oa), w_ob=(w_ob, m_w_ob, v_w_ob), w_o=(w_o, m_w_o, v_w_o),
                w_up=(w_up, m_w_up, v_w_up), w_down=(w_down, m_w_down, v_w_down),
                w_ple_gate=(w_ple_gate, m_w_ple_gate, v_w_ple_gate), w_ple=(w_ple, m_w_ple, v_w_ple))
    col_sharded = ("w_in", "w_qb", "w_kvb", "w_oa", "w_ob", "w_up", "w_ple")

    blocks = {}
    for name in PACK_W1 + ("w_oa", "w_ob", "w_ple"):
        blocks[name] = mats[name][0][0].T.reshape(-1, D_MODEL).astype(BF16)
    off_w1, _ = _pack_offsets(PACK_W1)
    off_w2, _ = _pack_offsets(PACK_W2)
    xf = x.reshape(t, D_MODEL)
    h, full1 = norm_x(xf, g_mix, tm, comm=gather_first_comm(_pack_rows(blocks, PACK_W1)))
    pack2, full1 = pack_late_weights(w_up, w_down, w_o, w_ple_gate, _pack_rows(blocks, ("w_oa", "w_ob", "w_ple")),
                                     comm=gather_pass_comm(full1))

    def gathered(full, offs, name, rows, width):
        return full[:, offs[name]:offs[name] + rows].reshape(-1, width)

    w_in_t = _expand_w_in(gathered(full1, off_w1, "w_in", 404, D_MODEL))
    w_qb_t = _pad_heads(gathered(full1, off_w1, "w_qb", 24, Q_LORA), H_A, 0)
    wkvb = gathered(full1, off_w1, "w_kvb", 16, KV_LORA).reshape(H_A, 2, 64, KV_LORA)
    w_kvb_t = jnp.concatenate([_pad_heads(wkvb[:, 0].reshape(-1, KV_LORA), H_A, 0),
                               _pad_heads(wkvb[:, 1].reshape(-1, KV_LORA), H_A, 0)], axis=0)

    tabs = _rope_tables(s_len)
    g_qn_p = jnp.pad(g_qn, ((0, 0), (0, HP - HD_B)))
    g_kn_p = jnp.pad(g_kn, ((0, 0), (0, HP - HD_B)))
    pf = p.reshape(t, PLE_DIM)
    tgt = loss_target.reshape(t, D_MODEL)

    zp = in_proj(h, w_in_t, tm)
    qa, ka, va, qb, kb, vb, cq, ckv = attn_prep(zp, tabs, g_qa, g_kva, g_qn_p, g_kn_p, w_qb_t, w_kvb_t, tm, s_len)
    oa, lse_a, full2 = attn_fwd(qa, ka, va, n_b, s_len, tq_f, "attn_a_fwd", comm=gather_first_comm(pack2))
    ob, lse_b, full2 = attn_fwd(qb, kb, vb, n_b, s_len, tq_f, "attn_b_fwd", comm=gather_pass_comm(full2))
    w_oa_t = _pad_heads(gathered(full2, off_w2, "w_oa", 64, H_A * V_DIM_A), H_A, 1)
    w_ob_t = _pad_heads(gathered(full2, off_w2, "w_ob", 64, H_B * HD_B), H_B, 1)
    w_ple_t = gathered(full2, off_w2, "w_ple", 32, PLE_DIM)
    x1, merged, yab = merge_fwd(oa, ob, zp, xf, w_oa_t, w_ob_t, full2, off_w2, tm)
    x2, u = mlp_fwd(x1, g_mlp, full2, off_w2, tm)
    dx2, dt, h3, dpe, st_ple, dx2b = ple_loss_bwd(x2, pf, tgt, g_ple, g_final.reshape(1, D_MODEL), full2, off_w2, w_ple_t, tm)
    dx1, da, h2, st_mlp, dx1b = mlp_bwd(dx2, x1, u, g_mlp, full2, off_w2, tm)

    core = lax.axis_index("c").astype(jnp.int32).reshape(1)
    chip = (2 * lax.axis_index("x") + lax.axis_index("y")).astype(jnp.int32).reshape(1)

    def packed(gblocks, names):
        return _pack_rows({n: gblocks[n].reshape(N_DEV, -1, D_MODEL).astype(BF16) for n in names}, names)

    off_g1, rows_g1 = _pack_offsets(PACK_G1)
    gpack1 = matmul_tn_packed(da, h2, "gw_up", 512, off_g1["w_up"], rows_g1)
    gpack1 = matmul_tn_packed(u, dx2b, "gw_down", 512, off_g1["w_down"], rows_g1, buf=gpack1, square_a=True)
    gpack1 = matmul_tn_packed(h3, dt, "gw_pg", 128, off_g1["w_ple_gate"], rows_g1, buf=gpack1)
    gple = matmul_tn(dpe, pf, "gw_ple").reshape(N_DEV, -1, D_MODEL).astype(BF16)
    gpack1 = lax.dynamic_update_slice(gpack1, gple, (0, off_g1["w_ple"], 0))
    doa, dob, dgab, dya, dyb, got1 = merge_bwd(dx1b, yab, zp, w_oa_t, w_ob_t, full2, off_w2, tm,
                                               comm=scatter_sibling_comm(gpack1))
    part1 = add_pairs(gpack1, got1, core)
    dqa, dka, dva, land1 = attn_bwd(qa, ka, va, doa, oa, lse_a, n_b, s_len, tq_b, "attn_a_bwd", comm=scatter_chips_comm(part1))
    gshard1 = sum_chips(part1, land1, chip)

    off_g2, rows_g2 = _pack_offsets(PACK_G2)
    g2 = dict(w_oa=_unpad_heads(matmul_tn(dya, oa, "gw_oa"), H_A, V_DIM_A, 1),
              w_ob=_unpad_heads(matmul_tn(dyb, ob, "gw_ob"), H_B, HD_B, 1))
    gpack2 = matmul_tn_packed(merged, dx1b, "gw_o", 128, off_g2["w_o"], rows_g2)
    gpack2 = lax.dynamic_update_slice(gpack2, packed(g2, ("w_oa", "w_ob")), (0, off_g2["w_oa"], 0))
    part2 = add_pairs(gpack2, exchange_sibling(gpack2), core)
    dqb, dkb, dvb, land2 = attn_bwd(qb, kb, vb, dob, ob, lse_b, n_b, s_len, tq_b, "attn_b_bwd", comm=scatter_chips_comm(part2))
    gshard2 = sum_chips(part2, land2, chip)
    dzq, dsm, dqap, dkva, st_prep = prep_bwd(dqa, dka, dva, dqb, dkb, dvb, zp, tabs, g_qa, g_kva, g_qn_p, g_kn_p,
                                             w_qb_t, w_kvb_t, tm, s_len)

    gkv = matmul_tn(dkva, ckv, "gw_kvb")
    g3 = dict(
        w_in=_collapse_w_in(matmul_tn(dzq, h, "gw_in_q"), matmul_tn(dgab, h, "gw_in_g"), matmul_tn(dsm, h, "gw_in_s")),
        w_qb=_unpad_heads(matmul_tn(dqap, cq, "gw_qb"), H_A, QK_NOPE + QK_ROPE, 0),
        w_kvb=jnp.stack([_unpad_heads(gkv[:H_A * HP], H_A, 64, 0).reshape(H_A, 64, KV_LORA),
                         _unpad_heads(gkv[H_A * HP:], H_A, 64, 0).reshape(H_A, 64, KV_LORA)], axis=1))
    gpack3 = packed(g3, PACK_G3)
    part3 = add_pairs(gpack3, exchange_sibling(gpack3), core)
    grad_x, st_mix, land3 = in_bwd(dzq, dgab, dsm, xf, dx1, g_mix, w_in_t, tm, comm=scatter_chips_comm(part3))
    gshard3 = sum_chips(part3, land3, chip)
    off_g3, _ = _pack_offsets(PACK_G3)
    shards = {n: (gshard1, off_g1[n]) for n in PACK_G1}
    shards.update({n: (gshard2, off_g2[n]) for n in PACK_G2})
    shards.update({n: (gshard3, off_g3[n]) for n in PACK_G3})

    stats = allreduce_stats(st_mix, st_prep, st_mlp, st_ple)
    loss = jnp.sum(stats[ST_LOSS])

    out_g, out_d, out_m, out_v = {}, {}, {}, {}
    for name, (w, m, v) in mats.items():
        gshard, off = shards[name]
        r, c = w.shape[1:]
        if name in col_sharded:
            g2 = gshard[off:off + (r * c) // D_MODEL].reshape(c, r)
            in_kernel = r % 128 == 0 and c % 128 == 0
            res = adamw(w, g2 if in_kernel else g2.T, m, v, "adamw_" + name, g_transposed=in_kernel)
        else:
            res = adamw(w, gshard[off:off + r], m, v, "adamw_" + name)
        out_g[name], out_d[name], out_m[name], out_v[name] = res

    gains = (("g_mix", g_mix, m_g_mix, v_g_mix, ST_G_MIX), ("g_qa", g_qa, m_g_qa, v_g_qa, ST_G_QA),
             ("g_kva", g_kva, m_g_kva, v_g_kva, ST_G_KVA), ("g_qn", g_qn, m_g_qn, v_g_qn, ST_G_QN),
             ("g_kn", g_kn, m_g_kn, v_g_kn, ST_G_KN), ("g_mlp", g_mlp, m_g_mlp, v_g_mlp, ST_G_MLP),
             ("g_ple", g_ple, m_g_ple, v_g_ple, ST_G_PLE), ("g_final", g_final, m_g_final, v_g_final, ST_G_FINAL))
    res = adamw_gains(stats, [(r_, w.reshape(1, -1), m.reshape(1, -1), v.reshape(1, -1)) for _, w, m, v, r_ in gains])
    for (name, w, _, _, _), (gg, gd, gm, gv) in zip(gains, res):
        out_g[name], out_d[name], out_m[name], out_v[name] = (a.reshape(w.shape) for a in (gg, gd, gm, gv))

    order = ("g_mix", "w_in", "g_qa", "w_qb", "g_kva", "w_kvb", "g_qn", "g_kn", "w_oa", "w_ob", "w_o", "g_mlp",
             "w_up", "w_down", "g_ple", "w_ple_gate", "w_ple", "g_final")
    return (loss, grad_x.reshape(x.shape), *[out_g[n] for n in order], *[out_d[n] for n in order],
            *[out_m[n] for n in order], *[out_v[n] for n in order])
```

```python
import numpy as np
import jax
import jax.numpy as jnp
from jax import lax
from jax.experimental import pallas as pl
from jax.experimental.pallas import tpu as pltpu

F32 = jnp.float32
BF16 = jnp.bfloat16

D_MODEL = 1024
EPS = 1e-6
ROPE_THETA = 10000.0
GRID_W = 64
H_A = 8
QK_NOPE = 64
QK_ROPE = 32
V_DIM_A = 64
Q_LORA = 256
KV_LORA = 128
H_B = 8
KV_B = 2
HD_B = 64
D_FF = 4 * D_MODEL
PLE_DIM = 256
HP = 128
ZP = 4096
N_DEV = 8
N_CHIP = 4

ADAM_LR = 0.001
ADAM_B1 = 0.9
ADAM_B2 = 0.999
ADAM_EPS = 1e-08
ADAM_WD = 0.01
ADAM_STEP = 10

VMEM_LIMIT = 52 * 1024 * 1024

PACK_ROWS = dict(w_in=416, w_qb=32, w_kvb=16, w_oa=64, w_ob=64, w_o=128, w_up=512, w_down=512, w_ple_gate=128, w_ple=32)
PACK_W1 = ("w_in", "w_qb", "w_kvb")
PACK_W2 = ("w_up", "w_down", "w_o", "w_ple_gate", "w_oa", "w_ob", "w_ple")
PACK_G1 = ("w_up", "w_down", "w_ple_gate", "w_ple")
PACK_G2 = ("w_o", "w_oa", "w_ob")
PACK_G3 = ("w_in", "w_qb", "w_kvb")


def _pack_offsets(names):
    off, o = {}, 0
    for n in names:
        off[n] = o
        o += PACK_ROWS[n]
    return off, o

ST_G_MIX, ST_G_QA, ST_G_KVA, ST_G_QN, ST_G_KN, ST_G_MLP, ST_G_PLE, ST_G_FINAL, ST_LOSS = range(9)
ST_ROWS = 16


def _dot_nn(a, b):
    return lax.dot_general(a, b, (((1,), (0,)), ((), ())), preferred_element_type=F32)


def _dot_nt(a, b):
    return lax.dot_general(a, b, (((1,), (1,)), ((), ())), preferred_element_type=F32)


def _dot_tn(a, b):
    return lax.dot_general(a, b, (((0,), (0,)), ((), ())), preferred_element_type=F32)


def _rstd(x, n):
    return lax.rsqrt(jnp.sum(x * x, axis=-1, keepdims=True) * (1.0 / n) + EPS)


def _rms_bwd(dy, xh, r, g, n):
    dxh = dy * g
    return r * (dxh - xh * (jnp.sum(dxh * xh, axis=-1, keepdims=True) * (1.0 / n)))


def _rope_fwd(x, c, s1, s2):
    return x * c + pltpu.roll(x, HP - 16, 1) * s1 + pltpu.roll(x, 16, 1) * s2


def _rope_bwd(d, c, s1, s2):
    return d * c + pltpu.roll(d * s1, 16, 1) + pltpu.roll(d * s2, HP - 16, 1)


def _colsum(v):
    return jnp.sum(v, axis=0, keepdims=True)


def _params(sem=None, vmem=VMEM_LIMIT):
    return pltpu.CompilerParams(dimension_semantics=sem, vmem_limit_bytes=vmem)


def _resident(shape):
    nd = len(shape)
    return pl.BlockSpec(shape, lambda *_: (0,) * nd, pipeline_mode=pl.Buffered(1))


def _rows(tm, width, col=0):
    return pl.BlockSpec((tm, width), lambda i: (i, col))


def _packed_weight(rows, off):
    return pl.BlockSpec((N_DEV, rows, D_MODEL), lambda *_: (0, off // rows, 0), pipeline_mode=pl.Buffered(1))


def _wrows(ref, start, size):
    rows = ref.shape[1]
    return ref[start // rows:(start + size) // rows].reshape(size, D_MODEL)


def _weight_fetches(pack_ref, weights, sems, n_chunks):
    planes = N_DEV // n_chunks
    out = []
    for c in range(n_chunks):
        sl = slice(c * planes, (c + 1) * planes)
        out.append([pltpu.make_async_copy(pack_ref.at[sl, off:off + dst.shape[1], :], dst.at[sl], sems.at[c * len(weights) + j])
                    for j, (off, dst) in enumerate(weights)])
    return out


def _wait_when(cond, copies):
    @pl.when(cond)
    def _():
        for cp in copies:
            cp.wait()


def _mesh_pos():
    return lax.axis_index("x"), lax.axis_index("y"), lax.axis_index("c")


def _flip(v, bit):
    return (1 - v) if bit else v


_ANY = pl.BlockSpec(memory_space=pl.ANY)
_MESH = pl.DeviceIdType.MESH


def _remote(src, dst, send_sems, recv_sems, k, to):
    return pltpu.make_async_remote_copy(src_ref=src, dst_ref=dst, send_sem=send_sems.at[k], recv_sem=recv_sems.at[k],
                                        device_id=to, device_id_type=_MESH)


def _sibling_copies(g_ref, got_ref, send_sems, recv_sems):
    x, y, c = _mesh_pos()
    return [_remote(g_ref.at[2 * j + (1 - c)], got_ref.at[j], send_sems, recv_sems, j, (x, y, 1 - c)) for j in range(N_CHIP)]


def _chip_copies(p_ref, land_ref, send_sems, recv_sems):
    x, y, c = _mesh_pos()
    copies = []
    for k in (1, 2, 3):
        tx, ty = _flip(x, k & 2), _flip(y, k & 1)
        copies.append(_remote(p_ref.at[2 * tx + ty], land_ref.at[k - 1], send_sems, recv_sems, k - 1, (tx, ty, c)))
    return copies


class _Comm:
    def __init__(self, ins, out_shapes, sems, make, aliases=None):
        self.ins, self.out_shapes, self.sems, self.make, self.aliases = list(ins), list(out_shapes), list(sems), make, aliases or {}


def _comm_parts(comm, n_in, n_out):
    if comm is None:
        return [], [], [], [], {}
    alias = {n_in + j: n_out + k for j, k in comm.aliases.items()}
    return comm.ins, [_ANY] * len(comm.ins), comm.out_shapes, comm.sems, alias


def _split_refs(refs, n_in, n_out, n_scratch, comm):
    n_ci = len(comm.ins) if comm else 0
    n_co = len(comm.out_shapes) if comm else 0
    cuts, i = [], 0
    for n in (n_in, n_ci, n_out, n_co, n_scratch):
        cuts.append(refs[i:i + n])
        i += n
    return (*cuts, refs[i:])


def _grid_edge(grid, last):
    cond = None
    for d, n in enumerate(grid):
        here = pl.program_id(d) == (n - 1 if last else 0)
        cond = here if cond is None else cond & here
    return cond


def _comm_start(comm, cin, cout, csem, grid):
    if comm is not None:
        @pl.when(_grid_edge(grid, False))
        def _():
            for cp in comm.make(cin, cout, csem):
                cp.start()


def _comm_finish(comm, cin, cout, csem, grid):
    if comm is not None:
        @pl.when(_grid_edge(grid, True))
        def _():
            for cp in comm.make(cin, cout, csem):
                cp.wait()


def gather_first_comm(shard):
    r, w = shard.shape

    def make(cin, cout, sems):
        (x_ref,), (out_ref,), (send_sems, recv_sems, local_sem) = cin, cout, sems
        x, y, c = _mesh_pos()
        mine = out_ref.at[4 * x + 2 * y + c]
        targets = [(x, y, 1 - c), (1 - x, y, c), (x, 1 - y, c), (1 - x, 1 - y, c)]
        return [_remote(x_ref, mine, send_sems, recv_sems, k, to) for k, to in enumerate(targets)] + [
            pltpu.make_async_copy(x_ref, mine, local_sem)]

    return _Comm([shard], [jax.ShapeDtypeStruct((N_DEV, r, w), shard.dtype)],
                 [pltpu.SemaphoreType.DMA((4,)), pltpu.SemaphoreType.DMA((4,)), pltpu.SemaphoreType.DMA], make)


def gather_pass_comm(full):
    def make(cin, cout, sems):
        (in_ref,), (out_ref,), (send_sems, recv_sems) = cin, cout, sems
        x, y, c = _mesh_pos()
        copies = []
        for k, (px, py) in enumerate([(1 - x, y), (x, 1 - y), (1 - x, 1 - y)]):
            idx = 4 * px + 2 * py + c
            copies.append(_remote(in_ref.at[idx], out_ref.at[idx], send_sems, recv_sems, k, (x, y, 1 - c)))
        return copies

    return _Comm([full], [jax.ShapeDtypeStruct(full.shape, full.dtype)],
                 [pltpu.SemaphoreType.DMA((3,)), pltpu.SemaphoreType.DMA((3,))], make, aliases={0: 0})


def scatter_sibling_comm(g):
    _, r, w = g.shape
    return _Comm([g], [jax.ShapeDtypeStruct((N_CHIP, r, w), g.dtype)],
                 [pltpu.SemaphoreType.DMA((N_CHIP,)), pltpu.SemaphoreType.DMA((N_CHIP,))],
                 lambda cin, cout, sems: _sibling_copies(cin[0], cout[0], sems[0], sems[1]))


def scatter_chips_comm(part):
    _, r, w = part.shape
    return _Comm([part], [jax.ShapeDtypeStruct((N_CHIP - 1, r, w), part.dtype)],
                 [pltpu.SemaphoreType.DMA((3,)), pltpu.SemaphoreType.DMA((3,))],
                 lambda cin, cout, sems: _chip_copies(cin[0], cout[0], sems[0], sems[1]))


def exchange_sibling(g):
    _, r, w = g.shape

    def body(g_ref, got_ref, send_sems, recv_sems):
        copies = _sibling_copies(g_ref, got_ref, send_sems, recv_sems)
        for cp in copies:
            cp.start()
        for cp in copies:
            cp.wait()

    return pl.pallas_call(
        body, name="exchange_sibling", out_shape=jax.ShapeDtypeStruct((N_CHIP, r, w), g.dtype),
        in_specs=[_ANY], out_specs=_ANY,
        scratch_shapes=[pltpu.SemaphoreType.DMA((N_CHIP,)), pltpu.SemaphoreType.DMA((N_CHIP,))],
    )(g)


def exchange_chips(part):
    _, r, w = part.shape

    def body(p_ref, land_ref, send_sems, recv_sems):
        copies = _chip_copies(p_ref, land_ref, send_sems, recv_sems)
        for cp in copies:
            cp.start()
        for cp in copies:
            cp.wait()

    return pl.pallas_call(
        body, name="exchange_chips", out_shape=jax.ShapeDtypeStruct((N_CHIP - 1, r, w), part.dtype),
        in_specs=[_ANY], out_specs=_ANY,
        scratch_shapes=[pltpu.SemaphoreType.DMA((3,)), pltpu.SemaphoreType.DMA((3,))],
    )(part)


def allreduce_stats(st_mix, st_prep, st_mlp, st_ple):
    def body(mix_ref, prep_ref, mlp_ref, ple_ref, out_ref, mine, gath, send_sems, recv_sems):
        x, y, c = _mesh_pos()
        me = 4 * x + 2 * y + c
        mine[...] = jnp.zeros_like(mine)
        mine[ST_G_MIX:ST_G_MIX + 1, :] = mix_ref[...]
        mine[ST_G_QA:ST_G_KN + 1, 0:256] = prep_ref[...]
        mine[ST_G_MLP:ST_G_MLP + 1, :] = mlp_ref[...]
        mine[ST_G_PLE:ST_LOSS + 1, :] = ple_ref[...]
        gath[me] = mine[...]
        copies = []
        for k in range(1, N_DEV):
            peer = (_flip(x, k & 4), _flip(y, k & 2), _flip(c, k & 1))
            copies.append(_remote(mine, gath.at[me], send_sems, recv_sems, k - 1, peer))
        for cp in copies:
            cp.start()
        for cp in copies:
            cp.wait()
        acc = gath[0]
        for d in range(1, N_DEV):
            acc = acc + gath[d]
        out_ref[...] = acc

    vm = pl.BlockSpec(memory_space=pltpu.VMEM)
    return pl.pallas_call(
        body, name="allreduce_stats", out_shape=jax.ShapeDtypeStruct((ST_ROWS, D_MODEL), F32),
        in_specs=[vm] * 4, out_specs=vm,
        scratch_shapes=[pltpu.VMEM((ST_ROWS, D_MODEL), F32), pltpu.VMEM((N_DEV, ST_ROWS, D_MODEL), F32),
                        pltpu.SemaphoreType.DMA((N_DEV - 1,)), pltpu.SemaphoreType.DMA((N_DEV - 1,))],
    )(st_mix, st_prep, st_mlp, st_ple)


def adamw_gains(stats, gains):
    c1 = 1.0 - ADAM_B1 ** ADAM_STEP
    c2 = 1.0 - ADAM_B2 ** ADAM_STEP
    n = len(gains)

    def body(st_ref, *refs):
        ins, outs = refs[:3 * n], refs[3 * n:]
        for i, (row, w, _, _) in enumerate(gains):
            width = w.shape[1]
            gv = st_ref[row:row + 1, 0:width]
            mn = ADAM_B1 * ins[3 * i + 1][...] + (1.0 - ADAM_B1) * gv
            vn = ADAM_B2 * ins[3 * i + 2][...] + (1.0 - ADAM_B2) * (gv * gv)
            outs[4 * i][...] = gv
            outs[4 * i + 1][...] = -ADAM_LR * ((mn / c1) / (jnp.sqrt(vn / c2) + ADAM_EPS) + ADAM_WD * ins[3 * i][...])
            outs[4 * i + 2][...] = mn
            outs[4 * i + 3][...] = vn

    vm = pl.BlockSpec(memory_space=pltpu.VMEM)
    flat = [a for (_, w, m, v) in gains for a in (w, m, v)]
    out_shape = tuple(jax.ShapeDtypeStruct(w.shape, F32) for (_, w, _, _) in gains for _ in range(4))
    res = pl.pallas_call(body, name="adamw_gains", out_shape=out_shape, in_specs=[vm] * (1 + 3 * n),
                         out_specs=tuple([vm] * (4 * n)))(stats, *flat)
    return [res[4 * i:4 * i + 4] for i in range(n)]


def _row_tile(r, cap=640):
    return max(d for d in range(16, min(r, cap) + 1, 16) if r % d == 0)


def add_pairs(g, got, core):
    n, r, w = got.shape
    tr = _row_tile(r)

    def body(c_ref, a_ref, b_ref, o_ref):
        o_ref[...] = (a_ref[...].astype(F32) + b_ref[...].astype(F32)).astype(o_ref.dtype)

    spec = pl.BlockSpec((1, tr, w), lambda i, j, c: (i, j, 0))
    return pl.pallas_call(
        body, name="add_pairs", out_shape=jax.ShapeDtypeStruct(got.shape, got.dtype),
        grid_spec=pltpu.PrefetchScalarGridSpec(
            num_scalar_prefetch=1, grid=(n, r // tr),
            in_specs=[pl.BlockSpec((1, tr, w), lambda i, j, c: (2 * i + c[0], j, 0)), spec], out_specs=spec),
        compiler_params=_params(("parallel", "parallel")),
    )(core, g, got)


def sum_chips(part, land, chip):
    _, r, w = part.shape
    tr = _row_tile(r)

    def body(c_ref, p_ref, l_ref, o_ref):
        acc = p_ref[0].astype(F32)
        for s in range(N_CHIP - 1):
            acc = acc + l_ref[s].astype(F32)
        o_ref[...] = acc

    return pl.pallas_call(
        body, name="sum_chips", out_shape=jax.ShapeDtypeStruct((r, w), F32),
        grid_spec=pltpu.PrefetchScalarGridSpec(
            num_scalar_prefetch=1, grid=(r // tr,),
            in_specs=[pl.BlockSpec((1, tr, w), lambda i, c: (c[0], i, 0)), pl.BlockSpec((N_CHIP - 1, tr, w), lambda i, c: (0, i, 0))],
            out_specs=pl.BlockSpec((tr, w), lambda i, c: (i, 0))),
        compiler_params=_params(("parallel",)),
    )(chip, part, land)


def norm_x(x, g_mix, tm, comm=None):
    t = x.shape[0]
    grid = (t // tm,)
    c_ins, c_in_specs, c_outs, c_sems, alias = _comm_parts(comm, 2, 1)

    def body(*refs):
        (x_ref, g_ref), cin, (h_ref,), cout, _, csem = _split_refs(refs, 2, 1, 0, comm)
        _comm_start(comm, cin, cout, csem, grid)
        xv = x_ref[...]
        h_ref[...] = (xv * _rstd(xv, D_MODEL) * g_ref[...]).astype(BF16)
        _comm_finish(comm, cin, cout, csem, grid)

    return pl.pallas_call(
        body, name="norm_x", grid=grid, out_shape=(jax.ShapeDtypeStruct((t, D_MODEL), BF16), *c_outs),
        in_specs=[_rows(tm, D_MODEL), _resident((1, D_MODEL)), *c_in_specs],
        out_specs=(_rows(tm, D_MODEL), *([_ANY] * len(c_outs))),
        scratch_shapes=c_sems, input_output_aliases=alias, compiler_params=_params(("arbitrary",)),
    )(x, g_mix, *c_ins)


def pack_late_weights(w_up, w_down, w_o, w_pg, small, comm=None):
    rows = sum(PACK_ROWS[n] for n in PACK_W2)
    c_ins, c_in_specs, c_outs, c_sems, alias = _comm_parts(comm, 5, 1)
    grid = (1,)

    def body(*refs):
        (up_ref, dn_ref, o_ref, pg_ref, sm_ref), cin, (out_ref,), cout, _, csem = _split_refs(refs, 5, 1, 0, comm)
        _comm_start(comm, cin, cout, csem, grid)
        out_ref[0:512, :] = up_ref[0].T.astype(BF16)
        out_ref[512:1024, :] = dn_ref[0].astype(BF16)
        out_ref[1024:1152, :] = o_ref[0].astype(BF16)
        out_ref[1152:1280, :] = pg_ref[0].astype(BF16)
        out_ref[1280:rows, :] = sm_ref[...]
        _comm_finish(comm, cin, cout, csem, grid)

    def whole(a):
        nd = a.ndim
        return pl.BlockSpec(a.shape, lambda i: (0,) * nd)

    args = (w_up, w_down, w_o, w_pg, small)
    return pl.pallas_call(
        body, name="pack_late_weights", grid=grid, out_shape=(jax.ShapeDtypeStruct((rows, D_MODEL), BF16), *c_outs),
        in_specs=[*[whole(a) for a in args], *c_in_specs],
        out_specs=(pl.BlockSpec((rows, D_MODEL), lambda i: (0, 0)), *([_ANY] * len(c_outs))),
        scratch_shapes=c_sems, input_output_aliases=alias, compiler_params=_params(("arbitrary",)),
    )(*args, *c_ins)


def in_proj(h, w_in_t, tm):
    t = h.shape[0]
    nc = 512

    def body(h_ref, w_ref, z_ref):
        hv = h_ref[...]
        for cidx in range(ZP // nc):
            z_ref[:, cidx * nc:(cidx + 1) * nc] = _dot_nt(hv, w_ref[cidx * nc:(cidx + 1) * nc, :])

    return pl.pallas_call(
        body, name="in_proj", grid=(t // tm,), out_shape=jax.ShapeDtypeStruct((t, ZP), F32),
        in_specs=[_rows(tm, D_MODEL), _resident((ZP, D_MODEL))],
        out_specs=_rows(tm, ZP), compiler_params=_params(("parallel",)),
    )(h, w_in_t)


def attn_prep(zp, tabs, g_qa, g_kva, g_qn, g_kn, w_qb_t, w_kvb_t, tm, s_len):
    t = zp.shape[0]
    nsb = s_len // tm
    scale_a = (QK_NOPE + QK_ROPE) ** -0.5
    scale_b = HD_B ** -0.5

    def body(qb_ref, qlat_ref, kb_ref, vb_ref, ckv_ref, kpe_ref, tab_ref, gqa_ref, gkva_ref, gqn_ref, gkn_ref,
             wqb_ref, wkvb_ref, qa_o, ka_o, va_o, qb_o, kb_o, vb_o, cq_o, ckvn_o):
        ca, s1a, s2a = tab_ref[0], tab_ref[1], tab_ref[2]
        ck = tab_ref[3]
        cb, s1b, s2b = tab_ref[4], tab_ref[5], tab_ref[6]
        ql = qlat_ref[...]
        cq = (ql * _rstd(ql, Q_LORA) * gqa_ref[...]).astype(BF16)
        cq_o[...] = cq
        qa = _dot_nt(cq, wqb_ref[...])
        slabs = [slice(h * HP, (h + 1) * HP) for h in range(H_A)]
        qa_o[...] = jnp.concatenate(
            [(_rope_fwd(qa[:, sl], ca, s1a, s2a) * scale_a).astype(BF16) for sl in slabs], axis=1)
        cr = ckv_ref[...]
        ckv = (cr * _rstd(cr, KV_LORA) * gkva_ref[...]).astype(BF16)
        ckvn_o[...] = ckv
        kva = _dot_nt(ckv, wkvb_ref[...])
        kpe = _rope_fwd(kpe_ref[...], ck, s1a, s2a)
        ka_o[...] = jnp.concatenate([(kva[:, sl] + kpe).astype(BF16) for sl in slabs], axis=1)
        va_o[...] = kva[:, H_A * HP:].astype(BF16)
        gqn, gkn = gqn_ref[...], gkn_ref[...]

        def norm_rope(ref, sl, g, scale):
            xs = ref[:, sl]
            y = _rope_fwd(xs * _rstd(xs, HD_B) * g, cb, s1b, s2b)
            return (y if scale is None else y * scale).astype(BF16)

        qb_o[...] = jnp.concatenate([norm_rope(qb_ref, sl, gqn, scale_b) for sl in slabs], axis=1)
        kb_o[...] = jnp.concatenate([norm_rope(kb_ref, sl, gkn, None) for sl in slabs[:KV_B]], axis=1)
        vb_o[...] = vb_ref[...].astype(BF16)

    def o(width):
        return jax.ShapeDtypeStruct((t, width), BF16)

    return pl.pallas_call(
        body, name="attn_prep", grid=(t // tm,),
        out_shape=(o(H_A * HP), o(H_A * HP), o(H_A * HP), o(H_B * HP), o(KV_B * HP), o(KV_B * HP), o(Q_LORA), o(KV_LORA)),
        in_specs=[_rows(tm, 1024, 2), _rows(tm, 256, 12), _rows(tm, 256, 13), _rows(tm, 256, 14),
                  _rows(tm, 128, 30), _rows(tm, 128, 31),
                  pl.BlockSpec((7, tm, HP), lambda i: (0, i % nsb, 0)),
                  _resident((1, Q_LORA)), _resident((1, KV_LORA)), _resident((1, HP)), _resident((1, HP)),
                  _resident((H_A * HP, Q_LORA)), _resident((2 * H_A * HP, KV_LORA))],
        out_specs=(_rows(tm, H_A * HP), _rows(tm, H_A * HP), _rows(tm, H_A * HP), _rows(tm, H_B * HP),
                   _rows(tm, KV_B * HP), _rows(tm, KV_B * HP), _rows(tm, Q_LORA), _rows(tm, KV_LORA)),
        compiler_params=_params(("parallel",)),
    )(zp, zp, zp, zp, zp, zp, tabs, g_qa, g_kva, g_qn, g_kn, w_qb_t, w_kvb_t)


def attn_fwd(q, k, v, n_b, s_len, tq, name, comm=None):
    t = q.shape[0]
    n_h, n_hk = q.shape[1] // HP, k.shape[1] // HP
    grp = n_h // n_hk
    nq = s_len // tq
    sub = min(tq, 256)
    grid = (n_b, n_h, nq)
    c_ins, c_in_specs, c_outs, c_sems, alias = _comm_parts(comm, 3, 2)

    def body(*refs):
        (q_ref, k_ref, v_ref), cin, (o_ref, lse_ref), cout, _, csem = _split_refs(refs, 3, 2, 0, comm)
        _comm_start(comm, cin, cout, csem, grid)
        kv, vv = k_ref[...], v_ref[...]
        for r in range(tq // sub):
            rows = slice(r * sub, (r + 1) * sub)
            s = _dot_nt(q_ref[rows, :], kv)
            m = jnp.max(s, axis=-1, keepdims=True)
            p = jnp.exp(s - m)
            l = jnp.sum(p, axis=-1, keepdims=True)
            o_ref[rows, :] = (_dot_nn(p.astype(BF16), vv) * (1.0 / l)).astype(o_ref.dtype)
            lse_ref[rows, :] = jnp.broadcast_to(m + jnp.log(l), (sub, HP))
        _comm_finish(comm, cin, cout, csem, grid)

    qspec = pl.BlockSpec((tq, HP), lambda b, h, i: (b * nq + i, h))
    kspec = pl.BlockSpec((s_len, HP), lambda b, h, i: (b, h // grp))
    return pl.pallas_call(
        body, name=name, grid=grid,
        out_shape=(jax.ShapeDtypeStruct((t, n_h * HP), BF16), jax.ShapeDtypeStruct((t, n_h * HP), F32), *c_outs),
        in_specs=[qspec, kspec, kspec, *c_in_specs], out_specs=(qspec, qspec, *([_ANY] * len(c_outs))),
        scratch_shapes=c_sems, input_output_aliases=alias,
        compiler_params=_params(("arbitrary", "arbitrary", "arbitrary")),
    )(q, k, v, *c_ins)


def merge_fwd(oa, ob, zp, x, w_oa_t, w_ob_t, wpack, off, tm):
    t = x.shape[0]

    def body(oa_ref, ob_ref, ga_ref, gb_ref, x_ref, woa_ref, wob_ref, wo_ref, x1_o, mg_o, y_o):
        ya = _dot_nt(oa_ref[...], woa_ref[...])
        yb = _dot_nt(ob_ref[...], wob_ref[...])
        y_o[:, 0:D_MODEL] = ya.astype(BF16)
        y_o[:, D_MODEL:2 * D_MODEL] = yb.astype(BF16)
        merged = (jax.nn.sigmoid(ga_ref[...]) * ya + jax.nn.sigmoid(gb_ref[...]) * yb).astype(BF16)
        mg_o[...] = merged
        x1_o[...] = x_ref[...] + _dot_nn(merged, _wrows(wo_ref, 0, D_MODEL))

    return pl.pallas_call(
        body, name="merge_fwd", grid=(t // tm,),
        out_shape=(jax.ShapeDtypeStruct((t, D_MODEL), F32), jax.ShapeDtypeStruct((t, D_MODEL), BF16),
                   jax.ShapeDtypeStruct((t, 2 * D_MODEL), BF16)),
        in_specs=[_rows(tm, H_A * HP), _rows(tm, H_B * HP), _rows(tm, 1024, 0), _rows(tm, 1024, 1), _rows(tm, D_MODEL),
                  _resident((D_MODEL, H_A * HP)), _resident((D_MODEL, H_B * HP)), _packed_weight(128, off["w_o"])],
        out_specs=(_rows(tm, D_MODEL), _rows(tm, D_MODEL), _rows(tm, 2 * D_MODEL)), compiler_params=_params(("parallel",)),
    )(oa, ob, zp, zp, x, w_oa_t, w_ob_t, wpack)


def mlp_fwd(x1, g_mlp, wpack, off, tm):
    t = x1.shape[0]
    fc = 1024
    n_ch = D_FF // fc

    def body(x_ref, g_ref, wp_ref, x2_o, u_o, wup_s, wdn_s, sems):
        first = pl.program_id(0) == 0
        fetch = _weight_fetches(wp_ref, ((off["w_up"], wup_s), (off["w_down"], wdn_s)), sems, n_ch)

        @pl.when(first)
        def _():
            for cps in fetch:
                for cp in cps:
                    cp.start()

        xv = x_ref[...]
        h2 = (xv * _rstd(xv, D_MODEL) * g_ref[...]).astype(BF16)
        acc = xv
        for cidx in range(n_ch):
            _wait_when(first, fetch[cidx])
            sl = slice(cidx * fc, (cidx + 1) * fc)
            u = jnp.maximum(_dot_nt(h2, _wrows(wup_s, cidx * fc, fc)), 0.0)
            u_o[:, sl] = u.astype(BF16)
            acc = acc + _dot_nn((u * u).astype(BF16), _wrows(wdn_s, cidx * fc, fc))
        x2_o[...] = acc

    wshape = pltpu.VMEM((N_DEV, D_FF // N_DEV, D_MODEL), BF16)
    return pl.pallas_call(
        body, name="mlp_fwd", grid=(t // tm,),
        out_shape=(jax.ShapeDtypeStruct((t, D_MODEL), F32), jax.ShapeDtypeStruct((t, D_FF), BF16)),
        in_specs=[_rows(tm, D_MODEL), _resident((1, D_MODEL)), _ANY],
        out_specs=(_rows(tm, D_MODEL), _rows(tm, D_FF)),
        scratch_shapes=[wshape, wshape, pltpu.SemaphoreType.DMA((2 * n_ch,))],
        compiler_params=_params(("arbitrary",)),
    )(x1, g_mlp, wpack)


def ple_loss_bwd(x2, p, tgt, g_ple, g_final, wpack, off, w_ple_t, tm):
    t = x2.shape[0]
    inv_d = 1.0 / D_MODEL

    def body(x2_ref, p_ref, tg_ref, gp_ref, gf_ref, wpg_ref, wple_ref, dx2_o, dt_o, h3_o, dpe_o, st_o, dx2b_o):
        @pl.when(pl.program_id(0) == 0)
        def _():
            st_o[...] = jnp.zeros_like(st_o)

        x2v = x2_ref[...]
        gp, gf = gp_ref[...], gf_ref[...]
        w_pg = _wrows(wpg_ref, 0, D_MODEL)
        r2 = _rstd(x2v, D_MODEL)
        xh2 = x2v * r2
        h3 = (xh2 * gp).astype(BF16)
        h3_o[...] = h3
        gate = jax.nn.sigmoid(_dot_nn(h3, w_pg))
        pe = _dot_nt(p_ref[...].astype(BF16), wple_ref[...])
        x3 = x2v + gate * pe
        r3 = _rstd(x3, D_MODEL)
        xh3 = x3 * r3
        err = xh3 * gf - tg_ref[...]
        dy = err * inv_d
        dx3 = _rms_bwd(dy, xh3, r3, gf, D_MODEL)
        dpe_o[...] = (dx3 * gate).astype(BF16)
        dt = (dx3 * pe * gate * (1.0 - gate)).astype(BF16)
        dt_o[...] = dt
        dh3 = _dot_nt(dt, w_pg)
        dx2 = dx3 + _rms_bwd(dh3, xh2, r2, gp, D_MODEL)
        dx2_o[...] = dx2
        dx2b_o[...] = dx2.astype(BF16)
        st_o[0:1, :] += _colsum(dh3 * xh2)
        st_o[1:2, :] += _colsum(dy * xh3)
        st_o[2:3, :] += _colsum(err * err) * (0.5 * inv_d)

    bf = jax.ShapeDtypeStruct((t, D_MODEL), BF16)
    return pl.pallas_call(
        body, name="ple_loss_bwd", grid=(t // tm,),
        out_shape=(jax.ShapeDtypeStruct((t, D_MODEL), F32), bf, bf, bf, jax.ShapeDtypeStruct((3, D_MODEL), F32), bf),
        in_specs=[_rows(tm, D_MODEL), _rows(tm, PLE_DIM), _rows(tm, D_MODEL), _resident((1, D_MODEL)), _resident((1, D_MODEL)),
                  _packed_weight(128, off["w_ple_gate"]), _resident((D_MODEL, PLE_DIM))],
        out_specs=(_rows(tm, D_MODEL), _rows(tm, D_MODEL), _rows(tm, D_MODEL), _rows(tm, D_MODEL),
                   pl.BlockSpec((3, D_MODEL), lambda i: (0, 0)), _rows(tm, D_MODEL)),
        compiler_params=_params(("arbitrary",)),
    )(x2, p, tgt, g_ple, g_final, wpack, w_ple_t)


def mlp_bwd(dx2, x1, u, g_mlp, wpack, off, tm):
    t = x1.shape[0]
    fc = 1024
    n_ch = D_FF // fc

    def body(dx2_ref, x1_ref, u_ref, g_ref, wp_ref, dx1_o, da_o, h2_o, st_o, dx1b_o, wup_s, wdn_s, sems):
        first = pl.program_id(0) == 0
        fetch = _weight_fetches(wp_ref, ((off["w_down"], wdn_s), (off["w_up"], wup_s)), sems, n_ch)

        @pl.when(first)
        def _():
            st_o[...] = jnp.zeros_like(st_o)
            for cps in fetch:
                for cp in cps:
                    cp.start()

        d2 = dx2_ref[...]
        d2b = d2.astype(BF16)
        dh2 = jnp.zeros((tm, D_MODEL), F32)
        for cidx in range(n_ch):
            _wait_when(first, fetch[cidx])
            sl = slice(cidx * fc, (cidx + 1) * fc)
            da = (_dot_nt(d2b, _wrows(wdn_s, cidx * fc, fc)) * (2.0 * u_ref[:, sl].astype(F32))).astype(BF16)
            da_o[:, sl] = da
            dh2 = dh2 + _dot_nn(da, _wrows(wup_s, cidx * fc, fc))
        xv = x1_ref[...]
        g = g_ref[...]
        r1 = _rstd(xv, D_MODEL)
        xh1 = xv * r1
        h2_o[...] = (xh1 * g).astype(BF16)
        st_o[...] += _colsum(dh2 * xh1)
        dx1 = d2 + _rms_bwd(dh2, xh1, r1, g, D_MODEL)
        dx1_o[...] = dx1
        dx1b_o[...] = dx1.astype(BF16)

    return pl.pallas_call(
        body, name="mlp_bwd", grid=(t // tm,),
        out_shape=(jax.ShapeDtypeStruct((t, D_MODEL), F32), jax.ShapeDtypeStruct((t, D_FF), BF16),
                   jax.ShapeDtypeStruct((t, D_MODEL), BF16), jax.ShapeDtypeStruct((1, D_MODEL), F32),
                   jax.ShapeDtypeStruct((t, D_MODEL), BF16)),
        in_specs=[_rows(tm, D_MODEL), _rows(tm, D_MODEL), _rows(tm, D_FF), _resident((1, D_MODEL)), _ANY],
        out_specs=(_rows(tm, D_MODEL), _rows(tm, D_FF), _rows(tm, D_MODEL), pl.BlockSpec((1, D_MODEL), lambda i: (0, 0)),
                   _rows(tm, D_MODEL)),
        scratch_shapes=[pltpu.VMEM((N_DEV, D_FF // N_DEV, D_MODEL), BF16), pltpu.VMEM((N_DEV, D_FF // N_DEV, D_MODEL), BF16),
                        pltpu.SemaphoreType.DMA((2 * n_ch,))],
        compiler_params=_params(("arbitrary",)),
    )(dx2, x1, u, g_mlp, wpack)


def merge_bwd(dx1b, yab, zp, w_oa_t, w_ob_t, wpack, off, tm, comm=None):
    t = dx1b.shape[0]
    grid = (t // tm,)
    c_ins, c_in_specs, c_outs, c_sems, alias = _comm_parts(comm, 7, 5)

    def body(*refs):
        ((dx1_ref, y_ref, ga_ref, gb_ref, woa_ref, wob_ref, wo_ref), cin,
         (doa_o, dob_o, dg_o, dya_o, dyb_o), cout, _, csem) = _split_refs(refs, 7, 5, 0, comm)
        _comm_start(comm, cin, cout, csem, grid)
        dm = _dot_nt(dx1_ref[...], _wrows(wo_ref, 0, D_MODEL))
        for g_ref, w_ref, do_o, dy_o, col in ((ga_ref, woa_ref, doa_o, dya_o, 0), (gb_ref, wob_ref, dob_o, dyb_o, 1)):
            cols = slice(col * D_MODEL, (col + 1) * D_MODEL)
            sg = jax.nn.sigmoid(g_ref[...])
            dyv = (dm * sg).astype(BF16)
            dy_o[...] = dyv
            dg_o[:, cols] = (dm * y_ref[:, cols].astype(F32) * sg * (1.0 - sg)).astype(BF16)
            do_o[...] = _dot_nn(dyv, w_ref[...]).astype(BF16)
        _comm_finish(comm, cin, cout, csem, grid)

    bf = jax.ShapeDtypeStruct((t, D_MODEL), BF16)
    return pl.pallas_call(
        body, name="merge_bwd", grid=grid,
        out_shape=(bf, bf, jax.ShapeDtypeStruct((t, ZP), BF16), bf, bf, *c_outs),
        in_specs=[_rows(tm, D_MODEL), _rows(tm, 2 * D_MODEL), _rows(tm, 1024, 0), _rows(tm, 1024, 1),
                  _resident((D_MODEL, H_A * HP)), _resident((D_MODEL, H_B * HP)), _packed_weight(128, off["w_o"]), *c_in_specs],
        out_specs=(_rows(tm, D_MODEL), _rows(tm, D_MODEL), _rows(tm, 2 * D_MODEL), _rows(tm, D_MODEL), _rows(tm, D_MODEL),
                   *([_ANY] * len(c_outs))),
        scratch_shapes=c_sems, input_output_aliases=alias,
        compiler_params=_params(("arbitrary",)),
    )(dx1b, yab, zp, zp, w_oa_t, w_ob_t, wpack, *c_ins)


def attn_bwd(q, k, v, do, o, lse, n_b, s_len, tq, name, comm=None):
    t = q.shape[0]
    n_h, n_hk = q.shape[1] // HP, k.shape[1] // HP
    grp = n_h // n_hk
    nq = s_len // tq
    sub = min(tq, 256)
    grid = (n_b, n_hk, grp, nq)
    c_ins, c_in_specs, c_outs, c_sems, alias = _comm_parts(comm, 6, 3)

    def body(*refs):
        ((q_ref, k_ref, v_ref, do_ref, o_ref, lse_ref), cin, (dq_o, dk_o, dv_o), cout, (p_s, ds_s, dk_acc, dv_acc),
         csem) = _split_refs(refs, 6, 3, 4, comm)
        _comm_start(comm, cin, cout, csem, grid)

        @pl.when((pl.program_id(2) == 0) & (pl.program_id(3) == 0))
        def _():
            dk_acc[...] = jnp.zeros_like(dk_acc)
            dv_acc[...] = jnp.zeros_like(dv_acc)

        kv, vv = k_ref[...], v_ref[...]
        for r in range(tq // sub):
            rows = slice(r * sub, (r + 1) * sub)
            qv, dov = q_ref[rows, :], do_ref[rows, :]
            delta = jnp.sum(dov.astype(F32) * o_ref[rows, :].astype(F32), axis=-1, keepdims=True)
            p = jnp.exp(_dot_nt(qv, kv) - lse_ref[rows, 0:1])
            ds = (p * (_dot_nt(dov, vv) - delta)).astype(BF16)
            p_s[rows, :] = p.astype(BF16)
            ds_s[rows, :] = ds
            dq_o[rows, :] = _dot_nn(ds, kv).astype(dq_o.dtype)
        dk_acc[...] += _dot_tn(ds_s[...], q_ref[...])
        dv_acc[...] += _dot_tn(p_s[...], do_ref[...])

        @pl.when((pl.program_id(2) == grp - 1) & (pl.program_id(3) == nq - 1))
        def _():
            dk_o[...] = dk_acc[...].astype(dk_o.dtype)
            dv_o[...] = dv_acc[...].astype(dv_o.dtype)

        _comm_finish(comm, cin, cout, csem, grid)

    qspec = pl.BlockSpec((tq, HP), lambda b, hk, g, i: (b * nq + i, hk * grp + g))
    kspec = pl.BlockSpec((s_len, HP), lambda b, hk, g, i: (b, hk))
    return pl.pallas_call(
        body, name=name, grid=grid,
        out_shape=(jax.ShapeDtypeStruct((t, n_h * HP), BF16), jax.ShapeDtypeStruct((t, n_hk * HP), BF16),
                   jax.ShapeDtypeStruct((t, n_hk * HP), BF16), *c_outs),
        in_specs=[qspec, kspec, kspec, qspec, qspec, qspec, *c_in_specs],
        out_specs=(qspec, kspec, kspec, *([_ANY] * len(c_outs))),
        scratch_shapes=[pltpu.VMEM((tq, s_len), BF16), pltpu.VMEM((tq, s_len), BF16),
                        pltpu.VMEM((s_len, HP), F32), pltpu.VMEM((s_len, HP), F32), *c_sems],
        input_output_aliases=alias,
        compiler_params=_params(("arbitrary", "arbitrary", "arbitrary", "arbitrary")),
    )(q, k, v, do, o, lse, *c_ins)


def prep_bwd(dqa, dka, dva, dqb, dkb, dvb, zp, dz, tabs, g_qa, g_kva, g_qn, g_kn, w_qb_t, w_kvb_t, tm, s_len):
    t = zp.shape[0]
    nsb = s_len // tm
    scale_a = (QK_NOPE + QK_ROPE) ** -0.5
    scale_b = HD_B ** -0.5

    def body(dqa_ref, dka_ref, dva_ref, dqb_ref, dkb_ref, dvb_ref, qb_ref, qlat_ref, kb_ref, ckv_ref, tab_ref,
             gqa_ref, gkva_ref, gqn_ref, gkn_ref, wqb_ref, wkvb_ref, _, dz_o, dqap_o, dkva_o, st_o):
        dzq_o, dsm_o = dz_o.at[:, 0:1024], dz_o.at[:, 1024:2048]

        @pl.when(pl.program_id(0) == 0)
        def _():
            st_o[...] = jnp.zeros_like(st_o)

        ca, s1a, s2a = tab_ref[0], tab_ref[1], tab_ref[2]
        ck = tab_ref[3]
        cb, s1b, s2b = tab_ref[4], tab_ref[5], tab_ref[6]
        for h in range(H_A):
            sl = slice(h * HP, (h + 1) * HP)
            dqap_o[:, sl] = _rope_bwd(dqa_ref[:, sl].astype(F32) * scale_a, ca, s1a, s2a).astype(BF16)
        dcq = _dot_nn(dqap_o[...], wqb_ref[...])
        ql = qlat_ref[...]
        rq = _rstd(ql, Q_LORA)
        xh = ql * rq
        gqa = gqa_ref[...]
        st_o[0:1, :] += _colsum(dcq * xh)
        dsm_o[:, 0:256] = _rms_bwd(dcq, xh, rq, gqa, Q_LORA).astype(BF16)
        dkpe = jnp.zeros((tm, HP), F32)
        for h in range(H_A):
            sl = slice(h * HP, (h + 1) * HP)
            dk = dka_ref[:, sl]
            dkpe = dkpe + dk.astype(F32)
            dkva_o[:, sl] = dk.astype(BF16)
        dkva_o[:, H_A * HP:] = dva_ref[...].astype(BF16)
        dsm_o[:, 896:1024] = _rope_bwd(dkpe, ck, s1a, s2a).astype(BF16)
        dckv = _dot_nn(dkva_o[...], wkvb_ref[...])
        cr = ckv_ref[...]
        rk = _rstd(cr, KV_LORA)
        xh = cr * rk
        st_o[1:2, 0:128] += _colsum(dckv * xh)
        dsm_o[:, 768:896] = _rms_bwd(dckv, xh, rk, gkva_ref[...], KV_LORA).astype(BF16)
        gqn, gkn = gqn_ref[...], gkn_ref[...]
        dgq = jnp.zeros((1, HP), F32)
        for h in range(H_B):
            sl = slice(h * HP, (h + 1) * HP)
            dy = _rope_bwd(dqb_ref[:, sl].astype(F32) * scale_b, cb, s1b, s2b)
            xs = qb_ref[:, sl]
            r = _rstd(xs, HD_B)
            xh = xs * r
            dgq = dgq + _colsum(dy * xh)
            dzq_o[:, sl] = _rms_bwd(dy, xh, r, gqn, HD_B).astype(BF16)
        st_o[2:3, 0:128] += dgq
        dgk = jnp.zeros((1, HP), F32)
        for h in range(KV_B):
            sl = slice(h * HP, (h + 1) * HP)
            dy = _rope_bwd(dkb_ref[:, sl].astype(F32), cb, s1b, s2b)
            xs = kb_ref[:, sl]
            r = _rstd(xs, HD_B)
            xh = xs * r
            dgk = dgk + _colsum(dy * xh)
            dsm_o[:, 256 + h * HP:256 + (h + 1) * HP] = _rms_bwd(dy, xh, r, gkn, HD_B).astype(BF16)
        st_o[3:4, 0:128] += dgk
        dsm_o[:, 512:768] = dvb_ref[...].astype(BF16)

    return pl.pallas_call(
        body, name="prep_bwd", grid=(t // tm,),
        out_shape=(jax.ShapeDtypeStruct((t, ZP), BF16), jax.ShapeDtypeStruct((t, 1024), BF16),
                   jax.ShapeDtypeStruct((t, 2048), BF16), jax.ShapeDtypeStruct((4, 256), F32)),
        in_specs=[_rows(tm, 1024), _rows(tm, 1024), _rows(tm, 1024), _rows(tm, 1024), _rows(tm, 256), _rows(tm, 256),
                  _rows(tm, 1024, 2), _rows(tm, 256, 12), _rows(tm, 256, 13), _rows(tm, 128, 30),
                  pl.BlockSpec((7, tm, HP), lambda i: (0, i % nsb, 0)),
                  _resident((1, Q_LORA)), _resident((1, KV_LORA)), _resident((1, HP)), _resident((1, HP)),
                  _resident((H_A * HP, Q_LORA)), _resident((2 * H_A * HP, KV_LORA)), _ANY],
        out_specs=(_rows(tm, 2048, 1), _rows(tm, 1024), _rows(tm, 2048), pl.BlockSpec((4, 256), lambda i: (0, 0))),
        input_output_aliases={17: 0}, compiler_params=_params(("arbitrary",)),
    )(dqa, dka, dva, dqb, dkb, dvb, zp, zp, zp, zp, tabs, g_qa, g_kva, g_qn, g_kn, w_qb_t, w_kvb_t, dz)


def in_bwd(dz, x, dx1, g_mix, w_in_t, tm, comm=None):
    t = x.shape[0]
    grid = (t // tm,)
    c_ins, c_in_specs, c_outs, c_sems, alias = _comm_parts(comm, 5, 2)

    def body(*refs):
        (dz_ref, x_ref, dx1_ref, g_ref, w_ref), cin, (dx_o, st_o), cout, _, csem = _split_refs(refs, 5, 2, 0, comm)
        _comm_start(comm, cin, cout, csem, grid)

        @pl.when(pl.program_id(0) == 0)
        def _():
            st_o[...] = jnp.zeros_like(st_o)

        dh = _dot_nn(dz_ref[...], w_ref[...])
        xv = x_ref[...]
        g = g_ref[...]
        r = _rstd(xv, D_MODEL)
        xh = xv * r
        st_o[...] += _colsum(dh * xh)
        dx_o[...] = dx1_ref[...] + _rms_bwd(dh, xh, r, g, D_MODEL)
        _comm_finish(comm, cin, cout, csem, grid)

    return pl.pallas_call(
        body, name="in_bwd", grid=grid,
        out_shape=(jax.ShapeDtypeStruct((t, D_MODEL), F32), jax.ShapeDtypeStruct((1, D_MODEL), F32), *c_outs),
        in_specs=[_rows(tm, ZP), _rows(tm, D_MODEL), _rows(tm, D_MODEL),
                  _resident((1, D_MODEL)), _resident((ZP, D_MODEL)), *c_in_specs],
        out_specs=(_rows(tm, D_MODEL), pl.BlockSpec((1, D_MODEL), lambda i: (0, 0)), *([_ANY] * len(c_outs))),
        scratch_shapes=c_sems, input_output_aliases=alias,
        compiler_params=_params(("arbitrary",)),
    )(dz, x, dx1, g_mix, w_in_t, *c_ins)


def matmul_tn(a, b, name, square_a=False):
    t, m = a.shape
    n = b.shape[1]
    bm = min(m, 512)
    tk = min(t, 4096)

    def body(a_ref, b_ref, o_ref):
        @pl.when(pl.program_id(1) == 0)
        def _():
            o_ref[...] = jnp.zeros_like(o_ref)

        av = a_ref[...]
        if square_a:
            av = (av.astype(F32) * av.astype(F32))
        o_ref[...] += _dot_tn(av.astype(BF16), b_ref[...].astype(BF16))

    return pl.pallas_call(
        body, name=name, grid=(m // bm, t // tk), out_shape=jax.ShapeDtypeStruct((m, n), F32),
        in_specs=[pl.BlockSpec((tk, bm), lambda i, kk: (kk, i)), pl.BlockSpec((tk, n), lambda i, kk: (kk, 0))],
        out_specs=pl.BlockSpec((bm, n), lambda i, kk: (i, 0)),
        compiler_params=_params(("parallel", "arbitrary")),
    )(a, b)


def matmul_tn_packed(a, b, name, rows, row_off, total_rows, buf=None, square_a=False):
    t, m = a.shape
    n = b.shape[1]
    pd = max(1, 512 // rows)
    bm = pd * rows
    tk = min(t, 4096)
    nk = t // tk

    def body(a_ref, b_ref, *rest):
        o_ref, acc = rest[-2], rest[-1]

        @pl.when(pl.program_id(1) == 0)
        def _():
            acc[...] = jnp.zeros_like(acc)

        av = a_ref[...]
        if square_a:
            av = (av.astype(F32) * av.astype(F32))
        acc[...] += _dot_tn(av.astype(BF16), b_ref[...].astype(BF16))

        @pl.when(pl.program_id(1) == nk - 1)
        def _():
            o_ref[...] = acc[...].reshape(pd, rows, n).astype(o_ref.dtype)

    in_specs = [pl.BlockSpec((tk, bm), lambda i, kk: (kk, i)), pl.BlockSpec((tk, n), lambda i, kk: (kk, 0))]
    args = [a, b]
    if buf is not None:
        in_specs.append(_ANY)
        args.append(buf)
    return pl.pallas_call(
        body, name=name, grid=(m // bm, nk), out_shape=jax.ShapeDtypeStruct((N_DEV, total_rows, n), BF16),
        in_specs=in_specs, out_specs=pl.BlockSpec((pd, rows, n), lambda i, kk: (i, row_off // rows, 0)),
        scratch_shapes=[pltpu.VMEM((bm, n), F32)], input_output_aliases={2: 0} if buf is not None else {},
        compiler_params=_params(("parallel", "arbitrary")),
    )(*args)


def adamw(w, g, m, v, name, g_transposed=False):
    _, r, c = w.shape
    tr = r if (g_transposed or r <= 256) else 256
    c1 = 1.0 - ADAM_B1 ** ADAM_STEP
    c2 = 1.0 - ADAM_B2 ** ADAM_STEP

    def body(w_ref, g_ref, m_ref, v_ref, g_o, d_o, m_o, v_o):
        gv = g_ref[...].T if g_transposed else g_ref[...]
        mn = ADAM_B1 * m_ref[0] + (1.0 - ADAM_B1) * gv
        vn = ADAM_B2 * v_ref[0] + (1.0 - ADAM_B2) * (gv * gv)
        g_o[0] = gv
        m_o[0] = mn
        v_o[0] = vn
        d_o[0] = -ADAM_LR * ((mn / c1) / (jnp.sqrt(vn / c2) + ADAM_EPS) + ADAM_WD * w_ref[0])

    spec = pl.BlockSpec((1, tr, c), lambda i: (0, i, 0))
    gspec = pl.BlockSpec((c, r), lambda i: (0, 0)) if g_transposed else pl.BlockSpec((tr, c), lambda i: (i, 0))
    shp = jax.ShapeDtypeStruct((1, r, c), F32)
    return pl.pallas_call(
        body, name=name, grid=(r // tr,), out_shape=(shp,) * 4, in_specs=[spec, gspec, spec, spec], out_specs=(spec,) * 4,
        compiler_params=_params(("parallel",)),
    )(w, g, m, v)


def _rope_tables(s_len):
    def angles(pos, dim):
        inv = np.float32(ROPE_THETA) ** (-np.arange(0, dim, 2, dtype=np.float32) / np.float32(dim))
        return pos.astype(np.float32)[:, None] * inv[None, :]

    tpos = np.arange(s_len)
    a1 = angles(tpos, QK_ROPE)
    ar = angles(tpos // GRID_W, HD_B // 2)
    ac = angles(tpos % GRID_W, HD_B // 2)
    z16 = np.zeros((s_len, 16), np.float32)
    z32 = np.zeros((s_len, 32), np.float32)
    z64 = np.zeros((s_len, 64), np.float32)
    one64 = np.ones((s_len, 64), np.float32)
    c1, s1 = np.cos(a1), np.sin(a1)
    ca = np.concatenate([one64, c1, c1, z32], axis=1)
    ck = np.concatenate([z64, c1, c1, z32], axis=1)
    s1a = np.concatenate([z64, -s1, z16, z32], axis=1)
    s2a = np.concatenate([z64, z16, s1, z32], axis=1)
    cr, sr, cc, sc = np.cos(ar), np.sin(ar), np.cos(ac), np.sin(ac)
    cb = np.concatenate([cr, cr, cc, cc, z64], axis=1)
    s1b = np.concatenate([-sr, z16, -sc, z16, z64], axis=1)
    s2b = np.concatenate([z16, sr, z16, sc, z64], axis=1)
    return jnp.asarray(np.stack([ca, s1a, s2a, ck, cb, s1b, s2b]).astype(np.float32))


def _pad_heads(a, n_heads, axis):
    shp = a.shape
    a = a.reshape(shp[:axis] + (n_heads, shp[axis] // n_heads) + shp[axis + 1:])
    pad = [(0, 0)] * a.ndim
    pad[axis + 1] = (0, HP - a.shape[axis + 1])
    a = jnp.pad(a, pad)
    return a.reshape(shp[:axis] + (n_heads * HP,) + shp[axis + 1:])


def _unpad_heads(a, n_heads, width, axis):
    shp = a.shape
    a = a.reshape(shp[:axis] + (n_heads, HP) + shp[axis + 1:])
    a = lax.slice_in_dim(a, 0, width, axis=axis + 1)
    return a.reshape(shp[:axis] + (n_heads * width,) + shp[axis + 1:])


def _pack_rows(blocks, names):
    parts = []
    for name in names:
        b = blocks[name]
        padr = PACK_ROWS[name] - b.shape[-2]
        if padr:
            b = jnp.pad(b, [(0, 0)] * (b.ndim - 2) + [(0, padr), (0, 0)])
        parts.append(b)
    return jnp.concatenate(parts, axis=parts[0].ndim - 2)


def _expand_w_in(wt):
    z64 = jnp.zeros((64, D_MODEL), wt.dtype)
    z32 = jnp.zeros((32, D_MODEL), wt.dtype)
    return jnp.concatenate([
        wt[1184:2208], wt[2208:3232], _pad_heads(wt[416:928], H_B, 0), wt[0:256],
        _pad_heads(wt[928:1056], KV_B, 0), _pad_heads(wt[1056:1184], KV_B, 0), wt[256:384],
        z64, wt[384:416], z32], axis=0)


def _collapse_w_in(dw):
    dg, dq, ds = dw[0:2048], dw[2048:3072], dw[3072:4096]
    return jnp.concatenate([
        ds[0:256], ds[768:896], ds[960:992], _unpad_heads(dq, H_B, HD_B, 0), _unpad_heads(ds[256:512], KV_B, HD_B, 0),
        _unpad_heads(ds[512:768], KV_B, HD_B, 0), dg], axis=0)


def kernel(x, p, g_mix, w_in, g_qa, w_qb, g_kva, w_kvb, g_qn, g_kn, w_oa, w_ob, w_o, g_mlp, w_up, w_down, g_ple, w_ple_gate, w_ple, g_final, loss_target, m_g_mix, m_w_in, m_g_qa, m_w_qb, m_g_kva, m_w_kvb, m_g_qn, m_g_kn, m_w_oa, m_w_ob, m_w_o, m_g_mlp, m_w_up, m_w_down, m_g_ple, m_w_ple_gate, m_w_ple, m_g_final, v_g_mix, v_w_in, v_g_qa, v_w_qb, v_g_kva, v_w_kvb, v_g_qn, v_g_kn, v_w_oa, v_w_ob, v_w_o, v_g_mlp, v_w_up, v_w_down, v_g_ple, v_w_ple_gate, v_w_ple, v_g_final):
    n_b, s_len, _ = x.shape
    t = n_b * s_len
    tm = min(512, s_len)
    tq_f = min(1024, s_len)
    tq_b = min(1024, s_len)

    mats = dict(w_in=(w_in, m_w_in, v_w_in), w_qb=(w_qb, m_w_qb, v_w_qb), w_kvb=(w_kvb, m_w_kvb, v_w_kvb),
                w_oa=(w_oa, m_w_oa, v_w_oa), w_ob=(w_ob, m_w_ob, v_w_ob), w_o=(w_o, m_w_o, v_w_o),
                w_up=(w_up, m_w_up, v_w_up), w_down=(w_down, m_w_down, v_w_down),
                w_ple_gate=(w_ple_gate, m_w_ple_gate, v_w_ple_gate), w_ple=(w_ple, m_w_ple, v_w_ple))
    col_sharded = ("w_in", "w_qb", "w_kvb", "w_oa", "w_ob", "w_up", "w_ple")

    blocks = {}
    for name in PACK_W1 + ("w_oa", "w_ob", "w_ple"):
        blocks[name] = mats[name][0][0].T.reshape(-1, D_MODEL).astype(BF16)
    off_w1, _ = _pack_offsets(PACK_W1)
    off_w2, _ = _pack_offsets(PACK_W2)
    xf = x.reshape(t, D_MODEL)
    h, full1 = norm_x(xf, g_mix, tm, comm=gather_first_comm(_pack_rows(blocks, PACK_W1)))
    pack2, full1 = pack_late_weights(w_up, w_down, w_o, w_ple_gate, _pack_rows(blocks, ("w_oa", "w_ob", "w_ple")),
                                     comm=gather_pass_comm(full1))

    def gathered(full, offs, name, rows, width):
        return full[:, offs[name]:offs[name] + rows].reshape(-1, width)

    w_in_t = _expand_w_in(gathered(full1, off_w1, "w_in", 404, D_MODEL))
    w_qb_t = _pad_heads(gathered(full1, off_w1, "w_qb", 24, Q_LORA), H_A, 0)
    wkvb = gathered(full1, off_w1, "w_kvb", 16, KV_LORA).reshape(H_A, 2, 64, KV_LORA)
    w_kvb_t = jnp.concatenate([_pad_heads(wkvb[:, 0].reshape(-1, KV_LORA), H_A, 0),
                               _pad_heads(wkvb[:, 1].reshape(-1, KV_LORA), H_A, 0)], axis=0)

    tabs = _rope_tables(s_len)
    g_qn_p = jnp.pad(g_qn, ((0, 0), (0, HP - HD_B)))
    g_kn_p = jnp.pad(g_kn, ((0, 0), (0, HP - HD_B)))
    pf = p.reshape(t, PLE_DIM)
    tgt = loss_target.reshape(t, D_MODEL)

    zp = in_proj(h, w_in_t, tm)
    qa, ka, va, qb, kb, vb, cq, ckv = attn_prep(zp, tabs, g_qa, g_kva, g_qn_p, g_kn_p, w_qb_t, w_kvb_t, tm, s_len)
    oa, lse_a, full2 = attn_fwd(qa, ka, va, n_b, s_len, tq_f, "attn_a_fwd", comm=gather_first_comm(pack2))
    ob, lse_b, full2 = attn_fwd(qb, kb, vb, n_b, s_len, tq_f, "attn_b_fwd", comm=gather_pass_comm(full2))
    w_oa_t = _pad_heads(gathered(full2, off_w2, "w_oa", 64, H_A * V_DIM_A), H_A, 1)
    w_ob_t = _pad_heads(gathered(full2, off_w2, "w_ob", 64, H_B * HD_B), H_B, 1)
    w_ple_t = gathered(full2, off_w2, "w_ple", 32, PLE_DIM)
    x1, merged, yab = merge_fwd(oa, ob, zp, xf, w_oa_t, w_ob_t, full2, off_w2, tm)
    x2, u = mlp_fwd(x1, g_mlp, full2, off_w2, tm)
    dx2, dt, h3, dpe, st_ple, dx2b = ple_loss_bwd(x2, pf, tgt, g_ple, g_final.reshape(1, D_MODEL), full2, off_w2, w_ple_t, tm)
    dx1, da, h2, st_mlp, dx1b = mlp_bwd(dx2, x1, u, g_mlp, full2, off_w2, min(256, tm))

    core = lax.axis_index("c").astype(jnp.int32).reshape(1)
    chip = (2 * lax.axis_index("x") + lax.axis_index("y")).astype(jnp.int32).reshape(1)

    def packed(gblocks, names):
        return _pack_rows({n: gblocks[n].reshape(N_DEV, -1, D_MODEL).astype(BF16) for n in names}, names)

    off_g1, rows_g1 = _pack_offsets(PACK_G1)
    gpack1 = matmul_tn_packed(da, h2, "gw_up", 512, off_g1["w_up"], rows_g1)
    gpack1 = matmul_tn_packed(u, dx2b, "gw_down", 512, off_g1["w_down"], rows_g1, buf=gpack1, square_a=True)
    gpack1 = matmul_tn_packed(h3, dt, "gw_pg", 128, off_g1["w_ple_gate"], rows_g1, buf=gpack1)
    gple = matmul_tn(dpe, pf, "gw_ple").reshape(N_DEV, -1, D_MODEL).astype(BF16)
    gpack1 = lax.dynamic_update_slice(gpack1, gple, (0, off_g1["w_ple"], 0))
    doa, dob, dz, dya, dyb, got1 = merge_bwd(dx1b, yab, zp, w_oa_t, w_ob_t, full2, off_w2, tm,
                                               comm=scatter_sibling_comm(gpack1))
    part1 = add_pairs(gpack1, got1, core)
    dqa, dka, dva, land1 = attn_bwd(qa, ka, va, doa, oa, lse_a, n_b, s_len, tq_b, "attn_a_bwd", comm=scatter_chips_comm(part1))
    gshard1 = sum_chips(part1, land1, chip)

    off_g2, rows_g2 = _pack_offsets(PACK_G2)
    g2 = dict(w_oa=_unpad_heads(matmul_tn(dya, oa, "gw_oa"), H_A, V_DIM_A, 1),
              w_ob=_unpad_heads(matmul_tn(dyb, ob, "gw_ob"), H_B, HD_B, 1))
    gpack2 = matmul_tn_packed(merged, dx1b, "gw_o", 128, off_g2["w_o"], rows_g2)
    gpack2 = lax.dynamic_update_slice(gpack2, packed(g2, ("w_oa", "w_ob")), (0, off_g2["w_oa"], 0))
    part2 = add_pairs(gpack2, exchange_sibling(gpack2), core)
    dqb, dkb, dvb, land2 = attn_bwd(qb, kb, vb, dob, ob, lse_b, n_b, s_len, tq_b, "attn_b_bwd", comm=scatter_chips_comm(part2))
    gshard2 = sum_chips(part2, land2, chip)
    dz, dqap, dkva, st_prep = prep_bwd(dqa, dka, dva, dqb, dkb, dvb, zp, dz, tabs, g_qa, g_kva, g_qn_p, g_kn_p,
                                       w_qb_t, w_kvb_t, tm, s_len)

    gkv = matmul_tn(dkva, ckv, "gw_kvb")
    g3 = dict(
        w_in=_collapse_w_in(matmul_tn(dz, h, "gw_in")),
        w_qb=_unpad_heads(matmul_tn(dqap, cq, "gw_qb"), H_A, QK_NOPE + QK_ROPE, 0),
        w_kvb=jnp.stack([_unpad_heads(gkv[:H_A * HP], H_A, 64, 0).reshape(H_A, 64, KV_LORA),
                         _unpad_heads(gkv[H_A * HP:], H_A, 64, 0).reshape(H_A, 64, KV_LORA)], axis=1))
    gpack3 = packed(g3, PACK_G3)
    part3 = add_pairs(gpack3, exchange_sibling(gpack3), core)
    grad_x, st_mix, land3 = in_bwd(dz, xf, dx1, g_mix, w_in_t, tm, comm=scatter_chips_comm(part3))
    gshard3 = sum_chips(part3, land3, chip)
    off_g3, _ = _pack_offsets(PACK_G3)
    shards = {n: (gshard1, off_g1[n]) for n in PACK_G1}
    shards.update({n: (gshard2, off_g2[n]) for n in PACK_G2})
    shards.update({n: (gshard3, off_g3[n]) for n in PACK_G3})

    stats = allreduce_stats(st_mix, st_prep, st_mlp, st_ple)
    loss = jnp.sum(stats[ST_LOSS])

    out_g, out_d, out_m, out_v = {}, {}, {}, {}
    for name, (w, m, v) in mats.items():
        gshard, off = shards[name]
        r, c = w.shape[1:]
        if name in col_sharded:
            g2 = gshard[off:off + (r * c) // D_MODEL].reshape(c, r)
            in_kernel = r % 128 == 0 and c % 128 == 0
            res = adamw(w, g2 if in_kernel else g2.T, m, v, "adamw_" + name, g_transposed=in_kernel)
        else:
            res = adamw(w, gshard[off:off + r], m, v, "adamw_" + name)
        out_g[name], out_d[name], out_m[name], out_v[name] = res

    gains = (("g_mix", g_mix, m_g_mix, v_g_mix, ST_G_MIX), ("g_qa", g_qa, m_g_qa, v_g_qa, ST_G_QA),
             ("g_kva", g_kva, m_g_kva, v_g_kva, ST_G_KVA), ("g_qn", g_qn, m_g_qn, v_g_qn, ST_G_QN),
             ("g_kn", g_kn, m_g_kn, v_g_kn, ST_G_KN), ("g_mlp", g_mlp, m_g_mlp, v_g_mlp, ST_G_MLP),
             ("g_ple", g_ple, m_g_ple, v_g_ple, ST_G_PLE), ("g_final", g_final, m_g_final, v_g_final, ST_G_FINAL))
    res = adamw_gains(stats, [(r_, w.reshape(1, -1), m.reshape(1, -1), v.reshape(1, -1)) for _, w, m, v, r_ in gains])
    for (name, w, _, _, _), (gg, gd, gm, gv) in zip(gains, res):
        out_g[name], out_d[name], out_m[name], out_v[name] = (a.reshape(w.shape) for a in (gg, gd, gm, gv))

    order = ("g_mix", "w_in", "g_qa", "w_qb", "g_kva", "w_kvb", "g_qn", "g_kn", "w_oa", "w_ob", "w_o", "g_mlp",
             "w_up", "w_down", "g_ple", "w_ple_gate", "w_ple", "g_final")
    return (loss, grad_x.reshape(x.shape), *[out_g[n] for n in order], *[out_d[n] for n in order],
            *[out_m[n] for n in order], *[out_v[n] for n in order])
```

```python
import numpy as np
import jax
import jax.numpy as jnp
from jax import lax
from jax.experimental import pallas as pl
from jax.experimental.pallas import tpu as pltpu

F32 = jnp.float32
BF16 = jnp.bfloat16

D_MODEL = 1024
EPS = 1e-6
ROPE_THETA = 10000.0
GRID_W = 64
H_A = 8
QK_NOPE = 64
QK_ROPE = 32
V_DIM_A = 64
Q_LORA = 256
KV_LORA = 128
H_B = 8
KV_B = 2
HD_B = 64
D_FF = 4 * D_MODEL
PLE_DIM = 256
HP = 128
ZP = 4096
N_DEV = 8
N_CHIP = 4

ADAM_LR = 0.001
ADAM_B1 = 0.9
ADAM_B2 = 0.999
ADAM_EPS = 1e-08
ADAM_WD = 0.01
ADAM_STEP = 10

VMEM_LIMIT = 52 * 1024 * 1024

PACK_ROWS = dict(w_in=416, w_qb=32, w_kvb=16, w_oa=64, w_ob=64, w_o=128, w_up=512, w_down=512, w_ple_gate=128, w_ple=32)
PACK_W1 = ("w_in", "w_qb", "w_kvb")
PACK_W2 = ("w_up", "w_down", "w_o", "w_ple_gate", "w_oa", "w_ob", "w_ple")
PACK_G1 = ("w_up", "w_down", "w_ple_gate", "w_ple")
PACK_G2 = ("w_o", "w_oa", "w_ob")
PACK_G3 = ("w_in", "w_qb", "w_kvb")


def _pack_offsets(names):
    off, o = {}, 0
    for n in names:
        off[n] = o
        o += PACK_ROWS[n]
    return off, o

ST_G_MIX, ST_G_QA, ST_G_KVA, ST_G_QN, ST_G_KN, ST_G_MLP, ST_G_PLE, ST_G_FINAL, ST_LOSS = range(9)
ST_ROWS = 16


def _dot_nn(a, b):
    return lax.dot_general(a, b, (((1,), (0,)), ((), ())), preferred_element_type=F32)


def _dot_nt(a, b):
    return lax.dot_general(a, b, (((1,), (1,)), ((), ())), preferred_element_type=F32)


def _dot_tn(a, b):
    return lax.dot_general(a, b, (((0,), (0,)), ((), ())), preferred_element_type=F32)


def _rstd(x, n):
    return lax.rsqrt(jnp.sum(x * x, axis=-1, keepdims=True) * (1.0 / n) + EPS)


def _rms_bwd(dy, xh, r, g, n):
    dxh = dy * g
    return r * (dxh - xh * (jnp.sum(dxh * xh, axis=-1, keepdims=True) * (1.0 / n)))


def _rope_fwd(x, c, s1, s2):
    return x * c + pltpu.roll(x, HP - 16, 1) * s1 + pltpu.roll(x, 16, 1) * s2


def _rope_bwd(d, c, s1, s2):
    return d * c + pltpu.roll(d * s1, 16, 1) + pltpu.roll(d * s2, HP - 16, 1)


def _colsum(v):
    return jnp.sum(v, axis=0, keepdims=True)


def _params(sem=None, vmem=VMEM_LIMIT):
    return pltpu.CompilerParams(dimension_semantics=sem, vmem_limit_bytes=vmem)


def _resident(shape):
    nd = len(shape)
    return pl.BlockSpec(shape, lambda *_: (0,) * nd, pipeline_mode=pl.Buffered(1))


def _rows(tm, width, col=0):
    return pl.BlockSpec((tm, width), lambda i: (i, col))


def _packed_weight(rows, off):
    return pl.BlockSpec((N_DEV, rows, D_MODEL), lambda *_: (0, off // rows, 0), pipeline_mode=pl.Buffered(1))


def _wrows(ref, start, size):
    rows = ref.shape[1]
    return ref[start // rows:(start + size) // rows].reshape(size, D_MODEL)


def _mesh_pos():
    return lax.axis_index("x"), lax.axis_index("y"), lax.axis_index("c")


def _flip(v, bit):
    return (1 - v) if bit else v


_ANY = pl.BlockSpec(memory_space=pl.ANY)
_MESH = pl.DeviceIdType.MESH


def _remote(src, dst, send_sems, recv_sems, k, to):
    return pltpu.make_async_remote_copy(src_ref=src, dst_ref=dst, send_sem=send_sems.at[k], recv_sem=recv_sems.at[k],
                                        device_id=to, device_id_type=_MESH)


def _sibling_copies(g_ref, got_ref, send_sems, recv_sems):
    x, y, c = _mesh_pos()
    return [_remote(g_ref.at[2 * j + (1 - c)], got_ref.at[j], send_sems, recv_sems, j, (x, y, 1 - c)) for j in range(N_CHIP)]


def _chip_copies(p_ref, land_ref, send_sems, recv_sems):
    x, y, c = _mesh_pos()
    copies = []
    for k in (1, 2, 3):
        tx, ty = _flip(x, k & 2), _flip(y, k & 1)
        copies.append(_remote(p_ref.at[2 * tx + ty], land_ref.at[k - 1], send_sems, recv_sems, k - 1, (tx, ty, c)))
    return copies


class _Comm:
    def __init__(self, ins, out_shapes, sems, make, aliases=None):
        self.ins, self.out_shapes, self.sems, self.make, self.aliases = list(ins), list(out_shapes), list(sems), make, aliases or {}


def _comm_parts(comm, n_in, n_out):
    if comm is None:
        return [], [], [], [], {}
    alias = {n_in + j: n_out + k for j, k in comm.aliases.items()}
    return comm.ins, [_ANY] * len(comm.ins), comm.out_shapes, comm.sems, alias


def _split_refs(refs, n_in, n_out, n_scratch, comm):
    n_ci = len(comm.ins) if comm else 0
    n_co = len(comm.out_shapes) if comm else 0
    cuts, i = [], 0
    for n in (n_in, n_ci, n_out, n_co, n_scratch):
        cuts.append(refs[i:i + n])
        i += n
    return (*cuts, refs[i:])


def _grid_edge(grid, last):
    cond = None
    for d, n in enumerate(grid):
        here = pl.program_id(d) == (n - 1 if last else 0)
        cond = here if cond is None else cond & here
    return cond


def _comm_start(comm, cin, cout, csem, grid):
    if comm is not None:
        @pl.when(_grid_edge(grid, False))
        def _():
            for cp in comm.make(cin, cout, csem):
                cp.start()


def _comm_finish(comm, cin, cout, csem, grid):
    if comm is not None:
        @pl.when(_grid_edge(grid, True))
        def _():
            for cp in comm.make(cin, cout, csem):
                cp.wait()


def gather_first_comm(shard):
    r, w = shard.shape

    def make(cin, cout, sems):
        (x_ref,), (out_ref,), (send_sems, recv_sems, local_sem) = cin, cout, sems
        x, y, c = _mesh_pos()
        mine = out_ref.at[4 * x + 2 * y + c]
        targets = [(x, y, 1 - c), (1 - x, y, c), (x, 1 - y, c), (1 - x, 1 - y, c)]
        return [_remote(x_ref, mine, send_sems, recv_sems, k, to) for k, to in enumerate(targets)] + [
            pltpu.make_async_copy(x_ref, mine, local_sem)]

    return _Comm([shard], [jax.ShapeDtypeStruct((N_DEV, r, w), shard.dtype)],
                 [pltpu.SemaphoreType.DMA((4,)), pltpu.SemaphoreType.DMA((4,)), pltpu.SemaphoreType.DMA], make)


def gather_pass_comm(full):
    def make(cin, cout, sems):
        (in_ref,), (out_ref,), (send_sems, recv_sems) = cin, cout, sems
        x, y, c = _mesh_pos()
        copies = []
        for k, (px, py) in enumerate([(1 - x, y), (x, 1 - y), (1 - x, 1 - y)]):
            idx = 4 * px + 2 * py + c
            copies.append(_remote(in_ref.at[idx], out_ref.at[idx], send_sems, recv_sems, k, (x, y, 1 - c)))
        return copies

    return _Comm([full], [jax.ShapeDtypeStruct(full.shape, full.dtype)],
                 [pltpu.SemaphoreType.DMA((3,)), pltpu.SemaphoreType.DMA((3,))], make, aliases={0: 0})


def scatter_sibling_comm(g):
    _, r, w = g.shape
    return _Comm([g], [jax.ShapeDtypeStruct((N_CHIP, r, w), g.dtype)],
                 [pltpu.SemaphoreType.DMA((N_CHIP,)), pltpu.SemaphoreType.DMA((N_CHIP,))],
                 lambda cin, cout, sems: _sibling_copies(cin[0], cout[0], sems[0], sems[1]))


def scatter_chips_comm(part):
    _, r, w = part.shape
    return _Comm([part], [jax.ShapeDtypeStruct((N_CHIP - 1, r, w), part.dtype)],
                 [pltpu.SemaphoreType.DMA((3,)), pltpu.SemaphoreType.DMA((3,))],
                 lambda cin, cout, sems: _chip_copies(cin[0], cout[0], sems[0], sems[1]))


def exchange_sibling(g):
    _, r, w = g.shape

    def body(g_ref, got_ref, send_sems, recv_sems):
        copies = _sibling_copies(g_ref, got_ref, send_sems, recv_sems)
        for cp in copies:
            cp.start()
        for cp in copies:
            cp.wait()

    return pl.pallas_call(
        body, name="exchange_sibling", out_shape=jax.ShapeDtypeStruct((N_CHIP, r, w), g.dtype),
        in_specs=[_ANY], out_specs=_ANY,
        scratch_shapes=[pltpu.SemaphoreType.DMA((N_CHIP,)), pltpu.SemaphoreType.DMA((N_CHIP,))],
    )(g)


def exchange_chips(part):
    _, r, w = part.shape

    def body(p_ref, land_ref, send_sems, recv_sems):
        copies = _chip_copies(p_ref, land_ref, send_sems, recv_sems)
        for cp in copies:
            cp.start()
        for cp in copies:
            cp.wait()

    return pl.pallas_call(
        body, name="exchange_chips", out_shape=jax.ShapeDtypeStruct((N_CHIP - 1, r, w), part.dtype),
        in_specs=[_ANY], out_specs=_ANY,
        scratch_shapes=[pltpu.SemaphoreType.DMA((3,)), pltpu.SemaphoreType.DMA((3,))],
    )(part)


def allreduce_stats(st_mix, st_prep, st_mlp, st_ple):
    def body(mix_ref, prep_ref, mlp_ref, ple_ref, out_ref, mine, gath, send_sems, recv_sems):
        x, y, c = _mesh_pos()
        me = 4 * x + 2 * y + c
        mine[...] = jnp.zeros_like(mine)
        mine[ST_G_MIX:ST_G_MIX + 1, :] = mix_ref[...]
        mine[ST_G_QA:ST_G_KN + 1, 0:256] = prep_ref[...]
        mine[ST_G_MLP:ST_G_MLP + 1, :] = mlp_ref[...]
        mine[ST_G_PLE:ST_LOSS + 1, :] = ple_ref[...]
        gath[me] = mine[...]
        copies = []
        for k in range(1, N_DEV):
            peer = (_flip(x, k & 4), _flip(y, k & 2), _flip(c, k & 1))
            copies.append(_remote(mine, gath.at[me], send_sems, recv_sems, k - 1, peer))
        for cp in copies:
            cp.start()
        for cp in copies:
            cp.wait()
        acc = gath[0]
        for d in range(1, N_DEV):
            acc = acc + gath[d]
        out_ref[...] = acc

    vm = pl.BlockSpec(memory_space=pltpu.VMEM)
    return pl.pallas_call(
        body, name="allreduce_stats", out_shape=jax.ShapeDtypeStruct((ST_ROWS, D_MODEL), F32),
        in_specs=[vm] * 4, out_specs=vm,
        scratch_shapes=[pltpu.VMEM((ST_ROWS, D_MODEL), F32), pltpu.VMEM((N_DEV, ST_ROWS, D_MODEL), F32),
                        pltpu.SemaphoreType.DMA((N_DEV - 1,)), pltpu.SemaphoreType.DMA((N_DEV - 1,))],
    )(st_mix, st_prep, st_mlp, st_ple)


def adamw_gains(stats, gains):
    c1 = 1.0 - ADAM_B1 ** ADAM_STEP
    c2 = 1.0 - ADAM_B2 ** ADAM_STEP
    n = len(gains)

    def body(st_ref, *refs):
        ins, outs = refs[:3 * n], refs[3 * n:]
        for i, (row, w, _, _) in enumerate(gains):
            width = w.shape[1]
            gv = st_ref[row:row + 1, 0:width]
            mn = ADAM_B1 * ins[3 * i + 1][...] + (1.0 - ADAM_B1) * gv
            vn = ADAM_B2 * ins[3 * i + 2][...] + (1.0 - ADAM_B2) * (gv * gv)
            outs[4 * i][...] = gv
            outs[4 * i + 1][...] = -ADAM_LR * ((mn / c1) / (jnp.sqrt(vn / c2) + ADAM_EPS) + ADAM_WD * ins[3 * i][...])
            outs[4 * i + 2][...] = mn
            outs[4 * i + 3][...] = vn

    vm = pl.BlockSpec(memory_space=pltpu.VMEM)
    flat = [a for (_, w, m, v) in gains for a in (w, m, v)]
    out_shape = tuple(jax.ShapeDtypeStruct(w.shape, F32) for (_, w, _, _) in gains for _ in range(4))
    res = pl.pallas_call(body, name="adamw_gains", out_shape=out_shape, in_specs=[vm] * (1 + 3 * n),
                         out_specs=tuple([vm] * (4 * n)))(stats, *flat)
    return [res[4 * i:4 * i + 4] for i in range(n)]


def _row_tile(r, cap=640):
    return max(d for d in range(16, min(r, cap) + 1, 16) if r % d == 0)


def add_pairs(g, got, core):
    n, r, w = got.shape
    tr = _row_tile(r)

    def body(c_ref, a_ref, b_ref, o_ref):
        o_ref[...] = (a_ref[...].astype(F32) + b_ref[...].astype(F32)).astype(o_ref.dtype)

    spec = pl.BlockSpec((1, tr, w), lambda i, j, c: (i, j, 0))
    return pl.pallas_call(
        body, name="add_pairs", out_shape=jax.ShapeDtypeStruct(got.shape, got.dtype),
        grid_spec=pltpu.PrefetchScalarGridSpec(
            num_scalar_prefetch=1, grid=(n, r // tr),
            in_specs=[pl.BlockSpec((1, tr, w), lambda i, j, c: (2 * i + c[0], j, 0)), spec], out_specs=spec),
        compiler_params=_params(("parallel", "parallel")),
    )(core, g, got)


def sum_chips(part, land, chip):
    _, r, w = part.shape
    tr = _row_tile(r)

    def body(c_ref, p_ref, l_ref, o_ref):
        acc = p_ref[0].astype(F32)
        for s in range(N_CHIP - 1):
            acc = acc + l_ref[s].astype(F32)
        o_ref[...] = acc

    return pl.pallas_call(
        body, name="sum_chips", out_shape=jax.ShapeDtypeStruct((r, w), F32),
        grid_spec=pltpu.PrefetchScalarGridSpec(
            num_scalar_prefetch=1, grid=(r // tr,),
            in_specs=[pl.BlockSpec((1, tr, w), lambda i, c: (c[0], i, 0)), pl.BlockSpec((N_CHIP - 1, tr, w), lambda i, c: (0, i, 0))],
            out_specs=pl.BlockSpec((tr, w), lambda i, c: (i, 0))),
        compiler_params=_params(("parallel",)),
    )(chip, part, land)


def norm_x(x, g_mix, tm, comm=None):
    t = x.shape[0]
    grid = (t // tm,)
    c_ins, c_in_specs, c_outs, c_sems, alias = _comm_parts(comm, 2, 1)

    def body(*refs):
        (x_ref, g_ref), cin, (h_ref,), cout, _, csem = _split_refs(refs, 2, 1, 0, comm)
        _comm_start(comm, cin, cout, csem, grid)
        xv = x_ref[...]
        h_ref[...] = (xv * _rstd(xv, D_MODEL) * g_ref[...]).astype(BF16)
        _comm_finish(comm, cin, cout, csem, grid)

    return pl.pallas_call(
        body, name="norm_x", grid=grid, out_shape=(jax.ShapeDtypeStruct((t, D_MODEL), BF16), *c_outs),
        in_specs=[_rows(tm, D_MODEL), _resident((1, D_MODEL)), *c_in_specs],
        out_specs=(_rows(tm, D_MODEL), *([_ANY] * len(c_outs))),
        scratch_shapes=c_sems, input_output_aliases=alias, compiler_params=_params(("arbitrary",)),
    )(x, g_mix, *c_ins)


def pack_late_weights(w_up, w_down, w_o, w_pg, small, comm=None):
    rows = sum(PACK_ROWS[n] for n in PACK_W2)
    c_ins, c_in_specs, c_outs, c_sems, alias = _comm_parts(comm, 5, 1)
    grid = (1,)

    def body(*refs):
        (up_ref, dn_ref, o_ref, pg_ref, sm_ref), cin, (out_ref,), cout, _, csem = _split_refs(refs, 5, 1, 0, comm)
        _comm_start(comm, cin, cout, csem, grid)
        out_ref[0:512, :] = up_ref[0].T.astype(BF16)
        out_ref[512:1024, :] = dn_ref[0].astype(BF16)
        out_ref[1024:1152, :] = o_ref[0].astype(BF16)
        out_ref[1152:1280, :] = pg_ref[0].astype(BF16)
        out_ref[1280:rows, :] = sm_ref[...]
        _comm_finish(comm, cin, cout, csem, grid)

    def whole(a):
        nd = a.ndim
        return pl.BlockSpec(a.shape, lambda i: (0,) * nd)

    args = (w_up, w_down, w_o, w_pg, small)
    return pl.pallas_call(
        body, name="pack_late_weights", grid=grid, out_shape=(jax.ShapeDtypeStruct((rows, D_MODEL), BF16), *c_outs),
        in_specs=[*[whole(a) for a in args], *c_in_specs],
        out_specs=(pl.BlockSpec((rows, D_MODEL), lambda i: (0, 0)), *([_ANY] * len(c_outs))),
        scratch_shapes=c_sems, input_output_aliases=alias, compiler_params=_params(("arbitrary",)),
    )(*args, *c_ins)


def in_proj(h, w_in_t, tm):
    t = h.shape[0]
    nc = 512

    def body(h_ref, w_ref, z_ref):
        hv = h_ref[...]
        for cidx in range(ZP // nc):
            z_ref[:, cidx * nc:(cidx + 1) * nc] = _dot_nt(hv, w_ref[cidx * nc:(cidx + 1) * nc, :])

    return pl.pallas_call(
        body, name="in_proj", grid=(t // tm,), out_shape=jax.ShapeDtypeStruct((t, ZP), F32),
        in_specs=[_rows(tm, D_MODEL), _resident((ZP, D_MODEL))],
        out_specs=_rows(tm, ZP), compiler_params=_params(("parallel",)),
    )(h, w_in_t)


def attn_prep(zp, tabs, g_qa, g_kva, g_qn, g_kn, w_qb_t, w_kvb_t, tm, s_len):
    t = zp.shape[0]
    nsb = s_len // tm
    scale_a = (QK_NOPE + QK_ROPE) ** -0.5
    scale_b = HD_B ** -0.5

    def body(qb_ref, qlat_ref, kb_ref, vb_ref, ckv_ref, kpe_ref, tab_ref, gqa_ref, gkva_ref, gqn_ref, gkn_ref,
             wqb_ref, wkvb_ref, qa_o, ka_o, va_o, qb_o, kb_o, vb_o, cq_o, ckvn_o):
        ca, s1a, s2a = tab_ref[0], tab_ref[1], tab_ref[2]
        ck = tab_ref[3]
        cb, s1b, s2b = tab_ref[4], tab_ref[5], tab_ref[6]
        ql = qlat_ref[...]
        cq = (ql * _rstd(ql, Q_LORA) * gqa_ref[...]).astype(BF16)
        cq_o[...] = cq
        qa = _dot_nt(cq, wqb_ref[...])
        slabs = [slice(h * HP, (h + 1) * HP) for h in range(H_A)]
        qa_o[...] = jnp.concatenate(
            [(_rope_fwd(qa[:, sl], ca, s1a, s2a) * scale_a).astype(BF16) for sl in slabs], axis=1)
        cr = ckv_ref[...]
        ckv = (cr * _rstd(cr, KV_LORA) * gkva_ref[...]).astype(BF16)
        ckvn_o[...] = ckv
        kva = _dot_nt(ckv, wkvb_ref[...])
        kpe = _rope_fwd(kpe_ref[...], ck, s1a, s2a)
        ka_o[...] = jnp.concatenate([(kva[:, sl] + kpe).astype(BF16) for sl in slabs], axis=1)
        va_o[...] = kva[:, H_A * HP:].astype(BF16)
        gqn, gkn = gqn_ref[...], gkn_ref[...]

        def norm_rope(ref, sl, g, scale):
            xs = ref[:, sl]
            y = _rope_fwd(xs * _rstd(xs, HD_B) * g, cb, s1b, s2b)
            return (y if scale is None else y * scale).astype(BF16)

        qb_o[...] = jnp.concatenate([norm_rope(qb_ref, sl, gqn, scale_b) for sl in slabs], axis=1)
        kb_o[...] = jnp.concatenate([norm_rope(kb_ref, sl, gkn, None) for sl in slabs[:KV_B]], axis=1)
        vb_o[...] = vb_ref[...].astype(BF16)

    def o(width):
        return jax.ShapeDtypeStruct((t, width), BF16)

    return pl.pallas_call(
        body, name="attn_prep", grid=(t // tm,),
        out_shape=(o(H_A * HP), o(H_A * HP), o(H_A * HP), o(H_B * HP), o(KV_B * HP), o(KV_B * HP), o(Q_LORA), o(KV_LORA)),
        in_specs=[_rows(tm, 1024, 2), _rows(tm, 256, 12), _rows(tm, 256, 13), _rows(tm, 256, 14),
                  _rows(tm, 128, 30), _rows(tm, 128, 31),
                  pl.BlockSpec((7, tm, HP), lambda i: (0, i % nsb, 0)),
                  _resident((1, Q_LORA)), _resident((1, KV_LORA)), _resident((1, HP)), _resident((1, HP)),
                  _resident((H_A * HP, Q_LORA)), _resident((2 * H_A * HP, KV_LORA))],
        out_specs=(_rows(tm, H_A * HP), _rows(tm, H_A * HP), _rows(tm, H_A * HP), _rows(tm, H_B * HP),
                   _rows(tm, KV_B * HP), _rows(tm, KV_B * HP), _rows(tm, Q_LORA), _rows(tm, KV_LORA)),
        compiler_params=_params(("parallel",)),
    )(zp, zp, zp, zp, zp, zp, tabs, g_qa, g_kva, g_qn, g_kn, w_qb_t, w_kvb_t)


def attn_fwd(q, k, v, n_b, s_len, tq, name, comm=None):
    t = q.shape[0]
    n_h, n_hk = q.shape[1] // HP, k.shape[1] // HP
    grp = n_h // n_hk
    nq = s_len // tq
    sub = min(tq, 256)
    grid = (n_b, n_h, nq)
    c_ins, c_in_specs, c_outs, c_sems, alias = _comm_parts(comm, 3, 2)

    def body(*refs):
        (q_ref, k_ref, v_ref), cin, (o_ref, lse_ref), cout, _, csem = _split_refs(refs, 3, 2, 0, comm)
        _comm_start(comm, cin, cout, csem, grid)
        kv, vv = k_ref[...], v_ref[...]
        for r in range(tq // sub):
            rows = slice(r * sub, (r + 1) * sub)
            s = _dot_nt(q_ref[rows, :], kv)
            m = jnp.max(s, axis=-1, keepdims=True)
            p = jnp.exp(s - m)
            l = jnp.sum(p, axis=-1, keepdims=True)
            o_ref[rows, :] = (_dot_nn(p.astype(BF16), vv) * (1.0 / l)).astype(o_ref.dtype)
            lse_ref[rows, :] = jnp.broadcast_to(m + jnp.log(l), (sub, HP))
        _comm_finish(comm, cin, cout, csem, grid)

    qspec = pl.BlockSpec((tq, HP), lambda b, h, i: (b * nq + i, h))
    kspec = pl.BlockSpec((s_len, HP), lambda b, h, i: (b, h // grp))
    return pl.pallas_call(
        body, name=name, grid=grid,
        out_shape=(jax.ShapeDtypeStruct((t, n_h * HP), BF16), jax.ShapeDtypeStruct((t, n_h * HP), F32), *c_outs),
        in_specs=[qspec, kspec, kspec, *c_in_specs], out_specs=(qspec, qspec, *([_ANY] * len(c_outs))),
        scratch_shapes=c_sems, input_output_aliases=alias,
        compiler_params=_params(("arbitrary", "arbitrary", "arbitrary")),
    )(q, k, v, *c_ins)


def merge_fwd(oa, ob, zp, x, w_oa_t, w_ob_t, wpack, off, tm):
    t = x.shape[0]

    def body(oa_ref, ob_ref, ga_ref, gb_ref, x_ref, woa_ref, wob_ref, wo_ref, x1_o, mg_o, y_o):
        ya = _dot_nt(oa_ref[...], woa_ref[...])
        yb = _dot_nt(ob_ref[...], wob_ref[...])
        y_o[:, 0:D_MODEL] = ya.astype(BF16)
        y_o[:, D_MODEL:2 * D_MODEL] = yb.astype(BF16)
        merged = (jax.nn.sigmoid(ga_ref[...]) * ya + jax.nn.sigmoid(gb_ref[...]) * yb).astype(BF16)
        mg_o[...] = merged
        x1_o[...] = x_ref[...] + _dot_nn(merged, _wrows(wo_ref, 0, D_MODEL))

    return pl.pallas_call(
        body, name="merge_fwd", grid=(t // tm,),
        out_shape=(jax.ShapeDtypeStruct((t, D_MODEL), F32), jax.ShapeDtypeStruct((t, D_MODEL), BF16),
                   jax.ShapeDtypeStruct((t, 2 * D_MODEL), BF16)),
        in_specs=[_rows(tm, H_A * HP), _rows(tm, H_B * HP), _rows(tm, 1024, 0), _rows(tm, 1024, 1), _rows(tm, D_MODEL),
                  _resident((D_MODEL, H_A * HP)), _resident((D_MODEL, H_B * HP)), _packed_weight(128, off["w_o"])],
        out_specs=(_rows(tm, D_MODEL), _rows(tm, D_MODEL), _rows(tm, 2 * D_MODEL)), compiler_params=_params(("parallel",)),
    )(oa, ob, zp, zp, x, w_oa_t, w_ob_t, wpack)


def mlp_fwd(x1, g_mlp, wpack, off, tm):
    t = x1.shape[0]
    fc = 1024

    def body(x_ref, g_ref, wup_ref, wdn_ref, x2_o, u_o):
        xv = x_ref[...]
        h2 = (xv * _rstd(xv, D_MODEL) * g_ref[...]).astype(BF16)
        acc = xv
        for cidx in range(D_FF // fc):
            sl = slice(cidx * fc, (cidx + 1) * fc)
            u = jnp.maximum(_dot_nt(h2, _wrows(wup_ref, cidx * fc, fc)), 0.0)
            u_o[:, sl] = u.astype(BF16)
            acc = acc + _dot_nn((u * u).astype(BF16), _wrows(wdn_ref, cidx * fc, fc))
        x2_o[...] = acc

    return pl.pallas_call(
        body, name="mlp_fwd", grid=(t // tm,),
        out_shape=(jax.ShapeDtypeStruct((t, D_MODEL), F32), jax.ShapeDtypeStruct((t, D_FF), BF16)),
        in_specs=[_rows(tm, D_MODEL), _resident((1, D_MODEL)), _packed_weight(512, off["w_up"]), _packed_weight(512, off["w_down"])],
        out_specs=(_rows(tm, D_MODEL), _rows(tm, D_FF)), compiler_params=_params(("parallel",)),
    )(x1, g_mlp, wpack, wpack)


def ple_loss_bwd(x2, p, tgt, g_ple, g_final, wpack, off, w_ple_t, tm):
    t = x2.shape[0]
    inv_d = 1.0 / D_MODEL

    def body(x2_ref, p_ref, tg_ref, gp_ref, gf_ref, wpg_ref, wple_ref, dx2_o, dt_o, h3_o, dpe_o, st_o, dx2b_o):
        @pl.when(pl.program_id(0) == 0)
        def _():
            st_o[...] = jnp.zeros_like(st_o)

        x2v = x2_ref[...]
        gp, gf = gp_ref[...], gf_ref[...]
        w_pg = _wrows(wpg_ref, 0, D_MODEL)
        r2 = _rstd(x2v, D_MODEL)
        xh2 = x2v * r2
        h3 = (xh2 * gp).astype(BF16)
        h3_o[...] = h3
        gate = jax.nn.sigmoid(_dot_nn(h3, w_pg))
        pe = _dot_nt(p_ref[...].astype(BF16), wple_ref[...])
        x3 = x2v + gate * pe
        r3 = _rstd(x3, D_MODEL)
        xh3 = x3 * r3
        err = xh3 * gf - tg_ref[...]
        dy = err * inv_d
        dx3 = _rms_bwd(dy, xh3, r3, gf, D_MODEL)
        dpe_o[...] = (dx3 * gate).astype(BF16)
        dt = (dx3 * pe * gate * (1.0 - gate)).astype(BF16)
        dt_o[...] = dt
        dh3 = _dot_nt(dt, w_pg)
        dx2 = dx3 + _rms_bwd(dh3, xh2, r2, gp, D_MODEL)
        dx2_o[...] = dx2
        dx2b_o[...] = dx2.astype(BF16)
        st_o[0:1, :] += _colsum(dh3 * xh2)
        st_o[1:2, :] += _colsum(dy * xh3)
        st_o[2:3, :] += _colsum(err * err) * (0.5 * inv_d)

    bf = jax.ShapeDtypeStruct((t, D_MODEL), BF16)
    return pl.pallas_call(
        body, name="ple_loss_bwd", grid=(t // tm,),
        out_shape=(jax.ShapeDtypeStruct((t, D_MODEL), F32), bf, bf, bf, jax.ShapeDtypeStruct((3, D_MODEL), F32), bf),
        in_specs=[_rows(tm, D_MODEL), _rows(tm, PLE_DIM), _rows(tm, D_MODEL), _resident((1, D_MODEL)), _resident((1, D_MODEL)),
                  _packed_weight(128, off["w_ple_gate"]), _resident((D_MODEL, PLE_DIM))],
        out_specs=(_rows(tm, D_MODEL), _rows(tm, D_MODEL), _rows(tm, D_MODEL), _rows(tm, D_MODEL),
                   pl.BlockSpec((3, D_MODEL), lambda i: (0, 0)), _rows(tm, D_MODEL)),
        compiler_params=_params(("arbitrary",)),
    )(x2, p, tgt, g_ple, g_final, wpack, w_ple_t)


def mlp_bwd(dx2, x1, u, g_mlp, wpack, off, tm):
    t = x1.shape[0]
    fc = 1024

    def body(dx2_ref, x1_ref, u_ref, g_ref, wup_ref, wdn_ref, dx1_o, da_o, h2_o, st_o, dx1b_o):
        @pl.when(pl.program_id(0) == 0)
        def _():
            st_o[...] = jnp.zeros_like(st_o)

        d2 = dx2_ref[...]
        d2b = d2.astype(BF16)
        dh2 = jnp.zeros((tm, D_MODEL), F32)
        for cidx in range(D_FF // fc):
            sl = slice(cidx * fc, (cidx + 1) * fc)
            da = (_dot_nt(d2b, _wrows(wdn_ref, cidx * fc, fc)) * (2.0 * u_ref[:, sl].astype(F32))).astype(BF16)
            da_o[:, sl] = da
            dh2 = dh2 + _dot_nn(da, _wrows(wup_ref, cidx * fc, fc))
        xv = x1_ref[...]
        g = g_ref[...]
        r1 = _rstd(xv, D_MODEL)
        xh1 = xv * r1
        h2_o[...] = (xh1 * g).astype(BF16)
        st_o[...] += _colsum(dh2 * xh1)
        dx1 = d2 + _rms_bwd(dh2, xh1, r1, g, D_MODEL)
        dx1_o[...] = dx1
        dx1b_o[...] = dx1.astype(BF16)

    return pl.pallas_call(
        body, name="mlp_bwd", grid=(t // tm,),
        out_shape=(jax.ShapeDtypeStruct((t, D_MODEL), F32), jax.ShapeDtypeStruct((t, D_FF), BF16),
                   jax.ShapeDtypeStruct((t, D_MODEL), BF16), jax.ShapeDtypeStruct((1, D_MODEL), F32),
                   jax.ShapeDtypeStruct((t, D_MODEL), BF16)),
        in_specs=[_rows(tm, D_MODEL), _rows(tm, D_MODEL), _rows(tm, D_FF), _resident((1, D_MODEL)),
                  _packed_weight(512, off["w_up"]), _packed_weight(512, off["w_down"])],
        out_specs=(_rows(tm, D_MODEL), _rows(tm, D_FF), _rows(tm, D_MODEL), pl.BlockSpec((1, D_MODEL), lambda i: (0, 0)),
                   _rows(tm, D_MODEL)),
        compiler_params=_params(("arbitrary",)),
    )(dx2, x1, u, g_mlp, wpack, wpack)


def merge_bwd(dx1b, yab, zp, w_oa_t, w_ob_t, wpack, off, tm, comm=None):
    t = dx1b.shape[0]
    grid = (t // tm,)
    c_ins, c_in_specs, c_outs, c_sems, alias = _comm_parts(comm, 7, 5)

    def body(*refs):
        ((dx1_ref, y_ref, ga_ref, gb_ref, woa_ref, wob_ref, wo_ref), cin,
         (doa_o, dob_o, dg_o, dya_o, dyb_o), cout, _, csem) = _split_refs(refs, 7, 5, 0, comm)
        _comm_start(comm, cin, cout, csem, grid)
        dm = _dot_nt(dx1_ref[...], _wrows(wo_ref, 0, D_MODEL))
        for g_ref, w_ref, do_o, dy_o, col in ((ga_ref, woa_ref, doa_o, dya_o, 0), (gb_ref, wob_ref, dob_o, dyb_o, 1)):
            cols = slice(col * D_MODEL, (col + 1) * D_MODEL)
            sg = jax.nn.sigmoid(g_ref[...])
            dyv = (dm * sg).astype(BF16)
            dy_o[...] = dyv
            dg_o[:, cols] = (dm * y_ref[:, cols].astype(F32) * sg * (1.0 - sg)).astype(BF16)
            do_o[...] = _dot_nn(dyv, w_ref[...]).astype(BF16)
        _comm_finish(comm, cin, cout, csem, grid)

    bf = jax.ShapeDtypeStruct((t, D_MODEL), BF16)
    return pl.pallas_call(
        body, name="merge_bwd", grid=grid,
        out_shape=(bf, bf, jax.ShapeDtypeStruct((t, ZP), BF16), bf, bf, *c_outs),
        in_specs=[_rows(tm, D_MODEL), _rows(tm, 2 * D_MODEL), _rows(tm, 1024, 0), _rows(tm, 1024, 1),
                  _resident((D_MODEL, H_A * HP)), _resident((D_MODEL, H_B * HP)), _packed_weight(128, off["w_o"]), *c_in_specs],
        out_specs=(_rows(tm, D_MODEL), _rows(tm, D_MODEL), _rows(tm, 2 * D_MODEL), _rows(tm, D_MODEL), _rows(tm, D_MODEL),
                   *([_ANY] * len(c_outs))),
        scratch_shapes=c_sems, input_output_aliases=alias,
        compiler_params=_params(("arbitrary",)),
    )(dx1b, yab, zp, zp, w_oa_t, w_ob_t, wpack, *c_ins)


def attn_bwd(q, k, v, do, o, lse, n_b, s_len, tq, name, comm=None):
    t = q.shape[0]
    n_h, n_hk = q.shape[1] // HP, k.shape[1] // HP
    grp = n_h // n_hk
    nq = s_len // tq
    sub = min(tq, 256)
    grid = (n_b, n_hk, grp, nq)
    c_ins, c_in_specs, c_outs, c_sems, alias = _comm_parts(comm, 6, 3)

    def body(*refs):
        ((q_ref, k_ref, v_ref, do_ref, o_ref, lse_ref), cin, (dq_o, dk_o, dv_o), cout, (p_s, ds_s, dk_acc, dv_acc),
         csem) = _split_refs(refs, 6, 3, 4, comm)
        _comm_start(comm, cin, cout, csem, grid)

        @pl.when((pl.program_id(2) == 0) & (pl.program_id(3) == 0))
        def _():
            dk_acc[...] = jnp.zeros_like(dk_acc)
            dv_acc[...] = jnp.zeros_like(dv_acc)

        kv, vv = k_ref[...], v_ref[...]
        for r in range(tq // sub):
            rows = slice(r * sub, (r + 1) * sub)
            qv, dov = q_ref[rows, :], do_ref[rows, :]
            delta = jnp.sum(dov.astype(F32) * o_ref[rows, :].astype(F32), axis=-1, keepdims=True)
            p = jnp.exp(_dot_nt(qv, kv) - lse_ref[rows, 0:1])
            ds = (p * (_dot_nt(dov, vv) - delta)).astype(BF16)
            p_s[rows, :] = p.astype(BF16)
            ds_s[rows, :] = ds
            dq_o[rows, :] = _dot_nn(ds, kv).astype(dq_o.dtype)
        dk_acc[...] += _dot_tn(ds_s[...], q_ref[...])
        dv_acc[...] += _dot_tn(p_s[...], do_ref[...])

        @pl.when((pl.program_id(2) == grp - 1) & (pl.program_id(3) == nq - 1))
        def _():
            dk_o[...] = dk_acc[...].astype(dk_o.dtype)
            dv_o[...] = dv_acc[...].astype(dv_o.dtype)

        _comm_finish(comm, cin, cout, csem, grid)

    qspec = pl.BlockSpec((tq, HP), lambda b, hk, g, i: (b * nq + i, hk * grp + g))
    kspec = pl.BlockSpec((s_len, HP), lambda b, hk, g, i: (b, hk))
    return pl.pallas_call(
        body, name=name, grid=grid,
        out_shape=(jax.ShapeDtypeStruct((t, n_h * HP), BF16), jax.ShapeDtypeStruct((t, n_hk * HP), BF16),
                   jax.ShapeDtypeStruct((t, n_hk * HP), BF16), *c_outs),
        in_specs=[qspec, kspec, kspec, qspec, qspec, qspec, *c_in_specs],
        out_specs=(qspec, kspec, kspec, *([_ANY] * len(c_outs))),
        scratch_shapes=[pltpu.VMEM((tq, s_len), BF16), pltpu.VMEM((tq, s_len), BF16),
                        pltpu.VMEM((s_len, HP), F32), pltpu.VMEM((s_len, HP), F32), *c_sems],
        input_output_aliases=alias,
        compiler_params=_params(("arbitrary", "arbitrary", "arbitrary", "arbitrary")),
    )(q, k, v, do, o, lse, *c_ins)


def prep_bwd(dqa, dka, dva, dqb, dkb, dvb, zp, dz, tabs, g_qa, g_kva, g_qn, g_kn, w_qb_t, w_kvb_t, tm, s_len):
    t = zp.shape[0]
    nsb = s_len // tm
    scale_a = (QK_NOPE + QK_ROPE) ** -0.5
    scale_b = HD_B ** -0.5

    def body(dqa_ref, dka_ref, dva_ref, dqb_ref, dkb_ref, dvb_ref, qb_ref, qlat_ref, kb_ref, ckv_ref, tab_ref,
             gqa_ref, gkva_ref, gqn_ref, gkn_ref, wqb_ref, wkvb_ref, _, dz_o, dqap_o, dkva_o, st_o):
        dzq_o, dsm_o = dz_o.at[:, 0:1024], dz_o.at[:, 1024:2048]

        @pl.when(pl.program_id(0) == 0)
        def _():
            st_o[...] = jnp.zeros_like(st_o)

        ca, s1a, s2a = tab_ref[0], tab_ref[1], tab_ref[2]
        ck = tab_ref[3]
        cb, s1b, s2b = tab_ref[4], tab_ref[5], tab_ref[6]
        for h in range(H_A):
            sl = slice(h * HP, (h + 1) * HP)
            dqap_o[:, sl] = _rope_bwd(dqa_ref[:, sl].astype(F32) * scale_a, ca, s1a, s2a).astype(BF16)
        dcq = _dot_nn(dqap_o[...], wqb_ref[...])
        ql = qlat_ref[...]
        rq = _rstd(ql, Q_LORA)
        xh = ql * rq
        gqa = gqa_ref[...]
        st_o[0:1, :] += _colsum(dcq * xh)
        dsm_o[:, 0:256] = _rms_bwd(dcq, xh, rq, gqa, Q_LORA).astype(BF16)
        dkpe = jnp.zeros((tm, HP), F32)
        for h in range(H_A):
            sl = slice(h * HP, (h + 1) * HP)
            dk = dka_ref[:, sl]
            dkpe = dkpe + dk.astype(F32)
            dkva_o[:, sl] = dk.astype(BF16)
        dkva_o[:, H_A * HP:] = dva_ref[...].astype(BF16)
        dsm_o[:, 896:1024] = _rope_bwd(dkpe, ck, s1a, s2a).astype(BF16)
        dckv = _dot_nn(dkva_o[...], wkvb_ref[...])
        cr = ckv_ref[...]
        rk = _rstd(cr, KV_LORA)
        xh = cr * rk
        st_o[1:2, 0:128] += _colsum(dckv * xh)
        dsm_o[:, 768:896] = _rms_bwd(dckv, xh, rk, gkva_ref[...], KV_LORA).astype(BF16)
        gqn, gkn = gqn_ref[...], gkn_ref[...]
        dgq = jnp.zeros((1, HP), F32)
        for h in range(H_B):
            sl = slice(h * HP, (h + 1) * HP)
            dy = _rope_bwd(dqb_ref[:, sl].astype(F32) * scale_b, cb, s1b, s2b)
            xs = qb_ref[:, sl]
            r = _rstd(xs, HD_B)
            xh = xs * r
            dgq = dgq + _colsum(dy * xh)
            dzq_o[:, sl] = _rms_bwd(dy, xh, r, gqn, HD_B).astype(BF16)
        st_o[2:3, 0:128] += dgq
        dgk = jnp.zeros((1, HP), F32)
        for h in range(KV_B):
            sl = slice(h * HP, (h + 1) * HP)
            dy = _rope_bwd(dkb_ref[:, sl].astype(F32), cb, s1b, s2b)
            xs = kb_ref[:, sl]
            r = _rstd(xs, HD_B)
            xh = xs * r
            dgk = dgk + _colsum(dy * xh)
            dsm_o[:, 256 + h * HP:256 + (h + 1) * HP] = _rms_bwd(dy, xh, r, gkn, HD_B).astype(BF16)
        st_o[3:4, 0:128] += dgk
        dsm_o[:, 512:768] = dvb_ref[...].astype(BF16)

    return pl.pallas_call(
        body, name="prep_bwd", grid=(t // tm,),
        out_shape=(jax.ShapeDtypeStruct((t, ZP), BF16), jax.ShapeDtypeStruct((t, 1024), BF16),
                   jax.ShapeDtypeStruct((t, 2048), BF16), jax.ShapeDtypeStruct((4, 256), F32)),
        in_specs=[_rows(tm, 1024), _rows(tm, 1024), _rows(tm, 1024), _rows(tm, 1024), _rows(tm, 256), _rows(tm, 256),
                  _rows(tm, 1024, 2), _rows(tm, 256, 12), _rows(tm, 256, 13), _rows(tm, 128, 30),
                  pl.BlockSpec((7, tm, HP), lambda i: (0, i % nsb, 0)),
                  _resident((1, Q_LORA)), _resident((1, KV_LORA)), _resident((1, HP)), _resident((1, HP)),
                  _resident((H_A * HP, Q_LORA)), _resident((2 * H_A * HP, KV_LORA)), _ANY],
        out_specs=(_rows(tm, 2048, 1), _rows(tm, 1024), _rows(tm, 2048), pl.BlockSpec((4, 256), lambda i: (0, 0))),
        input_output_aliases={17: 0}, compiler_params=_params(("arbitrary",)),
    )(dqa, dka, dva, dqb, dkb, dvb, zp, zp, zp, zp, tabs, g_qa, g_kva, g_qn, g_kn, w_qb_t, w_kvb_t, dz)


def in_bwd(dz, x, dx1, g_mix, w_in_t, tm, comm=None):
    t = x.shape[0]
    grid = (t // tm,)
    c_ins, c_in_specs, c_outs, c_sems, alias = _comm_parts(comm, 5, 2)

    def body(*refs):
        (dz_ref, x_ref, dx1_ref, g_ref, w_ref), cin, (dx_o, st_o), cout, _, csem = _split_refs(refs, 5, 2, 0, comm)
        _comm_start(comm, cin, cout, csem, grid)

        @pl.when(pl.program_id(0) == 0)
        def _():
            st_o[...] = jnp.zeros_like(st_o)

        dh = _dot_nn(dz_ref[...], w_ref[...])
        xv = x_ref[...]
        g = g_ref[...]
        r = _rstd(xv, D_MODEL)
        xh = xv * r
        st_o[...] += _colsum(dh * xh)
        dx_o[...] = dx1_ref[...] + _rms_bwd(dh, xh, r, g, D_MODEL)
        _comm_finish(comm, cin, cout, csem, grid)

    return pl.pallas_call(
        body, name="in_bwd", grid=grid,
        out_shape=(jax.ShapeDtypeStruct((t, D_MODEL), F32), jax.ShapeDtypeStruct((1, D_MODEL), F32), *c_outs),
        in_specs=[_rows(tm, ZP), _rows(tm, D_MODEL), _rows(tm, D_MODEL),
                  _resident((1, D_MODEL)), _resident((ZP, D_MODEL)), *c_in_specs],
        out_specs=(_rows(tm, D_MODEL), pl.BlockSpec((1, D_MODEL), lambda i: (0, 0)), *([_ANY] * len(c_outs))),
        scratch_shapes=c_sems, input_output_aliases=alias,
        compiler_params=_params(("arbitrary",)),
    )(dz, x, dx1, g_mix, w_in_t, *c_ins)


def matmul_tn(a, b, name, square_a=False):
    t, m = a.shape
    n = b.shape[1]
    bm = min(m, 512)
    tk = min(t, 4096)

    def body(a_ref, b_ref, o_ref):
        @pl.when(pl.program_id(1) == 0)
        def _():
            o_ref[...] = jnp.zeros_like(o_ref)

        av = a_ref[...]
        if square_a:
            av = (av.astype(F32) * av.astype(F32))
        o_ref[...] += _dot_tn(av.astype(BF16), b_ref[...].astype(BF16))

    return pl.pallas_call(
        body, name=name, grid=(m // bm, t // tk), out_shape=jax.ShapeDtypeStruct((m, n), F32),
        in_specs=[pl.BlockSpec((tk, bm), lambda i, kk: (kk, i)), pl.BlockSpec((tk, n), lambda i, kk: (kk, 0))],
        out_specs=pl.BlockSpec((bm, n), lambda i, kk: (i, 0)),
        compiler_params=_params(("parallel", "arbitrary")),
    )(a, b)


def matmul_tn_packed(a, b, name, rows, row_off, total_rows, buf=None, square_a=False):
    t, m = a.shape
    n = b.shape[1]
    pd = max(1, 512 // rows)
    bm = pd * rows
    tk = min(t, 4096)
    nk = t // tk

    def body(a_ref, b_ref, *rest):
        o_ref, acc = rest[-2], rest[-1]

        @pl.when(pl.program_id(1) == 0)
        def _():
            acc[...] = jnp.zeros_like(acc)

        av = a_ref[...]
        if square_a:
            av = (av.astype(F32) * av.astype(F32))
        acc[...] += _dot_tn(av.astype(BF16), b_ref[...].astype(BF16))

        @pl.when(pl.program_id(1) == nk - 1)
        def _():
            o_ref[...] = acc[...].reshape(pd, rows, n).astype(o_ref.dtype)

    in_specs = [pl.BlockSpec((tk, bm), lambda i, kk: (kk, i)), pl.BlockSpec((tk, n), lambda i, kk: (kk, 0))]
    args = [a, b]
    if buf is not None:
        in_specs.append(_ANY)
        args.append(buf)
    return pl.pallas_call(
        body, name=name, grid=(m // bm, nk), out_shape=jax.ShapeDtypeStruct((N_DEV, total_rows, n), BF16),
        in_specs=in_specs, out_specs=pl.BlockSpec((pd, rows, n), lambda i, kk: (i, row_off // rows, 0)),
        scratch_shapes=[pltpu.VMEM((bm, n), F32)], input_output_aliases={2: 0} if buf is not None else {},
        compiler_params=_params(("parallel", "arbitrary")),
    )(*args)


def adamw(w, g, m, v, name, g_transposed=False):
    _, r, c = w.shape
    tr = r if (g_transposed or r <= 256) else 256
    c1 = 1.0 - ADAM_B1 ** ADAM_STEP
    c2 = 1.0 - ADAM_B2 ** ADAM_STEP

    def body(w_ref, g_ref, m_ref, v_ref, g_o, d_o, m_o, v_o):
        gv = g_ref[...].T if g_transposed else g_ref[...]
        mn = ADAM_B1 * m_ref[0] + (1.0 - ADAM_B1) * gv
        vn = ADAM_B2 * v_ref[0] + (1.0 - ADAM_B2) * (gv * gv)
        g_o[0] = gv
        m_o[0] = mn
        v_o[0] = vn
        d_o[0] = -ADAM_LR * ((mn / c1) / (jnp.sqrt(vn / c2) + ADAM_EPS) + ADAM_WD * w_ref[0])

    spec = pl.BlockSpec((1, tr, c), lambda i: (0, i, 0))
    gspec = pl.BlockSpec((c, r), lambda i: (0, 0)) if g_transposed else pl.BlockSpec((tr, c), lambda i: (i, 0))
    shp = jax.ShapeDtypeStruct((1, r, c), F32)
    return pl.pallas_call(
        body, name=name, grid=(r // tr,), out_shape=(shp,) * 4, in_specs=[spec, gspec, spec, spec], out_specs=(spec,) * 4,
        compiler_params=_params(("parallel",)),
    )(w, g, m, v)


def _rope_tables(s_len):
    def angles(pos, dim):
        inv = np.float32(ROPE_THETA) ** (-np.arange(0, dim, 2, dtype=np.float32) / np.float32(dim))
        return pos.astype(np.float32)[:, None] * inv[None, :]

    tpos = np.arange(s_len)
    a1 = angles(tpos, QK_ROPE)
    ar = angles(tpos // GRID_W, HD_B // 2)
    ac = angles(tpos % GRID_W, HD_B // 2)
    z16 = np.zeros((s_len, 16), np.float32)
    z32 = np.zeros((s_len, 32), np.float32)
    z64 = np.zeros((s_len, 64), np.float32)
    one64 = np.ones((s_len, 64), np.float32)
    c1, s1 = np.cos(a1), np.sin(a1)
    ca = np.concatenate([one64, c1, c1, z32], axis=1)
    ck = np.concatenate([z64, c1, c1, z32], axis=1)
    s1a = np.concatenate([z64, -s1, z16, z32], axis=1)
    s2a = np.concatenate([z64, z16, s1, z32], axis=1)
    cr, sr, cc, sc = np.cos(ar), np.sin(ar), np.cos(ac), np.sin(ac)
    cb = np.concatenate([cr, cr, cc, cc, z64], axis=1)
    s1b = np.concatenate([-sr, z16, -sc, z16, z64], axis=1)
    s2b = np.concatenate([z16, sr, z16, sc, z64], axis=1)
    return jnp.asarray(np.stack([ca, s1a, s2a, ck, cb, s1b, s2b]).astype(np.float32))


def _pad_heads(a, n_heads, axis):
    shp = a.shape
    a = a.reshape(shp[:axis] + (n_heads, shp[axis] // n_heads) + shp[axis + 1:])
    pad = [(0, 0)] * a.ndim
    pad[axis + 1] = (0, HP - a.shape[axis + 1])
    a = jnp.pad(a, pad)
    return a.reshape(shp[:axis] + (n_heads * HP,) + shp[axis + 1:])


def _unpad_heads(a, n_heads, width, axis):
    shp = a.shape
    a = a.reshape(shp[:axis] + (n_heads, HP) + shp[axis + 1:])
    a = lax.slice_in_dim(a, 0, width, axis=axis + 1)
    return a.reshape(shp[:axis] + (n_heads * width,) + shp[axis + 1:])


def _pack_rows(blocks, names):
    parts = []
    for name in names:
        b = blocks[name]
        padr = PACK_ROWS[name] - b.shape[-2]
        if padr:
            b = jnp.pad(b, [(0, 0)] * (b.ndim - 2) + [(0, padr), (0, 0)])
        parts.append(b)
    return jnp.concatenate(parts, axis=parts[0].ndim - 2)


def _expand_w_in(wt):
    z64 = jnp.zeros((64, D_MODEL), wt.dtype)
    z32 = jnp.zeros((32, D_MODEL), wt.dtype)
    return jnp.concatenate([
        wt[1184:2208], wt[2208:3232], _pad_heads(wt[416:928], H_B, 0), wt[0:256],
        _pad_heads(wt[928:1056], KV_B, 0), _pad_heads(wt[1056:1184], KV_B, 0), wt[256:384],
        z64, wt[384:416], z32], axis=0)


def _collapse_w_in(dw):
    dg, dq, ds = dw[0:2048], dw[2048:3072], dw[3072:4096]
    return jnp.concatenate([
        ds[0:256], ds[768:896], ds[960:992], _unpad_heads(dq, H_B, HD_B, 0), _unpad_heads(ds[256:512], KV_B, HD_B, 0),
        _unpad_heads(ds[512:768], KV_B, HD_B, 0), dg], axis=0)


def kernel(x, p, g_mix, w_in, g_qa, w_qb, g_kva, w_kvb, g_qn, g_kn, w_oa, w_ob, w_o, g_mlp, w_up, w_down, g_ple, w_ple_gate, w_ple, g_final, loss_target, m_g_mix, m_w_in, m_g_qa, m_w_qb, m_g_kva, m_w_kvb, m_g_qn, m_g_kn, m_w_oa, m_w_ob, m_w_o, m_g_mlp, m_w_up, m_w_down, m_g_ple, m_w_ple_gate, m_w_ple, m_g_final, v_g_mix, v_w_in, v_g_qa, v_w_qb, v_g_kva, v_w_kvb, v_g_qn, v_g_kn, v_w_oa, v_w_ob, v_w_o, v_g_mlp, v_w_up, v_w_down, v_g_ple, v_w_ple_gate, v_w_ple, v_g_final):
    n_b, s_len, _ = x.shape
    t = n_b * s_len
    tm = min(512, s_len)
    tq_f = min(1024, s_len)
    tq_b = min(1024, s_len)

    mats = dict(w_in=(w_in, m_w_in, v_w_in), w_qb=(w_qb, m_w_qb, v_w_qb), w_kvb=(w_kvb, m_w_kvb, v_w_kvb),
                w_oa=(w_oa, m_w_oa, v_w_oa), w_ob=(w_ob, m_w_ob, v_w_ob), w_o=(w_o, m_w_o, v_w_o),
                w_up=(w_up, m_w_up, v_w_up), w_down=(w_down, m_w_down, v_w_down),
                w_ple_gate=(w_ple_gate, m_w_ple_gate, v_w_ple_gate), w_ple=(w_ple, m_w_ple, v_w_ple))
    col_sharded = ("w_in", "w_qb", "w_kvb", "w_oa", "w_ob", "w_up", "w_ple")

    blocks = {}
    for name in PACK_W1 + ("w_oa", "w_ob", "w_ple"):
        blocks[name] = mats[name][0][0].T.reshape(-1, D_MODEL).astype(BF16)
    off_w1, _ = _pack_offsets(PACK_W1)
    off_w2, _ = _pack_offsets(PACK_W2)
    xf = x.reshape(t, D_MODEL)
    h, full1 = norm_x(xf, g_mix, tm, comm=gather_first_comm(_pack_rows(blocks, PACK_W1)))
    pack2, full1 = pack_late_weights(w_up, w_down, w_o, w_ple_gate, _pack_rows(blocks, ("w_oa", "w_ob", "w_ple")),
                                     comm=gather_pass_comm(full1))

    def gathered(full, offs, name, rows, width):
        return full[:, offs[name]:offs[name] + rows].reshape(-1, width)

    w_in_t = _expand_w_in(gathered(full1, off_w1, "w_in", 404, D_MODEL))
    w_qb_t = _pad_heads(gathered(full1, off_w1, "w_qb", 24, Q_LORA), H_A, 0)
    wkvb = gathered(full1, off_w1, "w_kvb", 16, KV_LORA).reshape(H_A, 2, 64, KV_LORA)
    w_kvb_t = jnp.concatenate([_pad_heads(wkvb[:, 0].reshape(-1, KV_LORA), H_A, 0),
                               _pad_heads(wkvb[:, 1].reshape(-1, KV_LORA), H_A, 0)], axis=0)

    tabs = _rope_tables(s_len)
    g_qn_p = jnp.pad(g_qn, ((0, 0), (0, HP - HD_B)))
    g_kn_p = jnp.pad(g_kn, ((0, 0), (0, HP - HD_B)))
    pf = p.reshape(t, PLE_DIM)
    tgt = loss_target.reshape(t, D_MODEL)

    zp = in_proj(h, w_in_t, tm)
    qa, ka, va, qb, kb, vb, cq, ckv = attn_prep(zp, tabs, g_qa, g_kva, g_qn_p, g_kn_p, w_qb_t, w_kvb_t, tm, s_len)
    oa, lse_a, full2 = attn_fwd(qa, ka, va, n_b, s_len, tq_f, "attn_a_fwd", comm=gather_first_comm(pack2))
    ob, lse_b, full2 = attn_fwd(qb, kb, vb, n_b, s_len, tq_f, "attn_b_fwd", comm=gather_pass_comm(full2))
    w_oa_t = _pad_heads(gathered(full2, off_w2, "w_oa", 64, H_A * V_DIM_A), H_A, 1)
    w_ob_t = _pad_heads(gathered(full2, off_w2, "w_ob", 64, H_B * HD_B), H_B, 1)
    w_ple_t = gathered(full2, off_w2, "w_ple", 32, PLE_DIM)
    x1, merged, yab = merge_fwd(oa, ob, zp, xf, w_oa_t, w_ob_t, full2, off_w2, tm)
    x2, u = mlp_fwd(x1, g_mlp, full2, off_w2, tm)
    dx2, dt, h3, dpe, st_ple, dx2b = ple_loss_bwd(x2, pf, tgt, g_ple, g_final.reshape(1, D_MODEL), full2, off_w2, w_ple_t, tm)
    dx1, da, h2, st_mlp, dx1b = mlp_bwd(dx2, x1, u, g_mlp, full2, off_w2, tm)

    core = lax.axis_index("c").astype(jnp.int32).reshape(1)
    chip = (2 * lax.axis_index("x") + lax.axis_index("y")).astype(jnp.int32).reshape(1)

    def packed(gblocks, names):
        return _pack_rows({n: gblocks[n].reshape(N_DEV, -1, D_MODEL).astype(BF16) for n in names}, names)

    off_g1, rows_g1 = _pack_offsets(PACK_G1)
    gpack1 = matmul_tn_packed(da, h2, "gw_up", 512, off_g1["w_up"], rows_g1)
    gpack1 = matmul_tn_packed(u, dx2b, "gw_down", 512, off_g1["w_down"], rows_g1, buf=gpack1, square_a=True)
    gpack1 = matmul_tn_packed(h3, dt, "gw_pg", 128, off_g1["w_ple_gate"], rows_g1, buf=gpack1)
    gple = matmul_tn(dpe, pf, "gw_ple").reshape(N_DEV, -1, D_MODEL).astype(BF16)
    gpack1 = lax.dynamic_update_slice(gpack1, gple, (0, off_g1["w_ple"], 0))
    doa, dob, dz, dya, dyb, got1 = merge_bwd(dx1b, yab, zp, w_oa_t, w_ob_t, full2, off_w2, tm,
                                               comm=scatter_sibling_comm(gpack1))
    part1 = add_pairs(gpack1, got1, core)
    dqa, dka, dva, land1 = attn_bwd(qa, ka, va, doa, oa, lse_a, n_b, s_len, tq_b, "attn_a_bwd", comm=scatter_chips_comm(part1))
    gshard1 = sum_chips(part1, land1, chip)

    off_g2, rows_g2 = _pack_offsets(PACK_G2)
    g2 = dict(w_oa=_unpad_heads(matmul_tn(dya, oa, "gw_oa"), H_A, V_DIM_A, 1),
              w_ob=_unpad_heads(matmul_tn(dyb, ob, "gw_ob"), H_B, HD_B, 1))
    gpack2 = matmul_tn_packed(merged, dx1b, "gw_o", 128, off_g2["w_o"], rows_g2)
    gpack2 = lax.dynamic_update_slice(gpack2, packed(g2, ("w_oa", "w_ob")), (0, off_g2["w_oa"], 0))
    part2 = add_pairs(gpack2, exchange_sibling(gpack2), core)
    dqb, dkb, dvb, land2 = attn_bwd(qb, kb, vb, dob, ob, lse_b, n_b, s_len, tq_b, "attn_b_bwd", comm=scatter_chips_comm(part2))
    gshard2 = sum_chips(part2, land2, chip)
    dz, dqap, dkva, st_prep = prep_bwd(dqa, dka, dva, dqb, dkb, dvb, zp, dz, tabs, g_qa, g_kva, g_qn_p, g_kn_p,
                                       w_qb_t, w_kvb_t, tm, s_len)

    gkv = matmul_tn(dkva, ckv, "gw_kvb")
    g3 = dict(
        w_in=_collapse_w_in(matmul_tn(dz, h, "gw_in")),
        w_qb=_unpad_heads(matmul_tn(dqap, cq, "gw_qb"), H_A, QK_NOPE + QK_ROPE, 0),
        w_kvb=jnp.stack([_unpad_heads(gkv[:H_A * HP], H_A, 64, 0).reshape(H_A, 64, KV_LORA),
                         _unpad_heads(gkv[H_A * HP:], H_A, 64, 0).reshape(H_A, 64, KV_LORA)], axis=1))
    gpack3 = packed(g3, PACK_G3)
    part3 = add_pairs(gpack3, exchange_sibling(gpack3), core)
    grad_x, st_mix, land3 = in_bwd(dz, xf, dx1, g_mix, w_in_t, tm, comm=scatter_chips_comm(part3))
    gshard3 = sum_chips(part3, land3, chip)
    off_g3, _ = _pack_offsets(PACK_G3)
    shards = {n: (gshard1, off_g1[n]) for n in PACK_G1}
    shards.update({n: (gshard2, off_g2[n]) for n in PACK_G2})
    shards.update({n: (gshard3, off_g3[n]) for n in PACK_G3})

    stats = allreduce_stats(st_mix, st_prep, st_mlp, st_ple)
    loss = jnp.sum(stats[ST_LOSS])

    out_g, out_d, out_m, out_v = {}, {}, {}, {}
    for name, (w, m, v) in mats.items():
        gshard, off = shards[name]
        r, c = w.shape[1:]
        if name in col_sharded:
            g2 = gshard[off:off + (r * c) // D_MODEL].reshape(c, r)
            in_kernel = r % 128 == 0 and c % 128 == 0
            res = adamw(w, g2 if in_kernel else g2.T, m, v, "adamw_" + name, g_transposed=in_kernel)
        else:
            res = adamw(w, gshard[off:off + r], m, v, "adamw_" + name)
        out_g[name], out_d[name], out_m[name], out_v[name] = res

    gains = (("g_mix", g_mix, m_g_mix, v_g_mix, ST_G_MIX), ("g_qa", g_qa, m_g_qa, v_g_qa, ST_G_QA),
             ("g_kva", g_kva, m_g_kva, v_g_kva, ST_G_KVA), ("g_qn", g_qn, m_g_qn, v_g_qn, ST_G_QN),
             ("g_kn", g_kn, m_g_kn, v_g_kn, ST_G_KN), ("g_mlp", g_mlp, m_g_mlp, v_g_mlp, ST_G_MLP),
             ("g_ple", g_ple, m_g_ple, v_g_ple, ST_G_PLE), ("g_final", g_final, m_g_final, v_g_final, ST_G_FINAL))
    res = adamw_gains(stats, [(r_, w.reshape(1, -1), m.reshape(1, -1), v.reshape(1, -1)) for _, w, m, v, r_ in gains])
    for (name, w, _, _, _), (gg, gd, gm, gv) in zip(gains, res):
        out_g[name], out_d[name], out_m[name], out_v[name] = (a.reshape(w.shape) for a in (gg, gd, gm, gv))

    order = ("g_mix", "w_in", "g_qa", "w_qb", "g_kva", "w_kvb", "g_qn", "g_kn", "w_oa", "w_ob", "w_o", "g_mlp",
             "w_up", "w_down", "g_ple", "w_ple_gate", "w_ple", "g_final")
    return (loss, grad_x.reshape(x.shape), *[out_g[n] for n in order], *[out_d[n] for n in order],
            *[out_m[n] for n in order], *[out_v[n] for n in order])
```

```python
import numpy as np
import jax
import jax.numpy as jnp
from jax import lax
from jax.experimental import pallas as pl
from jax.experimental.pallas import tpu as pltpu

F32 = jnp.float32
BF16 = jnp.bfloat16

D_MODEL = 1024
EPS = 1e-6
ROPE_THETA = 10000.0
GRID_W = 64
H_A = 8
QK_NOPE = 64
QK_ROPE = 32
V_DIM_A = 64
Q_LORA = 256
KV_LORA = 128
H_B = 8
KV_B = 2
HD_B = 64
D_FF = 4 * D_MODEL
PLE_DIM = 256
HP = 128
ZP = 4096
N_DEV = 8
N_CHIP = 4

ADAM_LR = 0.001
ADAM_B1 = 0.9
ADAM_B2 = 0.999
ADAM_EPS = 1e-08
ADAM_WD = 0.01
ADAM_STEP = 10

VMEM_LIMIT = 52 * 1024 * 1024

PACK_ROWS = dict(w_in=416, w_qb=32, w_kvb=16, w_oa=64, w_ob=64, w_o=128, w_up=512, w_down=512, w_ple_gate=128, w_ple=32)
PACK_W1 = ("w_in", "w_qb", "w_kvb")
PACK_W2 = ("w_up", "w_down", "w_o", "w_ple_gate", "w_oa", "w_ob", "w_ple")
PACK_G1 = ("w_up", "w_down", "w_ple_gate", "w_ple")
PACK_G2 = ("w_o", "w_oa", "w_ob")
PACK_G3 = ("w_in", "w_qb", "w_kvb")


def _pack_offsets(names):
    off, o = {}, 0
    for n in names:
        off[n] = o
        o += PACK_ROWS[n]
    return off, o

ST_G_MIX, ST_G_QA, ST_G_KVA, ST_G_QN, ST_G_KN, ST_G_MLP, ST_G_PLE, ST_G_FINAL, ST_LOSS = range(9)
ST_ROWS = 16


def _dot_nn(a, b):
    return lax.dot_general(a, b, (((1,), (0,)), ((), ())), preferred_element_type=F32)


def _dot_nt(a, b):
    return lax.dot_general(a, b, (((1,), (1,)), ((), ())), preferred_element_type=F32)


def _dot_tn(a, b):
    return lax.dot_general(a, b, (((0,), (0,)), ((), ())), preferred_element_type=F32)


def _rstd(x, n):
    return lax.rsqrt(jnp.sum(x * x, axis=-1, keepdims=True) * (1.0 / n) + EPS)


def _rms_bwd(dy, xh, r, g, n):
    dxh = dy * g
    return r * (dxh - xh * (jnp.sum(dxh * xh, axis=-1, keepdims=True) * (1.0 / n)))


def _rope_fwd(x, c, s1, s2):
    return x * c + pltpu.roll(x, HP - 16, 1) * s1 + pltpu.roll(x, 16, 1) * s2


def _rope_bwd(d, c, s1, s2):
    return d * c + pltpu.roll(d * s1, 16, 1) + pltpu.roll(d * s2, HP - 16, 1)


def _colsum(v):
    return jnp.sum(v, axis=0, keepdims=True)


def _params(sem=None, vmem=VMEM_LIMIT):
    return pltpu.CompilerParams(dimension_semantics=sem, vmem_limit_bytes=vmem)


def _resident(shape):
    nd = len(shape)
    return pl.BlockSpec(shape, lambda *_: (0,) * nd, pipeline_mode=pl.Buffered(1))


def _rows(tm, width, col=0):
    return pl.BlockSpec((tm, width), lambda i: (i, col))


def _packed_weight(rows, off):
    return pl.BlockSpec((N_DEV, rows, D_MODEL), lambda *_: (0, off // rows, 0), pipeline_mode=pl.Buffered(1))


def _wrows(ref, start, size):
    rows = ref.shape[1]
    return ref[start // rows:(start + size) // rows].reshape(size, D_MODEL)


def _mesh_pos():
    return lax.axis_index("x"), lax.axis_index("y"), lax.axis_index("c")


def _flip(v, bit):
    return (1 - v) if bit else v


_ANY = pl.BlockSpec(memory_space=pl.ANY)
_MESH = pl.DeviceIdType.MESH


def _remote(src, dst, send_sems, recv_sems, k, to):
    return pltpu.make_async_remote_copy(src_ref=src, dst_ref=dst, send_sem=send_sems.at[k], recv_sem=recv_sems.at[k],
                                        device_id=to, device_id_type=_MESH)


def _sibling_copies(g_ref, got_ref, send_sems, recv_sems):
    x, y, c = _mesh_pos()
    return [_remote(g_ref.at[2 * j + (1 - c)], got_ref.at[j], send_sems, recv_sems, j, (x, y, 1 - c)) for j in range(N_CHIP)]


def _chip_copies(p_ref, land_ref, send_sems, recv_sems):
    x, y, c = _mesh_pos()
    copies = []
    for k in (1, 2, 3):
        tx, ty = _flip(x, k & 2), _flip(y, k & 1)
        copies.append(_remote(p_ref.at[2 * tx + ty], land_ref.at[k - 1], send_sems, recv_sems, k - 1, (tx, ty, c)))
    return copies


class _Comm:
    def __init__(self, ins, out_shapes, sems, make, aliases=None):
        self.ins, self.out_shapes, self.sems, self.make, self.aliases = list(ins), list(out_shapes), list(sems), make, aliases or {}


def _comm_parts(comm, n_in, n_out):
    if comm is None:
        return [], [], [], [], {}
    alias = {n_in + j: n_out + k for j, k in comm.aliases.items()}
    return comm.ins, [_ANY] * len(comm.ins), comm.out_shapes, comm.sems, alias


def _split_refs(refs, n_in, n_out, n_scratch, comm):
    n_ci = len(comm.ins) if comm else 0
    n_co = len(comm.out_shapes) if comm else 0
    cuts, i = [], 0
    for n in (n_in, n_ci, n_out, n_co, n_scratch):
        cuts.append(refs[i:i + n])
        i += n
    return (*cuts, refs[i:])


def _grid_edge(grid, last):
    cond = None
    for d, n in enumerate(grid):
        here = pl.program_id(d) == (n - 1 if last else 0)
        cond = here if cond is None else cond & here
    return cond


def _comm_start(comm, cin, cout, csem, grid):
    if comm is not None:
        @pl.when(_grid_edge(grid, False))
        def _():
            for cp in comm.make(cin, cout, csem):
                cp.start()


def _comm_finish(comm, cin, cout, csem, grid):
    if comm is not None:
        @pl.when(_grid_edge(grid, True))
        def _():
            for cp in comm.make(cin, cout, csem):
                cp.wait()


def gather_first_comm(shard):
    r, w = shard.shape

    def make(cin, cout, sems):
        (x_ref,), (out_ref,), (send_sems, recv_sems, local_sem) = cin, cout, sems
        x, y, c = _mesh_pos()
        mine = out_ref.at[4 * x + 2 * y + c]
        targets = [(x, y, 1 - c), (1 - x, y, c), (x, 1 - y, c), (1 - x, 1 - y, c)]
        return [_remote(x_ref, mine, send_sems, recv_sems, k, to) for k, to in enumerate(targets)] + [
            pltpu.make_async_copy(x_ref, mine, local_sem)]

    return _Comm([shard], [jax.ShapeDtypeStruct((N_DEV, r, w), shard.dtype)],
                 [pltpu.SemaphoreType.DMA((4,)), pltpu.SemaphoreType.DMA((4,)), pltpu.SemaphoreType.DMA], make)


def gather_pass_comm(full):
    def make(cin, cout, sems):
        (in_ref,), (out_ref,), (send_sems, recv_sems) = cin, cout, sems
        x, y, c = _mesh_pos()
        copies = []
        for k, (px, py) in enumerate([(1 - x, y), (x, 1 - y), (1 - x, 1 - y)]):
            idx = 4 * px + 2 * py + c
            copies.append(_remote(in_ref.at[idx], out_ref.at[idx], send_sems, recv_sems, k, (x, y, 1 - c)))
        return copies

    return _Comm([full], [jax.ShapeDtypeStruct(full.shape, full.dtype)],
                 [pltpu.SemaphoreType.DMA((3,)), pltpu.SemaphoreType.DMA((3,))], make, aliases={0: 0})


def scatter_sibling_comm(g):
    _, r, w = g.shape
    return _Comm([g], [jax.ShapeDtypeStruct((N_CHIP, r, w), g.dtype)],
                 [pltpu.SemaphoreType.DMA((N_CHIP,)), pltpu.SemaphoreType.DMA((N_CHIP,))],
                 lambda cin, cout, sems: _sibling_copies(cin[0], cout[0], sems[0], sems[1]))


def scatter_chips_comm(part):
    _, r, w = part.shape
    return _Comm([part], [jax.ShapeDtypeStruct((N_CHIP - 1, r, w), part.dtype)],
                 [pltpu.SemaphoreType.DMA((3,)), pltpu.SemaphoreType.DMA((3,))],
                 lambda cin, cout, sems: _chip_copies(cin[0], cout[0], sems[0], sems[1]))


def exchange_sibling(g):
    _, r, w = g.shape

    def body(g_ref, got_ref, send_sems, recv_sems):
        copies = _sibling_copies(g_ref, got_ref, send_sems, recv_sems)
        for cp in copies:
            cp.start()
        for cp in copies:
            cp.wait()

    return pl.pallas_call(
        body, name="exchange_sibling", out_shape=jax.ShapeDtypeStruct((N_CHIP, r, w), g.dtype),
        in_specs=[_ANY], out_specs=_ANY,
        scratch_shapes=[pltpu.SemaphoreType.DMA((N_CHIP,)), pltpu.SemaphoreType.DMA((N_CHIP,))],
    )(g)


def exchange_chips(part):
    _, r, w = part.shape

    def body(p_ref, land_ref, send_sems, recv_sems):
        copies = _chip_copies(p_ref, land_ref, send_sems, recv_sems)
        for cp in copies:
            cp.start()
        for cp in copies:
            cp.wait()

    return pl.pallas_call(
        body, name="exchange_chips", out_shape=jax.ShapeDtypeStruct((N_CHIP - 1, r, w), part.dtype),
        in_specs=[_ANY], out_specs=_ANY,
        scratch_shapes=[pltpu.SemaphoreType.DMA((3,)), pltpu.SemaphoreType.DMA((3,))],
    )(part)


def allreduce_stats(st_mix, st_prep, st_mlp, st_ple):
    def body(mix_ref, prep_ref, mlp_ref, ple_ref, out_ref, mine, gath, send_sems, recv_sems):
        x, y, c = _mesh_pos()
        me = 4 * x + 2 * y + c
        mine[...] = jnp.zeros_like(mine)
        mine[ST_G_MIX:ST_G_MIX + 1, :] = mix_ref[...]
        mine[ST_G_QA:ST_G_KN + 1, 0:256] = prep_ref[...]
        mine[ST_G_MLP:ST_G_MLP + 1, :] = mlp_ref[...]
        mine[ST_G_PLE:ST_LOSS + 1, :] = ple_ref[...]
        gath[me] = mine[...]
        copies = []
        for k in range(1, N_DEV):
            peer = (_flip(x, k & 4), _flip(y, k & 2), _flip(c, k & 1))
            copies.append(_remote(mine, gath.at[me], send_sems, recv_sems, k - 1, peer))
        for cp in copies:
            cp.start()
        for cp in copies:
            cp.wait()
        acc = gath[0]
        for d in range(1, N_DEV):
            acc = acc + gath[d]
        out_ref[...] = acc

    vm = pl.BlockSpec(memory_space=pltpu.VMEM)
    return pl.pallas_call(
        body, name="allreduce_stats", out_shape=jax.ShapeDtypeStruct((ST_ROWS, D_MODEL), F32),
        in_specs=[vm] * 4, out_specs=vm,
        scratch_shapes=[pltpu.VMEM((ST_ROWS, D_MODEL), F32), pltpu.VMEM((N_DEV, ST_ROWS, D_MODEL), F32),
                        pltpu.SemaphoreType.DMA((N_DEV - 1,)), pltpu.SemaphoreType.DMA((N_DEV - 1,))],
    )(st_mix, st_prep, st_mlp, st_ple)


def adamw_gains(stats, gains):
    c1 = 1.0 - ADAM_B1 ** ADAM_STEP
    c2 = 1.0 - ADAM_B2 ** ADAM_STEP
    n = len(gains)

    def body(st_ref, *refs):
        ins, outs = refs[:3 * n], refs[3 * n:]
        for i, (row, w, _, _) in enumerate(gains):
            width = w.shape[1]
            gv = st_ref[row:row + 1, 0:width]
            mn = ADAM_B1 * ins[3 * i + 1][...] + (1.0 - ADAM_B1) * gv
            vn = ADAM_B2 * ins[3 * i + 2][...] + (1.0 - ADAM_B2) * (gv * gv)
            outs[4 * i][...] = gv
            outs[4 * i + 1][...] = -ADAM_LR * ((mn / c1) / (jnp.sqrt(vn / c2) + ADAM_EPS) + ADAM_WD * ins[3 * i][...])
            outs[4 * i + 2][...] = mn
            outs[4 * i + 3][...] = vn

    vm = pl.BlockSpec(memory_space=pltpu.VMEM)
    flat = [a for (_, w, m, v) in gains for a in (w, m, v)]
    out_shape = tuple(jax.ShapeDtypeStruct(w.shape, F32) for (_, w, _, _) in gains for _ in range(4))
    res = pl.pallas_call(body, name="adamw_gains", out_shape=out_shape, in_specs=[vm] * (1 + 3 * n),
                         out_specs=tuple([vm] * (4 * n)))(stats, *flat)
    return [res[4 * i:4 * i + 4] for i in range(n)]


def _row_tile(r, cap=640):
    return max(d for d in range(16, min(r, cap) + 1, 16) if r % d == 0)


def add_pairs(g, got, core):
    n, r, w = got.shape
    tr = _row_tile(r)

    def body(c_ref, a_ref, b_ref, o_ref):
        o_ref[...] = (a_ref[...].astype(F32) + b_ref[...].astype(F32)).astype(o_ref.dtype)

    spec = pl.BlockSpec((1, tr, w), lambda i, j, c: (i, j, 0))
    return pl.pallas_call(
        body, name="add_pairs", out_shape=jax.ShapeDtypeStruct(got.shape, got.dtype),
        grid_spec=pltpu.PrefetchScalarGridSpec(
            num_scalar_prefetch=1, grid=(n, r // tr),
            in_specs=[pl.BlockSpec((1, tr, w), lambda i, j, c: (2 * i + c[0], j, 0)), spec], out_specs=spec),
        compiler_params=_params(("parallel", "parallel")),
    )(core, g, got)


def sum_chips(part, land, chip):
    _, r, w = part.shape
    tr = _row_tile(r)

    def body(c_ref, p_ref, l_ref, o_ref):
        acc = p_ref[0].astype(F32)
        for s in range(N_CHIP - 1):
            acc = acc + l_ref[s].astype(F32)
        o_ref[...] = acc

    return pl.pallas_call(
        body, name="sum_chips", out_shape=jax.ShapeDtypeStruct((r, w), F32),
        grid_spec=pltpu.PrefetchScalarGridSpec(
            num_scalar_prefetch=1, grid=(r // tr,),
            in_specs=[pl.BlockSpec((1, tr, w), lambda i, c: (c[0], i, 0)), pl.BlockSpec((N_CHIP - 1, tr, w), lambda i, c: (0, i, 0))],
            out_specs=pl.BlockSpec((tr, w), lambda i, c: (i, 0))),
        compiler_params=_params(("parallel",)),
    )(chip, part, land)


def norm_x(x, g_mix, tm, comm=None):
    t = x.shape[0]
    grid = (t // tm,)
    c_ins, c_in_specs, c_outs, c_sems, alias = _comm_parts(comm, 2, 1)

    def body(*refs):
        (x_ref, g_ref), cin, (h_ref,), cout, _, csem = _split_refs(refs, 2, 1, 0, comm)
        _comm_start(comm, cin, cout, csem, grid)
        xv = x_ref[...]
        h_ref[...] = (xv * _rstd(xv, D_MODEL) * g_ref[...]).astype(BF16)
        _comm_finish(comm, cin, cout, csem, grid)

    return pl.pallas_call(
        body, name="norm_x", grid=grid, out_shape=(jax.ShapeDtypeStruct((t, D_MODEL), BF16), *c_outs),
        in_specs=[_rows(tm, D_MODEL), _resident((1, D_MODEL)), *c_in_specs],
        out_specs=(_rows(tm, D_MODEL), *([_ANY] * len(c_outs))),
        scratch_shapes=c_sems, input_output_aliases=alias, compiler_params=_params(("arbitrary",)),
    )(x, g_mix, *c_ins)


def pack_late_weights(w_up, w_down, w_o, w_pg, small, comm=None):
    rows = sum(PACK_ROWS[n] for n in PACK_W2)
    c_ins, c_in_specs, c_outs, c_sems, alias = _comm_parts(comm, 5, 1)
    grid = (1,)

    def body(*refs):
        (up_ref, dn_ref, o_ref, pg_ref, sm_ref), cin, (out_ref,), cout, _, csem = _split_refs(refs, 5, 1, 0, comm)
        _comm_start(comm, cin, cout, csem, grid)
        out_ref[0:512, :] = up_ref[0].T.astype(BF16)
        out_ref[512:1024, :] = dn_ref[0].astype(BF16)
        out_ref[1024:1152, :] = o_ref[0].astype(BF16)
        out_ref[1152:1280, :] = pg_ref[0].astype(BF16)
        out_ref[1280:rows, :] = sm_ref[...]
        _comm_finish(comm, cin, cout, csem, grid)

    def whole(a):
        nd = a.ndim
        return pl.BlockSpec(a.shape, lambda i: (0,) * nd)

    args = (w_up, w_down, w_o, w_pg, small)
    return pl.pallas_call(
        body, name="pack_late_weights", grid=grid, out_shape=(jax.ShapeDtypeStruct((rows, D_MODEL), BF16), *c_outs),
        in_specs=[*[whole(a) for a in args], *c_in_specs],
        out_specs=(pl.BlockSpec((rows, D_MODEL), lambda i: (0, 0)), *([_ANY] * len(c_outs))),
        scratch_shapes=c_sems, input_output_aliases=alias, compiler_params=_params(("arbitrary",)),
    )(*args, *c_ins)


def in_proj(h, w_in_t, tm):
    t = h.shape[0]
    nc = 512

    def body(h_ref, w_ref, z_ref):
        hv = h_ref[...]
        for cidx in range(ZP // nc):
            z_ref[:, cidx * nc:(cidx + 1) * nc] = _dot_nt(hv, w_ref[cidx * nc:(cidx + 1) * nc, :])

    return pl.pallas_call(
        body, name="in_proj", grid=(t // tm,), out_shape=jax.ShapeDtypeStruct((t, ZP), F32),
        in_specs=[_rows(tm, D_MODEL), _resident((ZP, D_MODEL))],
        out_specs=_rows(tm, ZP), compiler_params=_params(("parallel",)),
    )(h, w_in_t)


def attn_prep(zp, tabs, g_qa, g_kva, g_qn, g_kn, w_qb_t, w_kvb_t, tm, s_len):
    t = zp.shape[0]
    nsb = s_len // tm
    scale_a = (QK_NOPE + QK_ROPE) ** -0.5
    scale_b = HD_B ** -0.5

    def body(qb_ref, qlat_ref, kb_ref, vb_ref, ckv_ref, kpe_ref, tab_ref, gqa_ref, gkva_ref, gqn_ref, gkn_ref,
             wqb_ref, wkvb_ref, qa_o, ka_o, va_o, qb_o, kb_o, vb_o, cq_o, ckvn_o):
        ca, s1a, s2a = tab_ref[0], tab_ref[1], tab_ref[2]
        ck = tab_ref[3]
        cb, s1b, s2b = tab_ref[4], tab_ref[5], tab_ref[6]
        ql = qlat_ref[...]
        cq = (ql * _rstd(ql, Q_LORA) * gqa_ref[...]).astype(BF16)
        cq_o[...] = cq
        qa = _dot_nt(cq, wqb_ref[...])
        slabs = [slice(h * HP, (h + 1) * HP) for h in range(H_A)]
        qa_o[...] = jnp.concatenate(
            [(_rope_fwd(qa[:, sl], ca, s1a, s2a) * scale_a).astype(BF16) for sl in slabs], axis=1)
        cr = ckv_ref[...]
        ckv = (cr * _rstd(cr, KV_LORA) * gkva_ref[...]).astype(BF16)
        ckvn_o[...] = ckv
        kva = _dot_nt(ckv, wkvb_ref[...])
        kpe = _rope_fwd(kpe_ref[...], ck, s1a, s2a)
        ka_o[...] = jnp.concatenate([(kva[:, sl] + kpe).astype(BF16) for sl in slabs], axis=1)
        va_o[...] = kva[:, H_A * HP:].astype(BF16)
        gqn, gkn = gqn_ref[...], gkn_ref[...]

        def norm_rope(ref, sl, g, scale):
            xs = ref[:, sl]
            y = _rope_fwd(xs * _rstd(xs, HD_B) * g, cb, s1b, s2b)
            return (y if scale is None else y * scale).astype(BF16)

        qb_o[...] = jnp.concatenate([norm_rope(qb_ref, sl, gqn, scale_b) for sl in slabs], axis=1)
        kb_o[...] = jnp.concatenate([norm_rope(kb_ref, sl, gkn, None) for sl in slabs[:KV_B]], axis=1)
        vb_o[...] = vb_ref[...].astype(BF16)

    def o(width):
        return jax.ShapeDtypeStruct((t, width), BF16)

    return pl.pallas_call(
        body, name="attn_prep", grid=(t // tm,),
        out_shape=(o(H_A * HP), o(H_A * HP), o(H_A * HP), o(H_B * HP), o(KV_B * HP), o(KV_B * HP), o(Q_LORA), o(KV_LORA)),
        in_specs=[_rows(tm, 1024, 2), _rows(tm, 256, 12), _rows(tm, 256, 13), _rows(tm, 256, 14),
                  _rows(tm, 128, 30), _rows(tm, 128, 31),
                  pl.BlockSpec((7, tm, HP), lambda i: (0, i % nsb, 0)),
                  _resident((1, Q_LORA)), _resident((1, KV_LORA)), _resident((1, HP)), _resident((1, HP)),
                  _resident((H_A * HP, Q_LORA)), _resident((2 * H_A * HP, KV_LORA))],
        out_specs=(_rows(tm, H_A * HP), _rows(tm, H_A * HP), _rows(tm, H_A * HP), _rows(tm, H_B * HP),
                   _rows(tm, KV_B * HP), _rows(tm, KV_B * HP), _rows(tm, Q_LORA), _rows(tm, KV_LORA)),
        compiler_params=_params(("parallel",)),
    )(zp, zp, zp, zp, zp, zp, tabs, g_qa, g_kva, g_qn, g_kn, w_qb_t, w_kvb_t)


def attn_fwd(q, k, v, n_b, s_len, tq, name, comm=None):
    t = q.shape[0]
    n_h, n_hk = q.shape[1] // HP, k.shape[1] // HP
    grp = n_h // n_hk
    nq = s_len // tq
    sub = min(tq, 256)
    grid = (n_b, n_h, nq)
    c_ins, c_in_specs, c_outs, c_sems, alias = _comm_parts(comm, 3, 2)

    def body(*refs):
        (q_ref, k_ref, v_ref), cin, (o_ref, lse_ref), cout, _, csem = _split_refs(refs, 3, 2, 0, comm)
        _comm_start(comm, cin, cout, csem, grid)
        kv, vv = k_ref[...], v_ref[...]
        for r in range(tq // sub):
            rows = slice(r * sub, (r + 1) * sub)
            s = _dot_nt(q_ref[rows, :], kv)
            m = jnp.max(s, axis=-1, keepdims=True)
            p = jnp.exp(s - m)
            l = jnp.sum(p, axis=-1, keepdims=True)
            o_ref[rows, :] = (_dot_nn(p.astype(BF16), vv) * (1.0 / l)).astype(o_ref.dtype)
            lse_ref[rows, :] = jnp.broadcast_to(m + jnp.log(l), (sub, HP))
        _comm_finish(comm, cin, cout, csem, grid)

    qspec = pl.BlockSpec((tq, HP), lambda b, h, i: (b * nq + i, h))
    kspec = pl.BlockSpec((s_len, HP), lambda b, h, i: (b, h // grp))
    return pl.pallas_call(
        body, name=name, grid=grid,
        out_shape=(jax.ShapeDtypeStruct((t, n_h * HP), BF16), jax.ShapeDtypeStruct((t, n_h * HP), F32), *c_outs),
        in_specs=[qspec, kspec, kspec, *c_in_specs], out_specs=(qspec, qspec, *([_ANY] * len(c_outs))),
        scratch_shapes=c_sems, input_output_aliases=alias,
        compiler_params=_params(("arbitrary", "arbitrary", "arbitrary")),
    )(q, k, v, *c_ins)


def merge_fwd(oa, ob, zp, x, w_oa_t, w_ob_t, wpack, off, tm):
    t = x.shape[0]

    def body(oa_ref, ob_ref, ga_ref, gb_ref, x_ref, woa_ref, wob_ref, wo_ref, x1_o, mg_o, y_o):
        ya = _dot_nt(oa_ref[...], woa_ref[...])
        yb = _dot_nt(ob_ref[...], wob_ref[...])
        y_o[:, 0:D_MODEL] = ya.astype(BF16)
        y_o[:, D_MODEL:2 * D_MODEL] = yb.astype(BF16)
        merged = (jax.nn.sigmoid(ga_ref[...]) * ya + jax.nn.sigmoid(gb_ref[...]) * yb).astype(BF16)
        mg_o[...] = merged
        x1_o[...] = x_ref[...] + _dot_nn(merged, _wrows(wo_ref, 0, D_MODEL))

    return pl.pallas_call(
        body, name="merge_fwd", grid=(t // tm,),
        out_shape=(jax.ShapeDtypeStruct((t, D_MODEL), F32), jax.ShapeDtypeStruct((t, D_MODEL), BF16),
                   jax.ShapeDtypeStruct((t, 2 * D_MODEL), BF16)),
        in_specs=[_rows(tm, H_A * HP), _rows(tm, H_B * HP), _rows(tm, 1024, 0), _rows(tm, 1024, 1), _rows(tm, D_MODEL),
                  _resident((D_MODEL, H_A * HP)), _resident((D_MODEL, H_B * HP)), _packed_weight(128, off["w_o"])],
        out_specs=(_rows(tm, D_MODEL), _rows(tm, D_MODEL), _rows(tm, 2 * D_MODEL)), compiler_params=_params(("parallel",)),
    )(oa, ob, zp, zp, x, w_oa_t, w_ob_t, wpack)


def mlp_fwd(x1, g_mlp, wpack, off, tm):
    t = x1.shape[0]
    fc = 1024

    def body(x_ref, g_ref, wup_ref, wdn_ref, x2_o, u_o):
        xv = x_ref[...]
        h2 = (xv * _rstd(xv, D_MODEL) * g_ref[...]).astype(BF16)
        acc = xv
        for cidx in range(D_FF // fc):
            sl = slice(cidx * fc, (cidx + 1) * fc)
            u = jnp.maximum(_dot_nt(h2, _wrows(wup_ref, cidx * fc, fc)), 0.0)
            u_o[:, sl] = u.astype(BF16)
            acc = acc + _dot_nn((u * u).astype(BF16), _wrows(wdn_ref, cidx * fc, fc))
        x2_o[...] = acc

    return pl.pallas_call(
        body, name="mlp_fwd", grid=(t // tm,),
        out_shape=(jax.ShapeDtypeStruct((t, D_MODEL), F32), jax.ShapeDtypeStruct((t, D_FF), BF16)),
        in_specs=[_rows(tm, D_MODEL), _resident((1, D_MODEL)), _packed_weight(512, off["w_up"]), _packed_weight(512, off["w_down"])],
        out_specs=(_rows(tm, D_MODEL), _rows(tm, D_FF)), compiler_params=_params(("parallel",)),
    )(x1, g_mlp, wpack, wpack)


def ple_loss_bwd(x2, p, tgt, g_ple, g_final, wpack, off, w_ple_t, tm):
    t = x2.shape[0]
    inv_d = 1.0 / D_MODEL

    def body(x2_ref, p_ref, tg_ref, gp_ref, gf_ref, wpg_ref, wple_ref, dx2_o, dt_o, h3_o, dpe_o, st_o, dx2b_o):
        @pl.when(pl.program_id(0) == 0)
        def _():
            st_o[...] = jnp.zeros_like(st_o)

        x2v = x2_ref[...]
        gp, gf = gp_ref[...], gf_ref[...]
        w_pg = _wrows(wpg_ref, 0, D_MODEL)
        r2 = _rstd(x2v, D_MODEL)
        xh2 = x2v * r2
        h3 = (xh2 * gp).astype(BF16)
        h3_o[...] = h3
        gate = jax.nn.sigmoid(_dot_nn(h3, w_pg))
        pe = _dot_nt(p_ref[...].astype(BF16), wple_ref[...])
        x3 = x2v + gate * pe
        r3 = _rstd(x3, D_MODEL)
        xh3 = x3 * r3
        err = xh3 * gf - tg_ref[...]
        dy = err * inv_d
        dx3 = _rms_bwd(dy, xh3, r3, gf, D_MODEL)
        dpe_o[...] = (dx3 * gate).astype(BF16)
        dt = (dx3 * pe * gate * (1.0 - gate)).astype(BF16)
        dt_o[...] = dt
        dh3 = _dot_nt(dt, w_pg)
        dx2 = dx3 + _rms_bwd(dh3, xh2, r2, gp, D_MODEL)
        dx2_o[...] = dx2
        dx2b_o[...] = dx2.astype(BF16)
        st_o[0:1, :] += _colsum(dh3 * xh2)
        st_o[1:2, :] += _colsum(dy * xh3)
        st_o[2:3, :] += _colsum(err * err) * (0.5 * inv_d)

    bf = jax.ShapeDtypeStruct((t, D_MODEL), BF16)
    return pl.pallas_call(
        body, name="ple_loss_bwd", grid=(t // tm,),
        out_shape=(jax.ShapeDtypeStruct((t, D_MODEL), F32), bf, bf, bf, jax.ShapeDtypeStruct((3, D_MODEL), F32), bf),
        in_specs=[_rows(tm, D_MODEL), _rows(tm, PLE_DIM), _rows(tm, D_MODEL), _resident((1, D_MODEL)), _resident((1, D_MODEL)),
                  _packed_weight(128, off["w_ple_gate"]), _resident((D_MODEL, PLE_DIM))],
        out_specs=(_rows(tm, D_MODEL), _rows(tm, D_MODEL), _rows(tm, D_MODEL), _rows(tm, D_MODEL),
                   pl.BlockSpec((3, D_MODEL), lambda i: (0, 0)), _rows(tm, D_MODEL)),
        compiler_params=_params(("arbitrary",)),
    )(x2, p, tgt, g_ple, g_final, wpack, w_ple_t)


def mlp_bwd(dx2, x1, u, g_mlp, wpack, off, tm):
    t = x1.shape[0]
    fc = 1024

    def body(dx2_ref, x1_ref, u_ref, g_ref, wup_ref, wdn_ref, dx1_o, da_o, h2_o, st_o, dx1b_o):
        @pl.when(pl.program_id(0) == 0)
        def _():
            st_o[...] = jnp.zeros_like(st_o)

        d2 = dx2_ref[...]
        d2b = d2.astype(BF16)
        dh2 = jnp.zeros((tm, D_MODEL), F32)
        for cidx in range(D_FF // fc):
            sl = slice(cidx * fc, (cidx + 1) * fc)
            da = (_dot_nt(d2b, _wrows(wdn_ref, cidx * fc, fc)) * (2.0 * u_ref[:, sl].astype(F32))).astype(BF16)
            da_o[:, sl] = da
            dh2 = dh2 + _dot_nn(da, _wrows(wup_ref, cidx * fc, fc))
        xv = x1_ref[...]
        g = g_ref[...]
        r1 = _rstd(xv, D_MODEL)
        xh1 = xv * r1
        h2_o[...] = (xh1 * g).astype(BF16)
        st_o[...] += _colsum(dh2 * xh1)
        dx1 = d2 + _rms_bwd(dh2, xh1, r1, g, D_MODEL)
        dx1_o[...] = dx1
        dx1b_o[...] = dx1.astype(BF16)

    return pl.pallas_call(
        body, name="mlp_bwd", grid=(t // tm,),
        out_shape=(jax.ShapeDtypeStruct((t, D_MODEL), F32), jax.ShapeDtypeStruct((t, D_FF), BF16),
                   jax.ShapeDtypeStruct((t, D_MODEL), BF16), jax.ShapeDtypeStruct((1, D_MODEL), F32),
                   jax.ShapeDtypeStruct((t, D_MODEL), BF16)),
        in_specs=[_rows(tm, D_MODEL), _rows(tm, D_MODEL), _rows(tm, D_FF), _resident((1, D_MODEL)),
                  _packed_weight(512, off["w_up"]), _packed_weight(512, off["w_down"])],
        out_specs=(_rows(tm, D_MODEL), _rows(tm, D_FF), _rows(tm, D_MODEL), pl.BlockSpec((1, D_MODEL), lambda i: (0, 0)),
                   _rows(tm, D_MODEL)),
        compiler_params=_params(("arbitrary",)),
    )(dx2, x1, u, g_mlp, wpack, wpack)


def merge_bwd(dx1b, yab, zp, w_oa_t, w_ob_t, wpack, off, tm, comm=None):
    t = dx1b.shape[0]
    grid = (t // tm,)
    c_ins, c_in_specs, c_outs, c_sems, alias = _comm_parts(comm, 7, 5)

    def body(*refs):
        ((dx1_ref, y_ref, ga_ref, gb_ref, woa_ref, wob_ref, wo_ref), cin,
         (doa_o, dob_o, dg_o, dya_o, dyb_o), cout, _, csem) = _split_refs(refs, 7, 5, 0, comm)
        _comm_start(comm, cin, cout, csem, grid)
        dm = _dot_nt(dx1_ref[...], _wrows(wo_ref, 0, D_MODEL))
        for g_ref, w_ref, do_o, dy_o, col in ((ga_ref, woa_ref, doa_o, dya_o, 0), (gb_ref, wob_ref, dob_o, dyb_o, 1)):
            cols = slice(col * D_MODEL, (col + 1) * D_MODEL)
            sg = jax.nn.sigmoid(g_ref[...])
            dyv = (dm * sg).astype(BF16)
            dy_o[...] = dyv
            dg_o[:, cols] = (dm * y_ref[:, cols].astype(F32) * sg * (1.0 - sg)).astype(BF16)
            do_o[...] = _dot_nn(dyv, w_ref[...]).astype(BF16)
        _comm_finish(comm, cin, cout, csem, grid)

    bf = jax.ShapeDtypeStruct((t, D_MODEL), BF16)
    return pl.pallas_call(
        body, name="merge_bwd", grid=grid,
        out_shape=(bf, bf, jax.ShapeDtypeStruct((t, ZP), BF16), bf, bf, *c_outs),
        in_specs=[_rows(tm, D_MODEL), _rows(tm, 2 * D_MODEL), _rows(tm, 1024, 0), _rows(tm, 1024, 1),
                  _resident((D_MODEL, H_A * HP)), _resident((D_MODEL, H_B * HP)), _packed_weight(128, off["w_o"]), *c_in_specs],
        out_specs=(_rows(tm, D_MODEL), _rows(tm, D_MODEL), _rows(tm, 2 * D_MODEL), _rows(tm, D_MODEL), _rows(tm, D_MODEL),
                   *([_ANY] * len(c_outs))),
        scratch_shapes=c_sems, input_output_aliases=alias,
        compiler_params=_params(("arbitrary",)),
    )(dx1b, yab, zp, zp, w_oa_t, w_ob_t, wpack, *c_ins)


def attn_bwd(q, k, v, do, o, lse, n_b, s_len, tq, name, comm=None):
    t = q.shape[0]
    n_h, n_hk = q.shape[1] // HP, k.shape[1] // HP
    grp = n_h // n_hk
    nq = s_len // tq
    sub = min(tq, 256)
    grid = (n_b, n_hk, grp, nq)
    c_ins, c_in_specs, c_outs, c_sems, alias = _comm_parts(comm, 6, 3)

    def body(*refs):
        ((q_ref, k_ref, v_ref, do_ref, o_ref, lse_ref), cin, (dq_o, dk_o, dv_o), cout, (p_s, ds_s, dk_acc, dv_acc),
         csem) = _split_refs(refs, 6, 3, 4, comm)
        _comm_start(comm, cin, cout, csem, grid)

        @pl.when((pl.program_id(2) == 0) & (pl.program_id(3) == 0))
        def _():
            dk_acc[...] = jnp.zeros_like(dk_acc)
            dv_acc[...] = jnp.zeros_like(dv_acc)

        kv, vv = k_ref[...], v_ref[...]
        for r in range(tq // sub):
            rows = slice(r * sub, (r + 1) * sub)
            qv, dov = q_ref[rows, :], do_ref[rows, :]
            delta = jnp.sum(dov.astype(F32) * o_ref[rows, :].astype(F32), axis=-1, keepdims=True)
            p = jnp.exp(_dot_nt(qv, kv) - lse_ref[rows, 0:1])
            ds = (p * (_dot_nt(dov, vv) - delta)).astype(BF16)
            p_s[rows, :] = p.astype(BF16)
            ds_s[rows, :] = ds
            dq_o[rows, :] = _dot_nn(ds, kv).astype(dq_o.dtype)
        dk_acc[...] += _dot_tn(ds_s[...], q_ref[...])
        dv_acc[...] += _dot_tn(p_s[...], do_ref[...])

        @pl.when((pl.program_id(2) == grp - 1) & (pl.program_id(3) == nq - 1))
        def _():
            dk_o[...] = dk_acc[...].astype(dk_o.dtype)
            dv_o[...] = dv_acc[...].astype(dv_o.dtype)

        _comm_finish(comm, cin, cout, csem, grid)

    qspec = pl.BlockSpec((tq, HP), lambda b, hk, g, i: (b * nq + i, hk * grp + g))
    kspec = pl.BlockSpec((s_len, HP), lambda b, hk, g, i: (b, hk))
    return pl.pallas_call(
        body, name=name, grid=grid,
        out_shape=(jax.ShapeDtypeStruct((t, n_h * HP), BF16), jax.ShapeDtypeStruct((t, n_hk * HP), BF16),
                   jax.ShapeDtypeStruct((t, n_hk * HP), BF16), *c_outs),
        in_specs=[qspec, kspec, kspec, qspec, qspec, qspec, *c_in_specs],
        out_specs=(qspec, kspec, kspec, *([_ANY] * len(c_outs))),
        scratch_shapes=[pltpu.VMEM((tq, s_len), BF16), pltpu.VMEM((tq, s_len), BF16),
                        pltpu.VMEM((s_len, HP), F32), pltpu.VMEM((s_len, HP), F32), *c_sems],
        input_output_aliases=alias,
        compiler_params=_params(("arbitrary", "arbitrary", "arbitrary", "arbitrary")),
    )(q, k, v, do, o, lse, *c_ins)


def prep_bwd(dqa, dka, dva, dqb, dkb, dvb, zp, dz, tabs, g_qa, g_kva, g_qn, g_kn, w_qb_t, w_kvb_t, tm, s_len):
    t = zp.shape[0]
    nsb = s_len // tm
    scale_a = (QK_NOPE + QK_ROPE) ** -0.5
    scale_b = HD_B ** -0.5

    def body(dqa_ref, dka_ref, dva_ref, dqb_ref, dkb_ref, dvb_ref, qb_ref, qlat_ref, kb_ref, ckv_ref, tab_ref,
             gqa_ref, gkva_ref, gqn_ref, gkn_ref, wqb_ref, wkvb_ref, _, dz_o, dqap_o, dkva_o, st_o):
        dzq_o, dsm_o = dz_o.at[:, 0:1024], dz_o.at[:, 1024:2048]

        @pl.when(pl.program_id(0) == 0)
        def _():
            st_o[...] = jnp.zeros_like(st_o)

        ca, s1a, s2a = tab_ref[0], tab_ref[1], tab_ref[2]
        ck = tab_ref[3]
        cb, s1b, s2b = tab_ref[4], tab_ref[5], tab_ref[6]
        for h in range(H_A):
            sl = slice(h * HP, (h + 1) * HP)
            dqap_o[:, sl] = _rope_bwd(dqa_ref[:, sl].astype(F32) * scale_a, ca, s1a, s2a).astype(BF16)
        dcq = _dot_nn(dqap_o[...], wqb_ref[...])
        ql = qlat_ref[...]
        rq = _rstd(ql, Q_LORA)
        xh = ql * rq
        gqa = gqa_ref[...]
        st_o[0:1, :] += _colsum(dcq * xh)
        dsm_o[:, 0:256] = _rms_bwd(dcq, xh, rq, gqa, Q_LORA).astype(BF16)
        dkpe = jnp.zeros((tm, HP), F32)
        for h in range(H_A):
            sl = slice(h * HP, (h + 1) * HP)
            dk = dka_ref[:, sl]
            dkpe = dkpe + dk.astype(F32)
            dkva_o[:, sl] = dk.astype(BF16)
        dkva_o[:, H_A * HP:] = dva_ref[...].astype(BF16)
        dsm_o[:, 896:1024] = _rope_bwd(dkpe, ck, s1a, s2a).astype(BF16)
        dckv = _dot_nn(dkva_o[...], wkvb_ref[...])
        cr = ckv_ref[...]
        rk = _rstd(cr, KV_LORA)
        xh = cr * rk
        st_o[1:2, 0:128] += _colsum(dckv * xh)
        dsm_o[:, 768:896] = _rms_bwd(dckv, xh, rk, gkva_ref[...], KV_LORA).astype(BF16)
        gqn, gkn = gqn_ref[...], gkn_ref[...]
        dgq = jnp.zeros((1, HP), F32)
        for h in range(H_B):
            sl = slice(h * HP, (h + 1) * HP)
            dy = _rope_bwd(dqb_ref[:, sl].astype(F32) * scale_b, cb, s1b, s2b)
            xs = qb_ref[:, sl]
            r = _rstd(xs, HD_B)
            xh = xs * r
            dgq = dgq + _colsum(dy * xh)
            dzq_o[:, sl] = _rms_bwd(dy, xh, r, gqn, HD_B).astype(BF16)
        st_o[2:3, 0:128] += dgq
        dgk = jnp.zeros((1, HP), F32)
        for h in range(KV_B):
            sl = slice(h * HP, (h + 1) * HP)
            dy = _rope_bwd(dkb_ref[:, sl].astype(F32), cb, s1b, s2b)
            xs = kb_ref[:, sl]
            r = _rstd(xs, HD_B)
            xh = xs * r
            dgk = dgk + _colsum(dy * xh)
            dsm_o[:, 256 + h * HP:256 + (h + 1) * HP] = _rms_bwd(dy, xh, r, gkn, HD_B).astype(BF16)
        st_o[3:4, 0:128] += dgk
        dsm_o[:, 512:768] = dvb_ref[...].astype(BF16)

    return pl.pallas_call(
        body, name="prep_bwd", grid=(t // tm,),
        out_shape=(jax.ShapeDtypeStruct((t, ZP), BF16), jax.ShapeDtypeStruct((t, 1024), BF16),
                   jax.ShapeDtypeStruct((t, 2048), BF16), jax.ShapeDtypeStruct((4, 256), F32)),
        in_specs=[_rows(tm, 1024), _rows(tm, 1024), _rows(tm, 1024), _rows(tm, 1024), _rows(tm, 256), _rows(tm, 256),
                  _rows(tm, 1024, 2), _rows(tm, 256, 12), _rows(tm, 256, 13), _rows(tm, 128, 30),
                  pl.BlockSpec((7, tm, HP), lambda i: (0, i % nsb, 0)),
                  _resident((1, Q_LORA)), _resident((1, KV_LORA)), _resident((1, HP)), _resident((1, HP)),
                  _resident((H_A * HP, Q_LORA)), _resident((2 * H_A * HP, KV_LORA)), _ANY],
        out_specs=(_rows(tm, 2048, 1), _rows(tm, 1024), _rows(tm, 2048), pl.BlockSpec((4, 256), lambda i: (0, 0))),
        input_output_aliases={17: 0}, compiler_params=_params(("arbitrary",)),
    )(dqa, dka, dva, dqb, dkb, dvb, zp, zp, zp, zp, tabs, g_qa, g_kva, g_qn, g_kn, w_qb_t, w_kvb_t, dz)


def in_bwd(dz, x, dx1, g_mix, w_in_t, tm, comm=None):
    t = x.shape[0]
    grid = (t // tm,)
    c_ins, c_in_specs, c_outs, c_sems, alias = _comm_parts(comm, 5, 2)

    def body(*refs):
        (dz_ref, x_ref, dx1_ref, g_ref, w_ref), cin, (dx_o, st_o), cout, _, csem = _split_refs(refs, 5, 2, 0, comm)
        _comm_start(comm, cin, cout, csem, grid)

        @pl.when(pl.program_id(0) == 0)
        def _():
            st_o[...] = jnp.zeros_like(st_o)

        dh = _dot_nn(dz_ref[...], w_ref[...])
        xv = x_ref[...]
        g = g_ref[...]
        r = _rstd(xv, D_MODEL)
        xh = xv * r
        st_o[...] += _colsum(dh * xh)
        dx_o[...] = dx1_ref[...] + _rms_bwd(dh, xh, r, g, D_MODEL)
        _comm_finish(comm, cin, cout, csem, grid)

    return pl.pallas_call(
        body, name="in_bwd", grid=grid,
        out_shape=(jax.ShapeDtypeStruct((t, D_MODEL), F32), jax.ShapeDtypeStruct((1, D_MODEL), F32), *c_outs),
        in_specs=[_rows(tm, ZP), _rows(tm, D_MODEL), _rows(tm, D_MODEL),
                  _resident((1, D_MODEL)), _resident((ZP, D_MODEL)), *c_in_specs],
        out_specs=(_rows(tm, D_MODEL), pl.BlockSpec((1, D_MODEL), lambda i: (0, 0)), *([_ANY] * len(c_outs))),
        scratch_shapes=c_sems, input_output_aliases=alias,
        compiler_params=_params(("arbitrary",)),
    )(dz, x, dx1, g_mix, w_in_t, *c_ins)


def matmul_tn(a, b, name, square_a=False):
    t, m = a.shape
    n = b.shape[1]
    bm = min(m, 512)
    tk = min(t, 4096)

    def body(a_ref, b_ref, o_ref):
        @pl.when(pl.program_id(1) == 0)
        def _():
            o_ref[...] = jnp.zeros_like(o_ref)

        av = a_ref[...]
        if square_a:
            av = (av.astype(F32) * av.astype(F32))
        o_ref[...] += _dot_tn(av.astype(BF16), b_ref[...].astype(BF16))

    return pl.pallas_call(
        body, name=name, grid=(m // bm, t // tk), out_shape=jax.ShapeDtypeStruct((m, n), F32),
        in_specs=[pl.BlockSpec((tk, bm), lambda i, kk: (kk, i)), pl.BlockSpec((tk, n), lambda i, kk: (kk, 0))],
        out_specs=pl.BlockSpec((bm, n), lambda i, kk: (i, 0)),
        compiler_params=_params(("parallel", "arbitrary")),
    )(a, b)


def matmul_tn_packed(a, b, name, rows, row_off, total_rows, buf=None, square_a=False):
    t, m = a.shape
    n = b.shape[1]
    pd = max(1, 512 // rows)
    bm = pd * rows
    tk = min(t, 4096)
    nk = t // tk

    def body(a_ref, b_ref, *rest):
        o_ref, acc = rest[-2], rest[-1]

        @pl.when(pl.program_id(1) == 0)
        def _():
            acc[...] = jnp.zeros_like(acc)

        av = a_ref[...]
        if square_a:
            av = (av.astype(F32) * av.astype(F32))
        acc[...] += _dot_tn(av.astype(BF16), b_ref[...].astype(BF16))

        @pl.when(pl.program_id(1) == nk - 1)
        def _():
            o_ref[...] = acc[...].reshape(pd, rows, n).astype(o_ref.dtype)

    in_specs = [pl.BlockSpec((tk, bm), lambda i, kk: (kk, i)), pl.BlockSpec((tk, n), lambda i, kk: (kk, 0))]
    args = [a, b]
    if buf is not None:
        in_specs.append(_ANY)
        args.append(buf)
    return pl.pallas_call(
        body, name=name, grid=(m // bm, nk), out_shape=jax.ShapeDtypeStruct((N_DEV, total_rows, n), BF16),
        in_specs=in_specs, out_specs=pl.BlockSpec((pd, rows, n), lambda i, kk: (i, row_off // rows, 0)),
        scratch_shapes=[pltpu.VMEM((bm, n), F32)], input_output_aliases={2: 0} if buf is not None else {},
        compiler_params=_params(("parallel", "arbitrary")),
    )(*args)


def adamw(w, g, m, v, name, g_transposed=False):
    _, r, c = w.shape
    tr = r if (g_transposed or r <= 256) else 256
    c1 = 1.0 - ADAM_B1 ** ADAM_STEP
    c2 = 1.0 - ADAM_B2 ** ADAM_STEP

    def body(w_ref, g_ref, m_ref, v_ref, g_o, d_o, m_o, v_o):
        gv = g_ref[...].T if g_transposed else g_ref[...]
        mn = ADAM_B1 * m_ref[0] + (1.0 - ADAM_B1) * gv
        vn = ADAM_B2 * v_ref[0] + (1.0 - ADAM_B2) * (gv * gv)
        g_o[0] = gv
        m_o[0] = mn
        v_o[0] = vn
        d_o[0] = -ADAM_LR * ((mn / c1) / (jnp.sqrt(vn / c2) + ADAM_EPS) + ADAM_WD * w_ref[0])

    spec = pl.BlockSpec((1, tr, c), lambda i: (0, i, 0))
    gspec = pl.BlockSpec((c, r), lambda i: (0, 0)) if g_transposed else pl.BlockSpec((tr, c), lambda i: (i, 0))
    shp = jax.ShapeDtypeStruct((1, r, c), F32)
    return pl.pallas_call(
        body, name=name, grid=(r // tr,), out_shape=(shp,) * 4, in_specs=[spec, gspec, spec, spec], out_specs=(spec,) * 4,
        compiler_params=_params(("parallel",)),
    )(w, g, m, v)


def _rope_tables(s_len):
    def angles(pos, dim):
        inv = np.float32(ROPE_THETA) ** (-np.arange(0, dim, 2, dtype=np.float32) / np.float32(dim))
        return pos.astype(np.float32)[:, None] * inv[None, :]

    tpos = np.arange(s_len)
    a1 = angles(tpos, QK_ROPE)
    ar = angles(tpos // GRID_W, HD_B // 2)
    ac = angles(tpos % GRID_W, HD_B // 2)
    z16 = np.zeros((s_len, 16), np.float32)
    z32 = np.zeros((s_len, 32), np.float32)
    z64 = np.zeros((s_len, 64), np.float32)
    one64 = np.ones((s_len, 64), np.float32)
    c1, s1 = np.cos(a1), np.sin(a1)
    ca = np.concatenate([one64, c1, c1, z32], axis=1)
    ck = np.concatenate([z64, c1, c1, z32], axis=1)
    s1a = np.concatenate([z64, -s1, z16, z32], axis=1)
    s2a = np.concatenate([z64, z16, s1, z32], axis=1)
    cr, sr, cc, sc = np.cos(ar), np.sin(ar), np.cos(ac), np.sin(ac)
    cb = np.concatenate([cr, cr, cc, cc, z64], axis=1)
    s1b = np.concatenate([-sr, z16, -sc, z16, z64], axis=1)
    s2b = np.concatenate([z16, sr, z16, sc, z64], axis=1)
    return jnp.asarray(np.stack([ca, s1a, s2a, ck, cb, s1b, s2b]).astype(np.float32))


def _pad_heads(a, n_heads, axis):
    shp = a.shape
    a = a.reshape(shp[:axis] + (n_heads, shp[axis] // n_heads) + shp[axis + 1:])
    pad = [(0, 0)] * a.ndim
    pad[axis + 1] = (0, HP - a.shape[axis + 1])
    a = jnp.pad(a, pad)
    return a.reshape(shp[:axis] + (n_heads * HP,) + shp[axis + 1:])


def _unpad_heads(a, n_heads, width, axis):
    shp = a.shape
    a = a.reshape(shp[:axis] + (n_heads, HP) + shp[axis + 1:])
    a = lax.slice_in_dim(a, 0, width, axis=axis + 1)
    return a.reshape(shp[:axis] + (n_heads * width,) + shp[axis + 1:])


def _pack_rows(blocks, names):
    parts = []
    for name in names:
        b = blocks[name]
        padr = PACK_ROWS[name] - b.shape[-2]
        if padr:
            b = jnp.pad(b, [(0, 0)] * (b.ndim - 2) + [(0, padr), (0, 0)])
        parts.append(b)
    return jnp.concatenate(parts, axis=parts[0].ndim - 2)


def _expand_w_in(wt):
    z64 = jnp.zeros((64, D_MODEL), wt.dtype)
    z32 = jnp.zeros((32, D_MODEL), wt.dtype)
    return jnp.concatenate([
        wt[1184:2208], wt[2208:3232], _pad_heads(wt[416:928], H_B, 0), wt[0:256],
        _pad_heads(wt[928:1056], KV_B, 0), _pad_heads(wt[1056:1184], KV_B, 0), wt[256:384],
        z64, wt[384:416], z32], axis=0)


def _collapse_w_in(dw):
    dg, dq, ds = dw[0:2048], dw[2048:3072], dw[3072:4096]
    return jnp.concatenate([
        ds[0:256], ds[768:896], ds[960:992], _unpad_heads(dq, H_B, HD_B, 0), _unpad_heads(ds[256:512], KV_B, HD_B, 0),
        _unpad_heads(ds[512:768], KV_B, HD_B, 0), dg], axis=0)


def kernel(x, p, g_mix, w_in, g_qa, w_qb, g_kva, w_kvb, g_qn, g_kn, w_oa, w_ob, w_o, g_mlp, w_up, w_down, g_ple, w_ple_gate, w_ple, g_final, loss_target, m_g_mix, m_w_in, m_g_qa, m_w_qb, m_g_kva, m_w_kvb, m_g_qn, m_g_kn, m_w_oa, m_w_ob, m_w_o, m_g_mlp, m_w_up, m_w_down, m_g_ple, m_w_ple_gate, m_w_ple, m_g_final, v_g_mix, v_w_in, v_g_qa, v_w_qb, v_g_kva, v_w_kvb, v_g_qn, v_g_kn, v_w_oa, v_w_ob, v_w_o, v_g_mlp, v_w_up, v_w_down, v_g_ple, v_w_ple_gate, v_w_ple, v_g_final):
    n_b, s_len, _ = x.shape
    t = n_b * s_len
    tm = min(512, s_len)
    tq_f = min(2048, s_len)
    tq_b = min(2048, s_len)

    mats = dict(w_in=(w_in, m_w_in, v_w_in), w_qb=(w_qb, m_w_qb, v_w_qb), w_kvb=(w_kvb, m_w_kvb, v_w_kvb),
                w_oa=(w_oa, m_w_oa, v_w_oa), w_ob=(w_ob, m_w_ob, v_w_ob), w_o=(w_o, m_w_o, v_w_o),
                w_up=(w_up, m_w_up, v_w_up), w_down=(w_down, m_w_down, v_w_down),
                w_ple_gate=(w_ple_gate, m_w_ple_gate, v_w_ple_gate), w_ple=(w_ple, m_w_ple, v_w_ple))
    col_sharded = ("w_in", "w_qb", "w_kvb", "w_oa", "w_ob", "w_up", "w_ple")

    blocks = {}
    for name in PACK_W1 + ("w_oa", "w_ob", "w_ple"):
        blocks[name] = mats[name][0][0].T.reshape(-1, D_MODEL).astype(BF16)
    off_w1, _ = _pack_offsets(PACK_W1)
    off_w2, _ = _pack_offsets(PACK_W2)
    xf = x.reshape(t, D_MODEL)
    h, full1 = norm_x(xf, g_mix, tm, comm=gather_first_comm(_pack_rows(blocks, PACK_W1)))
    pack2, full1 = pack_late_weights(w_up, w_down, w_o, w_ple_gate, _pack_rows(blocks, ("w_oa", "w_ob", "w_ple")),
                                     comm=gather_pass_comm(full1))

    def gathered(full, offs, name, rows, width):
        return full[:, offs[name]:offs[name] + rows].reshape(-1, width)

    w_in_t = _expand_w_in(gathered(full1, off_w1, "w_in", 404, D_MODEL))
    w_qb_t = _pad_heads(gathered(full1, off_w1, "w_qb", 24, Q_LORA), H_A, 0)
    wkvb = gathered(full1, off_w1, "w_kvb", 16, KV_LORA).reshape(H_A, 2, 64, KV_LORA)
    w_kvb_t = jnp.concatenate([_pad_heads(wkvb[:, 0].reshape(-1, KV_LORA), H_A, 0),
                               _pad_heads(wkvb[:, 1].reshape(-1, KV_LORA), H_A, 0)], axis=0)

    tabs = _rope_tables(s_len)
    g_qn_p = jnp.pad(g_qn, ((0, 0), (0, HP - HD_B)))
    g_kn_p = jnp.pad(g_kn, ((0, 0), (0, HP - HD_B)))
    pf = p.reshape(t, PLE_DIM)
    tgt = loss_target.reshape(t, D_MODEL)

    zp = in_proj(h, w_in_t, tm)
    qa, ka, va, qb, kb, vb, cq, ckv = attn_prep(zp, tabs, g_qa, g_kva, g_qn_p, g_kn_p, w_qb_t, w_kvb_t, tm, s_len)
    oa, lse_a, full2 = attn_fwd(qa, ka, va, n_b, s_len, tq_f, "attn_a_fwd", comm=gather_first_comm(pack2))
    ob, lse_b, full2 = attn_fwd(qb, kb, vb, n_b, s_len, tq_f, "attn_b_fwd", comm=gather_pass_comm(full2))
    w_oa_t = _pad_heads(gathered(full2, off_w2, "w_oa", 64, H_A * V_DIM_A), H_A, 1)
    w_ob_t = _pad_heads(gathered(full2, off_w2, "w_ob", 64, H_B * HD_B), H_B, 1)
    w_ple_t = gathered(full2, off_w2, "w_ple", 32, PLE_DIM)
    x1, merged, yab = merge_fwd(oa, ob, zp, xf, w_oa_t, w_ob_t, full2, off_w2, tm)
    x2, u = mlp_fwd(x1, g_mlp, full2, off_w2, tm)
    dx2, dt, h3, dpe, st_ple, dx2b = ple_loss_bwd(x2, pf, tgt, g_ple, g_final.reshape(1, D_MODEL), full2, off_w2, w_ple_t, tm)
    dx1, da, h2, st_mlp, dx1b = mlp_bwd(dx2, x1, u, g_mlp, full2, off_w2, tm)

    core = lax.axis_index("c").astype(jnp.int32).reshape(1)
    chip = (2 * lax.axis_index("x") + lax.axis_index("y")).astype(jnp.int32).reshape(1)

    def packed(gblocks, names):
        return _pack_rows({n: gblocks[n].reshape(N_DEV, -1, D_MODEL).astype(BF16) for n in names}, names)

    off_g1, rows_g1 = _pack_offsets(PACK_G1)
    gpack1 = matmul_tn_packed(da, h2, "gw_up", 512, off_g1["w_up"], rows_g1)
    gpack1 = matmul_tn_packed(u, dx2b, "gw_down", 512, off_g1["w_down"], rows_g1, buf=gpack1, square_a=True)
    gpack1 = matmul_tn_packed(h3, dt, "gw_pg", 128, off_g1["w_ple_gate"], rows_g1, buf=gpack1)
    gple = matmul_tn(dpe, pf, "gw_ple").reshape(N_DEV, -1, D_MODEL).astype(BF16)
    gpack1 = lax.dynamic_update_slice(gpack1, gple, (0, off_g1["w_ple"], 0))
    doa, dob, dz, dya, dyb, got1 = merge_bwd(dx1b, yab, zp, w_oa_t, w_ob_t, full2, off_w2, tm,
                                               comm=scatter_sibling_comm(gpack1))
    part1 = add_pairs(gpack1, got1, core)
    dqa, dka, dva, land1 = attn_bwd(qa, ka, va, doa, oa, lse_a, n_b, s_len, tq_b, "attn_a_bwd", comm=scatter_chips_comm(part1))
    gshard1 = sum_chips(part1, land1, chip)

    off_g2, rows_g2 = _pack_offsets(PACK_G2)
    g2 = dict(w_oa=_unpad_heads(matmul_tn(dya, oa, "gw_oa"), H_A, V_DIM_A, 1),
              w_ob=_unpad_heads(matmul_tn(dyb, ob, "gw_ob"), H_B, HD_B, 1))
    gpack2 = matmul_tn_packed(merged, dx1b, "gw_o", 128, off_g2["w_o"], rows_g2)
    gpack2 = lax.dynamic_update_slice(gpack2, packed(g2, ("w_oa", "w_ob")), (0, off_g2["w_oa"], 0))
    part2 = add_pairs(gpack2, exchange_sibling(gpack2), core)
    dqb, dkb, dvb, land2 = attn_bwd(qb, kb, vb, dob, ob, lse_b, n_b, s_len, tq_b, "attn_b_bwd", comm=scatter_chips_comm(part2))
    gshard2 = sum_chips(part2, land2, chip)
    dz, dqap, dkva, st_prep = prep_bwd(dqa, dka, dva, dqb, dkb, dvb, zp, dz, tabs, g_qa, g_kva, g_qn_p, g_kn_p,
                                       w_qb_t, w_kvb_t, tm, s_len)

    gkv = matmul_tn(dkva, ckv, "gw_kvb")
    g3 = dict(
        w_in=_collapse_w_in(matmul_tn(dz, h, "gw_in")),
        w_qb=_unpad_heads(matmul_tn(dqap, cq, "gw_qb"), H_A, QK_NOPE + QK_ROPE, 0),
        w_kvb=jnp.stack([_unpad_heads(gkv[:H_A * HP], H_A, 64, 0).reshape(H_A, 64, KV_LORA),
                         _unpad_heads(gkv[H_A * HP:], H_A, 64, 0).reshape(H_A, 64, KV_LORA)], axis=1))
    gpack3 = packed(g3, PACK_G3)
    part3 = add_pairs(gpack3, exchange_sibling(gpack3), core)
    grad_x, st_mix, land3 = in_bwd(dz, xf, dx1, g_mix, w_in_t, tm, comm=scatter_chips_comm(part3))
    gshard3 = sum_chips(part3, land3, chip)
    off_g3, _ = _pack_offsets(PACK_G3)
    shards = {n: (gshard1, off_g1[n]) for n in PACK_G1}
    shards.update({n: (gshard2, off_g2[n]) for n in PACK_G2})
    shards.update({n: (gshard3, off_g3[n]) for n in PACK_G3})

    stats = allreduce_stats(st_mix, st_prep, st_mlp, st_ple)
    loss = jnp.sum(stats[ST_LOSS])

    out_g, out_d, out_m, out_v = {}, {}, {}, {}
    for name, (w, m, v) in mats.items():
        gshard, off = shards[name]
        r, c = w.shape[1:]
        if name in col_sharded:
            g2 = gshard[off:off + (r * c) // D_MODEL].reshape(c, r)
            in_kernel = r % 128 == 0 and c % 128 == 0
            res = adamw(w, g2 if in_kernel else g2.T, m, v, "adamw_" + name, g_transposed=in_kernel)
        else:
            res = adamw(w, gshard[off:off + r], m, v, "adamw_" + name)
        out_g[name], out_d[name], out_m[name], out_v[name] = res

    gains = (("g_mix", g_mix, m_g_mix, v_g_mix, ST_G_MIX), ("g_qa", g_qa, m_g_qa, v_g_qa, ST_G_QA),
             ("g_kva", g_kva, m_g_kva, v_g_kva, ST_G_KVA), ("g_qn", g_qn, m_g_qn, v_g_qn, ST_G_QN),
             ("g_kn", g_kn, m_g_kn, v_g_kn, ST_G_KN), ("g_mlp", g_mlp, m_g_mlp, v_g_mlp, ST_G_MLP),
             ("g_ple", g_ple, m_g_ple, v_g_ple, ST_G_PLE), ("g_final", g_final, m_g_final, v_g_final, ST_G_FINAL))
    res = adamw_gains(stats, [(r_, w.reshape(1, -1), m.reshape(1, -1), v.reshape(1, -1)) for _, w, m, v, r_ in gains])
    for (name, w, _, _, _), (gg, gd, gm, gv) in zip(gains, res):
        out_g[name], out_d[name], out_m[name], out_v[name] = (a.reshape(w.shape) for a in (gg, gd, gm, gv))

    order = ("g_mix", "w_in", "g_qa", "w_qb", "g_kva", "w_kvb", "g_qn", "g_kn", "w_oa", "w_ob", "w_o", "g_mlp",
             "w_up", "w_down", "g_ple", "w_ple_gate", "w_ple", "g_final")
    return (loss, grad_x.reshape(x.shape), *[out_g[n] for n in order], *[out_d[n] for n in order],
            *[out_m[n] for n in order], *[out_v[n] for n in order])
```

```python
import numpy as np
import jax
import jax.numpy as jnp
from jax import lax
from jax.experimental import pallas as pl
from jax.experimental.pallas import tpu as pltpu

F32 = jnp.float32
BF16 = jnp.bfloat16

D_MODEL = 1024
EPS = 1e-6
ROPE_THETA = 10000.0
GRID_W = 64
H_A = 8
QK_NOPE = 64
QK_ROPE = 32
V_DIM_A = 64
Q_LORA = 256
KV_LORA = 128
H_B = 8
KV_B = 2
HD_B = 64
D_FF = 4 * D_MODEL
PLE_DIM = 256
HP = 128
ZP = 4096
N_DEV = 8
N_CHIP = 4

ADAM_LR = 0.001
ADAM_B1 = 0.9
ADAM_B2 = 0.999
ADAM_EPS = 1e-08
ADAM_WD = 0.01
ADAM_STEP = 10

VMEM_LIMIT = 52 * 1024 * 1024

PACK_ROWS = dict(w_in=416, w_qb=32, w_kvb=16, w_oa=64, w_ob=64, w_o=128, w_up=512, w_down=512, w_ple_gate=128, w_ple=32)
PACK_W1 = ("w_in", "w_qb", "w_kvb")
PACK_W2 = ("w_o", "w_ple_gate", "w_oa", "w_ob", "w_ple")
PACK_W3 = ("w_up", "w_down")
PACK_G1 = ("w_up", "w_down", "w_ple_gate", "w_ple")
PACK_G2 = ("w_o", "w_oa", "w_ob")
PACK_G3 = ("w_in", "w_qb", "w_kvb")


def _pack_offsets(names):
    off, o = {}, 0
    for n in names:
        off[n] = o
        o += PACK_ROWS[n]
    return off, o

ST_G_MIX, ST_G_QA, ST_G_KVA, ST_G_QN, ST_G_KN, ST_G_MLP, ST_G_PLE, ST_G_FINAL, ST_LOSS = range(9)
ST_ROWS = 16


def _dot_nn(a, b):
    return lax.dot_general(a, b, (((1,), (0,)), ((), ())), preferred_element_type=F32)


def _dot_nt(a, b):
    return lax.dot_general(a, b, (((1,), (1,)), ((), ())), preferred_element_type=F32)


def _dot_tn(a, b):
    return lax.dot_general(a, b, (((0,), (0,)), ((), ())), preferred_element_type=F32)


def _rstd(x, n):
    return lax.rsqrt(jnp.sum(x * x, axis=-1, keepdims=True) * (1.0 / n) + EPS)


def _rms_bwd(dy, xh, r, g, n):
    dxh = dy * g
    return r * (dxh - xh * (jnp.sum(dxh * xh, axis=-1, keepdims=True) * (1.0 / n)))


def _rope_fwd(x, c, s1, s2):
    return x * c + pltpu.roll(x, HP - 16, 1) * s1 + pltpu.roll(x, 16, 1) * s2


def _rope_bwd(d, c, s1, s2):
    return d * c + pltpu.roll(d * s1, 16, 1) + pltpu.roll(d * s2, HP - 16, 1)


def _colsum(v):
    return jnp.sum(v, axis=0, keepdims=True)


def _params(sem=None, vmem=VMEM_LIMIT):
    return pltpu.CompilerParams(dimension_semantics=sem, vmem_limit_bytes=vmem)


def _resident(shape):
    nd = len(shape)
    return pl.BlockSpec(shape, lambda *_: (0,) * nd, pipeline_mode=pl.Buffered(1))


def _rows(tm, width, col=0):
    return pl.BlockSpec((tm, width), lambda i: (i, col))


def _packed_weight(rows, off):
    return pl.BlockSpec((N_DEV, rows, D_MODEL), lambda *_: (0, off // rows, 0), pipeline_mode=pl.Buffered(1))


def _wrows(ref, start, size):
    rows = ref.shape[1]
    return ref[start // rows:(start + size) // rows].reshape(size, D_MODEL)


def _mesh_pos():
    return lax.axis_index("x"), lax.axis_index("y"), lax.axis_index("c")


def _flip(v, bit):
    return (1 - v) if bit else v


_ANY = pl.BlockSpec(memory_space=pl.ANY)
_MESH = pl.DeviceIdType.MESH


def _remote(src, dst, send_sems, recv_sems, k, to):
    return pltpu.make_async_remote_copy(src_ref=src, dst_ref=dst, send_sem=send_sems.at[k], recv_sem=recv_sems.at[k],
                                        device_id=to, device_id_type=_MESH)


def _sibling_copies(g_ref, got_ref, send_sems, recv_sems):
    x, y, c = _mesh_pos()
    return [_remote(g_ref.at[2 * j + (1 - c)], got_ref.at[j], send_sems, recv_sems, j, (x, y, 1 - c)) for j in range(N_CHIP)]


def _chip_copies(p_ref, land_ref, send_sems, recv_sems):
    x, y, c = _mesh_pos()
    copies = []
    for k in (1, 2, 3):
        tx, ty = _flip(x, k & 2), _flip(y, k & 1)
        copies.append(_remote(p_ref.at[2 * tx + ty], land_ref.at[k - 1], send_sems, recv_sems, k - 1, (tx, ty, c)))
    return copies


class _Comm:
    def __init__(self, ins, out_shapes, sems, make, aliases=None):
        self.ins, self.out_shapes, self.sems, self.make, self.aliases = list(ins), list(out_shapes), list(sems), make, aliases or {}


def _comm_parts(comm, n_in, n_out):
    if comm is None:
        return [], [], [], [], {}
    alias = {n_in + j: n_out + k for j, k in comm.aliases.items()}
    return comm.ins, [_ANY] * len(comm.ins), comm.out_shapes, comm.sems, alias


def _split_refs(refs, n_in, n_out, n_scratch, comm):
    n_ci = len(comm.ins) if comm else 0
    n_co = len(comm.out_shapes) if comm else 0
    cuts, i = [], 0
    for n in (n_in, n_ci, n_out, n_co, n_scratch):
        cuts.append(refs[i:i + n])
        i += n
    return (*cuts, refs[i:])


def _grid_edge(grid, last):
    cond = None
    for d, n in enumerate(grid):
        here = pl.program_id(d) == (n - 1 if last else 0)
        cond = here if cond is None else cond & here
    return cond


def _comm_start(comm, cin, cout, csem, grid):
    if comm is not None:
        @pl.when(_grid_edge(grid, False))
        def _():
            for cp in comm.make(cin, cout, csem):
                cp.start()


def _comm_finish(comm, cin, cout, csem, grid):
    if comm is not None:
        @pl.when(_grid_edge(grid, True))
        def _():
            for cp in comm.make(cin, cout, csem):
                cp.wait()


def gather_first_comm(shard):
    r, w = shard.shape

    def make(cin, cout, sems):
        (x_ref,), (out_ref,), (send_sems, recv_sems, local_sem) = cin, cout, sems
        x, y, c = _mesh_pos()
        mine = out_ref.at[4 * x + 2 * y + c]
        targets = [(x, y, 1 - c), (1 - x, y, c), (x, 1 - y, c), (1 - x, 1 - y, c)]
        return [_remote(x_ref, mine, send_sems, recv_sems, k, to) for k, to in enumerate(targets)] + [
            pltpu.make_async_copy(x_ref, mine, local_sem)]

    return _Comm([shard], [jax.ShapeDtypeStruct((N_DEV, r, w), shard.dtype)],
                 [pltpu.SemaphoreType.DMA((4,)), pltpu.SemaphoreType.DMA((4,)), pltpu.SemaphoreType.DMA], make)


def gather_pass_comm(full):
    def make(cin, cout, sems):
        (in_ref,), (out_ref,), (send_sems, recv_sems) = cin, cout, sems
        x, y, c = _mesh_pos()
        copies = []
        for k, (px, py) in enumerate([(1 - x, y), (x, 1 - y), (1 - x, 1 - y)]):
            idx = 4 * px + 2 * py + c
            copies.append(_remote(in_ref.at[idx], out_ref.at[idx], send_sems, recv_sems, k, (x, y, 1 - c)))
        return copies

    return _Comm([full], [jax.ShapeDtypeStruct(full.shape, full.dtype)],
                 [pltpu.SemaphoreType.DMA((3,)), pltpu.SemaphoreType.DMA((3,))], make, aliases={0: 0})


def scatter_sibling_comm(g):
    _, r, w = g.shape
    return _Comm([g], [jax.ShapeDtypeStruct((N_CHIP, r, w), g.dtype)],
                 [pltpu.SemaphoreType.DMA((N_CHIP,)), pltpu.SemaphoreType.DMA((N_CHIP,))],
                 lambda cin, cout, sems: _sibling_copies(cin[0], cout[0], sems[0], sems[1]))


def scatter_chips_comm(part):
    _, r, w = part.shape
    return _Comm([part], [jax.ShapeDtypeStruct((N_CHIP - 1, r, w), part.dtype)],
                 [pltpu.SemaphoreType.DMA((3,)), pltpu.SemaphoreType.DMA((3,))],
                 lambda cin, cout, sems: _chip_copies(cin[0], cout[0], sems[0], sems[1]))


def exchange_sibling(g):
    _, r, w = g.shape

    def body(g_ref, got_ref, send_sems, recv_sems):
        copies = _sibling_copies(g_ref, got_ref, send_sems, recv_sems)
        for cp in copies:
            cp.start()
        for cp in copies:
            cp.wait()

    return pl.pallas_call(
        body, name="exchange_sibling", out_shape=jax.ShapeDtypeStruct((N_CHIP, r, w), g.dtype),
        in_specs=[_ANY], out_specs=_ANY,
        scratch_shapes=[pltpu.SemaphoreType.DMA((N_CHIP,)), pltpu.SemaphoreType.DMA((N_CHIP,))],
    )(g)


def exchange_chips(part):
    _, r, w = part.shape

    def body(p_ref, land_ref, send_sems, recv_sems):
        copies = _chip_copies(p_ref, land_ref, send_sems, recv_sems)
        for cp in copies:
            cp.start()
        for cp in copies:
            cp.wait()

    return pl.pallas_call(
        body, name="exchange_chips", out_shape=jax.ShapeDtypeStruct((N_CHIP - 1, r, w), part.dtype),
        in_specs=[_ANY], out_specs=_ANY,
        scratch_shapes=[pltpu.SemaphoreType.DMA((3,)), pltpu.SemaphoreType.DMA((3,))],
    )(part)


def allreduce_stats(st_mix, st_prep, st_mlp, st_ple):
    def body(mix_ref, prep_ref, mlp_ref, ple_ref, out_ref, mine, gath, send_sems, recv_sems):
        x, y, c = _mesh_pos()
        me = 4 * x + 2 * y + c
        mine[...] = jnp.zeros_like(mine)
        mine[ST_G_MIX:ST_G_MIX + 1, :] = mix_ref[...]
        mine[ST_G_QA:ST_G_KN + 1, 0:256] = prep_ref[...]
        mine[ST_G_MLP:ST_G_MLP + 1, :] = mlp_ref[...]
        mine[ST_G_PLE:ST_LOSS + 1, :] = ple_ref[...]
        gath[me] = mine[...]
        copies = []
        for k in range(1, N_DEV):
            peer = (_flip(x, k & 4), _flip(y, k & 2), _flip(c, k & 1))
            copies.append(_remote(mine, gath.at[me], send_sems, recv_sems, k - 1, peer))
        for cp in copies:
            cp.start()
        for cp in copies:
            cp.wait()
        acc = gath[0]
        for d in range(1, N_DEV):
            acc = acc + gath[d]
        out_ref[...] = acc

    vm = pl.BlockSpec(memory_space=pltpu.VMEM)
    return pl.pallas_call(
        body, name="allreduce_stats", out_shape=jax.ShapeDtypeStruct((ST_ROWS, D_MODEL), F32),
        in_specs=[vm] * 4, out_specs=vm,
        scratch_shapes=[pltpu.VMEM((ST_ROWS, D_MODEL), F32), pltpu.VMEM((N_DEV, ST_ROWS, D_MODEL), F32),
                        pltpu.SemaphoreType.DMA((N_DEV - 1,)), pltpu.SemaphoreType.DMA((N_DEV - 1,))],
    )(st_mix, st_prep, st_mlp, st_ple)


def adamw_gains(stats, gains):
    c1 = 1.0 - ADAM_B1 ** ADAM_STEP
    c2 = 1.0 - ADAM_B2 ** ADAM_STEP
    n = len(gains)

    def body(st_ref, *refs):
        ins, outs = refs[:3 * n], refs[3 * n:]
        for i, (row, w, _, _) in enumerate(gains):
            width = w.shape[1]
            gv = st_ref[row:row + 1, 0:width]
            mn = ADAM_B1 * ins[3 * i + 1][...] + (1.0 - ADAM_B1) * gv
            vn = ADAM_B2 * ins[3 * i + 2][...] + (1.0 - ADAM_B2) * (gv * gv)
            outs[4 * i][...] = gv
            outs[4 * i + 1][...] = -ADAM_LR * ((mn / c1) / (jnp.sqrt(vn / c2) + ADAM_EPS) + ADAM_WD * ins[3 * i][...])
            outs[4 * i + 2][...] = mn
            outs[4 * i + 3][...] = vn

    vm = pl.BlockSpec(memory_space=pltpu.VMEM)
    flat = [a for (_, w, m, v) in gains for a in (w, m, v)]
    out_shape = tuple(jax.ShapeDtypeStruct(w.shape, F32) for (_, w, _, _) in gains for _ in range(4))
    res = pl.pallas_call(body, name="adamw_gains", out_shape=out_shape, in_specs=[vm] * (1 + 3 * n),
                         out_specs=tuple([vm] * (4 * n)))(stats, *flat)
    return [res[4 * i:4 * i + 4] for i in range(n)]


def _row_tile(r, cap=640):
    return max(d for d in range(16, min(r, cap) + 1, 16) if r % d == 0)


def add_pairs(g, got, core):
    n, r, w = got.shape
    tr = _row_tile(r)

    def body(c_ref, a_ref, b_ref, o_ref):
        o_ref[...] = (a_ref[...].astype(F32) + b_ref[...].astype(F32)).astype(o_ref.dtype)

    spec = pl.BlockSpec((1, tr, w), lambda i, j, c: (i, j, 0))
    return pl.pallas_call(
        body, name="add_pairs", out_shape=jax.ShapeDtypeStruct(got.shape, got.dtype),
        grid_spec=pltpu.PrefetchScalarGridSpec(
            num_scalar_prefetch=1, grid=(n, r // tr),
            in_specs=[pl.BlockSpec((1, tr, w), lambda i, j, c: (2 * i + c[0], j, 0)), spec], out_specs=spec),
        compiler_params=_params(("parallel", "parallel")),
    )(core, g, got)


def sum_chips(part, land, chip):
    _, r, w = part.shape
    tr = _row_tile(r)

    def body(c_ref, p_ref, l_ref, o_ref):
        acc = p_ref[0].astype(F32)
        for s in range(N_CHIP - 1):
            acc = acc + l_ref[s].astype(F32)
        o_ref[...] = acc

    return pl.pallas_call(
        body, name="sum_chips", out_shape=jax.ShapeDtypeStruct((r, w), F32),
        grid_spec=pltpu.PrefetchScalarGridSpec(
            num_scalar_prefetch=1, grid=(r // tr,),
            in_specs=[pl.BlockSpec((1, tr, w), lambda i, c: (c[0], i, 0)), pl.BlockSpec((N_CHIP - 1, tr, w), lambda i, c: (0, i, 0))],
            out_specs=pl.BlockSpec((tr, w), lambda i, c: (i, 0))),
        compiler_params=_params(("parallel",)),
    )(chip, part, land)


def norm_x(x, g_mix, tm, comm=None):
    t = x.shape[0]
    grid = (t // tm,)
    c_ins, c_in_specs, c_outs, c_sems, alias = _comm_parts(comm, 2, 1)

    def body(*refs):
        (x_ref, g_ref), cin, (h_ref,), cout, _, csem = _split_refs(refs, 2, 1, 0, comm)
        _comm_start(comm, cin, cout, csem, grid)
        xv = x_ref[...]
        h_ref[...] = (xv * _rstd(xv, D_MODEL) * g_ref[...]).astype(BF16)
        _comm_finish(comm, cin, cout, csem, grid)

    return pl.pallas_call(
        body, name="norm_x", grid=grid, out_shape=(jax.ShapeDtypeStruct((t, D_MODEL), BF16), *c_outs),
        in_specs=[_rows(tm, D_MODEL), _resident((1, D_MODEL)), *c_in_specs],
        out_specs=(_rows(tm, D_MODEL), *([_ANY] * len(c_outs))),
        scratch_shapes=c_sems, input_output_aliases=alias, compiler_params=_params(("arbitrary",)),
    )(x, g_mix, *c_ins)


def pack_late_weights(w_up, w_down, w_o, w_pg, small, comm=None):
    rows2 = sum(PACK_ROWS[n] for n in PACK_W2)
    rows3 = sum(PACK_ROWS[n] for n in PACK_W3)
    c_ins, c_in_specs, c_outs, c_sems, alias = _comm_parts(comm, 5, 2)
    grid = (1,)

    def body(*refs):
        (up_ref, dn_ref, o_ref, pg_ref, sm_ref), cin, (p2_ref, p3_ref), cout, _, csem = _split_refs(refs, 5, 2, 0, comm)
        _comm_start(comm, cin, cout, csem, grid)
        p2_ref[0:128, :] = o_ref[0].astype(BF16)
        p2_ref[128:256, :] = pg_ref[0].astype(BF16)
        p2_ref[256:rows2, :] = sm_ref[...]
        p3_ref[0:512, :] = up_ref[0].T.astype(BF16)
        p3_ref[512:1024, :] = dn_ref[0].astype(BF16)
        _comm_finish(comm, cin, cout, csem, grid)

    def whole(a):
        nd = a.ndim
        return pl.BlockSpec(a.shape, lambda i: (0,) * nd)

    args = (w_up, w_down, w_o, w_pg, small)
    return pl.pallas_call(
        body, name="pack_late_weights", grid=grid,
        out_shape=(jax.ShapeDtypeStruct((rows2, D_MODEL), BF16), jax.ShapeDtypeStruct((rows3, D_MODEL), BF16), *c_outs),
        in_specs=[*[whole(a) for a in args], *c_in_specs],
        out_specs=(pl.BlockSpec((rows2, D_MODEL), lambda i: (0, 0)), pl.BlockSpec((rows3, D_MODEL), lambda i: (0, 0)),
                   *([_ANY] * len(c_outs))),
        scratch_shapes=c_sems, input_output_aliases=alias, compiler_params=_params(("arbitrary",)),
    )(*args, *c_ins)


def in_proj(h, w_in_t, tm, comm=None):
    t = h.shape[0]
    nc = 512
    grid = (t // tm,)
    c_ins, c_in_specs, c_outs, c_sems, alias = _comm_parts(comm, 2, 1)

    def body(*refs):
        (h_ref, w_ref), cin, (z_ref,), cout, _, csem = _split_refs(refs, 2, 1, 0, comm)
        _comm_start(comm, cin, cout, csem, grid)
        hv = h_ref[...]
        for cidx in range(ZP // nc):
            z_ref[:, cidx * nc:(cidx + 1) * nc] = _dot_nt(hv, w_ref[cidx * nc:(cidx + 1) * nc, :])
        _comm_finish(comm, cin, cout, csem, grid)

    return pl.pallas_call(
        body, name="in_proj", grid=grid, out_shape=(jax.ShapeDtypeStruct((t, ZP), F32), *c_outs),
        in_specs=[_rows(tm, D_MODEL), _resident((ZP, D_MODEL)), *c_in_specs],
        out_specs=(_rows(tm, ZP), *([_ANY] * len(c_outs))),
        scratch_shapes=c_sems, input_output_aliases=alias, compiler_params=_params(("arbitrary",)),
    )(h, w_in_t, *c_ins)


def attn_prep(zp, tabs, g_qa, g_kva, g_qn, g_kn, w_qb_t, w_kvb_t, tm, s_len, comm=None):
    t = zp.shape[0]
    nsb = s_len // tm
    scale_a = (QK_NOPE + QK_ROPE) ** -0.5
    scale_b = HD_B ** -0.5

    grid = (t // tm,)
    c_ins, c_in_specs, c_outs, c_sems, alias = _comm_parts(comm, 13, 8)

    def body(*refs):
        ((qb_ref, qlat_ref, kb_ref, vb_ref, ckv_ref, kpe_ref, tab_ref, gqa_ref, gkva_ref, gqn_ref, gkn_ref, wqb_ref,
          wkvb_ref), cin, (qa_o, ka_o, va_o, qb_o, kb_o, vb_o, cq_o, ckvn_o), cout, _, csem) = _split_refs(refs, 13, 8, 0, comm)
        _comm_start(comm, cin, cout, csem, grid)
        ca, s1a, s2a = tab_ref[0], tab_ref[1], tab_ref[2]
        ck = tab_ref[3]
        cb, s1b, s2b = tab_ref[4], tab_ref[5], tab_ref[6]
        ql = qlat_ref[...]
        cq = (ql * _rstd(ql, Q_LORA) * gqa_ref[...]).astype(BF16)
        cq_o[...] = cq
        qa = _dot_nt(cq, wqb_ref[...])
        slabs = [slice(h * HP, (h + 1) * HP) for h in range(H_A)]
        qa_o[...] = jnp.concatenate(
            [(_rope_fwd(qa[:, sl], ca, s1a, s2a) * scale_a).astype(BF16) for sl in slabs], axis=1)
        cr = ckv_ref[...]
        ckv = (cr * _rstd(cr, KV_LORA) * gkva_ref[...]).astype(BF16)
        ckvn_o[...] = ckv
        kva = _dot_nt(ckv, wkvb_ref[...])
        kpe = _rope_fwd(kpe_ref[...], ck, s1a, s2a)
        ka_o[...] = jnp.concatenate([(kva[:, sl] + kpe).astype(BF16) for sl in slabs], axis=1)
        va_o[...] = kva[:, H_A * HP:].astype(BF16)
        gqn, gkn = gqn_ref[...], gkn_ref[...]

        def norm_rope(ref, sl, g, scale):
            xs = ref[:, sl]
            y = _rope_fwd(xs * _rstd(xs, HD_B) * g, cb, s1b, s2b)
            return (y if scale is None else y * scale).astype(BF16)

        qb_o[...] = jnp.concatenate([norm_rope(qb_ref, sl, gqn, scale_b) for sl in slabs], axis=1)
        kb_o[...] = jnp.concatenate([norm_rope(kb_ref, sl, gkn, None) for sl in slabs[:KV_B]], axis=1)
        vb_o[...] = vb_ref[...].astype(BF16)
        _comm_finish(comm, cin, cout, csem, grid)

    def o(width):
        return jax.ShapeDtypeStruct((t, width), BF16)

    return pl.pallas_call(
        body, name="attn_prep", grid=grid,
        out_shape=(o(H_A * HP), o(H_A * HP), o(H_A * HP), o(H_B * HP), o(KV_B * HP), o(KV_B * HP), o(Q_LORA), o(KV_LORA),
                   *c_outs),
        in_specs=[_rows(tm, 1024, 2), _rows(tm, 256, 12), _rows(tm, 256, 13), _rows(tm, 256, 14),
                  _rows(tm, 128, 30), _rows(tm, 128, 31),
                  pl.BlockSpec((7, tm, HP), lambda i: (0, i % nsb, 0)),
                  _resident((1, Q_LORA)), _resident((1, KV_LORA)), _resident((1, HP)), _resident((1, HP)),
                  _resident((H_A * HP, Q_LORA)), _resident((2 * H_A * HP, KV_LORA)), *c_in_specs],
        out_specs=(_rows(tm, H_A * HP), _rows(tm, H_A * HP), _rows(tm, H_A * HP), _rows(tm, H_B * HP),
                   _rows(tm, KV_B * HP), _rows(tm, KV_B * HP), _rows(tm, Q_LORA), _rows(tm, KV_LORA), *([_ANY] * len(c_outs))),
        scratch_shapes=c_sems, input_output_aliases=alias, compiler_params=_params(("arbitrary",)),
    )(zp, zp, zp, zp, zp, zp, tabs, g_qa, g_kva, g_qn, g_kn, w_qb_t, w_kvb_t, *c_ins)


def attn_fwd(q, k, v, n_b, s_len, tq, name, comm=None):
    t = q.shape[0]
    n_h, n_hk = q.shape[1] // HP, k.shape[1] // HP
    grp = n_h // n_hk
    nq = s_len // tq
    sub = min(tq, 256)
    grid = (n_b, n_h, nq)
    c_ins, c_in_specs, c_outs, c_sems, alias = _comm_parts(comm, 3, 2)

    def body(*refs):
        (q_ref, k_ref, v_ref), cin, (o_ref, lse_ref), cout, _, csem = _split_refs(refs, 3, 2, 0, comm)
        _comm_start(comm, cin, cout, csem, grid)
        kv, vv = k_ref[...], v_ref[...]
        for r in range(tq // sub):
            rows = slice(r * sub, (r + 1) * sub)
            s = _dot_nt(q_ref[rows, :], kv)
            m = jnp.max(s, axis=-1, keepdims=True)
            p = jnp.exp(s - m)
            l = jnp.sum(p, axis=-1, keepdims=True)
            o_ref[rows, :] = (_dot_nn(p.astype(BF16), vv) * (1.0 / l)).astype(o_ref.dtype)
            lse_ref[rows, :] = jnp.broadcast_to(m + jnp.log(l), (sub, HP))
        _comm_finish(comm, cin, cout, csem, grid)

    qspec = pl.BlockSpec((tq, HP), lambda b, h, i: (b * nq + i, h))
    kspec = pl.BlockSpec((s_len, HP), lambda b, h, i: (b, h // grp))
    return pl.pallas_call(
        body, name=name, grid=grid,
        out_shape=(jax.ShapeDtypeStruct((t, n_h * HP), BF16), jax.ShapeDtypeStruct((t, n_h * HP), F32), *c_outs),
        in_specs=[qspec, kspec, kspec, *c_in_specs], out_specs=(qspec, qspec, *([_ANY] * len(c_outs))),
        scratch_shapes=c_sems, input_output_aliases=alias,
        compiler_params=_params(("arbitrary", "arbitrary", "arbitrary")),
    )(q, k, v, *c_ins)


def merge_fwd(oa, ob, zp, x, w_oa_t, w_ob_t, wpack, off, tm):
    t = x.shape[0]

    def body(oa_ref, ob_ref, ga_ref, gb_ref, x_ref, woa_ref, wob_ref, wo_ref, x1_o, mg_o, y_o):
        ya = _dot_nt(oa_ref[...], woa_ref[...])
        yb = _dot_nt(ob_ref[...], wob_ref[...])
        y_o[:, 0:D_MODEL] = ya.astype(BF16)
        y_o[:, D_MODEL:2 * D_MODEL] = yb.astype(BF16)
        merged = (jax.nn.sigmoid(ga_ref[...]) * ya + jax.nn.sigmoid(gb_ref[...]) * yb).astype(BF16)
        mg_o[...] = merged
        x1_o[...] = x_ref[...] + _dot_nn(merged, _wrows(wo_ref, 0, D_MODEL))

    return pl.pallas_call(
        body, name="merge_fwd", grid=(t // tm,),
        out_shape=(jax.ShapeDtypeStruct((t, D_MODEL), F32), jax.ShapeDtypeStruct((t, D_MODEL), BF16),
                   jax.ShapeDtypeStruct((t, 2 * D_MODEL), BF16)),
        in_specs=[_rows(tm, H_A * HP), _rows(tm, H_B * HP), _rows(tm, 1024, 0), _rows(tm, 1024, 1), _rows(tm, D_MODEL),
                  _resident((D_MODEL, H_A * HP)), _resident((D_MODEL, H_B * HP)), _packed_weight(128, off["w_o"])],
        out_specs=(_rows(tm, D_MODEL), _rows(tm, D_MODEL), _rows(tm, 2 * D_MODEL)), compiler_params=_params(("parallel",)),
    )(oa, ob, zp, zp, x, w_oa_t, w_ob_t, wpack)


def mlp_fwd(x1, g_mlp, wpack, off, tm):
    t = x1.shape[0]
    fc = 1024

    def body(x_ref, g_ref, wup_ref, wdn_ref, x2_o, u_o):
        xv = x_ref[...]
        h2 = (xv * _rstd(xv, D_MODEL) * g_ref[...]).astype(BF16)
        acc = xv
        for cidx in range(D_FF // fc):
            sl = slice(cidx * fc, (cidx + 1) * fc)
            u = jnp.maximum(_dot_nt(h2, _wrows(wup_ref, cidx * fc, fc)), 0.0)
            u_o[:, sl] = u.astype(BF16)
            acc = acc + _dot_nn((u * u).astype(BF16), _wrows(wdn_ref, cidx * fc, fc))
        x2_o[...] = acc

    return pl.pallas_call(
        body, name="mlp_fwd", grid=(t // tm,),
        out_shape=(jax.ShapeDtypeStruct((t, D_MODEL), F32), jax.ShapeDtypeStruct((t, D_FF), BF16)),
        in_specs=[_rows(tm, D_MODEL), _resident((1, D_MODEL)), _packed_weight(512, off["w_up"]), _packed_weight(512, off["w_down"])],
        out_specs=(_rows(tm, D_MODEL), _rows(tm, D_FF)), compiler_params=_params(("parallel",)),
    )(x1, g_mlp, wpack, wpack)


def ple_loss_bwd(x2, p, tgt, g_ple, g_final, wpack, off, w_ple_t, tm):
    t = x2.shape[0]
    inv_d = 1.0 / D_MODEL

    def body(x2_ref, p_ref, tg_ref, gp_ref, gf_ref, wpg_ref, wple_ref, dx2_o, dt_o, h3_o, dpe_o, st_o, dx2b_o):
        @pl.when(pl.program_id(0) == 0)
        def _():
            st_o[...] = jnp.zeros_like(st_o)

        x2v = x2_ref[...]
        gp, gf = gp_ref[...], gf_ref[...]
        w_pg = _wrows(wpg_ref, 0, D_MODEL)
        r2 = _rstd(x2v, D_MODEL)
        xh2 = x2v * r2
        h3 = (xh2 * gp).astype(BF16)
        h3_o[...] = h3
        gate = jax.nn.sigmoid(_dot_nn(h3, w_pg))
        pe = _dot_nt(p_ref[...].astype(BF16), wple_ref[...])
        x3 = x2v + gate * pe
        r3 = _rstd(x3, D_MODEL)
        xh3 = x3 * r3
        err = xh3 * gf - tg_ref[...]
        dy = err * inv_d
        dx3 = _rms_bwd(dy, xh3, r3, gf, D_MODEL)
        dpe_o[...] = (dx3 * gate).astype(BF16)
        dt = (dx3 * pe * gate * (1.0 - gate)).astype(BF16)
        dt_o[...] = dt
        dh3 = _dot_nt(dt, w_pg)
        dx2 = dx3 + _rms_bwd(dh3, xh2, r2, gp, D_MODEL)
        dx2_o[...] = dx2
        dx2b_o[...] = dx2.astype(BF16)
        st_o[0:1, :] += _colsum(dh3 * xh2)
        st_o[1:2, :] += _colsum(dy * xh3)
        st_o[2:3, :] += _colsum(err * err) * (0.5 * inv_d)

    bf = jax.ShapeDtypeStruct((t, D_MODEL), BF16)
    return pl.pallas_call(
        body, name="ple_loss_bwd", grid=(t // tm,),
        out_shape=(jax.ShapeDtypeStruct((t, D_MODEL), F32), bf, bf, bf, jax.ShapeDtypeStruct((3, D_MODEL), F32), bf),
        in_specs=[_rows(tm, D_MODEL), _rows(tm, PLE_DIM), _rows(tm, D_MODEL), _resident((1, D_MODEL)), _resident((1, D_MODEL)),
                  _packed_weight(128, off["w_ple_gate"]), _resident((D_MODEL, PLE_DIM))],
        out_specs=(_rows(tm, D_MODEL), _rows(tm, D_MODEL), _rows(tm, D_MODEL), _rows(tm, D_MODEL),
                   pl.BlockSpec((3, D_MODEL), lambda i: (0, 0)), _rows(tm, D_MODEL)),
        compiler_params=_params(("arbitrary",)),
    )(x2, p, tgt, g_ple, g_final, wpack, w_ple_t)


def mlp_bwd(dx2, x1, u, g_mlp, wpack, off, tm):
    t = x1.shape[0]
    fc = 1024

    def body(dx2_ref, x1_ref, u_ref, g_ref, wup_ref, wdn_ref, dx1_o, da_o, h2_o, st_o, dx1b_o):
        @pl.when(pl.program_id(0) == 0)
        def _():
            st_o[...] = jnp.zeros_like(st_o)

        d2 = dx2_ref[...]
        d2b = d2.astype(BF16)
        dh2 = jnp.zeros((tm, D_MODEL), F32)
        for cidx in range(D_FF // fc):
            sl = slice(cidx * fc, (cidx + 1) * fc)
            da = (_dot_nt(d2b, _wrows(wdn_ref, cidx * fc, fc)) * (2.0 * u_ref[:, sl].astype(F32))).astype(BF16)
            da_o[:, sl] = da
            dh2 = dh2 + _dot_nn(da, _wrows(wup_ref, cidx * fc, fc))
        xv = x1_ref[...]
        g = g_ref[...]
        r1 = _rstd(xv, D_MODEL)
        xh1 = xv * r1
        h2_o[...] = (xh1 * g).astype(BF16)
        st_o[...] += _colsum(dh2 * xh1)
        dx1 = d2 + _rms_bwd(dh2, xh1, r1, g, D_MODEL)
        dx1_o[...] = dx1
        dx1b_o[...] = dx1.astype(BF16)

    return pl.pallas_call(
        body, name="mlp_bwd", grid=(t // tm,),
        out_shape=(jax.ShapeDtypeStruct((t, D_MODEL), F32), jax.ShapeDtypeStruct((t, D_FF), BF16),
                   jax.ShapeDtypeStruct((t, D_MODEL), BF16), jax.ShapeDtypeStruct((1, D_MODEL), F32),
                   jax.ShapeDtypeStruct((t, D_MODEL), BF16)),
        in_specs=[_rows(tm, D_MODEL), _rows(tm, D_MODEL), _rows(tm, D_FF), _resident((1, D_MODEL)),
                  _packed_weight(512, off["w_up"]), _packed_weight(512, off["w_down"])],
        out_specs=(_rows(tm, D_MODEL), _rows(tm, D_FF), _rows(tm, D_MODEL), pl.BlockSpec((1, D_MODEL), lambda i: (0, 0)),
                   _rows(tm, D_MODEL)),
        compiler_params=_params(("arbitrary",)),
    )(dx2, x1, u, g_mlp, wpack, wpack)


def merge_bwd(dx1b, yab, zp, w_oa_t, w_ob_t, wpack, off, tm, comm=None):
    t = dx1b.shape[0]
    grid = (t // tm,)
    c_ins, c_in_specs, c_outs, c_sems, alias = _comm_parts(comm, 7, 5)

    def body(*refs):
        ((dx1_ref, y_ref, ga_ref, gb_ref, woa_ref, wob_ref, wo_ref), cin,
         (doa_o, dob_o, dg_o, dya_o, dyb_o), cout, _, csem) = _split_refs(refs, 7, 5, 0, comm)
        _comm_start(comm, cin, cout, csem, grid)
        dm = _dot_nt(dx1_ref[...], _wrows(wo_ref, 0, D_MODEL))
        for g_ref, w_ref, do_o, dy_o, col in ((ga_ref, woa_ref, doa_o, dya_o, 0), (gb_ref, wob_ref, dob_o, dyb_o, 1)):
            cols = slice(col * D_MODEL, (col + 1) * D_MODEL)
            sg = jax.nn.sigmoid(g_ref[...])
            dyv = (dm * sg).astype(BF16)
            dy_o[...] = dyv
            dg_o[:, cols] = (dm * y_ref[:, cols].astype(F32) * sg * (1.0 - sg)).astype(BF16)
            do_o[...] = _dot_nn(dyv, w_ref[...]).astype(BF16)
        _comm_finish(comm, cin, cout, csem, grid)

    bf = jax.ShapeDtypeStruct((t, D_MODEL), BF16)
    return pl.pallas_call(
        body, name="merge_bwd", grid=grid,
        out_shape=(bf, bf, jax.ShapeDtypeStruct((t, ZP), BF16), bf, bf, *c_outs),
        in_specs=[_rows(tm, D_MODEL), _rows(tm, 2 * D_MODEL), _rows(tm, 1024, 0), _rows(tm, 1024, 1),
                  _resident((D_MODEL, H_A * HP)), _resident((D_MODEL, H_B * HP)), _packed_weight(128, off["w_o"]), *c_in_specs],
        out_specs=(_rows(tm, D_MODEL), _rows(tm, D_MODEL), _rows(tm, 2 * D_MODEL), _rows(tm, D_MODEL), _rows(tm, D_MODEL),
                   *([_ANY] * len(c_outs))),
        scratch_shapes=c_sems, input_output_aliases=alias,
        compiler_params=_params(("arbitrary",)),
    )(dx1b, yab, zp, zp, w_oa_t, w_ob_t, wpack, *c_ins)


def attn_bwd(q, k, v, do, o, lse, n_b, s_len, tq, name, comm=None):
    t = q.shape[0]
    n_h, n_hk = q.shape[1] // HP, k.shape[1] // HP
    grp = n_h // n_hk
    nq = s_len // tq
    sub = min(tq, 256)
    grid = (n_b, n_hk, grp, nq)
    c_ins, c_in_specs, c_outs, c_sems, alias = _comm_parts(comm, 6, 3)

    def body(*refs):
        ((q_ref, k_ref, v_ref, do_ref, o_ref, lse_ref), cin, (dq_o, dk_o, dv_o), cout, (p_s, ds_s, dk_acc, dv_acc),
         csem) = _split_refs(refs, 6, 3, 4, comm)
        _comm_start(comm, cin, cout, csem, grid)

        @pl.when((pl.program_id(2) == 0) & (pl.program_id(3) == 0))
        def _():
            dk_acc[...] = jnp.zeros_like(dk_acc)
            dv_acc[...] = jnp.zeros_like(dv_acc)

        kv, vv = k_ref[...], v_ref[...]
        for r in range(tq // sub):
            rows = slice(r * sub, (r + 1) * sub)
            qv, dov = q_ref[rows, :], do_ref[rows, :]
            delta = jnp.sum(dov.astype(F32) * o_ref[rows, :].astype(F32), axis=-1, keepdims=True)
            p = jnp.exp(_dot_nt(qv, kv) - lse_ref[rows, 0:1])
            ds = (p * (_dot_nt(dov, vv) - delta)).astype(BF16)
            p_s[rows, :] = p.astype(BF16)
            ds_s[rows, :] = ds
            dq_o[rows, :] = _dot_nn(ds, kv).astype(dq_o.dtype)
        dk_acc[...] += _dot_tn(ds_s[...], q_ref[...])
        dv_acc[...] += _dot_tn(p_s[...], do_ref[...])

        @pl.when((pl.program_id(2) == grp - 1) & (pl.program_id(3) == nq - 1))
        def _():
            dk_o[...] = dk_acc[...].astype(dk_o.dtype)
            dv_o[...] = dv_acc[...].astype(dv_o.dtype)

        _comm_finish(comm, cin, cout, csem, grid)

    qspec = pl.BlockSpec((tq, HP), lambda b, hk, g, i: (b * nq + i, hk * grp + g))
    kspec = pl.BlockSpec((s_len, HP), lambda b, hk, g, i: (b, hk))
    return pl.pallas_call(
        body, name=name, grid=grid,
        out_shape=(jax.ShapeDtypeStruct((t, n_h * HP), BF16), jax.ShapeDtypeStruct((t, n_hk * HP), BF16),
                   jax.ShapeDtypeStruct((t, n_hk * HP), BF16), *c_outs),
        in_specs=[qspec, kspec, kspec, qspec, qspec, qspec, *c_in_specs],
        out_specs=(qspec, kspec, kspec, *([_ANY] * len(c_outs))),
        scratch_shapes=[pltpu.VMEM((tq, s_len), BF16), pltpu.VMEM((tq, s_len), BF16),
                        pltpu.VMEM((s_len, HP), F32), pltpu.VMEM((s_len, HP), F32), *c_sems],
        input_output_aliases=alias,
        compiler_params=_params(("arbitrary", "arbitrary", "arbitrary", "arbitrary")),
    )(q, k, v, do, o, lse, *c_ins)


def prep_bwd(dqa, dka, dva, dqb, dkb, dvb, zp, dz, tabs, g_qa, g_kva, g_qn, g_kn, w_qb_t, w_kvb_t, tm, s_len):
    t = zp.shape[0]
    nsb = s_len // tm
    scale_a = (QK_NOPE + QK_ROPE) ** -0.5
    scale_b = HD_B ** -0.5

    def body(dqa_ref, dka_ref, dva_ref, dqb_ref, dkb_ref, dvb_ref, qb_ref, qlat_ref, kb_ref, ckv_ref, tab_ref,
             gqa_ref, gkva_ref, gqn_ref, gkn_ref, wqb_ref, wkvb_ref, _, dz_o, dqap_o, dkva_o, st_o):
        dzq_o, dsm_o = dz_o.at[:, 0:1024], dz_o.at[:, 1024:2048]

        @pl.when(pl.program_id(0) == 0)
        def _():
            st_o[...] = jnp.zeros_like(st_o)

        ca, s1a, s2a = tab_ref[0], tab_ref[1], tab_ref[2]
        ck = tab_ref[3]
        cb, s1b, s2b = tab_ref[4], tab_ref[5], tab_ref[6]
        for h in range(H_A):
            sl = slice(h * HP, (h + 1) * HP)
            dqap_o[:, sl] = _rope_bwd(dqa_ref[:, sl].astype(F32) * scale_a, ca, s1a, s2a).astype(BF16)
        dcq = _dot_nn(dqap_o[...], wqb_ref[...])
        ql = qlat_ref[...]
        rq = _rstd(ql, Q_LORA)
        xh = ql * rq
        gqa = gqa_ref[...]
        st_o[0:1, :] += _colsum(dcq * xh)
        dsm_o[:, 0:256] = _rms_bwd(dcq, xh, rq, gqa, Q_LORA).astype(BF16)
        dkpe = jnp.zeros((tm, HP), F32)
        for h in range(H_A):
            sl = slice(h * HP, (h + 1) * HP)
            dk = dka_ref[:, sl]
            dkpe = dkpe + dk.astype(F32)
            dkva_o[:, sl] = dk.astype(BF16)
        dkva_o[:, H_A * HP:] = dva_ref[...].astype(BF16)
        dsm_o[:, 896:1024] = _rope_bwd(dkpe, ck, s1a, s2a).astype(BF16)
        dckv = _dot_nn(dkva_o[...], wkvb_ref[...])
        cr = ckv_ref[...]
        rk = _rstd(cr, KV_LORA)
        xh = cr * rk
        st_o[1:2, 0:128] += _colsum(dckv * xh)
        dsm_o[:, 768:896] = _rms_bwd(dckv, xh, rk, gkva_ref[...], KV_LORA).astype(BF16)
        gqn, gkn = gqn_ref[...], gkn_ref[...]
        dgq = jnp.zeros((1, HP), F32)
        for h in range(H_B):
            sl = slice(h * HP, (h + 1) * HP)
            dy = _rope_bwd(dqb_ref[:, sl].astype(F32) * scale_b, cb, s1b, s2b)
            xs = qb_ref[:, sl]
            r = _rstd(xs, HD_B)
            xh = xs * r
            dgq = dgq + _colsum(dy * xh)
            dzq_o[:, sl] = _rms_bwd(dy, xh, r, gqn, HD_B).astype(BF16)
        st_o[2:3, 0:128] += dgq
        dgk = jnp.zeros((1, HP), F32)
        for h in range(KV_B):
            sl = slice(h * HP, (h + 1) * HP)
            dy = _rope_bwd(dkb_ref[:, sl].astype(F32), cb, s1b, s2b)
            xs = kb_ref[:, sl]
            r = _rstd(xs, HD_B)
            xh = xs * r
            dgk = dgk + _colsum(dy * xh)
            dsm_o[:, 256 + h * HP:256 + (h + 1) * HP] = _rms_bwd(dy, xh, r, gkn, HD_B).astype(BF16)
        st_o[3:4, 0:128] += dgk
        dsm_o[:, 512:768] = dvb_ref[...].astype(BF16)

    return pl.pallas_call(
        body, name="prep_bwd", grid=(t // tm,),
        out_shape=(jax.ShapeDtypeStruct((t, ZP), BF16), jax.ShapeDtypeStruct((t, 1024), BF16),
                   jax.ShapeDtypeStruct((t, 2048), BF16), jax.ShapeDtypeStruct((4, 256), F32)),
        in_specs=[_rows(tm, 1024), _rows(tm, 1024), _rows(tm, 1024), _rows(tm, 1024), _rows(tm, 256), _rows(tm, 256),
                  _rows(tm, 1024, 2), _rows(tm, 256, 12), _rows(tm, 256, 13), _rows(tm, 128, 30),
                  pl.BlockSpec((7, tm, HP), lambda i: (0, i % nsb, 0)),
                  _resident((1, Q_LORA)), _resident((1, KV_LORA)), _resident((1, HP)), _resident((1, HP)),
                  _resident((H_A * HP, Q_LORA)), _resident((2 * H_A * HP, KV_LORA)), _ANY],
        out_specs=(_rows(tm, 2048, 1), _rows(tm, 1024), _rows(tm, 2048), pl.BlockSpec((4, 256), lambda i: (0, 0))),
        input_output_aliases={17: 0}, compiler_params=_params(("arbitrary",)),
    )(dqa, dka, dva, dqb, dkb, dvb, zp, zp, zp, zp, tabs, g_qa, g_kva, g_qn, g_kn, w_qb_t, w_kvb_t, dz)


def in_bwd(dz, x, dx1, g_mix, w_in_t, tm, comm=None):
    t = x.shape[0]
    grid = (t // tm,)
    c_ins, c_in_specs, c_outs, c_sems, alias = _comm_parts(comm, 5, 2)

    def body(*refs):
        (dz_ref, x_ref, dx1_ref, g_ref, w_ref), cin, (dx_o, st_o), cout, _, csem = _split_refs(refs, 5, 2, 0, comm)
        _comm_start(comm, cin, cout, csem, grid)

        @pl.when(pl.program_id(0) == 0)
        def _():
            st_o[...] = jnp.zeros_like(st_o)

        dh = _dot_nn(dz_ref[...], w_ref[...])
        xv = x_ref[...]
        g = g_ref[...]
        r = _rstd(xv, D_MODEL)
        xh = xv * r
        st_o[...] += _colsum(dh * xh)
        dx_o[...] = dx1_ref[...] + _rms_bwd(dh, xh, r, g, D_MODEL)
        _comm_finish(comm, cin, cout, csem, grid)

    return pl.pallas_call(
        body, name="in_bwd", grid=grid,
        out_shape=(jax.ShapeDtypeStruct((t, D_MODEL), F32), jax.ShapeDtypeStruct((1, D_MODEL), F32), *c_outs),
        in_specs=[_rows(tm, ZP), _rows(tm, D_MODEL), _rows(tm, D_MODEL),
                  _resident((1, D_MODEL)), _resident((ZP, D_MODEL)), *c_in_specs],
        out_specs=(_rows(tm, D_MODEL), pl.BlockSpec((1, D_MODEL), lambda i: (0, 0)), *([_ANY] * len(c_outs))),
        scratch_shapes=c_sems, input_output_aliases=alias,
        compiler_params=_params(("arbitrary",)),
    )(dz, x, dx1, g_mix, w_in_t, *c_ins)


def matmul_tn(a, b, name, square_a=False):
    t, m = a.shape
    n = b.shape[1]
    bm = min(m, 512)
    tk = min(t, 4096)

    def body(a_ref, b_ref, o_ref):
        @pl.when(pl.program_id(1) == 0)
        def _():
            o_ref[...] = jnp.zeros_like(o_ref)

        av = a_ref[...]
        if square_a:
            av = (av.astype(F32) * av.astype(F32))
        o_ref[...] += _dot_tn(av.astype(BF16), b_ref[...].astype(BF16))

    return pl.pallas_call(
        body, name=name, grid=(m // bm, t // tk), out_shape=jax.ShapeDtypeStruct((m, n), F32),
        in_specs=[pl.BlockSpec((tk, bm), lambda i, kk: (kk, i)), pl.BlockSpec((tk, n), lambda i, kk: (kk, 0))],
        out_specs=pl.BlockSpec((bm, n), lambda i, kk: (i, 0)),
        compiler_params=_params(("parallel", "arbitrary")),
    )(a, b)


def matmul_tn_packed(a, b, name, rows, row_off, total_rows, buf=None, square_a=False):
    t, m = a.shape
    n = b.shape[1]
    pd = max(1, 512 // rows)
    bm = pd * rows
    tk = min(t, 4096)
    nk = t // tk

    def body(a_ref, b_ref, *rest):
        o_ref, acc = rest[-2], rest[-1]

        @pl.when(pl.program_id(1) == 0)
        def _():
            acc[...] = jnp.zeros_like(acc)

        av = a_ref[...]
        if square_a:
            av = (av.astype(F32) * av.astype(F32))
        acc[...] += _dot_tn(av.astype(BF16), b_ref[...].astype(BF16))

        @pl.when(pl.program_id(1) == nk - 1)
        def _():
            o_ref[...] = acc[...].reshape(pd, rows, n).astype(o_ref.dtype)

    in_specs = [pl.BlockSpec((tk, bm), lambda i, kk: (kk, i)), pl.BlockSpec((tk, n), lambda i, kk: (kk, 0))]
    args = [a, b]
    if buf is not None:
        in_specs.append(_ANY)
        args.append(buf)
    return pl.pallas_call(
        body, name=name, grid=(m // bm, nk), out_shape=jax.ShapeDtypeStruct((N_DEV, total_rows, n), BF16),
        in_specs=in_specs, out_specs=pl.BlockSpec((pd, rows, n), lambda i, kk: (i, row_off // rows, 0)),
        scratch_shapes=[pltpu.VMEM((bm, n), F32)], input_output_aliases={2: 0} if buf is not None else {},
        compiler_params=_params(("parallel", "arbitrary")),
    )(*args)


def adamw(w, g, m, v, name, g_transposed=False):
    _, r, c = w.shape
    tr = r if (g_transposed or r <= 256) else 256
    c1 = 1.0 - ADAM_B1 ** ADAM_STEP
    c2 = 1.0 - ADAM_B2 ** ADAM_STEP

    def body(w_ref, g_ref, m_ref, v_ref, g_o, d_o, m_o, v_o):
        gv = g_ref[...].T if g_transposed else g_ref[...]
        mn = ADAM_B1 * m_ref[0] + (1.0 - ADAM_B1) * gv
        vn = ADAM_B2 * v_ref[0] + (1.0 - ADAM_B2) * (gv * gv)
        g_o[0] = gv
        m_o[0] = mn
        v_o[0] = vn
        d_o[0] = -ADAM_LR * ((mn / c1) / (jnp.sqrt(vn / c2) + ADAM_EPS) + ADAM_WD * w_ref[0])

    spec = pl.BlockSpec((1, tr, c), lambda i: (0, i, 0))
    gspec = pl.BlockSpec((c, r), lambda i: (0, 0)) if g_transposed else pl.BlockSpec((tr, c), lambda i: (i, 0))
    shp = jax.ShapeDtypeStruct((1, r, c), F32)
    return pl.pallas_call(
        body, name=name, grid=(r // tr,), out_shape=(shp,) * 4, in_specs=[spec, gspec, spec, spec], out_specs=(spec,) * 4,
        compiler_params=_params(("parallel",)),
    )(w, g, m, v)


def _rope_tables(s_len):
    def angles(pos, dim):
        inv = np.float32(ROPE_THETA) ** (-np.arange(0, dim, 2, dtype=np.float32) / np.float32(dim))
        return pos.astype(np.float32)[:, None] * inv[None, :]

    tpos = np.arange(s_len)
    a1 = angles(tpos, QK_ROPE)
    ar = angles(tpos // GRID_W, HD_B // 2)
    ac = angles(tpos % GRID_W, HD_B // 2)
    z16 = np.zeros((s_len, 16), np.float32)
    z32 = np.zeros((s_len, 32), np.float32)
    z64 = np.zeros((s_len, 64), np.float32)
    one64 = np.ones((s_len, 64), np.float32)
    c1, s1 = np.cos(a1), np.sin(a1)
    ca = np.concatenate([one64, c1, c1, z32], axis=1)
    ck = np.concatenate([z64, c1, c1, z32], axis=1)
    s1a = np.concatenate([z64, -s1, z16, z32], axis=1)
    s2a = np.concatenate([z64, z16, s1, z32], axis=1)
    cr, sr, cc, sc = np.cos(ar), np.sin(ar), np.cos(ac), np.sin(ac)
    cb = np.concatenate([cr, cr, cc, cc, z64], axis=1)
    s1b = np.concatenate([-sr, z16, -sc, z16, z64], axis=1)
    s2b = np.concatenate([z16, sr, z16, sc, z64], axis=1)
    return jnp.asarray(np.stack([ca, s1a, s2a, ck, cb, s1b, s2b]).astype(np.float32))


def _pad_heads(a, n_heads, axis):
    shp = a.shape
    a = a.reshape(shp[:axis] + (n_heads, shp[axis] // n_heads) + shp[axis + 1:])
    pad = [(0, 0)] * a.ndim
    pad[axis + 1] = (0, HP - a.shape[axis + 1])
    a = jnp.pad(a, pad)
    return a.reshape(shp[:axis] + (n_heads * HP,) + shp[axis + 1:])


def _unpad_heads(a, n_heads, width, axis):
    shp = a.shape
    a = a.reshape(shp[:axis] + (n_heads, HP) + shp[axis + 1:])
    a = lax.slice_in_dim(a, 0, width, axis=axis + 1)
    return a.reshape(shp[:axis] + (n_heads * width,) + shp[axis + 1:])


def _pack_rows(blocks, names):
    parts = []
    for name in names:
        b = blocks[name]
        padr = PACK_ROWS[name] - b.shape[-2]
        if padr:
            b = jnp.pad(b, [(0, 0)] * (b.ndim - 2) + [(0, padr), (0, 0)])
        parts.append(b)
    return jnp.concatenate(parts, axis=parts[0].ndim - 2)


def _expand_w_in(wt):
    z64 = jnp.zeros((64, D_MODEL), wt.dtype)
    z32 = jnp.zeros((32, D_MODEL), wt.dtype)
    return jnp.concatenate([
        wt[1184:2208], wt[2208:3232], _pad_heads(wt[416:928], H_B, 0), wt[0:256],
        _pad_heads(wt[928:1056], KV_B, 0), _pad_heads(wt[1056:1184], KV_B, 0), wt[256:384],
        z64, wt[384:416], z32], axis=0)


def _collapse_w_in(dw):
    dg, dq, ds = dw[0:2048], dw[2048:3072], dw[3072:4096]
    return jnp.concatenate([
        ds[0:256], ds[768:896], ds[960:992], _unpad_heads(dq, H_B, HD_B, 0), _unpad_heads(ds[256:512], KV_B, HD_B, 0),
        _unpad_heads(ds[512:768], KV_B, HD_B, 0), dg], axis=0)


def kernel(x, p, g_mix, w_in, g_qa, w_qb, g_kva, w_kvb, g_qn, g_kn, w_oa, w_ob, w_o, g_mlp, w_up, w_down, g_ple, w_ple_gate, w_ple, g_final, loss_target, m_g_mix, m_w_in, m_g_qa, m_w_qb, m_g_kva, m_w_kvb, m_g_qn, m_g_kn, m_w_oa, m_w_ob, m_w_o, m_g_mlp, m_w_up, m_w_down, m_g_ple, m_w_ple_gate, m_w_ple, m_g_final, v_g_mix, v_w_in, v_g_qa, v_w_qb, v_g_kva, v_w_kvb, v_g_qn, v_g_kn, v_w_oa, v_w_ob, v_w_o, v_g_mlp, v_w_up, v_w_down, v_g_ple, v_w_ple_gate, v_w_ple, v_g_final):
    n_b, s_len, _ = x.shape
    t = n_b * s_len
    tm = min(512, s_len)
    tq_f = min(2048, s_len)
    tq_b = min(2048, s_len)

    mats = dict(w_in=(w_in, m_w_in, v_w_in), w_qb=(w_qb, m_w_qb, v_w_qb), w_kvb=(w_kvb, m_w_kvb, v_w_kvb),
                w_oa=(w_oa, m_w_oa, v_w_oa), w_ob=(w_ob, m_w_ob, v_w_ob), w_o=(w_o, m_w_o, v_w_o),
                w_up=(w_up, m_w_up, v_w_up), w_down=(w_down, m_w_down, v_w_down),
                w_ple_gate=(w_ple_gate, m_w_ple_gate, v_w_ple_gate), w_ple=(w_ple, m_w_ple, v_w_ple))
    col_sharded = ("w_in", "w_qb", "w_kvb", "w_oa", "w_ob", "w_up", "w_ple")

    blocks = {}
    for name in PACK_W1 + ("w_oa", "w_ob", "w_ple"):
        blocks[name] = mats[name][0][0].T.reshape(-1, D_MODEL).astype(BF16)
    off_w1, _ = _pack_offsets(PACK_W1)
    off_w2, _ = _pack_offsets(PACK_W2)
    off_w3, _ = _pack_offsets(PACK_W3)
    xf = x.reshape(t, D_MODEL)
    h, full1 = norm_x(xf, g_mix, tm, comm=gather_first_comm(_pack_rows(blocks, PACK_W1)))
    pack2, pack3, full1 = pack_late_weights(w_up, w_down, w_o, w_ple_gate, _pack_rows(blocks, ("w_oa", "w_ob", "w_ple")),
                                            comm=gather_pass_comm(full1))

    def gathered(full, offs, name, rows, width):
        return full[:, offs[name]:offs[name] + rows].reshape(-1, width)

    w_in_t = _expand_w_in(gathered(full1, off_w1, "w_in", 404, D_MODEL))
    w_qb_t = _pad_heads(gathered(full1, off_w1, "w_qb", 24, Q_LORA), H_A, 0)
    wkvb = gathered(full1, off_w1, "w_kvb", 16, KV_LORA).reshape(H_A, 2, 64, KV_LORA)
    w_kvb_t = jnp.concatenate([_pad_heads(wkvb[:, 0].reshape(-1, KV_LORA), H_A, 0),
                               _pad_heads(wkvb[:, 1].reshape(-1, KV_LORA), H_A, 0)], axis=0)

    tabs = _rope_tables(s_len)
    g_qn_p = jnp.pad(g_qn, ((0, 0), (0, HP - HD_B)))
    g_kn_p = jnp.pad(g_kn, ((0, 0), (0, HP - HD_B)))
    pf = p.reshape(t, PLE_DIM)
    tgt = loss_target.reshape(t, D_MODEL)

    zp, full2 = in_proj(h, w_in_t, tm, comm=gather_first_comm(pack2))
    qa, ka, va, qb, kb, vb, cq, ckv, full2 = attn_prep(zp, tabs, g_qa, g_kva, g_qn_p, g_kn_p, w_qb_t, w_kvb_t, tm, s_len,
                                                       comm=gather_pass_comm(full2))
    oa, lse_a, full3 = attn_fwd(qa, ka, va, n_b, s_len, tq_f, "attn_a_fwd", comm=gather_first_comm(pack3))
    ob, lse_b, full3 = attn_fwd(qb, kb, vb, n_b, s_len, tq_f, "attn_b_fwd", comm=gather_pass_comm(full3))
    w_oa_t = _pad_heads(gathered(full2, off_w2, "w_oa", 64, H_A * V_DIM_A), H_A, 1)
    w_ob_t = _pad_heads(gathered(full2, off_w2, "w_ob", 64, H_B * HD_B), H_B, 1)
    w_ple_t = gathered(full2, off_w2, "w_ple", 32, PLE_DIM)
    x1, merged, yab = merge_fwd(oa, ob, zp, xf, w_oa_t, w_ob_t, full2, off_w2, tm)
    x2, u = mlp_fwd(x1, g_mlp, full3, off_w3, tm)
    dx2, dt, h3, dpe, st_ple, dx2b = ple_loss_bwd(x2, pf, tgt, g_ple, g_final.reshape(1, D_MODEL), full2, off_w2, w_ple_t, tm)
    dx1, da, h2, st_mlp, dx1b = mlp_bwd(dx2, x1, u, g_mlp, full3, off_w3, tm)

    core = lax.axis_index("c").astype(jnp.int32).reshape(1)
    chip = (2 * lax.axis_index("x") + lax.axis_index("y")).astype(jnp.int32).reshape(1)

    def packed(gblocks, names):
        return _pack_rows({n: gblocks[n].reshape(N_DEV, -1, D_MODEL).astype(BF16) for n in names}, names)

    off_g1, rows_g1 = _pack_offsets(PACK_G1)
    gpack1 = matmul_tn_packed(da, h2, "gw_up", 512, off_g1["w_up"], rows_g1)
    gpack1 = matmul_tn_packed(u, dx2b, "gw_down", 512, off_g1["w_down"], rows_g1, buf=gpack1, square_a=True)
    gpack1 = matmul_tn_packed(h3, dt, "gw_pg", 128, off_g1["w_ple_gate"], rows_g1, buf=gpack1)
    gple = matmul_tn(dpe, pf, "gw_ple").reshape(N_DEV, -1, D_MODEL).astype(BF16)
    gpack1 = lax.dynamic_update_slice(gpack1, gple, (0, off_g1["w_ple"], 0))
    doa, dob, dz, dya, dyb, got1 = merge_bwd(dx1b, yab, zp, w_oa_t, w_ob_t, full2, off_w2, tm,
                                               comm=scatter_sibling_comm(gpack1))
    part1 = add_pairs(gpack1, got1, core)
    dqa, dka, dva, land1 = attn_bwd(qa, ka, va, doa, oa, lse_a, n_b, s_len, tq_b, "attn_a_bwd", comm=scatter_chips_comm(part1))
    gshard1 = sum_chips(part1, land1, chip)

    off_g2, rows_g2 = _pack_offsets(PACK_G2)
    g2 = dict(w_oa=_unpad_heads(matmul_tn(dya, oa, "gw_oa"), H_A, V_DIM_A, 1),
              w_ob=_unpad_heads(matmul_tn(dyb, ob, "gw_ob"), H_B, HD_B, 1))
    gpack2 = matmul_tn_packed(merged, dx1b, "gw_o", 128, off_g2["w_o"], rows_g2)
    gpack2 = lax.dynamic_update_slice(gpack2, packed(g2, ("w_oa", "w_ob")), (0, off_g2["w_oa"], 0))
    part2 = add_pairs(gpack2, exchange_sibling(gpack2), core)
    dqb, dkb, dvb, land2 = attn_bwd(qb, kb, vb, dob, ob, lse_b, n_b, s_len, tq_b, "attn_b_bwd", comm=scatter_chips_comm(part2))
    gshard2 = sum_chips(part2, land2, chip)
    dz, dqap, dkva, st_prep = prep_bwd(dqa, dka, dva, dqb, dkb, dvb, zp, dz, tabs, g_qa, g_kva, g_qn_p, g_kn_p,
                                       w_qb_t, w_kvb_t, tm, s_len)

    gkv = matmul_tn(dkva, ckv, "gw_kvb")
    g3 = dict(
        w_in=_collapse_w_in(matmul_tn(dz, h, "gw_in")),
        w_qb=_unpad_heads(matmul_tn(dqap, cq, "gw_qb"), H_A, QK_NOPE + QK_ROPE, 0),
        w_kvb=jnp.stack([_unpad_heads(gkv[:H_A * HP], H_A, 64, 0).reshape(H_A, 64, KV_LORA),
                         _unpad_heads(gkv[H_A * HP:], H_A, 64, 0).reshape(H_A, 64, KV_LORA)], axis=1))
    gpack3 = packed(g3, PACK_G3)
    part3 = add_pairs(gpack3, exchange_sibling(gpack3), core)
    grad_x, st_mix, land3 = in_bwd(dz, xf, dx1, g_mix, w_in_t, tm, comm=scatter_chips_comm(part3))
    gshard3 = sum_chips(part3, land3, chip)
    off_g3, _ = _pack_offsets(PACK_G3)
    shards = {n: (gshard1, off_g1[n]) for n in PACK_G1}
    shards.update({n: (gshard2, off_g2[n]) for n in PACK_G2})
    shards.update({n: (gshard3, off_g3[n]) for n in PACK_G3})

    stats = allreduce_stats(st_mix, st_prep, st_mlp, st_ple)
    loss = jnp.sum(stats[ST_LOSS])

    out_g, out_d, out_m, out_v = {}, {}, {}, {}
    for name, (w, m, v) in mats.items():
        gshard, off = shards[name]
        r, c = w.shape[1:]
        if name in col_sharded:
            g2 = gshard[off:off + (r * c) // D_MODEL].reshape(c, r)
            in_kernel = r % 128 == 0 and c % 128 == 0
            res = adamw(w, g2 if in_kernel else g2.T, m, v, "adamw_" + name, g_transposed=in_kernel)
        else:
            res = adamw(w, gshard[off:off + r], m, v, "adamw_" + name)
        out_g[name], out_d[name], out_m[name], out_v[name] = res

    gains = (("g_mix", g_mix, m_g_mix, v_g_mix, ST_G_MIX), ("g_qa", g_qa, m_g_qa, v_g_qa, ST_G_QA),
             ("g_kva", g_kva, m_g_kva, v_g_kva, ST_G_KVA), ("g_qn", g_qn, m_g_qn, v_g_qn, ST_G_QN),
             ("g_kn", g_kn, m_g_kn, v_g_kn, ST_G_KN), ("g_mlp", g_mlp, m_g_mlp, v_g_mlp, ST_G_MLP),
             ("g_ple", g_ple, m_g_ple, v_g_ple, ST_G_PLE), ("g_final", g_final, m_g_final, v_g_final, ST_G_FINAL))
    res = adamw_gains(stats, [(r_, w.reshape(1, -1), m.reshape(1, -1), v.reshape(1, -1)) for _, w, m, v, r_ in gains])
    for (name, w, _, _, _), (gg, gd, gm, gv) in zip(gains, res):
        out_g[name], out_d[name], out_m[name], out_v[name] = (a.reshape(w.shape) for a in (gg, gd, gm, gv))

    order = ("g_mix", "w_in", "g_qa", "w_qb", "g_kva", "w_kvb", "g_qn", "g_kn", "w_oa", "w_ob", "w_o", "g_mlp",
             "w_up", "w_down", "g_ple", "w_ple_gate", "w_ple", "g_final")
    return (loss, grad_x.reshape(x.shape), *[out_g[n] for n in order], *[out_d[n] for n in order],
            *[out_m[n] for n in order], *[out_v[n] for n in order])
```

```python
import numpy as np
import jax
import jax.numpy as jnp
from jax import lax
from jax.experimental import pallas as pl
from jax.experimental.pallas import tpu as pltpu

F32 = jnp.float32
BF16 = jnp.bfloat16

D_MODEL = 1024
EPS = 1e-6
ROPE_THETA = 10000.0
GRID_W = 64
H_A = 8
QK_NOPE = 64
QK_ROPE = 32
V_DIM_A = 64
Q_LORA = 256
KV_LORA = 128
H_B = 8
KV_B = 2
HD_B = 64
D_FF = 4 * D_MODEL
PLE_DIM = 256
HP = 128
ZP = 4096
N_DEV = 8
N_CHIP = 4

ADAM_LR = 0.001
ADAM_B1 = 0.9
ADAM_B2 = 0.999
ADAM_EPS = 1e-08
ADAM_WD = 0.01
ADAM_STEP = 10

VMEM_LIMIT = 52 * 1024 * 1024

PACK_ROWS = dict(w_in=416, w_qb=32, w_kvb=16, w_oa=64, w_ob=64, w_o=128, w_up=512, w_down=512, w_ple_gate=128, w_ple=32)
PACK_W1 = ("w_in", "w_qb", "w_kvb")
PACK_W2 = ("w_o", "w_ple_gate", "w_oa", "w_ob", "w_ple")
PACK_W3 = ("w_up", "w_down")
PACK_G1 = ("w_up", "w_down", "w_ple_gate", "w_ple")
PACK_G2 = ("w_o", "w_oa", "w_ob")
PACK_G3 = ("w_in", "w_qb", "w_kvb")


def _pack_offsets(names):
    off, o = {}, 0
    for n in names:
        off[n] = o
        o += PACK_ROWS[n]
    return off, o

ST_G_MIX, ST_G_QA, ST_G_KVA, ST_G_QN, ST_G_KN, ST_G_MLP, ST_G_PLE, ST_G_FINAL, ST_LOSS = range(9)
ST_ROWS = 16


def _dot_nn(a, b):
    return lax.dot_general(a, b, (((1,), (0,)), ((), ())), preferred_element_type=F32)


def _dot_nt(a, b):
    return lax.dot_general(a, b, (((1,), (1,)), ((), ())), preferred_element_type=F32)


def _dot_tn(a, b):
    return lax.dot_general(a, b, (((0,), (0,)), ((), ())), preferred_element_type=F32)


def _rstd(x, n):
    return lax.rsqrt(jnp.sum(x * x, axis=-1, keepdims=True) * (1.0 / n) + EPS)


def _rms_bwd(dy, xh, r, g, n):
    dxh = dy * g
    return r * (dxh - xh * (jnp.sum(dxh * xh, axis=-1, keepdims=True) * (1.0 / n)))


def _rope_fwd(x, c, s1, s2):
    return x * c + pltpu.roll(x, HP - 16, 1) * s1 + pltpu.roll(x, 16, 1) * s2


def _rope_bwd(d, c, s1, s2):
    return d * c + pltpu.roll(d * s1, 16, 1) + pltpu.roll(d * s2, HP - 16, 1)


def _colsum(v):
    return jnp.sum(v, axis=0, keepdims=True)


def _params(sem=None, vmem=VMEM_LIMIT):
    return pltpu.CompilerParams(dimension_semantics=sem, vmem_limit_bytes=vmem)


def _resident(shape):
    nd = len(shape)
    return pl.BlockSpec(shape, lambda *_: (0,) * nd, pipeline_mode=pl.Buffered(1))


def _rows(tm, width, col=0):
    return pl.BlockSpec((tm, width), lambda i: (i, col))


def _packed_weight(rows, off):
    return pl.BlockSpec((N_DEV, rows, D_MODEL), lambda *_: (0, off // rows, 0), pipeline_mode=pl.Buffered(1))


def _wrows(ref, start, size):
    rows = ref.shape[1]
    return ref[start // rows:(start + size) // rows].reshape(size, D_MODEL)


def _mesh_pos():
    return lax.axis_index("x"), lax.axis_index("y"), lax.axis_index("c")


def _flip(v, bit):
    return (1 - v) if bit else v


_ANY = pl.BlockSpec(memory_space=pl.ANY)
_MESH = pl.DeviceIdType.MESH


def _remote(src, dst, send_sems, recv_sems, k, to):
    return pltpu.make_async_remote_copy(src_ref=src, dst_ref=dst, send_sem=send_sems.at[k], recv_sem=recv_sems.at[k],
                                        device_id=to, device_id_type=_MESH)


def _sibling_copies(g_ref, got_ref, send_sems, recv_sems):
    x, y, c = _mesh_pos()
    return [_remote(g_ref.at[2 * j + (1 - c)], got_ref.at[j], send_sems, recv_sems, j, (x, y, 1 - c)) for j in range(N_CHIP)]


def _chip_copies(p_ref, land_ref, send_sems, recv_sems):
    x, y, c = _mesh_pos()
    copies = []
    for k in (1, 2, 3):
        tx, ty = _flip(x, k & 2), _flip(y, k & 1)
        copies.append(_remote(p_ref.at[2 * tx + ty], land_ref.at[k - 1], send_sems, recv_sems, k - 1, (tx, ty, c)))
    return copies


class _Comm:
    def __init__(self, ins, out_shapes, sems, make, aliases=None):
        self.ins, self.out_shapes, self.sems, self.make, self.aliases = list(ins), list(out_shapes), list(sems), make, aliases or {}


def _comm_parts(comm, n_in, n_out):
    if comm is None:
        return [], [], [], [], {}
    alias = {n_in + j: n_out + k for j, k in comm.aliases.items()}
    return comm.ins, [_ANY] * len(comm.ins), comm.out_shapes, comm.sems, alias


def _split_refs(refs, n_in, n_out, n_scratch, comm):
    n_ci = len(comm.ins) if comm else 0
    n_co = len(comm.out_shapes) if comm else 0
    cuts, i = [], 0
    for n in (n_in, n_ci, n_out, n_co, n_scratch):
        cuts.append(refs[i:i + n])
        i += n
    return (*cuts, refs[i:])


def _grid_edge(grid, last):
    cond = None
    for d, n in enumerate(grid):
        here = pl.program_id(d) == (n - 1 if last else 0)
        cond = here if cond is None else cond & here
    return cond


def _comm_start(comm, cin, cout, csem, grid):
    if comm is not None:
        @pl.when(_grid_edge(grid, False))
        def _():
            for cp in comm.make(cin, cout, csem):
                cp.start()


def _comm_finish(comm, cin, cout, csem, grid):
    if comm is not None:
        @pl.when(_grid_edge(grid, True))
        def _():
            for cp in comm.make(cin, cout, csem):
                cp.wait()


def gather_first_comm(shard):
    r, w = shard.shape

    def make(cin, cout, sems):
        (x_ref,), (out_ref,), (send_sems, recv_sems, local_sem) = cin, cout, sems
        x, y, c = _mesh_pos()
        mine = out_ref.at[4 * x + 2 * y + c]
        targets = [(x, y, 1 - c), (1 - x, y, c), (x, 1 - y, c), (1 - x, 1 - y, c)]
        return [_remote(x_ref, mine, send_sems, recv_sems, k, to) for k, to in enumerate(targets)] + [
            pltpu.make_async_copy(x_ref, mine, local_sem)]

    return _Comm([shard], [jax.ShapeDtypeStruct((N_DEV, r, w), shard.dtype)],
                 [pltpu.SemaphoreType.DMA((4,)), pltpu.SemaphoreType.DMA((4,)), pltpu.SemaphoreType.DMA], make)


def gather_pass_comm(full):
    def make(cin, cout, sems):
        (in_ref,), (out_ref,), (send_sems, recv_sems) = cin, cout, sems
        x, y, c = _mesh_pos()
        copies = []
        for k, (px, py) in enumerate([(1 - x, y), (x, 1 - y), (1 - x, 1 - y)]):
            idx = 4 * px + 2 * py + c
            copies.append(_remote(in_ref.at[idx], out_ref.at[idx], send_sems, recv_sems, k, (x, y, 1 - c)))
        return copies

    return _Comm([full], [jax.ShapeDtypeStruct(full.shape, full.dtype)],
                 [pltpu.SemaphoreType.DMA((3,)), pltpu.SemaphoreType.DMA((3,))], make, aliases={0: 0})


def scatter_sibling_comm(g):
    _, r, w = g.shape
    return _Comm([g], [jax.ShapeDtypeStruct((N_CHIP, r, w), g.dtype)],
                 [pltpu.SemaphoreType.DMA((N_CHIP,)), pltpu.SemaphoreType.DMA((N_CHIP,))],
                 lambda cin, cout, sems: _sibling_copies(cin[0], cout[0], sems[0], sems[1]))


def scatter_chips_comm(part):
    _, r, w = part.shape
    return _Comm([part], [jax.ShapeDtypeStruct((N_CHIP - 1, r, w), part.dtype)],
                 [pltpu.SemaphoreType.DMA((3,)), pltpu.SemaphoreType.DMA((3,))],
                 lambda cin, cout, sems: _chip_copies(cin[0], cout[0], sems[0], sems[1]))


def exchange_sibling(g):
    _, r, w = g.shape

    def body(g_ref, got_ref, send_sems, recv_sems):
        copies = _sibling_copies(g_ref, got_ref, send_sems, recv_sems)
        for cp in copies:
            cp.start()
        for cp in copies:
            cp.wait()

    return pl.pallas_call(
        body, name="exchange_sibling", out_shape=jax.ShapeDtypeStruct((N_CHIP, r, w), g.dtype),
        in_specs=[_ANY], out_specs=_ANY,
        scratch_shapes=[pltpu.SemaphoreType.DMA((N_CHIP,)), pltpu.SemaphoreType.DMA((N_CHIP,))],
    )(g)


def exchange_chips(part):
    _, r, w = part.shape

    def body(p_ref, land_ref, send_sems, recv_sems):
        copies = _chip_copies(p_ref, land_ref, send_sems, recv_sems)
        for cp in copies:
            cp.start()
        for cp in copies:
            cp.wait()

    return pl.pallas_call(
        body, name="exchange_chips", out_shape=jax.ShapeDtypeStruct((N_CHIP - 1, r, w), part.dtype),
        in_specs=[_ANY], out_specs=_ANY,
        scratch_shapes=[pltpu.SemaphoreType.DMA((3,)), pltpu.SemaphoreType.DMA((3,))],
    )(part)


def allreduce_stats(st_mix, st_prep, st_mlp, st_ple):
    def body(mix_ref, prep_ref, mlp_ref, ple_ref, out_ref, mine, gath, send_sems, recv_sems):
        x, y, c = _mesh_pos()
        me = 4 * x + 2 * y + c
        mine[...] = jnp.zeros_like(mine)
        mine[ST_G_MIX:ST_G_MIX + 1, :] = mix_ref[...]
        mine[ST_G_QA:ST_G_KN + 1, 0:256] = prep_ref[...]
        mine[ST_G_MLP:ST_G_MLP + 1, :] = mlp_ref[...]
        mine[ST_G_PLE:ST_LOSS + 1, :] = ple_ref[...]
        gath[me] = mine[...]
        copies = []
        for k in range(1, N_DEV):
            peer = (_flip(x, k & 4), _flip(y, k & 2), _flip(c, k & 1))
            copies.append(_remote(mine, gath.at[me], send_sems, recv_sems, k - 1, peer))
        for cp in copies:
            cp.start()
        for cp in copies:
            cp.wait()
        acc = gath[0]
        for d in range(1, N_DEV):
            acc = acc + gath[d]
        out_ref[...] = acc

    vm = pl.BlockSpec(memory_space=pltpu.VMEM)
    return pl.pallas_call(
        body, name="allreduce_stats", out_shape=jax.ShapeDtypeStruct((ST_ROWS, D_MODEL), F32),
        in_specs=[vm] * 4, out_specs=vm,
        scratch_shapes=[pltpu.VMEM((ST_ROWS, D_MODEL), F32), pltpu.VMEM((N_DEV, ST_ROWS, D_MODEL), F32),
                        pltpu.SemaphoreType.DMA((N_DEV - 1,)), pltpu.SemaphoreType.DMA((N_DEV - 1,))],
    )(st_mix, st_prep, st_mlp, st_ple)


def adamw_gains(stats, gains):
    c1 = 1.0 - ADAM_B1 ** ADAM_STEP
    c2 = 1.0 - ADAM_B2 ** ADAM_STEP
    n = len(gains)

    def body(st_ref, *refs):
        ins, outs = refs[:3 * n], refs[3 * n:]
        for i, (row, w, _, _) in enumerate(gains):
            width = w.shape[1]
            gv = st_ref[row:row + 1, 0:width]
            mn = ADAM_B1 * ins[3 * i + 1][...] + (1.0 - ADAM_B1) * gv
            vn = ADAM_B2 * ins[3 * i + 2][...] + (1.0 - ADAM_B2) * (gv * gv)
            outs[4 * i][...] = gv
            outs[4 * i + 1][...] = -ADAM_LR * ((mn / c1) / (jnp.sqrt(vn / c2) + ADAM_EPS) + ADAM_WD * ins[3 * i][...])
            outs[4 * i + 2][...] = mn
            outs[4 * i + 3][...] = vn

    vm = pl.BlockSpec(memory_space=pltpu.VMEM)
    flat = [a for (_, w, m, v) in gains for a in (w, m, v)]
    out_shape = tuple(jax.ShapeDtypeStruct(w.shape, F32) for (_, w, _, _) in gains for _ in range(4))
    res = pl.pallas_call(body, name="adamw_gains", out_shape=out_shape, in_specs=[vm] * (1 + 3 * n),
                         out_specs=tuple([vm] * (4 * n)))(stats, *flat)
    return [res[4 * i:4 * i + 4] for i in range(n)]


def _row_tile(r, cap=640):
    return max(d for d in range(16, min(r, cap) + 1, 16) if r % d == 0)


def add_pairs(g, got, core):
    n, r, w = got.shape
    tr = _row_tile(r)

    def body(c_ref, a_ref, b_ref, o_ref):
        o_ref[...] = (a_ref[...].astype(F32) + b_ref[...].astype(F32)).astype(o_ref.dtype)

    spec = pl.BlockSpec((1, tr, w), lambda i, j, c: (i, j, 0))
    return pl.pallas_call(
        body, name="add_pairs", out_shape=jax.ShapeDtypeStruct(got.shape, got.dtype),
        grid_spec=pltpu.PrefetchScalarGridSpec(
            num_scalar_prefetch=1, grid=(n, r // tr),
            in_specs=[pl.BlockSpec((1, tr, w), lambda i, j, c: (2 * i + c[0], j, 0)), spec], out_specs=spec),
        compiler_params=_params(("parallel", "parallel")),
    )(core, g, got)


def sum_chips(part, land, chip):
    _, r, w = part.shape
    tr = _row_tile(r)

    def body(c_ref, p_ref, l_ref, o_ref):
        acc = p_ref[0].astype(F32)
        for s in range(N_CHIP - 1):
            acc = acc + l_ref[s].astype(F32)
        o_ref[...] = acc

    return pl.pallas_call(
        body, name="sum_chips", out_shape=jax.ShapeDtypeStruct((r, w), F32),
        grid_spec=pltpu.PrefetchScalarGridSpec(
            num_scalar_prefetch=1, grid=(r // tr,),
            in_specs=[pl.BlockSpec((1, tr, w), lambda i, c: (c[0], i, 0)), pl.BlockSpec((N_CHIP - 1, tr, w), lambda i, c: (0, i, 0))],
            out_specs=pl.BlockSpec((tr, w), lambda i, c: (i, 0))),
        compiler_params=_params(("parallel",)),
    )(chip, part, land)


def norm_x(x, g_mix, tm, comm=None):
    t = x.shape[0]
    grid = (t // tm,)
    c_ins, c_in_specs, c_outs, c_sems, alias = _comm_parts(comm, 2, 1)

    def body(*refs):
        (x_ref, g_ref), cin, (h_ref,), cout, _, csem = _split_refs(refs, 2, 1, 0, comm)
        _comm_start(comm, cin, cout, csem, grid)
        xv = x_ref[...]
        h_ref[...] = (xv * _rstd(xv, D_MODEL) * g_ref[...]).astype(BF16)
        _comm_finish(comm, cin, cout, csem, grid)

    return pl.pallas_call(
        body, name="norm_x", grid=grid, out_shape=(jax.ShapeDtypeStruct((t, D_MODEL), BF16), *c_outs),
        in_specs=[_rows(tm, D_MODEL), _resident((1, D_MODEL)), *c_in_specs],
        out_specs=(_rows(tm, D_MODEL), *([_ANY] * len(c_outs))),
        scratch_shapes=c_sems, input_output_aliases=alias, compiler_params=_params(("arbitrary",)),
    )(x, g_mix, *c_ins)


def pack_late_weights(w_up, w_down, w_o, w_pg, small, comm=None):
    rows2 = sum(PACK_ROWS[n] for n in PACK_W2)
    rows3 = sum(PACK_ROWS[n] for n in PACK_W3)
    c_ins, c_in_specs, c_outs, c_sems, alias = _comm_parts(comm, 5, 2)
    grid = (1,)

    def body(*refs):
        (up_ref, dn_ref, o_ref, pg_ref, sm_ref), cin, (p2_ref, p3_ref), cout, _, csem = _split_refs(refs, 5, 2, 0, comm)
        _comm_start(comm, cin, cout, csem, grid)
        p2_ref[0:128, :] = o_ref[0].astype(BF16)
        p2_ref[128:256, :] = pg_ref[0].astype(BF16)
        p2_ref[256:rows2, :] = sm_ref[...]
        p3_ref[0:512, :] = up_ref[0].T.astype(BF16)
        p3_ref[512:1024, :] = dn_ref[0].astype(BF16)
        _comm_finish(comm, cin, cout, csem, grid)

    def whole(a):
        nd = a.ndim
        return pl.BlockSpec(a.shape, lambda i: (0,) * nd)

    args = (w_up, w_down, w_o, w_pg, small)
    return pl.pallas_call(
        body, name="pack_late_weights", grid=grid,
        out_shape=(jax.ShapeDtypeStruct((rows2, D_MODEL), BF16), jax.ShapeDtypeStruct((rows3, D_MODEL), BF16), *c_outs),
        in_specs=[*[whole(a) for a in args], *c_in_specs],
        out_specs=(pl.BlockSpec((rows2, D_MODEL), lambda i: (0, 0)), pl.BlockSpec((rows3, D_MODEL), lambda i: (0, 0)),
                   *([_ANY] * len(c_outs))),
        scratch_shapes=c_sems, input_output_aliases=alias, compiler_params=_params(("arbitrary",)),
    )(*args, *c_ins)


def in_proj(h, w_in_t, tm, comm=None):
    t = h.shape[0]
    nc = 512
    grid = (t // tm,)
    c_ins, c_in_specs, c_outs, c_sems, alias = _comm_parts(comm, 2, 1)

    def body(*refs):
        (h_ref, w_ref), cin, (z_ref,), cout, _, csem = _split_refs(refs, 2, 1, 0, comm)
        _comm_start(comm, cin, cout, csem, grid)
        hv = h_ref[...]
        for cidx in range(ZP // nc):
            z_ref[:, cidx * nc:(cidx + 1) * nc] = _dot_nt(hv, w_ref[cidx * nc:(cidx + 1) * nc, :])
        _comm_finish(comm, cin, cout, csem, grid)

    return pl.pallas_call(
        body, name="in_proj", grid=grid, out_shape=(jax.ShapeDtypeStruct((t, ZP), F32), *c_outs),
        in_specs=[_rows(tm, D_MODEL), _resident((ZP, D_MODEL)), *c_in_specs],
        out_specs=(_rows(tm, ZP), *([_ANY] * len(c_outs))),
        scratch_shapes=c_sems, input_output_aliases=alias, compiler_params=_params(("arbitrary",)),
    )(h, w_in_t, *c_ins)


def attn_prep(zp, tabs, g_qa, g_kva, g_qn, g_kn, w_qb_t, w_kvb_t, tm, s_len, comm=None):
    t = zp.shape[0]
    nsb = s_len // tm
    scale_a = (QK_NOPE + QK_ROPE) ** -0.5
    scale_b = HD_B ** -0.5

    grid = (t // tm,)
    c_ins, c_in_specs, c_outs, c_sems, alias = _comm_parts(comm, 13, 8)

    def body(*refs):
        ((qb_ref, qlat_ref, kb_ref, vb_ref, ckv_ref, kpe_ref, tab_ref, gqa_ref, gkva_ref, gqn_ref, gkn_ref, wqb_ref,
          wkvb_ref), cin, (qa_o, ka_o, va_o, qb_o, kb_o, vb_o, cq_o, ckvn_o), cout, _, csem) = _split_refs(refs, 13, 8, 0, comm)
        _comm_start(comm, cin, cout, csem, grid)
        ca, s1a, s2a = tab_ref[0], tab_ref[1], tab_ref[2]
        ck = tab_ref[3]
        cb, s1b, s2b = tab_ref[4], tab_ref[5], tab_ref[6]
        ql = qlat_ref[...]
        cq = (ql * _rstd(ql, Q_LORA) * gqa_ref[...]).astype(BF16)
        cq_o[...] = cq
        qa = _dot_nt(cq, wqb_ref[...])
        slabs = [slice(h * HP, (h + 1) * HP) for h in range(H_A)]
        qa_o[...] = jnp.concatenate(
            [(_rope_fwd(qa[:, sl], ca, s1a, s2a) * scale_a).astype(BF16) for sl in slabs], axis=1)
        cr = ckv_ref[...]
        ckv = (cr * _rstd(cr, KV_LORA) * gkva_ref[...]).astype(BF16)
        ckvn_o[...] = ckv
        kva = _dot_nt(ckv, wkvb_ref[...])
        kpe = _rope_fwd(kpe_ref[...], ck, s1a, s2a)
        ka_o[...] = jnp.concatenate([(kva[:, sl] + kpe).astype(BF16) for sl in slabs], axis=1)
        va_o[...] = kva[:, H_A * HP:].astype(BF16)
        gqn, gkn = gqn_ref[...], gkn_ref[...]

        def norm_rope(ref, sl, g, scale):
            xs = ref[:, sl]
            y = _rope_fwd(xs * _rstd(xs, HD_B) * g, cb, s1b, s2b)
            return (y if scale is None else y * scale).astype(BF16)

        qb_o[...] = jnp.concatenate([norm_rope(qb_ref, sl, gqn, scale_b) for sl in slabs], axis=1)
        kb_o[...] = jnp.concatenate([norm_rope(kb_ref, sl, gkn, None) for sl in slabs[:KV_B]], axis=1)
        vb_o[...] = vb_ref[...].astype(BF16)
        _comm_finish(comm, cin, cout, csem, grid)

    def o(width):
        return jax.ShapeDtypeStruct((t, width), BF16)

    return pl.pallas_call(
        body, name="attn_prep", grid=grid,
        out_shape=(o(H_A * HP), o(H_A * HP), o(H_A * HP), o(H_B * HP), o(KV_B * HP), o(KV_B * HP), o(Q_LORA), o(KV_LORA),
                   *c_outs),
        in_specs=[_rows(tm, 1024, 2), _rows(tm, 256, 12), _rows(tm, 256, 13), _rows(tm, 256, 14),
                  _rows(tm, 128, 30), _rows(tm, 128, 31),
                  pl.BlockSpec((7, tm, HP), lambda i: (0, i % nsb, 0)),
                  _resident((1, Q_LORA)), _resident((1, KV_LORA)), _resident((1, HP)), _resident((1, HP)),
                  _resident((H_A * HP, Q_LORA)), _resident((2 * H_A * HP, KV_LORA)), *c_in_specs],
        out_specs=(_rows(tm, H_A * HP), _rows(tm, H_A * HP), _rows(tm, H_A * HP), _rows(tm, H_B * HP),
                   _rows(tm, KV_B * HP), _rows(tm, KV_B * HP), _rows(tm, Q_LORA), _rows(tm, KV_LORA), *([_ANY] * len(c_outs))),
        scratch_shapes=c_sems, input_output_aliases=alias, compiler_params=_params(("arbitrary",)),
    )(zp, zp, zp, zp, zp, zp, tabs, g_qa, g_kva, g_qn, g_kn, w_qb_t, w_kvb_t, *c_ins)


def attn_fwd(q, k, v, n_b, s_len, tq, name, comm=None):
    t = q.shape[0]
    n_h, n_hk = q.shape[1] // HP, k.shape[1] // HP
    grp = n_h // n_hk
    nq = s_len // tq
    sub = min(tq, 256)
    grid = (n_b, n_h, nq)
    c_ins, c_in_specs, c_outs, c_sems, alias = _comm_parts(comm, 3, 2)

    def body(*refs):
        (q_ref, k_ref, v_ref), cin, (o_ref, lse_ref), cout, _, csem = _split_refs(refs, 3, 2, 0, comm)
        _comm_start(comm, cin, cout, csem, grid)
        kv, vv = k_ref[...], v_ref[...]
        for r in range(tq // sub):
            rows = slice(r * sub, (r + 1) * sub)
            s = _dot_nt(q_ref[rows, :], kv)
            m = jnp.max(s, axis=-1, keepdims=True)
            p = jnp.exp(s - m)
            l = jnp.sum(p, axis=-1, keepdims=True)
            o_ref[rows, :] = (_dot_nn(p.astype(BF16), vv) * (1.0 / l)).astype(o_ref.dtype)
            lse_ref[rows, :] = jnp.broadcast_to(m + jnp.log(l), (sub, HP))
        _comm_finish(comm, cin, cout, csem, grid)

    qspec = pl.BlockSpec((tq, HP), lambda b, h, i: (b * nq + i, h))
    kspec = pl.BlockSpec((s_len, HP), lambda b, h, i: (b, h // grp))
    return pl.pallas_call(
        body, name=name, grid=grid,
        out_shape=(jax.ShapeDtypeStruct((t, n_h * HP), BF16), jax.ShapeDtypeStruct((t, n_h * HP), F32), *c_outs),
        in_specs=[qspec, kspec, kspec, *c_in_specs], out_specs=(qspec, qspec, *([_ANY] * len(c_outs))),
        scratch_shapes=c_sems, input_output_aliases=alias,
        compiler_params=_params(("arbitrary", "arbitrary", "arbitrary")),
    )(q, k, v, *c_ins)


def merge_fwd(oa, ob, zp, x, w_oa_t, w_ob_t, wpack, off, tm):
    t = x.shape[0]

    def body(oa_ref, ob_ref, ga_ref, gb_ref, x_ref, woa_ref, wob_ref, wo_ref, x1_o, mg_o, y_o):
        ya = _dot_nt(oa_ref[...], woa_ref[...])
        yb = _dot_nt(ob_ref[...], wob_ref[...])
        y_o[:, 0:D_MODEL] = ya.astype(BF16)
        y_o[:, D_MODEL:2 * D_MODEL] = yb.astype(BF16)
        merged = (jax.nn.sigmoid(ga_ref[...]) * ya + jax.nn.sigmoid(gb_ref[...]) * yb).astype(BF16)
        mg_o[...] = merged
        x1_o[...] = x_ref[...] + _dot_nn(merged, _wrows(wo_ref, 0, D_MODEL))

    return pl.pallas_call(
        body, name="merge_fwd", grid=(t // tm,),
        out_shape=(jax.ShapeDtypeStruct((t, D_MODEL), F32), jax.ShapeDtypeStruct((t, D_MODEL), BF16),
                   jax.ShapeDtypeStruct((t, 2 * D_MODEL), BF16)),
        in_specs=[_rows(tm, H_A * HP), _rows(tm, H_B * HP), _rows(tm, 1024, 0), _rows(tm, 1024, 1), _rows(tm, D_MODEL),
                  _resident((D_MODEL, H_A * HP)), _resident((D_MODEL, H_B * HP)), _packed_weight(128, off["w_o"])],
        out_specs=(_rows(tm, D_MODEL), _rows(tm, D_MODEL), _rows(tm, 2 * D_MODEL)), compiler_params=_params(("parallel",)),
    )(oa, ob, zp, zp, x, w_oa_t, w_ob_t, wpack)


def mlp_fwd(x1, g_mlp, wpack, off, tm):
    t = x1.shape[0]
    fc = 1024

    def body(x_ref, g_ref, wup_ref, wdn_ref, x2_o, u_o):
        xv = x_ref[...]
        h2 = (xv * _rstd(xv, D_MODEL) * g_ref[...]).astype(BF16)
        acc = xv
        for cidx in range(D_FF // fc):
            sl = slice(cidx * fc, (cidx + 1) * fc)
            u = jnp.maximum(_dot_nt(h2, _wrows(wup_ref, cidx * fc, fc)), 0.0)
            u_o[:, sl] = u.astype(BF16)
            acc = acc + _dot_nn((u * u).astype(BF16), _wrows(wdn_ref, cidx * fc, fc))
        x2_o[...] = acc

    return pl.pallas_call(
        body, name="mlp_fwd", grid=(t // tm,),
        out_shape=(jax.ShapeDtypeStruct((t, D_MODEL), F32), jax.ShapeDtypeStruct((t, D_FF), BF16)),
        in_specs=[_rows(tm, D_MODEL), _resident((1, D_MODEL)), _packed_weight(512, off["w_up"]), _packed_weight(512, off["w_down"])],
        out_specs=(_rows(tm, D_MODEL), _rows(tm, D_FF)), compiler_params=_params(("parallel",)),
    )(x1, g_mlp, wpack, wpack)


def ple_loss_bwd(x2, p, tgt, g_ple, g_final, wpack, off, w_ple_t, tm):
    t = x2.shape[0]
    inv_d = 1.0 / D_MODEL

    def body(x2_ref, p_ref, tg_ref, gp_ref, gf_ref, wpg_ref, wple_ref, dx2_o, dt_o, h3_o, dpe_o, st_o, dx2b_o):
        @pl.when(pl.program_id(0) == 0)
        def _():
            st_o[...] = jnp.zeros_like(st_o)

        x2v = x2_ref[...]
        gp, gf = gp_ref[...], gf_ref[...]
        w_pg = _wrows(wpg_ref, 0, D_MODEL)
        r2 = _rstd(x2v, D_MODEL)
        xh2 = x2v * r2
        h3 = (xh2 * gp).astype(BF16)
        h3_o[...] = h3
        gate = jax.nn.sigmoid(_dot_nn(h3, w_pg))
        pe = _dot_nt(p_ref[...].astype(BF16), wple_ref[...])
        x3 = x2v + gate * pe
        r3 = _rstd(x3, D_MODEL)
        xh3 = x3 * r3
        err = xh3 * gf - tg_ref[...]
        dy = err * inv_d
        dx3 = _rms_bwd(dy, xh3, r3, gf, D_MODEL)
        dpe_o[...] = (dx3 * gate).astype(BF16)
        dt = (dx3 * pe * gate * (1.0 - gate)).astype(BF16)
        dt_o[...] = dt
        dh3 = _dot_nt(dt, w_pg)
        dx2 = dx3 + _rms_bwd(dh3, xh2, r2, gp, D_MODEL)
        dx2_o[...] = dx2
        dx2b_o[...] = dx2.astype(BF16)
        st_o[0:1, :] += _colsum(dh3 * xh2)
        st_o[1:2, :] += _colsum(dy * xh3)
        st_o[2:3, :] += _colsum(err * err) * (0.5 * inv_d)

    bf = jax.ShapeDtypeStruct((t, D_MODEL), BF16)
    return pl.pallas_call(
        body, name="ple_loss_bwd", grid=(t // tm,),
        out_shape=(jax.ShapeDtypeStruct((t, D_MODEL), F32), bf, bf, bf, jax.ShapeDtypeStruct((3, D_MODEL), F32), bf),
        in_specs=[_rows(tm, D_MODEL), _rows(tm, PLE_DIM), _rows(tm, D_MODEL), _resident((1, D_MODEL)), _resident((1, D_MODEL)),
                  _packed_weight(128, off["w_ple_gate"]), _resident((D_MODEL, PLE_DIM))],
        out_specs=(_rows(tm, D_MODEL), _rows(tm, D_MODEL), _rows(tm, D_MODEL), _rows(tm, D_MODEL),
                   pl.BlockSpec((3, D_MODEL), lambda i: (0, 0)), _rows(tm, D_MODEL)),
        compiler_params=_params(("arbitrary",)),
    )(x2, p, tgt, g_ple, g_final, wpack, w_ple_t)


def mlp_bwd(dx2, x1, u, g_mlp, wpack, off, tm):
    t = x1.shape[0]
    fc = 1024

    def body(dx2_ref, x1_ref, u_ref, g_ref, wup_ref, wdn_ref, dx1_o, da_o, h2_o, st_o, dx1b_o):
        @pl.when(pl.program_id(0) == 0)
        def _():
            st_o[...] = jnp.zeros_like(st_o)

        d2 = dx2_ref[...]
        d2b = d2.astype(BF16)
        dh2 = jnp.zeros((tm, D_MODEL), F32)
        for cidx in range(D_FF // fc):
            sl = slice(cidx * fc, (cidx + 1) * fc)
            da = (_dot_nt(d2b, _wrows(wdn_ref, cidx * fc, fc)) * (2.0 * u_ref[:, sl].astype(F32))).astype(BF16)
            da_o[:, sl] = da
            dh2 = dh2 + _dot_nn(da, _wrows(wup_ref, cidx * fc, fc))
        xv = x1_ref[...]
        g = g_ref[...]
        r1 = _rstd(xv, D_MODEL)
        xh1 = xv * r1
        h2_o[...] = (xh1 * g).astype(BF16)
        st_o[...] += _colsum(dh2 * xh1)
        dx1 = d2 + _rms_bwd(dh2, xh1, r1, g, D_MODEL)
        dx1_o[...] = dx1
        dx1b_o[...] = dx1.astype(BF16)

    return pl.pallas_call(
        body, name="mlp_bwd", grid=(t // tm,),
        out_shape=(jax.ShapeDtypeStruct((t, D_MODEL), F32), jax.ShapeDtypeStruct((t, D_FF), BF16),
                   jax.ShapeDtypeStruct((t, D_MODEL), BF16), jax.ShapeDtypeStruct((1, D_MODEL), F32),
                   jax.ShapeDtypeStruct((t, D_MODEL), BF16)),
        in_specs=[_rows(tm, D_MODEL), _rows(tm, D_MODEL), _rows(tm, D_FF), _resident((1, D_MODEL)),
                  _packed_weight(512, off["w_up"]), _packed_weight(512, off["w_down"])],
        out_specs=(_rows(tm, D_MODEL), _rows(tm, D_FF), _rows(tm, D_MODEL), pl.BlockSpec((1, D_MODEL), lambda i: (0, 0)),
                   _rows(tm, D_MODEL)),
        compiler_params=_params(("arbitrary",)),
    )(dx2, x1, u, g_mlp, wpack, wpack)


def merge_bwd(dx1b, yab, zp, w_oa_t, w_ob_t, wpack, off, tm, comm=None):
    t = dx1b.shape[0]
    grid = (t // tm,)
    c_ins, c_in_specs, c_outs, c_sems, alias = _comm_parts(comm, 7, 5)

    def body(*refs):
        ((dx1_ref, y_ref, ga_ref, gb_ref, woa_ref, wob_ref, wo_ref), cin,
         (doa_o, dob_o, dg_o, dya_o, dyb_o), cout, _, csem) = _split_refs(refs, 7, 5, 0, comm)
        _comm_start(comm, cin, cout, csem, grid)
        dm = _dot_nt(dx1_ref[...], _wrows(wo_ref, 0, D_MODEL))
        for g_ref, w_ref, do_o, dy_o, col in ((ga_ref, woa_ref, doa_o, dya_o, 0), (gb_ref, wob_ref, dob_o, dyb_o, 1)):
            cols = slice(col * D_MODEL, (col + 1) * D_MODEL)
            sg = jax.nn.sigmoid(g_ref[...])
            dyv = (dm * sg).astype(BF16)
            dy_o[...] = dyv
            dg_o[:, cols] = (dm * y_ref[:, cols].astype(F32) * sg * (1.0 - sg)).astype(BF16)
            do_o[...] = _dot_nn(dyv, w_ref[...]).astype(BF16)
        _comm_finish(comm, cin, cout, csem, grid)

    bf = jax.ShapeDtypeStruct((t, D_MODEL), BF16)
    return pl.pallas_call(
        body, name="merge_bwd", grid=grid,
        out_shape=(bf, bf, jax.ShapeDtypeStruct((t, ZP), BF16), bf, bf, *c_outs),
        in_specs=[_rows(tm, D_MODEL), _rows(tm, 2 * D_MODEL), _rows(tm, 1024, 0), _rows(tm, 1024, 1),
                  _resident((D_MODEL, H_A * HP)), _resident((D_MODEL, H_B * HP)), _packed_weight(128, off["w_o"]), *c_in_specs],
        out_specs=(_rows(tm, D_MODEL), _rows(tm, D_MODEL), _rows(tm, 2 * D_MODEL), _rows(tm, D_MODEL), _rows(tm, D_MODEL),
                   *([_ANY] * len(c_outs))),
        scratch_shapes=c_sems, input_output_aliases=alias,
        compiler_params=_params(("arbitrary",)),
    )(dx1b, yab, zp, zp, w_oa_t, w_ob_t, wpack, *c_ins)


def attn_bwd(q, k, v, do, o, lse, n_b, s_len, tq, name, comm=None):
    t = q.shape[0]
    n_h, n_hk = q.shape[1] // HP, k.shape[1] // HP
    grp = n_h // n_hk
    nq = s_len // tq
    sub = min(tq, 256)
    grid = (n_b, n_hk, grp, nq)
    c_ins, c_in_specs, c_outs, c_sems, alias = _comm_parts(comm, 6, 3)

    def body(*refs):
        ((q_ref, k_ref, v_ref, do_ref, o_ref, lse_ref), cin, (dq_o, dk_o, dv_o), cout, (p_s, ds_s, dk_acc, dv_acc),
         csem) = _split_refs(refs, 6, 3, 4, comm)
        _comm_start(comm, cin, cout, csem, grid)

        @pl.when((pl.program_id(2) == 0) & (pl.program_id(3) == 0))
        def _():
            dk_acc[...] = jnp.zeros_like(dk_acc)
            dv_acc[...] = jnp.zeros_like(dv_acc)

        kv, vv = k_ref[...], v_ref[...]
        for r in range(tq // sub):
            rows = slice(r * sub, (r + 1) * sub)
            qv, dov = q_ref[rows, :], do_ref[rows, :]
            delta = jnp.sum(dov.astype(F32) * o_ref[rows, :].astype(F32), axis=-1, keepdims=True)
            p = jnp.exp(_dot_nt(qv, kv) - lse_ref[rows, 0:1])
            ds = (p * (_dot_nt(dov, vv) - delta)).astype(BF16)
            p_s[rows, :] = p.astype(BF16)
            ds_s[rows, :] = ds
            dq_o[rows, :] = _dot_nn(ds, kv).astype(dq_o.dtype)
        dk_acc[...] += _dot_tn(ds_s[...], q_ref[...])
        dv_acc[...] += _dot_tn(p_s[...], do_ref[...])

        @pl.when((pl.program_id(2) == grp - 1) & (pl.program_id(3) == nq - 1))
        def _():
            dk_o[...] = dk_acc[...].astype(dk_o.dtype)
            dv_o[...] = dv_acc[...].astype(dv_o.dtype)

        _comm_finish(comm, cin, cout, csem, grid)

    qspec = pl.BlockSpec((tq, HP), lambda b, hk, g, i: (b * nq + i, hk * grp + g))
    kspec = pl.BlockSpec((s_len, HP), lambda b, hk, g, i: (b, hk))
    return pl.pallas_call(
        body, name=name, grid=grid,
        out_shape=(jax.ShapeDtypeStruct((t, n_h * HP), BF16), jax.ShapeDtypeStruct((t, n_hk * HP), BF16),
                   jax.ShapeDtypeStruct((t, n_hk * HP), BF16), *c_outs),
        in_specs=[qspec, kspec, kspec, qspec, qspec, qspec, *c_in_specs],
        out_specs=(qspec, kspec, kspec, *([_ANY] * len(c_outs))),
        scratch_shapes=[pltpu.VMEM((tq, s_len), BF16), pltpu.VMEM((tq, s_len), BF16),
                        pltpu.VMEM((s_len, HP), F32), pltpu.VMEM((s_len, HP), F32), *c_sems],
        input_output_aliases=alias,
        compiler_params=_params(("arbitrary", "arbitrary", "arbitrary", "arbitrary")),
    )(q, k, v, do, o, lse, *c_ins)


def prep_bwd(dqa, dka, dva, dqb, dkb, dvb, zp, dz, tabs, g_qa, g_kva, g_qn, g_kn, w_qb_t, w_kvb_t, tm, s_len):
    t = zp.shape[0]
    nsb = s_len // tm
    scale_a = (QK_NOPE + QK_ROPE) ** -0.5
    scale_b = HD_B ** -0.5

    def body(dqa_ref, dka_ref, dva_ref, dqb_ref, dkb_ref, dvb_ref, qb_ref, qlat_ref, kb_ref, ckv_ref, tab_ref,
             gqa_ref, gkva_ref, gqn_ref, gkn_ref, wqb_ref, wkvb_ref, _, dz_o, dqap_o, dkva_o, st_o):
        dzq_o, dsm_o = dz_o.at[:, 0:1024], dz_o.at[:, 1024:2048]

        @pl.when(pl.program_id(0) == 0)
        def _():
            st_o[...] = jnp.zeros_like(st_o)

        ca, s1a, s2a = tab_ref[0], tab_ref[1], tab_ref[2]
        ck = tab_ref[3]
        cb, s1b, s2b = tab_ref[4], tab_ref[5], tab_ref[6]
        for h in range(H_A):
            sl = slice(h * HP, (h + 1) * HP)
            dqap_o[:, sl] = _rope_bwd(dqa_ref[:, sl].astype(F32) * scale_a, ca, s1a, s2a).astype(BF16)
        dcq = _dot_nn(dqap_o[...], wqb_ref[...])
        ql = qlat_ref[...]
        rq = _rstd(ql, Q_LORA)
        xh = ql * rq
        gqa = gqa_ref[...]
        st_o[0:1, :] += _colsum(dcq * xh)
        dsm_o[:, 0:256] = _rms_bwd(dcq, xh, rq, gqa, Q_LORA).astype(BF16)
        dkpe = jnp.zeros((tm, HP), F32)
        for h in range(H_A):
            sl = slice(h * HP, (h + 1) * HP)
            dk = dka_ref[:, sl]
            dkpe = dkpe + dk.astype(F32)
            dkva_o[:, sl] = dk.astype(BF16)
        dkva_o[:, H_A * HP:] = dva_ref[...].astype(BF16)
        dsm_o[:, 896:1024] = _rope_bwd(dkpe, ck, s1a, s2a).astype(BF16)
        dckv = _dot_nn(dkva_o[...], wkvb_ref[...])
        cr = ckv_ref[...]
        rk = _rstd(cr, KV_LORA)
        xh = cr * rk
        st_o[1:2, 0:128] += _colsum(dckv * xh)
        dsm_o[:, 768:896] = _rms_bwd(dckv, xh, rk, gkva_ref[...], KV_LORA).astype(BF16)
        gqn, gkn = gqn_ref[...], gkn_ref[...]
        dgq = jnp.zeros((1, HP), F32)
        for h in range(H_B):
            sl = slice(h * HP, (h + 1) * HP)
            dy = _rope_bwd(dqb_ref[:, sl].astype(F32) * scale_b, cb, s1b, s2b)
            xs = qb_ref[:, sl]
            r = _rstd(xs, HD_B)
            xh = xs * r
            dgq = dgq + _colsum(dy * xh)
            dzq_o[:, sl] = _rms_bwd(dy, xh, r, gqn, HD_B).astype(BF16)
        st_o[2:3, 0:128] += dgq
        dgk = jnp.zeros((1, HP), F32)
        for h in range(KV_B):
            sl = slice(h * HP, (h + 1) * HP)
            dy = _rope_bwd(dkb_ref[:, sl].astype(F32), cb, s1b, s2b)
            xs = kb_ref[:, sl]
            r = _rstd(xs, HD_B)
            xh = xs * r
            dgk = dgk + _colsum(dy * xh)
            dsm_o[:, 256 + h * HP:256 + (h + 1) * HP] = _rms_bwd(dy, xh, r, gkn, HD_B).astype(BF16)
        st_o[3:4, 0:128] += dgk
        dsm_o[:, 512:768] = dvb_ref[...].astype(BF16)

    return pl.pallas_call(
        body, name="prep_bwd", grid=(t // tm,),
        out_shape=(jax.ShapeDtypeStruct((t, ZP), BF16), jax.ShapeDtypeStruct((t, 1024), BF16),
                   jax.ShapeDtypeStruct((t, 2048), BF16), jax.ShapeDtypeStruct((4, 256), F32)),
        in_specs=[_rows(tm, 1024), _rows(tm, 1024), _rows(tm, 1024), _rows(tm, 1024), _rows(tm, 256), _rows(tm, 256),
                  _rows(tm, 1024, 2), _rows(tm, 256, 12), _rows(tm, 256, 13), _rows(tm, 128, 30),
                  pl.BlockSpec((7, tm, HP), lambda i: (0, i % nsb, 0)),
                  _resident((1, Q_LORA)), _resident((1, KV_LORA)), _resident((1, HP)), _resident((1, HP)),
                  _resident((H_A * HP, Q_LORA)), _resident((2 * H_A * HP, KV_LORA)), _ANY],
        out_specs=(_rows(tm, 2048, 1), _rows(tm, 1024), _rows(tm, 2048), pl.BlockSpec((4, 256), lambda i: (0, 0))),
        input_output_aliases={17: 0}, compiler_params=_params(("arbitrary",)),
    )(dqa, dka, dva, dqb, dkb, dvb, zp, zp, zp, zp, tabs, g_qa, g_kva, g_qn, g_kn, w_qb_t, w_kvb_t, dz)


def in_bwd(dz, x, dx1, g_mix, w_in_t, tm, comm=None):
    t = x.shape[0]
    grid = (t // tm,)
    c_ins, c_in_specs, c_outs, c_sems, alias = _comm_parts(comm, 5, 2)

    def body(*refs):
        (dz_ref, x_ref, dx1_ref, g_ref, w_ref), cin, (dx_o, st_o), cout, _, csem = _split_refs(refs, 5, 2, 0, comm)
        _comm_start(comm, cin, cout, csem, grid)

        @pl.when(pl.program_id(0) == 0)
        def _():
            st_o[...] = jnp.zeros_like(st_o)

        dh = _dot_nn(dz_ref[...], w_ref[...])
        xv = x_ref[...]
        g = g_ref[...]
        r = _rstd(xv, D_MODEL)
        xh = xv * r
        st_o[...] += _colsum(dh * xh)
        dx_o[...] = dx1_ref[...] + _rms_bwd(dh, xh, r, g, D_MODEL)
        _comm_finish(comm, cin, cout, csem, grid)

    return pl.pallas_call(
        body, name="in_bwd", grid=grid,
        out_shape=(jax.ShapeDtypeStruct((t, D_MODEL), F32), jax.ShapeDtypeStruct((1, D_MODEL), F32), *c_outs),
        in_specs=[_rows(tm, ZP), _rows(tm, D_MODEL), _rows(tm, D_MODEL),
                  _resident((1, D_MODEL)), _resident((ZP, D_MODEL)), *c_in_specs],
        out_specs=(_rows(tm, D_MODEL), pl.BlockSpec((1, D_MODEL), lambda i: (0, 0)), *([_ANY] * len(c_outs))),
        scratch_shapes=c_sems, input_output_aliases=alias,
        compiler_params=_params(("arbitrary",)),
    )(dz, x, dx1, g_mix, w_in_t, *c_ins)


def matmul_tn(a, b, name, square_a=False):
    t, m = a.shape
    n = b.shape[1]
    bm = min(m, 512)
    tk = min(t, 4096)

    def body(a_ref, b_ref, o_ref):
        @pl.when(pl.program_id(1) == 0)
        def _():
            o_ref[...] = jnp.zeros_like(o_ref)

        av = a_ref[...]
        if square_a:
            av = (av.astype(F32) * av.astype(F32))
        o_ref[...] += _dot_tn(av.astype(BF16), b_ref[...].astype(BF16))

    return pl.pallas_call(
        body, name=name, grid=(m // bm, t // tk), out_shape=jax.ShapeDtypeStruct((m, n), F32),
        in_specs=[pl.BlockSpec((tk, bm), lambda i, kk: (kk, i)), pl.BlockSpec((tk, n), lambda i, kk: (kk, 0))],
        out_specs=pl.BlockSpec((bm, n), lambda i, kk: (i, 0)),
        compiler_params=_params(("parallel", "arbitrary")),
    )(a, b)


def matmul_tn_packed(a, b, name, rows, row_off, total_rows, buf=None, square_a=False):
    t, m = a.shape
    n = b.shape[1]
    pd = max(1, 512 // rows)
    bm = pd * rows
    tk = min(t, 4096)
    nk = t // tk

    def body(a_ref, b_ref, *rest):
        o_ref, acc = rest[-2], rest[-1]

        @pl.when(pl.program_id(1) == 0)
        def _():
            acc[...] = jnp.zeros_like(acc)

        av = a_ref[...]
        if square_a:
            av = (av.astype(F32) * av.astype(F32))
        acc[...] += _dot_tn(av.astype(BF16), b_ref[...].astype(BF16))

        @pl.when(pl.program_id(1) == nk - 1)
        def _():
            o_ref[...] = acc[...].reshape(pd, rows, n).astype(o_ref.dtype)

    in_specs = [pl.BlockSpec((tk, bm), lambda i, kk: (kk, i)), pl.BlockSpec((tk, n), lambda i, kk: (kk, 0))]
    args = [a, b]
    if buf is not None:
        in_specs.append(_ANY)
        args.append(buf)
    return pl.pallas_call(
        body, name=name, grid=(m // bm, nk), out_shape=jax.ShapeDtypeStruct((N_DEV, total_rows, n), BF16),
        in_specs=in_specs, out_specs=pl.BlockSpec((pd, rows, n), lambda i, kk: (i, row_off // rows, 0)),
        scratch_shapes=[pltpu.VMEM((bm, n), F32)], input_output_aliases={2: 0} if buf is not None else {},
        compiler_params=_params(("parallel", "arbitrary")),
    )(*args)


def adamw(w, g, m, v, name, g_transposed=False):
    _, r, c = w.shape
    tr = 256 if (not g_transposed and r > 256 and r % 256 == 0) else r
    c1 = 1.0 - ADAM_B1 ** ADAM_STEP
    c2 = 1.0 - ADAM_B2 ** ADAM_STEP

    def body(w_ref, g_ref, m_ref, v_ref, g_o, d_o, m_o, v_o):
        gv = g_ref[...].T if g_transposed else g_ref[...]
        mn = ADAM_B1 * m_ref[0] + (1.0 - ADAM_B1) * gv
        vn = ADAM_B2 * v_ref[0] + (1.0 - ADAM_B2) * (gv * gv)
        g_o[0] = gv
        m_o[0] = mn
        v_o[0] = vn
        d_o[0] = -ADAM_LR * ((mn / c1) / (jnp.sqrt(vn / c2) + ADAM_EPS) + ADAM_WD * w_ref[0])

    spec = pl.BlockSpec((1, tr, c), lambda i: (0, i, 0))
    gspec = pl.BlockSpec((c, r), lambda i: (0, 0)) if g_transposed else pl.BlockSpec((tr, c), lambda i: (i, 0))
    shp = jax.ShapeDtypeStruct((1, r, c), F32)
    return pl.pallas_call(
        body, name=name, grid=(r // tr,), out_shape=(shp,) * 4, in_specs=[spec, gspec, spec, spec], out_specs=(spec,) * 4,
        compiler_params=_params(("parallel",)),
    )(w, g, m, v)


def _rope_tables(s_len):
    def angles(pos, dim):
        inv = np.float32(ROPE_THETA) ** (-np.arange(0, dim, 2, dtype=np.float32) / np.float32(dim))
        return pos.astype(np.float32)[:, None] * inv[None, :]

    tpos = np.arange(s_len)
    a1 = angles(tpos, QK_ROPE)
    ar = angles(tpos // GRID_W, HD_B // 2)
    ac = angles(tpos % GRID_W, HD_B // 2)
    z16 = np.zeros((s_len, 16), np.float32)
    z32 = np.zeros((s_len, 32), np.float32)
    z64 = np.zeros((s_len, 64), np.float32)
    one64 = np.ones((s_len, 64), np.float32)
    c1, s1 = np.cos(a1), np.sin(a1)
    ca = np.concatenate([one64, c1, c1, z32], axis=1)
    ck = np.concatenate([z64, c1, c1, z32], axis=1)
    s1a = np.concatenate([z64, -s1, z16, z32], axis=1)
    s2a = np.concatenate([z64, z16, s1, z32], axis=1)
    cr, sr, cc, sc = np.cos(ar), np.sin(ar), np.cos(ac), np.sin(ac)
    cb = np.concatenate([cr, cr, cc, cc, z64], axis=1)
    s1b = np.concatenate([-sr, z16, -sc, z16, z64], axis=1)
    s2b = np.concatenate([z16, sr, z16, sc, z64], axis=1)
    return jnp.asarray(np.stack([ca, s1a, s2a, ck, cb, s1b, s2b]).astype(np.float32))


def _pad_heads(a, n_heads, axis):
    shp = a.shape
    a = a.reshape(shp[:axis] + (n_heads, shp[axis] // n_heads) + shp[axis + 1:])
    pad = [(0, 0)] * a.ndim
    pad[axis + 1] = (0, HP - a.shape[axis + 1])
    a = jnp.pad(a, pad)
    return a.reshape(shp[:axis] + (n_heads * HP,) + shp[axis + 1:])


def _unpad_heads(a, n_heads, width, axis):
    shp = a.shape
    a = a.reshape(shp[:axis] + (n_heads, HP) + shp[axis + 1:])
    a = lax.slice_in_dim(a, 0, width, axis=axis + 1)
    return a.reshape(shp[:axis] + (n_heads * width,) + shp[axis + 1:])


def _pack_rows(blocks, names):
    parts = []
    for name in names:
        b = blocks[name]
        padr = PACK_ROWS[name] - b.shape[-2]
        if padr:
            b = jnp.pad(b, [(0, 0)] * (b.ndim - 2) + [(0, padr), (0, 0)])
        parts.append(b)
    return jnp.concatenate(parts, axis=parts[0].ndim - 2)


def _expand_w_in(wt):
    z64 = jnp.zeros((64, D_MODEL), wt.dtype)
    z32 = jnp.zeros((32, D_MODEL), wt.dtype)
    return jnp.concatenate([
        wt[1184:2208], wt[2208:3232], _pad_heads(wt[416:928], H_B, 0), wt[0:256],
        _pad_heads(wt[928:1056], KV_B, 0), _pad_heads(wt[1056:1184], KV_B, 0), wt[256:384],
        z64, wt[384:416], z32], axis=0)


def _collapse_w_in(dw):
    dg, dq, ds = dw[0:2048], dw[2048:3072], dw[3072:4096]
    return jnp.concatenate([
        ds[0:256], ds[768:896], ds[960:992], _unpad_heads(dq, H_B, HD_B, 0), _unpad_heads(ds[256:512], KV_B, HD_B, 0),
        _unpad_heads(ds[512:768], KV_B, HD_B, 0), dg], axis=0)


def kernel(x, p, g_mix, w_in, g_qa, w_qb, g_kva, w_kvb, g_qn, g_kn, w_oa, w_ob, w_o, g_mlp, w_up, w_down, g_ple, w_ple_gate, w_ple, g_final, loss_target, m_g_mix, m_w_in, m_g_qa, m_w_qb, m_g_kva, m_w_kvb, m_g_qn, m_g_kn, m_w_oa, m_w_ob, m_w_o, m_g_mlp, m_w_up, m_w_down, m_g_ple, m_w_ple_gate, m_w_ple, m_g_final, v_g_mix, v_w_in, v_g_qa, v_w_qb, v_g_kva, v_w_kvb, v_g_qn, v_g_kn, v_w_oa, v_w_ob, v_w_o, v_g_mlp, v_w_up, v_w_down, v_g_ple, v_w_ple_gate, v_w_ple, v_g_final):
    n_b, s_len, _ = x.shape
    t = n_b * s_len
    tm = min(512, s_len)
    tq_f = min(2048, s_len)
    tq_b = min(2048, s_len)

    mats = dict(w_in=(w_in, m_w_in, v_w_in), w_qb=(w_qb, m_w_qb, v_w_qb), w_kvb=(w_kvb, m_w_kvb, v_w_kvb),
                w_oa=(w_oa, m_w_oa, v_w_oa), w_ob=(w_ob, m_w_ob, v_w_ob), w_o=(w_o, m_w_o, v_w_o),
                w_up=(w_up, m_w_up, v_w_up), w_down=(w_down, m_w_down, v_w_down),
                w_ple_gate=(w_ple_gate, m_w_ple_gate, v_w_ple_gate), w_ple=(w_ple, m_w_ple, v_w_ple))
    col_sharded = ("w_in", "w_qb", "w_kvb", "w_oa", "w_ob", "w_up", "w_ple")

    blocks = {}
    for name in PACK_W1 + ("w_oa", "w_ob", "w_ple"):
        blocks[name] = mats[name][0][0].T.reshape(-1, D_MODEL).astype(BF16)
    off_w1, _ = _pack_offsets(PACK_W1)
    off_w2, _ = _pack_offsets(PACK_W2)
    off_w3, _ = _pack_offsets(PACK_W3)
    xf = x.reshape(t, D_MODEL)
    h, full1 = norm_x(xf, g_mix, tm, comm=gather_first_comm(_pack_rows(blocks, PACK_W1)))
    pack2, pack3, full1 = pack_late_weights(w_up, w_down, w_o, w_ple_gate, _pack_rows(blocks, ("w_oa", "w_ob", "w_ple")),
                                            comm=gather_pass_comm(full1))

    def gathered(full, offs, name, rows, width):
        return full[:, offs[name]:offs[name] + rows].reshape(-1, width)

    w_in_t = _expand_w_in(gathered(full1, off_w1, "w_in", 404, D_MODEL))
    w_qb_t = _pad_heads(gathered(full1, off_w1, "w_qb", 24, Q_LORA), H_A, 0)
    wkvb = gathered(full1, off_w1, "w_kvb", 16, KV_LORA).reshape(H_A, 2, 64, KV_LORA)
    w_kvb_t = jnp.concatenate([_pad_heads(wkvb[:, 0].reshape(-1, KV_LORA), H_A, 0),
                               _pad_heads(wkvb[:, 1].reshape(-1, KV_LORA), H_A, 0)], axis=0)

    tabs = _rope_tables(s_len)
    g_qn_p = jnp.pad(g_qn, ((0, 0), (0, HP - HD_B)))
    g_kn_p = jnp.pad(g_kn, ((0, 0), (0, HP - HD_B)))
    pf = p.reshape(t, PLE_DIM)
    tgt = loss_target.reshape(t, D_MODEL)

    zp, full2 = in_proj(h, w_in_t, tm, comm=gather_first_comm(pack2))
    qa, ka, va, qb, kb, vb, cq, ckv, full2 = attn_prep(zp, tabs, g_qa, g_kva, g_qn_p, g_kn_p, w_qb_t, w_kvb_t, tm, s_len,
                                                       comm=gather_pass_comm(full2))
    oa, lse_a, full3 = attn_fwd(qa, ka, va, n_b, s_len, tq_f, "attn_a_fwd", comm=gather_first_comm(pack3))
    ob, lse_b, full3 = attn_fwd(qb, kb, vb, n_b, s_len, tq_f, "attn_b_fwd", comm=gather_pass_comm(full3))
    w_oa_t = _pad_heads(gathered(full2, off_w2, "w_oa", 64, H_A * V_DIM_A), H_A, 1)
    w_ob_t = _pad_heads(gathered(full2, off_w2, "w_ob", 64, H_B * HD_B), H_B, 1)
    w_ple_t = gathered(full2, off_w2, "w_ple", 32, PLE_DIM)
    x1, merged, yab = merge_fwd(oa, ob, zp, xf, w_oa_t, w_ob_t, full2, off_w2, tm)
    x2, u = mlp_fwd(x1, g_mlp, full3, off_w3, tm)
    dx2, dt, h3, dpe, st_ple, dx2b = ple_loss_bwd(x2, pf, tgt, g_ple, g_final.reshape(1, D_MODEL), full2, off_w2, w_ple_t, tm)
    dx1, da, h2, st_mlp, dx1b = mlp_bwd(dx2, x1, u, g_mlp, full3, off_w3, tm)

    core = lax.axis_index("c").astype(jnp.int32).reshape(1)
    chip = (2 * lax.axis_index("x") + lax.axis_index("y")).astype(jnp.int32).reshape(1)

    def packed(gblocks, names):
        return _pack_rows({n: gblocks[n].reshape(N_DEV, -1, D_MODEL).astype(BF16) for n in names}, names)

    off_g1, rows_g1 = _pack_offsets(PACK_G1)
    gpack1 = matmul_tn_packed(da, h2, "gw_up", 512, off_g1["w_up"], rows_g1)
    gpack1 = matmul_tn_packed(u, dx2b, "gw_down", 512, off_g1["w_down"], rows_g1, buf=gpack1, square_a=True)
    gpack1 = matmul_tn_packed(h3, dt, "gw_pg", 128, off_g1["w_ple_gate"], rows_g1, buf=gpack1)
    gple = matmul_tn(dpe, pf, "gw_ple").reshape(N_DEV, -1, D_MODEL).astype(BF16)
    gpack1 = lax.dynamic_update_slice(gpack1, gple, (0, off_g1["w_ple"], 0))
    doa, dob, dz, dya, dyb, got1 = merge_bwd(dx1b, yab, zp, w_oa_t, w_ob_t, full2, off_w2, tm,
                                               comm=scatter_sibling_comm(gpack1))
    part1 = add_pairs(gpack1, got1, core)
    dqa, dka, dva, land1 = attn_bwd(qa, ka, va, doa, oa, lse_a, n_b, s_len, tq_b, "attn_a_bwd", comm=scatter_chips_comm(part1))
    gshard1 = sum_chips(part1, land1, chip)

    off_g2, rows_g2 = _pack_offsets(PACK_G2)
    g2 = dict(w_oa=_unpad_heads(matmul_tn(dya, oa, "gw_oa"), H_A, V_DIM_A, 1),
              w_ob=_unpad_heads(matmul_tn(dyb, ob, "gw_ob"), H_B, HD_B, 1))
    gpack2 = matmul_tn_packed(merged, dx1b, "gw_o", 128, off_g2["w_o"], rows_g2)
    gpack2 = lax.dynamic_update_slice(gpack2, packed(g2, ("w_oa", "w_ob")), (0, off_g2["w_oa"], 0))
    part2 = add_pairs(gpack2, exchange_sibling(gpack2), core)
    dqb, dkb, dvb, land2 = attn_bwd(qb, kb, vb, dob, ob, lse_b, n_b, s_len, tq_b, "attn_b_bwd", comm=scatter_chips_comm(part2))
    gshard2 = sum_chips(part2, land2, chip)
    dz, dqap, dkva, st_prep = prep_bwd(dqa, dka, dva, dqb, dkb, dvb, zp, dz, tabs, g_qa, g_kva, g_qn_p, g_kn_p,
                                       w_qb_t, w_kvb_t, tm, s_len)

    gkv = matmul_tn(dkva, ckv, "gw_kvb")
    g3 = dict(
        w_in=_collapse_w_in(matmul_tn(dz, h, "gw_in")),
        w_qb=_unpad_heads(matmul_tn(dqap, cq, "gw_qb"), H_A, QK_NOPE + QK_ROPE, 0),
        w_kvb=jnp.stack([_unpad_heads(gkv[:H_A * HP], H_A, 64, 0).reshape(H_A, 64, KV_LORA),
                         _unpad_heads(gkv[H_A * HP:], H_A, 64, 0).reshape(H_A, 64, KV_LORA)], axis=1))
    gpack3 = packed(g3, PACK_G3)
    part3 = add_pairs(gpack3, exchange_sibling(gpack3), core)
    grad_x, st_mix, land3 = in_bwd(dz, xf, dx1, g_mix, w_in_t, tm, comm=scatter_chips_comm(part3))
    gshard3 = sum_chips(part3, land3, chip)
    off_g3, _ = _pack_offsets(PACK_G3)
    shards = {n: (gshard1, off_g1[n]) for n in PACK_G1}
    shards.update({n: (gshard2, off_g2[n]) for n in PACK_G2})
    shards.update({n: (gshard3, off_g3[n]) for n in PACK_G3})

    stats = allreduce_stats(st_mix, st_prep, st_mlp, st_ple)
    loss = jnp.sum(stats[ST_LOSS])

    out_g, out_d, out_m, out_v = {}, {}, {}, {}
    for name, (w, m, v) in mats.items():
        gshard, off = shards[name]
        r, c = w.shape[1:]
        if name in col_sharded:
            g2 = gshard[off:off + (r * c) // D_MODEL].reshape(c, r)
            if r % 128 == 0 and c % 128 == 0:
                res = adamw(w, g2, m, v, "adamw_" + name, g_transposed=True)
            else:
                res = adamw(w[0].T[None], g2, m[0].T[None], v[0].T[None], "adamw_" + name)
                res = tuple(a[0].T[None] for a in res)
        else:
            res = adamw(w, gshard[off:off + r], m, v, "adamw_" + name)
        out_g[name], out_d[name], out_m[name], out_v[name] = res

    gains = (("g_mix", g_mix, m_g_mix, v_g_mix, ST_G_MIX), ("g_qa", g_qa, m_g_qa, v_g_qa, ST_G_QA),
             ("g_kva", g_kva, m_g_kva, v_g_kva, ST_G_KVA), ("g_qn", g_qn, m_g_qn, v_g_qn, ST_G_QN),
             ("g_kn", g_kn, m_g_kn, v_g_kn, ST_G_KN), ("g_mlp", g_mlp, m_g_mlp, v_g_mlp, ST_G_MLP),
             ("g_ple", g_ple, m_g_ple, v_g_ple, ST_G_PLE), ("g_final", g_final, m_g_final, v_g_final, ST_G_FINAL))
    res = adamw_gains(stats, [(r_, w.reshape(1, -1), m.reshape(1, -1), v.reshape(1, -1)) for _, w, m, v, r_ in gains])
    for (name, w, _, _, _), (gg, gd, gm, gv) in zip(gains, res):
        out_g[name], out_d[name], out_m[name], out_v[name] = (a.reshape(w.shape) for a in (gg, gd, gm, gv))

    order = ("g_mix", "w_in", "g_qa", "w_qb", "g_kva", "w_kvb", "g_qn", "g_kn", "w_oa", "w_ob", "w_o", "g_mlp",
             "w_up", "w_down", "g_ple", "w_ple_gate", "w_ple", "g_final")
    return (loss, grad_x.reshape(x.shape), *[out_g[n] for n in order], *[out_d[n] for n in order],
            *[out_m[n] for n in order], *[out_v[n] for n in order])
```

```python
import numpy as np
import jax
import jax.numpy as jnp
from jax import lax
from jax.experimental import pallas as pl
from jax.experimental.pallas import tpu as pltpu

F32 = jnp.float32
BF16 = jnp.bfloat16

D_MODEL = 1024
EPS = 1e-6
ROPE_THETA = 10000.0
GRID_W = 64
H_A = 8
QK_NOPE = 64
QK_ROPE = 32
V_DIM_A = 64
Q_LORA = 256
KV_LORA = 128
H_B = 8
KV_B = 2
HD_B = 64
D_FF = 4 * D_MODEL
PLE_DIM = 256
HP = 128
ZP = 4096
N_DEV = 8
N_CHIP = 4

ADAM_LR = 0.001
ADAM_B1 = 0.9
ADAM_B2 = 0.999
ADAM_EPS = 1e-08
ADAM_WD = 0.01
ADAM_STEP = 10

VMEM_LIMIT = 52 * 1024 * 1024

PACK_ROWS = dict(w_in=416, w_qb=32, w_kvb=16, w_oa=64, w_ob=64, w_o=128, w_up=512, w_down=512, w_ple_gate=128, w_ple=32)
PACK_W1 = ("w_in", "w_qb", "w_kvb")
PACK_W2 = ("w_o", "w_ple_gate", "w_oa", "w_ob", "w_ple")
PACK_W3 = ("w_up", "w_down")
PACK_G1 = ("w_up", "w_down", "w_ple_gate", "w_ple")
PACK_G2 = ("w_o", "w_oa", "w_ob")
PACK_G3 = ("w_in", "w_qb", "w_kvb")


def _pack_offsets(names):
    off, o = {}, 0
    for n in names:
        off[n] = o
        o += PACK_ROWS[n]
    return off, o

ST_G_MIX, ST_G_QA, ST_G_KVA, ST_G_QN, ST_G_KN, ST_G_MLP, ST_G_PLE, ST_G_FINAL, ST_LOSS = range(9)
ST_ROWS = 16


def _dot_nn(a, b):
    return lax.dot_general(a, b, (((1,), (0,)), ((), ())), preferred_element_type=F32)


def _dot_nt(a, b):
    return lax.dot_general(a, b, (((1,), (1,)), ((), ())), preferred_element_type=F32)


def _dot_tn(a, b):
    return lax.dot_general(a, b, (((0,), (0,)), ((), ())), preferred_element_type=F32)


def _rstd(x, n):
    return lax.rsqrt(jnp.sum(x * x, axis=-1, keepdims=True) * (1.0 / n) + EPS)


def _rms_bwd(dy, xh, r, g, n):
    dxh = dy * g
    return r * (dxh - xh * (jnp.sum(dxh * xh, axis=-1, keepdims=True) * (1.0 / n)))


def _rope_fwd(x, c, s1, s2):
    return x * c + pltpu.roll(x, HP - 16, 1) * s1 + pltpu.roll(x, 16, 1) * s2


def _rope_bwd(d, c, s1, s2):
    return d * c + pltpu.roll(d * s1, 16, 1) + pltpu.roll(d * s2, HP - 16, 1)


def _colsum(v):
    return jnp.sum(v, axis=0, keepdims=True)


def _params(sem=None, vmem=VMEM_LIMIT):
    return pltpu.CompilerParams(dimension_semantics=sem, vmem_limit_bytes=vmem)


def _resident(shape):
    nd = len(shape)
    return pl.BlockSpec(shape, lambda *_: (0,) * nd, pipeline_mode=pl.Buffered(1))


def _rows(tm, width, col=0):
    return pl.BlockSpec((tm, width), lambda i: (i, col))


def _packed_weight(rows, off):
    return pl.BlockSpec((N_DEV, rows, D_MODEL), lambda *_: (0, off // rows, 0), pipeline_mode=pl.Buffered(1))


def _wrows(ref, start, size):
    rows = ref.shape[1]
    return ref[start // rows:(start + size) // rows].reshape(size, D_MODEL)


def _mesh_pos():
    return lax.axis_index("x"), lax.axis_index("y"), lax.axis_index("c")


def _flip(v, bit):
    return (1 - v) if bit else v


_ANY = pl.BlockSpec(memory_space=pl.ANY)
_MESH = pl.DeviceIdType.MESH


def _remote(src, dst, send_sems, recv_sems, k, to):
    return pltpu.make_async_remote_copy(src_ref=src, dst_ref=dst, send_sem=send_sems.at[k], recv_sem=recv_sems.at[k],
                                        device_id=to, device_id_type=_MESH)


def _sibling_copies(g_ref, got_ref, send_sems, recv_sems):
    x, y, c = _mesh_pos()
    return [_remote(g_ref.at[2 * j + (1 - c)], got_ref.at[j], send_sems, recv_sems, j, (x, y, 1 - c)) for j in range(N_CHIP)]


def _chip_copies(p_ref, land_ref, send_sems, recv_sems):
    x, y, c = _mesh_pos()
    copies = []
    for k in (1, 2, 3):
        tx, ty = _flip(x, k & 2), _flip(y, k & 1)
        copies.append(_remote(p_ref.at[2 * tx + ty], land_ref.at[k - 1], send_sems, recv_sems, k - 1, (tx, ty, c)))
    return copies


class _Comm:
    def __init__(self, ins, out_shapes, sems, make, aliases=None):
        self.ins, self.out_shapes, self.sems, self.make, self.aliases = list(ins), list(out_shapes), list(sems), make, aliases or {}


def _comm_parts(comm, n_in, n_out):
    if comm is None:
        return [], [], [], [], {}
    alias = {n_in + j: n_out + k for j, k in comm.aliases.items()}
    return comm.ins, [_ANY] * len(comm.ins), comm.out_shapes, comm.sems, alias


def _split_refs(refs, n_in, n_out, n_scratch, comm):
    n_ci = len(comm.ins) if comm else 0
    n_co = len(comm.out_shapes) if comm else 0
    cuts, i = [], 0
    for n in (n_in, n_ci, n_out, n_co, n_scratch):
        cuts.append(refs[i:i + n])
        i += n
    return (*cuts, refs[i:])


def _grid_edge(grid, last):
    cond = None
    for d, n in enumerate(grid):
        here = pl.program_id(d) == (n - 1 if last else 0)
        cond = here if cond is None else cond & here
    return cond


def _comm_start(comm, cin, cout, csem, grid):
    if comm is not None:
        @pl.when(_grid_edge(grid, False))
        def _():
            for cp in comm.make(cin, cout, csem):
                cp.start()


def _comm_finish(comm, cin, cout, csem, grid):
    if comm is not None:
        @pl.when(_grid_edge(grid, True))
        def _():
            for cp in comm.make(cin, cout, csem):
                cp.wait()


def gather_first_comm(shard):
    r, w = shard.shape

    def make(cin, cout, sems):
        (x_ref,), (out_ref,), (send_sems, recv_sems, local_sem) = cin, cout, sems
        x, y, c = _mesh_pos()
        mine = out_ref.at[4 * x + 2 * y + c]
        targets = [(x, y, 1 - c), (1 - x, y, c), (x, 1 - y, c), (1 - x, 1 - y, c)]
        return [_remote(x_ref, mine, send_sems, recv_sems, k, to) for k, to in enumerate(targets)] + [
            pltpu.make_async_copy(x_ref, mine, local_sem)]

    return _Comm([shard], [jax.ShapeDtypeStruct((N_DEV, r, w), shard.dtype)],
                 [pltpu.SemaphoreType.DMA((4,)), pltpu.SemaphoreType.DMA((4,)), pltpu.SemaphoreType.DMA], make)


def gather_pass_comm(full):
    def make(cin, cout, sems):
        (in_ref,), (out_ref,), (send_sems, recv_sems) = cin, cout, sems
        x, y, c = _mesh_pos()
        copies = []
        for k, (px, py) in enumerate([(1 - x, y), (x, 1 - y), (1 - x, 1 - y)]):
            idx = 4 * px + 2 * py + c
            copies.append(_remote(in_ref.at[idx], out_ref.at[idx], send_sems, recv_sems, k, (x, y, 1 - c)))
        return copies

    return _Comm([full], [jax.ShapeDtypeStruct(full.shape, full.dtype)],
                 [pltpu.SemaphoreType.DMA((3,)), pltpu.SemaphoreType.DMA((3,))], make, aliases={0: 0})


def scatter_sibling_comm(g):
    _, r, w = g.shape
    return _Comm([g], [jax.ShapeDtypeStruct((N_CHIP, r, w), g.dtype)],
                 [pltpu.SemaphoreType.DMA((N_CHIP,)), pltpu.SemaphoreType.DMA((N_CHIP,))],
                 lambda cin, cout, sems: _sibling_copies(cin[0], cout[0], sems[0], sems[1]))


def scatter_chips_comm(part):
    _, r, w = part.shape
    return _Comm([part], [jax.ShapeDtypeStruct((N_CHIP - 1, r, w), part.dtype)],
                 [pltpu.SemaphoreType.DMA((3,)), pltpu.SemaphoreType.DMA((3,))],
                 lambda cin, cout, sems: _chip_copies(cin[0], cout[0], sems[0], sems[1]))


def exchange_sibling(g):
    _, r, w = g.shape

    def body(g_ref, got_ref, send_sems, recv_sems):
        copies = _sibling_copies(g_ref, got_ref, send_sems, recv_sems)
        for cp in copies:
            cp.start()
        for cp in copies:
            cp.wait()

    return pl.pallas_call(
        body, name="exchange_sibling", out_shape=jax.ShapeDtypeStruct((N_CHIP, r, w), g.dtype),
        in_specs=[_ANY], out_specs=_ANY,
        scratch_shapes=[pltpu.SemaphoreType.DMA((N_CHIP,)), pltpu.SemaphoreType.DMA((N_CHIP,))],
    )(g)


def exchange_chips(part):
    _, r, w = part.shape

    def body(p_ref, land_ref, send_sems, recv_sems):
        copies = _chip_copies(p_ref, land_ref, send_sems, recv_sems)
        for cp in copies:
            cp.start()
        for cp in copies:
            cp.wait()

    return pl.pallas_call(
        body, name="exchange_chips", out_shape=jax.ShapeDtypeStruct((N_CHIP - 1, r, w), part.dtype),
        in_specs=[_ANY], out_specs=_ANY,
        scratch_shapes=[pltpu.SemaphoreType.DMA((3,)), pltpu.SemaphoreType.DMA((3,))],
    )(part)


def allreduce_stats(st_mix, st_prep, st_mlp, st_ple):
    def body(mix_ref, prep_ref, mlp_ref, ple_ref, out_ref, mine, gath, send_sems, recv_sems):
        x, y, c = _mesh_pos()
        me = 4 * x + 2 * y + c
        mine[...] = jnp.zeros_like(mine)
        mine[ST_G_MIX:ST_G_MIX + 1, :] = mix_ref[...]
        mine[ST_G_QA:ST_G_KN + 1, 0:256] = prep_ref[...]
        mine[ST_G_MLP:ST_G_MLP + 1, :] = mlp_ref[...]
        mine[ST_G_PLE:ST_LOSS + 1, :] = ple_ref[...]
        gath[me] = mine[...]
        copies = []
        for k in range(1, N_DEV):
            peer = (_flip(x, k & 4), _flip(y, k & 2), _flip(c, k & 1))
            copies.append(_remote(mine, gath.at[me], send_sems, recv_sems, k - 1, peer))
        for cp in copies:
            cp.start()
        for cp in copies:
            cp.wait()
        acc = gath[0]
        for d in range(1, N_DEV):
            acc = acc + gath[d]
        out_ref[...] = acc

    vm = pl.BlockSpec(memory_space=pltpu.VMEM)
    return pl.pallas_call(
        body, name="allreduce_stats", out_shape=jax.ShapeDtypeStruct((ST_ROWS, D_MODEL), F32),
        in_specs=[vm] * 4, out_specs=vm,
        scratch_shapes=[pltpu.VMEM((ST_ROWS, D_MODEL), F32), pltpu.VMEM((N_DEV, ST_ROWS, D_MODEL), F32),
                        pltpu.SemaphoreType.DMA((N_DEV - 1,)), pltpu.SemaphoreType.DMA((N_DEV - 1,))],
    )(st_mix, st_prep, st_mlp, st_ple)


def adamw_gains(stats, gains):
    c1 = 1.0 - ADAM_B1 ** ADAM_STEP
    c2 = 1.0 - ADAM_B2 ** ADAM_STEP
    n = len(gains)

    def body(st_ref, *refs):
        ins, outs = refs[:3 * n], refs[3 * n:]
        for i, (row, w, _, _) in enumerate(gains):
            width = w.shape[1]
            gv = st_ref[row:row + 1, 0:width]
            mn = ADAM_B1 * ins[3 * i + 1][...] + (1.0 - ADAM_B1) * gv
            vn = ADAM_B2 * ins[3 * i + 2][...] + (1.0 - ADAM_B2) * (gv * gv)
            outs[4 * i][...] = gv
            outs[4 * i + 1][...] = -ADAM_LR * ((mn / c1) / (jnp.sqrt(vn / c2) + ADAM_EPS) + ADAM_WD * ins[3 * i][...])
            outs[4 * i + 2][...] = mn
            outs[4 * i + 3][...] = vn

    vm = pl.BlockSpec(memory_space=pltpu.VMEM)
    flat = [a for (_, w, m, v) in gains for a in (w, m, v)]
    out_shape = tuple(jax.ShapeDtypeStruct(w.shape, F32) for (_, w, _, _) in gains for _ in range(4))
    res = pl.pallas_call(body, name="adamw_gains", out_shape=out_shape, in_specs=[vm] * (1 + 3 * n),
                         out_specs=tuple([vm] * (4 * n)))(stats, *flat)
    return [res[4 * i:4 * i + 4] for i in range(n)]


def _row_tile(r, cap=640):
    return max(d for d in range(16, min(r, cap) + 1, 16) if r % d == 0)


def add_pairs(g, got, core):
    n, r, w = got.shape
    tr = _row_tile(r)

    def body(c_ref, a_ref, b_ref, o_ref):
        o_ref[...] = (a_ref[...].astype(F32) + b_ref[...].astype(F32)).astype(o_ref.dtype)

    spec = pl.BlockSpec((1, tr, w), lambda i, j, c: (i, j, 0))
    return pl.pallas_call(
        body, name="add_pairs", out_shape=jax.ShapeDtypeStruct(got.shape, got.dtype),
        grid_spec=pltpu.PrefetchScalarGridSpec(
            num_scalar_prefetch=1, grid=(n, r // tr),
            in_specs=[pl.BlockSpec((1, tr, w), lambda i, j, c: (2 * i + c[0], j, 0)), spec], out_specs=spec),
        compiler_params=_params(("parallel", "parallel")),
    )(core, g, got)


def sum_chips(part, land, chip):
    _, r, w = part.shape
    tr = _row_tile(r)

    def body(c_ref, p_ref, l_ref, o_ref):
        acc = p_ref[0].astype(F32)
        for s in range(N_CHIP - 1):
            acc = acc + l_ref[s].astype(F32)
        o_ref[...] = acc

    return pl.pallas_call(
        body, name="sum_chips", out_shape=jax.ShapeDtypeStruct((r, w), F32),
        grid_spec=pltpu.PrefetchScalarGridSpec(
            num_scalar_prefetch=1, grid=(r // tr,),
            in_specs=[pl.BlockSpec((1, tr, w), lambda i, c: (c[0], i, 0)), pl.BlockSpec((N_CHIP - 1, tr, w), lambda i, c: (0, i, 0))],
            out_specs=pl.BlockSpec((tr, w), lambda i, c: (i, 0))),
        compiler_params=_params(("parallel",)),
    )(chip, part, land)


def norm_x(x, g_mix, tm, comm=None):
    t = x.shape[0]
    grid = (t // tm,)
    c_ins, c_in_specs, c_outs, c_sems, alias = _comm_parts(comm, 2, 1)

    def body(*refs):
        (x_ref, g_ref), cin, (h_ref,), cout, _, csem = _split_refs(refs, 2, 1, 0, comm)
        _comm_start(comm, cin, cout, csem, grid)
        xv = x_ref[...]
        h_ref[...] = (xv * _rstd(xv, D_MODEL) * g_ref[...]).astype(BF16)
        _comm_finish(comm, cin, cout, csem, grid)

    return pl.pallas_call(
        body, name="norm_x", grid=grid, out_shape=(jax.ShapeDtypeStruct((t, D_MODEL), BF16), *c_outs),
        in_specs=[_rows(tm, D_MODEL), _resident((1, D_MODEL)), *c_in_specs],
        out_specs=(_rows(tm, D_MODEL), *([_ANY] * len(c_outs))),
        scratch_shapes=c_sems, input_output_aliases=alias, compiler_params=_params(("arbitrary",)),
    )(x, g_mix, *c_ins)


def pack_late_weights(w_up, w_down, w_o, w_pg, small, comm=None):
    rows2 = sum(PACK_ROWS[n] for n in PACK_W2)
    rows3 = sum(PACK_ROWS[n] for n in PACK_W3)
    c_ins, c_in_specs, c_outs, c_sems, alias = _comm_parts(comm, 5, 2)
    grid = (1,)

    def body(*refs):
        (up_ref, dn_ref, o_ref, pg_ref, sm_ref), cin, (p2_ref, p3_ref), cout, _, csem = _split_refs(refs, 5, 2, 0, comm)
        _comm_start(comm, cin, cout, csem, grid)
        p2_ref[0:128, :] = o_ref[0].astype(BF16)
        p2_ref[128:256, :] = pg_ref[0].astype(BF16)
        p2_ref[256:rows2, :] = sm_ref[...]
        p3_ref[0:512, :] = up_ref[0].T.astype(BF16)
        p3_ref[512:1024, :] = dn_ref[0].astype(BF16)
        _comm_finish(comm, cin, cout, csem, grid)

    def whole(a):
        nd = a.ndim
        return pl.BlockSpec(a.shape, lambda i: (0,) * nd)

    args = (w_up, w_down, w_o, w_pg, small)
    return pl.pallas_call(
        body, name="pack_late_weights", grid=grid,
        out_shape=(jax.ShapeDtypeStruct((rows2, D_MODEL), BF16), jax.ShapeDtypeStruct((rows3, D_MODEL), BF16), *c_outs),
        in_specs=[*[whole(a) for a in args], *c_in_specs],
        out_specs=(pl.BlockSpec((rows2, D_MODEL), lambda i: (0, 0)), pl.BlockSpec((rows3, D_MODEL), lambda i: (0, 0)),
                   *([_ANY] * len(c_outs))),
        scratch_shapes=c_sems, input_output_aliases=alias, compiler_params=_params(("arbitrary",)),
    )(*args, *c_ins)


def in_proj(h, w_in_t, tm, comm=None):
    t = h.shape[0]
    nc = 512
    grid = (t // tm,)
    c_ins, c_in_specs, c_outs, c_sems, alias = _comm_parts(comm, 2, 1)

    def body(*refs):
        (h_ref, w_ref), cin, (z_ref,), cout, _, csem = _split_refs(refs, 2, 1, 0, comm)
        _comm_start(comm, cin, cout, csem, grid)
        hv = h_ref[...]
        for cidx in range(ZP // nc):
            z_ref[:, cidx * nc:(cidx + 1) * nc] = _dot_nt(hv, w_ref[cidx * nc:(cidx + 1) * nc, :])
        _comm_finish(comm, cin, cout, csem, grid)

    return pl.pallas_call(
        body, name="in_proj", grid=grid, out_shape=(jax.ShapeDtypeStruct((t, ZP), F32), *c_outs),
        in_specs=[_rows(tm, D_MODEL), _resident((ZP, D_MODEL)), *c_in_specs],
        out_specs=(_rows(tm, ZP), *([_ANY] * len(c_outs))),
        scratch_shapes=c_sems, input_output_aliases=alias, compiler_params=_params(("arbitrary",)),
    )(h, w_in_t, *c_ins)


def attn_prep(zp, tabs, g_qa, g_kva, g_qn, g_kn, w_qb_t, w_kvb_t, tm, s_len, comm=None):
    t = zp.shape[0]
    nsb = s_len // tm
    scale_a = (QK_NOPE + QK_ROPE) ** -0.5
    scale_b = HD_B ** -0.5

    grid = (t // tm,)
    c_ins, c_in_specs, c_outs, c_sems, alias = _comm_parts(comm, 13, 8)

    def body(*refs):
        ((qb_ref, qlat_ref, kb_ref, vb_ref, ckv_ref, kpe_ref, tab_ref, gqa_ref, gkva_ref, gqn_ref, gkn_ref, wqb_ref,
          wkvb_ref), cin, (qa_o, ka_o, va_o, qb_o, kb_o, vb_o, cq_o, ckvn_o), cout, _, csem) = _split_refs(refs, 13, 8, 0, comm)
        _comm_start(comm, cin, cout, csem, grid)
        ca, s1a, s2a = tab_ref[0], tab_ref[1], tab_ref[2]
        ck = tab_ref[3]
        cb, s1b, s2b = tab_ref[4], tab_ref[5], tab_ref[6]
        ql = qlat_ref[...]
        cq = (ql * _rstd(ql, Q_LORA) * gqa_ref[...]).astype(BF16)
        cq_o[...] = cq
        qa = _dot_nt(cq, wqb_ref[...])
        slabs = [slice(h * HP, (h + 1) * HP) for h in range(H_A)]
        qa_o[...] = jnp.concatenate(
            [(_rope_fwd(qa[:, sl], ca, s1a, s2a) * scale_a).astype(BF16) for sl in slabs], axis=1)
        cr = ckv_ref[...]
        ckv = (cr * _rstd(cr, KV_LORA) * gkva_ref[...]).astype(BF16)
        ckvn_o[...] = ckv
        kva = _dot_nt(ckv, wkvb_ref[...])
        kpe = _rope_fwd(kpe_ref[...], ck, s1a, s2a)
        ka_o[...] = jnp.concatenate([(kva[:, sl] + kpe).astype(BF16) for sl in slabs], axis=1)
        va_o[...] = kva[:, H_A * HP:].astype(BF16)
        gqn, gkn = gqn_ref[...], gkn_ref[...]

        def norm_rope(ref, sl, g, scale):
            xs = ref[:, sl]
            y = _rope_fwd(xs * _rstd(xs, HD_B) * g, cb, s1b, s2b)
            return (y if scale is None else y * scale).astype(BF16)

        qb_o[...] = jnp.concatenate([norm_rope(qb_ref, sl, gqn, scale_b) for sl in slabs], axis=1)
        kb_o[...] = jnp.concatenate([norm_rope(kb_ref, sl, gkn, None) for sl in slabs[:KV_B]], axis=1)
        vb_o[...] = vb_ref[...].astype(BF16)
        _comm_finish(comm, cin, cout, csem, grid)

    def o(width):
        return jax.ShapeDtypeStruct((t, width), BF16)

    return pl.pallas_call(
        body, name="attn_prep", grid=grid,
        out_shape=(o(H_A * HP), o(H_A * HP), o(H_A * HP), o(H_B * HP), o(KV_B * HP), o(KV_B * HP), o(Q_LORA), o(KV_LORA),
                   *c_outs),
        in_specs=[_rows(tm, 1024, 2), _rows(tm, 256, 12), _rows(tm, 256, 13), _rows(tm, 256, 14),
                  _rows(tm, 128, 30), _rows(tm, 128, 31),
                  pl.BlockSpec((7, tm, HP), lambda i: (0, i % nsb, 0)),
                  _resident((1, Q_LORA)), _resident((1, KV_LORA)), _resident((1, HP)), _resident((1, HP)),
                  _resident((H_A * HP, Q_LORA)), _resident((2 * H_A * HP, KV_LORA)), *c_in_specs],
        out_specs=(_rows(tm, H_A * HP), _rows(tm, H_A * HP), _rows(tm, H_A * HP), _rows(tm, H_B * HP),
                   _rows(tm, KV_B * HP), _rows(tm, KV_B * HP), _rows(tm, Q_LORA), _rows(tm, KV_LORA), *([_ANY] * len(c_outs))),
        scratch_shapes=c_sems, input_output_aliases=alias, compiler_params=_params(("arbitrary",)),
    )(zp, zp, zp, zp, zp, zp, tabs, g_qa, g_kva, g_qn, g_kn, w_qb_t, w_kvb_t, *c_ins)


def attn_fwd(q, k, v, n_b, s_len, tq, name, comm=None):
    t = q.shape[0]
    n_h, n_hk = q.shape[1] // HP, k.shape[1] // HP
    grp = n_h // n_hk
    nq = s_len // tq
    sub = min(tq, 256)
    grid = (n_b, n_h, nq)
    c_ins, c_in_specs, c_outs, c_sems, alias = _comm_parts(comm, 3, 2)

    def body(*refs):
        (q_ref, k_ref, v_ref), cin, (o_ref, lse_ref), cout, _, csem = _split_refs(refs, 3, 2, 0, comm)
        _comm_start(comm, cin, cout, csem, grid)
        kv, vv = k_ref[...], v_ref[...]
        for r in range(tq // sub):
            rows = slice(r * sub, (r + 1) * sub)
            s = _dot_nt(q_ref[rows, :], kv)
            m = jnp.max(s, axis=-1, keepdims=True)
            p = jnp.exp(s - m)
            l = jnp.sum(p, axis=-1, keepdims=True)
            o_ref[rows, :] = (_dot_nn(p.astype(BF16), vv) * (1.0 / l)).astype(o_ref.dtype)
            lse_ref[rows, :] = jnp.broadcast_to(m + jnp.log(l), (sub, HP))
        _comm_finish(comm, cin, cout, csem, grid)

    qspec = pl.BlockSpec((tq, HP), lambda b, h, i: (b * nq + i, h))
    kspec = pl.BlockSpec((s_len, HP), lambda b, h, i: (b, h // grp))
    return pl.pallas_call(
        body, name=name, grid=grid,
        out_shape=(jax.ShapeDtypeStruct((t, n_h * HP), BF16), jax.ShapeDtypeStruct((t, n_h * HP), F32), *c_outs),
        in_specs=[qspec, kspec, kspec, *c_in_specs], out_specs=(qspec, qspec, *([_ANY] * len(c_outs))),
        scratch_shapes=c_sems, input_output_aliases=alias,
        compiler_params=_params(("arbitrary", "arbitrary", "arbitrary")),
    )(q, k, v, *c_ins)


def merge_fwd(oa, ob, zp, x, w_oa_t, w_ob_t, wpack, off, tm):
    t = x.shape[0]

    def body(oa_ref, ob_ref, ga_ref, gb_ref, x_ref, woa_ref, wob_ref, wo_ref, x1_o, mg_o, y_o):
        ya = _dot_nt(oa_ref[...], woa_ref[...])
        yb = _dot_nt(ob_ref[...], wob_ref[...])
        y_o[:, 0:D_MODEL] = ya.astype(BF16)
        y_o[:, D_MODEL:2 * D_MODEL] = yb.astype(BF16)
        merged = (jax.nn.sigmoid(ga_ref[...]) * ya + jax.nn.sigmoid(gb_ref[...]) * yb).astype(BF16)
        mg_o[...] = merged
        x1_o[...] = x_ref[...] + _dot_nn(merged, _wrows(wo_ref, 0, D_MODEL))

    return pl.pallas_call(
        body, name="merge_fwd", grid=(t // tm,),
        out_shape=(jax.ShapeDtypeStruct((t, D_MODEL), F32), jax.ShapeDtypeStruct((t, D_MODEL), BF16),
                   jax.ShapeDtypeStruct((t, 2 * D_MODEL), BF16)),
        in_specs=[_rows(tm, H_A * HP), _rows(tm, H_B * HP), _rows(tm, 1024, 0), _rows(tm, 1024, 1), _rows(tm, D_MODEL),
                  _resident((D_MODEL, H_A * HP)), _resident((D_MODEL, H_B * HP)), _packed_weight(128, off["w_o"])],
        out_specs=(_rows(tm, D_MODEL), _rows(tm, D_MODEL), _rows(tm, 2 * D_MODEL)), compiler_params=_params(("parallel",)),
    )(oa, ob, zp, zp, x, w_oa_t, w_ob_t, wpack)


def mlp_fwd(x1, g_mlp, wpack, off, tm):
    t = x1.shape[0]
    fc = 1024

    def body(x_ref, g_ref, wup_ref, wdn_ref, x2_o, u_o):
        xv = x_ref[...]
        h2 = (xv * _rstd(xv, D_MODEL) * g_ref[...]).astype(BF16)
        acc = xv
        for cidx in range(D_FF // fc):
            sl = slice(cidx * fc, (cidx + 1) * fc)
            u = jnp.maximum(_dot_nt(h2, _wrows(wup_ref, cidx * fc, fc)), 0.0)
            u_o[:, sl] = u.astype(BF16)
            acc = acc + _dot_nn((u * u).astype(BF16), _wrows(wdn_ref, cidx * fc, fc))
        x2_o[...] = acc

    return pl.pallas_call(
        body, name="mlp_fwd", grid=(t // tm,),
        out_shape=(jax.ShapeDtypeStruct((t, D_MODEL), F32), jax.ShapeDtypeStruct((t, D_FF), BF16)),
        in_specs=[_rows(tm, D_MODEL), _resident((1, D_MODEL)), _packed_weight(512, off["w_up"]), _packed_weight(512, off["w_down"])],
        out_specs=(_rows(tm, D_MODEL), _rows(tm, D_FF)), compiler_params=_params(("parallel",)),
    )(x1, g_mlp, wpack, wpack)


def ple_loss_bwd(x2, p, tgt, g_ple, g_final, wpack, off, w_ple_t, tm):
    t = x2.shape[0]
    inv_d = 1.0 / D_MODEL

    def body(x2_ref, p_ref, tg_ref, gp_ref, gf_ref, wpg_ref, wple_ref, dx2_o, dt_o, h3_o, dpe_o, st_o, dx2b_o):
        @pl.when(pl.program_id(0) == 0)
        def _():
            st_o[...] = jnp.zeros_like(st_o)

        x2v = x2_ref[...]
        gp, gf = gp_ref[...], gf_ref[...]
        w_pg = _wrows(wpg_ref, 0, D_MODEL)
        r2 = _rstd(x2v, D_MODEL)
        xh2 = x2v * r2
        h3 = (xh2 * gp).astype(BF16)
        h3_o[...] = h3
        gate = jax.nn.sigmoid(_dot_nn(h3, w_pg))
        pe = _dot_nt(p_ref[...].astype(BF16), wple_ref[...])
        x3 = x2v + gate * pe
        r3 = _rstd(x3, D_MODEL)
        xh3 = x3 * r3
        err = xh3 * gf - tg_ref[...]
        dy = err * inv_d
        dx3 = _rms_bwd(dy, xh3, r3, gf, D_MODEL)
        dpe_o[...] = (dx3 * gate).astype(BF16)
        dt = (dx3 * pe * gate * (1.0 - gate)).astype(BF16)
        dt_o[...] = dt
        dh3 = _dot_nt(dt, w_pg)
        dx2 = dx3 + _rms_bwd(dh3, xh2, r2, gp, D_MODEL)
        dx2_o[...] = dx2
        dx2b_o[...] = dx2.astype(BF16)
        st_o[0:1, :] += _colsum(dh3 * xh2)
        st_o[1:2, :] += _colsum(dy * xh3)
        st_o[2:3, :] += _colsum(err * err) * (0.5 * inv_d)

    bf = jax.ShapeDtypeStruct((t, D_MODEL), BF16)
    return pl.pallas_call(
        body, name="ple_loss_bwd", grid=(t // tm,),
        out_shape=(jax.ShapeDtypeStruct((t, D_MODEL), F32), bf, bf, bf, jax.ShapeDtypeStruct((3, D_MODEL), F32), bf),
        in_specs=[_rows(tm, D_MODEL), _rows(tm, PLE_DIM), _rows(tm, D_MODEL), _resident((1, D_MODEL)), _resident((1, D_MODEL)),
                  _packed_weight(128, off["w_ple_gate"]), _resident((D_MODEL, PLE_DIM))],
        out_specs=(_rows(tm, D_MODEL), _rows(tm, D_MODEL), _rows(tm, D_MODEL), _rows(tm, D_MODEL),
                   pl.BlockSpec((3, D_MODEL), lambda i: (0, 0)), _rows(tm, D_MODEL)),
        compiler_params=_params(("arbitrary",)),
    )(x2, p, tgt, g_ple, g_final, wpack, w_ple_t)


def mlp_bwd(dx2, x1, u, g_mlp, wpack, off, tm):
    t = x1.shape[0]
    fc = 1024

    def body(dx2_ref, x1_ref, u_ref, g_ref, wup_ref, wdn_ref, dx1_o, da_o, h2_o, st_o, dx1b_o):
        @pl.when(pl.program_id(0) == 0)
        def _():
            st_o[...] = jnp.zeros_like(st_o)

        d2 = dx2_ref[...]
        d2b = d2.astype(BF16)
        dh2 = jnp.zeros((tm, D_MODEL), F32)
        for cidx in range(D_FF // fc):
            sl = slice(cidx * fc, (cidx + 1) * fc)
            da = (_dot_nt(d2b, _wrows(wdn_ref, cidx * fc, fc)) * (2.0 * u_ref[:, sl].astype(F32))).astype(BF16)
            da_o[:, sl] = da
            dh2 = dh2 + _dot_nn(da, _wrows(wup_ref, cidx * fc, fc))
        xv = x1_ref[...]
        g = g_ref[...]
        r1 = _rstd(xv, D_MODEL)
        xh1 = xv * r1
        h2_o[...] = (xh1 * g).astype(BF16)
        st_o[...] += _colsum(dh2 * xh1)
        dx1 = d2 + _rms_bwd(dh2, xh1, r1, g, D_MODEL)
        dx1_o[...] = dx1
        dx1b_o[...] = dx1.astype(BF16)

    return pl.pallas_call(
        body, name="mlp_bwd", grid=(t // tm,),
        out_shape=(jax.ShapeDtypeStruct((t, D_MODEL), F32), jax.ShapeDtypeStruct((t, D_FF), BF16),
                   jax.ShapeDtypeStruct((t, D_MODEL), BF16), jax.ShapeDtypeStruct((1, D_MODEL), F32),
                   jax.ShapeDtypeStruct((t, D_MODEL), BF16)),
        in_specs=[_rows(tm, D_MODEL), _rows(tm, D_MODEL), _rows(tm, D_FF), _resident((1, D_MODEL)),
                  _packed_weight(512, off["w_up"]), _packed_weight(512, off["w_down"])],
        out_specs=(_rows(tm, D_MODEL), _rows(tm, D_FF), _rows(tm, D_MODEL), pl.BlockSpec((1, D_MODEL), lambda i: (0, 0)),
                   _rows(tm, D_MODEL)),
        compiler_params=_params(("arbitrary",)),
    )(dx2, x1, u, g_mlp, wpack, wpack)


def merge_bwd(dx1b, yab, zp, w_oa_t, w_ob_t, wpack, off, tm, comm=None):
    t = dx1b.shape[0]
    grid = (t // tm,)
    c_ins, c_in_specs, c_outs, c_sems, alias = _comm_parts(comm, 7, 5)

    def body(*refs):
        ((dx1_ref, y_ref, ga_ref, gb_ref, woa_ref, wob_ref, wo_ref), cin,
         (doa_o, dob_o, dg_o, dya_o, dyb_o), cout, _, csem) = _split_refs(refs, 7, 5, 0, comm)
        _comm_start(comm, cin, cout, csem, grid)
        dm = _dot_nt(dx1_ref[...], _wrows(wo_ref, 0, D_MODEL))
        for g_ref, w_ref, do_o, dy_o, col in ((ga_ref, woa_ref, doa_o, dya_o, 0), (gb_ref, wob_ref, dob_o, dyb_o, 1)):
            cols = slice(col * D_MODEL, (col + 1) * D_MODEL)
            sg = jax.nn.sigmoid(g_ref[...])
            dyv = (dm * sg).astype(BF16)
            dy_o[...] = dyv
            dg_o[:, cols] = (dm * y_ref[:, cols].astype(F32) * sg * (1.0 - sg)).astype(BF16)
            do_o[...] = _dot_nn(dyv, w_ref[...]).astype(BF16)
        _comm_finish(comm, cin, cout, csem, grid)

    bf = jax.ShapeDtypeStruct((t, D_MODEL), BF16)
    return pl.pallas_call(
        body, name="merge_bwd", grid=grid,
        out_shape=(bf, bf, jax.ShapeDtypeStruct((t, ZP), BF16), bf, bf, *c_outs),
        in_specs=[_rows(tm, D_MODEL), _rows(tm, 2 * D_MODEL), _rows(tm, 1024, 0), _rows(tm, 1024, 1),
                  _resident((D_MODEL, H_A * HP)), _resident((D_MODEL, H_B * HP)), _packed_weight(128, off["w_o"]), *c_in_specs],
        out_specs=(_rows(tm, D_MODEL), _rows(tm, D_MODEL), _rows(tm, 2 * D_MODEL), _rows(tm, D_MODEL), _rows(tm, D_MODEL),
                   *([_ANY] * len(c_outs))),
        scratch_shapes=c_sems, input_output_aliases=alias,
        compiler_params=_params(("arbitrary",)),
    )(dx1b, yab, zp, zp, w_oa_t, w_ob_t, wpack, *c_ins)


def attn_bwd(q, k, v, do, o, lse, n_b, s_len, tq, name, comm=None):
    t = q.shape[0]
    n_h, n_hk = q.shape[1] // HP, k.shape[1] // HP
    grp = n_h // n_hk
    nq = s_len // tq
    sub = min(tq, 256)
    grid = (n_b, n_hk, grp, nq)
    c_ins, c_in_specs, c_outs, c_sems, alias = _comm_parts(comm, 6, 3)

    def body(*refs):
        ((q_ref, k_ref, v_ref, do_ref, o_ref, lse_ref), cin, (dq_o, dk_o, dv_o), cout, (p_s, ds_s, dk_acc, dv_acc),
         csem) = _split_refs(refs, 6, 3, 4, comm)
        _comm_start(comm, cin, cout, csem, grid)

        @pl.when((pl.program_id(2) == 0) & (pl.program_id(3) == 0))
        def _():
            dk_acc[...] = jnp.zeros_like(dk_acc)
            dv_acc[...] = jnp.zeros_like(dv_acc)

        kv, vv = k_ref[...], v_ref[...]
        for r in range(tq // sub):
            rows = slice(r * sub, (r + 1) * sub)
            qv, dov = q_ref[rows, :], do_ref[rows, :]
            delta = jnp.sum(dov.astype(F32) * o_ref[rows, :].astype(F32), axis=-1, keepdims=True)
            p = jnp.exp(_dot_nt(qv, kv) - lse_ref[rows, 0:1])
            ds = (p * (_dot_nt(dov, vv) - delta)).astype(BF16)
            p_s[rows, :] = p.astype(BF16)
            ds_s[rows, :] = ds
            dq_o[rows, :] = _dot_nn(ds, kv).astype(dq_o.dtype)
        dk_acc[...] += _dot_tn(ds_s[...], q_ref[...])
        dv_acc[...] += _dot_tn(p_s[...], do_ref[...])

        @pl.when((pl.program_id(2) == grp - 1) & (pl.program_id(3) == nq - 1))
        def _():
            dk_o[...] = dk_acc[...].astype(dk_o.dtype)
            dv_o[...] = dv_acc[...].astype(dv_o.dtype)

        _comm_finish(comm, cin, cout, csem, grid)

    qspec = pl.BlockSpec((tq, HP), lambda b, hk, g, i: (b * nq + i, hk * grp + g))
    kspec = pl.BlockSpec((s_len, HP), lambda b, hk, g, i: (b, hk))
    return pl.pallas_call(
        body, name=name, grid=grid,
        out_shape=(jax.ShapeDtypeStruct((t, n_h * HP), BF16), jax.ShapeDtypeStruct((t, n_hk * HP), BF16),
                   jax.ShapeDtypeStruct((t, n_hk * HP), BF16), *c_outs),
        in_specs=[qspec, kspec, kspec, qspec, qspec, qspec, *c_in_specs],
        out_specs=(qspec, kspec, kspec, *([_ANY] * len(c_outs))),
        scratch_shapes=[pltpu.VMEM((tq, s_len), BF16), pltpu.VMEM((tq, s_len), BF16),
                        pltpu.VMEM((s_len, HP), F32), pltpu.VMEM((s_len, HP), F32), *c_sems],
        input_output_aliases=alias,
        compiler_params=_params(("arbitrary", "arbitrary", "arbitrary", "arbitrary")),
    )(q, k, v, do, o, lse, *c_ins)


def prep_bwd(dqa, dka, dva, dqb, dkb, dvb, zp, dz, tabs, g_qa, g_kva, g_qn, g_kn, w_qb_t, w_kvb_t, tm, s_len):
    t = zp.shape[0]
    nsb = s_len // tm
    scale_a = (QK_NOPE + QK_ROPE) ** -0.5
    scale_b = HD_B ** -0.5

    def body(dqa_ref, dka_ref, dva_ref, dqb_ref, dkb_ref, dvb_ref, qb_ref, qlat_ref, kb_ref, ckv_ref, tab_ref,
             gqa_ref, gkva_ref, gqn_ref, gkn_ref, wqb_ref, wkvb_ref, _, dz_o, dqap_o, dkva_o, st_o):
        dzq_o, dsm_o = dz_o.at[:, 0:1024], dz_o.at[:, 1024:2048]

        @pl.when(pl.program_id(0) == 0)
        def _():
            st_o[...] = jnp.zeros_like(st_o)

        ca, s1a, s2a = tab_ref[0], tab_ref[1], tab_ref[2]
        ck = tab_ref[3]
        cb, s1b, s2b = tab_ref[4], tab_ref[5], tab_ref[6]
        for h in range(H_A):
            sl = slice(h * HP, (h + 1) * HP)
            dqap_o[:, sl] = _rope_bwd(dqa_ref[:, sl].astype(F32) * scale_a, ca, s1a, s2a).astype(BF16)
        dcq = _dot_nn(dqap_o[...], wqb_ref[...])
        ql = qlat_ref[...]
        rq = _rstd(ql, Q_LORA)
        xh = ql * rq
        gqa = gqa_ref[...]
        st_o[0:1, :] += _colsum(dcq * xh)
        dsm_o[:, 0:256] = _rms_bwd(dcq, xh, rq, gqa, Q_LORA).astype(BF16)
        dkpe = jnp.zeros((tm, HP), F32)
        for h in range(H_A):
            sl = slice(h * HP, (h + 1) * HP)
            dk = dka_ref[:, sl]
            dkpe = dkpe + dk.astype(F32)
            dkva_o[:, sl] = dk.astype(BF16)
        dkva_o[:, H_A * HP:] = dva_ref[...].astype(BF16)
        dsm_o[:, 896:1024] = _rope_bwd(dkpe, ck, s1a, s2a).astype(BF16)
        dckv = _dot_nn(dkva_o[...], wkvb_ref[...])
        cr = ckv_ref[...]
        rk = _rstd(cr, KV_LORA)
        xh = cr * rk
        st_o[1:2, 0:128] += _colsum(dckv * xh)
        dsm_o[:, 768:896] = _rms_bwd(dckv, xh, rk, gkva_ref[...], KV_LORA).astype(BF16)
        gqn, gkn = gqn_ref[...], gkn_ref[...]
        dgq = jnp.zeros((1, HP), F32)
        for h in range(H_B):
            sl = slice(h * HP, (h + 1) * HP)
            dy = _rope_bwd(dqb_ref[:, sl].astype(F32) * scale_b, cb, s1b, s2b)
            xs = qb_ref[:, sl]
            r = _rstd(xs, HD_B)
            xh = xs * r
            dgq = dgq + _colsum(dy * xh)
            dzq_o[:, sl] = _rms_bwd(dy, xh, r, gqn, HD_B).astype(BF16)
        st_o[2:3, 0:128] += dgq
        dgk = jnp.zeros((1, HP), F32)
        for h in range(KV_B):
            sl = slice(h * HP, (h + 1) * HP)
            dy = _rope_bwd(dkb_ref[:, sl].astype(F32), cb, s1b, s2b)
            xs = kb_ref[:, sl]
            r = _rstd(xs, HD_B)
            xh = xs * r
            dgk = dgk + _colsum(dy * xh)
            dsm_o[:, 256 + h * HP:256 + (h + 1) * HP] = _rms_bwd(dy, xh, r, gkn, HD_B).astype(BF16)
        st_o[3:4, 0:128] += dgk
        dsm_o[:, 512:768] = dvb_ref[...].astype(BF16)

    return pl.pallas_call(
        body, name="prep_bwd", grid=(t // tm,),
        out_shape=(jax.ShapeDtypeStruct((t, ZP), BF16), jax.ShapeDtypeStruct((t, 1024), BF16),
                   jax.ShapeDtypeStruct((t, 2048), BF16), jax.ShapeDtypeStruct((4, 256), F32)),
        in_specs=[_rows(tm, 1024), _rows(tm, 1024), _rows(tm, 1024), _rows(tm, 1024), _rows(tm, 256), _rows(tm, 256),
                  _rows(tm, 1024, 2), _rows(tm, 256, 12), _rows(tm, 256, 13), _rows(tm, 128, 30),
                  pl.BlockSpec((7, tm, HP), lambda i: (0, i % nsb, 0)),
                  _resident((1, Q_LORA)), _resident((1, KV_LORA)), _resident((1, HP)), _resident((1, HP)),
                  _resident((H_A * HP, Q_LORA)), _resident((2 * H_A * HP, KV_LORA)), _ANY],
        out_specs=(_rows(tm, 2048, 1), _rows(tm, 1024), _rows(tm, 2048), pl.BlockSpec((4, 256), lambda i: (0, 0))),
        input_output_aliases={17: 0}, compiler_params=_params(("arbitrary",)),
    )(dqa, dka, dva, dqb, dkb, dvb, zp, zp, zp, zp, tabs, g_qa, g_kva, g_qn, g_kn, w_qb_t, w_kvb_t, dz)


def in_bwd(dz, x, dx1, g_mix, w_in_t, tm, comm=None):
    t = x.shape[0]
    grid = (t // tm,)
    c_ins, c_in_specs, c_outs, c_sems, alias = _comm_parts(comm, 5, 2)

    def body(*refs):
        (dz_ref, x_ref, dx1_ref, g_ref, w_ref), cin, (dx_o, st_o), cout, _, csem = _split_refs(refs, 5, 2, 0, comm)
        _comm_start(comm, cin, cout, csem, grid)

        @pl.when(pl.program_id(0) == 0)
        def _():
            st_o[...] = jnp.zeros_like(st_o)

        dh = _dot_nn(dz_ref[...], w_ref[...])
        xv = x_ref[...]
        g = g_ref[...]
        r = _rstd(xv, D_MODEL)
        xh = xv * r
        st_o[...] += _colsum(dh * xh)
        dx_o[...] = dx1_ref[...] + _rms_bwd(dh, xh, r, g, D_MODEL)
        _comm_finish(comm, cin, cout, csem, grid)

    return pl.pallas_call(
        body, name="in_bwd", grid=grid,
        out_shape=(jax.ShapeDtypeStruct((t, D_MODEL), F32), jax.ShapeDtypeStruct((1, D_MODEL), F32), *c_outs),
        in_specs=[_rows(tm, ZP), _rows(tm, D_MODEL), _rows(tm, D_MODEL),
                  _resident((1, D_MODEL)), _resident((ZP, D_MODEL)), *c_in_specs],
        out_specs=(_rows(tm, D_MODEL), pl.BlockSpec((1, D_MODEL), lambda i: (0, 0)), *([_ANY] * len(c_outs))),
        scratch_shapes=c_sems, input_output_aliases=alias,
        compiler_params=_params(("arbitrary",)),
    )(dz, x, dx1, g_mix, w_in_t, *c_ins)


def matmul_tn(a, b, name, square_a=False):
    t, m = a.shape
    n = b.shape[1]
    bm = min(m, 512)
    tk = min(t, 4096)

    def body(a_ref, b_ref, o_ref):
        @pl.when(pl.program_id(1) == 0)
        def _():
            o_ref[...] = jnp.zeros_like(o_ref)

        av = a_ref[...]
        if square_a:
            av = (av.astype(F32) * av.astype(F32))
        o_ref[...] += _dot_tn(av.astype(BF16), b_ref[...].astype(BF16))

    return pl.pallas_call(
        body, name=name, grid=(m // bm, t // tk), out_shape=jax.ShapeDtypeStruct((m, n), F32),
        in_specs=[pl.BlockSpec((tk, bm), lambda i, kk: (kk, i)), pl.BlockSpec((tk, n), lambda i, kk: (kk, 0))],
        out_specs=pl.BlockSpec((bm, n), lambda i, kk: (i, 0)),
        compiler_params=_params(("parallel", "arbitrary")),
    )(a, b)


def matmul_tn_packed(a, b, name, rows, row_off, total_rows, buf=None, square_a=False):
    t, m = a.shape
    n = b.shape[1]
    pd = max(1, 512 // rows)
    bm = pd * rows
    tk = min(t, 4096)
    nk = t // tk

    def body(a_ref, b_ref, *rest):
        o_ref, acc = rest[-2], rest[-1]

        @pl.when(pl.program_id(1) == 0)
        def _():
            acc[...] = jnp.zeros_like(acc)

        av = a_ref[...]
        if square_a:
            av = (av.astype(F32) * av.astype(F32))
        acc[...] += _dot_tn(av.astype(BF16), b_ref[...].astype(BF16))

        @pl.when(pl.program_id(1) == nk - 1)
        def _():
            o_ref[...] = acc[...].reshape(pd, rows, n).astype(o_ref.dtype)

    in_specs = [pl.BlockSpec((tk, bm), lambda i, kk: (kk, i)), pl.BlockSpec((tk, n), lambda i, kk: (kk, 0))]
    args = [a, b]
    if buf is not None:
        in_specs.append(_ANY)
        args.append(buf)
    return pl.pallas_call(
        body, name=name, grid=(m // bm, nk), out_shape=jax.ShapeDtypeStruct((N_DEV, total_rows, n), BF16),
        in_specs=in_specs, out_specs=pl.BlockSpec((pd, rows, n), lambda i, kk: (i, row_off // rows, 0)),
        scratch_shapes=[pltpu.VMEM((bm, n), F32)], input_output_aliases={2: 0} if buf is not None else {},
        compiler_params=_params(("parallel", "arbitrary")),
    )(*args)


def adamw(w, g, m, v, name, g_transposed=False):
    _, r, c = w.shape
    tr = 256 if (not g_transposed and r > 256 and r % 256 == 0) else r
    c1 = 1.0 - ADAM_B1 ** ADAM_STEP
    c2 = 1.0 - ADAM_B2 ** ADAM_STEP

    def body(w_ref, g_ref, m_ref, v_ref, g_o, d_o, m_o, v_o):
        gv = g_ref[...].T if g_transposed else g_ref[...]
        mn = ADAM_B1 * m_ref[0] + (1.0 - ADAM_B1) * gv
        vn = ADAM_B2 * v_ref[0] + (1.0 - ADAM_B2) * (gv * gv)
        g_o[0] = gv
        m_o[0] = mn
        v_o[0] = vn
        d_o[0] = -ADAM_LR * ((mn / c1) / (jnp.sqrt(vn / c2) + ADAM_EPS) + ADAM_WD * w_ref[0])

    spec = pl.BlockSpec((1, tr, c), lambda i: (0, i, 0))
    gspec = pl.BlockSpec((c, r), lambda i: (0, 0)) if g_transposed else pl.BlockSpec((tr, c), lambda i: (i, 0))
    shp = jax.ShapeDtypeStruct((1, r, c), F32)
    return pl.pallas_call(
        body, name=name, grid=(r // tr,), out_shape=(shp,) * 4, in_specs=[spec, gspec, spec, spec], out_specs=(spec,) * 4,
        compiler_params=_params(("parallel",)),
    )(w, g, m, v)


def _rope_tables(s_len):
    def angles(pos, dim):
        inv = np.float32(ROPE_THETA) ** (-np.arange(0, dim, 2, dtype=np.float32) / np.float32(dim))
        return pos.astype(np.float32)[:, None] * inv[None, :]

    tpos = np.arange(s_len)
    a1 = angles(tpos, QK_ROPE)
    ar = angles(tpos // GRID_W, HD_B // 2)
    ac = angles(tpos % GRID_W, HD_B // 2)
    z16 = np.zeros((s_len, 16), np.float32)
    z32 = np.zeros((s_len, 32), np.float32)
    z64 = np.zeros((s_len, 64), np.float32)
    one64 = np.ones((s_len, 64), np.float32)
    c1, s1 = np.cos(a1), np.sin(a1)
    ca = np.concatenate([one64, c1, c1, z32], axis=1)
    ck = np.concatenate([z64, c1, c1, z32], axis=1)
    s1a = np.concatenate([z64, -s1, z16, z32], axis=1)
    s2a = np.concatenate([z64, z16, s1, z32], axis=1)
    cr, sr, cc, sc = np.cos(ar), np.sin(ar), np.cos(ac), np.sin(ac)
    cb = np.concatenate([cr, cr, cc, cc, z64], axis=1)
    s1b = np.concatenate([-sr, z16, -sc, z16, z64], axis=1)
    s2b = np.concatenate([z16, sr, z16, sc, z64], axis=1)
    return jnp.asarray(np.stack([ca, s1a, s2a, ck, cb, s1b, s2b]).astype(np.float32))


def _pad_heads(a, n_heads, axis):
    shp = a.shape
    a = a.reshape(shp[:axis] + (n_heads, shp[axis] // n_heads) + shp[axis + 1:])
    pad = [(0, 0)] * a.ndim
    pad[axis + 1] = (0, HP - a.shape[axis + 1])
    a = jnp.pad(a, pad)
    return a.reshape(shp[:axis] + (n_heads * HP,) + shp[axis + 1:])


def _unpad_heads(a, n_heads, width, axis):
    shp = a.shape
    a = a.reshape(shp[:axis] + (n_heads, HP) + shp[axis + 1:])
    a = lax.slice_in_dim(a, 0, width, axis=axis + 1)
    return a.reshape(shp[:axis] + (n_heads * width,) + shp[axis + 1:])


def _pack_rows(blocks, names):
    parts = []
    for name in names:
        b = blocks[name]
        padr = PACK_ROWS[name] - b.shape[-2]
        if padr:
            b = jnp.pad(b, [(0, 0)] * (b.ndim - 2) + [(0, padr), (0, 0)])
        parts.append(b)
    return jnp.concatenate(parts, axis=parts[0].ndim - 2)


W_IN_COLS = 3232 // N_DEV
_Z_RUNS = ([(0, 256, 3072), (256, 384, 3840), (384, 416, 4032)]
           + [(416 + 64 * h, 480 + 64 * h, 2048 + HP * h) for h in range(H_B)]
           + [(928 + 64 * j, 992 + 64 * j, 3328 + HP * j) for j in range(KV_B)]
           + [(1056 + 64 * j, 1120 + 64 * j, 3584 + HP * j) for j in range(KV_B)]
           + [(1184, 2208, 0), (2208, 3232, 1024)])


def _expand_w_in(wt):
    parts, cursor = [], 0
    for z0, z1, zp0 in sorted(_Z_RUNS, key=lambda run: run[2]):
        if zp0 > cursor:
            parts.append(jnp.zeros((zp0 - cursor, D_MODEL), wt.dtype))
        parts.append(wt[z0:z1])
        cursor = zp0 + (z1 - z0)
    parts.append(jnp.zeros((ZP - cursor, D_MODEL), wt.dtype))
    return jnp.concatenate(parts, axis=0)


def _collapse_w_in(dw):
    blocks = []
    for d in range(N_DEV):
        lo, hi = d * W_IN_COLS, (d + 1) * W_IN_COLS
        parts = []
        for z0, z1, zp0 in sorted(_Z_RUNS):
            a, b = max(z0, lo), min(z1, hi)
            if a < b:
                parts.append(dw[zp0 + a - z0:zp0 + b - z0])
        parts.append(jnp.zeros((PACK_ROWS["w_in"] - W_IN_COLS, D_MODEL), dw.dtype))
        blocks.append(jnp.concatenate(parts, axis=0))
    return jnp.stack(blocks)


def kernel(x, p, g_mix, w_in, g_qa, w_qb, g_kva, w_kvb, g_qn, g_kn, w_oa, w_ob, w_o, g_mlp, w_up, w_down, g_ple, w_ple_gate, w_ple, g_final, loss_target, m_g_mix, m_w_in, m_g_qa, m_w_qb, m_g_kva, m_w_kvb, m_g_qn, m_g_kn, m_w_oa, m_w_ob, m_w_o, m_g_mlp, m_w_up, m_w_down, m_g_ple, m_w_ple_gate, m_w_ple, m_g_final, v_g_mix, v_w_in, v_g_qa, v_w_qb, v_g_kva, v_w_kvb, v_g_qn, v_g_kn, v_w_oa, v_w_ob, v_w_o, v_g_mlp, v_w_up, v_w_down, v_g_ple, v_w_ple_gate, v_w_ple, v_g_final):
    n_b, s_len, _ = x.shape
    t = n_b * s_len
    tm = min(512, s_len)
    tq_f = min(2048, s_len)
    tq_b = min(2048, s_len)

    mats = dict(w_in=(w_in, m_w_in, v_w_in), w_qb=(w_qb, m_w_qb, v_w_qb), w_kvb=(w_kvb, m_w_kvb, v_w_kvb),
                w_oa=(w_oa, m_w_oa, v_w_oa), w_ob=(w_ob, m_w_ob, v_w_ob), w_o=(w_o, m_w_o, v_w_o),
                w_up=(w_up, m_w_up, v_w_up), w_down=(w_down, m_w_down, v_w_down),
                w_ple_gate=(w_ple_gate, m_w_ple_gate, v_w_ple_gate), w_ple=(w_ple, m_w_ple, v_w_ple))
    col_sharded = ("w_in", "w_qb", "w_kvb", "w_oa", "w_ob", "w_up", "w_ple")

    blocks = {}
    for name in PACK_W1 + ("w_oa", "w_ob", "w_ple"):
        blocks[name] = mats[name][0][0].T.reshape(-1, D_MODEL).astype(BF16)
    off_w1, _ = _pack_offsets(PACK_W1)
    off_w2, _ = _pack_offsets(PACK_W2)
    off_w3, _ = _pack_offsets(PACK_W3)
    xf = x.reshape(t, D_MODEL)
    h, full1 = norm_x(xf, g_mix, tm, comm=gather_first_comm(_pack_rows(blocks, PACK_W1)))
    pack2, pack3, full1 = pack_late_weights(w_up, w_down, w_o, w_ple_gate, _pack_rows(blocks, ("w_oa", "w_ob", "w_ple")),
                                            comm=gather_pass_comm(full1))

    def gathered(full, offs, name, rows, width):
        return full[:, offs[name]:offs[name] + rows].reshape(-1, width)

    w_in_t = _expand_w_in(gathered(full1, off_w1, "w_in", W_IN_COLS, D_MODEL))
    w_qb_t = _pad_heads(gathered(full1, off_w1, "w_qb", 24, Q_LORA), H_A, 0)
    wkvb = gathered(full1, off_w1, "w_kvb", 16, KV_LORA).reshape(H_A, 2, 64, KV_LORA)
    w_kvb_t = jnp.concatenate([_pad_heads(wkvb[:, 0].reshape(-1, KV_LORA), H_A, 0),
                               _pad_heads(wkvb[:, 1].reshape(-1, KV_LORA), H_A, 0)], axis=0)

    tabs = _rope_tables(s_len)
    g_qn_p = jnp.pad(g_qn, ((0, 0), (0, HP - HD_B)))
    g_kn_p = jnp.pad(g_kn, ((0, 0), (0, HP - HD_B)))
    pf = p.reshape(t, PLE_DIM)
    tgt = loss_target.reshape(t, D_MODEL)

    zp, full2 = in_proj(h, w_in_t, tm, comm=gather_first_comm(pack2))
    qa, ka, va, qb, kb, vb, cq, ckv, full2 = attn_prep(zp, tabs, g_qa, g_kva, g_qn_p, g_kn_p, w_qb_t, w_kvb_t, tm, s_len,
                                                       comm=gather_pass_comm(full2))
    oa, lse_a, full3 = attn_fwd(qa, ka, va, n_b, s_len, tq_f, "attn_a_fwd", comm=gather_first_comm(pack3))
    ob, lse_b, full3 = attn_fwd(qb, kb, vb, n_b, s_len, tq_f, "attn_b_fwd", comm=gather_pass_comm(full3))
    w_oa_t = _pad_heads(gathered(full2, off_w2, "w_oa", 64, H_A * V_DIM_A), H_A, 1)
    w_ob_t = _pad_heads(gathered(full2, off_w2, "w_ob", 64, H_B * HD_B), H_B, 1)
    w_ple_t = gathered(full2, off_w2, "w_ple", 32, PLE_DIM)
    x1, merged, yab = merge_fwd(oa, ob, zp, xf, w_oa_t, w_ob_t, full2, off_w2, tm)
    x2, u = mlp_fwd(x1, g_mlp, full3, off_w3, tm)
    dx2, dt, h3, dpe, st_ple, dx2b = ple_loss_bwd(x2, pf, tgt, g_ple, g_final.reshape(1, D_MODEL), full2, off_w2, w_ple_t, tm)
    dx1, da, h2, st_mlp, dx1b = mlp_bwd(dx2, x1, u, g_mlp, full3, off_w3, tm)

    core = lax.axis_index("c").astype(jnp.int32).reshape(1)
    chip = (2 * lax.axis_index("x") + lax.axis_index("y")).astype(jnp.int32).reshape(1)

    def packed(gblocks, names):
        return _pack_rows({n: gblocks[n].reshape(N_DEV, -1, D_MODEL).astype(BF16) for n in names}, names)

    off_g1, rows_g1 = _pack_offsets(PACK_G1)
    gpack1 = matmul_tn_packed(da, h2, "gw_up", 512, off_g1["w_up"], rows_g1)
    gpack1 = matmul_tn_packed(u, dx2b, "gw_down", 512, off_g1["w_down"], rows_g1, buf=gpack1, square_a=True)
    gpack1 = matmul_tn_packed(h3, dt, "gw_pg", 128, off_g1["w_ple_gate"], rows_g1, buf=gpack1)
    gple = matmul_tn(dpe, pf, "gw_ple").reshape(N_DEV, -1, D_MODEL).astype(BF16)
    gpack1 = lax.dynamic_update_slice(gpack1, gple, (0, off_g1["w_ple"], 0))
    doa, dob, dz, dya, dyb, got1 = merge_bwd(dx1b, yab, zp, w_oa_t, w_ob_t, full2, off_w2, tm,
                                               comm=scatter_sibling_comm(gpack1))
    part1 = add_pairs(gpack1, got1, core)
    dqa, dka, dva, land1 = attn_bwd(qa, ka, va, doa, oa, lse_a, n_b, s_len, tq_b, "attn_a_bwd", comm=scatter_chips_comm(part1))
    gshard1 = sum_chips(part1, land1, chip)

    off_g2, rows_g2 = _pack_offsets(PACK_G2)
    g2 = dict(w_oa=_unpad_heads(matmul_tn(dya, oa, "gw_oa"), H_A, V_DIM_A, 1),
              w_ob=_unpad_heads(matmul_tn(dyb, ob, "gw_ob"), H_B, HD_B, 1))
    gpack2 = matmul_tn_packed(merged, dx1b, "gw_o", 128, off_g2["w_o"], rows_g2)
    gpack2 = lax.dynamic_update_slice(gpack2, packed(g2, ("w_oa", "w_ob")), (0, off_g2["w_oa"], 0))
    part2 = add_pairs(gpack2, exchange_sibling(gpack2), core)
    dqb, dkb, dvb, land2 = attn_bwd(qb, kb, vb, dob, ob, lse_b, n_b, s_len, tq_b, "attn_b_bwd", comm=scatter_chips_comm(part2))
    gshard2 = sum_chips(part2, land2, chip)
    dz, dqap, dkva, st_prep = prep_bwd(dqa, dka, dva, dqb, dkb, dvb, zp, dz, tabs, g_qa, g_kva, g_qn_p, g_kn_p,
                                       w_qb_t, w_kvb_t, tm, s_len)

    gkv = matmul_tn(dkva, ckv, "gw_kvb")
    g3 = dict(
        w_in=_collapse_w_in(matmul_tn(dz, h, "gw_in")),
        w_qb=_unpad_heads(matmul_tn(dqap, cq, "gw_qb"), H_A, QK_NOPE + QK_ROPE, 0),
        w_kvb=jnp.stack([_unpad_heads(gkv[:H_A * HP], H_A, 64, 0).reshape(H_A, 64, KV_LORA),
                         _unpad_heads(gkv[H_A * HP:], H_A, 64, 0).reshape(H_A, 64, KV_LORA)], axis=1))
    gpack3 = packed(g3, PACK_G3)
    part3 = add_pairs(gpack3, exchange_sibling(gpack3), core)
    grad_x, st_mix, land3 = in_bwd(dz, xf, dx1, g_mix, w_in_t, tm, comm=scatter_chips_comm(part3))
    gshard3 = sum_chips(part3, land3, chip)
    off_g3, _ = _pack_offsets(PACK_G3)
    shards = {n: (gshard1, off_g1[n]) for n in PACK_G1}
    shards.update({n: (gshard2, off_g2[n]) for n in PACK_G2})
    shards.update({n: (gshard3, off_g3[n]) for n in PACK_G3})

    stats = allreduce_stats(st_mix, st_prep, st_mlp, st_ple)
    loss = jnp.sum(stats[ST_LOSS])

    out_g, out_d, out_m, out_v = {}, {}, {}, {}
    for name, (w, m, v) in mats.items():
        gshard, off = shards[name]
        r, c = w.shape[1:]
        if name in col_sharded:
            g2 = gshard[off:off + (r * c) // D_MODEL].reshape(c, r)
            if r % 128 == 0 and c % 128 == 0:
                res = adamw(w, g2, m, v, "adamw_" + name, g_transposed=True)
            else:
                res = adamw(w[0].T[None], g2, m[0].T[None], v[0].T[None], "adamw_" + name)
                res = tuple(a[0].T[None] for a in res)
        else:
            res = adamw(w, gshard[off:off + r], m, v, "adamw_" + name)
        out_g[name], out_d[name], out_m[name], out_v[name] = res

    gains = (("g_mix", g_mix, m_g_mix, v_g_mix, ST_G_MIX), ("g_qa", g_qa, m_g_qa, v_g_qa, ST_G_QA),
             ("g_kva", g_kva, m_g_kva, v_g_kva, ST_G_KVA), ("g_qn", g_qn, m_g_qn, v_g_qn, ST_G_QN),
             ("g_kn", g_kn, m_g_kn, v_g_kn, ST_G_KN), ("g_mlp", g_mlp, m_g_mlp, v_g_mlp, ST_G_MLP),
             ("g_ple", g_ple, m_g_ple, v_g_ple, ST_G_PLE), ("g_final", g_final, m_g_final, v_g_final, ST_G_FINAL))
    res = adamw_gains(stats, [(r_, w.reshape(1, -1), m.reshape(1, -1), v.reshape(1, -1)) for _, w, m, v, r_ in gains])
    for (name, w, _, _, _), (gg, gd, gm, gv) in zip(gains, res):
        out_g[name], out_d[name], out_m[name], out_v[name] = (a.reshape(w.shape) for a in (gg, gd, gm, gv))

    order = ("g_mix", "w_in", "g_qa", "w_qb", "g_kva", "w_kvb", "g_qn", "g_kn", "w_oa", "w_ob", "w_o", "g_mlp",
             "w_up", "w_down", "g_ple", "w_ple_gate", "w_ple", "g_final")
    return (loss, grad_x.reshape(x.shape), *[out_g[n] for n in order], *[out_d[n] for n in order],
            *[out_m[n] for n in order], *[out_v[n] for n in order])
```

```python
import numpy as np
import jax
import jax.numpy as jnp
from jax import lax
from jax.experimental import pallas as pl
from jax.experimental.pallas import tpu as pltpu

F32 = jnp.float32
BF16 = jnp.bfloat16

D_MODEL = 1024
EPS = 1e-6
ROPE_THETA = 10000.0
GRID_W = 64
H_A = 8
QK_NOPE = 64
QK_ROPE = 32
V_DIM_A = 64
Q_LORA = 256
KV_LORA = 128
H_B = 8
KV_B = 2
HD_B = 64
D_FF = 4 * D_MODEL
PLE_DIM = 256
HP = 128
ZP = 4096
N_DEV = 8
N_CHIP = 4

ADAM_LR = 0.001
ADAM_B1 = 0.9
ADAM_B2 = 0.999
ADAM_EPS = 1e-08
ADAM_WD = 0.01
ADAM_STEP = 10

VMEM_LIMIT = 52 * 1024 * 1024

PACK_ROWS = dict(w_in=416, w_qb=32, w_kvb=16, w_oa=64, w_ob=64, w_o=128, w_up=512, w_down=512, w_ple_gate=128, w_ple=32)
PACK_W1 = ("w_in", "w_qb", "w_kvb")
PACK_W2 = ("w_o", "w_ple_gate", "w_oa", "w_ob", "w_ple")
PACK_W3 = ("w_up", "w_down")
PACK_G1 = ("w_up", "w_down", "w_ple_gate", "w_ple")
PACK_G2 = ("w_o", "w_oa", "w_ob")
PACK_G3 = ("w_in", "w_qb", "w_kvb")


def _pack_offsets(names):
    off, o = {}, 0
    for n in names:
        off[n] = o
        o += PACK_ROWS[n]
    return off, o

ST_G_MIX, ST_G_QA, ST_G_KVA, ST_G_QN, ST_G_KN, ST_G_MLP, ST_G_PLE, ST_G_FINAL, ST_LOSS = range(9)
ST_ROWS = 16


def _dot_nn(a, b):
    return lax.dot_general(a, b, (((1,), (0,)), ((), ())), preferred_element_type=F32)


def _dot_nt(a, b):
    return lax.dot_general(a, b, (((1,), (1,)), ((), ())), preferred_element_type=F32)


def _dot_tn(a, b):
    return lax.dot_general(a, b, (((0,), (0,)), ((), ())), preferred_element_type=F32)


def _rstd(x, n):
    return lax.rsqrt(jnp.sum(x * x, axis=-1, keepdims=True) * (1.0 / n) + EPS)


def _rms_bwd(dy, xh, r, g, n):
    dxh = dy * g
    return r * (dxh - xh * (jnp.sum(dxh * xh, axis=-1, keepdims=True) * (1.0 / n)))


def _rope_fwd(x, c, s1, s2):
    return x * c + pltpu.roll(x, HP - 16, 1) * s1 + pltpu.roll(x, 16, 1) * s2


def _rope_bwd(d, c, s1, s2):
    return d * c + pltpu.roll(d * s1, 16, 1) + pltpu.roll(d * s2, HP - 16, 1)


def _colsum(v):
    return jnp.sum(v, axis=0, keepdims=True)


def _params(sem=None, vmem=VMEM_LIMIT):
    return pltpu.CompilerParams(dimension_semantics=sem, vmem_limit_bytes=vmem)


def _resident(shape):
    nd = len(shape)
    return pl.BlockSpec(shape, lambda *_: (0,) * nd, pipeline_mode=pl.Buffered(1))


def _rows(tm, width, col=0):
    return pl.BlockSpec((tm, width), lambda i: (i, col))


def _packed_weight(rows, off):
    return pl.BlockSpec((N_DEV, rows, D_MODEL), lambda *_: (0, off // rows, 0), pipeline_mode=pl.Buffered(1))


def _wrows(ref, start, size):
    rows = ref.shape[1]
    return ref[start // rows:(start + size) // rows].reshape(size, D_MODEL)


def _mesh_pos():
    return lax.axis_index("x"), lax.axis_index("y"), lax.axis_index("c")


def _flip(v, bit):
    return (1 - v) if bit else v


_ANY = pl.BlockSpec(memory_space=pl.ANY)
_MESH = pl.DeviceIdType.MESH


def _remote(src, dst, send_sems, recv_sems, k, to):
    return pltpu.make_async_remote_copy(src_ref=src, dst_ref=dst, send_sem=send_sems.at[k], recv_sem=recv_sems.at[k],
                                        device_id=to, device_id_type=_MESH)


def _sibling_copies(g_ref, got_ref, send_sems, recv_sems):
    x, y, c = _mesh_pos()
    return [_remote(g_ref.at[2 * j + (1 - c)], got_ref.at[j], send_sems, recv_sems, j, (x, y, 1 - c)) for j in range(N_CHIP)]


def _chip_copies(p_ref, land_ref, send_sems, recv_sems):
    x, y, c = _mesh_pos()
    copies = []
    for k in (1, 2, 3):
        tx, ty = _flip(x, k & 2), _flip(y, k & 1)
        copies.append(_remote(p_ref.at[2 * tx + ty], land_ref.at[k - 1], send_sems, recv_sems, k - 1, (tx, ty, c)))
    return copies


class _Comm:
    def __init__(self, ins, out_shapes, sems, make, aliases=None):
        self.ins, self.out_shapes, self.sems, self.make, self.aliases = list(ins), list(out_shapes), list(sems), make, aliases or {}


def _comm_parts(comm, n_in, n_out):
    if comm is None:
        return [], [], [], [], {}
    alias = {n_in + j: n_out + k for j, k in comm.aliases.items()}
    return comm.ins, [_ANY] * len(comm.ins), comm.out_shapes, comm.sems, alias


def _split_refs(refs, n_in, n_out, n_scratch, comm):
    n_ci = len(comm.ins) if comm else 0
    n_co = len(comm.out_shapes) if comm else 0
    cuts, i = [], 0
    for n in (n_in, n_ci, n_out, n_co, n_scratch):
        cuts.append(refs[i:i + n])
        i += n
    return (*cuts, refs[i:])


def _grid_edge(grid, last):
    cond = None
    for d, n in enumerate(grid):
        here = pl.program_id(d) == (n - 1 if last else 0)
        cond = here if cond is None else cond & here
    return cond


def _comm_start(comm, cin, cout, csem, grid):
    if comm is not None:
        @pl.when(_grid_edge(grid, False))
        def _():
            for cp in comm.make(cin, cout, csem):
                cp.start()


def _comm_finish(comm, cin, cout, csem, grid):
    if comm is not None:
        @pl.when(_grid_edge(grid, True))
        def _():
            for cp in comm.make(cin, cout, csem):
                cp.wait()


def gather_first_comm(shard):
    r, w = shard.shape

    def make(cin, cout, sems):
        (x_ref,), (out_ref,), (send_sems, recv_sems, local_sem) = cin, cout, sems
        x, y, c = _mesh_pos()
        mine = out_ref.at[4 * x + 2 * y + c]
        targets = [(x, y, 1 - c), (1 - x, y, c), (x, 1 - y, c), (1 - x, 1 - y, c)]
        return [_remote(x_ref, mine, send_sems, recv_sems, k, to) for k, to in enumerate(targets)] + [
            pltpu.make_async_copy(x_ref, mine, local_sem)]

    return _Comm([shard], [jax.ShapeDtypeStruct((N_DEV, r, w), shard.dtype)],
                 [pltpu.SemaphoreType.DMA((4,)), pltpu.SemaphoreType.DMA((4,)), pltpu.SemaphoreType.DMA], make)


def gather_pass_comm(full):
    def make(cin, cout, sems):
        (in_ref,), (out_ref,), (send_sems, recv_sems) = cin, cout, sems
        x, y, c = _mesh_pos()
        copies = []
        for k, (px, py) in enumerate([(1 - x, y), (x, 1 - y), (1 - x, 1 - y)]):
            idx = 4 * px + 2 * py + c
            copies.append(_remote(in_ref.at[idx], out_ref.at[idx], send_sems, recv_sems, k, (x, y, 1 - c)))
        return copies

    return _Comm([full], [jax.ShapeDtypeStruct(full.shape, full.dtype)],
                 [pltpu.SemaphoreType.DMA((3,)), pltpu.SemaphoreType.DMA((3,))], make, aliases={0: 0})


def scatter_sibling_comm(g):
    _, r, w = g.shape
    return _Comm([g], [jax.ShapeDtypeStruct((N_CHIP, r, w), g.dtype)],
                 [pltpu.SemaphoreType.DMA((N_CHIP,)), pltpu.SemaphoreType.DMA((N_CHIP,))],
                 lambda cin, cout, sems: _sibling_copies(cin[0], cout[0], sems[0], sems[1]))


def scatter_chips_comm(part):
    _, r, w = part.shape
    return _Comm([part], [jax.ShapeDtypeStruct((N_CHIP - 1, r, w), part.dtype)],
                 [pltpu.SemaphoreType.DMA((3,)), pltpu.SemaphoreType.DMA((3,))],
                 lambda cin, cout, sems: _chip_copies(cin[0], cout[0], sems[0], sems[1]))


def exchange_sibling(g):
    _, r, w = g.shape

    def body(g_ref, got_ref, send_sems, recv_sems):
        copies = _sibling_copies(g_ref, got_ref, send_sems, recv_sems)
        for cp in copies:
            cp.start()
        for cp in copies:
            cp.wait()

    return pl.pallas_call(
        body, name="exchange_sibling", out_shape=jax.ShapeDtypeStruct((N_CHIP, r, w), g.dtype),
        in_specs=[_ANY], out_specs=_ANY,
        scratch_shapes=[pltpu.SemaphoreType.DMA((N_CHIP,)), pltpu.SemaphoreType.DMA((N_CHIP,))],
    )(g)


def exchange_chips(part):
    _, r, w = part.shape

    def body(p_ref, land_ref, send_sems, recv_sems):
        copies = _chip_copies(p_ref, land_ref, send_sems, recv_sems)
        for cp in copies:
            cp.start()
        for cp in copies:
            cp.wait()

    return pl.pallas_call(
        body, name="exchange_chips", out_shape=jax.ShapeDtypeStruct((N_CHIP - 1, r, w), part.dtype),
        in_specs=[_ANY], out_specs=_ANY,
        scratch_shapes=[pltpu.SemaphoreType.DMA((3,)), pltpu.SemaphoreType.DMA((3,))],
    )(part)


def allreduce_stats(st_mix, st_prep, st_mlp, st_ple):
    def body(mix_ref, prep_ref, mlp_ref, ple_ref, out_ref, mine, gath, send_sems, recv_sems):
        x, y, c = _mesh_pos()
        me = 4 * x + 2 * y + c
        mine[...] = jnp.zeros_like(mine)
        mine[ST_G_MIX:ST_G_MIX + 1, :] = mix_ref[...]
        mine[ST_G_QA:ST_G_KN + 1, 0:256] = prep_ref[...]
        mine[ST_G_MLP:ST_G_MLP + 1, :] = mlp_ref[...]
        mine[ST_G_PLE:ST_LOSS + 1, :] = ple_ref[...]
        gath[me] = mine[...]
        copies = []
        for k in range(1, N_DEV):
            peer = (_flip(x, k & 4), _flip(y, k & 2), _flip(c, k & 1))
            copies.append(_remote(mine, gath.at[me], send_sems, recv_sems, k - 1, peer))
        for cp in copies:
            cp.start()
        for cp in copies:
            cp.wait()
        acc = gath[0]
        for d in range(1, N_DEV):
            acc = acc + gath[d]
        out_ref[...] = acc

    vm = pl.BlockSpec(memory_space=pltpu.VMEM)
    return pl.pallas_call(
        body, name="allreduce_stats", out_shape=jax.ShapeDtypeStruct((ST_ROWS, D_MODEL), F32),
        in_specs=[vm] * 4, out_specs=vm,
        scratch_shapes=[pltpu.VMEM((ST_ROWS, D_MODEL), F32), pltpu.VMEM((N_DEV, ST_ROWS, D_MODEL), F32),
                        pltpu.SemaphoreType.DMA((N_DEV - 1,)), pltpu.SemaphoreType.DMA((N_DEV - 1,))],
    )(st_mix, st_prep, st_mlp, st_ple)


def adamw_gains(stats, gains):
    c1 = 1.0 - ADAM_B1 ** ADAM_STEP
    c2 = 1.0 - ADAM_B2 ** ADAM_STEP
    n = len(gains)

    def body(st_ref, *refs):
        ins, outs = refs[:3 * n], refs[3 * n:]
        for i, (row, w, _, _) in enumerate(gains):
            width = w.shape[1]
            gv = st_ref[row:row + 1, 0:width]
            mn = ADAM_B1 * ins[3 * i + 1][...] + (1.0 - ADAM_B1) * gv
            vn = ADAM_B2 * ins[3 * i + 2][...] + (1.0 - ADAM_B2) * (gv * gv)
            outs[4 * i][...] = gv
            outs[4 * i + 1][...] = -ADAM_LR * ((mn / c1) / (jnp.sqrt(vn / c2) + ADAM_EPS) + ADAM_WD * ins[3 * i][...])
            outs[4 * i + 2][...] = mn
            outs[4 * i + 3][...] = vn

    vm = pl.BlockSpec(memory_space=pltpu.VMEM)
    flat = [a for (_, w, m, v) in gains for a in (w, m, v)]
    out_shape = tuple(jax.ShapeDtypeStruct(w.shape, F32) for (_, w, _, _) in gains for _ in range(4))
    res = pl.pallas_call(body, name="adamw_gains", out_shape=out_shape, in_specs=[vm] * (1 + 3 * n),
                         out_specs=tuple([vm] * (4 * n)))(stats, *flat)
    return [res[4 * i:4 * i + 4] for i in range(n)]


def _row_tile(r, cap=640):
    return max(d for d in range(16, min(r, cap) + 1, 16) if r % d == 0)


def add_pairs(g, got, core):
    n, r, w = got.shape
    tr = _row_tile(r)

    def body(c_ref, a_ref, b_ref, o_ref):
        o_ref[...] = (a_ref[...].astype(F32) + b_ref[...].astype(F32)).astype(o_ref.dtype)

    spec = pl.BlockSpec((1, tr, w), lambda i, j, c: (i, j, 0))
    return pl.pallas_call(
        body, name="add_pairs", out_shape=jax.ShapeDtypeStruct(got.shape, got.dtype),
        grid_spec=pltpu.PrefetchScalarGridSpec(
            num_scalar_prefetch=1, grid=(n, r // tr),
            in_specs=[pl.BlockSpec((1, tr, w), lambda i, j, c: (2 * i + c[0], j, 0)), spec], out_specs=spec),
        compiler_params=_params(("parallel", "parallel")),
    )(core, g, got)


def sum_chips(part, land, chip):
    _, r, w = part.shape
    tr = _row_tile(r)

    def body(c_ref, p_ref, l_ref, o_ref):
        acc = p_ref[0].astype(F32)
        for s in range(N_CHIP - 1):
            acc = acc + l_ref[s].astype(F32)
        o_ref[...] = acc

    return pl.pallas_call(
        body, name="sum_chips", out_shape=jax.ShapeDtypeStruct((r, w), F32),
        grid_spec=pltpu.PrefetchScalarGridSpec(
            num_scalar_prefetch=1, grid=(r // tr,),
            in_specs=[pl.BlockSpec((1, tr, w), lambda i, c: (c[0], i, 0)), pl.BlockSpec((N_CHIP - 1, tr, w), lambda i, c: (0, i, 0))],
            out_specs=pl.BlockSpec((tr, w), lambda i, c: (i, 0))),
        compiler_params=_params(("parallel",)),
    )(chip, part, land)


def norm_x(x, g_mix, tm, comm=None):
    t = x.shape[0]
    grid = (t // tm,)
    c_ins, c_in_specs, c_outs, c_sems, alias = _comm_parts(comm, 2, 1)

    def body(*refs):
        (x_ref, g_ref), cin, (h_ref,), cout, _, csem = _split_refs(refs, 2, 1, 0, comm)
        _comm_start(comm, cin, cout, csem, grid)
        xv = x_ref[...]
        h_ref[...] = (xv * _rstd(xv, D_MODEL) * g_ref[...]).astype(BF16)
        _comm_finish(comm, cin, cout, csem, grid)

    return pl.pallas_call(
        body, name="norm_x", grid=grid, out_shape=(jax.ShapeDtypeStruct((t, D_MODEL), BF16), *c_outs),
        in_specs=[_rows(tm, D_MODEL), _resident((1, D_MODEL)), *c_in_specs],
        out_specs=(_rows(tm, D_MODEL), *([_ANY] * len(c_outs))),
        scratch_shapes=c_sems, input_output_aliases=alias, compiler_params=_params(("arbitrary",)),
    )(x, g_mix, *c_ins)


def pack_late_weights(w_up, w_down, w_o, w_pg, small, comm=None):
    rows2 = sum(PACK_ROWS[n] for n in PACK_W2)
    rows3 = sum(PACK_ROWS[n] for n in PACK_W3)
    c_ins, c_in_specs, c_outs, c_sems, alias = _comm_parts(comm, 5, 2)
    grid = (1,)

    def body(*refs):
        (up_ref, dn_ref, o_ref, pg_ref, sm_ref), cin, (p2_ref, p3_ref), cout, _, csem = _split_refs(refs, 5, 2, 0, comm)
        _comm_start(comm, cin, cout, csem, grid)
        p2_ref[0:128, :] = o_ref[0].astype(BF16)
        p2_ref[128:256, :] = pg_ref[0].astype(BF16)
        p2_ref[256:rows2, :] = sm_ref[...]
        p3_ref[0:512, :] = up_ref[0].T.astype(BF16)
        p3_ref[512:1024, :] = dn_ref[0].astype(BF16)
        _comm_finish(comm, cin, cout, csem, grid)

    def whole(a):
        nd = a.ndim
        return pl.BlockSpec(a.shape, lambda i: (0,) * nd)

    args = (w_up, w_down, w_o, w_pg, small)
    return pl.pallas_call(
        body, name="pack_late_weights", grid=grid,
        out_shape=(jax.ShapeDtypeStruct((rows2, D_MODEL), BF16), jax.ShapeDtypeStruct((rows3, D_MODEL), BF16), *c_outs),
        in_specs=[*[whole(a) for a in args], *c_in_specs],
        out_specs=(pl.BlockSpec((rows2, D_MODEL), lambda i: (0, 0)), pl.BlockSpec((rows3, D_MODEL), lambda i: (0, 0)),
                   *([_ANY] * len(c_outs))),
        scratch_shapes=c_sems, input_output_aliases=alias, compiler_params=_params(("arbitrary",)),
    )(*args, *c_ins)


def in_proj(h, w_in_t, tm, comm=None):
    t = h.shape[0]
    nc = 512
    grid = (t // tm,)
    c_ins, c_in_specs, c_outs, c_sems, alias = _comm_parts(comm, 2, 1)

    def body(*refs):
        (h_ref, w_ref), cin, (z_ref,), cout, _, csem = _split_refs(refs, 2, 1, 0, comm)
        _comm_start(comm, cin, cout, csem, grid)
        hv = h_ref[...]
        for cidx in range(ZP // nc):
            z_ref[:, cidx * nc:(cidx + 1) * nc] = _dot_nt(hv, w_ref[cidx * nc:(cidx + 1) * nc, :])
        _comm_finish(comm, cin, cout, csem, grid)

    return pl.pallas_call(
        body, name="in_proj", grid=grid, out_shape=(jax.ShapeDtypeStruct((t, ZP), F32), *c_outs),
        in_specs=[_rows(tm, D_MODEL), _resident((ZP, D_MODEL)), *c_in_specs],
        out_specs=(_rows(tm, ZP), *([_ANY] * len(c_outs))),
        scratch_shapes=c_sems, input_output_aliases=alias, compiler_params=_params(("arbitrary",)),
    )(h, w_in_t, *c_ins)


def attn_prep(zp, tabs, g_qa, g_kva, g_qn, g_kn, w_qb_t, w_kvb_t, tm, s_len, comm=None):
    t = zp.shape[0]
    nsb = s_len // tm
    scale_a = (QK_NOPE + QK_ROPE) ** -0.5
    scale_b = HD_B ** -0.5

    grid = (t // tm,)
    c_ins, c_in_specs, c_outs, c_sems, alias = _comm_parts(comm, 13, 8)

    def body(*refs):
        ((qb_ref, qlat_ref, kb_ref, vb_ref, ckv_ref, kpe_ref, tab_ref, gqa_ref, gkva_ref, gqn_ref, gkn_ref, wqb_ref,
          wkvb_ref), cin, (qa_o, ka_o, va_o, qb_o, kb_o, vb_o, cq_o, ckvn_o), cout, _, csem) = _split_refs(refs, 13, 8, 0, comm)
        _comm_start(comm, cin, cout, csem, grid)
        ca, s1a, s2a = tab_ref[0], tab_ref[1], tab_ref[2]
        ck = tab_ref[3]
        cb, s1b, s2b = tab_ref[4], tab_ref[5], tab_ref[6]
        ql = qlat_ref[...]
        cq = (ql * _rstd(ql, Q_LORA) * gqa_ref[...]).astype(BF16)
        cq_o[...] = cq
        qa = _dot_nt(cq, wqb_ref[...])
        slabs = [slice(h * HP, (h + 1) * HP) for h in range(H_A)]
        qa_o[...] = jnp.concatenate(
            [(_rope_fwd(qa[:, sl], ca, s1a, s2a) * scale_a).astype(BF16) for sl in slabs], axis=1)
        cr = ckv_ref[...]
        ckv = (cr * _rstd(cr, KV_LORA) * gkva_ref[...]).astype(BF16)
        ckvn_o[...] = ckv
        kva = _dot_nt(ckv, wkvb_ref[...])
        kpe = _rope_fwd(kpe_ref[...], ck, s1a, s2a)
        ka_o[...] = jnp.concatenate([(kva[:, sl] + kpe).astype(BF16) for sl in slabs], axis=1)
        va_o[...] = kva[:, H_A * HP:].astype(BF16)
        gqn, gkn = gqn_ref[...], gkn_ref[...]

        def norm_rope(ref, sl, g, scale):
            xs = ref[:, sl]
            y = _rope_fwd(xs * _rstd(xs, HD_B) * g, cb, s1b, s2b)
            return (y if scale is None else y * scale).astype(BF16)

        qb_o[...] = jnp.concatenate([norm_rope(qb_ref, sl, gqn, scale_b) for sl in slabs], axis=1)
        kb_o[...] = jnp.concatenate([norm_rope(kb_ref, sl, gkn, None) for sl in slabs[:KV_B]], axis=1)
        vb_o[...] = vb_ref[...].astype(BF16)
        _comm_finish(comm, cin, cout, csem, grid)

    def o(width):
        return jax.ShapeDtypeStruct((t, width), BF16)

    return pl.pallas_call(
        body, name="attn_prep", grid=grid,
        out_shape=(o(H_A * HP), o(H_A * HP), o(H_A * HP), o(H_B * HP), o(KV_B * HP), o(KV_B * HP), o(Q_LORA), o(KV_LORA),
                   *c_outs),
        in_specs=[_rows(tm, 1024, 2), _rows(tm, 256, 12), _rows(tm, 256, 13), _rows(tm, 256, 14),
                  _rows(tm, 128, 30), _rows(tm, 128, 31),
                  pl.BlockSpec((7, tm, HP), lambda i: (0, i % nsb, 0)),
                  _resident((1, Q_LORA)), _resident((1, KV_LORA)), _resident((1, HP)), _resident((1, HP)),
                  _resident((H_A * HP, Q_LORA)), _resident((2 * H_A * HP, KV_LORA)), *c_in_specs],
        out_specs=(_rows(tm, H_A * HP), _rows(tm, H_A * HP), _rows(tm, H_A * HP), _rows(tm, H_B * HP),
                   _rows(tm, KV_B * HP), _rows(tm, KV_B * HP), _rows(tm, Q_LORA), _rows(tm, KV_LORA), *([_ANY] * len(c_outs))),
        scratch_shapes=c_sems, input_output_aliases=alias, compiler_params=_params(("arbitrary",)),
    )(zp, zp, zp, zp, zp, zp, tabs, g_qa, g_kva, g_qn, g_kn, w_qb_t, w_kvb_t, *c_ins)


def attn_fwd(q, k, v, n_b, s_len, tq, name, comm=None):
    t = q.shape[0]
    n_h, n_hk = q.shape[1] // HP, k.shape[1] // HP
    grp = n_h // n_hk
    nq = s_len // tq
    sub = min(tq, 256)
    grid = (n_b, n_h, nq)
    c_ins, c_in_specs, c_outs, c_sems, alias = _comm_parts(comm, 3, 2)

    def body(*refs):
        (q_ref, k_ref, v_ref), cin, (o_ref, lse_ref), cout, _, csem = _split_refs(refs, 3, 2, 0, comm)
        _comm_start(comm, cin, cout, csem, grid)
        kv, vv = k_ref[...], v_ref[...]
        for r in range(tq // sub):
            rows = slice(r * sub, (r + 1) * sub)
            s = _dot_nt(q_ref[rows, :], kv)
            m = jnp.max(s, axis=-1, keepdims=True)
            p = jnp.exp(s - m)
            l = jnp.sum(p, axis=-1, keepdims=True)
            o_ref[rows, :] = (_dot_nn(p.astype(BF16), vv) * (1.0 / l)).astype(o_ref.dtype)
            lse_ref[rows, :] = jnp.broadcast_to(m + jnp.log(l), (sub, HP))
        _comm_finish(comm, cin, cout, csem, grid)

    qspec = pl.BlockSpec((tq, HP), lambda b, h, i: (b * nq + i, h))
    kspec = pl.BlockSpec((s_len, HP), lambda b, h, i: (b, h // grp))
    return pl.pallas_call(
        body, name=name, grid=grid,
        out_shape=(jax.ShapeDtypeStruct((t, n_h * HP), BF16), jax.ShapeDtypeStruct((t, n_h * HP), F32), *c_outs),
        in_specs=[qspec, kspec, kspec, *c_in_specs], out_specs=(qspec, qspec, *([_ANY] * len(c_outs))),
        scratch_shapes=c_sems, input_output_aliases=alias,
        compiler_params=_params(("arbitrary", "arbitrary", "arbitrary")),
    )(q, k, v, *c_ins)


def merge_fwd(oa, ob, zp, x, w_oa_t, w_ob_t, wpack, off, tm):
    t = x.shape[0]

    def body(oa_ref, ob_ref, ga_ref, gb_ref, x_ref, woa_ref, wob_ref, wo_ref, x1_o, mg_o, y_o):
        ya = _dot_nt(oa_ref[...], woa_ref[...])
        yb = _dot_nt(ob_ref[...], wob_ref[...])
        y_o[:, 0:D_MODEL] = ya.astype(BF16)
        y_o[:, D_MODEL:2 * D_MODEL] = yb.astype(BF16)
        merged = (jax.nn.sigmoid(ga_ref[...]) * ya + jax.nn.sigmoid(gb_ref[...]) * yb).astype(BF16)
        mg_o[...] = merged
        x1_o[...] = x_ref[...] + _dot_nn(merged, _wrows(wo_ref, 0, D_MODEL))

    return pl.pallas_call(
        body, name="merge_fwd", grid=(t // tm,),
        out_shape=(jax.ShapeDtypeStruct((t, D_MODEL), F32), jax.ShapeDtypeStruct((t, D_MODEL), BF16),
                   jax.ShapeDtypeStruct((t, 2 * D_MODEL), BF16)),
        in_specs=[_rows(tm, H_A * HP), _rows(tm, H_B * HP), _rows(tm, 1024, 0), _rows(tm, 1024, 1), _rows(tm, D_MODEL),
                  _resident((D_MODEL, H_A * HP)), _resident((D_MODEL, H_B * HP)), _packed_weight(128, off["w_o"])],
        out_specs=(_rows(tm, D_MODEL), _rows(tm, D_MODEL), _rows(tm, 2 * D_MODEL)), compiler_params=_params(("parallel",)),
    )(oa, ob, zp, zp, x, w_oa_t, w_ob_t, wpack)


def mlp_fwd(x1, g_mlp, wpack, off, tm):
    t = x1.shape[0]
    fc = 1024

    def body(x_ref, g_ref, wup_ref, wdn_ref, x2_o, u_o):
        xv = x_ref[...]
        h2 = (xv * _rstd(xv, D_MODEL) * g_ref[...]).astype(BF16)
        acc = xv
        for cidx in range(D_FF // fc):
            sl = slice(cidx * fc, (cidx + 1) * fc)
            u = jnp.maximum(_dot_nt(h2, _wrows(wup_ref, cidx * fc, fc)), 0.0)
            u_o[:, sl] = u.astype(BF16)
            acc = acc + _dot_nn((u * u).astype(BF16), _wrows(wdn_ref, cidx * fc, fc))
        x2_o[...] = acc

    return pl.pallas_call(
        body, name="mlp_fwd", grid=(t // tm,),
        out_shape=(jax.ShapeDtypeStruct((t, D_MODEL), F32), jax.ShapeDtypeStruct((t, D_FF), BF16)),
        in_specs=[_rows(tm, D_MODEL), _resident((1, D_MODEL)), _packed_weight(512, off["w_up"]), _packed_weight(512, off["w_down"])],
        out_specs=(_rows(tm, D_MODEL), _rows(tm, D_FF)), compiler_params=_params(("parallel",)),
    )(x1, g_mlp, wpack, wpack)


def ple_loss_bwd(x2, p, tgt, g_ple, g_final, wpack, off, w_ple_t, tm):
    t = x2.shape[0]
    inv_d = 1.0 / D_MODEL

    def body(x2_ref, p_ref, tg_ref, gp_ref, gf_ref, wpg_ref, wple_ref, dx2_o, dt_o, h3_o, dpe_o, st_o, dx2b_o):
        @pl.when(pl.program_id(0) == 0)
        def _():
            st_o[...] = jnp.zeros_like(st_o)

        x2v = x2_ref[...]
        gp, gf = gp_ref[...], gf_ref[...]
        w_pg = _wrows(wpg_ref, 0, D_MODEL)
        r2 = _rstd(x2v, D_MODEL)
        xh2 = x2v * r2
        h3 = (xh2 * gp).astype(BF16)
        h3_o[...] = h3
        gate = jax.nn.sigmoid(_dot_nn(h3, w_pg))
        pe = _dot_nt(p_ref[...].astype(BF16), wple_ref[...])
        x3 = x2v + gate * pe
        r3 = _rstd(x3, D_MODEL)
        xh3 = x3 * r3
        err = xh3 * gf - tg_ref[...]
        dy = err * inv_d
        dx3 = _rms_bwd(dy, xh3, r3, gf, D_MODEL)
        dpe_o[...] = (dx3 * gate).astype(BF16)
        dt = (dx3 * pe * gate * (1.0 - gate)).astype(BF16)
        dt_o[...] = dt
        dh3 = _dot_nt(dt, w_pg)
        dx2 = dx3 + _rms_bwd(dh3, xh2, r2, gp, D_MODEL)
        dx2_o[...] = dx2
        dx2b_o[...] = dx2.astype(BF16)
        st_o[0:1, :] += _colsum(dh3 * xh2)
        st_o[1:2, :] += _colsum(dy * xh3)
        st_o[2:3, :] += _colsum(err * err) * (0.5 * inv_d)

    bf = jax.ShapeDtypeStruct((t, D_MODEL), BF16)
    return pl.pallas_call(
        body, name="ple_loss_bwd", grid=(t // tm,),
        out_shape=(jax.ShapeDtypeStruct((t, D_MODEL), F32), bf, bf, bf, jax.ShapeDtypeStruct((3, D_MODEL), F32), bf),
        in_specs=[_rows(tm, D_MODEL), _rows(tm, PLE_DIM), _rows(tm, D_MODEL), _resident((1, D_MODEL)), _resident((1, D_MODEL)),
                  _packed_weight(128, off["w_ple_gate"]), _resident((D_MODEL, PLE_DIM))],
        out_specs=(_rows(tm, D_MODEL), _rows(tm, D_MODEL), _rows(tm, D_MODEL), _rows(tm, D_MODEL),
                   pl.BlockSpec((3, D_MODEL), lambda i: (0, 0)), _rows(tm, D_MODEL)),
        compiler_params=_params(("arbitrary",)),
    )(x2, p, tgt, g_ple, g_final, wpack, w_ple_t)


def mlp_bwd(dx2, x1, u, g_mlp, wpack, off, tm):
    t = x1.shape[0]
    fc = 1024

    def body(dx2_ref, x1_ref, u_ref, g_ref, wup_ref, wdn_ref, dx1_o, da_o, h2_o, st_o, dx1b_o):
        @pl.when(pl.program_id(0) == 0)
        def _():
            st_o[...] = jnp.zeros_like(st_o)

        d2 = dx2_ref[...]
        d2b = d2.astype(BF16)
        dh2 = jnp.zeros((tm, D_MODEL), F32)
        for cidx in range(D_FF // fc):
            sl = slice(cidx * fc, (cidx + 1) * fc)
            da = (_dot_nt(d2b, _wrows(wdn_ref, cidx * fc, fc)) * (2.0 * u_ref[:, sl].astype(F32))).astype(BF16)
            da_o[:, sl] = da
            dh2 = dh2 + _dot_nn(da, _wrows(wup_ref, cidx * fc, fc))
        xv = x1_ref[...]
        g = g_ref[...]
        r1 = _rstd(xv, D_MODEL)
        xh1 = xv * r1
        h2_o[...] = (xh1 * g).astype(BF16)
        st_o[...] += _colsum(dh2 * xh1)
        dx1 = d2 + _rms_bwd(dh2, xh1, r1, g, D_MODEL)
        dx1_o[...] = dx1
        dx1b_o[...] = dx1.astype(BF16)

    return pl.pallas_call(
        body, name="mlp_bwd", grid=(t // tm,),
        out_shape=(jax.ShapeDtypeStruct((t, D_MODEL), F32), jax.ShapeDtypeStruct((t, D_FF), BF16),
                   jax.ShapeDtypeStruct((t, D_MODEL), BF16), jax.ShapeDtypeStruct((1, D_MODEL), F32),
                   jax.ShapeDtypeStruct((t, D_MODEL), BF16)),
        in_specs=[_rows(tm, D_MODEL), _rows(tm, D_MODEL), _rows(tm, D_FF), _resident((1, D_MODEL)),
                  _packed_weight(512, off["w_up"]), _packed_weight(512, off["w_down"])],
        out_specs=(_rows(tm, D_MODEL), _rows(tm, D_FF), _rows(tm, D_MODEL), pl.BlockSpec((1, D_MODEL), lambda i: (0, 0)),
                   _rows(tm, D_MODEL)),
        compiler_params=_params(("arbitrary",)),
    )(dx2, x1, u, g_mlp, wpack, wpack)


def merge_bwd(dx1b, yab, zp, w_oa_t, w_ob_t, wpack, off, tm, comm=None):
    t = dx1b.shape[0]
    grid = (t // tm,)
    c_ins, c_in_specs, c_outs, c_sems, alias = _comm_parts(comm, 7, 5)

    def body(*refs):
        ((dx1_ref, y_ref, ga_ref, gb_ref, woa_ref, wob_ref, wo_ref), cin,
         (doa_o, dob_o, dg_o, dya_o, dyb_o), cout, _, csem) = _split_refs(refs, 7, 5, 0, comm)
        _comm_start(comm, cin, cout, csem, grid)
        dm = _dot_nt(dx1_ref[...], _wrows(wo_ref, 0, D_MODEL))
        for g_ref, w_ref, do_o, dy_o, col in ((ga_ref, woa_ref, doa_o, dya_o, 0), (gb_ref, wob_ref, dob_o, dyb_o, 1)):
            cols = slice(col * D_MODEL, (col + 1) * D_MODEL)
            sg = jax.nn.sigmoid(g_ref[...])
            dyv = (dm * sg).astype(BF16)
            dy_o[...] = dyv
            dg_o[:, cols] = (dm * y_ref[:, cols].astype(F32) * sg * (1.0 - sg)).astype(BF16)
            do_o[...] = _dot_nn(dyv, w_ref[...]).astype(BF16)
        _comm_finish(comm, cin, cout, csem, grid)

    bf = jax.ShapeDtypeStruct((t, D_MODEL), BF16)
    return pl.pallas_call(
        body, name="merge_bwd", grid=grid,
        out_shape=(bf, bf, jax.ShapeDtypeStruct((t, ZP), BF16), bf, bf, *c_outs),
        in_specs=[_rows(tm, D_MODEL), _rows(tm, 2 * D_MODEL), _rows(tm, 1024, 0), _rows(tm, 1024, 1),
                  _resident((D_MODEL, H_A * HP)), _resident((D_MODEL, H_B * HP)), _packed_weight(128, off["w_o"]), *c_in_specs],
        out_specs=(_rows(tm, D_MODEL), _rows(tm, D_MODEL), _rows(tm, 2 * D_MODEL), _rows(tm, D_MODEL), _rows(tm, D_MODEL),
                   *([_ANY] * len(c_outs))),
        scratch_shapes=c_sems, input_output_aliases=alias,
        compiler_params=_params(("arbitrary",)),
    )(dx1b, yab, zp, zp, w_oa_t, w_ob_t, wpack, *c_ins)


def attn_bwd(q, k, v, do, o, lse, n_b, s_len, tq, name, comm=None):
    t = q.shape[0]
    n_h, n_hk = q.shape[1] // HP, k.shape[1] // HP
    grp = n_h // n_hk
    nq = s_len // tq
    sub = min(tq, 256)
    grid = (n_b, n_hk, grp, nq)
    c_ins, c_in_specs, c_outs, c_sems, alias = _comm_parts(comm, 6, 3)

    def body(*refs):
        ((q_ref, k_ref, v_ref, do_ref, o_ref, lse_ref), cin, (dq_o, dk_o, dv_o), cout, (p_s, ds_s, dk_acc, dv_acc),
         csem) = _split_refs(refs, 6, 3, 4, comm)
        _comm_start(comm, cin, cout, csem, grid)

        @pl.when((pl.program_id(2) == 0) & (pl.program_id(3) == 0))
        def _():
            dk_acc[...] = jnp.zeros_like(dk_acc)
            dv_acc[...] = jnp.zeros_like(dv_acc)

        kv, vv = k_ref[...], v_ref[...]
        for r in range(tq // sub):
            rows = slice(r * sub, (r + 1) * sub)
            qv, dov = q_ref[rows, :], do_ref[rows, :]
            delta = jnp.sum(dov.astype(F32) * o_ref[rows, :].astype(F32), axis=-1, keepdims=True)
            p = jnp.exp(_dot_nt(qv, kv) - lse_ref[rows, 0:1])
            ds = (p * (_dot_nt(dov, vv) - delta)).astype(BF16)
            p_s[rows, :] = p.astype(BF16)
            ds_s[rows, :] = ds
            dq_o[rows, :] = _dot_nn(ds, kv).astype(dq_o.dtype)
        dk_acc[...] += _dot_tn(ds_s[...], q_ref[...])
        dv_acc[...] += _dot_tn(p_s[...], do_ref[...])

        @pl.when((pl.program_id(2) == grp - 1) & (pl.program_id(3) == nq - 1))
        def _():
            dk_o[...] = dk_acc[...].astype(dk_o.dtype)
            dv_o[...] = dv_acc[...].astype(dv_o.dtype)

        _comm_finish(comm, cin, cout, csem, grid)

    qspec = pl.BlockSpec((tq, HP), lambda b, hk, g, i: (b * nq + i, hk * grp + g))
    kspec = pl.BlockSpec((s_len, HP), lambda b, hk, g, i: (b, hk))
    return pl.pallas_call(
        body, name=name, grid=grid,
        out_shape=(jax.ShapeDtypeStruct((t, n_h * HP), BF16), jax.ShapeDtypeStruct((t, n_hk * HP), BF16),
                   jax.ShapeDtypeStruct((t, n_hk * HP), BF16), *c_outs),
        in_specs=[qspec, kspec, kspec, qspec, qspec, qspec, *c_in_specs],
        out_specs=(qspec, kspec, kspec, *([_ANY] * len(c_outs))),
        scratch_shapes=[pltpu.VMEM((tq, s_len), BF16), pltpu.VMEM((tq, s_len), BF16),
                        pltpu.VMEM((s_len, HP), F32), pltpu.VMEM((s_len, HP), F32), *c_sems],
        input_output_aliases=alias,
        compiler_params=_params(("arbitrary", "arbitrary", "arbitrary", "arbitrary")),
    )(q, k, v, do, o, lse, *c_ins)


def prep_bwd(dqa, dka, dva, dqb, dkb, dvb, zp, dz, tabs, g_qa, g_kva, g_qn, g_kn, w_qb_t, w_kvb_t, tm, s_len):
    t = zp.shape[0]
    nsb = s_len // tm
    scale_a = (QK_NOPE + QK_ROPE) ** -0.5
    scale_b = HD_B ** -0.5

    def body(dqa_ref, dka_ref, dva_ref, dqb_ref, dkb_ref, dvb_ref, qb_ref, qlat_ref, kb_ref, ckv_ref, tab_ref,
             gqa_ref, gkva_ref, gqn_ref, gkn_ref, wqb_ref, wkvb_ref, _, dz_o, dqap_o, dkva_o, st_o):
        dzq_o, dsm_o = dz_o.at[:, 0:1024], dz_o.at[:, 1024:2048]

        @pl.when(pl.program_id(0) == 0)
        def _():
            st_o[...] = jnp.zeros_like(st_o)

        ca, s1a, s2a = tab_ref[0], tab_ref[1], tab_ref[2]
        ck = tab_ref[3]
        cb, s1b, s2b = tab_ref[4], tab_ref[5], tab_ref[6]
        for h in range(H_A):
            sl = slice(h * HP, (h + 1) * HP)
            dqap_o[:, sl] = _rope_bwd(dqa_ref[:, sl].astype(F32) * scale_a, ca, s1a, s2a).astype(BF16)
        dcq = _dot_nn(dqap_o[...], wqb_ref[...])
        ql = qlat_ref[...]
        rq = _rstd(ql, Q_LORA)
        xh = ql * rq
        gqa = gqa_ref[...]
        st_o[0:1, :] += _colsum(dcq * xh)
        dsm_o[:, 0:256] = _rms_bwd(dcq, xh, rq, gqa, Q_LORA).astype(BF16)
        dkpe = jnp.zeros((tm, HP), F32)
        for h in range(H_A):
            sl = slice(h * HP, (h + 1) * HP)
            dk = dka_ref[:, sl]
            dkpe = dkpe + dk.astype(F32)
            dkva_o[:, sl] = dk.astype(BF16)
        dkva_o[:, H_A * HP:] = dva_ref[...].astype(BF16)
        dsm_o[:, 896:1024] = _rope_bwd(dkpe, ck, s1a, s2a).astype(BF16)
        dckv = _dot_nn(dkva_o[...], wkvb_ref[...])
        cr = ckv_ref[...]
        rk = _rstd(cr, KV_LORA)
        xh = cr * rk
        st_o[1:2, 0:128] += _colsum(dckv * xh)
        dsm_o[:, 768:896] = _rms_bwd(dckv, xh, rk, gkva_ref[...], KV_LORA).astype(BF16)
        gqn, gkn = gqn_ref[...], gkn_ref[...]
        dgq = jnp.zeros((1, HP), F32)
        for h in range(H_B):
            sl = slice(h * HP, (h + 1) * HP)
            dy = _rope_bwd(dqb_ref[:, sl].astype(F32) * scale_b, cb, s1b, s2b)
            xs = qb_ref[:, sl]
            r = _rstd(xs, HD_B)
            xh = xs * r
            dgq = dgq + _colsum(dy * xh)
            dzq_o[:, sl] = _rms_bwd(dy, xh, r, gqn, HD_B).astype(BF16)
        st_o[2:3, 0:128] += dgq
        dgk = jnp.zeros((1, HP), F32)
        for h in range(KV_B):
            sl = slice(h * HP, (h + 1) * HP)
            dy = _rope_bwd(dkb_ref[:, sl].astype(F32), cb, s1b, s2b)
            xs = kb_ref[:, sl]
            r = _rstd(xs, HD_B)
            xh = xs * r
            dgk = dgk + _colsum(dy * xh)
            dsm_o[:, 256 + h * HP:256 + (h + 1) * HP] = _rms_bwd(dy, xh, r, gkn, HD_B).astype(BF16)
        st_o[3:4, 0:128] += dgk
        dsm_o[:, 512:768] = dvb_ref[...].astype(BF16)

    return pl.pallas_call(
        body, name="prep_bwd", grid=(t // tm,),
        out_shape=(jax.ShapeDtypeStruct((t, ZP), BF16), jax.ShapeDtypeStruct((t, 1024), BF16),
                   jax.ShapeDtypeStruct((t, 2048), BF16), jax.ShapeDtypeStruct((4, 256), F32)),
        in_specs=[_rows(tm, 1024), _rows(tm, 1024), _rows(tm, 1024), _rows(tm, 1024), _rows(tm, 256), _rows(tm, 256),
                  _rows(tm, 1024, 2), _rows(tm, 256, 12), _rows(tm, 256, 13), _rows(tm, 128, 30),
                  pl.BlockSpec((7, tm, HP), lambda i: (0, i % nsb, 0)),
                  _resident((1, Q_LORA)), _resident((1, KV_LORA)), _resident((1, HP)), _resident((1, HP)),
                  _resident((H_A * HP, Q_LORA)), _resident((2 * H_A * HP, KV_LORA)), _ANY],
        out_specs=(_rows(tm, 2048, 1), _rows(tm, 1024), _rows(tm, 2048), pl.BlockSpec((4, 256), lambda i: (0, 0))),
        input_output_aliases={17: 0}, compiler_params=_params(("arbitrary",)),
    )(dqa, dka, dva, dqb, dkb, dvb, zp, zp, zp, zp, tabs, g_qa, g_kva, g_qn, g_kn, w_qb_t, w_kvb_t, dz)


def in_bwd(dz, x, dx1, g_mix, w_in_t, tm, comm=None):
    t = x.shape[0]
    grid = (t // tm,)
    c_ins, c_in_specs, c_outs, c_sems, alias = _comm_parts(comm, 5, 2)

    def body(*refs):
        (dz_ref, x_ref, dx1_ref, g_ref, w_ref), cin, (dx_o, st_o), cout, _, csem = _split_refs(refs, 5, 2, 0, comm)
        _comm_start(comm, cin, cout, csem, grid)

        @pl.when(pl.program_id(0) == 0)
        def _():
            st_o[...] = jnp.zeros_like(st_o)

        dh = _dot_nn(dz_ref[...], w_ref[...])
        xv = x_ref[...]
        g = g_ref[...]
        r = _rstd(xv, D_MODEL)
        xh = xv * r
        st_o[...] += _colsum(dh * xh)
        dx_o[...] = dx1_ref[...] + _rms_bwd(dh, xh, r, g, D_MODEL)
        _comm_finish(comm, cin, cout, csem, grid)

    return pl.pallas_call(
        body, name="in_bwd", grid=grid,
        out_shape=(jax.ShapeDtypeStruct((t, D_MODEL), F32), jax.ShapeDtypeStruct((1, D_MODEL), F32), *c_outs),
        in_specs=[_rows(tm, ZP), _rows(tm, D_MODEL), _rows(tm, D_MODEL),
                  _resident((1, D_MODEL)), _resident((ZP, D_MODEL)), *c_in_specs],
        out_specs=(_rows(tm, D_MODEL), pl.BlockSpec((1, D_MODEL), lambda i: (0, 0)), *([_ANY] * len(c_outs))),
        scratch_shapes=c_sems, input_output_aliases=alias,
        compiler_params=_params(("arbitrary",)),
    )(dz, x, dx1, g_mix, w_in_t, *c_ins)


def matmul_tn(a, b, name, square_a=False):
    t, m = a.shape
    n = b.shape[1]
    bm = min(m, 512)
    tk = min(t, 4096)

    def body(a_ref, b_ref, o_ref):
        @pl.when(pl.program_id(1) == 0)
        def _():
            o_ref[...] = jnp.zeros_like(o_ref)

        av = a_ref[...]
        if square_a:
            av = (av.astype(F32) * av.astype(F32))
        o_ref[...] += _dot_tn(av.astype(BF16), b_ref[...].astype(BF16))

    return pl.pallas_call(
        body, name=name, grid=(m // bm, t // tk), out_shape=jax.ShapeDtypeStruct((m, n), F32),
        in_specs=[pl.BlockSpec((tk, bm), lambda i, kk: (kk, i)), pl.BlockSpec((tk, n), lambda i, kk: (kk, 0))],
        out_specs=pl.BlockSpec((bm, n), lambda i, kk: (i, 0)),
        compiler_params=_params(("parallel", "arbitrary")),
    )(a, b)


def matmul_tn_packed(a, b, name, rows, row_off, total_rows, buf=None, square_a=False):
    t, m = a.shape
    n = b.shape[1]
    pd = max(1, 512 // rows)
    bm = pd * rows
    tk = min(t, 4096)
    nk = t // tk

    def body(a_ref, b_ref, *rest):
        o_ref, acc = rest[-2], rest[-1]

        @pl.when(pl.program_id(1) == 0)
        def _():
            acc[...] = jnp.zeros_like(acc)

        av = a_ref[...]
        if square_a:
            av = (av.astype(F32) * av.astype(F32))
        acc[...] += _dot_tn(av.astype(BF16), b_ref[...].astype(BF16))

        @pl.when(pl.program_id(1) == nk - 1)
        def _():
            o_ref[...] = acc[...].reshape(pd, rows, n).astype(o_ref.dtype)

    in_specs = [pl.BlockSpec((tk, bm), lambda i, kk: (kk, i)), pl.BlockSpec((tk, n), lambda i, kk: (kk, 0))]
    args = [a, b]
    if buf is not None:
        in_specs.append(_ANY)
        args.append(buf)
    return pl.pallas_call(
        body, name=name, grid=(m // bm, nk), out_shape=jax.ShapeDtypeStruct((N_DEV, total_rows, n), BF16),
        in_specs=in_specs, out_specs=pl.BlockSpec((pd, rows, n), lambda i, kk: (i, row_off // rows, 0)),
        scratch_shapes=[pltpu.VMEM((bm, n), F32)], input_output_aliases={2: 0} if buf is not None else {},
        compiler_params=_params(("parallel", "arbitrary")),
    )(*args)


def adamw(w, g, m, v, name, g_transposed=False):
    _, r, c = w.shape
    tr = 256 if (not g_transposed and r > 256 and r % 256 == 0) else r
    c1 = 1.0 - ADAM_B1 ** ADAM_STEP
    c2 = 1.0 - ADAM_B2 ** ADAM_STEP

    def body(w_ref, g_ref, m_ref, v_ref, g_o, d_o, m_o, v_o):
        gv = g_ref[...].T if g_transposed else g_ref[...]
        mn = ADAM_B1 * m_ref[0] + (1.0 - ADAM_B1) * gv
        vn = ADAM_B2 * v_ref[0] + (1.0 - ADAM_B2) * (gv * gv)
        g_o[0] = gv
        m_o[0] = mn
        v_o[0] = vn
        d_o[0] = -ADAM_LR * ((mn / c1) / (jnp.sqrt(vn / c2) + ADAM_EPS) + ADAM_WD * w_ref[0])

    spec = pl.BlockSpec((1, tr, c), lambda i: (0, i, 0))
    gspec = pl.BlockSpec((c, r), lambda i: (0, 0)) if g_transposed else pl.BlockSpec((tr, c), lambda i: (i, 0))
    shp = jax.ShapeDtypeStruct((1, r, c), F32)
    return pl.pallas_call(
        body, name=name, grid=(r // tr,), out_shape=(shp,) * 4, in_specs=[spec, gspec, spec, spec], out_specs=(spec,) * 4,
        compiler_params=_params(("parallel",)),
    )(w, g, m, v)


def _rope_tables(s_len):
    def angles(pos, dim):
        inv = np.float32(ROPE_THETA) ** (-np.arange(0, dim, 2, dtype=np.float32) / np.float32(dim))
        return pos.astype(np.float32)[:, None] * inv[None, :]

    tpos = np.arange(s_len)
    a1 = angles(tpos, QK_ROPE)
    ar = angles(tpos // GRID_W, HD_B // 2)
    ac = angles(tpos % GRID_W, HD_B // 2)
    z16 = np.zeros((s_len, 16), np.float32)
    z32 = np.zeros((s_len, 32), np.float32)
    z64 = np.zeros((s_len, 64), np.float32)
    one64 = np.ones((s_len, 64), np.float32)
    c1, s1 = np.cos(a1), np.sin(a1)
    ca = np.concatenate([one64, c1, c1, z32], axis=1)
    ck = np.concatenate([z64, c1, c1, z32], axis=1)
    s1a = np.concatenate([z64, -s1, z16, z32], axis=1)
    s2a = np.concatenate([z64, z16, s1, z32], axis=1)
    cr, sr, cc, sc = np.cos(ar), np.sin(ar), np.cos(ac), np.sin(ac)
    cb = np.concatenate([cr, cr, cc, cc, z64], axis=1)
    s1b = np.concatenate([-sr, z16, -sc, z16, z64], axis=1)
    s2b = np.concatenate([z16, sr, z16, sc, z64], axis=1)
    return jnp.asarray(np.stack([ca, s1a, s2a, ck, cb, s1b, s2b]).astype(np.float32))


def _pad_heads(a, n_heads, axis):
    shp = a.shape
    a = a.reshape(shp[:axis] + (n_heads, shp[axis] // n_heads) + shp[axis + 1:])
    pad = [(0, 0)] * a.ndim
    pad[axis + 1] = (0, HP - a.shape[axis + 1])
    a = jnp.pad(a, pad)
    return a.reshape(shp[:axis] + (n_heads * HP,) + shp[axis + 1:])


def _unpad_heads(a, n_heads, width, axis):
    shp = a.shape
    a = a.reshape(shp[:axis] + (n_heads, HP) + shp[axis + 1:])
    a = lax.slice_in_dim(a, 0, width, axis=axis + 1)
    return a.reshape(shp[:axis] + (n_heads * width,) + shp[axis + 1:])


def _pack_rows(blocks, names):
    parts = []
    for name in names:
        b = blocks[name]
        padr = PACK_ROWS[name] - b.shape[-2]
        if padr:
            b = jnp.pad(b, [(0, 0)] * (b.ndim - 2) + [(0, padr), (0, 0)])
        parts.append(b)
    return jnp.concatenate(parts, axis=parts[0].ndim - 2)


W_IN_COLS = 3232 // N_DEV
_Z_RUNS = ([(0, 256, 3072), (256, 384, 3840), (384, 416, 4032)]
           + [(416 + 64 * h, 480 + 64 * h, 2048 + HP * h) for h in range(H_B)]
           + [(928 + 64 * j, 992 + 64 * j, 3328 + HP * j) for j in range(KV_B)]
           + [(1056 + 64 * j, 1120 + 64 * j, 3584 + HP * j) for j in range(KV_B)]
           + [(1184, 2208, 0), (2208, 3232, 1024)])


def _expand_w_in(wt):
    parts, cursor = [], 0
    for z0, z1, zp0 in sorted(_Z_RUNS, key=lambda run: run[2]):
        if zp0 > cursor:
            parts.append(jnp.zeros((zp0 - cursor, D_MODEL), wt.dtype))
        parts.append(wt[z0:z1])
        cursor = zp0 + (z1 - z0)
    parts.append(jnp.zeros((ZP - cursor, D_MODEL), wt.dtype))
    return jnp.concatenate(parts, axis=0)


def _collapse_w_in(dw):
    return jnp.concatenate([dw[zp0:zp0 + z1 - z0] for z0, z1, zp0 in sorted(_Z_RUNS)], axis=0)


def kernel(x, p, g_mix, w_in, g_qa, w_qb, g_kva, w_kvb, g_qn, g_kn, w_oa, w_ob, w_o, g_mlp, w_up, w_down, g_ple, w_ple_gate, w_ple, g_final, loss_target, m_g_mix, m_w_in, m_g_qa, m_w_qb, m_g_kva, m_w_kvb, m_g_qn, m_g_kn, m_w_oa, m_w_ob, m_w_o, m_g_mlp, m_w_up, m_w_down, m_g_ple, m_w_ple_gate, m_w_ple, m_g_final, v_g_mix, v_w_in, v_g_qa, v_w_qb, v_g_kva, v_w_kvb, v_g_qn, v_g_kn, v_w_oa, v_w_ob, v_w_o, v_g_mlp, v_w_up, v_w_down, v_g_ple, v_w_ple_gate, v_w_ple, v_g_final):
    n_b, s_len, _ = x.shape
    t = n_b * s_len
    tm = min(512, s_len)
    tq_f = min(2048, s_len)
    tq_b = min(2048, s_len)

    mats = dict(w_in=(w_in, m_w_in, v_w_in), w_qb=(w_qb, m_w_qb, v_w_qb), w_kvb=(w_kvb, m_w_kvb, v_w_kvb),
                w_oa=(w_oa, m_w_oa, v_w_oa), w_ob=(w_ob, m_w_ob, v_w_ob), w_o=(w_o, m_w_o, v_w_o),
                w_up=(w_up, m_w_up, v_w_up), w_down=(w_down, m_w_down, v_w_down),
                w_ple_gate=(w_ple_gate, m_w_ple_gate, v_w_ple_gate), w_ple=(w_ple, m_w_ple, v_w_ple))
    col_sharded = ("w_in", "w_qb", "w_kvb", "w_oa", "w_ob", "w_up", "w_ple")

    blocks = {}
    for name in PACK_W1 + ("w_oa", "w_ob", "w_ple"):
        blocks[name] = mats[name][0][0].T.reshape(-1, D_MODEL).astype(BF16)
    off_w1, _ = _pack_offsets(PACK_W1)
    off_w2, _ = _pack_offsets(PACK_W2)
    off_w3, _ = _pack_offsets(PACK_W3)
    xf = x.reshape(t, D_MODEL)
    h, full1 = norm_x(xf, g_mix, tm, comm=gather_first_comm(_pack_rows(blocks, PACK_W1)))
    pack2, pack3, full1 = pack_late_weights(w_up, w_down, w_o, w_ple_gate, _pack_rows(blocks, ("w_oa", "w_ob", "w_ple")),
                                            comm=gather_pass_comm(full1))

    def gathered(full, offs, name, rows, width):
        return full[:, offs[name]:offs[name] + rows].reshape(-1, width)

    w_in_t = _expand_w_in(gathered(full1, off_w1, "w_in", W_IN_COLS, D_MODEL))
    w_qb_t = _pad_heads(gathered(full1, off_w1, "w_qb", 24, Q_LORA), H_A, 0)
    wkvb = gathered(full1, off_w1, "w_kvb", 16, KV_LORA).reshape(H_A, 2, 64, KV_LORA)
    w_kvb_t = jnp.concatenate([_pad_heads(wkvb[:, 0].reshape(-1, KV_LORA), H_A, 0),
                               _pad_heads(wkvb[:, 1].reshape(-1, KV_LORA), H_A, 0)], axis=0)

    tabs = _rope_tables(s_len)
    g_qn_p = jnp.pad(g_qn, ((0, 0), (0, HP - HD_B)))
    g_kn_p = jnp.pad(g_kn, ((0, 0), (0, HP - HD_B)))
    pf = p.reshape(t, PLE_DIM)
    tgt = loss_target.reshape(t, D_MODEL)

    zp, full2 = in_proj(h, w_in_t, tm, comm=gather_first_comm(pack2))
    qa, ka, va, qb, kb, vb, cq, ckv, full2 = attn_prep(zp, tabs, g_qa, g_kva, g_qn_p, g_kn_p, w_qb_t, w_kvb_t, tm, s_len,
                                                       comm=gather_pass_comm(full2))
    oa, lse_a, full3 = attn_fwd(qa, ka, va, n_b, s_len, tq_f, "attn_a_fwd", comm=gather_first_comm(pack3))
    ob, lse_b, full3 = attn_fwd(qb, kb, vb, n_b, s_len, tq_f, "attn_b_fwd", comm=gather_pass_comm(full3))
    w_oa_t = _pad_heads(gathered(full2, off_w2, "w_oa", 64, H_A * V_DIM_A), H_A, 1)
    w_ob_t = _pad_heads(gathered(full2, off_w2, "w_ob", 64, H_B * HD_B), H_B, 1)
    w_ple_t = gathered(full2, off_w2, "w_ple", 32, PLE_DIM)
    x1, merged, yab = merge_fwd(oa, ob, zp, xf, w_oa_t, w_ob_t, full2, off_w2, tm)
    x2, u = mlp_fwd(x1, g_mlp, full3, off_w3, tm)
    dx2, dt, h3, dpe, st_ple, dx2b = ple_loss_bwd(x2, pf, tgt, g_ple, g_final.reshape(1, D_MODEL), full2, off_w2, w_ple_t, tm)
    dx1, da, h2, st_mlp, dx1b = mlp_bwd(dx2, x1, u, g_mlp, full3, off_w3, tm)

    core = lax.axis_index("c").astype(jnp.int32).reshape(1)
    chip = (2 * lax.axis_index("x") + lax.axis_index("y")).astype(jnp.int32).reshape(1)

    def packed(gblocks, names):
        return _pack_rows({n: gblocks[n].reshape(N_DEV, -1, D_MODEL).astype(BF16) for n in names}, names)

    off_g1, rows_g1 = _pack_offsets(PACK_G1)
    gpack1 = matmul_tn_packed(da, h2, "gw_up", 512, off_g1["w_up"], rows_g1)
    gpack1 = matmul_tn_packed(u, dx2b, "gw_down", 512, off_g1["w_down"], rows_g1, buf=gpack1, square_a=True)
    gpack1 = matmul_tn_packed(h3, dt, "gw_pg", 128, off_g1["w_ple_gate"], rows_g1, buf=gpack1)
    gple = matmul_tn(dpe, pf, "gw_ple").reshape(N_DEV, -1, D_MODEL).astype(BF16)
    gpack1 = lax.dynamic_update_slice(gpack1, gple, (0, off_g1["w_ple"], 0))
    doa, dob, dz, dya, dyb, got1 = merge_bwd(dx1b, yab, zp, w_oa_t, w_ob_t, full2, off_w2, tm,
                                               comm=scatter_sibling_comm(gpack1))
    part1 = add_pairs(gpack1, got1, core)
    dqa, dka, dva, land1 = attn_bwd(qa, ka, va, doa, oa, lse_a, n_b, s_len, tq_b, "attn_a_bwd", comm=scatter_chips_comm(part1))
    gshard1 = sum_chips(part1, land1, chip)

    off_g2, rows_g2 = _pack_offsets(PACK_G2)
    g2 = dict(w_oa=_unpad_heads(matmul_tn(dya, oa, "gw_oa"), H_A, V_DIM_A, 1),
              w_ob=_unpad_heads(matmul_tn(dyb, ob, "gw_ob"), H_B, HD_B, 1))
    gpack2 = matmul_tn_packed(merged, dx1b, "gw_o", 128, off_g2["w_o"], rows_g2)
    gpack2 = lax.dynamic_update_slice(gpack2, packed(g2, ("w_oa", "w_ob")), (0, off_g2["w_oa"], 0))
    part2 = add_pairs(gpack2, exchange_sibling(gpack2), core)
    dqb, dkb, dvb, land2 = attn_bwd(qb, kb, vb, dob, ob, lse_b, n_b, s_len, tq_b, "attn_b_bwd", comm=scatter_chips_comm(part2))
    gshard2 = sum_chips(part2, land2, chip)
    dz, dqap, dkva, st_prep = prep_bwd(dqa, dka, dva, dqb, dkb, dvb, zp, dz, tabs, g_qa, g_kva, g_qn_p, g_kn_p,
                                       w_qb_t, w_kvb_t, tm, s_len)

    gkv = matmul_tn(dkva, ckv, "gw_kvb")
    g3 = dict(
        w_in=_collapse_w_in(matmul_tn(dz, h, "gw_in")),
        w_qb=_unpad_heads(matmul_tn(dqap, cq, "gw_qb"), H_A, QK_NOPE + QK_ROPE, 0),
        w_kvb=jnp.stack([_unpad_heads(gkv[:H_A * HP], H_A, 64, 0).reshape(H_A, 64, KV_LORA),
                         _unpad_heads(gkv[H_A * HP:], H_A, 64, 0).reshape(H_A, 64, KV_LORA)], axis=1))
    gpack3 = packed(g3, PACK_G3)
    part3 = add_pairs(gpack3, exchange_sibling(gpack3), core)
    grad_x, st_mix, land3 = in_bwd(dz, xf, dx1, g_mix, w_in_t, tm, comm=scatter_chips_comm(part3))
    gshard3 = sum_chips(part3, land3, chip)
    off_g3, _ = _pack_offsets(PACK_G3)
    shards = {n: (gshard1, off_g1[n]) for n in PACK_G1}
    shards.update({n: (gshard2, off_g2[n]) for n in PACK_G2})
    shards.update({n: (gshard3, off_g3[n]) for n in PACK_G3})

    stats = allreduce_stats(st_mix, st_prep, st_mlp, st_ple)
    loss = jnp.sum(stats[ST_LOSS])

    out_g, out_d, out_m, out_v = {}, {}, {}, {}
    for name, (w, m, v) in mats.items():
        gshard, off = shards[name]
        r, c = w.shape[1:]
        if name in col_sharded:
            g2 = gshard[off:off + (r * c) // D_MODEL].reshape(c, r)
            if r % 128 == 0 and c % 128 == 0:
                res = adamw(w, g2, m, v, "adamw_" + name, g_transposed=True)
            else:
                res = adamw(w[0].T[None], g2, m[0].T[None], v[0].T[None], "adamw_" + name)
                res = tuple(a[0].T[None] for a in res)
        else:
            res = adamw(w, gshard[off:off + r], m, v, "adamw_" + name)
        out_g[name], out_d[name], out_m[name], out_v[name] = res

    gains = (("g_mix", g_mix, m_g_mix, v_g_mix, ST_G_MIX), ("g_qa", g_qa, m_g_qa, v_g_qa, ST_G_QA),
             ("g_kva", g_kva, m_g_kva, v_g_kva, ST_G_KVA), ("g_qn", g_qn, m_g_qn, v_g_qn, ST_G_QN),
             ("g_kn", g_kn, m_g_kn, v_g_kn, ST_G_KN), ("g_mlp", g_mlp, m_g_mlp, v_g_mlp, ST_G_MLP),
             ("g_ple", g_ple, m_g_ple, v_g_ple, ST_G_PLE), ("g_final", g_final, m_g_final, v_g_final, ST_G_FINAL))
    res = adamw_gains(stats, [(r_, w.reshape(1, -1), m.reshape(1, -1), v.reshape(1, -1)) for _, w, m, v, r_ in gains])
    for (name, w, _, _, _), (gg, gd, gm, gv) in zip(gains, res):
        out_g[name], out_d[name], out_m[name], out_v[name] = (a.reshape(w.shape) for a in (gg, gd, gm, gv))

    order = ("g_mix", "w_in", "g_qa", "w_qb", "g_kva", "w_kvb", "g_qn", "g_kn", "w_oa", "w_ob", "w_o", "g_mlp",
             "w_up", "w_down", "g_ple", "w_ple_gate", "w_ple", "g_final")
    return (loss, grad_x.reshape(x.shape), *[out_g[n] for n in order], *[out_d[n] for n in order],
            *[out_m[n] for n in order], *[out_v[n] for n in order])
```

```python
import numpy as np
import jax
import jax.numpy as jnp
from jax import lax
from jax.experimental import pallas as pl
from jax.experimental.pallas import tpu as pltpu

F32 = jnp.float32
BF16 = jnp.bfloat16

D_MODEL = 1024
EPS = 1e-6
ROPE_THETA = 10000.0
GRID_W = 64
H_A = 8
QK_NOPE = 64
QK_ROPE = 32
V_DIM_A = 64
Q_LORA = 256
KV_LORA = 128
H_B = 8
KV_B = 2
HD_B = 64
D_FF = 4 * D_MODEL
PLE_DIM = 256
HP = 128
ZP = 4096
N_DEV = 8
N_CHIP = 4

ADAM_LR = 0.001
ADAM_B1 = 0.9
ADAM_B2 = 0.999
ADAM_EPS = 1e-08
ADAM_WD = 0.01
ADAM_STEP = 10

VMEM_LIMIT = 52 * 1024 * 1024

PACK_ROWS = dict(w_in=416, w_qb=32, w_kvb=16, w_oa=64, w_ob=64, w_o=128, w_up=512, w_down=512, w_ple_gate=128, w_ple=32)
PACK_W1 = ("w_in", "w_qb", "w_kvb")
PACK_W2 = ("w_o", "w_ple_gate", "w_oa", "w_ob", "w_ple")
PACK_W3 = ("w_up", "w_down")
PACK_G1 = ("w_up", "w_down", "w_ple_gate", "w_ple")
PACK_G2 = ("w_o", "w_oa", "w_ob")
PACK_G3 = ("w_in", "w_qb", "w_kvb")


def _pack_offsets(names):
    off, o = {}, 0
    for n in names:
        off[n] = o
        o += PACK_ROWS[n]
    return off, o

ST_G_MIX, ST_G_QA, ST_G_KVA, ST_G_QN, ST_G_KN, ST_G_MLP, ST_G_PLE, ST_G_FINAL, ST_LOSS = range(9)
ST_ROWS = 16


def _dot_nn(a, b):
    return lax.dot_general(a, b, (((1,), (0,)), ((), ())), preferred_element_type=F32)


def _dot_nt(a, b):
    return lax.dot_general(a, b, (((1,), (1,)), ((), ())), preferred_element_type=F32)


def _dot_tn(a, b):
    return lax.dot_general(a, b, (((0,), (0,)), ((), ())), preferred_element_type=F32)


def _rstd(x, n):
    return lax.rsqrt(jnp.sum(x * x, axis=-1, keepdims=True) * (1.0 / n) + EPS)


def _rms_bwd(dy, xh, r, g, n):
    dxh = dy * g
    return r * (dxh - xh * (jnp.sum(dxh * xh, axis=-1, keepdims=True) * (1.0 / n)))


def _rope_fwd(x, c, s1, s2):
    return x * c + pltpu.roll(x, HP - 16, 1) * s1 + pltpu.roll(x, 16, 1) * s2


def _rope_bwd(d, c, s1, s2):
    return d * c + pltpu.roll(d * s1, 16, 1) + pltpu.roll(d * s2, HP - 16, 1)


def _colsum(v):
    return jnp.sum(v, axis=0, keepdims=True)


def _params(sem=None, vmem=VMEM_LIMIT):
    return pltpu.CompilerParams(dimension_semantics=sem, vmem_limit_bytes=vmem)


def _resident(shape):
    nd = len(shape)
    return pl.BlockSpec(shape, lambda *_: (0,) * nd, pipeline_mode=pl.Buffered(1))


def _rows(tm, width, col=0):
    return pl.BlockSpec((tm, width), lambda i: (i, col))


def _packed_weight(rows, off):
    return pl.BlockSpec((N_DEV, rows, D_MODEL), lambda *_: (0, off // rows, 0), pipeline_mode=pl.Buffered(1))


def _wrows(ref, start, size):
    rows = ref.shape[1]
    return ref[start // rows:(start + size) // rows].reshape(size, D_MODEL)


def _mesh_pos():
    return lax.axis_index("x"), lax.axis_index("y"), lax.axis_index("c")


def _flip(v, bit):
    return (1 - v) if bit else v


_ANY = pl.BlockSpec(memory_space=pl.ANY)
_MESH = pl.DeviceIdType.MESH


def _remote(src, dst, send_sems, recv_sems, k, to):
    return pltpu.make_async_remote_copy(src_ref=src, dst_ref=dst, send_sem=send_sems.at[k], recv_sem=recv_sems.at[k],
                                        device_id=to, device_id_type=_MESH)


def _sibling_copies(g_ref, got_ref, send_sems, recv_sems):
    x, y, c = _mesh_pos()
    return [_remote(g_ref.at[2 * j + (1 - c)], got_ref.at[j], send_sems, recv_sems, j, (x, y, 1 - c)) for j in range(N_CHIP)]


def _chip_copies(p_ref, land_ref, send_sems, recv_sems):
    x, y, c = _mesh_pos()
    copies = []
    for k in (1, 2, 3):
        tx, ty = _flip(x, k & 2), _flip(y, k & 1)
        copies.append(_remote(p_ref.at[2 * tx + ty], land_ref.at[k - 1], send_sems, recv_sems, k - 1, (tx, ty, c)))
    return copies


class _Comm:
    def __init__(self, ins, out_shapes, sems, make, aliases=None):
        self.ins, self.out_shapes, self.sems, self.make, self.aliases = list(ins), list(out_shapes), list(sems), make, aliases or {}


def _comm_parts(comm, n_in, n_out):
    if comm is None:
        return [], [], [], [], {}
    alias = {n_in + j: n_out + k for j, k in comm.aliases.items()}
    return comm.ins, [_ANY] * len(comm.ins), comm.out_shapes, comm.sems, alias


def _split_refs(refs, n_in, n_out, n_scratch, comm):
    n_ci = len(comm.ins) if comm else 0
    n_co = len(comm.out_shapes) if comm else 0
    cuts, i = [], 0
    for n in (n_in, n_ci, n_out, n_co, n_scratch):
        cuts.append(refs[i:i + n])
        i += n
    return (*cuts, refs[i:])


def _grid_edge(grid, last):
    cond = None
    for d, n in enumerate(grid):
        here = pl.program_id(d) == (n - 1 if last else 0)
        cond = here if cond is None else cond & here
    return cond


def _comm_start(comm, cin, cout, csem, grid):
    if comm is not None:
        @pl.when(_grid_edge(grid, False))
        def _():
            for cp in comm.make(cin, cout, csem):
                cp.start()


def _comm_finish(comm, cin, cout, csem, grid):
    if comm is not None:
        @pl.when(_grid_edge(grid, True))
        def _():
            for cp in comm.make(cin, cout, csem):
                cp.wait()


def gather_first_comm(shard):
    r, w = shard.shape

    def make(cin, cout, sems):
        (x_ref,), (out_ref,), (send_sems, recv_sems, local_sem) = cin, cout, sems
        x, y, c = _mesh_pos()
        mine = out_ref.at[4 * x + 2 * y + c]
        targets = [(x, y, 1 - c), (1 - x, y, c), (x, 1 - y, c), (1 - x, 1 - y, c)]
        return [_remote(x_ref, mine, send_sems, recv_sems, k, to) for k, to in enumerate(targets)] + [
            pltpu.make_async_copy(x_ref, mine, local_sem)]

    return _Comm([shard], [jax.ShapeDtypeStruct((N_DEV, r, w), shard.dtype)],
                 [pltpu.SemaphoreType.DMA((4,)), pltpu.SemaphoreType.DMA((4,)), pltpu.SemaphoreType.DMA], make)


def gather_pass_comm(full):
    def make(cin, cout, sems):
        (in_ref,), (out_ref,), (send_sems, recv_sems) = cin, cout, sems
        x, y, c = _mesh_pos()
        copies = []
        for k, (px, py) in enumerate([(1 - x, y), (x, 1 - y), (1 - x, 1 - y)]):
            idx = 4 * px + 2 * py + c
            copies.append(_remote(in_ref.at[idx], out_ref.at[idx], send_sems, recv_sems, k, (x, y, 1 - c)))
        return copies

    return _Comm([full], [jax.ShapeDtypeStruct(full.shape, full.dtype)],
                 [pltpu.SemaphoreType.DMA((3,)), pltpu.SemaphoreType.DMA((3,))], make, aliases={0: 0})


def scatter_sibling_comm(g):
    _, r, w = g.shape
    return _Comm([g], [jax.ShapeDtypeStruct((N_CHIP, r, w), g.dtype)],
                 [pltpu.SemaphoreType.DMA((N_CHIP,)), pltpu.SemaphoreType.DMA((N_CHIP,))],
                 lambda cin, cout, sems: _sibling_copies(cin[0], cout[0], sems[0], sems[1]))


def scatter_chips_comm(part):
    _, r, w = part.shape
    return _Comm([part], [jax.ShapeDtypeStruct((N_CHIP - 1, r, w), part.dtype)],
                 [pltpu.SemaphoreType.DMA((3,)), pltpu.SemaphoreType.DMA((3,))],
                 lambda cin, cout, sems: _chip_copies(cin[0], cout[0], sems[0], sems[1]))


def exchange_sibling(g):
    _, r, w = g.shape

    def body(g_ref, got_ref, send_sems, recv_sems):
        copies = _sibling_copies(g_ref, got_ref, send_sems, recv_sems)
        for cp in copies:
            cp.start()
        for cp in copies:
            cp.wait()

    return pl.pallas_call(
        body, name="exchange_sibling", out_shape=jax.ShapeDtypeStruct((N_CHIP, r, w), g.dtype),
        in_specs=[_ANY], out_specs=_ANY,
        scratch_shapes=[pltpu.SemaphoreType.DMA((N_CHIP,)), pltpu.SemaphoreType.DMA((N_CHIP,))],
    )(g)


def exchange_chips(part):
    _, r, w = part.shape

    def body(p_ref, land_ref, send_sems, recv_sems):
        copies = _chip_copies(p_ref, land_ref, send_sems, recv_sems)
        for cp in copies:
            cp.start()
        for cp in copies:
            cp.wait()

    return pl.pallas_call(
        body, name="exchange_chips", out_shape=jax.ShapeDtypeStruct((N_CHIP - 1, r, w), part.dtype),
        in_specs=[_ANY], out_specs=_ANY,
        scratch_shapes=[pltpu.SemaphoreType.DMA((3,)), pltpu.SemaphoreType.DMA((3,))],
    )(part)


def allreduce_stats(st_mix, st_prep, st_mlp, st_ple):
    def body(mix_ref, prep_ref, mlp_ref, ple_ref, out_ref, mine, gath, send_sems, recv_sems):
        x, y, c = _mesh_pos()
        me = 4 * x + 2 * y + c
        mine[...] = jnp.zeros_like(mine)
        mine[ST_G_MIX:ST_G_MIX + 1, :] = mix_ref[...]
        mine[ST_G_QA:ST_G_KN + 1, 0:256] = prep_ref[...]
        mine[ST_G_MLP:ST_G_MLP + 1, :] = mlp_ref[...]
        mine[ST_G_PLE:ST_LOSS + 1, :] = ple_ref[...]
        gath[me] = mine[...]
        copies = []
        for k in range(1, N_DEV):
            peer = (_flip(x, k & 4), _flip(y, k & 2), _flip(c, k & 1))
            copies.append(_remote(mine, gath.at[me], send_sems, recv_sems, k - 1, peer))
        for cp in copies:
            cp.start()
        for cp in copies:
            cp.wait()
        acc = gath[0]
        for d in range(1, N_DEV):
            acc = acc + gath[d]
        out_ref[...] = acc

    vm = pl.BlockSpec(memory_space=pltpu.VMEM)
    return pl.pallas_call(
        body, name="allreduce_stats", out_shape=jax.ShapeDtypeStruct((ST_ROWS, D_MODEL), F32),
        in_specs=[vm] * 4, out_specs=vm,
        scratch_shapes=[pltpu.VMEM((ST_ROWS, D_MODEL), F32), pltpu.VMEM((N_DEV, ST_ROWS, D_MODEL), F32),
                        pltpu.SemaphoreType.DMA((N_DEV - 1,)), pltpu.SemaphoreType.DMA((N_DEV - 1,))],
    )(st_mix, st_prep, st_mlp, st_ple)


def adamw_gains(stats, gains):
    c1 = 1.0 - ADAM_B1 ** ADAM_STEP
    c2 = 1.0 - ADAM_B2 ** ADAM_STEP
    n = len(gains)

    def body(st_ref, *refs):
        ins, outs = refs[:3 * n], refs[3 * n:]
        for i, (row, w, _, _) in enumerate(gains):
            width = w.shape[1]
            gv = st_ref[row:row + 1, 0:width]
            mn = ADAM_B1 * ins[3 * i + 1][...] + (1.0 - ADAM_B1) * gv
            vn = ADAM_B2 * ins[3 * i + 2][...] + (1.0 - ADAM_B2) * (gv * gv)
            outs[4 * i][...] = gv
            outs[4 * i + 1][...] = -ADAM_LR * ((mn / c1) / (jnp.sqrt(vn / c2) + ADAM_EPS) + ADAM_WD * ins[3 * i][...])
            outs[4 * i + 2][...] = mn
            outs[4 * i + 3][...] = vn

    vm = pl.BlockSpec(memory_space=pltpu.VMEM)
    flat = [a for (_, w, m, v) in gains for a in (w, m, v)]
    out_shape = tuple(jax.ShapeDtypeStruct(w.shape, F32) for (_, w, _, _) in gains for _ in range(4))
    res = pl.pallas_call(body, name="adamw_gains", out_shape=out_shape, in_specs=[vm] * (1 + 3 * n),
                         out_specs=tuple([vm] * (4 * n)))(stats, *flat)
    return [res[4 * i:4 * i + 4] for i in range(n)]


def _row_tile(r, cap=640):
    return max(d for d in range(16, min(r, cap) + 1, 16) if r % d == 0)


def add_pairs(g, got, core):
    n, r, w = got.shape
    tr = _row_tile(r)

    def body(c_ref, a_ref, b_ref, o_ref):
        o_ref[...] = (a_ref[...].astype(F32) + b_ref[...].astype(F32)).astype(o_ref.dtype)

    spec = pl.BlockSpec((1, tr, w), lambda i, j, c: (i, j, 0))
    return pl.pallas_call(
        body, name="add_pairs", out_shape=jax.ShapeDtypeStruct(got.shape, got.dtype),
        grid_spec=pltpu.PrefetchScalarGridSpec(
            num_scalar_prefetch=1, grid=(n, r // tr),
            in_specs=[pl.BlockSpec((1, tr, w), lambda i, j, c: (2 * i + c[0], j, 0)), spec], out_specs=spec),
        compiler_params=_params(("parallel", "parallel")),
    )(core, g, got)


def sum_chips(part, land, chip):
    _, r, w = part.shape
    tr = _row_tile(r)

    def body(c_ref, p_ref, l_ref, o_ref):
        acc = p_ref[0].astype(F32)
        for s in range(N_CHIP - 1):
            acc = acc + l_ref[s].astype(F32)
        o_ref[...] = acc

    return pl.pallas_call(
        body, name="sum_chips", out_shape=jax.ShapeDtypeStruct((r, w), F32),
        grid_spec=pltpu.PrefetchScalarGridSpec(
            num_scalar_prefetch=1, grid=(r // tr,),
            in_specs=[pl.BlockSpec((1, tr, w), lambda i, c: (c[0], i, 0)), pl.BlockSpec((N_CHIP - 1, tr, w), lambda i, c: (0, i, 0))],
            out_specs=pl.BlockSpec((tr, w), lambda i, c: (i, 0))),
        compiler_params=_params(("parallel",)),
    )(chip, part, land)


def norm_x(x, g_mix, tm, comm=None):
    t = x.shape[0]
    grid = (t // tm,)
    c_ins, c_in_specs, c_outs, c_sems, alias = _comm_parts(comm, 2, 1)

    def body(*refs):
        (x_ref, g_ref), cin, (h_ref,), cout, _, csem = _split_refs(refs, 2, 1, 0, comm)
        _comm_start(comm, cin, cout, csem, grid)
        xv = x_ref[...]
        h_ref[...] = (xv * _rstd(xv, D_MODEL) * g_ref[...]).astype(BF16)
        _comm_finish(comm, cin, cout, csem, grid)

    return pl.pallas_call(
        body, name="norm_x", grid=grid, out_shape=(jax.ShapeDtypeStruct((t, D_MODEL), BF16), *c_outs),
        in_specs=[_rows(tm, D_MODEL), _resident((1, D_MODEL)), *c_in_specs],
        out_specs=(_rows(tm, D_MODEL), *([_ANY] * len(c_outs))),
        scratch_shapes=c_sems, input_output_aliases=alias, compiler_params=_params(("arbitrary",)),
    )(x, g_mix, *c_ins)


def pack_late_weights(w_up, w_down, w_o, w_pg, small, comm=None):
    rows2 = sum(PACK_ROWS[n] for n in PACK_W2)
    rows3 = sum(PACK_ROWS[n] for n in PACK_W3)
    c_ins, c_in_specs, c_outs, c_sems, alias = _comm_parts(comm, 5, 2)
    grid = (1,)

    def body(*refs):
        (up_ref, dn_ref, o_ref, pg_ref, sm_ref), cin, (p2_ref, p3_ref), cout, _, csem = _split_refs(refs, 5, 2, 0, comm)
        _comm_start(comm, cin, cout, csem, grid)
        p2_ref[0:128, :] = o_ref[0].astype(BF16)
        p2_ref[128:256, :] = pg_ref[0].astype(BF16)
        p2_ref[256:rows2, :] = sm_ref[...]
        p3_ref[0:512, :] = up_ref[0].T.astype(BF16)
        p3_ref[512:1024, :] = dn_ref[0].astype(BF16)
        _comm_finish(comm, cin, cout, csem, grid)

    def whole(a):
        nd = a.ndim
        return pl.BlockSpec(a.shape, lambda i: (0,) * nd)

    args = (w_up, w_down, w_o, w_pg, small)
    return pl.pallas_call(
        body, name="pack_late_weights", grid=grid,
        out_shape=(jax.ShapeDtypeStruct((rows2, D_MODEL), BF16), jax.ShapeDtypeStruct((rows3, D_MODEL), BF16), *c_outs),
        in_specs=[*[whole(a) for a in args], *c_in_specs],
        out_specs=(pl.BlockSpec((rows2, D_MODEL), lambda i: (0, 0)), pl.BlockSpec((rows3, D_MODEL), lambda i: (0, 0)),
                   *([_ANY] * len(c_outs))),
        scratch_shapes=c_sems, input_output_aliases=alias, compiler_params=_params(("arbitrary",)),
    )(*args, *c_ins)


def in_proj(h, w_in_t, tm, comm=None):
    t = h.shape[0]
    nc = 512
    grid = (t // tm,)
    c_ins, c_in_specs, c_outs, c_sems, alias = _comm_parts(comm, 2, 1)

    def body(*refs):
        (h_ref, w_ref), cin, (z_ref,), cout, _, csem = _split_refs(refs, 2, 1, 0, comm)
        _comm_start(comm, cin, cout, csem, grid)
        hv = h_ref[...]
        for cidx in range(ZP // nc):
            z_ref[:, cidx * nc:(cidx + 1) * nc] = _dot_nt(hv, w_ref[cidx * nc:(cidx + 1) * nc, :])
        _comm_finish(comm, cin, cout, csem, grid)

    return pl.pallas_call(
        body, name="in_proj", grid=grid, out_shape=(jax.ShapeDtypeStruct((t, ZP), F32), *c_outs),
        in_specs=[_rows(tm, D_MODEL), _resident((ZP, D_MODEL)), *c_in_specs],
        out_specs=(_rows(tm, ZP), *([_ANY] * len(c_outs))),
        scratch_shapes=c_sems, input_output_aliases=alias, compiler_params=_params(("arbitrary",)),
    )(h, w_in_t, *c_ins)


def attn_prep(zp, tabs, g_qa, g_kva, g_qn, g_kn, w_qb_t, w_kvb_t, tm, s_len, comm=None):
    t = zp.shape[0]
    nsb = s_len // tm
    scale_a = (QK_NOPE + QK_ROPE) ** -0.5
    scale_b = HD_B ** -0.5

    grid = (t // tm,)
    c_ins, c_in_specs, c_outs, c_sems, alias = _comm_parts(comm, 13, 8)

    def body(*refs):
        ((qb_ref, qlat_ref, kb_ref, vb_ref, ckv_ref, kpe_ref, tab_ref, gqa_ref, gkva_ref, gqn_ref, gkn_ref, wqb_ref,
          wkvb_ref), cin, (qa_o, ka_o, va_o, qb_o, kb_o, vb_o, cq_o, ckvn_o), cout, _, csem) = _split_refs(refs, 13, 8, 0, comm)
        _comm_start(comm, cin, cout, csem, grid)
        ca, s1a, s2a = tab_ref[0], tab_ref[1], tab_ref[2]
        ck = tab_ref[3]
        cb, s1b, s2b = tab_ref[4], tab_ref[5], tab_ref[6]
        ql = qlat_ref[...]
        cq = (ql * _rstd(ql, Q_LORA) * gqa_ref[...]).astype(BF16)
        cq_o[...] = cq
        qa = _dot_nt(cq, wqb_ref[...])
        slabs = [slice(h * HP, (h + 1) * HP) for h in range(H_A)]
        qa_o[...] = jnp.concatenate(
            [(_rope_fwd(qa[:, sl], ca, s1a, s2a) * scale_a).astype(BF16) for sl in slabs], axis=1)
        cr = ckv_ref[...]
        ckv = (cr * _rstd(cr, KV_LORA) * gkva_ref[...]).astype(BF16)
        ckvn_o[...] = ckv
        kva = _dot_nt(ckv, wkvb_ref[...])
        kpe = _rope_fwd(kpe_ref[...], ck, s1a, s2a)
        ka_o[...] = jnp.concatenate([(kva[:, sl] + kpe).astype(BF16) for sl in slabs], axis=1)
        va_o[...] = kva[:, H_A * HP:].astype(BF16)
        gqn, gkn = gqn_ref[...], gkn_ref[...]

        def norm_rope(ref, sl, g, scale):
            xs = ref[:, sl]
            y = _rope_fwd(xs * _rstd(xs, HD_B) * g, cb, s1b, s2b)
            return (y if scale is None else y * scale).astype(BF16)

        qb_o[...] = jnp.concatenate([norm_rope(qb_ref, sl, gqn, scale_b) for sl in slabs], axis=1)
        kb_o[...] = jnp.concatenate([norm_rope(kb_ref, sl, gkn, None) for sl in slabs[:KV_B]], axis=1)
        vb_o[...] = vb_ref[...].astype(BF16)
        _comm_finish(comm, cin, cout, csem, grid)

    def o(width):
        return jax.ShapeDtypeStruct((t, width), BF16)

    return pl.pallas_call(
        body, name="attn_prep", grid=grid,
        out_shape=(o(H_A * HP), o(H_A * HP), o(H_A * HP), o(H_B * HP), o(KV_B * HP), o(KV_B * HP), o(Q_LORA), o(KV_LORA),
                   *c_outs),
        in_specs=[_rows(tm, 1024, 2), _rows(tm, 256, 12), _rows(tm, 256, 13), _rows(tm, 256, 14),
                  _rows(tm, 128, 30), _rows(tm, 128, 31),
                  pl.BlockSpec((7, tm, HP), lambda i: (0, i % nsb, 0)),
                  _resident((1, Q_LORA)), _resident((1, KV_LORA)), _resident((1, HP)), _resident((1, HP)),
                  _resident((H_A * HP, Q_LORA)), _resident((2 * H_A * HP, KV_LORA)), *c_in_specs],
        out_specs=(_rows(tm, H_A * HP), _rows(tm, H_A * HP), _rows(tm, H_A * HP), _rows(tm, H_B * HP),
                   _rows(tm, KV_B * HP), _rows(tm, KV_B * HP), _rows(tm, Q_LORA), _rows(tm, KV_LORA), *([_ANY] * len(c_outs))),
        scratch_shapes=c_sems, input_output_aliases=alias, compiler_params=_params(("arbitrary",)),
    )(zp, zp, zp, zp, zp, zp, tabs, g_qa, g_kva, g_qn, g_kn, w_qb_t, w_kvb_t, *c_ins)


def attn_fwd(q, k, v, n_b, s_len, tq, name, comm=None):
    t = q.shape[0]
    n_h, n_hk = q.shape[1] // HP, k.shape[1] // HP
    grp = n_h // n_hk
    nq = s_len // tq
    sub = min(tq, 256)
    grid = (n_b, n_h, nq)
    c_ins, c_in_specs, c_outs, c_sems, alias = _comm_parts(comm, 3, 2)

    def body(*refs):
        (q_ref, k_ref, v_ref), cin, (o_ref, lse_ref), cout, _, csem = _split_refs(refs, 3, 2, 0, comm)
        _comm_start(comm, cin, cout, csem, grid)
        kv, vv = k_ref[...], v_ref[...]
        for r in range(tq // sub):
            rows = slice(r * sub, (r + 1) * sub)
            s = _dot_nt(q_ref[rows, :], kv)
            m = jnp.max(s, axis=-1, keepdims=True)
            p = jnp.exp(s - m)
            l = jnp.sum(p, axis=-1, keepdims=True)
            o_ref[rows, :] = (_dot_nn(p.astype(BF16), vv) * (1.0 / l)).astype(o_ref.dtype)
            lse_ref[rows, :] = jnp.broadcast_to(m + jnp.log(l), (sub, HP))
        _comm_finish(comm, cin, cout, csem, grid)

    qspec = pl.BlockSpec((tq, HP), lambda b, h, i: (b * nq + i, h))
    kspec = pl.BlockSpec((s_len, HP), lambda b, h, i: (b, h // grp))
    return pl.pallas_call(
        body, name=name, grid=grid,
        out_shape=(jax.ShapeDtypeStruct((t, n_h * HP), BF16), jax.ShapeDtypeStruct((t, n_h * HP), F32), *c_outs),
        in_specs=[qspec, kspec, kspec, *c_in_specs], out_specs=(qspec, qspec, *([_ANY] * len(c_outs))),
        scratch_shapes=c_sems, input_output_aliases=alias,
        compiler_params=_params(("arbitrary", "arbitrary", "arbitrary")),
    )(q, k, v, *c_ins)


def merge_fwd(oa, ob, zp, x, w_oa_t, w_ob_t, wpack, off, tm):
    t = x.shape[0]

    def body(oa_ref, ob_ref, ga_ref, gb_ref, x_ref, woa_ref, wob_ref, wo_ref, x1_o, mg_o, y_o):
        ya = _dot_nt(oa_ref[...], woa_ref[...])
        yb = _dot_nt(ob_ref[...], wob_ref[...])
        y_o[:, 0:D_MODEL] = ya.astype(BF16)
        y_o[:, D_MODEL:2 * D_MODEL] = yb.astype(BF16)
        merged = (jax.nn.sigmoid(ga_ref[...]) * ya + jax.nn.sigmoid(gb_ref[...]) * yb).astype(BF16)
        mg_o[...] = merged
        x1_o[...] = x_ref[...] + _dot_nn(merged, _wrows(wo_ref, 0, D_MODEL))

    return pl.pallas_call(
        body, name="merge_fwd", grid=(t // tm,),
        out_shape=(jax.ShapeDtypeStruct((t, D_MODEL), F32), jax.ShapeDtypeStruct((t, D_MODEL), BF16),
                   jax.ShapeDtypeStruct((t, 2 * D_MODEL), BF16)),
        in_specs=[_rows(tm, H_A * HP), _rows(tm, H_B * HP), _rows(tm, 1024, 0), _rows(tm, 1024, 1), _rows(tm, D_MODEL),
                  _resident((D_MODEL, H_A * HP)), _resident((D_MODEL, H_B * HP)), _packed_weight(128, off["w_o"])],
        out_specs=(_rows(tm, D_MODEL), _rows(tm, D_MODEL), _rows(tm, 2 * D_MODEL)), compiler_params=_params(("parallel",)),
    )(oa, ob, zp, zp, x, w_oa_t, w_ob_t, wpack)


def mlp_fwd(x1, g_mlp, wpack, off, tm):
    t = x1.shape[0]
    fc = 1024

    def body(x_ref, g_ref, wup_ref, wdn_ref, x2_o, u_o):
        xv = x_ref[...]
        h2 = (xv * _rstd(xv, D_MODEL) * g_ref[...]).astype(BF16)
        acc = xv
        for cidx in range(D_FF // fc):
            sl = slice(cidx * fc, (cidx + 1) * fc)
            u = jnp.maximum(_dot_nt(h2, _wrows(wup_ref, cidx * fc, fc)), 0.0)
            u_o[:, sl] = u.astype(BF16)
            acc = acc + _dot_nn((u * u).astype(BF16), _wrows(wdn_ref, cidx * fc, fc))
        x2_o[...] = acc

    return pl.pallas_call(
        body, name="mlp_fwd", grid=(t // tm,),
        out_shape=(jax.ShapeDtypeStruct((t, D_MODEL), F32), jax.ShapeDtypeStruct((t, D_FF), BF16)),
        in_specs=[_rows(tm, D_MODEL), _resident((1, D_MODEL)), _packed_weight(512, off["w_up"]), _packed_weight(512, off["w_down"])],
        out_specs=(_rows(tm, D_MODEL), _rows(tm, D_FF)), compiler_params=_params(("parallel",)),
    )(x1, g_mlp, wpack, wpack)


def ple_loss_bwd(x2, p, tgt, g_ple, g_final, wpack, off, w_ple_t, tm):
    t = x2.shape[0]
    inv_d = 1.0 / D_MODEL

    def body(x2_ref, p_ref, tg_ref, gp_ref, gf_ref, wpg_ref, wple_ref, dx2_o, dt_o, h3_o, dpe_o, st_o, dx2b_o):
        @pl.when(pl.program_id(0) == 0)
        def _():
            st_o[...] = jnp.zeros_like(st_o)

        x2v = x2_ref[...]
        gp, gf = gp_ref[...], gf_ref[...]
        w_pg = _wrows(wpg_ref, 0, D_MODEL)
        r2 = _rstd(x2v, D_MODEL)
        xh2 = x2v * r2
        h3 = (xh2 * gp).astype(BF16)
        h3_o[...] = h3
        gate = jax.nn.sigmoid(_dot_nn(h3, w_pg))
        pe = _dot_nt(p_ref[...].astype(BF16), wple_ref[...])
        x3 = x2v + gate * pe
        r3 = _rstd(x3, D_MODEL)
        xh3 = x3 * r3
        err = xh3 * gf - tg_ref[...]
        dy = err * inv_d
        dx3 = _rms_bwd(dy, xh3, r3, gf, D_MODEL)
        dpe_o[...] = (dx3 * gate).astype(BF16)
        dt = (dx3 * pe * gate * (1.0 - gate)).astype(BF16)
        dt_o[...] = dt
        dh3 = _dot_nt(dt, w_pg)
        dx2 = dx3 + _rms_bwd(dh3, xh2, r2, gp, D_MODEL)
        dx2_o[...] = dx2
        dx2b_o[...] = dx2.astype(BF16)
        st_o[0:1, :] += _colsum(dh3 * xh2)
        st_o[1:2, :] += _colsum(dy * xh3)
        st_o[2:3, :] += _colsum(err * err) * (0.5 * inv_d)

    bf = jax.ShapeDtypeStruct((t, D_MODEL), BF16)
    return pl.pallas_call(
        body, name="ple_loss_bwd", grid=(t // tm,),
        out_shape=(jax.ShapeDtypeStruct((t, D_MODEL), F32), bf, bf, bf, jax.ShapeDtypeStruct((3, D_MODEL), F32), bf),
        in_specs=[_rows(tm, D_MODEL), _rows(tm, PLE_DIM), _rows(tm, D_MODEL), _resident((1, D_MODEL)), _resident((1, D_MODEL)),
                  _packed_weight(128, off["w_ple_gate"]), _resident((D_MODEL, PLE_DIM))],
        out_specs=(_rows(tm, D_MODEL), _rows(tm, D_MODEL), _rows(tm, D_MODEL), _rows(tm, D_MODEL),
                   pl.BlockSpec((3, D_MODEL), lambda i: (0, 0)), _rows(tm, D_MODEL)),
        compiler_params=_params(("arbitrary",)),
    )(x2, p, tgt, g_ple, g_final, wpack, w_ple_t)


def mlp_bwd(dx2, x1, u, g_mlp, wpack, off, tm):
    t = x1.shape[0]
    fc = 1024

    def body(dx2_ref, x1_ref, u_ref, g_ref, wup_ref, wdn_ref, dx1_o, da_o, h2_o, st_o, dx1b_o):
        @pl.when(pl.program_id(0) == 0)
        def _():
            st_o[...] = jnp.zeros_like(st_o)

        d2 = dx2_ref[...]
        d2b = d2.astype(BF16)
        dh2 = jnp.zeros((tm, D_MODEL), F32)
        for cidx in range(D_FF // fc):
            sl = slice(cidx * fc, (cidx + 1) * fc)
            da = (_dot_nt(d2b, _wrows(wdn_ref, cidx * fc, fc)) * (2.0 * u_ref[:, sl].astype(F32))).astype(BF16)
            da_o[:, sl] = da
            dh2 = dh2 + _dot_nn(da, _wrows(wup_ref, cidx * fc, fc))
        xv = x1_ref[...]
        g = g_ref[...]
        r1 = _rstd(xv, D_MODEL)
        xh1 = xv * r1
        h2_o[...] = (xh1 * g).astype(BF16)
        st_o[...] += _colsum(dh2 * xh1)
        dx1 = d2 + _rms_bwd(dh2, xh1, r1, g, D_MODEL)
        dx1_o[...] = dx1
        dx1b_o[...] = dx1.astype(BF16)

    return pl.pallas_call(
        body, name="mlp_bwd", grid=(t // tm,),
        out_shape=(jax.ShapeDtypeStruct((t, D_MODEL), F32), jax.ShapeDtypeStruct((t, D_FF), BF16),
                   jax.ShapeDtypeStruct((t, D_MODEL), BF16), jax.ShapeDtypeStruct((1, D_MODEL), F32),
                   jax.ShapeDtypeStruct((t, D_MODEL), BF16)),
        in_specs=[_rows(tm, D_MODEL), _rows(tm, D_MODEL), _rows(tm, D_FF), _resident((1, D_MODEL)),
                  _packed_weight(512, off["w_up"]), _packed_weight(512, off["w_down"])],
        out_specs=(_rows(tm, D_MODEL), _rows(tm, D_FF), _rows(tm, D_MODEL), pl.BlockSpec((1, D_MODEL), lambda i: (0, 0)),
                   _rows(tm, D_MODEL)),
        compiler_params=_params(("arbitrary",)),
    )(dx2, x1, u, g_mlp, wpack, wpack)


def merge_bwd(dx1b, yab, zp, w_oa_t, w_ob_t, wpack, off, tm, comm=None):
    t = dx1b.shape[0]
    grid = (t // tm,)
    c_ins, c_in_specs, c_outs, c_sems, alias = _comm_parts(comm, 7, 5)

    def body(*refs):
        ((dx1_ref, y_ref, ga_ref, gb_ref, woa_ref, wob_ref, wo_ref), cin,
         (doa_o, dob_o, dg_o, dya_o, dyb_o), cout, _, csem) = _split_refs(refs, 7, 5, 0, comm)
        _comm_start(comm, cin, cout, csem, grid)
        dm = _dot_nt(dx1_ref[...], _wrows(wo_ref, 0, D_MODEL))
        for g_ref, w_ref, do_o, dy_o, col in ((ga_ref, woa_ref, doa_o, dya_o, 0), (gb_ref, wob_ref, dob_o, dyb_o, 1)):
            cols = slice(col * D_MODEL, (col + 1) * D_MODEL)
            sg = jax.nn.sigmoid(g_ref[...])
            dyv = (dm * sg).astype(BF16)
            dy_o[...] = dyv
            dg_o[:, cols] = (dm * y_ref[:, cols].astype(F32) * sg * (1.0 - sg)).astype(BF16)
            do_o[...] = _dot_nn(dyv, w_ref[...]).astype(BF16)
        _comm_finish(comm, cin, cout, csem, grid)

    bf = jax.ShapeDtypeStruct((t, D_MODEL), BF16)
    return pl.pallas_call(
        body, name="merge_bwd", grid=grid,
        out_shape=(bf, bf, jax.ShapeDtypeStruct((t, ZP), BF16), bf, bf, *c_outs),
        in_specs=[_rows(tm, D_MODEL), _rows(tm, 2 * D_MODEL), _rows(tm, 1024, 0), _rows(tm, 1024, 1),
                  _resident((D_MODEL, H_A * HP)), _resident((D_MODEL, H_B * HP)), _packed_weight(128, off["w_o"]), *c_in_specs],
        out_specs=(_rows(tm, D_MODEL), _rows(tm, D_MODEL), _rows(tm, 2 * D_MODEL), _rows(tm, D_MODEL), _rows(tm, D_MODEL),
                   *([_ANY] * len(c_outs))),
        scratch_shapes=c_sems, input_output_aliases=alias,
        compiler_params=_params(("arbitrary",)),
    )(dx1b, yab, zp, zp, w_oa_t, w_ob_t, wpack, *c_ins)


def attn_bwd(q, k, v, do, o, lse, n_b, s_len, tq, name, comm=None):
    t = q.shape[0]
    n_h, n_hk = q.shape[1] // HP, k.shape[1] // HP
    grp = n_h // n_hk
    nq = s_len // tq
    sub = min(tq, 256)
    grid = (n_b, n_hk, grp, nq)
    c_ins, c_in_specs, c_outs, c_sems, alias = _comm_parts(comm, 6, 3)

    def body(*refs):
        ((q_ref, k_ref, v_ref, do_ref, o_ref, lse_ref), cin, (dq_o, dk_o, dv_o), cout, (p_s, ds_s, dk_acc, dv_acc),
         csem) = _split_refs(refs, 6, 3, 4, comm)
        _comm_start(comm, cin, cout, csem, grid)

        @pl.when((pl.program_id(2) == 0) & (pl.program_id(3) == 0))
        def _():
            dk_acc[...] = jnp.zeros_like(dk_acc)
            dv_acc[...] = jnp.zeros_like(dv_acc)

        kv, vv = k_ref[...], v_ref[...]
        for r in range(tq // sub):
            rows = slice(r * sub, (r + 1) * sub)
            qv, dov = q_ref[rows, :], do_ref[rows, :]
            delta = jnp.sum(dov.astype(F32) * o_ref[rows, :].astype(F32), axis=-1, keepdims=True)
            delta_row = jnp.broadcast_to(delta, (sub, HP)).T[0:1, :]
            lse_row = lse_ref[rows, :].T[0:1, :]
            pt = jnp.exp(_dot_nt(kv, qv) - lse_row)
            dst = (pt * (_dot_nt(vv, dov) - delta_row)).astype(BF16)
            p_s[:, rows] = pt.astype(BF16)
            ds_s[:, rows] = dst
            dq_o[rows, :] = _dot_tn(dst, kv).astype(dq_o.dtype)
        dk_acc[...] += _dot_nn(ds_s[...], q_ref[...])
        dv_acc[...] += _dot_nn(p_s[...], do_ref[...])

        @pl.when((pl.program_id(2) == grp - 1) & (pl.program_id(3) == nq - 1))
        def _():
            dk_o[...] = dk_acc[...].astype(dk_o.dtype)
            dv_o[...] = dv_acc[...].astype(dv_o.dtype)

        _comm_finish(comm, cin, cout, csem, grid)

    qspec = pl.BlockSpec((tq, HP), lambda b, hk, g, i: (b * nq + i, hk * grp + g))
    kspec = pl.BlockSpec((s_len, HP), lambda b, hk, g, i: (b, hk))
    return pl.pallas_call(
        body, name=name, grid=grid,
        out_shape=(jax.ShapeDtypeStruct((t, n_h * HP), BF16), jax.ShapeDtypeStruct((t, n_hk * HP), BF16),
                   jax.ShapeDtypeStruct((t, n_hk * HP), BF16), *c_outs),
        in_specs=[qspec, kspec, kspec, qspec, qspec, qspec, *c_in_specs],
        out_specs=(qspec, kspec, kspec, *([_ANY] * len(c_outs))),
        scratch_shapes=[pltpu.VMEM((s_len, tq), BF16), pltpu.VMEM((s_len, tq), BF16),
                        pltpu.VMEM((s_len, HP), F32), pltpu.VMEM((s_len, HP), F32), *c_sems],
        input_output_aliases=alias,
        compiler_params=_params(("arbitrary", "arbitrary", "arbitrary", "arbitrary")),
    )(q, k, v, do, o, lse, *c_ins)


def prep_bwd(dqa, dka, dva, dqb, dkb, dvb, zp, dz, tabs, g_qa, g_kva, g_qn, g_kn, w_qb_t, w_kvb_t, tm, s_len):
    t = zp.shape[0]
    nsb = s_len // tm
    scale_a = (QK_NOPE + QK_ROPE) ** -0.5
    scale_b = HD_B ** -0.5

    def body(dqa_ref, dka_ref, dva_ref, dqb_ref, dkb_ref, dvb_ref, qb_ref, qlat_ref, kb_ref, ckv_ref, tab_ref,
             gqa_ref, gkva_ref, gqn_ref, gkn_ref, wqb_ref, wkvb_ref, _, dz_o, dqap_o, dkva_o, st_o):
        dzq_o, dsm_o = dz_o.at[:, 0:1024], dz_o.at[:, 1024:2048]

        @pl.when(pl.program_id(0) == 0)
        def _():
            st_o[...] = jnp.zeros_like(st_o)

        ca, s1a, s2a = tab_ref[0], tab_ref[1], tab_ref[2]
        ck = tab_ref[3]
        cb, s1b, s2b = tab_ref[4], tab_ref[5], tab_ref[6]
        for h in range(H_A):
            sl = slice(h * HP, (h + 1) * HP)
            dqap_o[:, sl] = _rope_bwd(dqa_ref[:, sl].astype(F32) * scale_a, ca, s1a, s2a).astype(BF16)
        dcq = _dot_nn(dqap_o[...], wqb_ref[...])
        ql = qlat_ref[...]
        rq = _rstd(ql, Q_LORA)
        xh = ql * rq
        gqa = gqa_ref[...]
        st_o[0:1, :] += _colsum(dcq * xh)
        dsm_o[:, 0:256] = _rms_bwd(dcq, xh, rq, gqa, Q_LORA).astype(BF16)
        dkpe = jnp.zeros((tm, HP), F32)
        for h in range(H_A):
            sl = slice(h * HP, (h + 1) * HP)
            dk = dka_ref[:, sl]
            dkpe = dkpe + dk.astype(F32)
            dkva_o[:, sl] = dk.astype(BF16)
        dkva_o[:, H_A * HP:] = dva_ref[...].astype(BF16)
        dsm_o[:, 896:1024] = _rope_bwd(dkpe, ck, s1a, s2a).astype(BF16)
        dckv = _dot_nn(dkva_o[...], wkvb_ref[...])
        cr = ckv_ref[...]
        rk = _rstd(cr, KV_LORA)
        xh = cr * rk
        st_o[1:2, 0:128] += _colsum(dckv * xh)
        dsm_o[:, 768:896] = _rms_bwd(dckv, xh, rk, gkva_ref[...], KV_LORA).astype(BF16)
        gqn, gkn = gqn_ref[...], gkn_ref[...]
        dgq = jnp.zeros((1, HP), F32)
        for h in range(H_B):
            sl = slice(h * HP, (h + 1) * HP)
            dy = _rope_bwd(dqb_ref[:, sl].astype(F32) * scale_b, cb, s1b, s2b)
            xs = qb_ref[:, sl]
            r = _rstd(xs, HD_B)
            xh = xs * r
            dgq = dgq + _colsum(dy * xh)
            dzq_o[:, sl] = _rms_bwd(dy, xh, r, gqn, HD_B).astype(BF16)
        st_o[2:3, 0:128] += dgq
        dgk = jnp.zeros((1, HP), F32)
        for h in range(KV_B):
            sl = slice(h * HP, (h + 1) * HP)
            dy = _rope_bwd(dkb_ref[:, sl].astype(F32), cb, s1b, s2b)
            xs = kb_ref[:, sl]
            r = _rstd(xs, HD_B)
            xh = xs * r
            dgk = dgk + _colsum(dy * xh)
            dsm_o[:, 256 + h * HP:256 + (h + 1) * HP] = _rms_bwd(dy, xh, r, gkn, HD_B).astype(BF16)
        st_o[3:4, 0:128] += dgk
        dsm_o[:, 512:768] = dvb_ref[...].astype(BF16)

    return pl.pallas_call(
        body, name="prep_bwd", grid=(t // tm,),
        out_shape=(jax.ShapeDtypeStruct((t, ZP), BF16), jax.ShapeDtypeStruct((t, 1024), BF16),
                   jax.ShapeDtypeStruct((t, 2048), BF16), jax.ShapeDtypeStruct((4, 256), F32)),
        in_specs=[_rows(tm, 1024), _rows(tm, 1024), _rows(tm, 1024), _rows(tm, 1024), _rows(tm, 256), _rows(tm, 256),
                  _rows(tm, 1024, 2), _rows(tm, 256, 12), _rows(tm, 256, 13), _rows(tm, 128, 30),
                  pl.BlockSpec((7, tm, HP), lambda i: (0, i % nsb, 0)),
                  _resident((1, Q_LORA)), _resident((1, KV_LORA)), _resident((1, HP)), _resident((1, HP)),
                  _resident((H_A * HP, Q_LORA)), _resident((2 * H_A * HP, KV_LORA)), _ANY],
        out_specs=(_rows(tm, 2048, 1), _rows(tm, 1024), _rows(tm, 2048), pl.BlockSpec((4, 256), lambda i: (0, 0))),
        input_output_aliases={17: 0}, compiler_params=_params(("arbitrary",)),
    )(dqa, dka, dva, dqb, dkb, dvb, zp, zp, zp, zp, tabs, g_qa, g_kva, g_qn, g_kn, w_qb_t, w_kvb_t, dz)


def in_bwd(dz, x, dx1, g_mix, w_in_t, tm, comm=None):
    t = x.shape[0]
    grid = (t // tm,)
    c_ins, c_in_specs, c_outs, c_sems, alias = _comm_parts(comm, 5, 2)

    def body(*refs):
        (dz_ref, x_ref, dx1_ref, g_ref, w_ref), cin, (dx_o, st_o), cout, _, csem = _split_refs(refs, 5, 2, 0, comm)
        _comm_start(comm, cin, cout, csem, grid)

        @pl.when(pl.program_id(0) == 0)
        def _():
            st_o[...] = jnp.zeros_like(st_o)

        dh = _dot_nn(dz_ref[...], w_ref[...])
        xv = x_ref[...]
        g = g_ref[...]
        r = _rstd(xv, D_MODEL)
        xh = xv * r
        st_o[...] += _colsum(dh * xh)
        dx_o[...] = dx1_ref[...] + _rms_bwd(dh, xh, r, g, D_MODEL)
        _comm_finish(comm, cin, cout, csem, grid)

    return pl.pallas_call(
        body, name="in_bwd", grid=grid,
        out_shape=(jax.ShapeDtypeStruct((t, D_MODEL), F32), jax.ShapeDtypeStruct((1, D_MODEL), F32), *c_outs),
        in_specs=[_rows(tm, ZP), _rows(tm, D_MODEL), _rows(tm, D_MODEL),
                  _resident((1, D_MODEL)), _resident((ZP, D_MODEL)), *c_in_specs],
        out_specs=(_rows(tm, D_MODEL), pl.BlockSpec((1, D_MODEL), lambda i: (0, 0)), *([_ANY] * len(c_outs))),
        scratch_shapes=c_sems, input_output_aliases=alias,
        compiler_params=_params(("arbitrary",)),
    )(dz, x, dx1, g_mix, w_in_t, *c_ins)


def matmul_tn(a, b, name, square_a=False):
    t, m = a.shape
    n = b.shape[1]
    bm = min(m, 512)
    tk = min(t, 4096)

    def body(a_ref, b_ref, o_ref):
        @pl.when(pl.program_id(1) == 0)
        def _():
            o_ref[...] = jnp.zeros_like(o_ref)

        av = a_ref[...]
        if square_a:
            av = (av.astype(F32) * av.astype(F32))
        o_ref[...] += _dot_tn(av.astype(BF16), b_ref[...].astype(BF16))

    return pl.pallas_call(
        body, name=name, grid=(m // bm, t // tk), out_shape=jax.ShapeDtypeStruct((m, n), F32),
        in_specs=[pl.BlockSpec((tk, bm), lambda i, kk: (kk, i)), pl.BlockSpec((tk, n), lambda i, kk: (kk, 0))],
        out_specs=pl.BlockSpec((bm, n), lambda i, kk: (i, 0)),
        compiler_params=_params(("parallel", "arbitrary")),
    )(a, b)


def matmul_tn_packed(a, b, name, rows, row_off, total_rows, buf=None, square_a=False):
    t, m = a.shape
    n = b.shape[1]
    pd = max(1, 512 // rows)
    bm = pd * rows
    tk = min(t, 4096)
    nk = t // tk

    def body(a_ref, b_ref, *rest):
        o_ref, acc = rest[-2], rest[-1]

        @pl.when(pl.program_id(1) == 0)
        def _():
            acc[...] = jnp.zeros_like(acc)

        av = a_ref[...]
        if square_a:
            av = (av.astype(F32) * av.astype(F32))
        acc[...] += _dot_tn(av.astype(BF16), b_ref[...].astype(BF16))

        @pl.when(pl.program_id(1) == nk - 1)
        def _():
            o_ref[...] = acc[...].reshape(pd, rows, n).astype(o_ref.dtype)

    in_specs = [pl.BlockSpec((tk, bm), lambda i, kk: (kk, i)), pl.BlockSpec((tk, n), lambda i, kk: (kk, 0))]
    args = [a, b]
    if buf is not None:
        in_specs.append(_ANY)
        args.append(buf)
    return pl.pallas_call(
        body, name=name, grid=(m // bm, nk), out_shape=jax.ShapeDtypeStruct((N_DEV, total_rows, n), BF16),
        in_specs=in_specs, out_specs=pl.BlockSpec((pd, rows, n), lambda i, kk: (i, row_off // rows, 0)),
        scratch_shapes=[pltpu.VMEM((bm, n), F32)], input_output_aliases={2: 0} if buf is not None else {},
        compiler_params=_params(("parallel", "arbitrary")),
    )(*args)


def adamw(w, g, m, v, name, g_transposed=False):
    _, r, c = w.shape
    tr = 256 if (not g_transposed and r > 256 and r % 256 == 0) else r
    c1 = 1.0 - ADAM_B1 ** ADAM_STEP
    c2 = 1.0 - ADAM_B2 ** ADAM_STEP

    def body(w_ref, g_ref, m_ref, v_ref, g_o, d_o, m_o, v_o):
        gv = g_ref[...].T if g_transposed else g_ref[...]
        mn = ADAM_B1 * m_ref[0] + (1.0 - ADAM_B1) * gv
        vn = ADAM_B2 * v_ref[0] + (1.0 - ADAM_B2) * (gv * gv)
        g_o[0] = gv
        m_o[0] = mn
        v_o[0] = vn
        d_o[0] = -ADAM_LR * ((mn / c1) / (jnp.sqrt(vn / c2) + ADAM_EPS) + ADAM_WD * w_ref[0])

    spec = pl.BlockSpec((1, tr, c), lambda i: (0, i, 0))
    gspec = pl.BlockSpec((c, r), lambda i: (0, 0)) if g_transposed else pl.BlockSpec((tr, c), lambda i: (i, 0))
    shp = jax.ShapeDtypeStruct((1, r, c), F32)
    return pl.pallas_call(
        body, name=name, grid=(r // tr,), out_shape=(shp,) * 4, in_specs=[spec, gspec, spec, spec], out_specs=(spec,) * 4,
        compiler_params=_params(("parallel",)),
    )(w, g, m, v)


def _rope_tables(s_len):
    def angles(pos, dim):
        inv = np.float32(ROPE_THETA) ** (-np.arange(0, dim, 2, dtype=np.float32) / np.float32(dim))
        return pos.astype(np.float32)[:, None] * inv[None, :]

    tpos = np.arange(s_len)
    a1 = angles(tpos, QK_ROPE)
    ar = angles(tpos // GRID_W, HD_B // 2)
    ac = angles(tpos % GRID_W, HD_B // 2)
    z16 = np.zeros((s_len, 16), np.float32)
    z32 = np.zeros((s_len, 32), np.float32)
    z64 = np.zeros((s_len, 64), np.float32)
    one64 = np.ones((s_len, 64), np.float32)
    c1, s1 = np.cos(a1), np.sin(a1)
    ca = np.concatenate([one64, c1, c1, z32], axis=1)
    ck = np.concatenate([z64, c1, c1, z32], axis=1)
    s1a = np.concatenate([z64, -s1, z16, z32], axis=1)
    s2a = np.concatenate([z64, z16, s1, z32], axis=1)
    cr, sr, cc, sc = np.cos(ar), np.sin(ar), np.cos(ac), np.sin(ac)
    cb = np.concatenate([cr, cr, cc, cc, z64], axis=1)
    s1b = np.concatenate([-sr, z16, -sc, z16, z64], axis=1)
    s2b = np.concatenate([z16, sr, z16, sc, z64], axis=1)
    return jnp.asarray(np.stack([ca, s1a, s2a, ck, cb, s1b, s2b]).astype(np.float32))


def _pad_heads(a, n_heads, axis):
    shp = a.shape
    a = a.reshape(shp[:axis] + (n_heads, shp[axis] // n_heads) + shp[axis + 1:])
    pad = [(0, 0)] * a.ndim
    pad[axis + 1] = (0, HP - a.shape[axis + 1])
    a = jnp.pad(a, pad)
    return a.reshape(shp[:axis] + (n_heads * HP,) + shp[axis + 1:])


def _unpad_heads(a, n_heads, width, axis):
    shp = a.shape
    a = a.reshape(shp[:axis] + (n_heads, HP) + shp[axis + 1:])
    a = lax.slice_in_dim(a, 0, width, axis=axis + 1)
    return a.reshape(shp[:axis] + (n_heads * width,) + shp[axis + 1:])


def _pack_rows(blocks, names):
    parts = []
    for name in names:
        b = blocks[name]
        padr = PACK_ROWS[name] - b.shape[-2]
        if padr:
            b = jnp.pad(b, [(0, 0)] * (b.ndim - 2) + [(0, padr), (0, 0)])
        parts.append(b)
    return jnp.concatenate(parts, axis=parts[0].ndim - 2)


def _expand_w_in(wt):
    z64 = jnp.zeros((64, D_MODEL), wt.dtype)
    z32 = jnp.zeros((32, D_MODEL), wt.dtype)
    return jnp.concatenate([
        wt[1184:2208], wt[2208:3232], _pad_heads(wt[416:928], H_B, 0), wt[0:256],
        _pad_heads(wt[928:1056], KV_B, 0), _pad_heads(wt[1056:1184], KV_B, 0), wt[256:384],
        z64, wt[384:416], z32], axis=0)


def _collapse_w_in(dw):
    dg, dq, ds = dw[0:2048], dw[2048:3072], dw[3072:4096]
    return jnp.concatenate([
        ds[0:256], ds[768:896], ds[960:992], _unpad_heads(dq, H_B, HD_B, 0), _unpad_heads(ds[256:512], KV_B, HD_B, 0),
        _unpad_heads(ds[512:768], KV_B, HD_B, 0), dg], axis=0)


def kernel(x, p, g_mix, w_in, g_qa, w_qb, g_kva, w_kvb, g_qn, g_kn, w_oa, w_ob, w_o, g_mlp, w_up, w_down, g_ple, w_ple_gate, w_ple, g_final, loss_target, m_g_mix, m_w_in, m_g_qa, m_w_qb, m_g_kva, m_w_kvb, m_g_qn, m_g_kn, m_w_oa, m_w_ob, m_w_o, m_g_mlp, m_w_up, m_w_down, m_g_ple, m_w_ple_gate, m_w_ple, m_g_final, v_g_mix, v_w_in, v_g_qa, v_w_qb, v_g_kva, v_w_kvb, v_g_qn, v_g_kn, v_w_oa, v_w_ob, v_w_o, v_g_mlp, v_w_up, v_w_down, v_g_ple, v_w_ple_gate, v_w_ple, v_g_final):
    n_b, s_len, _ = x.shape
    t = n_b * s_len
    tm = min(512, s_len)
    tq_f = min(2048, s_len)
    tq_b = min(2048, s_len)

    mats = dict(w_in=(w_in, m_w_in, v_w_in), w_qb=(w_qb, m_w_qb, v_w_qb), w_kvb=(w_kvb, m_w_kvb, v_w_kvb),
                w_oa=(w_oa, m_w_oa, v_w_oa), w_ob=(w_ob, m_w_ob, v_w_ob), w_o=(w_o, m_w_o, v_w_o),
                w_up=(w_up, m_w_up, v_w_up), w_down=(w_down, m_w_down, v_w_down),
                w_ple_gate=(w_ple_gate, m_w_ple_gate, v_w_ple_gate), w_ple=(w_ple, m_w_ple, v_w_ple))
    col_sharded = ("w_in", "w_qb", "w_kvb", "w_oa", "w_ob", "w_up", "w_ple")

    blocks = {}
    for name in PACK_W1 + ("w_oa", "w_ob", "w_ple"):
        blocks[name] = mats[name][0][0].T.reshape(-1, D_MODEL).astype(BF16)
    off_w1, _ = _pack_offsets(PACK_W1)
    off_w2, _ = _pack_offsets(PACK_W2)
    off_w3, _ = _pack_offsets(PACK_W3)
    xf = x.reshape(t, D_MODEL)
    h, full1 = norm_x(xf, g_mix, tm, comm=gather_first_comm(_pack_rows(blocks, PACK_W1)))
    pack2, pack3, full1 = pack_late_weights(w_up, w_down, w_o, w_ple_gate, _pack_rows(blocks, ("w_oa", "w_ob", "w_ple")),
                                            comm=gather_pass_comm(full1))

    def gathered(full, offs, name, rows, width):
        return full[:, offs[name]:offs[name] + rows].reshape(-1, width)

    w_in_t = _expand_w_in(gathered(full1, off_w1, "w_in", 404, D_MODEL))
    w_qb_t = _pad_heads(gathered(full1, off_w1, "w_qb", 24, Q_LORA), H_A, 0)
    wkvb = gathered(full1, off_w1, "w_kvb", 16, KV_LORA).reshape(H_A, 2, 64, KV_LORA)
    w_kvb_t = jnp.concatenate([_pad_heads(wkvb[:, 0].reshape(-1, KV_LORA), H_A, 0),
                               _pad_heads(wkvb[:, 1].reshape(-1, KV_LORA), H_A, 0)], axis=0)

    tabs = _rope_tables(s_len)
    g_qn_p = jnp.pad(g_qn, ((0, 0), (0, HP - HD_B)))
    g_kn_p = jnp.pad(g_kn, ((0, 0), (0, HP - HD_B)))
    pf = p.reshape(t, PLE_DIM)
    tgt = loss_target.reshape(t, D_MODEL)

    zp, full2 = in_proj(h, w_in_t, tm, comm=gather_first_comm(pack2))
    qa, ka, va, qb, kb, vb, cq, ckv, full2 = attn_prep(zp, tabs, g_qa, g_kva, g_qn_p, g_kn_p, w_qb_t, w_kvb_t, tm, s_len,
                                                       comm=gather_pass_comm(full2))
    oa, lse_a, full3 = attn_fwd(qa, ka, va, n_b, s_len, tq_f, "attn_a_fwd", comm=gather_first_comm(pack3))
    ob, lse_b, full3 = attn_fwd(qb, kb, vb, n_b, s_len, tq_f, "attn_b_fwd", comm=gather_pass_comm(full3))
    w_oa_t = _pad_heads(gathered(full2, off_w2, "w_oa", 64, H_A * V_DIM_A), H_A, 1)
    w_ob_t = _pad_heads(gathered(full2, off_w2, "w_ob", 64, H_B * HD_B), H_B, 1)
    w_ple_t = gathered(full2, off_w2, "w_ple", 32, PLE_DIM)
    x1, merged, yab = merge_fwd(oa, ob, zp, xf, w_oa_t, w_ob_t, full2, off_w2, tm)
    x2, u = mlp_fwd(x1, g_mlp, full3, off_w3, tm)
    dx2, dt, h3, dpe, st_ple, dx2b = ple_loss_bwd(x2, pf, tgt, g_ple, g_final.reshape(1, D_MODEL), full2, off_w2, w_ple_t, tm)
    dx1, da, h2, st_mlp, dx1b = mlp_bwd(dx2, x1, u, g_mlp, full3, off_w3, tm)

    core = lax.axis_index("c").astype(jnp.int32).reshape(1)
    chip = (2 * lax.axis_index("x") + lax.axis_index("y")).astype(jnp.int32).reshape(1)

    def packed(gblocks, names):
        return _pack_rows({n: gblocks[n].reshape(N_DEV, -1, D_MODEL).astype(BF16) for n in names}, names)

    off_g1, rows_g1 = _pack_offsets(PACK_G1)
    gpack1 = matmul_tn_packed(da, h2, "gw_up", 512, off_g1["w_up"], rows_g1)
    gpack1 = matmul_tn_packed(u, dx2b, "gw_down", 512, off_g1["w_down"], rows_g1, buf=gpack1, square_a=True)
    gpack1 = matmul_tn_packed(h3, dt, "gw_pg", 128, off_g1["w_ple_gate"], rows_g1, buf=gpack1)
    gple = matmul_tn(dpe, pf, "gw_ple").reshape(N_DEV, -1, D_MODEL).astype(BF16)
    gpack1 = lax.dynamic_update_slice(gpack1, gple, (0, off_g1["w_ple"], 0))
    doa, dob, dz, dya, dyb, got1 = merge_bwd(dx1b, yab, zp, w_oa_t, w_ob_t, full2, off_w2, tm,
                                               comm=scatter_sibling_comm(gpack1))
    part1 = add_pairs(gpack1, got1, core)
    dqa, dka, dva, land1 = attn_bwd(qa, ka, va, doa, oa, lse_a, n_b, s_len, tq_b, "attn_a_bwd", comm=scatter_chips_comm(part1))
    gshard1 = sum_chips(part1, land1, chip)

    off_g2, rows_g2 = _pack_offsets(PACK_G2)
    g2 = dict(w_oa=_unpad_heads(matmul_tn(dya, oa, "gw_oa"), H_A, V_DIM_A, 1),
              w_ob=_unpad_heads(matmul_tn(dyb, ob, "gw_ob"), H_B, HD_B, 1))
    gpack2 = matmul_tn_packed(merged, dx1b, "gw_o", 128, off_g2["w_o"], rows_g2)
    gpack2 = lax.dynamic_update_slice(gpack2, packed(g2, ("w_oa", "w_ob")), (0, off_g2["w_oa"], 0))
    part2 = add_pairs(gpack2, exchange_sibling(gpack2), core)
    dqb, dkb, dvb, land2 = attn_bwd(qb, kb, vb, dob, ob, lse_b, n_b, s_len, tq_b, "attn_b_bwd", comm=scatter_chips_comm(part2))
    gshard2 = sum_chips(part2, land2, chip)
    dz, dqap, dkva, st_prep = prep_bwd(dqa, dka, dva, dqb, dkb, dvb, zp, dz, tabs, g_qa, g_kva, g_qn_p, g_kn_p,
                                       w_qb_t, w_kvb_t, tm, s_len)

    gkv = matmul_tn(dkva, ckv, "gw_kvb")
    g3 = dict(
        w_in=_collapse_w_in(matmul_tn(dz, h, "gw_in")),
        w_qb=_unpad_heads(matmul_tn(dqap, cq, "gw_qb"), H_A, QK_NOPE + QK_ROPE, 0),
        w_kvb=jnp.stack([_unpad_heads(gkv[:H_A * HP], H_A, 64, 0).reshape(H_A, 64, KV_LORA),
                         _unpad_heads(gkv[H_A * HP:], H_A, 64, 0).reshape(H_A, 64, KV_LORA)], axis=1))
    gpack3 = packed(g3, PACK_G3)
    part3 = add_pairs(gpack3, exchange_sibling(gpack3), core)
    grad_x, st_mix, land3 = in_bwd(dz, xf, dx1, g_mix, w_in_t, tm, comm=scatter_chips_comm(part3))
    gshard3 = sum_chips(part3, land3, chip)
    off_g3, _ = _pack_offsets(PACK_G3)
    shards = {n: (gshard1, off_g1[n]) for n in PACK_G1}
    shards.update({n: (gshard2, off_g2[n]) for n in PACK_G2})
    shards.update({n: (gshard3, off_g3[n]) for n in PACK_G3})

    stats = allreduce_stats(st_mix, st_prep, st_mlp, st_ple)
    loss = jnp.sum(stats[ST_LOSS])

    out_g, out_d, out_m, out_v = {}, {}, {}, {}
    for name, (w, m, v) in mats.items():
        gshard, off = shards[name]
        r, c = w.shape[1:]
        if name in col_sharded:
            g2 = gshard[off:off + (r * c) // D_MODEL].reshape(c, r)
            if r % 128 == 0 and c % 128 == 0:
                res = adamw(w, g2, m, v, "adamw_" + name, g_transposed=True)
            else:
                res = adamw(w[0].T[None], g2, m[0].T[None], v[0].T[None], "adamw_" + name)
                res = tuple(a[0].T[None] for a in res)
        else:
            res = adamw(w, gshard[off:off + r], m, v, "adamw_" + name)
        out_g[name], out_d[name], out_m[name], out_v[name] = res

    gains = (("g_mix", g_mix, m_g_mix, v_g_mix, ST_G_MIX), ("g_qa", g_qa, m_g_qa, v_g_qa, ST_G_QA),
             ("g_kva", g_kva, m_g_kva, v_g_kva, ST_G_KVA), ("g_qn", g_qn, m_g_qn, v_g_qn, ST_G_QN),
             ("g_kn", g_kn, m_g_kn, v_g_kn, ST_G_KN), ("g_mlp", g_mlp, m_g_mlp, v_g_mlp, ST_G_MLP),
             ("g_ple", g_ple, m_g_ple, v_g_ple, ST_G_PLE), ("g_final", g_final, m_g_final, v_g_final, ST_G_FINAL))
    res = adamw_gains(stats, [(r_, w.reshape(1, -1), m.reshape(1, -1), v.reshape(1, -1)) for _, w, m, v, r_ in gains])
    for (name, w, _, _, _), (gg, gd, gm, gv) in zip(gains, res):
        out_g[name], out_d[name], out_m[name], out_v[name] = (a.reshape(w.shape) for a in (gg, gd, gm, gv))

    order = ("g_mix", "w_in", "g_qa", "w_qb", "g_kva", "w_kvb", "g_qn", "g_kn", "w_oa", "w_ob", "w_o", "g_mlp",
             "w_up", "w_down", "g_ple", "w_ple_gate", "w_ple", "g_final")
    return (loss, grad_x.reshape(x.shape), *[out_g[n] for n in order], *[out_d[n] for n in order],
            *[out_m[n] for n in order], *[out_v[n] for n in order])
```

```python
import numpy as np
import jax
import jax.numpy as jnp
from jax import lax
from jax.experimental import pallas as pl
from jax.experimental.pallas import tpu as pltpu

F32 = jnp.float32
BF16 = jnp.bfloat16

D_MODEL = 1024
EPS = 1e-6
ROPE_THETA = 10000.0
GRID_W = 64
H_A = 8
QK_NOPE = 64
QK_ROPE = 32
V_DIM_A = 64
Q_LORA = 256
KV_LORA = 128
H_B = 8
KV_B = 2
HD_B = 64
D_FF = 4 * D_MODEL
PLE_DIM = 256
HP = 128
ZP = 4096
N_DEV = 8
N_CHIP = 4

ADAM_LR = 0.001
ADAM_B1 = 0.9
ADAM_B2 = 0.999
ADAM_EPS = 1e-08
ADAM_WD = 0.01
ADAM_STEP = 10

VMEM_LIMIT = 52 * 1024 * 1024

PACK_ROWS = dict(w_in=416, w_qb=32, w_kvb=16, w_oa=64, w_ob=64, w_o=128, w_up=512, w_down=512, w_ple_gate=128, w_ple=32)
PACK_W1 = ("w_in", "w_qb", "w_kvb")
PACK_W2 = ("w_o", "w_ple_gate", "w_oa", "w_ob", "w_ple")
PACK_W3 = ("w_up", "w_down")
PACK_G1 = ("w_up", "w_down", "w_ple_gate", "w_ple")
PACK_G2 = ("w_o", "w_oa", "w_ob")
PACK_G3 = ("w_in", "w_qb", "w_kvb")


def _pack_offsets(names):
    off, o = {}, 0
    for n in names:
        off[n] = o
        o += PACK_ROWS[n]
    return off, o

ST_G_MIX, ST_G_QA, ST_G_KVA, ST_G_QN, ST_G_KN, ST_G_MLP, ST_G_PLE, ST_G_FINAL, ST_LOSS = range(9)
ST_ROWS = 16


def _dot_nn(a, b):
    return lax.dot_general(a, b, (((1,), (0,)), ((), ())), preferred_element_type=F32)


def _dot_nt(a, b):
    return lax.dot_general(a, b, (((1,), (1,)), ((), ())), preferred_element_type=F32)


def _dot_tn(a, b):
    return lax.dot_general(a, b, (((0,), (0,)), ((), ())), preferred_element_type=F32)


def _rstd(x, n):
    return lax.rsqrt(jnp.sum(x * x, axis=-1, keepdims=True) * (1.0 / n) + EPS)


def _rms_bwd(dy, xh, r, g, n):
    dxh = dy * g
    return r * (dxh - xh * (jnp.sum(dxh * xh, axis=-1, keepdims=True) * (1.0 / n)))


def _rope_fwd(x, c, s1, s2):
    return x * c + pltpu.roll(x, HP - 16, 1) * s1 + pltpu.roll(x, 16, 1) * s2


def _rope_bwd(d, c, s1, s2):
    return d * c + pltpu.roll(d * s1, 16, 1) + pltpu.roll(d * s2, HP - 16, 1)


def _colsum(v):
    return jnp.sum(v, axis=0, keepdims=True)


def _params(sem=None, vmem=VMEM_LIMIT):
    return pltpu.CompilerParams(dimension_semantics=sem, vmem_limit_bytes=vmem)


def _resident(shape):
    nd = len(shape)
    return pl.BlockSpec(shape, lambda *_: (0,) * nd, pipeline_mode=pl.Buffered(1))


def _rows(tm, width, col=0):
    return pl.BlockSpec((tm, width), lambda i: (i, col))


def _packed_weight(rows, off):
    return pl.BlockSpec((N_DEV, rows, D_MODEL), lambda *_: (0, off // rows, 0), pipeline_mode=pl.Buffered(1))


def _wrows(ref, start, size):
    rows = ref.shape[1]
    return ref[start // rows:(start + size) // rows].reshape(size, D_MODEL)


def _mesh_pos():
    return lax.axis_index("x"), lax.axis_index("y"), lax.axis_index("c")


def _flip(v, bit):
    return (1 - v) if bit else v


_ANY = pl.BlockSpec(memory_space=pl.ANY)
_MESH = pl.DeviceIdType.MESH


def _remote(src, dst, send_sems, recv_sems, k, to):
    return pltpu.make_async_remote_copy(src_ref=src, dst_ref=dst, send_sem=send_sems.at[k], recv_sem=recv_sems.at[k],
                                        device_id=to, device_id_type=_MESH)


def _sibling_copies(g_ref, got_ref, send_sems, recv_sems):
    x, y, c = _mesh_pos()
    return [_remote(g_ref.at[2 * j + (1 - c)], got_ref.at[j], send_sems, recv_sems, j, (x, y, 1 - c)) for j in range(N_CHIP)]


def _chip_copies(p_ref, land_ref, send_sems, recv_sems):
    x, y, c = _mesh_pos()
    copies = []
    for k in (1, 2, 3):
        tx, ty = _flip(x, k & 2), _flip(y, k & 1)
        copies.append(_remote(p_ref.at[2 * tx + ty], land_ref.at[k - 1], send_sems, recv_sems, k - 1, (tx, ty, c)))
    return copies


class _Comm:
    def __init__(self, ins, out_shapes, sems, make, aliases=None):
        self.ins, self.out_shapes, self.sems, self.make, self.aliases = list(ins), list(out_shapes), list(sems), make, aliases or {}


def _join_comms(a, b):
    n_i, n_o, n_s = len(a.ins), len(a.out_shapes), len(a.sems)

    def make(cin, cout, sems):
        return a.make(cin[:n_i], cout[:n_o], sems[:n_s]) + b.make(cin[n_i:], cout[n_o:], sems[n_s:])

    aliases = dict(a.aliases)
    aliases.update({n_i + j: n_o + k for j, k in b.aliases.items()})
    return _Comm(a.ins + b.ins, a.out_shapes + b.out_shapes, a.sems + b.sems, make, aliases)


def _comm_parts(comm, n_in, n_out):
    if comm is None:
        return [], [], [], [], {}
    alias = {n_in + j: n_out + k for j, k in comm.aliases.items()}
    return comm.ins, [_ANY] * len(comm.ins), comm.out_shapes, comm.sems, alias


def _split_refs(refs, n_in, n_out, n_scratch, comm):
    n_ci = len(comm.ins) if comm else 0
    n_co = len(comm.out_shapes) if comm else 0
    cuts, i = [], 0
    for n in (n_in, n_ci, n_out, n_co, n_scratch):
        cuts.append(refs[i:i + n])
        i += n
    return (*cuts, refs[i:])


def _grid_edge(grid, last):
    cond = None
    for d, n in enumerate(grid):
        here = pl.program_id(d) == (n - 1 if last else 0)
        cond = here if cond is None else cond & here
    return cond


def _comm_start(comm, cin, cout, csem, grid):
    if comm is not None:
        @pl.when(_grid_edge(grid, False))
        def _():
            for cp in comm.make(cin, cout, csem):
                cp.start()


def _comm_finish(comm, cin, cout, csem, grid):
    if comm is not None:
        @pl.when(_grid_edge(grid, True))
        def _():
            for cp in comm.make(cin, cout, csem):
                cp.wait()


def gather_first_comm(shard):
    r, w = shard.shape

    def make(cin, cout, sems):
        (x_ref,), (out_ref,), (send_sems, recv_sems, local_sem) = cin, cout, sems
        x, y, c = _mesh_pos()
        mine = out_ref.at[4 * x + 2 * y + c]
        targets = [(x, y, 1 - c), (1 - x, y, c), (x, 1 - y, c), (1 - x, 1 - y, c)]
        return [_remote(x_ref, mine, send_sems, recv_sems, k, to) for k, to in enumerate(targets)] + [
            pltpu.make_async_copy(x_ref, mine, local_sem)]

    return _Comm([shard], [jax.ShapeDtypeStruct((N_DEV, r, w), shard.dtype)],
                 [pltpu.SemaphoreType.DMA((4,)), pltpu.SemaphoreType.DMA((4,)), pltpu.SemaphoreType.DMA], make)


def gather_pass_comm(full):
    def make(cin, cout, sems):
        (in_ref,), (out_ref,), (send_sems, recv_sems) = cin, cout, sems
        x, y, c = _mesh_pos()
        copies = []
        for k, (px, py) in enumerate([(1 - x, y), (x, 1 - y), (1 - x, 1 - y)]):
            idx = 4 * px + 2 * py + c
            copies.append(_remote(in_ref.at[idx], out_ref.at[idx], send_sems, recv_sems, k, (x, y, 1 - c)))
        return copies

    return _Comm([full], [jax.ShapeDtypeStruct(full.shape, full.dtype)],
                 [pltpu.SemaphoreType.DMA((3,)), pltpu.SemaphoreType.DMA((3,))], make, aliases={0: 0})


def scatter_sibling_comm(g):
    _, r, w = g.shape
    return _Comm([g], [jax.ShapeDtypeStruct((N_CHIP, r, w), g.dtype)],
                 [pltpu.SemaphoreType.DMA((N_CHIP,)), pltpu.SemaphoreType.DMA((N_CHIP,))],
                 lambda cin, cout, sems: _sibling_copies(cin[0], cout[0], sems[0], sems[1]))


def scatter_chips_comm(part):
    _, r, w = part.shape
    return _Comm([part], [jax.ShapeDtypeStruct((N_CHIP - 1, r, w), part.dtype)],
                 [pltpu.SemaphoreType.DMA((3,)), pltpu.SemaphoreType.DMA((3,))],
                 lambda cin, cout, sems: _chip_copies(cin[0], cout[0], sems[0], sems[1]))


def exchange_sibling(g):
    _, r, w = g.shape

    def body(g_ref, got_ref, send_sems, recv_sems):
        copies = _sibling_copies(g_ref, got_ref, send_sems, recv_sems)
        for cp in copies:
            cp.start()
        for cp in copies:
            cp.wait()

    return pl.pallas_call(
        body, name="exchange_sibling", out_shape=jax.ShapeDtypeStruct((N_CHIP, r, w), g.dtype),
        in_specs=[_ANY], out_specs=_ANY,
        scratch_shapes=[pltpu.SemaphoreType.DMA((N_CHIP,)), pltpu.SemaphoreType.DMA((N_CHIP,))],
    )(g)


def exchange_chips(part):
    _, r, w = part.shape

    def body(p_ref, land_ref, send_sems, recv_sems):
        copies = _chip_copies(p_ref, land_ref, send_sems, recv_sems)
        for cp in copies:
            cp.start()
        for cp in copies:
            cp.wait()

    return pl.pallas_call(
        body, name="exchange_chips", out_shape=jax.ShapeDtypeStruct((N_CHIP - 1, r, w), part.dtype),
        in_specs=[_ANY], out_specs=_ANY,
        scratch_shapes=[pltpu.SemaphoreType.DMA((3,)), pltpu.SemaphoreType.DMA((3,))],
    )(part)


def allreduce_stats(st_mix, st_prep, st_mlp, st_ple):
    def body(mix_ref, prep_ref, mlp_ref, ple_ref, out_ref, mine, gath, send_sems, recv_sems):
        x, y, c = _mesh_pos()
        me = 4 * x + 2 * y + c
        mine[...] = jnp.zeros_like(mine)
        mine[ST_G_MIX:ST_G_MIX + 1, :] = mix_ref[...]
        mine[ST_G_QA:ST_G_KN + 1, 0:256] = prep_ref[...]
        mine[ST_G_MLP:ST_G_MLP + 1, :] = mlp_ref[...]
        mine[ST_G_PLE:ST_LOSS + 1, :] = ple_ref[...]
        gath[me] = mine[...]
        copies = []
        for k in range(1, N_DEV):
            peer = (_flip(x, k & 4), _flip(y, k & 2), _flip(c, k & 1))
            copies.append(_remote(mine, gath.at[me], send_sems, recv_sems, k - 1, peer))
        for cp in copies:
            cp.start()
        for cp in copies:
            cp.wait()
        acc = gath[0]
        for d in range(1, N_DEV):
            acc = acc + gath[d]
        out_ref[...] = acc

    vm = pl.BlockSpec(memory_space=pltpu.VMEM)
    return pl.pallas_call(
        body, name="allreduce_stats", out_shape=jax.ShapeDtypeStruct((ST_ROWS, D_MODEL), F32),
        in_specs=[vm] * 4, out_specs=vm,
        scratch_shapes=[pltpu.VMEM((ST_ROWS, D_MODEL), F32), pltpu.VMEM((N_DEV, ST_ROWS, D_MODEL), F32),
                        pltpu.SemaphoreType.DMA((N_DEV - 1,)), pltpu.SemaphoreType.DMA((N_DEV - 1,))],
    )(st_mix, st_prep, st_mlp, st_ple)


def adamw_gains(stats, gains):
    c1 = 1.0 - ADAM_B1 ** ADAM_STEP
    c2 = 1.0 - ADAM_B2 ** ADAM_STEP
    n = len(gains)

    def body(st_ref, *refs):
        ins, outs = refs[:3 * n], refs[3 * n:]
        for i, (row, w, _, _) in enumerate(gains):
            width = w.shape[1]
            gv = st_ref[row:row + 1, 0:width]
            mn = ADAM_B1 * ins[3 * i + 1][...] + (1.0 - ADAM_B1) * gv
            vn = ADAM_B2 * ins[3 * i + 2][...] + (1.0 - ADAM_B2) * (gv * gv)
            outs[4 * i][...] = gv
            outs[4 * i + 1][...] = -ADAM_LR * ((mn / c1) / (jnp.sqrt(vn / c2) + ADAM_EPS) + ADAM_WD * ins[3 * i][...])
            outs[4 * i + 2][...] = mn
            outs[4 * i + 3][...] = vn

    vm = pl.BlockSpec(memory_space=pltpu.VMEM)
    flat = [a for (_, w, m, v) in gains for a in (w, m, v)]
    out_shape = tuple(jax.ShapeDtypeStruct(w.shape, F32) for (_, w, _, _) in gains for _ in range(4))
    res = pl.pallas_call(body, name="adamw_gains", out_shape=out_shape, in_specs=[vm] * (1 + 3 * n),
                         out_specs=tuple([vm] * (4 * n)))(stats, *flat)
    return [res[4 * i:4 * i + 4] for i in range(n)]


def _row_tile(r, cap=640):
    return max(d for d in range(16, min(r, cap) + 1, 16) if r % d == 0)


def add_pairs(g, got, core):
    n, r, w = got.shape
    tr = _row_tile(r)

    def body(c_ref, a_ref, b_ref, o_ref):
        o_ref[...] = (a_ref[...].astype(F32) + b_ref[...].astype(F32)).astype(o_ref.dtype)

    spec = pl.BlockSpec((1, tr, w), lambda i, j, c: (i, j, 0))
    return pl.pallas_call(
        body, name="add_pairs", out_shape=jax.ShapeDtypeStruct(got.shape, got.dtype),
        grid_spec=pltpu.PrefetchScalarGridSpec(
            num_scalar_prefetch=1, grid=(n, r // tr),
            in_specs=[pl.BlockSpec((1, tr, w), lambda i, j, c: (2 * i + c[0], j, 0)), spec], out_specs=spec),
        compiler_params=_params(("parallel", "parallel")),
    )(core, g, got)


def sum_chips(part, land, chip):
    _, r, w = part.shape
    tr = _row_tile(r)

    def body(c_ref, p_ref, l_ref, o_ref):
        acc = p_ref[0].astype(F32)
        for s in range(N_CHIP - 1):
            acc = acc + l_ref[s].astype(F32)
        o_ref[...] = acc

    return pl.pallas_call(
        body, name="sum_chips", out_shape=jax.ShapeDtypeStruct((r, w), F32),
        grid_spec=pltpu.PrefetchScalarGridSpec(
            num_scalar_prefetch=1, grid=(r // tr,),
            in_specs=[pl.BlockSpec((1, tr, w), lambda i, c: (c[0], i, 0)), pl.BlockSpec((N_CHIP - 1, tr, w), lambda i, c: (0, i, 0))],
            out_specs=pl.BlockSpec((tr, w), lambda i, c: (i, 0))),
        compiler_params=_params(("parallel",)),
    )(chip, part, land)


def norm_x(x, g_mix, tm, comm=None):
    t = x.shape[0]
    grid = (t // tm,)
    c_ins, c_in_specs, c_outs, c_sems, alias = _comm_parts(comm, 2, 1)

    def body(*refs):
        (x_ref, g_ref), cin, (h_ref,), cout, _, csem = _split_refs(refs, 2, 1, 0, comm)
        _comm_start(comm, cin, cout, csem, grid)
        xv = x_ref[...]
        h_ref[...] = (xv * _rstd(xv, D_MODEL) * g_ref[...]).astype(BF16)
        _comm_finish(comm, cin, cout, csem, grid)

    return pl.pallas_call(
        body, name="norm_x", grid=grid, out_shape=(jax.ShapeDtypeStruct((t, D_MODEL), BF16), *c_outs),
        in_specs=[_rows(tm, D_MODEL), _resident((1, D_MODEL)), *c_in_specs],
        out_specs=(_rows(tm, D_MODEL), *([_ANY] * len(c_outs))),
        scratch_shapes=c_sems, input_output_aliases=alias, compiler_params=_params(("arbitrary",)),
    )(x, g_mix, *c_ins)


def pack_late_weights(w_up, w_down, w_o, w_pg, small, comm=None):
    rows2 = sum(PACK_ROWS[n] for n in PACK_W2)
    rows3 = sum(PACK_ROWS[n] for n in PACK_W3)
    c_ins, c_in_specs, c_outs, c_sems, alias = _comm_parts(comm, 5, 2)
    grid = (1,)

    def body(*refs):
        (up_ref, dn_ref, o_ref, pg_ref, sm_ref), cin, (p2_ref, p3_ref), cout, _, csem = _split_refs(refs, 5, 2, 0, comm)
        _comm_start(comm, cin, cout, csem, grid)
        p2_ref[0:128, :] = o_ref[0].astype(BF16)
        p2_ref[128:256, :] = pg_ref[0].astype(BF16)
        p2_ref[256:rows2, :] = sm_ref[...]
        p3_ref[0:512, :] = up_ref[0].T.astype(BF16)
        p3_ref[512:1024, :] = dn_ref[0].astype(BF16)
        _comm_finish(comm, cin, cout, csem, grid)

    def whole(a):
        nd = a.ndim
        return pl.BlockSpec(a.shape, lambda i: (0,) * nd)

    args = (w_up, w_down, w_o, w_pg, small)
    return pl.pallas_call(
        body, name="pack_late_weights", grid=grid,
        out_shape=(jax.ShapeDtypeStruct((rows2, D_MODEL), BF16), jax.ShapeDtypeStruct((rows3, D_MODEL), BF16), *c_outs),
        in_specs=[*[whole(a) for a in args], *c_in_specs],
        out_specs=(pl.BlockSpec((rows2, D_MODEL), lambda i: (0, 0)), pl.BlockSpec((rows3, D_MODEL), lambda i: (0, 0)),
                   *([_ANY] * len(c_outs))),
        scratch_shapes=c_sems, input_output_aliases=alias, compiler_params=_params(("arbitrary",)),
    )(*args, *c_ins)


def in_proj(h, w_in_t, tm, comm=None):
    t = h.shape[0]
    nc = 512
    half = ZP // 2
    grid = (t // tm,)
    c_ins, c_in_specs, c_outs, c_sems, alias = _comm_parts(comm, 2, 2)

    def body(*refs):
        (h_ref, w_ref), cin, (zg_ref, zs_ref), cout, _, csem = _split_refs(refs, 2, 2, 0, comm)
        _comm_start(comm, cin, cout, csem, grid)
        hv = h_ref[...]
        for cidx in range(half // nc):
            zg_ref[:, cidx * nc:(cidx + 1) * nc] = _dot_nt(hv, w_ref[cidx * nc:(cidx + 1) * nc, :]).astype(BF16)
        for cidx in range(half // nc):
            zs_ref[:, cidx * nc:(cidx + 1) * nc] = _dot_nt(hv, w_ref[half + cidx * nc:half + (cidx + 1) * nc, :])
        _comm_finish(comm, cin, cout, csem, grid)

    return pl.pallas_call(
        body, name="in_proj", grid=grid,
        out_shape=(jax.ShapeDtypeStruct((t, half), BF16), jax.ShapeDtypeStruct((t, half), F32), *c_outs),
        in_specs=[_rows(tm, D_MODEL), _resident((ZP, D_MODEL)), *c_in_specs],
        out_specs=(_rows(tm, half), _rows(tm, half), *([_ANY] * len(c_outs))),
        scratch_shapes=c_sems, input_output_aliases=alias, compiler_params=_params(("arbitrary",)),
    )(h, w_in_t, *c_ins)


def attn_prep(zp, tabs, g_qa, g_kva, g_qn, g_kn, w_qb_t, w_kvb_t, tm, s_len, comm=None):
    t = zp.shape[0]
    nsb = s_len // tm
    scale_a = (QK_NOPE + QK_ROPE) ** -0.5
    scale_b = HD_B ** -0.5

    grid = (t // tm,)
    c_ins, c_in_specs, c_outs, c_sems, alias = _comm_parts(comm, 13, 8)

    def body(*refs):
        ((qb_ref, qlat_ref, kb_ref, vb_ref, ckv_ref, kpe_ref, tab_ref, gqa_ref, gkva_ref, gqn_ref, gkn_ref, wqb_ref,
          wkvb_ref), cin, (qa_o, ka_o, va_o, qb_o, kb_o, vb_o, cq_o, ckvn_o), cout, _, csem) = _split_refs(refs, 13, 8, 0, comm)
        _comm_start(comm, cin, cout, csem, grid)
        ca, s1a, s2a = tab_ref[0], tab_ref[1], tab_ref[2]
        ck = tab_ref[3]
        cb, s1b, s2b = tab_ref[4], tab_ref[5], tab_ref[6]
        ql = qlat_ref[...]
        cq = (ql * _rstd(ql, Q_LORA) * gqa_ref[...]).astype(BF16)
        cq_o[...] = cq
        qa = _dot_nt(cq, wqb_ref[...])
        slabs = [slice(h * HP, (h + 1) * HP) for h in range(H_A)]
        qa_o[...] = jnp.concatenate(
            [(_rope_fwd(qa[:, sl], ca, s1a, s2a) * scale_a).astype(BF16) for sl in slabs], axis=1)
        cr = ckv_ref[...]
        ckv = (cr * _rstd(cr, KV_LORA) * gkva_ref[...]).astype(BF16)
        ckvn_o[...] = ckv
        kva = _dot_nt(ckv, wkvb_ref[...])
        kpe = _rope_fwd(kpe_ref[...], ck, s1a, s2a)
        ka_o[...] = jnp.concatenate([(kva[:, sl] + kpe).astype(BF16) for sl in slabs], axis=1)
        va_o[...] = kva[:, H_A * HP:].astype(BF16)
        gqn, gkn = gqn_ref[...], gkn_ref[...]

        def norm_rope(ref, sl, g, scale):
            xs = ref[:, sl]
            y = _rope_fwd(xs * _rstd(xs, HD_B) * g, cb, s1b, s2b)
            return (y if scale is None else y * scale).astype(BF16)

        qb_o[...] = jnp.concatenate([norm_rope(qb_ref, sl, gqn, scale_b) for sl in slabs], axis=1)
        kb_o[...] = jnp.concatenate([norm_rope(kb_ref, sl, gkn, None) for sl in slabs[:KV_B]], axis=1)
        vb_o[...] = vb_ref[...].astype(BF16)
        _comm_finish(comm, cin, cout, csem, grid)

    def o(width):
        return jax.ShapeDtypeStruct((t, width), BF16)

    return pl.pallas_call(
        body, name="attn_prep", grid=grid,
        out_shape=(o(H_A * HP), o(H_A * HP), o(H_A * HP), o(H_B * HP), o(KV_B * HP), o(KV_B * HP), o(Q_LORA), o(KV_LORA),
                   *c_outs),
        in_specs=[_rows(tm, 1024, 0), _rows(tm, 256, 4), _rows(tm, 256, 5), _rows(tm, 256, 6),
                  _rows(tm, 128, 14), _rows(tm, 128, 15),
                  pl.BlockSpec((7, tm, HP), lambda i: (0, i % nsb, 0)),
                  _resident((1, Q_LORA)), _resident((1, KV_LORA)), _resident((1, HP)), _resident((1, HP)),
                  _resident((H_A * HP, Q_LORA)), _resident((2 * H_A * HP, KV_LORA)), *c_in_specs],
        out_specs=(_rows(tm, H_A * HP), _rows(tm, H_A * HP), _rows(tm, H_A * HP), _rows(tm, H_B * HP),
                   _rows(tm, KV_B * HP), _rows(tm, KV_B * HP), _rows(tm, Q_LORA), _rows(tm, KV_LORA), *([_ANY] * len(c_outs))),
        scratch_shapes=c_sems, input_output_aliases=alias, compiler_params=_params(("arbitrary",)),
    )(zp, zp, zp, zp, zp, zp, tabs, g_qa, g_kva, g_qn, g_kn, w_qb_t, w_kvb_t, *c_ins)


def attn_fwd(q, k, v, n_b, s_len, tq, name, comm=None):
    t = q.shape[0]
    n_h, n_hk = q.shape[1] // HP, k.shape[1] // HP
    grp = n_h // n_hk
    nq = s_len // tq
    sub = min(tq, 256)
    grid = (n_b, n_h, nq)
    c_ins, c_in_specs, c_outs, c_sems, alias = _comm_parts(comm, 3, 2)

    def body(*refs):
        (q_ref, k_ref, v_ref), cin, (o_ref, lse_ref), cout, _, csem = _split_refs(refs, 3, 2, 0, comm)
        _comm_start(comm, cin, cout, csem, grid)
        kv, vv = k_ref[...], v_ref[...]
        for r in range(tq // sub):
            rows = slice(r * sub, (r + 1) * sub)
            s = _dot_nt(q_ref[rows, :], kv)
            m = jnp.max(s, axis=-1, keepdims=True)
            p = jnp.exp(s - m)
            l = jnp.sum(p, axis=-1, keepdims=True)
            o_ref[rows, :] = (_dot_nn(p.astype(BF16), vv) * (1.0 / l)).astype(o_ref.dtype)
            lse_ref[rows, :] = jnp.broadcast_to(m + jnp.log(l), (sub, HP))
        _comm_finish(comm, cin, cout, csem, grid)

    qspec = pl.BlockSpec((tq, HP), lambda b, h, i: (b * nq + i, h))
    kspec = pl.BlockSpec((s_len, HP), lambda b, h, i: (b, h // grp))
    return pl.pallas_call(
        body, name=name, grid=grid,
        out_shape=(jax.ShapeDtypeStruct((t, n_h * HP), BF16), jax.ShapeDtypeStruct((t, n_h * HP), F32), *c_outs),
        in_specs=[qspec, kspec, kspec, *c_in_specs], out_specs=(qspec, qspec, *([_ANY] * len(c_outs))),
        scratch_shapes=c_sems, input_output_aliases=alias,
        compiler_params=_params(("arbitrary", "arbitrary", "arbitrary")),
    )(q, k, v, *c_ins)


def merge_fwd(oa, ob, zp, x, w_oa_t, w_ob_t, wpack, off, tm):
    t = x.shape[0]

    def body(oa_ref, ob_ref, ga_ref, gb_ref, x_ref, woa_ref, wob_ref, wo_ref, x1_o, mg_o, y_o):
        ya = _dot_nt(oa_ref[...], woa_ref[...])
        yb = _dot_nt(ob_ref[...], wob_ref[...])
        y_o[:, 0:D_MODEL] = ya.astype(BF16)
        y_o[:, D_MODEL:2 * D_MODEL] = yb.astype(BF16)
        merged = (jax.nn.sigmoid(ga_ref[...].astype(F32)) * ya + jax.nn.sigmoid(gb_ref[...].astype(F32)) * yb).astype(BF16)
        mg_o[...] = merged
        x1_o[...] = x_ref[...] + _dot_nn(merged, _wrows(wo_ref, 0, D_MODEL))

    return pl.pallas_call(
        body, name="merge_fwd", grid=(t // tm,),
        out_shape=(jax.ShapeDtypeStruct((t, D_MODEL), F32), jax.ShapeDtypeStruct((t, D_MODEL), BF16),
                   jax.ShapeDtypeStruct((t, 2 * D_MODEL), BF16)),
        in_specs=[_rows(tm, H_A * HP), _rows(tm, H_B * HP), _rows(tm, 1024, 0), _rows(tm, 1024, 1), _rows(tm, D_MODEL),
                  _resident((D_MODEL, H_A * HP)), _resident((D_MODEL, H_B * HP)), _packed_weight(128, off["w_o"])],
        out_specs=(_rows(tm, D_MODEL), _rows(tm, D_MODEL), _rows(tm, 2 * D_MODEL)), compiler_params=_params(("parallel",)),
    )(oa, ob, zp, zp, x, w_oa_t, w_ob_t, wpack)


def mlp_fwd(x1, g_mlp, wpack, off, tm):
    t = x1.shape[0]
    fc = 1024

    def body(x_ref, g_ref, wup_ref, wdn_ref, x2_o, u_o):
        xv = x_ref[...]
        h2 = (xv * _rstd(xv, D_MODEL) * g_ref[...]).astype(BF16)
        acc = xv
        for cidx in range(D_FF // fc):
            sl = slice(cidx * fc, (cidx + 1) * fc)
            u = jnp.maximum(_dot_nt(h2, _wrows(wup_ref, cidx * fc, fc)), 0.0)
            u_o[:, sl] = u.astype(BF16)
            acc = acc + _dot_nn((u * u).astype(BF16), _wrows(wdn_ref, cidx * fc, fc))
        x2_o[...] = acc

    return pl.pallas_call(
        body, name="mlp_fwd", grid=(t // tm,),
        out_shape=(jax.ShapeDtypeStruct((t, D_MODEL), F32), jax.ShapeDtypeStruct((t, D_FF), BF16)),
        in_specs=[_rows(tm, D_MODEL), _resident((1, D_MODEL)), _packed_weight(512, off["w_up"]), _packed_weight(512, off["w_down"])],
        out_specs=(_rows(tm, D_MODEL), _rows(tm, D_FF)), compiler_params=_params(("parallel",)),
    )(x1, g_mlp, wpack, wpack)


def ple_loss_bwd(x2, p, tgt, g_ple, g_final, wpack, off, w_ple_t, tm):
    t = x2.shape[0]
    inv_d = 1.0 / D_MODEL

    def body(x2_ref, p_ref, tg_ref, gp_ref, gf_ref, wpg_ref, wple_ref, dx2_o, dt_o, h3_o, dpe_o, st_o, dx2b_o):
        @pl.when(pl.program_id(0) == 0)
        def _():
            st_o[...] = jnp.zeros_like(st_o)

        x2v = x2_ref[...]
        gp, gf = gp_ref[...], gf_ref[...]
        w_pg = _wrows(wpg_ref, 0, D_MODEL)
        r2 = _rstd(x2v, D_MODEL)
        xh2 = x2v * r2
        h3 = (xh2 * gp).astype(BF16)
        h3_o[...] = h3
        gate = jax.nn.sigmoid(_dot_nn(h3, w_pg))
        pe = _dot_nt(p_ref[...].astype(BF16), wple_ref[...])
        x3 = x2v + gate * pe
        r3 = _rstd(x3, D_MODEL)
        xh3 = x3 * r3
        err = xh3 * gf - tg_ref[...]
        dy = err * inv_d
        dx3 = _rms_bwd(dy, xh3, r3, gf, D_MODEL)
        dpe_o[...] = (dx3 * gate).astype(BF16)
        dt = (dx3 * pe * gate * (1.0 - gate)).astype(BF16)
        dt_o[...] = dt
        dh3 = _dot_nt(dt, w_pg)
        dx2 = dx3 + _rms_bwd(dh3, xh2, r2, gp, D_MODEL)
        dx2_o[...] = dx2
        dx2b_o[...] = dx2.astype(BF16)
        st_o[0:1, :] += _colsum(dh3 * xh2)
        st_o[1:2, :] += _colsum(dy * xh3)
        st_o[2:3, :] += _colsum(err * err) * (0.5 * inv_d)

    bf = jax.ShapeDtypeStruct((t, D_MODEL), BF16)
    return pl.pallas_call(
        body, name="ple_loss_bwd", grid=(t // tm,),
        out_shape=(jax.ShapeDtypeStruct((t, D_MODEL), F32), bf, bf, bf, jax.ShapeDtypeStruct((3, D_MODEL), F32), bf),
        in_specs=[_rows(tm, D_MODEL), _rows(tm, PLE_DIM), _rows(tm, D_MODEL), _resident((1, D_MODEL)), _resident((1, D_MODEL)),
                  _packed_weight(128, off["w_ple_gate"]), _resident((D_MODEL, PLE_DIM))],
        out_specs=(_rows(tm, D_MODEL), _rows(tm, D_MODEL), _rows(tm, D_MODEL), _rows(tm, D_MODEL),
                   pl.BlockSpec((3, D_MODEL), lambda i: (0, 0)), _rows(tm, D_MODEL)),
        compiler_params=_params(("arbitrary",)),
    )(x2, p, tgt, g_ple, g_final, wpack, w_ple_t)


def mlp_bwd(dx2, x1, u, g_mlp, wpack, off, tm):
    t = x1.shape[0]
    fc = 1024

    def body(dx2_ref, x1_ref, u_ref, g_ref, wup_ref, wdn_ref, dx1_o, da_o, h2_o, st_o, dx1b_o):
        @pl.when(pl.program_id(0) == 0)
        def _():
            st_o[...] = jnp.zeros_like(st_o)

        d2 = dx2_ref[...]
        d2b = d2.astype(BF16)
        dh2 = jnp.zeros((tm, D_MODEL), F32)
        for cidx in range(D_FF // fc):
            sl = slice(cidx * fc, (cidx + 1) * fc)
            da = (_dot_nt(d2b, _wrows(wdn_ref, cidx * fc, fc)) * (2.0 * u_ref[:, sl].astype(F32))).astype(BF16)
            da_o[:, sl] = da
            dh2 = dh2 + _dot_nn(da, _wrows(wup_ref, cidx * fc, fc))
        xv = x1_ref[...]
        g = g_ref[...]
        r1 = _rstd(xv, D_MODEL)
        xh1 = xv * r1
        h2_o[...] = (xh1 * g).astype(BF16)
        st_o[...] += _colsum(dh2 * xh1)
        dx1 = d2 + _rms_bwd(dh2, xh1, r1, g, D_MODEL)
        dx1_o[...] = dx1
        dx1b_o[...] = dx1.astype(BF16)

    return pl.pallas_call(
        body, name="mlp_bwd", grid=(t // tm,),
        out_shape=(jax.ShapeDtypeStruct((t, D_MODEL), F32), jax.ShapeDtypeStruct((t, D_FF), BF16),
                   jax.ShapeDtypeStruct((t, D_MODEL), BF16), jax.ShapeDtypeStruct((1, D_MODEL), F32),
                   jax.ShapeDtypeStruct((t, D_MODEL), BF16)),
        in_specs=[_rows(tm, D_MODEL), _rows(tm, D_MODEL), _rows(tm, D_FF), _resident((1, D_MODEL)),
                  _packed_weight(512, off["w_up"]), _packed_weight(512, off["w_down"])],
        out_specs=(_rows(tm, D_MODEL), _rows(tm, D_FF), _rows(tm, D_MODEL), pl.BlockSpec((1, D_MODEL), lambda i: (0, 0)),
                   _rows(tm, D_MODEL)),
        compiler_params=_params(("arbitrary",)),
    )(dx2, x1, u, g_mlp, wpack, wpack)


def merge_bwd(dx1b, yab, zp, w_oa_t, w_ob_t, wpack, off, tm, comm=None):
    t = dx1b.shape[0]
    grid = (t // tm,)
    c_ins, c_in_specs, c_outs, c_sems, alias = _comm_parts(comm, 7, 5)

    def body(*refs):
        ((dx1_ref, y_ref, ga_ref, gb_ref, woa_ref, wob_ref, wo_ref), cin,
         (doa_o, dob_o, dg_o, dya_o, dyb_o), cout, _, csem) = _split_refs(refs, 7, 5, 0, comm)
        _comm_start(comm, cin, cout, csem, grid)
        dm = _dot_nt(dx1_ref[...], _wrows(wo_ref, 0, D_MODEL))
        for g_ref, w_ref, do_o, dy_o, col in ((ga_ref, woa_ref, doa_o, dya_o, 0), (gb_ref, wob_ref, dob_o, dyb_o, 1)):
            cols = slice(col * D_MODEL, (col + 1) * D_MODEL)
            sg = jax.nn.sigmoid(g_ref[...].astype(F32))
            dyv = (dm * sg).astype(BF16)
            dy_o[...] = dyv
            dg_o[:, cols] = (dm * y_ref[:, cols].astype(F32) * sg * (1.0 - sg)).astype(BF16)
            do_o[...] = _dot_nn(dyv, w_ref[...]).astype(BF16)
        _comm_finish(comm, cin, cout, csem, grid)

    bf = jax.ShapeDtypeStruct((t, D_MODEL), BF16)
    return pl.pallas_call(
        body, name="merge_bwd", grid=grid,
        out_shape=(bf, bf, jax.ShapeDtypeStruct((t, ZP), BF16), bf, bf, *c_outs),
        in_specs=[_rows(tm, D_MODEL), _rows(tm, 2 * D_MODEL), _rows(tm, 1024, 0), _rows(tm, 1024, 1),
                  _resident((D_MODEL, H_A * HP)), _resident((D_MODEL, H_B * HP)), _packed_weight(128, off["w_o"]), *c_in_specs],
        out_specs=(_rows(tm, D_MODEL), _rows(tm, D_MODEL), _rows(tm, 2 * D_MODEL), _rows(tm, D_MODEL), _rows(tm, D_MODEL),
                   *([_ANY] * len(c_outs))),
        scratch_shapes=c_sems, input_output_aliases=alias,
        compiler_params=_params(("arbitrary",)),
    )(dx1b, yab, zp, zp, w_oa_t, w_ob_t, wpack, *c_ins)


def attn_bwd(q, k, v, do, o, lse, n_b, s_len, tq, name, comm=None):
    t = q.shape[0]
    n_h, n_hk = q.shape[1] // HP, k.shape[1] // HP
    grp = n_h // n_hk
    nq = s_len // tq
    sub = min(tq, 256)
    grid = (n_b, n_hk, grp, nq)
    c_ins, c_in_specs, c_outs, c_sems, alias = _comm_parts(comm, 6, 3)

    def body(*refs):
        ((q_ref, k_ref, v_ref, do_ref, o_ref, lse_ref), cin, (dq_o, dk_o, dv_o), cout, (p_s, ds_s, dk_acc, dv_acc),
         csem) = _split_refs(refs, 6, 3, 4, comm)
        _comm_start(comm, cin, cout, csem, grid)

        @pl.when((pl.program_id(2) == 0) & (pl.program_id(3) == 0))
        def _():
            dk_acc[...] = jnp.zeros_like(dk_acc)
            dv_acc[...] = jnp.zeros_like(dv_acc)

        kv, vv = k_ref[...], v_ref[...]
        for r in range(tq // sub):
            rows = slice(r * sub, (r + 1) * sub)
            qv, dov = q_ref[rows, :], do_ref[rows, :]
            delta = jnp.sum(dov.astype(F32) * o_ref[rows, :].astype(F32), axis=-1, keepdims=True)
            delta_row = jnp.broadcast_to(delta, (sub, HP)).T[0:1, :]
            lse_row = lse_ref[rows, :].T[0:1, :]
            pt = jnp.exp(_dot_nt(kv, qv) - lse_row)
            dst = (pt * (_dot_nt(vv, dov) - delta_row)).astype(BF16)
            p_s[:, rows] = pt.astype(BF16)
            ds_s[:, rows] = dst
            dq_o[rows, :] = _dot_tn(dst, kv).astype(dq_o.dtype)
        dk_acc[...] += _dot_nn(ds_s[...], q_ref[...])
        dv_acc[...] += _dot_nn(p_s[...], do_ref[...])

        @pl.when((pl.program_id(2) == grp - 1) & (pl.program_id(3) == nq - 1))
        def _():
            dk_o[...] = dk_acc[...].astype(dk_o.dtype)
            dv_o[...] = dv_acc[...].astype(dv_o.dtype)

        _comm_finish(comm, cin, cout, csem, grid)

    qspec = pl.BlockSpec((tq, HP), lambda b, hk, g, i: (b * nq + i, hk * grp + g))
    kspec = pl.BlockSpec((s_len, HP), lambda b, hk, g, i: (b, hk))
    return pl.pallas_call(
        body, name=name, grid=grid,
        out_shape=(jax.ShapeDtypeStruct((t, n_h * HP), BF16), jax.ShapeDtypeStruct((t, n_hk * HP), BF16),
                   jax.ShapeDtypeStruct((t, n_hk * HP), BF16), *c_outs),
        in_specs=[qspec, kspec, kspec, qspec, qspec, qspec, *c_in_specs],
        out_specs=(qspec, kspec, kspec, *([_ANY] * len(c_outs))),
        scratch_shapes=[pltpu.VMEM((s_len, tq), BF16), pltpu.VMEM((s_len, tq), BF16),
                        pltpu.VMEM((s_len, HP), F32), pltpu.VMEM((s_len, HP), F32), *c_sems],
        input_output_aliases=alias,
        compiler_params=_params(("arbitrary", "arbitrary", "arbitrary", "arbitrary")),
    )(q, k, v, do, o, lse, *c_ins)


def prep_bwd(dqa, dka, dva, dqb, dkb, dvb, zp, dz, tabs, g_qa, g_kva, g_qn, g_kn, w_qb_t, w_kvb_t, tm, s_len):
    t = zp.shape[0]
    nsb = s_len // tm
    scale_a = (QK_NOPE + QK_ROPE) ** -0.5
    scale_b = HD_B ** -0.5

    def body(dqa_ref, dka_ref, dva_ref, dqb_ref, dkb_ref, dvb_ref, qb_ref, qlat_ref, kb_ref, ckv_ref, tab_ref,
             gqa_ref, gkva_ref, gqn_ref, gkn_ref, wqb_ref, wkvb_ref, _, dz_o, dqap_o, dkva_o, st_o):
        dzq_o, dsm_o = dz_o.at[:, 0:1024], dz_o.at[:, 1024:2048]

        @pl.when(pl.program_id(0) == 0)
        def _():
            st_o[...] = jnp.zeros_like(st_o)

        ca, s1a, s2a = tab_ref[0], tab_ref[1], tab_ref[2]
        ck = tab_ref[3]
        cb, s1b, s2b = tab_ref[4], tab_ref[5], tab_ref[6]
        for h in range(H_A):
            sl = slice(h * HP, (h + 1) * HP)
            dqap_o[:, sl] = _rope_bwd(dqa_ref[:, sl].astype(F32) * scale_a, ca, s1a, s2a).astype(BF16)
        dcq = _dot_nn(dqap_o[...], wqb_ref[...])
        ql = qlat_ref[...]
        rq = _rstd(ql, Q_LORA)
        xh = ql * rq
        gqa = gqa_ref[...]
        st_o[0:1, :] += _colsum(dcq * xh)
        dsm_o[:, 0:256] = _rms_bwd(dcq, xh, rq, gqa, Q_LORA).astype(BF16)
        dkpe = jnp.zeros((tm, HP), F32)
        for h in range(H_A):
            sl = slice(h * HP, (h + 1) * HP)
            dk = dka_ref[:, sl]
            dkpe = dkpe + dk.astype(F32)
            dkva_o[:, sl] = dk.astype(BF16)
        dkva_o[:, H_A * HP:] = dva_ref[...].astype(BF16)
        dsm_o[:, 896:1024] = _rope_bwd(dkpe, ck, s1a, s2a).astype(BF16)
        dckv = _dot_nn(dkva_o[...], wkvb_ref[...])
        cr = ckv_ref[...]
        rk = _rstd(cr, KV_LORA)
        xh = cr * rk
        st_o[1:2, 0:128] += _colsum(dckv * xh)
        dsm_o[:, 768:896] = _rms_bwd(dckv, xh, rk, gkva_ref[...], KV_LORA).astype(BF16)
        gqn, gkn = gqn_ref[...], gkn_ref[...]
        dgq = jnp.zeros((1, HP), F32)
        for h in range(H_B):
            sl = slice(h * HP, (h + 1) * HP)
            dy = _rope_bwd(dqb_ref[:, sl].astype(F32) * scale_b, cb, s1b, s2b)
            xs = qb_ref[:, sl]
            r = _rstd(xs, HD_B)
            xh = xs * r
            dgq = dgq + _colsum(dy * xh)
            dzq_o[:, sl] = _rms_bwd(dy, xh, r, gqn, HD_B).astype(BF16)
        st_o[2:3, 0:128] += dgq
        dgk = jnp.zeros((1, HP), F32)
        for h in range(KV_B):
            sl = slice(h * HP, (h + 1) * HP)
            dy = _rope_bwd(dkb_ref[:, sl].astype(F32), cb, s1b, s2b)
            xs = kb_ref[:, sl]
            r = _rstd(xs, HD_B)
            xh = xs * r
            dgk = dgk + _colsum(dy * xh)
            dsm_o[:, 256 + h * HP:256 + (h + 1) * HP] = _rms_bwd(dy, xh, r, gkn, HD_B).astype(BF16)
        st_o[3:4, 0:128] += dgk
        dsm_o[:, 512:768] = dvb_ref[...].astype(BF16)

    return pl.pallas_call(
        body, name="prep_bwd", grid=(t // tm,),
        out_shape=(jax.ShapeDtypeStruct((t, ZP), BF16), jax.ShapeDtypeStruct((t, 1024), BF16),
                   jax.ShapeDtypeStruct((t, 2048), BF16), jax.ShapeDtypeStruct((4, 256), F32)),
        in_specs=[_rows(tm, 1024), _rows(tm, 1024), _rows(tm, 1024), _rows(tm, 1024), _rows(tm, 256), _rows(tm, 256),
                  _rows(tm, 1024, 0), _rows(tm, 256, 4), _rows(tm, 256, 5), _rows(tm, 128, 14),
                  pl.BlockSpec((7, tm, HP), lambda i: (0, i % nsb, 0)),
                  _resident((1, Q_LORA)), _resident((1, KV_LORA)), _resident((1, HP)), _resident((1, HP)),
                  _resident((H_A * HP, Q_LORA)), _resident((2 * H_A * HP, KV_LORA)), _ANY],
        out_specs=(_rows(tm, 2048, 1), _rows(tm, 1024), _rows(tm, 2048), pl.BlockSpec((4, 256), lambda i: (0, 0))),
        input_output_aliases={17: 0}, compiler_params=_params(("arbitrary",)),
    )(dqa, dka, dva, dqb, dkb, dvb, zp, zp, zp, zp, tabs, g_qa, g_kva, g_qn, g_kn, w_qb_t, w_kvb_t, dz)


def in_bwd(dz, x, dx1, g_mix, w_in_t, tm, comm=None):
    t = x.shape[0]
    grid = (t // tm,)
    c_ins, c_in_specs, c_outs, c_sems, alias = _comm_parts(comm, 5, 2)

    def body(*refs):
        (dz_ref, x_ref, dx1_ref, g_ref, w_ref), cin, (dx_o, st_o), cout, _, csem = _split_refs(refs, 5, 2, 0, comm)
        _comm_start(comm, cin, cout, csem, grid)

        @pl.when(pl.program_id(0) == 0)
        def _():
            st_o[...] = jnp.zeros_like(st_o)

        dh = _dot_nn(dz_ref[...], w_ref[...])
        xv = x_ref[...]
        g = g_ref[...]
        r = _rstd(xv, D_MODEL)
        xh = xv * r
        st_o[...] += _colsum(dh * xh)
        dx_o[...] = dx1_ref[...] + _rms_bwd(dh, xh, r, g, D_MODEL)
        _comm_finish(comm, cin, cout, csem, grid)

    return pl.pallas_call(
        body, name="in_bwd", grid=grid,
        out_shape=(jax.ShapeDtypeStruct((t, D_MODEL), F32), jax.ShapeDtypeStruct((1, D_MODEL), F32), *c_outs),
        in_specs=[_rows(tm, ZP), _rows(tm, D_MODEL), _rows(tm, D_MODEL),
                  _resident((1, D_MODEL)), _resident((ZP, D_MODEL)), *c_in_specs],
        out_specs=(_rows(tm, D_MODEL), pl.BlockSpec((1, D_MODEL), lambda i: (0, 0)), *([_ANY] * len(c_outs))),
        scratch_shapes=c_sems, input_output_aliases=alias,
        compiler_params=_params(("arbitrary",)),
    )(dz, x, dx1, g_mix, w_in_t, *c_ins)


def matmul_tn(a, b, name, square_a=False):
    t, m = a.shape
    n = b.shape[1]
    bm = min(m, 512)
    tk = min(t, 4096)

    def body(a_ref, b_ref, o_ref):
        @pl.when(pl.program_id(1) == 0)
        def _():
            o_ref[...] = jnp.zeros_like(o_ref)

        av = a_ref[...]
        if square_a:
            av = (av.astype(F32) * av.astype(F32))
        o_ref[...] += _dot_tn(av.astype(BF16), b_ref[...].astype(BF16))

    return pl.pallas_call(
        body, name=name, grid=(m // bm, t // tk), out_shape=jax.ShapeDtypeStruct((m, n), F32),
        in_specs=[pl.BlockSpec((tk, bm), lambda i, kk: (kk, i)), pl.BlockSpec((tk, n), lambda i, kk: (kk, 0))],
        out_specs=pl.BlockSpec((bm, n), lambda i, kk: (i, 0)),
        compiler_params=_params(("parallel", "arbitrary")),
    )(a, b)


def matmul_tn_packed(a, b, name, rows, row_off, total_rows, buf=None, square_a=False):
    t, m = a.shape
    n = b.shape[1]
    pd = max(1, 512 // rows)
    bm = pd * rows
    tk = min(t, 4096)
    nk = t // tk

    def body(a_ref, b_ref, *rest):
        o_ref, acc = rest[-2], rest[-1]

        @pl.when(pl.program_id(1) == 0)
        def _():
            acc[...] = jnp.zeros_like(acc)

        av = a_ref[...]
        if square_a:
            av = (av.astype(F32) * av.astype(F32))
        acc[...] += _dot_tn(av.astype(BF16), b_ref[...].astype(BF16))

        @pl.when(pl.program_id(1) == nk - 1)
        def _():
            o_ref[...] = acc[...].reshape(pd, rows, n).astype(o_ref.dtype)

    in_specs = [pl.BlockSpec((tk, bm), lambda i, kk: (kk, i)), pl.BlockSpec((tk, n), lambda i, kk: (kk, 0))]
    args = [a, b]
    if buf is not None:
        in_specs.append(_ANY)
        args.append(buf)
    return pl.pallas_call(
        body, name=name, grid=(m // bm, nk), out_shape=jax.ShapeDtypeStruct((N_DEV, total_rows, n), BF16),
        in_specs=in_specs, out_specs=pl.BlockSpec((pd, rows, n), lambda i, kk: (i, row_off // rows, 0)),
        scratch_shapes=[pltpu.VMEM((bm, n), F32)], input_output_aliases={2: 0} if buf is not None else {},
        compiler_params=_params(("parallel", "arbitrary")),
    )(*args)


def adamw(w, g, m, v, name, g_transposed=False):
    _, r, c = w.shape
    tr = 256 if (not g_transposed and r > 256 and r % 256 == 0) else r
    c1 = 1.0 - ADAM_B1 ** ADAM_STEP
    c2 = 1.0 - ADAM_B2 ** ADAM_STEP

    def body(w_ref, g_ref, m_ref, v_ref, g_o, d_o, m_o, v_o):
        gv = g_ref[...].T if g_transposed else g_ref[...]
        mn = ADAM_B1 * m_ref[0] + (1.0 - ADAM_B1) * gv
        vn = ADAM_B2 * v_ref[0] + (1.0 - ADAM_B2) * (gv * gv)
        g_o[0] = gv
        m_o[0] = mn
        v_o[0] = vn
        d_o[0] = -ADAM_LR * ((mn / c1) / (jnp.sqrt(vn / c2) + ADAM_EPS) + ADAM_WD * w_ref[0])

    spec = pl.BlockSpec((1, tr, c), lambda i: (0, i, 0))
    gspec = pl.BlockSpec((c, r), lambda i: (0, 0)) if g_transposed else pl.BlockSpec((tr, c), lambda i: (i, 0))
    shp = jax.ShapeDtypeStruct((1, r, c), F32)
    return pl.pallas_call(
        body, name=name, grid=(r // tr,), out_shape=(shp,) * 4, in_specs=[spec, gspec, spec, spec], out_specs=(spec,) * 4,
        compiler_params=_params(("parallel",)),
    )(w, g, m, v)


def _rope_tables(s_len):
    def angles(pos, dim):
        inv = np.float32(ROPE_THETA) ** (-np.arange(0, dim, 2, dtype=np.float32) / np.float32(dim))
        return pos.astype(np.float32)[:, None] * inv[None, :]

    tpos = np.arange(s_len)
    a1 = angles(tpos, QK_ROPE)
    ar = angles(tpos // GRID_W, HD_B // 2)
    ac = angles(tpos % GRID_W, HD_B // 2)
    z16 = np.zeros((s_len, 16), np.float32)
    z32 = np.zeros((s_len, 32), np.float32)
    z64 = np.zeros((s_len, 64), np.float32)
    one64 = np.ones((s_len, 64), np.float32)
    c1, s1 = np.cos(a1), np.sin(a1)
    ca = np.concatenate([one64, c1, c1, z32], axis=1)
    ck = np.concatenate([z64, c1, c1, z32], axis=1)
    s1a = np.concatenate([z64, -s1, z16, z32], axis=1)
    s2a = np.concatenate([z64, z16, s1, z32], axis=1)
    cr, sr, cc, sc = np.cos(ar), np.sin(ar), np.cos(ac), np.sin(ac)
    cb = np.concatenate([cr, cr, cc, cc, z64], axis=1)
    s1b = np.concatenate([-sr, z16, -sc, z16, z64], axis=1)
    s2b = np.concatenate([z16, sr, z16, sc, z64], axis=1)
    return jnp.asarray(np.stack([ca, s1a, s2a, ck, cb, s1b, s2b]).astype(np.float32))


def _pad_heads(a, n_heads, axis):
    shp = a.shape
    a = a.reshape(shp[:axis] + (n_heads, shp[axis] // n_heads) + shp[axis + 1:])
    pad = [(0, 0)] * a.ndim
    pad[axis + 1] = (0, HP - a.shape[axis + 1])
    a = jnp.pad(a, pad)
    return a.reshape(shp[:axis] + (n_heads * HP,) + shp[axis + 1:])


def _unpad_heads(a, n_heads, width, axis):
    shp = a.shape
    a = a.reshape(shp[:axis] + (n_heads, HP) + shp[axis + 1:])
    a = lax.slice_in_dim(a, 0, width, axis=axis + 1)
    return a.reshape(shp[:axis] + (n_heads * width,) + shp[axis + 1:])


def _pack_rows(blocks, names):
    parts = []
    for name in names:
        b = blocks[name]
        padr = PACK_ROWS[name] - b.shape[-2]
        if padr:
            b = jnp.pad(b, [(0, 0)] * (b.ndim - 2) + [(0, padr), (0, 0)])
        parts.append(b)
    return jnp.concatenate(parts, axis=parts[0].ndim - 2)


def _expand_w_in(wt):
    z64 = jnp.zeros((64, D_MODEL), wt.dtype)
    z32 = jnp.zeros((32, D_MODEL), wt.dtype)
    return jnp.concatenate([
        wt[1184:2208], wt[2208:3232], _pad_heads(wt[416:928], H_B, 0), wt[0:256],
        _pad_heads(wt[928:1056], KV_B, 0), _pad_heads(wt[1056:1184], KV_B, 0), wt[256:384],
        z64, wt[384:416], z32], axis=0)


def _collapse_w_in(dw):
    dg, dq, ds = dw[0:2048], dw[2048:3072], dw[3072:4096]
    return jnp.concatenate([
        ds[0:256], ds[768:896], ds[960:992], _unpad_heads(dq, H_B, HD_B, 0), _unpad_heads(ds[256:512], KV_B, HD_B, 0),
        _unpad_heads(ds[512:768], KV_B, HD_B, 0), dg], axis=0)


def kernel(x, p, g_mix, w_in, g_qa, w_qb, g_kva, w_kvb, g_qn, g_kn, w_oa, w_ob, w_o, g_mlp, w_up, w_down, g_ple, w_ple_gate, w_ple, g_final, loss_target, m_g_mix, m_w_in, m_g_qa, m_w_qb, m_g_kva, m_w_kvb, m_g_qn, m_g_kn, m_w_oa, m_w_ob, m_w_o, m_g_mlp, m_w_up, m_w_down, m_g_ple, m_w_ple_gate, m_w_ple, m_g_final, v_g_mix, v_w_in, v_g_qa, v_w_qb, v_g_kva, v_w_kvb, v_g_qn, v_g_kn, v_w_oa, v_w_ob, v_w_o, v_g_mlp, v_w_up, v_w_down, v_g_ple, v_w_ple_gate, v_w_ple, v_g_final):
    n_b, s_len, _ = x.shape
    t = n_b * s_len
    tm = min(512, s_len)
    tq_f = min(2048, s_len)
    tq_b = min(2048, s_len)

    mats = dict(w_in=(w_in, m_w_in, v_w_in), w_qb=(w_qb, m_w_qb, v_w_qb), w_kvb=(w_kvb, m_w_kvb, v_w_kvb),
                w_oa=(w_oa, m_w_oa, v_w_oa), w_ob=(w_ob, m_w_ob, v_w_ob), w_o=(w_o, m_w_o, v_w_o),
                w_up=(w_up, m_w_up, v_w_up), w_down=(w_down, m_w_down, v_w_down),
                w_ple_gate=(w_ple_gate, m_w_ple_gate, v_w_ple_gate), w_ple=(w_ple, m_w_ple, v_w_ple))
    col_sharded = ("w_in", "w_qb", "w_kvb", "w_oa", "w_ob", "w_up", "w_ple")

    blocks = {}
    for name in PACK_W1 + ("w_oa", "w_ob", "w_ple"):
        blocks[name] = mats[name][0][0].T.reshape(-1, D_MODEL).astype(BF16)
    off_w1, _ = _pack_offsets(PACK_W1)
    off_w2, _ = _pack_offsets(PACK_W2)
    off_w3, _ = _pack_offsets(PACK_W3)
    xf = x.reshape(t, D_MODEL)
    h, full1 = norm_x(xf, g_mix, tm, comm=gather_first_comm(_pack_rows(blocks, PACK_W1)))
    pack2, pack3, full1 = pack_late_weights(w_up, w_down, w_o, w_ple_gate, _pack_rows(blocks, ("w_oa", "w_ob", "w_ple")),
                                            comm=gather_pass_comm(full1))

    def gathered(full, offs, name, rows, width):
        return full[:, offs[name]:offs[name] + rows].reshape(-1, width)

    w_in_t = _expand_w_in(gathered(full1, off_w1, "w_in", 404, D_MODEL))
    w_qb_t = _pad_heads(gathered(full1, off_w1, "w_qb", 24, Q_LORA), H_A, 0)
    wkvb = gathered(full1, off_w1, "w_kvb", 16, KV_LORA).reshape(H_A, 2, 64, KV_LORA)
    w_kvb_t = jnp.concatenate([_pad_heads(wkvb[:, 0].reshape(-1, KV_LORA), H_A, 0),
                               _pad_heads(wkvb[:, 1].reshape(-1, KV_LORA), H_A, 0)], axis=0)

    tabs = _rope_tables(s_len)
    g_qn_p = jnp.pad(g_qn, ((0, 0), (0, HP - HD_B)))
    g_kn_p = jnp.pad(g_kn, ((0, 0), (0, HP - HD_B)))
    pf = p.reshape(t, PLE_DIM)
    tgt = loss_target.reshape(t, D_MODEL)

    zg, zs, full2 = in_proj(h, w_in_t, tm, comm=gather_first_comm(pack2))
    qa, ka, va, qb, kb, vb, cq, ckv, full2 = attn_prep(zs, tabs, g_qa, g_kva, g_qn_p, g_kn_p, w_qb_t, w_kvb_t, tm, s_len,
                                                       comm=gather_pass_comm(full2))
    oa, lse_a, full3 = attn_fwd(qa, ka, va, n_b, s_len, tq_f, "attn_a_fwd", comm=gather_first_comm(pack3))
    ob, lse_b, full3 = attn_fwd(qb, kb, vb, n_b, s_len, tq_f, "attn_b_fwd", comm=gather_pass_comm(full3))
    w_oa_t = _pad_heads(gathered(full2, off_w2, "w_oa", 64, H_A * V_DIM_A), H_A, 1)
    w_ob_t = _pad_heads(gathered(full2, off_w2, "w_ob", 64, H_B * HD_B), H_B, 1)
    w_ple_t = gathered(full2, off_w2, "w_ple", 32, PLE_DIM)
    x1, merged, yab = merge_fwd(oa, ob, zg, xf, w_oa_t, w_ob_t, full2, off_w2, tm)
    x2, u = mlp_fwd(x1, g_mlp, full3, off_w3, tm)
    dx2, dt, h3, dpe, st_ple, dx2b = ple_loss_bwd(x2, pf, tgt, g_ple, g_final.reshape(1, D_MODEL), full2, off_w2, w_ple_t, tm)
    dx1, da, h2, st_mlp, dx1b = mlp_bwd(dx2, x1, u, g_mlp, full3, off_w3, tm)

    core = lax.axis_index("c").astype(jnp.int32).reshape(1)
    chip = (2 * lax.axis_index("x") + lax.axis_index("y")).astype(jnp.int32).reshape(1)

    def packed(gblocks, names):
        return _pack_rows({n: gblocks[n].reshape(N_DEV, -1, D_MODEL).astype(BF16) for n in names}, names)

    off_g1, rows_g1 = _pack_offsets(PACK_G1)
    gpack1 = matmul_tn_packed(da, h2, "gw_up", 512, off_g1["w_up"], rows_g1)
    gpack1 = matmul_tn_packed(u, dx2b, "gw_down", 512, off_g1["w_down"], rows_g1, buf=gpack1, square_a=True)
    gpack1 = matmul_tn_packed(h3, dt, "gw_pg", 128, off_g1["w_ple_gate"], rows_g1, buf=gpack1)
    gple = matmul_tn(dpe, pf, "gw_ple").reshape(N_DEV, -1, D_MODEL).astype(BF16)
    gpack1 = lax.dynamic_update_slice(gpack1, gple, (0, off_g1["w_ple"], 0))
    doa, dob, dz, dya, dyb, got1 = merge_bwd(dx1b, yab, zg, w_oa_t, w_ob_t, full2, off_w2, tm,
                                               comm=scatter_sibling_comm(gpack1))
    part1 = add_pairs(gpack1, got1, core)

    off_g2, rows_g2 = _pack_offsets(PACK_G2)
    g2 = dict(w_oa=_unpad_heads(matmul_tn(dya, oa, "gw_oa"), H_A, V_DIM_A, 1),
              w_ob=_unpad_heads(matmul_tn(dyb, ob, "gw_ob"), H_B, HD_B, 1))
    gpack2 = matmul_tn_packed(merged, dx1b, "gw_o", 128, off_g2["w_o"], rows_g2)
    gpack2 = lax.dynamic_update_slice(gpack2, packed(g2, ("w_oa", "w_ob")), (0, off_g2["w_oa"], 0))
    dqa, dka, dva, land1, got2 = attn_bwd(qa, ka, va, doa, oa, lse_a, n_b, s_len, tq_b, "attn_a_bwd",
                                          comm=_join_comms(scatter_chips_comm(part1), scatter_sibling_comm(gpack2)))
    gshard1 = sum_chips(part1, land1, chip)
    part2 = add_pairs(gpack2, got2, core)
    dqb, dkb, dvb, land2 = attn_bwd(qb, kb, vb, dob, ob, lse_b, n_b, s_len, tq_b, "attn_b_bwd", comm=scatter_chips_comm(part2))
    gshard2 = sum_chips(part2, land2, chip)
    dz, dqap, dkva, st_prep = prep_bwd(dqa, dka, dva, dqb, dkb, dvb, zs, dz, tabs, g_qa, g_kva, g_qn_p, g_kn_p,
                                       w_qb_t, w_kvb_t, tm, s_len)

    gkv = matmul_tn(dkva, ckv, "gw_kvb")
    g3 = dict(
        w_in=_collapse_w_in(matmul_tn(dz, h, "gw_in")),
        w_qb=_unpad_heads(matmul_tn(dqap, cq, "gw_qb"), H_A, QK_NOPE + QK_ROPE, 0),
        w_kvb=jnp.stack([_unpad_heads(gkv[:H_A * HP], H_A, 64, 0).reshape(H_A, 64, KV_LORA),
                         _unpad_heads(gkv[H_A * HP:], H_A, 64, 0).reshape(H_A, 64, KV_LORA)], axis=1))
    gpack3 = packed(g3, PACK_G3)
    part3 = add_pairs(gpack3, exchange_sibling(gpack3), core)
    grad_x, st_mix, land3 = in_bwd(dz, xf, dx1, g_mix, w_in_t, tm, comm=scatter_chips_comm(part3))
    gshard3 = sum_chips(part3, land3, chip)
    off_g3, _ = _pack_offsets(PACK_G3)
    shards = {n: (gshard1, off_g1[n]) for n in PACK_G1}
    shards.update({n: (gshard2, off_g2[n]) for n in PACK_G2})
    shards.update({n: (gshard3, off_g3[n]) for n in PACK_G3})

    stats = allreduce_stats(st_mix, st_prep, st_mlp, st_ple)
    loss = jnp.sum(stats[ST_LOSS])

    out_g, out_d, out_m, out_v = {}, {}, {}, {}
    for name, (w, m, v) in mats.items():
        gshard, off = shards[name]
        r, c = w.shape[1:]
        if name in col_sharded:
            g2 = gshard[off:off + (r * c) // D_MODEL].reshape(c, r)
            if r % 128 == 0 and c % 128 == 0:
                res = adamw(w, g2, m, v, "adamw_" + name, g_transposed=True)
            else:
                res = adamw(w[0].T[None], g2, m[0].T[None], v[0].T[None], "adamw_" + name)
                res = tuple(a[0].T[None] for a in res)
        else:
            res = adamw(w, gshard[off:off + r], m, v, "adamw_" + name)
        out_g[name], out_d[name], out_m[name], out_v[name] = res

    gains = (("g_mix", g_mix, m_g_mix, v_g_mix, ST_G_MIX), ("g_qa", g_qa, m_g_qa, v_g_qa, ST_G_QA),
             ("g_kva", g_kva, m_g_kva, v_g_kva, ST_G_KVA), ("g_qn", g_qn, m_g_qn, v_g_qn, ST_G_QN),
             ("g_kn", g_kn, m_g_kn, v_g_kn, ST_G_KN), ("g_mlp", g_mlp, m_g_mlp, v_g_mlp, ST_G_MLP),
             ("g_ple", g_ple, m_g_ple, v_g_ple, ST_G_PLE), ("g_final", g_final, m_g_final, v_g_final, ST_G_FINAL))
    res = adamw_gains(stats, [(r_, w.reshape(1, -1), m.reshape(1, -1), v.reshape(1, -1)) for _, w, m, v, r_ in gains])
    for (name, w, _, _, _), (gg, gd, gm, gv) in zip(gains, res):
        out_g[name], out_d[name], out_m[name], out_v[name] = (a.reshape(w.shape) for a in (gg, gd, gm, gv))

    order = ("g_mix", "w_in", "g_qa", "w_qb", "g_kva", "w_kvb", "g_qn", "g_kn", "w_oa", "w_ob", "w_o", "g_mlp",
             "w_up", "w_down", "g_ple", "w_ple_gate", "w_ple", "g_final")
    return (loss, grad_x.reshape(x.shape), *[out_g[n] for n in order], *[out_d[n] for n in order],
            *[out_m[n] for n in order], *[out_v[n] for n in order])
```

```python
import numpy as np
import jax
import jax.numpy as jnp
from jax import lax
from jax.experimental import pallas as pl
from jax.experimental.pallas import tpu as pltpu

F32 = jnp.float32
BF16 = jnp.bfloat16

D_MODEL = 1024
EPS = 1e-6
ROPE_THETA = 10000.0
GRID_W = 64
H_A = 8
QK_NOPE = 64
QK_ROPE = 32
V_DIM_A = 64
Q_LORA = 256
KV_LORA = 128
H_B = 8
KV_B = 2
HD_B = 64
D_FF = 4 * D_MODEL
PLE_DIM = 256
HP = 128
ZP = 4096
N_DEV = 8
N_CHIP = 4

ADAM_LR = 0.001
ADAM_B1 = 0.9
ADAM_B2 = 0.999
ADAM_EPS = 1e-08
ADAM_WD = 0.01
ADAM_STEP = 10

VMEM_LIMIT = 52 * 1024 * 1024

PACK_ROWS = dict(w_in=416, w_qb=32, w_kvb=16, w_oa=64, w_ob=64, w_o=128, w_up=512, w_down=512, w_ple_gate=128, w_ple=32)
PACK_W1 = ("w_in", "w_qb", "w_kvb")
PACK_W2 = ("w_o", "w_ple_gate", "w_oa", "w_ob", "w_ple")
PACK_W3 = ("w_up", "w_down")
PACK_G1 = ("w_up", "w_down", "w_ple_gate", "w_ple")
PACK_G2 = ("w_o", "w_oa", "w_ob")
PACK_G3 = ("w_in", "w_qb", "w_kvb")


def _pack_offsets(names):
    off, o = {}, 0
    for n in names:
        off[n] = o
        o += PACK_ROWS[n]
    return off, o

ST_G_MIX, ST_G_QA, ST_G_KVA, ST_G_QN, ST_G_KN, ST_G_MLP, ST_G_PLE, ST_G_FINAL, ST_LOSS = range(9)
ST_ROWS = 16


def _dot_nn(a, b):
    return lax.dot_general(a, b, (((1,), (0,)), ((), ())), preferred_element_type=F32)


def _dot_nt(a, b):
    return lax.dot_general(a, b, (((1,), (1,)), ((), ())), preferred_element_type=F32)


def _dot_tn(a, b):
    return lax.dot_general(a, b, (((0,), (0,)), ((), ())), preferred_element_type=F32)


def _rstd(x, n):
    return lax.rsqrt(jnp.sum(x * x, axis=-1, keepdims=True) * (1.0 / n) + EPS)


def _rms_bwd(dy, xh, r, g, n):
    dxh = dy * g
    return r * (dxh - xh * (jnp.sum(dxh * xh, axis=-1, keepdims=True) * (1.0 / n)))


def _rope_fwd(x, c, s1, s2):
    return x * c + pltpu.roll(x, HP - 16, 1) * s1 + pltpu.roll(x, 16, 1) * s2


def _rope_bwd(d, c, s1, s2):
    return d * c + pltpu.roll(d * s1, 16, 1) + pltpu.roll(d * s2, HP - 16, 1)


def _colsum(v):
    return jnp.sum(v, axis=0, keepdims=True)


def _params(sem=None, vmem=VMEM_LIMIT):
    return pltpu.CompilerParams(dimension_semantics=sem, vmem_limit_bytes=vmem)


def _resident(shape):
    nd = len(shape)
    return pl.BlockSpec(shape, lambda *_: (0,) * nd, pipeline_mode=pl.Buffered(1))


def _rows(tm, width, col=0):
    return pl.BlockSpec((tm, width), lambda i: (i, col))


def _packed_weight(rows, off):
    return pl.BlockSpec((N_DEV, rows, D_MODEL), lambda *_: (0, off // rows, 0), pipeline_mode=pl.Buffered(1))


def _wrows(ref, start, size):
    rows = ref.shape[1]
    return ref[start // rows:(start + size) // rows].reshape(size, D_MODEL)


def _mesh_pos():
    return lax.axis_index("x"), lax.axis_index("y"), lax.axis_index("c")


def _flip(v, bit):
    return (1 - v) if bit else v


_ANY = pl.BlockSpec(memory_space=pl.ANY)
_MESH = pl.DeviceIdType.MESH


def _remote(src, dst, send_sems, recv_sems, k, to):
    return pltpu.make_async_remote_copy(src_ref=src, dst_ref=dst, send_sem=send_sems.at[k], recv_sem=recv_sems.at[k],
                                        device_id=to, device_id_type=_MESH)


def _sibling_copies(g_ref, got_ref, send_sems, recv_sems):
    x, y, c = _mesh_pos()
    return [_remote(g_ref.at[2 * j + (1 - c)], got_ref.at[j], send_sems, recv_sems, j, (x, y, 1 - c)) for j in range(N_CHIP)]


def _chip_copies(p_ref, land_ref, send_sems, recv_sems):
    x, y, c = _mesh_pos()
    copies = []
    for k in (1, 2, 3):
        tx, ty = _flip(x, k & 2), _flip(y, k & 1)
        copies.append(_remote(p_ref.at[2 * tx + ty], land_ref.at[k - 1], send_sems, recv_sems, k - 1, (tx, ty, c)))
    return copies


class _Comm:
    def __init__(self, ins, out_shapes, sems, make, aliases=None):
        self.ins, self.out_shapes, self.sems, self.make, self.aliases = list(ins), list(out_shapes), list(sems), make, aliases or {}


def _join_comms(a, b):
    n_i, n_o, n_s = len(a.ins), len(a.out_shapes), len(a.sems)

    def make(cin, cout, sems):
        return a.make(cin[:n_i], cout[:n_o], sems[:n_s]) + b.make(cin[n_i:], cout[n_o:], sems[n_s:])

    aliases = dict(a.aliases)
    aliases.update({n_i + j: n_o + k for j, k in b.aliases.items()})
    return _Comm(a.ins + b.ins, a.out_shapes + b.out_shapes, a.sems + b.sems, make, aliases)


def _comm_parts(comm, n_in, n_out):
    if comm is None:
        return [], [], [], [], {}
    alias = {n_in + j: n_out + k for j, k in comm.aliases.items()}
    return comm.ins, [_ANY] * len(comm.ins), comm.out_shapes, comm.sems, alias


def _split_refs(refs, n_in, n_out, n_scratch, comm):
    n_ci = len(comm.ins) if comm else 0
    n_co = len(comm.out_shapes) if comm else 0
    cuts, i = [], 0
    for n in (n_in, n_ci, n_out, n_co, n_scratch):
        cuts.append(refs[i:i + n])
        i += n
    return (*cuts, refs[i:])


def _grid_edge(grid, last):
    cond = None
    for d, n in enumerate(grid):
        here = pl.program_id(d) == (n - 1 if last else 0)
        cond = here if cond is None else cond & here
    return cond


def _comm_start(comm, cin, cout, csem, grid):
    if comm is not None:
        @pl.when(_grid_edge(grid, False))
        def _():
            for cp in comm.make(cin, cout, csem):
                cp.start()


def _comm_finish(comm, cin, cout, csem, grid):
    if comm is not None:
        @pl.when(_grid_edge(grid, True))
        def _():
            for cp in comm.make(cin, cout, csem):
                cp.wait()


def gather_first_comm(shard):
    r, w = shard.shape

    def make(cin, cout, sems):
        (x_ref,), (out_ref,), (send_sems, recv_sems, local_sem) = cin, cout, sems
        x, y, c = _mesh_pos()
        mine = out_ref.at[4 * x + 2 * y + c]
        targets = [(x, y, 1 - c), (1 - x, y, c), (x, 1 - y, c), (1 - x, 1 - y, c)]
        return [_remote(x_ref, mine, send_sems, recv_sems, k, to) for k, to in enumerate(targets)] + [
            pltpu.make_async_copy(x_ref, mine, local_sem)]

    return _Comm([shard], [jax.ShapeDtypeStruct((N_DEV, r, w), shard.dtype)],
                 [pltpu.SemaphoreType.DMA((4,)), pltpu.SemaphoreType.DMA((4,)), pltpu.SemaphoreType.DMA], make)


def gather_pass_comm(full):
    def make(cin, cout, sems):
        (in_ref,), (out_ref,), (send_sems, recv_sems) = cin, cout, sems
        x, y, c = _mesh_pos()
        copies = []
        for k, (px, py) in enumerate([(1 - x, y), (x, 1 - y), (1 - x, 1 - y)]):
            idx = 4 * px + 2 * py + c
            copies.append(_remote(in_ref.at[idx], out_ref.at[idx], send_sems, recv_sems, k, (x, y, 1 - c)))
        return copies

    return _Comm([full], [jax.ShapeDtypeStruct(full.shape, full.dtype)],
                 [pltpu.SemaphoreType.DMA((3,)), pltpu.SemaphoreType.DMA((3,))], make, aliases={0: 0})


def scatter_sibling_comm(g):
    _, r, w = g.shape
    return _Comm([g], [jax.ShapeDtypeStruct((N_CHIP, r, w), g.dtype)],
                 [pltpu.SemaphoreType.DMA((N_CHIP,)), pltpu.SemaphoreType.DMA((N_CHIP,))],
                 lambda cin, cout, sems: _sibling_copies(cin[0], cout[0], sems[0], sems[1]))


def scatter_chips_comm(part):
    _, r, w = part.shape
    return _Comm([part], [jax.ShapeDtypeStruct((N_CHIP - 1, r, w), part.dtype)],
                 [pltpu.SemaphoreType.DMA((3,)), pltpu.SemaphoreType.DMA((3,))],
                 lambda cin, cout, sems: _chip_copies(cin[0], cout[0], sems[0], sems[1]))


def exchange_sibling(g):
    _, r, w = g.shape

    def body(g_ref, got_ref, send_sems, recv_sems):
        copies = _sibling_copies(g_ref, got_ref, send_sems, recv_sems)
        for cp in copies:
            cp.start()
        for cp in copies:
            cp.wait()

    return pl.pallas_call(
        body, name="exchange_sibling", out_shape=jax.ShapeDtypeStruct((N_CHIP, r, w), g.dtype),
        in_specs=[_ANY], out_specs=_ANY,
        scratch_shapes=[pltpu.SemaphoreType.DMA((N_CHIP,)), pltpu.SemaphoreType.DMA((N_CHIP,))],
    )(g)


def exchange_chips(part):
    _, r, w = part.shape

    def body(p_ref, land_ref, send_sems, recv_sems):
        copies = _chip_copies(p_ref, land_ref, send_sems, recv_sems)
        for cp in copies:
            cp.start()
        for cp in copies:
            cp.wait()

    return pl.pallas_call(
        body, name="exchange_chips", out_shape=jax.ShapeDtypeStruct((N_CHIP - 1, r, w), part.dtype),
        in_specs=[_ANY], out_specs=_ANY,
        scratch_shapes=[pltpu.SemaphoreType.DMA((3,)), pltpu.SemaphoreType.DMA((3,))],
    )(part)


def allreduce_stats(st_mix, st_prep, st_mlp, st_ple):
    def body(mix_ref, prep_ref, mlp_ref, ple_ref, out_ref, mine, gath, send_sems, recv_sems):
        x, y, c = _mesh_pos()
        me = 4 * x + 2 * y + c
        mine[...] = jnp.zeros_like(mine)
        mine[ST_G_MIX:ST_G_MIX + 1, :] = mix_ref[...]
        mine[ST_G_QA:ST_G_KN + 1, 0:256] = prep_ref[...]
        mine[ST_G_MLP:ST_G_MLP + 1, :] = mlp_ref[...]
        mine[ST_G_PLE:ST_LOSS + 1, :] = ple_ref[...]
        gath[me] = mine[...]
        copies = []
        for k in range(1, N_DEV):
            peer = (_flip(x, k & 4), _flip(y, k & 2), _flip(c, k & 1))
            copies.append(_remote(mine, gath.at[me], send_sems, recv_sems, k - 1, peer))
        for cp in copies:
            cp.start()
        for cp in copies:
            cp.wait()
        acc = gath[0]
        for d in range(1, N_DEV):
            acc = acc + gath[d]
        out_ref[...] = acc

    vm = pl.BlockSpec(memory_space=pltpu.VMEM)
    return pl.pallas_call(
        body, name="allreduce_stats", out_shape=jax.ShapeDtypeStruct((ST_ROWS, D_MODEL), F32),
        in_specs=[vm] * 4, out_specs=vm,
        scratch_shapes=[pltpu.VMEM((ST_ROWS, D_MODEL), F32), pltpu.VMEM((N_DEV, ST_ROWS, D_MODEL), F32),
                        pltpu.SemaphoreType.DMA((N_DEV - 1,)), pltpu.SemaphoreType.DMA((N_DEV - 1,))],
    )(st_mix, st_prep, st_mlp, st_ple)


def adamw_gains(stats, gains):
    c1 = 1.0 - ADAM_B1 ** ADAM_STEP
    c2 = 1.0 - ADAM_B2 ** ADAM_STEP
    n = len(gains)

    def body(st_ref, *refs):
        ins, outs = refs[:3 * n], refs[3 * n:]
        for i, (row, w, _, _) in enumerate(gains):
            width = w.shape[1]
            gv = st_ref[row:row + 1, 0:width]
            mn = ADAM_B1 * ins[3 * i + 1][...] + (1.0 - ADAM_B1) * gv
            vn = ADAM_B2 * ins[3 * i + 2][...] + (1.0 - ADAM_B2) * (gv * gv)
            outs[4 * i][...] = gv
            outs[4 * i + 1][...] = -ADAM_LR * ((mn / c1) / (jnp.sqrt(vn / c2) + ADAM_EPS) + ADAM_WD * ins[3 * i][...])
            outs[4 * i + 2][...] = mn
            outs[4 * i + 3][...] = vn

    vm = pl.BlockSpec(memory_space=pltpu.VMEM)
    flat = [a for (_, w, m, v) in gains for a in (w, m, v)]
    out_shape = tuple(jax.ShapeDtypeStruct(w.shape, F32) for (_, w, _, _) in gains for _ in range(4))
    res = pl.pallas_call(body, name="adamw_gains", out_shape=out_shape, in_specs=[vm] * (1 + 3 * n),
                         out_specs=tuple([vm] * (4 * n)))(stats, *flat)
    return [res[4 * i:4 * i + 4] for i in range(n)]


def _row_tile(r, cap=640):
    return max(d for d in range(16, min(r, cap) + 1, 16) if r % d == 0)


def add_pairs(g, got, core):
    n, r, w = got.shape
    tr = _row_tile(r)

    def body(c_ref, a_ref, b_ref, o_ref):
        o_ref[...] = (a_ref[...].astype(F32) + b_ref[...].astype(F32)).astype(o_ref.dtype)

    spec = pl.BlockSpec((1, tr, w), lambda i, j, c: (i, j, 0))
    return pl.pallas_call(
        body, name="add_pairs", out_shape=jax.ShapeDtypeStruct(got.shape, got.dtype),
        grid_spec=pltpu.PrefetchScalarGridSpec(
            num_scalar_prefetch=1, grid=(n, r // tr),
            in_specs=[pl.BlockSpec((1, tr, w), lambda i, j, c: (2 * i + c[0], j, 0)), spec], out_specs=spec),
        compiler_params=_params(("parallel", "parallel")),
    )(core, g, got)


def sum_chips(part, land, chip):
    _, r, w = part.shape
    tr = _row_tile(r)

    def body(c_ref, p_ref, l_ref, o_ref):
        acc = p_ref[0].astype(F32)
        for s in range(N_CHIP - 1):
            acc = acc + l_ref[s].astype(F32)
        o_ref[...] = acc

    return pl.pallas_call(
        body, name="sum_chips", out_shape=jax.ShapeDtypeStruct((r, w), F32),
        grid_spec=pltpu.PrefetchScalarGridSpec(
            num_scalar_prefetch=1, grid=(r // tr,),
            in_specs=[pl.BlockSpec((1, tr, w), lambda i, c: (c[0], i, 0)), pl.BlockSpec((N_CHIP - 1, tr, w), lambda i, c: (0, i, 0))],
            out_specs=pl.BlockSpec((tr, w), lambda i, c: (i, 0))),
        compiler_params=_params(("parallel",)),
    )(chip, part, land)


def norm_x(x, g_mix, tm, comm=None):
    t = x.shape[0]
    grid = (t // tm,)
    c_ins, c_in_specs, c_outs, c_sems, alias = _comm_parts(comm, 2, 1)

    def body(*refs):
        (x_ref, g_ref), cin, (h_ref,), cout, _, csem = _split_refs(refs, 2, 1, 0, comm)
        _comm_start(comm, cin, cout, csem, grid)
        xv = x_ref[...]
        h_ref[...] = (xv * _rstd(xv, D_MODEL) * g_ref[...]).astype(BF16)
        _comm_finish(comm, cin, cout, csem, grid)

    return pl.pallas_call(
        body, name="norm_x", grid=grid, out_shape=(jax.ShapeDtypeStruct((t, D_MODEL), BF16), *c_outs),
        in_specs=[_rows(tm, D_MODEL), _resident((1, D_MODEL)), *c_in_specs],
        out_specs=(_rows(tm, D_MODEL), *([_ANY] * len(c_outs))),
        scratch_shapes=c_sems, input_output_aliases=alias, compiler_params=_params(("arbitrary",)),
    )(x, g_mix, *c_ins)


def pack_late_weights(w_up, w_down, w_o, w_pg, small, comm=None):
    rows2 = sum(PACK_ROWS[n] for n in PACK_W2)
    rows3 = sum(PACK_ROWS[n] for n in PACK_W3)
    c_ins, c_in_specs, c_outs, c_sems, alias = _comm_parts(comm, 5, 2)
    grid = (1,)

    def body(*refs):
        (up_ref, dn_ref, o_ref, pg_ref, sm_ref), cin, (p2_ref, p3_ref), cout, _, csem = _split_refs(refs, 5, 2, 0, comm)
        _comm_start(comm, cin, cout, csem, grid)
        p2_ref[0:128, :] = o_ref[0].astype(BF16)
        p2_ref[128:256, :] = pg_ref[0].astype(BF16)
        p2_ref[256:rows2, :] = sm_ref[...]
        p3_ref[0:512, :] = up_ref[0].T.astype(BF16)
        p3_ref[512:1024, :] = dn_ref[0].astype(BF16)
        _comm_finish(comm, cin, cout, csem, grid)

    def whole(a):
        nd = a.ndim
        return pl.BlockSpec(a.shape, lambda i: (0,) * nd)

    args = (w_up, w_down, w_o, w_pg, small)
    return pl.pallas_call(
        body, name="pack_late_weights", grid=grid,
        out_shape=(jax.ShapeDtypeStruct((rows2, D_MODEL), BF16), jax.ShapeDtypeStruct((rows3, D_MODEL), BF16), *c_outs),
        in_specs=[*[whole(a) for a in args], *c_in_specs],
        out_specs=(pl.BlockSpec((rows2, D_MODEL), lambda i: (0, 0)), pl.BlockSpec((rows3, D_MODEL), lambda i: (0, 0)),
                   *([_ANY] * len(c_outs))),
        scratch_shapes=c_sems, input_output_aliases=alias, compiler_params=_params(("arbitrary",)),
    )(*args, *c_ins)


def in_proj(h, w_in_t, tm, comm=None):
    t = h.shape[0]
    nc = 512
    half = ZP // 2
    grid = (t // tm,)
    c_ins, c_in_specs, c_outs, c_sems, alias = _comm_parts(comm, 2, 2)

    def body(*refs):
        (h_ref, w_ref), cin, (zg_ref, zs_ref), cout, _, csem = _split_refs(refs, 2, 2, 0, comm)
        _comm_start(comm, cin, cout, csem, grid)
        hv = h_ref[...]
        for cidx in range(half // nc):
            zg_ref[:, cidx * nc:(cidx + 1) * nc] = _dot_nt(hv, w_ref[cidx * nc:(cidx + 1) * nc, :]).astype(BF16)
        for cidx in range(half // nc):
            zs_ref[:, cidx * nc:(cidx + 1) * nc] = _dot_nt(hv, w_ref[half + cidx * nc:half + (cidx + 1) * nc, :])
        _comm_finish(comm, cin, cout, csem, grid)

    return pl.pallas_call(
        body, name="in_proj", grid=grid,
        out_shape=(jax.ShapeDtypeStruct((t, half), BF16), jax.ShapeDtypeStruct((t, half), F32), *c_outs),
        in_specs=[_rows(tm, D_MODEL), _resident((ZP, D_MODEL)), *c_in_specs],
        out_specs=(_rows(tm, half), _rows(tm, half), *([_ANY] * len(c_outs))),
        scratch_shapes=c_sems, input_output_aliases=alias, compiler_params=_params(("arbitrary",)),
    )(h, w_in_t, *c_ins)


def attn_prep(zp, tabs, g_qa, g_kva, g_qn, g_kn, w_qb_t, w_kvb_t, tm, s_len, comm=None):
    t = zp.shape[0]
    nsb = s_len // tm
    scale_a = (QK_NOPE + QK_ROPE) ** -0.5
    scale_b = HD_B ** -0.5

    grid = (t // tm,)
    c_ins, c_in_specs, c_outs, c_sems, alias = _comm_parts(comm, 13, 8)

    def body(*refs):
        ((qb_ref, qlat_ref, kb_ref, vb_ref, ckv_ref, kpe_ref, tab_ref, gqa_ref, gkva_ref, gqn_ref, gkn_ref, wqb_ref,
          wkvb_ref), cin, (qa_o, ka_o, va_o, qb_o, kb_o, vb_o, cq_o, ckvn_o), cout, _, csem) = _split_refs(refs, 13, 8, 0, comm)
        _comm_start(comm, cin, cout, csem, grid)
        ca, s1a, s2a = tab_ref[0], tab_ref[1], tab_ref[2]
        ck = tab_ref[3]
        cb, s1b, s2b = tab_ref[4], tab_ref[5], tab_ref[6]
        ql = qlat_ref[...]
        cq = (ql * _rstd(ql, Q_LORA) * gqa_ref[...]).astype(BF16)
        cq_o[...] = cq
        qa = _dot_nt(cq, wqb_ref[...])
        slabs = [slice(h * HP, (h + 1) * HP) for h in range(H_A)]
        qa_o[...] = jnp.concatenate(
            [(_rope_fwd(qa[:, sl], ca, s1a, s2a) * scale_a).astype(BF16) for sl in slabs], axis=1)
        cr = ckv_ref[...]
        ckv = (cr * _rstd(cr, KV_LORA) * gkva_ref[...]).astype(BF16)
        ckvn_o[...] = ckv
        kva = _dot_nt(ckv, wkvb_ref[...])
        kpe = _rope_fwd(kpe_ref[...], ck, s1a, s2a)
        ka_o[...] = jnp.concatenate([(kva[:, sl] + kpe).astype(BF16) for sl in slabs], axis=1)
        va_o[...] = kva[:, H_A * HP:].astype(BF16)
        gqn, gkn = gqn_ref[...], gkn_ref[...]

        def norm_rope(ref, sl, g, scale):
            xs = ref[:, sl]
            y = _rope_fwd(xs * _rstd(xs, HD_B) * g, cb, s1b, s2b)
            return (y if scale is None else y * scale).astype(BF16)

        qb_o[...] = jnp.concatenate([norm_rope(qb_ref, sl, gqn, scale_b) for sl in slabs], axis=1)
        kb_o[...] = jnp.concatenate([norm_rope(kb_ref, sl, gkn, None) for sl in slabs[:KV_B]], axis=1)
        vb_o[...] = vb_ref[...].astype(BF16)
        _comm_finish(comm, cin, cout, csem, grid)

    def o(width):
        return jax.ShapeDtypeStruct((t, width), BF16)

    return pl.pallas_call(
        body, name="attn_prep", grid=grid,
        out_shape=(o(H_A * HP), o(H_A * HP), o(H_A * HP), o(H_B * HP), o(KV_B * HP), o(KV_B * HP), o(Q_LORA), o(KV_LORA),
                   *c_outs),
        in_specs=[_rows(tm, 1024, 0), _rows(tm, 256, 4), _rows(tm, 256, 5), _rows(tm, 256, 6),
                  _rows(tm, 128, 14), _rows(tm, 128, 15),
                  pl.BlockSpec((7, tm, HP), lambda i: (0, i % nsb, 0)),
                  _resident((1, Q_LORA)), _resident((1, KV_LORA)), _resident((1, HP)), _resident((1, HP)),
                  _resident((H_A * HP, Q_LORA)), _resident((2 * H_A * HP, KV_LORA)), *c_in_specs],
        out_specs=(_rows(tm, H_A * HP), _rows(tm, H_A * HP), _rows(tm, H_A * HP), _rows(tm, H_B * HP),
                   _rows(tm, KV_B * HP), _rows(tm, KV_B * HP), _rows(tm, Q_LORA), _rows(tm, KV_LORA), *([_ANY] * len(c_outs))),
        scratch_shapes=c_sems, input_output_aliases=alias, compiler_params=_params(("arbitrary",)),
    )(zp, zp, zp, zp, zp, zp, tabs, g_qa, g_kva, g_qn, g_kn, w_qb_t, w_kvb_t, *c_ins)


def attn_fwd(q, k, v, n_b, s_len, tq, name, comm=None):
    t = q.shape[0]
    n_h, n_hk = q.shape[1] // HP, k.shape[1] // HP
    grp = n_h // n_hk
    nq = s_len // tq
    sub = min(tq, 256)
    grid = (n_b, n_h, nq)
    c_ins, c_in_specs, c_outs, c_sems, alias = _comm_parts(comm, 3, 2)

    def body(*refs):
        (q_ref, k_ref, v_ref), cin, (o_ref, lse_ref), cout, _, csem = _split_refs(refs, 3, 2, 0, comm)
        _comm_start(comm, cin, cout, csem, grid)
        kv, vv = k_ref[...], v_ref[...]
        for r in range(tq // sub):
            rows = slice(r * sub, (r + 1) * sub)
            s = _dot_nt(q_ref[rows, :], kv)
            m = jnp.max(s, axis=-1, keepdims=True)
            p = jnp.exp(s - m)
            l = jnp.sum(p, axis=-1, keepdims=True)
            o_ref[rows, :] = (_dot_nn(p.astype(BF16), vv) * (1.0 / l)).astype(o_ref.dtype)
            lse_ref[0, :, rows] = jnp.broadcast_to(m + jnp.log(l), (sub, HP)).T[0:1, :]
        _comm_finish(comm, cin, cout, csem, grid)

    qspec = pl.BlockSpec((tq, HP), lambda b, h, i: (b * nq + i, h))
    kspec = pl.BlockSpec((s_len, HP), lambda b, h, i: (b, h // grp))
    lspec = pl.BlockSpec((1, 1, tq), lambda b, h, i: (b * n_h + h, 0, i))
    return pl.pallas_call(
        body, name=name, grid=grid,
        out_shape=(jax.ShapeDtypeStruct((t, n_h * HP), BF16), jax.ShapeDtypeStruct((n_b * n_h, 1, s_len), F32), *c_outs),
        in_specs=[qspec, kspec, kspec, *c_in_specs], out_specs=(qspec, lspec, *([_ANY] * len(c_outs))),
        scratch_shapes=c_sems, input_output_aliases=alias,
        compiler_params=_params(("arbitrary", "arbitrary", "arbitrary")),
    )(q, k, v, *c_ins)


def merge_fwd(oa, ob, zp, x, w_oa_t, w_ob_t, wpack, off, tm):
    t = x.shape[0]

    def body(oa_ref, ob_ref, ga_ref, gb_ref, x_ref, woa_ref, wob_ref, wo_ref, x1_o, mg_o, y_o):
        ya = _dot_nt(oa_ref[...], woa_ref[...])
        yb = _dot_nt(ob_ref[...], wob_ref[...])
        y_o[:, 0:D_MODEL] = ya.astype(BF16)
        y_o[:, D_MODEL:2 * D_MODEL] = yb.astype(BF16)
        merged = (jax.nn.sigmoid(ga_ref[...].astype(F32)) * ya + jax.nn.sigmoid(gb_ref[...].astype(F32)) * yb).astype(BF16)
        mg_o[...] = merged
        x1_o[...] = x_ref[...] + _dot_nn(merged, _wrows(wo_ref, 0, D_MODEL))

    return pl.pallas_call(
        body, name="merge_fwd", grid=(t // tm,),
        out_shape=(jax.ShapeDtypeStruct((t, D_MODEL), F32), jax.ShapeDtypeStruct((t, D_MODEL), BF16),
                   jax.ShapeDtypeStruct((t, 2 * D_MODEL), BF16)),
        in_specs=[_rows(tm, H_A * HP), _rows(tm, H_B * HP), _rows(tm, 1024, 0), _rows(tm, 1024, 1), _rows(tm, D_MODEL),
                  _resident((D_MODEL, H_A * HP)), _resident((D_MODEL, H_B * HP)), _packed_weight(128, off["w_o"])],
        out_specs=(_rows(tm, D_MODEL), _rows(tm, D_MODEL), _rows(tm, 2 * D_MODEL)), compiler_params=_params(("parallel",)),
    )(oa, ob, zp, zp, x, w_oa_t, w_ob_t, wpack)


def mlp_fwd(x1, g_mlp, wpack, off, tm):
    t = x1.shape[0]
    fc = 1024

    def body(x_ref, g_ref, wup_ref, wdn_ref, x2_o, u_o):
        xv = x_ref[...]
        h2 = (xv * _rstd(xv, D_MODEL) * g_ref[...]).astype(BF16)
        acc = xv
        for cidx in range(D_FF // fc):
            sl = slice(cidx * fc, (cidx + 1) * fc)
            u = jnp.maximum(_dot_nt(h2, _wrows(wup_ref, cidx * fc, fc)), 0.0)
            u_o[:, sl] = u.astype(BF16)
            acc = acc + _dot_nn((u * u).astype(BF16), _wrows(wdn_ref, cidx * fc, fc))
        x2_o[...] = acc

    return pl.pallas_call(
        body, name="mlp_fwd", grid=(t // tm,),
        out_shape=(jax.ShapeDtypeStruct((t, D_MODEL), F32), jax.ShapeDtypeStruct((t, D_FF), BF16)),
        in_specs=[_rows(tm, D_MODEL), _resident((1, D_MODEL)), _packed_weight(512, off["w_up"]), _packed_weight(512, off["w_down"])],
        out_specs=(_rows(tm, D_MODEL), _rows(tm, D_FF)), compiler_params=_params(("parallel",)),
    )(x1, g_mlp, wpack, wpack)


def ple_loss_bwd(x2, p, tgt, g_ple, g_final, wpack, off, w_ple_t, tm):
    t = x2.shape[0]
    inv_d = 1.0 / D_MODEL

    def body(x2_ref, p_ref, tg_ref, gp_ref, gf_ref, wpg_ref, wple_ref, dx2_o, dt_o, h3_o, dpe_o, st_o, dx2b_o):
        @pl.when(pl.program_id(0) == 0)
        def _():
            st_o[...] = jnp.zeros_like(st_o)

        x2v = x2_ref[...]
        gp, gf = gp_ref[...], gf_ref[...]
        w_pg = _wrows(wpg_ref, 0, D_MODEL)
        r2 = _rstd(x2v, D_MODEL)
        xh2 = x2v * r2
        h3 = (xh2 * gp).astype(BF16)
        h3_o[...] = h3
        gate = jax.nn.sigmoid(_dot_nn(h3, w_pg))
        pe = _dot_nt(p_ref[...].astype(BF16), wple_ref[...])
        x3 = x2v + gate * pe
        r3 = _rstd(x3, D_MODEL)
        xh3 = x3 * r3
        err = xh3 * gf - tg_ref[...]
        dy = err * inv_d
        dx3 = _rms_bwd(dy, xh3, r3, gf, D_MODEL)
        dpe_o[...] = (dx3 * gate).astype(BF16)
        dt = (dx3 * pe * gate * (1.0 - gate)).astype(BF16)
        dt_o[...] = dt
        dh3 = _dot_nt(dt, w_pg)
        dx2 = dx3 + _rms_bwd(dh3, xh2, r2, gp, D_MODEL)
        dx2_o[...] = dx2
        dx2b_o[...] = dx2.astype(BF16)
        st_o[0:1, :] += _colsum(dh3 * xh2)
        st_o[1:2, :] += _colsum(dy * xh3)
        st_o[2:3, :] += _colsum(err * err) * (0.5 * inv_d)

    bf = jax.ShapeDtypeStruct((t, D_MODEL), BF16)
    return pl.pallas_call(
        body, name="ple_loss_bwd", grid=(t // tm,),
        out_shape=(jax.ShapeDtypeStruct((t, D_MODEL), F32), bf, bf, bf, jax.ShapeDtypeStruct((3, D_MODEL), F32), bf),
        in_specs=[_rows(tm, D_MODEL), _rows(tm, PLE_DIM), _rows(tm, D_MODEL), _resident((1, D_MODEL)), _resident((1, D_MODEL)),
                  _packed_weight(128, off["w_ple_gate"]), _resident((D_MODEL, PLE_DIM))],
        out_specs=(_rows(tm, D_MODEL), _rows(tm, D_MODEL), _rows(tm, D_MODEL), _rows(tm, D_MODEL),
                   pl.BlockSpec((3, D_MODEL), lambda i: (0, 0)), _rows(tm, D_MODEL)),
        compiler_params=_params(("arbitrary",)),
    )(x2, p, tgt, g_ple, g_final, wpack, w_ple_t)


def mlp_bwd(dx2, x1, u, g_mlp, wpack, off, tm):
    t = x1.shape[0]
    fc = 1024

    def body(dx2_ref, x1_ref, u_ref, g_ref, wup_ref, wdn_ref, dx1_o, da_o, h2_o, st_o, dx1b_o):
        @pl.when(pl.program_id(0) == 0)
        def _():
            st_o[...] = jnp.zeros_like(st_o)

        d2 = dx2_ref[...]
        d2b = d2.astype(BF16)
        dh2 = jnp.zeros((tm, D_MODEL), F32)
        for cidx in range(D_FF // fc):
            sl = slice(cidx * fc, (cidx + 1) * fc)
            da = (_dot_nt(d2b, _wrows(wdn_ref, cidx * fc, fc)) * (2.0 * u_ref[:, sl].astype(F32))).astype(BF16)
            da_o[:, sl] = da
            dh2 = dh2 + _dot_nn(da, _wrows(wup_ref, cidx * fc, fc))
        xv = x1_ref[...]
        g = g_ref[...]
        r1 = _rstd(xv, D_MODEL)
        xh1 = xv * r1
        h2_o[...] = (xh1 * g).astype(BF16)
        st_o[...] += _colsum(dh2 * xh1)
        dx1 = d2 + _rms_bwd(dh2, xh1, r1, g, D_MODEL)
        dx1_o[...] = dx1
        dx1b_o[...] = dx1.astype(BF16)

    return pl.pallas_call(
        body, name="mlp_bwd", grid=(t // tm,),
        out_shape=(jax.ShapeDtypeStruct((t, D_MODEL), F32), jax.ShapeDtypeStruct((t, D_FF), BF16),
                   jax.ShapeDtypeStruct((t, D_MODEL), BF16), jax.ShapeDtypeStruct((1, D_MODEL), F32),
                   jax.ShapeDtypeStruct((t, D_MODEL), BF16)),
        in_specs=[_rows(tm, D_MODEL), _rows(tm, D_MODEL), _rows(tm, D_FF), _resident((1, D_MODEL)),
                  _packed_weight(512, off["w_up"]), _packed_weight(512, off["w_down"])],
        out_specs=(_rows(tm, D_MODEL), _rows(tm, D_FF), _rows(tm, D_MODEL), pl.BlockSpec((1, D_MODEL), lambda i: (0, 0)),
                   _rows(tm, D_MODEL)),
        compiler_params=_params(("arbitrary",)),
    )(dx2, x1, u, g_mlp, wpack, wpack)


def merge_bwd(dx1b, yab, zp, w_oa_t, w_ob_t, wpack, off, tm, comm=None):
    t = dx1b.shape[0]
    grid = (t // tm,)
    c_ins, c_in_specs, c_outs, c_sems, alias = _comm_parts(comm, 7, 5)

    def body(*refs):
        ((dx1_ref, y_ref, ga_ref, gb_ref, woa_ref, wob_ref, wo_ref), cin,
         (doa_o, dob_o, dg_o, dya_o, dyb_o), cout, _, csem) = _split_refs(refs, 7, 5, 0, comm)
        _comm_start(comm, cin, cout, csem, grid)
        dm = _dot_nt(dx1_ref[...], _wrows(wo_ref, 0, D_MODEL))
        for g_ref, w_ref, do_o, dy_o, col in ((ga_ref, woa_ref, doa_o, dya_o, 0), (gb_ref, wob_ref, dob_o, dyb_o, 1)):
            cols = slice(col * D_MODEL, (col + 1) * D_MODEL)
            sg = jax.nn.sigmoid(g_ref[...].astype(F32))
            dyv = (dm * sg).astype(BF16)
            dy_o[...] = dyv
            dg_o[:, cols] = (dm * y_ref[:, cols].astype(F32) * sg * (1.0 - sg)).astype(BF16)
            do_o[...] = _dot_nn(dyv, w_ref[...]).astype(BF16)
        _comm_finish(comm, cin, cout, csem, grid)

    bf = jax.ShapeDtypeStruct((t, D_MODEL), BF16)
    return pl.pallas_call(
        body, name="merge_bwd", grid=grid,
        out_shape=(bf, bf, jax.ShapeDtypeStruct((t, ZP), BF16), bf, bf, *c_outs),
        in_specs=[_rows(tm, D_MODEL), _rows(tm, 2 * D_MODEL), _rows(tm, 1024, 0), _rows(tm, 1024, 1),
                  _resident((D_MODEL, H_A * HP)), _resident((D_MODEL, H_B * HP)), _packed_weight(128, off["w_o"]), *c_in_specs],
        out_specs=(_rows(tm, D_MODEL), _rows(tm, D_MODEL), _rows(tm, 2 * D_MODEL), _rows(tm, D_MODEL), _rows(tm, D_MODEL),
                   *([_ANY] * len(c_outs))),
        scratch_shapes=c_sems, input_output_aliases=alias,
        compiler_params=_params(("arbitrary",)),
    )(dx1b, yab, zp, zp, w_oa_t, w_ob_t, wpack, *c_ins)


def attn_bwd(q, k, v, do, o, lse, n_b, s_len, tq, name, comm=None):
    t = q.shape[0]
    n_h, n_hk = q.shape[1] // HP, k.shape[1] // HP
    grp = n_h // n_hk
    nq = s_len // tq
    sub = min(tq, 256)
    grid = (n_b, n_hk, grp, nq)
    c_ins, c_in_specs, c_outs, c_sems, alias = _comm_parts(comm, 6, 3)

    def body(*refs):
        ((q_ref, k_ref, v_ref, do_ref, o_ref, lse_ref), cin, (dq_o, dk_o, dv_o), cout, (p_s, ds_s, dk_acc, dv_acc),
         csem) = _split_refs(refs, 6, 3, 4, comm)
        _comm_start(comm, cin, cout, csem, grid)

        @pl.when((pl.program_id(2) == 0) & (pl.program_id(3) == 0))
        def _():
            dk_acc[...] = jnp.zeros_like(dk_acc)
            dv_acc[...] = jnp.zeros_like(dv_acc)

        kv, vv = k_ref[...], v_ref[...]
        for r in range(tq // sub):
            rows = slice(r * sub, (r + 1) * sub)
            qv, dov = q_ref[rows, :], do_ref[rows, :]
            delta = jnp.sum(dov.astype(F32) * o_ref[rows, :].astype(F32), axis=-1, keepdims=True)
            delta_row = jnp.broadcast_to(delta, (sub, HP)).T[0:1, :]
            lse_row = lse_ref[0, :, rows]
            pt = jnp.exp(_dot_nt(kv, qv) - lse_row)
            dst = (pt * (_dot_nt(vv, dov) - delta_row)).astype(BF16)
            p_s[:, rows] = pt.astype(BF16)
            ds_s[:, rows] = dst
            dq_o[rows, :] = _dot_tn(dst, kv).astype(dq_o.dtype)
        dk_acc[...] += _dot_nn(ds_s[...], q_ref[...])
        dv_acc[...] += _dot_nn(p_s[...], do_ref[...])

        @pl.when((pl.program_id(2) == grp - 1) & (pl.program_id(3) == nq - 1))
        def _():
            dk_o[...] = dk_acc[...].astype(dk_o.dtype)
            dv_o[...] = dv_acc[...].astype(dv_o.dtype)

        _comm_finish(comm, cin, cout, csem, grid)

    qspec = pl.BlockSpec((tq, HP), lambda b, hk, g, i: (b * nq + i, hk * grp + g))
    kspec = pl.BlockSpec((s_len, HP), lambda b, hk, g, i: (b, hk))
    lspec = pl.BlockSpec((1, 1, tq), lambda b, hk, g, i: (b * n_h + hk * grp + g, 0, i))
    return pl.pallas_call(
        body, name=name, grid=grid,
        out_shape=(jax.ShapeDtypeStruct((t, n_h * HP), BF16), jax.ShapeDtypeStruct((t, n_hk * HP), BF16),
                   jax.ShapeDtypeStruct((t, n_hk * HP), BF16), *c_outs),
        in_specs=[qspec, kspec, kspec, qspec, qspec, lspec, *c_in_specs],
        out_specs=(qspec, kspec, kspec, *([_ANY] * len(c_outs))),
        scratch_shapes=[pltpu.VMEM((s_len, tq), BF16), pltpu.VMEM((s_len, tq), BF16),
                        pltpu.VMEM((s_len, HP), F32), pltpu.VMEM((s_len, HP), F32), *c_sems],
        input_output_aliases=alias,
        compiler_params=_params(("arbitrary", "arbitrary", "arbitrary", "arbitrary")),
    )(q, k, v, do, o, lse, *c_ins)


def prep_bwd(dqa, dka, dva, dqb, dkb, dvb, zp, dz, tabs, g_qa, g_kva, g_qn, g_kn, w_qb_t, w_kvb_t, tm, s_len):
    t = zp.shape[0]
    nsb = s_len // tm
    scale_a = (QK_NOPE + QK_ROPE) ** -0.5
    scale_b = HD_B ** -0.5

    def body(dqa_ref, dka_ref, dva_ref, dqb_ref, dkb_ref, dvb_ref, qb_ref, qlat_ref, kb_ref, ckv_ref, tab_ref,
             gqa_ref, gkva_ref, gqn_ref, gkn_ref, wqb_ref, wkvb_ref, _, dz_o, dqap_o, dkva_o, st_o):
        dzq_o, dsm_o = dz_o.at[:, 0:1024], dz_o.at[:, 1024:2048]

        @pl.when(pl.program_id(0) == 0)
        def _():
            st_o[...] = jnp.zeros_like(st_o)

        ca, s1a, s2a = tab_ref[0], tab_ref[1], tab_ref[2]
        ck = tab_ref[3]
        cb, s1b, s2b = tab_ref[4], tab_ref[5], tab_ref[6]
        for h in range(H_A):
            sl = slice(h * HP, (h + 1) * HP)
            dqap_o[:, sl] = _rope_bwd(dqa_ref[:, sl].astype(F32) * scale_a, ca, s1a, s2a).astype(BF16)
        dcq = _dot_nn(dqap_o[...], wqb_ref[...])
        ql = qlat_ref[...]
        rq = _rstd(ql, Q_LORA)
        xh = ql * rq
        gqa = gqa_ref[...]
        st_o[0:1, :] += _colsum(dcq * xh)
        dsm_o[:, 0:256] = _rms_bwd(dcq, xh, rq, gqa, Q_LORA).astype(BF16)
        dkpe = jnp.zeros((tm, HP), F32)
        for h in range(H_A):
            sl = slice(h * HP, (h + 1) * HP)
            dk = dka_ref[:, sl]
            dkpe = dkpe + dk.astype(F32)
            dkva_o[:, sl] = dk.astype(BF16)
        dkva_o[:, H_A * HP:] = dva_ref[...].astype(BF16)
        dsm_o[:, 896:1024] = _rope_bwd(dkpe, ck, s1a, s2a).astype(BF16)
        dckv = _dot_nn(dkva_o[...], wkvb_ref[...])
        cr = ckv_ref[...]
        rk = _rstd(cr, KV_LORA)
        xh = cr * rk
        st_o[1:2, 0:128] += _colsum(dckv * xh)
        dsm_o[:, 768:896] = _rms_bwd(dckv, xh, rk, gkva_ref[...], KV_LORA).astype(BF16)
        gqn, gkn = gqn_ref[...], gkn_ref[...]
        dgq = jnp.zeros((1, HP), F32)
        for h in range(H_B):
            sl = slice(h * HP, (h + 1) * HP)
            dy = _rope_bwd(dqb_ref[:, sl].astype(F32) * scale_b, cb, s1b, s2b)
            xs = qb_ref[:, sl]
            r = _rstd(xs, HD_B)
            xh = xs * r
            dgq = dgq + _colsum(dy * xh)
            dzq_o[:, sl] = _rms_bwd(dy, xh, r, gqn, HD_B).astype(BF16)
        st_o[2:3, 0:128] += dgq
        dgk = jnp.zeros((1, HP), F32)
        for h in range(KV_B):
            sl = slice(h * HP, (h + 1) * HP)
            dy = _rope_bwd(dkb_ref[:, sl].astype(F32), cb, s1b, s2b)
            xs = kb_ref[:, sl]
            r = _rstd(xs, HD_B)
            xh = xs * r
            dgk = dgk + _colsum(dy * xh)
            dsm_o[:, 256 + h * HP:256 + (h + 1) * HP] = _rms_bwd(dy, xh, r, gkn, HD_B).astype(BF16)
        st_o[3:4, 0:128] += dgk
        dsm_o[:, 512:768] = dvb_ref[...].astype(BF16)

    return pl.pallas_call(
        body, name="prep_bwd", grid=(t // tm,),
        out_shape=(jax.ShapeDtypeStruct((t, ZP), BF16), jax.ShapeDtypeStruct((t, 1024), BF16),
                   jax.ShapeDtypeStruct((t, 2048), BF16), jax.ShapeDtypeStruct((4, 256), F32)),
        in_specs=[_rows(tm, 1024), _rows(tm, 1024), _rows(tm, 1024), _rows(tm, 1024), _rows(tm, 256), _rows(tm, 256),
                  _rows(tm, 1024, 0), _rows(tm, 256, 4), _rows(tm, 256, 5), _rows(tm, 128, 14),
                  pl.BlockSpec((7, tm, HP), lambda i: (0, i % nsb, 0)),
                  _resident((1, Q_LORA)), _resident((1, KV_LORA)), _resident((1, HP)), _resident((1, HP)),
                  _resident((H_A * HP, Q_LORA)), _resident((2 * H_A * HP, KV_LORA)), _ANY],
        out_specs=(_rows(tm, 2048, 1), _rows(tm, 1024), _rows(tm, 2048), pl.BlockSpec((4, 256), lambda i: (0, 0))),
        input_output_aliases={17: 0}, compiler_params=_params(("arbitrary",)),
    )(dqa, dka, dva, dqb, dkb, dvb, zp, zp, zp, zp, tabs, g_qa, g_kva, g_qn, g_kn, w_qb_t, w_kvb_t, dz)


def in_bwd(dz, x, dx1, g_mix, w_in_t, tm, comm=None):
    t = x.shape[0]
    grid = (t // tm,)
    c_ins, c_in_specs, c_outs, c_sems, alias = _comm_parts(comm, 5, 2)

    def body(*refs):
        (dz_ref, x_ref, dx1_ref, g_ref, w_ref), cin, (dx_o, st_o), cout, _, csem = _split_refs(refs, 5, 2, 0, comm)
        _comm_start(comm, cin, cout, csem, grid)

        @pl.when(pl.program_id(0) == 0)
        def _():
            st_o[...] = jnp.zeros_like(st_o)

        dh = _dot_nn(dz_ref[...], w_ref[...])
        xv = x_ref[...]
        g = g_ref[...]
        r = _rstd(xv, D_MODEL)
        xh = xv * r
        st_o[...] += _colsum(dh * xh)
        dx_o[...] = dx1_ref[...] + _rms_bwd(dh, xh, r, g, D_MODEL)
        _comm_finish(comm, cin, cout, csem, grid)

    return pl.pallas_call(
        body, name="in_bwd", grid=grid,
        out_shape=(jax.ShapeDtypeStruct((t, D_MODEL), F32), jax.ShapeDtypeStruct((1, D_MODEL), F32), *c_outs),
        in_specs=[_rows(tm, ZP), _rows(tm, D_MODEL), _rows(tm, D_MODEL),
                  _resident((1, D_MODEL)), _resident((ZP, D_MODEL)), *c_in_specs],
        out_specs=(_rows(tm, D_MODEL), pl.BlockSpec((1, D_MODEL), lambda i: (0, 0)), *([_ANY] * len(c_outs))),
        scratch_shapes=c_sems, input_output_aliases=alias,
        compiler_params=_params(("arbitrary",)),
    )(dz, x, dx1, g_mix, w_in_t, *c_ins)


def matmul_tn(a, b, name, square_a=False):
    t, m = a.shape
    n = b.shape[1]
    bm = min(m, 512)
    tk = min(t, 4096)

    def body(a_ref, b_ref, o_ref):
        @pl.when(pl.program_id(1) == 0)
        def _():
            o_ref[...] = jnp.zeros_like(o_ref)

        av = a_ref[...]
        if square_a:
            av = (av.astype(F32) * av.astype(F32))
        o_ref[...] += _dot_tn(av.astype(BF16), b_ref[...].astype(BF16))

    return pl.pallas_call(
        body, name=name, grid=(m // bm, t // tk), out_shape=jax.ShapeDtypeStruct((m, n), F32),
        in_specs=[pl.BlockSpec((tk, bm), lambda i, kk: (kk, i)), pl.BlockSpec((tk, n), lambda i, kk: (kk, 0))],
        out_specs=pl.BlockSpec((bm, n), lambda i, kk: (i, 0)),
        compiler_params=_params(("parallel", "arbitrary")),
    )(a, b)


def matmul_tn_packed(a, b, name, rows, row_off, total_rows, buf=None, square_a=False):
    t, m = a.shape
    n = b.shape[1]
    pd = max(1, 512 // rows)
    bm = pd * rows
    tk = min(t, 4096)
    nk = t // tk

    def body(a_ref, b_ref, *rest):
        o_ref, acc = rest[-2], rest[-1]

        @pl.when(pl.program_id(1) == 0)
        def _():
            acc[...] = jnp.zeros_like(acc)

        av = a_ref[...]
        if square_a:
            av = (av.astype(F32) * av.astype(F32))
        acc[...] += _dot_tn(av.astype(BF16), b_ref[...].astype(BF16))

        @pl.when(pl.program_id(1) == nk - 1)
        def _():
            o_ref[...] = acc[...].reshape(pd, rows, n).astype(o_ref.dtype)

    in_specs = [pl.BlockSpec((tk, bm), lambda i, kk: (kk, i)), pl.BlockSpec((tk, n), lambda i, kk: (kk, 0))]
    args = [a, b]
    if buf is not None:
        in_specs.append(_ANY)
        args.append(buf)
    return pl.pallas_call(
        body, name=name, grid=(m // bm, nk), out_shape=jax.ShapeDtypeStruct((N_DEV, total_rows, n), BF16),
        in_specs=in_specs, out_specs=pl.BlockSpec((pd, rows, n), lambda i, kk: (i, row_off // rows, 0)),
        scratch_shapes=[pltpu.VMEM((bm, n), F32)], input_output_aliases={2: 0} if buf is not None else {},
        compiler_params=_params(("parallel", "arbitrary")),
    )(*args)


def adamw(w, g, m, v, name, g_transposed=False):
    _, r, c = w.shape
    tr = 256 if (not g_transposed and r > 256 and r % 256 == 0) else r
    c1 = 1.0 - ADAM_B1 ** ADAM_STEP
    c2 = 1.0 - ADAM_B2 ** ADAM_STEP

    def body(w_ref, g_ref, m_ref, v_ref, g_o, d_o, m_o, v_o):
        gv = g_ref[...].T if g_transposed else g_ref[...]
        mn = ADAM_B1 * m_ref[0] + (1.0 - ADAM_B1) * gv
        vn = ADAM_B2 * v_ref[0] + (1.0 - ADAM_B2) * (gv * gv)
        g_o[0] = gv
        m_o[0] = mn
        v_o[0] = vn
        d_o[0] = -ADAM_LR * ((mn / c1) / (jnp.sqrt(vn / c2) + ADAM_EPS) + ADAM_WD * w_ref[0])

    spec = pl.BlockSpec((1, tr, c), lambda i: (0, i, 0))
    gspec = pl.BlockSpec((c, r), lambda i: (0, 0)) if g_transposed else pl.BlockSpec((tr, c), lambda i: (i, 0))
    shp = jax.ShapeDtypeStruct((1, r, c), F32)
    return pl.pallas_call(
        body, name=name, grid=(r // tr,), out_shape=(shp,) * 4, in_specs=[spec, gspec, spec, spec], out_specs=(spec,) * 4,
        compiler_params=_params(("parallel",)),
    )(w, g, m, v)


def _rope_tables(s_len):
    def angles(pos, dim):
        inv = np.float32(ROPE_THETA) ** (-np.arange(0, dim, 2, dtype=np.float32) / np.float32(dim))
        return pos.astype(np.float32)[:, None] * inv[None, :]

    tpos = np.arange(s_len)
    a1 = angles(tpos, QK_ROPE)
    ar = angles(tpos // GRID_W, HD_B // 2)
    ac = angles(tpos % GRID_W, HD_B // 2)
    z16 = np.zeros((s_len, 16), np.float32)
    z32 = np.zeros((s_len, 32), np.float32)
    z64 = np.zeros((s_len, 64), np.float32)
    one64 = np.ones((s_len, 64), np.float32)
    c1, s1 = np.cos(a1), np.sin(a1)
    ca = np.concatenate([one64, c1, c1, z32], axis=1)
    ck = np.concatenate([z64, c1, c1, z32], axis=1)
    s1a = np.concatenate([z64, -s1, z16, z32], axis=1)
    s2a = np.concatenate([z64, z16, s1, z32], axis=1)
    cr, sr, cc, sc = np.cos(ar), np.sin(ar), np.cos(ac), np.sin(ac)
    cb = np.concatenate([cr, cr, cc, cc, z64], axis=1)
    s1b = np.concatenate([-sr, z16, -sc, z16, z64], axis=1)
    s2b = np.concatenate([z16, sr, z16, sc, z64], axis=1)
    return jnp.asarray(np.stack([ca, s1a, s2a, ck, cb, s1b, s2b]).astype(np.float32))


def _pad_heads(a, n_heads, axis):
    shp = a.shape
    a = a.reshape(shp[:axis] + (n_heads, shp[axis] // n_heads) + shp[axis + 1:])
    pad = [(0, 0)] * a.ndim
    pad[axis + 1] = (0, HP - a.shape[axis + 1])
    a = jnp.pad(a, pad)
    return a.reshape(shp[:axis] + (n_heads * HP,) + shp[axis + 1:])


def _unpad_heads(a, n_heads, width, axis):
    shp = a.shape
    a = a.reshape(shp[:axis] + (n_heads, HP) + shp[axis + 1:])
    a = lax.slice_in_dim(a, 0, width, axis=axis + 1)
    return a.reshape(shp[:axis] + (n_heads * width,) + shp[axis + 1:])


def _pack_rows(blocks, names):
    parts = []
    for name in names:
        b = blocks[name]
        padr = PACK_ROWS[name] - b.shape[-2]
        if padr:
            b = jnp.pad(b, [(0, 0)] * (b.ndim - 2) + [(0, padr), (0, 0)])
        parts.append(b)
    return jnp.concatenate(parts, axis=parts[0].ndim - 2)


def _expand_w_in(wt):
    z64 = jnp.zeros((64, D_MODEL), wt.dtype)
    z32 = jnp.zeros((32, D_MODEL), wt.dtype)
    return jnp.concatenate([
        wt[1184:2208], wt[2208:3232], _pad_heads(wt[416:928], H_B, 0), wt[0:256],
        _pad_heads(wt[928:1056], KV_B, 0), _pad_heads(wt[1056:1184], KV_B, 0), wt[256:384],
        z64, wt[384:416], z32], axis=0)


def _collapse_w_in(dw):
    dg, dq, ds = dw[0:2048], dw[2048:3072], dw[3072:4096]
    return jnp.concatenate([
        ds[0:256], ds[768:896], ds[960:992], _unpad_heads(dq, H_B, HD_B, 0), _unpad_heads(ds[256:512], KV_B, HD_B, 0),
        _unpad_heads(ds[512:768], KV_B, HD_B, 0), dg], axis=0)


def kernel(x, p, g_mix, w_in, g_qa, w_qb, g_kva, w_kvb, g_qn, g_kn, w_oa, w_ob, w_o, g_mlp, w_up, w_down, g_ple, w_ple_gate, w_ple, g_final, loss_target, m_g_mix, m_w_in, m_g_qa, m_w_qb, m_g_kva, m_w_kvb, m_g_qn, m_g_kn, m_w_oa, m_w_ob, m_w_o, m_g_mlp, m_w_up, m_w_down, m_g_ple, m_w_ple_gate, m_w_ple, m_g_final, v_g_mix, v_w_in, v_g_qa, v_w_qb, v_g_kva, v_w_kvb, v_g_qn, v_g_kn, v_w_oa, v_w_ob, v_w_o, v_g_mlp, v_w_up, v_w_down, v_g_ple, v_w_ple_gate, v_w_ple, v_g_final):
    n_b, s_len, _ = x.shape
    t = n_b * s_len
    tm = min(512, s_len)
    tq_f = min(2048, s_len)
    tq_b = min(2048, s_len)

    mats = dict(w_in=(w_in, m_w_in, v_w_in), w_qb=(w_qb, m_w_qb, v_w_qb), w_kvb=(w_kvb, m_w_kvb, v_w_kvb),
                w_oa=(w_oa, m_w_oa, v_w_oa), w_ob=(w_ob, m_w_ob, v_w_ob), w_o=(w_o, m_w_o, v_w_o),
                w_up=(w_up, m_w_up, v_w_up), w_down=(w_down, m_w_down, v_w_down),
                w_ple_gate=(w_ple_gate, m_w_ple_gate, v_w_ple_gate), w_ple=(w_ple, m_w_ple, v_w_ple))
    col_sharded = ("w_in", "w_qb", "w_kvb", "w_oa", "w_ob", "w_up", "w_ple")

    blocks = {}
    for name in PACK_W1 + ("w_oa", "w_ob", "w_ple"):
        blocks[name] = mats[name][0][0].T.reshape(-1, D_MODEL).astype(BF16)
    off_w1, _ = _pack_offsets(PACK_W1)
    off_w2, _ = _pack_offsets(PACK_W2)
    off_w3, _ = _pack_offsets(PACK_W3)
    xf = x.reshape(t, D_MODEL)
    h, full1 = norm_x(xf, g_mix, tm, comm=gather_first_comm(_pack_rows(blocks, PACK_W1)))
    pack2, pack3, full1 = pack_late_weights(w_up, w_down, w_o, w_ple_gate, _pack_rows(blocks, ("w_oa", "w_ob", "w_ple")),
                                            comm=gather_pass_comm(full1))

    def gathered(full, offs, name, rows, width):
        return full[:, offs[name]:offs[name] + rows].reshape(-1, width)

    w_in_t = _expand_w_in(gathered(full1, off_w1, "w_in", 404, D_MODEL))
    w_qb_t = _pad_heads(gathered(full1, off_w1, "w_qb", 24, Q_LORA), H_A, 0)
    wkvb = gathered(full1, off_w1, "w_kvb", 16, KV_LORA).reshape(H_A, 2, 64, KV_LORA)
    w_kvb_t = jnp.concatenate([_pad_heads(wkvb[:, 0].reshape(-1, KV_LORA), H_A, 0),
                               _pad_heads(wkvb[:, 1].reshape(-1, KV_LORA), H_A, 0)], axis=0)

    tabs = _rope_tables(s_len)
    g_qn_p = jnp.pad(g_qn, ((0, 0), (0, HP - HD_B)))
    g_kn_p = jnp.pad(g_kn, ((0, 0), (0, HP - HD_B)))
    pf = p.reshape(t, PLE_DIM)
    tgt = loss_target.reshape(t, D_MODEL)

    zg, zs, full2 = in_proj(h, w_in_t, tm, comm=gather_first_comm(pack2))
    qa, ka, va, qb, kb, vb, cq, ckv, full2 = attn_prep(zs, tabs, g_qa, g_kva, g_qn_p, g_kn_p, w_qb_t, w_kvb_t, tm, s_len,
                                                       comm=gather_pass_comm(full2))
    oa, lse_a, full3 = attn_fwd(qa, ka, va, n_b, s_len, tq_f, "attn_a_fwd", comm=gather_first_comm(pack3))
    ob, lse_b, full3 = attn_fwd(qb, kb, vb, n_b, s_len, tq_f, "attn_b_fwd", comm=gather_pass_comm(full3))
    w_oa_t = _pad_heads(gathered(full2, off_w2, "w_oa", 64, H_A * V_DIM_A), H_A, 1)
    w_ob_t = _pad_heads(gathered(full2, off_w2, "w_ob", 64, H_B * HD_B), H_B, 1)
    w_ple_t = gathered(full2, off_w2, "w_ple", 32, PLE_DIM)
    x1, merged, yab = merge_fwd(oa, ob, zg, xf, w_oa_t, w_ob_t, full2, off_w2, tm)
    x2, u = mlp_fwd(x1, g_mlp, full3, off_w3, tm)
    dx2, dt, h3, dpe, st_ple, dx2b = ple_loss_bwd(x2, pf, tgt, g_ple, g_final.reshape(1, D_MODEL), full2, off_w2, w_ple_t, tm)
    dx1, da, h2, st_mlp, dx1b = mlp_bwd(dx2, x1, u, g_mlp, full3, off_w3, tm)

    core = lax.axis_index("c").astype(jnp.int32).reshape(1)
    chip = (2 * lax.axis_index("x") + lax.axis_index("y")).astype(jnp.int32).reshape(1)

    def packed(gblocks, names):
        return _pack_rows({n: gblocks[n].reshape(N_DEV, -1, D_MODEL).astype(BF16) for n in names}, names)

    off_g1, rows_g1 = _pack_offsets(PACK_G1)
    gpack1 = matmul_tn_packed(da, h2, "gw_up", 512, off_g1["w_up"], rows_g1)
    gpack1 = matmul_tn_packed(u, dx2b, "gw_down", 512, off_g1["w_down"], rows_g1, buf=gpack1, square_a=True)
    gpack1 = matmul_tn_packed(h3, dt, "gw_pg", 128, off_g1["w_ple_gate"], rows_g1, buf=gpack1)
    gple = matmul_tn(dpe, pf, "gw_ple").reshape(N_DEV, -1, D_MODEL).astype(BF16)
    gpack1 = lax.dynamic_update_slice(gpack1, gple, (0, off_g1["w_ple"], 0))
    doa, dob, dz, dya, dyb, got1 = merge_bwd(dx1b, yab, zg, w_oa_t, w_ob_t, full2, off_w2, tm,
                                               comm=scatter_sibling_comm(gpack1))
    part1 = add_pairs(gpack1, got1, core)

    off_g2, rows_g2 = _pack_offsets(PACK_G2)
    g2 = dict(w_oa=_unpad_heads(matmul_tn(dya, oa, "gw_oa"), H_A, V_DIM_A, 1),
              w_ob=_unpad_heads(matmul_tn(dyb, ob, "gw_ob"), H_B, HD_B, 1))
    gpack2 = matmul_tn_packed(merged, dx1b, "gw_o", 128, off_g2["w_o"], rows_g2)
    gpack2 = lax.dynamic_update_slice(gpack2, packed(g2, ("w_oa", "w_ob")), (0, off_g2["w_oa"], 0))
    dqa, dka, dva, land1, got2 = attn_bwd(qa, ka, va, doa, oa, lse_a, n_b, s_len, tq_b, "attn_a_bwd",
                                          comm=_join_comms(scatter_chips_comm(part1), scatter_sibling_comm(gpack2)))
    gshard1 = sum_chips(part1, land1, chip)
    part2 = add_pairs(gpack2, got2, core)
    dqb, dkb, dvb, land2 = attn_bwd(qb, kb, vb, dob, ob, lse_b, n_b, s_len, tq_b, "attn_b_bwd", comm=scatter_chips_comm(part2))
    gshard2 = sum_chips(part2, land2, chip)
    dz, dqap, dkva, st_prep = prep_bwd(dqa, dka, dva, dqb, dkb, dvb, zs, dz, tabs, g_qa, g_kva, g_qn_p, g_kn_p,
                                       w_qb_t, w_kvb_t, tm, s_len)

    gkv = matmul_tn(dkva, ckv, "gw_kvb")
    g3 = dict(
        w_in=_collapse_w_in(matmul_tn(dz, h, "gw_in")),
        w_qb=_unpad_heads(matmul_tn(dqap, cq, "gw_qb"), H_A, QK_NOPE + QK_ROPE, 0),
        w_kvb=jnp.stack([_unpad_heads(gkv[:H_A * HP], H_A, 64, 0).reshape(H_A, 64, KV_LORA),
                         _unpad_heads(gkv[H_A * HP:], H_A, 64, 0).reshape(H_A, 64, KV_LORA)], axis=1))
    gpack3 = packed(g3, PACK_G3)
    part3 = add_pairs(gpack3, exchange_sibling(gpack3), core)
    grad_x, st_mix, land3 = in_bwd(dz, xf, dx1, g_mix, w_in_t, tm, comm=scatter_chips_comm(part3))
    gshard3 = sum_chips(part3, land3, chip)
    off_g3, _ = _pack_offsets(PACK_G3)
    shards = {n: (gshard1, off_g1[n]) for n in PACK_G1}
    shards.update({n: (gshard2, off_g2[n]) for n in PACK_G2})
    shards.update({n: (gshard3, off_g3[n]) for n in PACK_G3})

    stats = allreduce_stats(st_mix, st_prep, st_mlp, st_ple)
    loss = jnp.sum(stats[ST_LOSS])

    out_g, out_d, out_m, out_v = {}, {}, {}, {}
    for name, (w, m, v) in mats.items():
        gshard, off = shards[name]
        r, c = w.shape[1:]
        if name in col_sharded:
            g2 = gshard[off:off + (r * c) // D_MODEL].reshape(c, r)
            if r % 128 == 0 and c % 128 == 0:
                res = adamw(w, g2, m, v, "adamw_" + name, g_transposed=True)
            else:
                res = adamw(w[0].T[None], g2, m[0].T[None], v[0].T[None], "adamw_" + name)
                res = tuple(a[0].T[None] for a in res)
        else:
            res = adamw(w, gshard[off:off + r], m, v, "adamw_" + name)
        out_g[name], out_d[name], out_m[name], out_v[name] = res

    gains = (("g_mix", g_mix, m_g_mix, v_g_mix, ST_G_MIX), ("g_qa", g_qa, m_g_qa, v_g_qa, ST_G_QA),
             ("g_kva", g_kva, m_g_kva, v_g_kva, ST_G_KVA), ("g_qn", g_qn, m_g_qn, v_g_qn, ST_G_QN),
             ("g_kn", g_kn, m_g_kn, v_g_kn, ST_G_KN), ("g_mlp", g_mlp, m_g_mlp, v_g_mlp, ST_G_MLP),
             ("g_ple", g_ple, m_g_ple, v_g_ple, ST_G_PLE), ("g_final", g_final, m_g_final, v_g_final, ST_G_FINAL))
    res = adamw_gains(stats, [(r_, w.reshape(1, -1), m.reshape(1, -1), v.reshape(1, -1)) for _, w, m, v, r_ in gains])
    for (name, w, _, _, _), (gg, gd, gm, gv) in zip(gains, res):
        out_g[name], out_d[name], out_m[name], out_v[name] = (a.reshape(w.shape) for a in (gg, gd, gm, gv))

    order = ("g_mix", "w_in", "g_qa", "w_qb", "g_kva", "w_kvb", "g_qn", "g_kn", "w_oa", "w_ob", "w_o", "g_mlp",
             "w_up", "w_down", "g_ple", "w_ple_gate", "w_ple", "g_final")
    return (loss, grad_x.reshape(x.shape), *[out_g[n] for n in order], *[out_d[n] for n in order],
            *[out_m[n] for n in order], *[out_v[n] for n in order])
```

```python
import numpy as np
import jax
import jax.numpy as jnp
from jax import lax
from jax.experimental import pallas as pl
from jax.experimental.pallas import tpu as pltpu

F32 = jnp.float32
BF16 = jnp.bfloat16

D_MODEL = 1024
EPS = 1e-6
ROPE_THETA = 10000.0
GRID_W = 64
H_A = 8
QK_NOPE = 64
QK_ROPE = 32
V_DIM_A = 64
Q_LORA = 256
KV_LORA = 128
H_B = 8
KV_B = 2
HD_B = 64
D_FF = 4 * D_MODEL
PLE_DIM = 256
HP = 128
ZP = 4096
N_DEV = 8
N_CHIP = 4

ADAM_LR = 0.001
ADAM_B1 = 0.9
ADAM_B2 = 0.999
ADAM_EPS = 1e-08
ADAM_WD = 0.01
ADAM_STEP = 10

VMEM_LIMIT = 52 * 1024 * 1024

PACK_ROWS = dict(w_in=416, w_qb=32, w_kvb=16, w_oa=64, w_ob=64, w_o=128, w_up=512, w_down=512, w_ple_gate=128, w_ple=32)
PACK_W1 = ("w_in", "w_qb", "w_kvb")
PACK_W2 = ("w_o", "w_ple_gate", "w_oa", "w_ob", "w_ple")
PACK_W3 = ("w_up", "w_down")
PACK_G1 = ("w_up", "w_down", "w_ple_gate", "w_ple")
PACK_G2 = ("w_o", "w_oa", "w_ob")
PACK_G3 = ("w_in", "w_qb", "w_kvb")


def _pack_offsets(names):
    off, o = {}, 0
    for n in names:
        off[n] = o
        o += PACK_ROWS[n]
    return off, o

ST_G_MIX, ST_G_QA, ST_G_KVA, ST_G_QN, ST_G_KN, ST_G_MLP, ST_G_PLE, ST_G_FINAL, ST_LOSS = range(9)
ST_ROWS = 16


def _dot_nn(a, b):
    return lax.dot_general(a, b, (((1,), (0,)), ((), ())), preferred_element_type=F32)


def _dot_nt(a, b):
    return lax.dot_general(a, b, (((1,), (1,)), ((), ())), preferred_element_type=F32)


def _dot_tn(a, b):
    return lax.dot_general(a, b, (((0,), (0,)), ((), ())), preferred_element_type=F32)


def _rstd(x, n):
    return lax.rsqrt(jnp.sum(x * x, axis=-1, keepdims=True) * (1.0 / n) + EPS)


def _rms_bwd(dy, xh, r, g, n):
    dxh = dy * g
    return r * (dxh - xh * (jnp.sum(dxh * xh, axis=-1, keepdims=True) * (1.0 / n)))


def _rope_fwd(x, c, s1, s2):
    return x * c + pltpu.roll(x, HP - 16, 1) * s1 + pltpu.roll(x, 16, 1) * s2


def _rope_bwd(d, c, s1, s2):
    return d * c + pltpu.roll(d * s1, 16, 1) + pltpu.roll(d * s2, HP - 16, 1)


def _colsum(v):
    return jnp.sum(v, axis=0, keepdims=True)


def _params(sem=None, vmem=VMEM_LIMIT):
    return pltpu.CompilerParams(dimension_semantics=sem, vmem_limit_bytes=vmem)


def _resident(shape):
    nd = len(shape)
    return pl.BlockSpec(shape, lambda *_: (0,) * nd, pipeline_mode=pl.Buffered(1))


def _rows(tm, width, col=0):
    return pl.BlockSpec((tm, width), lambda i: (i, col))


def _packed_weight(rows, off):
    return pl.BlockSpec((N_DEV, rows, D_MODEL), lambda *_: (0, off // rows, 0), pipeline_mode=pl.Buffered(1))


def _wrows(ref, start, size):
    rows = ref.shape[1]
    return ref[start // rows:(start + size) // rows].reshape(size, D_MODEL)


def _mesh_pos():
    return lax.axis_index("x"), lax.axis_index("y"), lax.axis_index("c")


def _flip(v, bit):
    return (1 - v) if bit else v


_ANY = pl.BlockSpec(memory_space=pl.ANY)
_MESH = pl.DeviceIdType.MESH


def _remote(src, dst, send_sems, recv_sems, k, to):
    return pltpu.make_async_remote_copy(src_ref=src, dst_ref=dst, send_sem=send_sems.at[k], recv_sem=recv_sems.at[k],
                                        device_id=to, device_id_type=_MESH)


def _sibling_copies(g_ref, got_ref, send_sems, recv_sems):
    x, y, c = _mesh_pos()
    return [_remote(g_ref.at[2 * j + (1 - c)], got_ref.at[j], send_sems, recv_sems, j, (x, y, 1 - c)) for j in range(N_CHIP)]


def _chip_copies(p_ref, land_ref, send_sems, recv_sems):
    x, y, c = _mesh_pos()
    copies = []
    for k in (1, 2, 3):
        tx, ty = _flip(x, k & 2), _flip(y, k & 1)
        copies.append(_remote(p_ref.at[2 * tx + ty], land_ref.at[k - 1], send_sems, recv_sems, k - 1, (tx, ty, c)))
    return copies


class _Comm:
    def __init__(self, ins, out_shapes, sems, make, aliases=None):
        self.ins, self.out_shapes, self.sems, self.make, self.aliases = list(ins), list(out_shapes), list(sems), make, aliases or {}


def _join_comms(a, b):
    n_i, n_o, n_s = len(a.ins), len(a.out_shapes), len(a.sems)

    def make(cin, cout, sems):
        return a.make(cin[:n_i], cout[:n_o], sems[:n_s]) + b.make(cin[n_i:], cout[n_o:], sems[n_s:])

    aliases = dict(a.aliases)
    aliases.update({n_i + j: n_o + k for j, k in b.aliases.items()})
    return _Comm(a.ins + b.ins, a.out_shapes + b.out_shapes, a.sems + b.sems, make, aliases)


def _comm_parts(comm, n_in, n_out):
    if comm is None:
        return [], [], [], [], {}
    alias = {n_in + j: n_out + k for j, k in comm.aliases.items()}
    return comm.ins, [_ANY] * len(comm.ins), comm.out_shapes, comm.sems, alias


def _split_refs(refs, n_in, n_out, n_scratch, comm):
    n_ci = len(comm.ins) if comm else 0
    n_co = len(comm.out_shapes) if comm else 0
    cuts, i = [], 0
    for n in (n_in, n_ci, n_out, n_co, n_scratch):
        cuts.append(refs[i:i + n])
        i += n
    return (*cuts, refs[i:])


def _grid_edge(grid, last):
    cond = None
    for d, n in enumerate(grid):
        here = pl.program_id(d) == (n - 1 if last else 0)
        cond = here if cond is None else cond & here
    return cond


def _comm_start(comm, cin, cout, csem, grid):
    if comm is not None:
        @pl.when(_grid_edge(grid, False))
        def _():
            for cp in comm.make(cin, cout, csem):
                cp.start()


def _comm_finish(comm, cin, cout, csem, grid):
    if comm is not None:
        @pl.when(_grid_edge(grid, True))
        def _():
            for cp in comm.make(cin, cout, csem):
                cp.wait()


def gather_first_comm(shard):
    r, w = shard.shape

    def make(cin, cout, sems):
        (x_ref,), (out_ref,), (send_sems, recv_sems, local_sem) = cin, cout, sems
        x, y, c = _mesh_pos()
        mine = out_ref.at[4 * x + 2 * y + c]
        targets = [(x, y, 1 - c), (1 - x, y, c), (x, 1 - y, c), (1 - x, 1 - y, c)]
        return [_remote(x_ref, mine, send_sems, recv_sems, k, to) for k, to in enumerate(targets)] + [
            pltpu.make_async_copy(x_ref, mine, local_sem)]

    return _Comm([shard], [jax.ShapeDtypeStruct((N_DEV, r, w), shard.dtype)],
                 [pltpu.SemaphoreType.DMA((4,)), pltpu.SemaphoreType.DMA((4,)), pltpu.SemaphoreType.DMA], make)


def gather_pass_comm(full):
    def make(cin, cout, sems):
        (in_ref,), (out_ref,), (send_sems, recv_sems) = cin, cout, sems
        x, y, c = _mesh_pos()
        copies = []
        for k, (px, py) in enumerate([(1 - x, y), (x, 1 - y), (1 - x, 1 - y)]):
            idx = 4 * px + 2 * py + c
            copies.append(_remote(in_ref.at[idx], out_ref.at[idx], send_sems, recv_sems, k, (x, y, 1 - c)))
        return copies

    return _Comm([full], [jax.ShapeDtypeStruct(full.shape, full.dtype)],
                 [pltpu.SemaphoreType.DMA((3,)), pltpu.SemaphoreType.DMA((3,))], make, aliases={0: 0})


def gather3_first_comm(shard):
    r, w = shard.shape

    def make(cin, cout, sems):
        (x_ref,), (out_ref,), (send_sems, recv_sems, local_sem) = cin, cout, sems
        x, y, c = _mesh_pos()
        mine = out_ref.at[4 * x + 2 * y + c]
        targets = [(x, y, 1 - c), (1 - x, y, c), (x, 1 - y, c)]
        return [_remote(x_ref, mine, send_sems, recv_sems, k, to) for k, to in enumerate(targets)] + [
            pltpu.make_async_copy(x_ref, mine, local_sem)]

    return _Comm([shard], [jax.ShapeDtypeStruct((N_DEV, r, w), shard.dtype)],
                 [pltpu.SemaphoreType.DMA((3,)), pltpu.SemaphoreType.DMA((3,)), pltpu.SemaphoreType.DMA], make)


def gather3_second_comm(full):
    def make(cin, cout, sems):
        (in_ref,), (out_ref,), (send_sems, recv_sems) = cin, cout, sems
        x, y, c = _mesh_pos()
        copies = []
        for k, (px, py) in enumerate([(1 - x, y), (x, 1 - y)]):
            idx = 4 * px + 2 * py + c
            copies.append(_remote(in_ref.at[idx], out_ref.at[idx], send_sems, recv_sems, k, (x, y, 1 - c)))
        fx, fy = x * c + (1 - x) * (1 - c), y * (1 - c) + (1 - y) * c
        tx, ty = x * (1 - c) + (1 - x) * c, (1 - y) * (1 - c) + y * c
        idx = 4 * fx + 2 * fy + c
        copies.append(_remote(in_ref.at[idx], out_ref.at[idx], send_sems, recv_sems, 2, (tx, ty, c)))
        return copies

    return _Comm([full], [jax.ShapeDtypeStruct(full.shape, full.dtype)],
                 [pltpu.SemaphoreType.DMA((3,)), pltpu.SemaphoreType.DMA((3,))], make, aliases={0: 0})


def gather3_third_comm(full):
    def make(cin, cout, sems):
        (in_ref,), (out_ref,), (send_sems, recv_sems) = cin, cout, sems
        x, y, c = _mesh_pos()
        idx = 4 * (1 - x) + 2 * (1 - y) + c
        return [_remote(in_ref.at[idx], out_ref.at[idx], send_sems, recv_sems, 0, (x, y, 1 - c))]

    return _Comm([full], [jax.ShapeDtypeStruct(full.shape, full.dtype)],
                 [pltpu.SemaphoreType.DMA((1,)), pltpu.SemaphoreType.DMA((1,))], make, aliases={0: 0})


def run_comm(comm, name):
    c_ins, c_in_specs, c_outs, c_sems, alias = _comm_parts(comm, 0, 0)

    def body(*refs):
        _, cin, _, cout, _, csem = _split_refs(refs, 0, 0, 0, comm)
        for cp in comm.make(cin, cout, csem):
            cp.start()
        for cp in comm.make(cin, cout, csem):
            cp.wait()

    return pl.pallas_call(body, name=name, out_shape=tuple(c_outs), in_specs=c_in_specs, out_specs=tuple([_ANY] * len(c_outs)),
                          scratch_shapes=c_sems, input_output_aliases=alias)(*c_ins)


def scatter_sibling_comm(g):
    _, r, w = g.shape
    return _Comm([g], [jax.ShapeDtypeStruct((N_CHIP, r, w), g.dtype)],
                 [pltpu.SemaphoreType.DMA((N_CHIP,)), pltpu.SemaphoreType.DMA((N_CHIP,))],
                 lambda cin, cout, sems: _sibling_copies(cin[0], cout[0], sems[0], sems[1]))


def scatter_chips_comm(part):
    _, r, w = part.shape
    return _Comm([part], [jax.ShapeDtypeStruct((N_CHIP - 1, r, w), part.dtype)],
                 [pltpu.SemaphoreType.DMA((3,)), pltpu.SemaphoreType.DMA((3,))],
                 lambda cin, cout, sems: _chip_copies(cin[0], cout[0], sems[0], sems[1]))


def exchange_sibling(g):
    _, r, w = g.shape

    def body(g_ref, got_ref, send_sems, recv_sems):
        copies = _sibling_copies(g_ref, got_ref, send_sems, recv_sems)
        for cp in copies:
            cp.start()
        for cp in copies:
            cp.wait()

    return pl.pallas_call(
        body, name="exchange_sibling", out_shape=jax.ShapeDtypeStruct((N_CHIP, r, w), g.dtype),
        in_specs=[_ANY], out_specs=_ANY,
        scratch_shapes=[pltpu.SemaphoreType.DMA((N_CHIP,)), pltpu.SemaphoreType.DMA((N_CHIP,))],
    )(g)


def exchange_chips(part):
    _, r, w = part.shape

    def body(p_ref, land_ref, send_sems, recv_sems):
        copies = _chip_copies(p_ref, land_ref, send_sems, recv_sems)
        for cp in copies:
            cp.start()
        for cp in copies:
            cp.wait()

    return pl.pallas_call(
        body, name="exchange_chips", out_shape=jax.ShapeDtypeStruct((N_CHIP - 1, r, w), part.dtype),
        in_specs=[_ANY], out_specs=_ANY,
        scratch_shapes=[pltpu.SemaphoreType.DMA((3,)), pltpu.SemaphoreType.DMA((3,))],
    )(part)


def allreduce_stats(st_mix, st_prep, st_mlp, st_ple):
    def body(mix_ref, prep_ref, mlp_ref, ple_ref, out_ref, mine, gath, send_sems, recv_sems):
        x, y, c = _mesh_pos()
        me = 4 * x + 2 * y + c
        mine[...] = jnp.zeros_like(mine)
        mine[ST_G_MIX:ST_G_MIX + 1, :] = mix_ref[...]
        mine[ST_G_QA:ST_G_KN + 1, 0:256] = prep_ref[...]
        mine[ST_G_MLP:ST_G_MLP + 1, :] = mlp_ref[...]
        mine[ST_G_PLE:ST_LOSS + 1, :] = ple_ref[...]
        gath[me] = mine[...]
        copies = []
        for k in range(1, N_DEV):
            peer = (_flip(x, k & 4), _flip(y, k & 2), _flip(c, k & 1))
            copies.append(_remote(mine, gath.at[me], send_sems, recv_sems, k - 1, peer))
        for cp in copies:
            cp.start()
        for cp in copies:
            cp.wait()
        acc = gath[0]
        for d in range(1, N_DEV):
            acc = acc + gath[d]
        out_ref[...] = acc

    vm = pl.BlockSpec(memory_space=pltpu.VMEM)
    return pl.pallas_call(
        body, name="allreduce_stats", out_shape=jax.ShapeDtypeStruct((ST_ROWS, D_MODEL), F32),
        in_specs=[vm] * 4, out_specs=vm,
        scratch_shapes=[pltpu.VMEM((ST_ROWS, D_MODEL), F32), pltpu.VMEM((N_DEV, ST_ROWS, D_MODEL), F32),
                        pltpu.SemaphoreType.DMA((N_DEV - 1,)), pltpu.SemaphoreType.DMA((N_DEV - 1,))],
    )(st_mix, st_prep, st_mlp, st_ple)


def adamw_gains(stats, gains):
    c1 = 1.0 - ADAM_B1 ** ADAM_STEP
    c2 = 1.0 - ADAM_B2 ** ADAM_STEP
    n = len(gains)

    def body(st_ref, *refs):
        ins, outs = refs[:3 * n], refs[3 * n:]
        for i, (row, w, _, _) in enumerate(gains):
            width = w.shape[1]
            gv = st_ref[row:row + 1, 0:width]
            mn = ADAM_B1 * ins[3 * i + 1][...] + (1.0 - ADAM_B1) * gv
            vn = ADAM_B2 * ins[3 * i + 2][...] + (1.0 - ADAM_B2) * (gv * gv)
            outs[4 * i][...] = gv
            outs[4 * i + 1][...] = -ADAM_LR * ((mn / c1) / (jnp.sqrt(vn / c2) + ADAM_EPS) + ADAM_WD * ins[3 * i][...])
            outs[4 * i + 2][...] = mn
            outs[4 * i + 3][...] = vn

    vm = pl.BlockSpec(memory_space=pltpu.VMEM)
    flat = [a for (_, w, m, v) in gains for a in (w, m, v)]
    out_shape = tuple(jax.ShapeDtypeStruct(w.shape, F32) for (_, w, _, _) in gains for _ in range(4))
    res = pl.pallas_call(body, name="adamw_gains", out_shape=out_shape, in_specs=[vm] * (1 + 3 * n),
                         out_specs=tuple([vm] * (4 * n)))(stats, *flat)
    return [res[4 * i:4 * i + 4] for i in range(n)]


def _row_tile(r, cap=640):
    return max(d for d in range(16, min(r, cap) + 1, 16) if r % d == 0)


def add_pairs(g, got, core):
    n, r, w = got.shape
    tr = _row_tile(r)

    def body(c_ref, a_ref, b_ref, o_ref):
        o_ref[...] = (a_ref[...].astype(F32) + b_ref[...].astype(F32)).astype(o_ref.dtype)

    spec = pl.BlockSpec((1, tr, w), lambda i, j, c: (i, j, 0))
    return pl.pallas_call(
        body, name="add_pairs", out_shape=jax.ShapeDtypeStruct(got.shape, got.dtype),
        grid_spec=pltpu.PrefetchScalarGridSpec(
            num_scalar_prefetch=1, grid=(n, r // tr),
            in_specs=[pl.BlockSpec((1, tr, w), lambda i, j, c: (2 * i + c[0], j, 0)), spec], out_specs=spec),
        compiler_params=_params(("parallel", "parallel")),
    )(core, g, got)


def sum_chips(part, land, chip):
    _, r, w = part.shape
    tr = _row_tile(r)

    def body(c_ref, p_ref, l_ref, o_ref):
        acc = p_ref[0].astype(F32)
        for s in range(N_CHIP - 1):
            acc = acc + l_ref[s].astype(F32)
        o_ref[...] = acc

    return pl.pallas_call(
        body, name="sum_chips", out_shape=jax.ShapeDtypeStruct((r, w), F32),
        grid_spec=pltpu.PrefetchScalarGridSpec(
            num_scalar_prefetch=1, grid=(r // tr,),
            in_specs=[pl.BlockSpec((1, tr, w), lambda i, c: (c[0], i, 0)), pl.BlockSpec((N_CHIP - 1, tr, w), lambda i, c: (0, i, 0))],
            out_specs=pl.BlockSpec((tr, w), lambda i, c: (i, 0))),
        compiler_params=_params(("parallel",)),
    )(chip, part, land)


def norm_x(x, g_mix, tm, comm=None):
    t = x.shape[0]
    grid = (t // tm,)
    c_ins, c_in_specs, c_outs, c_sems, alias = _comm_parts(comm, 2, 1)

    def body(*refs):
        (x_ref, g_ref), cin, (h_ref,), cout, _, csem = _split_refs(refs, 2, 1, 0, comm)
        _comm_start(comm, cin, cout, csem, grid)
        xv = x_ref[...]
        h_ref[...] = (xv * _rstd(xv, D_MODEL) * g_ref[...]).astype(BF16)
        _comm_finish(comm, cin, cout, csem, grid)

    return pl.pallas_call(
        body, name="norm_x", grid=grid, out_shape=(jax.ShapeDtypeStruct((t, D_MODEL), BF16), *c_outs),
        in_specs=[_rows(tm, D_MODEL), _resident((1, D_MODEL)), *c_in_specs],
        out_specs=(_rows(tm, D_MODEL), *([_ANY] * len(c_outs))),
        scratch_shapes=c_sems, input_output_aliases=alias, compiler_params=_params(("arbitrary",)),
    )(x, g_mix, *c_ins)


def pack_late_weights(w_up, w_down, w_o, w_pg, small, comm=None):
    rows2 = sum(PACK_ROWS[n] for n in PACK_W2)
    rows3 = sum(PACK_ROWS[n] for n in PACK_W3)
    c_ins, c_in_specs, c_outs, c_sems, alias = _comm_parts(comm, 5, 2)
    grid = (1,)

    def body(*refs):
        (up_ref, dn_ref, o_ref, pg_ref, sm_ref), cin, (p2_ref, p3_ref), cout, _, csem = _split_refs(refs, 5, 2, 0, comm)
        _comm_start(comm, cin, cout, csem, grid)
        p2_ref[0:128, :] = o_ref[0].astype(BF16)
        p2_ref[128:256, :] = pg_ref[0].astype(BF16)
        p2_ref[256:rows2, :] = sm_ref[...]
        p3_ref[0:512, :] = up_ref[0].T.astype(BF16)
        p3_ref[512:1024, :] = dn_ref[0].astype(BF16)
        _comm_finish(comm, cin, cout, csem, grid)

    def whole(a):
        nd = a.ndim
        return pl.BlockSpec(a.shape, lambda i: (0,) * nd)

    args = (w_up, w_down, w_o, w_pg, small)
    return pl.pallas_call(
        body, name="pack_late_weights", grid=grid,
        out_shape=(jax.ShapeDtypeStruct((rows2, D_MODEL), BF16), jax.ShapeDtypeStruct((rows3, D_MODEL), BF16), *c_outs),
        in_specs=[*[whole(a) for a in args], *c_in_specs],
        out_specs=(pl.BlockSpec((rows2, D_MODEL), lambda i: (0, 0)), pl.BlockSpec((rows3, D_MODEL), lambda i: (0, 0)),
                   *([_ANY] * len(c_outs))),
        scratch_shapes=c_sems, input_output_aliases=alias, compiler_params=_params(("arbitrary",)),
    )(*args, *c_ins)


def in_proj(h, w_in_t, tm, comm=None):
    t = h.shape[0]
    nc = 512
    half = ZP // 2
    grid = (t // tm,)
    c_ins, c_in_specs, c_outs, c_sems, alias = _comm_parts(comm, 2, 2)

    def body(*refs):
        (h_ref, w_ref), cin, (zg_ref, zs_ref), cout, _, csem = _split_refs(refs, 2, 2, 0, comm)
        _comm_start(comm, cin, cout, csem, grid)
        hv = h_ref[...]
        for cidx in range(half // nc):
            zg_ref[:, cidx * nc:(cidx + 1) * nc] = _dot_nt(hv, w_ref[cidx * nc:(cidx + 1) * nc, :]).astype(BF16)
        for cidx in range(half // nc):
            zs_ref[:, cidx * nc:(cidx + 1) * nc] = _dot_nt(hv, w_ref[half + cidx * nc:half + (cidx + 1) * nc, :])
        _comm_finish(comm, cin, cout, csem, grid)

    return pl.pallas_call(
        body, name="in_proj", grid=grid,
        out_shape=(jax.ShapeDtypeStruct((t, half), BF16), jax.ShapeDtypeStruct((t, half), F32), *c_outs),
        in_specs=[_rows(tm, D_MODEL), _resident((ZP, D_MODEL)), *c_in_specs],
        out_specs=(_rows(tm, half), _rows(tm, half), *([_ANY] * len(c_outs))),
        scratch_shapes=c_sems, input_output_aliases=alias, compiler_params=_params(("arbitrary",)),
    )(h, w_in_t, *c_ins)


def attn_prep(zp, tabs, g_qa, g_kva, g_qn, g_kn, w_qb_t, w_kvb_t, tm, s_len, comm=None):
    t = zp.shape[0]
    nsb = s_len // tm
    scale_a = (QK_NOPE + QK_ROPE) ** -0.5
    scale_b = HD_B ** -0.5

    grid = (t // tm,)
    c_ins, c_in_specs, c_outs, c_sems, alias = _comm_parts(comm, 13, 8)

    def body(*refs):
        ((qb_ref, qlat_ref, kb_ref, vb_ref, ckv_ref, kpe_ref, tab_ref, gqa_ref, gkva_ref, gqn_ref, gkn_ref, wqb_ref,
          wkvb_ref), cin, (qa_o, ka_o, va_o, qb_o, kb_o, vb_o, cq_o, ckvn_o), cout, _, csem) = _split_refs(refs, 13, 8, 0, comm)
        _comm_start(comm, cin, cout, csem, grid)
        ca, s1a, s2a = tab_ref[0], tab_ref[1], tab_ref[2]
        ck = tab_ref[3]
        cb, s1b, s2b = tab_ref[4], tab_ref[5], tab_ref[6]
        ql = qlat_ref[...]
        cq = (ql * _rstd(ql, Q_LORA) * gqa_ref[...]).astype(BF16)
        cq_o[...] = cq
        qa = _dot_nt(cq, wqb_ref[...])
        slabs = [slice(h * HP, (h + 1) * HP) for h in range(H_A)]
        qa_o[...] = jnp.concatenate(
            [(_rope_fwd(qa[:, sl], ca, s1a, s2a) * scale_a).astype(BF16) for sl in slabs], axis=1)
        cr = ckv_ref[...]
        ckv = (cr * _rstd(cr, KV_LORA) * gkva_ref[...]).astype(BF16)
        ckvn_o[...] = ckv
        kva = _dot_nt(ckv, wkvb_ref[...])
        kpe = _rope_fwd(kpe_ref[...], ck, s1a, s2a)
        ka_o[...] = jnp.concatenate([(kva[:, sl] + kpe).astype(BF16) for sl in slabs], axis=1)
        va_o[...] = kva[:, H_A * HP:].astype(BF16)
        gqn, gkn = gqn_ref[...], gkn_ref[...]

        def norm_rope(ref, sl, g, scale):
            xs = ref[:, sl]
            y = _rope_fwd(xs * _rstd(xs, HD_B) * g, cb, s1b, s2b)
            return (y if scale is None else y * scale).astype(BF16)

        qb_o[...] = jnp.concatenate([norm_rope(qb_ref, sl, gqn, scale_b) for sl in slabs], axis=1)
        kb_o[...] = jnp.concatenate([norm_rope(kb_ref, sl, gkn, None) for sl in slabs[:KV_B]], axis=1)
        vb_o[...] = vb_ref[...].astype(BF16)
        _comm_finish(comm, cin, cout, csem, grid)

    def o(width):
        return jax.ShapeDtypeStruct((t, width), BF16)

    return pl.pallas_call(
        body, name="attn_prep", grid=grid,
        out_shape=(o(H_A * HP), o(H_A * HP), o(H_A * HP), o(H_B * HP), o(KV_B * HP), o(KV_B * HP), o(Q_LORA), o(KV_LORA),
                   *c_outs),
        in_specs=[_rows(tm, 1024, 0), _rows(tm, 256, 4), _rows(tm, 256, 5), _rows(tm, 256, 6),
                  _rows(tm, 128, 14), _rows(tm, 128, 15),
                  pl.BlockSpec((7, tm, HP), lambda i: (0, i % nsb, 0)),
                  _resident((1, Q_LORA)), _resident((1, KV_LORA)), _resident((1, HP)), _resident((1, HP)),
                  _resident((H_A * HP, Q_LORA)), _resident((2 * H_A * HP, KV_LORA)), *c_in_specs],
        out_specs=(_rows(tm, H_A * HP), _rows(tm, H_A * HP), _rows(tm, H_A * HP), _rows(tm, H_B * HP),
                   _rows(tm, KV_B * HP), _rows(tm, KV_B * HP), _rows(tm, Q_LORA), _rows(tm, KV_LORA), *([_ANY] * len(c_outs))),
        scratch_shapes=c_sems, input_output_aliases=alias, compiler_params=_params(("arbitrary",)),
    )(zp, zp, zp, zp, zp, zp, tabs, g_qa, g_kva, g_qn, g_kn, w_qb_t, w_kvb_t, *c_ins)


def attn_fwd(q, k, v, n_b, s_len, tq, name, comm=None):
    t = q.shape[0]
    n_h, n_hk = q.shape[1] // HP, k.shape[1] // HP
    grp = n_h // n_hk
    nq = s_len // tq
    sub = min(tq, 256)
    grid = (n_b, n_h, nq)
    c_ins, c_in_specs, c_outs, c_sems, alias = _comm_parts(comm, 3, 2)

    def body(*refs):
        (q_ref, k_ref, v_ref), cin, (o_ref, lse_ref), cout, _, csem = _split_refs(refs, 3, 2, 0, comm)
        _comm_start(comm, cin, cout, csem, grid)
        kv, vv = k_ref[...], v_ref[...]
        for r in range(tq // sub):
            rows = slice(r * sub, (r + 1) * sub)
            s = _dot_nt(q_ref[rows, :], kv)
            m = jnp.max(s, axis=-1, keepdims=True)
            p = jnp.exp(s - m)
            l = jnp.sum(p, axis=-1, keepdims=True)
            o_ref[rows, :] = (_dot_nn(p.astype(BF16), vv) * (1.0 / l)).astype(o_ref.dtype)
            lse_ref[rows, :] = jnp.broadcast_to(m + jnp.log(l), (sub, HP))
        _comm_finish(comm, cin, cout, csem, grid)

    qspec = pl.BlockSpec((tq, HP), lambda b, h, i: (b * nq + i, h))
    kspec = pl.BlockSpec((s_len, HP), lambda b, h, i: (b, h // grp))
    return pl.pallas_call(
        body, name=name, grid=grid,
        out_shape=(jax.ShapeDtypeStruct((t, n_h * HP), BF16), jax.ShapeDtypeStruct((t, n_h * HP), F32), *c_outs),
        in_specs=[qspec, kspec, kspec, *c_in_specs], out_specs=(qspec, qspec, *([_ANY] * len(c_outs))),
        scratch_shapes=c_sems, input_output_aliases=alias,
        compiler_params=_params(("arbitrary", "arbitrary", "arbitrary")),
    )(q, k, v, *c_ins)


def merge_fwd(oa, ob, zp, x, w_oa_t, w_ob_t, wpack, off, tm):
    t = x.shape[0]

    def body(oa_ref, ob_ref, ga_ref, gb_ref, x_ref, woa_ref, wob_ref, wo_ref, x1_o, mg_o, y_o):
        ya = _dot_nt(oa_ref[...], woa_ref[...])
        yb = _dot_nt(ob_ref[...], wob_ref[...])
        y_o[:, 0:D_MODEL] = ya.astype(BF16)
        y_o[:, D_MODEL:2 * D_MODEL] = yb.astype(BF16)
        merged = (jax.nn.sigmoid(ga_ref[...].astype(F32)) * ya + jax.nn.sigmoid(gb_ref[...].astype(F32)) * yb).astype(BF16)
        mg_o[...] = merged
        x1_o[...] = x_ref[...] + _dot_nn(merged, _wrows(wo_ref, 0, D_MODEL))

    return pl.pallas_call(
        body, name="merge_fwd", grid=(t // tm,),
        out_shape=(jax.ShapeDtypeStruct((t, D_MODEL), F32), jax.ShapeDtypeStruct((t, D_MODEL), BF16),
                   jax.ShapeDtypeStruct((t, 2 * D_MODEL), BF16)),
        in_specs=[_rows(tm, H_A * HP), _rows(tm, H_B * HP), _rows(tm, 1024, 0), _rows(tm, 1024, 1), _rows(tm, D_MODEL),
                  _resident((D_MODEL, H_A * HP)), _resident((D_MODEL, H_B * HP)), _packed_weight(128, off["w_o"])],
        out_specs=(_rows(tm, D_MODEL), _rows(tm, D_MODEL), _rows(tm, 2 * D_MODEL)), compiler_params=_params(("parallel",)),
    )(oa, ob, zp, zp, x, w_oa_t, w_ob_t, wpack)


def mlp_fwd(x1, g_mlp, wpack, off, tm):
    t = x1.shape[0]
    fc = 1024

    def body(x_ref, g_ref, wup_ref, wdn_ref, x2_o, u_o):
        xv = x_ref[...]
        h2 = (xv * _rstd(xv, D_MODEL) * g_ref[...]).astype(BF16)
        acc = xv
        for cidx in range(D_FF // fc):
            sl = slice(cidx * fc, (cidx + 1) * fc)
            u = jnp.maximum(_dot_nt(h2, _wrows(wup_ref, cidx * fc, fc)), 0.0)
            u_o[:, sl] = u.astype(BF16)
            acc = acc + _dot_nn((u * u).astype(BF16), _wrows(wdn_ref, cidx * fc, fc))
        x2_o[...] = acc

    return pl.pallas_call(
        body, name="mlp_fwd", grid=(t // tm,),
        out_shape=(jax.ShapeDtypeStruct((t, D_MODEL), F32), jax.ShapeDtypeStruct((t, D_FF), BF16)),
        in_specs=[_rows(tm, D_MODEL), _resident((1, D_MODEL)), _packed_weight(512, off["w_up"]), _packed_weight(512, off["w_down"])],
        out_specs=(_rows(tm, D_MODEL), _rows(tm, D_FF)), compiler_params=_params(("parallel",)),
    )(x1, g_mlp, wpack, wpack)


def ple_loss_bwd(x2, p, tgt, g_ple, g_final, wpack, off, w_ple_t, tm):
    t = x2.shape[0]
    inv_d = 1.0 / D_MODEL

    def body(x2_ref, p_ref, tg_ref, gp_ref, gf_ref, wpg_ref, wple_ref, dx2_o, dt_o, h3_o, dpe_o, st_o, dx2b_o):
        @pl.when(pl.program_id(0) == 0)
        def _():
            st_o[...] = jnp.zeros_like(st_o)

        x2v = x2_ref[...]
        gp, gf = gp_ref[...], gf_ref[...]
        w_pg = _wrows(wpg_ref, 0, D_MODEL)
        r2 = _rstd(x2v, D_MODEL)
        xh2 = x2v * r2
        h3 = (xh2 * gp).astype(BF16)
        h3_o[...] = h3
        gate = jax.nn.sigmoid(_dot_nn(h3, w_pg))
        pe = _dot_nt(p_ref[...].astype(BF16), wple_ref[...])
        x3 = x2v + gate * pe
        r3 = _rstd(x3, D_MODEL)
        xh3 = x3 * r3
        err = xh3 * gf - tg_ref[...]
        dy = err * inv_d
        dx3 = _rms_bwd(dy, xh3, r3, gf, D_MODEL)
        dpe_o[...] = (dx3 * gate).astype(BF16)
        dt = (dx3 * pe * gate * (1.0 - gate)).astype(BF16)
        dt_o[...] = dt
        dh3 = _dot_nt(dt, w_pg)
        dx2 = dx3 + _rms_bwd(dh3, xh2, r2, gp, D_MODEL)
        dx2_o[...] = dx2
        dx2b_o[...] = dx2.astype(BF16)
        st_o[0:1, :] += _colsum(dh3 * xh2)
        st_o[1:2, :] += _colsum(dy * xh3)
        st_o[2:3, :] += _colsum(err * err) * (0.5 * inv_d)

    bf = jax.ShapeDtypeStruct((t, D_MODEL), BF16)
    return pl.pallas_call(
        body, name="ple_loss_bwd", grid=(t // tm,),
        out_shape=(jax.ShapeDtypeStruct((t, D_MODEL), F32), bf, bf, bf, jax.ShapeDtypeStruct((3, D_MODEL), F32), bf),
        in_specs=[_rows(tm, D_MODEL), _rows(tm, PLE_DIM), _rows(tm, D_MODEL), _resident((1, D_MODEL)), _resident((1, D_MODEL)),
                  _packed_weight(128, off["w_ple_gate"]), _resident((D_MODEL, PLE_DIM))],
        out_specs=(_rows(tm, D_MODEL), _rows(tm, D_MODEL), _rows(tm, D_MODEL), _rows(tm, D_MODEL),
                   pl.BlockSpec((3, D_MODEL), lambda i: (0, 0)), _rows(tm, D_MODEL)),
        compiler_params=_params(("arbitrary",)),
    )(x2, p, tgt, g_ple, g_final, wpack, w_ple_t)


def mlp_bwd(dx2, x1, u, g_mlp, wpack, off, tm):
    t = x1.shape[0]
    fc = 1024

    def body(dx2_ref, x1_ref, u_ref, g_ref, wup_ref, wdn_ref, dx1_o, da_o, h2_o, st_o, dx1b_o):
        @pl.when(pl.program_id(0) == 0)
        def _():
            st_o[...] = jnp.zeros_like(st_o)

        d2 = dx2_ref[...]
        d2b = d2.astype(BF16)
        dh2 = jnp.zeros((tm, D_MODEL), F32)
        for cidx in range(D_FF // fc):
            sl = slice(cidx * fc, (cidx + 1) * fc)
            da = (_dot_nt(d2b, _wrows(wdn_ref, cidx * fc, fc)) * (2.0 * u_ref[:, sl].astype(F32))).astype(BF16)
            da_o[:, sl] = da
            dh2 = dh2 + _dot_nn(da, _wrows(wup_ref, cidx * fc, fc))
        xv = x1_ref[...]
        g = g_ref[...]
        r1 = _rstd(xv, D_MODEL)
        xh1 = xv * r1
        h2_o[...] = (xh1 * g).astype(BF16)
        st_o[...] += _colsum(dh2 * xh1)
        dx1 = d2 + _rms_bwd(dh2, xh1, r1, g, D_MODEL)
        dx1_o[...] = dx1
        dx1b_o[...] = dx1.astype(BF16)

    return pl.pallas_call(
        body, name="mlp_bwd", grid=(t // tm,),
        out_shape=(jax.ShapeDtypeStruct((t, D_MODEL), F32), jax.ShapeDtypeStruct((t, D_FF), BF16),
                   jax.ShapeDtypeStruct((t, D_MODEL), BF16), jax.ShapeDtypeStruct((1, D_MODEL), F32),
                   jax.ShapeDtypeStruct((t, D_MODEL), BF16)),
        in_specs=[_rows(tm, D_MODEL), _rows(tm, D_MODEL), _rows(tm, D_FF), _resident((1, D_MODEL)),
                  _packed_weight(512, off["w_up"]), _packed_weight(512, off["w_down"])],
        out_specs=(_rows(tm, D_MODEL), _rows(tm, D_FF), _rows(tm, D_MODEL), pl.BlockSpec((1, D_MODEL), lambda i: (0, 0)),
                   _rows(tm, D_MODEL)),
        compiler_params=_params(("arbitrary",)),
    )(dx2, x1, u, g_mlp, wpack, wpack)


def merge_bwd(dx1b, yab, zp, w_oa_t, w_ob_t, wpack, off, tm, comm=None):
    t = dx1b.shape[0]
    grid = (t // tm,)
    c_ins, c_in_specs, c_outs, c_sems, alias = _comm_parts(comm, 7, 5)

    def body(*refs):
        ((dx1_ref, y_ref, ga_ref, gb_ref, woa_ref, wob_ref, wo_ref), cin,
         (doa_o, dob_o, dg_o, dya_o, dyb_o), cout, _, csem) = _split_refs(refs, 7, 5, 0, comm)
        _comm_start(comm, cin, cout, csem, grid)
        dm = _dot_nt(dx1_ref[...], _wrows(wo_ref, 0, D_MODEL))
        for g_ref, w_ref, do_o, dy_o, col in ((ga_ref, woa_ref, doa_o, dya_o, 0), (gb_ref, wob_ref, dob_o, dyb_o, 1)):
            cols = slice(col * D_MODEL, (col + 1) * D_MODEL)
            sg = jax.nn.sigmoid(g_ref[...].astype(F32))
            dyv = (dm * sg).astype(BF16)
            dy_o[...] = dyv
            dg_o[:, cols] = (dm * y_ref[:, cols].astype(F32) * sg * (1.0 - sg)).astype(BF16)
            do_o[...] = _dot_nn(dyv, w_ref[...]).astype(BF16)
        _comm_finish(comm, cin, cout, csem, grid)

    bf = jax.ShapeDtypeStruct((t, D_MODEL), BF16)
    return pl.pallas_call(
        body, name="merge_bwd", grid=grid,
        out_shape=(bf, bf, jax.ShapeDtypeStruct((t, ZP), BF16), bf, bf, *c_outs),
        in_specs=[_rows(tm, D_MODEL), _rows(tm, 2 * D_MODEL), _rows(tm, 1024, 0), _rows(tm, 1024, 1),
                  _resident((D_MODEL, H_A * HP)), _resident((D_MODEL, H_B * HP)), _packed_weight(128, off["w_o"]), *c_in_specs],
        out_specs=(_rows(tm, D_MODEL), _rows(tm, D_MODEL), _rows(tm, 2 * D_MODEL), _rows(tm, D_MODEL), _rows(tm, D_MODEL),
                   *([_ANY] * len(c_outs))),
        scratch_shapes=c_sems, input_output_aliases=alias,
        compiler_params=_params(("arbitrary",)),
    )(dx1b, yab, zp, zp, w_oa_t, w_ob_t, wpack, *c_ins)


def attn_bwd(q, k, v, do, o, lse, n_b, s_len, tq, name, comm=None):
    t = q.shape[0]
    n_h, n_hk = q.shape[1] // HP, k.shape[1] // HP
    grp = n_h // n_hk
    nq = s_len // tq
    sub = min(tq, 256)
    grid = (n_b, n_hk, grp, nq)
    c_ins, c_in_specs, c_outs, c_sems, alias = _comm_parts(comm, 6, 3)

    def body(*refs):
        ((q_ref, k_ref, v_ref, do_ref, o_ref, lse_ref), cin, (dq_o, dk_o, dv_o), cout, (p_s, ds_s, dk_acc, dv_acc),
         csem) = _split_refs(refs, 6, 3, 4, comm)
        _comm_start(comm, cin, cout, csem, grid)

        @pl.when((pl.program_id(2) == 0) & (pl.program_id(3) == 0))
        def _():
            dk_acc[...] = jnp.zeros_like(dk_acc)
            dv_acc[...] = jnp.zeros_like(dv_acc)

        kv, vv = k_ref[...], v_ref[...]
        for r in range(tq // sub):
            rows = slice(r * sub, (r + 1) * sub)
            qv, dov = q_ref[rows, :], do_ref[rows, :]
            delta = jnp.sum(dov.astype(F32) * o_ref[rows, :].astype(F32), axis=-1, keepdims=True)
            delta_row = jnp.broadcast_to(delta, (sub, HP)).T[0:1, :]
            lse_row = lse_ref[rows, :].T[0:1, :]
            pt = jnp.exp(_dot_nt(kv, qv) - lse_row)
            dst = (pt * (_dot_nt(vv, dov) - delta_row)).astype(BF16)
            p_s[:, rows] = pt.astype(BF16)
            ds_s[:, rows] = dst
            dq_o[rows, :] = _dot_tn(dst, kv).astype(dq_o.dtype)
        dk_acc[...] += _dot_nn(ds_s[...], q_ref[...])
        dv_acc[...] += _dot_nn(p_s[...], do_ref[...])

        @pl.when((pl.program_id(2) == grp - 1) & (pl.program_id(3) == nq - 1))
        def _():
            dk_o[...] = dk_acc[...].astype(dk_o.dtype)
            dv_o[...] = dv_acc[...].astype(dv_o.dtype)

        _comm_finish(comm, cin, cout, csem, grid)

    qspec = pl.BlockSpec((tq, HP), lambda b, hk, g, i: (b * nq + i, hk * grp + g))
    kspec = pl.BlockSpec((s_len, HP), lambda b, hk, g, i: (b, hk))
    return pl.pallas_call(
        body, name=name, grid=grid,
        out_shape=(jax.ShapeDtypeStruct((t, n_h * HP), BF16), jax.ShapeDtypeStruct((t, n_hk * HP), BF16),
                   jax.ShapeDtypeStruct((t, n_hk * HP), BF16), *c_outs),
        in_specs=[qspec, kspec, kspec, qspec, qspec, qspec, *c_in_specs],
        out_specs=(qspec, kspec, kspec, *([_ANY] * len(c_outs))),
        scratch_shapes=[pltpu.VMEM((s_len, tq), BF16), pltpu.VMEM((s_len, tq), BF16),
                        pltpu.VMEM((s_len, HP), F32), pltpu.VMEM((s_len, HP), F32), *c_sems],
        input_output_aliases=alias,
        compiler_params=_params(("arbitrary", "arbitrary", "arbitrary", "arbitrary")),
    )(q, k, v, do, o, lse, *c_ins)


def prep_bwd(dqa, dka, dva, dqb, dkb, dvb, zp, dz, tabs, g_qa, g_kva, g_qn, g_kn, w_qb_t, w_kvb_t, tm, s_len):
    t = zp.shape[0]
    nsb = s_len // tm
    scale_a = (QK_NOPE + QK_ROPE) ** -0.5
    scale_b = HD_B ** -0.5

    def body(dqa_ref, dka_ref, dva_ref, dqb_ref, dkb_ref, dvb_ref, qb_ref, qlat_ref, kb_ref, ckv_ref, tab_ref,
             gqa_ref, gkva_ref, gqn_ref, gkn_ref, wqb_ref, wkvb_ref, _, dz_o, dqap_o, dkva_o, st_o):
        dzq_o, dsm_o = dz_o.at[:, 0:1024], dz_o.at[:, 1024:2048]

        @pl.when(pl.program_id(0) == 0)
        def _():
            st_o[...] = jnp.zeros_like(st_o)

        ca, s1a, s2a = tab_ref[0], tab_ref[1], tab_ref[2]
        ck = tab_ref[3]
        cb, s1b, s2b = tab_ref[4], tab_ref[5], tab_ref[6]
        for h in range(H_A):
            sl = slice(h * HP, (h + 1) * HP)
            dqap_o[:, sl] = _rope_bwd(dqa_ref[:, sl].astype(F32) * scale_a, ca, s1a, s2a).astype(BF16)
        dcq = _dot_nn(dqap_o[...], wqb_ref[...])
        ql = qlat_ref[...]
        rq = _rstd(ql, Q_LORA)
        xh = ql * rq
        gqa = gqa_ref[...]
        st_o[0:1, :] += _colsum(dcq * xh)
        dsm_o[:, 0:256] = _rms_bwd(dcq, xh, rq, gqa, Q_LORA).astype(BF16)
        dkpe = jnp.zeros((tm, HP), F32)
        for h in range(H_A):
            sl = slice(h * HP, (h + 1) * HP)
            dk = dka_ref[:, sl]
            dkpe = dkpe + dk.astype(F32)
            dkva_o[:, sl] = dk.astype(BF16)
        dkva_o[:, H_A * HP:] = dva_ref[...].astype(BF16)
        dsm_o[:, 896:1024] = _rope_bwd(dkpe, ck, s1a, s2a).astype(BF16)
        dckv = _dot_nn(dkva_o[...], wkvb_ref[...])
        cr = ckv_ref[...]
        rk = _rstd(cr, KV_LORA)
        xh = cr * rk
        st_o[1:2, 0:128] += _colsum(dckv * xh)
        dsm_o[:, 768:896] = _rms_bwd(dckv, xh, rk, gkva_ref[...], KV_LORA).astype(BF16)
        gqn, gkn = gqn_ref[...], gkn_ref[...]
        dgq = jnp.zeros((1, HP), F32)
        for h in range(H_B):
            sl = slice(h * HP, (h + 1) * HP)
            dy = _rope_bwd(dqb_ref[:, sl].astype(F32) * scale_b, cb, s1b, s2b)
            xs = qb_ref[:, sl]
            r = _rstd(xs, HD_B)
            xh = xs * r
            dgq = dgq + _colsum(dy * xh)
            dzq_o[:, sl] = _rms_bwd(dy, xh, r, gqn, HD_B).astype(BF16)
        st_o[2:3, 0:128] += dgq
        dgk = jnp.zeros((1, HP), F32)
        for h in range(KV_B):
            sl = slice(h * HP, (h + 1) * HP)
            dy = _rope_bwd(dkb_ref[:, sl].astype(F32), cb, s1b, s2b)
            xs = kb_ref[:, sl]
            r = _rstd(xs, HD_B)
            xh = xs * r
            dgk = dgk + _colsum(dy * xh)
            dsm_o[:, 256 + h * HP:256 + (h + 1) * HP] = _rms_bwd(dy, xh, r, gkn, HD_B).astype(BF16)
        st_o[3:4, 0:128] += dgk
        dsm_o[:, 512:768] = dvb_ref[...].astype(BF16)

    return pl.pallas_call(
        body, name="prep_bwd", grid=(t // tm,),
        out_shape=(jax.ShapeDtypeStruct((t, ZP), BF16), jax.ShapeDtypeStruct((t, 1024), BF16),
                   jax.ShapeDtypeStruct((t, 2048), BF16), jax.ShapeDtypeStruct((4, 256), F32)),
        in_specs=[_rows(tm, 1024), _rows(tm, 1024), _rows(tm, 1024), _rows(tm, 1024), _rows(tm, 256), _rows(tm, 256),
                  _rows(tm, 1024, 0), _rows(tm, 256, 4), _rows(tm, 256, 5), _rows(tm, 128, 14),
                  pl.BlockSpec((7, tm, HP), lambda i: (0, i % nsb, 0)),
                  _resident((1, Q_LORA)), _resident((1, KV_LORA)), _resident((1, HP)), _resident((1, HP)),
                  _resident((H_A * HP, Q_LORA)), _resident((2 * H_A * HP, KV_LORA)), _ANY],
        out_specs=(_rows(tm, 2048, 1), _rows(tm, 1024), _rows(tm, 2048), pl.BlockSpec((4, 256), lambda i: (0, 0))),
        input_output_aliases={17: 0}, compiler_params=_params(("arbitrary",)),
    )(dqa, dka, dva, dqb, dkb, dvb, zp, zp, zp, zp, tabs, g_qa, g_kva, g_qn, g_kn, w_qb_t, w_kvb_t, dz)


def in_bwd(dz, x, dx1, g_mix, w_in_t, tm, comm=None):
    t = x.shape[0]
    grid = (t // tm,)
    c_ins, c_in_specs, c_outs, c_sems, alias = _comm_parts(comm, 5, 2)

    def body(*refs):
        (dz_ref, x_ref, dx1_ref, g_ref, w_ref), cin, (dx_o, st_o), cout, _, csem = _split_refs(refs, 5, 2, 0, comm)
        _comm_start(comm, cin, cout, csem, grid)

        @pl.when(pl.program_id(0) == 0)
        def _():
            st_o[...] = jnp.zeros_like(st_o)

        dh = _dot_nn(dz_ref[...], w_ref[...])
        xv = x_ref[...]
        g = g_ref[...]
        r = _rstd(xv, D_MODEL)
        xh = xv * r
        st_o[...] += _colsum(dh * xh)
        dx_o[...] = dx1_ref[...] + _rms_bwd(dh, xh, r, g, D_MODEL)
        _comm_finish(comm, cin, cout, csem, grid)

    return pl.pallas_call(
        body, name="in_bwd", grid=grid,
        out_shape=(jax.ShapeDtypeStruct((t, D_MODEL), F32), jax.ShapeDtypeStruct((1, D_MODEL), F32), *c_outs),
        in_specs=[_rows(tm, ZP), _rows(tm, D_MODEL), _rows(tm, D_MODEL),
                  _resident((1, D_MODEL)), _resident((ZP, D_MODEL)), *c_in_specs],
        out_specs=(_rows(tm, D_MODEL), pl.BlockSpec((1, D_MODEL), lambda i: (0, 0)), *([_ANY] * len(c_outs))),
        scratch_shapes=c_sems, input_output_aliases=alias,
        compiler_params=_params(("arbitrary",)),
    )(dz, x, dx1, g_mix, w_in_t, *c_ins)


def matmul_tn(a, b, name, square_a=False):
    t, m = a.shape
    n = b.shape[1]
    bm = min(m, 512)
    tk = min(t, 4096)

    def body(a_ref, b_ref, o_ref):
        @pl.when(pl.program_id(1) == 0)
        def _():
            o_ref[...] = jnp.zeros_like(o_ref)

        av = a_ref[...]
        if square_a:
            av = (av.astype(F32) * av.astype(F32))
        o_ref[...] += _dot_tn(av.astype(BF16), b_ref[...].astype(BF16))

    return pl.pallas_call(
        body, name=name, grid=(m // bm, t // tk), out_shape=jax.ShapeDtypeStruct((m, n), F32),
        in_specs=[pl.BlockSpec((tk, bm), lambda i, kk: (kk, i)), pl.BlockSpec((tk, n), lambda i, kk: (kk, 0))],
        out_specs=pl.BlockSpec((bm, n), lambda i, kk: (i, 0)),
        compiler_params=_params(("parallel", "arbitrary")),
    )(a, b)


def matmul_tn_packed(a, b, name, rows, row_off, total_rows, buf=None, square_a=False):
    t, m = a.shape
    n = b.shape[1]
    pd = max(1, 512 // rows)
    bm = pd * rows
    tk = min(t, 4096)
    nk = t // tk

    def body(a_ref, b_ref, *rest):
        o_ref, acc = rest[-2], rest[-1]

        @pl.when(pl.program_id(1) == 0)
        def _():
            acc[...] = jnp.zeros_like(acc)

        av = a_ref[...]
        if square_a:
            av = (av.astype(F32) * av.astype(F32))
        acc[...] += _dot_tn(av.astype(BF16), b_ref[...].astype(BF16))

        @pl.when(pl.program_id(1) == nk - 1)
        def _():
            o_ref[...] = acc[...].reshape(pd, rows, n).astype(o_ref.dtype)

    in_specs = [pl.BlockSpec((tk, bm), lambda i, kk: (kk, i)), pl.BlockSpec((tk, n), lambda i, kk: (kk, 0))]
    args = [a, b]
    if buf is not None:
        in_specs.append(_ANY)
        args.append(buf)
    return pl.pallas_call(
        body, name=name, grid=(m // bm, nk), out_shape=jax.ShapeDtypeStruct((N_DEV, total_rows, n), BF16),
        in_specs=in_specs, out_specs=pl.BlockSpec((pd, rows, n), lambda i, kk: (i, row_off // rows, 0)),
        scratch_shapes=[pltpu.VMEM((bm, n), F32)], input_output_aliases={2: 0} if buf is not None else {},
        compiler_params=_params(("parallel", "arbitrary")),
    )(*args)


def adamw(w, g, m, v, name, g_transposed=False):
    _, r, c = w.shape
    tr = 256 if (not g_transposed and r > 256 and r % 256 == 0) else r
    c1 = 1.0 - ADAM_B1 ** ADAM_STEP
    c2 = 1.0 - ADAM_B2 ** ADAM_STEP

    def body(w_ref, g_ref, m_ref, v_ref, g_o, d_o, m_o, v_o):
        gv = g_ref[...].T if g_transposed else g_ref[...]
        mn = ADAM_B1 * m_ref[0] + (1.0 - ADAM_B1) * gv
        vn = ADAM_B2 * v_ref[0] + (1.0 - ADAM_B2) * (gv * gv)
        g_o[0] = gv
        m_o[0] = mn
        v_o[0] = vn
        d_o[0] = -ADAM_LR * ((mn / c1) / (jnp.sqrt(vn / c2) + ADAM_EPS) + ADAM_WD * w_ref[0])

    spec = pl.BlockSpec((1, tr, c), lambda i: (0, i, 0))
    gspec = pl.BlockSpec((c, r), lambda i: (0, 0)) if g_transposed else pl.BlockSpec((tr, c), lambda i: (i, 0))
    shp = jax.ShapeDtypeStruct((1, r, c), F32)
    return pl.pallas_call(
        body, name=name, grid=(r // tr,), out_shape=(shp,) * 4, in_specs=[spec, gspec, spec, spec], out_specs=(spec,) * 4,
        compiler_params=_params(("parallel",)),
    )(w, g, m, v)


def _rope_tables(s_len):
    def angles(pos, dim):
        inv = np.float32(ROPE_THETA) ** (-np.arange(0, dim, 2, dtype=np.float32) / np.float32(dim))
        return pos.astype(np.float32)[:, None] * inv[None, :]

    tpos = np.arange(s_len)
    a1 = angles(tpos, QK_ROPE)
    ar = angles(tpos // GRID_W, HD_B // 2)
    ac = angles(tpos % GRID_W, HD_B // 2)
    z16 = np.zeros((s_len, 16), np.float32)
    z32 = np.zeros((s_len, 32), np.float32)
    z64 = np.zeros((s_len, 64), np.float32)
    one64 = np.ones((s_len, 64), np.float32)
    c1, s1 = np.cos(a1), np.sin(a1)
    ca = np.concatenate([one64, c1, c1, z32], axis=1)
    ck = np.concatenate([z64, c1, c1, z32], axis=1)
    s1a = np.concatenate([z64, -s1, z16, z32], axis=1)
    s2a = np.concatenate([z64, z16, s1, z32], axis=1)
    cr, sr, cc, sc = np.cos(ar), np.sin(ar), np.cos(ac), np.sin(ac)
    cb = np.concatenate([cr, cr, cc, cc, z64], axis=1)
    s1b = np.concatenate([-sr, z16, -sc, z16, z64], axis=1)
    s2b = np.concatenate([z16, sr, z16, sc, z64], axis=1)
    return jnp.asarray(np.stack([ca, s1a, s2a, ck, cb, s1b, s2b]).astype(np.float32))


def _pad_heads(a, n_heads, axis):
    shp = a.shape
    a = a.reshape(shp[:axis] + (n_heads, shp[axis] // n_heads) + shp[axis + 1:])
    pad = [(0, 0)] * a.ndim
    pad[axis + 1] = (0, HP - a.shape[axis + 1])
    a = jnp.pad(a, pad)
    return a.reshape(shp[:axis] + (n_heads * HP,) + shp[axis + 1:])


def _unpad_heads(a, n_heads, width, axis):
    shp = a.shape
    a = a.reshape(shp[:axis] + (n_heads, HP) + shp[axis + 1:])
    a = lax.slice_in_dim(a, 0, width, axis=axis + 1)
    return a.reshape(shp[:axis] + (n_heads * width,) + shp[axis + 1:])


def _pack_rows(blocks, names):
    parts = []
    for name in names:
        b = blocks[name]
        padr = PACK_ROWS[name] - b.shape[-2]
        if padr:
            b = jnp.pad(b, [(0, 0)] * (b.ndim - 2) + [(0, padr), (0, 0)])
        parts.append(b)
    return jnp.concatenate(parts, axis=parts[0].ndim - 2)


def _expand_w_in(wt):
    z64 = jnp.zeros((64, D_MODEL), wt.dtype)
    z32 = jnp.zeros((32, D_MODEL), wt.dtype)
    return jnp.concatenate([
        wt[1184:2208], wt[2208:3232], _pad_heads(wt[416:928], H_B, 0), wt[0:256],
        _pad_heads(wt[928:1056], KV_B, 0), _pad_heads(wt[1056:1184], KV_B, 0), wt[256:384],
        z64, wt[384:416], z32], axis=0)


def _collapse_w_in(dw):
    dg, dq, ds = dw[0:2048], dw[2048:3072], dw[3072:4096]
    return jnp.concatenate([
        ds[0:256], ds[768:896], ds[960:992], _unpad_heads(dq, H_B, HD_B, 0), _unpad_heads(ds[256:512], KV_B, HD_B, 0),
        _unpad_heads(ds[512:768], KV_B, HD_B, 0), dg], axis=0)


def kernel(x, p, g_mix, w_in, g_qa, w_qb, g_kva, w_kvb, g_qn, g_kn, w_oa, w_ob, w_o, g_mlp, w_up, w_down, g_ple, w_ple_gate, w_ple, g_final, loss_target, m_g_mix, m_w_in, m_g_qa, m_w_qb, m_g_kva, m_w_kvb, m_g_qn, m_g_kn, m_w_oa, m_w_ob, m_w_o, m_g_mlp, m_w_up, m_w_down, m_g_ple, m_w_ple_gate, m_w_ple, m_g_final, v_g_mix, v_w_in, v_g_qa, v_w_qb, v_g_kva, v_w_kvb, v_g_qn, v_g_kn, v_w_oa, v_w_ob, v_w_o, v_g_mlp, v_w_up, v_w_down, v_g_ple, v_w_ple_gate, v_w_ple, v_g_final):
    n_b, s_len, _ = x.shape
    t = n_b * s_len
    tm = min(512, s_len)
    tq_f = min(2048, s_len)
    tq_b = min(2048, s_len)

    mats = dict(w_in=(w_in, m_w_in, v_w_in), w_qb=(w_qb, m_w_qb, v_w_qb), w_kvb=(w_kvb, m_w_kvb, v_w_kvb),
                w_oa=(w_oa, m_w_oa, v_w_oa), w_ob=(w_ob, m_w_ob, v_w_ob), w_o=(w_o, m_w_o, v_w_o),
                w_up=(w_up, m_w_up, v_w_up), w_down=(w_down, m_w_down, v_w_down),
                w_ple_gate=(w_ple_gate, m_w_ple_gate, v_w_ple_gate), w_ple=(w_ple, m_w_ple, v_w_ple))
    col_sharded = ("w_in", "w_qb", "w_kvb", "w_oa", "w_ob", "w_up", "w_ple")

    blocks = {}
    for name in PACK_W1 + ("w_oa", "w_ob", "w_ple"):
        blocks[name] = mats[name][0][0].T.reshape(-1, D_MODEL).astype(BF16)
    off_w1, _ = _pack_offsets(PACK_W1)
    off_w2, _ = _pack_offsets(PACK_W2)
    off_w3, _ = _pack_offsets(PACK_W3)
    xf = x.reshape(t, D_MODEL)
    h, full1 = norm_x(xf, g_mix, tm, comm=gather3_first_comm(_pack_rows(blocks, PACK_W1)))
    pack2, pack3, full1 = pack_late_weights(w_up, w_down, w_o, w_ple_gate, _pack_rows(blocks, ("w_oa", "w_ob", "w_ple")),
                                            comm=gather3_second_comm(full1))
    (full1,) = run_comm(gather3_third_comm(full1), "gather_last_hop")

    def gathered(full, offs, name, rows, width):
        return full[:, offs[name]:offs[name] + rows].reshape(-1, width)

    w_in_t = _expand_w_in(gathered(full1, off_w1, "w_in", 404, D_MODEL))
    w_qb_t = _pad_heads(gathered(full1, off_w1, "w_qb", 24, Q_LORA), H_A, 0)
    wkvb = gathered(full1, off_w1, "w_kvb", 16, KV_LORA).reshape(H_A, 2, 64, KV_LORA)
    w_kvb_t = jnp.concatenate([_pad_heads(wkvb[:, 0].reshape(-1, KV_LORA), H_A, 0),
                               _pad_heads(wkvb[:, 1].reshape(-1, KV_LORA), H_A, 0)], axis=0)

    tabs = _rope_tables(s_len)
    g_qn_p = jnp.pad(g_qn, ((0, 0), (0, HP - HD_B)))
    g_kn_p = jnp.pad(g_kn, ((0, 0), (0, HP - HD_B)))
    pf = p.reshape(t, PLE_DIM)
    tgt = loss_target.reshape(t, D_MODEL)

    zg, zs, full2 = in_proj(h, w_in_t, tm, comm=gather3_first_comm(pack2))
    qa, ka, va, qb, kb, vb, cq, ckv, full2 = attn_prep(zs, tabs, g_qa, g_kva, g_qn_p, g_kn_p, w_qb_t, w_kvb_t, tm, s_len,
                                                       comm=gather3_second_comm(full2))
    oa, lse_a, full3, full2 = attn_fwd(qa, ka, va, n_b, s_len, tq_f, "attn_a_fwd",
                                       comm=_join_comms(gather_first_comm(pack3), gather3_third_comm(full2)))
    ob, lse_b, full3 = attn_fwd(qb, kb, vb, n_b, s_len, tq_f, "attn_b_fwd", comm=gather_pass_comm(full3))
    w_oa_t = _pad_heads(gathered(full2, off_w2, "w_oa", 64, H_A * V_DIM_A), H_A, 1)
    w_ob_t = _pad_heads(gathered(full2, off_w2, "w_ob", 64, H_B * HD_B), H_B, 1)
    w_ple_t = gathered(full2, off_w2, "w_ple", 32, PLE_DIM)
    x1, merged, yab = merge_fwd(oa, ob, zg, xf, w_oa_t, w_ob_t, full2, off_w2, tm)
    x2, u = mlp_fwd(x1, g_mlp, full3, off_w3, tm)
    dx2, dt, h3, dpe, st_ple, dx2b = ple_loss_bwd(x2, pf, tgt, g_ple, g_final.reshape(1, D_MODEL), full2, off_w2, w_ple_t, tm)
    dx1, da, h2, st_mlp, dx1b = mlp_bwd(dx2, x1, u, g_mlp, full3, off_w3, tm)

    core = lax.axis_index("c").astype(jnp.int32).reshape(1)
    chip = (2 * lax.axis_index("x") + lax.axis_index("y")).astype(jnp.int32).reshape(1)

    def packed(gblocks, names):
        return _pack_rows({n: gblocks[n].reshape(N_DEV, -1, D_MODEL).astype(BF16) for n in names}, names)

    off_g1, rows_g1 = _pack_offsets(PACK_G1)
    gpack1 = matmul_tn_packed(da, h2, "gw_up", 512, off_g1["w_up"], rows_g1)
    gpack1 = matmul_tn_packed(u, dx2b, "gw_down", 512, off_g1["w_down"], rows_g1, buf=gpack1, square_a=True)
    gpack1 = matmul_tn_packed(h3, dt, "gw_pg", 128, off_g1["w_ple_gate"], rows_g1, buf=gpack1)
    gple = matmul_tn(dpe, pf, "gw_ple").reshape(N_DEV, -1, D_MODEL).astype(BF16)
    gpack1 = lax.dynamic_update_slice(gpack1, gple, (0, off_g1["w_ple"], 0))
    doa, dob, dz, dya, dyb, got1 = merge_bwd(dx1b, yab, zg, w_oa_t, w_ob_t, full2, off_w2, tm,
                                               comm=scatter_sibling_comm(gpack1))
    part1 = add_pairs(gpack1, got1, core)

    off_g2, rows_g2 = _pack_offsets(PACK_G2)
    g2 = dict(w_oa=_unpad_heads(matmul_tn(dya, oa, "gw_oa"), H_A, V_DIM_A, 1),
              w_ob=_unpad_heads(matmul_tn(dyb, ob, "gw_ob"), H_B, HD_B, 1))
    gpack2 = matmul_tn_packed(merged, dx1b, "gw_o", 128, off_g2["w_o"], rows_g2)
    gpack2 = lax.dynamic_update_slice(gpack2, packed(g2, ("w_oa", "w_ob")), (0, off_g2["w_oa"], 0))
    dqa, dka, dva, land1, got2 = attn_bwd(qa, ka, va, doa, oa, lse_a, n_b, s_len, tq_b, "attn_a_bwd",
                                          comm=_join_comms(scatter_chips_comm(part1), scatter_sibling_comm(gpack2)))
    gshard1 = sum_chips(part1, land1, chip)
    part2 = add_pairs(gpack2, got2, core)
    dqb, dkb, dvb, land2 = attn_bwd(qb, kb, vb, dob, ob, lse_b, n_b, s_len, tq_b, "attn_b_bwd", comm=scatter_chips_comm(part2))
    gshard2 = sum_chips(part2, land2, chip)
    dz, dqap, dkva, st_prep = prep_bwd(dqa, dka, dva, dqb, dkb, dvb, zs, dz, tabs, g_qa, g_kva, g_qn_p, g_kn_p,
                                       w_qb_t, w_kvb_t, tm, s_len)

    gkv = matmul_tn(dkva, ckv, "gw_kvb")
    g3 = dict(
        w_in=_collapse_w_in(matmul_tn(dz, h, "gw_in")),
        w_qb=_unpad_heads(matmul_tn(dqap, cq, "gw_qb"), H_A, QK_NOPE + QK_ROPE, 0),
        w_kvb=jnp.stack([_unpad_heads(gkv[:H_A * HP], H_A, 64, 0).reshape(H_A, 64, KV_LORA),
                         _unpad_heads(gkv[H_A * HP:], H_A, 64, 0).reshape(H_A, 64, KV_LORA)], axis=1))
    gpack3 = packed(g3, PACK_G3)
    part3 = add_pairs(gpack3, exchange_sibling(gpack3), core)
    grad_x, st_mix, land3 = in_bwd(dz, xf, dx1, g_mix, w_in_t, tm, comm=scatter_chips_comm(part3))
    gshard3 = sum_chips(part3, land3, chip)
    off_g3, _ = _pack_offsets(PACK_G3)
    shards = {n: (gshard1, off_g1[n]) for n in PACK_G1}
    shards.update({n: (gshard2, off_g2[n]) for n in PACK_G2})
    shards.update({n: (gshard3, off_g3[n]) for n in PACK_G3})

    stats = allreduce_stats(st_mix, st_prep, st_mlp, st_ple)
    loss = jnp.sum(stats[ST_LOSS])

    out_g, out_d, out_m, out_v = {}, {}, {}, {}
    for name, (w, m, v) in mats.items():
        gshard, off = shards[name]
        r, c = w.shape[1:]
        if name in col_sharded:
            g2 = gshard[off:off + (r * c) // D_MODEL].reshape(c, r)
            if r % 128 == 0 and c % 128 == 0:
                res = adamw(w, g2, m, v, "adamw_" + name, g_transposed=True)
            else:
                res = adamw(w[0].T[None], g2, m[0].T[None], v[0].T[None], "adamw_" + name)
                res = tuple(a[0].T[None] for a in res)
        else:
            res = adamw(w, gshard[off:off + r], m, v, "adamw_" + name)
        out_g[name], out_d[name], out_m[name], out_v[name] = res

    gains = (("g_mix", g_mix, m_g_mix, v_g_mix, ST_G_MIX), ("g_qa", g_qa, m_g_qa, v_g_qa, ST_G_QA),
             ("g_kva", g_kva, m_g_kva, v_g_kva, ST_G_KVA), ("g_qn", g_qn, m_g_qn, v_g_qn, ST_G_QN),
             ("g_kn", g_kn, m_g_kn, v_g_kn, ST_G_KN), ("g_mlp", g_mlp, m_g_mlp, v_g_mlp, ST_G_MLP),
             ("g_ple", g_ple, m_g_ple, v_g_ple, ST_G_PLE), ("g_final", g_final, m_g_final, v_g_final, ST_G_FINAL))
    res = adamw_gains(stats, [(r_, w.reshape(1, -1), m.reshape(1, -1), v.reshape(1, -1)) for _, w, m, v, r_ in gains])
    for (name, w, _, _, _), (gg, gd, gm, gv) in zip(gains, res):
        out_g[name], out_d[name], out_m[name], out_v[name] = (a.reshape(w.shape) for a in (gg, gd, gm, gv))

    order = ("g_mix", "w_in", "g_qa", "w_qb", "g_kva", "w_kvb", "g_qn", "g_kn", "w_oa", "w_ob", "w_o", "g_mlp",
             "w_up", "w_down", "g_ple", "w_ple_gate", "w_ple", "g_final")
    return (loss, grad_x.reshape(x.shape), *[out_g[n] for n in order], *[out_d[n] for n in order],
            *[out_m[n] for n in order], *[out_v[n] for n in order])
```

```python
import numpy as np
import jax
import jax.numpy as jnp
from jax import lax
from jax.experimental import pallas as pl
from jax.experimental.pallas import tpu as pltpu

F32 = jnp.float32
BF16 = jnp.bfloat16

D_MODEL = 1024
EPS = 1e-6
ROPE_THETA = 10000.0
GRID_W = 64
H_A = 8
QK_NOPE = 64
QK_ROPE = 32
V_DIM_A = 64
Q_LORA = 256
KV_LORA = 128
H_B = 8
KV_B = 2
HD_B = 64
D_FF = 4 * D_MODEL
PLE_DIM = 256
HP = 128
ZP = 4096
N_DEV = 8
N_CHIP = 4

ADAM_LR = 0.001
ADAM_B1 = 0.9
ADAM_B2 = 0.999
ADAM_EPS = 1e-08
ADAM_WD = 0.01
ADAM_STEP = 10

VMEM_LIMIT = 52 * 1024 * 1024

PACK_ROWS = dict(w_in=416, w_qb=32, w_kvb=16, w_oa=64, w_ob=64, w_o=128, w_up=512, w_down=512, w_ple_gate=128, w_ple=32)
PACK_W1 = ("w_in", "w_qb", "w_kvb")
PACK_W2 = ("w_o", "w_ple_gate", "w_oa", "w_ob", "w_ple")
PACK_W3 = ("w_up", "w_down")
PACK_G1 = ("w_up", "w_down", "w_ple_gate", "w_ple")
PACK_G2 = ("w_o", "w_oa", "w_ob")
PACK_G3 = ("w_in", "w_qb", "w_kvb")


def _pack_offsets(names):
    off, o = {}, 0
    for n in names:
        off[n] = o
        o += PACK_ROWS[n]
    return off, o

ST_G_MIX, ST_G_QA, ST_G_KVA, ST_G_QN, ST_G_KN, ST_G_MLP, ST_G_PLE, ST_G_FINAL, ST_LOSS = range(9)
ST_ROWS = 16


def _dot_nn(a, b):
    return lax.dot_general(a, b, (((1,), (0,)), ((), ())), preferred_element_type=F32)


def _dot_nt(a, b):
    return lax.dot_general(a, b, (((1,), (1,)), ((), ())), preferred_element_type=F32)


def _dot_tn(a, b):
    return lax.dot_general(a, b, (((0,), (0,)), ((), ())), preferred_element_type=F32)


def _rstd(x, n):
    return lax.rsqrt(jnp.sum(x * x, axis=-1, keepdims=True) * (1.0 / n) + EPS)


def _rms_bwd(dy, xh, r, g, n):
    dxh = dy * g
    return r * (dxh - xh * (jnp.sum(dxh * xh, axis=-1, keepdims=True) * (1.0 / n)))


def _rope_fwd(x, c, s1, s2):
    return x * c + pltpu.roll(x, HP - 16, 1) * s1 + pltpu.roll(x, 16, 1) * s2


def _rope_bwd(d, c, s1, s2):
    return d * c + pltpu.roll(d * s1, 16, 1) + pltpu.roll(d * s2, HP - 16, 1)


def _colsum(v):
    return jnp.sum(v, axis=0, keepdims=True)


def _params(sem=None, vmem=VMEM_LIMIT):
    return pltpu.CompilerParams(dimension_semantics=sem, vmem_limit_bytes=vmem)


def _resident(shape):
    nd = len(shape)
    return pl.BlockSpec(shape, lambda *_: (0,) * nd, pipeline_mode=pl.Buffered(1))


def _rows(tm, width, col=0):
    return pl.BlockSpec((tm, width), lambda i: (i, col))


def _packed_weight(rows, off):
    return pl.BlockSpec((N_DEV, rows, D_MODEL), lambda *_: (0, off // rows, 0), pipeline_mode=pl.Buffered(1))


def _wrows(ref, start, size):
    rows = ref.shape[1]
    return ref[start // rows:(start + size) // rows].reshape(size, D_MODEL)


def _mesh_pos():
    return lax.axis_index("x"), lax.axis_index("y"), lax.axis_index("c")


def _flip(v, bit):
    return (1 - v) if bit else v


_ANY = pl.BlockSpec(memory_space=pl.ANY)
_MESH = pl.DeviceIdType.MESH


def _remote(src, dst, send_sems, recv_sems, k, to):
    return pltpu.make_async_remote_copy(src_ref=src, dst_ref=dst, send_sem=send_sems.at[k], recv_sem=recv_sems.at[k],
                                        device_id=to, device_id_type=_MESH)


def _sibling_copies(g_ref, got_ref, send_sems, recv_sems):
    x, y, c = _mesh_pos()
    return [_remote(g_ref.at[2 * j + (1 - c)], got_ref.at[j], send_sems, recv_sems, j, (x, y, 1 - c)) for j in range(N_CHIP)]


def _chip_copies(p_ref, land_ref, send_sems, recv_sems):
    x, y, c = _mesh_pos()
    copies = []
    for k in (1, 2, 3):
        tx, ty = _flip(x, k & 2), _flip(y, k & 1)
        copies.append(_remote(p_ref.at[2 * tx + ty], land_ref.at[k - 1], send_sems, recv_sems, k - 1, (tx, ty, c)))
    return copies


class _Comm:
    def __init__(self, ins, out_shapes, sems, make, aliases=None, in_specs=None):
        self.ins, self.out_shapes, self.sems, self.make, self.aliases = list(ins), list(out_shapes), list(sems), make, aliases or {}
        self.in_specs = list(in_specs) if in_specs is not None else [_ANY] * len(self.ins)


def _join_comms(a, b):
    n_i, n_o, n_s = len(a.ins), len(a.out_shapes), len(a.sems)

    def make(cin, cout, sems):
        return a.make(cin[:n_i], cout[:n_o], sems[:n_s]) + b.make(cin[n_i:], cout[n_o:], sems[n_s:])

    aliases = dict(a.aliases)
    aliases.update({n_i + j: n_o + k for j, k in b.aliases.items()})
    return _Comm(a.ins + b.ins, a.out_shapes + b.out_shapes, a.sems + b.sems, make, aliases, a.in_specs + b.in_specs)


def _comm_parts(comm, n_in, n_out):
    if comm is None:
        return [], [], [], [], {}
    alias = {n_in + j: n_out + k for j, k in comm.aliases.items()}
    return comm.ins, comm.in_specs, comm.out_shapes, comm.sems, alias


def _split_refs(refs, n_in, n_out, n_scratch, comm):
    n_ci = len(comm.ins) if comm else 0
    n_co = len(comm.out_shapes) if comm else 0
    cuts, i = [], 0
    for n in (n_in, n_ci, n_out, n_co, n_scratch):
        cuts.append(refs[i:i + n])
        i += n
    return (*cuts, refs[i:])


def _grid_edge(grid, last):
    cond = None
    for d, n in enumerate(grid):
        here = pl.program_id(d) == (n - 1 if last else 0)
        cond = here if cond is None else cond & here
    return cond


def _comm_start(comm, cin, cout, csem, grid):
    if comm is not None:
        @pl.when(_grid_edge(grid, False))
        def _():
            for cp in comm.make(cin, cout, csem):
                cp.start()


def _comm_finish(comm, cin, cout, csem, grid):
    if comm is not None:
        @pl.when(_grid_edge(grid, True))
        def _():
            for cp in comm.make(cin, cout, csem):
                cp.wait()


def gather_first_comm(shard):
    r, w = shard.shape

    def make(cin, cout, sems):
        (x_ref,), (out_ref,), (send_sems, recv_sems, local_sem) = cin, cout, sems
        x, y, c = _mesh_pos()
        mine = out_ref.at[4 * x + 2 * y + c]
        targets = [(x, y, 1 - c), (1 - x, y, c), (x, 1 - y, c), (1 - x, 1 - y, c)]
        return [_remote(x_ref, mine, send_sems, recv_sems, k, to) for k, to in enumerate(targets)] + [
            pltpu.make_async_copy(x_ref, mine, local_sem)]

    return _Comm([shard], [jax.ShapeDtypeStruct((N_DEV, r, w), shard.dtype)],
                 [pltpu.SemaphoreType.DMA((4,)), pltpu.SemaphoreType.DMA((4,)), pltpu.SemaphoreType.DMA], make,
                 in_specs=[_resident(shard.shape)])


def gather_pass_comm(full):
    def make(cin, cout, sems):
        (in_ref,), (out_ref,), (send_sems, recv_sems) = cin, cout, sems
        x, y, c = _mesh_pos()
        copies = []
        for k, (px, py) in enumerate([(1 - x, y), (x, 1 - y), (1 - x, 1 - y)]):
            idx = 4 * px + 2 * py + c
            copies.append(_remote(in_ref.at[idx], out_ref.at[idx], send_sems, recv_sems, k, (x, y, 1 - c)))
        return copies

    return _Comm([full], [jax.ShapeDtypeStruct(full.shape, full.dtype)],
                 [pltpu.SemaphoreType.DMA((3,)), pltpu.SemaphoreType.DMA((3,))], make, aliases={0: 0})


def gather3_first_comm(shard):
    r, w = shard.shape

    def make(cin, cout, sems):
        (x_ref,), (out_ref,), (send_sems, recv_sems, local_sem) = cin, cout, sems
        x, y, c = _mesh_pos()
        mine = out_ref.at[4 * x + 2 * y + c]
        targets = [(x, y, 1 - c), (1 - x, y, c), (x, 1 - y, c)]
        return [_remote(x_ref, mine, send_sems, recv_sems, k, to) for k, to in enumerate(targets)] + [
            pltpu.make_async_copy(x_ref, mine, local_sem)]

    return _Comm([shard], [jax.ShapeDtypeStruct((N_DEV, r, w), shard.dtype)],
                 [pltpu.SemaphoreType.DMA((3,)), pltpu.SemaphoreType.DMA((3,)), pltpu.SemaphoreType.DMA], make,
                 in_specs=[_resident(shard.shape)])


def gather3_second_comm(full):
    def make(cin, cout, sems):
        (in_ref,), (out_ref,), (send_sems, recv_sems) = cin, cout, sems
        x, y, c = _mesh_pos()
        copies = []
        for k, (px, py) in enumerate([(1 - x, y), (x, 1 - y)]):
            idx = 4 * px + 2 * py + c
            copies.append(_remote(in_ref.at[idx], out_ref.at[idx], send_sems, recv_sems, k, (x, y, 1 - c)))
        fx, fy = x * c + (1 - x) * (1 - c), y * (1 - c) + (1 - y) * c
        tx, ty = x * (1 - c) + (1 - x) * c, (1 - y) * (1 - c) + y * c
        idx = 4 * fx + 2 * fy + c
        copies.append(_remote(in_ref.at[idx], out_ref.at[idx], send_sems, recv_sems, 2, (tx, ty, c)))
        return copies

    return _Comm([full], [jax.ShapeDtypeStruct(full.shape, full.dtype)],
                 [pltpu.SemaphoreType.DMA((3,)), pltpu.SemaphoreType.DMA((3,))], make, aliases={0: 0})


def gather3_third_comm(full):
    def make(cin, cout, sems):
        (in_ref,), (out_ref,), (send_sems, recv_sems) = cin, cout, sems
        x, y, c = _mesh_pos()
        idx = 4 * (1 - x) + 2 * (1 - y) + c
        return [_remote(in_ref.at[idx], out_ref.at[idx], send_sems, recv_sems, 0, (x, y, 1 - c))]

    return _Comm([full], [jax.ShapeDtypeStruct(full.shape, full.dtype)],
                 [pltpu.SemaphoreType.DMA((1,)), pltpu.SemaphoreType.DMA((1,))], make, aliases={0: 0})


def run_comm(comm, name):
    c_ins, c_in_specs, c_outs, c_sems, alias = _comm_parts(comm, 0, 0)

    def body(*refs):
        _, cin, _, cout, _, csem = _split_refs(refs, 0, 0, 0, comm)
        for cp in comm.make(cin, cout, csem):
            cp.start()
        for cp in comm.make(cin, cout, csem):
            cp.wait()

    return pl.pallas_call(body, name=name, out_shape=tuple(c_outs), in_specs=c_in_specs, out_specs=tuple([_ANY] * len(c_outs)),
                          scratch_shapes=c_sems, input_output_aliases=alias)(*c_ins)


def scatter_sibling_comm(g):
    _, r, w = g.shape
    return _Comm([g], [jax.ShapeDtypeStruct((N_CHIP, r, w), g.dtype)],
                 [pltpu.SemaphoreType.DMA((N_CHIP,)), pltpu.SemaphoreType.DMA((N_CHIP,))],
                 lambda cin, cout, sems: _sibling_copies(cin[0], cout[0], sems[0], sems[1]))


def scatter_chips_comm(part):
    _, r, w = part.shape
    return _Comm([part], [jax.ShapeDtypeStruct((N_CHIP - 1, r, w), part.dtype)],
                 [pltpu.SemaphoreType.DMA((3,)), pltpu.SemaphoreType.DMA((3,))],
                 lambda cin, cout, sems: _chip_copies(cin[0], cout[0], sems[0], sems[1]))


def exchange_sibling(g):
    _, r, w = g.shape

    def body(g_ref, got_ref, send_sems, recv_sems):
        copies = _sibling_copies(g_ref, got_ref, send_sems, recv_sems)
        for cp in copies:
            cp.start()
        for cp in copies:
            cp.wait()

    return pl.pallas_call(
        body, name="exchange_sibling", out_shape=jax.ShapeDtypeStruct((N_CHIP, r, w), g.dtype),
        in_specs=[_ANY], out_specs=_ANY,
        scratch_shapes=[pltpu.SemaphoreType.DMA((N_CHIP,)), pltpu.SemaphoreType.DMA((N_CHIP,))],
    )(g)


def exchange_chips(part):
    _, r, w = part.shape

    def body(p_ref, land_ref, send_sems, recv_sems):
        copies = _chip_copies(p_ref, land_ref, send_sems, recv_sems)
        for cp in copies:
            cp.start()
        for cp in copies:
            cp.wait()

    return pl.pallas_call(
        body, name="exchange_chips", out_shape=jax.ShapeDtypeStruct((N_CHIP - 1, r, w), part.dtype),
        in_specs=[_ANY], out_specs=_ANY,
        scratch_shapes=[pltpu.SemaphoreType.DMA((3,)), pltpu.SemaphoreType.DMA((3,))],
    )(part)


def allreduce_stats(st_mix, st_prep, st_mlp, st_ple):
    def body(mix_ref, prep_ref, mlp_ref, ple_ref, out_ref, mine, gath, send_sems, recv_sems):
        x, y, c = _mesh_pos()
        me = 4 * x + 2 * y + c
        mine[...] = jnp.zeros_like(mine)
        mine[ST_G_MIX:ST_G_MIX + 1, :] = mix_ref[...]
        mine[ST_G_QA:ST_G_KN + 1, 0:256] = prep_ref[...]
        mine[ST_G_MLP:ST_G_MLP + 1, :] = mlp_ref[...]
        mine[ST_G_PLE:ST_LOSS + 1, :] = ple_ref[...]
        gath[me] = mine[...]
        copies = []
        for k in range(1, N_DEV):
            peer = (_flip(x, k & 4), _flip(y, k & 2), _flip(c, k & 1))
            copies.append(_remote(mine, gath.at[me], send_sems, recv_sems, k - 1, peer))
        for cp in copies:
            cp.start()
        for cp in copies:
            cp.wait()
        acc = gath[0]
        for d in range(1, N_DEV):
            acc = acc + gath[d]
        out_ref[...] = acc

    vm = pl.BlockSpec(memory_space=pltpu.VMEM)
    return pl.pallas_call(
        body, name="allreduce_stats", out_shape=jax.ShapeDtypeStruct((ST_ROWS, D_MODEL), F32),
        in_specs=[vm] * 4, out_specs=vm,
        scratch_shapes=[pltpu.VMEM((ST_ROWS, D_MODEL), F32), pltpu.VMEM((N_DEV, ST_ROWS, D_MODEL), F32),
                        pltpu.SemaphoreType.DMA((N_DEV - 1,)), pltpu.SemaphoreType.DMA((N_DEV - 1,))],
    )(st_mix, st_prep, st_mlp, st_ple)


def adamw_gains(stats, gains):
    c1 = 1.0 - ADAM_B1 ** ADAM_STEP
    c2 = 1.0 - ADAM_B2 ** ADAM_STEP
    n = len(gains)

    def body(st_ref, *refs):
        ins, outs = refs[:3 * n], refs[3 * n:]
        for i, (row, w, _, _) in enumerate(gains):
            width = w.shape[1]
            gv = st_ref[row:row + 1, 0:width]
            mn = ADAM_B1 * ins[3 * i + 1][...] + (1.0 - ADAM_B1) * gv
            vn = ADAM_B2 * ins[3 * i + 2][...] + (1.0 - ADAM_B2) * (gv * gv)
            outs[4 * i][...] = gv
            outs[4 * i + 1][...] = -ADAM_LR * ((mn / c1) / (jnp.sqrt(vn / c2) + ADAM_EPS) + ADAM_WD * ins[3 * i][...])
            outs[4 * i + 2][...] = mn
            outs[4 * i + 3][...] = vn

    vm = pl.BlockSpec(memory_space=pltpu.VMEM)
    flat = [a for (_, w, m, v) in gains for a in (w, m, v)]
    out_shape = tuple(jax.ShapeDtypeStruct(w.shape, F32) for (_, w, _, _) in gains for _ in range(4))
    res = pl.pallas_call(body, name="adamw_gains", out_shape=out_shape, in_specs=[vm] * (1 + 3 * n),
                         out_specs=tuple([vm] * (4 * n)))(stats, *flat)
    return [res[4 * i:4 * i + 4] for i in range(n)]


def _row_tile(r, cap=640):
    return max(d for d in range(16, min(r, cap) + 1, 16) if r % d == 0)


def add_pairs(g, got, core):
    n, r, w = got.shape
    tr = _row_tile(r)

    def body(c_ref, a_ref, b_ref, o_ref):
        o_ref[...] = (a_ref[...].astype(F32) + b_ref[...].astype(F32)).astype(o_ref.dtype)

    spec = pl.BlockSpec((1, tr, w), lambda i, j, c: (i, j, 0))
    return pl.pallas_call(
        body, name="add_pairs", out_shape=jax.ShapeDtypeStruct(got.shape, got.dtype),
        grid_spec=pltpu.PrefetchScalarGridSpec(
            num_scalar_prefetch=1, grid=(n, r // tr),
            in_specs=[pl.BlockSpec((1, tr, w), lambda i, j, c: (2 * i + c[0], j, 0)), spec], out_specs=spec),
        compiler_params=_params(("parallel", "parallel")),
    )(core, g, got)


def sum_chips(part, land, chip):
    _, r, w = part.shape
    tr = _row_tile(r)

    def body(c_ref, p_ref, l_ref, o_ref):
        acc = p_ref[0].astype(F32)
        for s in range(N_CHIP - 1):
            acc = acc + l_ref[s].astype(F32)
        o_ref[...] = acc

    return pl.pallas_call(
        body, name="sum_chips", out_shape=jax.ShapeDtypeStruct((r, w), F32),
        grid_spec=pltpu.PrefetchScalarGridSpec(
            num_scalar_prefetch=1, grid=(r // tr,),
            in_specs=[pl.BlockSpec((1, tr, w), lambda i, c: (c[0], i, 0)), pl.BlockSpec((N_CHIP - 1, tr, w), lambda i, c: (0, i, 0))],
            out_specs=pl.BlockSpec((tr, w), lambda i, c: (i, 0))),
        compiler_params=_params(("parallel",)),
    )(chip, part, land)


def norm_x(x, g_mix, tm, comm=None):
    t = x.shape[0]
    grid = (t // tm,)
    c_ins, c_in_specs, c_outs, c_sems, alias = _comm_parts(comm, 2, 1)

    def body(*refs):
        (x_ref, g_ref), cin, (h_ref,), cout, _, csem = _split_refs(refs, 2, 1, 0, comm)
        _comm_start(comm, cin, cout, csem, grid)
        xv = x_ref[...]
        h_ref[...] = (xv * _rstd(xv, D_MODEL) * g_ref[...]).astype(BF16)
        _comm_finish(comm, cin, cout, csem, grid)

    return pl.pallas_call(
        body, name="norm_x", grid=grid, out_shape=(jax.ShapeDtypeStruct((t, D_MODEL), BF16), *c_outs),
        in_specs=[_rows(tm, D_MODEL), _resident((1, D_MODEL)), *c_in_specs],
        out_specs=(_rows(tm, D_MODEL), *([_ANY] * len(c_outs))),
        scratch_shapes=c_sems, input_output_aliases=alias, compiler_params=_params(("arbitrary",)),
    )(x, g_mix, *c_ins)


def pack_late_weights(w_up, w_down, w_o, w_pg, small, comm=None):
    rows2 = sum(PACK_ROWS[n] for n in PACK_W2)
    rows3 = sum(PACK_ROWS[n] for n in PACK_W3)
    c_ins, c_in_specs, c_outs, c_sems, alias = _comm_parts(comm, 5, 2)
    grid = (1,)

    def body(*refs):
        (up_ref, dn_ref, o_ref, pg_ref, sm_ref), cin, (p2_ref, p3_ref), cout, _, csem = _split_refs(refs, 5, 2, 0, comm)
        _comm_start(comm, cin, cout, csem, grid)
        p2_ref[0:128, :] = o_ref[0].astype(BF16)
        p2_ref[128:256, :] = pg_ref[0].astype(BF16)
        p2_ref[256:rows2, :] = sm_ref[...]
        p3_ref[0:512, :] = up_ref[0].T.astype(BF16)
        p3_ref[512:1024, :] = dn_ref[0].astype(BF16)
        _comm_finish(comm, cin, cout, csem, grid)

    def whole(a):
        nd = a.ndim
        return pl.BlockSpec(a.shape, lambda i: (0,) * nd)

    args = (w_up, w_down, w_o, w_pg, small)
    return pl.pallas_call(
        body, name="pack_late_weights", grid=grid,
        out_shape=(jax.ShapeDtypeStruct((rows2, D_MODEL), BF16), jax.ShapeDtypeStruct((rows3, D_MODEL), BF16), *c_outs),
        in_specs=[*[whole(a) for a in args], *c_in_specs],
        out_specs=(pl.BlockSpec((rows2, D_MODEL), lambda i: (0, 0)), pl.BlockSpec((rows3, D_MODEL), lambda i: (0, 0)),
                   *([_ANY] * len(c_outs))),
        scratch_shapes=c_sems, input_output_aliases=alias, compiler_params=_params(("arbitrary",)),
    )(*args, *c_ins)


def in_proj(h, w_in_t, tm, comm=None):
    t = h.shape[0]
    nc = 512
    half = ZP // 2
    grid = (t // tm,)
    c_ins, c_in_specs, c_outs, c_sems, alias = _comm_parts(comm, 2, 2)

    def body(*refs):
        (h_ref, w_ref), cin, (zg_ref, zs_ref), cout, _, csem = _split_refs(refs, 2, 2, 0, comm)
        _comm_start(comm, cin, cout, csem, grid)
        hv = h_ref[...]
        for cidx in range(half // nc):
            zg_ref[:, cidx * nc:(cidx + 1) * nc] = _dot_nt(hv, w_ref[cidx * nc:(cidx + 1) * nc, :]).astype(BF16)
        for cidx in range(half // nc):
            zs_ref[:, cidx * nc:(cidx + 1) * nc] = _dot_nt(hv, w_ref[half + cidx * nc:half + (cidx + 1) * nc, :])
        _comm_finish(comm, cin, cout, csem, grid)

    return pl.pallas_call(
        body, name="in_proj", grid=grid,
        out_shape=(jax.ShapeDtypeStruct((t, half), BF16), jax.ShapeDtypeStruct((t, half), F32), *c_outs),
        in_specs=[_rows(tm, D_MODEL), _resident((ZP, D_MODEL)), *c_in_specs],
        out_specs=(_rows(tm, half), _rows(tm, half), *([_ANY] * len(c_outs))),
        scratch_shapes=c_sems, input_output_aliases=alias, compiler_params=_params(("arbitrary",)),
    )(h, w_in_t, *c_ins)


def attn_prep(zp, tabs, g_qa, g_kva, g_qn, g_kn, w_qb_t, w_kvb_t, tm, s_len, comm=None):
    t = zp.shape[0]
    nsb = s_len // tm
    scale_a = (QK_NOPE + QK_ROPE) ** -0.5
    scale_b = HD_B ** -0.5

    grid = (t // tm,)
    c_ins, c_in_specs, c_outs, c_sems, alias = _comm_parts(comm, 13, 8)

    def body(*refs):
        ((qb_ref, qlat_ref, kb_ref, vb_ref, ckv_ref, kpe_ref, tab_ref, gqa_ref, gkva_ref, gqn_ref, gkn_ref, wqb_ref,
          wkvb_ref), cin, (qa_o, ka_o, va_o, qb_o, kb_o, vb_o, cq_o, ckvn_o), cout, _, csem) = _split_refs(refs, 13, 8, 0, comm)
        _comm_start(comm, cin, cout, csem, grid)
        ca, s1a, s2a = tab_ref[0], tab_ref[1], tab_ref[2]
        ck = tab_ref[3]
        cb, s1b, s2b = tab_ref[4], tab_ref[5], tab_ref[6]
        ql = qlat_ref[...]
        cq = (ql * _rstd(ql, Q_LORA) * gqa_ref[...]).astype(BF16)
        cq_o[...] = cq
        qa = _dot_nt(cq, wqb_ref[...])
        slabs = [slice(h * HP, (h + 1) * HP) for h in range(H_A)]
        qa_o[...] = jnp.concatenate(
            [(_rope_fwd(qa[:, sl], ca, s1a, s2a) * scale_a).astype(BF16) for sl in slabs], axis=1)
        cr = ckv_ref[...]
        ckv = (cr * _rstd(cr, KV_LORA) * gkva_ref[...]).astype(BF16)
        ckvn_o[...] = ckv
        kva = _dot_nt(ckv, wkvb_ref[...])
        kpe = _rope_fwd(kpe_ref[...], ck, s1a, s2a)
        ka_o[...] = jnp.concatenate([(kva[:, sl] + kpe).astype(BF16) for sl in slabs], axis=1)
        va_o[...] = kva[:, H_A * HP:].astype(BF16)
        gqn, gkn = gqn_ref[...], gkn_ref[...]

        def norm_rope(ref, sl, g, scale):
            xs = ref[:, sl]
            y = _rope_fwd(xs * _rstd(xs, HD_B) * g, cb, s1b, s2b)
            return (y if scale is None else y * scale).astype(BF16)

        qb_o[...] = jnp.concatenate([norm_rope(qb_ref, sl, gqn, scale_b) for sl in slabs], axis=1)
        kb_o[...] = jnp.concatenate([norm_rope(kb_ref, sl, gkn, None) for sl in slabs[:KV_B]], axis=1)
        vb_o[...] = vb_ref[...].astype(BF16)
        _comm_finish(comm, cin, cout, csem, grid)

    def o(width):
        return jax.ShapeDtypeStruct((t, width), BF16)

    return pl.pallas_call(
        body, name="attn_prep", grid=grid,
        out_shape=(o(H_A * HP), o(H_A * HP), o(H_A * HP), o(H_B * HP), o(KV_B * HP), o(KV_B * HP), o(Q_LORA), o(KV_LORA),
                   *c_outs),
        in_specs=[_rows(tm, 1024, 0), _rows(tm, 256, 4), _rows(tm, 256, 5), _rows(tm, 256, 6),
                  _rows(tm, 128, 14), _rows(tm, 128, 15),
                  pl.BlockSpec((7, tm, HP), lambda i: (0, i % nsb, 0)),
                  _resident((1, Q_LORA)), _resident((1, KV_LORA)), _resident((1, HP)), _resident((1, HP)),
                  _resident((H_A * HP, Q_LORA)), _resident((2 * H_A * HP, KV_LORA)), *c_in_specs],
        out_specs=(_rows(tm, H_A * HP), _rows(tm, H_A * HP), _rows(tm, H_A * HP), _rows(tm, H_B * HP),
                   _rows(tm, KV_B * HP), _rows(tm, KV_B * HP), _rows(tm, Q_LORA), _rows(tm, KV_LORA), *([_ANY] * len(c_outs))),
        scratch_shapes=c_sems, input_output_aliases=alias, compiler_params=_params(("arbitrary",)),
    )(zp, zp, zp, zp, zp, zp, tabs, g_qa, g_kva, g_qn, g_kn, w_qb_t, w_kvb_t, *c_ins)


def attn_fwd(q, k, v, n_b, s_len, tq, name, comm=None):
    t = q.shape[0]
    n_h, n_hk = q.shape[1] // HP, k.shape[1] // HP
    grp = n_h // n_hk
    nq = s_len // tq
    sub = min(tq, 256)
    grid = (n_b, n_h, nq)
    c_ins, c_in_specs, c_outs, c_sems, alias = _comm_parts(comm, 3, 2)

    def body(*refs):
        (q_ref, k_ref, v_ref), cin, (o_ref, lse_ref), cout, _, csem = _split_refs(refs, 3, 2, 0, comm)
        _comm_start(comm, cin, cout, csem, grid)
        kv, vv = k_ref[...], v_ref[...]
        for r in range(tq // sub):
            rows = slice(r * sub, (r + 1) * sub)
            s = _dot_nt(q_ref[rows, :], kv)
            m = jnp.max(s, axis=-1, keepdims=True)
            p = jnp.exp(s - m)
            l = jnp.sum(p, axis=-1, keepdims=True)
            o_ref[rows, :] = (_dot_nn(p.astype(BF16), vv) * (1.0 / l)).astype(o_ref.dtype)
            lse_ref[rows, :] = jnp.broadcast_to(m + jnp.log(l), (sub, HP))
        _comm_finish(comm, cin, cout, csem, grid)

    qspec = pl.BlockSpec((tq, HP), lambda b, h, i: (b * nq + i, h))
    kspec = pl.BlockSpec((s_len, HP), lambda b, h, i: (b, h // grp))
    return pl.pallas_call(
        body, name=name, grid=grid,
        out_shape=(jax.ShapeDtypeStruct((t, n_h * HP), BF16), jax.ShapeDtypeStruct((t, n_h * HP), F32), *c_outs),
        in_specs=[qspec, kspec, kspec, *c_in_specs], out_specs=(qspec, qspec, *([_ANY] * len(c_outs))),
        scratch_shapes=c_sems, input_output_aliases=alias,
        compiler_params=_params(("arbitrary", "arbitrary", "arbitrary")),
    )(q, k, v, *c_ins)


def merge_fwd(oa, ob, zp, x, w_oa_t, w_ob_t, wpack, off, tm):
    t = x.shape[0]

    def body(oa_ref, ob_ref, ga_ref, gb_ref, x_ref, woa_ref, wob_ref, wo_ref, x1_o, mg_o, y_o):
        ya = _dot_nt(oa_ref[...], woa_ref[...])
        yb = _dot_nt(ob_ref[...], wob_ref[...])
        y_o[:, 0:D_MODEL] = ya.astype(BF16)
        y_o[:, D_MODEL:2 * D_MODEL] = yb.astype(BF16)
        merged = (jax.nn.sigmoid(ga_ref[...].astype(F32)) * ya + jax.nn.sigmoid(gb_ref[...].astype(F32)) * yb).astype(BF16)
        mg_o[...] = merged
        x1_o[...] = x_ref[...] + _dot_nn(merged, _wrows(wo_ref, 0, D_MODEL))

    return pl.pallas_call(
        body, name="merge_fwd", grid=(t // tm,),
        out_shape=(jax.ShapeDtypeStruct((t, D_MODEL), F32), jax.ShapeDtypeStruct((t, D_MODEL), BF16),
                   jax.ShapeDtypeStruct((t, 2 * D_MODEL), BF16)),
        in_specs=[_rows(tm, H_A * HP), _rows(tm, H_B * HP), _rows(tm, 1024, 0), _rows(tm, 1024, 1), _rows(tm, D_MODEL),
                  _resident((D_MODEL, H_A * HP)), _resident((D_MODEL, H_B * HP)), _packed_weight(128, off["w_o"])],
        out_specs=(_rows(tm, D_MODEL), _rows(tm, D_MODEL), _rows(tm, 2 * D_MODEL)), compiler_params=_params(("parallel",)),
    )(oa, ob, zp, zp, x, w_oa_t, w_ob_t, wpack)


def mlp_fwd(x1, g_mlp, wpack, off, tm):
    t = x1.shape[0]
    fc = 1024

    def body(x_ref, g_ref, wup_ref, wdn_ref, x2_o, u_o):
        xv = x_ref[...]
        h2 = (xv * _rstd(xv, D_MODEL) * g_ref[...]).astype(BF16)
        acc = xv
        for cidx in range(D_FF // fc):
            sl = slice(cidx * fc, (cidx + 1) * fc)
            u = jnp.maximum(_dot_nt(h2, _wrows(wup_ref, cidx * fc, fc)), 0.0)
            u_o[:, sl] = u.astype(BF16)
            acc = acc + _dot_nn((u * u).astype(BF16), _wrows(wdn_ref, cidx * fc, fc))
        x2_o[...] = acc

    return pl.pallas_call(
        body, name="mlp_fwd", grid=(t // tm,),
        out_shape=(jax.ShapeDtypeStruct((t, D_MODEL), F32), jax.ShapeDtypeStruct((t, D_FF), BF16)),
        in_specs=[_rows(tm, D_MODEL), _resident((1, D_MODEL)), _packed_weight(512, off["w_up"]), _packed_weight(512, off["w_down"])],
        out_specs=(_rows(tm, D_MODEL), _rows(tm, D_FF)), compiler_params=_params(("parallel",)),
    )(x1, g_mlp, wpack, wpack)


def ple_loss_bwd(x2, p, tgt, g_ple, g_final, wpack, off, w_ple_t, tm):
    t = x2.shape[0]
    inv_d = 1.0 / D_MODEL

    def body(x2_ref, p_ref, tg_ref, gp_ref, gf_ref, wpg_ref, wple_ref, dx2_o, dt_o, h3_o, dpe_o, st_o, dx2b_o):
        @pl.when(pl.program_id(0) == 0)
        def _():
            st_o[...] = jnp.zeros_like(st_o)

        x2v = x2_ref[...]
        gp, gf = gp_ref[...], gf_ref[...]
        w_pg = _wrows(wpg_ref, 0, D_MODEL)
        r2 = _rstd(x2v, D_MODEL)
        xh2 = x2v * r2
        h3 = (xh2 * gp).astype(BF16)
        h3_o[...] = h3
        gate = jax.nn.sigmoid(_dot_nn(h3, w_pg))
        pe = _dot_nt(p_ref[...].astype(BF16), wple_ref[...])
        x3 = x2v + gate * pe
        r3 = _rstd(x3, D_MODEL)
        xh3 = x3 * r3
        err = xh3 * gf - tg_ref[...]
        dy = err * inv_d
        dx3 = _rms_bwd(dy, xh3, r3, gf, D_MODEL)
        dpe_o[...] = (dx3 * gate).astype(BF16)
        dt = (dx3 * pe * gate * (1.0 - gate)).astype(BF16)
        dt_o[...] = dt
        dh3 = _dot_nt(dt, w_pg)
        dx2 = dx3 + _rms_bwd(dh3, xh2, r2, gp, D_MODEL)
        dx2_o[...] = dx2
        dx2b_o[...] = dx2.astype(BF16)
        st_o[0:1, :] += _colsum(dh3 * xh2)
        st_o[1:2, :] += _colsum(dy * xh3)
        st_o[2:3, :] += _colsum(err * err) * (0.5 * inv_d)

    bf = jax.ShapeDtypeStruct((t, D_MODEL), BF16)
    return pl.pallas_call(
        body, name="ple_loss_bwd", grid=(t // tm,),
        out_shape=(jax.ShapeDtypeStruct((t, D_MODEL), F32), bf, bf, bf, jax.ShapeDtypeStruct((3, D_MODEL), F32), bf),
        in_specs=[_rows(tm, D_MODEL), _rows(tm, PLE_DIM), _rows(tm, D_MODEL), _resident((1, D_MODEL)), _resident((1, D_MODEL)),
                  _packed_weight(128, off["w_ple_gate"]), _resident((D_MODEL, PLE_DIM))],
        out_specs=(_rows(tm, D_MODEL), _rows(tm, D_MODEL), _rows(tm, D_MODEL), _rows(tm, D_MODEL),
                   pl.BlockSpec((3, D_MODEL), lambda i: (0, 0)), _rows(tm, D_MODEL)),
        compiler_params=_params(("arbitrary",)),
    )(x2, p, tgt, g_ple, g_final, wpack, w_ple_t)


def mlp_bwd(dx2, x1, u, g_mlp, wpack, off, tm):
    t = x1.shape[0]
    fc = 1024

    def body(dx2_ref, x1_ref, u_ref, g_ref, wup_ref, wdn_ref, dx1_o, da_o, h2_o, st_o, dx1b_o):
        @pl.when(pl.program_id(0) == 0)
        def _():
            st_o[...] = jnp.zeros_like(st_o)

        d2 = dx2_ref[...]
        d2b = d2.astype(BF16)
        dh2 = jnp.zeros((tm, D_MODEL), F32)
        for cidx in range(D_FF // fc):
            sl = slice(cidx * fc, (cidx + 1) * fc)
            da = (_dot_nt(d2b, _wrows(wdn_ref, cidx * fc, fc)) * (2.0 * u_ref[:, sl].astype(F32))).astype(BF16)
            da_o[:, sl] = da
            dh2 = dh2 + _dot_nn(da, _wrows(wup_ref, cidx * fc, fc))
        xv = x1_ref[...]
        g = g_ref[...]
        r1 = _rstd(xv, D_MODEL)
        xh1 = xv * r1
        h2_o[...] = (xh1 * g).astype(BF16)
        st_o[...] += _colsum(dh2 * xh1)
        dx1 = d2 + _rms_bwd(dh2, xh1, r1, g, D_MODEL)
        dx1_o[...] = dx1
        dx1b_o[...] = dx1.astype(BF16)

    return pl.pallas_call(
        body, name="mlp_bwd", grid=(t // tm,),
        out_shape=(jax.ShapeDtypeStruct((t, D_MODEL), F32), jax.ShapeDtypeStruct((t, D_FF), BF16),
                   jax.ShapeDtypeStruct((t, D_MODEL), BF16), jax.ShapeDtypeStruct((1, D_MODEL), F32),
                   jax.ShapeDtypeStruct((t, D_MODEL), BF16)),
        in_specs=[_rows(tm, D_MODEL), _rows(tm, D_MODEL), _rows(tm, D_FF), _resident((1, D_MODEL)),
                  _packed_weight(512, off["w_up"]), _packed_weight(512, off["w_down"])],
        out_specs=(_rows(tm, D_MODEL), _rows(tm, D_FF), _rows(tm, D_MODEL), pl.BlockSpec((1, D_MODEL), lambda i: (0, 0)),
                   _rows(tm, D_MODEL)),
        compiler_params=_params(("arbitrary",)),
    )(dx2, x1, u, g_mlp, wpack, wpack)


def merge_bwd(dx1b, yab, zp, w_oa_t, w_ob_t, wpack, off, tm, comm=None):
    t = dx1b.shape[0]
    grid = (t // tm,)
    c_ins, c_in_specs, c_outs, c_sems, alias = _comm_parts(comm, 7, 5)

    def body(*refs):
        ((dx1_ref, y_ref, ga_ref, gb_ref, woa_ref, wob_ref, wo_ref), cin,
         (doa_o, dob_o, dg_o, dya_o, dyb_o), cout, _, csem) = _split_refs(refs, 7, 5, 0, comm)
        _comm_start(comm, cin, cout, csem, grid)
        dm = _dot_nt(dx1_ref[...], _wrows(wo_ref, 0, D_MODEL))
        for g_ref, w_ref, do_o, dy_o, col in ((ga_ref, woa_ref, doa_o, dya_o, 0), (gb_ref, wob_ref, dob_o, dyb_o, 1)):
            cols = slice(col * D_MODEL, (col + 1) * D_MODEL)
            sg = jax.nn.sigmoid(g_ref[...].astype(F32))
            dyv = (dm * sg).astype(BF16)
            dy_o[...] = dyv
            dg_o[:, cols] = (dm * y_ref[:, cols].astype(F32) * sg * (1.0 - sg)).astype(BF16)
            do_o[...] = _dot_nn(dyv, w_ref[...]).astype(BF16)
        _comm_finish(comm, cin, cout, csem, grid)

    bf = jax.ShapeDtypeStruct((t, D_MODEL), BF16)
    return pl.pallas_call(
        body, name="merge_bwd", grid=grid,
        out_shape=(bf, bf, jax.ShapeDtypeStruct((t, ZP), BF16), bf, bf, *c_outs),
        in_specs=[_rows(tm, D_MODEL), _rows(tm, 2 * D_MODEL), _rows(tm, 1024, 0), _rows(tm, 1024, 1),
                  _resident((D_MODEL, H_A * HP)), _resident((D_MODEL, H_B * HP)), _packed_weight(128, off["w_o"]), *c_in_specs],
        out_specs=(_rows(tm, D_MODEL), _rows(tm, D_MODEL), _rows(tm, 2 * D_MODEL), _rows(tm, D_MODEL), _rows(tm, D_MODEL),
                   *([_ANY] * len(c_outs))),
        scratch_shapes=c_sems, input_output_aliases=alias,
        compiler_params=_params(("arbitrary",)),
    )(dx1b, yab, zp, zp, w_oa_t, w_ob_t, wpack, *c_ins)


def attn_bwd(q, k, v, do, o, lse, n_b, s_len, tq, name, comm=None):
    t = q.shape[0]
    n_h, n_hk = q.shape[1] // HP, k.shape[1] // HP
    grp = n_h // n_hk
    nq = s_len // tq
    sub = min(tq, 256)
    grid = (n_b, n_hk, grp, nq)
    c_ins, c_in_specs, c_outs, c_sems, alias = _comm_parts(comm, 6, 3)

    def body(*refs):
        ((q_ref, k_ref, v_ref, do_ref, o_ref, lse_ref), cin, (dq_o, dk_o, dv_o), cout, (p_s, ds_s, dk_acc, dv_acc),
         csem) = _split_refs(refs, 6, 3, 4, comm)
        _comm_start(comm, cin, cout, csem, grid)

        @pl.when((pl.program_id(2) == 0) & (pl.program_id(3) == 0))
        def _():
            dk_acc[...] = jnp.zeros_like(dk_acc)
            dv_acc[...] = jnp.zeros_like(dv_acc)

        kv, vv = k_ref[...], v_ref[...]
        for r in range(tq // sub):
            rows = slice(r * sub, (r + 1) * sub)
            qv, dov = q_ref[rows, :], do_ref[rows, :]
            delta = jnp.sum(dov.astype(F32) * o_ref[rows, :].astype(F32), axis=-1, keepdims=True)
            delta_row = jnp.broadcast_to(delta, (sub, HP)).T[0:1, :]
            lse_row = lse_ref[rows, :].T[0:1, :]
            pt = jnp.exp(_dot_nt(kv, qv) - lse_row)
            dst = (pt * (_dot_nt(vv, dov) - delta_row)).astype(BF16)
            p_s[:, rows] = pt.astype(BF16)
            ds_s[:, rows] = dst
            dq_o[rows, :] = _dot_tn(dst, kv).astype(dq_o.dtype)
        dk_acc[...] += _dot_nn(ds_s[...], q_ref[...])
        dv_acc[...] += _dot_nn(p_s[...], do_ref[...])

        @pl.when((pl.program_id(2) == grp - 1) & (pl.program_id(3) == nq - 1))
        def _():
            dk_o[...] = dk_acc[...].astype(dk_o.dtype)
            dv_o[...] = dv_acc[...].astype(dv_o.dtype)

        _comm_finish(comm, cin, cout, csem, grid)

    qspec = pl.BlockSpec((tq, HP), lambda b, hk, g, i: (b * nq + i, hk * grp + g))
    kspec = pl.BlockSpec((s_len, HP), lambda b, hk, g, i: (b, hk))
    return pl.pallas_call(
        body, name=name, grid=grid,
        out_shape=(jax.ShapeDtypeStruct((t, n_h * HP), BF16), jax.ShapeDtypeStruct((t, n_hk * HP), BF16),
                   jax.ShapeDtypeStruct((t, n_hk * HP), BF16), *c_outs),
        in_specs=[qspec, kspec, kspec, qspec, qspec, qspec, *c_in_specs],
        out_specs=(qspec, kspec, kspec, *([_ANY] * len(c_outs))),
        scratch_shapes=[pltpu.VMEM((s_len, tq), BF16), pltpu.VMEM((s_len, tq), BF16),
                        pltpu.VMEM((s_len, HP), F32), pltpu.VMEM((s_len, HP), F32), *c_sems],
        input_output_aliases=alias,
        compiler_params=_params(("arbitrary", "arbitrary", "arbitrary", "arbitrary")),
    )(q, k, v, do, o, lse, *c_ins)


def prep_bwd(dqa, dka, dva, dqb, dkb, dvb, zp, dz, tabs, g_qa, g_kva, g_qn, g_kn, w_qb_t, w_kvb_t, tm, s_len):
    t = zp.shape[0]
    nsb = s_len // tm
    scale_a = (QK_NOPE + QK_ROPE) ** -0.5
    scale_b = HD_B ** -0.5

    def body(dqa_ref, dka_ref, dva_ref, dqb_ref, dkb_ref, dvb_ref, qb_ref, qlat_ref, kb_ref, ckv_ref, tab_ref,
             gqa_ref, gkva_ref, gqn_ref, gkn_ref, wqb_ref, wkvb_ref, _, dz_o, dqap_o, dkva_o, st_o):
        dzq_o, dsm_o = dz_o.at[:, 0:1024], dz_o.at[:, 1024:2048]

        @pl.when(pl.program_id(0) == 0)
        def _():
            st_o[...] = jnp.zeros_like(st_o)

        ca, s1a, s2a = tab_ref[0], tab_ref[1], tab_ref[2]
        ck = tab_ref[3]
        cb, s1b, s2b = tab_ref[4], tab_ref[5], tab_ref[6]
        for h in range(H_A):
            sl = slice(h * HP, (h + 1) * HP)
            dqap_o[:, sl] = _rope_bwd(dqa_ref[:, sl].astype(F32) * scale_a, ca, s1a, s2a).astype(BF16)
        dcq = _dot_nn(dqap_o[...], wqb_ref[...])
        ql = qlat_ref[...]
        rq = _rstd(ql, Q_LORA)
        xh = ql * rq
        gqa = gqa_ref[...]
        st_o[0:1, :] += _colsum(dcq * xh)
        dsm_o[:, 0:256] = _rms_bwd(dcq, xh, rq, gqa, Q_LORA).astype(BF16)
        dkpe = jnp.zeros((tm, HP), F32)
        for h in range(H_A):
            sl = slice(h * HP, (h + 1) * HP)
            dk = dka_ref[:, sl]
            dkpe = dkpe + dk.astype(F32)
            dkva_o[:, sl] = dk.astype(BF16)
        dkva_o[:, H_A * HP:] = dva_ref[...].astype(BF16)
        dsm_o[:, 896:1024] = _rope_bwd(dkpe, ck, s1a, s2a).astype(BF16)
        dckv = _dot_nn(dkva_o[...], wkvb_ref[...])
        cr = ckv_ref[...]
        rk = _rstd(cr, KV_LORA)
        xh = cr * rk
        st_o[1:2, 0:128] += _colsum(dckv * xh)
        dsm_o[:, 768:896] = _rms_bwd(dckv, xh, rk, gkva_ref[...], KV_LORA).astype(BF16)
        gqn, gkn = gqn_ref[...], gkn_ref[...]
        dgq = jnp.zeros((1, HP), F32)
        for h in range(H_B):
            sl = slice(h * HP, (h + 1) * HP)
            dy = _rope_bwd(dqb_ref[:, sl].astype(F32) * scale_b, cb, s1b, s2b)
            xs = qb_ref[:, sl]
            r = _rstd(xs, HD_B)
            xh = xs * r
            dgq = dgq + _colsum(dy * xh)
            dzq_o[:, sl] = _rms_bwd(dy, xh, r, gqn, HD_B).astype(BF16)
        st_o[2:3, 0:128] += dgq
        dgk = jnp.zeros((1, HP), F32)
        for h in range(KV_B):
            sl = slice(h * HP, (h + 1) * HP)
            dy = _rope_bwd(dkb_ref[:, sl].astype(F32), cb, s1b, s2b)
            xs = kb_ref[:, sl]
            r = _rstd(xs, HD_B)
            xh = xs * r
            dgk = dgk + _colsum(dy * xh)
            dsm_o[:, 256 + h * HP:256 + (h + 1) * HP] = _rms_bwd(dy, xh, r, gkn, HD_B).astype(BF16)
        st_o[3:4, 0:128] += dgk
        dsm_o[:, 512:768] = dvb_ref[...].astype(BF16)

    return pl.pallas_call(
        body, name="prep_bwd", grid=(t // tm,),
        out_shape=(jax.ShapeDtypeStruct((t, ZP), BF16), jax.ShapeDtypeStruct((t, 1024), BF16),
                   jax.ShapeDtypeStruct((t, 2048), BF16), jax.ShapeDtypeStruct((4, 256), F32)),
        in_specs=[_rows(tm, 1024), _rows(tm, 1024), _rows(tm, 1024), _rows(tm, 1024), _rows(tm, 256), _rows(tm, 256),
                  _rows(tm, 1024, 0), _rows(tm, 256, 4), _rows(tm, 256, 5), _rows(tm, 128, 14),
                  pl.BlockSpec((7, tm, HP), lambda i: (0, i % nsb, 0)),
                  _resident((1, Q_LORA)), _resident((1, KV_LORA)), _resident((1, HP)), _resident((1, HP)),
                  _resident((H_A * HP, Q_LORA)), _resident((2 * H_A * HP, KV_LORA)), _ANY],
        out_specs=(_rows(tm, 2048, 1), _rows(tm, 1024), _rows(tm, 2048), pl.BlockSpec((4, 256), lambda i: (0, 0))),
        input_output_aliases={17: 0}, compiler_params=_params(("arbitrary",)),
    )(dqa, dka, dva, dqb, dkb, dvb, zp, zp, zp, zp, tabs, g_qa, g_kva, g_qn, g_kn, w_qb_t, w_kvb_t, dz)


def in_bwd(dz, x, dx1, g_mix, w_in_t, tm, comm=None):
    t = x.shape[0]
    grid = (t // tm,)
    c_ins, c_in_specs, c_outs, c_sems, alias = _comm_parts(comm, 5, 2)

    def body(*refs):
        (dz_ref, x_ref, dx1_ref, g_ref, w_ref), cin, (dx_o, st_o), cout, _, csem = _split_refs(refs, 5, 2, 0, comm)
        _comm_start(comm, cin, cout, csem, grid)

        @pl.when(pl.program_id(0) == 0)
        def _():
            st_o[...] = jnp.zeros_like(st_o)

        dh = _dot_nn(dz_ref[...], w_ref[...])
        xv = x_ref[...]
        g = g_ref[...]
        r = _rstd(xv, D_MODEL)
        xh = xv * r
        st_o[...] += _colsum(dh * xh)
        dx_o[...] = dx1_ref[...] + _rms_bwd(dh, xh, r, g, D_MODEL)
        _comm_finish(comm, cin, cout, csem, grid)

    return pl.pallas_call(
        body, name="in_bwd", grid=grid,
        out_shape=(jax.ShapeDtypeStruct((t, D_MODEL), F32), jax.ShapeDtypeStruct((1, D_MODEL), F32), *c_outs),
        in_specs=[_rows(tm, ZP), _rows(tm, D_MODEL), _rows(tm, D_MODEL),
                  _resident((1, D_MODEL)), _resident((ZP, D_MODEL)), *c_in_specs],
        out_specs=(_rows(tm, D_MODEL), pl.BlockSpec((1, D_MODEL), lambda i: (0, 0)), *([_ANY] * len(c_outs))),
        scratch_shapes=c_sems, input_output_aliases=alias,
        compiler_params=_params(("arbitrary",)),
    )(dz, x, dx1, g_mix, w_in_t, *c_ins)


def matmul_tn(a, b, name, square_a=False):
    t, m = a.shape
    n = b.shape[1]
    bm = min(m, 512)
    tk = min(t, 4096)

    def body(a_ref, b_ref, o_ref):
        @pl.when(pl.program_id(1) == 0)
        def _():
            o_ref[...] = jnp.zeros_like(o_ref)

        av = a_ref[...]
        if square_a:
            av = (av.astype(F32) * av.astype(F32))
        o_ref[...] += _dot_tn(av.astype(BF16), b_ref[...].astype(BF16))

    return pl.pallas_call(
        body, name=name, grid=(m // bm, t // tk), out_shape=jax.ShapeDtypeStruct((m, n), F32),
        in_specs=[pl.BlockSpec((tk, bm), lambda i, kk: (kk, i)), pl.BlockSpec((tk, n), lambda i, kk: (kk, 0))],
        out_specs=pl.BlockSpec((bm, n), lambda i, kk: (i, 0)),
        compiler_params=_params(("parallel", "arbitrary")),
    )(a, b)


def matmul_tn_packed(a, b, name, rows, row_off, total_rows, buf=None, square_a=False):
    t, m = a.shape
    n = b.shape[1]
    pd = max(1, 512 // rows)
    bm = pd * rows
    tk = min(t, 4096)
    nk = t // tk

    def body(a_ref, b_ref, *rest):
        o_ref, acc = rest[-2], rest[-1]

        @pl.when(pl.program_id(1) == 0)
        def _():
            acc[...] = jnp.zeros_like(acc)

        av = a_ref[...]
        if square_a:
            av = (av.astype(F32) * av.astype(F32))
        acc[...] += _dot_tn(av.astype(BF16), b_ref[...].astype(BF16))

        @pl.when(pl.program_id(1) == nk - 1)
        def _():
            o_ref[...] = acc[...].reshape(pd, rows, n).astype(o_ref.dtype)

    in_specs = [pl.BlockSpec((tk, bm), lambda i, kk: (kk, i)), pl.BlockSpec((tk, n), lambda i, kk: (kk, 0))]
    args = [a, b]
    if buf is not None:
        in_specs.append(_ANY)
        args.append(buf)
    return pl.pallas_call(
        body, name=name, grid=(m // bm, nk), out_shape=jax.ShapeDtypeStruct((N_DEV, total_rows, n), BF16),
        in_specs=in_specs, out_specs=pl.BlockSpec((pd, rows, n), lambda i, kk: (i, row_off // rows, 0)),
        scratch_shapes=[pltpu.VMEM((bm, n), F32)], input_output_aliases={2: 0} if buf is not None else {},
        compiler_params=_params(("parallel", "arbitrary")),
    )(*args)


def adamw(w, g, m, v, name, g_transposed=False):
    _, r, c = w.shape
    tr = 256 if (not g_transposed and r > 256 and r % 256 == 0) else r
    c1 = 1.0 - ADAM_B1 ** ADAM_STEP
    c2 = 1.0 - ADAM_B2 ** ADAM_STEP

    def body(w_ref, g_ref, m_ref, v_ref, g_o, d_o, m_o, v_o):
        gv = g_ref[...].T if g_transposed else g_ref[...]
        mn = ADAM_B1 * m_ref[0] + (1.0 - ADAM_B1) * gv
        vn = ADAM_B2 * v_ref[0] + (1.0 - ADAM_B2) * (gv * gv)
        g_o[0] = gv
        m_o[0] = mn
        v_o[0] = vn
        d_o[0] = -ADAM_LR * ((mn / c1) / (jnp.sqrt(vn / c2) + ADAM_EPS) + ADAM_WD * w_ref[0])

    spec = pl.BlockSpec((1, tr, c), lambda i: (0, i, 0))
    gspec = pl.BlockSpec((c, r), lambda i: (0, 0)) if g_transposed else pl.BlockSpec((tr, c), lambda i: (i, 0))
    shp = jax.ShapeDtypeStruct((1, r, c), F32)
    return pl.pallas_call(
        body, name=name, grid=(r // tr,), out_shape=(shp,) * 4, in_specs=[spec, gspec, spec, spec], out_specs=(spec,) * 4,
        compiler_params=_params(("parallel",)),
    )(w, g, m, v)


def _rope_tables(s_len):
    def angles(pos, dim):
        inv = np.float32(ROPE_THETA) ** (-np.arange(0, dim, 2, dtype=np.float32) / np.float32(dim))
        return pos.astype(np.float32)[:, None] * inv[None, :]

    tpos = np.arange(s_len)
    a1 = angles(tpos, QK_ROPE)
    ar = angles(tpos // GRID_W, HD_B // 2)
    ac = angles(tpos % GRID_W, HD_B // 2)
    z16 = np.zeros((s_len, 16), np.float32)
    z32 = np.zeros((s_len, 32), np.float32)
    z64 = np.zeros((s_len, 64), np.float32)
    one64 = np.ones((s_len, 64), np.float32)
    c1, s1 = np.cos(a1), np.sin(a1)
    ca = np.concatenate([one64, c1, c1, z32], axis=1)
    ck = np.concatenate([z64, c1, c1, z32], axis=1)
    s1a = np.concatenate([z64, -s1, z16, z32], axis=1)
    s2a = np.concatenate([z64, z16, s1, z32], axis=1)
    cr, sr, cc, sc = np.cos(ar), np.sin(ar), np.cos(ac), np.sin(ac)
    cb = np.concatenate([cr, cr, cc, cc, z64], axis=1)
    s1b = np.concatenate([-sr, z16, -sc, z16, z64], axis=1)
    s2b = np.concatenate([z16, sr, z16, sc, z64], axis=1)
    return jnp.asarray(np.stack([ca, s1a, s2a, ck, cb, s1b, s2b]).astype(np.float32))


def _pad_heads(a, n_heads, axis):
    shp = a.shape
    a = a.reshape(shp[:axis] + (n_heads, shp[axis] // n_heads) + shp[axis + 1:])
    pad = [(0, 0)] * a.ndim
    pad[axis + 1] = (0, HP - a.shape[axis + 1])
    a = jnp.pad(a, pad)
    return a.reshape(shp[:axis] + (n_heads * HP,) + shp[axis + 1:])


def _unpad_heads(a, n_heads, width, axis):
    shp = a.shape
    a = a.reshape(shp[:axis] + (n_heads, HP) + shp[axis + 1:])
    a = lax.slice_in_dim(a, 0, width, axis=axis + 1)
    return a.reshape(shp[:axis] + (n_heads * width,) + shp[axis + 1:])


def _pack_rows(blocks, names):
    parts = []
    for name in names:
        b = blocks[name]
        padr = PACK_ROWS[name] - b.shape[-2]
        if padr:
            b = jnp.pad(b, [(0, 0)] * (b.ndim - 2) + [(0, padr), (0, 0)])
        parts.append(b)
    return jnp.concatenate(parts, axis=parts[0].ndim - 2)


def _expand_w_in(wt):
    z64 = jnp.zeros((64, D_MODEL), wt.dtype)
    z32 = jnp.zeros((32, D_MODEL), wt.dtype)
    return jnp.concatenate([
        wt[1184:2208], wt[2208:3232], _pad_heads(wt[416:928], H_B, 0), wt[0:256],
        _pad_heads(wt[928:1056], KV_B, 0), _pad_heads(wt[1056:1184], KV_B, 0), wt[256:384],
        z64, wt[384:416], z32], axis=0)


def _collapse_w_in(dw):
    dg, dq, ds = dw[0:2048], dw[2048:3072], dw[3072:4096]
    return jnp.concatenate([
        ds[0:256], ds[768:896], ds[960:992], _unpad_heads(dq, H_B, HD_B, 0), _unpad_heads(ds[256:512], KV_B, HD_B, 0),
        _unpad_heads(ds[512:768], KV_B, HD_B, 0), dg], axis=0)


def kernel(x, p, g_mix, w_in, g_qa, w_qb, g_kva, w_kvb, g_qn, g_kn, w_oa, w_ob, w_o, g_mlp, w_up, w_down, g_ple, w_ple_gate, w_ple, g_final, loss_target, m_g_mix, m_w_in, m_g_qa, m_w_qb, m_g_kva, m_w_kvb, m_g_qn, m_g_kn, m_w_oa, m_w_ob, m_w_o, m_g_mlp, m_w_up, m_w_down, m_g_ple, m_w_ple_gate, m_w_ple, m_g_final, v_g_mix, v_w_in, v_g_qa, v_w_qb, v_g_kva, v_w_kvb, v_g_qn, v_g_kn, v_w_oa, v_w_ob, v_w_o, v_g_mlp, v_w_up, v_w_down, v_g_ple, v_w_ple_gate, v_w_ple, v_g_final):
    n_b, s_len, _ = x.shape
    t = n_b * s_len
    tm = min(512, s_len)
    tq_f = min(2048, s_len)
    tq_b = min(2048, s_len)

    mats = dict(w_in=(w_in, m_w_in, v_w_in), w_qb=(w_qb, m_w_qb, v_w_qb), w_kvb=(w_kvb, m_w_kvb, v_w_kvb),
                w_oa=(w_oa, m_w_oa, v_w_oa), w_ob=(w_ob, m_w_ob, v_w_ob), w_o=(w_o, m_w_o, v_w_o),
                w_up=(w_up, m_w_up, v_w_up), w_down=(w_down, m_w_down, v_w_down),
                w_ple_gate=(w_ple_gate, m_w_ple_gate, v_w_ple_gate), w_ple=(w_ple, m_w_ple, v_w_ple))
    col_sharded = ("w_in", "w_qb", "w_kvb", "w_oa", "w_ob", "w_up", "w_ple")

    blocks = {}
    for name in PACK_W1 + ("w_oa", "w_ob", "w_ple"):
        blocks[name] = mats[name][0][0].T.reshape(-1, D_MODEL).astype(BF16)
    off_w1, _ = _pack_offsets(PACK_W1)
    off_w2, _ = _pack_offsets(PACK_W2)
    off_w3, _ = _pack_offsets(PACK_W3)
    xf = x.reshape(t, D_MODEL)
    h, full1 = norm_x(xf, g_mix, tm, comm=gather3_first_comm(_pack_rows(blocks, PACK_W1)))
    pack2, pack3, full1 = pack_late_weights(w_up, w_down, w_o, w_ple_gate, _pack_rows(blocks, ("w_oa", "w_ob", "w_ple")),
                                            comm=gather3_second_comm(full1))
    (full1,) = run_comm(gather3_third_comm(full1), "gather_last_hop")

    def gathered(full, offs, name, rows, width):
        return full[:, offs[name]:offs[name] + rows].reshape(-1, width)

    w_in_t = _expand_w_in(gathered(full1, off_w1, "w_in", 404, D_MODEL))
    w_qb_t = _pad_heads(gathered(full1, off_w1, "w_qb", 24, Q_LORA), H_A, 0)
    wkvb = gathered(full1, off_w1, "w_kvb", 16, KV_LORA).reshape(H_A, 2, 64, KV_LORA)
    w_kvb_t = jnp.concatenate([_pad_heads(wkvb[:, 0].reshape(-1, KV_LORA), H_A, 0),
                               _pad_heads(wkvb[:, 1].reshape(-1, KV_LORA), H_A, 0)], axis=0)

    tabs = _rope_tables(s_len)
    g_qn_p = jnp.pad(g_qn, ((0, 0), (0, HP - HD_B)))
    g_kn_p = jnp.pad(g_kn, ((0, 0), (0, HP - HD_B)))
    pf = p.reshape(t, PLE_DIM)
    tgt = loss_target.reshape(t, D_MODEL)

    zg, zs, full2 = in_proj(h, w_in_t, tm, comm=gather3_first_comm(pack2))
    qa, ka, va, qb, kb, vb, cq, ckv, full2 = attn_prep(zs, tabs, g_qa, g_kva, g_qn_p, g_kn_p, w_qb_t, w_kvb_t, tm, s_len,
                                                       comm=gather3_second_comm(full2))
    oa, lse_a, full3, full2 = attn_fwd(qa, ka, va, n_b, s_len, tq_f, "attn_a_fwd",
                                       comm=_join_comms(gather_first_comm(pack3), gather3_third_comm(full2)))
    ob, lse_b, full3 = attn_fwd(qb, kb, vb, n_b, s_len, tq_f, "attn_b_fwd", comm=gather_pass_comm(full3))
    w_oa_t = _pad_heads(gathered(full2, off_w2, "w_oa", 64, H_A * V_DIM_A), H_A, 1)
    w_ob_t = _pad_heads(gathered(full2, off_w2, "w_ob", 64, H_B * HD_B), H_B, 1)
    w_ple_t = gathered(full2, off_w2, "w_ple", 32, PLE_DIM)
    x1, merged, yab = merge_fwd(oa, ob, zg, xf, w_oa_t, w_ob_t, full2, off_w2, tm)
    x2, u = mlp_fwd(x1, g_mlp, full3, off_w3, tm)
    dx2, dt, h3, dpe, st_ple, dx2b = ple_loss_bwd(x2, pf, tgt, g_ple, g_final.reshape(1, D_MODEL), full2, off_w2, w_ple_t, tm)
    dx1, da, h2, st_mlp, dx1b = mlp_bwd(dx2, x1, u, g_mlp, full3, off_w3, tm)

    core = lax.axis_index("c").astype(jnp.int32).reshape(1)
    chip = (2 * lax.axis_index("x") + lax.axis_index("y")).astype(jnp.int32).reshape(1)

    def packed(gblocks, names):
        return _pack_rows({n: gblocks[n].reshape(N_DEV, -1, D_MODEL).astype(BF16) for n in names}, names)

    off_g1, rows_g1 = _pack_offsets(PACK_G1)
    gpack1 = matmul_tn_packed(da, h2, "gw_up", 512, off_g1["w_up"], rows_g1)
    gpack1 = matmul_tn_packed(u, dx2b, "gw_down", 512, off_g1["w_down"], rows_g1, buf=gpack1, square_a=True)
    gpack1 = matmul_tn_packed(h3, dt, "gw_pg", 128, off_g1["w_ple_gate"], rows_g1, buf=gpack1)
    gple = matmul_tn(dpe, pf, "gw_ple").reshape(N_DEV, -1, D_MODEL).astype(BF16)
    gpack1 = lax.dynamic_update_slice(gpack1, gple, (0, off_g1["w_ple"], 0))
    doa, dob, dz, dya, dyb, got1 = merge_bwd(dx1b, yab, zg, w_oa_t, w_ob_t, full2, off_w2, tm,
                                               comm=scatter_sibling_comm(gpack1))
    part1 = add_pairs(gpack1, got1, core)

    off_g2, rows_g2 = _pack_offsets(PACK_G2)
    g2 = dict(w_oa=_unpad_heads(matmul_tn(dya, oa, "gw_oa"), H_A, V_DIM_A, 1),
              w_ob=_unpad_heads(matmul_tn(dyb, ob, "gw_ob"), H_B, HD_B, 1))
    gpack2 = matmul_tn_packed(merged, dx1b, "gw_o", 128, off_g2["w_o"], rows_g2)
    gpack2 = lax.dynamic_update_slice(gpack2, packed(g2, ("w_oa", "w_ob")), (0, off_g2["w_oa"], 0))
    dqa, dka, dva, land1, got2 = attn_bwd(qa, ka, va, doa, oa, lse_a, n_b, s_len, tq_b, "attn_a_bwd",
                                          comm=_join_comms(scatter_chips_comm(part1), scatter_sibling_comm(gpack2)))
    gshard1 = sum_chips(part1, land1, chip)
    part2 = add_pairs(gpack2, got2, core)
    dqb, dkb, dvb, land2 = attn_bwd(qb, kb, vb, dob, ob, lse_b, n_b, s_len, tq_b, "attn_b_bwd", comm=scatter_chips_comm(part2))
    gshard2 = sum_chips(part2, land2, chip)
    dz, dqap, dkva, st_prep = prep_bwd(dqa, dka, dva, dqb, dkb, dvb, zs, dz, tabs, g_qa, g_kva, g_qn_p, g_kn_p,
                                       w_qb_t, w_kvb_t, tm, s_len)

    gkv = matmul_tn(dkva, ckv, "gw_kvb")
    g3 = dict(
        w_in=_collapse_w_in(matmul_tn(dz, h, "gw_in")),
        w_qb=_unpad_heads(matmul_tn(dqap, cq, "gw_qb"), H_A, QK_NOPE + QK_ROPE, 0),
        w_kvb=jnp.stack([_unpad_heads(gkv[:H_A * HP], H_A, 64, 0).reshape(H_A, 64, KV_LORA),
                         _unpad_heads(gkv[H_A * HP:], H_A, 64, 0).reshape(H_A, 64, KV_LORA)], axis=1))
    gpack3 = packed(g3, PACK_G3)
    part3 = add_pairs(gpack3, exchange_sibling(gpack3), core)
    grad_x, st_mix, land3 = in_bwd(dz, xf, dx1, g_mix, w_in_t, tm, comm=scatter_chips_comm(part3))
    gshard3 = sum_chips(part3, land3, chip)
    off_g3, _ = _pack_offsets(PACK_G3)
    shards = {n: (gshard1, off_g1[n]) for n in PACK_G1}
    shards.update({n: (gshard2, off_g2[n]) for n in PACK_G2})
    shards.update({n: (gshard3, off_g3[n]) for n in PACK_G3})

    stats = allreduce_stats(st_mix, st_prep, st_mlp, st_ple)
    loss = jnp.sum(stats[ST_LOSS])

    out_g, out_d, out_m, out_v = {}, {}, {}, {}
    for name, (w, m, v) in mats.items():
        gshard, off = shards[name]
        r, c = w.shape[1:]
        if name in col_sharded:
            g2 = gshard[off:off + (r * c) // D_MODEL].reshape(c, r)
            if r % 128 == 0 and c % 128 == 0:
                res = adamw(w, g2, m, v, "adamw_" + name, g_transposed=True)
            else:
                res = adamw(w[0].T[None], g2, m[0].T[None], v[0].T[None], "adamw_" + name)
                res = tuple(a[0].T[None] for a in res)
        else:
            res = adamw(w, gshard[off:off + r], m, v, "adamw_" + name)
        out_g[name], out_d[name], out_m[name], out_v[name] = res

    gains = (("g_mix", g_mix, m_g_mix, v_g_mix, ST_G_MIX), ("g_qa", g_qa, m_g_qa, v_g_qa, ST_G_QA),
             ("g_kva", g_kva, m_g_kva, v_g_kva, ST_G_KVA), ("g_qn", g_qn, m_g_qn, v_g_qn, ST_G_QN),
             ("g_kn", g_kn, m_g_kn, v_g_kn, ST_G_KN), ("g_mlp", g_mlp, m_g_mlp, v_g_mlp, ST_G_MLP),
             ("g_ple", g_ple, m_g_ple, v_g_ple, ST_G_PLE), ("g_final", g_final, m_g_final, v_g_final, ST_G_FINAL))
    res = adamw_gains(stats, [(r_, w.reshape(1, -1), m.reshape(1, -1), v.reshape(1, -1)) for _, w, m, v, r_ in gains])
    for (name, w, _, _, _), (gg, gd, gm, gv) in zip(gains, res):
        out_g[name], out_d[name], out_m[name], out_v[name] = (a.reshape(w.shape) for a in (gg, gd, gm, gv))

    order = ("g_mix", "w_in", "g_qa", "w_qb", "g_kva", "w_kvb", "g_qn", "g_kn", "w_oa", "w_ob", "w_o", "g_mlp",
             "w_up", "w_down", "g_ple", "w_ple_gate", "w_ple", "g_final")
    return (loss, grad_x.reshape(x.shape), *[out_g[n] for n in order], *[out_d[n] for n in order],
            *[out_m[n] for n in order], *[out_v[n] for n in order])
```

```python
import numpy as np
import jax
import jax.numpy as jnp
from jax import lax
from jax.experimental import pallas as pl
from jax.experimental.pallas import tpu as pltpu

F32 = jnp.float32
BF16 = jnp.bfloat16

D_MODEL = 1024
EPS = 1e-6
ROPE_THETA = 10000.0
GRID_W = 64
H_A = 8
QK_NOPE = 64
QK_ROPE = 32
V_DIM_A = 64
Q_LORA = 256
KV_LORA = 128
H_B = 8
KV_B = 2
HD_B = 64
D_FF = 4 * D_MODEL
PLE_DIM = 256
HP = 128
ZP = 4096
N_DEV = 8
N_CHIP = 4

ADAM_LR = 0.001
ADAM_B1 = 0.9
ADAM_B2 = 0.999
ADAM_EPS = 1e-08
ADAM_WD = 0.01
ADAM_STEP = 10

VMEM_LIMIT = 52 * 1024 * 1024

PACK_ROWS = dict(w_in=416, w_qb=32, w_kvb=16, w_oa=64, w_ob=64, w_o=128, w_up=512, w_down=512, w_ple_gate=128, w_ple=32)
PACK_W1 = ("w_in", "w_qb", "w_kvb")
PACK_W2 = ("w_o", "w_ple_gate", "w_oa", "w_ob", "w_ple")
PACK_W3 = ("w_up", "w_down")
PACK_G1 = ("w_up", "w_down", "w_ple_gate", "w_ple")
PACK_G2 = ("w_o", "w_oa", "w_ob")
PACK_G3 = ("w_in", "w_qb", "w_kvb")


def _pack_offsets(names):
    off, o = {}, 0
    for n in names:
        off[n] = o
        o += PACK_ROWS[n]
    return off, o

ST_G_MIX, ST_G_QA, ST_G_KVA, ST_G_QN, ST_G_KN, ST_G_MLP, ST_G_PLE, ST_G_FINAL, ST_LOSS = range(9)
ST_ROWS = 16


def _dot_nn(a, b):
    return lax.dot_general(a, b, (((1,), (0,)), ((), ())), preferred_element_type=F32)


def _dot_nt(a, b):
    return lax.dot_general(a, b, (((1,), (1,)), ((), ())), preferred_element_type=F32)


def _dot_tn(a, b):
    return lax.dot_general(a, b, (((0,), (0,)), ((), ())), preferred_element_type=F32)


def _rstd(x, n):
    return lax.rsqrt(jnp.sum(x * x, axis=-1, keepdims=True) * (1.0 / n) + EPS)


def _rms_bwd(dy, xh, r, g, n):
    dxh = dy * g
    return r * (dxh - xh * (jnp.sum(dxh * xh, axis=-1, keepdims=True) * (1.0 / n)))


def _rope_fwd(x, c, s1, s2):
    return x * c + pltpu.roll(x, HP - 16, 1) * s1 + pltpu.roll(x, 16, 1) * s2


def _rope_bwd(d, c, s1, s2):
    return d * c + pltpu.roll(d * s1, 16, 1) + pltpu.roll(d * s2, HP - 16, 1)


def _colsum(v):
    return jnp.sum(v, axis=0, keepdims=True)


def _params(sem=None, vmem=VMEM_LIMIT):
    return pltpu.CompilerParams(dimension_semantics=sem, vmem_limit_bytes=vmem)


def _resident(shape):
    nd = len(shape)
    return pl.BlockSpec(shape, lambda *_: (0,) * nd, pipeline_mode=pl.Buffered(1))


def _rows(tm, width, col=0):
    return pl.BlockSpec((tm, width), lambda i: (i, col))


def _packed_weight(rows, off):
    return pl.BlockSpec((N_DEV, rows, D_MODEL), lambda *_: (0, off // rows, 0), pipeline_mode=pl.Buffered(1))


def _wrows(ref, start, size):
    rows = ref.shape[1]
    return ref[start // rows:(start + size) // rows].reshape(size, D_MODEL)


def _mesh_pos():
    return lax.axis_index("x"), lax.axis_index("y"), lax.axis_index("c")


def _flip(v, bit):
    return (1 - v) if bit else v


_ANY = pl.BlockSpec(memory_space=pl.ANY)
_MESH = pl.DeviceIdType.MESH


def _remote(src, dst, send_sems, recv_sems, k, to):
    return pltpu.make_async_remote_copy(src_ref=src, dst_ref=dst, send_sem=send_sems.at[k], recv_sem=recv_sems.at[k],
                                        device_id=to, device_id_type=_MESH)


def _sibling_copies(g_ref, got_ref, send_sems, recv_sems):
    x, y, c = _mesh_pos()
    return [_remote(g_ref.at[2 * j + (1 - c)], got_ref.at[j], send_sems, recv_sems, j, (x, y, 1 - c)) for j in range(N_CHIP)]


def _chip_copies(p_ref, land_ref, send_sems, recv_sems):
    x, y, c = _mesh_pos()
    copies = []
    for k in (1, 2, 3):
        tx, ty = _flip(x, k & 2), _flip(y, k & 1)
        copies.append(_remote(p_ref.at[2 * tx + ty], land_ref.at[k - 1], send_sems, recv_sems, k - 1, (tx, ty, c)))
    return copies


class _Comm:
    def __init__(self, ins, out_shapes, sems, make, aliases=None, make_tail=None):
        self.ins, self.out_shapes, self.sems, self.make, self.aliases = list(ins), list(out_shapes), list(sems), make, aliases or {}
        self.make_tail = make_tail


def _join_comms(a, b):
    assert a.make_tail is None and b.make_tail is None
    n_i, n_o, n_s = len(a.ins), len(a.out_shapes), len(a.sems)

    def make(cin, cout, sems):
        return a.make(cin[:n_i], cout[:n_o], sems[:n_s]) + b.make(cin[n_i:], cout[n_o:], sems[n_s:])

    aliases = dict(a.aliases)
    aliases.update({n_i + j: n_o + k for j, k in b.aliases.items()})
    return _Comm(a.ins + b.ins, a.out_shapes + b.out_shapes, a.sems + b.sems, make, aliases)


def _comm_parts(comm, n_in, n_out):
    if comm is None:
        return [], [], [], [], {}
    alias = {n_in + j: n_out + k for j, k in comm.aliases.items()}
    return comm.ins, [_ANY] * len(comm.ins), comm.out_shapes, comm.sems, alias


def _split_refs(refs, n_in, n_out, n_scratch, comm):
    n_ci = len(comm.ins) if comm else 0
    n_co = len(comm.out_shapes) if comm else 0
    cuts, i = [], 0
    for n in (n_in, n_ci, n_out, n_co, n_scratch):
        cuts.append(refs[i:i + n])
        i += n
    return (*cuts, refs[i:])


def _grid_edge(grid, last):
    cond = None
    for d, n in enumerate(grid):
        here = pl.program_id(d) == (n - 1 if last else 0)
        cond = here if cond is None else cond & here
    return cond


def _comm_start(comm, cin, cout, csem, grid):
    if comm is not None:
        @pl.when(_grid_edge(grid, False))
        def _():
            for cp in comm.make(cin, cout, csem):
                cp.start()


def _comm_finish(comm, cin, cout, csem, grid):
    if comm is not None:
        @pl.when(_grid_edge(grid, True))
        def _():
            for cp in comm.make(cin, cout, csem):
                cp.wait()
            if comm.make_tail is not None:
                tail = comm.make_tail(cin, cout, csem)
                for cp in tail:
                    cp.start()
                for cp in tail:
                    cp.wait()


def gather_first_comm(shard):
    r, w = shard.shape

    def make(cin, cout, sems):
        (x_ref,), (out_ref,), (send_sems, recv_sems, local_sem) = cin, cout, sems
        x, y, c = _mesh_pos()
        mine = out_ref.at[4 * x + 2 * y + c]
        targets = [(x, y, 1 - c), (1 - x, y, c), (x, 1 - y, c), (1 - x, 1 - y, c)]
        return [_remote(x_ref, mine, send_sems, recv_sems, k, to) for k, to in enumerate(targets)] + [
            pltpu.make_async_copy(x_ref, mine, local_sem)]

    return _Comm([shard], [jax.ShapeDtypeStruct((N_DEV, r, w), shard.dtype)],
                 [pltpu.SemaphoreType.DMA((4,)), pltpu.SemaphoreType.DMA((4,)), pltpu.SemaphoreType.DMA], make)


def gather_pass_comm(full):
    def make(cin, cout, sems):
        (in_ref,), (out_ref,), (send_sems, recv_sems) = cin, cout, sems
        x, y, c = _mesh_pos()
        copies = []
        for k, (px, py) in enumerate([(1 - x, y), (x, 1 - y), (1 - x, 1 - y)]):
            idx = 4 * px + 2 * py + c
            copies.append(_remote(in_ref.at[idx], out_ref.at[idx], send_sems, recv_sems, k, (x, y, 1 - c)))
        return copies

    return _Comm([full], [jax.ShapeDtypeStruct(full.shape, full.dtype)],
                 [pltpu.SemaphoreType.DMA((3,)), pltpu.SemaphoreType.DMA((3,))], make, aliases={0: 0})


def gather3_first_comm(shard):
    r, w = shard.shape

    def make(cin, cout, sems):
        (x_ref,), (out_ref,), (send_sems, recv_sems, local_sem) = cin, cout, sems
        x, y, c = _mesh_pos()
        mine = out_ref.at[4 * x + 2 * y + c]
        targets = [(x, y, 1 - c), (1 - x, y, c), (x, 1 - y, c)]
        return [_remote(x_ref, mine, send_sems, recv_sems, k, to) for k, to in enumerate(targets)] + [
            pltpu.make_async_copy(x_ref, mine, local_sem)]

    return _Comm([shard], [jax.ShapeDtypeStruct((N_DEV, r, w), shard.dtype)],
                 [pltpu.SemaphoreType.DMA((3,)), pltpu.SemaphoreType.DMA((3,)), pltpu.SemaphoreType.DMA], make)


def gather3_second_comm(full, then_third=False):
    def make(cin, cout, sems):
        (in_ref,), (out_ref,), (send_sems, recv_sems) = cin, cout, sems
        x, y, c = _mesh_pos()
        copies = []
        for k, (px, py) in enumerate([(1 - x, y), (x, 1 - y)]):
            idx = 4 * px + 2 * py + c
            copies.append(_remote(in_ref.at[idx], out_ref.at[idx], send_sems, recv_sems, k, (x, y, 1 - c)))
        fx, fy = x * c + (1 - x) * (1 - c), y * (1 - c) + (1 - y) * c
        tx, ty = x * (1 - c) + (1 - x) * c, (1 - y) * (1 - c) + y * c
        idx = 4 * fx + 2 * fy + c
        copies.append(_remote(in_ref.at[idx], out_ref.at[idx], send_sems, recv_sems, 2, (tx, ty, c)))
        return copies

    def third(cin, cout, sems):
        (out_ref,), (send_sems, recv_sems) = cout, sems
        x, y, c = _mesh_pos()
        idx = 4 * (1 - x) + 2 * (1 - y) + c
        return [_remote(out_ref.at[idx], out_ref.at[idx], send_sems, recv_sems, 3, (x, y, 1 - c))]

    n_sem = 4 if then_third else 3
    return _Comm([full], [jax.ShapeDtypeStruct(full.shape, full.dtype)],
                 [pltpu.SemaphoreType.DMA((n_sem,)), pltpu.SemaphoreType.DMA((n_sem,))], make, aliases={0: 0},
                 make_tail=third if then_third else None)


def gather3_third_comm(full):
    def make(cin, cout, sems):
        (in_ref,), (out_ref,), (send_sems, recv_sems) = cin, cout, sems
        x, y, c = _mesh_pos()
        idx = 4 * (1 - x) + 2 * (1 - y) + c
        return [_remote(in_ref.at[idx], out_ref.at[idx], send_sems, recv_sems, 0, (x, y, 1 - c))]

    return _Comm([full], [jax.ShapeDtypeStruct(full.shape, full.dtype)],
                 [pltpu.SemaphoreType.DMA((1,)), pltpu.SemaphoreType.DMA((1,))], make, aliases={0: 0})


def scatter_sibling_comm(g):
    _, r, w = g.shape
    return _Comm([g], [jax.ShapeDtypeStruct((N_CHIP, r, w), g.dtype)],
                 [pltpu.SemaphoreType.DMA((N_CHIP,)), pltpu.SemaphoreType.DMA((N_CHIP,))],
                 lambda cin, cout, sems: _sibling_copies(cin[0], cout[0], sems[0], sems[1]))


def scatter_chips_comm(part):
    _, r, w = part.shape
    return _Comm([part], [jax.ShapeDtypeStruct((N_CHIP - 1, r, w), part.dtype)],
                 [pltpu.SemaphoreType.DMA((3,)), pltpu.SemaphoreType.DMA((3,))],
                 lambda cin, cout, sems: _chip_copies(cin[0], cout[0], sems[0], sems[1]))


def exchange_sibling(g):
    _, r, w = g.shape

    def body(g_ref, got_ref, send_sems, recv_sems):
        copies = _sibling_copies(g_ref, got_ref, send_sems, recv_sems)
        for cp in copies:
            cp.start()
        for cp in copies:
            cp.wait()

    return pl.pallas_call(
        body, name="exchange_sibling", out_shape=jax.ShapeDtypeStruct((N_CHIP, r, w), g.dtype),
        in_specs=[_ANY], out_specs=_ANY,
        scratch_shapes=[pltpu.SemaphoreType.DMA((N_CHIP,)), pltpu.SemaphoreType.DMA((N_CHIP,))],
    )(g)


def exchange_chips(part):
    _, r, w = part.shape

    def body(p_ref, land_ref, send_sems, recv_sems):
        copies = _chip_copies(p_ref, land_ref, send_sems, recv_sems)
        for cp in copies:
            cp.start()
        for cp in copies:
            cp.wait()

    return pl.pallas_call(
        body, name="exchange_chips", out_shape=jax.ShapeDtypeStruct((N_CHIP - 1, r, w), part.dtype),
        in_specs=[_ANY], out_specs=_ANY,
        scratch_shapes=[pltpu.SemaphoreType.DMA((3,)), pltpu.SemaphoreType.DMA((3,))],
    )(part)


def allreduce_stats(st_mix, st_prep, st_mlp, st_ple):
    def body(mix_ref, prep_ref, mlp_ref, ple_ref, out_ref, mine, gath, send_sems, recv_sems):
        x, y, c = _mesh_pos()
        me = 4 * x + 2 * y + c
        mine[...] = jnp.zeros_like(mine)
        mine[ST_G_MIX:ST_G_MIX + 1, :] = mix_ref[...]
        mine[ST_G_QA:ST_G_KN + 1, 0:256] = prep_ref[...]
        mine[ST_G_MLP:ST_G_MLP + 1, :] = mlp_ref[...]
        mine[ST_G_PLE:ST_LOSS + 1, :] = ple_ref[...]
        gath[me] = mine[...]
        copies = []
        for k in range(1, N_DEV):
            peer = (_flip(x, k & 4), _flip(y, k & 2), _flip(c, k & 1))
            copies.append(_remote(mine, gath.at[me], send_sems, recv_sems, k - 1, peer))
        for cp in copies:
            cp.start()
        for cp in copies:
            cp.wait()
        acc = gath[0]
        for d in range(1, N_DEV):
            acc = acc + gath[d]
        out_ref[...] = acc

    vm = pl.BlockSpec(memory_space=pltpu.VMEM)
    return pl.pallas_call(
        body, name="allreduce_stats", out_shape=jax.ShapeDtypeStruct((ST_ROWS, D_MODEL), F32),
        in_specs=[vm] * 4, out_specs=vm,
        scratch_shapes=[pltpu.VMEM((ST_ROWS, D_MODEL), F32), pltpu.VMEM((N_DEV, ST_ROWS, D_MODEL), F32),
                        pltpu.SemaphoreType.DMA((N_DEV - 1,)), pltpu.SemaphoreType.DMA((N_DEV - 1,))],
    )(st_mix, st_prep, st_mlp, st_ple)


def adamw_gains(stats, gains):
    c1 = 1.0 - ADAM_B1 ** ADAM_STEP
    c2 = 1.0 - ADAM_B2 ** ADAM_STEP
    n = len(gains)

    def body(st_ref, *refs):
        ins, outs = refs[:3 * n], refs[3 * n:]
        for i, (row, w, _, _) in enumerate(gains):
            width = w.shape[1]
            gv = st_ref[row:row + 1, 0:width]
            mn = ADAM_B1 * ins[3 * i + 1][...] + (1.0 - ADAM_B1) * gv
            vn = ADAM_B2 * ins[3 * i + 2][...] + (1.0 - ADAM_B2) * (gv * gv)
            outs[4 * i][...] = gv
            outs[4 * i + 1][...] = -ADAM_LR * ((mn / c1) / (jnp.sqrt(vn / c2) + ADAM_EPS) + ADAM_WD * ins[3 * i][...])
            outs[4 * i + 2][...] = mn
            outs[4 * i + 3][...] = vn

    vm = pl.BlockSpec(memory_space=pltpu.VMEM)
    flat = [a for (_, w, m, v) in gains for a in (w, m, v)]
    out_shape = tuple(jax.ShapeDtypeStruct(w.shape, F32) for (_, w, _, _) in gains for _ in range(4))
    res = pl.pallas_call(body, name="adamw_gains", out_shape=out_shape, in_specs=[vm] * (1 + 3 * n),
                         out_specs=tuple([vm] * (4 * n)))(stats, *flat)
    return [res[4 * i:4 * i + 4] for i in range(n)]


def _row_tile(r, cap=640):
    return max(d for d in range(16, min(r, cap) + 1, 16) if r % d == 0)


def add_pairs(g, got, core):
    n, r, w = got.shape
    tr = _row_tile(r)

    def body(c_ref, a_ref, b_ref, o_ref):
        o_ref[...] = (a_ref[...].astype(F32) + b_ref[...].astype(F32)).astype(o_ref.dtype)

    spec = pl.BlockSpec((1, tr, w), lambda i, j, c: (i, j, 0))
    return pl.pallas_call(
        body, name="add_pairs", out_shape=jax.ShapeDtypeStruct(got.shape, got.dtype),
        grid_spec=pltpu.PrefetchScalarGridSpec(
            num_scalar_prefetch=1, grid=(n, r // tr),
            in_specs=[pl.BlockSpec((1, tr, w), lambda i, j, c: (2 * i + c[0], j, 0)), spec], out_specs=spec),
        compiler_params=_params(("parallel", "parallel")),
    )(core, g, got)


def sum_chips(part, land, chip):
    _, r, w = part.shape
    tr = _row_tile(r)

    def body(c_ref, p_ref, l_ref, o_ref):
        acc = p_ref[0].astype(F32)
        for s in range(N_CHIP - 1):
            acc = acc + l_ref[s].astype(F32)
        o_ref[...] = acc

    return pl.pallas_call(
        body, name="sum_chips", out_shape=jax.ShapeDtypeStruct((r, w), F32),
        grid_spec=pltpu.PrefetchScalarGridSpec(
            num_scalar_prefetch=1, grid=(r // tr,),
            in_specs=[pl.BlockSpec((1, tr, w), lambda i, c: (c[0], i, 0)), pl.BlockSpec((N_CHIP - 1, tr, w), lambda i, c: (0, i, 0))],
            out_specs=pl.BlockSpec((tr, w), lambda i, c: (i, 0))),
        compiler_params=_params(("parallel",)),
    )(chip, part, land)


def norm_x(x, g_mix, tm, comm=None):
    t = x.shape[0]
    grid = (t // tm,)
    c_ins, c_in_specs, c_outs, c_sems, alias = _comm_parts(comm, 2, 1)

    def body(*refs):
        (x_ref, g_ref), cin, (h_ref,), cout, _, csem = _split_refs(refs, 2, 1, 0, comm)
        _comm_start(comm, cin, cout, csem, grid)
        xv = x_ref[...]
        h_ref[...] = (xv * _rstd(xv, D_MODEL) * g_ref[...]).astype(BF16)
        _comm_finish(comm, cin, cout, csem, grid)

    return pl.pallas_call(
        body, name="norm_x", grid=grid, out_shape=(jax.ShapeDtypeStruct((t, D_MODEL), BF16), *c_outs),
        in_specs=[_rows(tm, D_MODEL), _resident((1, D_MODEL)), *c_in_specs],
        out_specs=(_rows(tm, D_MODEL), *([_ANY] * len(c_outs))),
        scratch_shapes=c_sems, input_output_aliases=alias, compiler_params=_params(("arbitrary",)),
    )(x, g_mix, *c_ins)


def pack_late_weights(w_up, w_down, w_o, w_pg, small, comm=None):
    rows2 = sum(PACK_ROWS[n] for n in PACK_W2)
    rows3 = sum(PACK_ROWS[n] for n in PACK_W3)
    c_ins, c_in_specs, c_outs, c_sems, alias = _comm_parts(comm, 5, 2)
    grid = (1,)

    def body(*refs):
        (up_ref, dn_ref, o_ref, pg_ref, sm_ref), cin, (p2_ref, p3_ref), cout, _, csem = _split_refs(refs, 5, 2, 0, comm)
        _comm_start(comm, cin, cout, csem, grid)
        p2_ref[0:128, :] = o_ref[0].astype(BF16)
        p2_ref[128:256, :] = pg_ref[0].astype(BF16)
        p2_ref[256:rows2, :] = sm_ref[...]
        p3_ref[0:512, :] = up_ref[0].T.astype(BF16)
        p3_ref[512:1024, :] = dn_ref[0].astype(BF16)
        _comm_finish(comm, cin, cout, csem, grid)

    def whole(a):
        nd = a.ndim
        return pl.BlockSpec(a.shape, lambda i: (0,) * nd)

    args = (w_up, w_down, w_o, w_pg, small)
    return pl.pallas_call(
        body, name="pack_late_weights", grid=grid,
        out_shape=(jax.ShapeDtypeStruct((rows2, D_MODEL), BF16), jax.ShapeDtypeStruct((rows3, D_MODEL), BF16), *c_outs),
        in_specs=[*[whole(a) for a in args], *c_in_specs],
        out_specs=(pl.BlockSpec((rows2, D_MODEL), lambda i: (0, 0)), pl.BlockSpec((rows3, D_MODEL), lambda i: (0, 0)),
                   *([_ANY] * len(c_outs))),
        scratch_shapes=c_sems, input_output_aliases=alias, compiler_params=_params(("arbitrary",)),
    )(*args, *c_ins)


def in_proj(h, w_in_t, tm, comm=None):
    t = h.shape[0]
    nc = 512
    half = ZP // 2
    grid = (t // tm,)
    c_ins, c_in_specs, c_outs, c_sems, alias = _comm_parts(comm, 2, 2)

    def body(*refs):
        (h_ref, w_ref), cin, (zg_ref, zs_ref), cout, _, csem = _split_refs(refs, 2, 2, 0, comm)
        _comm_start(comm, cin, cout, csem, grid)
        hv = h_ref[...]
        for cidx in range(half // nc):
            zg_ref[:, cidx * nc:(cidx + 1) * nc] = _dot_nt(hv, w_ref[cidx * nc:(cidx + 1) * nc, :]).astype(BF16)
        for cidx in range(half // nc):
            zs_ref[:, cidx * nc:(cidx + 1) * nc] = _dot_nt(hv, w_ref[half + cidx * nc:half + (cidx + 1) * nc, :])
        _comm_finish(comm, cin, cout, csem, grid)

    return pl.pallas_call(
        body, name="in_proj", grid=grid,
        out_shape=(jax.ShapeDtypeStruct((t, half), BF16), jax.ShapeDtypeStruct((t, half), F32), *c_outs),
        in_specs=[_rows(tm, D_MODEL), _resident((ZP, D_MODEL)), *c_in_specs],
        out_specs=(_rows(tm, half), _rows(tm, half), *([_ANY] * len(c_outs))),
        scratch_shapes=c_sems, input_output_aliases=alias, compiler_params=_params(("arbitrary",)),
    )(h, w_in_t, *c_ins)


def attn_prep(zp, tabs, g_qa, g_kva, g_qn, g_kn, w_qb_t, w_kvb_t, tm, s_len, comm=None):
    t = zp.shape[0]
    nsb = s_len // tm
    scale_a = (QK_NOPE + QK_ROPE) ** -0.5
    scale_b = HD_B ** -0.5

    grid = (t // tm,)
    c_ins, c_in_specs, c_outs, c_sems, alias = _comm_parts(comm, 13, 8)

    def body(*refs):
        ((qb_ref, qlat_ref, kb_ref, vb_ref, ckv_ref, kpe_ref, tab_ref, gqa_ref, gkva_ref, gqn_ref, gkn_ref, wqb_ref,
          wkvb_ref), cin, (qa_o, ka_o, va_o, qb_o, kb_o, vb_o, cq_o, ckvn_o), cout, _, csem) = _split_refs(refs, 13, 8, 0, comm)
        _comm_start(comm, cin, cout, csem, grid)
        ca, s1a, s2a = tab_ref[0], tab_ref[1], tab_ref[2]
        ck = tab_ref[3]
        cb, s1b, s2b = tab_ref[4], tab_ref[5], tab_ref[6]
        ql = qlat_ref[...]
        cq = (ql * _rstd(ql, Q_LORA) * gqa_ref[...]).astype(BF16)
        cq_o[...] = cq
        qa = _dot_nt(cq, wqb_ref[...])
        slabs = [slice(h * HP, (h + 1) * HP) for h in range(H_A)]
        qa_o[...] = jnp.concatenate(
            [(_rope_fwd(qa[:, sl], ca, s1a, s2a) * scale_a).astype(BF16) for sl in slabs], axis=1)
        cr = ckv_ref[...]
        ckv = (cr * _rstd(cr, KV_LORA) * gkva_ref[...]).astype(BF16)
        ckvn_o[...] = ckv
        kva = _dot_nt(ckv, wkvb_ref[...])
        kpe = _rope_fwd(kpe_ref[...], ck, s1a, s2a)
        ka_o[...] = jnp.concatenate([(kva[:, sl] + kpe).astype(BF16) for sl in slabs], axis=1)
        va_o[...] = kva[:, H_A * HP:].astype(BF16)
        gqn, gkn = gqn_ref[...], gkn_ref[...]

        def norm_rope(ref, sl, g, scale):
            xs = ref[:, sl]
            y = _rope_fwd(xs * _rstd(xs, HD_B) * g, cb, s1b, s2b)
            return (y if scale is None else y * scale).astype(BF16)

        qb_o[...] = jnp.concatenate([norm_rope(qb_ref, sl, gqn, scale_b) for sl in slabs], axis=1)
        kb_o[...] = jnp.concatenate([norm_rope(kb_ref, sl, gkn, None) for sl in slabs[:KV_B]], axis=1)
        vb_o[...] = vb_ref[...].astype(BF16)
        _comm_finish(comm, cin, cout, csem, grid)

    def o(width):
        return jax.ShapeDtypeStruct((t, width), BF16)

    return pl.pallas_call(
        body, name="attn_prep", grid=grid,
        out_shape=(o(H_A * HP), o(H_A * HP), o(H_A * HP), o(H_B * HP), o(KV_B * HP), o(KV_B * HP), o(Q_LORA), o(KV_LORA),
                   *c_outs),
        in_specs=[_rows(tm, 1024, 0), _rows(tm, 256, 4), _rows(tm, 256, 5), _rows(tm, 256, 6),
                  _rows(tm, 128, 14), _rows(tm, 128, 15),
                  pl.BlockSpec((7, tm, HP), lambda i: (0, i % nsb, 0)),
                  _resident((1, Q_LORA)), _resident((1, KV_LORA)), _resident((1, HP)), _resident((1, HP)),
                  _resident((H_A * HP, Q_LORA)), _resident((2 * H_A * HP, KV_LORA)), *c_in_specs],
        out_specs=(_rows(tm, H_A * HP), _rows(tm, H_A * HP), _rows(tm, H_A * HP), _rows(tm, H_B * HP),
                   _rows(tm, KV_B * HP), _rows(tm, KV_B * HP), _rows(tm, Q_LORA), _rows(tm, KV_LORA), *([_ANY] * len(c_outs))),
        scratch_shapes=c_sems, input_output_aliases=alias, compiler_params=_params(("arbitrary",)),
    )(zp, zp, zp, zp, zp, zp, tabs, g_qa, g_kva, g_qn, g_kn, w_qb_t, w_kvb_t, *c_ins)


def attn_fwd(q, k, v, n_b, s_len, tq, name, comm=None):
    t = q.shape[0]
    n_h, n_hk = q.shape[1] // HP, k.shape[1] // HP
    grp = n_h // n_hk
    nq = s_len // tq
    sub = min(tq, 256)
    grid = (n_b, n_h, nq)
    c_ins, c_in_specs, c_outs, c_sems, alias = _comm_parts(comm, 3, 2)

    def body(*refs):
        (q_ref, k_ref, v_ref), cin, (o_ref, lse_ref), cout, _, csem = _split_refs(refs, 3, 2, 0, comm)
        _comm_start(comm, cin, cout, csem, grid)
        kv, vv = k_ref[...], v_ref[...]
        for r in range(tq // sub):
            rows = slice(r * sub, (r + 1) * sub)
            s = _dot_nt(q_ref[rows, :], kv)
            m = jnp.max(s, axis=-1, keepdims=True)
            p = jnp.exp(s - m)
            l = jnp.sum(p, axis=-1, keepdims=True)
            o_ref[rows, :] = (_dot_nn(p.astype(BF16), vv) * (1.0 / l)).astype(o_ref.dtype)
            lse_ref[rows, :] = jnp.broadcast_to(m + jnp.log(l), (sub, HP))
        _comm_finish(comm, cin, cout, csem, grid)

    qspec = pl.BlockSpec((tq, HP), lambda b, h, i: (b * nq + i, h))
    kspec = pl.BlockSpec((s_len, HP), lambda b, h, i: (b, h // grp))
    return pl.pallas_call(
        body, name=name, grid=grid,
        out_shape=(jax.ShapeDtypeStruct((t, n_h * HP), BF16), jax.ShapeDtypeStruct((t, n_h * HP), F32), *c_outs),
        in_specs=[qspec, kspec, kspec, *c_in_specs], out_specs=(qspec, qspec, *([_ANY] * len(c_outs))),
        scratch_shapes=c_sems, input_output_aliases=alias,
        compiler_params=_params(("arbitrary", "arbitrary", "arbitrary")),
    )(q, k, v, *c_ins)


def merge_fwd(oa, ob, zp, x, w_oa_t, w_ob_t, wpack, off, tm):
    t = x.shape[0]

    def body(oa_ref, ob_ref, ga_ref, gb_ref, x_ref, woa_ref, wob_ref, wo_ref, x1_o, mg_o, y_o):
        ya = _dot_nt(oa_ref[...], woa_ref[...])
        yb = _dot_nt(ob_ref[...], wob_ref[...])
        y_o[:, 0:D_MODEL] = ya.astype(BF16)
        y_o[:, D_MODEL:2 * D_MODEL] = yb.astype(BF16)
        merged = (jax.nn.sigmoid(ga_ref[...].astype(F32)) * ya + jax.nn.sigmoid(gb_ref[...].astype(F32)) * yb).astype(BF16)
        mg_o[...] = merged
        x1_o[...] = x_ref[...] + _dot_nn(merged, _wrows(wo_ref, 0, D_MODEL))

    return pl.pallas_call(
        body, name="merge_fwd", grid=(t // tm,),
        out_shape=(jax.ShapeDtypeStruct((t, D_MODEL), F32), jax.ShapeDtypeStruct((t, D_MODEL), BF16),
                   jax.ShapeDtypeStruct((t, 2 * D_MODEL), BF16)),
        in_specs=[_rows(tm, H_A * HP), _rows(tm, H_B * HP), _rows(tm, 1024, 0), _rows(tm, 1024, 1), _rows(tm, D_MODEL),
                  _resident((D_MODEL, H_A * HP)), _resident((D_MODEL, H_B * HP)), _packed_weight(128, off["w_o"])],
        out_specs=(_rows(tm, D_MODEL), _rows(tm, D_MODEL), _rows(tm, 2 * D_MODEL)), compiler_params=_params(("parallel",)),
    )(oa, ob, zp, zp, x, w_oa_t, w_ob_t, wpack)


def mlp_fwd(x1, g_mlp, wpack, off, tm):
    t = x1.shape[0]
    fc = 1024

    def body(x_ref, g_ref, wup_ref, wdn_ref, x2_o, u_o):
        xv = x_ref[...]
        h2 = (xv * _rstd(xv, D_MODEL) * g_ref[...]).astype(BF16)
        acc = xv
        for cidx in range(D_FF // fc):
            sl = slice(cidx * fc, (cidx + 1) * fc)
            u = jnp.maximum(_dot_nt(h2, _wrows(wup_ref, cidx * fc, fc)), 0.0)
            u_o[:, sl] = u.astype(BF16)
            acc = acc + _dot_nn((u * u).astype(BF16), _wrows(wdn_ref, cidx * fc, fc))
        x2_o[...] = acc

    return pl.pallas_call(
        body, name="mlp_fwd", grid=(t // tm,),
        out_shape=(jax.ShapeDtypeStruct((t, D_MODEL), F32), jax.ShapeDtypeStruct((t, D_FF), BF16)),
        in_specs=[_rows(tm, D_MODEL), _resident((1, D_MODEL)), _packed_weight(512, off["w_up"]), _packed_weight(512, off["w_down"])],
        out_specs=(_rows(tm, D_MODEL), _rows(tm, D_FF)), compiler_params=_params(("parallel",)),
    )(x1, g_mlp, wpack, wpack)


def ple_loss_bwd(x2, p, tgt, g_ple, g_final, wpack, off, w_ple_t, tm):
    t = x2.shape[0]
    inv_d = 1.0 / D_MODEL

    def body(x2_ref, p_ref, tg_ref, gp_ref, gf_ref, wpg_ref, wple_ref, dx2_o, dt_o, h3_o, dpe_o, st_o, dx2b_o):
        @pl.when(pl.program_id(0) == 0)
        def _():
            st_o[...] = jnp.zeros_like(st_o)

        x2v = x2_ref[...]
        gp, gf = gp_ref[...], gf_ref[...]
        w_pg = _wrows(wpg_ref, 0, D_MODEL)
        r2 = _rstd(x2v, D_MODEL)
        xh2 = x2v * r2
        h3 = (xh2 * gp).astype(BF16)
        h3_o[...] = h3
        gate = jax.nn.sigmoid(_dot_nn(h3, w_pg))
        pe = _dot_nt(p_ref[...].astype(BF16), wple_ref[...])
        x3 = x2v + gate * pe
        r3 = _rstd(x3, D_MODEL)
        xh3 = x3 * r3
        err = xh3 * gf - tg_ref[...]
        dy = err * inv_d
        dx3 = _rms_bwd(dy, xh3, r3, gf, D_MODEL)
        dpe_o[...] = (dx3 * gate).astype(BF16)
        dt = (dx3 * pe * gate * (1.0 - gate)).astype(BF16)
        dt_o[...] = dt
        dh3 = _dot_nt(dt, w_pg)
        dx2 = dx3 + _rms_bwd(dh3, xh2, r2, gp, D_MODEL)
        dx2_o[...] = dx2
        dx2b_o[...] = dx2.astype(BF16)
        st_o[0:1, :] += _colsum(dh3 * xh2)
        st_o[1:2, :] += _colsum(dy * xh3)
        st_o[2:3, :] += _colsum(err * err) * (0.5 * inv_d)

    bf = jax.ShapeDtypeStruct((t, D_MODEL), BF16)
    return pl.pallas_call(
        body, name="ple_loss_bwd", grid=(t // tm,),
        out_shape=(jax.ShapeDtypeStruct((t, D_MODEL), F32), bf, bf, bf, jax.ShapeDtypeStruct((3, D_MODEL), F32), bf),
        in_specs=[_rows(tm, D_MODEL), _rows(tm, PLE_DIM), _rows(tm, D_MODEL), _resident((1, D_MODEL)), _resident((1, D_MODEL)),
                  _packed_weight(128, off["w_ple_gate"]), _resident((D_MODEL, PLE_DIM))],
        out_specs=(_rows(tm, D_MODEL), _rows(tm, D_MODEL), _rows(tm, D_MODEL), _rows(tm, D_MODEL),
                   pl.BlockSpec((3, D_MODEL), lambda i: (0, 0)), _rows(tm, D_MODEL)),
        compiler_params=_params(("arbitrary",)),
    )(x2, p, tgt, g_ple, g_final, wpack, w_ple_t)


def mlp_bwd(dx2, x1, u, g_mlp, wpack, off, tm):
    t = x1.shape[0]
    fc = 1024

    def body(dx2_ref, x1_ref, u_ref, g_ref, wup_ref, wdn_ref, dx1_o, da_o, h2_o, st_o, dx1b_o):
        @pl.when(pl.program_id(0) == 0)
        def _():
            st_o[...] = jnp.zeros_like(st_o)

        d2 = dx2_ref[...]
        d2b = d2.astype(BF16)
        dh2 = jnp.zeros((tm, D_MODEL), F32)
        for cidx in range(D_FF // fc):
            sl = slice(cidx * fc, (cidx + 1) * fc)
            da = (_dot_nt(d2b, _wrows(wdn_ref, cidx * fc, fc)) * (2.0 * u_ref[:, sl].astype(F32))).astype(BF16)
            da_o[:, sl] = da
            dh2 = dh2 + _dot_nn(da, _wrows(wup_ref, cidx * fc, fc))
        xv = x1_ref[...]
        g = g_ref[...]
        r1 = _rstd(xv, D_MODEL)
        xh1 = xv * r1
        h2_o[...] = (xh1 * g).astype(BF16)
        st_o[...] += _colsum(dh2 * xh1)
        dx1 = d2 + _rms_bwd(dh2, xh1, r1, g, D_MODEL)
        dx1_o[...] = dx1
        dx1b_o[...] = dx1.astype(BF16)

    return pl.pallas_call(
        body, name="mlp_bwd", grid=(t // tm,),
        out_shape=(jax.ShapeDtypeStruct((t, D_MODEL), F32), jax.ShapeDtypeStruct((t, D_FF), BF16),
                   jax.ShapeDtypeStruct((t, D_MODEL), BF16), jax.ShapeDtypeStruct((1, D_MODEL), F32),
                   jax.ShapeDtypeStruct((t, D_MODEL), BF16)),
        in_specs=[_rows(tm, D_MODEL), _rows(tm, D_MODEL), _rows(tm, D_FF), _resident((1, D_MODEL)),
                  _packed_weight(512, off["w_up"]), _packed_weight(512, off["w_down"])],
        out_specs=(_rows(tm, D_MODEL), _rows(tm, D_FF), _rows(tm, D_MODEL), pl.BlockSpec((1, D_MODEL), lambda i: (0, 0)),
                   _rows(tm, D_MODEL)),
        compiler_params=_params(("arbitrary",)),
    )(dx2, x1, u, g_mlp, wpack, wpack)


def merge_bwd(dx1b, yab, zp, w_oa_t, w_ob_t, wpack, off, tm, comm=None):
    t = dx1b.shape[0]
    grid = (t // tm,)
    c_ins, c_in_specs, c_outs, c_sems, alias = _comm_parts(comm, 7, 5)

    def body(*refs):
        ((dx1_ref, y_ref, ga_ref, gb_ref, woa_ref, wob_ref, wo_ref), cin,
         (doa_o, dob_o, dg_o, dya_o, dyb_o), cout, _, csem) = _split_refs(refs, 7, 5, 0, comm)
        _comm_start(comm, cin, cout, csem, grid)
        dm = _dot_nt(dx1_ref[...], _wrows(wo_ref, 0, D_MODEL))
        for g_ref, w_ref, do_o, dy_o, col in ((ga_ref, woa_ref, doa_o, dya_o, 0), (gb_ref, wob_ref, dob_o, dyb_o, 1)):
            cols = slice(col * D_MODEL, (col + 1) * D_MODEL)
            sg = jax.nn.sigmoid(g_ref[...].astype(F32))
            dyv = (dm * sg).astype(BF16)
            dy_o[...] = dyv
            dg_o[:, cols] = (dm * y_ref[:, cols].astype(F32) * sg * (1.0 - sg)).astype(BF16)
            do_o[...] = _dot_nn(dyv, w_ref[...]).astype(BF16)
        _comm_finish(comm, cin, cout, csem, grid)

    bf = jax.ShapeDtypeStruct((t, D_MODEL), BF16)
    return pl.pallas_call(
        body, name="merge_bwd", grid=grid,
        out_shape=(bf, bf, jax.ShapeDtypeStruct((t, ZP), BF16), bf, bf, *c_outs),
        in_specs=[_rows(tm, D_MODEL), _rows(tm, 2 * D_MODEL), _rows(tm, 1024, 0), _rows(tm, 1024, 1),
                  _resident((D_MODEL, H_A * HP)), _resident((D_MODEL, H_B * HP)), _packed_weight(128, off["w_o"]), *c_in_specs],
        out_specs=(_rows(tm, D_MODEL), _rows(tm, D_MODEL), _rows(tm, 2 * D_MODEL), _rows(tm, D_MODEL), _rows(tm, D_MODEL),
                   *([_ANY] * len(c_outs))),
        scratch_shapes=c_sems, input_output_aliases=alias,
        compiler_params=_params(("arbitrary",)),
    )(dx1b, yab, zp, zp, w_oa_t, w_ob_t, wpack, *c_ins)


def attn_bwd(q, k, v, do, o, lse, n_b, s_len, tq, name, comm=None):
    t = q.shape[0]
    n_h, n_hk = q.shape[1] // HP, k.shape[1] // HP
    grp = n_h // n_hk
    nq = s_len // tq
    sub = min(tq, 256)
    grid = (n_b, n_hk, grp, nq)
    c_ins, c_in_specs, c_outs, c_sems, alias = _comm_parts(comm, 6, 3)

    def body(*refs):
        ((q_ref, k_ref, v_ref, do_ref, o_ref, lse_ref), cin, (dq_o, dk_o, dv_o), cout, (p_s, ds_s, dk_acc, dv_acc),
         csem) = _split_refs(refs, 6, 3, 4, comm)
        _comm_start(comm, cin, cout, csem, grid)

        @pl.when((pl.program_id(2) == 0) & (pl.program_id(3) == 0))
        def _():
            dk_acc[...] = jnp.zeros_like(dk_acc)
            dv_acc[...] = jnp.zeros_like(dv_acc)

        kv, vv = k_ref[...], v_ref[...]
        for r in range(tq // sub):
            rows = slice(r * sub, (r + 1) * sub)
            qv, dov = q_ref[rows, :], do_ref[rows, :]
            delta = jnp.sum(dov.astype(F32) * o_ref[rows, :].astype(F32), axis=-1, keepdims=True)
            delta_row = jnp.broadcast_to(delta, (sub, HP)).T[0:1, :]
            lse_row = lse_ref[rows, :].T[0:1, :]
            pt = jnp.exp(_dot_nt(kv, qv) - lse_row)
            dst = (pt * (_dot_nt(vv, dov) - delta_row)).astype(BF16)
            p_s[:, rows] = pt.astype(BF16)
            ds_s[:, rows] = dst
            dq_o[rows, :] = _dot_tn(dst, kv).astype(dq_o.dtype)
        dk_acc[...] += _dot_nn(ds_s[...], q_ref[...])
        dv_acc[...] += _dot_nn(p_s[...], do_ref[...])

        @pl.when((pl.program_id(2) == grp - 1) & (pl.program_id(3) == nq - 1))
        def _():
            dk_o[...] = dk_acc[...].astype(dk_o.dtype)
            dv_o[...] = dv_acc[...].astype(dv_o.dtype)

        _comm_finish(comm, cin, cout, csem, grid)

    qspec = pl.BlockSpec((tq, HP), lambda b, hk, g, i: (b * nq + i, hk * grp + g))
    kspec = pl.BlockSpec((s_len, HP), lambda b, hk, g, i: (b, hk))
    return pl.pallas_call(
        body, name=name, grid=grid,
        out_shape=(jax.ShapeDtypeStruct((t, n_h * HP), BF16), jax.ShapeDtypeStruct((t, n_hk * HP), BF16),
                   jax.ShapeDtypeStruct((t, n_hk * HP), BF16), *c_outs),
        in_specs=[qspec, kspec, kspec, qspec, qspec, qspec, *c_in_specs],
        out_specs=(qspec, kspec, kspec, *([_ANY] * len(c_outs))),
        scratch_shapes=[pltpu.VMEM((s_len, tq), BF16), pltpu.VMEM((s_len, tq), BF16),
                        pltpu.VMEM((s_len, HP), F32), pltpu.VMEM((s_len, HP), F32), *c_sems],
        input_output_aliases=alias,
        compiler_params=_params(("arbitrary", "arbitrary", "arbitrary", "arbitrary")),
    )(q, k, v, do, o, lse, *c_ins)


def prep_bwd(dqa, dka, dva, dqb, dkb, dvb, zp, dz, tabs, g_qa, g_kva, g_qn, g_kn, w_qb_t, w_kvb_t, tm, s_len):
    t = zp.shape[0]
    nsb = s_len // tm
    scale_a = (QK_NOPE + QK_ROPE) ** -0.5
    scale_b = HD_B ** -0.5

    def body(dqa_ref, dka_ref, dva_ref, dqb_ref, dkb_ref, dvb_ref, qb_ref, qlat_ref, kb_ref, ckv_ref, tab_ref,
             gqa_ref, gkva_ref, gqn_ref, gkn_ref, wqb_ref, wkvb_ref, _, dz_o, dqap_o, dkva_o, st_o):
        dzq_o, dsm_o = dz_o.at[:, 0:1024], dz_o.at[:, 1024:2048]

        @pl.when(pl.program_id(0) == 0)
        def _():
            st_o[...] = jnp.zeros_like(st_o)

        ca, s1a, s2a = tab_ref[0], tab_ref[1], tab_ref[2]
        ck = tab_ref[3]
        cb, s1b, s2b = tab_ref[4], tab_ref[5], tab_ref[6]
        for h in range(H_A):
            sl = slice(h * HP, (h + 1) * HP)
            dqap_o[:, sl] = _rope_bwd(dqa_ref[:, sl].astype(F32) * scale_a, ca, s1a, s2a).astype(BF16)
        dcq = _dot_nn(dqap_o[...], wqb_ref[...])
        ql = qlat_ref[...]
        rq = _rstd(ql, Q_LORA)
        xh = ql * rq
        gqa = gqa_ref[...]
        st_o[0:1, :] += _colsum(dcq * xh)
        dsm_o[:, 0:256] = _rms_bwd(dcq, xh, rq, gqa, Q_LORA).astype(BF16)
        dkpe = jnp.zeros((tm, HP), F32)
        for h in range(H_A):
            sl = slice(h * HP, (h + 1) * HP)
            dk = dka_ref[:, sl]
            dkpe = dkpe + dk.astype(F32)
            dkva_o[:, sl] = dk.astype(BF16)
        dkva_o[:, H_A * HP:] = dva_ref[...].astype(BF16)
        dsm_o[:, 896:1024] = _rope_bwd(dkpe, ck, s1a, s2a).astype(BF16)
        dckv = _dot_nn(dkva_o[...], wkvb_ref[...])
        cr = ckv_ref[...]
        rk = _rstd(cr, KV_LORA)
        xh = cr * rk
        st_o[1:2, 0:128] += _colsum(dckv * xh)
        dsm_o[:, 768:896] = _rms_bwd(dckv, xh, rk, gkva_ref[...], KV_LORA).astype(BF16)
        gqn, gkn = gqn_ref[...], gkn_ref[...]
        dgq = jnp.zeros((1, HP), F32)
        for h in range(H_B):
            sl = slice(h * HP, (h + 1) * HP)
            dy = _rope_bwd(dqb_ref[:, sl].astype(F32) * scale_b, cb, s1b, s2b)
            xs = qb_ref[:, sl]
            r = _rstd(xs, HD_B)
            xh = xs * r
            dgq = dgq + _colsum(dy * xh)
            dzq_o[:, sl] = _rms_bwd(dy, xh, r, gqn, HD_B).astype(BF16)
        st_o[2:3, 0:128] += dgq
        dgk = jnp.zeros((1, HP), F32)
        for h in range(KV_B):
            sl = slice(h * HP, (h + 1) * HP)
            dy = _rope_bwd(dkb_ref[:, sl].astype(F32), cb, s1b, s2b)
            xs = kb_ref[:, sl]
            r = _rstd(xs, HD_B)
            xh = xs * r
            dgk = dgk + _colsum(dy * xh)
            dsm_o[:, 256 + h * HP:256 + (h + 1) * HP] = _rms_bwd(dy, xh, r, gkn, HD_B).astype(BF16)
        st_o[3:4, 0:128] += dgk
        dsm_o[:, 512:768] = dvb_ref[...].astype(BF16)

    return pl.pallas_call(
        body, name="prep_bwd", grid=(t // tm,),
        out_shape=(jax.ShapeDtypeStruct((t, ZP), BF16), jax.ShapeDtypeStruct((t, 1024), BF16),
                   jax.ShapeDtypeStruct((t, 2048), BF16), jax.ShapeDtypeStruct((4, 256), F32)),
        in_specs=[_rows(tm, 1024), _rows(tm, 1024), _rows(tm, 1024), _rows(tm, 1024), _rows(tm, 256), _rows(tm, 256),
                  _rows(tm, 1024, 0), _rows(tm, 256, 4), _rows(tm, 256, 5), _rows(tm, 128, 14),
                  pl.BlockSpec((7, tm, HP), lambda i: (0, i % nsb, 0)),
                  _resident((1, Q_LORA)), _resident((1, KV_LORA)), _resident((1, HP)), _resident((1, HP)),
                  _resident((H_A * HP, Q_LORA)), _resident((2 * H_A * HP, KV_LORA)), _ANY],
        out_specs=(_rows(tm, 2048, 1), _rows(tm, 1024), _rows(tm, 2048), pl.BlockSpec((4, 256), lambda i: (0, 0))),
        input_output_aliases={17: 0}, compiler_params=_params(("arbitrary",)),
    )(dqa, dka, dva, dqb, dkb, dvb, zp, zp, zp, zp, tabs, g_qa, g_kva, g_qn, g_kn, w_qb_t, w_kvb_t, dz)


def in_bwd(dz, x, dx1, g_mix, w_in_t, tm, comm=None):
    t = x.shape[0]
    grid = (t // tm,)
    c_ins, c_in_specs, c_outs, c_sems, alias = _comm_parts(comm, 5, 2)

    def body(*refs):
        (dz_ref, x_ref, dx1_ref, g_ref, w_ref), cin, (dx_o, st_o), cout, _, csem = _split_refs(refs, 5, 2, 0, comm)
        _comm_start(comm, cin, cout, csem, grid)

        @pl.when(pl.program_id(0) == 0)
        def _():
            st_o[...] = jnp.zeros_like(st_o)

        dh = _dot_nn(dz_ref[...], w_ref[...])
        xv = x_ref[...]
        g = g_ref[...]
        r = _rstd(xv, D_MODEL)
        xh = xv * r
        st_o[...] += _colsum(dh * xh)
        dx_o[...] = dx1_ref[...] + _rms_bwd(dh, xh, r, g, D_MODEL)
        _comm_finish(comm, cin, cout, csem, grid)

    return pl.pallas_call(
        body, name="in_bwd", grid=grid,
        out_shape=(jax.ShapeDtypeStruct((t, D_MODEL), F32), jax.ShapeDtypeStruct((1, D_MODEL), F32), *c_outs),
        in_specs=[_rows(tm, ZP), _rows(tm, D_MODEL), _rows(tm, D_MODEL),
                  _resident((1, D_MODEL)), _resident((ZP, D_MODEL)), *c_in_specs],
        out_specs=(_rows(tm, D_MODEL), pl.BlockSpec((1, D_MODEL), lambda i: (0, 0)), *([_ANY] * len(c_outs))),
        scratch_shapes=c_sems, input_output_aliases=alias,
        compiler_params=_params(("arbitrary",)),
    )(dz, x, dx1, g_mix, w_in_t, *c_ins)


def matmul_tn(a, b, name, square_a=False):
    t, m = a.shape
    n = b.shape[1]
    bm = min(m, 512)
    tk = min(t, 4096)

    def body(a_ref, b_ref, o_ref):
        @pl.when(pl.program_id(1) == 0)
        def _():
            o_ref[...] = jnp.zeros_like(o_ref)

        av = a_ref[...]
        if square_a:
            av = (av.astype(F32) * av.astype(F32))
        o_ref[...] += _dot_tn(av.astype(BF16), b_ref[...].astype(BF16))

    return pl.pallas_call(
        body, name=name, grid=(m // bm, t // tk), out_shape=jax.ShapeDtypeStruct((m, n), F32),
        in_specs=[pl.BlockSpec((tk, bm), lambda i, kk: (kk, i)), pl.BlockSpec((tk, n), lambda i, kk: (kk, 0))],
        out_specs=pl.BlockSpec((bm, n), lambda i, kk: (i, 0)),
        compiler_params=_params(("parallel", "arbitrary")),
    )(a, b)


def matmul_tn_packed(a, b, name, rows, row_off, total_rows, buf=None, square_a=False):
    t, m = a.shape
    n = b.shape[1]
    pd = max(1, 512 // rows)
    bm = pd * rows
    tk = min(t, 4096)
    nk = t // tk

    def body(a_ref, b_ref, *rest):
        o_ref, acc = rest[-2], rest[-1]

        @pl.when(pl.program_id(1) == 0)
        def _():
            acc[...] = jnp.zeros_like(acc)

        av = a_ref[...]
        if square_a:
            av = (av.astype(F32) * av.astype(F32))
        acc[...] += _dot_tn(av.astype(BF16), b_ref[...].astype(BF16))

        @pl.when(pl.program_id(1) == nk - 1)
        def _():
            o_ref[...] = acc[...].reshape(pd, rows, n).astype(o_ref.dtype)

    in_specs = [pl.BlockSpec((tk, bm), lambda i, kk: (kk, i)), pl.BlockSpec((tk, n), lambda i, kk: (kk, 0))]
    args = [a, b]
    if buf is not None:
        in_specs.append(_ANY)
        args.append(buf)
    return pl.pallas_call(
        body, name=name, grid=(m // bm, nk), out_shape=jax.ShapeDtypeStruct((N_DEV, total_rows, n), BF16),
        in_specs=in_specs, out_specs=pl.BlockSpec((pd, rows, n), lambda i, kk: (i, row_off // rows, 0)),
        scratch_shapes=[pltpu.VMEM((bm, n), F32)], input_output_aliases={2: 0} if buf is not None else {},
        compiler_params=_params(("parallel", "arbitrary")),
    )(*args)


def adamw(w, g, m, v, name, g_transposed=False):
    _, r, c = w.shape
    tr = 256 if (not g_transposed and r > 256 and r % 256 == 0) else r
    c1 = 1.0 - ADAM_B1 ** ADAM_STEP
    c2 = 1.0 - ADAM_B2 ** ADAM_STEP

    def body(w_ref, g_ref, m_ref, v_ref, g_o, d_o, m_o, v_o):
        gv = g_ref[...].T if g_transposed else g_ref[...]
        mn = ADAM_B1 * m_ref[0] + (1.0 - ADAM_B1) * gv
        vn = ADAM_B2 * v_ref[0] + (1.0 - ADAM_B2) * (gv * gv)
        g_o[0] = gv
        m_o[0] = mn
        v_o[0] = vn
        d_o[0] = -ADAM_LR * ((mn / c1) / (jnp.sqrt(vn / c2) + ADAM_EPS) + ADAM_WD * w_ref[0])

    spec = pl.BlockSpec((1, tr, c), lambda i: (0, i, 0))
    gspec = pl.BlockSpec((c, r), lambda i: (0, 0)) if g_transposed else pl.BlockSpec((tr, c), lambda i: (i, 0))
    shp = jax.ShapeDtypeStruct((1, r, c), F32)
    return pl.pallas_call(
        body, name=name, grid=(r // tr,), out_shape=(shp,) * 4, in_specs=[spec, gspec, spec, spec], out_specs=(spec,) * 4,
        compiler_params=_params(("parallel",)),
    )(w, g, m, v)


def _rope_tables(s_len):
    def angles(pos, dim):
        inv = np.float32(ROPE_THETA) ** (-np.arange(0, dim, 2, dtype=np.float32) / np.float32(dim))
        return pos.astype(np.float32)[:, None] * inv[None, :]

    tpos = np.arange(s_len)
    a1 = angles(tpos, QK_ROPE)
    ar = angles(tpos // GRID_W, HD_B // 2)
    ac = angles(tpos % GRID_W, HD_B // 2)
    z16 = np.zeros((s_len, 16), np.float32)
    z32 = np.zeros((s_len, 32), np.float32)
    z64 = np.zeros((s_len, 64), np.float32)
    one64 = np.ones((s_len, 64), np.float32)
    c1, s1 = np.cos(a1), np.sin(a1)
    ca = np.concatenate([one64, c1, c1, z32], axis=1)
    ck = np.concatenate([z64, c1, c1, z32], axis=1)
    s1a = np.concatenate([z64, -s1, z16, z32], axis=1)
    s2a = np.concatenate([z64, z16, s1, z32], axis=1)
    cr, sr, cc, sc = np.cos(ar), np.sin(ar), np.cos(ac), np.sin(ac)
    cb = np.concatenate([cr, cr, cc, cc, z64], axis=1)
    s1b = np.concatenate([-sr, z16, -sc, z16, z64], axis=1)
    s2b = np.concatenate([z16, sr, z16, sc, z64], axis=1)
    return jnp.asarray(np.stack([ca, s1a, s2a, ck, cb, s1b, s2b]).astype(np.float32))


def _pad_heads(a, n_heads, axis):
    shp = a.shape
    a = a.reshape(shp[:axis] + (n_heads, shp[axis] // n_heads) + shp[axis + 1:])
    pad = [(0, 0)] * a.ndim
    pad[axis + 1] = (0, HP - a.shape[axis + 1])
    a = jnp.pad(a, pad)
    return a.reshape(shp[:axis] + (n_heads * HP,) + shp[axis + 1:])


def _unpad_heads(a, n_heads, width, axis):
    shp = a.shape
    a = a.reshape(shp[:axis] + (n_heads, HP) + shp[axis + 1:])
    a = lax.slice_in_dim(a, 0, width, axis=axis + 1)
    return a.reshape(shp[:axis] + (n_heads * width,) + shp[axis + 1:])


def _pack_rows(blocks, names):
    parts = []
    for name in names:
        b = blocks[name]
        padr = PACK_ROWS[name] - b.shape[-2]
        if padr:
            b = jnp.pad(b, [(0, 0)] * (b.ndim - 2) + [(0, padr), (0, 0)])
        parts.append(b)
    return jnp.concatenate(parts, axis=parts[0].ndim - 2)


def _expand_w_in(wt):
    z64 = jnp.zeros((64, D_MODEL), wt.dtype)
    z32 = jnp.zeros((32, D_MODEL), wt.dtype)
    return jnp.concatenate([
        wt[1184:2208], wt[2208:3232], _pad_heads(wt[416:928], H_B, 0), wt[0:256],
        _pad_heads(wt[928:1056], KV_B, 0), _pad_heads(wt[1056:1184], KV_B, 0), wt[256:384],
        z64, wt[384:416], z32], axis=0)


def _collapse_w_in(dw):
    dg, dq, ds = dw[0:2048], dw[2048:3072], dw[3072:4096]
    return jnp.concatenate([
        ds[0:256], ds[768:896], ds[960:992], _unpad_heads(dq, H_B, HD_B, 0), _unpad_heads(ds[256:512], KV_B, HD_B, 0),
        _unpad_heads(ds[512:768], KV_B, HD_B, 0), dg], axis=0)


def kernel(x, p, g_mix, w_in, g_qa, w_qb, g_kva, w_kvb, g_qn, g_kn, w_oa, w_ob, w_o, g_mlp, w_up, w_down, g_ple, w_ple_gate, w_ple, g_final, loss_target, m_g_mix, m_w_in, m_g_qa, m_w_qb, m_g_kva, m_w_kvb, m_g_qn, m_g_kn, m_w_oa, m_w_ob, m_w_o, m_g_mlp, m_w_up, m_w_down, m_g_ple, m_w_ple_gate, m_w_ple, m_g_final, v_g_mix, v_w_in, v_g_qa, v_w_qb, v_g_kva, v_w_kvb, v_g_qn, v_g_kn, v_w_oa, v_w_ob, v_w_o, v_g_mlp, v_w_up, v_w_down, v_g_ple, v_w_ple_gate, v_w_ple, v_g_final):
    n_b, s_len, _ = x.shape
    t = n_b * s_len
    tm = min(512, s_len)
    tq_f = min(2048, s_len)
    tq_b = min(2048, s_len)

    mats = dict(w_in=(w_in, m_w_in, v_w_in), w_qb=(w_qb, m_w_qb, v_w_qb), w_kvb=(w_kvb, m_w_kvb, v_w_kvb),
                w_oa=(w_oa, m_w_oa, v_w_oa), w_ob=(w_ob, m_w_ob, v_w_ob), w_o=(w_o, m_w_o, v_w_o),
                w_up=(w_up, m_w_up, v_w_up), w_down=(w_down, m_w_down, v_w_down),
                w_ple_gate=(w_ple_gate, m_w_ple_gate, v_w_ple_gate), w_ple=(w_ple, m_w_ple, v_w_ple))
    col_sharded = ("w_in", "w_qb", "w_kvb", "w_oa", "w_ob", "w_up", "w_ple")

    blocks = {}
    for name in PACK_W1 + ("w_oa", "w_ob", "w_ple"):
        blocks[name] = mats[name][0][0].T.reshape(-1, D_MODEL).astype(BF16)
    off_w1, _ = _pack_offsets(PACK_W1)
    off_w2, _ = _pack_offsets(PACK_W2)
    off_w3, _ = _pack_offsets(PACK_W3)
    xf = x.reshape(t, D_MODEL)
    h, full1 = norm_x(xf, g_mix, tm, comm=gather3_first_comm(_pack_rows(blocks, PACK_W1)))
    pack2, pack3, full1 = pack_late_weights(w_up, w_down, w_o, w_ple_gate, _pack_rows(blocks, ("w_oa", "w_ob", "w_ple")),
                                            comm=gather3_second_comm(full1, then_third=True))

    def gathered(full, offs, name, rows, width):
        return full[:, offs[name]:offs[name] + rows].reshape(-1, width)

    w_in_t = _expand_w_in(gathered(full1, off_w1, "w_in", 404, D_MODEL))
    w_qb_t = _pad_heads(gathered(full1, off_w1, "w_qb", 24, Q_LORA), H_A, 0)
    wkvb = gathered(full1, off_w1, "w_kvb", 16, KV_LORA).reshape(H_A, 2, 64, KV_LORA)
    w_kvb_t = jnp.concatenate([_pad_heads(wkvb[:, 0].reshape(-1, KV_LORA), H_A, 0),
                               _pad_heads(wkvb[:, 1].reshape(-1, KV_LORA), H_A, 0)], axis=0)

    tabs = _rope_tables(s_len)
    g_qn_p = jnp.pad(g_qn, ((0, 0), (0, HP - HD_B)))
    g_kn_p = jnp.pad(g_kn, ((0, 0), (0, HP - HD_B)))
    pf = p.reshape(t, PLE_DIM)
    tgt = loss_target.reshape(t, D_MODEL)

    zg, zs, full2 = in_proj(h, w_in_t, tm, comm=gather3_first_comm(pack2))
    qa, ka, va, qb, kb, vb, cq, ckv, full2 = attn_prep(zs, tabs, g_qa, g_kva, g_qn_p, g_kn_p, w_qb_t, w_kvb_t, tm, s_len,
                                                       comm=gather3_second_comm(full2))
    oa, lse_a, full3, full2 = attn_fwd(qa, ka, va, n_b, s_len, tq_f, "attn_a_fwd",
                                       comm=_join_comms(gather_first_comm(pack3), gather3_third_comm(full2)))
    ob, lse_b, full3 = attn_fwd(qb, kb, vb, n_b, s_len, tq_f, "attn_b_fwd", comm=gather_pass_comm(full3))
    w_oa_t = _pad_heads(gathered(full2, off_w2, "w_oa", 64, H_A * V_DIM_A), H_A, 1)
    w_ob_t = _pad_heads(gathered(full2, off_w2, "w_ob", 64, H_B * HD_B), H_B, 1)
    w_ple_t = gathered(full2, off_w2, "w_ple", 32, PLE_DIM)
    x1, merged, yab = merge_fwd(oa, ob, zg, xf, w_oa_t, w_ob_t, full2, off_w2, tm)
    x2, u = mlp_fwd(x1, g_mlp, full3, off_w3, tm)
    dx2, dt, h3, dpe, st_ple, dx2b = ple_loss_bwd(x2, pf, tgt, g_ple, g_final.reshape(1, D_MODEL), full2, off_w2, w_ple_t, tm)
    dx1, da, h2, st_mlp, dx1b = mlp_bwd(dx2, x1, u, g_mlp, full3, off_w3, tm)

    core = lax.axis_index("c").astype(jnp.int32).reshape(1)
    chip = (2 * lax.axis_index("x") + lax.axis_index("y")).astype(jnp.int32).reshape(1)

    def packed(gblocks, names):
        return _pack_rows({n: gblocks[n].reshape(N_DEV, -1, D_MODEL).astype(BF16) for n in names}, names)

    off_g1, rows_g1 = _pack_offsets(PACK_G1)
    gpack1 = matmul_tn_packed(da, h2, "gw_up", 512, off_g1["w_up"], rows_g1)
    gpack1 = matmul_tn_packed(u, dx2b, "gw_down", 512, off_g1["w_down"], rows_g1, buf=gpack1, square_a=True)
    gpack1 = matmul_tn_packed(h3, dt, "gw_pg", 128, off_g1["w_ple_gate"], rows_g1, buf=gpack1)
    gple = matmul_tn(dpe, pf, "gw_ple").reshape(N_DEV, -1, D_MODEL).astype(BF16)
    gpack1 = lax.dynamic_update_slice(gpack1, gple, (0, off_g1["w_ple"], 0))
    doa, dob, dz, dya, dyb, got1 = merge_bwd(dx1b, yab, zg, w_oa_t, w_ob_t, full2, off_w2, tm,
                                               comm=scatter_sibling_comm(gpack1))
    part1 = add_pairs(gpack1, got1, core)

    off_g2, rows_g2 = _pack_offsets(PACK_G2)
    g2 = dict(w_oa=_unpad_heads(matmul_tn(dya, oa, "gw_oa"), H_A, V_DIM_A, 1),
              w_ob=_unpad_heads(matmul_tn(dyb, ob, "gw_ob"), H_B, HD_B, 1))
    gpack2 = matmul_tn_packed(merged, dx1b, "gw_o", 128, off_g2["w_o"], rows_g2)
    gpack2 = lax.dynamic_update_slice(gpack2, packed(g2, ("w_oa", "w_ob")), (0, off_g2["w_oa"], 0))
    dqa, dka, dva, land1, got2 = attn_bwd(qa, ka, va, doa, oa, lse_a, n_b, s_len, tq_b, "attn_a_bwd",
                                          comm=_join_comms(scatter_chips_comm(part1), scatter_sibling_comm(gpack2)))
    gshard1 = sum_chips(part1, land1, chip)
    part2 = add_pairs(gpack2, got2, core)
    dqb, dkb, dvb, land2 = attn_bwd(qb, kb, vb, dob, ob, lse_b, n_b, s_len, tq_b, "attn_b_bwd", comm=scatter_chips_comm(part2))
    gshard2 = sum_chips(part2, land2, chip)
    dz, dqap, dkva, st_prep = prep_bwd(dqa, dka, dva, dqb, dkb, dvb, zs, dz, tabs, g_qa, g_kva, g_qn_p, g_kn_p,
                                       w_qb_t, w_kvb_t, tm, s_len)

    gkv = matmul_tn(dkva, ckv, "gw_kvb")
    g3 = dict(
        w_in=_collapse_w_in(matmul_tn(dz, h, "gw_in")),
        w_qb=_unpad_heads(matmul_tn(dqap, cq, "gw_qb"), H_A, QK_NOPE + QK_ROPE, 0),
        w_kvb=jnp.stack([_unpad_heads(gkv[:H_A * HP], H_A, 64, 0).reshape(H_A, 64, KV_LORA),
                         _unpad_heads(gkv[H_A * HP:], H_A, 64, 0).reshape(H_A, 64, KV_LORA)], axis=1))
    gpack3 = packed(g3, PACK_G3)
    part3 = add_pairs(gpack3, exchange_sibling(gpack3), core)
    grad_x, st_mix, land3 = in_bwd(dz, xf, dx1, g_mix, w_in_t, tm, comm=scatter_chips_comm(part3))
    gshard3 = sum_chips(part3, land3, chip)
    off_g3, _ = _pack_offsets(PACK_G3)
    shards = {n: (gshard1, off_g1[n]) for n in PACK_G1}
    shards.update({n: (gshard2, off_g2[n]) for n in PACK_G2})
    shards.update({n: (gshard3, off_g3[n]) for n in PACK_G3})

    stats = allreduce_stats(st_mix, st_prep, st_mlp, st_ple)
    loss = jnp.sum(stats[ST_LOSS])

    out_g, out_d, out_m, out_v = {}, {}, {}, {}
    for name, (w, m, v) in mats.items():
        gshard, off = shards[name]
        r, c = w.shape[1:]
        if name in col_sharded:
            g2 = gshard[off:off + (r * c) // D_MODEL].reshape(c, r)
            if r % 128 == 0 and c % 128 == 0:
                res = adamw(w, g2, m, v, "adamw_" + name, g_transposed=True)
            else:
                res = adamw(w[0].T[None], g2, m[0].T[None], v[0].T[None], "adamw_" + name)
                res = tuple(a[0].T[None] for a in res)
        else:
            res = adamw(w, gshard[off:off + r], m, v, "adamw_" + name)
        out_g[name], out_d[name], out_m[name], out_v[name] = res

    gains = (("g_mix", g_mix, m_g_mix, v_g_mix, ST_G_MIX), ("g_qa", g_qa, m_g_qa, v_g_qa, ST_G_QA),
             ("g_kva", g_kva, m_g_kva, v_g_kva, ST_G_KVA), ("g_qn", g_qn, m_g_qn, v_g_qn, ST_G_QN),
             ("g_kn", g_kn, m_g_kn, v_g_kn, ST_G_KN), ("g_mlp", g_mlp, m_g_mlp, v_g_mlp, ST_G_MLP),
             ("g_ple", g_ple, m_g_ple, v_g_ple, ST_G_PLE), ("g_final", g_final, m_g_final, v_g_final, ST_G_FINAL))
    res = adamw_gains(stats, [(r_, w.reshape(1, -1), m.reshape(1, -1), v.reshape(1, -1)) for _, w, m, v, r_ in gains])
    for (name, w, _, _, _), (gg, gd, gm, gv) in zip(gains, res):
        out_g[name], out_d[name], out_m[name], out_v[name] = (a.reshape(w.shape) for a in (gg, gd, gm, gv))

    order = ("g_mix", "w_in", "g_qa", "w_qb", "g_kva", "w_kvb", "g_qn", "g_kn", "w_oa", "w_ob", "w_o", "g_mlp",
             "w_up", "w_down", "g_ple", "w_ple_gate", "w_ple", "g_final")
    return (loss, grad_x.reshape(x.shape), *[out_g[n] for n in order], *[out_d[n] for n in order],
            *[out_m[n] for n in order], *[out_v[n] for n in order])
```

```python
import numpy as np
import jax
import jax.numpy as jnp
from jax import lax
from jax.experimental import pallas as pl
from jax.experimental.pallas import tpu as pltpu

F32 = jnp.float32
BF16 = jnp.bfloat16

D_MODEL = 1024
EPS = 1e-6
ROPE_THETA = 10000.0
GRID_W = 64
H_A = 8
QK_NOPE = 64
QK_ROPE = 32
V_DIM_A = 64
Q_LORA = 256
KV_LORA = 128
H_B = 8
KV_B = 2
HD_B = 64
D_FF = 4 * D_MODEL
PLE_DIM = 256
HP = 128
ZP = 4096
N_DEV = 8
N_CHIP = 4

ADAM_LR = 0.001
ADAM_B1 = 0.9
ADAM_B2 = 0.999
ADAM_EPS = 1e-08
ADAM_WD = 0.01
ADAM_STEP = 10

VMEM_LIMIT = 52 * 1024 * 1024

PACK_ROWS = dict(w_in=416, w_qb=32, w_kvb=16, w_oa=64, w_ob=64, w_o=128, w_up=512, w_down=512, w_ple_gate=128, w_ple=32)
PACK_W1 = ("w_in", "w_qb", "w_kvb")
PACK_W2 = ("w_o", "w_ple_gate", "w_oa", "w_ob", "w_ple")
PACK_W3 = ("w_up", "w_down")
PACK_G1 = ("w_up", "w_down", "w_ple_gate", "w_ple")
PACK_G2 = ("w_o", "w_oa", "w_ob")
PACK_G3 = ("w_in", "w_qb", "w_kvb")


def _pack_offsets(names):
    off, o = {}, 0
    for n in names:
        off[n] = o
        o += PACK_ROWS[n]
    return off, o

ST_G_MIX, ST_G_QA, ST_G_KVA, ST_G_QN, ST_G_KN, ST_G_MLP, ST_G_PLE, ST_G_FINAL, ST_LOSS = range(9)
ST_ROWS = 16


def _dot_nn(a, b):
    return lax.dot_general(a, b, (((1,), (0,)), ((), ())), preferred_element_type=F32)


def _dot_nt(a, b):
    return lax.dot_general(a, b, (((1,), (1,)), ((), ())), preferred_element_type=F32)


def _dot_tn(a, b):
    return lax.dot_general(a, b, (((0,), (0,)), ((), ())), preferred_element_type=F32)


def _rstd(x, n):
    return lax.rsqrt(jnp.sum(x * x, axis=-1, keepdims=True) * (1.0 / n) + EPS)


def _rms_bwd(dy, xh, r, g, n):
    dxh = dy * g
    return r * (dxh - xh * (jnp.sum(dxh * xh, axis=-1, keepdims=True) * (1.0 / n)))


def _rope_fwd(x, c, s1, s2):
    return x * c + pltpu.roll(x, HP - 16, 1) * s1 + pltpu.roll(x, 16, 1) * s2


def _rope_bwd(d, c, s1, s2):
    return d * c + pltpu.roll(d * s1, 16, 1) + pltpu.roll(d * s2, HP - 16, 1)


def _colsum(v):
    return jnp.sum(v, axis=0, keepdims=True)


def _params(sem=None, vmem=VMEM_LIMIT):
    return pltpu.CompilerParams(dimension_semantics=sem, vmem_limit_bytes=vmem)


def _resident(shape):
    nd = len(shape)
    return pl.BlockSpec(shape, lambda *_: (0,) * nd, pipeline_mode=pl.Buffered(1))


def _rows(tm, width, col=0):
    return pl.BlockSpec((tm, width), lambda i: (i, col))


def _packed_weight(rows, off):
    return pl.BlockSpec((N_DEV, rows, D_MODEL), lambda *_: (0, off // rows, 0), pipeline_mode=pl.Buffered(1))


def _wrows(ref, start, size):
    rows = ref.shape[1]
    return ref[start // rows:(start + size) // rows].reshape(size, D_MODEL)


def _mesh_pos():
    return lax.axis_index("x"), lax.axis_index("y"), lax.axis_index("c")


def _flip(v, bit):
    return (1 - v) if bit else v


_ANY = pl.BlockSpec(memory_space=pl.ANY)
_MESH = pl.DeviceIdType.MESH


def _remote(src, dst, send_sems, recv_sems, k, to):
    return pltpu.make_async_remote_copy(src_ref=src, dst_ref=dst, send_sem=send_sems.at[k], recv_sem=recv_sems.at[k],
                                        device_id=to, device_id_type=_MESH)


def _sibling_copies(g_ref, got_ref, send_sems, recv_sems):
    x, y, c = _mesh_pos()
    return [_remote(g_ref.at[2 * j + (1 - c)], got_ref.at[j], send_sems, recv_sems, j, (x, y, 1 - c)) for j in range(N_CHIP)]


def _chip_copies(p_ref, land_ref, send_sems, recv_sems):
    x, y, c = _mesh_pos()
    copies = []
    for k in (1, 2, 3):
        tx, ty = _flip(x, k & 2), _flip(y, k & 1)
        copies.append(_remote(p_ref.at[2 * tx + ty], land_ref.at[k - 1], send_sems, recv_sems, k - 1, (tx, ty, c)))
    return copies


class _Comm:
    def __init__(self, ins, out_shapes, sems, make, aliases=None, make_tail=None):
        self.ins, self.out_shapes, self.sems, self.make, self.aliases = list(ins), list(out_shapes), list(sems), make, aliases or {}
        self.make_tail = make_tail


def _join_comms(a, b):
    assert a.make_tail is None and b.make_tail is None
    n_i, n_o, n_s = len(a.ins), len(a.out_shapes), len(a.sems)

    def make(cin, cout, sems):
        return a.make(cin[:n_i], cout[:n_o], sems[:n_s]) + b.make(cin[n_i:], cout[n_o:], sems[n_s:])

    aliases = dict(a.aliases)
    aliases.update({n_i + j: n_o + k for j, k in b.aliases.items()})
    return _Comm(a.ins + b.ins, a.out_shapes + b.out_shapes, a.sems + b.sems, make, aliases)


def _comm_parts(comm, n_in, n_out):
    if comm is None:
        return [], [], [], [], {}
    alias = {n_in + j: n_out + k for j, k in comm.aliases.items()}
    return comm.ins, [_ANY] * len(comm.ins), comm.out_shapes, comm.sems, alias


def _split_refs(refs, n_in, n_out, n_scratch, comm):
    n_ci = len(comm.ins) if comm else 0
    n_co = len(comm.out_shapes) if comm else 0
    cuts, i = [], 0
    for n in (n_in, n_ci, n_out, n_co, n_scratch):
        cuts.append(refs[i:i + n])
        i += n
    return (*cuts, refs[i:])


def _grid_edge(grid, last):
    cond = None
    for d, n in enumerate(grid):
        here = pl.program_id(d) == (n - 1 if last else 0)
        cond = here if cond is None else cond & here
    return cond


def _comm_start(comm, cin, cout, csem, grid):
    if comm is not None:
        @pl.when(_grid_edge(grid, False))
        def _():
            for cp in comm.make(cin, cout, csem):
                cp.start()


def _comm_finish(comm, cin, cout, csem, grid):
    if comm is not None:
        @pl.when(_grid_edge(grid, True))
        def _():
            for cp in comm.make(cin, cout, csem):
                cp.wait()
            if comm.make_tail is not None:
                tail = comm.make_tail(cin, cout, csem)
                for cp in tail:
                    cp.start()
                for cp in tail:
                    cp.wait()


def gather_first_comm(shard):
    r, w = shard.shape

    def make(cin, cout, sems):
        (x_ref,), (out_ref,), (send_sems, recv_sems, local_sem) = cin, cout, sems
        x, y, c = _mesh_pos()
        mine = out_ref.at[4 * x + 2 * y + c]
        targets = [(x, y, 1 - c), (1 - x, y, c), (x, 1 - y, c), (1 - x, 1 - y, c)]
        return [_remote(x_ref, mine, send_sems, recv_sems, k, to) for k, to in enumerate(targets)] + [
            pltpu.make_async_copy(x_ref, mine, local_sem)]

    return _Comm([shard], [jax.ShapeDtypeStruct((N_DEV, r, w), shard.dtype)],
                 [pltpu.SemaphoreType.DMA((4,)), pltpu.SemaphoreType.DMA((4,)), pltpu.SemaphoreType.DMA], make)


def gather_pass_comm(full):
    def make(cin, cout, sems):
        (in_ref,), (out_ref,), (send_sems, recv_sems) = cin, cout, sems
        x, y, c = _mesh_pos()
        copies = []
        for k, (px, py) in enumerate([(1 - x, y), (x, 1 - y), (1 - x, 1 - y)]):
            idx = 4 * px + 2 * py + c
            copies.append(_remote(in_ref.at[idx], out_ref.at[idx], send_sems, recv_sems, k, (x, y, 1 - c)))
        return copies

    return _Comm([full], [jax.ShapeDtypeStruct(full.shape, full.dtype)],
                 [pltpu.SemaphoreType.DMA((3,)), pltpu.SemaphoreType.DMA((3,))], make, aliases={0: 0})


def gather3_first_comm(shard):
    r, w = shard.shape

    def make(cin, cout, sems):
        (x_ref,), (out_ref,), (send_sems, recv_sems, local_sem) = cin, cout, sems
        x, y, c = _mesh_pos()
        mine = out_ref.at[4 * x + 2 * y + c]
        targets = [(x, y, 1 - c), (1 - x, y, c), (x, 1 - y, c)]
        return [_remote(x_ref, mine, send_sems, recv_sems, k, to) for k, to in enumerate(targets)] + [
            pltpu.make_async_copy(x_ref, mine, local_sem)]

    return _Comm([shard], [jax.ShapeDtypeStruct((N_DEV, r, w), shard.dtype)],
                 [pltpu.SemaphoreType.DMA((3,)), pltpu.SemaphoreType.DMA((3,)), pltpu.SemaphoreType.DMA], make)


def gather3_second_comm(full, then_third=False):
    def make(cin, cout, sems):
        (in_ref,), (out_ref,), (send_sems, recv_sems) = cin, cout, sems
        x, y, c = _mesh_pos()
        copies = []
        for k, (px, py) in enumerate([(1 - x, y), (x, 1 - y)]):
            idx = 4 * px + 2 * py + c
            copies.append(_remote(in_ref.at[idx], out_ref.at[idx], send_sems, recv_sems, k, (x, y, 1 - c)))
        fx, fy = x * c + (1 - x) * (1 - c), y * (1 - c) + (1 - y) * c
        tx, ty = x * (1 - c) + (1 - x) * c, (1 - y) * (1 - c) + y * c
        idx = 4 * fx + 2 * fy + c
        copies.append(_remote(in_ref.at[idx], out_ref.at[idx], send_sems, recv_sems, 2, (tx, ty, c)))
        return copies

    def third(cin, cout, sems):
        (out_ref,), (send_sems, recv_sems) = cout, sems
        x, y, c = _mesh_pos()
        idx = 4 * (1 - x) + 2 * (1 - y) + c
        return [_remote(out_ref.at[idx], out_ref.at[idx], send_sems, recv_sems, 3, (x, y, 1 - c))]

    n_sem = 4 if then_third else 3
    return _Comm([full], [jax.ShapeDtypeStruct(full.shape, full.dtype)],
                 [pltpu.SemaphoreType.DMA((n_sem,)), pltpu.SemaphoreType.DMA((n_sem,))], make, aliases={0: 0},
                 make_tail=third if then_third else None)


def gather3_third_comm(full):
    def make(cin, cout, sems):
        (in_ref,), (out_ref,), (send_sems, recv_sems) = cin, cout, sems
        x, y, c = _mesh_pos()
        idx = 4 * (1 - x) + 2 * (1 - y) + c
        return [_remote(in_ref.at[idx], out_ref.at[idx], send_sems, recv_sems, 0, (x, y, 1 - c))]

    return _Comm([full], [jax.ShapeDtypeStruct(full.shape, full.dtype)],
                 [pltpu.SemaphoreType.DMA((1,)), pltpu.SemaphoreType.DMA((1,))], make, aliases={0: 0})


def scatter_sibling_comm(g):
    _, r, w = g.shape
    return _Comm([g], [jax.ShapeDtypeStruct((N_CHIP, r, w), g.dtype)],
                 [pltpu.SemaphoreType.DMA((N_CHIP,)), pltpu.SemaphoreType.DMA((N_CHIP,))],
                 lambda cin, cout, sems: _sibling_copies(cin[0], cout[0], sems[0], sems[1]))


def scatter_chips_comm(part):
    _, r, w = part.shape
    return _Comm([part], [jax.ShapeDtypeStruct((N_CHIP - 1, r, w), part.dtype)],
                 [pltpu.SemaphoreType.DMA((3,)), pltpu.SemaphoreType.DMA((3,))],
                 lambda cin, cout, sems: _chip_copies(cin[0], cout[0], sems[0], sems[1]))


def exchange_sibling(g):
    _, r, w = g.shape

    def body(g_ref, got_ref, send_sems, recv_sems):
        copies = _sibling_copies(g_ref, got_ref, send_sems, recv_sems)
        for cp in copies:
            cp.start()
        for cp in copies:
            cp.wait()

    return pl.pallas_call(
        body, name="exchange_sibling", out_shape=jax.ShapeDtypeStruct((N_CHIP, r, w), g.dtype),
        in_specs=[_ANY], out_specs=_ANY,
        scratch_shapes=[pltpu.SemaphoreType.DMA((N_CHIP,)), pltpu.SemaphoreType.DMA((N_CHIP,))],
    )(g)


def exchange_chips(part):
    _, r, w = part.shape

    def body(p_ref, land_ref, send_sems, recv_sems):
        copies = _chip_copies(p_ref, land_ref, send_sems, recv_sems)
        for cp in copies:
            cp.start()
        for cp in copies:
            cp.wait()

    return pl.pallas_call(
        body, name="exchange_chips", out_shape=jax.ShapeDtypeStruct((N_CHIP - 1, r, w), part.dtype),
        in_specs=[_ANY], out_specs=_ANY,
        scratch_shapes=[pltpu.SemaphoreType.DMA((3,)), pltpu.SemaphoreType.DMA((3,))],
    )(part)


def allreduce_stats(st_mix, st_prep, st_mlp, st_ple):
    def body(mix_ref, prep_ref, mlp_ref, ple_ref, out_ref, mine, gath, send_sems, recv_sems):
        x, y, c = _mesh_pos()
        me = 4 * x + 2 * y + c
        mine[...] = jnp.zeros_like(mine)
        mine[ST_G_MIX:ST_G_MIX + 1, :] = mix_ref[...]
        mine[ST_G_QA:ST_G_KN + 1, 0:256] = prep_ref[...]
        mine[ST_G_MLP:ST_G_MLP + 1, :] = mlp_ref[...]
        mine[ST_G_PLE:ST_LOSS + 1, :] = ple_ref[...]
        gath[me] = mine[...]
        copies = []
        for k in range(1, N_DEV):
            peer = (_flip(x, k & 4), _flip(y, k & 2), _flip(c, k & 1))
            copies.append(_remote(mine, gath.at[me], send_sems, recv_sems, k - 1, peer))
        for cp in copies:
            cp.start()
        for cp in copies:
            cp.wait()
        acc = gath[0]
        for d in range(1, N_DEV):
            acc = acc + gath[d]
        out_ref[...] = acc

    vm = pl.BlockSpec(memory_space=pltpu.VMEM)
    return pl.pallas_call(
        body, name="allreduce_stats", out_shape=jax.ShapeDtypeStruct((ST_ROWS, D_MODEL), F32),
        in_specs=[vm] * 4, out_specs=vm,
        scratch_shapes=[pltpu.VMEM((ST_ROWS, D_MODEL), F32), pltpu.VMEM((N_DEV, ST_ROWS, D_MODEL), F32),
                        pltpu.SemaphoreType.DMA((N_DEV - 1,)), pltpu.SemaphoreType.DMA((N_DEV - 1,))],
    )(st_mix, st_prep, st_mlp, st_ple)


def adamw_gains(stats, gains):
    c1 = 1.0 - ADAM_B1 ** ADAM_STEP
    c2 = 1.0 - ADAM_B2 ** ADAM_STEP
    n = len(gains)

    def body(st_ref, *refs):
        ins, outs = refs[:3 * n], refs[3 * n:]
        for i, (row, w, _, _) in enumerate(gains):
            width = w.shape[1]
            gv = st_ref[row:row + 1, 0:width]
            mn = ADAM_B1 * ins[3 * i + 1][...] + (1.0 - ADAM_B1) * gv
            vn = ADAM_B2 * ins[3 * i + 2][...] + (1.0 - ADAM_B2) * (gv * gv)
            outs[4 * i][...] = gv
            outs[4 * i + 1][...] = -ADAM_LR * ((mn / c1) / (jnp.sqrt(vn / c2) + ADAM_EPS) + ADAM_WD * ins[3 * i][...])
            outs[4 * i + 2][...] = mn
            outs[4 * i + 3][...] = vn

    vm = pl.BlockSpec(memory_space=pltpu.VMEM)
    flat = [a for (_, w, m, v) in gains for a in (w, m, v)]
    out_shape = tuple(jax.ShapeDtypeStruct(w.shape, F32) for (_, w, _, _) in gains for _ in range(4))
    res = pl.pallas_call(body, name="adamw_gains", out_shape=out_shape, in_specs=[vm] * (1 + 3 * n),
                         out_specs=tuple([vm] * (4 * n)))(stats, *flat)
    return [res[4 * i:4 * i + 4] for i in range(n)]


def _row_tile(r, cap=640):
    return max(d for d in range(16, min(r, cap) + 1, 16) if r % d == 0)


def add_pairs(g, got, core):
    n, r, w = got.shape
    tr = _row_tile(r)

    def body(c_ref, a_ref, b_ref, o_ref):
        o_ref[...] = (a_ref[...].astype(F32) + b_ref[...].astype(F32)).astype(o_ref.dtype)

    spec = pl.BlockSpec((1, tr, w), lambda i, j, c: (i, j, 0))
    return pl.pallas_call(
        body, name="add_pairs", out_shape=jax.ShapeDtypeStruct(got.shape, got.dtype),
        grid_spec=pltpu.PrefetchScalarGridSpec(
            num_scalar_prefetch=1, grid=(n, r // tr),
            in_specs=[pl.BlockSpec((1, tr, w), lambda i, j, c: (2 * i + c[0], j, 0)), spec], out_specs=spec),
        compiler_params=_params(("parallel", "parallel")),
    )(core, g, got)


def sum_chips(part, land, chip):
    _, r, w = part.shape
    tr = _row_tile(r)

    def body(c_ref, p_ref, l_ref, o_ref):
        acc = p_ref[0].astype(F32)
        for s in range(N_CHIP - 1):
            acc = acc + l_ref[s].astype(F32)
        o_ref[...] = acc

    return pl.pallas_call(
        body, name="sum_chips", out_shape=jax.ShapeDtypeStruct((r, w), F32),
        grid_spec=pltpu.PrefetchScalarGridSpec(
            num_scalar_prefetch=1, grid=(r // tr,),
            in_specs=[pl.BlockSpec((1, tr, w), lambda i, c: (c[0], i, 0)), pl.BlockSpec((N_CHIP - 1, tr, w), lambda i, c: (0, i, 0))],
            out_specs=pl.BlockSpec((tr, w), lambda i, c: (i, 0))),
        compiler_params=_params(("parallel",)),
    )(chip, part, land)


def norm_x(x, g_mix, tm, comm=None):
    t = x.shape[0]
    grid = (t // tm,)
    c_ins, c_in_specs, c_outs, c_sems, alias = _comm_parts(comm, 2, 1)

    def body(*refs):
        (x_ref, g_ref), cin, (h_ref,), cout, _, csem = _split_refs(refs, 2, 1, 0, comm)
        _comm_start(comm, cin, cout, csem, grid)
        xv = x_ref[...]
        h_ref[...] = (xv * _rstd(xv, D_MODEL) * g_ref[...]).astype(BF16)
        _comm_finish(comm, cin, cout, csem, grid)

    return pl.pallas_call(
        body, name="norm_x", grid=grid, out_shape=(jax.ShapeDtypeStruct((t, D_MODEL), BF16), *c_outs),
        in_specs=[_rows(tm, D_MODEL), _resident((1, D_MODEL)), *c_in_specs],
        out_specs=(_rows(tm, D_MODEL), *([_ANY] * len(c_outs))),
        scratch_shapes=c_sems, input_output_aliases=alias, compiler_params=_params(("arbitrary",)),
    )(x, g_mix, *c_ins)


def pack_late_weights(w_up, w_down, w_o, w_pg, small, comm=None):
    rows2 = sum(PACK_ROWS[n] for n in PACK_W2)
    rows3 = sum(PACK_ROWS[n] for n in PACK_W3)
    c_ins, c_in_specs, c_outs, c_sems, alias = _comm_parts(comm, 5, 2)
    grid = (1,)

    def body(*refs):
        (up_ref, dn_ref, o_ref, pg_ref, sm_ref), cin, (p2_ref, p3_ref), cout, _, csem = _split_refs(refs, 5, 2, 0, comm)
        _comm_start(comm, cin, cout, csem, grid)
        p2_ref[0:128, :] = o_ref[0].astype(BF16)
        p2_ref[128:256, :] = pg_ref[0].astype(BF16)
        p2_ref[256:rows2, :] = sm_ref[...]
        p3_ref[0:512, :] = up_ref[0].T.astype(BF16)
        p3_ref[512:1024, :] = dn_ref[0].astype(BF16)
        _comm_finish(comm, cin, cout, csem, grid)

    def whole(a):
        nd = a.ndim
        return pl.BlockSpec(a.shape, lambda i: (0,) * nd)

    args = (w_up, w_down, w_o, w_pg, small)
    return pl.pallas_call(
        body, name="pack_late_weights", grid=grid,
        out_shape=(jax.ShapeDtypeStruct((rows2, D_MODEL), BF16), jax.ShapeDtypeStruct((rows3, D_MODEL), BF16), *c_outs),
        in_specs=[*[whole(a) for a in args], *c_in_specs],
        out_specs=(pl.BlockSpec((rows2, D_MODEL), lambda i: (0, 0)), pl.BlockSpec((rows3, D_MODEL), lambda i: (0, 0)),
                   *([_ANY] * len(c_outs))),
        scratch_shapes=c_sems, input_output_aliases=alias, compiler_params=_params(("arbitrary",)),
    )(*args, *c_ins)


def in_proj(h, w_in_t, tm, comm=None):
    t = h.shape[0]
    nc = 512
    half = ZP // 2
    grid = (t // tm,)
    c_ins, c_in_specs, c_outs, c_sems, alias = _comm_parts(comm, 2, 2)

    def body(*refs):
        (h_ref, w_ref), cin, (zg_ref, zs_ref), cout, _, csem = _split_refs(refs, 2, 2, 0, comm)
        _comm_start(comm, cin, cout, csem, grid)
        hv = h_ref[...]
        for cidx in range(half // nc):
            zg_ref[:, cidx * nc:(cidx + 1) * nc] = _dot_nt(hv, w_ref[cidx * nc:(cidx + 1) * nc, :]).astype(BF16)
        for cidx in range(half // nc):
            zs_ref[:, cidx * nc:(cidx + 1) * nc] = _dot_nt(hv, w_ref[half + cidx * nc:half + (cidx + 1) * nc, :])
        _comm_finish(comm, cin, cout, csem, grid)

    return pl.pallas_call(
        body, name="in_proj", grid=grid,
        out_shape=(jax.ShapeDtypeStruct((t, half), BF16), jax.ShapeDtypeStruct((t, half), F32), *c_outs),
        in_specs=[_rows(tm, D_MODEL), _resident((ZP, D_MODEL)), *c_in_specs],
        out_specs=(_rows(tm, half), _rows(tm, half), *([_ANY] * len(c_outs))),
        scratch_shapes=c_sems, input_output_aliases=alias, compiler_params=_params(("arbitrary",)),
    )(h, w_in_t, *c_ins)


def attn_prep(zp, tabs, g_qa, g_kva, g_qn, g_kn, w_qb_t, w_kvb_t, tm, s_len, comm=None):
    t = zp.shape[0]
    nsb = s_len // tm
    scale_a = (QK_NOPE + QK_ROPE) ** -0.5
    scale_b = HD_B ** -0.5

    grid = (t // tm,)
    c_ins, c_in_specs, c_outs, c_sems, alias = _comm_parts(comm, 13, 8)

    def body(*refs):
        ((qb_ref, qlat_ref, kb_ref, vb_ref, ckv_ref, kpe_ref, tab_ref, gqa_ref, gkva_ref, gqn_ref, gkn_ref, wqb_ref,
          wkvb_ref), cin, (qa_o, ka_o, va_o, qb_o, kb_o, vb_o, cq_o, ckvn_o), cout, _, csem) = _split_refs(refs, 13, 8, 0, comm)
        _comm_start(comm, cin, cout, csem, grid)
        ca, s1a, s2a = tab_ref[0], tab_ref[1], tab_ref[2]
        ck = tab_ref[3]
        cb, s1b, s2b = tab_ref[4], tab_ref[5], tab_ref[6]
        ql = qlat_ref[...]
        cq = (ql * _rstd(ql, Q_LORA) * gqa_ref[...]).astype(BF16)
        cq_o[...] = cq
        qa = _dot_nt(cq, wqb_ref[...])
        slabs = [slice(h * HP, (h + 1) * HP) for h in range(H_A)]
        qa_o[...] = jnp.concatenate(
            [(_rope_fwd(qa[:, sl], ca, s1a, s2a) * scale_a).astype(BF16) for sl in slabs], axis=1)
        cr = ckv_ref[...]
        ckv = (cr * _rstd(cr, KV_LORA) * gkva_ref[...]).astype(BF16)
        ckvn_o[...] = ckv
        kva = _dot_nt(ckv, wkvb_ref[...])
        kpe = _rope_fwd(kpe_ref[...], ck, s1a, s2a)
        ka_o[...] = jnp.concatenate([(kva[:, sl] + kpe).astype(BF16) for sl in slabs], axis=1)
        va_o[...] = kva[:, H_A * HP:].astype(BF16)
        gqn, gkn = gqn_ref[...], gkn_ref[...]

        def norm_rope(ref, sl, g, scale):
            xs = ref[:, sl]
            y = _rope_fwd(xs * _rstd(xs, HD_B) * g, cb, s1b, s2b)
            return (y if scale is None else y * scale).astype(BF16)

        qb_o[...] = jnp.concatenate([norm_rope(qb_ref, sl, gqn, scale_b) for sl in slabs], axis=1)
        kb_o[...] = jnp.concatenate([norm_rope(kb_ref, sl, gkn, None) for sl in slabs[:KV_B]], axis=1)
        vb_o[...] = vb_ref[...].astype(BF16)
        _comm_finish(comm, cin, cout, csem, grid)

    def o(width):
        return jax.ShapeDtypeStruct((t, width), BF16)

    return pl.pallas_call(
        body, name="attn_prep", grid=grid,
        out_shape=(o(H_A * HP), o(H_A * HP), o(H_A * HP), o(H_B * HP), o(KV_B * HP), o(KV_B * HP), o(Q_LORA), o(KV_LORA),
                   *c_outs),
        in_specs=[_rows(tm, 1024, 0), _rows(tm, 256, 4), _rows(tm, 256, 5), _rows(tm, 256, 6),
                  _rows(tm, 128, 14), _rows(tm, 128, 15),
                  pl.BlockSpec((7, tm, HP), lambda i: (0, i % nsb, 0)),
                  _resident((1, Q_LORA)), _resident((1, KV_LORA)), _resident((1, HP)), _resident((1, HP)),
                  _resident((H_A * HP, Q_LORA)), _resident((2 * H_A * HP, KV_LORA)), *c_in_specs],
        out_specs=(_rows(tm, H_A * HP), _rows(tm, H_A * HP), _rows(tm, H_A * HP), _rows(tm, H_B * HP),
                   _rows(tm, KV_B * HP), _rows(tm, KV_B * HP), _rows(tm, Q_LORA), _rows(tm, KV_LORA), *([_ANY] * len(c_outs))),
        scratch_shapes=c_sems, input_output_aliases=alias, compiler_params=_params(("arbitrary",)),
    )(zp, zp, zp, zp, zp, zp, tabs, g_qa, g_kva, g_qn, g_kn, w_qb_t, w_kvb_t, *c_ins)


def attn_fwd(q, k, v, n_b, s_len, tq, name, comm=None):
    t = q.shape[0]
    n_h, n_hk = q.shape[1] // HP, k.shape[1] // HP
    grp = n_h // n_hk
    nq = s_len // tq
    sub = min(tq, 256)
    grid = (n_b, n_h, nq)
    c_ins, c_in_specs, c_outs, c_sems, alias = _comm_parts(comm, 3, 2)

    def body(*refs):
        (q_ref, k_ref, v_ref), cin, (o_ref, lse_ref), cout, _, csem = _split_refs(refs, 3, 2, 0, comm)
        _comm_start(comm, cin, cout, csem, grid)
        kv, vv = k_ref[...], v_ref[...]
        for r in range(tq // sub):
            rows = slice(r * sub, (r + 1) * sub)
            s = _dot_nt(q_ref[rows, :], kv)
            m = jnp.max(s, axis=-1, keepdims=True)
            p = jnp.exp(s - m)
            l = jnp.sum(p, axis=-1, keepdims=True)
            o_ref[rows, :] = (_dot_nn(p.astype(BF16), vv) * (1.0 / l)).astype(o_ref.dtype)
            lse_ref[rows, :] = jnp.broadcast_to(m + jnp.log(l), (sub, HP))
        _comm_finish(comm, cin, cout, csem, grid)

    qspec = pl.BlockSpec((tq, HP), lambda b, h, i: (b * nq + i, h))
    kspec = pl.BlockSpec((s_len, HP), lambda b, h, i: (b, h // grp))
    return pl.pallas_call(
        body, name=name, grid=grid,
        out_shape=(jax.ShapeDtypeStruct((t, n_h * HP), BF16), jax.ShapeDtypeStruct((t, n_h * HP), F32), *c_outs),
        in_specs=[qspec, kspec, kspec, *c_in_specs], out_specs=(qspec, qspec, *([_ANY] * len(c_outs))),
        scratch_shapes=c_sems, input_output_aliases=alias,
        compiler_params=_params(("arbitrary", "arbitrary", "arbitrary")),
    )(q, k, v, *c_ins)


def merge_fwd(oa, ob, zp, x, w_oa_t, w_ob_t, wpack, off, tm):
    t = x.shape[0]

    def body(oa_ref, ob_ref, ga_ref, gb_ref, x_ref, woa_ref, wob_ref, wo_ref, x1_o, mg_o, y_o):
        ya = _dot_nt(oa_ref[...], woa_ref[...])
        yb = _dot_nt(ob_ref[...], wob_ref[...])
        y_o[:, 0:D_MODEL] = ya.astype(BF16)
        y_o[:, D_MODEL:2 * D_MODEL] = yb.astype(BF16)
        merged = (jax.nn.sigmoid(ga_ref[...].astype(F32)) * ya + jax.nn.sigmoid(gb_ref[...].astype(F32)) * yb).astype(BF16)
        mg_o[...] = merged
        x1_o[...] = x_ref[...] + _dot_nn(merged, _wrows(wo_ref, 0, D_MODEL))

    return pl.pallas_call(
        body, name="merge_fwd", grid=(t // tm,),
        out_shape=(jax.ShapeDtypeStruct((t, D_MODEL), F32), jax.ShapeDtypeStruct((t, D_MODEL), BF16),
                   jax.ShapeDtypeStruct((t, 2 * D_MODEL), BF16)),
        in_specs=[_rows(tm, H_A * HP), _rows(tm, H_B * HP), _rows(tm, 1024, 0), _rows(tm, 1024, 1), _rows(tm, D_MODEL),
                  _resident((D_MODEL, H_A * HP)), _resident((D_MODEL, H_B * HP)), _packed_weight(128, off["w_o"])],
        out_specs=(_rows(tm, D_MODEL), _rows(tm, D_MODEL), _rows(tm, 2 * D_MODEL)), compiler_params=_params(("parallel",)),
    )(oa, ob, zp, zp, x, w_oa_t, w_ob_t, wpack)


def mlp_fwd(x1, g_mlp, wpack, off, tm):
    t = x1.shape[0]
    fc = 1024

    def body(x_ref, g_ref, wup_ref, wdn_ref, x2_o, u_o):
        xv = x_ref[...]
        h2 = (xv * _rstd(xv, D_MODEL) * g_ref[...]).astype(BF16)
        acc = xv
        for cidx in range(D_FF // fc):
            sl = slice(cidx * fc, (cidx + 1) * fc)
            u = jnp.maximum(_dot_nt(h2, _wrows(wup_ref, cidx * fc, fc)), 0.0)
            u_o[:, sl] = u.astype(BF16)
            acc = acc + _dot_nn((u * u).astype(BF16), _wrows(wdn_ref, cidx * fc, fc))
        x2_o[...] = acc

    return pl.pallas_call(
        body, name="mlp_fwd", grid=(t // tm,),
        out_shape=(jax.ShapeDtypeStruct((t, D_MODEL), F32), jax.ShapeDtypeStruct((t, D_FF), BF16)),
        in_specs=[_rows(tm, D_MODEL), _resident((1, D_MODEL)), _packed_weight(512, off["w_up"]), _packed_weight(512, off["w_down"])],
        out_specs=(_rows(tm, D_MODEL), _rows(tm, D_FF)), compiler_params=_params(("parallel",)),
    )(x1, g_mlp, wpack, wpack)


def ple_loss_bwd(x2, p, tgt, g_ple, g_final, wpack, off, w_ple_t, tm):
    t = x2.shape[0]
    inv_d = 1.0 / D_MODEL

    def body(x2_ref, p_ref, tg_ref, gp_ref, gf_ref, wpg_ref, wple_ref, dx2_o, dt_o, h3_o, dpe_o, st_o, dx2b_o):
        @pl.when(pl.program_id(0) == 0)
        def _():
            st_o[...] = jnp.zeros_like(st_o)

        x2v = x2_ref[...]
        gp, gf = gp_ref[...], gf_ref[...]
        w_pg = _wrows(wpg_ref, 0, D_MODEL)
        r2 = _rstd(x2v, D_MODEL)
        xh2 = x2v * r2
        h3 = (xh2 * gp).astype(BF16)
        h3_o[...] = h3
        gate = jax.nn.sigmoid(_dot_nn(h3, w_pg))
        pe = _dot_nt(p_ref[...].astype(BF16), wple_ref[...])
        x3 = x2v + gate * pe
        r3 = _rstd(x3, D_MODEL)
        xh3 = x3 * r3
        err = xh3 * gf - tg_ref[...]
        dx3 = _rms_bwd(err, xh3, r3, gf * inv_d, D_MODEL)
        dpe = dx3 * gate
        dpe_o[...] = dpe.astype(BF16)
        dt = (dpe * pe * (1.0 - gate)).astype(BF16)
        dt_o[...] = dt
        dh3 = _dot_nt(dt, w_pg)
        dx2 = dx3 + _rms_bwd(dh3, xh2, r2, gp, D_MODEL)
        dx2_o[...] = dx2
        dx2b_o[...] = dx2.astype(BF16)
        st_o[0:1, :] += _colsum(dh3 * xh2)
        st_o[1:2, :] += _colsum(err * xh3) * inv_d
        st_o[2:3, :] += _colsum(err * err) * (0.5 * inv_d)

    bf = jax.ShapeDtypeStruct((t, D_MODEL), BF16)
    return pl.pallas_call(
        body, name="ple_loss_bwd", grid=(t // tm,),
        out_shape=(jax.ShapeDtypeStruct((t, D_MODEL), F32), bf, bf, bf, jax.ShapeDtypeStruct((3, D_MODEL), F32), bf),
        in_specs=[_rows(tm, D_MODEL), _rows(tm, PLE_DIM), _rows(tm, D_MODEL), _resident((1, D_MODEL)), _resident((1, D_MODEL)),
                  _packed_weight(128, off["w_ple_gate"]), _resident((D_MODEL, PLE_DIM))],
        out_specs=(_rows(tm, D_MODEL), _rows(tm, D_MODEL), _rows(tm, D_MODEL), _rows(tm, D_MODEL),
                   pl.BlockSpec((3, D_MODEL), lambda i: (0, 0)), _rows(tm, D_MODEL)),
        compiler_params=_params(("arbitrary",)),
    )(x2, p, tgt, g_ple, g_final, wpack, w_ple_t)


def mlp_bwd(dx2, x1, u, g_mlp, wpack, off, tm):
    t = x1.shape[0]
    fc = 1024

    def body(dx2_ref, x1_ref, u_ref, g_ref, wup_ref, wdn_ref, dx1_o, da_o, h2_o, st_o, dx1b_o):
        @pl.when(pl.program_id(0) == 0)
        def _():
            st_o[...] = jnp.zeros_like(st_o)

        d2 = dx2_ref[...]
        d2b = d2.astype(BF16)
        dh2 = jnp.zeros((tm, D_MODEL), F32)
        for cidx in range(D_FF // fc):
            sl = slice(cidx * fc, (cidx + 1) * fc)
            da = (_dot_nt(d2b, _wrows(wdn_ref, cidx * fc, fc)) * (2.0 * u_ref[:, sl].astype(F32))).astype(BF16)
            da_o[:, sl] = da
            dh2 = dh2 + _dot_nn(da, _wrows(wup_ref, cidx * fc, fc))
        xv = x1_ref[...]
        g = g_ref[...]
        r1 = _rstd(xv, D_MODEL)
        xh1 = xv * r1
        h2_o[...] = (xh1 * g).astype(BF16)
        st_o[...] += _colsum(dh2 * xh1)
        dx1 = d2 + _rms_bwd(dh2, xh1, r1, g, D_MODEL)
        dx1_o[...] = dx1
        dx1b_o[...] = dx1.astype(BF16)

    return pl.pallas_call(
        body, name="mlp_bwd", grid=(t // tm,),
        out_shape=(jax.ShapeDtypeStruct((t, D_MODEL), F32), jax.ShapeDtypeStruct((t, D_FF), BF16),
                   jax.ShapeDtypeStruct((t, D_MODEL), BF16), jax.ShapeDtypeStruct((1, D_MODEL), F32),
                   jax.ShapeDtypeStruct((t, D_MODEL), BF16)),
        in_specs=[_rows(tm, D_MODEL), _rows(tm, D_MODEL), _rows(tm, D_FF), _resident((1, D_MODEL)),
                  _packed_weight(512, off["w_up"]), _packed_weight(512, off["w_down"])],
        out_specs=(_rows(tm, D_MODEL), _rows(tm, D_FF), _rows(tm, D_MODEL), pl.BlockSpec((1, D_MODEL), lambda i: (0, 0)),
                   _rows(tm, D_MODEL)),
        compiler_params=_params(("arbitrary",)),
    )(dx2, x1, u, g_mlp, wpack, wpack)


def merge_bwd(dx1b, yab, zp, w_oa_t, w_ob_t, wpack, off, tm, comm=None):
    t = dx1b.shape[0]
    grid = (t // tm,)
    c_ins, c_in_specs, c_outs, c_sems, alias = _comm_parts(comm, 7, 5)

    def body(*refs):
        ((dx1_ref, y_ref, ga_ref, gb_ref, woa_ref, wob_ref, wo_ref), cin,
         (doa_o, dob_o, dg_o, dya_o, dyb_o), cout, _, csem) = _split_refs(refs, 7, 5, 0, comm)
        _comm_start(comm, cin, cout, csem, grid)
        dm = _dot_nt(dx1_ref[...], _wrows(wo_ref, 0, D_MODEL))
        for g_ref, w_ref, do_o, dy_o, col in ((ga_ref, woa_ref, doa_o, dya_o, 0), (gb_ref, wob_ref, dob_o, dyb_o, 1)):
            cols = slice(col * D_MODEL, (col + 1) * D_MODEL)
            sg = jax.nn.sigmoid(g_ref[...].astype(F32))
            dyv = (dm * sg).astype(BF16)
            dy_o[...] = dyv
            dg_o[:, cols] = (dm * y_ref[:, cols].astype(F32) * sg * (1.0 - sg)).astype(BF16)
            do_o[...] = _dot_nn(dyv, w_ref[...]).astype(BF16)
        _comm_finish(comm, cin, cout, csem, grid)

    bf = jax.ShapeDtypeStruct((t, D_MODEL), BF16)
    return pl.pallas_call(
        body, name="merge_bwd", grid=grid,
        out_shape=(bf, bf, jax.ShapeDtypeStruct((t, ZP), BF16), bf, bf, *c_outs),
        in_specs=[_rows(tm, D_MODEL), _rows(tm, 2 * D_MODEL), _rows(tm, 1024, 0), _rows(tm, 1024, 1),
                  _resident((D_MODEL, H_A * HP)), _resident((D_MODEL, H_B * HP)), _packed_weight(128, off["w_o"]), *c_in_specs],
        out_specs=(_rows(tm, D_MODEL), _rows(tm, D_MODEL), _rows(tm, 2 * D_MODEL), _rows(tm, D_MODEL), _rows(tm, D_MODEL),
                   *([_ANY] * len(c_outs))),
        scratch_shapes=c_sems, input_output_aliases=alias,
        compiler_params=_params(("arbitrary",)),
    )(dx1b, yab, zp, zp, w_oa_t, w_ob_t, wpack, *c_ins)


def attn_bwd(q, k, v, do, o, lse, n_b, s_len, tq, name, comm=None):
    t = q.shape[0]
    n_h, n_hk = q.shape[1] // HP, k.shape[1] // HP
    grp = n_h // n_hk
    nq = s_len // tq
    sub = min(tq, 256)
    grid = (n_b, n_hk, grp, nq)
    c_ins, c_in_specs, c_outs, c_sems, alias = _comm_parts(comm, 6, 3)

    def body(*refs):
        ((q_ref, k_ref, v_ref, do_ref, o_ref, lse_ref), cin, (dq_o, dk_o, dv_o), cout, (p_s, ds_s, dk_acc, dv_acc),
         csem) = _split_refs(refs, 6, 3, 4, comm)
        _comm_start(comm, cin, cout, csem, grid)

        @pl.when((pl.program_id(2) == 0) & (pl.program_id(3) == 0))
        def _():
            dk_acc[...] = jnp.zeros_like(dk_acc)
            dv_acc[...] = jnp.zeros_like(dv_acc)

        kv, vv = k_ref[...], v_ref[...]
        for r in range(tq // sub):
            rows = slice(r * sub, (r + 1) * sub)
            qv, dov = q_ref[rows, :], do_ref[rows, :]
            delta = jnp.sum(dov.astype(F32) * o_ref[rows, :].astype(F32), axis=-1, keepdims=True)
            delta_row = jnp.broadcast_to(delta, (sub, HP)).T[0:1, :]
            lse_row = lse_ref[rows, :].T[0:1, :]
            pt = jnp.exp(_dot_nt(kv, qv) - lse_row)
            dst = (pt * (_dot_nt(vv, dov) - delta_row)).astype(BF16)
            p_s[:, rows] = pt.astype(BF16)
            ds_s[:, rows] = dst
            dq_o[rows, :] = _dot_tn(dst, kv).astype(dq_o.dtype)
        dk_acc[...] += _dot_nn(ds_s[...], q_ref[...])
        dv_acc[...] += _dot_nn(p_s[...], do_ref[...])

        @pl.when((pl.program_id(2) == grp - 1) & (pl.program_id(3) == nq - 1))
        def _():
            dk_o[...] = dk_acc[...].astype(dk_o.dtype)
            dv_o[...] = dv_acc[...].astype(dv_o.dtype)

        _comm_finish(comm, cin, cout, csem, grid)

    qspec = pl.BlockSpec((tq, HP), lambda b, hk, g, i: (b * nq + i, hk * grp + g))
    kspec = pl.BlockSpec((s_len, HP), lambda b, hk, g, i: (b, hk))
    return pl.pallas_call(
        body, name=name, grid=grid,
        out_shape=(jax.ShapeDtypeStruct((t, n_h * HP), BF16), jax.ShapeDtypeStruct((t, n_hk * HP), BF16),
                   jax.ShapeDtypeStruct((t, n_hk * HP), BF16), *c_outs),
        in_specs=[qspec, kspec, kspec, qspec, qspec, qspec, *c_in_specs],
        out_specs=(qspec, kspec, kspec, *([_ANY] * len(c_outs))),
        scratch_shapes=[pltpu.VMEM((s_len, tq), BF16), pltpu.VMEM((s_len, tq), BF16),
                        pltpu.VMEM((s_len, HP), F32), pltpu.VMEM((s_len, HP), F32), *c_sems],
        input_output_aliases=alias,
        compiler_params=_params(("arbitrary", "arbitrary", "arbitrary", "arbitrary")),
    )(q, k, v, do, o, lse, *c_ins)


def prep_bwd(dqa, dka, dva, dqb, dkb, dvb, zp, dz, tabs, g_qa, g_kva, g_qn, g_kn, w_qb_t, w_kvb_t, tm, s_len):
    t = zp.shape[0]
    nsb = s_len // tm
    scale_a = (QK_NOPE + QK_ROPE) ** -0.5
    scale_b = HD_B ** -0.5

    def body(dqa_ref, dka_ref, dva_ref, dqb_ref, dkb_ref, dvb_ref, qb_ref, qlat_ref, kb_ref, ckv_ref, tab_ref,
             gqa_ref, gkva_ref, gqn_ref, gkn_ref, wqb_ref, wkvb_ref, _, dz_o, dqap_o, dkva_o, st_o):
        dzq_o, dsm_o = dz_o.at[:, 0:1024], dz_o.at[:, 1024:2048]

        @pl.when(pl.program_id(0) == 0)
        def _():
            st_o[...] = jnp.zeros_like(st_o)

        ca, s1a, s2a = tab_ref[0], tab_ref[1], tab_ref[2]
        ck = tab_ref[3]
        cb, s1b, s2b = tab_ref[4], tab_ref[5], tab_ref[6]
        for h in range(H_A):
            sl = slice(h * HP, (h + 1) * HP)
            dqap_o[:, sl] = _rope_bwd(dqa_ref[:, sl].astype(F32) * scale_a, ca, s1a, s2a).astype(BF16)
        dcq = _dot_nn(dqap_o[...], wqb_ref[...])
        ql = qlat_ref[...]
        rq = _rstd(ql, Q_LORA)
        xh = ql * rq
        gqa = gqa_ref[...]
        st_o[0:1, :] += _colsum(dcq * xh)
        dsm_o[:, 0:256] = _rms_bwd(dcq, xh, rq, gqa, Q_LORA).astype(BF16)
        dkpe = jnp.zeros((tm, HP), F32)
        for h in range(H_A):
            sl = slice(h * HP, (h + 1) * HP)
            dk = dka_ref[:, sl]
            dkpe = dkpe + dk.astype(F32)
            dkva_o[:, sl] = dk.astype(BF16)
        dkva_o[:, H_A * HP:] = dva_ref[...].astype(BF16)
        dsm_o[:, 896:1024] = _rope_bwd(dkpe, ck, s1a, s2a).astype(BF16)
        dckv = _dot_nn(dkva_o[...], wkvb_ref[...])
        cr = ckv_ref[...]
        rk = _rstd(cr, KV_LORA)
        xh = cr * rk
        st_o[1:2, 0:128] += _colsum(dckv * xh)
        dsm_o[:, 768:896] = _rms_bwd(dckv, xh, rk, gkva_ref[...], KV_LORA).astype(BF16)
        gqn, gkn = gqn_ref[...], gkn_ref[...]
        dgq = jnp.zeros((1, HP), F32)
        for h in range(H_B):
            sl = slice(h * HP, (h + 1) * HP)
            dy = _rope_bwd(dqb_ref[:, sl].astype(F32) * scale_b, cb, s1b, s2b)
            xs = qb_ref[:, sl]
            r = _rstd(xs, HD_B)
            xh = xs * r
            dgq = dgq + _colsum(dy * xh)
            dzq_o[:, sl] = _rms_bwd(dy, xh, r, gqn, HD_B).astype(BF16)
        st_o[2:3, 0:128] += dgq
        dgk = jnp.zeros((1, HP), F32)
        for h in range(KV_B):
            sl = slice(h * HP, (h + 1) * HP)
            dy = _rope_bwd(dkb_ref[:, sl].astype(F32), cb, s1b, s2b)
            xs = kb_ref[:, sl]
            r = _rstd(xs, HD_B)
            xh = xs * r
            dgk = dgk + _colsum(dy * xh)
            dsm_o[:, 256 + h * HP:256 + (h + 1) * HP] = _rms_bwd(dy, xh, r, gkn, HD_B).astype(BF16)
        st_o[3:4, 0:128] += dgk
        dsm_o[:, 512:768] = dvb_ref[...].astype(BF16)

    return pl.pallas_call(
        body, name="prep_bwd", grid=(t // tm,),
        out_shape=(jax.ShapeDtypeStruct((t, ZP), BF16), jax.ShapeDtypeStruct((t, 1024), BF16),
                   jax.ShapeDtypeStruct((t, 2048), BF16), jax.ShapeDtypeStruct((4, 256), F32)),
        in_specs=[_rows(tm, 1024), _rows(tm, 1024), _rows(tm, 1024), _rows(tm, 1024), _rows(tm, 256), _rows(tm, 256),
                  _rows(tm, 1024, 0), _rows(tm, 256, 4), _rows(tm, 256, 5), _rows(tm, 128, 14),
                  pl.BlockSpec((7, tm, HP), lambda i: (0, i % nsb, 0)),
                  _resident((1, Q_LORA)), _resident((1, KV_LORA)), _resident((1, HP)), _resident((1, HP)),
                  _resident((H_A * HP, Q_LORA)), _resident((2 * H_A * HP, KV_LORA)), _ANY],
        out_specs=(_rows(tm, 2048, 1), _rows(tm, 1024), _rows(tm, 2048), pl.BlockSpec((4, 256), lambda i: (0, 0))),
        input_output_aliases={17: 0}, compiler_params=_params(("arbitrary",)),
    )(dqa, dka, dva, dqb, dkb, dvb, zp, zp, zp, zp, tabs, g_qa, g_kva, g_qn, g_kn, w_qb_t, w_kvb_t, dz)


def in_bwd(dz, x, dx1, g_mix, w_in_t, tm, comm=None):
    t = x.shape[0]
    grid = (t // tm,)
    c_ins, c_in_specs, c_outs, c_sems, alias = _comm_parts(comm, 5, 2)

    def body(*refs):
        (dz_ref, x_ref, dx1_ref, g_ref, w_ref), cin, (dx_o, st_o), cout, _, csem = _split_refs(refs, 5, 2, 0, comm)
        _comm_start(comm, cin, cout, csem, grid)

        @pl.when(pl.program_id(0) == 0)
        def _():
            st_o[...] = jnp.zeros_like(st_o)

        dh = _dot_nn(dz_ref[...], w_ref[...])
        xv = x_ref[...]
        g = g_ref[...]
        r = _rstd(xv, D_MODEL)
        xh = xv * r
        st_o[...] += _colsum(dh * xh)
        dx_o[...] = dx1_ref[...] + _rms_bwd(dh, xh, r, g, D_MODEL)
        _comm_finish(comm, cin, cout, csem, grid)

    return pl.pallas_call(
        body, name="in_bwd", grid=grid,
        out_shape=(jax.ShapeDtypeStruct((t, D_MODEL), F32), jax.ShapeDtypeStruct((1, D_MODEL), F32), *c_outs),
        in_specs=[_rows(tm, ZP), _rows(tm, D_MODEL), _rows(tm, D_MODEL),
                  _resident((1, D_MODEL)), _resident((ZP, D_MODEL)), *c_in_specs],
        out_specs=(_rows(tm, D_MODEL), pl.BlockSpec((1, D_MODEL), lambda i: (0, 0)), *([_ANY] * len(c_outs))),
        scratch_shapes=c_sems, input_output_aliases=alias,
        compiler_params=_params(("arbitrary",)),
    )(dz, x, dx1, g_mix, w_in_t, *c_ins)


def matmul_tn(a, b, name, square_a=False):
    t, m = a.shape
    n = b.shape[1]
    bm = min(m, 512)
    tk = min(t, 4096 if m > 1024 else 1024)

    def body(a_ref, b_ref, o_ref):
        @pl.when(pl.program_id(1) == 0)
        def _():
            o_ref[...] = jnp.zeros_like(o_ref)

        av = a_ref[...]
        if square_a:
            av = (av.astype(F32) * av.astype(F32))
        o_ref[...] += _dot_tn(av.astype(BF16), b_ref[...].astype(BF16))

    return pl.pallas_call(
        body, name=name, grid=(m // bm, t // tk), out_shape=jax.ShapeDtypeStruct((m, n), F32),
        in_specs=[pl.BlockSpec((tk, bm), lambda i, kk: (kk, i)), pl.BlockSpec((tk, n), lambda i, kk: (kk, 0))],
        out_specs=pl.BlockSpec((bm, n), lambda i, kk: (i, 0)),
        compiler_params=_params(("parallel", "arbitrary")),
    )(a, b)


def matmul_tn_packed(a, b, name, rows, row_off, total_rows, buf=None, square_a=False):
    t, m = a.shape
    n = b.shape[1]
    pd = max(1, 512 // rows)
    bm = pd * rows
    tk = min(t, 4096 if m > 1024 else 1024)
    nk = t // tk

    def body(a_ref, b_ref, *rest):
        o_ref, acc = rest[-2], rest[-1]

        @pl.when(pl.program_id(1) == 0)
        def _():
            acc[...] = jnp.zeros_like(acc)

        av = a_ref[...]
        if square_a:
            av = (av.astype(F32) * av.astype(F32))
        acc[...] += _dot_tn(av.astype(BF16), b_ref[...].astype(BF16))

        @pl.when(pl.program_id(1) == nk - 1)
        def _():
            o_ref[...] = acc[...].reshape(pd, rows, n).astype(o_ref.dtype)

    in_specs = [pl.BlockSpec((tk, bm), lambda i, kk: (kk, i)), pl.BlockSpec((tk, n), lambda i, kk: (kk, 0))]
    args = [a, b]
    if buf is not None:
        in_specs.append(_ANY)
        args.append(buf)
    return pl.pallas_call(
        body, name=name, grid=(m // bm, nk), out_shape=jax.ShapeDtypeStruct((N_DEV, total_rows, n), BF16),
        in_specs=in_specs, out_specs=pl.BlockSpec((pd, rows, n), lambda i, kk: (i, row_off // rows, 0)),
        scratch_shapes=[pltpu.VMEM((bm, n), F32)], input_output_aliases={2: 0} if buf is not None else {},
        compiler_params=_params(("parallel", "arbitrary")),
    )(*args)


def adamw(w, g, m, v, name, g_transposed=False):
    _, r, c = w.shape
    tr = 256 if (not g_transposed and r > 256 and r % 256 == 0) else r
    c1 = 1.0 - ADAM_B1 ** ADAM_STEP
    c2 = 1.0 - ADAM_B2 ** ADAM_STEP

    def body(w_ref, g_ref, m_ref, v_ref, g_o, d_o, m_o, v_o):
        gv = g_ref[...].T if g_transposed else g_ref[...]
        mn = ADAM_B1 * m_ref[0] + (1.0 - ADAM_B1) * gv
        vn = ADAM_B2 * v_ref[0] + (1.0 - ADAM_B2) * (gv * gv)
        g_o[0] = gv
        m_o[0] = mn
        v_o[0] = vn
        d_o[0] = -ADAM_LR * ((mn / c1) / (jnp.sqrt(vn / c2) + ADAM_EPS) + ADAM_WD * w_ref[0])

    spec = pl.BlockSpec((1, tr, c), lambda i: (0, i, 0))
    gspec = pl.BlockSpec((c, r), lambda i: (0, 0)) if g_transposed else pl.BlockSpec((tr, c), lambda i: (i, 0))
    shp = jax.ShapeDtypeStruct((1, r, c), F32)
    return pl.pallas_call(
        body, name=name, grid=(r // tr,), out_shape=(shp,) * 4, in_specs=[spec, gspec, spec, spec], out_specs=(spec,) * 4,
        compiler_params=_params(("parallel",)),
    )(w, g, m, v)


def _rope_tables(s_len):
    def angles(pos, dim):
        inv = np.float32(ROPE_THETA) ** (-np.arange(0, dim, 2, dtype=np.float32) / np.float32(dim))
        return pos.astype(np.float32)[:, None] * inv[None, :]

    tpos = np.arange(s_len)
    a1 = angles(tpos, QK_ROPE)
    ar = angles(tpos // GRID_W, HD_B // 2)
    ac = angles(tpos % GRID_W, HD_B // 2)
    z16 = np.zeros((s_len, 16), np.float32)
    z32 = np.zeros((s_len, 32), np.float32)
    z64 = np.zeros((s_len, 64), np.float32)
    one64 = np.ones((s_len, 64), np.float32)
    c1, s1 = np.cos(a1), np.sin(a1)
    ca = np.concatenate([one64, c1, c1, z32], axis=1)
    ck = np.concatenate([z64, c1, c1, z32], axis=1)
    s1a = np.concatenate([z64, -s1, z16, z32], axis=1)
    s2a = np.concatenate([z64, z16, s1, z32], axis=1)
    cr, sr, cc, sc = np.cos(ar), np.sin(ar), np.cos(ac), np.sin(ac)
    cb = np.concatenate([cr, cr, cc, cc, z64], axis=1)
    s1b = np.concatenate([-sr, z16, -sc, z16, z64], axis=1)
    s2b = np.concatenate([z16, sr, z16, sc, z64], axis=1)
    return jnp.asarray(np.stack([ca, s1a, s2a, ck, cb, s1b, s2b]).astype(np.float32))


def _pad_heads(a, n_heads, axis):
    shp = a.shape
    a = a.reshape(shp[:axis] + (n_heads, shp[axis] // n_heads) + shp[axis + 1:])
    pad = [(0, 0)] * a.ndim
    pad[axis + 1] = (0, HP - a.shape[axis + 1])
    a = jnp.pad(a, pad)
    return a.reshape(shp[:axis] + (n_heads * HP,) + shp[axis + 1:])


def _unpad_heads(a, n_heads, width, axis):
    shp = a.shape
    a = a.reshape(shp[:axis] + (n_heads, HP) + shp[axis + 1:])
    a = lax.slice_in_dim(a, 0, width, axis=axis + 1)
    return a.reshape(shp[:axis] + (n_heads * width,) + shp[axis + 1:])


def _pack_rows(blocks, names):
    parts = []
    for name in names:
        b = blocks[name]
        padr = PACK_ROWS[name] - b.shape[-2]
        if padr:
            b = jnp.pad(b, [(0, 0)] * (b.ndim - 2) + [(0, padr), (0, 0)])
        parts.append(b)
    return jnp.concatenate(parts, axis=parts[0].ndim - 2)


def _expand_w_in(wt):
    z64 = jnp.zeros((64, D_MODEL), wt.dtype)
    z32 = jnp.zeros((32, D_MODEL), wt.dtype)
    return jnp.concatenate([
        wt[1184:2208], wt[2208:3232], _pad_heads(wt[416:928], H_B, 0), wt[0:256],
        _pad_heads(wt[928:1056], KV_B, 0), _pad_heads(wt[1056:1184], KV_B, 0), wt[256:384],
        z64, wt[384:416], z32], axis=0)


def _collapse_w_in(dw):
    dg, dq, ds = dw[0:2048], dw[2048:3072], dw[3072:4096]
    return jnp.concatenate([
        ds[0:256], ds[768:896], ds[960:992], _unpad_heads(dq, H_B, HD_B, 0), _unpad_heads(ds[256:512], KV_B, HD_B, 0),
        _unpad_heads(ds[512:768], KV_B, HD_B, 0), dg], axis=0)


def kernel(x, p, g_mix, w_in, g_qa, w_qb, g_kva, w_kvb, g_qn, g_kn, w_oa, w_ob, w_o, g_mlp, w_up, w_down, g_ple, w_ple_gate, w_ple, g_final, loss_target, m_g_mix, m_w_in, m_g_qa, m_w_qb, m_g_kva, m_w_kvb, m_g_qn, m_g_kn, m_w_oa, m_w_ob, m_w_o, m_g_mlp, m_w_up, m_w_down, m_g_ple, m_w_ple_gate, m_w_ple, m_g_final, v_g_mix, v_w_in, v_g_qa, v_w_qb, v_g_kva, v_w_kvb, v_g_qn, v_g_kn, v_w_oa, v_w_ob, v_w_o, v_g_mlp, v_w_up, v_w_down, v_g_ple, v_w_ple_gate, v_w_ple, v_g_final):
    n_b, s_len, _ = x.shape
    t = n_b * s_len
    tm = min(512, s_len)
    tq_f = min(2048, s_len)
    tq_b = min(2048, s_len)

    mats = dict(w_in=(w_in, m_w_in, v_w_in), w_qb=(w_qb, m_w_qb, v_w_qb), w_kvb=(w_kvb, m_w_kvb, v_w_kvb),
                w_oa=(w_oa, m_w_oa, v_w_oa), w_ob=(w_ob, m_w_ob, v_w_ob), w_o=(w_o, m_w_o, v_w_o),
                w_up=(w_up, m_w_up, v_w_up), w_down=(w_down, m_w_down, v_w_down),
                w_ple_gate=(w_ple_gate, m_w_ple_gate, v_w_ple_gate), w_ple=(w_ple, m_w_ple, v_w_ple))
    col_sharded = ("w_in", "w_qb", "w_kvb", "w_oa", "w_ob", "w_up", "w_ple")

    blocks = {}
    for name in PACK_W1 + ("w_oa", "w_ob", "w_ple"):
        blocks[name] = mats[name][0][0].T.reshape(-1, D_MODEL).astype(BF16)
    off_w1, _ = _pack_offsets(PACK_W1)
    off_w2, _ = _pack_offsets(PACK_W2)
    off_w3, _ = _pack_offsets(PACK_W3)
    xf = x.reshape(t, D_MODEL)
    h, full1 = norm_x(xf, g_mix, tm, comm=gather3_first_comm(_pack_rows(blocks, PACK_W1)))
    pack2, pack3, full1 = pack_late_weights(w_up, w_down, w_o, w_ple_gate, _pack_rows(blocks, ("w_oa", "w_ob", "w_ple")),
                                            comm=gather3_second_comm(full1, then_third=True))

    def gathered(full, offs, name, rows, width):
        return full[:, offs[name]:offs[name] + rows].reshape(-1, width)

    w_in_t = _expand_w_in(gathered(full1, off_w1, "w_in", 404, D_MODEL))
    w_qb_t = _pad_heads(gathered(full1, off_w1, "w_qb", 24, Q_LORA), H_A, 0)
    wkvb = gathered(full1, off_w1, "w_kvb", 16, KV_LORA).reshape(H_A, 2, 64, KV_LORA)
    w_kvb_t = jnp.concatenate([_pad_heads(wkvb[:, 0].reshape(-1, KV_LORA), H_A, 0),
                               _pad_heads(wkvb[:, 1].reshape(-1, KV_LORA), H_A, 0)], axis=0)

    tabs = _rope_tables(s_len)
    g_qn_p = jnp.pad(g_qn, ((0, 0), (0, HP - HD_B)))
    g_kn_p = jnp.pad(g_kn, ((0, 0), (0, HP - HD_B)))
    pf = p.reshape(t, PLE_DIM)
    tgt = loss_target.reshape(t, D_MODEL)

    zg, zs, full2 = in_proj(h, w_in_t, tm, comm=gather3_first_comm(pack2))
    qa, ka, va, qb, kb, vb, cq, ckv, full2 = attn_prep(zs, tabs, g_qa, g_kva, g_qn_p, g_kn_p, w_qb_t, w_kvb_t, tm, s_len,
                                                       comm=gather3_second_comm(full2))
    oa, lse_a, full3, full2 = attn_fwd(qa, ka, va, n_b, s_len, tq_f, "attn_a_fwd",
                                       comm=_join_comms(gather_first_comm(pack3), gather3_third_comm(full2)))
    ob, lse_b, full3 = attn_fwd(qb, kb, vb, n_b, s_len, tq_f, "attn_b_fwd", comm=gather_pass_comm(full3))
    w_oa_t = _pad_heads(gathered(full2, off_w2, "w_oa", 64, H_A * V_DIM_A), H_A, 1)
    w_ob_t = _pad_heads(gathered(full2, off_w2, "w_ob", 64, H_B * HD_B), H_B, 1)
    w_ple_t = gathered(full2, off_w2, "w_ple", 32, PLE_DIM)
    x1, merged, yab = merge_fwd(oa, ob, zg, xf, w_oa_t, w_ob_t, full2, off_w2, tm)
    x2, u = mlp_fwd(x1, g_mlp, full3, off_w3, tm)
    dx2, dt, h3, dpe, st_ple, dx2b = ple_loss_bwd(x2, pf, tgt, g_ple, g_final.reshape(1, D_MODEL), full2, off_w2, w_ple_t, tm)
    dx1, da, h2, st_mlp, dx1b = mlp_bwd(dx2, x1, u, g_mlp, full3, off_w3, tm)

    core = lax.axis_index("c").astype(jnp.int32).reshape(1)
    chip = (2 * lax.axis_index("x") + lax.axis_index("y")).astype(jnp.int32).reshape(1)

    def packed(gblocks, names):
        return _pack_rows({n: gblocks[n].reshape(N_DEV, -1, D_MODEL).astype(BF16) for n in names}, names)

    off_g1, rows_g1 = _pack_offsets(PACK_G1)
    gpack1 = matmul_tn_packed(da, h2, "gw_up", 512, off_g1["w_up"], rows_g1)
    gpack1 = matmul_tn_packed(u, dx2b, "gw_down", 512, off_g1["w_down"], rows_g1, buf=gpack1, square_a=True)
    gpack1 = matmul_tn_packed(h3, dt, "gw_pg", 128, off_g1["w_ple_gate"], rows_g1, buf=gpack1)
    gple = matmul_tn(dpe, pf, "gw_ple").reshape(N_DEV, -1, D_MODEL).astype(BF16)
    gpack1 = lax.dynamic_update_slice(gpack1, gple, (0, off_g1["w_ple"], 0))
    doa, dob, dz, dya, dyb, got1 = merge_bwd(dx1b, yab, zg, w_oa_t, w_ob_t, full2, off_w2, tm,
                                               comm=scatter_sibling_comm(gpack1))
    part1 = add_pairs(gpack1, got1, core)

    off_g2, rows_g2 = _pack_offsets(PACK_G2)
    g2 = dict(w_oa=_unpad_heads(matmul_tn(dya, oa, "gw_oa"), H_A, V_DIM_A, 1),
              w_ob=_unpad_heads(matmul_tn(dyb, ob, "gw_ob"), H_B, HD_B, 1))
    gpack2 = matmul_tn_packed(merged, dx1b, "gw_o", 128, off_g2["w_o"], rows_g2)
    gpack2 = lax.dynamic_update_slice(gpack2, packed(g2, ("w_oa", "w_ob")), (0, off_g2["w_oa"], 0))
    dqa, dka, dva, land1, got2 = attn_bwd(qa, ka, va, doa, oa, lse_a, n_b, s_len, tq_b, "attn_a_bwd",
                                          comm=_join_comms(scatter_chips_comm(part1), scatter_sibling_comm(gpack2)))
    gshard1 = sum_chips(part1, land1, chip)
    part2 = add_pairs(gpack2, got2, core)
    dqb, dkb, dvb, land2 = attn_bwd(qb, kb, vb, dob, ob, lse_b, n_b, s_len, tq_b, "attn_b_bwd", comm=scatter_chips_comm(part2))
    gshard2 = sum_chips(part2, land2, chip)
    dz, dqap, dkva, st_prep = prep_bwd(dqa, dka, dva, dqb, dkb, dvb, zs, dz, tabs, g_qa, g_kva, g_qn_p, g_kn_p,
                                       w_qb_t, w_kvb_t, tm, s_len)

    gkv = matmul_tn(dkva, ckv, "gw_kvb")
    g3 = dict(
        w_in=_collapse_w_in(matmul_tn(dz, h, "gw_in")),
        w_qb=_unpad_heads(matmul_tn(dqap, cq, "gw_qb"), H_A, QK_NOPE + QK_ROPE, 0),
        w_kvb=jnp.stack([_unpad_heads(gkv[:H_A * HP], H_A, 64, 0).reshape(H_A, 64, KV_LORA),
                         _unpad_heads(gkv[H_A * HP:], H_A, 64, 0).reshape(H_A, 64, KV_LORA)], axis=1))
    gpack3 = packed(g3, PACK_G3)
    part3 = add_pairs(gpack3, exchange_sibling(gpack3), core)
    grad_x, st_mix, land3 = in_bwd(dz, xf, dx1, g_mix, w_in_t, tm, comm=scatter_chips_comm(part3))
    gshard3 = sum_chips(part3, land3, chip)
    off_g3, _ = _pack_offsets(PACK_G3)
    shards = {n: (gshard1, off_g1[n]) for n in PACK_G1}
    shards.update({n: (gshard2, off_g2[n]) for n in PACK_G2})
    shards.update({n: (gshard3, off_g3[n]) for n in PACK_G3})

    stats = allreduce_stats(st_mix, st_prep, st_mlp, st_ple)
    loss = jnp.sum(stats[ST_LOSS])

    out_g, out_d, out_m, out_v = {}, {}, {}, {}
    for name, (w, m, v) in mats.items():
        gshard, off = shards[name]
        r, c = w.shape[1:]
        if name in col_sharded:
            g2 = gshard[off:off + (r * c) // D_MODEL].reshape(c, r)
            if r % 128 == 0 and c % 128 == 0:
                res = adamw(w, g2, m, v, "adamw_" + name, g_transposed=True)
            else:
                res = adamw(w[0].T[None], g2, m[0].T[None], v[0].T[None], "adamw_" + name)
                res = tuple(a[0].T[None] for a in res)
        else:
            res = adamw(w, gshard[off:off + r], m, v, "adamw_" + name)
        out_g[name], out_d[name], out_m[name], out_v[name] = res

    gains = (("g_mix", g_mix, m_g_mix, v_g_mix, ST_G_MIX), ("g_qa", g_qa, m_g_qa, v_g_qa, ST_G_QA),
             ("g_kva", g_kva, m_g_kva, v_g_kva, ST_G_KVA), ("g_qn", g_qn, m_g_qn, v_g_qn, ST_G_QN),
             ("g_kn", g_kn, m_g_kn, v_g_kn, ST_G_KN), ("g_mlp", g_mlp, m_g_mlp, v_g_mlp, ST_G_MLP),
             ("g_ple", g_ple, m_g_ple, v_g_ple, ST_G_PLE), ("g_final", g_final, m_g_final, v_g_final, ST_G_FINAL))
    res = adamw_gains(stats, [(r_, w.reshape(1, -1), m.reshape(1, -1), v.reshape(1, -1)) for _, w, m, v, r_ in gains])
    for (name, w, _, _, _), (gg, gd, gm, gv) in zip(gains, res):
        out_g[name], out_d[name], out_m[name], out_v[name] = (a.reshape(w.shape) for a in (gg, gd, gm, gv))

    order = ("g_mix", "w_in", "g_qa", "w_qb", "g_kva", "w_kvb", "g_qn", "g_kn", "w_oa", "w_ob", "w_o", "g_mlp",
             "w_up", "w_down", "g_ple", "w_ple_gate", "w_ple", "g_final")
    return (loss, grad_x.reshape(x.shape), *[out_g[n] for n in order], *[out_d[n] for n in order],
            *[out_m[n] for n in order], *[out_v[n] for n in order])
```

```python
import numpy as np
import jax
import jax.numpy as jnp
from jax import lax
from jax.experimental import pallas as pl
from jax.experimental.pallas import tpu as pltpu

F32 = jnp.float32
BF16 = jnp.bfloat16

D_MODEL = 1024
EPS = 1e-6
ROPE_THETA = 10000.0
GRID_W = 64
H_A = 8
QK_NOPE = 64
QK_ROPE = 32
V_DIM_A = 64
Q_LORA = 256
KV_LORA = 128
H_B = 8
KV_B = 2
HD_B = 64
D_FF = 4 * D_MODEL
PLE_DIM = 256
HP = 128
ZP = 4096
N_DEV = 8
N_CHIP = 4

ADAM_LR = 0.001
ADAM_B1 = 0.9
ADAM_B2 = 0.999
ADAM_EPS = 1e-08
ADAM_WD = 0.01
ADAM_STEP = 10

VMEM_LIMIT = 52 * 1024 * 1024

PACK_ROWS = dict(w_in=416, w_qb=32, w_kvb=16, w_oa=64, w_ob=64, w_o=128, w_up=512, w_down=512, w_ple_gate=128, w_ple=32)
PACK_W1 = ("w_in", "w_qb", "w_kvb")
PACK_W2 = ("w_o", "w_ple_gate", "w_oa", "w_ob", "w_ple")
PACK_W3 = ("w_up", "w_down")
PACK_G1 = ("w_up", "w_down", "w_ple_gate", "w_ple")
PACK_G2 = ("w_o", "w_oa", "w_ob")
PACK_G3 = ("w_in", "w_qb", "w_kvb")


def _pack_offsets(names):
    off, o = {}, 0
    for n in names:
        off[n] = o
        o += PACK_ROWS[n]
    return off, o

ST_G_MIX, ST_G_QA, ST_G_KVA, ST_G_QN, ST_G_KN, ST_G_MLP, ST_G_PLE, ST_G_FINAL, ST_LOSS = range(9)
ST_ROWS = 16


def _dot_nn(a, b):
    return lax.dot_general(a, b, (((1,), (0,)), ((), ())), preferred_element_type=F32)


def _dot_nt(a, b):
    return lax.dot_general(a, b, (((1,), (1,)), ((), ())), preferred_element_type=F32)


def _dot_tn(a, b):
    return lax.dot_general(a, b, (((0,), (0,)), ((), ())), preferred_element_type=F32)


def _rstd(x, n):
    return lax.rsqrt(jnp.sum(x * x, axis=-1, keepdims=True) * (1.0 / n) + EPS)


def _rms_bwd(dy, xh, r, g, n):
    dxh = dy * g
    return r * (dxh - xh * (jnp.sum(dxh * xh, axis=-1, keepdims=True) * (1.0 / n)))


def _rope_fwd(x, c, s1, s2):
    return x * c + pltpu.roll(x, HP - 16, 1) * s1 + pltpu.roll(x, 16, 1) * s2


def _rope_bwd(d, c, s1, s2):
    return d * c + pltpu.roll(d * s1, 16, 1) + pltpu.roll(d * s2, HP - 16, 1)


def _colsum(v):
    return jnp.sum(v, axis=0, keepdims=True)


def _params(sem=None, vmem=VMEM_LIMIT):
    return pltpu.CompilerParams(dimension_semantics=sem, vmem_limit_bytes=vmem)


def _resident(shape):
    nd = len(shape)
    return pl.BlockSpec(shape, lambda *_: (0,) * nd, pipeline_mode=pl.Buffered(1))


def _rows(tm, width, col=0):
    return pl.BlockSpec((tm, width), lambda i: (i, col))


def _packed_weight(rows, off):
    return pl.BlockSpec((N_DEV, rows, D_MODEL), lambda *_: (0, off // rows, 0), pipeline_mode=pl.Buffered(1))


def _wrows(ref, start, size):
    rows = ref.shape[1]
    return ref[start // rows:(start + size) // rows].reshape(size, D_MODEL)


def _mesh_pos():
    return lax.axis_index("x"), lax.axis_index("y"), lax.axis_index("c")


def _flip(v, bit):
    return (1 - v) if bit else v


_ANY = pl.BlockSpec(memory_space=pl.ANY)
_MESH = pl.DeviceIdType.MESH


def _remote(src, dst, send_sems, recv_sems, k, to):
    return pltpu.make_async_remote_copy(src_ref=src, dst_ref=dst, send_sem=send_sems.at[k], recv_sem=recv_sems.at[k],
                                        device_id=to, device_id_type=_MESH)


def _sibling_copies(g_ref, got_ref, send_sems, recv_sems):
    x, y, c = _mesh_pos()
    return [_remote(g_ref.at[2 * j + (1 - c)], got_ref.at[j], send_sems, recv_sems, j, (x, y, 1 - c)) for j in range(N_CHIP)]


def _chip_copies(p_ref, land_ref, send_sems, recv_sems):
    x, y, c = _mesh_pos()
    copies = []
    for k in (1, 2, 3):
        tx, ty = _flip(x, k & 2), _flip(y, k & 1)
        copies.append(_remote(p_ref.at[2 * tx + ty], land_ref.at[k - 1], send_sems, recv_sems, k - 1, (tx, ty, c)))
    return copies


class _Comm:
    def __init__(self, ins, out_shapes, sems, make, aliases=None, make_tail=None):
        self.ins, self.out_shapes, self.sems, self.make, self.aliases = list(ins), list(out_shapes), list(sems), make, aliases or {}
        self.make_tail = make_tail


def _join_comms(a, b):
    assert a.make_tail is None and b.make_tail is None
    n_i, n_o, n_s = len(a.ins), len(a.out_shapes), len(a.sems)

    def make(cin, cout, sems):
        return a.make(cin[:n_i], cout[:n_o], sems[:n_s]) + b.make(cin[n_i:], cout[n_o:], sems[n_s:])

    aliases = dict(a.aliases)
    aliases.update({n_i + j: n_o + k for j, k in b.aliases.items()})
    return _Comm(a.ins + b.ins, a.out_shapes + b.out_shapes, a.sems + b.sems, make, aliases)


def _comm_parts(comm, n_in, n_out):
    if comm is None:
        return [], [], [], [], {}
    alias = {n_in + j: n_out + k for j, k in comm.aliases.items()}
    return comm.ins, [_ANY] * len(comm.ins), comm.out_shapes, comm.sems, alias


def _split_refs(refs, n_in, n_out, n_scratch, comm):
    n_ci = len(comm.ins) if comm else 0
    n_co = len(comm.out_shapes) if comm else 0
    cuts, i = [], 0
    for n in (n_in, n_ci, n_out, n_co, n_scratch):
        cuts.append(refs[i:i + n])
        i += n
    return (*cuts, refs[i:])


def _grid_edge(grid, last):
    cond = None
    for d, n in enumerate(grid):
        here = pl.program_id(d) == (n - 1 if last else 0)
        cond = here if cond is None else cond & here
    return cond


def _comm_start(comm, cin, cout, csem, grid):
    if comm is not None:
        @pl.when(_grid_edge(grid, False))
        def _():
            for cp in comm.make(cin, cout, csem):
                cp.start()


def _comm_finish(comm, cin, cout, csem, grid):
    if comm is not None:
        @pl.when(_grid_edge(grid, True))
        def _():
            for cp in comm.make(cin, cout, csem):
                cp.wait()
            if comm.make_tail is not None:
                tail = comm.make_tail(cin, cout, csem)
                for cp in tail:
                    cp.start()
                for cp in tail:
                    cp.wait()


def gather_first_comm(shard):
    r, w = shard.shape

    def make(cin, cout, sems):
        (x_ref,), (out_ref,), (send_sems, recv_sems, local_sem) = cin, cout, sems
        x, y, c = _mesh_pos()
        mine = out_ref.at[4 * x + 2 * y + c]
        targets = [(x, y, 1 - c), (1 - x, y, c), (x, 1 - y, c), (1 - x, 1 - y, c)]
        return [_remote(x_ref, mine, send_sems, recv_sems, k, to) for k, to in enumerate(targets)] + [
            pltpu.make_async_copy(x_ref, mine, local_sem)]

    return _Comm([shard], [jax.ShapeDtypeStruct((N_DEV, r, w), shard.dtype)],
                 [pltpu.SemaphoreType.DMA((4,)), pltpu.SemaphoreType.DMA((4,)), pltpu.SemaphoreType.DMA], make)


def gather_pass_comm(full):
    def make(cin, cout, sems):
        (in_ref,), (out_ref,), (send_sems, recv_sems) = cin, cout, sems
        x, y, c = _mesh_pos()
        copies = []
        for k, (px, py) in enumerate([(1 - x, y), (x, 1 - y), (1 - x, 1 - y)]):
            idx = 4 * px + 2 * py + c
            copies.append(_remote(in_ref.at[idx], out_ref.at[idx], send_sems, recv_sems, k, (x, y, 1 - c)))
        return copies

    return _Comm([full], [jax.ShapeDtypeStruct(full.shape, full.dtype)],
                 [pltpu.SemaphoreType.DMA((3,)), pltpu.SemaphoreType.DMA((3,))], make, aliases={0: 0})


def gather3_first_comm(shard):
    r, w = shard.shape

    def make(cin, cout, sems):
        (x_ref,), (out_ref,), (send_sems, recv_sems, local_sem) = cin, cout, sems
        x, y, c = _mesh_pos()
        mine = out_ref.at[4 * x + 2 * y + c]
        targets = [(x, y, 1 - c), (1 - x, y, c), (x, 1 - y, c)]
        return [_remote(x_ref, mine, send_sems, recv_sems, k, to) for k, to in enumerate(targets)] + [
            pltpu.make_async_copy(x_ref, mine, local_sem)]

    return _Comm([shard], [jax.ShapeDtypeStruct((N_DEV, r, w), shard.dtype)],
                 [pltpu.SemaphoreType.DMA((3,)), pltpu.SemaphoreType.DMA((3,)), pltpu.SemaphoreType.DMA], make)


def gather3_second_comm(full, then_third=False):
    def make(cin, cout, sems):
        (in_ref,), (out_ref,), (send_sems, recv_sems) = cin, cout, sems
        x, y, c = _mesh_pos()
        copies = []
        for k, (px, py) in enumerate([(1 - x, y), (x, 1 - y)]):
            idx = 4 * px + 2 * py + c
            copies.append(_remote(in_ref.at[idx], out_ref.at[idx], send_sems, recv_sems, k, (x, y, 1 - c)))
        fx, fy = x * c + (1 - x) * (1 - c), y * (1 - c) + (1 - y) * c
        tx, ty = x * (1 - c) + (1 - x) * c, (1 - y) * (1 - c) + y * c
        idx = 4 * fx + 2 * fy + c
        copies.append(_remote(in_ref.at[idx], out_ref.at[idx], send_sems, recv_sems, 2, (tx, ty, c)))
        return copies

    def third(cin, cout, sems):
        (out_ref,), (send_sems, recv_sems) = cout, sems
        x, y, c = _mesh_pos()
        idx = 4 * (1 - x) + 2 * (1 - y) + c
        return [_remote(out_ref.at[idx], out_ref.at[idx], send_sems, recv_sems, 3, (x, y, 1 - c))]

    n_sem = 4 if then_third else 3
    return _Comm([full], [jax.ShapeDtypeStruct(full.shape, full.dtype)],
                 [pltpu.SemaphoreType.DMA((n_sem,)), pltpu.SemaphoreType.DMA((n_sem,))], make, aliases={0: 0},
                 make_tail=third if then_third else None)


def gather3_third_comm(full):
    def make(cin, cout, sems):
        (in_ref,), (out_ref,), (send_sems, recv_sems) = cin, cout, sems
        x, y, c = _mesh_pos()
        idx = 4 * (1 - x) + 2 * (1 - y) + c
        return [_remote(in_ref.at[idx], out_ref.at[idx], send_sems, recv_sems, 0, (x, y, 1 - c))]

    return _Comm([full], [jax.ShapeDtypeStruct(full.shape, full.dtype)],
                 [pltpu.SemaphoreType.DMA((1,)), pltpu.SemaphoreType.DMA((1,))], make, aliases={0: 0})


def scatter_sibling_comm(g):
    _, r, w = g.shape
    return _Comm([g], [jax.ShapeDtypeStruct((N_CHIP, r, w), g.dtype)],
                 [pltpu.SemaphoreType.DMA((N_CHIP,)), pltpu.SemaphoreType.DMA((N_CHIP,))],
                 lambda cin, cout, sems: _sibling_copies(cin[0], cout[0], sems[0], sems[1]))


def scatter_chips_comm(part):
    _, r, w = part.shape
    return _Comm([part], [jax.ShapeDtypeStruct((N_CHIP - 1, r, w), part.dtype)],
                 [pltpu.SemaphoreType.DMA((3,)), pltpu.SemaphoreType.DMA((3,))],
                 lambda cin, cout, sems: _chip_copies(cin[0], cout[0], sems[0], sems[1]))


def exchange_sibling(g):
    _, r, w = g.shape

    def body(g_ref, got_ref, send_sems, recv_sems):
        copies = _sibling_copies(g_ref, got_ref, send_sems, recv_sems)
        for cp in copies:
            cp.start()
        for cp in copies:
            cp.wait()

    return pl.pallas_call(
        body, name="exchange_sibling", out_shape=jax.ShapeDtypeStruct((N_CHIP, r, w), g.dtype),
        in_specs=[_ANY], out_specs=_ANY,
        scratch_shapes=[pltpu.SemaphoreType.DMA((N_CHIP,)), pltpu.SemaphoreType.DMA((N_CHIP,))],
    )(g)


def exchange_chips(part):
    _, r, w = part.shape

    def body(p_ref, land_ref, send_sems, recv_sems):
        copies = _chip_copies(p_ref, land_ref, send_sems, recv_sems)
        for cp in copies:
            cp.start()
        for cp in copies:
            cp.wait()

    return pl.pallas_call(
        body, name="exchange_chips", out_shape=jax.ShapeDtypeStruct((N_CHIP - 1, r, w), part.dtype),
        in_specs=[_ANY], out_specs=_ANY,
        scratch_shapes=[pltpu.SemaphoreType.DMA((3,)), pltpu.SemaphoreType.DMA((3,))],
    )(part)


def allreduce_stats(st_mix, st_prep, st_mlp, st_ple):
    def body(mix_ref, prep_ref, mlp_ref, ple_ref, out_ref, mine, gath, send_sems, recv_sems):
        x, y, c = _mesh_pos()
        me = 4 * x + 2 * y + c
        mine[...] = jnp.zeros_like(mine)
        mine[ST_G_MIX:ST_G_MIX + 1, :] = mix_ref[...]
        mine[ST_G_QA:ST_G_KN + 1, 0:256] = prep_ref[...]
        mine[ST_G_MLP:ST_G_MLP + 1, :] = mlp_ref[...]
        mine[ST_G_PLE:ST_LOSS + 1, :] = ple_ref[...]
        gath[me] = mine[...]
        copies = []
        for k in range(1, N_DEV):
            peer = (_flip(x, k & 4), _flip(y, k & 2), _flip(c, k & 1))
            copies.append(_remote(mine, gath.at[me], send_sems, recv_sems, k - 1, peer))
        for cp in copies:
            cp.start()
        for cp in copies:
            cp.wait()
        acc = gath[0]
        for d in range(1, N_DEV):
            acc = acc + gath[d]
        out_ref[...] = acc

    vm = pl.BlockSpec(memory_space=pltpu.VMEM)
    return pl.pallas_call(
        body, name="allreduce_stats", out_shape=jax.ShapeDtypeStruct((ST_ROWS, D_MODEL), F32),
        in_specs=[vm] * 4, out_specs=vm,
        scratch_shapes=[pltpu.VMEM((ST_ROWS, D_MODEL), F32), pltpu.VMEM((N_DEV, ST_ROWS, D_MODEL), F32),
                        pltpu.SemaphoreType.DMA((N_DEV - 1,)), pltpu.SemaphoreType.DMA((N_DEV - 1,))],
    )(st_mix, st_prep, st_mlp, st_ple)


def adamw_gains(stats, gains):
    c1 = 1.0 - ADAM_B1 ** ADAM_STEP
    c2 = 1.0 - ADAM_B2 ** ADAM_STEP
    n = len(gains)

    def body(st_ref, *refs):
        ins, outs = refs[:3 * n], refs[3 * n:]
        for i, (row, w, _, _) in enumerate(gains):
            width = w.shape[1]
            gv = st_ref[row:row + 1, 0:width]
            mn = ADAM_B1 * ins[3 * i + 1][...] + (1.0 - ADAM_B1) * gv
            vn = ADAM_B2 * ins[3 * i + 2][...] + (1.0 - ADAM_B2) * (gv * gv)
            outs[4 * i][...] = gv
            outs[4 * i + 1][...] = -ADAM_LR * ((mn / c1) / (jnp.sqrt(vn / c2) + ADAM_EPS) + ADAM_WD * ins[3 * i][...])
            outs[4 * i + 2][...] = mn
            outs[4 * i + 3][...] = vn

    vm = pl.BlockSpec(memory_space=pltpu.VMEM)
    flat = [a for (_, w, m, v) in gains for a in (w, m, v)]
    out_shape = tuple(jax.ShapeDtypeStruct(w.shape, F32) for (_, w, _, _) in gains for _ in range(4))
    res = pl.pallas_call(body, name="adamw_gains", out_shape=out_shape, in_specs=[vm] * (1 + 3 * n),
                         out_specs=tuple([vm] * (4 * n)))(stats, *flat)
    return [res[4 * i:4 * i + 4] for i in range(n)]


def _row_tile(r, cap=640):
    return max(d for d in range(16, min(r, cap) + 1, 16) if r % d == 0)


def add_pairs(g, got, core):
    n, r, w = got.shape
    tr = _row_tile(r)

    def body(c_ref, a_ref, b_ref, o_ref):
        o_ref[...] = (a_ref[...].astype(F32) + b_ref[...].astype(F32)).astype(o_ref.dtype)

    spec = pl.BlockSpec((1, tr, w), lambda i, j, c: (i, j, 0))
    return pl.pallas_call(
        body, name="add_pairs", out_shape=jax.ShapeDtypeStruct(got.shape, got.dtype),
        grid_spec=pltpu.PrefetchScalarGridSpec(
            num_scalar_prefetch=1, grid=(n, r // tr),
            in_specs=[pl.BlockSpec((1, tr, w), lambda i, j, c: (2 * i + c[0], j, 0)), spec], out_specs=spec),
        compiler_params=_params(("parallel", "parallel")),
    )(core, g, got)


def sum_chips(part, land, chip):
    _, r, w = part.shape
    tr = _row_tile(r)

    def body(c_ref, p_ref, l_ref, o_ref):
        acc = p_ref[0].astype(F32)
        for s in range(N_CHIP - 1):
            acc = acc + l_ref[s].astype(F32)
        o_ref[...] = acc

    return pl.pallas_call(
        body, name="sum_chips", out_shape=jax.ShapeDtypeStruct((r, w), F32),
        grid_spec=pltpu.PrefetchScalarGridSpec(
            num_scalar_prefetch=1, grid=(r // tr,),
            in_specs=[pl.BlockSpec((1, tr, w), lambda i, c: (c[0], i, 0)), pl.BlockSpec((N_CHIP - 1, tr, w), lambda i, c: (0, i, 0))],
            out_specs=pl.BlockSpec((tr, w), lambda i, c: (i, 0))),
        compiler_params=_params(("parallel",)),
    )(chip, part, land)


def norm_x(x, g_mix, tm, comm=None):
    t = x.shape[0]
    grid = (t // tm,)
    c_ins, c_in_specs, c_outs, c_sems, alias = _comm_parts(comm, 2, 1)

    def body(*refs):
        (x_ref, g_ref), cin, (h_ref,), cout, _, csem = _split_refs(refs, 2, 1, 0, comm)
        _comm_start(comm, cin, cout, csem, grid)
        xv = x_ref[...]
        h_ref[...] = (xv * _rstd(xv, D_MODEL) * g_ref[...]).astype(BF16)
        _comm_finish(comm, cin, cout, csem, grid)

    return pl.pallas_call(
        body, name="norm_x", grid=grid, out_shape=(jax.ShapeDtypeStruct((t, D_MODEL), BF16), *c_outs),
        in_specs=[_rows(tm, D_MODEL), _resident((1, D_MODEL)), *c_in_specs],
        out_specs=(_rows(tm, D_MODEL), *([_ANY] * len(c_outs))),
        scratch_shapes=c_sems, input_output_aliases=alias, compiler_params=_params(("arbitrary",)),
    )(x, g_mix, *c_ins)


def pack_late_weights(w_up, w_down, w_o, w_pg, small, comm=None):
    rows2 = sum(PACK_ROWS[n] for n in PACK_W2)
    rows3 = sum(PACK_ROWS[n] for n in PACK_W3)
    c_ins, c_in_specs, c_outs, c_sems, alias = _comm_parts(comm, 5, 2)
    grid = (1,)

    def body(*refs):
        (up_ref, dn_ref, o_ref, pg_ref, sm_ref), cin, (p2_ref, p3_ref), cout, _, csem = _split_refs(refs, 5, 2, 0, comm)
        _comm_start(comm, cin, cout, csem, grid)
        p2_ref[0:128, :] = o_ref[0].astype(BF16)
        p2_ref[128:256, :] = pg_ref[0].astype(BF16)
        p2_ref[256:rows2, :] = sm_ref[...]
        p3_ref[0:512, :] = up_ref[0].T.astype(BF16)
        p3_ref[512:1024, :] = dn_ref[0].astype(BF16)
        _comm_finish(comm, cin, cout, csem, grid)

    def whole(a):
        nd = a.ndim
        return pl.BlockSpec(a.shape, lambda i: (0,) * nd)

    args = (w_up, w_down, w_o, w_pg, small)
    return pl.pallas_call(
        body, name="pack_late_weights", grid=grid,
        out_shape=(jax.ShapeDtypeStruct((rows2, D_MODEL), BF16), jax.ShapeDtypeStruct((rows3, D_MODEL), BF16), *c_outs),
        in_specs=[*[whole(a) for a in args], *c_in_specs],
        out_specs=(pl.BlockSpec((rows2, D_MODEL), lambda i: (0, 0)), pl.BlockSpec((rows3, D_MODEL), lambda i: (0, 0)),
                   *([_ANY] * len(c_outs))),
        scratch_shapes=c_sems, input_output_aliases=alias, compiler_params=_params(("arbitrary",)),
    )(*args, *c_ins)


def in_proj(h, w_in_t, tm, comm=None):
    t = h.shape[0]
    nc = 512
    half = ZP // 2
    grid = (t // tm,)
    c_ins, c_in_specs, c_outs, c_sems, alias = _comm_parts(comm, 2, 2)

    def body(*refs):
        (h_ref, w_ref), cin, (zg_ref, zs_ref), cout, _, csem = _split_refs(refs, 2, 2, 0, comm)
        _comm_start(comm, cin, cout, csem, grid)
        hv = h_ref[...]
        for cidx in range(half // nc):
            zg_ref[:, cidx * nc:(cidx + 1) * nc] = _dot_nt(hv, w_ref[cidx * nc:(cidx + 1) * nc, :]).astype(BF16)
        for cidx in range(half // nc):
            zs_ref[:, cidx * nc:(cidx + 1) * nc] = _dot_nt(hv, w_ref[half + cidx * nc:half + (cidx + 1) * nc, :]).astype(BF16)
        _comm_finish(comm, cin, cout, csem, grid)

    return pl.pallas_call(
        body, name="in_proj", grid=grid,
        out_shape=(jax.ShapeDtypeStruct((t, half), BF16), jax.ShapeDtypeStruct((t, half), BF16), *c_outs),
        in_specs=[_rows(tm, D_MODEL), _resident((ZP, D_MODEL)), *c_in_specs],
        out_specs=(_rows(tm, half), _rows(tm, half), *([_ANY] * len(c_outs))),
        scratch_shapes=c_sems, input_output_aliases=alias, compiler_params=_params(("arbitrary",)),
    )(h, w_in_t, *c_ins)


def attn_prep(zp, tabs, g_qa, g_kva, g_qn, g_kn, w_qb_t, w_kvb_t, tm, s_len, comm=None):
    t = zp.shape[0]
    nsb = s_len // tm
    scale_a = (QK_NOPE + QK_ROPE) ** -0.5
    scale_b = HD_B ** -0.5

    grid = (t // tm,)
    c_ins, c_in_specs, c_outs, c_sems, alias = _comm_parts(comm, 13, 8)

    def body(*refs):
        ((qb_ref, qlat_ref, kb_ref, vb_ref, ckv_ref, kpe_ref, tab_ref, gqa_ref, gkva_ref, gqn_ref, gkn_ref, wqb_ref,
          wkvb_ref), cin, (qa_o, ka_o, va_o, qb_o, kb_o, vb_o, cq_o, ckvn_o), cout, _, csem) = _split_refs(refs, 13, 8, 0, comm)
        _comm_start(comm, cin, cout, csem, grid)
        ca, s1a, s2a = tab_ref[0], tab_ref[1], tab_ref[2]
        ck = tab_ref[3]
        cb, s1b, s2b = tab_ref[4], tab_ref[5], tab_ref[6]
        ql = qlat_ref[...].astype(F32)
        cq = (ql * _rstd(ql, Q_LORA) * gqa_ref[...]).astype(BF16)
        cq_o[...] = cq
        qa = _dot_nt(cq, wqb_ref[...])
        slabs = [slice(h * HP, (h + 1) * HP) for h in range(H_A)]
        qa_o[...] = jnp.concatenate(
            [(_rope_fwd(qa[:, sl], ca, s1a, s2a) * scale_a).astype(BF16) for sl in slabs], axis=1)
        cr = ckv_ref[...].astype(F32)
        ckv = (cr * _rstd(cr, KV_LORA) * gkva_ref[...]).astype(BF16)
        ckvn_o[...] = ckv
        kva = _dot_nt(ckv, wkvb_ref[...])
        kpe = _rope_fwd(kpe_ref[...].astype(F32), ck, s1a, s2a)
        ka_o[...] = jnp.concatenate([(kva[:, sl] + kpe).astype(BF16) for sl in slabs], axis=1)
        va_o[...] = kva[:, H_A * HP:].astype(BF16)
        gqn, gkn = gqn_ref[...], gkn_ref[...]

        def norm_rope(ref, sl, g, scale):
            xs = ref[:, sl].astype(F32)
            y = _rope_fwd(xs * _rstd(xs, HD_B) * g, cb, s1b, s2b)
            return (y if scale is None else y * scale).astype(BF16)

        qb_o[...] = jnp.concatenate([norm_rope(qb_ref, sl, gqn, scale_b) for sl in slabs], axis=1)
        kb_o[...] = jnp.concatenate([norm_rope(kb_ref, sl, gkn, None) for sl in slabs[:KV_B]], axis=1)
        vb_o[...] = vb_ref[...].astype(BF16)
        _comm_finish(comm, cin, cout, csem, grid)

    def o(width):
        return jax.ShapeDtypeStruct((t, width), BF16)

    return pl.pallas_call(
        body, name="attn_prep", grid=grid,
        out_shape=(o(H_A * HP), o(H_A * HP), o(H_A * HP), o(H_B * HP), o(KV_B * HP), o(KV_B * HP), o(Q_LORA), o(KV_LORA),
                   *c_outs),
        in_specs=[_rows(tm, 1024, 0), _rows(tm, 256, 4), _rows(tm, 256, 5), _rows(tm, 256, 6),
                  _rows(tm, 128, 14), _rows(tm, 128, 15),
                  pl.BlockSpec((7, tm, HP), lambda i: (0, i % nsb, 0)),
                  _resident((1, Q_LORA)), _resident((1, KV_LORA)), _resident((1, HP)), _resident((1, HP)),
                  _resident((H_A * HP, Q_LORA)), _resident((2 * H_A * HP, KV_LORA)), *c_in_specs],
        out_specs=(_rows(tm, H_A * HP), _rows(tm, H_A * HP), _rows(tm, H_A * HP), _rows(tm, H_B * HP),
                   _rows(tm, KV_B * HP), _rows(tm, KV_B * HP), _rows(tm, Q_LORA), _rows(tm, KV_LORA), *([_ANY] * len(c_outs))),
        scratch_shapes=c_sems, input_output_aliases=alias, compiler_params=_params(("arbitrary",)),
    )(zp, zp, zp, zp, zp, zp, tabs, g_qa, g_kva, g_qn, g_kn, w_qb_t, w_kvb_t, *c_ins)


def attn_fwd(q, k, v, n_b, s_len, tq, name, comm=None):
    t = q.shape[0]
    n_h, n_hk = q.shape[1] // HP, k.shape[1] // HP
    grp = n_h // n_hk
    nq = s_len // tq
    sub = min(tq, 256)
    grid = (n_b, n_h, nq)
    c_ins, c_in_specs, c_outs, c_sems, alias = _comm_parts(comm, 3, 2)

    def body(*refs):
        (q_ref, k_ref, v_ref), cin, (o_ref, lse_ref), cout, _, csem = _split_refs(refs, 3, 2, 0, comm)
        _comm_start(comm, cin, cout, csem, grid)
        kv, vv = k_ref[...], v_ref[...]
        for r in range(tq // sub):
            rows = slice(r * sub, (r + 1) * sub)
            s = _dot_nt(q_ref[rows, :], kv)
            m = jnp.max(s, axis=-1, keepdims=True)
            p = jnp.exp(s - m)
            l = jnp.sum(p, axis=-1, keepdims=True)
            o_ref[rows, :] = (_dot_nn(p.astype(BF16), vv) * (1.0 / l)).astype(o_ref.dtype)
            lse_ref[rows, :] = jnp.broadcast_to(m + jnp.log(l), (sub, HP))
        _comm_finish(comm, cin, cout, csem, grid)

    qspec = pl.BlockSpec((tq, HP), lambda b, h, i: (b * nq + i, h))
    kspec = pl.BlockSpec((s_len, HP), lambda b, h, i: (b, h // grp))
    return pl.pallas_call(
        body, name=name, grid=grid,
        out_shape=(jax.ShapeDtypeStruct((t, n_h * HP), BF16), jax.ShapeDtypeStruct((t, n_h * HP), F32), *c_outs),
        in_specs=[qspec, kspec, kspec, *c_in_specs], out_specs=(qspec, qspec, *([_ANY] * len(c_outs))),
        scratch_shapes=c_sems, input_output_aliases=alias,
        compiler_params=_params(("arbitrary", "arbitrary", "arbitrary")),
    )(q, k, v, *c_ins)


def merge_fwd(oa, ob, zp, x, w_oa_t, w_ob_t, wpack, off, tm):
    t = x.shape[0]

    def body(oa_ref, ob_ref, ga_ref, gb_ref, x_ref, woa_ref, wob_ref, wo_ref, x1_o, mg_o, y_o):
        ya = _dot_nt(oa_ref[...], woa_ref[...])
        yb = _dot_nt(ob_ref[...], wob_ref[...])
        y_o[:, 0:D_MODEL] = ya.astype(BF16)
        y_o[:, D_MODEL:2 * D_MODEL] = yb.astype(BF16)
        merged = (jax.nn.sigmoid(ga_ref[...].astype(F32)) * ya + jax.nn.sigmoid(gb_ref[...].astype(F32)) * yb).astype(BF16)
        mg_o[...] = merged
        x1_o[...] = x_ref[...] + _dot_nn(merged, _wrows(wo_ref, 0, D_MODEL))

    return pl.pallas_call(
        body, name="merge_fwd", grid=(t // tm,),
        out_shape=(jax.ShapeDtypeStruct((t, D_MODEL), F32), jax.ShapeDtypeStruct((t, D_MODEL), BF16),
                   jax.ShapeDtypeStruct((t, 2 * D_MODEL), BF16)),
        in_specs=[_rows(tm, H_A * HP), _rows(tm, H_B * HP), _rows(tm, 1024, 0), _rows(tm, 1024, 1), _rows(tm, D_MODEL),
                  _resident((D_MODEL, H_A * HP)), _resident((D_MODEL, H_B * HP)), _packed_weight(128, off["w_o"])],
        out_specs=(_rows(tm, D_MODEL), _rows(tm, D_MODEL), _rows(tm, 2 * D_MODEL)), compiler_params=_params(("parallel",)),
    )(oa, ob, zp, zp, x, w_oa_t, w_ob_t, wpack)


def mlp_fwd(x1, g_mlp, wpack, off, tm):
    t = x1.shape[0]
    fc = 1024

    def body(x_ref, g_ref, wup_ref, wdn_ref, x2_o, u_o):
        xv = x_ref[...]
        h2 = (xv * _rstd(xv, D_MODEL) * g_ref[...]).astype(BF16)
        acc = xv
        for cidx in range(D_FF // fc):
            sl = slice(cidx * fc, (cidx + 1) * fc)
            u = jnp.maximum(_dot_nt(h2, _wrows(wup_ref, cidx * fc, fc)), 0.0)
            u_o[:, sl] = u.astype(BF16)
            acc = acc + _dot_nn((u * u).astype(BF16), _wrows(wdn_ref, cidx * fc, fc))
        x2_o[...] = acc

    return pl.pallas_call(
        body, name="mlp_fwd", grid=(t // tm,),
        out_shape=(jax.ShapeDtypeStruct((t, D_MODEL), F32), jax.ShapeDtypeStruct((t, D_FF), BF16)),
        in_specs=[_rows(tm, D_MODEL), _resident((1, D_MODEL)), _packed_weight(512, off["w_up"]), _packed_weight(512, off["w_down"])],
        out_specs=(_rows(tm, D_MODEL), _rows(tm, D_FF)), compiler_params=_params(("parallel",)),
    )(x1, g_mlp, wpack, wpack)


def ple_loss_bwd(x2, p, tgt, g_ple, g_final, wpack, off, w_ple_t, tm):
    t = x2.shape[0]
    inv_d = 1.0 / D_MODEL

    def body(x2_ref, p_ref, tg_ref, gp_ref, gf_ref, wpg_ref, wple_ref, dx2_o, dt_o, h3_o, dpe_o, st_o, dx2b_o):
        @pl.when(pl.program_id(0) == 0)
        def _():
            st_o[...] = jnp.zeros_like(st_o)

        x2v = x2_ref[...]
        gp, gf = gp_ref[...], gf_ref[...]
        w_pg = _wrows(wpg_ref, 0, D_MODEL)
        r2 = _rstd(x2v, D_MODEL)
        xh2 = x2v * r2
        h3 = (xh2 * gp).astype(BF16)
        h3_o[...] = h3
        gate = jax.nn.sigmoid(_dot_nn(h3, w_pg))
        pe = _dot_nt(p_ref[...].astype(BF16), wple_ref[...])
        x3 = x2v + gate * pe
        r3 = _rstd(x3, D_MODEL)
        xh3 = x3 * r3
        err = xh3 * gf - tg_ref[...]
        dy = err * inv_d
        dx3 = _rms_bwd(dy, xh3, r3, gf, D_MODEL)
        dpe_o[...] = (dx3 * gate).astype(BF16)
        dt = (dx3 * pe * gate * (1.0 - gate)).astype(BF16)
        dt_o[...] = dt
        dh3 = _dot_nt(dt, w_pg)
        dx2 = dx3 + _rms_bwd(dh3, xh2, r2, gp, D_MODEL)
        dx2_o[...] = dx2
        dx2b_o[...] = dx2.astype(BF16)
        st_o[0:1, :] += _colsum(dh3 * xh2)
        st_o[1:2, :] += _colsum(dy * xh3)
        st_o[2:3, :] += _colsum(err * err) * (0.5 * inv_d)

    bf = jax.ShapeDtypeStruct((t, D_MODEL), BF16)
    return pl.pallas_call(
        body, name="ple_loss_bwd", grid=(t // tm,),
        out_shape=(jax.ShapeDtypeStruct((t, D_MODEL), F32), bf, bf, bf, jax.ShapeDtypeStruct((3, D_MODEL), F32), bf),
        in_specs=[_rows(tm, D_MODEL), _rows(tm, PLE_DIM), _rows(tm, D_MODEL), _resident((1, D_MODEL)), _resident((1, D_MODEL)),
                  _packed_weight(128, off["w_ple_gate"]), _resident((D_MODEL, PLE_DIM))],
        out_specs=(_rows(tm, D_MODEL), _rows(tm, D_MODEL), _rows(tm, D_MODEL), _rows(tm, D_MODEL),
                   pl.BlockSpec((3, D_MODEL), lambda i: (0, 0)), _rows(tm, D_MODEL)),
        compiler_params=_params(("arbitrary",)),
    )(x2, p, tgt, g_ple, g_final, wpack, w_ple_t)


def mlp_bwd(dx2, x1, u, g_mlp, wpack, off, tm):
    t = x1.shape[0]
    fc = 1024

    def body(dx2_ref, x1_ref, u_ref, g_ref, wup_ref, wdn_ref, dx1_o, da_o, h2_o, st_o, dx1b_o):
        @pl.when(pl.program_id(0) == 0)
        def _():
            st_o[...] = jnp.zeros_like(st_o)

        d2 = dx2_ref[...]
        d2b = d2.astype(BF16)
        dh2 = jnp.zeros((tm, D_MODEL), F32)
        for cidx in range(D_FF // fc):
            sl = slice(cidx * fc, (cidx + 1) * fc)
            da = (_dot_nt(d2b, _wrows(wdn_ref, cidx * fc, fc)) * (2.0 * u_ref[:, sl].astype(F32))).astype(BF16)
            da_o[:, sl] = da
            dh2 = dh2 + _dot_nn(da, _wrows(wup_ref, cidx * fc, fc))
        xv = x1_ref[...]
        g = g_ref[...]
        r1 = _rstd(xv, D_MODEL)
        xh1 = xv * r1
        h2_o[...] = (xh1 * g).astype(BF16)
        st_o[...] += _colsum(dh2 * xh1)
        dx1 = d2 + _rms_bwd(dh2, xh1, r1, g, D_MODEL)
        dx1_o[...] = dx1
        dx1b_o[...] = dx1.astype(BF16)

    return pl.pallas_call(
        body, name="mlp_bwd", grid=(t // tm,),
        out_shape=(jax.ShapeDtypeStruct((t, D_MODEL), F32), jax.ShapeDtypeStruct((t, D_FF), BF16),
                   jax.ShapeDtypeStruct((t, D_MODEL), BF16), jax.ShapeDtypeStruct((1, D_MODEL), F32),
                   jax.ShapeDtypeStruct((t, D_MODEL), BF16)),
        in_specs=[_rows(tm, D_MODEL), _rows(tm, D_MODEL), _rows(tm, D_FF), _resident((1, D_MODEL)),
                  _packed_weight(512, off["w_up"]), _packed_weight(512, off["w_down"])],
        out_specs=(_rows(tm, D_MODEL), _rows(tm, D_FF), _rows(tm, D_MODEL), pl.BlockSpec((1, D_MODEL), lambda i: (0, 0)),
                   _rows(tm, D_MODEL)),
        compiler_params=_params(("arbitrary",)),
    )(dx2, x1, u, g_mlp, wpack, wpack)


def merge_bwd(dx1b, yab, zp, w_oa_t, w_ob_t, wpack, off, tm, comm=None):
    t = dx1b.shape[0]
    grid = (t // tm,)
    c_ins, c_in_specs, c_outs, c_sems, alias = _comm_parts(comm, 7, 5)

    def body(*refs):
        ((dx1_ref, y_ref, ga_ref, gb_ref, woa_ref, wob_ref, wo_ref), cin,
         (doa_o, dob_o, dg_o, dya_o, dyb_o), cout, _, csem) = _split_refs(refs, 7, 5, 0, comm)
        _comm_start(comm, cin, cout, csem, grid)
        dm = _dot_nt(dx1_ref[...], _wrows(wo_ref, 0, D_MODEL))
        for g_ref, w_ref, do_o, dy_o, col in ((ga_ref, woa_ref, doa_o, dya_o, 0), (gb_ref, wob_ref, dob_o, dyb_o, 1)):
            cols = slice(col * D_MODEL, (col + 1) * D_MODEL)
            sg = jax.nn.sigmoid(g_ref[...].astype(F32))
            dyv = (dm * sg).astype(BF16)
            dy_o[...] = dyv
            dg_o[:, cols] = (dm * y_ref[:, cols].astype(F32) * sg * (1.0 - sg)).astype(BF16)
            do_o[...] = _dot_nn(dyv, w_ref[...]).astype(BF16)
        _comm_finish(comm, cin, cout, csem, grid)

    bf = jax.ShapeDtypeStruct((t, D_MODEL), BF16)
    return pl.pallas_call(
        body, name="merge_bwd", grid=grid,
        out_shape=(bf, bf, jax.ShapeDtypeStruct((t, ZP), BF16), bf, bf, *c_outs),
        in_specs=[_rows(tm, D_MODEL), _rows(tm, 2 * D_MODEL), _rows(tm, 1024, 0), _rows(tm, 1024, 1),
                  _resident((D_MODEL, H_A * HP)), _resident((D_MODEL, H_B * HP)), _packed_weight(128, off["w_o"]), *c_in_specs],
        out_specs=(_rows(tm, D_MODEL), _rows(tm, D_MODEL), _rows(tm, 2 * D_MODEL), _rows(tm, D_MODEL), _rows(tm, D_MODEL),
                   *([_ANY] * len(c_outs))),
        scratch_shapes=c_sems, input_output_aliases=alias,
        compiler_params=_params(("arbitrary",)),
    )(dx1b, yab, zp, zp, w_oa_t, w_ob_t, wpack, *c_ins)


def attn_bwd(q, k, v, do, o, lse, n_b, s_len, tq, name, comm=None):
    t = q.shape[0]
    n_h, n_hk = q.shape[1] // HP, k.shape[1] // HP
    grp = n_h // n_hk
    nq = s_len // tq
    sub = min(tq, 256)
    grid = (n_b, n_hk, grp, nq)
    c_ins, c_in_specs, c_outs, c_sems, alias = _comm_parts(comm, 6, 3)

    def body(*refs):
        ((q_ref, k_ref, v_ref, do_ref, o_ref, lse_ref), cin, (dq_o, dk_o, dv_o), cout, (p_s, ds_s, dk_acc, dv_acc),
         csem) = _split_refs(refs, 6, 3, 4, comm)
        _comm_start(comm, cin, cout, csem, grid)

        @pl.when((pl.program_id(2) == 0) & (pl.program_id(3) == 0))
        def _():
            dk_acc[...] = jnp.zeros_like(dk_acc)
            dv_acc[...] = jnp.zeros_like(dv_acc)

        kv, vv = k_ref[...], v_ref[...]
        for r in range(tq // sub):
            rows = slice(r * sub, (r + 1) * sub)
            qv, dov = q_ref[rows, :], do_ref[rows, :]
            delta = jnp.sum(dov.astype(F32) * o_ref[rows, :].astype(F32), axis=-1, keepdims=True)
            delta_row = jnp.broadcast_to(delta, (sub, HP)).T[0:1, :]
            lse_row = lse_ref[rows, :].T[0:1, :]
            pt = jnp.exp(_dot_nt(kv, qv) - lse_row)
            dst = (pt * (_dot_nt(vv, dov) - delta_row)).astype(BF16)
            p_s[:, rows] = pt.astype(BF16)
            ds_s[:, rows] = dst
            dq_o[rows, :] = _dot_tn(dst, kv).astype(dq_o.dtype)
        dk_acc[...] += _dot_nn(ds_s[...], q_ref[...])
        dv_acc[...] += _dot_nn(p_s[...], do_ref[...])

        @pl.when((pl.program_id(2) == grp - 1) & (pl.program_id(3) == nq - 1))
        def _():
            dk_o[...] = dk_acc[...].astype(dk_o.dtype)
            dv_o[...] = dv_acc[...].astype(dv_o.dtype)

        _comm_finish(comm, cin, cout, csem, grid)

    qspec = pl.BlockSpec((tq, HP), lambda b, hk, g, i: (b * nq + i, hk * grp + g))
    kspec = pl.BlockSpec((s_len, HP), lambda b, hk, g, i: (b, hk))
    return pl.pallas_call(
        body, name=name, grid=grid,
        out_shape=(jax.ShapeDtypeStruct((t, n_h * HP), BF16), jax.ShapeDtypeStruct((t, n_hk * HP), BF16),
                   jax.ShapeDtypeStruct((t, n_hk * HP), BF16), *c_outs),
        in_specs=[qspec, kspec, kspec, qspec, qspec, qspec, *c_in_specs],
        out_specs=(qspec, kspec, kspec, *([_ANY] * len(c_outs))),
        scratch_shapes=[pltpu.VMEM((s_len, tq), BF16), pltpu.VMEM((s_len, tq), BF16),
                        pltpu.VMEM((s_len, HP), F32), pltpu.VMEM((s_len, HP), F32), *c_sems],
        input_output_aliases=alias,
        compiler_params=_params(("arbitrary", "arbitrary", "arbitrary", "arbitrary")),
    )(q, k, v, do, o, lse, *c_ins)


def prep_bwd(dqa, dka, dva, dqb, dkb, dvb, zp, dz, tabs, g_qa, g_kva, g_qn, g_kn, w_qb_t, w_kvb_t, tm, s_len):
    t = zp.shape[0]
    nsb = s_len // tm
    scale_a = (QK_NOPE + QK_ROPE) ** -0.5
    scale_b = HD_B ** -0.5

    def body(dqa_ref, dka_ref, dva_ref, dqb_ref, dkb_ref, dvb_ref, qb_ref, qlat_ref, kb_ref, ckv_ref, tab_ref,
             gqa_ref, gkva_ref, gqn_ref, gkn_ref, wqb_ref, wkvb_ref, _, dz_o, dqap_o, dkva_o, st_o):
        dzq_o, dsm_o = dz_o.at[:, 0:1024], dz_o.at[:, 1024:2048]

        @pl.when(pl.program_id(0) == 0)
        def _():
            st_o[...] = jnp.zeros_like(st_o)

        ca, s1a, s2a = tab_ref[0], tab_ref[1], tab_ref[2]
        ck = tab_ref[3]
        cb, s1b, s2b = tab_ref[4], tab_ref[5], tab_ref[6]
        for h in range(H_A):
            sl = slice(h * HP, (h + 1) * HP)
            dqap_o[:, sl] = _rope_bwd(dqa_ref[:, sl].astype(F32) * scale_a, ca, s1a, s2a).astype(BF16)
        dcq = _dot_nn(dqap_o[...], wqb_ref[...])
        ql = qlat_ref[...].astype(F32)
        rq = _rstd(ql, Q_LORA)
        xh = ql * rq
        gqa = gqa_ref[...]
        st_o[0:1, :] += _colsum(dcq * xh)
        dsm_o[:, 0:256] = _rms_bwd(dcq, xh, rq, gqa, Q_LORA).astype(BF16)
        dkpe = jnp.zeros((tm, HP), F32)
        for h in range(H_A):
            sl = slice(h * HP, (h + 1) * HP)
            dk = dka_ref[:, sl]
            dkpe = dkpe + dk.astype(F32)
            dkva_o[:, sl] = dk.astype(BF16)
        dkva_o[:, H_A * HP:] = dva_ref[...].astype(BF16)
        dsm_o[:, 896:1024] = _rope_bwd(dkpe, ck, s1a, s2a).astype(BF16)
        dckv = _dot_nn(dkva_o[...], wkvb_ref[...])
        cr = ckv_ref[...].astype(F32)
        rk = _rstd(cr, KV_LORA)
        xh = cr * rk
        st_o[1:2, 0:128] += _colsum(dckv * xh)
        dsm_o[:, 768:896] = _rms_bwd(dckv, xh, rk, gkva_ref[...], KV_LORA).astype(BF16)
        gqn, gkn = gqn_ref[...], gkn_ref[...]
        dgq = jnp.zeros((1, HP), F32)
        for h in range(H_B):
            sl = slice(h * HP, (h + 1) * HP)
            dy = _rope_bwd(dqb_ref[:, sl].astype(F32) * scale_b, cb, s1b, s2b)
            xs = qb_ref[:, sl].astype(F32)
            r = _rstd(xs, HD_B)
            xh = xs * r
            dgq = dgq + _colsum(dy * xh)
            dzq_o[:, sl] = _rms_bwd(dy, xh, r, gqn, HD_B).astype(BF16)
        st_o[2:3, 0:128] += dgq
        dgk = jnp.zeros((1, HP), F32)
        for h in range(KV_B):
            sl = slice(h * HP, (h + 1) * HP)
            dy = _rope_bwd(dkb_ref[:, sl].astype(F32), cb, s1b, s2b)
            xs = kb_ref[:, sl].astype(F32)
            r = _rstd(xs, HD_B)
            xh = xs * r
            dgk = dgk + _colsum(dy * xh)
            dsm_o[:, 256 + h * HP:256 + (h + 1) * HP] = _rms_bwd(dy, xh, r, gkn, HD_B).astype(BF16)
        st_o[3:4, 0:128] += dgk
        dsm_o[:, 512:768] = dvb_ref[...].astype(BF16)

    return pl.pallas_call(
        body, name="prep_bwd", grid=(t // tm,),
        out_shape=(jax.ShapeDtypeStruct((t, ZP), BF16), jax.ShapeDtypeStruct((t, 1024), BF16),
                   jax.ShapeDtypeStruct((t, 2048), BF16), jax.ShapeDtypeStruct((4, 256), F32)),
        in_specs=[_rows(tm, 1024), _rows(tm, 1024), _rows(tm, 1024), _rows(tm, 1024), _rows(tm, 256), _rows(tm, 256),
                  _rows(tm, 1024, 0), _rows(tm, 256, 4), _rows(tm, 256, 5), _rows(tm, 128, 14),
                  pl.BlockSpec((7, tm, HP), lambda i: (0, i % nsb, 0)),
                  _resident((1, Q_LORA)), _resident((1, KV_LORA)), _resident((1, HP)), _resident((1, HP)),
                  _resident((H_A * HP, Q_LORA)), _resident((2 * H_A * HP, KV_LORA)), _ANY],
        out_specs=(_rows(tm, 2048, 1), _rows(tm, 1024), _rows(tm, 2048), pl.BlockSpec((4, 256), lambda i: (0, 0))),
        input_output_aliases={17: 0}, compiler_params=_params(("arbitrary",)),
    )(dqa, dka, dva, dqb, dkb, dvb, zp, zp, zp, zp, tabs, g_qa, g_kva, g_qn, g_kn, w_qb_t, w_kvb_t, dz)


def in_bwd(dz, x, dx1, g_mix, w_in_t, tm, comm=None):
    t = x.shape[0]
    grid = (t // tm,)
    c_ins, c_in_specs, c_outs, c_sems, alias = _comm_parts(comm, 5, 2)

    def body(*refs):
        (dz_ref, x_ref, dx1_ref, g_ref, w_ref), cin, (dx_o, st_o), cout, _, csem = _split_refs(refs, 5, 2, 0, comm)
        _comm_start(comm, cin, cout, csem, grid)

        @pl.when(pl.program_id(0) == 0)
        def _():
            st_o[...] = jnp.zeros_like(st_o)

        dh = _dot_nn(dz_ref[...], w_ref[...])
        xv = x_ref[...]
        g = g_ref[...]
        r = _rstd(xv, D_MODEL)
        xh = xv * r
        st_o[...] += _colsum(dh * xh)
        dx_o[...] = dx1_ref[...] + _rms_bwd(dh, xh, r, g, D_MODEL)
        _comm_finish(comm, cin, cout, csem, grid)

    return pl.pallas_call(
        body, name="in_bwd", grid=grid,
        out_shape=(jax.ShapeDtypeStruct((t, D_MODEL), F32), jax.ShapeDtypeStruct((1, D_MODEL), F32), *c_outs),
        in_specs=[_rows(tm, ZP), _rows(tm, D_MODEL), _rows(tm, D_MODEL),
                  _resident((1, D_MODEL)), _resident((ZP, D_MODEL)), *c_in_specs],
        out_specs=(_rows(tm, D_MODEL), pl.BlockSpec((1, D_MODEL), lambda i: (0, 0)), *([_ANY] * len(c_outs))),
        scratch_shapes=c_sems, input_output_aliases=alias,
        compiler_params=_params(("arbitrary",)),
    )(dz, x, dx1, g_mix, w_in_t, *c_ins)


def matmul_tn(a, b, name, square_a=False):
    t, m = a.shape
    n = b.shape[1]
    bm = min(m, 512)
    tk = min(t, 4096)

    def body(a_ref, b_ref, o_ref):
        @pl.when(pl.program_id(1) == 0)
        def _():
            o_ref[...] = jnp.zeros_like(o_ref)

        av = a_ref[...]
        if square_a:
            av = (av.astype(F32) * av.astype(F32))
        o_ref[...] += _dot_tn(av.astype(BF16), b_ref[...].astype(BF16))

    return pl.pallas_call(
        body, name=name, grid=(m // bm, t // tk), out_shape=jax.ShapeDtypeStruct((m, n), F32),
        in_specs=[pl.BlockSpec((tk, bm), lambda i, kk: (kk, i)), pl.BlockSpec((tk, n), lambda i, kk: (kk, 0))],
        out_specs=pl.BlockSpec((bm, n), lambda i, kk: (i, 0)),
        compiler_params=_params(("parallel", "arbitrary")),
    )(a, b)


def matmul_tn_packed(a, b, name, rows, row_off, total_rows, buf=None, square_a=False):
    t, m = a.shape
    n = b.shape[1]
    pd = max(1, 512 // rows)
    bm = pd * rows
    tk = min(t, 4096)
    nk = t // tk

    def body(a_ref, b_ref, *rest):
        o_ref, acc = rest[-2], rest[-1]

        @pl.when(pl.program_id(1) == 0)
        def _():
            acc[...] = jnp.zeros_like(acc)

        av = a_ref[...]
        if square_a:
            av = (av.astype(F32) * av.astype(F32))
        acc[...] += _dot_tn(av.astype(BF16), b_ref[...].astype(BF16))

        @pl.when(pl.program_id(1) == nk - 1)
        def _():
            o_ref[...] = acc[...].reshape(pd, rows, n).astype(o_ref.dtype)

    in_specs = [pl.BlockSpec((tk, bm), lambda i, kk: (kk, i)), pl.BlockSpec((tk, n), lambda i, kk: (kk, 0))]
    args = [a, b]
    if buf is not None:
        in_specs.append(_ANY)
        args.append(buf)
    return pl.pallas_call(
        body, name=name, grid=(m // bm, nk), out_shape=jax.ShapeDtypeStruct((N_DEV, total_rows, n), BF16),
        in_specs=in_specs, out_specs=pl.BlockSpec((pd, rows, n), lambda i, kk: (i, row_off // rows, 0)),
        scratch_shapes=[pltpu.VMEM((bm, n), F32)], input_output_aliases={2: 0} if buf is not None else {},
        compiler_params=_params(("parallel", "arbitrary")),
    )(*args)


def adamw(w, g, m, v, name, g_transposed=False):
    _, r, c = w.shape
    tr = 256 if (not g_transposed and r > 256 and r % 256 == 0) else r
    c1 = 1.0 - ADAM_B1 ** ADAM_STEP
    c2 = 1.0 - ADAM_B2 ** ADAM_STEP

    def body(w_ref, g_ref, m_ref, v_ref, g_o, d_o, m_o, v_o):
        gv = g_ref[...].T if g_transposed else g_ref[...]
        mn = ADAM_B1 * m_ref[0] + (1.0 - ADAM_B1) * gv
        vn = ADAM_B2 * v_ref[0] + (1.0 - ADAM_B2) * (gv * gv)
        g_o[0] = gv
        m_o[0] = mn
        v_o[0] = vn
        d_o[0] = -ADAM_LR * ((mn / c1) / (jnp.sqrt(vn / c2) + ADAM_EPS) + ADAM_WD * w_ref[0])

    spec = pl.BlockSpec((1, tr, c), lambda i: (0, i, 0))
    gspec = pl.BlockSpec((c, r), lambda i: (0, 0)) if g_transposed else pl.BlockSpec((tr, c), lambda i: (i, 0))
    shp = jax.ShapeDtypeStruct((1, r, c), F32)
    return pl.pallas_call(
        body, name=name, grid=(r // tr,), out_shape=(shp,) * 4, in_specs=[spec, gspec, spec, spec], out_specs=(spec,) * 4,
        compiler_params=_params(("parallel",)),
    )(w, g, m, v)


def _rope_tables(s_len):
    def angles(pos, dim):
        inv = np.float32(ROPE_THETA) ** (-np.arange(0, dim, 2, dtype=np.float32) / np.float32(dim))
        return pos.astype(np.float32)[:, None] * inv[None, :]

    tpos = np.arange(s_len)
    a1 = angles(tpos, QK_ROPE)
    ar = angles(tpos // GRID_W, HD_B // 2)
    ac = angles(tpos % GRID_W, HD_B // 2)
    z16 = np.zeros((s_len, 16), np.float32)
    z32 = np.zeros((s_len, 32), np.float32)
    z64 = np.zeros((s_len, 64), np.float32)
    one64 = np.ones((s_len, 64), np.float32)
    c1, s1 = np.cos(a1), np.sin(a1)
    ca = np.concatenate([one64, c1, c1, z32], axis=1)
    ck = np.concatenate([z64, c1, c1, z32], axis=1)
    s1a = np.concatenate([z64, -s1, z16, z32], axis=1)
    s2a = np.concatenate([z64, z16, s1, z32], axis=1)
    cr, sr, cc, sc = np.cos(ar), np.sin(ar), np.cos(ac), np.sin(ac)
    cb = np.concatenate([cr, cr, cc, cc, z64], axis=1)
    s1b = np.concatenate([-sr, z16, -sc, z16, z64], axis=1)
    s2b = np.concatenate([z16, sr, z16, sc, z64], axis=1)
    return jnp.asarray(np.stack([ca, s1a, s2a, ck, cb, s1b, s2b]).astype(np.float32))


def _pad_heads(a, n_heads, axis):
    shp = a.shape
    a = a.reshape(shp[:axis] + (n_heads, shp[axis] // n_heads) + shp[axis + 1:])
    pad = [(0, 0)] * a.ndim
    pad[axis + 1] = (0, HP - a.shape[axis + 1])
    a = jnp.pad(a, pad)
    return a.reshape(shp[:axis] + (n_heads * HP,) + shp[axis + 1:])


def _unpad_heads(a, n_heads, width, axis):
    shp = a.shape
    a = a.reshape(shp[:axis] + (n_heads, HP) + shp[axis + 1:])
    a = lax.slice_in_dim(a, 0, width, axis=axis + 1)
    return a.reshape(shp[:axis] + (n_heads * width,) + shp[axis + 1:])


def _pack_rows(blocks, names):
    parts = []
    for name in names:
        b = blocks[name]
        padr = PACK_ROWS[name] - b.shape[-2]
        if padr:
            b = jnp.pad(b, [(0, 0)] * (b.ndim - 2) + [(0, padr), (0, 0)])
        parts.append(b)
    return jnp.concatenate(parts, axis=parts[0].ndim - 2)


def _expand_w_in(wt):
    z64 = jnp.zeros((64, D_MODEL), wt.dtype)
    z32 = jnp.zeros((32, D_MODEL), wt.dtype)
    return jnp.concatenate([
        wt[1184:2208], wt[2208:3232], _pad_heads(wt[416:928], H_B, 0), wt[0:256],
        _pad_heads(wt[928:1056], KV_B, 0), _pad_heads(wt[1056:1184], KV_B, 0), wt[256:384],
        z64, wt[384:416], z32], axis=0)


def _collapse_w_in(dw):
    dg, dq, ds = dw[0:2048], dw[2048:3072], dw[3072:4096]
    return jnp.concatenate([
        ds[0:256], ds[768:896], ds[960:992], _unpad_heads(dq, H_B, HD_B, 0), _unpad_heads(ds[256:512], KV_B, HD_B, 0),
        _unpad_heads(ds[512:768], KV_B, HD_B, 0), dg], axis=0)


def kernel(x, p, g_mix, w_in, g_qa, w_qb, g_kva, w_kvb, g_qn, g_kn, w_oa, w_ob, w_o, g_mlp, w_up, w_down, g_ple, w_ple_gate, w_ple, g_final, loss_target, m_g_mix, m_w_in, m_g_qa, m_w_qb, m_g_kva, m_w_kvb, m_g_qn, m_g_kn, m_w_oa, m_w_ob, m_w_o, m_g_mlp, m_w_up, m_w_down, m_g_ple, m_w_ple_gate, m_w_ple, m_g_final, v_g_mix, v_w_in, v_g_qa, v_w_qb, v_g_kva, v_w_kvb, v_g_qn, v_g_kn, v_w_oa, v_w_ob, v_w_o, v_g_mlp, v_w_up, v_w_down, v_g_ple, v_w_ple_gate, v_w_ple, v_g_final):
    n_b, s_len, _ = x.shape
    t = n_b * s_len
    tm = min(512, s_len)
    tq_f = min(2048, s_len)
    tq_b = min(2048, s_len)

    mats = dict(w_in=(w_in, m_w_in, v_w_in), w_qb=(w_qb, m_w_qb, v_w_qb), w_kvb=(w_kvb, m_w_kvb, v_w_kvb),
                w_oa=(w_oa, m_w_oa, v_w_oa), w_ob=(w_ob, m_w_ob, v_w_ob), w_o=(w_o, m_w_o, v_w_o),
                w_up=(w_up, m_w_up, v_w_up), w_down=(w_down, m_w_down, v_w_down),
                w_ple_gate=(w_ple_gate, m_w_ple_gate, v_w_ple_gate), w_ple=(w_ple, m_w_ple, v_w_ple))
    col_sharded = ("w_in", "w_qb", "w_kvb", "w_oa", "w_ob", "w_up", "w_ple")

    blocks = {}
    for name in PACK_W1 + ("w_oa", "w_ob", "w_ple"):
        blocks[name] = mats[name][0][0].T.reshape(-1, D_MODEL).astype(BF16)
    off_w1, _ = _pack_offsets(PACK_W1)
    off_w2, _ = _pack_offsets(PACK_W2)
    off_w3, _ = _pack_offsets(PACK_W3)
    xf = x.reshape(t, D_MODEL)
    h, full1 = norm_x(xf, g_mix, tm, comm=gather3_first_comm(_pack_rows(blocks, PACK_W1)))
    pack2, pack3, full1 = pack_late_weights(w_up, w_down, w_o, w_ple_gate, _pack_rows(blocks, ("w_oa", "w_ob", "w_ple")),
                                            comm=gather3_second_comm(full1, then_third=True))

    def gathered(full, offs, name, rows, width):
        return full[:, offs[name]:offs[name] + rows].reshape(-1, width)

    w_in_t = _expand_w_in(gathered(full1, off_w1, "w_in", 404, D_MODEL))
    w_qb_t = _pad_heads(gathered(full1, off_w1, "w_qb", 24, Q_LORA), H_A, 0)
    wkvb = gathered(full1, off_w1, "w_kvb", 16, KV_LORA).reshape(H_A, 2, 64, KV_LORA)
    w_kvb_t = jnp.concatenate([_pad_heads(wkvb[:, 0].reshape(-1, KV_LORA), H_A, 0),
                               _pad_heads(wkvb[:, 1].reshape(-1, KV_LORA), H_A, 0)], axis=0)

    tabs = _rope_tables(s_len)
    g_qn_p = jnp.pad(g_qn, ((0, 0), (0, HP - HD_B)))
    g_kn_p = jnp.pad(g_kn, ((0, 0), (0, HP - HD_B)))
    pf = p.reshape(t, PLE_DIM)
    tgt = loss_target.reshape(t, D_MODEL)

    zg, zs, full2 = in_proj(h, w_in_t, tm, comm=gather3_first_comm(pack2))
    qa, ka, va, qb, kb, vb, cq, ckv, full2 = attn_prep(zs, tabs, g_qa, g_kva, g_qn_p, g_kn_p, w_qb_t, w_kvb_t, tm, s_len,
                                                       comm=gather3_second_comm(full2))
    oa, lse_a, full3, full2 = attn_fwd(qa, ka, va, n_b, s_len, tq_f, "attn_a_fwd",
                                       comm=_join_comms(gather_first_comm(pack3), gather3_third_comm(full2)))
    ob, lse_b, full3 = attn_fwd(qb, kb, vb, n_b, s_len, tq_f, "attn_b_fwd", comm=gather_pass_comm(full3))
    w_oa_t = _pad_heads(gathered(full2, off_w2, "w_oa", 64, H_A * V_DIM_A), H_A, 1)
    w_ob_t = _pad_heads(gathered(full2, off_w2, "w_ob", 64, H_B * HD_B), H_B, 1)
    w_ple_t = gathered(full2, off_w2, "w_ple", 32, PLE_DIM)
    x1, merged, yab = merge_fwd(oa, ob, zg, xf, w_oa_t, w_ob_t, full2, off_w2, tm)
    x2, u = mlp_fwd(x1, g_mlp, full3, off_w3, tm)
    dx2, dt, h3, dpe, st_ple, dx2b = ple_loss_bwd(x2, pf, tgt, g_ple, g_final.reshape(1, D_MODEL), full2, off_w2, w_ple_t, tm)
    dx1, da, h2, st_mlp, dx1b = mlp_bwd(dx2, x1, u, g_mlp, full3, off_w3, tm)

    core = lax.axis_index("c").astype(jnp.int32).reshape(1)
    chip = (2 * lax.axis_index("x") + lax.axis_index("y")).astype(jnp.int32).reshape(1)

    def packed(gblocks, names):
        return _pack_rows({n: gblocks[n].reshape(N_DEV, -1, D_MODEL).astype(BF16) for n in names}, names)

    off_g1, rows_g1 = _pack_offsets(PACK_G1)
    gpack1 = matmul_tn_packed(da, h2, "gw_up", 512, off_g1["w_up"], rows_g1)
    gpack1 = matmul_tn_packed(u, dx2b, "gw_down", 512, off_g1["w_down"], rows_g1, buf=gpack1, square_a=True)
    gpack1 = matmul_tn_packed(h3, dt, "gw_pg", 128, off_g1["w_ple_gate"], rows_g1, buf=gpack1)
    gple = matmul_tn(dpe, pf, "gw_ple").reshape(N_DEV, -1, D_MODEL).astype(BF16)
    gpack1 = lax.dynamic_update_slice(gpack1, gple, (0, off_g1["w_ple"], 0))
    doa, dob, dz, dya, dyb, got1 = merge_bwd(dx1b, yab, zg, w_oa_t, w_ob_t, full2, off_w2, tm,
                                               comm=scatter_sibling_comm(gpack1))
    part1 = add_pairs(gpack1, got1, core)

    off_g2, rows_g2 = _pack_offsets(PACK_G2)
    g2 = dict(w_oa=_unpad_heads(matmul_tn(dya, oa, "gw_oa"), H_A, V_DIM_A, 1),
              w_ob=_unpad_heads(matmul_tn(dyb, ob, "gw_ob"), H_B, HD_B, 1))
    gpack2 = matmul_tn_packed(merged, dx1b, "gw_o", 128, off_g2["w_o"], rows_g2)
    gpack2 = lax.dynamic_update_slice(gpack2, packed(g2, ("w_oa", "w_ob")), (0, off_g2["w_oa"], 0))
    dqa, dka, dva, land1, got2 = attn_bwd(qa, ka, va, doa, oa, lse_a, n_b, s_len, tq_b, "attn_a_bwd",
                                          comm=_join_comms(scatter_chips_comm(part1), scatter_sibling_comm(gpack2)))
    gshard1 = sum_chips(part1, land1, chip)
    part2 = add_pairs(gpack2, got2, core)
    dqb, dkb, dvb, land2 = attn_bwd(qb, kb, vb, dob, ob, lse_b, n_b, s_len, tq_b, "attn_b_bwd", comm=scatter_chips_comm(part2))
    gshard2 = sum_chips(part2, land2, chip)
    dz, dqap, dkva, st_prep = prep_bwd(dqa, dka, dva, dqb, dkb, dvb, zs, dz, tabs, g_qa, g_kva, g_qn_p, g_kn_p,
                                       w_qb_t, w_kvb_t, tm, s_len)

    gkv = matmul_tn(dkva, ckv, "gw_kvb")
    g3 = dict(
        w_in=_collapse_w_in(matmul_tn(dz, h, "gw_in")),
        w_qb=_unpad_heads(matmul_tn(dqap, cq, "gw_qb"), H_A, QK_NOPE + QK_ROPE, 0),
        w_kvb=jnp.stack([_unpad_heads(gkv[:H_A * HP], H_A, 64, 0).reshape(H_A, 64, KV_LORA),
                         _unpad_heads(gkv[H_A * HP:], H_A, 64, 0).reshape(H_A, 64, KV_LORA)], axis=1))
    gpack3 = packed(g3, PACK_G3)
    part3 = add_pairs(gpack3, exchange_sibling(gpack3), core)
    grad_x, st_mix, land3 = in_bwd(dz, xf, dx1, g_mix, w_in_t, tm, comm=scatter_chips_comm(part3))
    gshard3 = sum_chips(part3, land3, chip)
    off_g3, _ = _pack_offsets(PACK_G3)
    shards = {n: (gshard1, off_g1[n]) for n in PACK_G1}
    shards.update({n: (gshard2, off_g2[n]) for n in PACK_G2})
    shards.update({n: (gshard3, off_g3[n]) for n in PACK_G3})

    stats = allreduce_stats(st_mix, st_prep, st_mlp, st_ple)
    loss = jnp.sum(stats[ST_LOSS])

    out_g, out_d, out_m, out_v = {}, {}, {}, {}
    for name, (w, m, v) in mats.items():
        gshard, off = shards[name]
        r, c = w.shape[1:]
        if name in col_sharded:
            g2 = gshard[off:off + (r * c) // D_MODEL].reshape(c, r)
            if r % 128 == 0 and c % 128 == 0:
                res = adamw(w, g2, m, v, "adamw_" + name, g_transposed=True)
            else:
                res = adamw(w[0].T[None], g2, m[0].T[None], v[0].T[None], "adamw_" + name)
                res = tuple(a[0].T[None] for a in res)
        else:
            res = adamw(w, gshard[off:off + r], m, v, "adamw_" + name)
        out_g[name], out_d[name], out_m[name], out_v[name] = res

    gains = (("g_mix", g_mix, m_g_mix, v_g_mix, ST_G_MIX), ("g_qa", g_qa, m_g_qa, v_g_qa, ST_G_QA),
             ("g_kva", g_kva, m_g_kva, v_g_kva, ST_G_KVA), ("g_qn", g_qn, m_g_qn, v_g_qn, ST_G_QN),
             ("g_kn", g_kn, m_g_kn, v_g_kn, ST_G_KN), ("g_mlp", g_mlp, m_g_mlp, v_g_mlp, ST_G_MLP),
             ("g_ple", g_ple, m_g_ple, v_g_ple, ST_G_PLE), ("g_final", g_final, m_g_final, v_g_final, ST_G_FINAL))
    res = adamw_gains(stats, [(r_, w.reshape(1, -1), m.reshape(1, -1), v.reshape(1, -1)) for _, w, m, v, r_ in gains])
    for (name, w, _, _, _), (gg, gd, gm, gv) in zip(gains, res):
        out_g[name], out_d[name], out_m[name], out_v[name] = (a.reshape(w.shape) for a in (gg, gd, gm, gv))

    order = ("g_mix", "w_in", "g_qa", "w_qb", "g_kva", "w_kvb", "g_qn", "g_kn", "w_oa", "w_ob", "w_o", "g_mlp",
             "w_up", "w_down", "g_ple", "w_ple_gate", "w_ple", "g_final")
    return (loss, grad_x.reshape(x.shape), *[out_g[n] for n in order], *[out_d[n] for n in order],
            *[out_m[n] for n in order], *[out_v[n] for n in order])
```

```python
import numpy as np
import jax
import jax.numpy as jnp
from jax import lax
from jax.experimental import pallas as pl
from jax.experimental.pallas import tpu as pltpu

F32 = jnp.float32
BF16 = jnp.bfloat16

D_MODEL = 1024
EPS = 1e-6
ROPE_THETA = 10000.0
GRID_W = 64
H_A = 8
QK_NOPE = 64
QK_ROPE = 32
V_DIM_A = 64
Q_LORA = 256
KV_LORA = 128
H_B = 8
KV_B = 2
HD_B = 64
D_FF = 4 * D_MODEL
PLE_DIM = 256
HP = 128
ZP = 4096
N_DEV = 8
N_CHIP = 4

ADAM_LR = 0.001
ADAM_B1 = 0.9
ADAM_B2 = 0.999
ADAM_EPS = 1e-08
ADAM_WD = 0.01
ADAM_STEP = 10

VMEM_LIMIT = 52 * 1024 * 1024

PACK_ROWS = dict(w_in=416, w_qb=32, w_kvb=16, w_oa=64, w_ob=64, w_o=128, w_up=512, w_down=512, w_ple_gate=128, w_ple=32)
PACK_W1 = ("w_in", "w_qb", "w_kvb")
PACK_W2 = ("w_o", "w_ple_gate", "w_oa", "w_ob", "w_ple")
PACK_W3 = ("w_up", "w_down")
PACK_G1 = ("w_up", "w_down", "w_ple_gate", "w_ple")
PACK_G2 = ("w_o", "w_oa", "w_ob")
PACK_G3 = ("w_in", "w_qb", "w_kvb")


def _pack_offsets(names):
    off, o = {}, 0
    for n in names:
        off[n] = o
        o += PACK_ROWS[n]
    return off, o

ST_G_MIX, ST_G_QA, ST_G_KVA, ST_G_QN, ST_G_KN, ST_G_MLP, ST_G_PLE, ST_G_FINAL, ST_LOSS = range(9)
ST_ROWS = 16


def _dot_nn(a, b):
    return lax.dot_general(a, b, (((1,), (0,)), ((), ())), preferred_element_type=F32)


def _dot_nt(a, b):
    return lax.dot_general(a, b, (((1,), (1,)), ((), ())), preferred_element_type=F32)


def _dot_tn(a, b):
    return lax.dot_general(a, b, (((0,), (0,)), ((), ())), preferred_element_type=F32)


def _rstd(x, n):
    return lax.rsqrt(jnp.sum(x * x, axis=-1, keepdims=True) * (1.0 / n) + EPS)


def _rms_bwd(dy, xh, r, g, n):
    dxh = dy * g
    return r * (dxh - xh * (jnp.sum(dxh * xh, axis=-1, keepdims=True) * (1.0 / n)))


def _rope_fwd(x, c, s1, s2):
    return x * c + pltpu.roll(x, HP - 16, 1) * s1 + pltpu.roll(x, 16, 1) * s2


def _rope_bwd(d, c, s1, s2):
    return d * c + pltpu.roll(d * s1, 16, 1) + pltpu.roll(d * s2, HP - 16, 1)


def _colsum(v):
    return jnp.sum(v, axis=0, keepdims=True)


def _params(sem=None, vmem=VMEM_LIMIT):
    return pltpu.CompilerParams(dimension_semantics=sem, vmem_limit_bytes=vmem)


def _resident(shape):
    nd = len(shape)
    return pl.BlockSpec(shape, lambda *_: (0,) * nd, pipeline_mode=pl.Buffered(1))


def _rows(tm, width, col=0):
    return pl.BlockSpec((tm, width), lambda i: (i, col))


def _packed_weight(rows, off):
    return pl.BlockSpec((N_DEV, rows, D_MODEL), lambda *_: (0, off // rows, 0), pipeline_mode=pl.Buffered(1))


def _wrows(ref, start, size):
    rows = ref.shape[1]
    return ref[start // rows:(start + size) // rows].reshape(size, D_MODEL)


def _mesh_pos():
    return lax.axis_index("x"), lax.axis_index("y"), lax.axis_index("c")


def _flip(v, bit):
    return (1 - v) if bit else v


_ANY = pl.BlockSpec(memory_space=pl.ANY)
_MESH = pl.DeviceIdType.MESH


def _remote(src, dst, send_sems, recv_sems, k, to):
    return pltpu.make_async_remote_copy(src_ref=src, dst_ref=dst, send_sem=send_sems.at[k], recv_sem=recv_sems.at[k],
                                        device_id=to, device_id_type=_MESH)


def _sibling_copies(g_ref, got_ref, send_sems, recv_sems):
    x, y, c = _mesh_pos()
    return [_remote(g_ref.at[2 * j + (1 - c)], got_ref.at[j], send_sems, recv_sems, j, (x, y, 1 - c)) for j in range(N_CHIP)]


def _chip_copies(p_ref, land_ref, send_sems, recv_sems):
    x, y, c = _mesh_pos()
    copies = []
    for k in (1, 2, 3):
        tx, ty = _flip(x, k & 2), _flip(y, k & 1)
        copies.append(_remote(p_ref.at[2 * tx + ty], land_ref.at[k - 1], send_sems, recv_sems, k - 1, (tx, ty, c)))
    return copies


class _Comm:
    def __init__(self, ins, out_shapes, sems, make, aliases=None, make_tail=None):
        self.ins, self.out_shapes, self.sems, self.make, self.aliases = list(ins), list(out_shapes), list(sems), make, aliases or {}
        self.make_tail = make_tail


def _join_comms(a, b):
    assert a.make_tail is None and b.make_tail is None
    n_i, n_o, n_s = len(a.ins), len(a.out_shapes), len(a.sems)

    def make(cin, cout, sems):
        return a.make(cin[:n_i], cout[:n_o], sems[:n_s]) + b.make(cin[n_i:], cout[n_o:], sems[n_s:])

    aliases = dict(a.aliases)
    aliases.update({n_i + j: n_o + k for j, k in b.aliases.items()})
    return _Comm(a.ins + b.ins, a.out_shapes + b.out_shapes, a.sems + b.sems, make, aliases)


def _comm_parts(comm, n_in, n_out):
    if comm is None:
        return [], [], [], [], {}
    alias = {n_in + j: n_out + k for j, k in comm.aliases.items()}
    return comm.ins, [_ANY] * len(comm.ins), comm.out_shapes, comm.sems, alias


def _split_refs(refs, n_in, n_out, n_scratch, comm):
    n_ci = len(comm.ins) if comm else 0
    n_co = len(comm.out_shapes) if comm else 0
    cuts, i = [], 0
    for n in (n_in, n_ci, n_out, n_co, n_scratch):
        cuts.append(refs[i:i + n])
        i += n
    return (*cuts, refs[i:])


def _grid_edge(grid, last):
    cond = None
    for d, n in enumerate(grid):
        here = pl.program_id(d) == (n - 1 if last else 0)
        cond = here if cond is None else cond & here
    return cond


def _comm_start(comm, cin, cout, csem, grid):
    if comm is not None:
        @pl.when(_grid_edge(grid, False))
        def _():
            for cp in comm.make(cin, cout, csem):
                cp.start()


def _comm_finish(comm, cin, cout, csem, grid):
    if comm is not None:
        @pl.when(_grid_edge(grid, True))
        def _():
            for cp in comm.make(cin, cout, csem):
                cp.wait()
            if comm.make_tail is not None:
                tail = comm.make_tail(cin, cout, csem)
                for cp in tail:
                    cp.start()
                for cp in tail:
                    cp.wait()


def gather_first_comm(shard):
    r, w = shard.shape

    def make(cin, cout, sems):
        (x_ref,), (out_ref,), (send_sems, recv_sems, local_sem) = cin, cout, sems
        x, y, c = _mesh_pos()
        mine = out_ref.at[4 * x + 2 * y + c]
        targets = [(x, y, 1 - c), (1 - x, y, c), (x, 1 - y, c), (1 - x, 1 - y, c)]
        return [_remote(x_ref, mine, send_sems, recv_sems, k, to) for k, to in enumerate(targets)] + [
            pltpu.make_async_copy(x_ref, mine, local_sem)]

    return _Comm([shard], [jax.ShapeDtypeStruct((N_DEV, r, w), shard.dtype)],
                 [pltpu.SemaphoreType.DMA((4,)), pltpu.SemaphoreType.DMA((4,)), pltpu.SemaphoreType.DMA], make)


def gather_pass_comm(full):
    def make(cin, cout, sems):
        (in_ref,), (out_ref,), (send_sems, recv_sems) = cin, cout, sems
        x, y, c = _mesh_pos()
        copies = []
        for k, (px, py) in enumerate([(1 - x, y), (x, 1 - y), (1 - x, 1 - y)]):
            idx = 4 * px + 2 * py + c
            copies.append(_remote(in_ref.at[idx], out_ref.at[idx], send_sems, recv_sems, k, (x, y, 1 - c)))
        return copies

    return _Comm([full], [jax.ShapeDtypeStruct(full.shape, full.dtype)],
                 [pltpu.SemaphoreType.DMA((3,)), pltpu.SemaphoreType.DMA((3,))], make, aliases={0: 0})


def gather3_first_comm(shard):
    r, w = shard.shape

    def make(cin, cout, sems):
        (x_ref,), (out_ref,), (send_sems, recv_sems, local_sem) = cin, cout, sems
        x, y, c = _mesh_pos()
        mine = out_ref.at[4 * x + 2 * y + c]
        targets = [(x, y, 1 - c), (1 - x, y, c), (x, 1 - y, c)]
        return [_remote(x_ref, mine, send_sems, recv_sems, k, to) for k, to in enumerate(targets)] + [
            pltpu.make_async_copy(x_ref, mine, local_sem)]

    return _Comm([shard], [jax.ShapeDtypeStruct((N_DEV, r, w), shard.dtype)],
                 [pltpu.SemaphoreType.DMA((3,)), pltpu.SemaphoreType.DMA((3,)), pltpu.SemaphoreType.DMA], make)


def gather3_second_comm(full, then_third=False):
    def make(cin, cout, sems):
        (in_ref,), (out_ref,), (send_sems, recv_sems) = cin, cout, sems
        x, y, c = _mesh_pos()
        copies = []
        for k, (px, py) in enumerate([(1 - x, y), (x, 1 - y)]):
            idx = 4 * px + 2 * py + c
            copies.append(_remote(in_ref.at[idx], out_ref.at[idx], send_sems, recv_sems, k, (x, y, 1 - c)))
        fx, fy = x * c + (1 - x) * (1 - c), y * (1 - c) + (1 - y) * c
        tx, ty = x * (1 - c) + (1 - x) * c, (1 - y) * (1 - c) + y * c
        idx = 4 * fx + 2 * fy + c
        copies.append(_remote(in_ref.at[idx], out_ref.at[idx], send_sems, recv_sems, 2, (tx, ty, c)))
        return copies

    def third(cin, cout, sems):
        (out_ref,), (send_sems, recv_sems) = cout, sems
        x, y, c = _mesh_pos()
        idx = 4 * (1 - x) + 2 * (1 - y) + c
        return [_remote(out_ref.at[idx], out_ref.at[idx], send_sems, recv_sems, 3, (x, y, 1 - c))]

    n_sem = 4 if then_third else 3
    return _Comm([full], [jax.ShapeDtypeStruct(full.shape, full.dtype)],
                 [pltpu.SemaphoreType.DMA((n_sem,)), pltpu.SemaphoreType.DMA((n_sem,))], make, aliases={0: 0},
                 make_tail=third if then_third else None)


def gather3_third_comm(full):
    def make(cin, cout, sems):
        (in_ref,), (out_ref,), (send_sems, recv_sems) = cin, cout, sems
        x, y, c = _mesh_pos()
        idx = 4 * (1 - x) + 2 * (1 - y) + c
        return [_remote(in_ref.at[idx], out_ref.at[idx], send_sems, recv_sems, 0, (x, y, 1 - c))]

    return _Comm([full], [jax.ShapeDtypeStruct(full.shape, full.dtype)],
                 [pltpu.SemaphoreType.DMA((1,)), pltpu.SemaphoreType.DMA((1,))], make, aliases={0: 0})


def scatter_sibling_comm(g):
    _, r, w = g.shape
    return _Comm([g], [jax.ShapeDtypeStruct((N_CHIP, r, w), g.dtype)],
                 [pltpu.SemaphoreType.DMA((N_CHIP,)), pltpu.SemaphoreType.DMA((N_CHIP,))],
                 lambda cin, cout, sems: _sibling_copies(cin[0], cout[0], sems[0], sems[1]))


def scatter_chips_comm(part):
    _, r, w = part.shape
    return _Comm([part], [jax.ShapeDtypeStruct((N_CHIP - 1, r, w), part.dtype)],
                 [pltpu.SemaphoreType.DMA((3,)), pltpu.SemaphoreType.DMA((3,))],
                 lambda cin, cout, sems: _chip_copies(cin[0], cout[0], sems[0], sems[1]))


def exchange_sibling(g):
    _, r, w = g.shape

    def body(g_ref, got_ref, send_sems, recv_sems):
        copies = _sibling_copies(g_ref, got_ref, send_sems, recv_sems)
        for cp in copies:
            cp.start()
        for cp in copies:
            cp.wait()

    return pl.pallas_call(
        body, name="exchange_sibling", out_shape=jax.ShapeDtypeStruct((N_CHIP, r, w), g.dtype),
        in_specs=[_ANY], out_specs=_ANY,
        scratch_shapes=[pltpu.SemaphoreType.DMA((N_CHIP,)), pltpu.SemaphoreType.DMA((N_CHIP,))],
    )(g)


def exchange_chips(part):
    _, r, w = part.shape

    def body(p_ref, land_ref, send_sems, recv_sems):
        copies = _chip_copies(p_ref, land_ref, send_sems, recv_sems)
        for cp in copies:
            cp.start()
        for cp in copies:
            cp.wait()

    return pl.pallas_call(
        body, name="exchange_chips", out_shape=jax.ShapeDtypeStruct((N_CHIP - 1, r, w), part.dtype),
        in_specs=[_ANY], out_specs=_ANY,
        scratch_shapes=[pltpu.SemaphoreType.DMA((3,)), pltpu.SemaphoreType.DMA((3,))],
    )(part)


def allreduce_stats(st_mix, st_prep, st_mlp, st_ple):
    def body(mix_ref, prep_ref, mlp_ref, ple_ref, out_ref, mine, gath, send_sems, recv_sems):
        x, y, c = _mesh_pos()
        me = 4 * x + 2 * y + c
        mine[...] = jnp.zeros_like(mine)
        mine[ST_G_MIX:ST_G_MIX + 1, :] = mix_ref[...]
        mine[ST_G_QA:ST_G_KN + 1, 0:256] = prep_ref[...]
        mine[ST_G_MLP:ST_G_MLP + 1, :] = mlp_ref[...]
        mine[ST_G_PLE:ST_LOSS + 1, :] = ple_ref[...]
        gath[me] = mine[...]
        copies = []
        for k in range(1, N_DEV):
            peer = (_flip(x, k & 4), _flip(y, k & 2), _flip(c, k & 1))
            copies.append(_remote(mine, gath.at[me], send_sems, recv_sems, k - 1, peer))
        for cp in copies:
            cp.start()
        for cp in copies:
            cp.wait()
        acc = gath[0]
        for d in range(1, N_DEV):
            acc = acc + gath[d]
        out_ref[...] = acc

    vm = pl.BlockSpec(memory_space=pltpu.VMEM)
    return pl.pallas_call(
        body, name="allreduce_stats", out_shape=jax.ShapeDtypeStruct((ST_ROWS, D_MODEL), F32),
        in_specs=[vm] * 4, out_specs=vm,
        scratch_shapes=[pltpu.VMEM((ST_ROWS, D_MODEL), F32), pltpu.VMEM((N_DEV, ST_ROWS, D_MODEL), F32),
                        pltpu.SemaphoreType.DMA((N_DEV - 1,)), pltpu.SemaphoreType.DMA((N_DEV - 1,))],
    )(st_mix, st_prep, st_mlp, st_ple)


def adamw_gains(stats, gains):
    c1 = 1.0 - ADAM_B1 ** ADAM_STEP
    c2 = 1.0 - ADAM_B2 ** ADAM_STEP
    n = len(gains)

    def body(st_ref, *refs):
        ins, outs = refs[:3 * n], refs[3 * n:]
        for i, (row, w, _, _) in enumerate(gains):
            width = w.shape[1]
            gv = st_ref[row:row + 1, 0:width]
            mn = ADAM_B1 * ins[3 * i + 1][...] + (1.0 - ADAM_B1) * gv
            vn = ADAM_B2 * ins[3 * i + 2][...] + (1.0 - ADAM_B2) * (gv * gv)
            outs[4 * i][...] = gv
            outs[4 * i + 1][...] = -ADAM_LR * ((mn / c1) / (jnp.sqrt(vn / c2) + ADAM_EPS) + ADAM_WD * ins[3 * i][...])
            outs[4 * i + 2][...] = mn
            outs[4 * i + 3][...] = vn

    vm = pl.BlockSpec(memory_space=pltpu.VMEM)
    flat = [a for (_, w, m, v) in gains for a in (w, m, v)]
    out_shape = tuple(jax.ShapeDtypeStruct(w.shape, F32) for (_, w, _, _) in gains for _ in range(4))
    res = pl.pallas_call(body, name="adamw_gains", out_shape=out_shape, in_specs=[vm] * (1 + 3 * n),
                         out_specs=tuple([vm] * (4 * n)))(stats, *flat)
    return [res[4 * i:4 * i + 4] for i in range(n)]


def _row_tile(r, cap=640):
    return max(d for d in range(16, min(r, cap) + 1, 16) if r % d == 0)


def add_pairs(g, got, core):
    n, r, w = got.shape
    tr = _row_tile(r)

    def body(c_ref, a_ref, b_ref, o_ref):
        o_ref[...] = (a_ref[...].astype(F32) + b_ref[...].astype(F32)).astype(o_ref.dtype)

    spec = pl.BlockSpec((1, tr, w), lambda i, j, c: (i, j, 0))
    return pl.pallas_call(
        body, name="add_pairs", out_shape=jax.ShapeDtypeStruct(got.shape, got.dtype),
        grid_spec=pltpu.PrefetchScalarGridSpec(
            num_scalar_prefetch=1, grid=(n, r // tr),
            in_specs=[pl.BlockSpec((1, tr, w), lambda i, j, c: (2 * i + c[0], j, 0)), spec], out_specs=spec),
        compiler_params=_params(("parallel", "parallel")),
    )(core, g, got)


def sum_chips(part, land, chip):
    _, r, w = part.shape
    tr = _row_tile(r)

    def body(c_ref, p_ref, l_ref, o_ref):
        acc = p_ref[0].astype(F32)
        for s in range(N_CHIP - 1):
            acc = acc + l_ref[s].astype(F32)
        o_ref[...] = acc

    return pl.pallas_call(
        body, name="sum_chips", out_shape=jax.ShapeDtypeStruct((r, w), F32),
        grid_spec=pltpu.PrefetchScalarGridSpec(
            num_scalar_prefetch=1, grid=(r // tr,),
            in_specs=[pl.BlockSpec((1, tr, w), lambda i, c: (c[0], i, 0)), pl.BlockSpec((N_CHIP - 1, tr, w), lambda i, c: (0, i, 0))],
            out_specs=pl.BlockSpec((tr, w), lambda i, c: (i, 0))),
        compiler_params=_params(("parallel",)),
    )(chip, part, land)


def norm_x(x, g_mix, tm, comm=None):
    t = x.shape[0]
    grid = (t // tm,)
    c_ins, c_in_specs, c_outs, c_sems, alias = _comm_parts(comm, 2, 1)

    def body(*refs):
        (x_ref, g_ref), cin, (h_ref,), cout, _, csem = _split_refs(refs, 2, 1, 0, comm)
        _comm_start(comm, cin, cout, csem, grid)
        xv = x_ref[...]
        h_ref[...] = (xv * _rstd(xv, D_MODEL) * g_ref[...]).astype(BF16)
        _comm_finish(comm, cin, cout, csem, grid)

    return pl.pallas_call(
        body, name="norm_x", grid=grid, out_shape=(jax.ShapeDtypeStruct((t, D_MODEL), BF16), *c_outs),
        in_specs=[_rows(tm, D_MODEL), _resident((1, D_MODEL)), *c_in_specs],
        out_specs=(_rows(tm, D_MODEL), *([_ANY] * len(c_outs))),
        scratch_shapes=c_sems, input_output_aliases=alias, compiler_params=_params(("arbitrary",)),
    )(x, g_mix, *c_ins)


def pack_late_weights(w_up, w_down, w_o, w_pg, small, comm=None):
    rows2 = sum(PACK_ROWS[n] for n in PACK_W2)
    rows3 = sum(PACK_ROWS[n] for n in PACK_W3)
    c_ins, c_in_specs, c_outs, c_sems, alias = _comm_parts(comm, 5, 2)
    grid = (1,)

    def body(*refs):
        (up_ref, dn_ref, o_ref, pg_ref, sm_ref), cin, (p2_ref, p3_ref), cout, _, csem = _split_refs(refs, 5, 2, 0, comm)
        _comm_start(comm, cin, cout, csem, grid)
        p2_ref[0:128, :] = o_ref[0].astype(BF16)
        p2_ref[128:256, :] = pg_ref[0].astype(BF16)
        p2_ref[256:rows2, :] = sm_ref[...]
        p3_ref[0:512, :] = up_ref[0].T.astype(BF16)
        p3_ref[512:1024, :] = dn_ref[0].astype(BF16)
        _comm_finish(comm, cin, cout, csem, grid)

    def whole(a):
        nd = a.ndim
        return pl.BlockSpec(a.shape, lambda i: (0,) * nd)

    args = (w_up, w_down, w_o, w_pg, small)
    return pl.pallas_call(
        body, name="pack_late_weights", grid=grid,
        out_shape=(jax.ShapeDtypeStruct((rows2, D_MODEL), BF16), jax.ShapeDtypeStruct((rows3, D_MODEL), BF16), *c_outs),
        in_specs=[*[whole(a) for a in args], *c_in_specs],
        out_specs=(pl.BlockSpec((rows2, D_MODEL), lambda i: (0, 0)), pl.BlockSpec((rows3, D_MODEL), lambda i: (0, 0)),
                   *([_ANY] * len(c_outs))),
        scratch_shapes=c_sems, input_output_aliases=alias, compiler_params=_params(("arbitrary",)),
    )(*args, *c_ins)


def in_proj(h, w_in_t, tm, comm=None):
    t = h.shape[0]
    nc = 512
    half = ZP // 2
    grid = (t // tm,)
    c_ins, c_in_specs, c_outs, c_sems, alias = _comm_parts(comm, 2, 2)

    def body(*refs):
        (h_ref, w_ref), cin, (zg_ref, zs_ref), cout, _, csem = _split_refs(refs, 2, 2, 0, comm)
        _comm_start(comm, cin, cout, csem, grid)
        hv = h_ref[...]
        for cidx in range(half // nc):
            zg_ref[:, cidx * nc:(cidx + 1) * nc] = _dot_nt(hv, w_ref[cidx * nc:(cidx + 1) * nc, :]).astype(BF16)
        for cidx in range(half // nc):
            zs_ref[:, cidx * nc:(cidx + 1) * nc] = _dot_nt(hv, w_ref[half + cidx * nc:half + (cidx + 1) * nc, :]).astype(BF16)
        _comm_finish(comm, cin, cout, csem, grid)

    return pl.pallas_call(
        body, name="in_proj", grid=grid,
        out_shape=(jax.ShapeDtypeStruct((t, half), BF16), jax.ShapeDtypeStruct((t, half), BF16), *c_outs),
        in_specs=[_rows(tm, D_MODEL), _resident((ZP, D_MODEL)), *c_in_specs],
        out_specs=(_rows(tm, half), _rows(tm, half), *([_ANY] * len(c_outs))),
        scratch_shapes=c_sems, input_output_aliases=alias, compiler_params=_params(("arbitrary",)),
    )(h, w_in_t, *c_ins)


def attn_prep(zp, tabs, g_qa, g_kva, g_qn, g_kn, w_qb_t, w_kvb_t, tm, s_len, comm=None):
    t = zp.shape[0]
    nsb = s_len // tm
    scale_a = (QK_NOPE + QK_ROPE) ** -0.5
    scale_b = HD_B ** -0.5

    grid = (t // tm,)
    c_ins, c_in_specs, c_outs, c_sems, alias = _comm_parts(comm, 13, 8)

    def body(*refs):
        ((qb_ref, qlat_ref, kb_ref, vb_ref, ckv_ref, kpe_ref, tab_ref, gqa_ref, gkva_ref, gqn_ref, gkn_ref, wqb_ref,
          wkvb_ref), cin, (qa_o, ka_o, va_o, qb_o, kb_o, vb_o, cq_o, ckvn_o), cout, _, csem) = _split_refs(refs, 13, 8, 0, comm)
        _comm_start(comm, cin, cout, csem, grid)
        ca, s1a, s2a = tab_ref[0], tab_ref[1], tab_ref[2]
        ck = tab_ref[3]
        cb, s1b, s2b = tab_ref[4], tab_ref[5], tab_ref[6]
        ql = qlat_ref[...].astype(F32)
        cq = (ql * _rstd(ql, Q_LORA) * gqa_ref[...]).astype(BF16)
        cq_o[...] = cq
        qa = _dot_nt(cq, wqb_ref[...])
        slabs = [slice(h * HP, (h + 1) * HP) for h in range(H_A)]
        qa_o[...] = jnp.concatenate(
            [(_rope_fwd(qa[:, sl], ca, s1a, s2a) * scale_a).astype(BF16) for sl in slabs], axis=1)
        cr = ckv_ref[...].astype(F32)
        ckv = (cr * _rstd(cr, KV_LORA) * gkva_ref[...]).astype(BF16)
        ckvn_o[...] = ckv
        kva = _dot_nt(ckv, wkvb_ref[...])
        kpe = _rope_fwd(kpe_ref[...].astype(F32), ck, s1a, s2a)
        ka_o[...] = jnp.concatenate([(kva[:, sl] + kpe).astype(BF16) for sl in slabs], axis=1)
        va_o[...] = kva[:, H_A * HP:].astype(BF16)
        gqn, gkn = gqn_ref[...], gkn_ref[...]

        def norm_rope(ref, sl, g, scale):
            xs = ref[:, sl].astype(F32)
            y = _rope_fwd(xs * _rstd(xs, HD_B) * g, cb, s1b, s2b)
            return (y if scale is None else y * scale).astype(BF16)

        qb_o[...] = jnp.concatenate([norm_rope(qb_ref, sl, gqn, scale_b) for sl in slabs], axis=1)
        kb_o[...] = jnp.concatenate([norm_rope(kb_ref, sl, gkn, None) for sl in slabs[:KV_B]], axis=1)
        vb_o[...] = vb_ref[...].astype(BF16)
        _comm_finish(comm, cin, cout, csem, grid)

    def o(width):
        return jax.ShapeDtypeStruct((t, width), BF16)

    return pl.pallas_call(
        body, name="attn_prep", grid=grid,
        out_shape=(o(H_A * HP), o(H_A * HP), o(H_A * HP), o(H_B * HP), o(KV_B * HP), o(KV_B * HP), o(Q_LORA), o(KV_LORA),
                   *c_outs),
        in_specs=[_rows(tm, 1024, 0), _rows(tm, 256, 4), _rows(tm, 256, 5), _rows(tm, 256, 6),
                  _rows(tm, 128, 14), _rows(tm, 128, 15),
                  pl.BlockSpec((7, tm, HP), lambda i: (0, i % nsb, 0)),
                  _resident((1, Q_LORA)), _resident((1, KV_LORA)), _resident((1, HP)), _resident((1, HP)),
                  _resident((H_A * HP, Q_LORA)), _resident((2 * H_A * HP, KV_LORA)), *c_in_specs],
        out_specs=(_rows(tm, H_A * HP), _rows(tm, H_A * HP), _rows(tm, H_A * HP), _rows(tm, H_B * HP),
                   _rows(tm, KV_B * HP), _rows(tm, KV_B * HP), _rows(tm, Q_LORA), _rows(tm, KV_LORA), *([_ANY] * len(c_outs))),
        scratch_shapes=c_sems, input_output_aliases=alias, compiler_params=_params(("arbitrary",)),
    )(zp, zp, zp, zp, zp, zp, tabs, g_qa, g_kva, g_qn, g_kn, w_qb_t, w_kvb_t, *c_ins)


def attn_fwd(q, k, v, n_b, s_len, tq, name, comm=None):
    t = q.shape[0]
    n_h, n_hk = q.shape[1] // HP, k.shape[1] // HP
    grp = n_h // n_hk
    nq = s_len // tq
    sub = min(tq, 256)
    grid = (n_b, n_h, nq)
    c_ins, c_in_specs, c_outs, c_sems, alias = _comm_parts(comm, 3, 2)

    def body(*refs):
        (q_ref, k_ref, v_ref), cin, (o_ref, lse_ref), cout, _, csem = _split_refs(refs, 3, 2, 0, comm)
        _comm_start(comm, cin, cout, csem, grid)
        kv, vv = k_ref[...], v_ref[...]
        for r in range(tq // sub):
            rows = slice(r * sub, (r + 1) * sub)
            s = _dot_nt(q_ref[rows, :], kv)
            m = jnp.max(s, axis=-1, keepdims=True)
            p = jnp.exp(s - m)
            l = jnp.sum(p, axis=-1, keepdims=True)
            o_ref[rows, :] = (_dot_nn(p.astype(BF16), vv) * (1.0 / l)).astype(o_ref.dtype)
            lse_ref[rows, :] = jnp.broadcast_to(m + jnp.log(l), (sub, HP))
        _comm_finish(comm, cin, cout, csem, grid)

    qspec = pl.BlockSpec((tq, HP), lambda b, h, i: (b * nq + i, h))
    kspec = pl.BlockSpec((s_len, HP), lambda b, h, i: (b, h // grp))
    return pl.pallas_call(
        body, name=name, grid=grid,
        out_shape=(jax.ShapeDtypeStruct((t, n_h * HP), BF16), jax.ShapeDtypeStruct((t, n_h * HP), F32), *c_outs),
        in_specs=[qspec, kspec, kspec, *c_in_specs], out_specs=(qspec, qspec, *([_ANY] * len(c_outs))),
        scratch_shapes=c_sems, input_output_aliases=alias,
        compiler_params=_params(("arbitrary", "arbitrary", "arbitrary")),
    )(q, k, v, *c_ins)


def merge_fwd(oa, ob, zp, x, w_oa_t, w_ob_t, wpack, off, tm):
    t = x.shape[0]

    def body(oa_ref, ob_ref, ga_ref, gb_ref, x_ref, woa_ref, wob_ref, wo_ref, x1_o, mg_o, y_o):
        ya = _dot_nt(oa_ref[...], woa_ref[...])
        yb = _dot_nt(ob_ref[...], wob_ref[...])
        y_o[:, 0:D_MODEL] = ya.astype(BF16)
        y_o[:, D_MODEL:2 * D_MODEL] = yb.astype(BF16)
        merged = (jax.nn.sigmoid(ga_ref[...].astype(F32)) * ya + jax.nn.sigmoid(gb_ref[...].astype(F32)) * yb).astype(BF16)
        mg_o[...] = merged
        x1_o[...] = x_ref[...] + _dot_nn(merged, _wrows(wo_ref, 0, D_MODEL))

    return pl.pallas_call(
        body, name="merge_fwd", grid=(t // tm,),
        out_shape=(jax.ShapeDtypeStruct((t, D_MODEL), F32), jax.ShapeDtypeStruct((t, D_MODEL), BF16),
                   jax.ShapeDtypeStruct((t, 2 * D_MODEL), BF16)),
        in_specs=[_rows(tm, H_A * HP), _rows(tm, H_B * HP), _rows(tm, 1024, 0), _rows(tm, 1024, 1), _rows(tm, D_MODEL),
                  _resident((D_MODEL, H_A * HP)), _resident((D_MODEL, H_B * HP)), _packed_weight(128, off["w_o"])],
        out_specs=(_rows(tm, D_MODEL), _rows(tm, D_MODEL), _rows(tm, 2 * D_MODEL)), compiler_params=_params(("parallel",)),
    )(oa, ob, zp, zp, x, w_oa_t, w_ob_t, wpack)


def mlp_fwd(x1, g_mlp, wpack, off, tm):
    t = x1.shape[0]
    fc = 1024

    def body(x_ref, g_ref, wup_ref, wdn_ref, x2_o, u_o):
        xv = x_ref[...]
        h2 = (xv * _rstd(xv, D_MODEL) * g_ref[...]).astype(BF16)
        acc = xv
        for cidx in range(D_FF // fc):
            sl = slice(cidx * fc, (cidx + 1) * fc)
            u = jnp.maximum(_dot_nt(h2, _wrows(wup_ref, cidx * fc, fc)), 0.0)
            u_o[:, sl] = u.astype(BF16)
            acc = acc + _dot_nn((u * u).astype(BF16), _wrows(wdn_ref, cidx * fc, fc))
        x2_o[...] = acc

    return pl.pallas_call(
        body, name="mlp_fwd", grid=(t // tm,),
        out_shape=(jax.ShapeDtypeStruct((t, D_MODEL), F32), jax.ShapeDtypeStruct((t, D_FF), BF16)),
        in_specs=[_rows(tm, D_MODEL), _resident((1, D_MODEL)), _packed_weight(512, off["w_up"]), _packed_weight(512, off["w_down"])],
        out_specs=(_rows(tm, D_MODEL), _rows(tm, D_FF)), compiler_params=_params(("parallel",)),
    )(x1, g_mlp, wpack, wpack)


def ple_loss_bwd(x2, p, tgt, g_ple, g_final, wpack, off, w_ple_t, tm):
    t = x2.shape[0]
    inv_d = 1.0 / D_MODEL

    def body(x2_ref, p_ref, tg_ref, gp_ref, gf_ref, wpg_ref, wple_ref, dx2_o, dt_o, h3_o, dpe_o, st_o, dx2b_o):
        @pl.when(pl.program_id(0) == 0)
        def _():
            st_o[...] = jnp.zeros_like(st_o)

        x2v = x2_ref[...]
        gp, gf = gp_ref[...], gf_ref[...]
        w_pg = _wrows(wpg_ref, 0, D_MODEL)
        r2 = _rstd(x2v, D_MODEL)
        xh2 = x2v * r2
        h3 = (xh2 * gp).astype(BF16)
        h3_o[...] = h3
        gate = jax.nn.sigmoid(_dot_nn(h3, w_pg))
        pe = _dot_nt(p_ref[...].astype(BF16), wple_ref[...])
        x3 = x2v + gate * pe
        r3 = _rstd(x3, D_MODEL)
        xh3 = x3 * r3
        err = xh3 * gf - tg_ref[...]
        dx3 = _rms_bwd(err, xh3, r3, gf * inv_d, D_MODEL)
        dpe = dx3 * gate
        dpe_o[...] = dpe.astype(BF16)
        dt = (dpe * pe * (1.0 - gate)).astype(BF16)
        dt_o[...] = dt
        dh3 = _dot_nt(dt, w_pg)
        dx2 = dx3 + _rms_bwd(dh3, xh2, r2, gp, D_MODEL)
        dx2_o[...] = dx2
        dx2b_o[...] = dx2.astype(BF16)
        st_o[0:1, :] += _colsum(dh3 * xh2)
        st_o[1:2, :] += _colsum(err * xh3) * inv_d
        st_o[2:3, :] += _colsum(err * err) * (0.5 * inv_d)

    bf = jax.ShapeDtypeStruct((t, D_MODEL), BF16)
    return pl.pallas_call(
        body, name="ple_loss_bwd", grid=(t // tm,),
        out_shape=(jax.ShapeDtypeStruct((t, D_MODEL), F32), bf, bf, bf, jax.ShapeDtypeStruct((3, D_MODEL), F32), bf),
        in_specs=[_rows(tm, D_MODEL), _rows(tm, PLE_DIM), _rows(tm, D_MODEL), _resident((1, D_MODEL)), _resident((1, D_MODEL)),
                  _packed_weight(128, off["w_ple_gate"]), _resident((D_MODEL, PLE_DIM))],
        out_specs=(_rows(tm, D_MODEL), _rows(tm, D_MODEL), _rows(tm, D_MODEL), _rows(tm, D_MODEL),
                   pl.BlockSpec((3, D_MODEL), lambda i: (0, 0)), _rows(tm, D_MODEL)),
        compiler_params=_params(("arbitrary",)),
    )(x2, p, tgt, g_ple, g_final, wpack, w_ple_t)


def mlp_bwd(dx2, x1, u, g_mlp, wpack, off, tm):
    t = x1.shape[0]
    fc = 1024

    def body(dx2_ref, x1_ref, u_ref, g_ref, wup_ref, wdn_ref, dx1_o, da_o, h2_o, st_o, dx1b_o):
        @pl.when(pl.program_id(0) == 0)
        def _():
            st_o[...] = jnp.zeros_like(st_o)

        d2 = dx2_ref[...]
        d2b = d2.astype(BF16)
        dh2 = jnp.zeros((tm, D_MODEL), F32)
        for cidx in range(D_FF // fc):
            sl = slice(cidx * fc, (cidx + 1) * fc)
            da = (_dot_nt(d2b, _wrows(wdn_ref, cidx * fc, fc)) * (2.0 * u_ref[:, sl].astype(F32))).astype(BF16)
            da_o[:, sl] = da
            dh2 = dh2 + _dot_nn(da, _wrows(wup_ref, cidx * fc, fc))
        xv = x1_ref[...]
        g = g_ref[...]
        r1 = _rstd(xv, D_MODEL)
        xh1 = xv * r1
        h2_o[...] = (xh1 * g).astype(BF16)
        st_o[...] += _colsum(dh2 * xh1)
        dx1 = d2 + _rms_bwd(dh2, xh1, r1, g, D_MODEL)
        dx1_o[...] = dx1
        dx1b_o[...] = dx1.astype(BF16)

    return pl.pallas_call(
        body, name="mlp_bwd", grid=(t // tm,),
        out_shape=(jax.ShapeDtypeStruct((t, D_MODEL), F32), jax.ShapeDtypeStruct((t, D_FF), BF16),
                   jax.ShapeDtypeStruct((t, D_MODEL), BF16), jax.ShapeDtypeStruct((1, D_MODEL), F32),
                   jax.ShapeDtypeStruct((t, D_MODEL), BF16)),
        in_specs=[_rows(tm, D_MODEL), _rows(tm, D_MODEL), _rows(tm, D_FF), _resident((1, D_MODEL)),
                  _packed_weight(512, off["w_up"]), _packed_weight(512, off["w_down"])],
        out_specs=(_rows(tm, D_MODEL), _rows(tm, D_FF), _rows(tm, D_MODEL), pl.BlockSpec((1, D_MODEL), lambda i: (0, 0)),
                   _rows(tm, D_MODEL)),
        compiler_params=_params(("arbitrary",)),
    )(dx2, x1, u, g_mlp, wpack, wpack)


def merge_bwd(dx1b, yab, zp, w_oa_t, w_ob_t, wpack, off, tm, comm=None):
    t = dx1b.shape[0]
    grid = (t // tm,)
    c_ins, c_in_specs, c_outs, c_sems, alias = _comm_parts(comm, 7, 5)

    def body(*refs):
        ((dx1_ref, y_ref, ga_ref, gb_ref, woa_ref, wob_ref, wo_ref), cin,
         (doa_o, dob_o, dg_o, dya_o, dyb_o), cout, _, csem) = _split_refs(refs, 7, 5, 0, comm)
        _comm_start(comm, cin, cout, csem, grid)
        dm = _dot_nt(dx1_ref[...], _wrows(wo_ref, 0, D_MODEL))
        for g_ref, w_ref, do_o, dy_o, col in ((ga_ref, woa_ref, doa_o, dya_o, 0), (gb_ref, wob_ref, dob_o, dyb_o, 1)):
            cols = slice(col * D_MODEL, (col + 1) * D_MODEL)
            sg = jax.nn.sigmoid(g_ref[...].astype(F32))
            dyv = (dm * sg).astype(BF16)
            dy_o[...] = dyv
            dg_o[:, cols] = (dm * y_ref[:, cols].astype(F32) * sg * (1.0 - sg)).astype(BF16)
            do_o[...] = _dot_nn(dyv, w_ref[...]).astype(BF16)
        _comm_finish(comm, cin, cout, csem, grid)

    bf = jax.ShapeDtypeStruct((t, D_MODEL), BF16)
    return pl.pallas_call(
        body, name="merge_bwd", grid=grid,
        out_shape=(bf, bf, jax.ShapeDtypeStruct((t, ZP), BF16), bf, bf, *c_outs),
        in_specs=[_rows(tm, D_MODEL), _rows(tm, 2 * D_MODEL), _rows(tm, 1024, 0), _rows(tm, 1024, 1),
                  _resident((D_MODEL, H_A * HP)), _resident((D_MODEL, H_B * HP)), _packed_weight(128, off["w_o"]), *c_in_specs],
        out_specs=(_rows(tm, D_MODEL), _rows(tm, D_MODEL), _rows(tm, 2 * D_MODEL), _rows(tm, D_MODEL), _rows(tm, D_MODEL),
                   *([_ANY] * len(c_outs))),
        scratch_shapes=c_sems, input_output_aliases=alias,
        compiler_params=_params(("arbitrary",)),
    )(dx1b, yab, zp, zp, w_oa_t, w_ob_t, wpack, *c_ins)


def attn_bwd(q, k, v, do, o, lse, n_b, s_len, tq, name, comm=None):
    t = q.shape[0]
    n_h, n_hk = q.shape[1] // HP, k.shape[1] // HP
    grp = n_h // n_hk
    nq = s_len // tq
    sub = min(tq, 256)
    grid = (n_b, n_hk, grp, nq)
    c_ins, c_in_specs, c_outs, c_sems, alias = _comm_parts(comm, 6, 3)

    def body(*refs):
        ((q_ref, k_ref, v_ref, do_ref, o_ref, lse_ref), cin, (dq_o, dk_o, dv_o), cout, (p_s, ds_s, dk_acc, dv_acc),
         csem) = _split_refs(refs, 6, 3, 4, comm)
        _comm_start(comm, cin, cout, csem, grid)

        @pl.when((pl.program_id(2) == 0) & (pl.program_id(3) == 0))
        def _():
            dk_acc[...] = jnp.zeros_like(dk_acc)
            dv_acc[...] = jnp.zeros_like(dv_acc)

        kv, vv = k_ref[...], v_ref[...]
        for r in range(tq // sub):
            rows = slice(r * sub, (r + 1) * sub)
            qv, dov = q_ref[rows, :], do_ref[rows, :]
            delta = jnp.sum(dov.astype(F32) * o_ref[rows, :].astype(F32), axis=-1, keepdims=True)
            delta_row = jnp.broadcast_to(delta, (sub, HP)).T[0:1, :]
            lse_row = lse_ref[rows, :].T[0:1, :]
            pt = jnp.exp(_dot_nt(kv, qv) - lse_row)
            dst = (pt * (_dot_nt(vv, dov) - delta_row)).astype(BF16)
            p_s[:, rows] = pt.astype(BF16)
            ds_s[:, rows] = dst
            dq_o[rows, :] = _dot_tn(dst, kv).astype(dq_o.dtype)
        dk_acc[...] += _dot_nn(ds_s[...], q_ref[...])
        dv_acc[...] += _dot_nn(p_s[...], do_ref[...])

        @pl.when((pl.program_id(2) == grp - 1) & (pl.program_id(3) == nq - 1))
        def _():
            dk_o[...] = dk_acc[...].astype(dk_o.dtype)
            dv_o[...] = dv_acc[...].astype(dv_o.dtype)

        _comm_finish(comm, cin, cout, csem, grid)

    qspec = pl.BlockSpec((tq, HP), lambda b, hk, g, i: (b * nq + i, hk * grp + g))
    kspec = pl.BlockSpec((s_len, HP), lambda b, hk, g, i: (b, hk))
    return pl.pallas_call(
        body, name=name, grid=grid,
        out_shape=(jax.ShapeDtypeStruct((t, n_h * HP), BF16), jax.ShapeDtypeStruct((t, n_hk * HP), BF16),
                   jax.ShapeDtypeStruct((t, n_hk * HP), BF16), *c_outs),
        in_specs=[qspec, kspec, kspec, qspec, qspec, qspec, *c_in_specs],
        out_specs=(qspec, kspec, kspec, *([_ANY] * len(c_outs))),
        scratch_shapes=[pltpu.VMEM((s_len, tq), BF16), pltpu.VMEM((s_len, tq), BF16),
                        pltpu.VMEM((s_len, HP), F32), pltpu.VMEM((s_len, HP), F32), *c_sems],
        input_output_aliases=alias,
        compiler_params=_params(("arbitrary", "arbitrary", "arbitrary", "arbitrary")),
    )(q, k, v, do, o, lse, *c_ins)


def prep_bwd(dqa, dka, dva, dqb, dkb, dvb, zp, dz, tabs, g_qa, g_kva, g_qn, g_kn, w_qb_t, w_kvb_t, tm, s_len):
    t = zp.shape[0]
    nsb = s_len // tm
    scale_a = (QK_NOPE + QK_ROPE) ** -0.5
    scale_b = HD_B ** -0.5

    def body(dqa_ref, dka_ref, dva_ref, dqb_ref, dkb_ref, dvb_ref, qb_ref, qlat_ref, kb_ref, ckv_ref, tab_ref,
             gqa_ref, gkva_ref, gqn_ref, gkn_ref, wqb_ref, wkvb_ref, _, dz_o, dqap_o, dkva_o, st_o):
        dzq_o, dsm_o = dz_o.at[:, 0:1024], dz_o.at[:, 1024:2048]

        @pl.when(pl.program_id(0) == 0)
        def _():
            st_o[...] = jnp.zeros_like(st_o)

        ca, s1a, s2a = tab_ref[0], tab_ref[1], tab_ref[2]
        ck = tab_ref[3]
        cb, s1b, s2b = tab_ref[4], tab_ref[5], tab_ref[6]
        for h in range(H_A):
            sl = slice(h * HP, (h + 1) * HP)
            dqap_o[:, sl] = _rope_bwd(dqa_ref[:, sl].astype(F32) * scale_a, ca, s1a, s2a).astype(BF16)
        dcq = _dot_nn(dqap_o[...], wqb_ref[...])
        ql = qlat_ref[...].astype(F32)
        rq = _rstd(ql, Q_LORA)
        xh = ql * rq
        gqa = gqa_ref[...]
        st_o[0:1, :] += _colsum(dcq * xh)
        dsm_o[:, 0:256] = _rms_bwd(dcq, xh, rq, gqa, Q_LORA).astype(BF16)
        dkpe = jnp.zeros((tm, HP), F32)
        for h in range(H_A):
            sl = slice(h * HP, (h + 1) * HP)
            dk = dka_ref[:, sl]
            dkpe = dkpe + dk.astype(F32)
            dkva_o[:, sl] = dk.astype(BF16)
        dkva_o[:, H_A * HP:] = dva_ref[...].astype(BF16)
        dsm_o[:, 896:1024] = _rope_bwd(dkpe, ck, s1a, s2a).astype(BF16)
        dckv = _dot_nn(dkva_o[...], wkvb_ref[...])
        cr = ckv_ref[...].astype(F32)
        rk = _rstd(cr, KV_LORA)
        xh = cr * rk
        st_o[1:2, 0:128] += _colsum(dckv * xh)
        dsm_o[:, 768:896] = _rms_bwd(dckv, xh, rk, gkva_ref[...], KV_LORA).astype(BF16)
        gqn, gkn = gqn_ref[...], gkn_ref[...]
        dgq = jnp.zeros((1, HP), F32)
        for h in range(H_B):
            sl = slice(h * HP, (h + 1) * HP)
            dy = _rope_bwd(dqb_ref[:, sl].astype(F32) * scale_b, cb, s1b, s2b)
            xs = qb_ref[:, sl].astype(F32)
            r = _rstd(xs, HD_B)
            xh = xs * r
            dgq = dgq + _colsum(dy * xh)
            dzq_o[:, sl] = _rms_bwd(dy, xh, r, gqn, HD_B).astype(BF16)
        st_o[2:3, 0:128] += dgq
        dgk = jnp.zeros((1, HP), F32)
        for h in range(KV_B):
            sl = slice(h * HP, (h + 1) * HP)
            dy = _rope_bwd(dkb_ref[:, sl].astype(F32), cb, s1b, s2b)
            xs = kb_ref[:, sl].astype(F32)
            r = _rstd(xs, HD_B)
            xh = xs * r
            dgk = dgk + _colsum(dy * xh)
            dsm_o[:, 256 + h * HP:256 + (h + 1) * HP] = _rms_bwd(dy, xh, r, gkn, HD_B).astype(BF16)
        st_o[3:4, 0:128] += dgk
        dsm_o[:, 512:768] = dvb_ref[...].astype(BF16)

    return pl.pallas_call(
        body, name="prep_bwd", grid=(t // tm,),
        out_shape=(jax.ShapeDtypeStruct((t, ZP), BF16), jax.ShapeDtypeStruct((t, 1024), BF16),
                   jax.ShapeDtypeStruct((t, 2048), BF16), jax.ShapeDtypeStruct((4, 256), F32)),
        in_specs=[_rows(tm, 1024), _rows(tm, 1024), _rows(tm, 1024), _rows(tm, 1024), _rows(tm, 256), _rows(tm, 256),
                  _rows(tm, 1024, 0), _rows(tm, 256, 4), _rows(tm, 256, 5), _rows(tm, 128, 14),
                  pl.BlockSpec((7, tm, HP), lambda i: (0, i % nsb, 0)),
                  _resident((1, Q_LORA)), _resident((1, KV_LORA)), _resident((1, HP)), _resident((1, HP)),
                  _resident((H_A * HP, Q_LORA)), _resident((2 * H_A * HP, KV_LORA)), _ANY],
        out_specs=(_rows(tm, 2048, 1), _rows(tm, 1024), _rows(tm, 2048), pl.BlockSpec((4, 256), lambda i: (0, 0))),
        input_output_aliases={17: 0}, compiler_params=_params(("arbitrary",)),
    )(dqa, dka, dva, dqb, dkb, dvb, zp, zp, zp, zp, tabs, g_qa, g_kva, g_qn, g_kn, w_qb_t, w_kvb_t, dz)


def in_bwd(dz, x, dx1, g_mix, w_in_t, tm, comm=None):
    t = x.shape[0]
    grid = (t // tm,)
    c_ins, c_in_specs, c_outs, c_sems, alias = _comm_parts(comm, 5, 2)

    def body(*refs):
        (dz_ref, x_ref, dx1_ref, g_ref, w_ref), cin, (dx_o, st_o), cout, _, csem = _split_refs(refs, 5, 2, 0, comm)
        _comm_start(comm, cin, cout, csem, grid)

        @pl.when(pl.program_id(0) == 0)
        def _():
            st_o[...] = jnp.zeros_like(st_o)

        dh = _dot_nn(dz_ref[...], w_ref[...])
        xv = x_ref[...]
        g = g_ref[...]
        r = _rstd(xv, D_MODEL)
        xh = xv * r
        st_o[...] += _colsum(dh * xh)
        dx_o[...] = dx1_ref[...] + _rms_bwd(dh, xh, r, g, D_MODEL)
        _comm_finish(comm, cin, cout, csem, grid)

    return pl.pallas_call(
        body, name="in_bwd", grid=grid,
        out_shape=(jax.ShapeDtypeStruct((t, D_MODEL), F32), jax.ShapeDtypeStruct((1, D_MODEL), F32), *c_outs),
        in_specs=[_rows(tm, ZP), _rows(tm, D_MODEL), _rows(tm, D_MODEL),
                  _resident((1, D_MODEL)), _resident((ZP, D_MODEL)), *c_in_specs],
        out_specs=(_rows(tm, D_MODEL), pl.BlockSpec((1, D_MODEL), lambda i: (0, 0)), *([_ANY] * len(c_outs))),
        scratch_shapes=c_sems, input_output_aliases=alias,
        compiler_params=_params(("arbitrary",)),
    )(dz, x, dx1, g_mix, w_in_t, *c_ins)


def matmul_tn(a, b, name, square_a=False):
    t, m = a.shape
    n = b.shape[1]
    bm = min(m, 512)
    tk = min(t, 4096)

    def body(a_ref, b_ref, o_ref):
        @pl.when(pl.program_id(1) == 0)
        def _():
            o_ref[...] = jnp.zeros_like(o_ref)

        av = a_ref[...]
        if square_a:
            av = (av.astype(F32) * av.astype(F32))
        o_ref[...] += _dot_tn(av.astype(BF16), b_ref[...].astype(BF16))

    return pl.pallas_call(
        body, name=name, grid=(m // bm, t // tk), out_shape=jax.ShapeDtypeStruct((m, n), F32),
        in_specs=[pl.BlockSpec((tk, bm), lambda i, kk: (kk, i)), pl.BlockSpec((tk, n), lambda i, kk: (kk, 0))],
        out_specs=pl.BlockSpec((bm, n), lambda i, kk: (i, 0)),
        compiler_params=_params(("parallel", "arbitrary")),
    )(a, b)


def matmul_tn_packed(a, b, name, rows, row_off, total_rows, buf=None, square_a=False):
    t, m = a.shape
    n = b.shape[1]
    pd = max(1, 512 // rows)
    bm = pd * rows
    tk = min(t, 4096)
    nk = t // tk

    def body(a_ref, b_ref, *rest):
        o_ref, acc = rest[-2], rest[-1]

        @pl.when(pl.program_id(1) == 0)
        def _():
            acc[...] = jnp.zeros_like(acc)

        av = a_ref[...]
        if square_a:
            av = (av.astype(F32) * av.astype(F32))
        acc[...] += _dot_tn(av.astype(BF16), b_ref[...].astype(BF16))

        @pl.when(pl.program_id(1) == nk - 1)
        def _():
            o_ref[...] = acc[...].reshape(pd, rows, n).astype(o_ref.dtype)

    in_specs = [pl.BlockSpec((tk, bm), lambda i, kk: (kk, i)), pl.BlockSpec((tk, n), lambda i, kk: (kk, 0))]
    args = [a, b]
    if buf is not None:
        in_specs.append(_ANY)
        args.append(buf)
    return pl.pallas_call(
        body, name=name, grid=(m // bm, nk), out_shape=jax.ShapeDtypeStruct((N_DEV, total_rows, n), BF16),
        in_specs=in_specs, out_specs=pl.BlockSpec((pd, rows, n), lambda i, kk: (i, row_off // rows, 0)),
        scratch_shapes=[pltpu.VMEM((bm, n), F32)], input_output_aliases={2: 0} if buf is not None else {},
        compiler_params=_params(("parallel", "arbitrary")),
    )(*args)


def adamw(w, g, m, v, name, g_transposed=False):
    _, r, c = w.shape
    tr = 256 if (not g_transposed and r > 256 and r % 256 == 0) else r
    c1 = 1.0 - ADAM_B1 ** ADAM_STEP
    c2 = 1.0 - ADAM_B2 ** ADAM_STEP

    def body(w_ref, g_ref, m_ref, v_ref, g_o, d_o, m_o, v_o):
        gv = g_ref[...].T if g_transposed else g_ref[...]
        mn = ADAM_B1 * m_ref[0] + (1.0 - ADAM_B1) * gv
        vn = ADAM_B2 * v_ref[0] + (1.0 - ADAM_B2) * (gv * gv)
        g_o[0] = gv
        m_o[0] = mn
        v_o[0] = vn
        d_o[0] = -ADAM_LR * ((mn / c1) / (jnp.sqrt(vn / c2) + ADAM_EPS) + ADAM_WD * w_ref[0])

    spec = pl.BlockSpec((1, tr, c), lambda i: (0, i, 0))
    gspec = pl.BlockSpec((c, r), lambda i: (0, 0)) if g_transposed else pl.BlockSpec((tr, c), lambda i: (i, 0))
    shp = jax.ShapeDtypeStruct((1, r, c), F32)
    return pl.pallas_call(
        body, name=name, grid=(r // tr,), out_shape=(shp,) * 4, in_specs=[spec, gspec, spec, spec], out_specs=(spec,) * 4,
        compiler_params=_params(("parallel",)),
    )(w, g, m, v)


def _rope_tables(s_len):
    def angles(pos, dim):
        inv = np.float32(ROPE_THETA) ** (-np.arange(0, dim, 2, dtype=np.float32) / np.float32(dim))
        return pos.astype(np.float32)[:, None] * inv[None, :]

    tpos = np.arange(s_len)
    a1 = angles(tpos, QK_ROPE)
    ar = angles(tpos // GRID_W, HD_B // 2)
    ac = angles(tpos % GRID_W, HD_B // 2)
    z16 = np.zeros((s_len, 16), np.float32)
    z32 = np.zeros((s_len, 32), np.float32)
    z64 = np.zeros((s_len, 64), np.float32)
    one64 = np.ones((s_len, 64), np.float32)
    c1, s1 = np.cos(a1), np.sin(a1)
    ca = np.concatenate([one64, c1, c1, z32], axis=1)
    ck = np.concatenate([z64, c1, c1, z32], axis=1)
    s1a = np.concatenate([z64, -s1, z16, z32], axis=1)
    s2a = np.concatenate([z64, z16, s1, z32], axis=1)
    cr, sr, cc, sc = np.cos(ar), np.sin(ar), np.cos(ac), np.sin(ac)
    cb = np.concatenate([cr, cr, cc, cc, z64], axis=1)
    s1b = np.concatenate([-sr, z16, -sc, z16, z64], axis=1)
    s2b = np.concatenate([z16, sr, z16, sc, z64], axis=1)
    return jnp.asarray(np.stack([ca, s1a, s2a, ck, cb, s1b, s2b]).astype(np.float32))


def _pad_heads(a, n_heads, axis):
    shp = a.shape
    a = a.reshape(shp[:axis] + (n_heads, shp[axis] // n_heads) + shp[axis + 1:])
    pad = [(0, 0)] * a.ndim
    pad[axis + 1] = (0, HP - a.shape[axis + 1])
    a = jnp.pad(a, pad)
    return a.reshape(shp[:axis] + (n_heads * HP,) + shp[axis + 1:])


def _unpad_heads(a, n_heads, width, axis):
    shp = a.shape
    a = a.reshape(shp[:axis] + (n_heads, HP) + shp[axis + 1:])
    a = lax.slice_in_dim(a, 0, width, axis=axis + 1)
    return a.reshape(shp[:axis] + (n_heads * width,) + shp[axis + 1:])


def _pack_rows(blocks, names):
    parts = []
    for name in names:
        b = blocks[name]
        padr = PACK_ROWS[name] - b.shape[-2]
        if padr:
            b = jnp.pad(b, [(0, 0)] * (b.ndim - 2) + [(0, padr), (0, 0)])
        parts.append(b)
    return jnp.concatenate(parts, axis=parts[0].ndim - 2)


def _expand_w_in(wt):
    z64 = jnp.zeros((64, D_MODEL), wt.dtype)
    z32 = jnp.zeros((32, D_MODEL), wt.dtype)
    return jnp.concatenate([
        wt[1184:2208], wt[2208:3232], _pad_heads(wt[416:928], H_B, 0), wt[0:256],
        _pad_heads(wt[928:1056], KV_B, 0), _pad_heads(wt[1056:1184], KV_B, 0), wt[256:384],
        z64, wt[384:416], z32], axis=0)


def _collapse_w_in(dw):
    dg, dq, ds = dw[0:2048], dw[2048:3072], dw[3072:4096]
    return jnp.concatenate([
        ds[0:256], ds[768:896], ds[960:992], _unpad_heads(dq, H_B, HD_B, 0), _unpad_heads(ds[256:512], KV_B, HD_B, 0),
        _unpad_heads(ds[512:768], KV_B, HD_B, 0), dg], axis=0)


def kernel(x, p, g_mix, w_in, g_qa, w_qb, g_kva, w_kvb, g_qn, g_kn, w_oa, w_ob, w_o, g_mlp, w_up, w_down, g_ple, w_ple_gate, w_ple, g_final, loss_target, m_g_mix, m_w_in, m_g_qa, m_w_qb, m_g_kva, m_w_kvb, m_g_qn, m_g_kn, m_w_oa, m_w_ob, m_w_o, m_g_mlp, m_w_up, m_w_down, m_g_ple, m_w_ple_gate, m_w_ple, m_g_final, v_g_mix, v_w_in, v_g_qa, v_w_qb, v_g_kva, v_w_kvb, v_g_qn, v_g_kn, v_w_oa, v_w_ob, v_w_o, v_g_mlp, v_w_up, v_w_down, v_g_ple, v_w_ple_gate, v_w_ple, v_g_final):
    n_b, s_len, _ = x.shape
    t = n_b * s_len
    tm = min(512, s_len)
    tq_f = min(2048, s_len)
    tq_b = min(2048, s_len)

    mats = dict(w_in=(w_in, m_w_in, v_w_in), w_qb=(w_qb, m_w_qb, v_w_qb), w_kvb=(w_kvb, m_w_kvb, v_w_kvb),
                w_oa=(w_oa, m_w_oa, v_w_oa), w_ob=(w_ob, m_w_ob, v_w_ob), w_o=(w_o, m_w_o, v_w_o),
                w_up=(w_up, m_w_up, v_w_up), w_down=(w_down, m_w_down, v_w_down),
                w_ple_gate=(w_ple_gate, m_w_ple_gate, v_w_ple_gate), w_ple=(w_ple, m_w_ple, v_w_ple))
    col_sharded = ("w_in", "w_qb", "w_kvb", "w_oa", "w_ob", "w_up", "w_ple")

    blocks = {}
    for name in PACK_W1 + ("w_oa", "w_ob", "w_ple"):
        blocks[name] = mats[name][0][0].T.reshape(-1, D_MODEL).astype(BF16)
    off_w1, _ = _pack_offsets(PACK_W1)
    off_w2, _ = _pack_offsets(PACK_W2)
    off_w3, _ = _pack_offsets(PACK_W3)
    xf = x.reshape(t, D_MODEL)
    h, full1 = norm_x(xf, g_mix, tm, comm=gather3_first_comm(_pack_rows(blocks, PACK_W1)))
    pack2, pack3, full1 = pack_late_weights(w_up, w_down, w_o, w_ple_gate, _pack_rows(blocks, ("w_oa", "w_ob", "w_ple")),
                                            comm=gather3_second_comm(full1, then_third=True))

    def gathered(full, offs, name, rows, width):
        return full[:, offs[name]:offs[name] + rows].reshape(-1, width)

    w_in_t = _expand_w_in(gathered(full1, off_w1, "w_in", 404, D_MODEL))
    w_qb_t = _pad_heads(gathered(full1, off_w1, "w_qb", 24, Q_LORA), H_A, 0)
    wkvb = gathered(full1, off_w1, "w_kvb", 16, KV_LORA).reshape(H_A, 2, 64, KV_LORA)
    w_kvb_t = jnp.concatenate([_pad_heads(wkvb[:, 0].reshape(-1, KV_LORA), H_A, 0),
                               _pad_heads(wkvb[:, 1].reshape(-1, KV_LORA), H_A, 0)], axis=0)

    tabs = _rope_tables(s_len)
    g_qn_p = jnp.pad(g_qn, ((0, 0), (0, HP - HD_B)))
    g_kn_p = jnp.pad(g_kn, ((0, 0), (0, HP - HD_B)))
    pf = p.reshape(t, PLE_DIM)
    tgt = loss_target.reshape(t, D_MODEL)

    zg, zs, full2 = in_proj(h, w_in_t, tm, comm=gather3_first_comm(pack2))
    qa, ka, va, qb, kb, vb, cq, ckv, full2 = attn_prep(zs, tabs, g_qa, g_kva, g_qn_p, g_kn_p, w_qb_t, w_kvb_t, tm, s_len,
                                                       comm=gather3_second_comm(full2))
    oa, lse_a, full3, full2 = attn_fwd(qa, ka, va, n_b, s_len, tq_f, "attn_a_fwd",
                                       comm=_join_comms(gather_first_comm(pack3), gather3_third_comm(full2)))
    ob, lse_b, full3 = attn_fwd(qb, kb, vb, n_b, s_len, tq_f, "attn_b_fwd", comm=gather_pass_comm(full3))
    w_oa_t = _pad_heads(gathered(full2, off_w2, "w_oa", 64, H_A * V_DIM_A), H_A, 1)
    w_ob_t = _pad_heads(gathered(full2, off_w2, "w_ob", 64, H_B * HD_B), H_B, 1)
    w_ple_t = gathered(full2, off_w2, "w_ple", 32, PLE_DIM)
    x1, merged, yab = merge_fwd(oa, ob, zg, xf, w_oa_t, w_ob_t, full2, off_w2, tm)
    x2, u = mlp_fwd(x1, g_mlp, full3, off_w3, tm)
    dx2, dt, h3, dpe, st_ple, dx2b = ple_loss_bwd(x2, pf, tgt, g_ple, g_final.reshape(1, D_MODEL), full2, off_w2, w_ple_t, tm)
    dx1, da, h2, st_mlp, dx1b = mlp_bwd(dx2, x1, u, g_mlp, full3, off_w3, tm)

    core = lax.axis_index("c").astype(jnp.int32).reshape(1)
    chip = (2 * lax.axis_index("x") + lax.axis_index("y")).astype(jnp.int32).reshape(1)

    def packed(gblocks, names):
        return _pack_rows({n: gblocks[n].reshape(N_DEV, -1, D_MODEL).astype(BF16) for n in names}, names)

    off_g1, rows_g1 = _pack_offsets(PACK_G1)
    gpack1 = matmul_tn_packed(da, h2, "gw_up", 512, off_g1["w_up"], rows_g1)
    gpack1 = matmul_tn_packed(u, dx2b, "gw_down", 512, off_g1["w_down"], rows_g1, buf=gpack1, square_a=True)
    gpack1 = matmul_tn_packed(h3, dt, "gw_pg", 128, off_g1["w_ple_gate"], rows_g1, buf=gpack1)
    gple = matmul_tn(dpe, pf, "gw_ple").reshape(N_DEV, -1, D_MODEL).astype(BF16)
    gpack1 = lax.dynamic_update_slice(gpack1, gple, (0, off_g1["w_ple"], 0))
    doa, dob, dz, dya, dyb, got1 = merge_bwd(dx1b, yab, zg, w_oa_t, w_ob_t, full2, off_w2, tm,
                                               comm=scatter_sibling_comm(gpack1))
    part1 = add_pairs(gpack1, got1, core)

    off_g2, rows_g2 = _pack_offsets(PACK_G2)
    g2 = dict(w_oa=_unpad_heads(matmul_tn(dya, oa, "gw_oa"), H_A, V_DIM_A, 1),
              w_ob=_unpad_heads(matmul_tn(dyb, ob, "gw_ob"), H_B, HD_B, 1))
    gpack2 = matmul_tn_packed(merged, dx1b, "gw_o", 128, off_g2["w_o"], rows_g2)
    gpack2 = lax.dynamic_update_slice(gpack2, packed(g2, ("w_oa", "w_ob")), (0, off_g2["w_oa"], 0))
    dqa, dka, dva, land1, got2 = attn_bwd(qa, ka, va, doa, oa, lse_a, n_b, s_len, tq_b, "attn_a_bwd",
                                          comm=_join_comms(scatter_chips_comm(part1), scatter_sibling_comm(gpack2)))
    gshard1 = sum_chips(part1, land1, chip)
    part2 = add_pairs(gpack2, got2, core)
    dqb, dkb, dvb, land2 = attn_bwd(qb, kb, vb, dob, ob, lse_b, n_b, s_len, tq_b, "attn_b_bwd", comm=scatter_chips_comm(part2))
    gshard2 = sum_chips(part2, land2, chip)
    dz, dqap, dkva, st_prep = prep_bwd(dqa, dka, dva, dqb, dkb, dvb, zs, dz, tabs, g_qa, g_kva, g_qn_p, g_kn_p,
                                       w_qb_t, w_kvb_t, tm, s_len)

    gkv = matmul_tn(dkva, ckv, "gw_kvb")
    g3 = dict(
        w_in=_collapse_w_in(matmul_tn(dz, h, "gw_in")),
        w_qb=_unpad_heads(matmul_tn(dqap, cq, "gw_qb"), H_A, QK_NOPE + QK_ROPE, 0),
        w_kvb=jnp.stack([_unpad_heads(gkv[:H_A * HP], H_A, 64, 0).reshape(H_A, 64, KV_LORA),
                         _unpad_heads(gkv[H_A * HP:], H_A, 64, 0).reshape(H_A, 64, KV_LORA)], axis=1))
    gpack3 = packed(g3, PACK_G3)
    part3 = add_pairs(gpack3, exchange_sibling(gpack3), core)
    grad_x, st_mix, land3 = in_bwd(dz, xf, dx1, g_mix, w_in_t, tm, comm=scatter_chips_comm(part3))
    gshard3 = sum_chips(part3, land3, chip)
    off_g3, _ = _pack_offsets(PACK_G3)
    shards = {n: (gshard1, off_g1[n]) for n in PACK_G1}
    shards.update({n: (gshard2, off_g2[n]) for n in PACK_G2})
    shards.update({n: (gshard3, off_g3[n]) for n in PACK_G3})

    stats = allreduce_stats(st_mix, st_prep, st_mlp, st_ple)
    loss = jnp.sum(stats[ST_LOSS])

    out_g, out_d, out_m, out_v = {}, {}, {}, {}
    for name, (w, m, v) in mats.items():
        gshard, off = shards[name]
        r, c = w.shape[1:]
        if name in col_sharded:
            g2 = gshard[off:off + (r * c) // D_MODEL].reshape(c, r)
            if r % 128 == 0 and c % 128 == 0:
                res = adamw(w, g2, m, v, "adamw_" + name, g_transposed=True)
            else:
                res = adamw(w[0].T[None], g2, m[0].T[None], v[0].T[None], "adamw_" + name)
                res = tuple(a[0].T[None] for a in res)
        else:
            res = adamw(w, gshard[off:off + r], m, v, "adamw_" + name)
        out_g[name], out_d[name], out_m[name], out_v[name] = res

    gains = (("g_mix", g_mix, m_g_mix, v_g_mix, ST_G_MIX), ("g_qa", g_qa, m_g_qa, v_g_qa, ST_G_QA),
             ("g_kva", g_kva, m_g_kva, v_g_kva, ST_G_KVA), ("g_qn", g_qn, m_g_qn, v_g_qn, ST_G_QN),
             ("g_kn", g_kn, m_g_kn, v_g_kn, ST_G_KN), ("g_mlp", g_mlp, m_g_mlp, v_g_mlp, ST_G_MLP),
             ("g_ple", g_ple, m_g_ple, v_g_ple, ST_G_PLE), ("g_final", g_final, m_g_final, v_g_final, ST_G_FINAL))
    res = adamw_gains(stats, [(r_, w.reshape(1, -1), m.reshape(1, -1), v.reshape(1, -1)) for _, w, m, v, r_ in gains])
    for (name, w, _, _, _), (gg, gd, gm, gv) in zip(gains, res):
        out_g[name], out_d[name], out_m[name], out_v[name] = (a.reshape(w.shape) for a in (gg, gd, gm, gv))

    order = ("g_mix", "w_in", "g_qa", "w_qb", "g_kva", "w_kvb", "g_qn", "g_kn", "w_oa", "w_ob", "w_o", "g_mlp",
             "w_up", "w_down", "g_ple", "w_ple_gate", "w_ple", "g_final")
    return (loss, grad_x.reshape(x.shape), *[out_g[n] for n in order], *[out_d[n] for n in order],
            *[out_m[n] for n in order], *[out_v[n] for n in order])
```

```python
import numpy as np
import jax
import jax.numpy as jnp
from jax import lax
from jax.experimental import pallas as pl
from jax.experimental.pallas import tpu as pltpu

F32 = jnp.float32
BF16 = jnp.bfloat16

D_MODEL = 1024
EPS = 1e-6
ROPE_THETA = 10000.0
GRID_W = 64
H_A = 8
QK_NOPE = 64
QK_ROPE = 32
V_DIM_A = 64
Q_LORA = 256
KV_LORA = 128
H_B = 8
KV_B = 2
HD_B = 64
D_FF = 4 * D_MODEL
PLE_DIM = 256
HP = 128
ZP = 4096
N_DEV = 8
N_CHIP = 4

ADAM_LR = 0.001
ADAM_B1 = 0.9
ADAM_B2 = 0.999
ADAM_EPS = 1e-08
ADAM_WD = 0.01
ADAM_STEP = 10

VMEM_LIMIT = 52 * 1024 * 1024

PACK_ROWS = dict(w_in=416, w_qb=32, w_kvb=16, w_oa=64, w_ob=64, w_o=128, w_up=512, w_down=512, w_ple_gate=128, w_ple=32)
PACK_W1 = ("w_in", "w_qb", "w_kvb")
PACK_W2 = ("w_o", "w_ple_gate", "w_oa", "w_ob", "w_ple")
PACK_W3 = ("w_up", "w_down")
PACK_G1 = ("w_up", "w_down", "w_ple_gate", "w_ple")
PACK_G2 = ("w_o", "w_oa", "w_ob")
PACK_G3 = ("w_in", "w_qb", "w_kvb")


def _pack_offsets(names):
    off, o = {}, 0
    for n in names:
        off[n] = o
        o += PACK_ROWS[n]
    return off, o

ST_G_MIX, ST_G_QA, ST_G_KVA, ST_G_QN, ST_G_KN, ST_G_MLP, ST_G_PLE, ST_G_FINAL, ST_LOSS = range(9)
ST_ROWS = 16


def _dot_nn(a, b):
    return lax.dot_general(a, b, (((1,), (0,)), ((), ())), preferred_element_type=F32)


def _dot_nt(a, b):
    return lax.dot_general(a, b, (((1,), (1,)), ((), ())), preferred_element_type=F32)


def _dot_tn(a, b):
    return lax.dot_general(a, b, (((0,), (0,)), ((), ())), preferred_element_type=F32)


def _rstd(x, n):
    return lax.rsqrt(jnp.sum(x * x, axis=-1, keepdims=True) * (1.0 / n) + EPS)


def _rms_bwd(dy, xh, r, g, n):
    dxh = dy * g
    return r * (dxh - xh * (jnp.sum(dxh * xh, axis=-1, keepdims=True) * (1.0 / n)))


def _rope_fwd(x, c, s1, s2):
    return x * c + pltpu.roll(x, HP - 16, 1) * s1 + pltpu.roll(x, 16, 1) * s2


def _rope_bwd(d, c, s1, s2):
    return d * c + pltpu.roll(d * s1, 16, 1) + pltpu.roll(d * s2, HP - 16, 1)


def _colsum(v):
    return jnp.sum(v, axis=0, keepdims=True)


def _params(sem=None, vmem=VMEM_LIMIT):
    return pltpu.CompilerParams(dimension_semantics=sem, vmem_limit_bytes=vmem)


def _resident(shape):
    nd = len(shape)
    return pl.BlockSpec(shape, lambda *_: (0,) * nd, pipeline_mode=pl.Buffered(1))


def _rows(tm, width, col=0):
    return pl.BlockSpec((tm, width), lambda i: (i, col))


def _packed_weight(rows, off):
    return pl.BlockSpec((N_DEV, rows, D_MODEL), lambda *_: (0, off // rows, 0), pipeline_mode=pl.Buffered(1))


def _wrows(ref, start, size):
    rows = ref.shape[1]
    return ref[start // rows:(start + size) // rows].reshape(size, D_MODEL)


def _mesh_pos():
    return lax.axis_index("x"), lax.axis_index("y"), lax.axis_index("c")


def _flip(v, bit):
    return (1 - v) if bit else v


_ANY = pl.BlockSpec(memory_space=pl.ANY)
_MESH = pl.DeviceIdType.MESH


def _remote(src, dst, send_sems, recv_sems, k, to):
    return pltpu.make_async_remote_copy(src_ref=src, dst_ref=dst, send_sem=send_sems.at[k], recv_sem=recv_sems.at[k],
                                        device_id=to, device_id_type=_MESH)


def _sibling_copies(g_ref, got_ref, send_sems, recv_sems):
    x, y, c = _mesh_pos()
    return [_remote(g_ref.at[2 * j + (1 - c)], got_ref.at[j], send_sems, recv_sems, j, (x, y, 1 - c)) for j in range(N_CHIP)]


def _chip_copies(p_ref, land_ref, send_sems, recv_sems):
    x, y, c = _mesh_pos()
    copies = []
    for k in (1, 2, 3):
        tx, ty = _flip(x, k & 2), _flip(y, k & 1)
        copies.append(_remote(p_ref.at[2 * tx + ty], land_ref.at[k - 1], send_sems, recv_sems, k - 1, (tx, ty, c)))
    return copies


class _Comm:
    def __init__(self, ins, out_shapes, sems, make, aliases=None, make_tail=None):
        self.ins, self.out_shapes, self.sems, self.make, self.aliases = list(ins), list(out_shapes), list(sems), make, aliases or {}
        self.make_tail = make_tail


def _join_comms(a, b):
    assert a.make_tail is None and b.make_tail is None
    n_i, n_o, n_s = len(a.ins), len(a.out_shapes), len(a.sems)

    def make(cin, cout, sems):
        return a.make(cin[:n_i], cout[:n_o], sems[:n_s]) + b.make(cin[n_i:], cout[n_o:], sems[n_s:])

    aliases = dict(a.aliases)
    aliases.update({n_i + j: n_o + k for j, k in b.aliases.items()})
    return _Comm(a.ins + b.ins, a.out_shapes + b.out_shapes, a.sems + b.sems, make, aliases)


def _comm_parts(comm, n_in, n_out):
    if comm is None:
        return [], [], [], [], {}
    alias = {n_in + j: n_out + k for j, k in comm.aliases.items()}
    return comm.ins, [_ANY] * len(comm.ins), comm.out_shapes, comm.sems, alias


def _split_refs(refs, n_in, n_out, n_scratch, comm):
    n_ci = len(comm.ins) if comm else 0
    n_co = len(comm.out_shapes) if comm else 0
    cuts, i = [], 0
    for n in (n_in, n_ci, n_out, n_co, n_scratch):
        cuts.append(refs[i:i + n])
        i += n
    return (*cuts, refs[i:])


def _grid_edge(grid, last):
    cond = None
    for d, n in enumerate(grid):
        here = pl.program_id(d) == (n - 1 if last else 0)
        cond = here if cond is None else cond & here
    return cond


def _comm_start(comm, cin, cout, csem, grid):
    if comm is not None:
        @pl.when(_grid_edge(grid, False))
        def _():
            for cp in comm.make(cin, cout, csem):
                cp.start()


def _comm_finish(comm, cin, cout, csem, grid):
    if comm is not None:
        @pl.when(_grid_edge(grid, True))
        def _():
            for cp in comm.make(cin, cout, csem):
                cp.wait()
            if comm.make_tail is not None:
                tail = comm.make_tail(cin, cout, csem)
                for cp in tail:
                    cp.start()
                for cp in tail:
                    cp.wait()


def gather_first_comm(shard):
    r, w = shard.shape

    def make(cin, cout, sems):
        (x_ref,), (out_ref,), (send_sems, recv_sems, local_sem) = cin, cout, sems
        x, y, c = _mesh_pos()
        mine = out_ref.at[4 * x + 2 * y + c]
        targets = [(x, y, 1 - c), (1 - x, y, c), (x, 1 - y, c), (1 - x, 1 - y, c)]
        return [_remote(x_ref, mine, send_sems, recv_sems, k, to) for k, to in enumerate(targets)] + [
            pltpu.make_async_copy(x_ref, mine, local_sem)]

    return _Comm([shard], [jax.ShapeDtypeStruct((N_DEV, r, w), shard.dtype)],
                 [pltpu.SemaphoreType.DMA((4,)), pltpu.SemaphoreType.DMA((4,)), pltpu.SemaphoreType.DMA], make)


def gather_pass_comm(full):
    def make(cin, cout, sems):
        (in_ref,), (out_ref,), (send_sems, recv_sems) = cin, cout, sems
        x, y, c = _mesh_pos()
        copies = []
        for k, (px, py) in enumerate([(1 - x, y), (x, 1 - y), (1 - x, 1 - y)]):
            idx = 4 * px + 2 * py + c
            copies.append(_remote(in_ref.at[idx], out_ref.at[idx], send_sems, recv_sems, k, (x, y, 1 - c)))
        return copies

    return _Comm([full], [jax.ShapeDtypeStruct(full.shape, full.dtype)],
                 [pltpu.SemaphoreType.DMA((3,)), pltpu.SemaphoreType.DMA((3,))], make, aliases={0: 0})


def gather3_first_comm(shard):
    r, w = shard.shape

    def make(cin, cout, sems):
        (x_ref,), (out_ref,), (send_sems, recv_sems, local_sem) = cin, cout, sems
        x, y, c = _mesh_pos()
        mine = out_ref.at[4 * x + 2 * y + c]
        targets = [(x, y, 1 - c), (1 - x, y, c), (x, 1 - y, c)]
        return [_remote(x_ref, mine, send_sems, recv_sems, k, to) for k, to in enumerate(targets)] + [
            pltpu.make_async_copy(x_ref, mine, local_sem)]

    return _Comm([shard], [jax.ShapeDtypeStruct((N_DEV, r, w), shard.dtype)],
                 [pltpu.SemaphoreType.DMA((3,)), pltpu.SemaphoreType.DMA((3,)), pltpu.SemaphoreType.DMA], make)


def gather3_second_comm(full, then_third=False):
    def make(cin, cout, sems):
        (in_ref,), (out_ref,), (send_sems, recv_sems) = cin, cout, sems
        x, y, c = _mesh_pos()
        copies = []
        for k, (px, py) in enumerate([(1 - x, y), (x, 1 - y)]):
            idx = 4 * px + 2 * py + c
            copies.append(_remote(in_ref.at[idx], out_ref.at[idx], send_sems, recv_sems, k, (x, y, 1 - c)))
        fx, fy = x * c + (1 - x) * (1 - c), y * (1 - c) + (1 - y) * c
        tx, ty = x * (1 - c) + (1 - x) * c, (1 - y) * (1 - c) + y * c
        idx = 4 * fx + 2 * fy + c
        copies.append(_remote(in_ref.at[idx], out_ref.at[idx], send_sems, recv_sems, 2, (tx, ty, c)))
        return copies

    def third(cin, cout, sems):
        (out_ref,), (send_sems, recv_sems) = cout, sems
        x, y, c = _mesh_pos()
        idx = 4 * (1 - x) + 2 * (1 - y) + c
        return [_remote(out_ref.at[idx], out_ref.at[idx], send_sems, recv_sems, 3, (x, y, 1 - c))]

    n_sem = 4 if then_third else 3
    return _Comm([full], [jax.ShapeDtypeStruct(full.shape, full.dtype)],
                 [pltpu.SemaphoreType.DMA((n_sem,)), pltpu.SemaphoreType.DMA((n_sem,))], make, aliases={0: 0},
                 make_tail=third if then_third else None)


def gather3_third_comm(full):
    def make(cin, cout, sems):
        (in_ref,), (out_ref,), (send_sems, recv_sems) = cin, cout, sems
        x, y, c = _mesh_pos()
        idx = 4 * (1 - x) + 2 * (1 - y) + c
        return [_remote(in_ref.at[idx], out_ref.at[idx], send_sems, recv_sems, 0, (x, y, 1 - c))]

    return _Comm([full], [jax.ShapeDtypeStruct(full.shape, full.dtype)],
                 [pltpu.SemaphoreType.DMA((1,)), pltpu.SemaphoreType.DMA((1,))], make, aliases={0: 0})


def scatter_sibling_comm(g):
    _, r, w = g.shape
    return _Comm([g], [jax.ShapeDtypeStruct((N_CHIP, r, w), g.dtype)],
                 [pltpu.SemaphoreType.DMA((N_CHIP,)), pltpu.SemaphoreType.DMA((N_CHIP,))],
                 lambda cin, cout, sems: _sibling_copies(cin[0], cout[0], sems[0], sems[1]))


def scatter_chips_comm(part):
    _, r, w = part.shape
    return _Comm([part], [jax.ShapeDtypeStruct((N_CHIP - 1, r, w), part.dtype)],
                 [pltpu.SemaphoreType.DMA((3,)), pltpu.SemaphoreType.DMA((3,))],
                 lambda cin, cout, sems: _chip_copies(cin[0], cout[0], sems[0], sems[1]))


def exchange_sibling(g):
    _, r, w = g.shape

    def body(g_ref, got_ref, send_sems, recv_sems):
        copies = _sibling_copies(g_ref, got_ref, send_sems, recv_sems)
        for cp in copies:
            cp.start()
        for cp in copies:
            cp.wait()

    return pl.pallas_call(
        body, name="exchange_sibling", out_shape=jax.ShapeDtypeStruct((N_CHIP, r, w), g.dtype),
        in_specs=[_ANY], out_specs=_ANY,
        scratch_shapes=[pltpu.SemaphoreType.DMA((N_CHIP,)), pltpu.SemaphoreType.DMA((N_CHIP,))],
    )(g)


def exchange_chips(part):
    _, r, w = part.shape

    def body(p_ref, land_ref, send_sems, recv_sems):
        copies = _chip_copies(p_ref, land_ref, send_sems, recv_sems)
        for cp in copies:
            cp.start()
        for cp in copies:
            cp.wait()

    return pl.pallas_call(
        body, name="exchange_chips", out_shape=jax.ShapeDtypeStruct((N_CHIP - 1, r, w), part.dtype),
        in_specs=[_ANY], out_specs=_ANY,
        scratch_shapes=[pltpu.SemaphoreType.DMA((3,)), pltpu.SemaphoreType.DMA((3,))],
    )(part)


def allreduce_stats(st_mix, st_prep, st_mlp, st_ple):
    def body(mix_ref, prep_ref, mlp_ref, ple_ref, out_ref, mine, gath, send_sems, recv_sems):
        x, y, c = _mesh_pos()
        me = 4 * x + 2 * y + c
        mine[...] = jnp.zeros_like(mine)
        mine[ST_G_MIX:ST_G_MIX + 1, :] = mix_ref[...]
        mine[ST_G_QA:ST_G_KN + 1, 0:256] = prep_ref[...]
        mine[ST_G_MLP:ST_G_MLP + 1, :] = mlp_ref[...]
        mine[ST_G_PLE:ST_LOSS + 1, :] = ple_ref[...]
        gath[me] = mine[...]
        copies = []
        for k in range(1, N_DEV):
            peer = (_flip(x, k & 4), _flip(y, k & 2), _flip(c, k & 1))
            copies.append(_remote(mine, gath.at[me], send_sems, recv_sems, k - 1, peer))
        for cp in copies:
            cp.start()
        for cp in copies:
            cp.wait()
        acc = gath[0]
        for d in range(1, N_DEV):
            acc = acc + gath[d]
        out_ref[...] = acc

    vm = pl.BlockSpec(memory_space=pltpu.VMEM)
    return pl.pallas_call(
        body, name="allreduce_stats", out_shape=jax.ShapeDtypeStruct((ST_ROWS, D_MODEL), F32),
        in_specs=[vm] * 4, out_specs=vm,
        scratch_shapes=[pltpu.VMEM((ST_ROWS, D_MODEL), F32), pltpu.VMEM((N_DEV, ST_ROWS, D_MODEL), F32),
                        pltpu.SemaphoreType.DMA((N_DEV - 1,)), pltpu.SemaphoreType.DMA((N_DEV - 1,))],
    )(st_mix, st_prep, st_mlp, st_ple)


def adamw_gains(stats, gains):
    c1 = 1.0 - ADAM_B1 ** ADAM_STEP
    c2 = 1.0 - ADAM_B2 ** ADAM_STEP
    n = len(gains)

    def body(st_ref, *refs):
        ins, outs = refs[:3 * n], refs[3 * n:]
        for i, (row, w, _, _) in enumerate(gains):
            width = w.shape[1]
            gv = st_ref[row:row + 1, 0:width]
            mn = ADAM_B1 * ins[3 * i + 1][...] + (1.0 - ADAM_B1) * gv
            vn = ADAM_B2 * ins[3 * i + 2][...] + (1.0 - ADAM_B2) * (gv * gv)
            outs[4 * i][...] = gv
            outs[4 * i + 1][...] = -ADAM_LR * ((mn / c1) / (jnp.sqrt(vn / c2) + ADAM_EPS) + ADAM_WD * ins[3 * i][...])
            outs[4 * i + 2][...] = mn
            outs[4 * i + 3][...] = vn

    vm = pl.BlockSpec(memory_space=pltpu.VMEM)
    flat = [a for (_, w, m, v) in gains for a in (w, m, v)]
    out_shape = tuple(jax.ShapeDtypeStruct(w.shape, F32) for (_, w, _, _) in gains for _ in range(4))
    res = pl.pallas_call(body, name="adamw_gains", out_shape=out_shape, in_specs=[vm] * (1 + 3 * n),
                         out_specs=tuple([vm] * (4 * n)))(stats, *flat)
    return [res[4 * i:4 * i + 4] for i in range(n)]


def _row_tile(r, cap=640):
    return max(d for d in range(16, min(r, cap) + 1, 16) if r % d == 0)


def add_pairs(g, got, core):
    n, r, w = got.shape
    tr = _row_tile(r)

    def body(c_ref, a_ref, b_ref, o_ref):
        o_ref[...] = (a_ref[...].astype(F32) + b_ref[...].astype(F32)).astype(o_ref.dtype)

    spec = pl.BlockSpec((1, tr, w), lambda i, j, c: (i, j, 0))
    return pl.pallas_call(
        body, name="add_pairs", out_shape=jax.ShapeDtypeStruct(got.shape, got.dtype),
        grid_spec=pltpu.PrefetchScalarGridSpec(
            num_scalar_prefetch=1, grid=(n, r // tr),
            in_specs=[pl.BlockSpec((1, tr, w), lambda i, j, c: (2 * i + c[0], j, 0)), spec], out_specs=spec),
        compiler_params=_params(("parallel", "parallel")),
    )(core, g, got)


def sum_chips(part, land, chip):
    _, r, w = part.shape
    tr = _row_tile(r)

    def body(c_ref, p_ref, l_ref, o_ref):
        acc = p_ref[0].astype(F32)
        for s in range(N_CHIP - 1):
            acc = acc + l_ref[s].astype(F32)
        o_ref[...] = acc

    return pl.pallas_call(
        body, name="sum_chips", out_shape=jax.ShapeDtypeStruct((r, w), F32),
        grid_spec=pltpu.PrefetchScalarGridSpec(
            num_scalar_prefetch=1, grid=(r // tr,),
            in_specs=[pl.BlockSpec((1, tr, w), lambda i, c: (c[0], i, 0)), pl.BlockSpec((N_CHIP - 1, tr, w), lambda i, c: (0, i, 0))],
            out_specs=pl.BlockSpec((tr, w), lambda i, c: (i, 0))),
        compiler_params=_params(("parallel",)),
    )(chip, part, land)


def norm_x(x, g_mix, tm, comm=None):
    t = x.shape[0]
    grid = (t // tm,)
    c_ins, c_in_specs, c_outs, c_sems, alias = _comm_parts(comm, 2, 1)

    def body(*refs):
        (x_ref, g_ref), cin, (h_ref,), cout, _, csem = _split_refs(refs, 2, 1, 0, comm)
        _comm_start(comm, cin, cout, csem, grid)
        xv = x_ref[...]
        h_ref[...] = (xv * _rstd(xv, D_MODEL) * g_ref[...]).astype(BF16)
        _comm_finish(comm, cin, cout, csem, grid)

    return pl.pallas_call(
        body, name="norm_x", grid=grid, out_shape=(jax.ShapeDtypeStruct((t, D_MODEL), BF16), *c_outs),
        in_specs=[_rows(tm, D_MODEL), _resident((1, D_MODEL)), *c_in_specs],
        out_specs=(_rows(tm, D_MODEL), *([_ANY] * len(c_outs))),
        scratch_shapes=c_sems, input_output_aliases=alias, compiler_params=_params(("arbitrary",)),
    )(x, g_mix, *c_ins)


def pack_late_weights(w_up, w_down, w_o, w_pg, small, comm=None):
    rows2 = sum(PACK_ROWS[n] for n in PACK_W2)
    rows3 = sum(PACK_ROWS[n] for n in PACK_W3)
    c_ins, c_in_specs, c_outs, c_sems, alias = _comm_parts(comm, 5, 2)
    grid = (1,)

    def body(*refs):
        (up_ref, dn_ref, o_ref, pg_ref, sm_ref), cin, (p2_ref, p3_ref), cout, _, csem = _split_refs(refs, 5, 2, 0, comm)
        _comm_start(comm, cin, cout, csem, grid)
        p2_ref[0:128, :] = o_ref[0].astype(BF16)
        p2_ref[128:256, :] = pg_ref[0].astype(BF16)
        p2_ref[256:rows2, :] = sm_ref[...]
        p3_ref[0:512, :] = up_ref[0].T.astype(BF16)
        p3_ref[512:1024, :] = dn_ref[0].astype(BF16)
        _comm_finish(comm, cin, cout, csem, grid)

    def whole(a):
        nd = a.ndim
        return pl.BlockSpec(a.shape, lambda i: (0,) * nd)

    args = (w_up, w_down, w_o, w_pg, small)
    return pl.pallas_call(
        body, name="pack_late_weights", grid=grid,
        out_shape=(jax.ShapeDtypeStruct((rows2, D_MODEL), BF16), jax.ShapeDtypeStruct((rows3, D_MODEL), BF16), *c_outs),
        in_specs=[*[whole(a) for a in args], *c_in_specs],
        out_specs=(pl.BlockSpec((rows2, D_MODEL), lambda i: (0, 0)), pl.BlockSpec((rows3, D_MODEL), lambda i: (0, 0)),
                   *([_ANY] * len(c_outs))),
        scratch_shapes=c_sems, input_output_aliases=alias, compiler_params=_params(("arbitrary",)),
    )(*args, *c_ins)


def in_proj(h, w_in_t, tm, comm=None):
    t = h.shape[0]
    nc = 512
    half = ZP // 2
    grid = (t // tm,)
    c_ins, c_in_specs, c_outs, c_sems, alias = _comm_parts(comm, 2, 2)

    def body(*refs):
        (h_ref, w_ref), cin, (zg_ref, zs_ref), cout, _, csem = _split_refs(refs, 2, 2, 0, comm)
        _comm_start(comm, cin, cout, csem, grid)
        hv = h_ref[...]
        for cidx in range(half // nc):
            zg_ref[:, cidx * nc:(cidx + 1) * nc] = _dot_nt(hv, w_ref[cidx * nc:(cidx + 1) * nc, :]).astype(BF16)
        for cidx in range(half // nc):
            zs_ref[:, cidx * nc:(cidx + 1) * nc] = _dot_nt(hv, w_ref[half + cidx * nc:half + (cidx + 1) * nc, :]).astype(BF16)
        _comm_finish(comm, cin, cout, csem, grid)

    return pl.pallas_call(
        body, name="in_proj", grid=grid,
        out_shape=(jax.ShapeDtypeStruct((t, half), BF16), jax.ShapeDtypeStruct((t, half), BF16), *c_outs),
        in_specs=[_rows(tm, D_MODEL), _resident((ZP, D_MODEL)), *c_in_specs],
        out_specs=(_rows(tm, half), _rows(tm, half), *([_ANY] * len(c_outs))),
        scratch_shapes=c_sems, input_output_aliases=alias, compiler_params=_params(("arbitrary",)),
    )(h, w_in_t, *c_ins)


def attn_prep(zp, tabs, g_qa, g_kva, g_qn, g_kn, w_qb_t, w_kvb_t, tm, s_len, comm=None):
    t = zp.shape[0]
    nsb = s_len // tm
    scale_a = (QK_NOPE + QK_ROPE) ** -0.5
    scale_b = HD_B ** -0.5

    grid = (t // tm,)
    c_ins, c_in_specs, c_outs, c_sems, alias = _comm_parts(comm, 13, 8)

    def body(*refs):
        ((qb_ref, qlat_ref, kb_ref, vb_ref, ckv_ref, kpe_ref, tab_ref, gqa_ref, gkva_ref, gqn_ref, gkn_ref, wqb_ref,
          wkvb_ref), cin, (qa_o, ka_o, va_o, qb_o, kb_o, vb_o, cq_o, ckvn_o), cout, _, csem) = _split_refs(refs, 13, 8, 0, comm)
        _comm_start(comm, cin, cout, csem, grid)
        ca, s1a, s2a = tab_ref[0], tab_ref[1], tab_ref[2]
        ck = tab_ref[3]
        cb, s1b, s2b = tab_ref[4], tab_ref[5], tab_ref[6]
        ql = qlat_ref[...].astype(F32)
        cq = (ql * _rstd(ql, Q_LORA) * gqa_ref[...]).astype(BF16)
        cq_o[...] = cq
        qa = _dot_nt(cq, wqb_ref[...])
        slabs = [slice(h * HP, (h + 1) * HP) for h in range(H_A)]
        qa_o[...] = jnp.concatenate(
            [(_rope_fwd(qa[:, sl], ca, s1a, s2a) * scale_a).astype(BF16) for sl in slabs], axis=1)
        cr = ckv_ref[...].astype(F32)
        ckv = (cr * _rstd(cr, KV_LORA) * gkva_ref[...]).astype(BF16)
        ckvn_o[...] = ckv
        kva = _dot_nt(ckv, wkvb_ref[...])
        kpe = _rope_fwd(kpe_ref[...].astype(F32), ck, s1a, s2a)
        ka_o[...] = jnp.concatenate([(kva[:, sl] + kpe).astype(BF16) for sl in slabs], axis=1)
        va_o[...] = kva[:, H_A * HP:].astype(BF16)
        gqn, gkn = gqn_ref[...], gkn_ref[...]

        def norm_rope(ref, sl, g, scale):
            xs = ref[:, sl].astype(F32)
            y = _rope_fwd(xs * _rstd(xs, HD_B) * g, cb, s1b, s2b)
            return (y if scale is None else y * scale).astype(BF16)

        qb_o[...] = jnp.concatenate([norm_rope(qb_ref, sl, gqn, scale_b) for sl in slabs], axis=1)
        kb_o[...] = jnp.concatenate([norm_rope(kb_ref, sl, gkn, None) for sl in slabs[:KV_B]], axis=1)
        vb_o[...] = vb_ref[...].astype(BF16)
        _comm_finish(comm, cin, cout, csem, grid)

    def o(width):
        return jax.ShapeDtypeStruct((t, width), BF16)

    return pl.pallas_call(
        body, name="attn_prep", grid=grid,
        out_shape=(o(H_A * HP), o(H_A * HP), o(H_A * HP), o(H_B * HP), o(KV_B * HP), o(KV_B * HP), o(Q_LORA), o(KV_LORA),
                   *c_outs),
        in_specs=[_rows(tm, 1024, 0), _rows(tm, 256, 4), _rows(tm, 256, 5), _rows(tm, 256, 6),
                  _rows(tm, 128, 14), _rows(tm, 128, 15),
                  pl.BlockSpec((7, tm, HP), lambda i: (0, i % nsb, 0)),
                  _resident((1, Q_LORA)), _resident((1, KV_LORA)), _resident((1, HP)), _resident((1, HP)),
                  _resident((H_A * HP, Q_LORA)), _resident((2 * H_A * HP, KV_LORA)), *c_in_specs],
        out_specs=(_rows(tm, H_A * HP), _rows(tm, H_A * HP), _rows(tm, H_A * HP), _rows(tm, H_B * HP),
                   _rows(tm, KV_B * HP), _rows(tm, KV_B * HP), _rows(tm, Q_LORA), _rows(tm, KV_LORA), *([_ANY] * len(c_outs))),
        scratch_shapes=c_sems, input_output_aliases=alias, compiler_params=_params(("arbitrary",)),
    )(zp, zp, zp, zp, zp, zp, tabs, g_qa, g_kva, g_qn, g_kn, w_qb_t, w_kvb_t, *c_ins)


def attn_fwd(q, k, v, n_b, s_len, tq, name, comm=None):
    t = q.shape[0]
    n_h, n_hk = q.shape[1] // HP, k.shape[1] // HP
    grp = n_h // n_hk
    nq = s_len // tq
    sub = min(tq, 256)
    grid = (n_b, n_h, nq)
    c_ins, c_in_specs, c_outs, c_sems, alias = _comm_parts(comm, 3, 2)

    def body(*refs):
        (q_ref, k_ref, v_ref), cin, (o_ref, lse_ref), cout, _, csem = _split_refs(refs, 3, 2, 0, comm)
        _comm_start(comm, cin, cout, csem, grid)
        kv, vv = k_ref[...], v_ref[...]
        for r in range(tq // sub):
            rows = slice(r * sub, (r + 1) * sub)
            s = _dot_nt(q_ref[rows, :], kv)
            m = jnp.max(s, axis=-1, keepdims=True)
            p = jnp.exp(s - m)
            l = jnp.sum(p, axis=-1, keepdims=True)
            o_ref[rows, :] = (_dot_nn(p.astype(BF16), vv) * (1.0 / l)).astype(o_ref.dtype)
            lse_ref[rows, :] = jnp.broadcast_to(m + jnp.log(l), (sub, HP))
        _comm_finish(comm, cin, cout, csem, grid)

    qspec = pl.BlockSpec((tq, HP), lambda b, h, i: (b * nq + i, h))
    kspec = pl.BlockSpec((s_len, HP), lambda b, h, i: (b, h // grp))
    return pl.pallas_call(
        body, name=name, grid=grid,
        out_shape=(jax.ShapeDtypeStruct((t, n_h * HP), BF16), jax.ShapeDtypeStruct((t, n_h * HP), F32), *c_outs),
        in_specs=[qspec, kspec, kspec, *c_in_specs], out_specs=(qspec, qspec, *([_ANY] * len(c_outs))),
        scratch_shapes=c_sems, input_output_aliases=alias,
        compiler_params=_params(("arbitrary", "arbitrary", "arbitrary")),
    )(q, k, v, *c_ins)


def merge_fwd(oa, ob, zp, x, w_oa_t, w_ob_t, wpack, off, tm):
    t = x.shape[0]

    def body(oa_ref, ob_ref, ga_ref, gb_ref, x_ref, woa_ref, wob_ref, wo_ref, x1_o, mg_o, y_o):
        ya = _dot_nt(oa_ref[...], woa_ref[...])
        yb = _dot_nt(ob_ref[...], wob_ref[...])
        y_o[:, 0:D_MODEL] = ya.astype(BF16)
        y_o[:, D_MODEL:2 * D_MODEL] = yb.astype(BF16)
        merged = (jax.nn.sigmoid(ga_ref[...].astype(F32)) * ya + jax.nn.sigmoid(gb_ref[...].astype(F32)) * yb).astype(BF16)
        mg_o[...] = merged
        x1_o[...] = x_ref[...] + _dot_nn(merged, _wrows(wo_ref, 0, D_MODEL))

    return pl.pallas_call(
        body, name="merge_fwd", grid=(t // tm,),
        out_shape=(jax.ShapeDtypeStruct((t, D_MODEL), F32), jax.ShapeDtypeStruct((t, D_MODEL), BF16),
                   jax.ShapeDtypeStruct((t, 2 * D_MODEL), BF16)),
        in_specs=[_rows(tm, H_A * HP), _rows(tm, H_B * HP), _rows(tm, 1024, 0), _rows(tm, 1024, 1), _rows(tm, D_MODEL),
                  _resident((D_MODEL, H_A * HP)), _resident((D_MODEL, H_B * HP)), _packed_weight(128, off["w_o"])],
        out_specs=(_rows(tm, D_MODEL), _rows(tm, D_MODEL), _rows(tm, 2 * D_MODEL)), compiler_params=_params(("parallel",)),
    )(oa, ob, zp, zp, x, w_oa_t, w_ob_t, wpack)


def mlp_fwd(x1, g_mlp, wpack, off, tm):
    t = x1.shape[0]
    fc = 1024

    def body(x_ref, g_ref, wup_ref, wdn_ref, x2_o, u_o):
        xv = x_ref[...]
        h2 = (xv * _rstd(xv, D_MODEL) * g_ref[...]).astype(BF16)
        acc = xv
        for cidx in range(D_FF // fc):
            sl = slice(cidx * fc, (cidx + 1) * fc)
            u = jnp.maximum(_dot_nt(h2, _wrows(wup_ref, cidx * fc, fc)), 0.0)
            u_o[:, sl] = u.astype(BF16)
            acc = acc + _dot_nn((u * u).astype(BF16), _wrows(wdn_ref, cidx * fc, fc))
        x2_o[...] = acc

    return pl.pallas_call(
        body, name="mlp_fwd", grid=(t // tm,),
        out_shape=(jax.ShapeDtypeStruct((t, D_MODEL), F32), jax.ShapeDtypeStruct((t, D_FF), BF16)),
        in_specs=[_rows(tm, D_MODEL), _resident((1, D_MODEL)), _packed_weight(512, off["w_up"]), _packed_weight(512, off["w_down"])],
        out_specs=(_rows(tm, D_MODEL), _rows(tm, D_FF)), compiler_params=_params(("parallel",)),
    )(x1, g_mlp, wpack, wpack)


def ple_loss_bwd(x2, p, tgt, g_ple, g_final, wpack, off, w_ple_t, tm):
    t = x2.shape[0]
    inv_d = 1.0 / D_MODEL

    def body(x2_ref, p_ref, tg_ref, gp_ref, gf_ref, wpg_ref, wple_ref, dx2_o, dt_o, h3_o, dpe_o, st_o, dx2b_o):
        @pl.when(pl.program_id(0) == 0)
        def _():
            st_o[...] = jnp.zeros_like(st_o)

        x2v = x2_ref[...]
        gp, gf = gp_ref[...], gf_ref[...]
        w_pg = _wrows(wpg_ref, 0, D_MODEL)
        r2 = _rstd(x2v, D_MODEL)
        xh2 = x2v * r2
        h3 = (xh2 * gp).astype(BF16)
        h3_o[...] = h3
        gate = jax.nn.sigmoid(_dot_nn(h3, w_pg))
        pe = _dot_nt(p_ref[...].astype(BF16), wple_ref[...])
        x3 = x2v + gate * pe
        r3 = _rstd(x3, D_MODEL)
        xh3 = x3 * r3
        err = xh3 * gf - tg_ref[...]
        dx3 = _rms_bwd(err, xh3, r3, gf * inv_d, D_MODEL)
        dpe = dx3 * gate
        dpe_o[...] = dpe.astype(BF16)
        dt = (dpe * pe * (1.0 - gate)).astype(BF16)
        dt_o[...] = dt
        dh3 = _dot_nt(dt, w_pg)
        dx2 = dx3 + _rms_bwd(dh3, xh2, r2, gp, D_MODEL)
        dx2_o[...] = dx2
        dx2b_o[...] = dx2.astype(BF16)
        st_o[0:1, :] += _colsum(dh3 * xh2)
        st_o[1:2, :] += _colsum(err * xh3) * inv_d
        st_o[2:3, :] += _colsum(err * err) * (0.5 * inv_d)

    bf = jax.ShapeDtypeStruct((t, D_MODEL), BF16)
    return pl.pallas_call(
        body, name="ple_loss_bwd", grid=(t // tm,),
        out_shape=(jax.ShapeDtypeStruct((t, D_MODEL), F32), bf, bf, bf, jax.ShapeDtypeStruct((3, D_MODEL), F32), bf),
        in_specs=[_rows(tm, D_MODEL), _rows(tm, PLE_DIM), _rows(tm, D_MODEL), _resident((1, D_MODEL)), _resident((1, D_MODEL)),
                  _packed_weight(128, off["w_ple_gate"]), _resident((D_MODEL, PLE_DIM))],
        out_specs=(_rows(tm, D_MODEL), _rows(tm, D_MODEL), _rows(tm, D_MODEL), _rows(tm, D_MODEL),
                   pl.BlockSpec((3, D_MODEL), lambda i: (0, 0)), _rows(tm, D_MODEL)),
        compiler_params=_params(("arbitrary",)),
    )(x2, p, tgt, g_ple, g_final, wpack, w_ple_t)


def mlp_bwd(dx2, x1, u, g_mlp, wpack, off, tm):
    t = x1.shape[0]
    fc = 1024

    def body(dx2_ref, x1_ref, u_ref, g_ref, wup_ref, wdn_ref, dx1_o, da_o, h2_o, st_o, dx1b_o):
        @pl.when(pl.program_id(0) == 0)
        def _():
            st_o[...] = jnp.zeros_like(st_o)

        d2 = dx2_ref[...]
        d2b = d2.astype(BF16)
        dh2 = jnp.zeros((tm, D_MODEL), F32)
        for cidx in range(D_FF // fc):
            sl = slice(cidx * fc, (cidx + 1) * fc)
            da = (_dot_nt(d2b, _wrows(wdn_ref, cidx * fc, fc)) * (2.0 * u_ref[:, sl].astype(F32))).astype(BF16)
            da_o[:, sl] = da
            dh2 = dh2 + _dot_nn(da, _wrows(wup_ref, cidx * fc, fc))
        xv = x1_ref[...]
        g = g_ref[...]
        r1 = _rstd(xv, D_MODEL)
        xh1 = xv * r1
        h2_o[...] = (xh1 * g).astype(BF16)
        st_o[...] += _colsum(dh2 * xh1)
        dx1 = d2 + _rms_bwd(dh2, xh1, r1, g, D_MODEL)
        dx1_o[...] = dx1
        dx1b_o[...] = dx1.astype(BF16)

    return pl.pallas_call(
        body, name="mlp_bwd", grid=(t // tm,),
        out_shape=(jax.ShapeDtypeStruct((t, D_MODEL), F32), jax.ShapeDtypeStruct((t, D_FF), BF16),
                   jax.ShapeDtypeStruct((t, D_MODEL), BF16), jax.ShapeDtypeStruct((1, D_MODEL), F32),
                   jax.ShapeDtypeStruct((t, D_MODEL), BF16)),
        in_specs=[_rows(tm, D_MODEL), _rows(tm, D_MODEL), _rows(tm, D_FF), _resident((1, D_MODEL)),
                  _packed_weight(512, off["w_up"]), _packed_weight(512, off["w_down"])],
        out_specs=(_rows(tm, D_MODEL), _rows(tm, D_FF), _rows(tm, D_MODEL), pl.BlockSpec((1, D_MODEL), lambda i: (0, 0)),
                   _rows(tm, D_MODEL)),
        compiler_params=_params(("arbitrary",)),
    )(dx2, x1, u, g_mlp, wpack, wpack)


def merge_bwd(dx1b, yab, zp, w_oa_t, w_ob_t, wpack, off, tm, comm=None):
    t = dx1b.shape[0]
    grid = (t // tm,)
    c_ins, c_in_specs, c_outs, c_sems, alias = _comm_parts(comm, 7, 5)

    def body(*refs):
        ((dx1_ref, y_ref, ga_ref, gb_ref, woa_ref, wob_ref, wo_ref), cin,
         (doa_o, dob_o, dg_o, dya_o, dyb_o), cout, _, csem) = _split_refs(refs, 7, 5, 0, comm)
        _comm_start(comm, cin, cout, csem, grid)
        dm = _dot_nt(dx1_ref[...], _wrows(wo_ref, 0, D_MODEL))
        for g_ref, w_ref, do_o, dy_o, col in ((ga_ref, woa_ref, doa_o, dya_o, 0), (gb_ref, wob_ref, dob_o, dyb_o, 1)):
            cols = slice(col * D_MODEL, (col + 1) * D_MODEL)
            sg = jax.nn.sigmoid(g_ref[...].astype(F32))
            dyv = (dm * sg).astype(BF16)
            dy_o[...] = dyv
            dg_o[:, cols] = (dm * y_ref[:, cols].astype(F32) * sg * (1.0 - sg)).astype(BF16)
            do_o[...] = _dot_nn(dyv, w_ref[...]).astype(BF16)
        _comm_finish(comm, cin, cout, csem, grid)

    bf = jax.ShapeDtypeStruct((t, D_MODEL), BF16)
    return pl.pallas_call(
        body, name="merge_bwd", grid=grid,
        out_shape=(bf, bf, jax.ShapeDtypeStruct((t, ZP), BF16), bf, bf, *c_outs),
        in_specs=[_rows(tm, D_MODEL), _rows(tm, 2 * D_MODEL), _rows(tm, 1024, 0), _rows(tm, 1024, 1),
                  _resident((D_MODEL, H_A * HP)), _resident((D_MODEL, H_B * HP)), _packed_weight(128, off["w_o"]), *c_in_specs],
        out_specs=(_rows(tm, D_MODEL), _rows(tm, D_MODEL), _rows(tm, 2 * D_MODEL), _rows(tm, D_MODEL), _rows(tm, D_MODEL),
                   *([_ANY] * len(c_outs))),
        scratch_shapes=c_sems, input_output_aliases=alias,
        compiler_params=_params(("arbitrary",)),
    )(dx1b, yab, zp, zp, w_oa_t, w_ob_t, wpack, *c_ins)


def attn_bwd(q, k, v, do, o, lse, n_b, s_len, tq, name, comm=None):
    t = q.shape[0]
    n_h, n_hk = q.shape[1] // HP, k.shape[1] // HP
    grp = n_h // n_hk
    nq = s_len // tq
    sub = min(tq, 256)
    grid = (n_b, n_hk, grp, nq)
    c_ins, c_in_specs, c_outs, c_sems, alias = _comm_parts(comm, 6, 3)

    def body(*refs):
        ((q_ref, k_ref, v_ref, do_ref, o_ref, lse_ref), cin, (dq_o, dk_o, dv_o), cout, (p_s, ds_s, dk_acc, dv_acc),
         csem) = _split_refs(refs, 6, 3, 4, comm)
        _comm_start(comm, cin, cout, csem, grid)

        @pl.when((pl.program_id(2) == 0) & (pl.program_id(3) == 0))
        def _():
            dk_acc[...] = jnp.zeros_like(dk_acc)
            dv_acc[...] = jnp.zeros_like(dv_acc)

        kv, vv = k_ref[...], v_ref[...]
        for r in range(tq // sub):
            rows = slice(r * sub, (r + 1) * sub)
            qv, dov = q_ref[rows, :], do_ref[rows, :]
            delta = jnp.sum(dov.astype(F32) * o_ref[rows, :].astype(F32), axis=-1, keepdims=True)
            delta_row = jnp.broadcast_to(delta, (sub, HP)).T[0:1, :]
            lse_row = lse_ref[rows, :].T[0:1, :]
            pt = jnp.exp(_dot_nt(kv, qv) - lse_row)
            dst = (pt * (_dot_nt(vv, dov) - delta_row)).astype(BF16)
            p_s[:, rows] = pt.astype(BF16)
            ds_s[:, rows] = dst
            dq_o[rows, :] = _dot_tn(dst, kv).astype(dq_o.dtype)
        dk_acc[...] += _dot_nn(ds_s[...], q_ref[...])
        dv_acc[...] += _dot_nn(p_s[...], do_ref[...])

        @pl.when((pl.program_id(2) == grp - 1) & (pl.program_id(3) == nq - 1))
        def _():
            dk_o[...] = dk_acc[...].astype(dk_o.dtype)
            dv_o[...] = dv_acc[...].astype(dv_o.dtype)

        _comm_finish(comm, cin, cout, csem, grid)

    qspec = pl.BlockSpec((tq, HP), lambda b, hk, g, i: (b * nq + i, hk * grp + g))
    kspec = pl.BlockSpec((s_len, HP), lambda b, hk, g, i: (b, hk))
    return pl.pallas_call(
        body, name=name, grid=grid,
        out_shape=(jax.ShapeDtypeStruct((t, n_h * HP), BF16), jax.ShapeDtypeStruct((t, n_hk * HP), BF16),
                   jax.ShapeDtypeStruct((t, n_hk * HP), BF16), *c_outs),
        in_specs=[qspec, kspec, kspec, qspec, qspec, qspec, *c_in_specs],
        out_specs=(qspec, kspec, kspec, *([_ANY] * len(c_outs))),
        scratch_shapes=[pltpu.VMEM((s_len, tq), BF16), pltpu.VMEM((s_len, tq), BF16),
                        pltpu.VMEM((s_len, HP), F32), pltpu.VMEM((s_len, HP), F32), *c_sems],
        input_output_aliases=alias,
        compiler_params=_params(("arbitrary", "arbitrary", "arbitrary", "arbitrary")),
    )(q, k, v, do, o, lse, *c_ins)


def prep_bwd(dqa, dka, dva, dqb, dkb, dvb, zp, dz, tabs, g_qa, g_kva, g_qn, g_kn, w_qb_t, w_kvb_t, tm, s_len):
    t = zp.shape[0]
    nsb = s_len // tm
    scale_a = (QK_NOPE + QK_ROPE) ** -0.5
    scale_b = HD_B ** -0.5

    def body(dqa_ref, dka_ref, dva_ref, dqb_ref, dkb_ref, dvb_ref, qb_ref, qlat_ref, kb_ref, ckv_ref, tab_ref,
             gqa_ref, gkva_ref, gqn_ref, gkn_ref, wqb_ref, wkvb_ref, _, dz_o, dqap_o, dkva_o, st_o):
        dzq_o, dsm_o = dz_o.at[:, 0:1024], dz_o.at[:, 1024:2048]

        @pl.when(pl.program_id(0) == 0)
        def _():
            st_o[...] = jnp.zeros_like(st_o)

        ca, s1a, s2a = tab_ref[0], tab_ref[1], tab_ref[2]
        ck = tab_ref[3]
        cb, s1b, s2b = tab_ref[4], tab_ref[5], tab_ref[6]
        for h in range(H_A):
            sl = slice(h * HP, (h + 1) * HP)
            dqap_o[:, sl] = _rope_bwd(dqa_ref[:, sl].astype(F32) * scale_a, ca, s1a, s2a).astype(BF16)
        dcq = _dot_nn(dqap_o[...], wqb_ref[...])
        ql = qlat_ref[...].astype(F32)
        rq = _rstd(ql, Q_LORA)
        xh = ql * rq
        gqa = gqa_ref[...]
        st_o[0:1, :] += _colsum(dcq * xh)
        dsm_o[:, 0:256] = _rms_bwd(dcq, xh, rq, gqa, Q_LORA).astype(BF16)
        dkpe = jnp.zeros((tm, HP), F32)
        for h in range(H_A):
            sl = slice(h * HP, (h + 1) * HP)
            dk = dka_ref[:, sl]
            dkpe = dkpe + dk.astype(F32)
            dkva_o[:, sl] = dk.astype(BF16)
        dkva_o[:, H_A * HP:] = dva_ref[...].astype(BF16)
        dsm_o[:, 896:1024] = _rope_bwd(dkpe, ck, s1a, s2a).astype(BF16)
        dckv = _dot_nn(dkva_o[...], wkvb_ref[...])
        cr = ckv_ref[...].astype(F32)
        rk = _rstd(cr, KV_LORA)
        xh = cr * rk
        st_o[1:2, 0:128] += _colsum(dckv * xh)
        dsm_o[:, 768:896] = _rms_bwd(dckv, xh, rk, gkva_ref[...], KV_LORA).astype(BF16)
        gqn, gkn = gqn_ref[...], gkn_ref[...]
        dgq = jnp.zeros((1, HP), F32)
        for h in range(H_B):
            sl = slice(h * HP, (h + 1) * HP)
            dy = _rope_bwd(dqb_ref[:, sl].astype(F32) * scale_b, cb, s1b, s2b)
            xs = qb_ref[:, sl].astype(F32)
            r = _rstd(xs, HD_B)
            xh = xs * r
            dgq = dgq + _colsum(dy * xh)
            dzq_o[:, sl] = _rms_bwd(dy, xh, r, gqn, HD_B).astype(BF16)
        st_o[2:3, 0:128] += dgq
        dgk = jnp.zeros((1, HP), F32)
        for h in range(KV_B):
            sl = slice(h * HP, (h + 1) * HP)
            dy = _rope_bwd(dkb_ref[:, sl].astype(F32), cb, s1b, s2b)
            xs = kb_ref[:, sl].astype(F32)
            r = _rstd(xs, HD_B)
            xh = xs * r
            dgk = dgk + _colsum(dy * xh)
            dsm_o[:, 256 + h * HP:256 + (h + 1) * HP] = _rms_bwd(dy, xh, r, gkn, HD_B).astype(BF16)
        st_o[3:4, 0:128] += dgk
        dsm_o[:, 512:768] = dvb_ref[...].astype(BF16)

    return pl.pallas_call(
        body, name="prep_bwd", grid=(t // tm,),
        out_shape=(jax.ShapeDtypeStruct((t, ZP), BF16), jax.ShapeDtypeStruct((t, 1024), BF16),
                   jax.ShapeDtypeStruct((t, 2048), BF16), jax.ShapeDtypeStruct((4, 256), F32)),
        in_specs=[_rows(tm, 1024), _rows(tm, 1024), _rows(tm, 1024), _rows(tm, 1024), _rows(tm, 256), _rows(tm, 256),
                  _rows(tm, 1024, 0), _rows(tm, 256, 4), _rows(tm, 256, 5), _rows(tm, 128, 14),
                  pl.BlockSpec((7, tm, HP), lambda i: (0, i % nsb, 0)),
                  _resident((1, Q_LORA)), _resident((1, KV_LORA)), _resident((1, HP)), _resident((1, HP)),
                  _resident((H_A * HP, Q_LORA)), _resident((2 * H_A * HP, KV_LORA)), _ANY],
        out_specs=(_rows(tm, 2048, 1), _rows(tm, 1024), _rows(tm, 2048), pl.BlockSpec((4, 256), lambda i: (0, 0))),
        input_output_aliases={17: 0}, compiler_params=_params(("arbitrary",)),
    )(dqa, dka, dva, dqb, dkb, dvb, zp, zp, zp, zp, tabs, g_qa, g_kva, g_qn, g_kn, w_qb_t, w_kvb_t, dz)


def in_bwd(dz, x, dx1, g_mix, w_in_t, tm, comm=None):
    t = x.shape[0]
    grid = (t // tm,)
    c_ins, c_in_specs, c_outs, c_sems, alias = _comm_parts(comm, 5, 2)

    def body(*refs):
        (dz_ref, x_ref, dx1_ref, g_ref, w_ref), cin, (dx_o, st_o), cout, _, csem = _split_refs(refs, 5, 2, 0, comm)
        _comm_start(comm, cin, cout, csem, grid)

        @pl.when(pl.program_id(0) == 0)
        def _():
            st_o[...] = jnp.zeros_like(st_o)

        dh = _dot_nn(dz_ref[...], w_ref[...])
        xv = x_ref[...]
        g = g_ref[...]
        r = _rstd(xv, D_MODEL)
        xh = xv * r
        st_o[...] += _colsum(dh * xh)
        dx_o[...] = dx1_ref[...] + _rms_bwd(dh, xh, r, g, D_MODEL)
        _comm_finish(comm, cin, cout, csem, grid)

    return pl.pallas_call(
        body, name="in_bwd", grid=grid,
        out_shape=(jax.ShapeDtypeStruct((t, D_MODEL), F32), jax.ShapeDtypeStruct((1, D_MODEL), F32), *c_outs),
        in_specs=[_rows(tm, ZP), _rows(tm, D_MODEL), _rows(tm, D_MODEL),
                  _resident((1, D_MODEL)), _resident((ZP, D_MODEL)), *c_in_specs],
        out_specs=(_rows(tm, D_MODEL), pl.BlockSpec((1, D_MODEL), lambda i: (0, 0)), *([_ANY] * len(c_outs))),
        scratch_shapes=c_sems, input_output_aliases=alias,
        compiler_params=_params(("arbitrary",)),
    )(dz, x, dx1, g_mix, w_in_t, *c_ins)


def matmul_tn(a, b, name):
    t, m = a.shape
    n = b.shape[1]
    bm = min(m, 512)

    def body(a_ref, b_ref, o_ref):
        o_ref[...] = _dot_tn(a_ref[...].astype(BF16), b_ref[...].astype(BF16)).astype(BF16)

    return pl.pallas_call(
        body, name=name, grid=(m // bm,), out_shape=jax.ShapeDtypeStruct((m, n), BF16),
        in_specs=[pl.BlockSpec((t, bm), lambda i: (0, i)), pl.BlockSpec((t, n), lambda i: (0, 0))],
        out_specs=pl.BlockSpec((bm, n), lambda i: (i, 0)),
        compiler_params=_params(("parallel",)),
    )(a, b)


def matmul_tn_packed(a, b, name, rows, row_off, total_rows, buf=None, square_a=False):
    t, m = a.shape
    n = b.shape[1]
    pd = max(1, 512 // rows)
    bm = pd * rows
    tk = min(t, 4096)
    nk = t // tk

    def body(a_ref, b_ref, *rest):
        o_ref, acc = rest[-2], rest[-1]

        @pl.when(pl.program_id(1) == 0)
        def _():
            acc[...] = jnp.zeros_like(acc)

        av = a_ref[...]
        if square_a:
            av = (av.astype(F32) * av.astype(F32))
        acc[...] += _dot_tn(av.astype(BF16), b_ref[...].astype(BF16))

        @pl.when(pl.program_id(1) == nk - 1)
        def _():
            o_ref[...] = acc[...].reshape(pd, rows, n).astype(o_ref.dtype)

    in_specs = [pl.BlockSpec((tk, bm), lambda i, kk: (kk, i)), pl.BlockSpec((tk, n), lambda i, kk: (kk, 0))]
    args = [a, b]
    if buf is not None:
        in_specs.append(_ANY)
        args.append(buf)
    return pl.pallas_call(
        body, name=name, grid=(m // bm, nk), out_shape=jax.ShapeDtypeStruct((N_DEV, total_rows, n), BF16),
        in_specs=in_specs, out_specs=pl.BlockSpec((pd, rows, n), lambda i, kk: (i, row_off // rows, 0)),
        scratch_shapes=[pltpu.VMEM((bm, n), F32)], input_output_aliases={2: 0} if buf is not None else {},
        compiler_params=_params(("parallel", "arbitrary")),
    )(*args)


def adamw(w, g, m, v, name, g_transposed=False):
    _, r, c = w.shape
    tr = 256 if (not g_transposed and r > 256 and r % 256 == 0) else r
    c1 = 1.0 - ADAM_B1 ** ADAM_STEP
    c2 = 1.0 - ADAM_B2 ** ADAM_STEP

    def body(w_ref, g_ref, m_ref, v_ref, g_o, d_o, m_o, v_o):
        gv = g_ref[...].T if g_transposed else g_ref[...]
        mn = ADAM_B1 * m_ref[0] + (1.0 - ADAM_B1) * gv
        vn = ADAM_B2 * v_ref[0] + (1.0 - ADAM_B2) * (gv * gv)
        g_o[0] = gv
        m_o[0] = mn
        v_o[0] = vn
        d_o[0] = -ADAM_LR * ((mn / c1) / (jnp.sqrt(vn / c2) + ADAM_EPS) + ADAM_WD * w_ref[0])

    spec = pl.BlockSpec((1, tr, c), lambda i: (0, i, 0))
    gspec = pl.BlockSpec((c, r), lambda i: (0, 0)) if g_transposed else pl.BlockSpec((tr, c), lambda i: (i, 0))
    shp = jax.ShapeDtypeStruct((1, r, c), F32)
    return pl.pallas_call(
        body, name=name, grid=(r // tr,), out_shape=(shp,) * 4, in_specs=[spec, gspec, spec, spec], out_specs=(spec,) * 4,
        compiler_params=_params(("parallel",)),
    )(w, g, m, v)


def _rope_tables(s_len):
    def angles(pos, dim):
        inv = np.float32(ROPE_THETA) ** (-np.arange(0, dim, 2, dtype=np.float32) / np.float32(dim))
        return pos.astype(np.float32)[:, None] * inv[None, :]

    tpos = np.arange(s_len)
    a1 = angles(tpos, QK_ROPE)
    ar = angles(tpos // GRID_W, HD_B // 2)
    ac = angles(tpos % GRID_W, HD_B // 2)
    z16 = np.zeros((s_len, 16), np.float32)
    z32 = np.zeros((s_len, 32), np.float32)
    z64 = np.zeros((s_len, 64), np.float32)
    one64 = np.ones((s_len, 64), np.float32)
    c1, s1 = np.cos(a1), np.sin(a1)
    ca = np.concatenate([one64, c1, c1, z32], axis=1)
    ck = np.concatenate([z64, c1, c1, z32], axis=1)
    s1a = np.concatenate([z64, -s1, z16, z32], axis=1)
    s2a = np.concatenate([z64, z16, s1, z32], axis=1)
    cr, sr, cc, sc = np.cos(ar), np.sin(ar), np.cos(ac), np.sin(ac)
    cb = np.concatenate([cr, cr, cc, cc, z64], axis=1)
    s1b = np.concatenate([-sr, z16, -sc, z16, z64], axis=1)
    s2b = np.concatenate([z16, sr, z16, sc, z64], axis=1)
    return jnp.asarray(np.stack([ca, s1a, s2a, ck, cb, s1b, s2b]).astype(np.float32))


def _pad_heads(a, n_heads, axis):
    shp = a.shape
    a = a.reshape(shp[:axis] + (n_heads, shp[axis] // n_heads) + shp[axis + 1:])
    pad = [(0, 0)] * a.ndim
    pad[axis + 1] = (0, HP - a.shape[axis + 1])
    a = jnp.pad(a, pad)
    return a.reshape(shp[:axis] + (n_heads * HP,) + shp[axis + 1:])


def _unpad_heads(a, n_heads, width, axis):
    shp = a.shape
    a = a.reshape(shp[:axis] + (n_heads, HP) + shp[axis + 1:])
    a = lax.slice_in_dim(a, 0, width, axis=axis + 1)
    return a.reshape(shp[:axis] + (n_heads * width,) + shp[axis + 1:])


def _pack_rows(blocks, names):
    parts = []
    for name in names:
        b = blocks[name]
        padr = PACK_ROWS[name] - b.shape[-2]
        if padr:
            b = jnp.pad(b, [(0, 0)] * (b.ndim - 2) + [(0, padr), (0, 0)])
        parts.append(b)
    return jnp.concatenate(parts, axis=parts[0].ndim - 2)


def _expand_w_in(wt):
    z64 = jnp.zeros((64, D_MODEL), wt.dtype)
    z32 = jnp.zeros((32, D_MODEL), wt.dtype)
    return jnp.concatenate([
        wt[1184:2208], wt[2208:3232], _pad_heads(wt[416:928], H_B, 0), wt[0:256],
        _pad_heads(wt[928:1056], KV_B, 0), _pad_heads(wt[1056:1184], KV_B, 0), wt[256:384],
        z64, wt[384:416], z32], axis=0)


def _collapse_w_in(dw):
    dg, dq, ds = dw[0:2048], dw[2048:3072], dw[3072:4096]
    return jnp.concatenate([
        ds[0:256], ds[768:896], ds[960:992], _unpad_heads(dq, H_B, HD_B, 0), _unpad_heads(ds[256:512], KV_B, HD_B, 0),
        _unpad_heads(ds[512:768], KV_B, HD_B, 0), dg], axis=0)


def kernel(x, p, g_mix, w_in, g_qa, w_qb, g_kva, w_kvb, g_qn, g_kn, w_oa, w_ob, w_o, g_mlp, w_up, w_down, g_ple, w_ple_gate, w_ple, g_final, loss_target, m_g_mix, m_w_in, m_g_qa, m_w_qb, m_g_kva, m_w_kvb, m_g_qn, m_g_kn, m_w_oa, m_w_ob, m_w_o, m_g_mlp, m_w_up, m_w_down, m_g_ple, m_w_ple_gate, m_w_ple, m_g_final, v_g_mix, v_w_in, v_g_qa, v_w_qb, v_g_kva, v_w_kvb, v_g_qn, v_g_kn, v_w_oa, v_w_ob, v_w_o, v_g_mlp, v_w_up, v_w_down, v_g_ple, v_w_ple_gate, v_w_ple, v_g_final):
    n_b, s_len, _ = x.shape
    t = n_b * s_len
    tm = min(512, s_len)
    tq_f = min(2048, s_len)
    tq_b = min(2048, s_len)

    mats = dict(w_in=(w_in, m_w_in, v_w_in), w_qb=(w_qb, m_w_qb, v_w_qb), w_kvb=(w_kvb, m_w_kvb, v_w_kvb),
                w_oa=(w_oa, m_w_oa, v_w_oa), w_ob=(w_ob, m_w_ob, v_w_ob), w_o=(w_o, m_w_o, v_w_o),
                w_up=(w_up, m_w_up, v_w_up), w_down=(w_down, m_w_down, v_w_down),
                w_ple_gate=(w_ple_gate, m_w_ple_gate, v_w_ple_gate), w_ple=(w_ple, m_w_ple, v_w_ple))
    col_sharded = ("w_in", "w_qb", "w_kvb", "w_oa", "w_ob", "w_up", "w_ple")

    blocks = {}
    for name in PACK_W1 + ("w_oa", "w_ob", "w_ple"):
        blocks[name] = mats[name][0][0].T.reshape(-1, D_MODEL).astype(BF16)
    off_w1, _ = _pack_offsets(PACK_W1)
    off_w2, _ = _pack_offsets(PACK_W2)
    off_w3, _ = _pack_offsets(PACK_W3)
    xf = x.reshape(t, D_MODEL)
    h, full1 = norm_x(xf, g_mix, tm, comm=gather3_first_comm(_pack_rows(blocks, PACK_W1)))
    pack2, pack3, full1 = pack_late_weights(w_up, w_down, w_o, w_ple_gate, _pack_rows(blocks, ("w_oa", "w_ob", "w_ple")),
                                            comm=gather3_second_comm(full1, then_third=True))

    def gathered(full, offs, name, rows, width):
        return full[:, offs[name]:offs[name] + rows].reshape(-1, width)

    w_in_t = _expand_w_in(gathered(full1, off_w1, "w_in", 404, D_MODEL))
    w_qb_t = _pad_heads(gathered(full1, off_w1, "w_qb", 24, Q_LORA), H_A, 0)
    wkvb = gathered(full1, off_w1, "w_kvb", 16, KV_LORA).reshape(H_A, 2, 64, KV_LORA)
    w_kvb_t = jnp.concatenate([_pad_heads(wkvb[:, 0].reshape(-1, KV_LORA), H_A, 0),
                               _pad_heads(wkvb[:, 1].reshape(-1, KV_LORA), H_A, 0)], axis=0)

    tabs = _rope_tables(s_len)
    g_qn_p = jnp.pad(g_qn, ((0, 0), (0, HP - HD_B)))
    g_kn_p = jnp.pad(g_kn, ((0, 0), (0, HP - HD_B)))
    pf = p.reshape(t, PLE_DIM)
    tgt = loss_target.reshape(t, D_MODEL)

    zg, zs, full2 = in_proj(h, w_in_t, tm, comm=gather3_first_comm(pack2))
    qa, ka, va, qb, kb, vb, cq, ckv, full2 = attn_prep(zs, tabs, g_qa, g_kva, g_qn_p, g_kn_p, w_qb_t, w_kvb_t, tm, s_len,
                                                       comm=gather3_second_comm(full2))
    oa, lse_a, full3, full2 = attn_fwd(qa, ka, va, n_b, s_len, tq_f, "attn_a_fwd",
                                       comm=_join_comms(gather_first_comm(pack3), gather3_third_comm(full2)))
    ob, lse_b, full3 = attn_fwd(qb, kb, vb, n_b, s_len, tq_f, "attn_b_fwd", comm=gather_pass_comm(full3))
    w_oa_t = _pad_heads(gathered(full2, off_w2, "w_oa", 64, H_A * V_DIM_A), H_A, 1)
    w_ob_t = _pad_heads(gathered(full2, off_w2, "w_ob", 64, H_B * HD_B), H_B, 1)
    w_ple_t = gathered(full2, off_w2, "w_ple", 32, PLE_DIM)
    x1, merged, yab = merge_fwd(oa, ob, zg, xf, w_oa_t, w_ob_t, full2, off_w2, tm)
    x2, u = mlp_fwd(x1, g_mlp, full3, off_w3, tm)
    dx2, dt, h3, dpe, st_ple, dx2b = ple_loss_bwd(x2, pf, tgt, g_ple, g_final.reshape(1, D_MODEL), full2, off_w2, w_ple_t, tm)
    dx1, da, h2, st_mlp, dx1b = mlp_bwd(dx2, x1, u, g_mlp, full3, off_w3, tm)

    core = lax.axis_index("c").astype(jnp.int32).reshape(1)
    chip = (2 * lax.axis_index("x") + lax.axis_index("y")).astype(jnp.int32).reshape(1)

    def packed(gblocks, names):
        return _pack_rows({n: gblocks[n].reshape(N_DEV, -1, D_MODEL).astype(BF16) for n in names}, names)

    off_g1, rows_g1 = _pack_offsets(PACK_G1)
    gpack1 = matmul_tn_packed(da, h2, "gw_up", 512, off_g1["w_up"], rows_g1)
    gpack1 = matmul_tn_packed(u, dx2b, "gw_down", 512, off_g1["w_down"], rows_g1, buf=gpack1, square_a=True)
    gpack1 = matmul_tn_packed(h3, dt, "gw_pg", 128, off_g1["w_ple_gate"], rows_g1, buf=gpack1)
    gple = matmul_tn(dpe, pf, "gw_ple").reshape(N_DEV, -1, D_MODEL).astype(BF16)
    gpack1 = lax.dynamic_update_slice(gpack1, gple, (0, off_g1["w_ple"], 0))
    doa, dob, dz, dya, dyb, got1 = merge_bwd(dx1b, yab, zg, w_oa_t, w_ob_t, full2, off_w2, tm,
                                               comm=scatter_sibling_comm(gpack1))
    part1 = add_pairs(gpack1, got1, core)

    off_g2, rows_g2 = _pack_offsets(PACK_G2)
    g2 = dict(w_oa=_unpad_heads(matmul_tn(dya, oa, "gw_oa"), H_A, V_DIM_A, 1),
              w_ob=_unpad_heads(matmul_tn(dyb, ob, "gw_ob"), H_B, HD_B, 1))
    gpack2 = matmul_tn_packed(merged, dx1b, "gw_o", 128, off_g2["w_o"], rows_g2)
    gpack2 = lax.dynamic_update_slice(gpack2, packed(g2, ("w_oa", "w_ob")), (0, off_g2["w_oa"], 0))
    dqa, dka, dva, land1, got2 = attn_bwd(qa, ka, va, doa, oa, lse_a, n_b, s_len, tq_b, "attn_a_bwd",
                                          comm=_join_comms(scatter_chips_comm(part1), scatter_sibling_comm(gpack2)))
    gshard1 = sum_chips(part1, land1, chip)
    part2 = add_pairs(gpack2, got2, core)
    dqb, dkb, dvb, land2 = attn_bwd(qb, kb, vb, dob, ob, lse_b, n_b, s_len, tq_b, "attn_b_bwd", comm=scatter_chips_comm(part2))
    gshard2 = sum_chips(part2, land2, chip)
    dz, dqap, dkva, st_prep = prep_bwd(dqa, dka, dva, dqb, dkb, dvb, zs, dz, tabs, g_qa, g_kva, g_qn_p, g_kn_p,
                                       w_qb_t, w_kvb_t, tm, s_len)

    gkv = matmul_tn(dkva, ckv, "gw_kvb")
    g3 = dict(
        w_in=_collapse_w_in(matmul_tn(dz, h, "gw_in")),
        w_qb=_unpad_heads(matmul_tn(dqap, cq, "gw_qb"), H_A, QK_NOPE + QK_ROPE, 0),
        w_kvb=jnp.stack([_unpad_heads(gkv[:H_A * HP], H_A, 64, 0).reshape(H_A, 64, KV_LORA),
                         _unpad_heads(gkv[H_A * HP:], H_A, 64, 0).reshape(H_A, 64, KV_LORA)], axis=1))
    gpack3 = packed(g3, PACK_G3)
    part3 = add_pairs(gpack3, exchange_sibling(gpack3), core)
    grad_x, st_mix, land3 = in_bwd(dz, xf, dx1, g_mix, w_in_t, tm, comm=scatter_chips_comm(part3))
    gshard3 = sum_chips(part3, land3, chip)
    off_g3, _ = _pack_offsets(PACK_G3)
    shards = {n: (gshard1, off_g1[n]) for n in PACK_G1}
    shards.update({n: (gshard2, off_g2[n]) for n in PACK_G2})
    shards.update({n: (gshard3, off_g3[n]) for n in PACK_G3})

    stats = allreduce_stats(st_mix, st_prep, st_mlp, st_ple)
    loss = jnp.sum(stats[ST_LOSS])

    out_g, out_d, out_m, out_v = {}, {}, {}, {}
    for name, (w, m, v) in mats.items():
        gshard, off = shards[name]
        r, c = w.shape[1:]
        if name in col_sharded:
            g2 = gshard[off:off + (r * c) // D_MODEL].reshape(c, r)
            if r % 128 == 0 and c % 128 == 0:
                res = adamw(w, g2, m, v, "adamw_" + name, g_transposed=True)
            else:
                res = adamw(w[0].T[None], g2, m[0].T[None], v[0].T[None], "adamw_" + name)
                res = tuple(a[0].T[None] for a in res)
        else:
            res = adamw(w, gshard[off:off + r], m, v, "adamw_" + name)
        out_g[name], out_d[name], out_m[name], out_v[name] = res

    gains = (("g_mix", g_mix, m_g_mix, v_g_mix, ST_G_MIX), ("g_qa", g_qa, m_g_qa, v_g_qa, ST_G_QA),
             ("g_kva", g_kva, m_g_kva, v_g_kva, ST_G_KVA), ("g_qn", g_qn, m_g_qn, v_g_qn, ST_G_QN),
             ("g_kn", g_kn, m_g_kn, v_g_kn, ST_G_KN), ("g_mlp", g_mlp, m_g_mlp, v_g_mlp, ST_G_MLP),
             ("g_ple", g_ple, m_g_ple, v_g_ple, ST_G_PLE), ("g_final", g_final, m_g_final, v_g_final, ST_G_FINAL))
    res = adamw_gains(stats, [(r_, w.reshape(1, -1), m.reshape(1, -1), v.reshape(1, -1)) for _, w, m, v, r_ in gains])
    for (name, w, _, _, _), (gg, gd, gm, gv) in zip(gains, res):
        out_g[name], out_d[name], out_m[name], out_v[name] = (a.reshape(w.shape) for a in (gg, gd, gm, gv))

    order = ("g_mix", "w_in", "g_qa", "w_qb", "g_kva", "w_kvb", "g_qn", "g_kn", "w_oa", "w_ob", "w_o", "g_mlp",
             "w_up", "w_down", "g_ple", "w_ple_gate", "w_ple", "g_final")
    return (loss, grad_x.reshape(x.shape), *[out_g[n] for n in order], *[out_d[n] for n in order],
            *[out_m[n] for n in order], *[out_v[n] for n in order])
```

```python
import numpy as np
import jax
import jax.numpy as jnp
from jax import lax
from jax.experimental import pallas as pl
from jax.experimental.pallas import tpu as pltpu

F32 = jnp.float32
BF16 = jnp.bfloat16

D_MODEL = 1024
EPS = 1e-6
ROPE_THETA = 10000.0
GRID_W = 64
H_A = 8
QK_NOPE = 64
QK_ROPE = 32
V_DIM_A = 64
Q_LORA = 256
KV_LORA = 128
H_B = 8
KV_B = 2
HD_B = 64
D_FF = 4 * D_MODEL
PLE_DIM = 256
HP = 128
ZP = 4096
N_DEV = 8
N_CHIP = 4

ADAM_LR = 0.001
ADAM_B1 = 0.9
ADAM_B2 = 0.999
ADAM_EPS = 1e-08
ADAM_WD = 0.01
ADAM_STEP = 10

VMEM_LIMIT = 52 * 1024 * 1024

PACK_ROWS = dict(w_in=416, w_qb=32, w_kvb=16, w_oa=64, w_ob=64, w_o=128, w_up=512, w_down=512, w_ple_gate=128, w_ple=32)
PACK_W1 = ("w_in", "w_qb", "w_kvb")
PACK_W2 = ("w_o", "w_ple_gate", "w_oa", "w_ob", "w_ple")
PACK_W3 = ("w_up", "w_down")
PACK_G1 = ("w_up", "w_down", "w_ple_gate", "w_ple")
PACK_G2 = ("w_o", "w_oa", "w_ob")
PACK_G3 = ("w_in", "w_qb", "w_kvb")


def _pack_offsets(names):
    off, o = {}, 0
    for n in names:
        off[n] = o
        o += PACK_ROWS[n]
    return off, o

ST_G_MIX, ST_G_QA, ST_G_KVA, ST_G_QN, ST_G_KN, ST_G_MLP, ST_G_PLE, ST_G_FINAL, ST_LOSS = range(9)
ST_ROWS = 16


def _dot_nn(a, b):
    return lax.dot_general(a, b, (((1,), (0,)), ((), ())), preferred_element_type=F32)


def _dot_nt(a, b):
    return lax.dot_general(a, b, (((1,), (1,)), ((), ())), preferred_element_type=F32)


def _dot_tn(a, b):
    return lax.dot_general(a, b, (((0,), (0,)), ((), ())), preferred_element_type=F32)


def _rstd(x, n):
    return lax.rsqrt(jnp.sum(x * x, axis=-1, keepdims=True) * (1.0 / n) + EPS)


def _rms_bwd(dy, xh, r, g, n):
    dxh = dy * g
    return r * (dxh - xh * (jnp.sum(dxh * xh, axis=-1, keepdims=True) * (1.0 / n)))


def _rope_fwd(x, c, s1, s2):
    return x * c + pltpu.roll(x, HP - 16, 1) * s1 + pltpu.roll(x, 16, 1) * s2


def _rope_bwd(d, c, s1, s2):
    return d * c + pltpu.roll(d * s1, 16, 1) + pltpu.roll(d * s2, HP - 16, 1)


def _colsum(v):
    return jnp.sum(v, axis=0, keepdims=True)


def _params(sem=None, vmem=VMEM_LIMIT):
    return pltpu.CompilerParams(dimension_semantics=sem, vmem_limit_bytes=vmem)


def _resident(shape):
    nd = len(shape)
    return pl.BlockSpec(shape, lambda *_: (0,) * nd, pipeline_mode=pl.Buffered(1))


def _rows(tm, width, col=0):
    return pl.BlockSpec((tm, width), lambda i: (i, col))


def _packed_weight(rows, off):
    return pl.BlockSpec((N_DEV, rows, D_MODEL), lambda *_: (0, off // rows, 0), pipeline_mode=pl.Buffered(1))


def _wrows(ref, start, size):
    rows = ref.shape[1]
    return ref[start // rows:(start + size) // rows].reshape(size, D_MODEL)


def _mesh_pos():
    return lax.axis_index("x"), lax.axis_index("y"), lax.axis_index("c")


def _flip(v, bit):
    return (1 - v) if bit else v


_ANY = pl.BlockSpec(memory_space=pl.ANY)
_MESH = pl.DeviceIdType.MESH


def _remote(src, dst, send_sems, recv_sems, k, to):
    return pltpu.make_async_remote_copy(src_ref=src, dst_ref=dst, send_sem=send_sems.at[k], recv_sem=recv_sems.at[k],
                                        device_id=to, device_id_type=_MESH)


def _sibling_copies(g_ref, got_ref, send_sems, recv_sems):
    x, y, c = _mesh_pos()
    return [_remote(g_ref.at[2 * j + (1 - c)], got_ref.at[j], send_sems, recv_sems, j, (x, y, 1 - c)) for j in range(N_CHIP)]


def _chip_copies(p_ref, land_ref, send_sems, recv_sems):
    x, y, c = _mesh_pos()
    copies = []
    for k in (1, 2, 3):
        tx, ty = _flip(x, k & 2), _flip(y, k & 1)
        copies.append(_remote(p_ref.at[2 * tx + ty], land_ref.at[k - 1], send_sems, recv_sems, k - 1, (tx, ty, c)))
    return copies


class _Comm:
    def __init__(self, ins, out_shapes, sems, make, aliases=None, make_tail=None):
        self.ins, self.out_shapes, self.sems, self.make, self.aliases = list(ins), list(out_shapes), list(sems), make, aliases or {}
        self.make_tail = make_tail


def _join_comms(a, b):
    assert a.make_tail is None and b.make_tail is None
    n_i, n_o, n_s = len(a.ins), len(a.out_shapes), len(a.sems)

    def make(cin, cout, sems):
        return a.make(cin[:n_i], cout[:n_o], sems[:n_s]) + b.make(cin[n_i:], cout[n_o:], sems[n_s:])

    aliases = dict(a.aliases)
    aliases.update({n_i + j: n_o + k for j, k in b.aliases.items()})
    return _Comm(a.ins + b.ins, a.out_shapes + b.out_shapes, a.sems + b.sems, make, aliases)


def _comm_parts(comm, n_in, n_out):
    if comm is None:
        return [], [], [], [], {}
    alias = {n_in + j: n_out + k for j, k in comm.aliases.items()}
    return comm.ins, [_ANY] * len(comm.ins), comm.out_shapes, comm.sems, alias


def _split_refs(refs, n_in, n_out, n_scratch, comm):
    n_ci = len(comm.ins) if comm else 0
    n_co = len(comm.out_shapes) if comm else 0
    cuts, i = [], 0
    for n in (n_in, n_ci, n_out, n_co, n_scratch):
        cuts.append(refs[i:i + n])
        i += n
    return (*cuts, refs[i:])


def _grid_edge(grid, last):
    cond = None
    for d, n in enumerate(grid):
        here = pl.program_id(d) == (n - 1 if last else 0)
        cond = here if cond is None else cond & here
    return cond


def _comm_start(comm, cin, cout, csem, grid):
    if comm is not None:
        @pl.when(_grid_edge(grid, False))
        def _():
            for cp in comm.make(cin, cout, csem):
                cp.start()


def _comm_finish(comm, cin, cout, csem, grid):
    if comm is not None:
        @pl.when(_grid_edge(grid, True))
        def _():
            for cp in comm.make(cin, cout, csem):
                cp.wait()
            if comm.make_tail is not None:
                tail = comm.make_tail(cin, cout, csem)
                for cp in tail:
                    cp.start()
                for cp in tail:
                    cp.wait()


def gather_first_comm(shard):
    r, w = shard.shape

    def make(cin, cout, sems):
        (x_ref,), (out_ref,), (send_sems, recv_sems, local_sem) = cin, cout, sems
        x, y, c = _mesh_pos()
        mine = out_ref.at[4 * x + 2 * y + c]
        targets = [(x, y, 1 - c), (1 - x, y, c), (x, 1 - y, c), (1 - x, 1 - y, c)]
        return [_remote(x_ref, mine, send_sems, recv_sems, k, to) for k, to in enumerate(targets)] + [
            pltpu.make_async_copy(x_ref, mine, local_sem)]

    return _Comm([shard], [jax.ShapeDtypeStruct((N_DEV, r, w), shard.dtype)],
                 [pltpu.SemaphoreType.DMA((4,)), pltpu.SemaphoreType.DMA((4,)), pltpu.SemaphoreType.DMA], make)


def gather_pass_comm(full):
    def make(cin, cout, sems):
        (in_ref,), (out_ref,), (send_sems, recv_sems) = cin, cout, sems
        x, y, c = _mesh_pos()
        copies = []
        for k, (px, py) in enumerate([(1 - x, y), (x, 1 - y), (1 - x, 1 - y)]):
            idx = 4 * px + 2 * py + c
            copies.append(_remote(in_ref.at[idx], out_ref.at[idx], send_sems, recv_sems, k, (x, y, 1 - c)))
        return copies

    return _Comm([full], [jax.ShapeDtypeStruct(full.shape, full.dtype)],
                 [pltpu.SemaphoreType.DMA((3,)), pltpu.SemaphoreType.DMA((3,))], make, aliases={0: 0})


def gather3_first_comm(shard):
    r, w = shard.shape

    def make(cin, cout, sems):
        (x_ref,), (out_ref,), (send_sems, recv_sems, local_sem) = cin, cout, sems
        x, y, c = _mesh_pos()
        mine = out_ref.at[4 * x + 2 * y + c]
        targets = [(x, y, 1 - c), (1 - x, y, c), (x, 1 - y, c)]
        return [_remote(x_ref, mine, send_sems, recv_sems, k, to) for k, to in enumerate(targets)] + [
            pltpu.make_async_copy(x_ref, mine, local_sem)]

    return _Comm([shard], [jax.ShapeDtypeStruct((N_DEV, r, w), shard.dtype)],
                 [pltpu.SemaphoreType.DMA((3,)), pltpu.SemaphoreType.DMA((3,)), pltpu.SemaphoreType.DMA], make)


def gather3_second_comm(full, then_third=False):
    def make(cin, cout, sems):
        (in_ref,), (out_ref,), (send_sems, recv_sems) = cin, cout, sems
        x, y, c = _mesh_pos()
        copies = []
        for k, (px, py) in enumerate([(1 - x, y), (x, 1 - y)]):
            idx = 4 * px + 2 * py + c
            copies.append(_remote(in_ref.at[idx], out_ref.at[idx], send_sems, recv_sems, k, (x, y, 1 - c)))
        fx, fy = x * c + (1 - x) * (1 - c), y * (1 - c) + (1 - y) * c
        tx, ty = x * (1 - c) + (1 - x) * c, (1 - y) * (1 - c) + y * c
        idx = 4 * fx + 2 * fy + c
        copies.append(_remote(in_ref.at[idx], out_ref.at[idx], send_sems, recv_sems, 2, (tx, ty, c)))
        return copies

    def third(cin, cout, sems):
        (out_ref,), (send_sems, recv_sems) = cout, sems
        x, y, c = _mesh_pos()
        idx = 4 * (1 - x) + 2 * (1 - y) + c
        return [_remote(out_ref.at[idx], out_ref.at[idx], send_sems, recv_sems, 3, (x, y, 1 - c))]

    n_sem = 4 if then_third else 3
    return _Comm([full], [jax.ShapeDtypeStruct(full.shape, full.dtype)],
                 [pltpu.SemaphoreType.DMA((n_sem,)), pltpu.SemaphoreType.DMA((n_sem,))], make, aliases={0: 0},
                 make_tail=third if then_third else None)


def gather3_third_comm(full):
    def make(cin, cout, sems):
        (in_ref,), (out_ref,), (send_sems, recv_sems) = cin, cout, sems
        x, y, c = _mesh_pos()
        idx = 4 * (1 - x) + 2 * (1 - y) + c
        return [_remote(in_ref.at[idx], out_ref.at[idx], send_sems, recv_sems, 0, (x, y, 1 - c))]

    return _Comm([full], [jax.ShapeDtypeStruct(full.shape, full.dtype)],
                 [pltpu.SemaphoreType.DMA((1,)), pltpu.SemaphoreType.DMA((1,))], make, aliases={0: 0})


def scatter_sibling_comm(g):
    _, r, w = g.shape
    return _Comm([g], [jax.ShapeDtypeStruct((N_CHIP, r, w), g.dtype)],
                 [pltpu.SemaphoreType.DMA((N_CHIP,)), pltpu.SemaphoreType.DMA((N_CHIP,))],
                 lambda cin, cout, sems: _sibling_copies(cin[0], cout[0], sems[0], sems[1]))


def scatter_chips_comm(part):
    _, r, w = part.shape
    return _Comm([part], [jax.ShapeDtypeStruct((N_CHIP - 1, r, w), part.dtype)],
                 [pltpu.SemaphoreType.DMA((3,)), pltpu.SemaphoreType.DMA((3,))],
                 lambda cin, cout, sems: _chip_copies(cin[0], cout[0], sems[0], sems[1]))


def exchange_sibling(g):
    _, r, w = g.shape

    def body(g_ref, got_ref, send_sems, recv_sems):
        copies = _sibling_copies(g_ref, got_ref, send_sems, recv_sems)
        for cp in copies:
            cp.start()
        for cp in copies:
            cp.wait()

    return pl.pallas_call(
        body, name="exchange_sibling", out_shape=jax.ShapeDtypeStruct((N_CHIP, r, w), g.dtype),
        in_specs=[_ANY], out_specs=_ANY,
        scratch_shapes=[pltpu.SemaphoreType.DMA((N_CHIP,)), pltpu.SemaphoreType.DMA((N_CHIP,))],
    )(g)


def exchange_chips(part):
    _, r, w = part.shape

    def body(p_ref, land_ref, send_sems, recv_sems):
        copies = _chip_copies(p_ref, land_ref, send_sems, recv_sems)
        for cp in copies:
            cp.start()
        for cp in copies:
            cp.wait()

    return pl.pallas_call(
        body, name="exchange_chips", out_shape=jax.ShapeDtypeStruct((N_CHIP - 1, r, w), part.dtype),
        in_specs=[_ANY], out_specs=_ANY,
        scratch_shapes=[pltpu.SemaphoreType.DMA((3,)), pltpu.SemaphoreType.DMA((3,))],
    )(part)


def allreduce_stats(st_mix, st_prep, st_mlp, st_ple):
    def body(mix_ref, prep_ref, mlp_ref, ple_ref, out_ref, mine, gath, send_sems, recv_sems):
        x, y, c = _mesh_pos()
        me = 4 * x + 2 * y + c
        mine[...] = jnp.zeros_like(mine)
        mine[ST_G_MIX:ST_G_MIX + 1, :] = mix_ref[...]
        mine[ST_G_QA:ST_G_KN + 1, 0:256] = prep_ref[...]
        mine[ST_G_MLP:ST_G_MLP + 1, :] = mlp_ref[...]
        mine[ST_G_PLE:ST_LOSS + 1, :] = ple_ref[...]
        gath[me] = mine[...]
        copies = []
        for k in range(1, N_DEV):
            peer = (_flip(x, k & 4), _flip(y, k & 2), _flip(c, k & 1))
            copies.append(_remote(mine, gath.at[me], send_sems, recv_sems, k - 1, peer))
        for cp in copies:
            cp.start()
        for cp in copies:
            cp.wait()
        acc = gath[0]
        for d in range(1, N_DEV):
            acc = acc + gath[d]
        out_ref[...] = acc

    vm = pl.BlockSpec(memory_space=pltpu.VMEM)
    return pl.pallas_call(
        body, name="allreduce_stats", out_shape=jax.ShapeDtypeStruct((ST_ROWS, D_MODEL), F32),
        in_specs=[vm] * 4, out_specs=vm,
        scratch_shapes=[pltpu.VMEM((ST_ROWS, D_MODEL), F32), pltpu.VMEM((N_DEV, ST_ROWS, D_MODEL), F32),
                        pltpu.SemaphoreType.DMA((N_DEV - 1,)), pltpu.SemaphoreType.DMA((N_DEV - 1,))],
    )(st_mix, st_prep, st_mlp, st_ple)


def adamw_gains(stats, gains):
    c1 = 1.0 - ADAM_B1 ** ADAM_STEP
    c2 = 1.0 - ADAM_B2 ** ADAM_STEP
    n = len(gains)

    def body(st_ref, *refs):
        ins, outs = refs[:3 * n], refs[3 * n:]
        for i, (row, w, _, _) in enumerate(gains):
            width = w.shape[1]
            gv = st_ref[row:row + 1, 0:width]
            mn = ADAM_B1 * ins[3 * i + 1][...] + (1.0 - ADAM_B1) * gv
            vn = ADAM_B2 * ins[3 * i + 2][...] + (1.0 - ADAM_B2) * (gv * gv)
            outs[4 * i][...] = gv
            outs[4 * i + 1][...] = -ADAM_LR * ((mn / c1) / (jnp.sqrt(vn / c2) + ADAM_EPS) + ADAM_WD * ins[3 * i][...])
            outs[4 * i + 2][...] = mn
            outs[4 * i + 3][...] = vn

    vm = pl.BlockSpec(memory_space=pltpu.VMEM)
    flat = [a for (_, w, m, v) in gains for a in (w, m, v)]
    out_shape = tuple(jax.ShapeDtypeStruct(w.shape, F32) for (_, w, _, _) in gains for _ in range(4))
    res = pl.pallas_call(body, name="adamw_gains", out_shape=out_shape, in_specs=[vm] * (1 + 3 * n),
                         out_specs=tuple([vm] * (4 * n)))(stats, *flat)
    return [res[4 * i:4 * i + 4] for i in range(n)]


def _row_tile(r, cap=640):
    return max(d for d in range(16, min(r, cap) + 1, 16) if r % d == 0)


def add_pairs(g, got, core):
    n, r, w = got.shape
    tr = _row_tile(r)

    def body(c_ref, a_ref, b_ref, o_ref):
        o_ref[...] = (a_ref[...].astype(F32) + b_ref[...].astype(F32)).astype(o_ref.dtype)

    spec = pl.BlockSpec((1, tr, w), lambda i, j, c: (i, j, 0))
    return pl.pallas_call(
        body, name="add_pairs", out_shape=jax.ShapeDtypeStruct(got.shape, got.dtype),
        grid_spec=pltpu.PrefetchScalarGridSpec(
            num_scalar_prefetch=1, grid=(n, r // tr),
            in_specs=[pl.BlockSpec((1, tr, w), lambda i, j, c: (2 * i + c[0], j, 0)), spec], out_specs=spec),
        compiler_params=_params(("parallel", "parallel")),
    )(core, g, got)


def sum_chips(part, land, chip):
    _, r, w = part.shape
    tr = _row_tile(r)

    def body(c_ref, p_ref, l_ref, o_ref):
        acc = p_ref[0].astype(F32)
        for s in range(N_CHIP - 1):
            acc = acc + l_ref[s].astype(F32)
        o_ref[...] = acc

    return pl.pallas_call(
        body, name="sum_chips", out_shape=jax.ShapeDtypeStruct((r, w), F32),
        grid_spec=pltpu.PrefetchScalarGridSpec(
            num_scalar_prefetch=1, grid=(r // tr,),
            in_specs=[pl.BlockSpec((1, tr, w), lambda i, c: (c[0], i, 0)), pl.BlockSpec((N_CHIP - 1, tr, w), lambda i, c: (0, i, 0))],
            out_specs=pl.BlockSpec((tr, w), lambda i, c: (i, 0))),
        compiler_params=_params(("parallel",)),
    )(chip, part, land)


def norm_x(x, g_mix, tm, comm=None):
    t = x.shape[0]
    grid = (t // tm,)
    c_ins, c_in_specs, c_outs, c_sems, alias = _comm_parts(comm, 2, 1)

    def body(*refs):
        (x_ref, g_ref), cin, (h_ref,), cout, _, csem = _split_refs(refs, 2, 1, 0, comm)
        _comm_start(comm, cin, cout, csem, grid)
        xv = x_ref[...]
        h_ref[...] = (xv * _rstd(xv, D_MODEL) * g_ref[...]).astype(BF16)
        _comm_finish(comm, cin, cout, csem, grid)

    return pl.pallas_call(
        body, name="norm_x", grid=grid, out_shape=(jax.ShapeDtypeStruct((t, D_MODEL), BF16), *c_outs),
        in_specs=[_rows(tm, D_MODEL), _resident((1, D_MODEL)), *c_in_specs],
        out_specs=(_rows(tm, D_MODEL), *([_ANY] * len(c_outs))),
        scratch_shapes=c_sems, input_output_aliases=alias, compiler_params=_params(("arbitrary",)),
    )(x, g_mix, *c_ins)


def pack_late_weights(w_up, w_down, w_o, w_pg, small, comm=None):
    rows2 = sum(PACK_ROWS[n] for n in PACK_W2)
    rows3 = sum(PACK_ROWS[n] for n in PACK_W3)
    c_ins, c_in_specs, c_outs, c_sems, alias = _comm_parts(comm, 5, 2)
    grid = (1,)

    def body(*refs):
        (up_ref, dn_ref, o_ref, pg_ref, sm_ref), cin, (p2_ref, p3_ref), cout, _, csem = _split_refs(refs, 5, 2, 0, comm)
        _comm_start(comm, cin, cout, csem, grid)
        p2_ref[0:128, :] = o_ref[0].astype(BF16)
        p2_ref[128:256, :] = pg_ref[0].astype(BF16)
        p2_ref[256:rows2, :] = sm_ref[...]
        p3_ref[0:512, :] = up_ref[0].T.astype(BF16)
        p3_ref[512:1024, :] = dn_ref[0].astype(BF16)
        _comm_finish(comm, cin, cout, csem, grid)

    def whole(a):
        nd = a.ndim
        return pl.BlockSpec(a.shape, lambda i: (0,) * nd)

    args = (w_up, w_down, w_o, w_pg, small)
    return pl.pallas_call(
        body, name="pack_late_weights", grid=grid,
        out_shape=(jax.ShapeDtypeStruct((rows2, D_MODEL), BF16), jax.ShapeDtypeStruct((rows3, D_MODEL), BF16), *c_outs),
        in_specs=[*[whole(a) for a in args], *c_in_specs],
        out_specs=(pl.BlockSpec((rows2, D_MODEL), lambda i: (0, 0)), pl.BlockSpec((rows3, D_MODEL), lambda i: (0, 0)),
                   *([_ANY] * len(c_outs))),
        scratch_shapes=c_sems, input_output_aliases=alias, compiler_params=_params(("arbitrary",)),
    )(*args, *c_ins)


def in_proj(h, w_in_t, tm, comm=None):
    t = h.shape[0]
    nc = 512
    half = ZP // 2
    grid = (t // tm,)
    c_ins, c_in_specs, c_outs, c_sems, alias = _comm_parts(comm, 2, 2)

    def body(*refs):
        (h_ref, w_ref), cin, (zg_ref, zs_ref), cout, _, csem = _split_refs(refs, 2, 2, 0, comm)
        _comm_start(comm, cin, cout, csem, grid)
        hv = h_ref[...]
        for cidx in range(half // nc):
            zg_ref[:, cidx * nc:(cidx + 1) * nc] = _dot_nt(hv, w_ref[cidx * nc:(cidx + 1) * nc, :]).astype(BF16)
        for cidx in range(half // nc):
            zs_ref[:, cidx * nc:(cidx + 1) * nc] = _dot_nt(hv, w_ref[half + cidx * nc:half + (cidx + 1) * nc, :]).astype(BF16)
        _comm_finish(comm, cin, cout, csem, grid)

    return pl.pallas_call(
        body, name="in_proj", grid=grid,
        out_shape=(jax.ShapeDtypeStruct((t, half), BF16), jax.ShapeDtypeStruct((t, half), BF16), *c_outs),
        in_specs=[_rows(tm, D_MODEL), _resident((ZP, D_MODEL)), *c_in_specs],
        out_specs=(_rows(tm, half), _rows(tm, half), *([_ANY] * len(c_outs))),
        scratch_shapes=c_sems, input_output_aliases=alias, compiler_params=_params(("arbitrary",)),
    )(h, w_in_t, *c_ins)


def attn_prep(zp, tabs, g_qa, g_kva, g_qn, g_kn, w_qb_t, w_kvb_t, tm, s_len, comm=None):
    t = zp.shape[0]
    nsb = s_len // tm
    scale_a = (QK_NOPE + QK_ROPE) ** -0.5
    scale_b = HD_B ** -0.5

    grid = (t // tm,)
    c_ins, c_in_specs, c_outs, c_sems, alias = _comm_parts(comm, 13, 8)

    def body(*refs):
        ((qb_ref, qlat_ref, kb_ref, vb_ref, ckv_ref, kpe_ref, tab_ref, gqa_ref, gkva_ref, gqn_ref, gkn_ref, wqb_ref,
          wkvb_ref), cin, (qa_o, ka_o, va_o, qb_o, kb_o, vb_o, cq_o, ckvn_o), cout, _, csem) = _split_refs(refs, 13, 8, 0, comm)
        _comm_start(comm, cin, cout, csem, grid)
        ca, s1a, s2a = tab_ref[0], tab_ref[1], tab_ref[2]
        ck = tab_ref[3]
        cb, s1b, s2b = tab_ref[4], tab_ref[5], tab_ref[6]
        ql = qlat_ref[...].astype(F32)
        cq = (ql * _rstd(ql, Q_LORA) * gqa_ref[...]).astype(BF16)
        cq_o[...] = cq
        qa = _dot_nt(cq, wqb_ref[...])
        slabs = [slice(h * HP, (h + 1) * HP) for h in range(H_A)]
        qa_o[...] = jnp.concatenate(
            [(_rope_fwd(qa[:, sl], ca, s1a, s2a) * scale_a).astype(BF16) for sl in slabs], axis=1)
        cr = ckv_ref[...].astype(F32)
        ckv = (cr * _rstd(cr, KV_LORA) * gkva_ref[...]).astype(BF16)
        ckvn_o[...] = ckv
        kva = _dot_nt(ckv, wkvb_ref[...])
        kpe = _rope_fwd(kpe_ref[...].astype(F32), ck, s1a, s2a)
        ka_o[...] = jnp.concatenate([(kva[:, sl] + kpe).astype(BF16) for sl in slabs], axis=1)
        va_o[...] = kva[:, H_A * HP:].astype(BF16)
        gqn, gkn = gqn_ref[...], gkn_ref[...]

        def norm_rope(ref, sl, g, scale):
            xs = ref[:, sl].astype(F32)
            y = _rope_fwd(xs * _rstd(xs, HD_B) * g, cb, s1b, s2b)
            return (y if scale is None else y * scale).astype(BF16)

        qb_o[...] = jnp.concatenate([norm_rope(qb_ref, sl, gqn, scale_b) for sl in slabs], axis=1)
        kb_o[...] = jnp.concatenate([norm_rope(kb_ref, sl, gkn, None) for sl in slabs[:KV_B]], axis=1)
        vb_o[...] = vb_ref[...].astype(BF16)
        _comm_finish(comm, cin, cout, csem, grid)

    def o(width):
        return jax.ShapeDtypeStruct((t, width), BF16)

    return pl.pallas_call(
        body, name="attn_prep", grid=grid,
        out_shape=(o(H_A * HP), o(H_A * HP), o(H_A * HP), o(H_B * HP), o(KV_B * HP), o(KV_B * HP), o(Q_LORA), o(KV_LORA),
                   *c_outs),
        in_specs=[_rows(tm, 1024, 0), _rows(tm, 256, 4), _rows(tm, 256, 5), _rows(tm, 256, 6),
                  _rows(tm, 128, 14), _rows(tm, 128, 15),
                  pl.BlockSpec((7, tm, HP), lambda i: (0, i % nsb, 0)),
                  _resident((1, Q_LORA)), _resident((1, KV_LORA)), _resident((1, HP)), _resident((1, HP)),
                  _resident((H_A * HP, Q_LORA)), _resident((2 * H_A * HP, KV_LORA)), *c_in_specs],
        out_specs=(_rows(tm, H_A * HP), _rows(tm, H_A * HP), _rows(tm, H_A * HP), _rows(tm, H_B * HP),
                   _rows(tm, KV_B * HP), _rows(tm, KV_B * HP), _rows(tm, Q_LORA), _rows(tm, KV_LORA), *([_ANY] * len(c_outs))),
        scratch_shapes=c_sems, input_output_aliases=alias, compiler_params=_params(("arbitrary",)),
    )(zp, zp, zp, zp, zp, zp, tabs, g_qa, g_kva, g_qn, g_kn, w_qb_t, w_kvb_t, *c_ins)


def attn_fwd(q, k, v, n_b, s_len, tq, name, comm=None):
    t = q.shape[0]
    n_h, n_hk = q.shape[1] // HP, k.shape[1] // HP
    grp = n_h // n_hk
    nq = s_len // tq
    sub = min(tq, 256)
    grid = (n_b, n_h, nq)
    c_ins, c_in_specs, c_outs, c_sems, alias = _comm_parts(comm, 3, 2)

    def body(*refs):
        (q_ref, k_ref, v_ref), cin, (o_ref, lse_ref), cout, _, csem = _split_refs(refs, 3, 2, 0, comm)
        _comm_start(comm, cin, cout, csem, grid)
        kv, vv = k_ref[...], v_ref[...]
        for r in range(tq // sub):
            rows = slice(r * sub, (r + 1) * sub)
            s = _dot_nt(q_ref[rows, :], kv)
            m = jnp.max(s, axis=-1, keepdims=True)
            p = jnp.exp(s - m)
            l = jnp.sum(p, axis=-1, keepdims=True)
            o_ref[rows, :] = (_dot_nn(p.astype(BF16), vv) * (1.0 / l)).astype(o_ref.dtype)
            lse_ref[rows, :] = jnp.broadcast_to(m + jnp.log(l), (sub, HP))
        _comm_finish(comm, cin, cout, csem, grid)

    qspec = pl.BlockSpec((tq, HP), lambda b, h, i: (b * nq + i, h))
    kspec = pl.BlockSpec((s_len, HP), lambda b, h, i: (b, h // grp))
    return pl.pallas_call(
        body, name=name, grid=grid,
        out_shape=(jax.ShapeDtypeStruct((t, n_h * HP), BF16), jax.ShapeDtypeStruct((t, n_h * HP), F32), *c_outs),
        in_specs=[qspec, kspec, kspec, *c_in_specs], out_specs=(qspec, qspec, *([_ANY] * len(c_outs))),
        scratch_shapes=c_sems, input_output_aliases=alias,
        compiler_params=_params(("arbitrary", "arbitrary", "arbitrary")),
    )(q, k, v, *c_ins)


def merge_fwd(oa, ob, zp, x, w_oa_t, w_ob_t, wpack, off, tm):
    t = x.shape[0]

    def body(oa_ref, ob_ref, ga_ref, gb_ref, x_ref, woa_ref, wob_ref, wo_ref, x1_o, mg_o, y_o):
        ya = _dot_nt(oa_ref[...], woa_ref[...])
        yb = _dot_nt(ob_ref[...], wob_ref[...])
        y_o[:, 0:D_MODEL] = ya.astype(BF16)
        y_o[:, D_MODEL:2 * D_MODEL] = yb.astype(BF16)
        merged = (jax.nn.sigmoid(ga_ref[...].astype(F32)) * ya + jax.nn.sigmoid(gb_ref[...].astype(F32)) * yb).astype(BF16)
        mg_o[...] = merged
        x1_o[...] = x_ref[...] + _dot_nn(merged, _wrows(wo_ref, 0, D_MODEL))

    return pl.pallas_call(
        body, name="merge_fwd", grid=(t // tm,),
        out_shape=(jax.ShapeDtypeStruct((t, D_MODEL), F32), jax.ShapeDtypeStruct((t, D_MODEL), BF16),
                   jax.ShapeDtypeStruct((t, 2 * D_MODEL), BF16)),
        in_specs=[_rows(tm, H_A * HP), _rows(tm, H_B * HP), _rows(tm, 1024, 0), _rows(tm, 1024, 1), _rows(tm, D_MODEL),
                  _resident((D_MODEL, H_A * HP)), _resident((D_MODEL, H_B * HP)), _packed_weight(128, off["w_o"])],
        out_specs=(_rows(tm, D_MODEL), _rows(tm, D_MODEL), _rows(tm, 2 * D_MODEL)), compiler_params=_params(("parallel",)),
    )(oa, ob, zp, zp, x, w_oa_t, w_ob_t, wpack)


def mlp_fwd(x1, g_mlp, wpack, off, tm):
    t = x1.shape[0]
    fc = 1024

    def body(x_ref, g_ref, wup_ref, wdn_ref, x2_o, u_o):
        xv = x_ref[...]
        h2 = (xv * _rstd(xv, D_MODEL) * g_ref[...]).astype(BF16)
        acc = xv
        for cidx in range(D_FF // fc):
            sl = slice(cidx * fc, (cidx + 1) * fc)
            u = jnp.maximum(_dot_nt(h2, _wrows(wup_ref, cidx * fc, fc)), 0.0)
            u_o[:, sl] = u.astype(BF16)
            acc = acc + _dot_nn((u * u).astype(BF16), _wrows(wdn_ref, cidx * fc, fc))
        x2_o[...] = acc

    return pl.pallas_call(
        body, name="mlp_fwd", grid=(t // tm,),
        out_shape=(jax.ShapeDtypeStruct((t, D_MODEL), F32), jax.ShapeDtypeStruct((t, D_FF), BF16)),
        in_specs=[_rows(tm, D_MODEL), _resident((1, D_MODEL)), _packed_weight(512, off["w_up"]), _packed_weight(512, off["w_down"])],
        out_specs=(_rows(tm, D_MODEL), _rows(tm, D_FF)), compiler_params=_params(("parallel",)),
    )(x1, g_mlp, wpack, wpack)


def ple_loss_bwd(x2, p, tgt, g_ple, g_final, wpack, off, w_ple_t, tm):
    t = x2.shape[0]
    inv_d = 1.0 / D_MODEL

    def body(x2_ref, p_ref, tg_ref, gp_ref, gf_ref, wpg_ref, wple_ref, dx2_o, dt_o, h3_o, dpe_o, st_o, dx2b_o):
        @pl.when(pl.program_id(0) == 0)
        def _():
            st_o[...] = jnp.zeros_like(st_o)

        x2v = x2_ref[...]
        gp, gf = gp_ref[...], gf_ref[...]
        w_pg = _wrows(wpg_ref, 0, D_MODEL)
        r2 = _rstd(x2v, D_MODEL)
        xh2 = x2v * r2
        h3 = (xh2 * gp).astype(BF16)
        h3_o[...] = h3
        gate = jax.nn.sigmoid(_dot_nn(h3, w_pg))
        pe = _dot_nt(p_ref[...].astype(BF16), wple_ref[...])
        x3 = x2v + gate * pe
        r3 = _rstd(x3, D_MODEL)
        xh3 = x3 * r3
        err = xh3 * gf - tg_ref[...]
        dx3 = _rms_bwd(err, xh3, r3, gf * inv_d, D_MODEL)
        dpe = dx3 * gate
        dpe_o[...] = dpe.astype(BF16)
        dt = (dpe * pe * (1.0 - gate)).astype(BF16)
        dt_o[...] = dt
        dh3 = _dot_nt(dt, w_pg)
        dx2 = dx3 + _rms_bwd(dh3, xh2, r2, gp, D_MODEL)
        dx2_o[...] = dx2
        dx2b_o[...] = dx2.astype(BF16)
        st_o[0:1, :] += _colsum(dh3 * xh2)
        st_o[1:2, :] += _colsum(err * xh3) * inv_d
        st_o[2:3, :] += _colsum(err * err) * (0.5 * inv_d)

    bf = jax.ShapeDtypeStruct((t, D_MODEL), BF16)
    return pl.pallas_call(
        body, name="ple_loss_bwd", grid=(t // tm,),
        out_shape=(jax.ShapeDtypeStruct((t, D_MODEL), F32), bf, bf, bf, jax.ShapeDtypeStruct((3, D_MODEL), F32), bf),
        in_specs=[_rows(tm, D_MODEL), _rows(tm, PLE_DIM), _rows(tm, D_MODEL), _resident((1, D_MODEL)), _resident((1, D_MODEL)),
                  _packed_weight(128, off["w_ple_gate"]), _resident((D_MODEL, PLE_DIM))],
        out_specs=(_rows(tm, D_MODEL), _rows(tm, D_MODEL), _rows(tm, D_MODEL), _rows(tm, D_MODEL),
                   pl.BlockSpec((3, D_MODEL), lambda i: (0, 0)), _rows(tm, D_MODEL)),
        compiler_params=_params(("arbitrary",)),
    )(x2, p, tgt, g_ple, g_final, wpack, w_ple_t)


def mlp_bwd(dx2, x1, u, g_mlp, wpack, off, tm):
    t = x1.shape[0]
    fc = 1024

    def body(dx2_ref, x1_ref, u_ref, g_ref, wup_ref, wdn_ref, dx1_o, da_o, h2_o, st_o, dx1b_o):
        @pl.when(pl.program_id(0) == 0)
        def _():
            st_o[...] = jnp.zeros_like(st_o)

        d2 = dx2_ref[...]
        d2b = d2.astype(BF16)
        dh2 = jnp.zeros((tm, D_MODEL), F32)
        for cidx in range(D_FF // fc):
            sl = slice(cidx * fc, (cidx + 1) * fc)
            da = (_dot_nt(d2b, _wrows(wdn_ref, cidx * fc, fc)) * (2.0 * u_ref[:, sl].astype(F32))).astype(BF16)
            da_o[:, sl] = da
            dh2 = dh2 + _dot_nn(da, _wrows(wup_ref, cidx * fc, fc))
        xv = x1_ref[...]
        g = g_ref[...]
        r1 = _rstd(xv, D_MODEL)
        xh1 = xv * r1
        h2_o[...] = (xh1 * g).astype(BF16)
        st_o[...] += _colsum(dh2 * xh1)
        dx1 = d2 + _rms_bwd(dh2, xh1, r1, g, D_MODEL)
        dx1_o[...] = dx1
        dx1b_o[...] = dx1.astype(BF16)

    return pl.pallas_call(
        body, name="mlp_bwd", grid=(t // tm,),
        out_shape=(jax.ShapeDtypeStruct((t, D_MODEL), F32), jax.ShapeDtypeStruct((t, D_FF), BF16),
                   jax.ShapeDtypeStruct((t, D_MODEL), BF16), jax.ShapeDtypeStruct((1, D_MODEL), F32),
                   jax.ShapeDtypeStruct((t, D_MODEL), BF16)),
        in_specs=[_rows(tm, D_MODEL), _rows(tm, D_MODEL), _rows(tm, D_FF), _resident((1, D_MODEL)),
                  _packed_weight(512, off["w_up"]), _packed_weight(512, off["w_down"])],
        out_specs=(_rows(tm, D_MODEL), _rows(tm, D_FF), _rows(tm, D_MODEL), pl.BlockSpec((1, D_MODEL), lambda i: (0, 0)),
                   _rows(tm, D_MODEL)),
        compiler_params=_params(("arbitrary",)),
    )(dx2, x1, u, g_mlp, wpack, wpack)


def merge_bwd(dx1b, yab, zp, w_oa_t, w_ob_t, wpack, off, tm, comm=None):
    t = dx1b.shape[0]
    grid = (t // tm,)
    c_ins, c_in_specs, c_outs, c_sems, alias = _comm_parts(comm, 7, 5)

    def body(*refs):
        ((dx1_ref, y_ref, ga_ref, gb_ref, woa_ref, wob_ref, wo_ref), cin,
         (doa_o, dob_o, dg_o, dya_o, dyb_o), cout, _, csem) = _split_refs(refs, 7, 5, 0, comm)
        _comm_start(comm, cin, cout, csem, grid)
        dm = _dot_nt(dx1_ref[...], _wrows(wo_ref, 0, D_MODEL))
        for g_ref, w_ref, do_o, dy_o, col in ((ga_ref, woa_ref, doa_o, dya_o, 0), (gb_ref, wob_ref, dob_o, dyb_o, 1)):
            cols = slice(col * D_MODEL, (col + 1) * D_MODEL)
            sg = jax.nn.sigmoid(g_ref[...].astype(F32))
            dyv = (dm * sg).astype(BF16)
            dy_o[...] = dyv
            dg_o[:, cols] = (dm * y_ref[:, cols].astype(F32) * sg * (1.0 - sg)).astype(BF16)
            do_o[...] = _dot_nn(dyv, w_ref[...]).astype(BF16)
        _comm_finish(comm, cin, cout, csem, grid)

    bf = jax.ShapeDtypeStruct((t, D_MODEL), BF16)
    return pl.pallas_call(
        body, name="merge_bwd", grid=grid,
        out_shape=(bf, bf, jax.ShapeDtypeStruct((t, ZP), BF16), bf, bf, *c_outs),
        in_specs=[_rows(tm, D_MODEL), _rows(tm, 2 * D_MODEL), _rows(tm, 1024, 0), _rows(tm, 1024, 1),
                  _resident((D_MODEL, H_A * HP)), _resident((D_MODEL, H_B * HP)), _packed_weight(128, off["w_o"]), *c_in_specs],
        out_specs=(_rows(tm, D_MODEL), _rows(tm, D_MODEL), _rows(tm, 2 * D_MODEL), _rows(tm, D_MODEL), _rows(tm, D_MODEL),
                   *([_ANY] * len(c_outs))),
        scratch_shapes=c_sems, input_output_aliases=alias,
        compiler_params=_params(("arbitrary",)),
    )(dx1b, yab, zp, zp, w_oa_t, w_ob_t, wpack, *c_ins)


def attn_bwd(q, k, v, do, o, lse, n_b, s_len, tq, name, comm=None):
    t = q.shape[0]
    n_h, n_hk = q.shape[1] // HP, k.shape[1] // HP
    grp = n_h // n_hk
    nq = s_len // tq
    sub = min(tq, 256)
    grid = (n_b, n_hk, grp, nq)
    c_ins, c_in_specs, c_outs, c_sems, alias = _comm_parts(comm, 6, 3)

    def body(*refs):
        ((q_ref, k_ref, v_ref, do_ref, o_ref, lse_ref), cin, (dq_o, dk_o, dv_o), cout, (p_s, ds_s, dk_acc, dv_acc),
         csem) = _split_refs(refs, 6, 3, 4, comm)
        _comm_start(comm, cin, cout, csem, grid)

        @pl.when((pl.program_id(2) == 0) & (pl.program_id(3) == 0))
        def _():
            dk_acc[...] = jnp.zeros_like(dk_acc)
            dv_acc[...] = jnp.zeros_like(dv_acc)

        kv, vv = k_ref[...], v_ref[...]
        for r in range(tq // sub):
            rows = slice(r * sub, (r + 1) * sub)
            qv, dov = q_ref[rows, :], do_ref[rows, :]
            delta = jnp.sum(dov.astype(F32) * o_ref[rows, :].astype(F32), axis=-1, keepdims=True)
            delta_row = jnp.broadcast_to(delta, (sub, HP)).T[0:1, :]
            lse_row = lse_ref[rows, :].T[0:1, :]
            pt = jnp.exp(_dot_nt(kv, qv) - lse_row)
            dst = (pt * (_dot_nt(vv, dov) - delta_row)).astype(BF16)
            p_s[:, rows] = pt.astype(BF16)
            ds_s[:, rows] = dst
            dq_o[rows, :] = _dot_tn(dst, kv).astype(dq_o.dtype)
        dk_acc[...] += _dot_nn(ds_s[...], q_ref[...])
        dv_acc[...] += _dot_nn(p_s[...], do_ref[...])

        @pl.when((pl.program_id(2) == grp - 1) & (pl.program_id(3) == nq - 1))
        def _():
            dk_o[...] = dk_acc[...].astype(dk_o.dtype)
            dv_o[...] = dv_acc[...].astype(dv_o.dtype)

        _comm_finish(comm, cin, cout, csem, grid)

    qspec = pl.BlockSpec((tq, HP), lambda b, hk, g, i: (b * nq + i, hk * grp + g))
    kspec = pl.BlockSpec((s_len, HP), lambda b, hk, g, i: (b, hk))
    return pl.pallas_call(
        body, name=name, grid=grid,
        out_shape=(jax.ShapeDtypeStruct((t, n_h * HP), BF16), jax.ShapeDtypeStruct((t, n_hk * HP), BF16),
                   jax.ShapeDtypeStruct((t, n_hk * HP), BF16), *c_outs),
        in_specs=[qspec, kspec, kspec, qspec, qspec, qspec, *c_in_specs],
        out_specs=(qspec, kspec, kspec, *([_ANY] * len(c_outs))),
        scratch_shapes=[pltpu.VMEM((s_len, tq), BF16), pltpu.VMEM((s_len, tq), BF16),
                        pltpu.VMEM((s_len, HP), F32), pltpu.VMEM((s_len, HP), F32), *c_sems],
        input_output_aliases=alias,
        compiler_params=_params(("arbitrary", "arbitrary", "arbitrary", "arbitrary")),
    )(q, k, v, do, o, lse, *c_ins)


def prep_bwd(dqa, dka, dva, dqb, dkb, dvb, zp, dz, tabs, g_qa, g_kva, g_qn, g_kn, w_qb_t, w_kvb_t, tm, s_len):
    t = zp.shape[0]
    nsb = s_len // tm
    scale_a = (QK_NOPE + QK_ROPE) ** -0.5
    scale_b = HD_B ** -0.5

    def body(dqa_ref, dka_ref, dva_ref, dqb_ref, dkb_ref, dvb_ref, qb_ref, qlat_ref, kb_ref, ckv_ref, tab_ref,
             gqa_ref, gkva_ref, gqn_ref, gkn_ref, wqb_ref, wkvb_ref, _, dz_o, dqap_o, dkva_o, st_o):
        dzq_o, dsm_o = dz_o.at[:, 0:1024], dz_o.at[:, 1024:2048]

        @pl.when(pl.program_id(0) == 0)
        def _():
            st_o[...] = jnp.zeros_like(st_o)

        ca, s1a, s2a = tab_ref[0], tab_ref[1], tab_ref[2]
        ck = tab_ref[3]
        cb, s1b, s2b = tab_ref[4], tab_ref[5], tab_ref[6]
        for h in range(H_A):
            sl = slice(h * HP, (h + 1) * HP)
            dqap_o[:, sl] = _rope_bwd(dqa_ref[:, sl].astype(F32) * scale_a, ca, s1a, s2a).astype(BF16)
        dcq = _dot_nn(dqap_o[...], wqb_ref[...])
        ql = qlat_ref[...].astype(F32)
        rq = _rstd(ql, Q_LORA)
        xh = ql * rq
        gqa = gqa_ref[...]
        st_o[0:1, :] += _colsum(dcq * xh)
        dsm_o[:, 0:256] = _rms_bwd(dcq, xh, rq, gqa, Q_LORA).astype(BF16)
        dkpe = jnp.zeros((tm, HP), F32)
        for h in range(H_A):
            sl = slice(h * HP, (h + 1) * HP)
            dk = dka_ref[:, sl]
            dkpe = dkpe + dk.astype(F32)
            dkva_o[:, sl] = dk.astype(BF16)
        dkva_o[:, H_A * HP:] = dva_ref[...].astype(BF16)
        dsm_o[:, 896:1024] = _rope_bwd(dkpe, ck, s1a, s2a).astype(BF16)
        dckv = _dot_nn(dkva_o[...], wkvb_ref[...])
        cr = ckv_ref[...].astype(F32)
        rk = _rstd(cr, KV_LORA)
        xh = cr * rk
        st_o[1:2, 0:128] += _colsum(dckv * xh)
        dsm_o[:, 768:896] = _rms_bwd(dckv, xh, rk, gkva_ref[...], KV_LORA).astype(BF16)
        gqn, gkn = gqn_ref[...], gkn_ref[...]
        dgq = jnp.zeros((1, HP), F32)
        for h in range(H_B):
            sl = slice(h * HP, (h + 1) * HP)
            dy = _rope_bwd(dqb_ref[:, sl].astype(F32) * scale_b, cb, s1b, s2b)
            xs = qb_ref[:, sl].astype(F32)
            r = _rstd(xs, HD_B)
            xh = xs * r
            dgq = dgq + _colsum(dy * xh)
            dzq_o[:, sl] = _rms_bwd(dy, xh, r, gqn, HD_B).astype(BF16)
        st_o[2:3, 0:128] += dgq
        dgk = jnp.zeros((1, HP), F32)
        for h in range(KV_B):
            sl = slice(h * HP, (h + 1) * HP)
            dy = _rope_bwd(dkb_ref[:, sl].astype(F32), cb, s1b, s2b)
            xs = kb_ref[:, sl].astype(F32)
            r = _rstd(xs, HD_B)
            xh = xs * r
            dgk = dgk + _colsum(dy * xh)
            dsm_o[:, 256 + h * HP:256 + (h + 1) * HP] = _rms_bwd(dy, xh, r, gkn, HD_B).astype(BF16)
        st_o[3:4, 0:128] += dgk
        dsm_o[:, 512:768] = dvb_ref[...].astype(BF16)

    return pl.pallas_call(
        body, name="prep_bwd", grid=(t // tm,),
        out_shape=(jax.ShapeDtypeStruct((t, ZP), BF16), jax.ShapeDtypeStruct((t, 1024), BF16),
                   jax.ShapeDtypeStruct((t, 2048), BF16), jax.ShapeDtypeStruct((4, 256), F32)),
        in_specs=[_rows(tm, 1024), _rows(tm, 1024), _rows(tm, 1024), _rows(tm, 1024), _rows(tm, 256), _rows(tm, 256),
                  _rows(tm, 1024, 0), _rows(tm, 256, 4), _rows(tm, 256, 5), _rows(tm, 128, 14),
                  pl.BlockSpec((7, tm, HP), lambda i: (0, i % nsb, 0)),
                  _resident((1, Q_LORA)), _resident((1, KV_LORA)), _resident((1, HP)), _resident((1, HP)),
                  _resident((H_A * HP, Q_LORA)), _resident((2 * H_A * HP, KV_LORA)), _ANY],
        out_specs=(_rows(tm, 2048, 1), _rows(tm, 1024), _rows(tm, 2048), pl.BlockSpec((4, 256), lambda i: (0, 0))),
        input_output_aliases={17: 0}, compiler_params=_params(("arbitrary",)),
    )(dqa, dka, dva, dqb, dkb, dvb, zp, zp, zp, zp, tabs, g_qa, g_kva, g_qn, g_kn, w_qb_t, w_kvb_t, dz)


def in_bwd(dz, x, dx1, g_mix, w_in_t, tm, comm=None):
    t = x.shape[0]
    grid = (t // tm,)
    c_ins, c_in_specs, c_outs, c_sems, alias = _comm_parts(comm, 5, 2)

    def body(*refs):
        (dz_ref, x_ref, dx1_ref, g_ref, w_ref), cin, (dx_o, st_o), cout, _, csem = _split_refs(refs, 5, 2, 0, comm)
        _comm_start(comm, cin, cout, csem, grid)

        @pl.when(pl.program_id(0) == 0)
        def _():
            st_o[...] = jnp.zeros_like(st_o)

        dh = _dot_nn(dz_ref[...], w_ref[...])
        xv = x_ref[...]
        g = g_ref[...]
        r = _rstd(xv, D_MODEL)
        xh = xv * r
        st_o[...] += _colsum(dh * xh)
        dx_o[...] = dx1_ref[...] + _rms_bwd(dh, xh, r, g, D_MODEL)
        _comm_finish(comm, cin, cout, csem, grid)

    return pl.pallas_call(
        body, name="in_bwd", grid=grid,
        out_shape=(jax.ShapeDtypeStruct((t, D_MODEL), F32), jax.ShapeDtypeStruct((1, D_MODEL), F32), *c_outs),
        in_specs=[_rows(tm, ZP), _rows(tm, D_MODEL), _rows(tm, D_MODEL),
                  _resident((1, D_MODEL)), _resident((ZP, D_MODEL)), *c_in_specs],
        out_specs=(_rows(tm, D_MODEL), pl.BlockSpec((1, D_MODEL), lambda i: (0, 0)), *([_ANY] * len(c_outs))),
        scratch_shapes=c_sems, input_output_aliases=alias,
        compiler_params=_params(("arbitrary",)),
    )(dz, x, dx1, g_mix, w_in_t, *c_ins)


def matmul_tn(a, b, name):
    t, m = a.shape
    n = b.shape[1]
    bm = min(m, 512)

    def body(a_ref, b_ref, o_ref):
        o_ref[...] = _dot_tn(a_ref[...].astype(BF16), b_ref[...].astype(BF16)).astype(BF16)

    return pl.pallas_call(
        body, name=name, grid=(m // bm,), out_shape=jax.ShapeDtypeStruct((m, n), BF16),
        in_specs=[pl.BlockSpec((t, bm), lambda i: (0, i)), pl.BlockSpec((t, n), lambda i: (0, 0))],
        out_specs=pl.BlockSpec((bm, n), lambda i: (i, 0)),
        compiler_params=_params(("parallel",)),
    )(a, b)


def matmul_tn_packed(a, b, name, rows, row_off, total_rows, buf=None, square_a=False):
    t, m = a.shape
    n = b.shape[1]
    pd = max(1, 512 // rows)
    bm = pd * rows

    def body(a_ref, b_ref, *rest):
        o_ref = rest[-1]
        av = a_ref[...]
        if square_a:
            av = (av.astype(F32) * av.astype(F32))
        o_ref[...] = _dot_tn(av.astype(BF16), b_ref[...].astype(BF16)).reshape(pd, rows, n).astype(o_ref.dtype)

    in_specs = [pl.BlockSpec((t, bm), lambda i: (0, i)), pl.BlockSpec((t, n), lambda i: (0, 0))]
    args = [a, b]
    if buf is not None:
        in_specs.append(_ANY)
        args.append(buf)
    return pl.pallas_call(
        body, name=name, grid=(m // bm,), out_shape=jax.ShapeDtypeStruct((N_DEV, total_rows, n), BF16),
        in_specs=in_specs, out_specs=pl.BlockSpec((pd, rows, n), lambda i: (i, row_off // rows, 0)),
        input_output_aliases={2: 0} if buf is not None else {},
        compiler_params=_params(("parallel",)),
    )(*args)


def adamw(w, g, m, v, name, g_transposed=False):
    _, r, c = w.shape
    tr = 256 if (not g_transposed and r > 256 and r % 256 == 0) else r
    c1 = 1.0 - ADAM_B1 ** ADAM_STEP
    c2 = 1.0 - ADAM_B2 ** ADAM_STEP

    def body(w_ref, g_ref, m_ref, v_ref, g_o, d_o, m_o, v_o):
        gv = g_ref[...].T if g_transposed else g_ref[...]
        mn = ADAM_B1 * m_ref[0] + (1.0 - ADAM_B1) * gv
        vn = ADAM_B2 * v_ref[0] + (1.0 - ADAM_B2) * (gv * gv)
        g_o[0] = gv
        m_o[0] = mn
        v_o[0] = vn
        d_o[0] = -ADAM_LR * ((mn / c1) / (jnp.sqrt(vn / c2) + ADAM_EPS) + ADAM_WD * w_ref[0])

    spec = pl.BlockSpec((1, tr, c), lambda i: (0, i, 0))
    gspec = pl.BlockSpec((c, r), lambda i: (0, 0)) if g_transposed else pl.BlockSpec((tr, c), lambda i: (i, 0))
    shp = jax.ShapeDtypeStruct((1, r, c), F32)
    return pl.pallas_call(
        body, name=name, grid=(r // tr,), out_shape=(shp,) * 4, in_specs=[spec, gspec, spec, spec], out_specs=(spec,) * 4,
        compiler_params=_params(("parallel",)),
    )(w, g, m, v)


def _rope_tables(s_len):
    def angles(pos, dim):
        inv = np.float32(ROPE_THETA) ** (-np.arange(0, dim, 2, dtype=np.float32) / np.float32(dim))
        return pos.astype(np.float32)[:, None] * inv[None, :]

    tpos = np.arange(s_len)
    a1 = angles(tpos, QK_ROPE)
    ar = angles(tpos // GRID_W, HD_B // 2)
    ac = angles(tpos % GRID_W, HD_B // 2)
    z16 = np.zeros((s_len, 16), np.float32)
    z32 = np.zeros((s_len, 32), np.float32)
    z64 = np.zeros((s_len, 64), np.float32)
    one64 = np.ones((s_len, 64), np.float32)
    c1, s1 = np.cos(a1), np.sin(a1)
    ca = np.concatenate([one64, c1, c1, z32], axis=1)
    ck = np.concatenate([z64, c1, c1, z32], axis=1)
    s1a = np.concatenate([z64, -s1, z16, z32], axis=1)
    s2a = np.concatenate([z64, z16, s1, z32], axis=1)
    cr, sr, cc, sc = np.cos(ar), np.sin(ar), np.cos(ac), np.sin(ac)
    cb = np.concatenate([cr, cr, cc, cc, z64], axis=1)
    s1b = np.concatenate([-sr, z16, -sc, z16, z64], axis=1)
    s2b = np.concatenate([z16, sr, z16, sc, z64], axis=1)
    return jnp.asarray(np.stack([ca, s1a, s2a, ck, cb, s1b, s2b]).astype(np.float32))


def _pad_heads(a, n_heads, axis):
    shp = a.shape
    a = a.reshape(shp[:axis] + (n_heads, shp[axis] // n_heads) + shp[axis + 1:])
    pad = [(0, 0)] * a.ndim
    pad[axis + 1] = (0, HP - a.shape[axis + 1])
    a = jnp.pad(a, pad)
    return a.reshape(shp[:axis] + (n_heads * HP,) + shp[axis + 1:])


def _unpad_heads(a, n_heads, width, axis):
    shp = a.shape
    a = a.reshape(shp[:axis] + (n_heads, HP) + shp[axis + 1:])
    a = lax.slice_in_dim(a, 0, width, axis=axis + 1)
    return a.reshape(shp[:axis] + (n_heads * width,) + shp[axis + 1:])


def _pack_rows(blocks, names):
    parts = []
    for name in names:
        b = blocks[name]
        padr = PACK_ROWS[name] - b.shape[-2]
        if padr:
            b = jnp.pad(b, [(0, 0)] * (b.ndim - 2) + [(0, padr), (0, 0)])
        parts.append(b)
    return jnp.concatenate(parts, axis=parts[0].ndim - 2)


def _expand_w_in(wt):
    z64 = jnp.zeros((64, D_MODEL), wt.dtype)
    z32 = jnp.zeros((32, D_MODEL), wt.dtype)
    return jnp.concatenate([
        wt[1184:2208], wt[2208:3232], _pad_heads(wt[416:928], H_B, 0), wt[0:256],
        _pad_heads(wt[928:1056], KV_B, 0), _pad_heads(wt[1056:1184], KV_B, 0), wt[256:384],
        z64, wt[384:416], z32], axis=0)


def _collapse_w_in(dw):
    dg, dq, ds = dw[0:2048], dw[2048:3072], dw[3072:4096]
    return jnp.concatenate([
        ds[0:256], ds[768:896], ds[960:992], _unpad_heads(dq, H_B, HD_B, 0), _unpad_heads(ds[256:512], KV_B, HD_B, 0),
        _unpad_heads(ds[512:768], KV_B, HD_B, 0), dg], axis=0)


def kernel(x, p, g_mix, w_in, g_qa, w_qb, g_kva, w_kvb, g_qn, g_kn, w_oa, w_ob, w_o, g_mlp, w_up, w_down, g_ple, w_ple_gate, w_ple, g_final, loss_target, m_g_mix, m_w_in, m_g_qa, m_w_qb, m_g_kva, m_w_kvb, m_g_qn, m_g_kn, m_w_oa, m_w_ob, m_w_o, m_g_mlp, m_w_up, m_w_down, m_g_ple, m_w_ple_gate, m_w_ple, m_g_final, v_g_mix, v_w_in, v_g_qa, v_w_qb, v_g_kva, v_w_kvb, v_g_qn, v_g_kn, v_w_oa, v_w_ob, v_w_o, v_g_mlp, v_w_up, v_w_down, v_g_ple, v_w_ple_gate, v_w_ple, v_g_final):
    n_b, s_len, _ = x.shape
    t = n_b * s_len
    tm = min(512, s_len)
    tq_f = min(2048, s_len)
    tq_b = min(2048, s_len)

    mats = dict(w_in=(w_in, m_w_in, v_w_in), w_qb=(w_qb, m_w_qb, v_w_qb), w_kvb=(w_kvb, m_w_kvb, v_w_kvb),
                w_oa=(w_oa, m_w_oa, v_w_oa), w_ob=(w_ob, m_w_ob, v_w_ob), w_o=(w_o, m_w_o, v_w_o),
                w_up=(w_up, m_w_up, v_w_up), w_down=(w_down, m_w_down, v_w_down),
                w_ple_gate=(w_ple_gate, m_w_ple_gate, v_w_ple_gate), w_ple=(w_ple, m_w_ple, v_w_ple))
    col_sharded = ("w_in", "w_qb", "w_kvb", "w_oa", "w_ob", "w_up", "w_ple")

    blocks = {}
    for name in PACK_W1 + ("w_oa", "w_ob", "w_ple"):
        blocks[name] = mats[name][0][0].T.reshape(-1, D_MODEL).astype(BF16)
    off_w1, _ = _pack_offsets(PACK_W1)
    off_w2, _ = _pack_offsets(PACK_W2)
    off_w3, _ = _pack_offsets(PACK_W3)
    xf = x.reshape(t, D_MODEL)
    h, full1 = norm_x(xf, g_mix, tm, comm=gather3_first_comm(_pack_rows(blocks, PACK_W1)))
    pack2, pack3, full1 = pack_late_weights(w_up, w_down, w_o, w_ple_gate, _pack_rows(blocks, ("w_oa", "w_ob", "w_ple")),
                                            comm=gather3_second_comm(full1, then_third=True))

    def gathered(full, offs, name, rows, width):
        return full[:, offs[name]:offs[name] + rows].reshape(-1, width)

    w_in_t = _expand_w_in(gathered(full1, off_w1, "w_in", 404, D_MODEL))
    w_qb_t = _pad_heads(gathered(full1, off_w1, "w_qb", 24, Q_LORA), H_A, 0)
    wkvb = gathered(full1, off_w1, "w_kvb", 16, KV_LORA).reshape(H_A, 2, 64, KV_LORA)
    w_kvb_t = jnp.concatenate([_pad_heads(wkvb[:, 0].reshape(-1, KV_LORA), H_A, 0),
                               _pad_heads(wkvb[:, 1].reshape(-1, KV_LORA), H_A, 0)], axis=0)

    tabs = _rope_tables(s_len)
    g_qn_p = jnp.pad(g_qn, ((0, 0), (0, HP - HD_B)))
    g_kn_p = jnp.pad(g_kn, ((0, 0), (0, HP - HD_B)))
    pf = p.reshape(t, PLE_DIM)
    tgt = loss_target.reshape(t, D_MODEL)

    zg, zs, full2 = in_proj(h, w_in_t, tm, comm=gather3_first_comm(pack2))
    qa, ka, va, qb, kb, vb, cq, ckv, full2 = attn_prep(zs, tabs, g_qa, g_kva, g_qn_p, g_kn_p, w_qb_t, w_kvb_t, tm, s_len,
                                                       comm=gather3_second_comm(full2))
    oa, lse_a, full3, full2 = attn_fwd(qa, ka, va, n_b, s_len, tq_f, "attn_a_fwd",
                                       comm=_join_comms(gather_first_comm(pack3), gather3_third_comm(full2)))
    ob, lse_b, full3 = attn_fwd(qb, kb, vb, n_b, s_len, tq_f, "attn_b_fwd", comm=gather_pass_comm(full3))
    w_oa_t = _pad_heads(gathered(full2, off_w2, "w_oa", 64, H_A * V_DIM_A), H_A, 1)
    w_ob_t = _pad_heads(gathered(full2, off_w2, "w_ob", 64, H_B * HD_B), H_B, 1)
    w_ple_t = gathered(full2, off_w2, "w_ple", 32, PLE_DIM)
    x1, merged, yab = merge_fwd(oa, ob, zg, xf, w_oa_t, w_ob_t, full2, off_w2, tm)
    x2, u = mlp_fwd(x1, g_mlp, full3, off_w3, tm)
    dx2, dt, h3, dpe, st_ple, dx2b = ple_loss_bwd(x2, pf, tgt, g_ple, g_final.reshape(1, D_MODEL), full2, off_w2, w_ple_t, tm)
    dx1, da, h2, st_mlp, dx1b = mlp_bwd(dx2, x1, u, g_mlp, full3, off_w3, tm)

    core = lax.axis_index("c").astype(jnp.int32).reshape(1)
    chip = (2 * lax.axis_index("x") + lax.axis_index("y")).astype(jnp.int32).reshape(1)

    def packed(gblocks, names):
        return _pack_rows({n: gblocks[n].reshape(N_DEV, -1, D_MODEL).astype(BF16) for n in names}, names)

    off_g1, rows_g1 = _pack_offsets(PACK_G1)
    gpack1 = matmul_tn_packed(da, h2, "gw_up", 512, off_g1["w_up"], rows_g1)
    gpack1 = matmul_tn_packed(u, dx2b, "gw_down", 512, off_g1["w_down"], rows_g1, buf=gpack1, square_a=True)
    gpack1 = matmul_tn_packed(h3, dt, "gw_pg", 128, off_g1["w_ple_gate"], rows_g1, buf=gpack1)
    gple = matmul_tn(dpe, pf, "gw_ple").reshape(N_DEV, -1, D_MODEL).astype(BF16)
    gpack1 = lax.dynamic_update_slice(gpack1, gple, (0, off_g1["w_ple"], 0))
    doa, dob, dz, dya, dyb, got1 = merge_bwd(dx1b, yab, zg, w_oa_t, w_ob_t, full2, off_w2, tm,
                                               comm=scatter_sibling_comm(gpack1))
    part1 = add_pairs(gpack1, got1, core)

    off_g2, rows_g2 = _pack_offsets(PACK_G2)
    g2 = dict(w_oa=_unpad_heads(matmul_tn(dya, oa, "gw_oa"), H_A, V_DIM_A, 1),
              w_ob=_unpad_heads(matmul_tn(dyb, ob, "gw_ob"), H_B, HD_B, 1))
    gpack2 = matmul_tn_packed(merged, dx1b, "gw_o", 128, off_g2["w_o"], rows_g2)
    gpack2 = lax.dynamic_update_slice(gpack2, packed(g2, ("w_oa", "w_ob")), (0, off_g2["w_oa"], 0))
    dqa, dka, dva, land1, got2 = attn_bwd(qa, ka, va, doa, oa, lse_a, n_b, s_len, tq_b, "attn_a_bwd",
                                          comm=_join_comms(scatter_chips_comm(part1), scatter_sibling_comm(gpack2)))
    gshard1 = sum_chips(part1, land1, chip)
    part2 = add_pairs(gpack2, got2, core)
    dqb, dkb, dvb, land2 = attn_bwd(qb, kb, vb, dob, ob, lse_b, n_b, s_len, tq_b, "attn_b_bwd", comm=scatter_chips_comm(part2))
    gshard2 = sum_chips(part2, land2, chip)
    dz, dqap, dkva, st_prep = prep_bwd(dqa, dka, dva, dqb, dkb, dvb, zs, dz, tabs, g_qa, g_kva, g_qn_p, g_kn_p,
                                       w_qb_t, w_kvb_t, tm, s_len)

    gkv = matmul_tn(dkva, ckv, "gw_kvb")
    g3 = dict(
        w_in=_collapse_w_in(matmul_tn(dz, h, "gw_in")),
        w_qb=_unpad_heads(matmul_tn(dqap, cq, "gw_qb"), H_A, QK_NOPE + QK_ROPE, 0),
        w_kvb=jnp.stack([_unpad_heads(gkv[:H_A * HP], H_A, 64, 0).reshape(H_A, 64, KV_LORA),
                         _unpad_heads(gkv[H_A * HP:], H_A, 64, 0).reshape(H_A, 64, KV_LORA)], axis=1))
    gpack3 = packed(g3, PACK_G3)
    part3 = add_pairs(gpack3, exchange_sibling(gpack3), core)
    grad_x, st_mix, land3 = in_bwd(dz, xf, dx1, g_mix, w_in_t, tm, comm=scatter_chips_comm(part3))
    gshard3 = sum_chips(part3, land3, chip)
    off_g3, _ = _pack_offsets(PACK_G3)
    shards = {n: (gshard1, off_g1[n]) for n in PACK_G1}
    shards.update({n: (gshard2, off_g2[n]) for n in PACK_G2})
    shards.update({n: (gshard3, off_g3[n]) for n in PACK_G3})

    stats = allreduce_stats(st_mix, st_prep, st_mlp, st_ple)
    loss = jnp.sum(stats[ST_LOSS])

    out_g, out_d, out_m, out_v = {}, {}, {}, {}
    for name, (w, m, v) in mats.items():
        gshard, off = shards[name]
        r, c = w.shape[1:]
        if name in col_sharded:
            g2 = gshard[off:off + (r * c) // D_MODEL].reshape(c, r)
            if r % 128 == 0 and c % 128 == 0:
                res = adamw(w, g2, m, v, "adamw_" + name, g_transposed=True)
            else:
                res = adamw(w[0].T[None], g2, m[0].T[None], v[0].T[None], "adamw_" + name)
                res = tuple(a[0].T[None] for a in res)
        else:
            res = adamw(w, gshard[off:off + r], m, v, "adamw_" + name)
        out_g[name], out_d[name], out_m[name], out_v[name] = res

    gains = (("g_mix", g_mix, m_g_mix, v_g_mix, ST_G_MIX), ("g_qa", g_qa, m_g_qa, v_g_qa, ST_G_QA),
             ("g_kva", g_kva, m_g_kva, v_g_kva, ST_G_KVA), ("g_qn", g_qn, m_g_qn, v_g_qn, ST_G_QN),
             ("g_kn", g_kn, m_g_kn, v_g_kn, ST_G_KN), ("g_mlp", g_mlp, m_g_mlp, v_g_mlp, ST_G_MLP),
             ("g_ple", g_ple, m_g_ple, v_g_ple, ST_G_PLE), ("g_final", g_final, m_g_final, v_g_final, ST_G_FINAL))
    res = adamw_gains(stats, [(r_, w.reshape(1, -1), m.reshape(1, -1), v.reshape(1, -1)) for _, w, m, v, r_ in gains])
    for (name, w, _, _, _), (gg, gd, gm, gv) in zip(gains, res):
        out_g[name], out_d[name], out_m[name], out_v[name] = (a.reshape(w.shape) for a in (gg, gd, gm, gv))

    order = ("g_mix", "w_in", "g_qa", "w_qb", "g_kva", "w_kvb", "g_qn", "g_kn", "w_oa", "w_ob", "w_o", "g_mlp",
             "w_up", "w_down", "g_ple", "w_ple_gate", "w_ple", "g_final")
    return (loss, grad_x.reshape(x.shape), *[out_g[n] for n in order], *[out_d[n] for n in order],
            *[out_m[n] for n in order], *[out_v[n] for n in order])
```

```python
import numpy as np
import jax
import jax.numpy as jnp
from jax import lax
from jax.experimental import pallas as pl
from jax.experimental.pallas import tpu as pltpu

F32 = jnp.float32
BF16 = jnp.bfloat16

D_MODEL = 1024
EPS = 1e-6
ROPE_THETA = 10000.0
GRID_W = 64
H_A = 8
QK_NOPE = 64
QK_ROPE = 32
V_DIM_A = 64
Q_LORA = 256
KV_LORA = 128
H_B = 8
KV_B = 2
HD_B = 64
D_FF = 4 * D_MODEL
PLE_DIM = 256
HP = 128
ZP = 4096
N_DEV = 8
N_CHIP = 4

ADAM_LR = 0.001
ADAM_B1 = 0.9
ADAM_B2 = 0.999
ADAM_EPS = 1e-08
ADAM_WD = 0.01
ADAM_STEP = 10

VMEM_LIMIT = 52 * 1024 * 1024

PACK_ROWS = dict(w_in=416, w_qb=32, w_kvb=16, w_oa=64, w_ob=64, w_o=128, w_up=512, w_down=512, w_ple_gate=128, w_ple=32)
PACK_W1 = ("w_in", "w_qb", "w_kvb")
PACK_W2 = ("w_o", "w_ple_gate", "w_oa", "w_ob", "w_ple")
PACK_W3 = ("w_up", "w_down")
PACK_G1 = ("w_up", "w_down", "w_ple_gate", "w_ple")
PACK_G2 = ("w_o", "w_oa", "w_ob")
PACK_G3 = ("w_in", "w_qb", "w_kvb")


def _pack_offsets(names):
    off, o = {}, 0
    for n in names:
        off[n] = o
        o += PACK_ROWS[n]
    return off, o

ST_G_MIX, ST_G_QA, ST_G_KVA, ST_G_QN, ST_G_KN, ST_G_MLP, ST_G_PLE, ST_G_FINAL, ST_LOSS = range(9)
ST_ROWS = 16


def _dot_nn(a, b):
    return lax.dot_general(a, b, (((1,), (0,)), ((), ())), preferred_element_type=F32)


def _dot_nt(a, b):
    return lax.dot_general(a, b, (((1,), (1,)), ((), ())), preferred_element_type=F32)


def _dot_tn(a, b):
    return lax.dot_general(a, b, (((0,), (0,)), ((), ())), preferred_element_type=F32)


def _rstd(x, n):
    return lax.rsqrt(jnp.sum(x * x, axis=-1, keepdims=True) * (1.0 / n) + EPS)


def _rms_bwd(dy, xh, r, g, n):
    dxh = dy * g
    return r * (dxh - xh * (jnp.sum(dxh * xh, axis=-1, keepdims=True) * (1.0 / n)))


def _rope_fwd(x, c, s1, s2):
    return x * c + pltpu.roll(x, HP - 16, 1) * s1 + pltpu.roll(x, 16, 1) * s2


def _rope_bwd(d, c, s1, s2):
    return d * c + pltpu.roll(d * s1, 16, 1) + pltpu.roll(d * s2, HP - 16, 1)


def _colsum(v):
    return jnp.sum(v, axis=0, keepdims=True)


def _params(sem=None, vmem=VMEM_LIMIT):
    return pltpu.CompilerParams(dimension_semantics=sem, vmem_limit_bytes=vmem)


def _resident(shape):
    nd = len(shape)
    return pl.BlockSpec(shape, lambda *_: (0,) * nd, pipeline_mode=pl.Buffered(1))


def _rows(tm, width, col=0):
    return pl.BlockSpec((tm, width), lambda i: (i, col))


def _packed_weight(rows, off):
    return pl.BlockSpec((N_DEV, rows, D_MODEL), lambda *_: (0, off // rows, 0), pipeline_mode=pl.Buffered(1))


def _wrows(ref, start, size):
    rows = ref.shape[1]
    return ref[start // rows:(start + size) // rows].reshape(size, D_MODEL)


def _mesh_pos():
    return lax.axis_index("x"), lax.axis_index("y"), lax.axis_index("c")


def _flip(v, bit):
    return (1 - v) if bit else v


_ANY = pl.BlockSpec(memory_space=pl.ANY)
_MESH = pl.DeviceIdType.MESH


def _remote(src, dst, send_sems, recv_sems, k, to):
    return pltpu.make_async_remote_copy(src_ref=src, dst_ref=dst, send_sem=send_sems.at[k], recv_sem=recv_sems.at[k],
                                        device_id=to, device_id_type=_MESH)


def _sibling_copies(g_ref, got_ref, send_sems, recv_sems):
    x, y, c = _mesh_pos()
    return [_remote(g_ref.at[2 * j + (1 - c)], got_ref.at[j], send_sems, recv_sems, j, (x, y, 1 - c)) for j in range(N_CHIP)]


def _chip_copies(p_ref, land_ref, send_sems, recv_sems):
    x, y, c = _mesh_pos()
    copies = []
    for k in (1, 2, 3):
        tx, ty = _flip(x, k & 2), _flip(y, k & 1)
        copies.append(_remote(p_ref.at[2 * tx + ty], land_ref.at[k - 1], send_sems, recv_sems, k - 1, (tx, ty, c)))
    return copies


class _Comm:
    def __init__(self, ins, out_shapes, sems, make, aliases=None, make_tail=None):
        self.ins, self.out_shapes, self.sems, self.make, self.aliases = list(ins), list(out_shapes), list(sems), make, aliases or {}
        self.make_tail = make_tail


def _join_comms(a, b):
    assert a.make_tail is None and b.make_tail is None
    n_i, n_o, n_s = len(a.ins), len(a.out_shapes), len(a.sems)

    def make(cin, cout, sems):
        return a.make(cin[:n_i], cout[:n_o], sems[:n_s]) + b.make(cin[n_i:], cout[n_o:], sems[n_s:])

    aliases = dict(a.aliases)
    aliases.update({n_i + j: n_o + k for j, k in b.aliases.items()})
    return _Comm(a.ins + b.ins, a.out_shapes + b.out_shapes, a.sems + b.sems, make, aliases)


def _comm_parts(comm, n_in, n_out):
    if comm is None:
        return [], [], [], [], {}
    alias = {n_in + j: n_out + k for j, k in comm.aliases.items()}
    return comm.ins, [_ANY] * len(comm.ins), comm.out_shapes, comm.sems, alias


def _split_refs(refs, n_in, n_out, n_scratch, comm):
    n_ci = len(comm.ins) if comm else 0
    n_co = len(comm.out_shapes) if comm else 0
    cuts, i = [], 0
    for n in (n_in, n_ci, n_out, n_co, n_scratch):
        cuts.append(refs[i:i + n])
        i += n
    return (*cuts, refs[i:])


def _grid_edge(grid, last):
    cond = None
    for d, n in enumerate(grid):
        here = pl.program_id(d) == (n - 1 if last else 0)
        cond = here if cond is None else cond & here
    return cond


def _comm_start(comm, cin, cout, csem, grid):
    if comm is not None:
        @pl.when(_grid_edge(grid, False))
        def _():
            for cp in comm.make(cin, cout, csem):
                cp.start()


def _comm_finish(comm, cin, cout, csem, grid):
    if comm is not None:
        @pl.when(_grid_edge(grid, True))
        def _():
            for cp in comm.make(cin, cout, csem):
                cp.wait()
            if comm.make_tail is not None:
                tail = comm.make_tail(cin, cout, csem)
                for cp in tail:
                    cp.start()
                for cp in tail:
                    cp.wait()


def gather_first_comm(shard):
    r, w = shard.shape

    def make(cin, cout, sems):
        (x_ref,), (out_ref,), (send_sems, recv_sems, local_sem) = cin, cout, sems
        x, y, c = _mesh_pos()
        mine = out_ref.at[4 * x + 2 * y + c]
        targets = [(x, y, 1 - c), (1 - x, y, c), (x, 1 - y, c), (1 - x, 1 - y, c)]
        return [_remote(x_ref, mine, send_sems, recv_sems, k, to) for k, to in enumerate(targets)] + [
            pltpu.make_async_copy(x_ref, mine, local_sem)]

    return _Comm([shard], [jax.ShapeDtypeStruct((N_DEV, r, w), shard.dtype)],
                 [pltpu.SemaphoreType.DMA((4,)), pltpu.SemaphoreType.DMA((4,)), pltpu.SemaphoreType.DMA], make)


def gather_pass_comm(full):
    def make(cin, cout, sems):
        (in_ref,), (out_ref,), (send_sems, recv_sems) = cin, cout, sems
        x, y, c = _mesh_pos()
        copies = []
        for k, (px, py) in enumerate([(1 - x, y), (x, 1 - y), (1 - x, 1 - y)]):
            idx = 4 * px + 2 * py + c
            copies.append(_remote(in_ref.at[idx], out_ref.at[idx], send_sems, recv_sems, k, (x, y, 1 - c)))
        return copies

    return _Comm([full], [jax.ShapeDtypeStruct(full.shape, full.dtype)],
                 [pltpu.SemaphoreType.DMA((3,)), pltpu.SemaphoreType.DMA((3,))], make, aliases={0: 0})


def gather3_first_comm(shard):
    r, w = shard.shape

    def make(cin, cout, sems):
        (x_ref,), (out_ref,), (send_sems, recv_sems, local_sem) = cin, cout, sems
        x, y, c = _mesh_pos()
        mine = out_ref.at[4 * x + 2 * y + c]
        targets = [(x, y, 1 - c), (1 - x, y, c), (x, 1 - y, c)]
        return [_remote(x_ref, mine, send_sems, recv_sems, k, to) for k, to in enumerate(targets)] + [
            pltpu.make_async_copy(x_ref, mine, local_sem)]

    return _Comm([shard], [jax.ShapeDtypeStruct((N_DEV, r, w), shard.dtype)],
                 [pltpu.SemaphoreType.DMA((3,)), pltpu.SemaphoreType.DMA((3,)), pltpu.SemaphoreType.DMA], make)


def gather3_second_comm(full, then_third=False):
    def make(cin, cout, sems):
        (in_ref,), (out_ref,), (send_sems, recv_sems) = cin, cout, sems
        x, y, c = _mesh_pos()
        copies = []
        for k, (px, py) in enumerate([(1 - x, y), (x, 1 - y)]):
            idx = 4 * px + 2 * py + c
            copies.append(_remote(in_ref.at[idx], out_ref.at[idx], send_sems, recv_sems, k, (x, y, 1 - c)))
        fx, fy = x * c + (1 - x) * (1 - c), y * (1 - c) + (1 - y) * c
        tx, ty = x * (1 - c) + (1 - x) * c, (1 - y) * (1 - c) + y * c
        idx = 4 * fx + 2 * fy + c
        copies.append(_remote(in_ref.at[idx], out_ref.at[idx], send_sems, recv_sems, 2, (tx, ty, c)))
        return copies

    def third(cin, cout, sems):
        (out_ref,), (send_sems, recv_sems) = cout, sems
        x, y, c = _mesh_pos()
        idx = 4 * (1 - x) + 2 * (1 - y) + c
        return [_remote(out_ref.at[idx], out_ref.at[idx], send_sems, recv_sems, 3, (x, y, 1 - c))]

    n_sem = 4 if then_third else 3
    return _Comm([full], [jax.ShapeDtypeStruct(full.shape, full.dtype)],
                 [pltpu.SemaphoreType.DMA((n_sem,)), pltpu.SemaphoreType.DMA((n_sem,))], make, aliases={0: 0},
                 make_tail=third if then_third else None)


def gather3_third_comm(full):
    def make(cin, cout, sems):
        (in_ref,), (out_ref,), (send_sems, recv_sems) = cin, cout, sems
        x, y, c = _mesh_pos()
        idx = 4 * (1 - x) + 2 * (1 - y) + c
        return [_remote(in_ref.at[idx], out_ref.at[idx], send_sems, recv_sems, 0, (x, y, 1 - c))]

    return _Comm([full], [jax.ShapeDtypeStruct(full.shape, full.dtype)],
                 [pltpu.SemaphoreType.DMA((1,)), pltpu.SemaphoreType.DMA((1,))], make, aliases={0: 0})


def scatter_sibling_comm(g):
    _, r, w = g.shape
    return _Comm([g], [jax.ShapeDtypeStruct((N_CHIP, r, w), g.dtype)],
                 [pltpu.SemaphoreType.DMA((N_CHIP,)), pltpu.SemaphoreType.DMA((N_CHIP,))],
                 lambda cin, cout, sems: _sibling_copies(cin[0], cout[0], sems[0], sems[1]))


def scatter_chips_comm(part):
    _, r, w = part.shape
    return _Comm([part], [jax.ShapeDtypeStruct((N_CHIP - 1, r, w), part.dtype)],
                 [pltpu.SemaphoreType.DMA((3,)), pltpu.SemaphoreType.DMA((3,))],
                 lambda cin, cout, sems: _chip_copies(cin[0], cout[0], sems[0], sems[1]))


def exchange_sibling(g):
    _, r, w = g.shape

    def body(g_ref, got_ref, send_sems, recv_sems):
        copies = _sibling_copies(g_ref, got_ref, send_sems, recv_sems)
        for cp in copies:
            cp.start()
        for cp in copies:
            cp.wait()

    return pl.pallas_call(
        body, name="exchange_sibling", out_shape=jax.ShapeDtypeStruct((N_CHIP, r, w), g.dtype),
        in_specs=[_ANY], out_specs=_ANY,
        scratch_shapes=[pltpu.SemaphoreType.DMA((N_CHIP,)), pltpu.SemaphoreType.DMA((N_CHIP,))],
    )(g)


def exchange_chips(part):
    _, r, w = part.shape

    def body(p_ref, land_ref, send_sems, recv_sems):
        copies = _chip_copies(p_ref, land_ref, send_sems, recv_sems)
        for cp in copies:
            cp.start()
        for cp in copies:
            cp.wait()

    return pl.pallas_call(
        body, name="exchange_chips", out_shape=jax.ShapeDtypeStruct((N_CHIP - 1, r, w), part.dtype),
        in_specs=[_ANY], out_specs=_ANY,
        scratch_shapes=[pltpu.SemaphoreType.DMA((3,)), pltpu.SemaphoreType.DMA((3,))],
    )(part)


def allreduce_stats(st_mix, st_prep, st_mlp, st_ple):
    def body(mix_ref, prep_ref, mlp_ref, ple_ref, out_ref, mine, gath, send_sems, recv_sems):
        x, y, c = _mesh_pos()
        me = 4 * x + 2 * y + c
        mine[...] = jnp.zeros_like(mine)
        mine[ST_G_MIX:ST_G_MIX + 1, :] = mix_ref[...]
        mine[ST_G_QA:ST_G_KN + 1, 0:256] = prep_ref[...]
        mine[ST_G_MLP:ST_G_MLP + 1, :] = mlp_ref[...]
        mine[ST_G_PLE:ST_LOSS + 1, :] = ple_ref[...]
        gath[me] = mine[...]
        copies = []
        for k in range(1, N_DEV):
            peer = (_flip(x, k & 4), _flip(y, k & 2), _flip(c, k & 1))
            copies.append(_remote(mine, gath.at[me], send_sems, recv_sems, k - 1, peer))
        for cp in copies:
            cp.start()
        for cp in copies:
            cp.wait()
        acc = gath[0]
        for d in range(1, N_DEV):
            acc = acc + gath[d]
        out_ref[...] = acc

    vm = pl.BlockSpec(memory_space=pltpu.VMEM)
    return pl.pallas_call(
        body, name="allreduce_stats", out_shape=jax.ShapeDtypeStruct((ST_ROWS, D_MODEL), F32),
        in_specs=[vm] * 4, out_specs=vm,
        scratch_shapes=[pltpu.VMEM((ST_ROWS, D_MODEL), F32), pltpu.VMEM((N_DEV, ST_ROWS, D_MODEL), F32),
                        pltpu.SemaphoreType.DMA((N_DEV - 1,)), pltpu.SemaphoreType.DMA((N_DEV - 1,))],
    )(st_mix, st_prep, st_mlp, st_ple)


def adamw_gains(stats, gains):
    c1 = 1.0 - ADAM_B1 ** ADAM_STEP
    c2 = 1.0 - ADAM_B2 ** ADAM_STEP
    n = len(gains)

    def body(st_ref, *refs):
        ins, outs = refs[:3 * n], refs[3 * n:]
        for i, (row, w, _, _) in enumerate(gains):
            width = w.shape[1]
            gv = st_ref[row:row + 1, 0:width]
            mn = ADAM_B1 * ins[3 * i + 1][...] + (1.0 - ADAM_B1) * gv
            vn = ADAM_B2 * ins[3 * i + 2][...] + (1.0 - ADAM_B2) * (gv * gv)
            outs[4 * i][...] = gv
            outs[4 * i + 1][...] = -ADAM_LR * ((mn / c1) / (jnp.sqrt(vn / c2) + ADAM_EPS) + ADAM_WD * ins[3 * i][...])
            outs[4 * i + 2][...] = mn
            outs[4 * i + 3][...] = vn

    vm = pl.BlockSpec(memory_space=pltpu.VMEM)
    flat = [a for (_, w, m, v) in gains for a in (w, m, v)]
    out_shape = tuple(jax.ShapeDtypeStruct(w.shape, F32) for (_, w, _, _) in gains for _ in range(4))
    res = pl.pallas_call(body, name="adamw_gains", out_shape=out_shape, in_specs=[vm] * (1 + 3 * n),
                         out_specs=tuple([vm] * (4 * n)))(stats, *flat)
    return [res[4 * i:4 * i + 4] for i in range(n)]


def _row_tile(r, cap=640):
    return max(d for d in range(16, min(r, cap) + 1, 16) if r % d == 0)


def add_pairs(g, got, core):
    n, r, w = got.shape
    tr = _row_tile(r)

    def body(c_ref, a_ref, b_ref, o_ref):
        o_ref[...] = (a_ref[...].astype(F32) + b_ref[...].astype(F32)).astype(o_ref.dtype)

    spec = pl.BlockSpec((1, tr, w), lambda i, j, c: (i, j, 0))
    return pl.pallas_call(
        body, name="add_pairs", out_shape=jax.ShapeDtypeStruct(got.shape, got.dtype),
        grid_spec=pltpu.PrefetchScalarGridSpec(
            num_scalar_prefetch=1, grid=(n, r // tr),
            in_specs=[pl.BlockSpec((1, tr, w), lambda i, j, c: (2 * i + c[0], j, 0)), spec], out_specs=spec),
        compiler_params=_params(("parallel", "parallel")),
    )(core, g, got)


def sum_chips(part, land, chip, row0=0, nrows=None):
    _, r, w = part.shape
    nrows = r - row0 if nrows is None else nrows
    tr = _row_tile(nrows)
    b0 = row0 // tr

    def body(c_ref, p_ref, l_ref, o_ref):
        acc = p_ref[0].astype(F32)
        for s in range(N_CHIP - 1):
            acc = acc + l_ref[s].astype(F32)
        o_ref[...] = acc

    return pl.pallas_call(
        body, name="sum_chips", out_shape=jax.ShapeDtypeStruct((nrows, w), F32),
        grid_spec=pltpu.PrefetchScalarGridSpec(
            num_scalar_prefetch=1, grid=(nrows // tr,),
            in_specs=[pl.BlockSpec((1, tr, w), lambda i, c: (c[0], b0 + i, 0)),
                      pl.BlockSpec((N_CHIP - 1, tr, w), lambda i, c: (0, b0 + i, 0))],
            out_specs=pl.BlockSpec((tr, w), lambda i, c: (i, 0))),
        compiler_params=_params(("parallel",)),
    )(chip, part, land)


def adamw_from_parts(w, part, land, chip, row_off, m, v, name, g_transposed=False):
    _, r, c = w.shape
    rows, width = (c, r) if g_transposed else (r, c)
    blk = row_off // rows
    c1 = 1.0 - ADAM_B1 ** ADAM_STEP
    c2 = 1.0 - ADAM_B2 ** ADAM_STEP

    def body(c_ref, w_ref, p_ref, l_ref, m_ref, v_ref, g_o, d_o, m_o, v_o):
        gv = p_ref[0].astype(F32)
        for s in range(N_CHIP - 1):
            gv = gv + l_ref[s].astype(F32)
        if g_transposed:
            gv = gv.T
        mn = ADAM_B1 * m_ref[0] + (1.0 - ADAM_B1) * gv
        vn = ADAM_B2 * v_ref[0] + (1.0 - ADAM_B2) * (gv * gv)
        g_o[0] = gv
        m_o[0] = mn
        v_o[0] = vn
        d_o[0] = -ADAM_LR * ((mn / c1) / (jnp.sqrt(vn / c2) + ADAM_EPS) + ADAM_WD * w_ref[0])

    wspec = pl.BlockSpec((1, r, c), lambda i, ch: (0, 0, 0))
    shp = jax.ShapeDtypeStruct((1, r, c), F32)
    return pl.pallas_call(
        body, name=name, out_shape=(shp,) * 4,
        grid_spec=pltpu.PrefetchScalarGridSpec(
            num_scalar_prefetch=1, grid=(1,),
            in_specs=[wspec, pl.BlockSpec((1, rows, width), lambda i, ch: (ch[0], blk, 0)),
                      pl.BlockSpec((N_CHIP - 1, rows, width), lambda i, ch: (0, blk, 0)), wspec, wspec],
            out_specs=(wspec,) * 4),
        compiler_params=_params(("arbitrary",)),
    )(chip, w, part, land, m, v)


def norm_x(x, g_mix, tm, comm=None):
    t = x.shape[0]
    grid = (t // tm,)
    c_ins, c_in_specs, c_outs, c_sems, alias = _comm_parts(comm, 2, 1)

    def body(*refs):
        (x_ref, g_ref), cin, (h_ref,), cout, _, csem = _split_refs(refs, 2, 1, 0, comm)
        _comm_start(comm, cin, cout, csem, grid)
        xv = x_ref[...]
        h_ref[...] = (xv * _rstd(xv, D_MODEL) * g_ref[...]).astype(BF16)
        _comm_finish(comm, cin, cout, csem, grid)

    return pl.pallas_call(
        body, name="norm_x", grid=grid, out_shape=(jax.ShapeDtypeStruct((t, D_MODEL), BF16), *c_outs),
        in_specs=[_rows(tm, D_MODEL), _resident((1, D_MODEL)), *c_in_specs],
        out_specs=(_rows(tm, D_MODEL), *([_ANY] * len(c_outs))),
        scratch_shapes=c_sems, input_output_aliases=alias, compiler_params=_params(("arbitrary",)),
    )(x, g_mix, *c_ins)


def pack_late_weights(w_up, w_down, w_o, w_pg, small, comm=None):
    rows2 = sum(PACK_ROWS[n] for n in PACK_W2)
    rows3 = sum(PACK_ROWS[n] for n in PACK_W3)
    c_ins, c_in_specs, c_outs, c_sems, alias = _comm_parts(comm, 5, 2)
    grid = (1,)

    def body(*refs):
        (up_ref, dn_ref, o_ref, pg_ref, sm_ref), cin, (p2_ref, p3_ref), cout, _, csem = _split_refs(refs, 5, 2, 0, comm)
        _comm_start(comm, cin, cout, csem, grid)
        p2_ref[0:128, :] = o_ref[0].astype(BF16)
        p2_ref[128:256, :] = pg_ref[0].astype(BF16)
        p2_ref[256:rows2, :] = sm_ref[...]
        p3_ref[0:512, :] = up_ref[0].T.astype(BF16)
        p3_ref[512:1024, :] = dn_ref[0].astype(BF16)
        _comm_finish(comm, cin, cout, csem, grid)

    def whole(a):
        nd = a.ndim
        return pl.BlockSpec(a.shape, lambda i: (0,) * nd)

    args = (w_up, w_down, w_o, w_pg, small)
    return pl.pallas_call(
        body, name="pack_late_weights", grid=grid,
        out_shape=(jax.ShapeDtypeStruct((rows2, D_MODEL), BF16), jax.ShapeDtypeStruct((rows3, D_MODEL), BF16), *c_outs),
        in_specs=[*[whole(a) for a in args], *c_in_specs],
        out_specs=(pl.BlockSpec((rows2, D_MODEL), lambda i: (0, 0)), pl.BlockSpec((rows3, D_MODEL), lambda i: (0, 0)),
                   *([_ANY] * len(c_outs))),
        scratch_shapes=c_sems, input_output_aliases=alias, compiler_params=_params(("arbitrary",)),
    )(*args, *c_ins)


def in_proj(h, w_in_t, tm, comm=None):
    t = h.shape[0]
    nc = 512
    half = ZP // 2
    grid = (t // tm,)
    c_ins, c_in_specs, c_outs, c_sems, alias = _comm_parts(comm, 2, 2)

    def body(*refs):
        (h_ref, w_ref), cin, (zg_ref, zs_ref), cout, _, csem = _split_refs(refs, 2, 2, 0, comm)
        _comm_start(comm, cin, cout, csem, grid)
        hv = h_ref[...]
        for cidx in range(half // nc):
            zg_ref[:, cidx * nc:(cidx + 1) * nc] = _dot_nt(hv, w_ref[cidx * nc:(cidx + 1) * nc, :]).astype(BF16)
        for cidx in range(half // nc):
            zs_ref[:, cidx * nc:(cidx + 1) * nc] = _dot_nt(hv, w_ref[half + cidx * nc:half + (cidx + 1) * nc, :]).astype(BF16)
        _comm_finish(comm, cin, cout, csem, grid)

    return pl.pallas_call(
        body, name="in_proj", grid=grid,
        out_shape=(jax.ShapeDtypeStruct((t, half), BF16), jax.ShapeDtypeStruct((t, half), BF16), *c_outs),
        in_specs=[_rows(tm, D_MODEL), _resident((ZP, D_MODEL)), *c_in_specs],
        out_specs=(_rows(tm, half), _rows(tm, half), *([_ANY] * len(c_outs))),
        scratch_shapes=c_sems, input_output_aliases=alias, compiler_params=_params(("arbitrary",)),
    )(h, w_in_t, *c_ins)


def attn_prep(zp, tabs, g_qa, g_kva, g_qn, g_kn, w_qb_t, w_kvb_t, tm, s_len, comm=None):
    t = zp.shape[0]
    nsb = s_len // tm
    scale_a = (QK_NOPE + QK_ROPE) ** -0.5
    scale_b = HD_B ** -0.5

    grid = (t // tm,)
    c_ins, c_in_specs, c_outs, c_sems, alias = _comm_parts(comm, 13, 8)

    def body(*refs):
        ((qb_ref, qlat_ref, kb_ref, vb_ref, ckv_ref, kpe_ref, tab_ref, gqa_ref, gkva_ref, gqn_ref, gkn_ref, wqb_ref,
          wkvb_ref), cin, (qa_o, ka_o, va_o, qb_o, kb_o, vb_o, cq_o, ckvn_o), cout, _, csem) = _split_refs(refs, 13, 8, 0, comm)
        _comm_start(comm, cin, cout, csem, grid)
        ca, s1a, s2a = tab_ref[0], tab_ref[1], tab_ref[2]
        ck = tab_ref[3]
        cb, s1b, s2b = tab_ref[4], tab_ref[5], tab_ref[6]
        ql = qlat_ref[...].astype(F32)
        cq = (ql * _rstd(ql, Q_LORA) * gqa_ref[...]).astype(BF16)
        cq_o[...] = cq
        qa = _dot_nt(cq, wqb_ref[...])
        slabs = [slice(h * HP, (h + 1) * HP) for h in range(H_A)]
        qa_o[...] = jnp.concatenate(
            [(_rope_fwd(qa[:, sl], ca, s1a, s2a) * scale_a).astype(BF16) for sl in slabs], axis=1)
        cr = ckv_ref[...].astype(F32)
        ckv = (cr * _rstd(cr, KV_LORA) * gkva_ref[...]).astype(BF16)
        ckvn_o[...] = ckv
        kva = _dot_nt(ckv, wkvb_ref[...])
        kpe = _rope_fwd(kpe_ref[...].astype(F32), ck, s1a, s2a)
        ka_o[...] = jnp.concatenate([(kva[:, sl] + kpe).astype(BF16) for sl in slabs], axis=1)
        va_o[...] = kva[:, H_A * HP:].astype(BF16)
        gqn, gkn = gqn_ref[...], gkn_ref[...]

        def norm_rope(ref, sl, g, scale):
            xs = ref[:, sl].astype(F32)
            y = _rope_fwd(xs * _rstd(xs, HD_B) * g, cb, s1b, s2b)
            return (y if scale is None else y * scale).astype(BF16)

        qb_o[...] = jnp.concatenate([norm_rope(qb_ref, sl, gqn, scale_b) for sl in slabs], axis=1)
        kb_o[...] = jnp.concatenate([norm_rope(kb_ref, sl, gkn, None) for sl in slabs[:KV_B]], axis=1)
        vb_o[...] = vb_ref[...].astype(BF16)
        _comm_finish(comm, cin, cout, csem, grid)

    def o(width):
        return jax.ShapeDtypeStruct((t, width), BF16)

    return pl.pallas_call(
        body, name="attn_prep", grid=grid,
        out_shape=(o(H_A * HP), o(H_A * HP), o(H_A * HP), o(H_B * HP), o(KV_B * HP), o(KV_B * HP), o(Q_LORA), o(KV_LORA),
                   *c_outs),
        in_specs=[_rows(tm, 1024, 0), _rows(tm, 256, 4), _rows(tm, 256, 5), _rows(tm, 256, 6),
                  _rows(tm, 128, 14), _rows(tm, 128, 15),
                  pl.BlockSpec((7, tm, HP), lambda i: (0, i % nsb, 0)),
                  _resident((1, Q_LORA)), _resident((1, KV_LORA)), _resident((1, HP)), _resident((1, HP)),
                  _resident((H_A * HP, Q_LORA)), _resident((2 * H_A * HP, KV_LORA)), *c_in_specs],
        out_specs=(_rows(tm, H_A * HP), _rows(tm, H_A * HP), _rows(tm, H_A * HP), _rows(tm, H_B * HP),
                   _rows(tm, KV_B * HP), _rows(tm, KV_B * HP), _rows(tm, Q_LORA), _rows(tm, KV_LORA), *([_ANY] * len(c_outs))),
        scratch_shapes=c_sems, input_output_aliases=alias, compiler_params=_params(("arbitrary",)),
    )(zp, zp, zp, zp, zp, zp, tabs, g_qa, g_kva, g_qn, g_kn, w_qb_t, w_kvb_t, *c_ins)


def attn_fwd(q, k, v, n_b, s_len, tq, name, comm=None):
    t = q.shape[0]
    n_h, n_hk = q.shape[1] // HP, k.shape[1] // HP
    grp = n_h // n_hk
    nq = s_len // tq
    sub = min(tq, 256)
    grid = (n_b, n_h, nq)
    c_ins, c_in_specs, c_outs, c_sems, alias = _comm_parts(comm, 3, 2)

    def body(*refs):
        (q_ref, k_ref, v_ref), cin, (o_ref, lse_ref), cout, _, csem = _split_refs(refs, 3, 2, 0, comm)
        _comm_start(comm, cin, cout, csem, grid)
        kv, vv = k_ref[...], v_ref[...]
        for r in range(tq // sub):
            rows = slice(r * sub, (r + 1) * sub)
            s = _dot_nt(q_ref[rows, :], kv)
            m = jnp.max(s, axis=-1, keepdims=True)
            p = jnp.exp(s - m)
            l = jnp.sum(p, axis=-1, keepdims=True)
            o_ref[rows, :] = (_dot_nn(p.astype(BF16), vv) * (1.0 / l)).astype(o_ref.dtype)
            lse_ref[rows, :] = jnp.broadcast_to(m + jnp.log(l), (sub, HP))
        _comm_finish(comm, cin, cout, csem, grid)

    qspec = pl.BlockSpec((tq, HP), lambda b, h, i: (b * nq + i, h))
    kspec = pl.BlockSpec((s_len, HP), lambda b, h, i: (b, h // grp))
    return pl.pallas_call(
        body, name=name, grid=grid,
        out_shape=(jax.ShapeDtypeStruct((t, n_h * HP), BF16), jax.ShapeDtypeStruct((t, n_h * HP), F32), *c_outs),
        in_specs=[qspec, kspec, kspec, *c_in_specs], out_specs=(qspec, qspec, *([_ANY] * len(c_outs))),
        scratch_shapes=c_sems, input_output_aliases=alias,
        compiler_params=_params(("arbitrary", "arbitrary", "arbitrary")),
    )(q, k, v, *c_ins)


def merge_fwd(oa, ob, zp, x, w_oa_t, w_ob_t, wpack, off, tm):
    t = x.shape[0]

    def body(oa_ref, ob_ref, ga_ref, gb_ref, x_ref, woa_ref, wob_ref, wo_ref, x1_o, mg_o, y_o):
        ya = _dot_nt(oa_ref[...], woa_ref[...])
        yb = _dot_nt(ob_ref[...], wob_ref[...])
        y_o[:, 0:D_MODEL] = ya.astype(BF16)
        y_o[:, D_MODEL:2 * D_MODEL] = yb.astype(BF16)
        merged = (jax.nn.sigmoid(ga_ref[...].astype(F32)) * ya + jax.nn.sigmoid(gb_ref[...].astype(F32)) * yb).astype(BF16)
        mg_o[...] = merged
        x1_o[...] = x_ref[...] + _dot_nn(merged, _wrows(wo_ref, 0, D_MODEL))

    return pl.pallas_call(
        body, name="merge_fwd", grid=(t // tm,),
        out_shape=(jax.ShapeDtypeStruct((t, D_MODEL), F32), jax.ShapeDtypeStruct((t, D_MODEL), BF16),
                   jax.ShapeDtypeStruct((t, 2 * D_MODEL), BF16)),
        in_specs=[_rows(tm, H_A * HP), _rows(tm, H_B * HP), _rows(tm, 1024, 0), _rows(tm, 1024, 1), _rows(tm, D_MODEL),
                  _resident((D_MODEL, H_A * HP)), _resident((D_MODEL, H_B * HP)), _packed_weight(128, off["w_o"])],
        out_specs=(_rows(tm, D_MODEL), _rows(tm, D_MODEL), _rows(tm, 2 * D_MODEL)), compiler_params=_params(("parallel",)),
    )(oa, ob, zp, zp, x, w_oa_t, w_ob_t, wpack)


def mlp_fwd(x1, g_mlp, wpack, off, tm):
    t = x1.shape[0]
    fc = 1024

    def body(x_ref, g_ref, wup_ref, wdn_ref, x2_o, u_o):
        xv = x_ref[...]
        h2 = (xv * _rstd(xv, D_MODEL) * g_ref[...]).astype(BF16)
        acc = xv
        for cidx in range(D_FF // fc):
            sl = slice(cidx * fc, (cidx + 1) * fc)
            u = jnp.maximum(_dot_nt(h2, _wrows(wup_ref, cidx * fc, fc)), 0.0)
            u_o[:, sl] = u.astype(BF16)
            acc = acc + _dot_nn((u * u).astype(BF16), _wrows(wdn_ref, cidx * fc, fc))
        x2_o[...] = acc

    return pl.pallas_call(
        body, name="mlp_fwd", grid=(t // tm,),
        out_shape=(jax.ShapeDtypeStruct((t, D_MODEL), F32), jax.ShapeDtypeStruct((t, D_FF), BF16)),
        in_specs=[_rows(tm, D_MODEL), _resident((1, D_MODEL)), _packed_weight(512, off["w_up"]), _packed_weight(512, off["w_down"])],
        out_specs=(_rows(tm, D_MODEL), _rows(tm, D_FF)), compiler_params=_params(("parallel",)),
    )(x1, g_mlp, wpack, wpack)


def ple_loss_bwd(x2, p, tgt, g_ple, g_final, wpack, off, w_ple_t, tm):
    t = x2.shape[0]
    inv_d = 1.0 / D_MODEL

    def body(x2_ref, p_ref, tg_ref, gp_ref, gf_ref, wpg_ref, wple_ref, dx2_o, dt_o, h3_o, dpe_o, st_o, dx2b_o):
        @pl.when(pl.program_id(0) == 0)
        def _():
            st_o[...] = jnp.zeros_like(st_o)

        x2v = x2_ref[...]
        gp, gf = gp_ref[...], gf_ref[...]
        w_pg = _wrows(wpg_ref, 0, D_MODEL)
        r2 = _rstd(x2v, D_MODEL)
        xh2 = x2v * r2
        h3 = (xh2 * gp).astype(BF16)
        h3_o[...] = h3
        gate = jax.nn.sigmoid(_dot_nn(h3, w_pg))
        pe = _dot_nt(p_ref[...].astype(BF16), wple_ref[...])
        x3 = x2v + gate * pe
        r3 = _rstd(x3, D_MODEL)
        xh3 = x3 * r3
        err = xh3 * gf - tg_ref[...]
        dx3 = _rms_bwd(err, xh3, r3, gf * inv_d, D_MODEL)
        dpe = dx3 * gate
        dpe_o[...] = dpe.astype(BF16)
        dt = (dpe * pe * (1.0 - gate)).astype(BF16)
        dt_o[...] = dt
        dh3 = _dot_nt(dt, w_pg)
        dx2 = dx3 + _rms_bwd(dh3, xh2, r2, gp, D_MODEL)
        dx2_o[...] = dx2
        dx2b_o[...] = dx2.astype(BF16)
        st_o[0:1, :] += _colsum(dh3 * xh2)
        st_o[1:2, :] += _colsum(err * xh3) * inv_d
        st_o[2:3, :] += _colsum(err * err) * (0.5 * inv_d)

    bf = jax.ShapeDtypeStruct((t, D_MODEL), BF16)
    return pl.pallas_call(
        body, name="ple_loss_bwd", grid=(t // tm,),
        out_shape=(jax.ShapeDtypeStruct((t, D_MODEL), F32), bf, bf, bf, jax.ShapeDtypeStruct((3, D_MODEL), F32), bf),
        in_specs=[_rows(tm, D_MODEL), _rows(tm, PLE_DIM), _rows(tm, D_MODEL), _resident((1, D_MODEL)), _resident((1, D_MODEL)),
                  _packed_weight(128, off["w_ple_gate"]), _resident((D_MODEL, PLE_DIM))],
        out_specs=(_rows(tm, D_MODEL), _rows(tm, D_MODEL), _rows(tm, D_MODEL), _rows(tm, D_MODEL),
                   pl.BlockSpec((3, D_MODEL), lambda i: (0, 0)), _rows(tm, D_MODEL)),
        compiler_params=_params(("arbitrary",)),
    )(x2, p, tgt, g_ple, g_final, wpack, w_ple_t)


def mlp_bwd(dx2, x1, u, g_mlp, wpack, off, tm):
    t = x1.shape[0]
    fc = 1024

    def body(dx2_ref, x1_ref, u_ref, g_ref, wup_ref, wdn_ref, dx1_o, da_o, h2_o, st_o, dx1b_o):
        @pl.when(pl.program_id(0) == 0)
        def _():
            st_o[...] = jnp.zeros_like(st_o)

        d2 = dx2_ref[...]
        d2b = d2.astype(BF16)
        dh2 = jnp.zeros((tm, D_MODEL), F32)
        for cidx in range(D_FF // fc):
            sl = slice(cidx * fc, (cidx + 1) * fc)
            da = (_dot_nt(d2b, _wrows(wdn_ref, cidx * fc, fc)) * (2.0 * u_ref[:, sl].astype(F32))).astype(BF16)
            da_o[:, sl] = da
            dh2 = dh2 + _dot_nn(da, _wrows(wup_ref, cidx * fc, fc))
        xv = x1_ref[...]
        g = g_ref[...]
        r1 = _rstd(xv, D_MODEL)
        xh1 = xv * r1
        h2_o[...] = (xh1 * g).astype(BF16)
        st_o[...] += _colsum(dh2 * xh1)
        dx1 = d2 + _rms_bwd(dh2, xh1, r1, g, D_MODEL)
        dx1_o[...] = dx1
        dx1b_o[...] = dx1.astype(BF16)

    return pl.pallas_call(
        body, name="mlp_bwd", grid=(t // tm,),
        out_shape=(jax.ShapeDtypeStruct((t, D_MODEL), F32), jax.ShapeDtypeStruct((t, D_FF), BF16),
                   jax.ShapeDtypeStruct((t, D_MODEL), BF16), jax.ShapeDtypeStruct((1, D_MODEL), F32),
                   jax.ShapeDtypeStruct((t, D_MODEL), BF16)),
        in_specs=[_rows(tm, D_MODEL), _rows(tm, D_MODEL), _rows(tm, D_FF), _resident((1, D_MODEL)),
                  _packed_weight(512, off["w_up"]), _packed_weight(512, off["w_down"])],
        out_specs=(_rows(tm, D_MODEL), _rows(tm, D_FF), _rows(tm, D_MODEL), pl.BlockSpec((1, D_MODEL), lambda i: (0, 0)),
                   _rows(tm, D_MODEL)),
        compiler_params=_params(("arbitrary",)),
    )(dx2, x1, u, g_mlp, wpack, wpack)


def merge_bwd(dx1b, yab, zp, w_oa_t, w_ob_t, wpack, off, tm, comm=None):
    t = dx1b.shape[0]
    grid = (t // tm,)
    c_ins, c_in_specs, c_outs, c_sems, alias = _comm_parts(comm, 7, 5)

    def body(*refs):
        ((dx1_ref, y_ref, ga_ref, gb_ref, woa_ref, wob_ref, wo_ref), cin,
         (doa_o, dob_o, dg_o, dya_o, dyb_o), cout, _, csem) = _split_refs(refs, 7, 5, 0, comm)
        _comm_start(comm, cin, cout, csem, grid)
        dm = _dot_nt(dx1_ref[...], _wrows(wo_ref, 0, D_MODEL))
        for g_ref, w_ref, do_o, dy_o, col in ((ga_ref, woa_ref, doa_o, dya_o, 0), (gb_ref, wob_ref, dob_o, dyb_o, 1)):
            cols = slice(col * D_MODEL, (col + 1) * D_MODEL)
            sg = jax.nn.sigmoid(g_ref[...].astype(F32))
            dyv = (dm * sg).astype(BF16)
            dy_o[...] = dyv
            dg_o[:, cols] = (dm * y_ref[:, cols].astype(F32) * sg * (1.0 - sg)).astype(BF16)
            do_o[...] = _dot_nn(dyv, w_ref[...]).astype(BF16)
        _comm_finish(comm, cin, cout, csem, grid)

    bf = jax.ShapeDtypeStruct((t, D_MODEL), BF16)
    return pl.pallas_call(
        body, name="merge_bwd", grid=grid,
        out_shape=(bf, bf, jax.ShapeDtypeStruct((t, ZP), BF16), bf, bf, *c_outs),
        in_specs=[_rows(tm, D_MODEL), _rows(tm, 2 * D_MODEL), _rows(tm, 1024, 0), _rows(tm, 1024, 1),
                  _resident((D_MODEL, H_A * HP)), _resident((D_MODEL, H_B * HP)), _packed_weight(128, off["w_o"]), *c_in_specs],
        out_specs=(_rows(tm, D_MODEL), _rows(tm, D_MODEL), _rows(tm, 2 * D_MODEL), _rows(tm, D_MODEL), _rows(tm, D_MODEL),
                   *([_ANY] * len(c_outs))),
        scratch_shapes=c_sems, input_output_aliases=alias,
        compiler_params=_params(("arbitrary",)),
    )(dx1b, yab, zp, zp, w_oa_t, w_ob_t, wpack, *c_ins)


def attn_bwd(q, k, v, do, o, lse, n_b, s_len, tq, name, comm=None):
    t = q.shape[0]
    n_h, n_hk = q.shape[1] // HP, k.shape[1] // HP
    grp = n_h // n_hk
    nq = s_len // tq
    sub = min(tq, 256)
    grid = (n_b, n_hk, grp, nq)
    c_ins, c_in_specs, c_outs, c_sems, alias = _comm_parts(comm, 6, 3)

    def body(*refs):
        ((q_ref, k_ref, v_ref, do_ref, o_ref, lse_ref), cin, (dq_o, dk_o, dv_o), cout, (p_s, ds_s, dk_acc, dv_acc),
         csem) = _split_refs(refs, 6, 3, 4, comm)
        _comm_start(comm, cin, cout, csem, grid)

        @pl.when((pl.program_id(2) == 0) & (pl.program_id(3) == 0))
        def _():
            dk_acc[...] = jnp.zeros_like(dk_acc)
            dv_acc[...] = jnp.zeros_like(dv_acc)

        kv, vv = k_ref[...], v_ref[...]
        for r in range(tq // sub):
            rows = slice(r * sub, (r + 1) * sub)
            qv, dov = q_ref[rows, :], do_ref[rows, :]
            delta = jnp.sum(dov.astype(F32) * o_ref[rows, :].astype(F32), axis=-1, keepdims=True)
            delta_row = jnp.broadcast_to(delta, (sub, HP)).T[0:1, :]
            lse_row = lse_ref[rows, :].T[0:1, :]
            pt = jnp.exp(_dot_nt(kv, qv) - lse_row)
            dst = (pt * (_dot_nt(vv, dov) - delta_row)).astype(BF16)
            p_s[:, rows] = pt.astype(BF16)
            ds_s[:, rows] = dst
            dq_o[rows, :] = _dot_tn(dst, kv).astype(dq_o.dtype)
        dk_acc[...] += _dot_nn(ds_s[...], q_ref[...])
        dv_acc[...] += _dot_nn(p_s[...], do_ref[...])

        @pl.when((pl.program_id(2) == grp - 1) & (pl.program_id(3) == nq - 1))
        def _():
            dk_o[...] = dk_acc[...].astype(dk_o.dtype)
            dv_o[...] = dv_acc[...].astype(dv_o.dtype)

        _comm_finish(comm, cin, cout, csem, grid)

    qspec = pl.BlockSpec((tq, HP), lambda b, hk, g, i: (b * nq + i, hk * grp + g))
    kspec = pl.BlockSpec((s_len, HP), lambda b, hk, g, i: (b, hk))
    return pl.pallas_call(
        body, name=name, grid=grid,
        out_shape=(jax.ShapeDtypeStruct((t, n_h * HP), BF16), jax.ShapeDtypeStruct((t, n_hk * HP), BF16),
                   jax.ShapeDtypeStruct((t, n_hk * HP), BF16), *c_outs),
        in_specs=[qspec, kspec, kspec, qspec, qspec, qspec, *c_in_specs],
        out_specs=(qspec, kspec, kspec, *([_ANY] * len(c_outs))),
        scratch_shapes=[pltpu.VMEM((s_len, tq), BF16), pltpu.VMEM((s_len, tq), BF16),
                        pltpu.VMEM((s_len, HP), F32), pltpu.VMEM((s_len, HP), F32), *c_sems],
        input_output_aliases=alias,
        compiler_params=_params(("arbitrary", "arbitrary", "arbitrary", "arbitrary")),
    )(q, k, v, do, o, lse, *c_ins)


def prep_bwd(dqa, dka, dva, dqb, dkb, dvb, zp, dz, tabs, g_qa, g_kva, g_qn, g_kn, w_qb_t, w_kvb_t, tm, s_len):
    t = zp.shape[0]
    nsb = s_len // tm
    scale_a = (QK_NOPE + QK_ROPE) ** -0.5
    scale_b = HD_B ** -0.5

    def body(dqa_ref, dka_ref, dva_ref, dqb_ref, dkb_ref, dvb_ref, qb_ref, qlat_ref, kb_ref, ckv_ref, tab_ref,
             gqa_ref, gkva_ref, gqn_ref, gkn_ref, wqb_ref, wkvb_ref, _, dz_o, dqap_o, dkva_o, st_o):
        dzq_o, dsm_o = dz_o.at[:, 0:1024], dz_o.at[:, 1024:2048]

        @pl.when(pl.program_id(0) == 0)
        def _():
            st_o[...] = jnp.zeros_like(st_o)

        ca, s1a, s2a = tab_ref[0], tab_ref[1], tab_ref[2]
        ck = tab_ref[3]
        cb, s1b, s2b = tab_ref[4], tab_ref[5], tab_ref[6]
        for h in range(H_A):
            sl = slice(h * HP, (h + 1) * HP)
            dqap_o[:, sl] = _rope_bwd(dqa_ref[:, sl].astype(F32) * scale_a, ca, s1a, s2a).astype(BF16)
        dcq = _dot_nn(dqap_o[...], wqb_ref[...])
        ql = qlat_ref[...].astype(F32)
        rq = _rstd(ql, Q_LORA)
        xh = ql * rq
        gqa = gqa_ref[...]
        st_o[0:1, :] += _colsum(dcq * xh)
        dsm_o[:, 0:256] = _rms_bwd(dcq, xh, rq, gqa, Q_LORA).astype(BF16)
        dkpe = jnp.zeros((tm, HP), F32)
        for h in range(H_A):
            sl = slice(h * HP, (h + 1) * HP)
            dk = dka_ref[:, sl]
            dkpe = dkpe + dk.astype(F32)
            dkva_o[:, sl] = dk.astype(BF16)
        dkva_o[:, H_A * HP:] = dva_ref[...].astype(BF16)
        dsm_o[:, 896:1024] = _rope_bwd(dkpe, ck, s1a, s2a).astype(BF16)
        dckv = _dot_nn(dkva_o[...], wkvb_ref[...])
        cr = ckv_ref[...].astype(F32)
        rk = _rstd(cr, KV_LORA)
        xh = cr * rk
        st_o[1:2, 0:128] += _colsum(dckv * xh)
        dsm_o[:, 768:896] = _rms_bwd(dckv, xh, rk, gkva_ref[...], KV_LORA).astype(BF16)
        gqn, gkn = gqn_ref[...], gkn_ref[...]
        dgq = jnp.zeros((1, HP), F32)
        for h in range(H_B):
            sl = slice(h * HP, (h + 1) * HP)
            dy = _rope_bwd(dqb_ref[:, sl].astype(F32) * scale_b, cb, s1b, s2b)
            xs = qb_ref[:, sl].astype(F32)
            r = _rstd(xs, HD_B)
            xh = xs * r
            dgq = dgq + _colsum(dy * xh)
            dzq_o[:, sl] = _rms_bwd(dy, xh, r, gqn, HD_B).astype(BF16)
        st_o[2:3, 0:128] += dgq
        dgk = jnp.zeros((1, HP), F32)
        for h in range(KV_B):
            sl = slice(h * HP, (h + 1) * HP)
            dy = _rope_bwd(dkb_ref[:, sl].astype(F32), cb, s1b, s2b)
            xs = kb_ref[:, sl].astype(F32)
            r = _rstd(xs, HD_B)
            xh = xs * r
            dgk = dgk + _colsum(dy * xh)
            dsm_o[:, 256 + h * HP:256 + (h + 1) * HP] = _rms_bwd(dy, xh, r, gkn, HD_B).astype(BF16)
        st_o[3:4, 0:128] += dgk
        dsm_o[:, 512:768] = dvb_ref[...].astype(BF16)

    return pl.pallas_call(
        body, name="prep_bwd", grid=(t // tm,),
        out_shape=(jax.ShapeDtypeStruct((t, ZP), BF16), jax.ShapeDtypeStruct((t, 1024), BF16),
                   jax.ShapeDtypeStruct((t, 2048), BF16), jax.ShapeDtypeStruct((4, 256), F32)),
        in_specs=[_rows(tm, 1024), _rows(tm, 1024), _rows(tm, 1024), _rows(tm, 1024), _rows(tm, 256), _rows(tm, 256),
                  _rows(tm, 1024, 0), _rows(tm, 256, 4), _rows(tm, 256, 5), _rows(tm, 128, 14),
                  pl.BlockSpec((7, tm, HP), lambda i: (0, i % nsb, 0)),
                  _resident((1, Q_LORA)), _resident((1, KV_LORA)), _resident((1, HP)), _resident((1, HP)),
                  _resident((H_A * HP, Q_LORA)), _resident((2 * H_A * HP, KV_LORA)), _ANY],
        out_specs=(_rows(tm, 2048, 1), _rows(tm, 1024), _rows(tm, 2048), pl.BlockSpec((4, 256), lambda i: (0, 0))),
        input_output_aliases={17: 0}, compiler_params=_params(("arbitrary",)),
    )(dqa, dka, dva, dqb, dkb, dvb, zp, zp, zp, zp, tabs, g_qa, g_kva, g_qn, g_kn, w_qb_t, w_kvb_t, dz)


def in_bwd(dz, x, dx1, g_mix, w_in_t, tm, comm=None):
    t = x.shape[0]
    grid = (t // tm,)
    c_ins, c_in_specs, c_outs, c_sems, alias = _comm_parts(comm, 5, 2)

    def body(*refs):
        (dz_ref, x_ref, dx1_ref, g_ref, w_ref), cin, (dx_o, st_o), cout, _, csem = _split_refs(refs, 5, 2, 0, comm)
        _comm_start(comm, cin, cout, csem, grid)

        @pl.when(pl.program_id(0) == 0)
        def _():
            st_o[...] = jnp.zeros_like(st_o)

        dh = _dot_nn(dz_ref[...], w_ref[...])
        xv = x_ref[...]
        g = g_ref[...]
        r = _rstd(xv, D_MODEL)
        xh = xv * r
        st_o[...] += _colsum(dh * xh)
        dx_o[...] = dx1_ref[...] + _rms_bwd(dh, xh, r, g, D_MODEL)
        _comm_finish(comm, cin, cout, csem, grid)

    return pl.pallas_call(
        body, name="in_bwd", grid=grid,
        out_shape=(jax.ShapeDtypeStruct((t, D_MODEL), F32), jax.ShapeDtypeStruct((1, D_MODEL), F32), *c_outs),
        in_specs=[_rows(tm, ZP), _rows(tm, D_MODEL), _rows(tm, D_MODEL),
                  _resident((1, D_MODEL)), _resident((ZP, D_MODEL)), *c_in_specs],
        out_specs=(_rows(tm, D_MODEL), pl.BlockSpec((1, D_MODEL), lambda i: (0, 0)), *([_ANY] * len(c_outs))),
        scratch_shapes=c_sems, input_output_aliases=alias,
        compiler_params=_params(("arbitrary",)),
    )(dz, x, dx1, g_mix, w_in_t, *c_ins)


def matmul_tn(a, b, name):
    t, m = a.shape
    n = b.shape[1]
    bm = min(m, 512)

    def body(a_ref, b_ref, o_ref):
        o_ref[...] = _dot_tn(a_ref[...].astype(BF16), b_ref[...].astype(BF16)).astype(BF16)

    return pl.pallas_call(
        body, name=name, grid=(m // bm,), out_shape=jax.ShapeDtypeStruct((m, n), BF16),
        in_specs=[pl.BlockSpec((t, bm), lambda i: (0, i)), pl.BlockSpec((t, n), lambda i: (0, 0))],
        out_specs=pl.BlockSpec((bm, n), lambda i: (i, 0)),
        compiler_params=_params(("parallel",)),
    )(a, b)


def matmul_tn_packed(a, b, name, rows, row_off, total_rows, buf=None, square_a=False):
    t, m = a.shape
    n = b.shape[1]
    pd = max(1, 512 // rows)
    bm = pd * rows
    tk = min(t, 4096)
    nk = t // tk

    def body(a_ref, b_ref, *rest):
        o_ref, acc = rest[-2], rest[-1]

        @pl.when(pl.program_id(1) == 0)
        def _():
            acc[...] = jnp.zeros_like(acc)

        av = a_ref[...]
        if square_a:
            av = (av.astype(F32) * av.astype(F32))
        acc[...] += _dot_tn(av.astype(BF16), b_ref[...].astype(BF16))

        @pl.when(pl.program_id(1) == nk - 1)
        def _():
            o_ref[...] = acc[...].reshape(pd, rows, n).astype(o_ref.dtype)

    in_specs = [pl.BlockSpec((tk, bm), lambda i, kk: (kk, i)), pl.BlockSpec((tk, n), lambda i, kk: (kk, 0))]
    args = [a, b]
    if buf is not None:
        in_specs.append(_ANY)
        args.append(buf)
    return pl.pallas_call(
        body, name=name, grid=(m // bm, nk), out_shape=jax.ShapeDtypeStruct((N_DEV, total_rows, n), BF16),
        in_specs=in_specs, out_specs=pl.BlockSpec((pd, rows, n), lambda i, kk: (i, row_off // rows, 0)),
        scratch_shapes=[pltpu.VMEM((bm, n), F32)], input_output_aliases={2: 0} if buf is not None else {},
        compiler_params=_params(("parallel", "arbitrary")),
    )(*args)


def adamw(w, g, m, v, name, g_transposed=False):
    _, r, c = w.shape
    tr = 256 if (not g_transposed and r > 256 and r % 256 == 0) else r
    c1 = 1.0 - ADAM_B1 ** ADAM_STEP
    c2 = 1.0 - ADAM_B2 ** ADAM_STEP

    def body(w_ref, g_ref, m_ref, v_ref, g_o, d_o, m_o, v_o):
        gv = g_ref[...].T if g_transposed else g_ref[...]
        mn = ADAM_B1 * m_ref[0] + (1.0 - ADAM_B1) * gv
        vn = ADAM_B2 * v_ref[0] + (1.0 - ADAM_B2) * (gv * gv)
        g_o[0] = gv
        m_o[0] = mn
        v_o[0] = vn
        d_o[0] = -ADAM_LR * ((mn / c1) / (jnp.sqrt(vn / c2) + ADAM_EPS) + ADAM_WD * w_ref[0])

    spec = pl.BlockSpec((1, tr, c), lambda i: (0, i, 0))
    gspec = pl.BlockSpec((c, r), lambda i: (0, 0)) if g_transposed else pl.BlockSpec((tr, c), lambda i: (i, 0))
    shp = jax.ShapeDtypeStruct((1, r, c), F32)
    return pl.pallas_call(
        body, name=name, grid=(r // tr,), out_shape=(shp,) * 4, in_specs=[spec, gspec, spec, spec], out_specs=(spec,) * 4,
        compiler_params=_params(("parallel",)),
    )(w, g, m, v)


def _rope_tables(s_len):
    def angles(pos, dim):
        inv = np.float32(ROPE_THETA) ** (-np.arange(0, dim, 2, dtype=np.float32) / np.float32(dim))
        return pos.astype(np.float32)[:, None] * inv[None, :]

    tpos = np.arange(s_len)
    a1 = angles(tpos, QK_ROPE)
    ar = angles(tpos // GRID_W, HD_B // 2)
    ac = angles(tpos % GRID_W, HD_B // 2)
    z16 = np.zeros((s_len, 16), np.float32)
    z32 = np.zeros((s_len, 32), np.float32)
    z64 = np.zeros((s_len, 64), np.float32)
    one64 = np.ones((s_len, 64), np.float32)
    c1, s1 = np.cos(a1), np.sin(a1)
    ca = np.concatenate([one64, c1, c1, z32], axis=1)
    ck = np.concatenate([z64, c1, c1, z32], axis=1)
    s1a = np.concatenate([z64, -s1, z16, z32], axis=1)
    s2a = np.concatenate([z64, z16, s1, z32], axis=1)
    cr, sr, cc, sc = np.cos(ar), np.sin(ar), np.cos(ac), np.sin(ac)
    cb = np.concatenate([cr, cr, cc, cc, z64], axis=1)
    s1b = np.concatenate([-sr, z16, -sc, z16, z64], axis=1)
    s2b = np.concatenate([z16, sr, z16, sc, z64], axis=1)
    return jnp.asarray(np.stack([ca, s1a, s2a, ck, cb, s1b, s2b]).astype(np.float32))


def _pad_heads(a, n_heads, axis):
    shp = a.shape
    a = a.reshape(shp[:axis] + (n_heads, shp[axis] // n_heads) + shp[axis + 1:])
    pad = [(0, 0)] * a.ndim
    pad[axis + 1] = (0, HP - a.shape[axis + 1])
    a = jnp.pad(a, pad)
    return a.reshape(shp[:axis] + (n_heads * HP,) + shp[axis + 1:])


def _unpad_heads(a, n_heads, width, axis):
    shp = a.shape
    a = a.reshape(shp[:axis] + (n_heads, HP) + shp[axis + 1:])
    a = lax.slice_in_dim(a, 0, width, axis=axis + 1)
    return a.reshape(shp[:axis] + (n_heads * width,) + shp[axis + 1:])


def _pack_rows(blocks, names):
    parts = []
    for name in names:
        b = blocks[name]
        padr = PACK_ROWS[name] - b.shape[-2]
        if padr:
            b = jnp.pad(b, [(0, 0)] * (b.ndim - 2) + [(0, padr), (0, 0)])
        parts.append(b)
    return jnp.concatenate(parts, axis=parts[0].ndim - 2)


def _expand_w_in(wt):
    z64 = jnp.zeros((64, D_MODEL), wt.dtype)
    z32 = jnp.zeros((32, D_MODEL), wt.dtype)
    return jnp.concatenate([
        wt[1184:2208], wt[2208:3232], _pad_heads(wt[416:928], H_B, 0), wt[0:256],
        _pad_heads(wt[928:1056], KV_B, 0), _pad_heads(wt[1056:1184], KV_B, 0), wt[256:384],
        z64, wt[384:416], z32], axis=0)


def _collapse_w_in(dw):
    dg, dq, ds = dw[0:2048], dw[2048:3072], dw[3072:4096]
    return jnp.concatenate([
        ds[0:256], ds[768:896], ds[960:992], _unpad_heads(dq, H_B, HD_B, 0), _unpad_heads(ds[256:512], KV_B, HD_B, 0),
        _unpad_heads(ds[512:768], KV_B, HD_B, 0), dg], axis=0)


def kernel(x, p, g_mix, w_in, g_qa, w_qb, g_kva, w_kvb, g_qn, g_kn, w_oa, w_ob, w_o, g_mlp, w_up, w_down, g_ple, w_ple_gate, w_ple, g_final, loss_target, m_g_mix, m_w_in, m_g_qa, m_w_qb, m_g_kva, m_w_kvb, m_g_qn, m_g_kn, m_w_oa, m_w_ob, m_w_o, m_g_mlp, m_w_up, m_w_down, m_g_ple, m_w_ple_gate, m_w_ple, m_g_final, v_g_mix, v_w_in, v_g_qa, v_w_qb, v_g_kva, v_w_kvb, v_g_qn, v_g_kn, v_w_oa, v_w_ob, v_w_o, v_g_mlp, v_w_up, v_w_down, v_g_ple, v_w_ple_gate, v_w_ple, v_g_final):
    n_b, s_len, _ = x.shape
    t = n_b * s_len
    tm = min(512, s_len)
    tq_f = min(2048, s_len)
    tq_b = min(2048, s_len)

    mats = dict(w_in=(w_in, m_w_in, v_w_in), w_qb=(w_qb, m_w_qb, v_w_qb), w_kvb=(w_kvb, m_w_kvb, v_w_kvb),
                w_oa=(w_oa, m_w_oa, v_w_oa), w_ob=(w_ob, m_w_ob, v_w_ob), w_o=(w_o, m_w_o, v_w_o),
                w_up=(w_up, m_w_up, v_w_up), w_down=(w_down, m_w_down, v_w_down),
                w_ple_gate=(w_ple_gate, m_w_ple_gate, v_w_ple_gate), w_ple=(w_ple, m_w_ple, v_w_ple))
    col_sharded = ("w_in", "w_qb", "w_kvb", "w_oa", "w_ob", "w_up", "w_ple")

    blocks = {}
    for name in PACK_W1 + ("w_oa", "w_ob", "w_ple"):
        blocks[name] = mats[name][0][0].T.reshape(-1, D_MODEL).astype(BF16)
    off_w1, _ = _pack_offsets(PACK_W1)
    off_w2, _ = _pack_offsets(PACK_W2)
    off_w3, _ = _pack_offsets(PACK_W3)
    xf = x.reshape(t, D_MODEL)
    h, full1 = norm_x(xf, g_mix, tm, comm=gather3_first_comm(_pack_rows(blocks, PACK_W1)))
    pack2, pack3, full1 = pack_late_weights(w_up, w_down, w_o, w_ple_gate, _pack_rows(blocks, ("w_oa", "w_ob", "w_ple")),
                                            comm=gather3_second_comm(full1, then_third=True))

    def gathered(full, offs, name, rows, width):
        return full[:, offs[name]:offs[name] + rows].reshape(-1, width)

    w_in_t = _expand_w_in(gathered(full1, off_w1, "w_in", 404, D_MODEL))
    w_qb_t = _pad_heads(gathered(full1, off_w1, "w_qb", 24, Q_LORA), H_A, 0)
    wkvb = gathered(full1, off_w1, "w_kvb", 16, KV_LORA).reshape(H_A, 2, 64, KV_LORA)
    w_kvb_t = jnp.concatenate([_pad_heads(wkvb[:, 0].reshape(-1, KV_LORA), H_A, 0),
                               _pad_heads(wkvb[:, 1].reshape(-1, KV_LORA), H_A, 0)], axis=0)

    tabs = _rope_tables(s_len)
    g_qn_p = jnp.pad(g_qn, ((0, 0), (0, HP - HD_B)))
    g_kn_p = jnp.pad(g_kn, ((0, 0), (0, HP - HD_B)))
    pf = p.reshape(t, PLE_DIM)
    tgt = loss_target.reshape(t, D_MODEL)

    zg, zs, full2 = in_proj(h, w_in_t, tm, comm=gather3_first_comm(pack2))
    qa, ka, va, qb, kb, vb, cq, ckv, full2 = attn_prep(zs, tabs, g_qa, g_kva, g_qn_p, g_kn_p, w_qb_t, w_kvb_t, tm, s_len,
                                                       comm=gather3_second_comm(full2))
    oa, lse_a, full3, full2 = attn_fwd(qa, ka, va, n_b, s_len, tq_f, "attn_a_fwd",
                                       comm=_join_comms(gather_first_comm(pack3), gather3_third_comm(full2)))
    ob, lse_b, full3 = attn_fwd(qb, kb, vb, n_b, s_len, tq_f, "attn_b_fwd", comm=gather_pass_comm(full3))
    w_oa_t = _pad_heads(gathered(full2, off_w2, "w_oa", 64, H_A * V_DIM_A), H_A, 1)
    w_ob_t = _pad_heads(gathered(full2, off_w2, "w_ob", 64, H_B * HD_B), H_B, 1)
    w_ple_t = gathered(full2, off_w2, "w_ple", 32, PLE_DIM)
    x1, merged, yab = merge_fwd(oa, ob, zg, xf, w_oa_t, w_ob_t, full2, off_w2, tm)
    x2, u = mlp_fwd(x1, g_mlp, full3, off_w3, tm)
    dx2, dt, h3, dpe, st_ple, dx2b = ple_loss_bwd(x2, pf, tgt, g_ple, g_final.reshape(1, D_MODEL), full2, off_w2, w_ple_t, tm)
    dx1, da, h2, st_mlp, dx1b = mlp_bwd(dx2, x1, u, g_mlp, full3, off_w3, tm)

    core = lax.axis_index("c").astype(jnp.int32).reshape(1)
    chip = (2 * lax.axis_index("x") + lax.axis_index("y")).astype(jnp.int32).reshape(1)

    def packed(gblocks, names):
        return _pack_rows({n: gblocks[n].reshape(N_DEV, -1, D_MODEL).astype(BF16) for n in names}, names)

    off_g1, rows_g1 = _pack_offsets(PACK_G1)
    gpack1 = matmul_tn_packed(da, h2, "gw_up", 512, off_g1["w_up"], rows_g1)
    gpack1 = matmul_tn_packed(u, dx2b, "gw_down", 512, off_g1["w_down"], rows_g1, buf=gpack1, square_a=True)
    gpack1 = matmul_tn_packed(h3, dt, "gw_pg", 128, off_g1["w_ple_gate"], rows_g1, buf=gpack1)
    gple = matmul_tn(dpe, pf, "gw_ple").reshape(N_DEV, -1, D_MODEL).astype(BF16)
    gpack1 = lax.dynamic_update_slice(gpack1, gple, (0, off_g1["w_ple"], 0))
    doa, dob, dz, dya, dyb, got1 = merge_bwd(dx1b, yab, zg, w_oa_t, w_ob_t, full2, off_w2, tm,
                                               comm=scatter_sibling_comm(gpack1))
    part1 = add_pairs(gpack1, got1, core)

    off_g2, rows_g2 = _pack_offsets(PACK_G2)
    g2 = dict(w_oa=_unpad_heads(matmul_tn(dya, oa, "gw_oa"), H_A, V_DIM_A, 1),
              w_ob=_unpad_heads(matmul_tn(dyb, ob, "gw_ob"), H_B, HD_B, 1))
    gpack2 = matmul_tn_packed(merged, dx1b, "gw_o", 128, off_g2["w_o"], rows_g2)
    gpack2 = lax.dynamic_update_slice(gpack2, packed(g2, ("w_oa", "w_ob")), (0, off_g2["w_oa"], 0))
    dqa, dka, dva, land1, got2 = attn_bwd(qa, ka, va, doa, oa, lse_a, n_b, s_len, tq_b, "attn_a_bwd",
                                          comm=_join_comms(scatter_chips_comm(part1), scatter_sibling_comm(gpack2)))
    gshard1 = sum_chips(part1, land1, chip, off_g1["w_ple"], PACK_ROWS["w_ple"])
    part2 = add_pairs(gpack2, got2, core)
    dqb, dkb, dvb, land2 = attn_bwd(qb, kb, vb, dob, ob, lse_b, n_b, s_len, tq_b, "attn_b_bwd", comm=scatter_chips_comm(part2))
    gshard2 = sum_chips(part2, land2, chip, off_g2["w_oa"], 128)
    dz, dqap, dkva, st_prep = prep_bwd(dqa, dka, dva, dqb, dkb, dvb, zs, dz, tabs, g_qa, g_kva, g_qn_p, g_kn_p,
                                       w_qb_t, w_kvb_t, tm, s_len)

    gkv = matmul_tn(dkva, ckv, "gw_kvb")
    g3 = dict(
        w_in=_collapse_w_in(matmul_tn(dz, h, "gw_in")),
        w_qb=_unpad_heads(matmul_tn(dqap, cq, "gw_qb"), H_A, QK_NOPE + QK_ROPE, 0),
        w_kvb=jnp.stack([_unpad_heads(gkv[:H_A * HP], H_A, 64, 0).reshape(H_A, 64, KV_LORA),
                         _unpad_heads(gkv[H_A * HP:], H_A, 64, 0).reshape(H_A, 64, KV_LORA)], axis=1))
    gpack3 = packed(g3, PACK_G3)
    part3 = add_pairs(gpack3, exchange_sibling(gpack3), core)
    grad_x, st_mix, land3 = in_bwd(dz, xf, dx1, g_mix, w_in_t, tm, comm=scatter_chips_comm(part3))
    gshard3 = sum_chips(part3, land3, chip)
    off_g3, _ = _pack_offsets(PACK_G3)
    shards = {"w_ple": (gshard1, 0), "w_oa": (gshard2, 0), "w_ob": (gshard2, PACK_ROWS["w_oa"])}
    shards.update({n: (gshard3, off_g3[n]) for n in PACK_G3})
    from_parts = {"w_up": (part1, land1, off_g1["w_up"]), "w_down": (part1, land1, off_g1["w_down"]),
                  "w_ple_gate": (part1, land1, off_g1["w_ple_gate"]), "w_o": (part2, land2, off_g2["w_o"])}

    stats = allreduce_stats(st_mix, st_prep, st_mlp, st_ple)
    loss = jnp.sum(stats[ST_LOSS])

    out_g, out_d, out_m, out_v = {}, {}, {}, {}
    for name, (w, m, v) in mats.items():
        if name in from_parts:
            part_n, land_n, off = from_parts[name]
            out_g[name], out_d[name], out_m[name], out_v[name] = adamw_from_parts(
                w, part_n, land_n, chip, off, m, v, "adamw_" + name, g_transposed=name in col_sharded)
            continue
        gshard, off = shards[name]
        r, c = w.shape[1:]
        if name in col_sharded:
            g2 = gshard[off:off + (r * c) // D_MODEL].reshape(c, r)
            if r % 128 == 0 and c % 128 == 0:
                res = adamw(w, g2, m, v, "adamw_" + name, g_transposed=True)
            else:
                res = adamw(w[0].T[None], g2, m[0].T[None], v[0].T[None], "adamw_" + name)
                res = tuple(a[0].T[None] for a in res)
        else:
            res = adamw(w, gshard[off:off + r], m, v, "adamw_" + name)
        out_g[name], out_d[name], out_m[name], out_v[name] = res

    gains = (("g_mix", g_mix, m_g_mix, v_g_mix, ST_G_MIX), ("g_qa", g_qa, m_g_qa, v_g_qa, ST_G_QA),
             ("g_kva", g_kva, m_g_kva, v_g_kva, ST_G_KVA), ("g_qn", g_qn, m_g_qn, v_g_qn, ST_G_QN),
             ("g_kn", g_kn, m_g_kn, v_g_kn, ST_G_KN), ("g_mlp", g_mlp, m_g_mlp, v_g_mlp, ST_G_MLP),
             ("g_ple", g_ple, m_g_ple, v_g_ple, ST_G_PLE), ("g_final", g_final, m_g_final, v_g_final, ST_G_FINAL))
    res = adamw_gains(stats, [(r_, w.reshape(1, -1), m.reshape(1, -1), v.reshape(1, -1)) for _, w, m, v, r_ in gains])
    for (name, w, _, _, _), (gg, gd, gm, gv) in zip(gains, res):
        out_g[name], out_d[name], out_m[name], out_v[name] = (a.reshape(w.shape) for a in (gg, gd, gm, gv))

    order = ("g_mix", "w_in", "g_qa", "w_qb", "g_kva", "w_kvb", "g_qn", "g_kn", "w_oa", "w_ob", "w_o", "g_mlp",
             "w_up", "w_down", "g_ple", "w_ple_gate", "w_ple", "g_final")
    return (loss, grad_x.reshape(x.shape), *[out_g[n] for n in order], *[out_d[n] for n in order],
            *[out_m[n] for n in order], *[out_v[n] for n in order])
```

```python
import numpy as np
import jax
import jax.numpy as jnp
from jax import lax
from jax.experimental import pallas as pl
from jax.experimental.pallas import tpu as pltpu

F32 = jnp.float32
BF16 = jnp.bfloat16

D_MODEL = 1024
EPS = 1e-6
ROPE_THETA = 10000.0
GRID_W = 64
H_A = 8
QK_NOPE = 64
QK_ROPE = 32
V_DIM_A = 64
Q_LORA = 256
KV_LORA = 128
H_B = 8
KV_B = 2
HD_B = 64
D_FF = 4 * D_MODEL
PLE_DIM = 256
HP = 128
ZP = 4096
N_DEV = 8
N_CHIP = 4

ADAM_LR = 0.001
ADAM_B1 = 0.9
ADAM_B2 = 0.999
ADAM_EPS = 1e-08
ADAM_WD = 0.01
ADAM_STEP = 10

VMEM_LIMIT = 52 * 1024 * 1024

PACK_ROWS = dict(w_in=416, w_qb=32, w_kvb=16, w_oa=64, w_ob=64, w_o=128, w_up=512, w_down=512, w_ple_gate=128, w_ple=32)
PACK_W1 = ("w_in", "w_qb", "w_kvb")
PACK_W2 = ("w_o", "w_ple_gate", "w_oa", "w_ob", "w_ple")
PACK_W3 = ("w_up", "w_down")
PACK_G1 = ("w_up", "w_down", "w_ple_gate", "w_ple")
PACK_G2 = ("w_o", "w_oa", "w_ob")
PACK_G3 = ("w_in", "w_qb", "w_kvb")


def _pack_offsets(names):
    off, o = {}, 0
    for n in names:
        off[n] = o
        o += PACK_ROWS[n]
    return off, o

ST_G_MIX, ST_G_QA, ST_G_KVA, ST_G_QN, ST_G_KN, ST_G_MLP, ST_G_PLE, ST_G_FINAL, ST_LOSS = range(9)
ST_ROWS = 16


def _dot_nn(a, b):
    return lax.dot_general(a, b, (((1,), (0,)), ((), ())), preferred_element_type=F32)


def _dot_nt(a, b):
    return lax.dot_general(a, b, (((1,), (1,)), ((), ())), preferred_element_type=F32)


def _dot_tn(a, b):
    return lax.dot_general(a, b, (((0,), (0,)), ((), ())), preferred_element_type=F32)


def _rstd(x, n):
    return lax.rsqrt(jnp.sum(x * x, axis=-1, keepdims=True) * (1.0 / n) + EPS)


def _rms_bwd(dy, xh, r, g, n):
    dxh = dy * g
    return r * (dxh - xh * (jnp.sum(dxh * xh, axis=-1, keepdims=True) * (1.0 / n)))


def _rope_fwd(x, c, s1, s2):
    return x * c + pltpu.roll(x, HP - 16, 1) * s1 + pltpu.roll(x, 16, 1) * s2


def _rope_bwd(d, c, s1, s2):
    return d * c + pltpu.roll(d * s1, 16, 1) + pltpu.roll(d * s2, HP - 16, 1)


def _colsum(v):
    return jnp.sum(v, axis=0, keepdims=True)


def _params(sem=None, vmem=VMEM_LIMIT):
    return pltpu.CompilerParams(dimension_semantics=sem, vmem_limit_bytes=vmem)


def _resident(shape):
    nd = len(shape)
    return pl.BlockSpec(shape, lambda *_: (0,) * nd, pipeline_mode=pl.Buffered(1))


def _rows(tm, width, col=0):
    return pl.BlockSpec((tm, width), lambda i: (i, col))


def _packed_weight(rows, off):
    return pl.BlockSpec((N_DEV, rows, D_MODEL), lambda *_: (0, off // rows, 0), pipeline_mode=pl.Buffered(1))


def _wrows(ref, start, size):
    rows = ref.shape[1]
    return ref[start // rows:(start + size) // rows].reshape(size, D_MODEL)


def _mesh_pos():
    return lax.axis_index("x"), lax.axis_index("y"), lax.axis_index("c")


def _flip(v, bit):
    return (1 - v) if bit else v


_ANY = pl.BlockSpec(memory_space=pl.ANY)
_MESH = pl.DeviceIdType.MESH


def _remote(src, dst, send_sems, recv_sems, k, to):
    return pltpu.make_async_remote_copy(src_ref=src, dst_ref=dst, send_sem=send_sems.at[k], recv_sem=recv_sems.at[k],
                                        device_id=to, device_id_type=_MESH)


def _sibling_copies(g_ref, got_ref, send_sems, recv_sems):
    x, y, c = _mesh_pos()
    return [_remote(g_ref.at[2 * j + (1 - c)], got_ref.at[j], send_sems, recv_sems, j, (x, y, 1 - c)) for j in range(N_CHIP)]


def _chip_copies(p_ref, land_ref, send_sems, recv_sems):
    x, y, c = _mesh_pos()
    copies = []
    for k in (1, 2, 3):
        tx, ty = _flip(x, k & 2), _flip(y, k & 1)
        copies.append(_remote(p_ref.at[2 * tx + ty], land_ref.at[k - 1], send_sems, recv_sems, k - 1, (tx, ty, c)))
    return copies


class _Comm:
    def __init__(self, ins, out_shapes, sems, make, aliases=None, make_tail=None):
        self.ins, self.out_shapes, self.sems, self.make, self.aliases = list(ins), list(out_shapes), list(sems), make, aliases or {}
        self.make_tail = make_tail


def _join_comms(a, b):
    assert a.make_tail is None and b.make_tail is None
    n_i, n_o, n_s = len(a.ins), len(a.out_shapes), len(a.sems)

    def make(cin, cout, sems):
        return a.make(cin[:n_i], cout[:n_o], sems[:n_s]) + b.make(cin[n_i:], cout[n_o:], sems[n_s:])

    aliases = dict(a.aliases)
    aliases.update({n_i + j: n_o + k for j, k in b.aliases.items()})
    return _Comm(a.ins + b.ins, a.out_shapes + b.out_shapes, a.sems + b.sems, make, aliases)


def _comm_parts(comm, n_in, n_out):
    if comm is None:
        return [], [], [], [], {}
    alias = {n_in + j: n_out + k for j, k in comm.aliases.items()}
    return comm.ins, [_ANY] * len(comm.ins), comm.out_shapes, comm.sems, alias


def _split_refs(refs, n_in, n_out, n_scratch, comm):
    n_ci = len(comm.ins) if comm else 0
    n_co = len(comm.out_shapes) if comm else 0
    cuts, i = [], 0
    for n in (n_in, n_ci, n_out, n_co, n_scratch):
        cuts.append(refs[i:i + n])
        i += n
    return (*cuts, refs[i:])


def _grid_edge(grid, last):
    cond = None
    for d, n in enumerate(grid):
        here = pl.program_id(d) == (n - 1 if last else 0)
        cond = here if cond is None else cond & here
    return cond


def _comm_start(comm, cin, cout, csem, grid):
    if comm is not None:
        @pl.when(_grid_edge(grid, False))
        def _():
            for cp in comm.make(cin, cout, csem):
                cp.start()


def _comm_finish(comm, cin, cout, csem, grid):
    if comm is not None:
        @pl.when(_grid_edge(grid, True))
        def _():
            for cp in comm.make(cin, cout, csem):
                cp.wait()
            if comm.make_tail is not None:
                tail = comm.make_tail(cin, cout, csem)
                for cp in tail:
                    cp.start()
                for cp in tail:
                    cp.wait()


def gather_first_comm(shard):
    r, w = shard.shape

    def make(cin, cout, sems):
        (x_ref,), (out_ref,), (send_sems, recv_sems, local_sem) = cin, cout, sems
        x, y, c = _mesh_pos()
        mine = out_ref.at[4 * x + 2 * y + c]
        targets = [(x, y, 1 - c), (1 - x, y, c), (x, 1 - y, c), (1 - x, 1 - y, c)]
        return [_remote(x_ref, mine, send_sems, recv_sems, k, to) for k, to in enumerate(targets)] + [
            pltpu.make_async_copy(x_ref, mine, local_sem)]

    return _Comm([shard], [jax.ShapeDtypeStruct((N_DEV, r, w), shard.dtype)],
                 [pltpu.SemaphoreType.DMA((4,)), pltpu.SemaphoreType.DMA((4,)), pltpu.SemaphoreType.DMA], make)


def gather_pass_comm(full):
    def make(cin, cout, sems):
        (in_ref,), (out_ref,), (send_sems, recv_sems) = cin, cout, sems
        x, y, c = _mesh_pos()
        copies = []
        for k, (px, py) in enumerate([(1 - x, y), (x, 1 - y), (1 - x, 1 - y)]):
            idx = 4 * px + 2 * py + c
            copies.append(_remote(in_ref.at[idx], out_ref.at[idx], send_sems, recv_sems, k, (x, y, 1 - c)))
        return copies

    return _Comm([full], [jax.ShapeDtypeStruct(full.shape, full.dtype)],
                 [pltpu.SemaphoreType.DMA((3,)), pltpu.SemaphoreType.DMA((3,))], make, aliases={0: 0})


def gather3_first_comm(shard):
    r, w = shard.shape

    def make(cin, cout, sems):
        (x_ref,), (out_ref,), (send_sems, recv_sems, local_sem) = cin, cout, sems
        x, y, c = _mesh_pos()
        mine = out_ref.at[4 * x + 2 * y + c]
        targets = [(x, y, 1 - c), (1 - x, y, c), (x, 1 - y, c)]
        return [_remote(x_ref, mine, send_sems, recv_sems, k, to) for k, to in enumerate(targets)] + [
            pltpu.make_async_copy(x_ref, mine, local_sem)]

    return _Comm([shard], [jax.ShapeDtypeStruct((N_DEV, r, w), shard.dtype)],
                 [pltpu.SemaphoreType.DMA((3,)), pltpu.SemaphoreType.DMA((3,)), pltpu.SemaphoreType.DMA], make)


def gather3_second_comm(full, then_third=False):
    def make(cin, cout, sems):
        (in_ref,), (out_ref,), (send_sems, recv_sems) = cin, cout, sems
        x, y, c = _mesh_pos()
        copies = []
        for k, (px, py) in enumerate([(1 - x, y), (x, 1 - y)]):
            idx = 4 * px + 2 * py + c
            copies.append(_remote(in_ref.at[idx], out_ref.at[idx], send_sems, recv_sems, k, (x, y, 1 - c)))
        fx, fy = x * c + (1 - x) * (1 - c), y * (1 - c) + (1 - y) * c
        tx, ty = x * (1 - c) + (1 - x) * c, (1 - y) * (1 - c) + y * c
        idx = 4 * fx + 2 * fy + c
        copies.append(_remote(in_ref.at[idx], out_ref.at[idx], send_sems, recv_sems, 2, (tx, ty, c)))
        return copies

    def third(cin, cout, sems):
        (out_ref,), (send_sems, recv_sems) = cout, sems
        x, y, c = _mesh_pos()
        idx = 4 * (1 - x) + 2 * (1 - y) + c
        return [_remote(out_ref.at[idx], out_ref.at[idx], send_sems, recv_sems, 3, (x, y, 1 - c))]

    n_sem = 4 if then_third else 3
    return _Comm([full], [jax.ShapeDtypeStruct(full.shape, full.dtype)],
                 [pltpu.SemaphoreType.DMA((n_sem,)), pltpu.SemaphoreType.DMA((n_sem,))], make, aliases={0: 0},
                 make_tail=third if then_third else None)


def gather3_third_comm(full):
    def make(cin, cout, sems):
        (in_ref,), (out_ref,), (send_sems, recv_sems) = cin, cout, sems
        x, y, c = _mesh_pos()
        idx = 4 * (1 - x) + 2 * (1 - y) + c
        return [_remote(in_ref.at[idx], out_ref.at[idx], send_sems, recv_sems, 0, (x, y, 1 - c))]

    return _Comm([full], [jax.ShapeDtypeStruct(full.shape, full.dtype)],
                 [pltpu.SemaphoreType.DMA((1,)), pltpu.SemaphoreType.DMA((1,))], make, aliases={0: 0})


def scatter_sibling_comm(g):
    _, r, w = g.shape
    return _Comm([g], [jax.ShapeDtypeStruct((N_CHIP, r, w), g.dtype)],
                 [pltpu.SemaphoreType.DMA((N_CHIP,)), pltpu.SemaphoreType.DMA((N_CHIP,))],
                 lambda cin, cout, sems: _sibling_copies(cin[0], cout[0], sems[0], sems[1]))


def scatter_chips_comm(part):
    _, r, w = part.shape
    return _Comm([part], [jax.ShapeDtypeStruct((N_CHIP - 1, r, w), part.dtype)],
                 [pltpu.SemaphoreType.DMA((3,)), pltpu.SemaphoreType.DMA((3,))],
                 lambda cin, cout, sems: _chip_copies(cin[0], cout[0], sems[0], sems[1]))


def exchange_sibling(g):
    _, r, w = g.shape

    def body(g_ref, got_ref, send_sems, recv_sems):
        copies = _sibling_copies(g_ref, got_ref, send_sems, recv_sems)
        for cp in copies:
            cp.start()
        for cp in copies:
            cp.wait()

    return pl.pallas_call(
        body, name="exchange_sibling", out_shape=jax.ShapeDtypeStruct((N_CHIP, r, w), g.dtype),
        in_specs=[_ANY], out_specs=_ANY,
        scratch_shapes=[pltpu.SemaphoreType.DMA((N_CHIP,)), pltpu.SemaphoreType.DMA((N_CHIP,))],
    )(g)


def exchange_chips(part):
    _, r, w = part.shape

    def body(p_ref, land_ref, send_sems, recv_sems):
        copies = _chip_copies(p_ref, land_ref, send_sems, recv_sems)
        for cp in copies:
            cp.start()
        for cp in copies:
            cp.wait()

    return pl.pallas_call(
        body, name="exchange_chips", out_shape=jax.ShapeDtypeStruct((N_CHIP - 1, r, w), part.dtype),
        in_specs=[_ANY], out_specs=_ANY,
        scratch_shapes=[pltpu.SemaphoreType.DMA((3,)), pltpu.SemaphoreType.DMA((3,))],
    )(part)


def allreduce_stats(st_mix, st_prep, st_mlp, st_ple):
    def body(mix_ref, prep_ref, mlp_ref, ple_ref, out_ref, mine, gath, send_sems, recv_sems):
        x, y, c = _mesh_pos()
        me = 4 * x + 2 * y + c
        mine[...] = jnp.zeros_like(mine)
        mine[ST_G_MIX:ST_G_MIX + 1, :] = mix_ref[...]
        mine[ST_G_QA:ST_G_KN + 1, 0:256] = prep_ref[...]
        mine[ST_G_MLP:ST_G_MLP + 1, :] = mlp_ref[...]
        mine[ST_G_PLE:ST_LOSS + 1, :] = ple_ref[...]
        gath[me] = mine[...]
        copies = []
        for k in range(1, N_DEV):
            peer = (_flip(x, k & 4), _flip(y, k & 2), _flip(c, k & 1))
            copies.append(_remote(mine, gath.at[me], send_sems, recv_sems, k - 1, peer))
        for cp in copies:
            cp.start()
        for cp in copies:
            cp.wait()
        acc = gath[0]
        for d in range(1, N_DEV):
            acc = acc + gath[d]
        out_ref[...] = acc

    vm = pl.BlockSpec(memory_space=pltpu.VMEM)
    return pl.pallas_call(
        body, name="allreduce_stats", out_shape=jax.ShapeDtypeStruct((ST_ROWS, D_MODEL), F32),
        in_specs=[vm] * 4, out_specs=vm,
        scratch_shapes=[pltpu.VMEM((ST_ROWS, D_MODEL), F32), pltpu.VMEM((N_DEV, ST_ROWS, D_MODEL), F32),
                        pltpu.SemaphoreType.DMA((N_DEV - 1,)), pltpu.SemaphoreType.DMA((N_DEV - 1,))],
    )(st_mix, st_prep, st_mlp, st_ple)


def adamw_gains(stats, gains):
    c1 = 1.0 - ADAM_B1 ** ADAM_STEP
    c2 = 1.0 - ADAM_B2 ** ADAM_STEP
    n = len(gains)

    def body(st_ref, *refs):
        ins, outs = refs[:3 * n], refs[3 * n:]
        for i, (row, w, _, _) in enumerate(gains):
            width = w.shape[1]
            gv = st_ref[row:row + 1, 0:width]
            mn = ADAM_B1 * ins[3 * i + 1][...] + (1.0 - ADAM_B1) * gv
            vn = ADAM_B2 * ins[3 * i + 2][...] + (1.0 - ADAM_B2) * (gv * gv)
            outs[4 * i][...] = gv
            outs[4 * i + 1][...] = -ADAM_LR * ((mn / c1) / (jnp.sqrt(vn / c2) + ADAM_EPS) + ADAM_WD * ins[3 * i][...])
            outs[4 * i + 2][...] = mn
            outs[4 * i + 3][...] = vn

    vm = pl.BlockSpec(memory_space=pltpu.VMEM)
    flat = [a for (_, w, m, v) in gains for a in (w, m, v)]
    out_shape = tuple(jax.ShapeDtypeStruct(w.shape, F32) for (_, w, _, _) in gains for _ in range(4))
    res = pl.pallas_call(body, name="adamw_gains", out_shape=out_shape, in_specs=[vm] * (1 + 3 * n),
                         out_specs=tuple([vm] * (4 * n)))(stats, *flat)
    return [res[4 * i:4 * i + 4] for i in range(n)]


def _row_tile(r, cap=640):
    return max(d for d in range(16, min(r, cap) + 1, 16) if r % d == 0)


def add_pairs(g, got, core):
    n, r, w = got.shape
    tr = _row_tile(r)

    def body(c_ref, a_ref, b_ref, o_ref):
        o_ref[...] = (a_ref[...].astype(F32) + b_ref[...].astype(F32)).astype(o_ref.dtype)

    spec = pl.BlockSpec((1, tr, w), lambda i, j, c: (i, j, 0))
    return pl.pallas_call(
        body, name="add_pairs", out_shape=jax.ShapeDtypeStruct(got.shape, got.dtype),
        grid_spec=pltpu.PrefetchScalarGridSpec(
            num_scalar_prefetch=1, grid=(n, r // tr),
            in_specs=[pl.BlockSpec((1, tr, w), lambda i, j, c: (2 * i + c[0], j, 0)), spec], out_specs=spec),
        compiler_params=_params(("parallel", "parallel")),
    )(core, g, got)


def sum_chips(part, land, chip, row0=0, nrows=None):
    _, r, w = part.shape
    nrows = r - row0 if nrows is None else nrows
    tr = _row_tile(nrows)
    b0 = row0 // tr

    def body(c_ref, p_ref, l_ref, o_ref):
        acc = p_ref[0].astype(F32)
        for s in range(N_CHIP - 1):
            acc = acc + l_ref[s].astype(F32)
        o_ref[...] = acc

    return pl.pallas_call(
        body, name="sum_chips", out_shape=jax.ShapeDtypeStruct((nrows, w), F32),
        grid_spec=pltpu.PrefetchScalarGridSpec(
            num_scalar_prefetch=1, grid=(nrows // tr,),
            in_specs=[pl.BlockSpec((1, tr, w), lambda i, c: (c[0], b0 + i, 0)),
                      pl.BlockSpec((N_CHIP - 1, tr, w), lambda i, c: (0, b0 + i, 0))],
            out_specs=pl.BlockSpec((tr, w), lambda i, c: (i, 0))),
        compiler_params=_params(("parallel",)),
    )(chip, part, land)


def adamw_from_parts(w, part, land, chip, row_off, m, v, name, g_transposed=False):
    _, r, c = w.shape
    rows, width = (c, r) if g_transposed else (r, c)
    blk = row_off // rows
    c1 = 1.0 - ADAM_B1 ** ADAM_STEP
    c2 = 1.0 - ADAM_B2 ** ADAM_STEP

    def body(c_ref, w_ref, p_ref, l_ref, m_ref, v_ref, g_o, d_o, m_o, v_o):
        gv = p_ref[0].astype(F32)
        for s in range(N_CHIP - 1):
            gv = gv + l_ref[s].astype(F32)
        if g_transposed:
            gv = gv.T
        mn = ADAM_B1 * m_ref[0] + (1.0 - ADAM_B1) * gv
        vn = ADAM_B2 * v_ref[0] + (1.0 - ADAM_B2) * (gv * gv)
        g_o[0] = gv
        m_o[0] = mn
        v_o[0] = vn
        d_o[0] = -ADAM_LR * ((mn / c1) / (jnp.sqrt(vn / c2) + ADAM_EPS) + ADAM_WD * w_ref[0])

    wspec = pl.BlockSpec((1, r, c), lambda i, ch: (0, 0, 0))
    shp = jax.ShapeDtypeStruct((1, r, c), F32)
    return pl.pallas_call(
        body, name=name, out_shape=(shp,) * 4,
        grid_spec=pltpu.PrefetchScalarGridSpec(
            num_scalar_prefetch=1, grid=(1,),
            in_specs=[wspec, pl.BlockSpec((1, rows, width), lambda i, ch: (ch[0], blk, 0)),
                      pl.BlockSpec((N_CHIP - 1, rows, width), lambda i, ch: (0, blk, 0)), wspec, wspec],
            out_specs=(wspec,) * 4),
        compiler_params=_params(("arbitrary",)),
    )(chip, w, part, land, m, v)


def norm_x(x, g_mix, tm, comm=None):
    t = x.shape[0]
    grid = (t // tm,)
    c_ins, c_in_specs, c_outs, c_sems, alias = _comm_parts(comm, 2, 1)

    def body(*refs):
        (x_ref, g_ref), cin, (h_ref,), cout, _, csem = _split_refs(refs, 2, 1, 0, comm)
        _comm_start(comm, cin, cout, csem, grid)
        xv = x_ref[...]
        h_ref[...] = (xv * _rstd(xv, D_MODEL) * g_ref[...]).astype(BF16)
        _comm_finish(comm, cin, cout, csem, grid)

    return pl.pallas_call(
        body, name="norm_x", grid=grid, out_shape=(jax.ShapeDtypeStruct((t, D_MODEL), BF16), *c_outs),
        in_specs=[_rows(tm, D_MODEL), _resident((1, D_MODEL)), *c_in_specs],
        out_specs=(_rows(tm, D_MODEL), *([_ANY] * len(c_outs))),
        scratch_shapes=c_sems, input_output_aliases=alias, compiler_params=_params(("arbitrary",)),
    )(x, g_mix, *c_ins)


def pack_late_weights(w_up, w_down, w_o, w_pg, small, comm=None):
    rows2 = sum(PACK_ROWS[n] for n in PACK_W2)
    rows3 = sum(PACK_ROWS[n] for n in PACK_W3)
    c_ins, c_in_specs, c_outs, c_sems, alias = _comm_parts(comm, 5, 2)
    grid = (1,)

    def body(*refs):
        (up_ref, dn_ref, o_ref, pg_ref, sm_ref), cin, (p2_ref, p3_ref), cout, _, csem = _split_refs(refs, 5, 2, 0, comm)
        _comm_start(comm, cin, cout, csem, grid)
        p2_ref[0:128, :] = o_ref[0].astype(BF16)
        p2_ref[128:256, :] = pg_ref[0].astype(BF16)
        p2_ref[256:rows2, :] = sm_ref[...]
        p3_ref[0:512, :] = up_ref[0].T.astype(BF16)
        p3_ref[512:1024, :] = dn_ref[0].astype(BF16)
        _comm_finish(comm, cin, cout, csem, grid)

    def whole(a):
        nd = a.ndim
        return pl.BlockSpec(a.shape, lambda i: (0,) * nd)

    args = (w_up, w_down, w_o, w_pg, small)
    return pl.pallas_call(
        body, name="pack_late_weights", grid=grid,
        out_shape=(jax.ShapeDtypeStruct((rows2, D_MODEL), BF16), jax.ShapeDtypeStruct((rows3, D_MODEL), BF16), *c_outs),
        in_specs=[*[whole(a) for a in args], *c_in_specs],
        out_specs=(pl.BlockSpec((rows2, D_MODEL), lambda i: (0, 0)), pl.BlockSpec((rows3, D_MODEL), lambda i: (0, 0)),
                   *([_ANY] * len(c_outs))),
        scratch_shapes=c_sems, input_output_aliases=alias, compiler_params=_params(("arbitrary",)),
    )(*args, *c_ins)


def in_proj(h, w_in_t, tm, comm=None):
    t = h.shape[0]
    nc = 512
    half = ZP // 2
    grid = (t // tm,)
    c_ins, c_in_specs, c_outs, c_sems, alias = _comm_parts(comm, 2, 2)

    def body(*refs):
        (h_ref, w_ref), cin, (zg_ref, zs_ref), cout, _, csem = _split_refs(refs, 2, 2, 0, comm)
        _comm_start(comm, cin, cout, csem, grid)
        hv = h_ref[...]
        for cidx in range(half // nc):
            zg_ref[:, cidx * nc:(cidx + 1) * nc] = _dot_nt(hv, w_ref[cidx * nc:(cidx + 1) * nc, :]).astype(BF16)
        for cidx in range(half // nc):
            zs_ref[:, cidx * nc:(cidx + 1) * nc] = _dot_nt(hv, w_ref[half + cidx * nc:half + (cidx + 1) * nc, :]).astype(BF16)
        _comm_finish(comm, cin, cout, csem, grid)

    return pl.pallas_call(
        body, name="in_proj", grid=grid,
        out_shape=(jax.ShapeDtypeStruct((t, half), BF16), jax.ShapeDtypeStruct((t, half), BF16), *c_outs),
        in_specs=[_rows(tm, D_MODEL), _resident((ZP, D_MODEL)), *c_in_specs],
        out_specs=(_rows(tm, half), _rows(tm, half), *([_ANY] * len(c_outs))),
        scratch_shapes=c_sems, input_output_aliases=alias, compiler_params=_params(("arbitrary",)),
    )(h, w_in_t, *c_ins)


def attn_prep(zp, tabs, g_qa, g_kva, g_qn, g_kn, w_qb_t, w_kvb_t, tm, s_len, comm=None):
    t = zp.shape[0]
    nsb = s_len // tm
    scale_a = (QK_NOPE + QK_ROPE) ** -0.5
    scale_b = HD_B ** -0.5

    grid = (t // tm,)
    c_ins, c_in_specs, c_outs, c_sems, alias = _comm_parts(comm, 13, 8)

    def body(*refs):
        ((qb_ref, qlat_ref, kb_ref, vb_ref, ckv_ref, kpe_ref, tab_ref, gqa_ref, gkva_ref, gqn_ref, gkn_ref, wqb_ref,
          wkvb_ref), cin, (qa_o, ka_o, va_o, qb_o, kb_o, vb_o, cq_o, ckvn_o), cout, _, csem) = _split_refs(refs, 13, 8, 0, comm)
        _comm_start(comm, cin, cout, csem, grid)
        ca, s1a, s2a = tab_ref[0], tab_ref[1], tab_ref[2]
        ck = tab_ref[3]
        cb, s1b, s2b = tab_ref[4], tab_ref[5], tab_ref[6]
        ql = qlat_ref[...].astype(F32)
        cq = (ql * _rstd(ql, Q_LORA) * gqa_ref[...]).astype(BF16)
        cq_o[...] = cq
        slabs = [slice(h * HP, (h + 1) * HP) for h in range(H_A)]
        for sl in slabs:
            qa_o[:, sl] = (_rope_fwd(_dot_nt(cq, wqb_ref[sl, :]), ca, s1a, s2a) * scale_a).astype(BF16)
        cr = ckv_ref[...].astype(F32)
        ckv = (cr * _rstd(cr, KV_LORA) * gkva_ref[...]).astype(BF16)
        ckvn_o[...] = ckv
        kpe = _rope_fwd(kpe_ref[...].astype(F32), ck, s1a, s2a)
        for h, sl in enumerate(slabs):
            ka_o[:, sl] = (_dot_nt(ckv, wkvb_ref[sl, :]) + kpe).astype(BF16)
            va_o[:, sl] = _dot_nt(ckv, wkvb_ref[(H_A + h) * HP:(H_A + h + 1) * HP, :]).astype(BF16)
        gqn, gkn = gqn_ref[...], gkn_ref[...]

        def norm_rope(ref, sl, g, scale):
            xs = ref[:, sl].astype(F32)
            y = _rope_fwd(xs * _rstd(xs, HD_B) * g, cb, s1b, s2b)
            return (y if scale is None else y * scale).astype(BF16)

        qb_o[...] = jnp.concatenate([norm_rope(qb_ref, sl, gqn, scale_b) for sl in slabs], axis=1)
        kb_o[...] = jnp.concatenate([norm_rope(kb_ref, sl, gkn, None) for sl in slabs[:KV_B]], axis=1)
        vb_o[...] = vb_ref[...].astype(BF16)
        _comm_finish(comm, cin, cout, csem, grid)

    def o(width):
        return jax.ShapeDtypeStruct((t, width), BF16)

    return pl.pallas_call(
        body, name="attn_prep", grid=grid,
        out_shape=(o(H_A * HP), o(H_A * HP), o(H_A * HP), o(H_B * HP), o(KV_B * HP), o(KV_B * HP), o(Q_LORA), o(KV_LORA),
                   *c_outs),
        in_specs=[_rows(tm, 1024, 0), _rows(tm, 256, 4), _rows(tm, 256, 5), _rows(tm, 256, 6),
                  _rows(tm, 128, 14), _rows(tm, 128, 15),
                  pl.BlockSpec((7, tm, HP), lambda i: (0, i % nsb, 0)),
                  _resident((1, Q_LORA)), _resident((1, KV_LORA)), _resident((1, HP)), _resident((1, HP)),
                  _resident((H_A * HP, Q_LORA)), _resident((2 * H_A * HP, KV_LORA)), *c_in_specs],
        out_specs=(_rows(tm, H_A * HP), _rows(tm, H_A * HP), _rows(tm, H_A * HP), _rows(tm, H_B * HP),
                   _rows(tm, KV_B * HP), _rows(tm, KV_B * HP), _rows(tm, Q_LORA), _rows(tm, KV_LORA), *([_ANY] * len(c_outs))),
        scratch_shapes=c_sems, input_output_aliases=alias, compiler_params=_params(("arbitrary",)),
    )(zp, zp, zp, zp, zp, zp, tabs, g_qa, g_kva, g_qn, g_kn, w_qb_t, w_kvb_t, *c_ins)


def attn_fwd(q, k, v, n_b, s_len, tq, name, comm=None):
    t = q.shape[0]
    n_h, n_hk = q.shape[1] // HP, k.shape[1] // HP
    grp = n_h // n_hk
    nq = s_len // tq
    sub = min(tq, 256)
    grid = (n_b, n_h, nq)
    c_ins, c_in_specs, c_outs, c_sems, alias = _comm_parts(comm, 3, 2)

    def body(*refs):
        (q_ref, k_ref, v_ref), cin, (o_ref, lse_ref), cout, _, csem = _split_refs(refs, 3, 2, 0, comm)
        _comm_start(comm, cin, cout, csem, grid)
        kv, vv = k_ref[...], v_ref[...]
        for r in range(tq // sub):
            rows = slice(r * sub, (r + 1) * sub)
            s = _dot_nt(q_ref[rows, :], kv)
            m = jnp.max(s, axis=-1, keepdims=True)
            p = jnp.exp(s - m)
            l = jnp.sum(p, axis=-1, keepdims=True)
            o_ref[rows, :] = (_dot_nn(p.astype(BF16), vv) * (1.0 / l)).astype(o_ref.dtype)
            lse_ref[rows, :] = jnp.broadcast_to(m + jnp.log(l), (sub, HP))
        _comm_finish(comm, cin, cout, csem, grid)

    qspec = pl.BlockSpec((tq, HP), lambda b, h, i: (b * nq + i, h))
    kspec = pl.BlockSpec((s_len, HP), lambda b, h, i: (b, h // grp))
    return pl.pallas_call(
        body, name=name, grid=grid,
        out_shape=(jax.ShapeDtypeStruct((t, n_h * HP), BF16), jax.ShapeDtypeStruct((t, n_h * HP), F32), *c_outs),
        in_specs=[qspec, kspec, kspec, *c_in_specs], out_specs=(qspec, qspec, *([_ANY] * len(c_outs))),
        scratch_shapes=c_sems, input_output_aliases=alias,
        compiler_params=_params(("arbitrary", "arbitrary", "arbitrary")),
    )(q, k, v, *c_ins)


def merge_fwd(oa, ob, zp, x, w_oa_t, w_ob_t, wpack, off, tm):
    t = x.shape[0]

    def body(oa_ref, ob_ref, ga_ref, gb_ref, x_ref, woa_ref, wob_ref, wo_ref, x1_o, mg_o, y_o):
        ya = _dot_nt(oa_ref[...], woa_ref[...])
        yb = _dot_nt(ob_ref[...], wob_ref[...])
        y_o[:, 0:D_MODEL] = ya.astype(BF16)
        y_o[:, D_MODEL:2 * D_MODEL] = yb.astype(BF16)
        merged = (jax.nn.sigmoid(ga_ref[...].astype(F32)) * ya + jax.nn.sigmoid(gb_ref[...].astype(F32)) * yb).astype(BF16)
        mg_o[...] = merged
        x1_o[...] = x_ref[...] + _dot_nn(merged, _wrows(wo_ref, 0, D_MODEL))

    return pl.pallas_call(
        body, name="merge_fwd", grid=(t // tm,),
        out_shape=(jax.ShapeDtypeStruct((t, D_MODEL), F32), jax.ShapeDtypeStruct((t, D_MODEL), BF16),
                   jax.ShapeDtypeStruct((t, 2 * D_MODEL), BF16)),
        in_specs=[_rows(tm, H_A * HP), _rows(tm, H_B * HP), _rows(tm, 1024, 0), _rows(tm, 1024, 1), _rows(tm, D_MODEL),
                  _resident((D_MODEL, H_A * HP)), _resident((D_MODEL, H_B * HP)), _packed_weight(128, off["w_o"])],
        out_specs=(_rows(tm, D_MODEL), _rows(tm, D_MODEL), _rows(tm, 2 * D_MODEL)), compiler_params=_params(("parallel",)),
    )(oa, ob, zp, zp, x, w_oa_t, w_ob_t, wpack)


def mlp_fwd(x1, g_mlp, wpack, off, tm):
    t = x1.shape[0]
    fc = 1024

    def body(x_ref, g_ref, wup_ref, wdn_ref, x2_o, u_o):
        xv = x_ref[...]
        h2 = (xv * _rstd(xv, D_MODEL) * g_ref[...]).astype(BF16)
        acc = xv
        for cidx in range(D_FF // fc):
            sl = slice(cidx * fc, (cidx + 1) * fc)
            u = jnp.maximum(_dot_nt(h2, _wrows(wup_ref, cidx * fc, fc)), 0.0)
            u_o[:, sl] = u.astype(BF16)
            acc = acc + _dot_nn((u * u).astype(BF16), _wrows(wdn_ref, cidx * fc, fc))
        x2_o[...] = acc

    return pl.pallas_call(
        body, name="mlp_fwd", grid=(t // tm,),
        out_shape=(jax.ShapeDtypeStruct((t, D_MODEL), F32), jax.ShapeDtypeStruct((t, D_FF), BF16)),
        in_specs=[_rows(tm, D_MODEL), _resident((1, D_MODEL)), _packed_weight(512, off["w_up"]), _packed_weight(512, off["w_down"])],
        out_specs=(_rows(tm, D_MODEL), _rows(tm, D_FF)), compiler_params=_params(("parallel",)),
    )(x1, g_mlp, wpack, wpack)


def ple_loss_bwd(x2, p, tgt, g_ple, g_final, wpack, off, w_ple_t, tm):
    t = x2.shape[0]
    inv_d = 1.0 / D_MODEL

    def body(x2_ref, p_ref, tg_ref, gp_ref, gf_ref, wpg_ref, wple_ref, dx2_o, dt_o, h3_o, dpe_o, st_o, dx2b_o):
        @pl.when(pl.program_id(0) == 0)
        def _():
            st_o[...] = jnp.zeros_like(st_o)

        x2v = x2_ref[...]
        gp, gf = gp_ref[...], gf_ref[...]
        w_pg = _wrows(wpg_ref, 0, D_MODEL)
        r2 = _rstd(x2v, D_MODEL)
        xh2 = x2v * r2
        h3 = (xh2 * gp).astype(BF16)
        h3_o[...] = h3
        gate = jax.nn.sigmoid(_dot_nn(h3, w_pg))
        pe = _dot_nt(p_ref[...].astype(BF16), wple_ref[...])
        x3 = x2v + gate * pe
        r3 = _rstd(x3, D_MODEL)
        xh3 = x3 * r3
        err = xh3 * gf - tg_ref[...]
        dx3 = _rms_bwd(err, xh3, r3, gf * inv_d, D_MODEL)
        dpe = dx3 * gate
        dpe_o[...] = dpe.astype(BF16)
        dt = (dpe * pe * (1.0 - gate)).astype(BF16)
        dt_o[...] = dt
        dh3 = _dot_nt(dt, w_pg)
        dx2 = dx3 + _rms_bwd(dh3, xh2, r2, gp, D_MODEL)
        dx2_o[...] = dx2
        dx2b_o[...] = dx2.astype(BF16)
        st_o[0:1, :] += _colsum(dh3 * xh2)
        st_o[1:2, :] += _colsum(err * xh3) * inv_d
        st_o[2:3, :] += _colsum(err * err) * (0.5 * inv_d)

    bf = jax.ShapeDtypeStruct((t, D_MODEL), BF16)
    return pl.pallas_call(
        body, name="ple_loss_bwd", grid=(t // tm,),
        out_shape=(jax.ShapeDtypeStruct((t, D_MODEL), F32), bf, bf, bf, jax.ShapeDtypeStruct((3, D_MODEL), F32), bf),
        in_specs=[_rows(tm, D_MODEL), _rows(tm, PLE_DIM), _rows(tm, D_MODEL), _resident((1, D_MODEL)), _resident((1, D_MODEL)),
                  _packed_weight(128, off["w_ple_gate"]), _resident((D_MODEL, PLE_DIM))],
        out_specs=(_rows(tm, D_MODEL), _rows(tm, D_MODEL), _rows(tm, D_MODEL), _rows(tm, D_MODEL),
                   pl.BlockSpec((3, D_MODEL), lambda i: (0, 0)), _rows(tm, D_MODEL)),
        compiler_params=_params(("arbitrary",)),
    )(x2, p, tgt, g_ple, g_final, wpack, w_ple_t)


def mlp_bwd(dx2, x1, u, g_mlp, wpack, off, tm):
    t = x1.shape[0]
    fc = 1024

    def body(dx2_ref, x1_ref, u_ref, g_ref, wup_ref, wdn_ref, dx1_o, da_o, h2_o, st_o, dx1b_o):
        @pl.when(pl.program_id(0) == 0)
        def _():
            st_o[...] = jnp.zeros_like(st_o)

        d2 = dx2_ref[...]
        d2b = d2.astype(BF16)
        dh2 = jnp.zeros((tm, D_MODEL), F32)
        for cidx in range(D_FF // fc):
            sl = slice(cidx * fc, (cidx + 1) * fc)
            da = (_dot_nt(d2b, _wrows(wdn_ref, cidx * fc, fc)) * (2.0 * u_ref[:, sl].astype(F32))).astype(BF16)
            da_o[:, sl] = da
            dh2 = dh2 + _dot_nn(da, _wrows(wup_ref, cidx * fc, fc))
        xv = x1_ref[...]
        g = g_ref[...]
        r1 = _rstd(xv, D_MODEL)
        xh1 = xv * r1
        h2_o[...] = (xh1 * g).astype(BF16)
        st_o[...] += _colsum(dh2 * xh1)
        dx1 = d2 + _rms_bwd(dh2, xh1, r1, g, D_MODEL)
        dx1_o[...] = dx1
        dx1b_o[...] = dx1.astype(BF16)

    return pl.pallas_call(
        body, name="mlp_bwd", grid=(t // tm,),
        out_shape=(jax.ShapeDtypeStruct((t, D_MODEL), F32), jax.ShapeDtypeStruct((t, D_FF), BF16),
                   jax.ShapeDtypeStruct((t, D_MODEL), BF16), jax.ShapeDtypeStruct((1, D_MODEL), F32),
                   jax.ShapeDtypeStruct((t, D_MODEL), BF16)),
        in_specs=[_rows(tm, D_MODEL), _rows(tm, D_MODEL), _rows(tm, D_FF), _resident((1, D_MODEL)),
                  _packed_weight(512, off["w_up"]), _packed_weight(512, off["w_down"])],
        out_specs=(_rows(tm, D_MODEL), _rows(tm, D_FF), _rows(tm, D_MODEL), pl.BlockSpec((1, D_MODEL), lambda i: (0, 0)),
                   _rows(tm, D_MODEL)),
        compiler_params=_params(("arbitrary",)),
    )(dx2, x1, u, g_mlp, wpack, wpack)


def merge_bwd(dx1b, yab, zp, w_oa_t, w_ob_t, wpack, off, tm, comm=None):
    t = dx1b.shape[0]
    grid = (t // tm,)
    c_ins, c_in_specs, c_outs, c_sems, alias = _comm_parts(comm, 7, 5)

    def body(*refs):
        ((dx1_ref, y_ref, ga_ref, gb_ref, woa_ref, wob_ref, wo_ref), cin,
         (doa_o, dob_o, dg_o, dya_o, dyb_o), cout, _, csem) = _split_refs(refs, 7, 5, 0, comm)
        _comm_start(comm, cin, cout, csem, grid)
        dm = _dot_nt(dx1_ref[...], _wrows(wo_ref, 0, D_MODEL))
        for g_ref, w_ref, do_o, dy_o, col in ((ga_ref, woa_ref, doa_o, dya_o, 0), (gb_ref, wob_ref, dob_o, dyb_o, 1)):
            cols = slice(col * D_MODEL, (col + 1) * D_MODEL)
            sg = jax.nn.sigmoid(g_ref[...].astype(F32))
            dyv = (dm * sg).astype(BF16)
            dy_o[...] = dyv
            dg_o[:, cols] = (dm * y_ref[:, cols].astype(F32) * sg * (1.0 - sg)).astype(BF16)
            do_o[...] = _dot_nn(dyv, w_ref[...]).astype(BF16)
        _comm_finish(comm, cin, cout, csem, grid)

    bf = jax.ShapeDtypeStruct((t, D_MODEL), BF16)
    return pl.pallas_call(
        body, name="merge_bwd", grid=grid,
        out_shape=(bf, bf, jax.ShapeDtypeStruct((t, ZP), BF16), bf, bf, *c_outs),
        in_specs=[_rows(tm, D_MODEL), _rows(tm, 2 * D_MODEL), _rows(tm, 1024, 0), _rows(tm, 1024, 1),
                  _resident((D_MODEL, H_A * HP)), _resident((D_MODEL, H_B * HP)), _packed_weight(128, off["w_o"]), *c_in_specs],
        out_specs=(_rows(tm, D_MODEL), _rows(tm, D_MODEL), _rows(tm, 2 * D_MODEL), _rows(tm, D_MODEL), _rows(tm, D_MODEL),
                   *([_ANY] * len(c_outs))),
        scratch_shapes=c_sems, input_output_aliases=alias,
        compiler_params=_params(("arbitrary",)),
    )(dx1b, yab, zp, zp, w_oa_t, w_ob_t, wpack, *c_ins)


def attn_bwd(q, k, v, do, o, lse, n_b, s_len, tq, name, comm=None):
    t = q.shape[0]
    n_h, n_hk = q.shape[1] // HP, k.shape[1] // HP
    grp = n_h // n_hk
    nq = s_len // tq
    sub = min(tq, 256)
    grid = (n_b, n_hk, grp, nq)
    c_ins, c_in_specs, c_outs, c_sems, alias = _comm_parts(comm, 6, 3)

    def body(*refs):
        ((q_ref, k_ref, v_ref, do_ref, o_ref, lse_ref), cin, (dq_o, dk_o, dv_o), cout, (p_s, ds_s, dk_acc, dv_acc),
         csem) = _split_refs(refs, 6, 3, 4, comm)
        _comm_start(comm, cin, cout, csem, grid)

        @pl.when((pl.program_id(2) == 0) & (pl.program_id(3) == 0))
        def _():
            dk_acc[...] = jnp.zeros_like(dk_acc)
            dv_acc[...] = jnp.zeros_like(dv_acc)

        kv, vv = k_ref[...], v_ref[...]
        for r in range(tq // sub):
            rows = slice(r * sub, (r + 1) * sub)
            qv, dov = q_ref[rows, :], do_ref[rows, :]
            delta = jnp.sum(dov.astype(F32) * o_ref[rows, :].astype(F32), axis=-1, keepdims=True)
            delta_row = jnp.broadcast_to(delta, (sub, HP)).T[0:1, :]
            lse_row = lse_ref[rows, :].T[0:1, :]
            pt = jnp.exp(_dot_nt(kv, qv) - lse_row)
            dst = (pt * (_dot_nt(vv, dov) - delta_row)).astype(BF16)
            p_s[:, rows] = pt.astype(BF16)
            ds_s[:, rows] = dst
            dq_o[rows, :] = _dot_tn(dst, kv).astype(dq_o.dtype)
        dk_acc[...] += _dot_nn(ds_s[...], q_ref[...])
        dv_acc[...] += _dot_nn(p_s[...], do_ref[...])

        @pl.when((pl.program_id(2) == grp - 1) & (pl.program_id(3) == nq - 1))
        def _():
            dk_o[...] = dk_acc[...].astype(dk_o.dtype)
            dv_o[...] = dv_acc[...].astype(dv_o.dtype)

        _comm_finish(comm, cin, cout, csem, grid)

    qspec = pl.BlockSpec((tq, HP), lambda b, hk, g, i: (b * nq + i, hk * grp + g))
    kspec = pl.BlockSpec((s_len, HP), lambda b, hk, g, i: (b, hk))
    return pl.pallas_call(
        body, name=name, grid=grid,
        out_shape=(jax.ShapeDtypeStruct((t, n_h * HP), BF16), jax.ShapeDtypeStruct((t, n_hk * HP), BF16),
                   jax.ShapeDtypeStruct((t, n_hk * HP), BF16), *c_outs),
        in_specs=[qspec, kspec, kspec, qspec, qspec, qspec, *c_in_specs],
        out_specs=(qspec, kspec, kspec, *([_ANY] * len(c_outs))),
        scratch_shapes=[pltpu.VMEM((s_len, tq), BF16), pltpu.VMEM((s_len, tq), BF16),
                        pltpu.VMEM((s_len, HP), F32), pltpu.VMEM((s_len, HP), F32), *c_sems],
        input_output_aliases=alias,
        compiler_params=_params(("arbitrary", "arbitrary", "arbitrary", "arbitrary")),
    )(q, k, v, do, o, lse, *c_ins)


def prep_bwd(dqa, dka, dva, dqb, dkb, dvb, zp, dz, tabs, g_qa, g_kva, g_qn, g_kn, w_qb_t, w_kvb_t, tm, s_len):
    t = zp.shape[0]
    nsb = s_len // tm
    scale_a = (QK_NOPE + QK_ROPE) ** -0.5
    scale_b = HD_B ** -0.5

    def body(dqa_ref, dka_ref, dva_ref, dqb_ref, dkb_ref, dvb_ref, qb_ref, qlat_ref, kb_ref, ckv_ref, tab_ref,
             gqa_ref, gkva_ref, gqn_ref, gkn_ref, wqb_ref, wkvb_ref, _, dz_o, dqap_o, dkva_o, st_o):
        dzq_o, dsm_o = dz_o.at[:, 0:1024], dz_o.at[:, 1024:2048]

        @pl.when(pl.program_id(0) == 0)
        def _():
            st_o[...] = jnp.zeros_like(st_o)

        ca, s1a, s2a = tab_ref[0], tab_ref[1], tab_ref[2]
        ck = tab_ref[3]
        cb, s1b, s2b = tab_ref[4], tab_ref[5], tab_ref[6]
        for h in range(H_A):
            sl = slice(h * HP, (h + 1) * HP)
            dqap_o[:, sl] = _rope_bwd(dqa_ref[:, sl].astype(F32) * scale_a, ca, s1a, s2a).astype(BF16)
        dcq = _dot_nn(dqap_o[...], wqb_ref[...])
        ql = qlat_ref[...].astype(F32)
        rq = _rstd(ql, Q_LORA)
        xh = ql * rq
        gqa = gqa_ref[...]
        st_o[0:1, :] += _colsum(dcq * xh)
        dsm_o[:, 0:256] = _rms_bwd(dcq, xh, rq, gqa, Q_LORA).astype(BF16)
        dkpe = jnp.zeros((tm, HP), F32)
        for h in range(H_A):
            sl = slice(h * HP, (h + 1) * HP)
            dk = dka_ref[:, sl]
            dkpe = dkpe + dk.astype(F32)
            dkva_o[:, sl] = dk.astype(BF16)
        dkva_o[:, H_A * HP:] = dva_ref[...].astype(BF16)
        dsm_o[:, 896:1024] = _rope_bwd(dkpe, ck, s1a, s2a).astype(BF16)
        dckv = _dot_nn(dkva_o[...], wkvb_ref[...])
        cr = ckv_ref[...].astype(F32)
        rk = _rstd(cr, KV_LORA)
        xh = cr * rk
        st_o[1:2, 0:128] += _colsum(dckv * xh)
        dsm_o[:, 768:896] = _rms_bwd(dckv, xh, rk, gkva_ref[...], KV_LORA).astype(BF16)
        gqn, gkn = gqn_ref[...], gkn_ref[...]
        dgq = jnp.zeros((1, HP), F32)
        for h in range(H_B):
            sl = slice(h * HP, (h + 1) * HP)
            dy = _rope_bwd(dqb_ref[:, sl].astype(F32) * scale_b, cb, s1b, s2b)
            xs = qb_ref[:, sl].astype(F32)
            r = _rstd(xs, HD_B)
            xh = xs * r
            dgq = dgq + _colsum(dy * xh)
            dzq_o[:, sl] = _rms_bwd(dy, xh, r, gqn, HD_B).astype(BF16)
        st_o[2:3, 0:128] += dgq
        dgk = jnp.zeros((1, HP), F32)
        for h in range(KV_B):
            sl = slice(h * HP, (h + 1) * HP)
            dy = _rope_bwd(dkb_ref[:, sl].astype(F32), cb, s1b, s2b)
            xs = kb_ref[:, sl].astype(F32)
            r = _rstd(xs, HD_B)
            xh = xs * r
            dgk = dgk + _colsum(dy * xh)
            dsm_o[:, 256 + h * HP:256 + (h + 1) * HP] = _rms_bwd(dy, xh, r, gkn, HD_B).astype(BF16)
        st_o[3:4, 0:128] += dgk
        dsm_o[:, 512:768] = dvb_ref[...].astype(BF16)

    return pl.pallas_call(
        body, name="prep_bwd", grid=(t // tm,),
        out_shape=(jax.ShapeDtypeStruct((t, ZP), BF16), jax.ShapeDtypeStruct((t, 1024), BF16),
                   jax.ShapeDtypeStruct((t, 2048), BF16), jax.ShapeDtypeStruct((4, 256), F32)),
        in_specs=[_rows(tm, 1024), _rows(tm, 1024), _rows(tm, 1024), _rows(tm, 1024), _rows(tm, 256), _rows(tm, 256),
                  _rows(tm, 1024, 0), _rows(tm, 256, 4), _rows(tm, 256, 5), _rows(tm, 128, 14),
                  pl.BlockSpec((7, tm, HP), lambda i: (0, i % nsb, 0)),
                  _resident((1, Q_LORA)), _resident((1, KV_LORA)), _resident((1, HP)), _resident((1, HP)),
                  _resident((H_A * HP, Q_LORA)), _resident((2 * H_A * HP, KV_LORA)), _ANY],
        out_specs=(_rows(tm, 2048, 1), _rows(tm, 1024), _rows(tm, 2048), pl.BlockSpec((4, 256), lambda i: (0, 0))),
        input_output_aliases={17: 0}, compiler_params=_params(("arbitrary",)),
    )(dqa, dka, dva, dqb, dkb, dvb, zp, zp, zp, zp, tabs, g_qa, g_kva, g_qn, g_kn, w_qb_t, w_kvb_t, dz)


def in_bwd(dz, x, dx1, g_mix, w_in_t, tm, comm=None):
    t = x.shape[0]
    grid = (t // tm,)
    c_ins, c_in_specs, c_outs, c_sems, alias = _comm_parts(comm, 5, 2)

    def body(*refs):
        (dz_ref, x_ref, dx1_ref, g_ref, w_ref), cin, (dx_o, st_o), cout, _, csem = _split_refs(refs, 5, 2, 0, comm)
        _comm_start(comm, cin, cout, csem, grid)

        @pl.when(pl.program_id(0) == 0)
        def _():
            st_o[...] = jnp.zeros_like(st_o)

        dh = _dot_nn(dz_ref[...], w_ref[...])
        xv = x_ref[...]
        g = g_ref[...]
        r = _rstd(xv, D_MODEL)
        xh = xv * r
        st_o[...] += _colsum(dh * xh)
        dx_o[...] = dx1_ref[...] + _rms_bwd(dh, xh, r, g, D_MODEL)
        _comm_finish(comm, cin, cout, csem, grid)

    return pl.pallas_call(
        body, name="in_bwd", grid=grid,
        out_shape=(jax.ShapeDtypeStruct((t, D_MODEL), F32), jax.ShapeDtypeStruct((1, D_MODEL), F32), *c_outs),
        in_specs=[_rows(tm, ZP), _rows(tm, D_MODEL), _rows(tm, D_MODEL),
                  _resident((1, D_MODEL)), _resident((ZP, D_MODEL)), *c_in_specs],
        out_specs=(_rows(tm, D_MODEL), pl.BlockSpec((1, D_MODEL), lambda i: (0, 0)), *([_ANY] * len(c_outs))),
        scratch_shapes=c_sems, input_output_aliases=alias,
        compiler_params=_params(("arbitrary",)),
    )(dz, x, dx1, g_mix, w_in_t, *c_ins)


def matmul_tn(a, b, name):
    t, m = a.shape
    n = b.shape[1]
    bm = min(m, 512)

    def body(a_ref, b_ref, o_ref):
        o_ref[...] = _dot_tn(a_ref[...].astype(BF16), b_ref[...].astype(BF16)).astype(BF16)

    return pl.pallas_call(
        body, name=name, grid=(m // bm,), out_shape=jax.ShapeDtypeStruct((m, n), BF16),
        in_specs=[pl.BlockSpec((t, bm), lambda i: (0, i)), pl.BlockSpec((t, n), lambda i: (0, 0))],
        out_specs=pl.BlockSpec((bm, n), lambda i: (i, 0)),
        compiler_params=_params(("parallel",)),
    )(a, b)


def matmul_tn_packed(a, b, name, rows, row_off, total_rows, buf=None, square_a=False):
    t, m = a.shape
    n = b.shape[1]
    pd = max(1, 512 // rows)
    bm = pd * rows
    tk = min(t, 4096)
    nk = t // tk

    def body(a_ref, b_ref, *rest):
        o_ref, acc = rest[-2], rest[-1]

        @pl.when(pl.program_id(1) == 0)
        def _():
            acc[...] = jnp.zeros_like(acc)

        av = a_ref[...]
        if square_a:
            av = (av.astype(F32) * av.astype(F32))
        acc[...] += _dot_tn(av.astype(BF16), b_ref[...].astype(BF16))

        @pl.when(pl.program_id(1) == nk - 1)
        def _():
            o_ref[...] = acc[...].reshape(pd, rows, n).astype(o_ref.dtype)

    in_specs = [pl.BlockSpec((tk, bm), lambda i, kk: (kk, i)), pl.BlockSpec((tk, n), lambda i, kk: (kk, 0))]
    args = [a, b]
    if buf is not None:
        in_specs.append(_ANY)
        args.append(buf)
    return pl.pallas_call(
        body, name=name, grid=(m // bm, nk), out_shape=jax.ShapeDtypeStruct((N_DEV, total_rows, n), BF16),
        in_specs=in_specs, out_specs=pl.BlockSpec((pd, rows, n), lambda i, kk: (i, row_off // rows, 0)),
        scratch_shapes=[pltpu.VMEM((bm, n), F32)], input_output_aliases={2: 0} if buf is not None else {},
        compiler_params=_params(("parallel", "arbitrary")),
    )(*args)


def adamw(w, g, m, v, name, g_transposed=False):
    _, r, c = w.shape
    tr = 256 if (not g_transposed and r > 256 and r % 256 == 0) else r
    c1 = 1.0 - ADAM_B1 ** ADAM_STEP
    c2 = 1.0 - ADAM_B2 ** ADAM_STEP

    def body(w_ref, g_ref, m_ref, v_ref, g_o, d_o, m_o, v_o):
        gv = g_ref[...].T if g_transposed else g_ref[...]
        mn = ADAM_B1 * m_ref[0] + (1.0 - ADAM_B1) * gv
        vn = ADAM_B2 * v_ref[0] + (1.0 - ADAM_B2) * (gv * gv)
        g_o[0] = gv
        m_o[0] = mn
        v_o[0] = vn
        d_o[0] = -ADAM_LR * ((mn / c1) / (jnp.sqrt(vn / c2) + ADAM_EPS) + ADAM_WD * w_ref[0])

    spec = pl.BlockSpec((1, tr, c), lambda i: (0, i, 0))
    gspec = pl.BlockSpec((c, r), lambda i: (0, 0)) if g_transposed else pl.BlockSpec((tr, c), lambda i: (i, 0))
    shp = jax.ShapeDtypeStruct((1, r, c), F32)
    return pl.pallas_call(
        body, name=name, grid=(r // tr,), out_shape=(shp,) * 4, in_specs=[spec, gspec, spec, spec], out_specs=(spec,) * 4,
        compiler_params=_params(("parallel",)),
    )(w, g, m, v)


def _rope_tables(s_len):
    def angles(pos, dim):
        inv = np.float32(ROPE_THETA) ** (-np.arange(0, dim, 2, dtype=np.float32) / np.float32(dim))
        return pos.astype(np.float32)[:, None] * inv[None, :]

    tpos = np.arange(s_len)
    a1 = angles(tpos, QK_ROPE)
    ar = angles(tpos // GRID_W, HD_B // 2)
    ac = angles(tpos % GRID_W, HD_B // 2)
    z16 = np.zeros((s_len, 16), np.float32)
    z32 = np.zeros((s_len, 32), np.float32)
    z64 = np.zeros((s_len, 64), np.float32)
    one64 = np.ones((s_len, 64), np.float32)
    c1, s1 = np.cos(a1), np.sin(a1)
    ca = np.concatenate([one64, c1, c1, z32], axis=1)
    ck = np.concatenate([z64, c1, c1, z32], axis=1)
    s1a = np.concatenate([z64, -s1, z16, z32], axis=1)
    s2a = np.concatenate([z64, z16, s1, z32], axis=1)
    cr, sr, cc, sc = np.cos(ar), np.sin(ar), np.cos(ac), np.sin(ac)
    cb = np.concatenate([cr, cr, cc, cc, z64], axis=1)
    s1b = np.concatenate([-sr, z16, -sc, z16, z64], axis=1)
    s2b = np.concatenate([z16, sr, z16, sc, z64], axis=1)
    return jnp.asarray(np.stack([ca, s1a, s2a, ck, cb, s1b, s2b]).astype(np.float32))


def _pad_heads(a, n_heads, axis):
    shp = a.shape
    a = a.reshape(shp[:axis] + (n_heads, shp[axis] // n_heads) + shp[axis + 1:])
    pad = [(0, 0)] * a.ndim
    pad[axis + 1] = (0, HP - a.shape[axis + 1])
    a = jnp.pad(a, pad)
    return a.reshape(shp[:axis] + (n_heads * HP,) + shp[axis + 1:])


def _unpad_heads(a, n_heads, width, axis):
    shp = a.shape
    a = a.reshape(shp[:axis] + (n_heads, HP) + shp[axis + 1:])
    a = lax.slice_in_dim(a, 0, width, axis=axis + 1)
    return a.reshape(shp[:axis] + (n_heads * width,) + shp[axis + 1:])


def _pack_rows(blocks, names):
    parts = []
    for name in names:
        b = blocks[name]
        padr = PACK_ROWS[name] - b.shape[-2]
        if padr:
            b = jnp.pad(b, [(0, 0)] * (b.ndim - 2) + [(0, padr), (0, 0)])
        parts.append(b)
    return jnp.concatenate(parts, axis=parts[0].ndim - 2)


def _expand_w_in(wt):
    z64 = jnp.zeros((64, D_MODEL), wt.dtype)
    z32 = jnp.zeros((32, D_MODEL), wt.dtype)
    return jnp.concatenate([
        wt[1184:2208], wt[2208:3232], _pad_heads(wt[416:928], H_B, 0), wt[0:256],
        _pad_heads(wt[928:1056], KV_B, 0), _pad_heads(wt[1056:1184], KV_B, 0), wt[256:384],
        z64, wt[384:416], z32], axis=0)


def _collapse_w_in(dw):
    dg, dq, ds = dw[0:2048], dw[2048:3072], dw[3072:4096]
    return jnp.concatenate([
        ds[0:256], ds[768:896], ds[960:992], _unpad_heads(dq, H_B, HD_B, 0), _unpad_heads(ds[256:512], KV_B, HD_B, 0),
        _unpad_heads(ds[512:768], KV_B, HD_B, 0), dg], axis=0)


def kernel(x, p, g_mix, w_in, g_qa, w_qb, g_kva, w_kvb, g_qn, g_kn, w_oa, w_ob, w_o, g_mlp, w_up, w_down, g_ple, w_ple_gate, w_ple, g_final, loss_target, m_g_mix, m_w_in, m_g_qa, m_w_qb, m_g_kva, m_w_kvb, m_g_qn, m_g_kn, m_w_oa, m_w_ob, m_w_o, m_g_mlp, m_w_up, m_w_down, m_g_ple, m_w_ple_gate, m_w_ple, m_g_final, v_g_mix, v_w_in, v_g_qa, v_w_qb, v_g_kva, v_w_kvb, v_g_qn, v_g_kn, v_w_oa, v_w_ob, v_w_o, v_g_mlp, v_w_up, v_w_down, v_g_ple, v_w_ple_gate, v_w_ple, v_g_final):
    n_b, s_len, _ = x.shape
    t = n_b * s_len
    tm = min(512, s_len)
    tq_f = min(2048, s_len)
    tq_b = min(2048, s_len)

    mats = dict(w_in=(w_in, m_w_in, v_w_in), w_qb=(w_qb, m_w_qb, v_w_qb), w_kvb=(w_kvb, m_w_kvb, v_w_kvb),
                w_oa=(w_oa, m_w_oa, v_w_oa), w_ob=(w_ob, m_w_ob, v_w_ob), w_o=(w_o, m_w_o, v_w_o),
                w_up=(w_up, m_w_up, v_w_up), w_down=(w_down, m_w_down, v_w_down),
                w_ple_gate=(w_ple_gate, m_w_ple_gate, v_w_ple_gate), w_ple=(w_ple, m_w_ple, v_w_ple))
    col_sharded = ("w_in", "w_qb", "w_kvb", "w_oa", "w_ob", "w_up", "w_ple")

    blocks = {}
    for name in PACK_W1 + ("w_oa", "w_ob", "w_ple"):
        blocks[name] = mats[name][0][0].T.reshape(-1, D_MODEL).astype(BF16)
    off_w1, _ = _pack_offsets(PACK_W1)
    off_w2, _ = _pack_offsets(PACK_W2)
    off_w3, _ = _pack_offsets(PACK_W3)
    xf = x.reshape(t, D_MODEL)
    h, full1 = norm_x(xf, g_mix, tm, comm=gather3_first_comm(_pack_rows(blocks, PACK_W1)))
    pack2, pack3, full1 = pack_late_weights(w_up, w_down, w_o, w_ple_gate, _pack_rows(blocks, ("w_oa", "w_ob", "w_ple")),
                                            comm=gather3_second_comm(full1, then_third=True))

    def gathered(full, offs, name, rows, width):
        return full[:, offs[name]:offs[name] + rows].reshape(-1, width)

    w_in_t = _expand_w_in(gathered(full1, off_w1, "w_in", 404, D_MODEL))
    w_qb_t = _pad_heads(gathered(full1, off_w1, "w_qb", 24, Q_LORA), H_A, 0)
    wkvb = gathered(full1, off_w1, "w_kvb", 16, KV_LORA).reshape(H_A, 2, 64, KV_LORA)
    w_kvb_t = jnp.concatenate([_pad_heads(wkvb[:, 0].reshape(-1, KV_LORA), H_A, 0),
                               _pad_heads(wkvb[:, 1].reshape(-1, KV_LORA), H_A, 0)], axis=0)

    tabs = _rope_tables(s_len)
    g_qn_p = jnp.pad(g_qn, ((0, 0), (0, HP - HD_B)))
    g_kn_p = jnp.pad(g_kn, ((0, 0), (0, HP - HD_B)))
    pf = p.reshape(t, PLE_DIM)
    tgt = loss_target.reshape(t, D_MODEL)

    zg, zs, full2 = in_proj(h, w_in_t, tm, comm=gather3_first_comm(pack2))
    qa, ka, va, qb, kb, vb, cq, ckv, full2 = attn_prep(zs, tabs, g_qa, g_kva, g_qn_p, g_kn_p, w_qb_t, w_kvb_t, tm, s_len,
                                                       comm=gather3_second_comm(full2))
    oa, lse_a, full3, full2 = attn_fwd(qa, ka, va, n_b, s_len, tq_f, "attn_a_fwd",
                                       comm=_join_comms(gather_first_comm(pack3), gather3_third_comm(full2)))
    ob, lse_b, full3 = attn_fwd(qb, kb, vb, n_b, s_len, tq_f, "attn_b_fwd", comm=gather_pass_comm(full3))
    w_oa_t = _pad_heads(gathered(full2, off_w2, "w_oa", 64, H_A * V_DIM_A), H_A, 1)
    w_ob_t = _pad_heads(gathered(full2, off_w2, "w_ob", 64, H_B * HD_B), H_B, 1)
    w_ple_t = gathered(full2, off_w2, "w_ple", 32, PLE_DIM)
    x1, merged, yab = merge_fwd(oa, ob, zg, xf, w_oa_t, w_ob_t, full2, off_w2, tm)
    x2, u = mlp_fwd(x1, g_mlp, full3, off_w3, tm)
    dx2, dt, h3, dpe, st_ple, dx2b = ple_loss_bwd(x2, pf, tgt, g_ple, g_final.reshape(1, D_MODEL), full2, off_w2, w_ple_t, tm)
    dx1, da, h2, st_mlp, dx1b = mlp_bwd(dx2, x1, u, g_mlp, full3, off_w3, tm)

    core = lax.axis_index("c").astype(jnp.int32).reshape(1)
    chip = (2 * lax.axis_index("x") + lax.axis_index("y")).astype(jnp.int32).reshape(1)

    def packed(gblocks, names):
        return _pack_rows({n: gblocks[n].reshape(N_DEV, -1, D_MODEL).astype(BF16) for n in names}, names)

    off_g1, rows_g1 = _pack_offsets(PACK_G1)
    gpack1 = matmul_tn_packed(da, h2, "gw_up", 512, off_g1["w_up"], rows_g1)
    gpack1 = matmul_tn_packed(u, dx2b, "gw_down", 512, off_g1["w_down"], rows_g1, buf=gpack1, square_a=True)
    gpack1 = matmul_tn_packed(h3, dt, "gw_pg", 128, off_g1["w_ple_gate"], rows_g1, buf=gpack1)
    gple = matmul_tn(dpe, pf, "gw_ple").reshape(N_DEV, -1, D_MODEL).astype(BF16)
    gpack1 = lax.dynamic_update_slice(gpack1, gple, (0, off_g1["w_ple"], 0))
    doa, dob, dz, dya, dyb, got1 = merge_bwd(dx1b, yab, zg, w_oa_t, w_ob_t, full2, off_w2, tm,
                                               comm=scatter_sibling_comm(gpack1))
    part1 = add_pairs(gpack1, got1, core)

    off_g2, rows_g2 = _pack_offsets(PACK_G2)
    g2 = dict(w_oa=_unpad_heads(matmul_tn(dya, oa, "gw_oa"), H_A, V_DIM_A, 1),
              w_ob=_unpad_heads(matmul_tn(dyb, ob, "gw_ob"), H_B, HD_B, 1))
    gpack2 = matmul_tn_packed(merged, dx1b, "gw_o", 128, off_g2["w_o"], rows_g2)
    gpack2 = lax.dynamic_update_slice(gpack2, packed(g2, ("w_oa", "w_ob")), (0, off_g2["w_oa"], 0))
    dqa, dka, dva, land1, got2 = attn_bwd(qa, ka, va, doa, oa, lse_a, n_b, s_len, tq_b, "attn_a_bwd",
                                          comm=_join_comms(scatter_chips_comm(part1), scatter_sibling_comm(gpack2)))
    gshard1 = sum_chips(part1, land1, chip, off_g1["w_ple"], PACK_ROWS["w_ple"])
    part2 = add_pairs(gpack2, got2, core)
    dqb, dkb, dvb, land2 = attn_bwd(qb, kb, vb, dob, ob, lse_b, n_b, s_len, tq_b, "attn_b_bwd", comm=scatter_chips_comm(part2))
    gshard2 = sum_chips(part2, land2, chip, off_g2["w_oa"], 128)
    dz, dqap, dkva, st_prep = prep_bwd(dqa, dka, dva, dqb, dkb, dvb, zs, dz, tabs, g_qa, g_kva, g_qn_p, g_kn_p,
                                       w_qb_t, w_kvb_t, tm, s_len)

    gkv = matmul_tn(dkva, ckv, "gw_kvb")
    g3 = dict(
        w_in=_collapse_w_in(matmul_tn(dz, h, "gw_in")),
        w_qb=_unpad_heads(matmul_tn(dqap, cq, "gw_qb"), H_A, QK_NOPE + QK_ROPE, 0),
        w_kvb=jnp.stack([_unpad_heads(gkv[:H_A * HP], H_A, 64, 0).reshape(H_A, 64, KV_LORA),
                         _unpad_heads(gkv[H_A * HP:], H_A, 64, 0).reshape(H_A, 64, KV_LORA)], axis=1))
    gpack3 = packed(g3, PACK_G3)
    part3 = add_pairs(gpack3, exchange_sibling(gpack3), core)
    grad_x, st_mix, land3 = in_bwd(dz, xf, dx1, g_mix, w_in_t, tm, comm=scatter_chips_comm(part3))
    gshard3 = sum_chips(part3, land3, chip)
    off_g3, _ = _pack_offsets(PACK_G3)
    shards = {"w_ple": (gshard1, 0), "w_oa": (gshard2, 0), "w_ob": (gshard2, PACK_ROWS["w_oa"])}
    shards.update({n: (gshard3, off_g3[n]) for n in PACK_G3})
    from_parts = {"w_up": (part1, land1, off_g1["w_up"]), "w_down": (part1, land1, off_g1["w_down"]),
                  "w_ple_gate": (part1, land1, off_g1["w_ple_gate"]), "w_o": (part2, land2, off_g2["w_o"])}

    stats = allreduce_stats(st_mix, st_prep, st_mlp, st_ple)
    loss = jnp.sum(stats[ST_LOSS])

    out_g, out_d, out_m, out_v = {}, {}, {}, {}
    for name, (w, m, v) in mats.items():
        if name in from_parts:
            part_n, land_n, off = from_parts[name]
            out_g[name], out_d[name], out_m[name], out_v[name] = adamw_from_parts(
                w, part_n, land_n, chip, off, m, v, "adamw_" + name, g_transposed=name in col_sharded)
            continue
        gshard, off = shards[name]
        r, c = w.shape[1:]
        if name in col_sharded:
            g2 = gshard[off:off + (r * c) // D_MODEL].reshape(c, r)
            if r % 128 == 0 and c % 128 == 0:
                res = adamw(w, g2, m, v, "adamw_" + name, g_transposed=True)
            else:
                res = adamw(w[0].T[None], g2, m[0].T[None], v[0].T[None], "adamw_" + name)
                res = tuple(a[0].T[None] for a in res)
        else:
            res = adamw(w, gshard[off:off + r], m, v, "adamw_" + name)
        out_g[name], out_d[name], out_m[name], out_v[name] = res

    gains = (("g_mix", g_mix, m_g_mix, v_g_mix, ST_G_MIX), ("g_qa", g_qa, m_g_qa, v_g_qa, ST_G_QA),
             ("g_kva", g_kva, m_g_kva, v_g_kva, ST_G_KVA), ("g_qn", g_qn, m_g_qn, v_g_qn, ST_G_QN),
             ("g_kn", g_kn, m_g_kn, v_g_kn, ST_G_KN), ("g_mlp", g_mlp, m_g_mlp, v_g_mlp, ST_G_MLP),
             ("g_ple", g_ple, m_g_ple, v_g_ple, ST_G_PLE), ("g_final", g_final, m_g_final, v_g_final, ST_G_FINAL))
    res = adamw_gains(stats, [(r_, w.reshape(1, -1), m.reshape(1, -1), v.reshape(1, -1)) for _, w, m, v, r_ in gains])
    for (name, w, _, _, _), (gg, gd, gm, gv) in zip(gains, res):
        out_g[name], out_d[name], out_m[name], out_v[name] = (a.reshape(w.shape) for a in (gg, gd, gm, gv))

    order = ("g_mix", "w_in", "g_qa", "w_qb", "g_kva", "w_kvb", "g_qn", "g_kn", "w_oa", "w_ob", "w_o", "g_mlp",
             "w_up", "w_down", "g_ple", "w_ple_gate", "w_ple", "g_final")
    return (loss, grad_x.reshape(x.shape), *[out_g[n] for n in order], *[out_d[n] for n in order],
            *[out_m[n] for n in order], *[out_v[n] for n in order])
```
